```python
import math
import jax, jax.numpy as jnp
from jax import lax
import numpy as np

D_MODEL = 1024
BATCH = 8
SEQ = 2048
DEPTH = 2

HEAD_DIM = 64
BLK = 128
H_SB = 4
DIL_PATTERNS = ((128, 1), (512, 4), (2048, 16))
H_PER_DIL = 2
H_DIL = H_PER_DIL * len(DIL_PATTERNS)
H_SWA_Q = 6
H_SWA_KV = 2
SWA_WINDOW = 128
N_BUCKETS = 32
MAX_REL_DIST = 2048
N_SOFT_HEADS = H_DIL + H_SWA_Q
D_FF = 2816
RMS_EPS = 1e-6
ADA_CHUNKS = 9
IN_SPLITS = (H_SB * HEAD_DIM, H_SB * HEAD_DIM, H_SB * HEAD_DIM,
             H_DIL * HEAD_DIM, H_DIL * HEAD_DIM, H_DIL * HEAD_DIM,
             H_SWA_Q * HEAD_DIM, H_SWA_KV * HEAD_DIM, H_SWA_KV * HEAD_DIM,
             D_MODEL, D_MODEL, D_MODEL)
D_IN = sum(IN_SPLITS)

kernel_name = 'hybrid_sb_dilated_swa_macaron'


def rmsnorm(x, g):
    xf = x.astype(jnp.float32)
    y = xf * lax.rsqrt(jnp.mean(xf * xf, axis=-1, keepdims=True) + RMS_EPS)
    return (y * g.astype(jnp.float32)).astype(x.dtype)


def swiglu(h, wg, wu, wd):
    return (jax.nn.silu(h @ wg) * (h @ wu)) @ wd


def t5_bucket(n):
    max_exact = N_BUCKETS // 2
    nf = jnp.maximum(n, 1).astype(jnp.float32)
    large = max_exact + (jnp.log(nf / max_exact) / math.log(MAX_REL_DIST / max_exact)
                         * (N_BUCKETS - max_exact)).astype(jnp.int32)
    large = jnp.minimum(large, N_BUCKETS - 1)
    return jnp.where(n < max_exact, n, large)


def band_bias(table_cols, dilation):
    rel = jnp.arange(BLK)[:, None] + BLK - jnp.arange(2 * BLK)[None, :]
    b = t5_bucket(jnp.maximum(rel, 0) * dilation)
    return jnp.transpose(table_cols[b], (2, 0, 1)).astype(jnp.float32)


def banded_attention(q, k, v, bias, max_dist, sinks=None):
    N, L, Hk, G, Dh = q.shape
    nb = L // BLK
    qb = q.reshape(N, nb, BLK, Hk, G, Dh).astype(jnp.float32)

    def band(t):
        tb = t.reshape(N, nb, BLK, Hk, Dh).astype(jnp.float32)
        prev = jnp.pad(tb, ((0, 0), (1, 0), (0, 0), (0, 0), (0, 0)))[:, :-1]
        return jnp.concatenate([prev, tb], axis=2)

    kk, vv = band(k), band(v)
    s = jnp.einsum('nbqhgd,nbkhd->nbhgqk', qb, kk) * (Dh ** -0.5) + bias
    rel = jnp.arange(BLK)[:, None] + BLK - jnp.arange(2 * BLK)[None, :]
    in_band = (rel >= 0) & (rel <= max_dist)
    key_pos = jnp.arange(nb)[:, None] * BLK - BLK + jnp.arange(2 * BLK)[None, :]
    mask = in_band[None] & (key_pos >= 0)[:, None, :]
    s = jnp.where(mask[None, :, None, None], s, -jnp.inf)
    m = jnp.max(s, axis=-1)
    if sinks is not None:
        sk = sinks.astype(jnp.float32)[None, None, :, :, None]
        m = jnp.maximum(m, sk)
    p = jnp.exp(s - m[..., None])
    denom = jnp.sum(p, axis=-1)
    if sinks is not None:
        denom = denom + jnp.exp(sk - m)
    o = jnp.einsum('nbhgqk,nbkhd->nbqhgd', p, vv) / jnp.transpose(denom, (0, 1, 4, 2, 3))[..., None]
    lse = jnp.transpose(m + jnp.log(denom), (0, 1, 4, 2, 3))
    return o.reshape(N, L, Hk, G, Dh), lse.reshape(N, L, Hk, G)


def stick_breaking_mixer(q, k, v):
    Bn, S, H, Dh = q.shape
    nb = S // BLK
    kf, vf = k.astype(jnp.float32), v.astype(jnp.float32)
    qb = q.astype(jnp.float32).reshape(Bn, nb, BLK, H, Dh).transpose(1, 0, 2, 3, 4)
    key_pos = jnp.arange(S)

    def block(args):
        qblk, i = args
        z = jnp.einsum('bqhd,bkhd->bhqk', qblk, kf) * (Dh ** -0.5)
        q_pos = i * BLK + jnp.arange(BLK)
        before = key_pos[None, :] < q_pos[:, None]
        log_fail = jnp.where(before, jax.nn.log_sigmoid(-z), 0.0)
        between = lax.cumsum(log_fail, axis=3, reverse=True) - log_fail
        w = jnp.where(before, jnp.exp(jax.nn.log_sigmoid(z) + between), 0.0)
        return jnp.einsum('bhqk,bkhd->bqhd', w, vf)

    o = lax.map(block, (qb, jnp.arange(nb)))
    return o.transpose(1, 0, 2, 3, 4).reshape(Bn, S, H, Dh)


def dilated_mixer(q, k, v, rel_table):
    Bn, S = q.shape[:2]
    outs, lses = [], []
    for g, (w, d) in enumerate(DIL_PATTERNS):
        hs = slice(g * H_PER_DIL, (g + 1) * H_PER_DIL)
        Ls = S // d
        Lp = -(-Ls // BLK) * BLK

        def sub(t):
            t = t[:, :, hs].reshape(Bn, Ls, d, H_PER_DIL, HEAD_DIM).transpose(0, 2, 1, 3, 4)
            t = t.reshape(Bn * d, Ls, H_PER_DIL, HEAD_DIM)
            return jnp.pad(t, ((0, 0), (0, Lp - Ls), (0, 0), (0, 0)))

        bias = band_bias(rel_table[:, hs], d)[:, None]
        o, lse = banded_attention(sub(q)[:, :, :, None], sub(k), sub(v), bias, w // d)
        o = o[:, :Ls, :, 0].reshape(Bn, d, Ls, H_PER_DIL, HEAD_DIM).transpose(0, 2, 1, 3, 4)
        lse = lse[:, :Ls, :, 0].reshape(Bn, d, Ls, H_PER_DIL).transpose(0, 2, 1, 3)
        outs.append(o.reshape(Bn, S, H_PER_DIL, HEAD_DIM))
        lses.append(lse.reshape(Bn, S, H_PER_DIL))
    alpha = jax.nn.softmax(jnp.stack(lses), axis=0)
    return jnp.sum(alpha[..., None] * jnp.stack(outs), axis=0)


def swa_mixer(q, k, v, rel_table, sinks):
    Bn, S = q.shape[:2]
    G = H_SWA_Q // H_SWA_KV
    bias = band_bias(rel_table[:, H_DIL:], 1).reshape(H_SWA_KV, G, BLK, 2 * BLK)
    o, _ = banded_attention(q.reshape(Bn, S, H_SWA_KV, G, HEAD_DIM), k, v, bias,
                            SWA_WINDOW - 1, sinks.reshape(H_SWA_KV, G))
    return o.reshape(Bn, S, H_SWA_Q, HEAD_DIM)


def _fwd_setup_inputs(seed: int = 0) -> dict:
    key = jax.random.key(seed)
    ks = jax.random.split(key, 18)
    f32 = jnp.float32
    nrm = lambda k, shape, scale: jax.random.normal(k, shape, f32) * scale
    return {
        'x': nrm(ks[0], (BATCH, SEQ, D_MODEL), 1.0),
        'c': nrm(ks[1], (BATCH, D_MODEL), 1.0),
        'w_ada': nrm(ks[2], (DEPTH, D_MODEL, ADA_CHUNKS * D_MODEL), 0.5 * D_MODEL ** -0.5),
        'b_ada': nrm(ks[3], (DEPTH, ADA_CHUNKS * D_MODEL), 0.02),
        'norm_gain': 1.0 + nrm(ks[4], (DEPTH, 3, D_MODEL), 0.05),
        'w_ffn_gate': nrm(ks[5], (DEPTH, 2, D_MODEL, D_FF), D_MODEL ** -0.5),
        'w_ffn_up': nrm(ks[6], (DEPTH, 2, D_MODEL, D_FF), D_MODEL ** -0.5),
        'w_ffn_down': nrm(ks[7], (DEPTH, 2, D_FF, D_MODEL), D_FF ** -0.5),
        'w_in': nrm(ks[8], (DEPTH, D_MODEL, D_IN), D_MODEL ** -0.5),
        'w_br_sb': nrm(ks[9], (DEPTH, H_SB * HEAD_DIM, D_MODEL), (H_SB * HEAD_DIM) ** -0.5),
        'w_br_dil': nrm(ks[10], (DEPTH, H_PER_DIL * HEAD_DIM, D_MODEL), (H_PER_DIL * HEAD_DIM) ** -0.5),
        'w_br_swa': nrm(ks[11], (DEPTH, H_SWA_Q * HEAD_DIM, D_MODEL), (H_SWA_Q * HEAD_DIM) ** -0.5),
        'w_out': nrm(ks[12], (DEPTH, D_MODEL, D_MODEL), D_MODEL ** -0.5),
        'sinks': nrm(ks[13], (DEPTH, H_SWA_Q), 0.5),
        'rel_bias': nrm(ks[14], (N_BUCKETS, N_SOFT_HEADS), 0.5),
        'final_gain': 1.0 + nrm(ks[15], (D_MODEL,), 0.05),
    }


def _fwd_reference(x, c, w_ada, b_ada, norm_gain, w_ffn_gate, w_ffn_up, w_ffn_down, w_in,
              w_br_sb, w_br_dil, w_br_swa, w_out, sinks, rel_bias, final_gain):
    Bn, S, _ = x.shape
    split_idx = np.cumsum(IN_SPLITS)[:-1].tolist()
    heads = lambda t: t.reshape(Bn, S, -1, HEAD_DIM)
    for l in range(DEPTH):
        mod = (jax.nn.silu(c) @ w_ada[l] + b_ada[l]).reshape(Bn, 3, 3, D_MODEL)[:, :, :, None, :]

        def pre(xx, j):
            return rmsnorm(xx, norm_gain[l, j]) * (1 + mod[:, j, 1]) + mod[:, j, 0]

        h = pre(x, 0)
        x = x + 0.5 * mod[:, 0, 2] * swiglu(h, w_ffn_gate[l, 0], w_ffn_up[l, 0], w_ffn_down[l, 0])

        h = pre(x, 1)
        (q_sb, k_sb, v_sb, q_dil, k_dil, v_dil, q_swa, k_swa, v_swa,
         g_sb, g_dil, g_swa) = jnp.split(h @ w_in[l], split_idx, axis=-1)
        o_sb = stick_breaking_mixer(heads(q_sb), heads(k_sb), heads(v_sb)).reshape(Bn, S, -1).astype(x.dtype)
        o_dil = dilated_mixer(heads(q_dil), heads(k_dil), heads(v_dil), rel_bias).reshape(Bn, S, -1).astype(x.dtype)
        o_swa = swa_mixer(heads(q_swa), heads(k_swa), heads(v_swa), rel_bias, sinks[l]).reshape(Bn, S, -1).astype(x.dtype)
        merged = (jax.nn.sigmoid(g_sb) * (o_sb @ w_br_sb[l])
                  + jax.nn.sigmoid(g_dil) * (o_dil @ w_br_dil[l])
                  + jax.nn.sigmoid(g_swa) * (o_swa @ w_br_swa[l]))
        x = x + mod[:, 1, 2] * (merged @ w_out[l])

        h = pre(x, 2)
        x = x + 0.5 * mod[:, 2, 2] * swiglu(h, w_ffn_gate[l, 1], w_ffn_up[l, 1], w_ffn_down[l, 1])
    return rmsnorm(x, final_gain)


import jax as _jax
import jax.numpy as _jnp

TWIN_FORMAT = 'train_step'
FWD_PARAMS = ['x', 'c', 'w_ada', 'b_ada', 'norm_gain', 'w_ffn_gate', 'w_ffn_up', 'w_ffn_down', 'w_in', 'w_br_sb', 'w_br_dil', 'w_br_swa', 'w_out', 'sinks', 'rel_bias', 'final_gain']
TWIN_WEIGHTS = ['w_ada', 'b_ada', 'norm_gain', 'w_ffn_gate', 'w_ffn_up', 'w_ffn_down', 'w_in', 'w_br_sb', 'w_br_dil', 'w_br_swa', 'w_out', 'sinks', 'rel_bias', 'final_gain']
TWIN_DIFF_INPUT = 'x'
TWIN_INPUTS = ['x', 'c', 'w_ada', 'b_ada', 'norm_gain', 'w_ffn_gate', 'w_ffn_up', 'w_ffn_down', 'w_in', 'w_br_sb', 'w_br_dil', 'w_br_swa', 'w_out', 'sinks', 'rel_bias', 'final_gain', 'loss_target', 'm_w_ada', 'm_b_ada', 'm_norm_gain', 'm_w_ffn_gate', 'm_w_ffn_up', 'm_w_ffn_down', 'm_w_in', 'm_w_br_sb', 'm_w_br_dil', 'm_w_br_swa', 'm_w_out', 'm_sinks', 'm_rel_bias', 'm_final_gain', 'v_w_ada', 'v_b_ada', 'v_norm_gain', 'v_w_ffn_gate', 'v_w_ffn_up', 'v_w_ffn_down', 'v_w_in', 'v_w_br_sb', 'v_w_br_dil', 'v_w_br_swa', 'v_w_out', 'v_sinks', 'v_rel_bias', 'v_final_gain']
TWIN_OUTPUTS = ['loss', 'grad_x', 'grad_w_ada', 'grad_b_ada', 'grad_norm_gain', 'grad_w_ffn_gate', 'grad_w_ffn_up', 'grad_w_ffn_down', 'grad_w_in', 'grad_w_br_sb', 'grad_w_br_dil', 'grad_w_br_swa', 'grad_w_out', 'grad_sinks', 'grad_rel_bias', 'grad_final_gain', 'delta_w_ada', 'delta_b_ada', 'delta_norm_gain', 'delta_w_ffn_gate', 'delta_w_ffn_up', 'delta_w_ffn_down', 'delta_w_in', 'delta_w_br_sb', 'delta_w_br_dil', 'delta_w_br_swa', 'delta_w_out', 'delta_sinks', 'delta_rel_bias', 'delta_final_gain', 'new_m_w_ada', 'new_m_b_ada', 'new_m_norm_gain', 'new_m_w_ffn_gate', 'new_m_w_ffn_up', 'new_m_w_ffn_down', 'new_m_w_in', 'new_m_w_br_sb', 'new_m_w_br_dil', 'new_m_w_br_swa', 'new_m_w_out', 'new_m_sinks', 'new_m_rel_bias', 'new_m_final_gain', 'new_v_w_ada', 'new_v_b_ada', 'new_v_norm_gain', 'new_v_w_ffn_gate', 'new_v_w_ffn_up', 'new_v_w_ffn_down', 'new_v_w_in', 'new_v_w_br_sb', 'new_v_w_br_dil', 'new_v_w_br_swa', 'new_v_w_out', 'new_v_sinks', 'new_v_rel_bias', 'new_v_final_gain']
TWIN_LEAF_KINDS = {'loss': 'loss', 'grad_x': 'grad_x', 'grad_w_ada': 'grad_w', 'grad_b_ada': 'grad_w', 'grad_norm_gain': 'grad_w', 'grad_w_ffn_gate': 'grad_w', 'grad_w_ffn_up': 'grad_w', 'grad_w_ffn_down': 'grad_w', 'grad_w_in': 'grad_w', 'grad_w_br_sb': 'grad_w', 'grad_w_br_dil': 'grad_w', 'grad_w_br_swa': 'grad_w', 'grad_w_out': 'grad_w', 'grad_sinks': 'grad_w', 'grad_rel_bias': 'grad_w', 'grad_final_gain': 'grad_w', 'delta_w_ada': 'delta_w', 'delta_b_ada': 'delta_w', 'delta_norm_gain': 'delta_w', 'delta_w_ffn_gate': 'delta_w', 'delta_w_ffn_up': 'delta_w', 'delta_w_ffn_down': 'delta_w', 'delta_w_in': 'delta_w', 'delta_w_br_sb': 'delta_w', 'delta_w_br_dil': 'delta_w', 'delta_w_br_swa': 'delta_w', 'delta_w_out': 'delta_w', 'delta_sinks': 'delta_w', 'delta_rel_bias': 'delta_w', 'delta_final_gain': 'delta_w', 'new_m_w_ada': 'new_m', 'new_m_b_ada': 'new_m', 'new_m_norm_gain': 'new_m', 'new_m_w_ffn_gate': 'new_m', 'new_m_w_ffn_up': 'new_m', 'new_m_w_ffn_down': 'new_m', 'new_m_w_in': 'new_m', 'new_m_w_br_sb': 'new_m', 'new_m_w_br_dil': 'new_m', 'new_m_w_br_swa': 'new_m', 'new_m_w_out': 'new_m', 'new_m_sinks': 'new_m', 'new_m_rel_bias': 'new_m', 'new_m_final_gain': 'new_m', 'new_v_w_ada': 'new_v', 'new_v_b_ada': 'new_v', 'new_v_norm_gain': 'new_v', 'new_v_w_ffn_gate': 'new_v', 'new_v_w_ffn_up': 'new_v', 'new_v_w_ffn_down': 'new_v', 'new_v_w_in': 'new_v', 'new_v_w_br_sb': 'new_v', 'new_v_w_br_dil': 'new_v', 'new_v_w_br_swa': 'new_v', 'new_v_w_out': 'new_v', 'new_v_sinks': 'new_v', 'new_v_rel_bias': 'new_v', 'new_v_final_gain': 'new_v'}


def _forward(args):
    return _fwd_reference(*[args[k] for k in FWD_PARAMS])


def _output_shape():
    out = _jax.eval_shape(lambda: _forward(_fwd_setup_inputs(0)))
    return out.shape, out.dtype

N_MICROBATCH = 1
ADAM_LR = 0.001
ADAM_B1 = 0.9
ADAM_B2 = 0.999
ADAM_EPS = 1e-08
ADAM_WD = 0.01
ADAM_STEP = 10
PER_EXAMPLE_BATCH_AXIS = {'x': 0, 'c': 0, 'loss_target': 0}
SHARED_INPUTS = []
_WEIGHT_DTYPES = {'w_ada': _jnp.float32, 'b_ada': _jnp.float32, 'norm_gain': _jnp.float32, 'w_ffn_gate': _jnp.float32, 'w_ffn_up': _jnp.float32, 'w_ffn_down': _jnp.float32, 'w_in': _jnp.float32, 'w_br_sb': _jnp.float32, 'w_br_dil': _jnp.float32, 'w_br_swa': _jnp.float32, 'w_out': _jnp.float32, 'sinks': _jnp.float32, 'rel_bias': _jnp.float32, 'final_gain': _jnp.float32}
MOMENT_SCALE = {'w_ada': 2.374981e-02, 'b_ada': 3.841357e-02, 'norm_gain': 2.041682e-02, 'w_ffn_gate': 8.647720e-03, 'w_ffn_up': 8.390952e-03, 'w_ffn_down': 1.390826e-02, 'w_in': 1.074553e-02, 'w_br_sb': 1.634600e-02, 'w_br_dil': 7.163349e-03, 'w_br_swa': 8.369482e-03, 'w_out': 1.917748e-02, 'sinks': 7.036596e-03, 'rel_bias': 1.294843e-02, 'final_gain': 1.601058e+01}


def _to_microbatches(a, axis):
    t = _jnp.moveaxis(a, axis, 0)
    t = t.reshape((N_MICROBATCH, t.shape[0] // N_MICROBATCH) + t.shape[1:])
    return _jnp.moveaxis(t, 1, axis + 1)


def setup_inputs(seed: int = 0) -> dict:
    inp = _fwd_setup_inputs(seed)
    key = _jax.random.fold_in(_jax.random.key(seed), 7919)
    shape, _ = _output_shape()
    out = dict(inp)
    out["loss_target"] = _jax.random.normal(_jax.random.fold_in(key, 0), shape, _jnp.float32)
    for i, name in enumerate(TWIN_WEIGHTS):
        w = inp[name].astype(_jnp.float32)
        if MOMENT_SCALE is None:
            s = _jnp.sqrt(_jnp.mean(_jnp.square(w)) + 1e-30)
        else:
            s = MOMENT_SCALE[name]
        km, kv = _jax.random.split(_jax.random.fold_in(key, i + 1))
        out[name] = w
        out["m_" + name] = s * _jax.random.normal(km, w.shape, _jnp.float32)
        out["v_" + name] = (s * s) * _jax.random.uniform(kv, w.shape, _jnp.float32, 0.5, 1.5)
    if N_MICROBATCH > 1:
        for name, axis in PER_EXAMPLE_BATCH_AXIS.items():
            out[name] = _to_microbatches(out[name], axis)
    return {'x': out['x'], 'c': out['c'], 'w_ada': out['w_ada'], 'b_ada': out['b_ada'], 'norm_gain': out['norm_gain'], 'w_ffn_gate': out['w_ffn_gate'], 'w_ffn_up': out['w_ffn_up'], 'w_ffn_down': out['w_ffn_down'], 'w_in': out['w_in'], 'w_br_sb': out['w_br_sb'], 'w_br_dil': out['w_br_dil'], 'w_br_swa': out['w_br_swa'], 'w_out': out['w_out'], 'sinks': out['sinks'], 'rel_bias': out['rel_bias'], 'final_gain': out['final_gain'], 'loss_target': out['loss_target'], 'm_w_ada': out['m_w_ada'], 'm_b_ada': out['m_b_ada'], 'm_norm_gain': out['m_norm_gain'], 'm_w_ffn_gate': out['m_w_ffn_gate'], 'm_w_ffn_up': out['m_w_ffn_up'], 'm_w_ffn_down': out['m_w_ffn_down'], 'm_w_in': out['m_w_in'], 'm_w_br_sb': out['m_w_br_sb'], 'm_w_br_dil': out['m_w_br_dil'], 'm_w_br_swa': out['m_w_br_swa'], 'm_w_out': out['m_w_out'], 'm_sinks': out['m_sinks'], 'm_rel_bias': out['m_rel_bias'], 'm_final_gain': out['m_final_gain'], 'v_w_ada': out['v_w_ada'], 'v_b_ada': out['v_b_ada'], 'v_norm_gain': out['v_norm_gain'], 'v_w_ffn_gate': out['v_w_ffn_gate'], 'v_w_ffn_up': out['v_w_ffn_up'], 'v_w_ffn_down': out['v_w_ffn_down'], 'v_w_in': out['v_w_in'], 'v_w_br_sb': out['v_w_br_sb'], 'v_w_br_dil': out['v_w_br_dil'], 'v_w_br_swa': out['v_w_br_swa'], 'v_w_out': out['v_w_out'], 'v_sinks': out['v_sinks'], 'v_rel_bias': out['v_rel_bias'], 'v_final_gain': out['v_final_gain']}


def _loss(weights, diff, rest, loss_target):
    with _jax.named_scope("forward"):
        args = {**rest, TWIN_DIFF_INPUT: diff, **{k: w.astype(_WEIGHT_DTYPES[k]) for k, w in weights.items()}}
        y = _forward(args)
    with _jax.named_scope("loss_head"):
        err = _jnp.square(y.astype(_jnp.float32) - loss_target)
        return 0.5 * _jnp.sum(_jnp.mean(err, axis=-1)) if err.ndim else 0.5 * err


def _adamw(w, g, m, v):
    m = ADAM_B1 * m + (1.0 - ADAM_B1) * g
    v = ADAM_B2 * v + (1.0 - ADAM_B2) * _jnp.square(g)
    m_hat = m / (1.0 - ADAM_B1 ** ADAM_STEP)
    v_hat = v / (1.0 - ADAM_B2 ** ADAM_STEP)
    delta = -ADAM_LR * (m_hat / (_jnp.sqrt(v_hat) + ADAM_EPS) + ADAM_WD * w)
    return delta, m, v


def reference(x, c, w_ada, b_ada, norm_gain, w_ffn_gate, w_ffn_up, w_ffn_down, w_in, w_br_sb, w_br_dil, w_br_swa, w_out, sinks, rel_bias, final_gain, loss_target, m_w_ada, m_b_ada, m_norm_gain, m_w_ffn_gate, m_w_ffn_up, m_w_ffn_down, m_w_in, m_w_br_sb, m_w_br_dil, m_w_br_swa, m_w_out, m_sinks, m_rel_bias, m_final_gain, v_w_ada, v_b_ada, v_norm_gain, v_w_ffn_gate, v_w_ffn_up, v_w_ffn_down, v_w_in, v_w_br_sb, v_w_br_dil, v_w_br_swa, v_w_out, v_sinks, v_rel_bias, v_final_gain):
    given = dict(x=x, c=c, w_ada=w_ada, b_ada=b_ada, norm_gain=norm_gain, w_ffn_gate=w_ffn_gate, w_ffn_up=w_ffn_up, w_ffn_down=w_ffn_down, w_in=w_in, w_br_sb=w_br_sb, w_br_dil=w_br_dil, w_br_swa=w_br_swa, w_out=w_out, sinks=sinks, rel_bias=rel_bias, final_gain=final_gain, loss_target=loss_target, m_w_ada=m_w_ada, m_b_ada=m_b_ada, m_norm_gain=m_norm_gain, m_w_ffn_gate=m_w_ffn_gate, m_w_ffn_up=m_w_ffn_up, m_w_ffn_down=m_w_ffn_down, m_w_in=m_w_in, m_w_br_sb=m_w_br_sb, m_w_br_dil=m_w_br_dil, m_w_br_swa=m_w_br_swa, m_w_out=m_w_out, m_sinks=m_sinks, m_rel_bias=m_rel_bias, m_final_gain=m_final_gain, v_w_ada=v_w_ada, v_b_ada=v_b_ada, v_norm_gain=v_norm_gain, v_w_ffn_gate=v_w_ffn_gate, v_w_ffn_up=v_w_ffn_up, v_w_ffn_down=v_w_ffn_down, v_w_in=v_w_in, v_w_br_sb=v_w_br_sb, v_w_br_dil=v_w_br_dil, v_w_br_swa=v_w_br_swa, v_w_out=v_w_out, v_sinks=v_sinks, v_rel_bias=v_rel_bias, v_final_gain=v_final_gain)
    weights = {n: given[n] for n in TWIN_WEIGHTS}
    shared = {n: given[n] for n in SHARED_INPUTS}
    per_example = {n: given[n] for n in ['x', 'c']}
    grad_fn = _jax.value_and_grad(_loss, argnums=(0, 1))

    def one_microbatch(ex, loss_target):
        ex = dict(ex)
        diff = ex.pop(TWIN_DIFF_INPUT)
        return grad_fn(weights, diff, {**shared, **ex}, loss_target)

    if N_MICROBATCH == 1:
        loss, (grad_w, grad_x) = one_microbatch(per_example, given["loss_target"])
    else:
        def body(carry, xs):
            loss_sum, grad_sum = carry
            l_k, (gw_k, gx_k) = one_microbatch(xs[0], xs[1])
            with _jax.named_scope("update"):
                return (loss_sum + l_k, _jax.tree.map(_jnp.add, grad_sum, gw_k)), gx_k

        init = (_jnp.zeros((), _jnp.float32), _jax.tree.map(_jnp.zeros_like, weights))
        (loss, grad_w), grad_x = _jax.lax.scan(body, init, (per_example, given["loss_target"]))
    with _jax.named_scope("update"):
        delta_w, new_m, new_v = {}, {}, {}
        for n in TWIN_WEIGHTS:
            delta_w[n], new_m[n], new_v[n] = _adamw(weights[n], grad_w[n], given["m_" + n], given["v_" + n])
    return (loss, grad_x, *[grad_w[n] for n in TWIN_WEIGHTS], *[delta_w[n] for n in TWIN_WEIGHTS],
            *[new_m[n] for n in TWIN_WEIGHTS], *[new_v[n] for n in TWIN_WEIGHTS])
```

```python
import functools
import math

import jax
import jax.numpy as jnp
from jax import lax
from jax.experimental import pallas as pl
from jax.experimental.pallas import tpu as pltpu

F32 = jnp.float32
BF16 = jnp.bfloat16

D_MODEL = 1024
SEQ = 2048
DEPTH = 2
HEAD_DIM = 64
BLK = 128
H_SB = 4
DIL_PATTERNS = ((128, 1), (512, 4), (2048, 16))
H_PER_DIL = 2
H_DIL = 6
H_SWA_Q = 6
H_SWA_KV = 2
SWA_WINDOW = 128
N_BUCKETS = 32
MAX_REL_DIST = 2048
D_FF = 2816
RMS_EPS = 1e-6
N_CHIPS = 4
N_DEV = 8
FF_SHARD = D_FF // N_CHIPS
D_QKV = 2560
D_IN = D_QKV + 3 * D_MODEL
IN_SHARD = D_IN // N_CHIPS
D_SHARD = D_MODEL // N_CHIPS
BR_ROWS = 768
NEG = -1e30
QK_SCALE = HEAD_DIM ** -0.5

ADAM_LR = 0.001
ADAM_B1 = 0.9
ADAM_B2 = 0.999
ADAM_EPS = 1e-08
ADAM_WD = 0.01
ADAM_STEP = 10

VMEM_LIMIT = 48 * 1024 * 1024
ROW_TILE = 256
MM_TILE = 512

NN = (((1,), (0,)), ((), ()))
NT = (((1,), (1,)), ((), ()))
TN = (((0,), (0,)), ((), ()))


def _params(sem=None):
    return pltpu.CompilerParams(dimension_semantics=sem, vmem_limit_bytes=VMEM_LIMIT)


def _dot(a, b, dims):
    return lax.dot_general(a, b, dims, preferred_element_type=F32)


def _sigmoid(x):
    return 1.0 / (1.0 + jnp.exp(-x))


def _matmul(name, grid, nk, k_axis, dims, n_pairs, in_specs, out_specs, out_shape, acc_shape, epilogue,
            operands, sem, aliases=None, prologue=None):
    n_in = len(in_specs)
    n_out = len(out_specs)

    def partial(ins):
        tot = None
        for p in range(n_pairs):
            a = ins[2 * p][...]
            if prologue is not None:
                a = prologue(p, a, ins)
            d = _dot(a, ins[2 * p + 1][...], dims)
            tot = d if tot is None else tot + d
        return tot

    def body(*refs):
        ins, outs = refs[:n_in], refs[n_in:n_in + n_out]
        ids = tuple(pl.program_id(a) for a in range(len(grid)))
        if nk == 1:
            epilogue(partial(ins), ins, outs, ids)
            return
        acc = refs[n_in + n_out]
        k = ids[k_axis]

        @pl.when(k == 0)
        def _():
            acc[...] = partial(ins)

        @pl.when(k > 0)
        def _():
            acc[...] += partial(ins)

        @pl.when(k == nk - 1)
        def _():
            epilogue(acc[...], ins, outs, ids)

    return pl.pallas_call(
        body, grid=grid, in_specs=in_specs, out_specs=out_specs, out_shape=out_shape,
        scratch_shapes=[] if nk == 1 else [pltpu.VMEM(acc_shape, F32)],
        input_output_aliases=aliases or {}, name=name, compiler_params=_params(sem),
    )(*operands)


def _row_spec(width=D_MODEL):
    return pl.BlockSpec((ROW_TILE, width), lambda i: (i, 0))


def _vec_spec(rows=1, width=D_MODEL):
    return pl.BlockSpec((rows, width), lambda i: (0, 0))


def prenorm(x, gain, scale, shift):
    def body(x_ref, g_ref, sc_ref, sh_ref, h_ref):
        xv = x_ref[...]
        r = lax.rsqrt(jnp.mean(xv * xv, axis=-1, keepdims=True) + RMS_EPS)
        h_ref[...] = (((xv * r) * g_ref[...]) * (1.0 + sc_ref[...]) + sh_ref[...]).astype(BF16)

    return pl.pallas_call(
        body, grid=(SEQ // ROW_TILE,), in_specs=[_row_spec(), _vec_spec(), _vec_spec(), _vec_spec()],
        out_specs=_row_spec(), out_shape=jax.ShapeDtypeStruct((SEQ, D_MODEL), BF16),
        name="prenorm", compiler_params=_params(("parallel",)),
    )(x, gain, scale, shift)


def resid_bwd(dxo, f, coef, mult):
    def body(dx_ref, f_ref, c_ref, df_ref, dc_ref):
        dx = dx_ref[...]
        df_ref[...] = (dx * (mult * c_ref[...])).astype(BF16)
        part = mult * jnp.sum(dx * f_ref[...], axis=0, keepdims=True)

        @pl.when(pl.program_id(0) == 0)
        def _():
            dc_ref[...] = jnp.zeros_like(dc_ref)

        dc_ref[0:1, :] += part

    return pl.pallas_call(
        body, grid=(SEQ // ROW_TILE,), in_specs=[_row_spec(), _row_spec(), _vec_spec()],
        out_specs=[_row_spec(), _vec_spec(8)],
        out_shape=[jax.ShapeDtypeStruct((SEQ, D_MODEL), BF16), jax.ShapeDtypeStruct((8, D_MODEL), F32)],
        name="resid_bwd", compiler_params=_params(("arbitrary",)),
    )(dxo, f, coef)


def final_loss(x, gain, target):
    def body(x_ref, g_ref, t_ref, loss_ref, dx_ref, dg_ref):
        xv = x_ref[...]
        g = g_ref[...]
        r = lax.rsqrt(jnp.mean(xv * xv, axis=-1, keepdims=True) + RMS_EPS)
        xh = xv * r
        e = xh * g - t_ref[...]
        part = 0.5 * jnp.sum(jnp.mean(e * e, axis=-1, keepdims=True), axis=0, keepdims=True)
        dy = e * (1.0 / D_MODEL)
        dyg = dy * g
        dx_ref[...] = r * (dyg - xh * jnp.mean(dyg * xh, axis=-1, keepdims=True))

        @pl.when(pl.program_id(0) == 0)
        def _():
            loss_ref[...] = jnp.zeros_like(loss_ref)
            dg_ref[...] = jnp.zeros_like(dg_ref)

        loss_ref[...] += jnp.broadcast_to(part, loss_ref.shape)
        dg_ref[0:1, :] += jnp.sum(dy * xh, axis=0, keepdims=True)

    return pl.pallas_call(
        body, grid=(SEQ // ROW_TILE,), in_specs=[_row_spec(), _vec_spec(), _row_spec()],
        out_specs=[_vec_spec(8, 128), _row_spec(), _vec_spec(8)],
        out_shape=[jax.ShapeDtypeStruct((8, 128), F32), jax.ShapeDtypeStruct((SEQ, D_MODEL), F32),
                   jax.ShapeDtypeStruct((8, D_MODEL), F32)],
        name="final_loss", compiler_params=_params(("arbitrary",)),
    )(x, gain, target)


def _prenorm_bwd_epilogue(dh, x_ref, dxo_ref, g_ref, sc_ref, dx_ref, stats_ref, first):
    xv = x_ref[...]
    g = g_ref[...]
    r = lax.rsqrt(jnp.mean(xv * xv, axis=-1, keepdims=True) + RMS_EPS)
    xh = xv * r
    dn = dh * (1.0 + sc_ref[...])
    dxh = dn * g
    dx_ref[...] = dxo_ref[...] + r * (dxh - xh * jnp.mean(dxh * xh, axis=-1, keepdims=True))

    @pl.when(first)
    def _():
        stats_ref[...] = jnp.zeros_like(stats_ref)

    stats_ref[0:1, :] += jnp.sum(dh, axis=0, keepdims=True)
    stats_ref[1:2, :] += jnp.sum(dh * (xh * g), axis=0, keepdims=True)
    stats_ref[2:3, :] += jnp.sum(dn * xh, axis=0, keepdims=True)


def ffn_up(h, wg_all, wu_all, lf):
    def body(h_ref, wg_ref, wu_ref, a_ref, b_ref, s_ref):
        hv = h_ref[...]
        a = _dot(hv, wg_ref[...], NN)
        b = _dot(hv, wu_ref[...], NN)
        a_ref[...] = a
        b_ref[...] = b
        s_ref[...] = (a * _sigmoid(a) * b).astype(BF16)

    w_spec = pl.BlockSpec((None, None, D_MODEL, FF_SHARD), lambda j, i: (j, lf, 0, 0))
    o_spec = pl.BlockSpec((None, MM_TILE, FF_SHARD), lambda j, i: (j, i, 0))
    hid = (N_CHIPS, SEQ, FF_SHARD)
    return pl.pallas_call(
        body, grid=(N_CHIPS, SEQ // MM_TILE),
        in_specs=[pl.BlockSpec((MM_TILE, D_MODEL), lambda j, i: (i, 0)), w_spec, w_spec],
        out_specs=[o_spec, o_spec, o_spec],
        out_shape=[jax.ShapeDtypeStruct(hid, F32), jax.ShapeDtypeStruct(hid, F32), jax.ShapeDtypeStruct(hid, BF16)],
        name="ffn_up", compiler_params=_params(("parallel", "parallel")),
    )(h, wg_all, wu_all)


def matmul_residual(name, a, a_spec, w_all, w_spec, x, coef, mult):
    def epilogue(acc, ins, outs, ids):
        outs[0][...] = acc
        outs[1][...] = ins[2][...] + (mult * ins[3][...]) * acc

    row = pl.BlockSpec((MM_TILE, D_MODEL), lambda i, j: (i, 0))
    return _matmul(
        name, (SEQ // MM_TILE, N_CHIPS), N_CHIPS, 1, NN, 1,
        [a_spec, w_spec, row, pl.BlockSpec((1, D_MODEL), lambda i, j: (0, 0))], [row, row],
        [jax.ShapeDtypeStruct((SEQ, D_MODEL), F32)] * 2, (MM_TILE, D_MODEL), epilogue,
        (a, w_all, x, coef), ("parallel", "arbitrary"))


def ffn_down(s, wd_all, lf, x, gate):
    return matmul_residual(
        "ffn_down", s, pl.BlockSpec((None, MM_TILE, FF_SHARD), lambda i, j: (j, i, 0)),
        wd_all, pl.BlockSpec((None, None, FF_SHARD, D_MODEL), lambda i, j: (j, lf, 0, 0)), x, gate, 0.5)


def ffn_bwd_hidden(df, wd_all, lf, a, b):
    def epilogue(ds, ins, outs, ids):
        av, bv = ins[2][...], ins[3][...]
        sig = _sigmoid(av)
        outs[0][...] = (ds * bv * (sig * (1.0 + av * (1.0 - sig)))).astype(BF16)
        outs[1][...] = (ds * (av * sig)).astype(BF16)

    hid_spec = pl.BlockSpec((None, MM_TILE, FF_SHARD), lambda j, i: (j, i, 0))
    hid = jax.ShapeDtypeStruct((N_CHIPS, SEQ, FF_SHARD), BF16)
    return _matmul(
        "ffn_bwd_hidden", (N_CHIPS, SEQ // MM_TILE), 1, None, NT, 1,
        [pl.BlockSpec((MM_TILE, D_MODEL), lambda j, i: (i, 0)),
         pl.BlockSpec((None, None, FF_SHARD, D_MODEL), lambda j, i: (j, lf, 0, 0)), hid_spec, hid_spec],
        [hid_spec, hid_spec], [hid, hid], None, epilogue, (df, wd_all, a, b), ("parallel", "parallel"))


def grad_weight(name, lhs, lhs_spec, rhs, rhs_spec, g_all, out_block, out_index, acc_shape):
    def epilogue(acc, ins, outs, ids):
        outs[0][...] = acc.astype(BF16)

    return _matmul(
        name, (N_CHIPS, SEQ // MM_TILE), SEQ // MM_TILE, 1, TN, 1,
        [lhs_spec, rhs_spec, pl.BlockSpec(memory_space=pl.ANY)], [pl.BlockSpec(out_block, out_index)],
        [jax.ShapeDtypeStruct(g_all.shape, BF16)], acc_shape, epilogue, (lhs, rhs, g_all),
        ("parallel", "arbitrary"), aliases={2: 0})[0]


def ffn_grad_weights(h, s, df, da, db, g_gate, g_up, g_down, lf):
    tok = pl.BlockSpec((MM_TILE, D_MODEL), lambda j, k: (k, 0))
    hid = pl.BlockSpec((None, MM_TILE, FF_SHARD), lambda j, k: (j, k, 0))
    up_block, up_index = (None, None, D_MODEL, FF_SHARD), (lambda j, k: (j, lf, 0, 0))
    g_gate = grad_weight("grad_w_gate", h, tok, da, hid, g_gate, up_block, up_index, (D_MODEL, FF_SHARD))
    g_up = grad_weight("grad_w_up", h, tok, db, hid, g_up, up_block, up_index, (D_MODEL, FF_SHARD))
    g_down = grad_weight("grad_w_down", s, hid, df, tok, g_down, (None, None, FF_SHARD, D_MODEL),
                         lambda j, k: (j, lf, 0, 0), (FF_SHARD, D_MODEL))
    return g_gate, g_up, g_down


def matmul_prenorm_bwd(name, pairs, pair_specs, x, dxo, gain, scale):
    n = len(pairs)

    def epilogue(dh, ins, outs, ids):
        _prenorm_bwd_epilogue(dh, ins[n], ins[n + 1], ins[n + 2], ins[n + 3], outs[0], outs[1], ids[0] == 0)

    row = pl.BlockSpec((MM_TILE, D_MODEL), lambda i, j: (i, 0))
    vec = pl.BlockSpec((1, D_MODEL), lambda i, j: (0, 0))
    return _matmul(
        name, (SEQ // MM_TILE, N_CHIPS), N_CHIPS, 1, NT, len(pairs) // 2,
        list(pair_specs) + [row, row, vec, vec], [row, pl.BlockSpec((8, D_MODEL), lambda i, j: (0, 0))],
        [jax.ShapeDtypeStruct((SEQ, D_MODEL), F32), jax.ShapeDtypeStruct((8, D_MODEL), F32)],
        (MM_TILE, D_MODEL), epilogue, tuple(pairs) + (x, dxo, gain, scale), ("arbitrary", "arbitrary"))


def ffn_bwd_input(da, db, wg_all, wu_all, lf, x, dxo, gain, scale):
    hid = pl.BlockSpec((None, MM_TILE, FF_SHARD), lambda i, j: (j, i, 0))
    w = pl.BlockSpec((None, None, D_MODEL, FF_SHARD), lambda i, j: (j, lf, 0, 0))
    return matmul_prenorm_bwd("ffn_bwd_input", (da, wg_all, db, wu_all), (hid, w, hid, w), x, dxo, gain, scale)


def in_proj(h, w_all, l):
    def epilogue(acc, ins, outs, ids):
        outs[0][...] = acc

    return _matmul(
        "in_proj", (N_CHIPS, SEQ // MM_TILE), 1, None, NN, 1,
        [pl.BlockSpec((MM_TILE, D_MODEL), lambda j, i: (i, 0)),
         pl.BlockSpec((None, None, D_MODEL, IN_SHARD), lambda j, i: (j, l, 0, 0))],
        [pl.BlockSpec((MM_TILE, IN_SHARD), lambda j, i: (i, j))], [jax.ShapeDtypeStruct((SEQ, D_IN), F32)],
        None, epilogue, (h, w_all), ("parallel", "parallel"))[0]


_GATE_BLOCK0 = D_QKV // D_SHARD


def _branch_products(o, w_ref):
    ob = o.astype(BF16)
    return (_dot(ob[:, 0:256], w_ref[0:256, :], NN), _dot(ob[:, 256:384], w_ref[256:384, :], NN),
            _dot(ob[:, 384:768], w_ref[384:768, :], NN))


def merge_branches(o_cat, wbr_all, l, proj):
    def body(o_ref, w_ref, g0_ref, g1_ref, g2_ref, m_ref):
        u = _branch_products(o_ref[...], w_ref)
        m_ref[...] = (_sigmoid(g0_ref[...]) * u[0] + _sigmoid(g1_ref[...]) * u[1]
                      + _sigmoid(g2_ref[...]) * u[2]).astype(BF16)

    def gate_spec(b):
        return pl.BlockSpec((MM_TILE, D_SHARD), lambda i, j: (i, _GATE_BLOCK0 + 4 * b + j))

    return pl.pallas_call(
        body, grid=(SEQ // MM_TILE, N_CHIPS),
        in_specs=[pl.BlockSpec((MM_TILE, BR_ROWS), lambda i, j: (i, 0)),
                  pl.BlockSpec((None, None, BR_ROWS, D_SHARD), lambda i, j: (j, l, 0, 0)),
                  gate_spec(0), gate_spec(1), gate_spec(2)],
        out_specs=pl.BlockSpec((MM_TILE, D_SHARD), lambda i, j: (i, j)),
        out_shape=jax.ShapeDtypeStruct((SEQ, D_MODEL), BF16),
        name="merge_branches", compiler_params=_params(("parallel", "parallel")),
    )(o_cat, wbr_all, proj, proj, proj)


def out_proj(merged, wout_all, l, x, gate):
    return matmul_residual(
        "out_proj", merged, pl.BlockSpec((MM_TILE, D_SHARD), lambda i, j: (i, j)),
        wout_all, pl.BlockSpec((None, None, D_SHARD, D_MODEL), lambda i, j: (j, l, 0, 0)), x, gate, 1.0)


def merge_bwd(dmo, wout_all, o_cat, wbr_all, l, proj):
    def epilogue(dm, ins, outs, ids):
        u = _branch_products(ins[2][...], ins[3])
        for b in range(3):
            sig = _sigmoid(ins[4 + b][...])
            outs[b][...] = (dm * sig).astype(BF16)
            outs[3 + b][...] = (dm * u[b] * (sig * (1.0 - sig))).astype(BF16)

    def gate_spec(b):
        return pl.BlockSpec((MM_TILE, D_SHARD), lambda j, i: (i, _GATE_BLOCK0 + 4 * b + j))

    col = pl.BlockSpec((MM_TILE, D_SHARD), lambda j, i: (i, j))
    du = jax.ShapeDtypeStruct((SEQ, D_MODEL), BF16)
    return _matmul(
        "merge_bwd", (N_CHIPS, SEQ // MM_TILE), 1, None, NT, 1,
        [pl.BlockSpec((MM_TILE, D_MODEL), lambda j, i: (i, 0)),
         pl.BlockSpec((None, None, D_SHARD, D_MODEL), lambda j, i: (j, l, 0, 0)),
         pl.BlockSpec((MM_TILE, BR_ROWS), lambda j, i: (i, 0)),
         pl.BlockSpec((None, None, BR_ROWS, D_SHARD), lambda j, i: (j, l, 0, 0)),
         gate_spec(0), gate_spec(1), gate_spec(2)],
        [col] * 6, [du] * 6,
        None, epilogue, (dmo, wout_all, o_cat, wbr_all, proj, proj, proj), ("parallel", "parallel"))


def branch_bwd_input(du, wbr_all, l):
    def body(d0_ref, d1_ref, d2_ref, w_ref, o_ref, acc):
        j = pl.program_id(1)
        parts = (_dot(d0_ref[...], w_ref[0:256, :], NT), _dot(d1_ref[...], w_ref[256:384, :], NT),
                 _dot(d2_ref[...], w_ref[384:768, :], NT))

        @pl.when(j == 0)
        def _():
            acc[:, 0:256], acc[:, 256:384], acc[:, 384:768] = parts

        @pl.when(j > 0)
        def _():
            acc[:, 0:256] += parts[0]
            acc[:, 256:384] += parts[1]
            acc[:, 384:768] += parts[2]

        @pl.when(j == N_CHIPS - 1)
        def _():
            o_ref[...] = acc[...]

    col = pl.BlockSpec((MM_TILE, D_SHARD), lambda i, j: (i, j))
    return pl.pallas_call(
        body, grid=(SEQ // MM_TILE, N_CHIPS),
        in_specs=[col, col, col, pl.BlockSpec((None, None, BR_ROWS, D_SHARD), lambda i, j: (j, l, 0, 0))],
        out_specs=pl.BlockSpec((MM_TILE, BR_ROWS), lambda i, j: (i, 0)),
        out_shape=jax.ShapeDtypeStruct((SEQ, BR_ROWS), F32),
        scratch_shapes=[pltpu.VMEM((MM_TILE, BR_ROWS), F32)],
        name="branch_bwd_input", compiler_params=_params(("parallel", "arbitrary")),
    )(du[0], du[1], du[2], wbr_all)


def branch_grad_weights(o_cat, du, g_br, l):
    def body(o_ref, d0_ref, d1_ref, d2_ref, g_in, g_ref, acc):
        del g_in
        k = pl.program_id(1)
        ob = o_ref[...].astype(BF16)
        parts = (_dot(ob[:, 0:256], d0_ref[...], TN), _dot(ob[:, 256:384], d1_ref[...], TN),
                 _dot(ob[:, 384:768], d2_ref[...], TN))

        @pl.when(k == 0)
        def _():
            acc[0:256, :], acc[256:384, :], acc[384:768, :] = parts

        @pl.when(k > 0)
        def _():
            acc[0:256, :] += parts[0]
            acc[256:384, :] += parts[1]
            acc[384:768, :] += parts[2]

        @pl.when(k == SEQ // MM_TILE - 1)
        def _():
            g_ref[...] = acc[...].astype(BF16)

    col = pl.BlockSpec((MM_TILE, D_SHARD), lambda j, k: (k, j))
    return pl.pallas_call(
        body, grid=(N_CHIPS, SEQ // MM_TILE),
        in_specs=[pl.BlockSpec((MM_TILE, BR_ROWS), lambda j, k: (k, 0)), col, col, col,
                  pl.BlockSpec(memory_space=pl.ANY)],
        out_specs=pl.BlockSpec((None, None, BR_ROWS, D_SHARD), lambda j, k: (j, l, 0, 0)),
        out_shape=jax.ShapeDtypeStruct(g_br.shape, BF16),
        scratch_shapes=[pltpu.VMEM((BR_ROWS, D_SHARD), F32)], input_output_aliases={4: 0},
        name="branch_grad_weights", compiler_params=_params(("parallel", "arbitrary")),
    )(o_cat, du[0], du[1], du[2], g_br)


def mixer_bwd_input(dproj, win_all, l, x, dxo, gain, scale):
    return matmul_prenorm_bwd(
        "mixer_bwd_input", (dproj, win_all),
        (pl.BlockSpec((MM_TILE, IN_SHARD), lambda i, j: (i, j)),
         pl.BlockSpec((None, None, D_MODEL, IN_SHARD), lambda i, j: (j, l, 0, 0))), x, dxo, gain, scale)


def _split_dot(v, tri):
    hi = v.astype(BF16)
    lo = (v - hi.astype(F32)).astype(BF16)
    return _dot(hi, tri, NN) + _dot(lo, tri, NN)


def _tri(cmp):
    return cmp(lax.broadcasted_iota(jnp.int32, (BLK, BLK), 0), lax.broadcasted_iota(jnp.int32, (BLK, BLK), 1)).astype(BF16)


def _sb_scores(qs, k_ref, i, j):
    rows = pl.ds(pl.multiple_of(j * BLK, BLK), BLK)
    z = _dot(qs, k_ref[rows, :], NT)
    q_pos = i * BLK + lax.broadcasted_iota(jnp.int32, (BLK, BLK), 0)
    k_pos = j * BLK + lax.broadcasted_iota(jnp.int32, (BLK, BLK), 1)
    before = k_pos < q_pos
    soft = jnp.log(1.0 + jnp.exp(-jnp.abs(z)))
    log_fail = jnp.where(before, -(jnp.maximum(z, 0.0) + soft), 0.0)
    log_hit = jnp.minimum(z, 0.0) - soft
    return rows, before, log_fail, log_hit


def sb_forward(q, k, v):
    def body(q_ref, k_ref, v_ref, o_ref, tot_ref):
        i = pl.program_id(1)
        qs = q_ref[...]
        later = _tri(lambda r, c: r > c)

        def step(t, carry):
            o, run = carry
            rows, before, log_fail, log_hit = _sb_scores(qs, k_ref, i, i - t)
            between = _split_dot(log_fail, later) + run
            w = jnp.where(before, jnp.exp(log_hit + between), 0.0)
            o = o + _dot(w.astype(BF16), v_ref[rows, :], NN)
            return o, run + jnp.sum(log_fail, axis=1, keepdims=True)

        o, run = lax.fori_loop(0, i + 1, step, (jnp.zeros((BLK, HEAD_DIM), F32), jnp.zeros((BLK, 1), F32)))
        o_ref[...] = o
        tot_ref[...] = run

    blk = pl.BlockSpec((None, BLK, HEAD_DIM), lambda h, i: (h, i, 0))
    col = pl.BlockSpec((None, BLK, 1), lambda h, i: (h, i, 0))
    full = pl.BlockSpec((None, SEQ, HEAD_DIM), lambda h, i: (h, 0, 0))
    return pl.pallas_call(
        body, grid=(H_SB, SEQ // BLK), in_specs=[blk, full, full], out_specs=[blk, col],
        out_shape=[jax.ShapeDtypeStruct((H_SB, SEQ, HEAD_DIM), F32), jax.ShapeDtypeStruct((H_SB, SEQ, 1), F32)],
        name="sb_forward", compiler_params=_params(("parallel", "parallel")),
    )(q, k, v)


def sb_backward(q, k, v, total, do):
    def body(q_ref, k_ref, v_ref, tot_ref, do_ref, dq_ref, dk_ref, dv_ref):
        i = pl.program_id(1)

        @pl.when(i == 0)
        def _():
            dk_ref[...] = jnp.zeros_like(dk_ref)
            dv_ref[...] = jnp.zeros_like(dv_ref)

        qs = q_ref[...]
        dob = do_ref[...].astype(BF16)
        total_v = tot_ref[...]
        upto = _tri(lambda r, c: r <= c)
        earlier = _tri(lambda r, c: r < c)

        def step(j, carry):
            dq, seen, g_seen = carry
            rows, before, log_fail, log_hit = _sb_scores(qs, k_ref, i, j)
            between = total_v - (seen + _split_dot(log_fail, upto))
            w = jnp.where(before, jnp.exp(log_hit + between), 0.0)
            g = _dot(dob, v_ref[rows, :], NT) * w
            g_earlier = g_seen + _split_dot(g, earlier)
            sig = jnp.exp(log_hit)
            dz = jnp.where(before, g * (1.0 - sig) - g_earlier * sig, 0.0).astype(BF16)
            dq = dq + _dot(dz, k_ref[rows, :], NN)
            dk_ref[rows, :] += _dot(dz, qs, TN)
            dv_ref[rows, :] += _dot(w.astype(BF16), dob, TN)
            return dq, seen + jnp.sum(log_fail, axis=1, keepdims=True), g_seen + jnp.sum(g, axis=1, keepdims=True)

        zero = jnp.zeros((BLK, 1), F32)
        dq, _, _ = lax.fori_loop(0, i + 1, step, (jnp.zeros((BLK, HEAD_DIM), F32), zero, zero))
        dq_ref[...] = dq * QK_SCALE

    blk = pl.BlockSpec((None, BLK, HEAD_DIM), lambda h, i: (h, i, 0))
    col = pl.BlockSpec((None, BLK, 1), lambda h, i: (h, i, 0))
    full = pl.BlockSpec((None, SEQ, HEAD_DIM), lambda h, i: (h, 0, 0))
    shape = jax.ShapeDtypeStruct((H_SB, SEQ, HEAD_DIM), F32)
    return pl.pallas_call(
        body, grid=(H_SB, SEQ // BLK), in_specs=[blk, full, full, col, blk], out_specs=[blk, full, full],
        out_shape=[shape, shape, shape],
        name="sb_backward", compiler_params=_params(("parallel", "arbitrary")),
    )(q, k, v, total, do)


def _band_scores(q_ref, kp_ref, ko_ref, bias_ref, has_prev):
    qs = q_ref[...]
    s_prev = _dot(qs, kp_ref[...], NT) + bias_ref[:, 0:BLK]
    s_prev = jnp.where(has_prev, s_prev, NEG)
    s_own = _dot(qs, ko_ref[...], NT) + bias_ref[:, BLK:2 * BLK]
    return qs, s_prev, s_own


def banded_forward(name, q, k, v, bias, sinks, group, prev_mask):
    hq_n, t_n = q.shape[0], q.shape[1]

    def body(sink_ref, q_ref, kp_ref, ko_ref, vp_ref, vo_ref, bias_ref, o_ref, lse_ref):
        hq, b = pl.program_id(0), pl.program_id(1)
        has_prev = (b & prev_mask(hq)) != 0
        _, s_prev, s_own = _band_scores(q_ref, kp_ref, ko_ref, bias_ref, has_prev)
        sink = sink_ref[hq]
        m = jnp.maximum(jnp.maximum(jnp.max(s_prev, axis=1, keepdims=True), jnp.max(s_own, axis=1, keepdims=True)), sink)
        p_prev = jnp.exp(s_prev - m)
        p_own = jnp.exp(s_own - m)
        denom = jnp.sum(p_prev, axis=1, keepdims=True) + jnp.sum(p_own, axis=1, keepdims=True) + jnp.exp(sink - m)
        o = _dot(p_prev.astype(BF16), vp_ref[...], NN) + _dot(p_own.astype(BF16), vo_ref[...], NN)
        o_ref[...] = o / denom
        lse_ref[...] = m + jnp.log(denom)

    qb = pl.BlockSpec((None, BLK, HEAD_DIM), lambda h, b: (h, b, 0))
    own = pl.BlockSpec((None, BLK, HEAD_DIM), lambda h, b: (h // group, b, 0))
    prev = pl.BlockSpec((None, BLK, HEAD_DIM), lambda h, b: (h // group, jnp.maximum(b - 1, 0), 0))
    return pl.pallas_call(
        body, grid=(hq_n, t_n // BLK),
        in_specs=[pl.BlockSpec(memory_space=pltpu.SMEM), qb, prev, own, prev, own,
                  pl.BlockSpec((None, BLK, 2 * BLK), lambda h, b: (h, 0, 0))],
        out_specs=[qb, pl.BlockSpec((None, BLK, 1), lambda h, b: (h, b, 0))],
        out_shape=[jax.ShapeDtypeStruct((hq_n, t_n, HEAD_DIM), F32), jax.ShapeDtypeStruct((hq_n, t_n, 1), F32)],
        name=name, compiler_params=_params(("parallel", "parallel")),
    )(sinks, q, k, k, v, v, bias)


def banded_backward(name, q, k, v, bias, sinks, o, lse, do, dlse, group, prev_mask):
    hq_n, t_n, hkv_n = q.shape[0], q.shape[1], k.shape[0]

    def body(sink_ref, q_ref, kp_ref, ko_ref, vp_ref, vo_ref, bias_ref, o_ref, lse_ref, do_ref, dlse_ref,
             dq_ref, dk_ref, dv_ref, dbias_ref, dsink_ref):
        hkv, g, b = pl.program_id(0), pl.program_id(1), pl.program_id(2)
        hq = hkv * group + g
        has_prev = (b & prev_mask(hq)) != 0

        @pl.when(jnp.logical_and(g == 0, b == 0))
        def _():
            dk_ref[...] = jnp.zeros_like(dk_ref)
            dv_ref[...] = jnp.zeros_like(dv_ref)

        @pl.when(b == 0)
        def _():
            dbias_ref[...] = jnp.zeros_like(dbias_ref)
            dsink_ref[...] = jnp.zeros_like(dsink_ref)

        qs, s_prev, s_own = _band_scores(q_ref, kp_ref, ko_ref, bias_ref, has_prev)
        lse_v = lse_ref[...]
        dov = do_ref[...]
        dob = dov.astype(BF16)
        shift = dlse_ref[...] - jnp.sum(dov * o_ref[...], axis=1, keepdims=True)
        p_prev = jnp.exp(s_prev - lse_v)
        p_own = jnp.exp(s_own - lse_v)
        ds_prev = p_prev * (_dot(dob, vp_ref[...], NT) + shift)
        ds_own = p_own * (_dot(dob, vo_ref[...], NT) + shift)
        dbias_ref[:, 0:BLK] += ds_prev
        dbias_ref[:, BLK:2 * BLK] += ds_own
        dsink_ref[...] += jnp.sum(jnp.exp(sink_ref[hq] - lse_v) * shift)
        ds_prev = ds_prev.astype(BF16)
        ds_own = ds_own.astype(BF16)
        dq_ref[...] = (_dot(ds_prev, kp_ref[...], NN) + _dot(ds_own, ko_ref[...], NN)) * QK_SCALE
        rows_prev = pl.ds(pl.multiple_of(jnp.maximum(b - 1, 0) * BLK, BLK), BLK)
        rows_own = pl.ds(pl.multiple_of(b * BLK, BLK), BLK)
        dk_ref[rows_prev, :] += _dot(ds_prev, qs, TN)
        dk_ref[rows_own, :] += _dot(ds_own, qs, TN)
        dv_ref[rows_prev, :] += _dot(p_prev.astype(BF16), dob, TN)
        dv_ref[rows_own, :] += _dot(p_own.astype(BF16), dob, TN)

    def qspec(width):
        return pl.BlockSpec((None, BLK, width), lambda hk, g, b: (hk * group + g, b, 0))

    own = pl.BlockSpec((None, BLK, HEAD_DIM), lambda hk, g, b: (hk, b, 0))
    prev = pl.BlockSpec((None, BLK, HEAD_DIM), lambda hk, g, b: (hk, jnp.maximum(b - 1, 0), 0))
    kv_full = pl.BlockSpec((None, t_n, HEAD_DIM), lambda hk, g, b: (hk, 0, 0))
    per_head = lambda rows, width: pl.BlockSpec((None, rows, width), lambda hk, g, b: (hk * group + g, 0, 0))
    kv_shape = jax.ShapeDtypeStruct((hkv_n, t_n, HEAD_DIM), F32)
    return pl.pallas_call(
        body, grid=(hkv_n, group, t_n // BLK),
        in_specs=[pl.BlockSpec(memory_space=pltpu.SMEM), qspec(HEAD_DIM), prev, own, prev, own,
                  per_head(BLK, 2 * BLK), qspec(HEAD_DIM), qspec(1), qspec(HEAD_DIM), qspec(1)],
        out_specs=[qspec(HEAD_DIM), kv_full, kv_full, per_head(BLK, 2 * BLK), per_head(1, BLK)],
        out_shape=[jax.ShapeDtypeStruct((hq_n, t_n, HEAD_DIM), F32), kv_shape, kv_shape,
                   jax.ShapeDtypeStruct((hq_n, BLK, 2 * BLK), F32), jax.ShapeDtypeStruct((hq_n, 1, BLK), F32)],
        name=name, compiler_params=_params(("parallel", "arbitrary", "arbitrary")),
    )(sinks, q, k, k, v, v, bias, o, lse, do, dlse)


def _dil_prev_mask(hq):
    group = hq // H_PER_DIL
    return jnp.where(group == 0, 15, jnp.where(group == 1, 3, 0))


def _swa_prev_mask(hq):
    del hq
    return 15


def dilated_merge(o, lse):
    def body(o_ref, l_ref, out_ref):
        lv = l_ref[...]
        m = jnp.max(lv, axis=0, keepdims=True)
        e = jnp.exp(lv - m)
        alpha = e / jnp.sum(e, axis=0, keepdims=True)
        out_ref[...] = jnp.sum(alpha * o_ref[...], axis=0)

    return pl.pallas_call(
        body, grid=(H_PER_DIL, SEQ // ROW_TILE),
        in_specs=[pl.BlockSpec((3, None, ROW_TILE, HEAD_DIM), lambda h, i: (0, h, i, 0)),
                  pl.BlockSpec((3, None, ROW_TILE, 1), lambda h, i: (0, h, i, 0))],
        out_specs=pl.BlockSpec((None, ROW_TILE, HEAD_DIM), lambda h, i: (h, i, 0)),
        out_shape=jax.ShapeDtypeStruct((H_PER_DIL, SEQ, HEAD_DIM), F32),
        name="dilated_merge", compiler_params=_params(("parallel", "parallel")),
    )(o, lse)


def dilated_merge_bwd(o, lse, dout):
    def body(o_ref, l_ref, d_ref, do_ref, dl_ref):
        lv = l_ref[...]
        m = jnp.max(lv, axis=0, keepdims=True)
        e = jnp.exp(lv - m)
        alpha = e / jnp.sum(e, axis=0, keepdims=True)
        dv = d_ref[...][None]
        do_ref[...] = alpha * dv
        dalpha = jnp.sum(dv * o_ref[...], axis=-1, keepdims=True)
        dl_ref[...] = alpha * (dalpha - jnp.sum(alpha * dalpha, axis=0, keepdims=True))

    o_spec = pl.BlockSpec((3, None, ROW_TILE, HEAD_DIM), lambda h, i: (0, h, i, 0))
    l_spec = pl.BlockSpec((3, None, ROW_TILE, 1), lambda h, i: (0, h, i, 0))
    return pl.pallas_call(
        body, grid=(H_PER_DIL, SEQ // ROW_TILE),
        in_specs=[o_spec, l_spec, pl.BlockSpec((None, ROW_TILE, HEAD_DIM), lambda h, i: (h, i, 0))],
        out_specs=[o_spec, l_spec],
        out_shape=[jax.ShapeDtypeStruct(o.shape, F32), jax.ShapeDtypeStruct(lse.shape, F32)],
        name="dilated_merge_bwd", compiler_params=_params(("parallel", "parallel")),
    )(o, lse, dout)


def rel_bias_reduce(dbias0, dbias1, bucket):
    def body(d0_ref, d1_ref, b_ref, o_ref):
        dv, bv = d0_ref[...] + d1_ref[...], b_ref[...]
        lane = lax.broadcasted_iota(jnp.int32, (1, BLK), 1)
        acc = jnp.zeros((1, BLK), F32)
        for bkt in range(N_BUCKETS):
            acc = acc + jnp.where(lane == bkt, jnp.sum(jnp.where(bv == bkt, dv, 0.0)), 0.0)
        o_ref[...] = acc

    tile = pl.BlockSpec((None, BLK, 2 * BLK), lambda h: (h, 0, 0))
    return pl.pallas_call(
        body, grid=(dbias0.shape[0],), in_specs=[tile, tile, tile],
        out_specs=pl.BlockSpec((None, 1, BLK), lambda h: (h, 0, 0)),
        out_shape=jax.ShapeDtypeStruct((dbias0.shape[0], 1, BLK), F32),
        name="rel_bias_reduce", compiler_params=_params(("parallel",)),
    )(dbias0, dbias1, bucket)


def _heads(t):
    return t.reshape(SEQ, -1, HEAD_DIM).transpose(1, 0, 2)


def _unheads(t):
    return t.transpose(1, 0, 2).reshape(SEQ, -1)


def _dilate(t):
    parts = []
    for g, (_, d) in enumerate(DIL_PATTERNS):
        tg = t[:, 128 * g:128 * (g + 1)].reshape(SEQ // d, d, H_PER_DIL, HEAD_DIM).transpose(2, 1, 0, 3)
        parts.append(tg.reshape(H_PER_DIL, SEQ, HEAD_DIM))
    return jnp.concatenate(parts, axis=0)


def _undilate(t):
    outs = []
    for g, (_, d) in enumerate(DIL_PATTERNS):
        tg = t[2 * g:2 * g + 2].reshape(H_PER_DIL, d, SEQ // d, -1).transpose(0, 2, 1, 3)
        outs.append(tg.reshape(H_PER_DIL, SEQ, -1))
    return jnp.stack(outs)


def _redilate(t):
    parts = []
    for g, (_, d) in enumerate(DIL_PATTERNS):
        tg = t[g].reshape(H_PER_DIL, SEQ // d, d, -1).transpose(0, 2, 1, 3)
        parts.append(tg.reshape(H_PER_DIL, SEQ, -1))
    return jnp.concatenate(parts, axis=0)


def _t5_bucket(n):
    max_exact = N_BUCKETS // 2
    nf = jnp.maximum(n, 1).astype(F32)
    large = max_exact + (jnp.log(nf / max_exact) / math.log(MAX_REL_DIST / max_exact)
                         * (N_BUCKETS - max_exact)).astype(jnp.int32)
    large = jnp.minimum(large, N_BUCKETS - 1)
    return jnp.where(n < max_exact, n, large)


def band_tables(rel_bias):
    rel = jnp.arange(BLK)[:, None] + BLK - jnp.arange(2 * BLK)[None, :]
    tiles, buckets = [], []
    patterns = [(d, w // d) for w, d in DIL_PATTERNS for _ in range(H_PER_DIL)] + [(1, SWA_WINDOW - 1)] * H_SWA_Q
    for col, (d, max_dist) in enumerate(patterns):
        band = (rel >= 0) & (rel <= max_dist)
        bucket = _t5_bucket(jnp.maximum(rel, 0) * d)
        tiles.append(jnp.where(band, rel_bias[bucket, col], NEG))
        buckets.append(jnp.where(band, bucket, -1))
    tiles = jnp.stack(tiles).astype(F32)
    return tiles[:H_DIL], tiles[H_DIL:], jnp.stack(buckets).astype(jnp.int32)


def _vec(v):
    return v.reshape(1, D_MODEL)


def ffn_forward(lf, x, gain, mod, w):
    h = prenorm(x, _vec(gain), _vec(mod[1]), _vec(mod[0]))
    a, b, s = ffn_up(h, w["gate"], w["up"], lf)
    f, xo = ffn_down(s, w["down"], lf, x, _vec(mod[2]))
    return xo, (x, h, a, b, s, f)


def ffn_backward(lf, dxo, saved, gain, mod, w, g):
    x, h, a, b, s, f = saved
    df, dgate = resid_bwd(dxo, f, _vec(mod[2]), 0.5)
    da, db = ffn_bwd_hidden(df, w["down"], lf, a, b)
    g["gate"], g["up"], g["down"] = ffn_grad_weights(h, s, df, da, db, g["gate"], g["up"], g["down"], lf)
    dx, stats = ffn_bwd_input(da, db, w["gate"], w["up"], lf, x, dxo, _vec(gain), _vec(mod[1]))
    return dx, jnp.stack([stats[0], stats[1], dgate[0]]), stats[2]


def mixer_forward(l, x, gain, mod, sinks, bias_dil, bias_swa, w):
    h = prenorm(x, _vec(gain), _vec(mod[1]), _vec(mod[0]))
    proj = in_proj(h, w["in"], l)
    qkv = proj[:, :D_QKV].astype(BF16)
    q_sb, k_sb, v_sb = _heads(qkv[:, 0:256] * QK_SCALE), _heads(qkv[:, 256:512]), _heads(qkv[:, 512:768])
    q_dil, k_dil, v_dil = _dilate(qkv[:, 768:1152] * QK_SCALE), _dilate(qkv[:, 1152:1536]), _dilate(qkv[:, 1536:1920])
    q_swa, k_swa, v_swa = _heads(qkv[:, 1920:2304] * QK_SCALE), _heads(qkv[:, 2304:2432]), _heads(qkv[:, 2432:2560])
    o_sb, total_sb = sb_forward(q_sb, k_sb, v_sb)
    no_sinks = jnp.full((H_DIL,), NEG, F32)
    o_dd, lse_dd = banded_forward("dilated_forward", q_dil, k_dil, v_dil, bias_dil, no_sinks, 1, _dil_prev_mask)
    o_dt, lse_dt = _undilate(o_dd), _undilate(lse_dd)
    o_dil = dilated_merge(o_dt, lse_dt)
    o_swa, lse_swa = banded_forward("swa_forward", q_swa, k_swa, v_swa, bias_swa, sinks, H_SWA_Q // H_SWA_KV,
                                    _swa_prev_mask)
    o_cat = jnp.concatenate([_unheads(o_sb), _unheads(o_dil), _unheads(o_swa)], axis=1)
    merged = merge_branches(o_cat, w["br"], l, proj)
    mo, xo = out_proj(merged, w["out"], l, x, _vec(mod[2]))
    saved = (x, h, proj, (q_sb, k_sb, v_sb, total_sb), (q_dil, k_dil, v_dil, o_dd, lse_dd, o_dt, lse_dt),
             (q_swa, k_swa, v_swa, o_swa, lse_swa), o_cat, merged, mo)
    return xo, saved


def mixer_backward(l, dxo, saved, gain, mod, sinks, bias_dil, bias_swa, w, g):
    x, h, proj, sb, dil, swa, o_cat, merged, mo = saved
    dmo, dgate = resid_bwd(dxo, mo, _vec(mod[2]), 1.0)
    tok = pl.BlockSpec((MM_TILE, D_MODEL), lambda j, k: (k, 0))
    g["out"] = grad_weight("grad_w_out", merged, pl.BlockSpec((MM_TILE, D_SHARD), lambda j, k: (k, j)), dmo, tok,
                           g["out"], (None, None, D_SHARD, D_MODEL), lambda j, k: (j, l, 0, 0), (D_SHARD, D_MODEL))
    du0, du1, du2, dg0, dg1, dg2 = merge_bwd(dmo, w["out"], o_cat, w["br"], l, proj)
    du = (du0, du1, du2)
    do_cat = branch_bwd_input(du, w["br"], l)
    g["br"] = branch_grad_weights(o_cat, du, g["br"], l)

    q_sb, k_sb, v_sb, total_sb = sb
    dq_sb, dk_sb, dv_sb = sb_backward(q_sb, k_sb, v_sb, total_sb, _heads(do_cat[:, 0:256]))

    q_dil, k_dil, v_dil, o_dd, lse_dd, o_dt, lse_dt = dil
    do_dt, dlse_dt = dilated_merge_bwd(o_dt, lse_dt, _heads(do_cat[:, 256:384]))
    no_sinks = jnp.full((H_DIL,), NEG, F32)
    dq_dil, dk_dil, dv_dil, dbias_dil, _ = banded_backward(
        "dilated_backward", q_dil, k_dil, v_dil, bias_dil, no_sinks, o_dd, lse_dd, _redilate(do_dt), _redilate(dlse_dt),
        1, _dil_prev_mask)

    q_swa, k_swa, v_swa, o_swa, lse_swa = swa
    dq_swa, dk_swa, dv_swa, dbias_swa, dsinks = banded_backward(
        "swa_backward", q_swa, k_swa, v_swa, bias_swa, sinks, o_swa, lse_swa, _heads(do_cat[:, 384:768]),
        jnp.zeros_like(lse_swa), H_SWA_Q // H_SWA_KV, _swa_prev_mask)

    def tokens(t):
        return _undilate(t).transpose(2, 0, 1, 3).reshape(SEQ, -1)

    dproj = jnp.concatenate(
        [_unheads(dq_sb), _unheads(dk_sb), _unheads(dv_sb), tokens(dq_dil), tokens(dk_dil), tokens(dv_dil),
         _unheads(dq_swa), _unheads(dk_swa), _unheads(dv_swa)], axis=1).astype(BF16)
    dproj = jnp.concatenate([dproj, dg0, dg1, dg2], axis=1)
    g["in"] = grad_weight("grad_w_in", h, tok, dproj, pl.BlockSpec((MM_TILE, IN_SHARD), lambda j, k: (k, j)),
                          g["in"], (None, None, D_MODEL, IN_SHARD), lambda j, k: (j, l, 0, 0), (D_MODEL, IN_SHARD))
    dx, stats = mixer_bwd_input(dproj, w["in"], l, x, dxo, _vec(gain), _vec(mod[1]))
    dmod = jnp.stack([stats[0], stats[1], dgate[0]])
    return dx, dmod, stats[2], jnp.concatenate([dbias_dil, dbias_swa], axis=0), dsinks[:, 0, 0]


GRAD_SHAPES = {
    "gate": (N_CHIPS, 2 * DEPTH, D_MODEL, FF_SHARD), "up": (N_CHIPS, 2 * DEPTH, D_MODEL, FF_SHARD),
    "down": (N_CHIPS, 2 * DEPTH, FF_SHARD, D_MODEL), "in": (N_CHIPS, DEPTH, D_MODEL, IN_SHARD),
    "br": (N_CHIPS, DEPTH, BR_ROWS, D_SHARD), "out": (N_CHIPS, DEPTH, D_SHARD, D_MODEL),
}


def device_step(x, target, mod, gains, final_gain, sinks, rel_bias, w):
    bias_dil, bias_swa, bucket = band_tables(rel_bias)
    saved = []
    for l in range(DEPTH):
        x, s0 = ffn_forward(2 * l, x, gains[l, 0], mod[l, 0], w)
        x, s1 = mixer_forward(l, x, gains[l, 1], mod[l, 1], sinks[l], bias_dil, bias_swa, w)
        x, s2 = ffn_forward(2 * l + 1, x, gains[l, 2], mod[l, 2], w)
        saved.append((s0, s1, s2))
    loss, dx, dfinal = final_loss(x, _vec(final_gain), target)

    g = {k: lax.empty(shape, BF16) for k, shape in GRAD_SHAPES.items()}
    dmod, dgains, dbias, dsinks = [None] * DEPTH, [None] * DEPTH, [None] * DEPTH, [None] * DEPTH
    for l in reversed(range(DEPTH)):
        s0, s1, s2 = saved[l]
        dx, dm2, dg2 = ffn_backward(2 * l + 1, dx, s2, gains[l, 2], mod[l, 2], w, g)
        dx, dm1, dg1, dbias[l], dsinks[l] = mixer_backward(l, dx, s1, gains[l, 1], mod[l, 1], sinks[l], bias_dil,
                                                           bias_swa, w, g)
        dx, dm0, dg0 = ffn_backward(2 * l, dx, s0, gains[l, 0], mod[l, 0], w, g)
        dmod[l] = jnp.stack([dm0, dm1, dm2])
        dgains[l] = jnp.stack([dg0, dg1, dg2])
    drel = rel_bias_reduce(dbias[0], dbias[1], bucket)[:, 0, :N_BUCKETS].T
    return loss, dx, g, jnp.stack(dmod), jnp.stack(dgains), dfinal[0], jnp.stack(dsinks), drel


MESH = pl.DeviceIdType.MESH
CHIP_FLIPS = ((1, 0), (0, 1), (1, 1))
ANY = pl.BlockSpec(memory_space=pl.ANY)


def _position():
    return lax.axis_index("x"), lax.axis_index("y"), lax.axis_index("c")


def all_gather_small(name, piece):
    def body(x_ref, out_ref, send_sems, recv_sems, local_sem):
        x, y, c = _position()
        me, sibling = (x, y, c), (x, y, 1 - c)
        chips = [(x ^ fx, y ^ fy) for fx, fy in CHIP_FLIPS]

        def rows(px, py, pc):
            return out_ref.at[4 * px + 2 * py + pc]

        def copy(k, block, to, src=None):
            return pltpu.make_async_remote_copy(
                src_ref=rows(*block) if src is None else src, dst_ref=rows(*block),
                send_sem=send_sems.at[k], recv_sem=recv_sems.at[k], device_id=to, device_id_type=MESH)

        mine = pltpu.make_async_copy(x_ref, rows(*me), local_sem)
        mine.start()
        first = [copy(0, me, sibling, src=x_ref)]
        first += [copy(1 + j, me, (*chip, c), src=x_ref) for j, chip in enumerate(chips)]
        for cp in first:
            cp.start()
        passed = [copy(4 + j, (*chip, c), sibling) for j, chip in enumerate(chips)]
        for j, chip in enumerate(chips):
            copy(1 + j, (*chip, c), me).wait_recv()
            passed[j].start()
        copy(0, sibling, me).wait_recv()
        for j, chip in enumerate(chips):
            copy(4 + j, (*chip, 1 - c), me).wait_recv()
        for cp in first + passed:
            cp.wait_send()
        mine.wait()

    return pl.pallas_call(
        body, out_shape=jax.ShapeDtypeStruct((N_DEV,) + piece.shape, piece.dtype),
        in_specs=[pl.BlockSpec(memory_space=pltpu.VMEM)], out_specs=pl.BlockSpec(memory_space=pltpu.VMEM),
        scratch_shapes=[pltpu.SemaphoreType.DMA((7,)), pltpu.SemaphoreType.DMA((7,)), pltpu.SemaphoreType.DMA],
        name=name,
    )(piece)


def exchange(name, operands, out_shapes, aliases, plan):
    n_in, n_out = len(operands), len(out_shapes)

    def body(*refs):
        ins, outs = refs[:n_in], refs[n_in:n_in + n_out]
        send_sems, recv_sems, local_sems = refs[n_in + n_out:]
        x, y, c = _position()
        local, sends, recvs = plan(ins, outs, x, y, c)
        local = [pltpu.make_async_copy(s, d, local_sems.at[k]) for k, (s, d) in enumerate(local)]
        for cp in local:
            cp.start()
        remote = [pltpu.make_async_remote_copy(src_ref=s, dst_ref=d, send_sem=send_sems.at[k], recv_sem=recv_sems.at[k],
                                               device_id=dev, device_id_type=MESH)
                  for k, (s, d, dev) in enumerate(sends)]
        for cp in remote:
            cp.start()
        for k, r in enumerate(recvs):
            pltpu.make_async_remote_copy(src_ref=r, dst_ref=r, send_sem=send_sems.at[k], recv_sem=recv_sems.at[k],
                                         device_id=(x, y, c), device_id_type=MESH).wait_recv()
        for cp in remote:
            cp.wait_send()
        for cp in local:
            cp.wait()

    n_sends, n_local = plan.n_sends, max(plan.n_local, 1)
    return pl.pallas_call(
        body, out_shape=out_shapes, in_specs=[ANY] * n_in, out_specs=[ANY] * n_out,
        scratch_shapes=[pltpu.SemaphoreType.DMA((n_sends,)), pltpu.SemaphoreType.DMA((n_sends,)),
                        pltpu.SemaphoreType.DMA((n_local,))],
        input_output_aliases=aliases, name=name,
    )(*operands)


def _plan(n_local, n_sends):
    def wrap(fn):
        fn.n_local, fn.n_sends = n_local, n_sends
        return fn
    return wrap


def _half(ref, axis, c):
    rows = ref.shape[axis] // 2
    idx = [slice(None)] * len(ref.shape)
    idx[axis] = pl.ds(pl.multiple_of(c * rows, 16), rows)
    return ref.at[tuple(idx)]


def gather_weights(shards):
    n = len(shards)

    @_plan(n, 3 * n)
    def over_chips(ins, outs, x, y, c):
        me = 2 * x + y
        local = [(s, o.at[me]) for s, o in zip(ins, outs)]
        sends, recvs = [], []
        for s, o in zip(ins, outs):
            for fx, fy in CHIP_FLIPS:
                sends.append((_half(s, 1, c), _half(o.at[me], 1, c), (x ^ fx, y ^ fy, c)))
                recvs.append(_half(o.at[2 * (x ^ fx) + (y ^ fy)], 1, c))
        return local, sends, recvs

    @_plan(0, 3 * n)
    def to_sibling(ins, outs, x, y, c):
        sends, recvs = [], []
        for o in outs:
            for fx, fy in CHIP_FLIPS:
                slab = o.at[2 * (x ^ fx) + (y ^ fy)]
                sends.append((_half(slab, 1, c), _half(slab, 1, c), (x, y, 1 - c)))
                recvs.append(_half(slab, 1, 1 - c))
        return [], sends, recvs

    shapes = [jax.ShapeDtypeStruct((N_CHIPS,) + s.shape, s.dtype) for s in shards]
    halves = exchange("gather_weights_chips", shards, shapes, {}, over_chips)
    return exchange("gather_weights_sibling", halves, shapes, {k: k for k in range(n)}, to_sibling)


def reduce_gradients(grads, xi, yi, ci):
    n = len(grads)
    chip = (2 * xi + yi).astype(jnp.int32).reshape(1)
    core = ci.astype(jnp.int32).reshape(1)

    @_plan(0, n)
    def swap_halves(ins, outs, x, y, c):
        sends = [(_half(g, 2, 1 - c), o, (x, y, 1 - c)) for g, o in zip(ins, outs)]
        return [], sends, list(outs)

    half_shapes = [jax.ShapeDtypeStruct(g.shape[:2] + (g.shape[2] // 2, g.shape[3]), BF16) for g in grads]
    landed = exchange("reduce_swap_halves", grads, half_shapes, {}, swap_halves)
    chip_sums = [_add_halves(g, la, core) for g, la in zip(grads, landed)]

    @_plan(0, 3 * n)
    def over_chips(ins, outs, x, y, c):
        sends, recvs = [], []
        for s, o in zip(ins, outs):
            for k, (fx, fy) in enumerate(CHIP_FLIPS):
                sends.append((s.at[2 * (x ^ fx) + (y ^ fy)], o.at[k], (x ^ fx, y ^ fy, c)))
                recvs.append(o.at[k])
        return [], sends, recvs

    land_shapes = [jax.ShapeDtypeStruct((3,) + s.shape[1:], BF16) for s in chip_sums]
    from_chips = exchange("reduce_over_chips", chip_sums, land_shapes, {}, over_chips)
    halves = [_add_chips(s, la, chip) for s, la in zip(chip_sums, from_chips)]

    @_plan(n, n)
    def share_halves(ins, outs, x, y, c):
        local = [(h, _half(o, 1, c)) for h, o in zip(ins, outs)]
        sends = [(h, _half(o, 1, c), (x, y, 1 - c)) for h, o in zip(ins, outs)]
        return local, sends, [_half(o, 1, 1 - c) for o in outs]

    full_shapes = [jax.ShapeDtypeStruct((h.shape[0], 2 * h.shape[1], h.shape[2]), F32) for h in halves]
    return exchange("reduce_share_halves", halves, full_shapes, {}, share_halves)


def _row_tile(rows, cols):
    best = 16
    for t in range(16, rows + 1, 16):
        if rows % t == 0 and t * cols <= 256 * 1024:
            best = t
    return best


def _add_halves(g, landed, core):
    _, n_l, rh, cols = landed.shape
    tr = _row_tile(rh, cols)
    per_half = rh // tr

    def body(core_ref, g_ref, la_ref, o_ref):
        del core_ref
        o_ref[...] = (g_ref[...].astype(F32) + la_ref[...].astype(F32)).astype(BF16)

    blk = (None, None, tr, cols)
    return pl.pallas_call(
        body, out_shape=jax.ShapeDtypeStruct(landed.shape, BF16),
        grid_spec=pltpu.PrefetchScalarGridSpec(
            num_scalar_prefetch=1, grid=(N_CHIPS, n_l, per_half),
            in_specs=[pl.BlockSpec(blk, lambda j, l, r, core_ref: (j, l, core_ref[0] * per_half + r, 0)),
                      pl.BlockSpec(blk, lambda j, l, r, core_ref: (j, l, r, 0))],
            out_specs=pl.BlockSpec(blk, lambda j, l, r, core_ref: (j, l, r, 0))),
        name="reduce_add_halves", compiler_params=_params(("parallel", "parallel", "parallel")),
    )(core, g, landed)


def _add_chips(sums, landed, chip):
    _, n_l, rh, cols = sums.shape
    tr = _row_tile(rh, cols)

    def body(chip_ref, s_ref, la_ref, o_ref):
        del chip_ref
        o_ref[...] = ((s_ref[...].astype(F32) + la_ref[0].astype(F32)) + la_ref[1].astype(F32)) + la_ref[2].astype(F32)

    return pl.pallas_call(
        body, out_shape=jax.ShapeDtypeStruct((n_l, rh, cols), F32),
        grid_spec=pltpu.PrefetchScalarGridSpec(
            num_scalar_prefetch=1, grid=(n_l, rh // tr),
            in_specs=[pl.BlockSpec((None, None, tr, cols), lambda l, r, chip_ref: (chip_ref[0], l, r, 0)),
                      pl.BlockSpec((3, None, tr, cols), lambda l, r, chip_ref: (0, l, r, 0))],
            out_specs=pl.BlockSpec((None, tr, cols), lambda l, r, chip_ref: (l, r, 0))),
        name="reduce_add_chips", compiler_params=_params(("parallel", "parallel")),
    )(chip, sums, landed)


def sum_devices(parts):
    def body(p_ref, o_ref):
        acc = p_ref[0]
        for d in range(1, N_DEV):
            acc = acc + p_ref[d]
        o_ref[...] = acc

    return pl.pallas_call(body, out_shape=jax.ShapeDtypeStruct(parts.shape[1:], F32), name="sum_devices")(parts)


ADA_SHARD = 9 * D_MODEL // N_CHIPS
ADA_TILE = 768
ADA_ROWS = 16


def ada_forward(c_rows, w_ada, b_shard):
    def body(c_ref, w_ref, b_ref, o_ref):
        cv = c_ref[...]
        o_ref[...] = _dot((cv * _sigmoid(cv)).astype(BF16), w_ref[...].astype(BF16), NN) + b_ref[...]

    return pl.pallas_call(
        body, grid=(DEPTH, ADA_SHARD // ADA_TILE),
        in_specs=[pl.BlockSpec((ADA_ROWS, D_MODEL), lambda l, n: (0, 0)),
                  pl.BlockSpec((None, D_MODEL, ADA_TILE), lambda l, n: (l, 0, n)),
                  pl.BlockSpec((None, 1, ADA_TILE), lambda l, n: (l, 0, n))],
        out_specs=pl.BlockSpec((None, ADA_ROWS, ADA_TILE), lambda l, n: (l, 0, n)),
        out_shape=jax.ShapeDtypeStruct((DEPTH, ADA_ROWS, ADA_SHARD), F32),
        name="ada_forward", compiler_params=_params(("parallel", "parallel")),
    )(c_rows, w_ada, b_shard)


def ada_backward(c_rows, dmod_rows):
    def body(c_ref, d_ref, o_ref):
        cv = c_ref[...]
        o_ref[...] = _dot((cv * _sigmoid(cv)).astype(BF16), d_ref[...].astype(BF16), TN)

    return pl.pallas_call(
        body, grid=(DEPTH, ADA_SHARD // ADA_TILE),
        in_specs=[pl.BlockSpec((ADA_ROWS, D_MODEL), lambda l, n: (0, 0)),
                  pl.BlockSpec((None, ADA_ROWS, ADA_TILE), lambda l, n: (l, 0, n))],
        out_specs=pl.BlockSpec((None, D_MODEL, ADA_TILE), lambda l, n: (l, 0, n)),
        out_shape=jax.ShapeDtypeStruct((DEPTH, D_MODEL, ADA_SHARD), F32),
        name="ada_backward", compiler_params=_params(("parallel", "parallel")),
    )(c_rows, dmod_rows)


def adamw(name, w, g, m, v):
    shape = w.shape
    cols = shape[-1]
    rows = w.size // cols
    tr = _row_tile(rows, cols) if rows % 16 == 0 else rows
    c1 = 1.0 / (1.0 - ADAM_B1 ** ADAM_STEP)
    c2 = 1.0 / (1.0 - ADAM_B2 ** ADAM_STEP)

    def body(w_ref, g_ref, m_ref, v_ref, d_ref, mo_ref, vo_ref):
        gv = g_ref[...]
        mn = ADAM_B1 * m_ref[...] + (1.0 - ADAM_B1) * gv
        vn = ADAM_B2 * v_ref[...] + (1.0 - ADAM_B2) * (gv * gv)
        mo_ref[...] = mn
        vo_ref[...] = vn
        d_ref[...] = -ADAM_LR * ((mn * c1) / (jnp.sqrt(vn * c2) + ADAM_EPS) + ADAM_WD * w_ref[...])

    spec = pl.BlockSpec((tr, cols), lambda i: (i, 0))
    out = jax.ShapeDtypeStruct((rows, cols), F32)
    res = pl.pallas_call(
        body, grid=(rows // tr,), in_specs=[spec] * 4, out_specs=[spec] * 3, out_shape=[out] * 3,
        name=name, compiler_params=_params(("parallel",)),
    )(*[t.reshape(rows, cols) for t in (w, g, m, v)])
    return tuple(r.reshape(shape) for r in res)


def _pack(parts, rows):
    flat = jnp.concatenate([p.reshape(-1) for p in parts])
    return jnp.pad(flat, (0, rows * 128 - flat.size)).reshape(rows, 128)


def _unpack(flat, shapes):
    out, at = [], 0
    for s in shapes:
        n = math.prod(s)
        out.append(flat[at:at + n].reshape(s))
        at += n
    return out


def kernel(x, c, w_ada, b_ada, norm_gain, w_ffn_gate, w_ffn_up, w_ffn_down, w_in, w_br_sb, w_br_dil, w_br_swa, w_out, sinks, rel_bias, final_gain, loss_target, m_w_ada, m_b_ada, m_norm_gain, m_w_ffn_gate, m_w_ffn_up, m_w_ffn_down, m_w_in, m_w_br_sb, m_w_br_dil, m_w_br_swa, m_w_out, m_sinks, m_rel_bias, m_final_gain, v_w_ada, v_b_ada, v_norm_gain, v_w_ffn_gate, v_w_ffn_up, v_w_ffn_down, v_w_in, v_w_br_sb, v_w_br_dil, v_w_br_swa, v_w_out, v_sinks, v_rel_bias, v_final_gain):
    xi, yi, ci = _position()
    chip = 2 * xi + yi
    dev = 2 * chip + ci

    c_all = all_gather_small("gather_c", c.reshape(8, 128)).reshape(N_DEV, D_MODEL)
    c_rows = jnp.pad(c_all, ((0, ADA_ROWS - N_DEV), (0, 0)))
    b_shard = lax.dynamic_slice_in_dim(b_ada, chip * ADA_SHARD, ADA_SHARD, axis=1).reshape(DEPTH, 1, ADA_SHARD)
    mod_shard = ada_forward(c_rows, w_ada, b_shard)[:, :N_DEV]
    n_mod = DEPTH * N_DEV * ADA_SHARD
    gathered = all_gather_small("gather_mod", _pack([mod_shard, norm_gain], 304))[::2].reshape(N_CHIPS, -1)
    mod_all = gathered[:, :n_mod].reshape(N_CHIPS, DEPTH, N_DEV, ADA_SHARD)
    mod = lax.dynamic_index_in_dim(mod_all, dev, axis=2, keepdims=False)
    mod = mod.transpose(1, 0, 2).reshape(DEPTH, 3, 3, D_MODEL)
    gains = gathered[:, n_mod:n_mod + DEPTH * 3 * D_SHARD].reshape(N_CHIPS, DEPTH, 3, D_SHARD)
    gains = gains.transpose(1, 2, 0, 3).reshape(DEPTH, 3, D_MODEL)

    names = ("gate", "up", "down", "in", "br", "out")
    shards = [
        w_ffn_gate.reshape(2 * DEPTH, D_MODEL, FF_SHARD), w_ffn_up.reshape(2 * DEPTH, D_MODEL, FF_SHARD),
        w_ffn_down.reshape(2 * DEPTH, FF_SHARD, D_MODEL), w_in,
        jnp.concatenate([w_br_sb, w_br_dil, w_br_swa], axis=1), w_out]
    w = dict(zip(names, gather_weights([s.astype(BF16) for s in shards])))

    loss, dx, g, dmod, dgains, dfinal, dsinks, drel = device_step(
        x[0], loss_target[0], mod, gains, final_gain, sinks, rel_bias, w)

    small_shapes = [(DEPTH, 9 * D_MODEL), (DEPTH, 3, D_MODEL), (D_MODEL,), (DEPTH, H_SWA_Q), (N_BUCKETS, 12), (1,)]
    small_all = all_gather_small("gather_small_grads", _pack([dmod, dgains, dfinal, dsinks, drel, loss[0, 0:1]], 208))
    g_b_ada, g_gain_full, g_final, g_sinks, g_rel, loss_sum = _unpack(sum_devices(small_all).reshape(-1), small_shapes)
    g_gain = lax.dynamic_slice_in_dim(g_gain_full, chip * D_SHARD, D_SHARD, axis=2)
    dmod_all = small_all.reshape(N_DEV, -1)[:, :DEPTH * 9 * D_MODEL].reshape(N_DEV, DEPTH, 9 * D_MODEL)
    dmod_rows = lax.dynamic_slice_in_dim(dmod_all, chip * ADA_SHARD, ADA_SHARD, axis=2).transpose(1, 0, 2)
    g_w_ada = ada_backward(c_rows, jnp.pad(dmod_rows, ((0, 0), (0, ADA_ROWS - N_DEV), (0, 0))))

    g_gate, g_up, g_down, g_in, g_br, g_out = reduce_gradients([g[k] for k in names], xi, yi, ci)
    grads = [
        g_w_ada, g_b_ada, g_gain, g_gate.reshape(w_ffn_gate.shape), g_up.reshape(w_ffn_up.shape),
        g_down.reshape(w_ffn_down.shape), g_in, g_br[:, 0:256], g_br[:, 256:384], g_br[:, 384:768], g_out,
        g_sinks, g_rel, g_final]
    weights = [w_ada, b_ada, norm_gain, w_ffn_gate, w_ffn_up, w_ffn_down, w_in, w_br_sb, w_br_dil, w_br_swa, w_out,
               sinks, rel_bias, final_gain]
    ms = [m_w_ada, m_b_ada, m_norm_gain, m_w_ffn_gate, m_w_ffn_up, m_w_ffn_down, m_w_in, m_w_br_sb, m_w_br_dil,
          m_w_br_swa, m_w_out, m_sinks, m_rel_bias, m_final_gain]
    vs = [v_w_ada, v_b_ada, v_norm_gain, v_w_ffn_gate, v_w_ffn_up, v_w_ffn_down, v_w_in, v_w_br_sb, v_w_br_dil,
          v_w_br_swa, v_w_out, v_sinks, v_rel_bias, v_final_gain]

    small = (1, 2, 11, 12, 13)
    deltas, new_ms, new_vs = [None] * 14, [None] * 14, [None] * 14
    for k in range(14):
        if k not in small:
            deltas[k], new_ms[k], new_vs[k] = adamw(f"adamw_{k}", weights[k], grads[k], ms[k], vs[k])
    shapes = [weights[k].shape for k in small]
    packed = [_pack([t[k] for k in small], 168) for t in (weights, grads, ms, vs)]
    for dst, res in zip((deltas, new_ms, new_vs), adamw("adamw_small", *packed)):
        for k, t in zip(small, _unpack(res.reshape(-1), shapes)):
            dst[k] = t
    return (loss_sum[0], dx[None], *grads, *deltas, *new_ms, *new_vs)
```

```python
import functools
import math

import jax
import jax.numpy as jnp
from jax import lax
from jax.experimental import pallas as pl
from jax.experimental.pallas import tpu as pltpu

F32 = jnp.float32
BF16 = jnp.bfloat16

D_MODEL = 1024
SEQ = 2048
DEPTH = 2
HEAD_DIM = 64
BLK = 128
H_SB = 4
DIL_PATTERNS = ((128, 1), (512, 4), (2048, 16))
H_PER_DIL = 2
H_DIL = 6
H_SWA_Q = 6
H_SWA_KV = 2
SWA_WINDOW = 128
N_BUCKETS = 32
MAX_REL_DIST = 2048
D_FF = 2816
RMS_EPS = 1e-6
N_CHIPS = 4
N_DEV = 8
FF_SHARD = D_FF // N_CHIPS
D_QKV = 2560
D_IN = D_QKV + 3 * D_MODEL
IN_SHARD = D_IN // N_CHIPS
D_SHARD = D_MODEL // N_CHIPS
BR_ROWS = 768
NEG = -1e30
QK_SCALE = HEAD_DIM ** -0.5

ADAM_LR = 0.001
ADAM_B1 = 0.9
ADAM_B2 = 0.999
ADAM_EPS = 1e-08
ADAM_WD = 0.01
ADAM_STEP = 10

VMEM_LIMIT = 48 * 1024 * 1024
ROW_TILE = 256
MM_TILE = 512

NN = (((1,), (0,)), ((), ()))
NT = (((1,), (1,)), ((), ()))
TN = (((0,), (0,)), ((), ()))


def _params(sem=None):
    return pltpu.CompilerParams(dimension_semantics=sem, vmem_limit_bytes=VMEM_LIMIT)


def _dot(a, b, dims):
    return lax.dot_general(a, b, dims, preferred_element_type=F32)


def _sigmoid(x):
    return 1.0 / (1.0 + jnp.exp(-x))


def _matmul(name, grid, nk, k_axis, dims, n_pairs, in_specs, out_specs, out_shape, acc_shape, epilogue,
            operands, sem, aliases=None, prologue=None):
    n_in = len(in_specs)
    n_out = len(out_specs)

    def partial(ins):
        tot = None
        for p in range(n_pairs):
            a = ins[2 * p][...]
            if prologue is not None:
                a = prologue(p, a, ins)
            d = _dot(a, ins[2 * p + 1][...], dims)
            tot = d if tot is None else tot + d
        return tot

    def body(*refs):
        ins, outs = refs[:n_in], refs[n_in:n_in + n_out]
        ids = tuple(pl.program_id(a) for a in range(len(grid)))
        if nk == 1:
            epilogue(partial(ins), ins, outs, ids)
            return
        acc = refs[n_in + n_out]
        k = ids[k_axis]

        @pl.when(k == 0)
        def _():
            acc[...] = partial(ins)

        @pl.when(k > 0)
        def _():
            acc[...] += partial(ins)

        @pl.when(k == nk - 1)
        def _():
            epilogue(acc[...], ins, outs, ids)

    return pl.pallas_call(
        body, grid=grid, in_specs=in_specs, out_specs=out_specs, out_shape=out_shape,
        scratch_shapes=[] if nk == 1 else [pltpu.VMEM(acc_shape, F32)],
        input_output_aliases=aliases or {}, name=name, compiler_params=_params(sem),
    )(*operands)


def _row_spec(width=D_MODEL):
    return pl.BlockSpec((ROW_TILE, width), lambda i: (i, 0))


def _vec_spec(rows=1, width=D_MODEL):
    return pl.BlockSpec((rows, width), lambda i: (0, 0))


def prenorm(x, gain, scale, shift):
    def body(x_ref, g_ref, sc_ref, sh_ref, h_ref):
        xv = x_ref[...]
        r = lax.rsqrt(jnp.mean(xv * xv, axis=-1, keepdims=True) + RMS_EPS)
        h_ref[...] = (((xv * r) * g_ref[...]) * (1.0 + sc_ref[...]) + sh_ref[...]).astype(BF16)

    return pl.pallas_call(
        body, grid=(SEQ // ROW_TILE,), in_specs=[_row_spec(), _vec_spec(), _vec_spec(), _vec_spec()],
        out_specs=_row_spec(), out_shape=jax.ShapeDtypeStruct((SEQ, D_MODEL), BF16),
        name="prenorm", compiler_params=_params(("parallel",)),
    )(x, gain, scale, shift)


def resid_bwd(dxo, f, coef, mult):
    def body(dx_ref, f_ref, c_ref, df_ref, dc_ref):
        dx = dx_ref[...]
        df_ref[...] = (dx * (mult * c_ref[...])).astype(BF16)
        part = mult * jnp.sum(dx * f_ref[...], axis=0, keepdims=True)

        @pl.when(pl.program_id(0) == 0)
        def _():
            dc_ref[...] = jnp.zeros_like(dc_ref)

        dc_ref[0:1, :] += part

    return pl.pallas_call(
        body, grid=(SEQ // ROW_TILE,), in_specs=[_row_spec(), _row_spec(), _vec_spec()],
        out_specs=[_row_spec(), _vec_spec(8)],
        out_shape=[jax.ShapeDtypeStruct((SEQ, D_MODEL), BF16), jax.ShapeDtypeStruct((8, D_MODEL), F32)],
        name="resid_bwd", compiler_params=_params(("arbitrary",)),
    )(dxo, f, coef)


def final_loss(x, gain, target):
    def body(x_ref, g_ref, t_ref, loss_ref, dx_ref, dg_ref):
        xv = x_ref[...]
        g = g_ref[...]
        r = lax.rsqrt(jnp.mean(xv * xv, axis=-1, keepdims=True) + RMS_EPS)
        xh = xv * r
        e = xh * g - t_ref[...]
        part = 0.5 * jnp.sum(jnp.mean(e * e, axis=-1, keepdims=True), axis=0, keepdims=True)
        dy = e * (1.0 / D_MODEL)
        dyg = dy * g
        dx_ref[...] = r * (dyg - xh * jnp.mean(dyg * xh, axis=-1, keepdims=True))

        @pl.when(pl.program_id(0) == 0)
        def _():
            loss_ref[...] = jnp.zeros_like(loss_ref)
            dg_ref[...] = jnp.zeros_like(dg_ref)

        loss_ref[...] += jnp.broadcast_to(part, loss_ref.shape)
        dg_ref[0:1, :] += jnp.sum(dy * xh, axis=0, keepdims=True)

    return pl.pallas_call(
        body, grid=(SEQ // ROW_TILE,), in_specs=[_row_spec(), _vec_spec(), _row_spec()],
        out_specs=[_vec_spec(8, 128), _row_spec(), _vec_spec(8)],
        out_shape=[jax.ShapeDtypeStruct((8, 128), F32), jax.ShapeDtypeStruct((SEQ, D_MODEL), F32),
                   jax.ShapeDtypeStruct((8, D_MODEL), F32)],
        name="final_loss", compiler_params=_params(("arbitrary",)),
    )(x, gain, target)


def _prenorm_bwd_epilogue(dh, x_ref, dxo_ref, g_ref, sc_ref, dx_ref, stats_ref, first):
    xv = x_ref[...]
    g = g_ref[...]
    r = lax.rsqrt(jnp.mean(xv * xv, axis=-1, keepdims=True) + RMS_EPS)
    xh = xv * r
    dn = dh * (1.0 + sc_ref[...])
    dxh = dn * g
    dx_ref[...] = dxo_ref[...] + r * (dxh - xh * jnp.mean(dxh * xh, axis=-1, keepdims=True))

    @pl.when(first)
    def _():
        stats_ref[...] = jnp.zeros_like(stats_ref)

    stats_ref[0:1, :] += jnp.sum(dh, axis=0, keepdims=True)
    stats_ref[1:2, :] += jnp.sum(dh * (xh * g), axis=0, keepdims=True)
    stats_ref[2:3, :] += jnp.sum(dn * xh, axis=0, keepdims=True)


def ffn_up(h, wg_all, wu_all, lf):
    def body(h_ref, wg_ref, wu_ref, a_ref, b_ref, s_ref):
        hv = h_ref[...]
        a = _dot(hv, wg_ref[...], NN)
        b = _dot(hv, wu_ref[...], NN)
        a_ref[...] = a
        b_ref[...] = b
        s_ref[...] = (a * _sigmoid(a) * b).astype(BF16)

    w_spec = pl.BlockSpec((None, None, D_MODEL, FF_SHARD), lambda j, i: (j, lf, 0, 0))
    o_spec = pl.BlockSpec((None, MM_TILE, FF_SHARD), lambda j, i: (j, i, 0))
    hid = (N_CHIPS, SEQ, FF_SHARD)
    return pl.pallas_call(
        body, grid=(N_CHIPS, SEQ // MM_TILE),
        in_specs=[pl.BlockSpec((MM_TILE, D_MODEL), lambda j, i: (i, 0)), w_spec, w_spec],
        out_specs=[o_spec, o_spec, o_spec],
        out_shape=[jax.ShapeDtypeStruct(hid, F32), jax.ShapeDtypeStruct(hid, F32), jax.ShapeDtypeStruct(hid, BF16)],
        name="ffn_up", compiler_params=_params(("parallel", "parallel")),
    )(h, wg_all, wu_all)


def matmul_residual(name, a, a_spec, w_all, w_spec, x, coef, mult):
    def epilogue(acc, ins, outs, ids):
        outs[0][...] = acc
        outs[1][...] = ins[2][...] + (mult * ins[3][...]) * acc

    row = pl.BlockSpec((MM_TILE, D_MODEL), lambda i, j: (i, 0))
    return _matmul(
        name, (SEQ // MM_TILE, N_CHIPS), N_CHIPS, 1, NN, 1,
        [a_spec, w_spec, row, pl.BlockSpec((1, D_MODEL), lambda i, j: (0, 0))], [row, row],
        [jax.ShapeDtypeStruct((SEQ, D_MODEL), F32)] * 2, (MM_TILE, D_MODEL), epilogue,
        (a, w_all, x, coef), ("parallel", "arbitrary"))


def ffn_down(s, wd_all, lf, x, gate):
    return matmul_residual(
        "ffn_down", s, pl.BlockSpec((None, MM_TILE, FF_SHARD), lambda i, j: (j, i, 0)),
        wd_all, pl.BlockSpec((None, None, FF_SHARD, D_MODEL), lambda i, j: (j, lf, 0, 0)), x, gate, 0.5)


def ffn_bwd_hidden(df, wd_all, lf, a, b):
    def epilogue(ds, ins, outs, ids):
        av, bv = ins[2][...], ins[3][...]
        sig = _sigmoid(av)
        outs[0][...] = (ds * bv * (sig * (1.0 + av * (1.0 - sig)))).astype(BF16)
        outs[1][...] = (ds * (av * sig)).astype(BF16)

    hid_spec = pl.BlockSpec((None, MM_TILE, FF_SHARD), lambda j, i: (j, i, 0))
    hid = jax.ShapeDtypeStruct((N_CHIPS, SEQ, FF_SHARD), BF16)
    return _matmul(
        "ffn_bwd_hidden", (N_CHIPS, SEQ // MM_TILE), 1, None, NT, 1,
        [pl.BlockSpec((MM_TILE, D_MODEL), lambda j, i: (i, 0)),
         pl.BlockSpec((None, None, FF_SHARD, D_MODEL), lambda j, i: (j, lf, 0, 0)), hid_spec, hid_spec],
        [hid_spec, hid_spec], [hid, hid], None, epilogue, (df, wd_all, a, b), ("parallel", "parallel"))


def grad_weight(name, lhs, lhs_spec, rhs, rhs_spec, g_all, out_block, out_index, acc_shape):
    def epilogue(acc, ins, outs, ids):
        outs[0][...] = acc.astype(BF16)

    return _matmul(
        name, (N_CHIPS, SEQ // MM_TILE), SEQ // MM_TILE, 1, TN, 1,
        [lhs_spec, rhs_spec, pl.BlockSpec(memory_space=pl.ANY)], [pl.BlockSpec(out_block, out_index)],
        [jax.ShapeDtypeStruct(g_all.shape, BF16)], acc_shape, epilogue, (lhs, rhs, g_all),
        ("parallel", "arbitrary"), aliases={2: 0})[0]


def ffn_grad_weights(h, s, df, da, db, g_gate, g_up, g_down, lf):
    tok = pl.BlockSpec((MM_TILE, D_MODEL), lambda j, k: (k, 0))
    hid = pl.BlockSpec((None, MM_TILE, FF_SHARD), lambda j, k: (j, k, 0))
    up_block, up_index = (None, None, D_MODEL, FF_SHARD), (lambda j, k: (j, lf, 0, 0))
    g_gate = grad_weight("grad_w_gate", h, tok, da, hid, g_gate, up_block, up_index, (D_MODEL, FF_SHARD))
    g_up = grad_weight("grad_w_up", h, tok, db, hid, g_up, up_block, up_index, (D_MODEL, FF_SHARD))
    g_down = grad_weight("grad_w_down", s, hid, df, tok, g_down, (None, None, FF_SHARD, D_MODEL),
                         lambda j, k: (j, lf, 0, 0), (FF_SHARD, D_MODEL))
    return g_gate, g_up, g_down


def matmul_prenorm_bwd(name, pairs, pair_specs, x, dxo, gain, scale):
    n = len(pairs)

    def epilogue(dh, ins, outs, ids):
        _prenorm_bwd_epilogue(dh, ins[n], ins[n + 1], ins[n + 2], ins[n + 3], outs[0], outs[1], ids[0] == 0)

    row = pl.BlockSpec((MM_TILE, D_MODEL), lambda i, j: (i, 0))
    vec = pl.BlockSpec((1, D_MODEL), lambda i, j: (0, 0))
    return _matmul(
        name, (SEQ // MM_TILE, N_CHIPS), N_CHIPS, 1, NT, len(pairs) // 2,
        list(pair_specs) + [row, row, vec, vec], [row, pl.BlockSpec((8, D_MODEL), lambda i, j: (0, 0))],
        [jax.ShapeDtypeStruct((SEQ, D_MODEL), F32), jax.ShapeDtypeStruct((8, D_MODEL), F32)],
        (MM_TILE, D_MODEL), epilogue, tuple(pairs) + (x, dxo, gain, scale), ("arbitrary", "arbitrary"))


def ffn_bwd_input(da, db, wg_all, wu_all, lf, x, dxo, gain, scale):
    hid = pl.BlockSpec((None, MM_TILE, FF_SHARD), lambda i, j: (j, i, 0))
    w = pl.BlockSpec((None, None, D_MODEL, FF_SHARD), lambda i, j: (j, lf, 0, 0))
    return matmul_prenorm_bwd("ffn_bwd_input", (da, wg_all, db, wu_all), (hid, w, hid, w), x, dxo, gain, scale)


def in_proj(h, w_all, l):
    def epilogue(acc, ins, outs, ids):
        outs[0][...] = acc

    return _matmul(
        "in_proj", (N_CHIPS, SEQ // MM_TILE), 1, None, NN, 1,
        [pl.BlockSpec((MM_TILE, D_MODEL), lambda j, i: (i, 0)),
         pl.BlockSpec((None, None, D_MODEL, IN_SHARD), lambda j, i: (j, l, 0, 0))],
        [pl.BlockSpec((MM_TILE, IN_SHARD), lambda j, i: (i, j))], [jax.ShapeDtypeStruct((SEQ, D_IN), F32)],
        None, epilogue, (h, w_all), ("parallel", "parallel"))[0]


_GATE_BLOCK0 = D_QKV // D_SHARD


def _branch_products(o, w_ref):
    ob = o.astype(BF16)
    return (_dot(ob[:, 0:256], w_ref[0:256, :], NN), _dot(ob[:, 256:384], w_ref[256:384, :], NN),
            _dot(ob[:, 384:768], w_ref[384:768, :], NN))


def merge_branches(o_cat, wbr_all, l, proj):
    def body(o_ref, w_ref, g0_ref, g1_ref, g2_ref, m_ref):
        u = _branch_products(o_ref[...], w_ref)
        m_ref[...] = (_sigmoid(g0_ref[...]) * u[0] + _sigmoid(g1_ref[...]) * u[1]
                      + _sigmoid(g2_ref[...]) * u[2]).astype(BF16)

    def gate_spec(b):
        return pl.BlockSpec((MM_TILE, D_SHARD), lambda i, j: (i, _GATE_BLOCK0 + 4 * b + j))

    return pl.pallas_call(
        body, grid=(SEQ // MM_TILE, N_CHIPS),
        in_specs=[pl.BlockSpec((MM_TILE, BR_ROWS), lambda i, j: (i, 0)),
                  pl.BlockSpec((None, None, BR_ROWS, D_SHARD), lambda i, j: (j, l, 0, 0)),
                  gate_spec(0), gate_spec(1), gate_spec(2)],
        out_specs=pl.BlockSpec((MM_TILE, D_SHARD), lambda i, j: (i, j)),
        out_shape=jax.ShapeDtypeStruct((SEQ, D_MODEL), BF16),
        name="merge_branches", compiler_params=_params(("parallel", "parallel")),
    )(o_cat, wbr_all, proj, proj, proj)


def out_proj(merged, wout_all, l, x, gate):
    return matmul_residual(
        "out_proj", merged, pl.BlockSpec((MM_TILE, D_SHARD), lambda i, j: (i, j)),
        wout_all, pl.BlockSpec((None, None, D_SHARD, D_MODEL), lambda i, j: (j, l, 0, 0)), x, gate, 1.0)


def merge_bwd(dmo, wout_all, o_cat, wbr_all, l, proj):
    def epilogue(dm, ins, outs, ids):
        u = _branch_products(ins[2][...], ins[3])
        for b in range(3):
            sig = _sigmoid(ins[4 + b][...])
            outs[b][...] = (dm * sig).astype(BF16)
            outs[3 + b][...] = (dm * u[b] * (sig * (1.0 - sig))).astype(BF16)

    def gate_spec(b):
        return pl.BlockSpec((MM_TILE, D_SHARD), lambda j, i: (i, _GATE_BLOCK0 + 4 * b + j))

    col = pl.BlockSpec((MM_TILE, D_SHARD), lambda j, i: (i, j))
    du = jax.ShapeDtypeStruct((SEQ, D_MODEL), BF16)
    return _matmul(
        "merge_bwd", (N_CHIPS, SEQ // MM_TILE), 1, None, NT, 1,
        [pl.BlockSpec((MM_TILE, D_MODEL), lambda j, i: (i, 0)),
         pl.BlockSpec((None, None, D_SHARD, D_MODEL), lambda j, i: (j, l, 0, 0)),
         pl.BlockSpec((MM_TILE, BR_ROWS), lambda j, i: (i, 0)),
         pl.BlockSpec((None, None, BR_ROWS, D_SHARD), lambda j, i: (j, l, 0, 0)),
         gate_spec(0), gate_spec(1), gate_spec(2)],
        [col] * 6, [du] * 6,
        None, epilogue, (dmo, wout_all, o_cat, wbr_all, proj, proj, proj), ("parallel", "parallel"))


def branch_bwd_input(du, wbr_all, l):
    def body(d0_ref, d1_ref, d2_ref, w_ref, o_ref, acc):
        j = pl.program_id(1)
        parts = (_dot(d0_ref[...], w_ref[0:256, :], NT), _dot(d1_ref[...], w_ref[256:384, :], NT),
                 _dot(d2_ref[...], w_ref[384:768, :], NT))

        @pl.when(j == 0)
        def _():
            acc[:, 0:256], acc[:, 256:384], acc[:, 384:768] = parts

        @pl.when(j > 0)
        def _():
            acc[:, 0:256] += parts[0]
            acc[:, 256:384] += parts[1]
            acc[:, 384:768] += parts[2]

        @pl.when(j == N_CHIPS - 1)
        def _():
            o_ref[...] = acc[...]

    col = pl.BlockSpec((MM_TILE, D_SHARD), lambda i, j: (i, j))
    return pl.pallas_call(
        body, grid=(SEQ // MM_TILE, N_CHIPS),
        in_specs=[col, col, col, pl.BlockSpec((None, None, BR_ROWS, D_SHARD), lambda i, j: (j, l, 0, 0))],
        out_specs=pl.BlockSpec((MM_TILE, BR_ROWS), lambda i, j: (i, 0)),
        out_shape=jax.ShapeDtypeStruct((SEQ, BR_ROWS), F32),
        scratch_shapes=[pltpu.VMEM((MM_TILE, BR_ROWS), F32)],
        name="branch_bwd_input", compiler_params=_params(("parallel", "arbitrary")),
    )(du[0], du[1], du[2], wbr_all)


def branch_grad_weights(o_cat, du, g_br, l):
    def body(o_ref, d0_ref, d1_ref, d2_ref, g_in, g_ref, acc):
        del g_in
        k = pl.program_id(1)
        ob = o_ref[...].astype(BF16)
        parts = (_dot(ob[:, 0:256], d0_ref[...], TN), _dot(ob[:, 256:384], d1_ref[...], TN),
                 _dot(ob[:, 384:768], d2_ref[...], TN))

        @pl.when(k == 0)
        def _():
            acc[0:256, :], acc[256:384, :], acc[384:768, :] = parts

        @pl.when(k > 0)
        def _():
            acc[0:256, :] += parts[0]
            acc[256:384, :] += parts[1]
            acc[384:768, :] += parts[2]

        @pl.when(k == SEQ // MM_TILE - 1)
        def _():
            g_ref[...] = acc[...].astype(BF16)

    col = pl.BlockSpec((MM_TILE, D_SHARD), lambda j, k: (k, j))
    return pl.pallas_call(
        body, grid=(N_CHIPS, SEQ // MM_TILE),
        in_specs=[pl.BlockSpec((MM_TILE, BR_ROWS), lambda j, k: (k, 0)), col, col, col,
                  pl.BlockSpec(memory_space=pl.ANY)],
        out_specs=pl.BlockSpec((None, None, BR_ROWS, D_SHARD), lambda j, k: (j, l, 0, 0)),
        out_shape=jax.ShapeDtypeStruct(g_br.shape, BF16),
        scratch_shapes=[pltpu.VMEM((BR_ROWS, D_SHARD), F32)], input_output_aliases={4: 0},
        name="branch_grad_weights", compiler_params=_params(("parallel", "arbitrary")),
    )(o_cat, du[0], du[1], du[2], g_br)


def mixer_bwd_input(dproj, win_all, l, x, dxo, gain, scale):
    return matmul_prenorm_bwd(
        "mixer_bwd_input", (dproj, win_all),
        (pl.BlockSpec((MM_TILE, IN_SHARD), lambda i, j: (i, j)),
         pl.BlockSpec((None, None, D_MODEL, IN_SHARD), lambda i, j: (j, l, 0, 0))), x, dxo, gain, scale)


def _split_dot(v, tri):
    hi = v.astype(BF16)
    lo = (v - hi.astype(F32)).astype(BF16)
    return _dot(hi, tri, NN) + _dot(lo, tri, NN)


def _tri(cmp):
    return cmp(lax.broadcasted_iota(jnp.int32, (BLK, BLK), 0), lax.broadcasted_iota(jnp.int32, (BLK, BLK), 1)).astype(BF16)


def _sb_scores(qs, k_ref, i, j):
    rows = pl.ds(pl.multiple_of(j * BLK, BLK), BLK)
    z = _dot(qs, k_ref[rows, :], NT)
    q_pos = i * BLK + lax.broadcasted_iota(jnp.int32, (BLK, BLK), 0)
    k_pos = j * BLK + lax.broadcasted_iota(jnp.int32, (BLK, BLK), 1)
    before = k_pos < q_pos
    soft = jnp.log(1.0 + jnp.exp(-jnp.abs(z)))
    log_fail = jnp.where(before, -(jnp.maximum(z, 0.0) + soft), 0.0)
    log_hit = jnp.minimum(z, 0.0) - soft
    return rows, before, log_fail, log_hit


def sb_forward(q, k, v):
    def body(q_ref, k_ref, v_ref, o_ref, tot_ref):
        i = pl.program_id(1)
        qs = q_ref[...]
        later = _tri(lambda r, c: r > c)

        def step(t, carry):
            o, run = carry
            rows, before, log_fail, log_hit = _sb_scores(qs, k_ref, i, i - t)
            between = _split_dot(log_fail, later) + run
            w = jnp.where(before, jnp.exp(log_hit + between), 0.0)
            o = o + _dot(w.astype(BF16), v_ref[rows, :], NN)
            return o, run + jnp.sum(log_fail, axis=1, keepdims=True)

        o, run = lax.fori_loop(0, i + 1, step, (jnp.zeros((BLK, HEAD_DIM), F32), jnp.zeros((BLK, 1), F32)))
        o_ref[...] = o
        tot_ref[...] = run

    blk = pl.BlockSpec((None, BLK, HEAD_DIM), lambda h, i: (h, i, 0))
    col = pl.BlockSpec((None, BLK, 1), lambda h, i: (h, i, 0))
    full = pl.BlockSpec((None, SEQ, HEAD_DIM), lambda h, i: (h, 0, 0))
    return pl.pallas_call(
        body, grid=(H_SB, SEQ // BLK), in_specs=[blk, full, full], out_specs=[blk, col],
        out_shape=[jax.ShapeDtypeStruct((H_SB, SEQ, HEAD_DIM), F32), jax.ShapeDtypeStruct((H_SB, SEQ, 1), F32)],
        name="sb_forward", compiler_params=_params(("parallel", "parallel")),
    )(q, k, v)


def sb_backward(q, k, v, total, do):
    def body(q_ref, k_ref, v_ref, tot_ref, do_ref, dq_ref, dk_ref, dv_ref):
        i = pl.program_id(1)

        @pl.when(i == 0)
        def _():
            dk_ref[...] = jnp.zeros_like(dk_ref)
            dv_ref[...] = jnp.zeros_like(dv_ref)

        qs = q_ref[...]
        dob = do_ref[...].astype(BF16)
        total_v = tot_ref[...]
        upto = _tri(lambda r, c: r <= c)
        earlier = _tri(lambda r, c: r < c)

        def step(j, carry):
            dq, seen, g_seen = carry
            rows, before, log_fail, log_hit = _sb_scores(qs, k_ref, i, j)
            between = total_v - (seen + _split_dot(log_fail, upto))
            w = jnp.where(before, jnp.exp(log_hit + between), 0.0)
            g = _dot(dob, v_ref[rows, :], NT) * w
            g_earlier = g_seen + _split_dot(g, earlier)
            sig = jnp.exp(log_hit)
            dz = jnp.where(before, g * (1.0 - sig) - g_earlier * sig, 0.0).astype(BF16)
            dq = dq + _dot(dz, k_ref[rows, :], NN)
            dk_ref[rows, :] += _dot(dz, qs, TN)
            dv_ref[rows, :] += _dot(w.astype(BF16), dob, TN)
            return dq, seen + jnp.sum(log_fail, axis=1, keepdims=True), g_seen + jnp.sum(g, axis=1, keepdims=True)

        zero = jnp.zeros((BLK, 1), F32)
        dq, _, _ = lax.fori_loop(0, i + 1, step, (jnp.zeros((BLK, HEAD_DIM), F32), zero, zero))
        dq_ref[...] = dq * QK_SCALE

    blk = pl.BlockSpec((None, BLK, HEAD_DIM), lambda h, i: (h, i, 0))
    col = pl.BlockSpec((None, BLK, 1), lambda h, i: (h, i, 0))
    full = pl.BlockSpec((None, SEQ, HEAD_DIM), lambda h, i: (h, 0, 0))
    shape = jax.ShapeDtypeStruct((H_SB, SEQ, HEAD_DIM), F32)
    return pl.pallas_call(
        body, grid=(H_SB, SEQ // BLK), in_specs=[blk, full, full, col, blk], out_specs=[blk, full, full],
        out_shape=[shape, shape, shape],
        name="sb_backward", compiler_params=_params(("parallel", "arbitrary")),
    )(q, k, v, total, do)


def _band_scores(q_ref, kp_ref, ko_ref, bias_ref, has_prev):
    qs = q_ref[...]
    s_prev = _dot(qs, kp_ref[...], NT) + bias_ref[:, 0:BLK]
    s_prev = jnp.where(has_prev, s_prev, NEG)
    s_own = _dot(qs, ko_ref[...], NT) + bias_ref[:, BLK:2 * BLK]
    return qs, s_prev, s_own


def banded_forward(name, q, k, v, bias, sinks, group, prev_mask):
    hq_n, t_n = q.shape[0], q.shape[1]

    def body(sink_ref, q_ref, kp_ref, ko_ref, vp_ref, vo_ref, bias_ref, o_ref, lse_ref):
        hq, b = pl.program_id(0), pl.program_id(1)
        has_prev = (b & prev_mask(hq)) != 0
        _, s_prev, s_own = _band_scores(q_ref, kp_ref, ko_ref, bias_ref, has_prev)
        sink = sink_ref[hq]
        m = jnp.maximum(jnp.maximum(jnp.max(s_prev, axis=1, keepdims=True), jnp.max(s_own, axis=1, keepdims=True)), sink)
        p_prev = jnp.exp(s_prev - m)
        p_own = jnp.exp(s_own - m)
        denom = jnp.sum(p_prev, axis=1, keepdims=True) + jnp.sum(p_own, axis=1, keepdims=True) + jnp.exp(sink - m)
        o = _dot(p_prev.astype(BF16), vp_ref[...], NN) + _dot(p_own.astype(BF16), vo_ref[...], NN)
        o_ref[...] = o / denom
        lse_ref[...] = m + jnp.log(denom)

    qb = pl.BlockSpec((None, BLK, HEAD_DIM), lambda h, b: (h, b, 0))
    own = pl.BlockSpec((None, BLK, HEAD_DIM), lambda h, b: (h // group, b, 0))
    prev = pl.BlockSpec((None, BLK, HEAD_DIM), lambda h, b: (h // group, jnp.maximum(b - 1, 0), 0))
    return pl.pallas_call(
        body, grid=(hq_n, t_n // BLK),
        in_specs=[pl.BlockSpec(memory_space=pltpu.SMEM), qb, prev, own, prev, own,
                  pl.BlockSpec((None, BLK, 2 * BLK), lambda h, b: (h, 0, 0))],
        out_specs=[qb, pl.BlockSpec((None, BLK, 1), lambda h, b: (h, b, 0))],
        out_shape=[jax.ShapeDtypeStruct((hq_n, t_n, HEAD_DIM), F32), jax.ShapeDtypeStruct((hq_n, t_n, 1), F32)],
        name=name, compiler_params=_params(("parallel", "parallel")),
    )(sinks, q, k, k, v, v, bias)


def banded_backward(name, q, k, v, bias, sinks, o, lse, do, dlse, group, prev_mask):
    hq_n, t_n, hkv_n = q.shape[0], q.shape[1], k.shape[0]

    def body(sink_ref, q_ref, kp_ref, ko_ref, vp_ref, vo_ref, bias_ref, o_ref, lse_ref, do_ref, dlse_ref,
             dq_ref, dk_ref, dv_ref, dbias_ref, dsink_ref):
        hkv, g, b = pl.program_id(0), pl.program_id(1), pl.program_id(2)
        hq = hkv * group + g
        has_prev = (b & prev_mask(hq)) != 0

        @pl.when(jnp.logical_and(g == 0, b == 0))
        def _():
            dk_ref[...] = jnp.zeros_like(dk_ref)
            dv_ref[...] = jnp.zeros_like(dv_ref)

        @pl.when(b == 0)
        def _():
            dbias_ref[...] = jnp.zeros_like(dbias_ref)
            dsink_ref[...] = jnp.zeros_like(dsink_ref)

        qs, s_prev, s_own = _band_scores(q_ref, kp_ref, ko_ref, bias_ref, has_prev)
        lse_v = lse_ref[...]
        dov = do_ref[...]
        dob = dov.astype(BF16)
        shift = dlse_ref[...] - jnp.sum(dov * o_ref[...], axis=1, keepdims=True)
        p_prev = jnp.exp(s_prev - lse_v)
        p_own = jnp.exp(s_own - lse_v)
        ds_prev = p_prev * (_dot(dob, vp_ref[...], NT) + shift)
        ds_own = p_own * (_dot(dob, vo_ref[...], NT) + shift)
        dbias_ref[:, 0:BLK] += ds_prev
        dbias_ref[:, BLK:2 * BLK] += ds_own
        dsink_ref[...] += jnp.sum(jnp.exp(sink_ref[hq] - lse_v) * shift)
        ds_prev = ds_prev.astype(BF16)
        ds_own = ds_own.astype(BF16)
        dq_ref[...] = (_dot(ds_prev, kp_ref[...], NN) + _dot(ds_own, ko_ref[...], NN)) * QK_SCALE
        rows_prev = pl.ds(pl.multiple_of(jnp.maximum(b - 1, 0) * BLK, BLK), BLK)
        rows_own = pl.ds(pl.multiple_of(b * BLK, BLK), BLK)
        dk_ref[rows_prev, :] += _dot(ds_prev, qs, TN)
        dk_ref[rows_own, :] += _dot(ds_own, qs, TN)
        dv_ref[rows_prev, :] += _dot(p_prev.astype(BF16), dob, TN)
        dv_ref[rows_own, :] += _dot(p_own.astype(BF16), dob, TN)

    def qspec(width):
        return pl.BlockSpec((None, BLK, width), lambda hk, g, b: (hk * group + g, b, 0))

    own = pl.BlockSpec((None, BLK, HEAD_DIM), lambda hk, g, b: (hk, b, 0))
    prev = pl.BlockSpec((None, BLK, HEAD_DIM), lambda hk, g, b: (hk, jnp.maximum(b - 1, 0), 0))
    kv_full = pl.BlockSpec((None, t_n, HEAD_DIM), lambda hk, g, b: (hk, 0, 0))
    per_head = lambda rows, width: pl.BlockSpec((None, rows, width), lambda hk, g, b: (hk * group + g, 0, 0))
    kv_shape = jax.ShapeDtypeStruct((hkv_n, t_n, HEAD_DIM), F32)
    return pl.pallas_call(
        body, grid=(hkv_n, group, t_n // BLK),
        in_specs=[pl.BlockSpec(memory_space=pltpu.SMEM), qspec(HEAD_DIM), prev, own, prev, own,
                  per_head(BLK, 2 * BLK), qspec(HEAD_DIM), qspec(1), qspec(HEAD_DIM), qspec(1)],
        out_specs=[qspec(HEAD_DIM), kv_full, kv_full, per_head(BLK, 2 * BLK), per_head(1, BLK)],
        out_shape=[jax.ShapeDtypeStruct((hq_n, t_n, HEAD_DIM), F32), kv_shape, kv_shape,
                   jax.ShapeDtypeStruct((hq_n, BLK, 2 * BLK), F32), jax.ShapeDtypeStruct((hq_n, 1, BLK), F32)],
        name=name, compiler_params=_params(("parallel", "arbitrary", "arbitrary")),
    )(sinks, q, k, k, v, v, bias, o, lse, do, dlse)


def _dil_prev_mask(hq):
    group = hq // H_PER_DIL
    return jnp.where(group == 0, 15, jnp.where(group == 1, 3, 0))


def _swa_prev_mask(hq):
    del hq
    return 15


def dilated_merge(o, lse):
    def body(o_ref, l_ref, out_ref):
        lv = l_ref[...]
        m = jnp.max(lv, axis=0, keepdims=True)
        e = jnp.exp(lv - m)
        alpha = e / jnp.sum(e, axis=0, keepdims=True)
        out_ref[...] = jnp.sum(alpha * o_ref[...], axis=0)

    return pl.pallas_call(
        body, grid=(H_PER_DIL, SEQ // ROW_TILE),
        in_specs=[pl.BlockSpec((3, None, ROW_TILE, HEAD_DIM), lambda h, i: (0, h, i, 0)),
                  pl.BlockSpec((3, None, ROW_TILE, 1), lambda h, i: (0, h, i, 0))],
        out_specs=pl.BlockSpec((None, ROW_TILE, HEAD_DIM), lambda h, i: (h, i, 0)),
        out_shape=jax.ShapeDtypeStruct((H_PER_DIL, SEQ, HEAD_DIM), F32),
        name="dilated_merge", compiler_params=_params(("parallel", "parallel")),
    )(o, lse)


def dilated_merge_bwd(o, lse, dout):
    def body(o_ref, l_ref, d_ref, do_ref, dl_ref):
        lv = l_ref[...]
        m = jnp.max(lv, axis=0, keepdims=True)
        e = jnp.exp(lv - m)
        alpha = e / jnp.sum(e, axis=0, keepdims=True)
        dv = d_ref[...][None]
        do_ref[...] = alpha * dv
        dalpha = jnp.sum(dv * o_ref[...], axis=-1, keepdims=True)
        dl_ref[...] = alpha * (dalpha - jnp.sum(alpha * dalpha, axis=0, keepdims=True))

    o_spec = pl.BlockSpec((3, None, ROW_TILE, HEAD_DIM), lambda h, i: (0, h, i, 0))
    l_spec = pl.BlockSpec((3, None, ROW_TILE, 1), lambda h, i: (0, h, i, 0))
    return pl.pallas_call(
        body, grid=(H_PER_DIL, SEQ // ROW_TILE),
        in_specs=[o_spec, l_spec, pl.BlockSpec((None, ROW_TILE, HEAD_DIM), lambda h, i: (h, i, 0))],
        out_specs=[o_spec, l_spec],
        out_shape=[jax.ShapeDtypeStruct(o.shape, F32), jax.ShapeDtypeStruct(lse.shape, F32)],
        name="dilated_merge_bwd", compiler_params=_params(("parallel", "parallel")),
    )(o, lse, dout)


def rel_bias_reduce(dbias0, dbias1, bucket):
    def body(d0_ref, d1_ref, b_ref, o_ref):
        dv, bv = d0_ref[...] + d1_ref[...], b_ref[...]
        lane = lax.broadcasted_iota(jnp.int32, (1, BLK), 1)
        acc = jnp.zeros((1, BLK), F32)
        for bkt in range(N_BUCKETS):
            acc = acc + jnp.where(lane == bkt, jnp.sum(jnp.where(bv == bkt, dv, 0.0)), 0.0)
        o_ref[...] = acc

    tile = pl.BlockSpec((None, BLK, 2 * BLK), lambda h: (h, 0, 0))
    return pl.pallas_call(
        body, grid=(dbias0.shape[0],), in_specs=[tile, tile, tile],
        out_specs=pl.BlockSpec((None, 1, BLK), lambda h: (h, 0, 0)),
        out_shape=jax.ShapeDtypeStruct((dbias0.shape[0], 1, BLK), F32),
        name="rel_bias_reduce", compiler_params=_params(("parallel",)),
    )(dbias0, dbias1, bucket)


def _heads(t):
    return t.reshape(SEQ, -1, HEAD_DIM).transpose(1, 0, 2)


def _unheads(t):
    return t.transpose(1, 0, 2).reshape(SEQ, -1)


def _dilate(t):
    parts = []
    for g, (_, d) in enumerate(DIL_PATTERNS):
        tg = t[:, 128 * g:128 * (g + 1)].reshape(SEQ // d, d, H_PER_DIL, HEAD_DIM).transpose(2, 1, 0, 3)
        parts.append(tg.reshape(H_PER_DIL, SEQ, HEAD_DIM))
    return jnp.concatenate(parts, axis=0)


def _undilate(t):
    outs = []
    for g, (_, d) in enumerate(DIL_PATTERNS):
        tg = t[2 * g:2 * g + 2].reshape(H_PER_DIL, d, SEQ // d, -1).transpose(0, 2, 1, 3)
        outs.append(tg.reshape(H_PER_DIL, SEQ, -1))
    return jnp.stack(outs)


def _redilate(t):
    parts = []
    for g, (_, d) in enumerate(DIL_PATTERNS):
        tg = t[g].reshape(H_PER_DIL, SEQ // d, d, -1).transpose(0, 2, 1, 3)
        parts.append(tg.reshape(H_PER_DIL, SEQ, -1))
    return jnp.concatenate(parts, axis=0)


def _t5_bucket(n):
    max_exact = N_BUCKETS // 2
    nf = jnp.maximum(n, 1).astype(F32)
    large = max_exact + (jnp.log(nf / max_exact) / math.log(MAX_REL_DIST / max_exact)
                         * (N_BUCKETS - max_exact)).astype(jnp.int32)
    large = jnp.minimum(large, N_BUCKETS - 1)
    return jnp.where(n < max_exact, n, large)


def band_tables(rel_bias):
    rel = jnp.arange(BLK)[:, None] + BLK - jnp.arange(2 * BLK)[None, :]
    buckets = []
    patterns = [(d, w // d) for w, d in DIL_PATTERNS for _ in range(H_PER_DIL)] + [(1, SWA_WINDOW - 1)] * H_SWA_Q
    for d, max_dist in patterns:
        band = (rel >= 0) & (rel <= max_dist)
        buckets.append(jnp.where(band, _t5_bucket(jnp.maximum(rel, 0) * d), -1))
    buckets = jnp.stack(buckets).astype(jnp.int32)

    def body(table_ref, b_ref, o_ref):
        h = pl.program_id(0)
        bv = b_ref[...]
        tile = jnp.full(bv.shape, NEG, F32)
        for bkt in range(N_BUCKETS):
            tile = jnp.where(bv == bkt, table_ref[h, bkt], tile)
        o_ref[...] = tile

    spec = pl.BlockSpec((None, BLK, 2 * BLK), lambda h: (h, 0, 0))
    tiles = pl.pallas_call(
        body, grid=(len(patterns),), in_specs=[pl.BlockSpec(memory_space=pltpu.SMEM), spec], out_specs=spec,
        out_shape=jax.ShapeDtypeStruct(buckets.shape, F32), name="band_tables", compiler_params=_params(("parallel",)),
    )(rel_bias.T, buckets)
    return tiles[:H_DIL], tiles[H_DIL:], buckets


def _vec(v):
    return v.reshape(1, D_MODEL)


def ffn_forward(lf, x, gain, mod, w):
    h = prenorm(x, _vec(gain), _vec(mod[1]), _vec(mod[0]))
    a, b, s = ffn_up(h, w["gate"], w["up"], lf)
    f, xo = ffn_down(s, w["down"], lf, x, _vec(mod[2]))
    return xo, (x, h, a, b, s, f)


def ffn_backward(lf, dxo, saved, gain, mod, w, g):
    x, h, a, b, s, f = saved
    df, dgate = resid_bwd(dxo, f, _vec(mod[2]), 0.5)
    da, db = ffn_bwd_hidden(df, w["down"], lf, a, b)
    g["gate"], g["up"], g["down"] = ffn_grad_weights(h, s, df, da, db, g["gate"], g["up"], g["down"], lf)
    dx, stats = ffn_bwd_input(da, db, w["gate"], w["up"], lf, x, dxo, _vec(gain), _vec(mod[1]))
    return dx, jnp.stack([stats[0], stats[1], dgate[0]]), stats[2]


def mixer_forward(l, x, gain, mod, sinks, bias_dil, bias_swa, w):
    h = prenorm(x, _vec(gain), _vec(mod[1]), _vec(mod[0]))
    proj = in_proj(h, w["in"], l)
    qkv = proj[:, :D_QKV].astype(BF16)
    q_sb, k_sb, v_sb = _heads(qkv[:, 0:256] * QK_SCALE), _heads(qkv[:, 256:512]), _heads(qkv[:, 512:768])
    q_dil, k_dil, v_dil = _dilate(qkv[:, 768:1152] * QK_SCALE), _dilate(qkv[:, 1152:1536]), _dilate(qkv[:, 1536:1920])
    q_swa, k_swa, v_swa = _heads(qkv[:, 1920:2304] * QK_SCALE), _heads(qkv[:, 2304:2432]), _heads(qkv[:, 2432:2560])
    o_sb, total_sb = sb_forward(q_sb, k_sb, v_sb)
    no_sinks = jnp.full((H_DIL,), NEG, F32)
    o_dd, lse_dd = banded_forward("dilated_forward", q_dil, k_dil, v_dil, bias_dil, no_sinks, 1, _dil_prev_mask)
    o_dt, lse_dt = _undilate(o_dd), _undilate(lse_dd)
    o_dil = dilated_merge(o_dt, lse_dt)
    o_swa, lse_swa = banded_forward("swa_forward", q_swa, k_swa, v_swa, bias_swa, sinks, H_SWA_Q // H_SWA_KV,
                                    _swa_prev_mask)
    o_cat = jnp.concatenate([_unheads(o_sb), _unheads(o_dil), _unheads(o_swa)], axis=1)
    merged = merge_branches(o_cat, w["br"], l, proj)
    mo, xo = out_proj(merged, w["out"], l, x, _vec(mod[2]))
    saved = (x, h, proj, (q_sb, k_sb, v_sb, total_sb), (q_dil, k_dil, v_dil, o_dd, lse_dd, o_dt, lse_dt),
             (q_swa, k_swa, v_swa, o_swa, lse_swa), o_cat, merged, mo)
    return xo, saved


def mixer_backward(l, dxo, saved, gain, mod, sinks, bias_dil, bias_swa, w, g):
    x, h, proj, sb, dil, swa, o_cat, merged, mo = saved
    dmo, dgate = resid_bwd(dxo, mo, _vec(mod[2]), 1.0)
    tok = pl.BlockSpec((MM_TILE, D_MODEL), lambda j, k: (k, 0))
    g["out"] = grad_weight("grad_w_out", merged, pl.BlockSpec((MM_TILE, D_SHARD), lambda j, k: (k, j)), dmo, tok,
                           g["out"], (None, None, D_SHARD, D_MODEL), lambda j, k: (j, l, 0, 0), (D_SHARD, D_MODEL))
    du0, du1, du2, dg0, dg1, dg2 = merge_bwd(dmo, w["out"], o_cat, w["br"], l, proj)
    du = (du0, du1, du2)
    do_cat = branch_bwd_input(du, w["br"], l)
    g["br"] = branch_grad_weights(o_cat, du, g["br"], l)

    q_sb, k_sb, v_sb, total_sb = sb
    dq_sb, dk_sb, dv_sb = sb_backward(q_sb, k_sb, v_sb, total_sb, _heads(do_cat[:, 0:256]))

    q_dil, k_dil, v_dil, o_dd, lse_dd, o_dt, lse_dt = dil
    do_dt, dlse_dt = dilated_merge_bwd(o_dt, lse_dt, _heads(do_cat[:, 256:384]))
    no_sinks = jnp.full((H_DIL,), NEG, F32)
    dq_dil, dk_dil, dv_dil, dbias_dil, _ = banded_backward(
        "dilated_backward", q_dil, k_dil, v_dil, bias_dil, no_sinks, o_dd, lse_dd, _redilate(do_dt), _redilate(dlse_dt),
        1, _dil_prev_mask)

    q_swa, k_swa, v_swa, o_swa, lse_swa = swa
    dq_swa, dk_swa, dv_swa, dbias_swa, dsinks = banded_backward(
        "swa_backward", q_swa, k_swa, v_swa, bias_swa, sinks, o_swa, lse_swa, _heads(do_cat[:, 384:768]),
        jnp.zeros_like(lse_swa), H_SWA_Q // H_SWA_KV, _swa_prev_mask)

    def tokens(t):
        return _undilate(t).transpose(2, 0, 1, 3).reshape(SEQ, -1)

    dproj = jnp.concatenate(
        [_unheads(dq_sb), _unheads(dk_sb), _unheads(dv_sb), tokens(dq_dil), tokens(dk_dil), tokens(dv_dil),
         _unheads(dq_swa), _unheads(dk_swa), _unheads(dv_swa)], axis=1).astype(BF16)
    dproj = jnp.concatenate([dproj, dg0, dg1, dg2], axis=1)
    g["in"] = grad_weight("grad_w_in", h, tok, dproj, pl.BlockSpec((MM_TILE, IN_SHARD), lambda j, k: (k, j)),
                          g["in"], (None, None, D_MODEL, IN_SHARD), lambda j, k: (j, l, 0, 0), (D_MODEL, IN_SHARD))
    dx, stats = mixer_bwd_input(dproj, w["in"], l, x, dxo, _vec(gain), _vec(mod[1]))
    dmod = jnp.stack([stats[0], stats[1], dgate[0]])
    return dx, dmod, stats[2], jnp.concatenate([dbias_dil, dbias_swa], axis=0), dsinks[:, 0, 0]


GRAD_SHAPES = {
    "gate": (N_CHIPS, 2 * DEPTH, D_MODEL, FF_SHARD), "up": (N_CHIPS, 2 * DEPTH, D_MODEL, FF_SHARD),
    "down": (N_CHIPS, 2 * DEPTH, FF_SHARD, D_MODEL), "in": (N_CHIPS, DEPTH, D_MODEL, IN_SHARD),
    "br": (N_CHIPS, DEPTH, BR_ROWS, D_SHARD), "out": (N_CHIPS, DEPTH, D_SHARD, D_MODEL),
}


def device_step(x, target, mod, gains, final_gain, sinks, rel_bias, w):
    bias_dil, bias_swa, bucket = band_tables(rel_bias)
    saved = []
    for l in range(DEPTH):
        x, s0 = ffn_forward(2 * l, x, gains[l, 0], mod[l, 0], w)
        x, s1 = mixer_forward(l, x, gains[l, 1], mod[l, 1], sinks[l], bias_dil, bias_swa, w)
        x, s2 = ffn_forward(2 * l + 1, x, gains[l, 2], mod[l, 2], w)
        saved.append((s0, s1, s2))
    loss, dx, dfinal = final_loss(x, _vec(final_gain), target)

    g = {k: lax.empty(shape, BF16) for k, shape in GRAD_SHAPES.items()}
    dmod, dgains, dbias, dsinks = [None] * DEPTH, [None] * DEPTH, [None] * DEPTH, [None] * DEPTH
    for l in reversed(range(DEPTH)):
        s0, s1, s2 = saved[l]
        dx, dm2, dg2 = ffn_backward(2 * l + 1, dx, s2, gains[l, 2], mod[l, 2], w, g)
        dx, dm1, dg1, dbias[l], dsinks[l] = mixer_backward(l, dx, s1, gains[l, 1], mod[l, 1], sinks[l], bias_dil,
                                                           bias_swa, w, g)
        dx, dm0, dg0 = ffn_backward(2 * l, dx, s0, gains[l, 0], mod[l, 0], w, g)
        dmod[l] = jnp.stack([dm0, dm1, dm2])
        dgains[l] = jnp.stack([dg0, dg1, dg2])
    drel = rel_bias_reduce(dbias[0], dbias[1], bucket)[:, 0, :N_BUCKETS].T
    return loss, dx, g, jnp.stack(dmod), jnp.stack(dgains), dfinal[0], jnp.stack(dsinks), drel


MESH = pl.DeviceIdType.MESH
CHIP_FLIPS = ((1, 0), (0, 1), (1, 1))
ANY = pl.BlockSpec(memory_space=pl.ANY)


def _position():
    return lax.axis_index("x"), lax.axis_index("y"), lax.axis_index("c")


def all_gather_small(name, piece):
    def body(x_ref, out_ref, send_sems, recv_sems, local_sem):
        x, y, c = _position()
        me, sibling = (x, y, c), (x, y, 1 - c)
        chips = [(x ^ fx, y ^ fy) for fx, fy in CHIP_FLIPS]

        def rows(px, py, pc):
            return out_ref.at[4 * px + 2 * py + pc]

        def copy(k, block, to, src=None):
            return pltpu.make_async_remote_copy(
                src_ref=rows(*block) if src is None else src, dst_ref=rows(*block),
                send_sem=send_sems.at[k], recv_sem=recv_sems.at[k], device_id=to, device_id_type=MESH)

        mine = pltpu.make_async_copy(x_ref, rows(*me), local_sem)
        mine.start()
        first = [copy(0, me, sibling, src=x_ref)]
        first += [copy(1 + j, me, (*chip, c), src=x_ref) for j, chip in enumerate(chips)]
        for cp in first:
            cp.start()
        passed = [copy(4 + j, (*chip, c), sibling) for j, chip in enumerate(chips)]
        for j, chip in enumerate(chips):
            copy(1 + j, (*chip, c), me).wait_recv()
            passed[j].start()
        copy(0, sibling, me).wait_recv()
        for j, chip in enumerate(chips):
            copy(4 + j, (*chip, 1 - c), me).wait_recv()
        for cp in first + passed:
            cp.wait_send()
        mine.wait()

    return pl.pallas_call(
        body, out_shape=jax.ShapeDtypeStruct((N_DEV,) + piece.shape, piece.dtype),
        in_specs=[pl.BlockSpec(memory_space=pltpu.VMEM)], out_specs=pl.BlockSpec(memory_space=pltpu.VMEM),
        scratch_shapes=[pltpu.SemaphoreType.DMA((7,)), pltpu.SemaphoreType.DMA((7,)), pltpu.SemaphoreType.DMA],
        name=name,
    )(piece)


def exchange(name, operands, out_shapes, aliases, plan):
    n_in, n_out = len(operands), len(out_shapes)

    def body(*refs):
        ins, outs = refs[:n_in], refs[n_in:n_in + n_out]
        send_sems, recv_sems, local_sems = refs[n_in + n_out:]
        x, y, c = _position()
        local, sends, recvs = plan(ins, outs, x, y, c)
        local = [pltpu.make_async_copy(s, d, local_sems.at[k]) for k, (s, d) in enumerate(local)]
        for cp in local:
            cp.start()
        remote = [pltpu.make_async_remote_copy(src_ref=s, dst_ref=d, send_sem=send_sems.at[k], recv_sem=recv_sems.at[k],
                                               device_id=dev, device_id_type=MESH)
                  for k, (s, d, dev) in enumerate(sends)]
        for cp in remote:
            cp.start()
        for k, r in enumerate(recvs):
            pltpu.make_async_remote_copy(src_ref=r, dst_ref=r, send_sem=send_sems.at[k], recv_sem=recv_sems.at[k],
                                         device_id=(x, y, c), device_id_type=MESH).wait_recv()
        for cp in remote:
            cp.wait_send()
        for cp in local:
            cp.wait()

    n_sends, n_local = plan.n_sends, max(plan.n_local, 1)
    return pl.pallas_call(
        body, out_shape=out_shapes, in_specs=[ANY] * n_in, out_specs=[ANY] * n_out,
        scratch_shapes=[pltpu.SemaphoreType.DMA((n_sends,)), pltpu.SemaphoreType.DMA((n_sends,)),
                        pltpu.SemaphoreType.DMA((n_local,))],
        input_output_aliases=aliases, name=name,
    )(*operands)


def _plan(n_local, n_sends):
    def wrap(fn):
        fn.n_local, fn.n_sends = n_local, n_sends
        return fn
    return wrap


def _half(ref, axis, c):
    rows = ref.shape[axis] // 2
    idx = [slice(None)] * len(ref.shape)
    idx[axis] = pl.ds(pl.multiple_of(c * rows, 16), rows)
    return ref.at[tuple(idx)]


def cast_into_slot(name, shard, chip):
    n_l, rows, cols = shard.shape
    tr = _row_tile(rows, cols)

    def body(chip_ref, s_ref, o_ref):
        del chip_ref
        o_ref[...] = s_ref[...].astype(BF16)

    return pl.pallas_call(
        body, out_shape=jax.ShapeDtypeStruct((N_CHIPS,) + shard.shape, BF16),
        grid_spec=pltpu.PrefetchScalarGridSpec(
            num_scalar_prefetch=1, grid=(n_l, rows // tr),
            in_specs=[pl.BlockSpec((None, tr, cols), lambda l, r, chip_ref: (l, r, 0))],
            out_specs=pl.BlockSpec((None, None, tr, cols), lambda l, r, chip_ref: (chip_ref[0], l, r, 0))),
        name=name, compiler_params=_params(("parallel", "parallel")),
    )(chip, shard)


def gather_weights(names, shards, chip):
    n = len(shards)
    slots = [cast_into_slot("cast_" + k, s, chip) for k, s in zip(names, shards)]

    @_plan(0, 3 * n)
    def over_chips(ins, outs, x, y, c):
        me = 2 * x + y
        sends, recvs = [], []
        for o in outs:
            for fx, fy in CHIP_FLIPS:
                sends.append((_half(o.at[me], 1, c), _half(o.at[me], 1, c), (x ^ fx, y ^ fy, c)))
                recvs.append(_half(o.at[2 * (x ^ fx) + (y ^ fy)], 1, c))
        return [], sends, recvs

    @_plan(0, 3 * n)
    def to_sibling(ins, outs, x, y, c):
        sends, recvs = [], []
        for o in outs:
            for fx, fy in CHIP_FLIPS:
                slab = o.at[2 * (x ^ fx) + (y ^ fy)]
                sends.append((_half(slab, 1, c), _half(slab, 1, c), (x, y, 1 - c)))
                recvs.append(_half(slab, 1, 1 - c))
        return [], sends, recvs

    shapes = [jax.ShapeDtypeStruct(s.shape, BF16) for s in slots]
    through = {k: k for k in range(n)}
    halves = exchange("gather_weights_chips", slots, shapes, through, over_chips)
    return exchange("gather_weights_sibling", halves, shapes, through, to_sibling)


def reduce_gradients(grads, xi, yi, ci):
    n = len(grads)
    chip = (2 * xi + yi).astype(jnp.int32).reshape(1)
    core = ci.astype(jnp.int32).reshape(1)

    @_plan(0, n)
    def swap_halves(ins, outs, x, y, c):
        sends = [(_half(g, 2, 1 - c), o, (x, y, 1 - c)) for g, o in zip(ins, outs)]
        return [], sends, list(outs)

    half_shapes = [jax.ShapeDtypeStruct(g.shape[:2] + (g.shape[2] // 2, g.shape[3]), BF16) for g in grads]
    landed = exchange("reduce_swap_halves", grads, half_shapes, {}, swap_halves)
    chip_sums = [_add_halves(g, la, core) for g, la in zip(grads, landed)]

    @_plan(0, 3 * n)
    def over_chips(ins, outs, x, y, c):
        sends, recvs = [], []
        for s, o in zip(ins, outs):
            for k, (fx, fy) in enumerate(CHIP_FLIPS):
                sends.append((s.at[2 * (x ^ fx) + (y ^ fy)], o.at[k], (x ^ fx, y ^ fy, c)))
                recvs.append(o.at[k])
        return [], sends, recvs

    land_shapes = [jax.ShapeDtypeStruct((3,) + s.shape[1:], BF16) for s in chip_sums]
    from_chips = exchange("reduce_over_chips", chip_sums, land_shapes, {}, over_chips)
    halves = [_add_chips(s, la, jnp.concatenate([chip, core])) for s, la in zip(chip_sums, from_chips)]

    @_plan(0, n)
    def share_halves(ins, outs, x, y, c):
        sends = [(_half(o, 1, c), _half(o, 1, c), (x, y, 1 - c)) for o in outs]
        return [], sends, [_half(o, 1, 1 - c) for o in outs]

    full_shapes = [jax.ShapeDtypeStruct(h.shape, F32) for h in halves]
    return exchange("reduce_share_halves", halves, full_shapes, {k: k for k in range(n)}, share_halves)


def _row_tile(rows, cols):
    best = 16
    for t in range(16, rows + 1, 16):
        if rows % t == 0 and t * cols <= 256 * 1024:
            best = t
    return best


def _add_halves(g, landed, core):
    _, n_l, rh, cols = landed.shape
    tr = _row_tile(rh, cols)
    per_half = rh // tr

    def body(core_ref, g_ref, la_ref, o_ref):
        del core_ref
        o_ref[...] = (g_ref[...].astype(F32) + la_ref[...].astype(F32)).astype(BF16)

    blk = (None, None, tr, cols)
    return pl.pallas_call(
        body, out_shape=jax.ShapeDtypeStruct(landed.shape, BF16),
        grid_spec=pltpu.PrefetchScalarGridSpec(
            num_scalar_prefetch=1, grid=(N_CHIPS, n_l, per_half),
            in_specs=[pl.BlockSpec(blk, lambda j, l, r, core_ref: (j, l, core_ref[0] * per_half + r, 0)),
                      pl.BlockSpec(blk, lambda j, l, r, core_ref: (j, l, r, 0))],
            out_specs=pl.BlockSpec(blk, lambda j, l, r, core_ref: (j, l, r, 0))),
        name="reduce_add_halves", compiler_params=_params(("parallel", "parallel", "parallel")),
    )(core, g, landed)


def _add_chips(sums, landed, place):
    _, n_l, rh, cols = sums.shape
    tr = _row_tile(rh, cols)
    per_half = rh // tr

    def body(place_ref, s_ref, la_ref, o_ref):
        del place_ref
        o_ref[...] = ((s_ref[...].astype(F32) + la_ref[0].astype(F32)) + la_ref[1].astype(F32)) + la_ref[2].astype(F32)

    return pl.pallas_call(
        body, out_shape=jax.ShapeDtypeStruct((n_l, 2 * rh, cols), F32),
        grid_spec=pltpu.PrefetchScalarGridSpec(
            num_scalar_prefetch=1, grid=(n_l, per_half),
            in_specs=[pl.BlockSpec((None, None, tr, cols), lambda l, r, place_ref: (place_ref[0], l, r, 0)),
                      pl.BlockSpec((3, None, tr, cols), lambda l, r, place_ref: (0, l, r, 0))],
            out_specs=pl.BlockSpec((None, tr, cols), lambda l, r, place_ref: (l, place_ref[1] * per_half + r, 0))),
        name="reduce_add_chips", compiler_params=_params(("parallel", "parallel")),
    )(place, sums, landed)


def sum_devices(parts):
    def body(p_ref, o_ref):
        acc = p_ref[0]
        for d in range(1, N_DEV):
            acc = acc + p_ref[d]
        o_ref[...] = acc

    return pl.pallas_call(body, out_shape=jax.ShapeDtypeStruct(parts.shape[1:], F32), name="sum_devices")(parts)


ADA_SHARD = 9 * D_MODEL // N_CHIPS
ADA_TILE = 768
ADA_ROWS = 16


def ada_forward(c_rows, w_ada, b_shard):
    def body(c_ref, w_ref, b_ref, o_ref):
        cv = c_ref[...]
        o_ref[...] = _dot((cv * _sigmoid(cv)).astype(BF16), w_ref[...].astype(BF16), NN) + b_ref[...]

    return pl.pallas_call(
        body, grid=(DEPTH, ADA_SHARD // ADA_TILE),
        in_specs=[pl.BlockSpec((ADA_ROWS, D_MODEL), lambda l, n: (0, 0)),
                  pl.BlockSpec((None, D_MODEL, ADA_TILE), lambda l, n: (l, 0, n)),
                  pl.BlockSpec((None, 1, ADA_TILE), lambda l, n: (l, 0, n))],
        out_specs=pl.BlockSpec((None, ADA_ROWS, ADA_TILE), lambda l, n: (l, 0, n)),
        out_shape=jax.ShapeDtypeStruct((DEPTH, ADA_ROWS, ADA_SHARD), F32),
        name="ada_forward", compiler_params=_params(("parallel", "parallel")),
    )(c_rows, w_ada, b_shard)


def ada_backward(c_rows, dmod_rows):
    def body(c_ref, d_ref, o_ref):
        cv = c_ref[...]
        o_ref[...] = _dot((cv * _sigmoid(cv)).astype(BF16), d_ref[...].astype(BF16), TN)

    return pl.pallas_call(
        body, grid=(DEPTH, ADA_SHARD // ADA_TILE),
        in_specs=[pl.BlockSpec((ADA_ROWS, D_MODEL), lambda l, n: (0, 0)),
                  pl.BlockSpec((None, ADA_ROWS, ADA_TILE), lambda l, n: (l, 0, n))],
        out_specs=pl.BlockSpec((None, D_MODEL, ADA_TILE), lambda l, n: (l, 0, n)),
        out_shape=jax.ShapeDtypeStruct((DEPTH, D_MODEL, ADA_SHARD), F32),
        name="ada_backward", compiler_params=_params(("parallel", "parallel")),
    )(c_rows, dmod_rows)


def adamw(name, w, g, m, v):
    shape = w.shape
    cols = shape[-1]
    rows = w.size // cols
    tr = _row_tile(rows, cols) if rows % 16 == 0 else rows
    c1 = 1.0 / (1.0 - ADAM_B1 ** ADAM_STEP)
    c2 = 1.0 / (1.0 - ADAM_B2 ** ADAM_STEP)

    def body(w_ref, g_ref, m_ref, v_ref, d_ref, mo_ref, vo_ref):
        gv = g_ref[...]
        mn = ADAM_B1 * m_ref[...] + (1.0 - ADAM_B1) * gv
        vn = ADAM_B2 * v_ref[...] + (1.0 - ADAM_B2) * (gv * gv)
        mo_ref[...] = mn
        vo_ref[...] = vn
        d_ref[...] = -ADAM_LR * ((mn * c1) / (jnp.sqrt(vn * c2) + ADAM_EPS) + ADAM_WD * w_ref[...])

    spec = pl.BlockSpec((tr, cols), lambda i: (i, 0))
    out = jax.ShapeDtypeStruct((rows, cols), F32)
    res = pl.pallas_call(
        body, grid=(rows // tr,), in_specs=[spec] * 4, out_specs=[spec] * 3, out_shape=[out] * 3,
        name=name, compiler_params=_params(("parallel",)),
    )(*[t.reshape(rows, cols) for t in (w, g, m, v)])
    return tuple(r.reshape(shape) for r in res)


def _pack(parts, rows):
    flat = jnp.concatenate([p.reshape(-1) for p in parts])
    return jnp.pad(flat, (0, rows * 128 - flat.size)).reshape(rows, 128)


def _unpack(flat, shapes):
    out, at = [], 0
    for s in shapes:
        n = math.prod(s)
        out.append(flat[at:at + n].reshape(s))
        at += n
    return out


def kernel(x, c, w_ada, b_ada, norm_gain, w_ffn_gate, w_ffn_up, w_ffn_down, w_in, w_br_sb, w_br_dil, w_br_swa, w_out, sinks, rel_bias, final_gain, loss_target, m_w_ada, m_b_ada, m_norm_gain, m_w_ffn_gate, m_w_ffn_up, m_w_ffn_down, m_w_in, m_w_br_sb, m_w_br_dil, m_w_br_swa, m_w_out, m_sinks, m_rel_bias, m_final_gain, v_w_ada, v_b_ada, v_norm_gain, v_w_ffn_gate, v_w_ffn_up, v_w_ffn_down, v_w_in, v_w_br_sb, v_w_br_dil, v_w_br_swa, v_w_out, v_sinks, v_rel_bias, v_final_gain):
    xi, yi, ci = _position()
    chip = 2 * xi + yi
    dev = 2 * chip + ci

    c_all = all_gather_small("gather_c", c.reshape(8, 128)).reshape(N_DEV, D_MODEL)
    c_rows = jnp.pad(c_all, ((0, ADA_ROWS - N_DEV), (0, 0)))
    b_shard = lax.dynamic_slice_in_dim(b_ada, chip * ADA_SHARD, ADA_SHARD, axis=1).reshape(DEPTH, 1, ADA_SHARD)
    mod_shard = ada_forward(c_rows, w_ada, b_shard)[:, :N_DEV]
    n_mod = DEPTH * N_DEV * ADA_SHARD
    gathered = all_gather_small("gather_mod", _pack([mod_shard, norm_gain], 304))[::2].reshape(N_CHIPS, -1)
    mod_all = gathered[:, :n_mod].reshape(N_CHIPS, DEPTH, N_DEV, ADA_SHARD)
    mod = lax.dynamic_index_in_dim(mod_all, dev, axis=2, keepdims=False)
    mod = mod.transpose(1, 0, 2).reshape(DEPTH, 3, 3, D_MODEL)
    gains = gathered[:, n_mod:n_mod + DEPTH * 3 * D_SHARD].reshape(N_CHIPS, DEPTH, 3, D_SHARD)
    gains = gains.transpose(1, 2, 0, 3).reshape(DEPTH, 3, D_MODEL)

    names = ("gate", "up", "down", "in", "br", "out")
    shards = [
        w_ffn_gate.reshape(2 * DEPTH, D_MODEL, FF_SHARD), w_ffn_up.reshape(2 * DEPTH, D_MODEL, FF_SHARD),
        w_ffn_down.reshape(2 * DEPTH, FF_SHARD, D_MODEL), w_in,
        jnp.concatenate([w_br_sb, w_br_dil, w_br_swa], axis=1), w_out]
    w = dict(zip(names, gather_weights(names, shards, chip.astype(jnp.int32).reshape(1))))

    loss, dx, g, dmod, dgains, dfinal, dsinks, drel = device_step(
        x[0], loss_target[0], mod, gains, final_gain, sinks, rel_bias, w)

    small_shapes = [(DEPTH, 9 * D_MODEL), (DEPTH, 3, D_MODEL), (D_MODEL,), (DEPTH, H_SWA_Q), (N_BUCKETS, 12), (1,)]
    small_all = all_gather_small("gather_small_grads", _pack([dmod, dgains, dfinal, dsinks, drel, loss[0, 0:1]], 208))
    g_b_ada, g_gain_full, g_final, g_sinks, g_rel, loss_sum = _unpack(sum_devices(small_all).reshape(-1), small_shapes)
    g_gain = lax.dynamic_slice_in_dim(g_gain_full, chip * D_SHARD, D_SHARD, axis=2)
    dmod_all = small_all.reshape(N_DEV, -1)[:, :DEPTH * 9 * D_MODEL].reshape(N_DEV, DEPTH, 9 * D_MODEL)
    dmod_rows = lax.dynamic_slice_in_dim(dmod_all, chip * ADA_SHARD, ADA_SHARD, axis=2).transpose(1, 0, 2)
    g_w_ada = ada_backward(c_rows, jnp.pad(dmod_rows, ((0, 0), (0, ADA_ROWS - N_DEV), (0, 0))))

    g_gate, g_up, g_down, g_in, g_br, g_out = reduce_gradients([g[k] for k in names], xi, yi, ci)
    grads = [
        g_w_ada, g_b_ada, g_gain, g_gate.reshape(w_ffn_gate.shape), g_up.reshape(w_ffn_up.shape),
        g_down.reshape(w_ffn_down.shape), g_in, g_br[:, 0:256], g_br[:, 256:384], g_br[:, 384:768], g_out,
        g_sinks, g_rel, g_final]
    weights = [w_ada, b_ada, norm_gain, w_ffn_gate, w_ffn_up, w_ffn_down, w_in, w_br_sb, w_br_dil, w_br_swa, w_out,
               sinks, rel_bias, final_gain]
    ms = [m_w_ada, m_b_ada, m_norm_gain, m_w_ffn_gate, m_w_ffn_up, m_w_ffn_down, m_w_in, m_w_br_sb, m_w_br_dil,
          m_w_br_swa, m_w_out, m_sinks, m_rel_bias, m_final_gain]
    vs = [v_w_ada, v_b_ada, v_norm_gain, v_w_ffn_gate, v_w_ffn_up, v_w_ffn_down, v_w_in, v_w_br_sb, v_w_br_dil,
          v_w_br_swa, v_w_out, v_sinks, v_rel_bias, v_final_gain]

    small = (1, 2, 11, 12, 13)
    deltas, new_ms, new_vs = [None] * 14, [None] * 14, [None] * 14
    for k in range(14):
        if k not in small:
            deltas[k], new_ms[k], new_vs[k] = adamw(f"adamw_{k}", weights[k], grads[k], ms[k], vs[k])
    shapes = [weights[k].shape for k in small]
    packed = [_pack([t[k] for k in small], 168) for t in (weights, grads, ms, vs)]
    for dst, res in zip((deltas, new_ms, new_vs), adamw("adamw_small", *packed)):
        for k, t in zip(small, _unpack(res.reshape(-1), shapes)):
            dst[k] = t
    return (loss_sum[0], dx[None], *grads, *deltas, *new_ms, *new_vs)
```

```python
import functools
import math

import jax
import jax.numpy as jnp
from jax import lax
from jax.experimental import pallas as pl
from jax.experimental.pallas import tpu as pltpu

F32 = jnp.float32
BF16 = jnp.bfloat16

D_MODEL = 1024
SEQ = 2048
DEPTH = 2
HEAD_DIM = 64
BLK = 128
H_SB = 4
DIL_PATTERNS = ((128, 1), (512, 4), (2048, 16))
H_PER_DIL = 2
H_DIL = 6
H_SWA_Q = 6
H_SWA_KV = 2
SWA_WINDOW = 128
N_BUCKETS = 32
MAX_REL_DIST = 2048
D_FF = 2816
RMS_EPS = 1e-6
N_CHIPS = 4
N_DEV = 8
FF_SHARD = D_FF // N_CHIPS
D_QKV = 2560
D_IN = D_QKV + 3 * D_MODEL
IN_SHARD = D_IN // N_CHIPS
D_SHARD = D_MODEL // N_CHIPS
BR_ROWS = 768
NEG = -1e30
QK_SCALE = HEAD_DIM ** -0.5

ADAM_LR = 0.001
ADAM_B1 = 0.9
ADAM_B2 = 0.999
ADAM_EPS = 1e-08
ADAM_WD = 0.01
ADAM_STEP = 10

VMEM_LIMIT = 48 * 1024 * 1024
ROW_TILE = 256
MM_TILE = 512

NN = (((1,), (0,)), ((), ()))
NT = (((1,), (1,)), ((), ()))
TN = (((0,), (0,)), ((), ()))


def _params(sem=None):
    return pltpu.CompilerParams(dimension_semantics=sem, vmem_limit_bytes=VMEM_LIMIT)


def _dot(a, b, dims):
    return lax.dot_general(a, b, dims, preferred_element_type=F32)


def _sigmoid(x):
    return 1.0 / (1.0 + jnp.exp(-x))


def _matmul(name, grid, nk, k_axis, dims, n_pairs, in_specs, out_specs, out_shape, acc_shape, epilogue,
            operands, sem, aliases=None, prologue=None):
    n_in = len(in_specs)
    n_out = len(out_specs)

    def partial(ins):
        tot = None
        for p in range(n_pairs):
            a = ins[2 * p][...]
            if prologue is not None:
                a = prologue(p, a, ins)
            d = _dot(a, ins[2 * p + 1][...], dims)
            tot = d if tot is None else tot + d
        return tot

    def body(*refs):
        ins, outs = refs[:n_in], refs[n_in:n_in + n_out]
        ids = tuple(pl.program_id(a) for a in range(len(grid)))
        if nk == 1:
            epilogue(partial(ins), ins, outs, ids)
            return
        acc = refs[n_in + n_out]
        k = ids[k_axis]

        @pl.when(k == 0)
        def _():
            acc[...] = partial(ins)

        @pl.when(k > 0)
        def _():
            acc[...] += partial(ins)

        @pl.when(k == nk - 1)
        def _():
            epilogue(acc[...], ins, outs, ids)

    return pl.pallas_call(
        body, grid=grid, in_specs=in_specs, out_specs=out_specs, out_shape=out_shape,
        scratch_shapes=[] if nk == 1 else [pltpu.VMEM(acc_shape, F32)],
        input_output_aliases=aliases or {}, name=name, compiler_params=_params(sem),
    )(*operands)


def _row_spec(width=D_MODEL):
    return pl.BlockSpec((ROW_TILE, width), lambda i: (i, 0))


def _vec_spec(rows=1, width=D_MODEL):
    return pl.BlockSpec((rows, width), lambda i: (0, 0))


def prenorm(x, gain, scale, shift):
    def body(x_ref, g_ref, sc_ref, sh_ref, h_ref):
        xv = x_ref[...]
        r = lax.rsqrt(jnp.mean(xv * xv, axis=-1, keepdims=True) + RMS_EPS)
        h_ref[...] = (((xv * r) * g_ref[...]) * (1.0 + sc_ref[...]) + sh_ref[...]).astype(BF16)

    return pl.pallas_call(
        body, grid=(SEQ // ROW_TILE,), in_specs=[_row_spec(), _vec_spec(), _vec_spec(), _vec_spec()],
        out_specs=_row_spec(), out_shape=jax.ShapeDtypeStruct((SEQ, D_MODEL), BF16),
        name="prenorm", compiler_params=_params(("parallel",)),
    )(x, gain, scale, shift)


def resid_bwd(dxo, f, coef, mult):
    def body(dx_ref, f_ref, c_ref, df_ref, dc_ref):
        dx = dx_ref[...]
        df_ref[...] = (dx * (mult * c_ref[...])).astype(BF16)
        part = mult * jnp.sum(dx * f_ref[...], axis=0, keepdims=True)

        @pl.when(pl.program_id(0) == 0)
        def _():
            dc_ref[...] = jnp.zeros_like(dc_ref)

        dc_ref[0:1, :] += part

    return pl.pallas_call(
        body, grid=(SEQ // ROW_TILE,), in_specs=[_row_spec(), _row_spec(), _vec_spec()],
        out_specs=[_row_spec(), _vec_spec(8)],
        out_shape=[jax.ShapeDtypeStruct((SEQ, D_MODEL), BF16), jax.ShapeDtypeStruct((8, D_MODEL), F32)],
        name="resid_bwd", compiler_params=_params(("arbitrary",)),
    )(dxo, f, coef)


def final_loss(x, gain, target):
    def body(x_ref, g_ref, t_ref, loss_ref, dx_ref, dg_ref):
        xv = x_ref[...]
        g = g_ref[...]
        r = lax.rsqrt(jnp.mean(xv * xv, axis=-1, keepdims=True) + RMS_EPS)
        xh = xv * r
        e = xh * g - t_ref[...]
        part = 0.5 * jnp.sum(jnp.mean(e * e, axis=-1, keepdims=True), axis=0, keepdims=True)
        dy = e * (1.0 / D_MODEL)
        dyg = dy * g
        dx_ref[...] = r * (dyg - xh * jnp.mean(dyg * xh, axis=-1, keepdims=True))

        @pl.when(pl.program_id(0) == 0)
        def _():
            loss_ref[...] = jnp.zeros_like(loss_ref)
            dg_ref[...] = jnp.zeros_like(dg_ref)

        loss_ref[...] += jnp.broadcast_to(part, loss_ref.shape)
        dg_ref[0:1, :] += jnp.sum(dy * xh, axis=0, keepdims=True)

    return pl.pallas_call(
        body, grid=(SEQ // ROW_TILE,), in_specs=[_row_spec(), _vec_spec(), _row_spec()],
        out_specs=[_vec_spec(8, 128), _row_spec(), _vec_spec(8)],
        out_shape=[jax.ShapeDtypeStruct((8, 128), F32), jax.ShapeDtypeStruct((SEQ, D_MODEL), F32),
                   jax.ShapeDtypeStruct((8, D_MODEL), F32)],
        name="final_loss", compiler_params=_params(("arbitrary",)),
    )(x, gain, target)


def _prenorm_bwd_epilogue(dh, x_ref, dxo_ref, g_ref, sc_ref, dx_ref, stats_ref, first):
    xv = x_ref[...]
    g = g_ref[...]
    r = lax.rsqrt(jnp.mean(xv * xv, axis=-1, keepdims=True) + RMS_EPS)
    xh = xv * r
    dn = dh * (1.0 + sc_ref[...])
    dxh = dn * g
    dx_ref[...] = dxo_ref[...] + r * (dxh - xh * jnp.mean(dxh * xh, axis=-1, keepdims=True))

    @pl.when(first)
    def _():
        stats_ref[...] = jnp.zeros_like(stats_ref)

    stats_ref[0:1, :] += jnp.sum(dh, axis=0, keepdims=True)
    stats_ref[1:2, :] += jnp.sum(dh * (xh * g), axis=0, keepdims=True)
    stats_ref[2:3, :] += jnp.sum(dn * xh, axis=0, keepdims=True)


def ffn_up(h, wg_all, wu_all, lf):
    def body(h_ref, wg_ref, wu_ref, a_ref, b_ref, s_ref):
        hv = h_ref[...]
        a = _dot(hv, wg_ref[...], NN)
        b = _dot(hv, wu_ref[...], NN)
        a_ref[...] = a
        b_ref[...] = b
        s_ref[...] = (a * _sigmoid(a) * b).astype(BF16)

    w_spec = pl.BlockSpec((None, None, D_MODEL, FF_SHARD), lambda j, i: (j, lf, 0, 0))
    o_spec = pl.BlockSpec((None, MM_TILE, FF_SHARD), lambda j, i: (j, i, 0))
    hid = (N_CHIPS, SEQ, FF_SHARD)
    return pl.pallas_call(
        body, grid=(N_CHIPS, SEQ // MM_TILE),
        in_specs=[pl.BlockSpec((MM_TILE, D_MODEL), lambda j, i: (i, 0)), w_spec, w_spec],
        out_specs=[o_spec, o_spec, o_spec],
        out_shape=[jax.ShapeDtypeStruct(hid, F32), jax.ShapeDtypeStruct(hid, F32), jax.ShapeDtypeStruct(hid, BF16)],
        name="ffn_up", compiler_params=_params(("parallel", "parallel")),
    )(h, wg_all, wu_all)


def matmul_residual(name, a, a_spec, w_all, w_spec, x, coef, mult):
    def epilogue(acc, ins, outs, ids):
        outs[0][...] = acc
        outs[1][...] = ins[2][...] + (mult * ins[3][...]) * acc

    row = pl.BlockSpec((MM_TILE, D_MODEL), lambda i, j: (i, 0))
    return _matmul(
        name, (SEQ // MM_TILE, N_CHIPS), N_CHIPS, 1, NN, 1,
        [a_spec, w_spec, row, pl.BlockSpec((1, D_MODEL), lambda i, j: (0, 0))], [row, row],
        [jax.ShapeDtypeStruct((SEQ, D_MODEL), F32)] * 2, (MM_TILE, D_MODEL), epilogue,
        (a, w_all, x, coef), ("parallel", "arbitrary"))


def ffn_down(s, wd_all, lf, x, gate):
    return matmul_residual(
        "ffn_down", s, pl.BlockSpec((None, MM_TILE, FF_SHARD), lambda i, j: (j, i, 0)),
        wd_all, pl.BlockSpec((None, None, FF_SHARD, D_MODEL), lambda i, j: (j, lf, 0, 0)), x, gate, 0.5)


def ffn_bwd_hidden(df, wd_all, lf, a, b):
    def epilogue(ds, ins, outs, ids):
        av, bv = ins[2][...], ins[3][...]
        sig = _sigmoid(av)
        outs[0][...] = (ds * bv * (sig * (1.0 + av * (1.0 - sig)))).astype(BF16)
        outs[1][...] = (ds * (av * sig)).astype(BF16)

    hid_spec = pl.BlockSpec((None, MM_TILE, FF_SHARD), lambda j, i: (j, i, 0))
    hid = jax.ShapeDtypeStruct((N_CHIPS, SEQ, FF_SHARD), BF16)
    return _matmul(
        "ffn_bwd_hidden", (N_CHIPS, SEQ // MM_TILE), 1, None, NT, 1,
        [pl.BlockSpec((MM_TILE, D_MODEL), lambda j, i: (i, 0)),
         pl.BlockSpec((None, None, FF_SHARD, D_MODEL), lambda j, i: (j, lf, 0, 0)), hid_spec, hid_spec],
        [hid_spec, hid_spec], [hid, hid], None, epilogue, (df, wd_all, a, b), ("parallel", "parallel"))


def grad_weight(name, lhs, lhs_spec, rhs, rhs_spec, g_all, out_block, out_index, acc_shape):
    def epilogue(acc, ins, outs, ids):
        outs[0][...] = acc.astype(BF16)

    return _matmul(
        name, (N_CHIPS, SEQ // MM_TILE), SEQ // MM_TILE, 1, TN, 1,
        [lhs_spec, rhs_spec, pl.BlockSpec(memory_space=pl.ANY)], [pl.BlockSpec(out_block, out_index)],
        [jax.ShapeDtypeStruct(g_all.shape, BF16)], acc_shape, epilogue, (lhs, rhs, g_all),
        ("parallel", "arbitrary"), aliases={2: 0})[0]


def ffn_grad_weights(h, s, df, da, db, g_gate, g_up, g_down, lf):
    tok = pl.BlockSpec((MM_TILE, D_MODEL), lambda j, k: (k, 0))
    hid = pl.BlockSpec((None, MM_TILE, FF_SHARD), lambda j, k: (j, k, 0))
    up_block, up_index = (None, None, D_MODEL, FF_SHARD), (lambda j, k: (j, lf, 0, 0))
    g_gate = grad_weight("grad_w_gate", h, tok, da, hid, g_gate, up_block, up_index, (D_MODEL, FF_SHARD))
    g_up = grad_weight("grad_w_up", h, tok, db, hid, g_up, up_block, up_index, (D_MODEL, FF_SHARD))
    g_down = grad_weight("grad_w_down", s, hid, df, tok, g_down, (None, None, FF_SHARD, D_MODEL),
                         lambda j, k: (j, lf, 0, 0), (FF_SHARD, D_MODEL))
    return g_gate, g_up, g_down


def matmul_prenorm_bwd(name, pairs, pair_specs, x, dxo, gain, scale):
    n = len(pairs)

    def epilogue(dh, ins, outs, ids):
        _prenorm_bwd_epilogue(dh, ins[n], ins[n + 1], ins[n + 2], ins[n + 3], outs[0], outs[1], ids[0] == 0)

    row = pl.BlockSpec((MM_TILE, D_MODEL), lambda i, j: (i, 0))
    vec = pl.BlockSpec((1, D_MODEL), lambda i, j: (0, 0))
    return _matmul(
        name, (SEQ // MM_TILE, N_CHIPS), N_CHIPS, 1, NT, len(pairs) // 2,
        list(pair_specs) + [row, row, vec, vec], [row, pl.BlockSpec((8, D_MODEL), lambda i, j: (0, 0))],
        [jax.ShapeDtypeStruct((SEQ, D_MODEL), F32), jax.ShapeDtypeStruct((8, D_MODEL), F32)],
        (MM_TILE, D_MODEL), epilogue, tuple(pairs) + (x, dxo, gain, scale), ("arbitrary", "arbitrary"))


def ffn_bwd_input(da, db, wg_all, wu_all, lf, x, dxo, gain, scale):
    hid = pl.BlockSpec((None, MM_TILE, FF_SHARD), lambda i, j: (j, i, 0))
    w = pl.BlockSpec((None, None, D_MODEL, FF_SHARD), lambda i, j: (j, lf, 0, 0))
    return matmul_prenorm_bwd("ffn_bwd_input", (da, wg_all, db, wu_all), (hid, w, hid, w), x, dxo, gain, scale)


def in_proj(h, w_all, l):
    def epilogue(acc, ins, outs, ids):
        outs[0][...] = acc

    return _matmul(
        "in_proj", (N_CHIPS, SEQ // MM_TILE), 1, None, NN, 1,
        [pl.BlockSpec((MM_TILE, D_MODEL), lambda j, i: (i, 0)),
         pl.BlockSpec((None, None, D_MODEL, IN_SHARD), lambda j, i: (j, l, 0, 0))],
        [pl.BlockSpec((MM_TILE, IN_SHARD), lambda j, i: (i, j))], [jax.ShapeDtypeStruct((SEQ, D_IN), F32)],
        None, epilogue, (h, w_all), ("parallel", "parallel"))[0]


_GATE_BLOCK0 = D_QKV // D_SHARD


def _branch_products(o, w_ref):
    ob = o.astype(BF16)
    return (_dot(ob[:, 0:256], w_ref[0:256, :], NN), _dot(ob[:, 256:384], w_ref[256:384, :], NN),
            _dot(ob[:, 384:768], w_ref[384:768, :], NN))


def merge_branches(o_cat, wbr_all, l, proj):
    def body(o_ref, w_ref, g0_ref, g1_ref, g2_ref, m_ref):
        u = _branch_products(o_ref[...], w_ref)
        m_ref[...] = (_sigmoid(g0_ref[...]) * u[0] + _sigmoid(g1_ref[...]) * u[1]
                      + _sigmoid(g2_ref[...]) * u[2]).astype(BF16)

    def gate_spec(b):
        return pl.BlockSpec((MM_TILE, D_SHARD), lambda i, j: (i, _GATE_BLOCK0 + 4 * b + j))

    return pl.pallas_call(
        body, grid=(SEQ // MM_TILE, N_CHIPS),
        in_specs=[pl.BlockSpec((MM_TILE, BR_ROWS), lambda i, j: (i, 0)),
                  pl.BlockSpec((None, None, BR_ROWS, D_SHARD), lambda i, j: (j, l, 0, 0)),
                  gate_spec(0), gate_spec(1), gate_spec(2)],
        out_specs=pl.BlockSpec((MM_TILE, D_SHARD), lambda i, j: (i, j)),
        out_shape=jax.ShapeDtypeStruct((SEQ, D_MODEL), BF16),
        name="merge_branches", compiler_params=_params(("parallel", "parallel")),
    )(o_cat, wbr_all, proj, proj, proj)


def out_proj(merged, wout_all, l, x, gate):
    return matmul_residual(
        "out_proj", merged, pl.BlockSpec((MM_TILE, D_SHARD), lambda i, j: (i, j)),
        wout_all, pl.BlockSpec((None, None, D_SHARD, D_MODEL), lambda i, j: (j, l, 0, 0)), x, gate, 1.0)


def merge_bwd(dmo, wout_all, o_cat, wbr_all, l, proj):
    def epilogue(dm, ins, outs, ids):
        u = _branch_products(ins[2][...], ins[3])
        for b in range(3):
            sig = _sigmoid(ins[4 + b][...])
            outs[b][...] = (dm * sig).astype(BF16)
            outs[3 + b][...] = (dm * u[b] * (sig * (1.0 - sig))).astype(BF16)

    def gate_spec(b):
        return pl.BlockSpec((MM_TILE, D_SHARD), lambda j, i: (i, _GATE_BLOCK0 + 4 * b + j))

    col = pl.BlockSpec((MM_TILE, D_SHARD), lambda j, i: (i, j))
    du = jax.ShapeDtypeStruct((SEQ, D_MODEL), BF16)
    return _matmul(
        "merge_bwd", (N_CHIPS, SEQ // MM_TILE), 1, None, NT, 1,
        [pl.BlockSpec((MM_TILE, D_MODEL), lambda j, i: (i, 0)),
         pl.BlockSpec((None, None, D_SHARD, D_MODEL), lambda j, i: (j, l, 0, 0)),
         pl.BlockSpec((MM_TILE, BR_ROWS), lambda j, i: (i, 0)),
         pl.BlockSpec((None, None, BR_ROWS, D_SHARD), lambda j, i: (j, l, 0, 0)),
         gate_spec(0), gate_spec(1), gate_spec(2)],
        [col] * 6, [du] * 6,
        None, epilogue, (dmo, wout_all, o_cat, wbr_all, proj, proj, proj), ("parallel", "parallel"))


def branch_bwd_input(du, wbr_all, l):
    def body(d0_ref, d1_ref, d2_ref, w_ref, o_ref, acc):
        j = pl.program_id(1)
        parts = (_dot(d0_ref[...], w_ref[0:256, :], NT), _dot(d1_ref[...], w_ref[256:384, :], NT),
                 _dot(d2_ref[...], w_ref[384:768, :], NT))

        @pl.when(j == 0)
        def _():
            acc[:, 0:256], acc[:, 256:384], acc[:, 384:768] = parts

        @pl.when(j > 0)
        def _():
            acc[:, 0:256] += parts[0]
            acc[:, 256:384] += parts[1]
            acc[:, 384:768] += parts[2]

        @pl.when(j == N_CHIPS - 1)
        def _():
            o_ref[...] = acc[...]

    col = pl.BlockSpec((MM_TILE, D_SHARD), lambda i, j: (i, j))
    return pl.pallas_call(
        body, grid=(SEQ // MM_TILE, N_CHIPS),
        in_specs=[col, col, col, pl.BlockSpec((None, None, BR_ROWS, D_SHARD), lambda i, j: (j, l, 0, 0))],
        out_specs=pl.BlockSpec((MM_TILE, BR_ROWS), lambda i, j: (i, 0)),
        out_shape=jax.ShapeDtypeStruct((SEQ, BR_ROWS), F32),
        scratch_shapes=[pltpu.VMEM((MM_TILE, BR_ROWS), F32)],
        name="branch_bwd_input", compiler_params=_params(("parallel", "arbitrary")),
    )(du[0], du[1], du[2], wbr_all)


def branch_grad_weights(o_cat, du, g_br, l):
    def body(o_ref, d0_ref, d1_ref, d2_ref, g_in, g_ref, acc):
        del g_in
        k = pl.program_id(1)
        ob = o_ref[...].astype(BF16)
        parts = (_dot(ob[:, 0:256], d0_ref[...], TN), _dot(ob[:, 256:384], d1_ref[...], TN),
                 _dot(ob[:, 384:768], d2_ref[...], TN))

        @pl.when(k == 0)
        def _():
            acc[0:256, :], acc[256:384, :], acc[384:768, :] = parts

        @pl.when(k > 0)
        def _():
            acc[0:256, :] += parts[0]
            acc[256:384, :] += parts[1]
            acc[384:768, :] += parts[2]

        @pl.when(k == SEQ // MM_TILE - 1)
        def _():
            g_ref[...] = acc[...].astype(BF16)

    col = pl.BlockSpec((MM_TILE, D_SHARD), lambda j, k: (k, j))
    return pl.pallas_call(
        body, grid=(N_CHIPS, SEQ // MM_TILE),
        in_specs=[pl.BlockSpec((MM_TILE, BR_ROWS), lambda j, k: (k, 0)), col, col, col,
                  pl.BlockSpec(memory_space=pl.ANY)],
        out_specs=pl.BlockSpec((None, None, BR_ROWS, D_SHARD), lambda j, k: (j, l, 0, 0)),
        out_shape=jax.ShapeDtypeStruct(g_br.shape, BF16),
        scratch_shapes=[pltpu.VMEM((BR_ROWS, D_SHARD), F32)], input_output_aliases={4: 0},
        name="branch_grad_weights", compiler_params=_params(("parallel", "arbitrary")),
    )(o_cat, du[0], du[1], du[2], g_br)


def mixer_bwd_input(dproj, win_all, l, x, dxo, gain, scale):
    return matmul_prenorm_bwd(
        "mixer_bwd_input", (dproj, win_all),
        (pl.BlockSpec((MM_TILE, IN_SHARD), lambda i, j: (i, j)),
         pl.BlockSpec((None, None, D_MODEL, IN_SHARD), lambda i, j: (j, l, 0, 0))), x, dxo, gain, scale)


BATCH_QK = (((2,), (2,)), ((0,), (0,)))
BATCH_PV = (((2,), (1,)), ((0,), (0,)))
BATCH_TN = (((1,), (1,)), ((0,), (0,)))


def _split_dot(v, tri):
    v2 = v.reshape(v.shape[0] * BLK, BLK)
    hi = v2.astype(BF16)
    lo = (v2 - hi.astype(F32)).astype(BF16)
    return (_dot(hi, tri, NN) + _dot(lo, tri, NN)).reshape(v.shape)


def _tri(cmp):
    return cmp(lax.broadcasted_iota(jnp.int32, (BLK, BLK), 0), lax.broadcasted_iota(jnp.int32, (BLK, BLK), 1)).astype(BF16)


def _sb_scores(qs, k_ref, i, j):
    rows = pl.ds(pl.multiple_of(j * BLK, BLK), BLK)
    z = _dot(qs, k_ref[:, rows, :], BATCH_QK)
    tile = (1, BLK, BLK)
    before = (j * BLK + lax.broadcasted_iota(jnp.int32, tile, 2)) < (i * BLK + lax.broadcasted_iota(jnp.int32, tile, 1))
    soft = jnp.log(1.0 + jnp.exp(-jnp.abs(z)))
    log_fail = jnp.where(before, -(jnp.maximum(z, 0.0) + soft), 0.0)
    log_hit = jnp.minimum(z, 0.0) - soft
    return rows, before, log_fail, log_hit


def sb_forward(q, k, v):
    def body(q_ref, k_ref, v_ref, o_ref, tot_ref):
        i = pl.program_id(0)
        qs = q_ref[...]
        later = _tri(lambda r, c: r > c)

        def step(t, carry):
            o, run = carry
            rows, before, log_fail, log_hit = _sb_scores(qs, k_ref, i, i - t)
            between = _split_dot(log_fail, later) + run
            w = jnp.where(before, jnp.exp(log_hit + between), 0.0)
            o = o + _dot(w.astype(BF16), v_ref[:, rows, :], BATCH_PV)
            return o, run + jnp.sum(log_fail, axis=2, keepdims=True)

        o, run = lax.fori_loop(0, i + 1, step, (jnp.zeros((H_SB, BLK, HEAD_DIM), F32), jnp.zeros((H_SB, BLK, 1), F32)))
        o_ref[...] = o
        tot_ref[...] = run

    blk = pl.BlockSpec((H_SB, BLK, HEAD_DIM), lambda i: (0, i, 0))
    col = pl.BlockSpec((H_SB, BLK, 1), lambda i: (0, i, 0))
    full = pl.BlockSpec((H_SB, SEQ, HEAD_DIM), lambda i: (0, 0, 0))
    return pl.pallas_call(
        body, grid=(SEQ // BLK,), in_specs=[blk, full, full], out_specs=[blk, col],
        out_shape=[jax.ShapeDtypeStruct((H_SB, SEQ, HEAD_DIM), F32), jax.ShapeDtypeStruct((H_SB, SEQ, 1), F32)],
        name="sb_forward", compiler_params=_params(("parallel",)),
    )(q, k, v)


def sb_backward(q, k, v, total, do):
    def body(q_ref, k_ref, v_ref, tot_ref, do_ref, dq_ref, dk_ref, dv_ref):
        i = pl.program_id(0)

        @pl.when(i == 0)
        def _():
            dk_ref[...] = jnp.zeros_like(dk_ref)
            dv_ref[...] = jnp.zeros_like(dv_ref)

        qs = q_ref[...]
        dob = do_ref[...].astype(BF16)
        total_v = tot_ref[...]
        upto = _tri(lambda r, c: r <= c)
        earlier = _tri(lambda r, c: r < c)

        def step(j, carry):
            dq, seen, g_seen = carry
            rows, before, log_fail, log_hit = _sb_scores(qs, k_ref, i, j)
            between = total_v - (seen + _split_dot(log_fail, upto))
            w = jnp.where(before, jnp.exp(log_hit + between), 0.0)
            g = _dot(dob, v_ref[:, rows, :], BATCH_QK) * w
            g_earlier = g_seen + _split_dot(g, earlier)
            sig = jnp.exp(log_hit)
            dz = jnp.where(before, g * (1.0 - sig) - g_earlier * sig, 0.0).astype(BF16)
            dq = dq + _dot(dz, k_ref[:, rows, :], BATCH_PV)
            dk_ref[:, rows, :] += _dot(dz, qs, BATCH_TN)
            dv_ref[:, rows, :] += _dot(w.astype(BF16), dob, BATCH_TN)
            return dq, seen + jnp.sum(log_fail, axis=2, keepdims=True), g_seen + jnp.sum(g, axis=2, keepdims=True)

        zero = jnp.zeros((H_SB, BLK, 1), F32)
        dq, _, _ = lax.fori_loop(0, i + 1, step, (jnp.zeros((H_SB, BLK, HEAD_DIM), F32), zero, zero))
        dq_ref[...] = dq * QK_SCALE

    blk = pl.BlockSpec((H_SB, BLK, HEAD_DIM), lambda i: (0, i, 0))
    col = pl.BlockSpec((H_SB, BLK, 1), lambda i: (0, i, 0))
    full = pl.BlockSpec((H_SB, SEQ, HEAD_DIM), lambda i: (0, 0, 0))
    shape = jax.ShapeDtypeStruct((H_SB, SEQ, HEAD_DIM), F32)
    return pl.pallas_call(
        body, grid=(SEQ // BLK,), in_specs=[blk, full, full, col, blk], out_specs=[blk, full, full],
        out_shape=[shape, shape, shape],
        name="sb_backward", compiler_params=_params(("arbitrary",)),
    )(q, k, v, total, do)


def _band_scores(q_ref, kp_ref, ko_ref, bias_ref, hb, prev_mask):
    b = pl.program_id(1)
    qs = q_ref[...]
    s_prev = _dot(qs, kp_ref[...], BATCH_QK) + bias_ref[:, :, 0:BLK]
    s_prev = jnp.concatenate(
        [jnp.where((b & prev_mask(pl.program_id(0) * hb + t)) != 0, s_prev[t:t + 1], NEG) for t in range(hb)], axis=0)
    s_own = _dot(qs, ko_ref[...], BATCH_QK) + bias_ref[:, :, BLK:2 * BLK]
    return qs, s_prev, s_own


def _band_specs(hb, rows, t_n):
    def q_spec(width):
        return pl.BlockSpec((hb, None, rows, width), lambda h, b: (h, b, 0, 0))

    own = pl.BlockSpec((hb, BLK, HEAD_DIM), lambda h, b: (h, b, 0))
    prev = pl.BlockSpec((hb, BLK, HEAD_DIM), lambda h, b: (h, jnp.maximum(b - 1, 0), 0))
    per_head = lambda r, width: pl.BlockSpec((hb, r, width), lambda h, b: (h, 0, 0))
    return q_spec, own, prev, per_head


def banded_forward(name, q, k, v, bias, sinks, hb, prev_mask):
    h_n, nb, rows, _ = q.shape

    def body(q_ref, kp_ref, ko_ref, vp_ref, vo_ref, bias_ref, sink_ref, o_ref, lse_ref):
        _, s_prev, s_own = _band_scores(q_ref, kp_ref, ko_ref, bias_ref, hb, prev_mask)
        sink = sink_ref[...]
        m = jnp.maximum(jnp.maximum(jnp.max(s_prev, axis=2, keepdims=True), jnp.max(s_own, axis=2, keepdims=True)), sink)
        p_prev = jnp.exp(s_prev - m)
        p_own = jnp.exp(s_own - m)
        denom = jnp.sum(p_prev, axis=2, keepdims=True) + jnp.sum(p_own, axis=2, keepdims=True) + jnp.exp(sink - m)
        o = _dot(p_prev.astype(BF16), vp_ref[...], BATCH_PV) + _dot(p_own.astype(BF16), vo_ref[...], BATCH_PV)
        o_ref[...] = o / denom
        lse_ref[...] = m + jnp.log(denom)

    q_spec, own, prev, per_head = _band_specs(hb, rows, k.shape[1])
    return pl.pallas_call(
        body, grid=(h_n // hb, nb),
        in_specs=[q_spec(HEAD_DIM), prev, own, prev, own, per_head(rows, 2 * BLK), per_head(rows, 1)],
        out_specs=[q_spec(HEAD_DIM), q_spec(1)],
        out_shape=[jax.ShapeDtypeStruct(q.shape, F32), jax.ShapeDtypeStruct((h_n, nb, rows, 1), F32)],
        name=name, compiler_params=_params(("parallel", "parallel")),
    )(q, k, k, v, v, bias, sinks)


def banded_backward(name, q, k, v, bias, sinks, o, lse, do, dlse, hb, prev_mask):
    h_n, nb, rows, _ = q.shape
    t_n = k.shape[1]

    def body(q_ref, kp_ref, ko_ref, vp_ref, vo_ref, bias_ref, sink_ref, o_ref, lse_ref, do_ref, dlse_ref,
             dq_ref, dk_ref, dv_ref, dbias_ref, dsink_ref):
        b = pl.program_id(1)

        @pl.when(b == 0)
        def _():
            dk_ref[...] = jnp.zeros_like(dk_ref)
            dv_ref[...] = jnp.zeros_like(dv_ref)
            dbias_ref[...] = jnp.zeros_like(dbias_ref)
            dsink_ref[...] = jnp.zeros_like(dsink_ref)

        qs, s_prev, s_own = _band_scores(q_ref, kp_ref, ko_ref, bias_ref, hb, prev_mask)
        lse_v = lse_ref[...]
        dov = do_ref[...]
        dob = dov.astype(BF16)
        shift = dlse_ref[...] - jnp.sum(dov * o_ref[...], axis=2, keepdims=True)
        p_prev = jnp.exp(s_prev - lse_v)
        p_own = jnp.exp(s_own - lse_v)
        ds_prev = p_prev * (_dot(dob, vp_ref[...], BATCH_QK) + shift)
        ds_own = p_own * (_dot(dob, vo_ref[...], BATCH_QK) + shift)
        dbias_ref[:, :, 0:BLK] += ds_prev
        dbias_ref[:, :, BLK:2 * BLK] += ds_own
        d_sink = jnp.exp(sink_ref[...] - lse_v) * shift
        for g in range(rows // BLK):
            dsink_ref[:, g:g + 1, :] += jnp.sum(d_sink[:, g * BLK:(g + 1) * BLK, :], axis=1, keepdims=True)
        ds_prev = ds_prev.astype(BF16)
        ds_own = ds_own.astype(BF16)
        dq_ref[...] = (_dot(ds_prev, kp_ref[...], BATCH_PV) + _dot(ds_own, ko_ref[...], BATCH_PV)) * QK_SCALE
        rows_prev = pl.ds(pl.multiple_of(jnp.maximum(b - 1, 0) * BLK, BLK), BLK)
        rows_own = pl.ds(pl.multiple_of(b * BLK, BLK), BLK)
        dk_ref[:, rows_prev, :] += _dot(ds_prev, qs, BATCH_TN)
        dk_ref[:, rows_own, :] += _dot(ds_own, qs, BATCH_TN)
        dv_ref[:, rows_prev, :] += _dot(p_prev.astype(BF16), dob, BATCH_TN)
        dv_ref[:, rows_own, :] += _dot(p_own.astype(BF16), dob, BATCH_TN)

    q_spec, own, prev, per_head = _band_specs(hb, rows, t_n)
    kv_full = per_head(t_n, HEAD_DIM)
    kv_shape = jax.ShapeDtypeStruct((h_n, t_n, HEAD_DIM), F32)
    return pl.pallas_call(
        body, grid=(h_n // hb, nb),
        in_specs=[q_spec(HEAD_DIM), prev, own, prev, own, per_head(rows, 2 * BLK), per_head(rows, 1),
                  q_spec(HEAD_DIM), q_spec(1), q_spec(HEAD_DIM), q_spec(1)],
        out_specs=[q_spec(HEAD_DIM), kv_full, kv_full, per_head(rows, 2 * BLK), per_head(rows // BLK, BLK)],
        out_shape=[jax.ShapeDtypeStruct(q.shape, F32), kv_shape, kv_shape,
                   jax.ShapeDtypeStruct((h_n, rows, 2 * BLK), F32), jax.ShapeDtypeStruct((h_n, rows // BLK, BLK), F32)],
        name=name, compiler_params=_params(("parallel", "arbitrary")),
    )(q, k, k, v, v, bias, sinks, o, lse, do, dlse)


def _dil_prev_mask(head):
    group = head // H_PER_DIL
    return jnp.where(group == 0, 15, jnp.where(group == 1, 3, 0))


def _swa_prev_mask(head):
    del head
    return 15


DIL_HEADS_PER_STEP = 3
SWA_GROUP = H_SWA_Q // H_SWA_KV
N_BLK = SEQ // BLK


def dilated_merge(o, lse):
    def body(o_ref, l_ref, out_ref):
        lv = l_ref[...]
        m = jnp.max(lv, axis=0, keepdims=True)
        e = jnp.exp(lv - m)
        alpha = e / jnp.sum(e, axis=0, keepdims=True)
        out_ref[...] = jnp.sum(alpha * o_ref[...], axis=0)

    return pl.pallas_call(
        body, grid=(H_PER_DIL, SEQ // ROW_TILE),
        in_specs=[pl.BlockSpec((3, None, ROW_TILE, HEAD_DIM), lambda h, i: (0, h, i, 0)),
                  pl.BlockSpec((3, None, ROW_TILE, 1), lambda h, i: (0, h, i, 0))],
        out_specs=pl.BlockSpec((None, ROW_TILE, HEAD_DIM), lambda h, i: (h, i, 0)),
        out_shape=jax.ShapeDtypeStruct((H_PER_DIL, SEQ, HEAD_DIM), F32),
        name="dilated_merge", compiler_params=_params(("parallel", "parallel")),
    )(o, lse)


def dilated_merge_bwd(o, lse, dout):
    def body(o_ref, l_ref, d_ref, do_ref, dl_ref):
        lv = l_ref[...]
        m = jnp.max(lv, axis=0, keepdims=True)
        e = jnp.exp(lv - m)
        alpha = e / jnp.sum(e, axis=0, keepdims=True)
        dv = d_ref[...][None]
        do_ref[...] = alpha * dv
        dalpha = jnp.sum(dv * o_ref[...], axis=-1, keepdims=True)
        dl_ref[...] = alpha * (dalpha - jnp.sum(alpha * dalpha, axis=0, keepdims=True))

    o_spec = pl.BlockSpec((3, None, ROW_TILE, HEAD_DIM), lambda h, i: (0, h, i, 0))
    l_spec = pl.BlockSpec((3, None, ROW_TILE, 1), lambda h, i: (0, h, i, 0))
    return pl.pallas_call(
        body, grid=(H_PER_DIL, SEQ // ROW_TILE),
        in_specs=[o_spec, l_spec, pl.BlockSpec((None, ROW_TILE, HEAD_DIM), lambda h, i: (h, i, 0))],
        out_specs=[o_spec, l_spec],
        out_shape=[jax.ShapeDtypeStruct(o.shape, F32), jax.ShapeDtypeStruct(lse.shape, F32)],
        name="dilated_merge_bwd", compiler_params=_params(("parallel", "parallel")),
    )(o, lse, dout)


def rel_bias_reduce(dbias0, dbias1, bucket):
    def body(d0_ref, d1_ref, b_ref, o_ref):
        dv, bv = d0_ref[...] + d1_ref[...], b_ref[...]
        lane = lax.broadcasted_iota(jnp.int32, (1, BLK), 1)
        acc = jnp.zeros((1, BLK), F32)
        for bkt in range(N_BUCKETS):
            acc = acc + jnp.where(lane == bkt, jnp.sum(jnp.where(bv == bkt, dv, 0.0)), 0.0)
        o_ref[...] = acc

    tile = pl.BlockSpec((None, BLK, 2 * BLK), lambda h: (h, 0, 0))
    return pl.pallas_call(
        body, grid=(dbias0.shape[0],), in_specs=[tile, tile, tile],
        out_specs=pl.BlockSpec((None, 1, BLK), lambda h: (h, 0, 0)),
        out_shape=jax.ShapeDtypeStruct((dbias0.shape[0], 1, BLK), F32),
        name="rel_bias_reduce", compiler_params=_params(("parallel",)),
    )(dbias0, dbias1, bucket)


def _heads(t):
    return t.reshape(SEQ, -1, HEAD_DIM).transpose(1, 0, 2)


def _unheads(t):
    return t.transpose(1, 0, 2).reshape(SEQ, -1)


def _dilate(t):
    parts = []
    for g, (_, d) in enumerate(DIL_PATTERNS):
        tg = t[:, 128 * g:128 * (g + 1)].reshape(SEQ // d, d, H_PER_DIL, HEAD_DIM).transpose(2, 1, 0, 3)
        parts.append(tg.reshape(H_PER_DIL, SEQ, HEAD_DIM))
    return jnp.concatenate(parts, axis=0)


def _undilate(t):
    outs = []
    for g, (_, d) in enumerate(DIL_PATTERNS):
        tg = t[2 * g:2 * g + 2].reshape(H_PER_DIL, d, SEQ // d, -1).transpose(0, 2, 1, 3)
        outs.append(tg.reshape(H_PER_DIL, SEQ, -1))
    return jnp.stack(outs)


def _redilate(t):
    parts = []
    for g, (_, d) in enumerate(DIL_PATTERNS):
        tg = t[g].reshape(H_PER_DIL, SEQ // d, d, -1).transpose(0, 2, 1, 3)
        parts.append(tg.reshape(H_PER_DIL, SEQ, -1))
    return jnp.concatenate(parts, axis=0)


def _t5_bucket(n):
    max_exact = N_BUCKETS // 2
    nf = jnp.maximum(n, 1).astype(F32)
    large = max_exact + (jnp.log(nf / max_exact) / math.log(MAX_REL_DIST / max_exact)
                         * (N_BUCKETS - max_exact)).astype(jnp.int32)
    large = jnp.minimum(large, N_BUCKETS - 1)
    return jnp.where(n < max_exact, n, large)


def band_tables(rel_bias):
    rel = jnp.arange(BLK)[:, None] + BLK - jnp.arange(2 * BLK)[None, :]
    buckets = []
    patterns = [(d, w // d) for w, d in DIL_PATTERNS for _ in range(H_PER_DIL)] + [(1, SWA_WINDOW - 1)] * H_SWA_Q
    for d, max_dist in patterns:
        band = (rel >= 0) & (rel <= max_dist)
        buckets.append(jnp.where(band, _t5_bucket(jnp.maximum(rel, 0) * d), -1))
    buckets = jnp.stack(buckets).astype(jnp.int32)

    def body(table_ref, b_ref, o_ref):
        h = pl.program_id(0)
        bv = b_ref[...]
        tile = jnp.full(bv.shape, NEG, F32)
        for bkt in range(N_BUCKETS):
            tile = jnp.where(bv == bkt, table_ref[h, bkt], tile)
        o_ref[...] = tile

    spec = pl.BlockSpec((None, BLK, 2 * BLK), lambda h: (h, 0, 0))
    tiles = pl.pallas_call(
        body, grid=(len(patterns),), in_specs=[pl.BlockSpec(memory_space=pltpu.SMEM), spec], out_specs=spec,
        out_shape=jax.ShapeDtypeStruct(buckets.shape, F32), name="band_tables", compiler_params=_params(("parallel",)),
    )(rel_bias.T, buckets)
    return tiles[:H_DIL], tiles[H_DIL:], buckets


def _swa_rows(t):
    t = t.reshape(N_BLK, BLK, H_SWA_KV, SWA_GROUP, HEAD_DIM).transpose(2, 0, 3, 1, 4)
    return t.reshape(H_SWA_KV, N_BLK, SWA_GROUP * BLK, HEAD_DIM)


def _swa_tokens(t):
    t = t.reshape(H_SWA_KV, N_BLK, SWA_GROUP, BLK, HEAD_DIM).transpose(1, 3, 0, 2, 4)
    return t.reshape(SEQ, H_SWA_Q * HEAD_DIM)


def _sink_rows(sinks):
    return jnp.broadcast_to(sinks.reshape(H_SWA_KV, SWA_GROUP, 1, 1), (H_SWA_KV, SWA_GROUP, BLK, 1)).reshape(
        H_SWA_KV, SWA_GROUP * BLK, 1)


def _no_sinks():
    return jnp.full((H_DIL, BLK, 1), NEG, F32)


def _vec(v):
    return v.reshape(1, D_MODEL)


def ffn_forward(lf, x, gain, mod, w):
    h = prenorm(x, _vec(gain), _vec(mod[1]), _vec(mod[0]))
    a, b, s = ffn_up(h, w["gate"], w["up"], lf)
    f, xo = ffn_down(s, w["down"], lf, x, _vec(mod[2]))
    return xo, (x, h, a, b, s, f)


def ffn_backward(lf, dxo, saved, gain, mod, w, g):
    x, h, a, b, s, f = saved
    df, dgate = resid_bwd(dxo, f, _vec(mod[2]), 0.5)
    da, db = ffn_bwd_hidden(df, w["down"], lf, a, b)
    g["gate"], g["up"], g["down"] = ffn_grad_weights(h, s, df, da, db, g["gate"], g["up"], g["down"], lf)
    dx, stats = ffn_bwd_input(da, db, w["gate"], w["up"], lf, x, dxo, _vec(gain), _vec(mod[1]))
    return dx, jnp.stack([stats[0], stats[1], dgate[0]]), stats[2]


def mixer_forward(l, x, gain, mod, sinks, bias_dil, bias_swa, w):
    h = prenorm(x, _vec(gain), _vec(mod[1]), _vec(mod[0]))
    proj = in_proj(h, w["in"], l)
    qkv = proj[:, :D_QKV].astype(BF16)
    q_sb, k_sb, v_sb = _heads(qkv[:, 0:256] * QK_SCALE), _heads(qkv[:, 256:512]), _heads(qkv[:, 512:768])
    q_dil, k_dil, v_dil = _dilate(qkv[:, 768:1152] * QK_SCALE), _dilate(qkv[:, 1152:1536]), _dilate(qkv[:, 1536:1920])
    q_swa, k_swa, v_swa = _swa_rows(qkv[:, 1920:2304] * QK_SCALE), _heads(qkv[:, 2304:2432]), _heads(qkv[:, 2432:2560])
    o_sb, total_sb = sb_forward(q_sb, k_sb, v_sb)
    q_dil = q_dil.reshape(H_DIL, N_BLK, BLK, HEAD_DIM)
    o_dd, lse_dd = banded_forward("dilated_forward", q_dil, k_dil, v_dil, bias_dil, _no_sinks(), DIL_HEADS_PER_STEP,
                                  _dil_prev_mask)
    o_dt, lse_dt = _undilate(o_dd.reshape(H_DIL, SEQ, HEAD_DIM)), _undilate(lse_dd.reshape(H_DIL, SEQ, 1))
    o_dil = dilated_merge(o_dt, lse_dt)
    bias_swa = bias_swa.reshape(H_SWA_KV, SWA_GROUP * BLK, 2 * BLK)
    o_swa, lse_swa = banded_forward("swa_forward", q_swa, k_swa, v_swa, bias_swa, _sink_rows(sinks), 1, _swa_prev_mask)
    o_cat = jnp.concatenate([_unheads(o_sb), _unheads(o_dil), _swa_tokens(o_swa)], axis=1)
    merged = merge_branches(o_cat, w["br"], l, proj)
    mo, xo = out_proj(merged, w["out"], l, x, _vec(mod[2]))
    saved = (x, h, proj, (q_sb, k_sb, v_sb, total_sb), (q_dil, k_dil, v_dil, o_dd, lse_dd, o_dt, lse_dt),
             (q_swa, k_swa, v_swa, o_swa, lse_swa), o_cat, merged, mo)
    return xo, saved


def mixer_backward(l, dxo, saved, gain, mod, sinks, bias_dil, bias_swa, w, g):
    x, h, proj, sb, dil, swa, o_cat, merged, mo = saved
    dmo, dgate = resid_bwd(dxo, mo, _vec(mod[2]), 1.0)
    tok = pl.BlockSpec((MM_TILE, D_MODEL), lambda j, k: (k, 0))
    g["out"] = grad_weight("grad_w_out", merged, pl.BlockSpec((MM_TILE, D_SHARD), lambda j, k: (k, j)), dmo, tok,
                           g["out"], (None, None, D_SHARD, D_MODEL), lambda j, k: (j, l, 0, 0), (D_SHARD, D_MODEL))
    du0, du1, du2, dg0, dg1, dg2 = merge_bwd(dmo, w["out"], o_cat, w["br"], l, proj)
    du = (du0, du1, du2)
    do_cat = branch_bwd_input(du, w["br"], l)
    g["br"] = branch_grad_weights(o_cat, du, g["br"], l)

    q_sb, k_sb, v_sb, total_sb = sb
    dq_sb, dk_sb, dv_sb = sb_backward(q_sb, k_sb, v_sb, total_sb, _heads(do_cat[:, 0:256]))

    q_dil, k_dil, v_dil, o_dd, lse_dd, o_dt, lse_dt = dil
    do_dt, dlse_dt = dilated_merge_bwd(o_dt, lse_dt, _heads(do_cat[:, 256:384]))
    dq_dil, dk_dil, dv_dil, dbias_dil, _ = banded_backward(
        "dilated_backward", q_dil, k_dil, v_dil, bias_dil, _no_sinks(), o_dd, lse_dd,
        _redilate(do_dt).reshape(q_dil.shape), _redilate(dlse_dt).reshape(lse_dd.shape), DIL_HEADS_PER_STEP, _dil_prev_mask)

    q_swa, k_swa, v_swa, o_swa, lse_swa = swa
    bias_swa = bias_swa.reshape(H_SWA_KV, SWA_GROUP * BLK, 2 * BLK)
    dq_swa, dk_swa, dv_swa, dbias_swa, dsinks = banded_backward(
        "swa_backward", q_swa, k_swa, v_swa, bias_swa, _sink_rows(sinks), o_swa, lse_swa, _swa_rows(do_cat[:, 384:768]),
        jnp.zeros_like(lse_swa), 1, _swa_prev_mask)
    dbias_swa = dbias_swa.reshape(H_SWA_Q, BLK, 2 * BLK)

    def tokens(t):
        return _undilate(t).transpose(2, 0, 1, 3).reshape(SEQ, -1)

    dproj = jnp.concatenate(
        [_unheads(dq_sb), _unheads(dk_sb), _unheads(dv_sb), tokens(dq_dil.reshape(H_DIL, SEQ, HEAD_DIM)), tokens(dk_dil),
         tokens(dv_dil), _swa_tokens(dq_swa), _unheads(dk_swa), _unheads(dv_swa)], axis=1).astype(BF16)
    dproj = jnp.concatenate([dproj, dg0, dg1, dg2], axis=1)
    g["in"] = grad_weight("grad_w_in", h, tok, dproj, pl.BlockSpec((MM_TILE, IN_SHARD), lambda j, k: (k, j)),
                          g["in"], (None, None, D_MODEL, IN_SHARD), lambda j, k: (j, l, 0, 0), (D_MODEL, IN_SHARD))
    dx, stats = mixer_bwd_input(dproj, w["in"], l, x, dxo, _vec(gain), _vec(mod[1]))
    dmod = jnp.stack([stats[0], stats[1], dgate[0]])
    return dx, dmod, stats[2], jnp.concatenate([dbias_dil, dbias_swa], axis=0), dsinks[:, :, 0].reshape(H_SWA_Q)


GRAD_SHAPES = {
    "gate": (N_CHIPS, 2 * DEPTH, D_MODEL, FF_SHARD), "up": (N_CHIPS, 2 * DEPTH, D_MODEL, FF_SHARD),
    "down": (N_CHIPS, 2 * DEPTH, FF_SHARD, D_MODEL), "in": (N_CHIPS, DEPTH, D_MODEL, IN_SHARD),
    "br": (N_CHIPS, DEPTH, BR_ROWS, D_SHARD), "out": (N_CHIPS, DEPTH, D_SHARD, D_MODEL),
}


def device_step(x, target, mod, gains, final_gain, sinks, rel_bias, w):
    bias_dil, bias_swa, bucket = band_tables(rel_bias)
    saved = []
    for l in range(DEPTH):
        x, s0 = ffn_forward(2 * l, x, gains[l, 0], mod[l, 0], w)
        x, s1 = mixer_forward(l, x, gains[l, 1], mod[l, 1], sinks[l], bias_dil, bias_swa, w)
        x, s2 = ffn_forward(2 * l + 1, x, gains[l, 2], mod[l, 2], w)
        saved.append((s0, s1, s2))
    loss, dx, dfinal = final_loss(x, _vec(final_gain), target)

    g = {k: lax.empty(shape, BF16) for k, shape in GRAD_SHAPES.items()}
    dmod, dgains, dbias, dsinks = [None] * DEPTH, [None] * DEPTH, [None] * DEPTH, [None] * DEPTH
    for l in reversed(range(DEPTH)):
        s0, s1, s2 = saved[l]
        dx, dm2, dg2 = ffn_backward(2 * l + 1, dx, s2, gains[l, 2], mod[l, 2], w, g)
        dx, dm1, dg1, dbias[l], dsinks[l] = mixer_backward(l, dx, s1, gains[l, 1], mod[l, 1], sinks[l], bias_dil,
                                                           bias_swa, w, g)
        dx, dm0, dg0 = ffn_backward(2 * l, dx, s0, gains[l, 0], mod[l, 0], w, g)
        dmod[l] = jnp.stack([dm0, dm1, dm2])
        dgains[l] = jnp.stack([dg0, dg1, dg2])
    drel = rel_bias_reduce(dbias[0], dbias[1], bucket)[:, 0, :N_BUCKETS].T
    return loss, dx, g, jnp.stack(dmod), jnp.stack(dgains), dfinal[0], jnp.stack(dsinks), drel


MESH = pl.DeviceIdType.MESH
CHIP_FLIPS = ((1, 0), (0, 1), (1, 1))
ANY = pl.BlockSpec(memory_space=pl.ANY)


def _position():
    return lax.axis_index("x"), lax.axis_index("y"), lax.axis_index("c")


def all_gather_small(name, piece):
    def body(x_ref, out_ref, send_sems, recv_sems, local_sem):
        x, y, c = _position()
        me, sibling = (x, y, c), (x, y, 1 - c)
        chips = [(x ^ fx, y ^ fy) for fx, fy in CHIP_FLIPS]

        def rows(px, py, pc):
            return out_ref.at[4 * px + 2 * py + pc]

        def copy(k, block, to, src=None):
            return pltpu.make_async_remote_copy(
                src_ref=rows(*block) if src is None else src, dst_ref=rows(*block),
                send_sem=send_sems.at[k], recv_sem=recv_sems.at[k], device_id=to, device_id_type=MESH)

        mine = pltpu.make_async_copy(x_ref, rows(*me), local_sem)
        mine.start()
        first = [copy(0, me, sibling, src=x_ref)]
        first += [copy(1 + j, me, (*chip, c), src=x_ref) for j, chip in enumerate(chips)]
        for cp in first:
            cp.start()
        passed = [copy(4 + j, (*chip, c), sibling) for j, chip in enumerate(chips)]
        for j, chip in enumerate(chips):
            copy(1 + j, (*chip, c), me).wait_recv()
            passed[j].start()
        copy(0, sibling, me).wait_recv()
        for j, chip in enumerate(chips):
            copy(4 + j, (*chip, 1 - c), me).wait_recv()
        for cp in first + passed:
            cp.wait_send()
        mine.wait()

    return pl.pallas_call(
        body, out_shape=jax.ShapeDtypeStruct((N_DEV,) + piece.shape, piece.dtype),
        in_specs=[pl.BlockSpec(memory_space=pltpu.VMEM)], out_specs=pl.BlockSpec(memory_space=pltpu.VMEM),
        scratch_shapes=[pltpu.SemaphoreType.DMA((7,)), pltpu.SemaphoreType.DMA((7,)), pltpu.SemaphoreType.DMA],
        name=name,
    )(piece)


def exchange(name, operands, out_shapes, aliases, plan):
    n_in, n_out = len(operands), len(out_shapes)

    def body(*refs):
        ins, outs = refs[:n_in], refs[n_in:n_in + n_out]
        send_sems, recv_sems, local_sems = refs[n_in + n_out:]
        x, y, c = _position()
        local, sends, recvs = plan(ins, outs, x, y, c)
        local = [pltpu.make_async_copy(s, d, local_sems.at[k]) for k, (s, d) in enumerate(local)]
        for cp in local:
            cp.start()
        remote = [pltpu.make_async_remote_copy(src_ref=s, dst_ref=d, send_sem=send_sems.at[k], recv_sem=recv_sems.at[k],
                                               device_id=dev, device_id_type=MESH)
                  for k, (s, d, dev) in enumerate(sends)]
        for cp in remote:
            cp.start()
        for k, r in enumerate(recvs):
            pltpu.make_async_remote_copy(src_ref=r, dst_ref=r, send_sem=send_sems.at[k], recv_sem=recv_sems.at[k],
                                         device_id=(x, y, c), device_id_type=MESH).wait_recv()
        for cp in remote:
            cp.wait_send()
        for cp in local:
            cp.wait()

    n_sends, n_local = plan.n_sends, max(plan.n_local, 1)
    return pl.pallas_call(
        body, out_shape=out_shapes, in_specs=[ANY] * n_in, out_specs=[ANY] * n_out,
        scratch_shapes=[pltpu.SemaphoreType.DMA((n_sends,)), pltpu.SemaphoreType.DMA((n_sends,)),
                        pltpu.SemaphoreType.DMA((n_local,))],
        input_output_aliases=aliases, name=name,
    )(*operands)


def _plan(n_local, n_sends):
    def wrap(fn):
        fn.n_local, fn.n_sends = n_local, n_sends
        return fn
    return wrap


def _half(ref, axis, c):
    rows = ref.shape[axis] // 2
    idx = [slice(None)] * len(ref.shape)
    idx[axis] = pl.ds(pl.multiple_of(c * rows, 16), rows)
    return ref.at[tuple(idx)]


def cast_into_slot(name, shard, chip):
    n_l, rows, cols = shard.shape
    tr = _row_tile(rows, cols)

    def body(chip_ref, s_ref, o_ref):
        del chip_ref
        o_ref[...] = s_ref[...].astype(BF16)

    return pl.pallas_call(
        body, out_shape=jax.ShapeDtypeStruct((N_CHIPS,) + shard.shape, BF16),
        grid_spec=pltpu.PrefetchScalarGridSpec(
            num_scalar_prefetch=1, grid=(n_l, rows // tr),
            in_specs=[pl.BlockSpec((None, tr, cols), lambda l, r, chip_ref: (l, r, 0))],
            out_specs=pl.BlockSpec((None, None, tr, cols), lambda l, r, chip_ref: (chip_ref[0], l, r, 0))),
        name=name, compiler_params=_params(("parallel", "parallel")),
    )(chip, shard)


def gather_weights(names, shards, chip):
    n = len(shards)
    slots = [cast_into_slot("cast_" + k, s, chip) for k, s in zip(names, shards)]

    @_plan(0, 3 * n)
    def over_chips(ins, outs, x, y, c):
        me = 2 * x + y
        sends, recvs = [], []
        for o in outs:
            for fx, fy in CHIP_FLIPS:
                sends.append((_half(o.at[me], 1, c), _half(o.at[me], 1, c), (x ^ fx, y ^ fy, c)))
                recvs.append(_half(o.at[2 * (x ^ fx) + (y ^ fy)], 1, c))
        return [], sends, recvs

    @_plan(0, 3 * n)
    def to_sibling(ins, outs, x, y, c):
        sends, recvs = [], []
        for o in outs:
            for fx, fy in CHIP_FLIPS:
                slab = o.at[2 * (x ^ fx) + (y ^ fy)]
                sends.append((_half(slab, 1, c), _half(slab, 1, c), (x, y, 1 - c)))
                recvs.append(_half(slab, 1, 1 - c))
        return [], sends, recvs

    shapes = [jax.ShapeDtypeStruct(s.shape, BF16) for s in slots]
    through = {k: k for k in range(n)}
    halves = exchange("gather_weights_chips", slots, shapes, through, over_chips)
    return exchange("gather_weights_sibling", halves, shapes, through, to_sibling)


def reduce_gradients(grads, xi, yi, ci):
    n = len(grads)
    chip = (2 * xi + yi).astype(jnp.int32).reshape(1)
    core = ci.astype(jnp.int32).reshape(1)

    @_plan(0, n)
    def swap_halves(ins, outs, x, y, c):
        sends = [(_half(g, 2, 1 - c), o, (x, y, 1 - c)) for g, o in zip(ins, outs)]
        return [], sends, list(outs)

    half_shapes = [jax.ShapeDtypeStruct(g.shape[:2] + (g.shape[2] // 2, g.shape[3]), BF16) for g in grads]
    landed = exchange("reduce_swap_halves", grads, half_shapes, {}, swap_halves)
    chip_sums = [_add_halves(g, la, core) for g, la in zip(grads, landed)]

    @_plan(0, 3 * n)
    def over_chips(ins, outs, x, y, c):
        sends, recvs = [], []
        for s, o in zip(ins, outs):
            for k, (fx, fy) in enumerate(CHIP_FLIPS):
                sends.append((s.at[2 * (x ^ fx) + (y ^ fy)], o.at[k], (x ^ fx, y ^ fy, c)))
                recvs.append(o.at[k])
        return [], sends, recvs

    land_shapes = [jax.ShapeDtypeStruct((3,) + s.shape[1:], BF16) for s in chip_sums]
    from_chips = exchange("reduce_over_chips", chip_sums, land_shapes, {}, over_chips)
    halves = [_add_chips(s, la, jnp.concatenate([chip, core])) for s, la in zip(chip_sums, from_chips)]

    @_plan(0, n)
    def share_halves(ins, outs, x, y, c):
        sends = [(_half(o, 1, c), _half(o, 1, c), (x, y, 1 - c)) for o in outs]
        return [], sends, [_half(o, 1, 1 - c) for o in outs]

    full_shapes = [jax.ShapeDtypeStruct(h.shape, F32) for h in halves]
    return exchange("reduce_share_halves", halves, full_shapes, {k: k for k in range(n)}, share_halves)


def _row_tile(rows, cols):
    best = 16
    for t in range(16, rows + 1, 16):
        if rows % t == 0 and t * cols <= 256 * 1024:
            best = t
    return best


def _add_halves(g, landed, core):
    _, n_l, rh, cols = landed.shape
    tr = _row_tile(rh, cols)
    per_half = rh // tr

    def body(core_ref, g_ref, la_ref, o_ref):
        del core_ref
        o_ref[...] = (g_ref[...].astype(F32) + la_ref[...].astype(F32)).astype(BF16)

    blk = (None, None, tr, cols)
    return pl.pallas_call(
        body, out_shape=jax.ShapeDtypeStruct(landed.shape, BF16),
        grid_spec=pltpu.PrefetchScalarGridSpec(
            num_scalar_prefetch=1, grid=(N_CHIPS, n_l, per_half),
            in_specs=[pl.BlockSpec(blk, lambda j, l, r, core_ref: (j, l, core_ref[0] * per_half + r, 0)),
                      pl.BlockSpec(blk, lambda j, l, r, core_ref: (j, l, r, 0))],
            out_specs=pl.BlockSpec(blk, lambda j, l, r, core_ref: (j, l, r, 0))),
        name="reduce_add_halves", compiler_params=_params(("parallel", "parallel", "parallel")),
    )(core, g, landed)


def _add_chips(sums, landed, place):
    _, n_l, rh, cols = sums.shape
    tr = _row_tile(rh, cols)
    per_half = rh // tr

    def body(place_ref, s_ref, la_ref, o_ref):
        del place_ref
        o_ref[...] = ((s_ref[...].astype(F32) + la_ref[0].astype(F32)) + la_ref[1].astype(F32)) + la_ref[2].astype(F32)

    return pl.pallas_call(
        body, out_shape=jax.ShapeDtypeStruct((n_l, 2 * rh, cols), F32),
        grid_spec=pltpu.PrefetchScalarGridSpec(
            num_scalar_prefetch=1, grid=(n_l, per_half),
            in_specs=[pl.BlockSpec((None, None, tr, cols), lambda l, r, place_ref: (place_ref[0], l, r, 0)),
                      pl.BlockSpec((3, None, tr, cols), lambda l, r, place_ref: (0, l, r, 0))],
            out_specs=pl.BlockSpec((None, tr, cols), lambda l, r, place_ref: (l, place_ref[1] * per_half + r, 0))),
        name="reduce_add_chips", compiler_params=_params(("parallel", "parallel")),
    )(place, sums, landed)


def sum_devices(parts):
    def body(p_ref, o_ref):
        acc = p_ref[0]
        for d in range(1, N_DEV):
            acc = acc + p_ref[d]
        o_ref[...] = acc

    return pl.pallas_call(body, out_shape=jax.ShapeDtypeStruct(parts.shape[1:], F32), name="sum_devices")(parts)


ADA_SHARD = 9 * D_MODEL // N_CHIPS
ADA_TILE = 768
ADA_ROWS = 16


def ada_forward(c_rows, w_ada, b_shard):
    def body(c_ref, w_ref, b_ref, o_ref):
        cv = c_ref[...]
        o_ref[...] = _dot((cv * _sigmoid(cv)).astype(BF16), w_ref[...].astype(BF16), NN) + b_ref[...]

    return pl.pallas_call(
        body, grid=(DEPTH, ADA_SHARD // ADA_TILE),
        in_specs=[pl.BlockSpec((ADA_ROWS, D_MODEL), lambda l, n: (0, 0)),
                  pl.BlockSpec((None, D_MODEL, ADA_TILE), lambda l, n: (l, 0, n)),
                  pl.BlockSpec((None, 1, ADA_TILE), lambda l, n: (l, 0, n))],
        out_specs=pl.BlockSpec((None, ADA_ROWS, ADA_TILE), lambda l, n: (l, 0, n)),
        out_shape=jax.ShapeDtypeStruct((DEPTH, ADA_ROWS, ADA_SHARD), F32),
        name="ada_forward", compiler_params=_params(("parallel", "parallel")),
    )(c_rows, w_ada, b_shard)


def ada_backward(c_rows, dmod_rows):
    def body(c_ref, d_ref, o_ref):
        cv = c_ref[...]
        o_ref[...] = _dot((cv * _sigmoid(cv)).astype(BF16), d_ref[...].astype(BF16), TN)

    return pl.pallas_call(
        body, grid=(DEPTH, ADA_SHARD // ADA_TILE),
        in_specs=[pl.BlockSpec((ADA_ROWS, D_MODEL), lambda l, n: (0, 0)),
                  pl.BlockSpec((None, ADA_ROWS, ADA_TILE), lambda l, n: (l, 0, n))],
        out_specs=pl.BlockSpec((None, D_MODEL, ADA_TILE), lambda l, n: (l, 0, n)),
        out_shape=jax.ShapeDtypeStruct((DEPTH, D_MODEL, ADA_SHARD), F32),
        name="ada_backward", compiler_params=_params(("parallel", "parallel")),
    )(c_rows, dmod_rows)


def adamw(name, w, g, m, v):
    shape = w.shape
    cols = shape[-1]
    rows = w.size // cols
    tr = _row_tile(rows, cols) if rows % 16 == 0 else rows
    c1 = 1.0 / (1.0 - ADAM_B1 ** ADAM_STEP)
    c2 = 1.0 / (1.0 - ADAM_B2 ** ADAM_STEP)

    def body(w_ref, g_ref, m_ref, v_ref, d_ref, mo_ref, vo_ref):
        gv = g_ref[...]
        mn = ADAM_B1 * m_ref[...] + (1.0 - ADAM_B1) * gv
        vn = ADAM_B2 * v_ref[...] + (1.0 - ADAM_B2) * (gv * gv)
        mo_ref[...] = mn
        vo_ref[...] = vn
        d_ref[...] = -ADAM_LR * ((mn * c1) / (jnp.sqrt(vn * c2) + ADAM_EPS) + ADAM_WD * w_ref[...])

    spec = pl.BlockSpec((tr, cols), lambda i: (i, 0))
    out = jax.ShapeDtypeStruct((rows, cols), F32)
    res = pl.pallas_call(
        body, grid=(rows // tr,), in_specs=[spec] * 4, out_specs=[spec] * 3, out_shape=[out] * 3,
        name=name, compiler_params=_params(("parallel",)),
    )(*[t.reshape(rows, cols) for t in (w, g, m, v)])
    return tuple(r.reshape(shape) for r in res)


def _pack(parts, rows):
    flat = jnp.concatenate([p.reshape(-1) for p in parts])
    return jnp.pad(flat, (0, rows * 128 - flat.size)).reshape(rows, 128)


def _unpack(flat, shapes):
    out, at = [], 0
    for s in shapes:
        n = math.prod(s)
        out.append(flat[at:at + n].reshape(s))
        at += n
    return out


def kernel(x, c, w_ada, b_ada, norm_gain, w_ffn_gate, w_ffn_up, w_ffn_down, w_in, w_br_sb, w_br_dil, w_br_swa, w_out, sinks, rel_bias, final_gain, loss_target, m_w_ada, m_b_ada, m_norm_gain, m_w_ffn_gate, m_w_ffn_up, m_w_ffn_down, m_w_in, m_w_br_sb, m_w_br_dil, m_w_br_swa, m_w_out, m_sinks, m_rel_bias, m_final_gain, v_w_ada, v_b_ada, v_norm_gain, v_w_ffn_gate, v_w_ffn_up, v_w_ffn_down, v_w_in, v_w_br_sb, v_w_br_dil, v_w_br_swa, v_w_out, v_sinks, v_rel_bias, v_final_gain):
    xi, yi, ci = _position()
    chip = 2 * xi + yi
    dev = 2 * chip + ci

    c_all = all_gather_small("gather_c", c.reshape(8, 128)).reshape(N_DEV, D_MODEL)
    c_rows = jnp.pad(c_all, ((0, ADA_ROWS - N_DEV), (0, 0)))
    b_shard = lax.dynamic_slice_in_dim(b_ada, chip * ADA_SHARD, ADA_SHARD, axis=1).reshape(DEPTH, 1, ADA_SHARD)
    mod_shard = ada_forward(c_rows, w_ada, b_shard)[:, :N_DEV]
    n_mod = DEPTH * N_DEV * ADA_SHARD
    gathered = all_gather_small("gather_mod", _pack([mod_shard, norm_gain], 304))[::2].reshape(N_CHIPS, -1)
    mod_all = gathered[:, :n_mod].reshape(N_CHIPS, DEPTH, N_DEV, ADA_SHARD)
    mod = lax.dynamic_index_in_dim(mod_all, dev, axis=2, keepdims=False)
    mod = mod.transpose(1, 0, 2).reshape(DEPTH, 3, 3, D_MODEL)
    gains = gathered[:, n_mod:n_mod + DEPTH * 3 * D_SHARD].reshape(N_CHIPS, DEPTH, 3, D_SHARD)
    gains = gains.transpose(1, 2, 0, 3).reshape(DEPTH, 3, D_MODEL)

    names = ("gate", "up", "down", "in", "br", "out")
    shards = [
        w_ffn_gate.reshape(2 * DEPTH, D_MODEL, FF_SHARD), w_ffn_up.reshape(2 * DEPTH, D_MODEL, FF_SHARD),
        w_ffn_down.reshape(2 * DEPTH, FF_SHARD, D_MODEL), w_in,
        jnp.concatenate([w_br_sb, w_br_dil, w_br_swa], axis=1), w_out]
    w = dict(zip(names, gather_weights(names, shards, chip.astype(jnp.int32).reshape(1))))

    loss, dx, g, dmod, dgains, dfinal, dsinks, drel = device_step(
        x[0], loss_target[0], mod, gains, final_gain, sinks, rel_bias, w)

    small_shapes = [(DEPTH, 9 * D_MODEL), (DEPTH, 3, D_MODEL), (D_MODEL,), (DEPTH, H_SWA_Q), (N_BUCKETS, 12), (1,)]
    small_all = all_gather_small("gather_small_grads", _pack([dmod, dgains, dfinal, dsinks, drel, loss[0, 0:1]], 208))
    g_b_ada, g_gain_full, g_final, g_sinks, g_rel, loss_sum = _unpack(sum_devices(small_all).reshape(-1), small_shapes)
    g_gain = lax.dynamic_slice_in_dim(g_gain_full, chip * D_SHARD, D_SHARD, axis=2)
    dmod_all = small_all.reshape(N_DEV, -1)[:, :DEPTH * 9 * D_MODEL].reshape(N_DEV, DEPTH, 9 * D_MODEL)
    dmod_rows = lax.dynamic_slice_in_dim(dmod_all, chip * ADA_SHARD, ADA_SHARD, axis=2).transpose(1, 0, 2)
    g_w_ada = ada_backward(c_rows, jnp.pad(dmod_rows, ((0, 0), (0, ADA_ROWS - N_DEV), (0, 0))))

    g_gate, g_up, g_down, g_in, g_br, g_out = reduce_gradients([g[k] for k in names], xi, yi, ci)
    grads = [
        g_w_ada, g_b_ada, g_gain, g_gate.reshape(w_ffn_gate.shape), g_up.reshape(w_ffn_up.shape),
        g_down.reshape(w_ffn_down.shape), g_in, g_br[:, 0:256], g_br[:, 256:384], g_br[:, 384:768], g_out,
        g_sinks, g_rel, g_final]
    weights = [w_ada, b_ada, norm_gain, w_ffn_gate, w_ffn_up, w_ffn_down, w_in, w_br_sb, w_br_dil, w_br_swa, w_out,
               sinks, rel_bias, final_gain]
    ms = [m_w_ada, m_b_ada, m_norm_gain, m_w_ffn_gate, m_w_ffn_up, m_w_ffn_down, m_w_in, m_w_br_sb, m_w_br_dil,
          m_w_br_swa, m_w_out, m_sinks, m_rel_bias, m_final_gain]
    vs = [v_w_ada, v_b_ada, v_norm_gain, v_w_ffn_gate, v_w_ffn_up, v_w_ffn_down, v_w_in, v_w_br_sb, v_w_br_dil,
          v_w_br_swa, v_w_out, v_sinks, v_rel_bias, v_final_gain]

    small = (1, 2, 11, 12, 13)
    deltas, new_ms, new_vs = [None] * 14, [None] * 14, [None] * 14
    for k in range(14):
        if k not in small:
            deltas[k], new_ms[k], new_vs[k] = adamw(f"adamw_{k}", weights[k], grads[k], ms[k], vs[k])
    shapes = [weights[k].shape for k in small]
    packed = [_pack([t[k] for k in small], 168) for t in (weights, grads, ms, vs)]
    for dst, res in zip((deltas, new_ms, new_vs), adamw("adamw_small", *packed)):
        for k, t in zip(small, _unpack(res.reshape(-1), shapes)):
            dst[k] = t
    return (loss_sum[0], dx[None], *grads, *deltas, *new_ms, *new_vs)
```

```python
import functools
import math

import jax
import jax.numpy as jnp
from jax import lax
from jax.experimental import pallas as pl
from jax.experimental.pallas import tpu as pltpu

F32 = jnp.float32
BF16 = jnp.bfloat16

D_MODEL = 1024
SEQ = 2048
DEPTH = 2
HEAD_DIM = 64
BLK = 128
H_SB = 4
DIL_PATTERNS = ((128, 1), (512, 4), (2048, 16))
H_PER_DIL = 2
H_DIL = 6
H_SWA_Q = 6
H_SWA_KV = 2
SWA_WINDOW = 128
N_BUCKETS = 32
MAX_REL_DIST = 2048
D_FF = 2816
RMS_EPS = 1e-6
N_CHIPS = 4
N_DEV = 8
FF_SHARD = D_FF // N_CHIPS
D_QKV = 2560
D_IN = D_QKV + 3 * D_MODEL
IN_SHARD = D_IN // N_CHIPS
D_SHARD = D_MODEL // N_CHIPS
BR_ROWS = 768
NEG = -1e30
QK_SCALE = HEAD_DIM ** -0.5

ADAM_LR = 0.001
ADAM_B1 = 0.9
ADAM_B2 = 0.999
ADAM_EPS = 1e-08
ADAM_WD = 0.01
ADAM_STEP = 10

VMEM_LIMIT = 48 * 1024 * 1024
ROW_TILE = 256
MM_TILE = 512

NN = (((1,), (0,)), ((), ()))
NT = (((1,), (1,)), ((), ()))
TN = (((0,), (0,)), ((), ()))


def _params(sem=None):
    return pltpu.CompilerParams(dimension_semantics=sem, vmem_limit_bytes=VMEM_LIMIT)


def _dot(a, b, dims):
    return lax.dot_general(a, b, dims, preferred_element_type=F32)


def _sigmoid(x):
    return 1.0 / (1.0 + jnp.exp(-x))


def _matmul(name, grid, nk, k_axis, dims, n_pairs, in_specs, out_specs, out_shape, acc_shape, epilogue,
            operands, sem, aliases=None, prologue=None):
    n_in = len(in_specs)
    n_out = len(out_specs)

    def partial(ins):
        tot = None
        for p in range(n_pairs):
            a = ins[2 * p][...]
            if prologue is not None:
                a = prologue(p, a, ins)
            d = _dot(a, ins[2 * p + 1][...], dims)
            tot = d if tot is None else tot + d
        return tot

    def body(*refs):
        ins, outs = refs[:n_in], refs[n_in:n_in + n_out]
        ids = tuple(pl.program_id(a) for a in range(len(grid)))
        if nk == 1:
            epilogue(partial(ins), ins, outs, ids)
            return
        acc = refs[n_in + n_out]
        k = ids[k_axis]

        @pl.when(k == 0)
        def _():
            acc[...] = partial(ins)

        @pl.when(k > 0)
        def _():
            acc[...] += partial(ins)

        @pl.when(k == nk - 1)
        def _():
            epilogue(acc[...], ins, outs, ids)

    return pl.pallas_call(
        body, grid=grid, in_specs=in_specs, out_specs=out_specs, out_shape=out_shape,
        scratch_shapes=[] if nk == 1 else [pltpu.VMEM(acc_shape, F32)],
        input_output_aliases=aliases or {}, name=name, compiler_params=_params(sem),
    )(*operands)


def _row_spec(width=D_MODEL):
    return pl.BlockSpec((ROW_TILE, width), lambda i: (i, 0))


def _vec_spec(rows=1, width=D_MODEL):
    return pl.BlockSpec((rows, width), lambda i: (0, 0))


def prenorm(x, gain, scale, shift):
    def body(x_ref, g_ref, sc_ref, sh_ref, h_ref):
        xv = x_ref[...]
        r = lax.rsqrt(jnp.mean(xv * xv, axis=-1, keepdims=True) + RMS_EPS)
        h_ref[...] = (((xv * r) * g_ref[...]) * (1.0 + sc_ref[...]) + sh_ref[...]).astype(BF16)

    return pl.pallas_call(
        body, grid=(SEQ // ROW_TILE,), in_specs=[_row_spec(), _vec_spec(), _vec_spec(), _vec_spec()],
        out_specs=_row_spec(), out_shape=jax.ShapeDtypeStruct((SEQ, D_MODEL), BF16),
        name="prenorm", compiler_params=_params(("parallel",)),
    )(x, gain, scale, shift)


def resid_bwd(dxo, f, coef, mult):
    def body(dx_ref, f_ref, c_ref, df_ref, dc_ref):
        dx = dx_ref[...]
        df_ref[...] = (dx * (mult * c_ref[...])).astype(BF16)
        part = mult * jnp.sum(dx * f_ref[...], axis=0, keepdims=True)

        @pl.when(pl.program_id(0) == 0)
        def _():
            dc_ref[...] = jnp.zeros_like(dc_ref)

        dc_ref[0:1, :] += part

    return pl.pallas_call(
        body, grid=(SEQ // ROW_TILE,), in_specs=[_row_spec(), _row_spec(), _vec_spec()],
        out_specs=[_row_spec(), _vec_spec(8)],
        out_shape=[jax.ShapeDtypeStruct((SEQ, D_MODEL), BF16), jax.ShapeDtypeStruct((8, D_MODEL), F32)],
        name="resid_bwd", compiler_params=_params(("arbitrary",)),
    )(dxo, f, coef)


def final_loss(x, gain, target):
    def body(x_ref, g_ref, t_ref, loss_ref, dx_ref, dg_ref):
        xv = x_ref[...]
        g = g_ref[...]
        r = lax.rsqrt(jnp.mean(xv * xv, axis=-1, keepdims=True) + RMS_EPS)
        xh = xv * r
        e = xh * g - t_ref[...]
        part = 0.5 * jnp.sum(jnp.mean(e * e, axis=-1, keepdims=True), axis=0, keepdims=True)
        dy = e * (1.0 / D_MODEL)
        dyg = dy * g
        dx_ref[...] = r * (dyg - xh * jnp.mean(dyg * xh, axis=-1, keepdims=True))

        @pl.when(pl.program_id(0) == 0)
        def _():
            loss_ref[...] = jnp.zeros_like(loss_ref)
            dg_ref[...] = jnp.zeros_like(dg_ref)

        loss_ref[...] += jnp.broadcast_to(part, loss_ref.shape)
        dg_ref[0:1, :] += jnp.sum(dy * xh, axis=0, keepdims=True)

    return pl.pallas_call(
        body, grid=(SEQ // ROW_TILE,), in_specs=[_row_spec(), _vec_spec(), _row_spec()],
        out_specs=[_vec_spec(8, 128), _row_spec(), _vec_spec(8)],
        out_shape=[jax.ShapeDtypeStruct((8, 128), F32), jax.ShapeDtypeStruct((SEQ, D_MODEL), F32),
                   jax.ShapeDtypeStruct((8, D_MODEL), F32)],
        name="final_loss", compiler_params=_params(("arbitrary",)),
    )(x, gain, target)


def _prenorm_bwd_epilogue(dh, x_ref, dxo_ref, g_ref, sc_ref, dx_ref, stats_ref, first):
    xv = x_ref[...]
    g = g_ref[...]
    r = lax.rsqrt(jnp.mean(xv * xv, axis=-1, keepdims=True) + RMS_EPS)
    xh = xv * r
    dn = dh * (1.0 + sc_ref[...])
    dxh = dn * g
    dx_ref[...] = dxo_ref[...] + r * (dxh - xh * jnp.mean(dxh * xh, axis=-1, keepdims=True))

    @pl.when(first)
    def _():
        stats_ref[...] = jnp.zeros_like(stats_ref)

    stats_ref[0:1, :] += jnp.sum(dh, axis=0, keepdims=True)
    stats_ref[1:2, :] += jnp.sum(dh * (xh * g), axis=0, keepdims=True)
    stats_ref[2:3, :] += jnp.sum(dn * xh, axis=0, keepdims=True)


def ffn_up(h, wg_all, wu_all):
    def body(h_ref, wg_ref, wu_ref, a_ref, b_ref, s_ref):
        hv = h_ref[...]
        a = _dot(hv, wg_ref[...], NN)
        b = _dot(hv, wu_ref[...], NN)
        a_ref[...] = a
        b_ref[...] = b
        s_ref[...] = (a * _sigmoid(a) * b).astype(BF16)

    w_spec = pl.BlockSpec((None, D_MODEL, FF_SHARD), lambda j, i: (j, 0, 0))
    o_spec = pl.BlockSpec((None, MM_TILE, FF_SHARD), lambda j, i: (j, i, 0))
    hid = (N_CHIPS, SEQ, FF_SHARD)
    return pl.pallas_call(
        body, grid=(N_CHIPS, SEQ // MM_TILE),
        in_specs=[pl.BlockSpec((MM_TILE, D_MODEL), lambda j, i: (i, 0)), w_spec, w_spec],
        out_specs=[o_spec, o_spec, o_spec],
        out_shape=[jax.ShapeDtypeStruct(hid, F32), jax.ShapeDtypeStruct(hid, F32), jax.ShapeDtypeStruct(hid, BF16)],
        name="ffn_up", compiler_params=_params(("parallel", "parallel")),
    )(h, wg_all, wu_all)


def matmul_residual(name, a, a_spec, w_all, w_spec, x, coef, mult):
    def epilogue(acc, ins, outs, ids):
        outs[0][...] = acc
        outs[1][...] = ins[2][...] + (mult * ins[3][...]) * acc

    row = pl.BlockSpec((MM_TILE, D_MODEL), lambda i, j: (i, 0))
    return _matmul(
        name, (SEQ // MM_TILE, N_CHIPS), N_CHIPS, 1, NN, 1,
        [a_spec, w_spec, row, pl.BlockSpec((1, D_MODEL), lambda i, j: (0, 0))], [row, row],
        [jax.ShapeDtypeStruct((SEQ, D_MODEL), F32)] * 2, (MM_TILE, D_MODEL), epilogue,
        (a, w_all, x, coef), ("parallel", "arbitrary"))


def ffn_down(s, wd_all, x, gate):
    return matmul_residual(
        "ffn_down", s, pl.BlockSpec((None, MM_TILE, FF_SHARD), lambda i, j: (j, i, 0)),
        wd_all, pl.BlockSpec((None, FF_SHARD, D_MODEL), lambda i, j: (j, 0, 0)), x, gate, 0.5)


def ffn_bwd_hidden(df, wd_all, a, b):
    def epilogue(ds, ins, outs, ids):
        av, bv = ins[2][...], ins[3][...]
        sig = _sigmoid(av)
        outs[0][...] = (ds * bv * (sig * (1.0 + av * (1.0 - sig)))).astype(BF16)
        outs[1][...] = (ds * (av * sig)).astype(BF16)

    hid_spec = pl.BlockSpec((None, MM_TILE, FF_SHARD), lambda j, i: (j, i, 0))
    hid = jax.ShapeDtypeStruct((N_CHIPS, SEQ, FF_SHARD), BF16)
    return _matmul(
        "ffn_bwd_hidden", (N_CHIPS, SEQ // MM_TILE), 1, None, NT, 1,
        [pl.BlockSpec((MM_TILE, D_MODEL), lambda j, i: (i, 0)),
         pl.BlockSpec((None, FF_SHARD, D_MODEL), lambda j, i: (j, 0, 0)), hid_spec, hid_spec],
        [hid_spec, hid_spec], [hid, hid], None, epilogue, (df, wd_all, a, b), ("parallel", "parallel"))


def grad_weight(name, lhs, lhs_spec, rhs, rhs_spec, shape):
    def epilogue(acc, ins, outs, ids):
        outs[0][...] = acc.astype(BF16)

    return _matmul(
        name, (N_CHIPS, SEQ // MM_TILE), SEQ // MM_TILE, 1, TN, 1,
        [lhs_spec, rhs_spec], [pl.BlockSpec((None,) + shape, lambda j, k: (j, 0, 0))],
        [jax.ShapeDtypeStruct((N_CHIPS,) + shape, BF16)], shape, epilogue, (lhs, rhs), ("parallel", "arbitrary"))[0]


def ffn_grad_weights(h, s, df, da, db):
    tok = pl.BlockSpec((MM_TILE, D_MODEL), lambda j, k: (k, 0))
    hid = pl.BlockSpec((None, MM_TILE, FF_SHARD), lambda j, k: (j, k, 0))
    return (grad_weight("grad_w_gate", h, tok, da, hid, (D_MODEL, FF_SHARD)),
            grad_weight("grad_w_up", h, tok, db, hid, (D_MODEL, FF_SHARD)),
            grad_weight("grad_w_down", s, hid, df, tok, (FF_SHARD, D_MODEL)))


def matmul_prenorm_bwd(name, pairs, pair_specs, x, dxo, gain, scale):
    n = len(pairs)

    def epilogue(dh, ins, outs, ids):
        _prenorm_bwd_epilogue(dh, ins[n], ins[n + 1], ins[n + 2], ins[n + 3], outs[0], outs[1], ids[0] == 0)

    row = pl.BlockSpec((MM_TILE, D_MODEL), lambda i, j: (i, 0))
    vec = pl.BlockSpec((1, D_MODEL), lambda i, j: (0, 0))
    return _matmul(
        name, (SEQ // MM_TILE, N_CHIPS), N_CHIPS, 1, NT, len(pairs) // 2,
        list(pair_specs) + [row, row, vec, vec], [row, pl.BlockSpec((8, D_MODEL), lambda i, j: (0, 0))],
        [jax.ShapeDtypeStruct((SEQ, D_MODEL), F32), jax.ShapeDtypeStruct((8, D_MODEL), F32)],
        (MM_TILE, D_MODEL), epilogue, tuple(pairs) + (x, dxo, gain, scale), ("arbitrary", "arbitrary"))


def ffn_bwd_input(da, db, wg_all, wu_all, x, dxo, gain, scale):
    hid = pl.BlockSpec((None, MM_TILE, FF_SHARD), lambda i, j: (j, i, 0))
    w = pl.BlockSpec((None, D_MODEL, FF_SHARD), lambda i, j: (j, 0, 0))
    return matmul_prenorm_bwd("ffn_bwd_input", (da, wg_all, db, wu_all), (hid, w, hid, w), x, dxo, gain, scale)


def in_proj(h, w_all):
    def epilogue(acc, ins, outs, ids):
        outs[0][...] = acc

    return _matmul(
        "in_proj", (N_CHIPS, SEQ // MM_TILE), 1, None, NN, 1,
        [pl.BlockSpec((MM_TILE, D_MODEL), lambda j, i: (i, 0)),
         pl.BlockSpec((None, D_MODEL, IN_SHARD), lambda j, i: (j, 0, 0))],
        [pl.BlockSpec((MM_TILE, IN_SHARD), lambda j, i: (i, j))], [jax.ShapeDtypeStruct((SEQ, D_IN), F32)],
        None, epilogue, (h, w_all), ("parallel", "parallel"))[0]


_GATE_BLOCK0 = D_QKV // D_SHARD


def _branch_products(o, w_ref):
    ob = o.astype(BF16)
    return (_dot(ob[:, 0:256], w_ref[0:256, :], NN), _dot(ob[:, 256:384], w_ref[256:384, :], NN),
            _dot(ob[:, 384:768], w_ref[384:768, :], NN))


def merge_branches(o_cat, wbr_all, proj):
    def body(o_ref, w_ref, g0_ref, g1_ref, g2_ref, m_ref):
        u = _branch_products(o_ref[...], w_ref)
        m_ref[...] = (_sigmoid(g0_ref[...]) * u[0] + _sigmoid(g1_ref[...]) * u[1]
                      + _sigmoid(g2_ref[...]) * u[2]).astype(BF16)

    def gate_spec(b):
        return pl.BlockSpec((MM_TILE, D_SHARD), lambda i, j: (i, _GATE_BLOCK0 + 4 * b + j))

    return pl.pallas_call(
        body, grid=(SEQ // MM_TILE, N_CHIPS),
        in_specs=[pl.BlockSpec((MM_TILE, BR_ROWS), lambda i, j: (i, 0)),
                  pl.BlockSpec((None, BR_ROWS, D_SHARD), lambda i, j: (j, 0, 0)),
                  gate_spec(0), gate_spec(1), gate_spec(2)],
        out_specs=pl.BlockSpec((MM_TILE, D_SHARD), lambda i, j: (i, j)),
        out_shape=jax.ShapeDtypeStruct((SEQ, D_MODEL), BF16),
        name="merge_branches", compiler_params=_params(("parallel", "parallel")),
    )(o_cat, wbr_all, proj, proj, proj)


def out_proj(merged, wout_all, x, gate):
    return matmul_residual(
        "out_proj", merged, pl.BlockSpec((MM_TILE, D_SHARD), lambda i, j: (i, j)),
        wout_all, pl.BlockSpec((None, D_SHARD, D_MODEL), lambda i, j: (j, 0, 0)), x, gate, 1.0)


def merge_bwd(dmo, wout_all, o_cat, wbr_all, proj):
    def epilogue(dm, ins, outs, ids):
        u = _branch_products(ins[2][...], ins[3])
        for b in range(3):
            sig = _sigmoid(ins[4 + b][...])
            outs[b][...] = (dm * sig).astype(BF16)
            outs[3 + b][...] = (dm * u[b] * (sig * (1.0 - sig))).astype(BF16)

    def gate_spec(b):
        return pl.BlockSpec((MM_TILE, D_SHARD), lambda j, i: (i, _GATE_BLOCK0 + 4 * b + j))

    col = pl.BlockSpec((MM_TILE, D_SHARD), lambda j, i: (i, j))
    du = jax.ShapeDtypeStruct((SEQ, D_MODEL), BF16)
    return _matmul(
        "merge_bwd", (N_CHIPS, SEQ // MM_TILE), 1, None, NT, 1,
        [pl.BlockSpec((MM_TILE, D_MODEL), lambda j, i: (i, 0)),
         pl.BlockSpec((None, D_SHARD, D_MODEL), lambda j, i: (j, 0, 0)),
         pl.BlockSpec((MM_TILE, BR_ROWS), lambda j, i: (i, 0)),
         pl.BlockSpec((None, BR_ROWS, D_SHARD), lambda j, i: (j, 0, 0)),
         gate_spec(0), gate_spec(1), gate_spec(2)],
        [col] * 6, [du] * 6,
        None, epilogue, (dmo, wout_all, o_cat, wbr_all, proj, proj, proj), ("parallel", "parallel"))


def branch_bwd_input(du, wbr_all):
    def body(d0_ref, d1_ref, d2_ref, w_ref, o_ref, acc):
        j = pl.program_id(1)
        parts = (_dot(d0_ref[...], w_ref[0:256, :], NT), _dot(d1_ref[...], w_ref[256:384, :], NT),
                 _dot(d2_ref[...], w_ref[384:768, :], NT))

        @pl.when(j == 0)
        def _():
            acc[:, 0:256], acc[:, 256:384], acc[:, 384:768] = parts

        @pl.when(j > 0)
        def _():
            acc[:, 0:256] += parts[0]
            acc[:, 256:384] += parts[1]
            acc[:, 384:768] += parts[2]

        @pl.when(j == N_CHIPS - 1)
        def _():
            o_ref[...] = acc[...]

    col = pl.BlockSpec((MM_TILE, D_SHARD), lambda i, j: (i, j))
    return pl.pallas_call(
        body, grid=(SEQ // MM_TILE, N_CHIPS),
        in_specs=[col, col, col, pl.BlockSpec((None, BR_ROWS, D_SHARD), lambda i, j: (j, 0, 0))],
        out_specs=pl.BlockSpec((MM_TILE, BR_ROWS), lambda i, j: (i, 0)),
        out_shape=jax.ShapeDtypeStruct((SEQ, BR_ROWS), F32),
        scratch_shapes=[pltpu.VMEM((MM_TILE, BR_ROWS), F32)],
        name="branch_bwd_input", compiler_params=_params(("parallel", "arbitrary")),
    )(du[0], du[1], du[2], wbr_all)


def branch_grad_weights(o_cat, du):
    def body(o_ref, d0_ref, d1_ref, d2_ref, g_ref, acc):
        k = pl.program_id(1)
        ob = o_ref[...].astype(BF16)
        parts = (_dot(ob[:, 0:256], d0_ref[...], TN), _dot(ob[:, 256:384], d1_ref[...], TN),
                 _dot(ob[:, 384:768], d2_ref[...], TN))

        @pl.when(k == 0)
        def _():
            acc[0:256, :], acc[256:384, :], acc[384:768, :] = parts

        @pl.when(k > 0)
        def _():
            acc[0:256, :] += parts[0]
            acc[256:384, :] += parts[1]
            acc[384:768, :] += parts[2]

        @pl.when(k == SEQ // MM_TILE - 1)
        def _():
            g_ref[...] = acc[...].astype(BF16)

    col = pl.BlockSpec((MM_TILE, D_SHARD), lambda j, k: (k, j))
    return pl.pallas_call(
        body, grid=(N_CHIPS, SEQ // MM_TILE),
        in_specs=[pl.BlockSpec((MM_TILE, BR_ROWS), lambda j, k: (k, 0)), col, col, col],
        out_specs=pl.BlockSpec((None, BR_ROWS, D_SHARD), lambda j, k: (j, 0, 0)),
        out_shape=jax.ShapeDtypeStruct((N_CHIPS, BR_ROWS, D_SHARD), BF16),
        scratch_shapes=[pltpu.VMEM((BR_ROWS, D_SHARD), F32)],
        name="branch_grad_weights", compiler_params=_params(("parallel", "arbitrary")),
    )(o_cat, du[0], du[1], du[2])


def mixer_bwd_input(dproj, win_all, x, dxo, gain, scale):
    return matmul_prenorm_bwd(
        "mixer_bwd_input", (dproj, win_all),
        (pl.BlockSpec((MM_TILE, IN_SHARD), lambda i, j: (i, j)),
         pl.BlockSpec((None, D_MODEL, IN_SHARD), lambda i, j: (j, 0, 0))), x, dxo, gain, scale)


BATCH_QK = (((2,), (2,)), ((0,), (0,)))
BATCH_PV = (((2,), (1,)), ((0,), (0,)))
BATCH_TN = (((1,), (1,)), ((0,), (0,)))


def _split_dot(v, tri):
    v2 = v.reshape(v.shape[0] * BLK, BLK)
    hi = v2.astype(BF16)
    lo = (v2 - hi.astype(F32)).astype(BF16)
    return (_dot(hi, tri, NN) + _dot(lo, tri, NN)).reshape(v.shape)


def _tri(cmp):
    return cmp(lax.broadcasted_iota(jnp.int32, (BLK, BLK), 0), lax.broadcasted_iota(jnp.int32, (BLK, BLK), 1)).astype(BF16)


def _sb_scores(qs, k_ref, i, j):
    rows = pl.ds(pl.multiple_of(j * BLK, BLK), BLK)
    z = _dot(qs, k_ref[:, rows, :], BATCH_QK)
    tile = (1, BLK, BLK)
    before = (j * BLK + lax.broadcasted_iota(jnp.int32, tile, 2)) < (i * BLK + lax.broadcasted_iota(jnp.int32, tile, 1))
    soft = jnp.log(1.0 + jnp.exp(-jnp.abs(z)))
    log_fail = jnp.where(before, -(jnp.maximum(z, 0.0) + soft), 0.0)
    log_hit = jnp.minimum(z, 0.0) - soft
    return rows, before, log_fail, log_hit


def sb_forward(q, k, v):
    def body(q_ref, k_ref, v_ref, o_ref, tot_ref):
        i = pl.program_id(0)
        qs = q_ref[...]
        later = _tri(lambda r, c: r > c)

        def step(t, carry):
            o, run = carry
            rows, before, log_fail, log_hit = _sb_scores(qs, k_ref, i, i - t)
            between = _split_dot(log_fail, later) + run
            w = jnp.where(before, jnp.exp(log_hit + between), 0.0)
            o = o + _dot(w.astype(BF16), v_ref[:, rows, :], BATCH_PV)
            return o, run + jnp.sum(log_fail, axis=2, keepdims=True)

        o, run = lax.fori_loop(0, i + 1, step, (jnp.zeros((H_SB, BLK, HEAD_DIM), F32), jnp.zeros((H_SB, BLK, 1), F32)))
        o_ref[...] = o
        tot_ref[...] = run

    blk = pl.BlockSpec((H_SB, BLK, HEAD_DIM), lambda i: (0, i, 0))
    col = pl.BlockSpec((H_SB, BLK, 1), lambda i: (0, i, 0))
    full = pl.BlockSpec((H_SB, SEQ, HEAD_DIM), lambda i: (0, 0, 0))
    return pl.pallas_call(
        body, grid=(SEQ // BLK,), in_specs=[blk, full, full], out_specs=[blk, col],
        out_shape=[jax.ShapeDtypeStruct((H_SB, SEQ, HEAD_DIM), F32), jax.ShapeDtypeStruct((H_SB, SEQ, 1), F32)],
        name="sb_forward", compiler_params=_params(("parallel",)),
    )(q, k, v)


def sb_backward(q, k, v, total, do):
    def body(q_ref, k_ref, v_ref, tot_ref, do_ref, dq_ref, dk_ref, dv_ref):
        i = pl.program_id(0)

        @pl.when(i == 0)
        def _():
            dk_ref[...] = jnp.zeros_like(dk_ref)
            dv_ref[...] = jnp.zeros_like(dv_ref)

        qs = q_ref[...]
        dob = do_ref[...].astype(BF16)
        total_v = tot_ref[...]
        upto = _tri(lambda r, c: r <= c)
        earlier = _tri(lambda r, c: r < c)

        def step(j, carry):
            dq, seen, g_seen = carry
            rows, before, log_fail, log_hit = _sb_scores(qs, k_ref, i, j)
            between = total_v - (seen + _split_dot(log_fail, upto))
            w = jnp.where(before, jnp.exp(log_hit + between), 0.0)
            g = _dot(dob, v_ref[:, rows, :], BATCH_QK) * w
            g_earlier = g_seen + _split_dot(g, earlier)
            sig = jnp.exp(log_hit)
            dz = jnp.where(before, g * (1.0 - sig) - g_earlier * sig, 0.0).astype(BF16)
            dq = dq + _dot(dz, k_ref[:, rows, :], BATCH_PV)
            dk_ref[:, rows, :] += _dot(dz, qs, BATCH_TN)
            dv_ref[:, rows, :] += _dot(w.astype(BF16), dob, BATCH_TN)
            return dq, seen + jnp.sum(log_fail, axis=2, keepdims=True), g_seen + jnp.sum(g, axis=2, keepdims=True)

        zero = jnp.zeros((H_SB, BLK, 1), F32)
        dq, _, _ = lax.fori_loop(0, i + 1, step, (jnp.zeros((H_SB, BLK, HEAD_DIM), F32), zero, zero))
        dq_ref[...] = dq * QK_SCALE

    blk = pl.BlockSpec((H_SB, BLK, HEAD_DIM), lambda i: (0, i, 0))
    col = pl.BlockSpec((H_SB, BLK, 1), lambda i: (0, i, 0))
    full = pl.BlockSpec((H_SB, SEQ, HEAD_DIM), lambda i: (0, 0, 0))
    shape = jax.ShapeDtypeStruct((H_SB, SEQ, HEAD_DIM), F32)
    return pl.pallas_call(
        body, grid=(SEQ // BLK,), in_specs=[blk, full, full, col, blk], out_specs=[blk, full, full],
        out_shape=[shape, shape, shape],
        name="sb_backward", compiler_params=_params(("arbitrary",)),
    )(q, k, v, total, do)


def _band_scores(q_ref, kp_ref, ko_ref, bias_ref, hb, prev_mask):
    b = pl.program_id(1)
    qs = q_ref[...]
    s_prev = _dot(qs, kp_ref[...], BATCH_QK) + bias_ref[:, :, 0:BLK]
    s_prev = jnp.concatenate(
        [jnp.where((b & prev_mask(pl.program_id(0) * hb + t)) != 0, s_prev[t:t + 1], NEG) for t in range(hb)], axis=0)
    s_own = _dot(qs, ko_ref[...], BATCH_QK) + bias_ref[:, :, BLK:2 * BLK]
    return qs, s_prev, s_own


def _band_specs(hb, rows, t_n):
    def q_spec(width):
        return pl.BlockSpec((hb, None, rows, width), lambda h, b: (h, b, 0, 0))

    own = pl.BlockSpec((hb, BLK, HEAD_DIM), lambda h, b: (h, b, 0))
    prev = pl.BlockSpec((hb, BLK, HEAD_DIM), lambda h, b: (h, jnp.maximum(b - 1, 0), 0))
    per_head = lambda r, width: pl.BlockSpec((hb, r, width), lambda h, b: (h, 0, 0))
    return q_spec, own, prev, per_head


def banded_forward(name, q, k, v, bias, sinks, hb, prev_mask):
    h_n, nb, rows, _ = q.shape

    def body(q_ref, kp_ref, ko_ref, vp_ref, vo_ref, bias_ref, sink_ref, o_ref, lse_ref):
        _, s_prev, s_own = _band_scores(q_ref, kp_ref, ko_ref, bias_ref, hb, prev_mask)
        sink = sink_ref[...]
        m = jnp.maximum(jnp.maximum(jnp.max(s_prev, axis=2, keepdims=True), jnp.max(s_own, axis=2, keepdims=True)), sink)
        p_prev = jnp.exp(s_prev - m)
        p_own = jnp.exp(s_own - m)
        denom = jnp.sum(p_prev, axis=2, keepdims=True) + jnp.sum(p_own, axis=2, keepdims=True) + jnp.exp(sink - m)
        o = _dot(p_prev.astype(BF16), vp_ref[...], BATCH_PV) + _dot(p_own.astype(BF16), vo_ref[...], BATCH_PV)
        o_ref[...] = o / denom
        lse_ref[...] = m + jnp.log(denom)

    q_spec, own, prev, per_head = _band_specs(hb, rows, k.shape[1])
    return pl.pallas_call(
        body, grid=(h_n // hb, nb),
        in_specs=[q_spec(HEAD_DIM), prev, own, prev, own, per_head(rows, 2 * BLK), per_head(rows, 1)],
        out_specs=[q_spec(HEAD_DIM), q_spec(1)],
        out_shape=[jax.ShapeDtypeStruct(q.shape, F32), jax.ShapeDtypeStruct((h_n, nb, rows, 1), F32)],
        name=name, compiler_params=_params(("parallel", "parallel")),
    )(q, k, k, v, v, bias, sinks)


def banded_backward(name, q, k, v, bias, sinks, o, lse, do, dlse, hb, prev_mask):
    h_n, nb, rows, _ = q.shape
    t_n = k.shape[1]

    def body(q_ref, kp_ref, ko_ref, vp_ref, vo_ref, bias_ref, sink_ref, o_ref, lse_ref, do_ref, dlse_ref,
             dq_ref, dk_ref, dv_ref, dbias_ref, dsink_ref):
        b = pl.program_id(1)

        @pl.when(b == 0)
        def _():
            dk_ref[...] = jnp.zeros_like(dk_ref)
            dv_ref[...] = jnp.zeros_like(dv_ref)
            dbias_ref[...] = jnp.zeros_like(dbias_ref)
            dsink_ref[...] = jnp.zeros_like(dsink_ref)

        qs, s_prev, s_own = _band_scores(q_ref, kp_ref, ko_ref, bias_ref, hb, prev_mask)
        lse_v = lse_ref[...]
        dov = do_ref[...]
        dob = dov.astype(BF16)
        shift = dlse_ref[...] - jnp.sum(dov * o_ref[...], axis=2, keepdims=True)
        p_prev = jnp.exp(s_prev - lse_v)
        p_own = jnp.exp(s_own - lse_v)
        ds_prev = p_prev * (_dot(dob, vp_ref[...], BATCH_QK) + shift)
        ds_own = p_own * (_dot(dob, vo_ref[...], BATCH_QK) + shift)
        dbias_ref[:, :, 0:BLK] += ds_prev
        dbias_ref[:, :, BLK:2 * BLK] += ds_own
        d_sink = jnp.exp(sink_ref[...] - lse_v) * shift
        for g in range(rows // BLK):
            dsink_ref[:, g:g + 1, :] += jnp.sum(d_sink[:, g * BLK:(g + 1) * BLK, :], axis=1, keepdims=True)
        ds_prev = ds_prev.astype(BF16)
        ds_own = ds_own.astype(BF16)
        dq_ref[...] = (_dot(ds_prev, kp_ref[...], BATCH_PV) + _dot(ds_own, ko_ref[...], BATCH_PV)) * QK_SCALE
        rows_prev = pl.ds(pl.multiple_of(jnp.maximum(b - 1, 0) * BLK, BLK), BLK)
        rows_own = pl.ds(pl.multiple_of(b * BLK, BLK), BLK)
        dk_ref[:, rows_prev, :] += _dot(ds_prev, qs, BATCH_TN)
        dk_ref[:, rows_own, :] += _dot(ds_own, qs, BATCH_TN)
        dv_ref[:, rows_prev, :] += _dot(p_prev.astype(BF16), dob, BATCH_TN)
        dv_ref[:, rows_own, :] += _dot(p_own.astype(BF16), dob, BATCH_TN)

    q_spec, own, prev, per_head = _band_specs(hb, rows, t_n)
    kv_full = per_head(t_n, HEAD_DIM)
    kv_shape = jax.ShapeDtypeStruct((h_n, t_n, HEAD_DIM), F32)
    return pl.pallas_call(
        body, grid=(h_n // hb, nb),
        in_specs=[q_spec(HEAD_DIM), prev, own, prev, own, per_head(rows, 2 * BLK), per_head(rows, 1),
                  q_spec(HEAD_DIM), q_spec(1), q_spec(HEAD_DIM), q_spec(1)],
        out_specs=[q_spec(HEAD_DIM), kv_full, kv_full, per_head(rows, 2 * BLK), per_head(rows // BLK, BLK)],
        out_shape=[jax.ShapeDtypeStruct(q.shape, F32), kv_shape, kv_shape,
                   jax.ShapeDtypeStruct((h_n, rows, 2 * BLK), F32), jax.ShapeDtypeStruct((h_n, rows // BLK, BLK), F32)],
        name=name, compiler_params=_params(("parallel", "arbitrary")),
    )(q, k, k, v, v, bias, sinks, o, lse, do, dlse)


def _dil_prev_mask(head):
    group = head // H_PER_DIL
    return jnp.where(group == 0, 15, jnp.where(group == 1, 3, 0))


def _swa_prev_mask(head):
    del head
    return 15


DIL_HEADS_PER_STEP = 3
SWA_GROUP = H_SWA_Q // H_SWA_KV
N_BLK = SEQ // BLK


def dilated_merge(o, lse):
    def body(o_ref, l_ref, out_ref):
        lv = l_ref[...]
        m = jnp.max(lv, axis=0, keepdims=True)
        e = jnp.exp(lv - m)
        alpha = e / jnp.sum(e, axis=0, keepdims=True)
        out_ref[...] = jnp.sum(alpha * o_ref[...], axis=0)

    return pl.pallas_call(
        body, grid=(H_PER_DIL, SEQ // ROW_TILE),
        in_specs=[pl.BlockSpec((3, None, ROW_TILE, HEAD_DIM), lambda h, i: (0, h, i, 0)),
                  pl.BlockSpec((3, None, ROW_TILE, 1), lambda h, i: (0, h, i, 0))],
        out_specs=pl.BlockSpec((None, ROW_TILE, HEAD_DIM), lambda h, i: (h, i, 0)),
        out_shape=jax.ShapeDtypeStruct((H_PER_DIL, SEQ, HEAD_DIM), F32),
        name="dilated_merge", compiler_params=_params(("parallel", "parallel")),
    )(o, lse)


def dilated_merge_bwd(o, lse, dout):
    def body(o_ref, l_ref, d_ref, do_ref, dl_ref):
        lv = l_ref[...]
        m = jnp.max(lv, axis=0, keepdims=True)
        e = jnp.exp(lv - m)
        alpha = e / jnp.sum(e, axis=0, keepdims=True)
        dv = d_ref[...][None]
        do_ref[...] = alpha * dv
        dalpha = jnp.sum(dv * o_ref[...], axis=-1, keepdims=True)
        dl_ref[...] = alpha * (dalpha - jnp.sum(alpha * dalpha, axis=0, keepdims=True))

    o_spec = pl.BlockSpec((3, None, ROW_TILE, HEAD_DIM), lambda h, i: (0, h, i, 0))
    l_spec = pl.BlockSpec((3, None, ROW_TILE, 1), lambda h, i: (0, h, i, 0))
    return pl.pallas_call(
        body, grid=(H_PER_DIL, SEQ // ROW_TILE),
        in_specs=[o_spec, l_spec, pl.BlockSpec((None, ROW_TILE, HEAD_DIM), lambda h, i: (h, i, 0))],
        out_specs=[o_spec, l_spec],
        out_shape=[jax.ShapeDtypeStruct(o.shape, F32), jax.ShapeDtypeStruct(lse.shape, F32)],
        name="dilated_merge_bwd", compiler_params=_params(("parallel", "parallel")),
    )(o, lse, dout)


def rel_bias_reduce(dbias0, dbias1, bucket):
    def body(d0_ref, d1_ref, b_ref, o_ref):
        dv, bv = d0_ref[...] + d1_ref[...], b_ref[...]
        lane = lax.broadcasted_iota(jnp.int32, (1, BLK), 1)
        acc = jnp.zeros((1, BLK), F32)
        for bkt in range(N_BUCKETS):
            acc = acc + jnp.where(lane == bkt, jnp.sum(jnp.where(bv == bkt, dv, 0.0)), 0.0)
        o_ref[...] = acc

    tile = pl.BlockSpec((None, BLK, 2 * BLK), lambda h: (h, 0, 0))
    return pl.pallas_call(
        body, grid=(dbias0.shape[0],), in_specs=[tile, tile, tile],
        out_specs=pl.BlockSpec((None, 1, BLK), lambda h: (h, 0, 0)),
        out_shape=jax.ShapeDtypeStruct((dbias0.shape[0], 1, BLK), F32),
        name="rel_bias_reduce", compiler_params=_params(("parallel",)),
    )(dbias0, dbias1, bucket)


def _heads(t):
    return t.reshape(SEQ, -1, HEAD_DIM).transpose(1, 0, 2)


def _unheads(t):
    return t.transpose(1, 0, 2).reshape(SEQ, -1)


def _dilate(t):
    parts = []
    for g, (_, d) in enumerate(DIL_PATTERNS):
        tg = t[:, 128 * g:128 * (g + 1)].reshape(SEQ // d, d, H_PER_DIL, HEAD_DIM).transpose(2, 1, 0, 3)
        parts.append(tg.reshape(H_PER_DIL, SEQ, HEAD_DIM))
    return jnp.concatenate(parts, axis=0)


def _undilate(t):
    outs = []
    for g, (_, d) in enumerate(DIL_PATTERNS):
        tg = t[2 * g:2 * g + 2].reshape(H_PER_DIL, d, SEQ // d, -1).transpose(0, 2, 1, 3)
        outs.append(tg.reshape(H_PER_DIL, SEQ, -1))
    return jnp.stack(outs)


def _redilate(t):
    parts = []
    for g, (_, d) in enumerate(DIL_PATTERNS):
        tg = t[g].reshape(H_PER_DIL, SEQ // d, d, -1).transpose(0, 2, 1, 3)
        parts.append(tg.reshape(H_PER_DIL, SEQ, -1))
    return jnp.concatenate(parts, axis=0)


def _t5_bucket(n):
    max_exact = N_BUCKETS // 2
    nf = jnp.maximum(n, 1).astype(F32)
    large = max_exact + (jnp.log(nf / max_exact) / math.log(MAX_REL_DIST / max_exact)
                         * (N_BUCKETS - max_exact)).astype(jnp.int32)
    large = jnp.minimum(large, N_BUCKETS - 1)
    return jnp.where(n < max_exact, n, large)


def band_tables(rel_bias):
    rel = jnp.arange(BLK)[:, None] + BLK - jnp.arange(2 * BLK)[None, :]
    buckets = []
    patterns = [(d, w // d) for w, d in DIL_PATTERNS for _ in range(H_PER_DIL)] + [(1, SWA_WINDOW - 1)] * H_SWA_Q
    for d, max_dist in patterns:
        band = (rel >= 0) & (rel <= max_dist)
        buckets.append(jnp.where(band, _t5_bucket(jnp.maximum(rel, 0) * d), -1))
    buckets = jnp.stack(buckets).astype(jnp.int32)

    def body(table_ref, b_ref, o_ref):
        h = pl.program_id(0)
        bv = b_ref[...]
        tile = jnp.full(bv.shape, NEG, F32)
        for bkt in range(N_BUCKETS):
            tile = jnp.where(bv == bkt, table_ref[h, bkt], tile)
        o_ref[...] = tile

    spec = pl.BlockSpec((None, BLK, 2 * BLK), lambda h: (h, 0, 0))
    tiles = pl.pallas_call(
        body, grid=(len(patterns),), in_specs=[pl.BlockSpec(memory_space=pltpu.SMEM), spec], out_specs=spec,
        out_shape=jax.ShapeDtypeStruct(buckets.shape, F32), name="band_tables", compiler_params=_params(("parallel",)),
    )(rel_bias.T, buckets)
    return tiles[:H_DIL], tiles[H_DIL:], buckets


def _swa_rows(t):
    t = t.reshape(N_BLK, BLK, H_SWA_KV, SWA_GROUP, HEAD_DIM).transpose(2, 0, 3, 1, 4)
    return t.reshape(H_SWA_KV, N_BLK, SWA_GROUP * BLK, HEAD_DIM)


def _swa_tokens(t):
    t = t.reshape(H_SWA_KV, N_BLK, SWA_GROUP, BLK, HEAD_DIM).transpose(1, 3, 0, 2, 4)
    return t.reshape(SEQ, H_SWA_Q * HEAD_DIM)


def _sink_rows(sinks):
    return jnp.broadcast_to(sinks.reshape(H_SWA_KV, SWA_GROUP, 1, 1), (H_SWA_KV, SWA_GROUP, BLK, 1)).reshape(
        H_SWA_KV, SWA_GROUP * BLK, 1)


def _no_sinks():
    return jnp.full((H_DIL, BLK, 1), NEG, F32)


def _vec(v):
    return v.reshape(1, D_MODEL)


def ffn_forward(x, gain, mod, w):
    h = prenorm(x, _vec(gain), _vec(mod[1]), _vec(mod[0]))
    a, b, s = ffn_up(h, w[0], w[1])
    f, xo = ffn_down(s, w[2], x, _vec(mod[2]))
    return xo, (x, h, a, b, s, f)


def ffn_backward(dxo, saved, gain, mod, w):
    x, h, a, b, s, f = saved
    df, dgate = resid_bwd(dxo, f, _vec(mod[2]), 0.5)
    da, db = ffn_bwd_hidden(df, w[2], a, b)
    grads = ffn_grad_weights(h, s, df, da, db)
    dx, stats = ffn_bwd_input(da, db, w[0], w[1], x, dxo, _vec(gain), _vec(mod[1]))
    return dx, jnp.stack([stats[0], stats[1], dgate[0]]), stats[2], grads


def mixer_forward(x, gain, mod, sinks, bias_dil, bias_swa, w):
    h = prenorm(x, _vec(gain), _vec(mod[1]), _vec(mod[0]))
    proj = in_proj(h, w[0])
    qkv = proj[:, :D_QKV].astype(BF16)
    q_sb, k_sb, v_sb = _heads(qkv[:, 0:256] * QK_SCALE), _heads(qkv[:, 256:512]), _heads(qkv[:, 512:768])
    q_dil, k_dil, v_dil = _dilate(qkv[:, 768:1152] * QK_SCALE), _dilate(qkv[:, 1152:1536]), _dilate(qkv[:, 1536:1920])
    q_swa, k_swa, v_swa = _swa_rows(qkv[:, 1920:2304] * QK_SCALE), _heads(qkv[:, 2304:2432]), _heads(qkv[:, 2432:2560])
    o_sb, total_sb = sb_forward(q_sb, k_sb, v_sb)
    q_dil = q_dil.reshape(H_DIL, N_BLK, BLK, HEAD_DIM)
    o_dd, lse_dd = banded_forward("dilated_forward", q_dil, k_dil, v_dil, bias_dil, _no_sinks(), DIL_HEADS_PER_STEP,
                                  _dil_prev_mask)
    o_dt, lse_dt = _undilate(o_dd.reshape(H_DIL, SEQ, HEAD_DIM)), _undilate(lse_dd.reshape(H_DIL, SEQ, 1))
    o_dil = dilated_merge(o_dt, lse_dt)
    bias_swa = bias_swa.reshape(H_SWA_KV, SWA_GROUP * BLK, 2 * BLK)
    o_swa, lse_swa = banded_forward("swa_forward", q_swa, k_swa, v_swa, bias_swa, _sink_rows(sinks), 1, _swa_prev_mask)
    o_cat = jnp.concatenate([_unheads(o_sb), _unheads(o_dil), _swa_tokens(o_swa)], axis=1)
    merged = merge_branches(o_cat, w[1], proj)
    mo, xo = out_proj(merged, w[2], x, _vec(mod[2]))
    saved = (x, h, proj, (q_sb, k_sb, v_sb, total_sb), (q_dil, k_dil, v_dil, o_dd, lse_dd, o_dt, lse_dt),
             (q_swa, k_swa, v_swa, o_swa, lse_swa), o_cat, merged, mo)
    return xo, saved


def mixer_backward(dxo, saved, gain, mod, sinks, bias_dil, bias_swa, w):
    x, h, proj, sb, dil, swa, o_cat, merged, mo = saved
    dmo, dgate = resid_bwd(dxo, mo, _vec(mod[2]), 1.0)
    tok = pl.BlockSpec((MM_TILE, D_MODEL), lambda j, k: (k, 0))
    g_out = grad_weight("grad_w_out", merged, pl.BlockSpec((MM_TILE, D_SHARD), lambda j, k: (k, j)), dmo, tok,
                        (D_SHARD, D_MODEL))
    du0, du1, du2, dg0, dg1, dg2 = merge_bwd(dmo, w[2], o_cat, w[1], proj)
    du = (du0, du1, du2)
    do_cat = branch_bwd_input(du, w[1])
    g_br = branch_grad_weights(o_cat, du)

    q_sb, k_sb, v_sb, total_sb = sb
    dq_sb, dk_sb, dv_sb = sb_backward(q_sb, k_sb, v_sb, total_sb, _heads(do_cat[:, 0:256]))

    q_dil, k_dil, v_dil, o_dd, lse_dd, o_dt, lse_dt = dil
    do_dt, dlse_dt = dilated_merge_bwd(o_dt, lse_dt, _heads(do_cat[:, 256:384]))
    dq_dil, dk_dil, dv_dil, dbias_dil, _ = banded_backward(
        "dilated_backward", q_dil, k_dil, v_dil, bias_dil, _no_sinks(), o_dd, lse_dd,
        _redilate(do_dt).reshape(q_dil.shape), _redilate(dlse_dt).reshape(lse_dd.shape), DIL_HEADS_PER_STEP, _dil_prev_mask)

    q_swa, k_swa, v_swa, o_swa, lse_swa = swa
    bias_swa = bias_swa.reshape(H_SWA_KV, SWA_GROUP * BLK, 2 * BLK)
    dq_swa, dk_swa, dv_swa, dbias_swa, dsinks = banded_backward(
        "swa_backward", q_swa, k_swa, v_swa, bias_swa, _sink_rows(sinks), o_swa, lse_swa, _swa_rows(do_cat[:, 384:768]),
        jnp.zeros_like(lse_swa), 1, _swa_prev_mask)
    dbias_swa = dbias_swa.reshape(H_SWA_Q, BLK, 2 * BLK)

    def tokens(t):
        return _undilate(t).transpose(2, 0, 1, 3).reshape(SEQ, -1)

    dproj = jnp.concatenate(
        [_unheads(dq_sb), _unheads(dk_sb), _unheads(dv_sb), tokens(dq_dil.reshape(H_DIL, SEQ, HEAD_DIM)), tokens(dk_dil),
         tokens(dv_dil), _swa_tokens(dq_swa), _unheads(dk_swa), _unheads(dv_swa)], axis=1).astype(BF16)
    dproj = jnp.concatenate([dproj, dg0, dg1, dg2], axis=1)
    g_in = grad_weight("grad_w_in", h, tok, dproj, pl.BlockSpec((MM_TILE, IN_SHARD), lambda j, k: (k, j)),
                       (D_MODEL, IN_SHARD))
    dx, stats = mixer_bwd_input(dproj, w[0], x, dxo, _vec(gain), _vec(mod[1]))
    dmod = jnp.stack([stats[0], stats[1], dgate[0]])
    dbias = jnp.concatenate([dbias_dil, dbias_swa], axis=0)
    return dx, dmod, stats[2], dbias, dsinks[:, :, 0].reshape(H_SWA_Q), (g_in, g_br, g_out)


N_UNITS = 3 * DEPTH


def device_step(x, target, mod, gains, final_gain, sinks, rel_bias, get_weights, put_grads):
    bias_dil, bias_swa, bucket = band_tables(rel_bias)
    saved, weights = [], []
    for u in range(N_UNITS):
        l, j = divmod(u, 3)
        w = get_weights(u, x)
        if j == 1:
            x, s = mixer_forward(x, gains[l, 1], mod[l, 1], sinks[l], bias_dil, bias_swa, w)
        else:
            x, s = ffn_forward(x, gains[l, j], mod[l, j], w)
        saved.append(s)
        weights.append(w)
    loss, dx, dfinal = final_loss(x, _vec(final_gain), target)

    dmod = [[None] * 3 for _ in range(DEPTH)]
    dgains = [[None] * 3 for _ in range(DEPTH)]
    dbias, dsinks = [None] * DEPTH, [None] * DEPTH
    zero = jnp.zeros((1, 1), F32)
    for u in reversed(range(N_UNITS)):
        l, j = divmod(u, 3)
        gain = gains[l, j] + zero[0]
        if j == 1:
            dx, dmod[l][j], dgains[l][j], dbias[l], dsinks[l], grads = mixer_backward(
                dx, saved[u], gain, mod[l, 1], sinks[l], bias_dil, bias_swa, weights[u])
        else:
            dx, dmod[l][j], dgains[l][j], grads = ffn_backward(dx, saved[u], gain, mod[l, j], weights[u])
        zero = put_grads(u, grads)
    drel = rel_bias_reduce(dbias[0], dbias[1], bucket)[:, 0, :N_BUCKETS].T
    stack2 = lambda t: jnp.stack([jnp.stack(r) for r in t])
    return loss, dx, stack2(dmod), stack2(dgains), dfinal[0], jnp.stack(dsinks), drel


MESH = pl.DeviceIdType.MESH
CHIP_FLIPS = ((1, 0), (0, 1), (1, 1))
ANY = pl.BlockSpec(memory_space=pl.ANY)


def _position():
    return lax.axis_index("x"), lax.axis_index("y"), lax.axis_index("c")


def all_gather_small(name, piece):
    def body(x_ref, out_ref, send_sems, recv_sems, local_sem):
        x, y, c = _position()
        me, sibling = (x, y, c), (x, y, 1 - c)
        chips = [(x ^ fx, y ^ fy) for fx, fy in CHIP_FLIPS]

        def rows(px, py, pc):
            return out_ref.at[4 * px + 2 * py + pc]

        def copy(k, block, to, src=None):
            return pltpu.make_async_remote_copy(
                src_ref=rows(*block) if src is None else src, dst_ref=rows(*block),
                send_sem=send_sems.at[k], recv_sem=recv_sems.at[k], device_id=to, device_id_type=MESH)

        mine = pltpu.make_async_copy(x_ref, rows(*me), local_sem)
        mine.start()
        first = [copy(0, me, sibling, src=x_ref)]
        first += [copy(1 + j, me, (*chip, c), src=x_ref) for j, chip in enumerate(chips)]
        for cp in first:
            cp.start()
        passed = [copy(4 + j, (*chip, c), sibling) for j, chip in enumerate(chips)]
        for j, chip in enumerate(chips):
            copy(1 + j, (*chip, c), me).wait_recv()
            passed[j].start()
        copy(0, sibling, me).wait_recv()
        for j, chip in enumerate(chips):
            copy(4 + j, (*chip, 1 - c), me).wait_recv()
        for cp in first + passed:
            cp.wait_send()
        mine.wait()

    return pl.pallas_call(
        body, out_shape=jax.ShapeDtypeStruct((N_DEV,) + piece.shape, piece.dtype),
        in_specs=[pl.BlockSpec(memory_space=pltpu.VMEM)], out_specs=pl.BlockSpec(memory_space=pltpu.VMEM),
        scratch_shapes=[pltpu.SemaphoreType.DMA((7,)), pltpu.SemaphoreType.DMA((7,)), pltpu.SemaphoreType.DMA],
        name=name,
    )(piece)


def exchange(name, operands, out_shapes, aliases, plan):
    n_in, n_out = len(operands), len(out_shapes)

    def body(*refs):
        ins, outs = refs[:n_in], refs[n_in:n_in + n_out]
        send_sems, recv_sems, local_sems = refs[n_in + n_out:]
        x, y, c = _position()
        local, sends, recvs = plan(ins, outs, x, y, c)
        local = [pltpu.make_async_copy(s, d, local_sems.at[k]) for k, (s, d) in enumerate(local)]
        for cp in local:
            cp.start()
        remote = [pltpu.make_async_remote_copy(src_ref=s, dst_ref=d, send_sem=send_sems.at[k], recv_sem=recv_sems.at[k],
                                               device_id=dev, device_id_type=MESH)
                  for k, (s, d, dev) in enumerate(sends)]
        for cp in remote:
            cp.start()
        for k, r in enumerate(recvs):
            pltpu.make_async_remote_copy(src_ref=r, dst_ref=r, send_sem=send_sems.at[k], recv_sem=recv_sems.at[k],
                                         device_id=(x, y, c), device_id_type=MESH).wait_recv()
        for cp in remote:
            cp.wait_send()
        for cp in local:
            cp.wait()

    n_sends, n_local = plan.n_sends, max(plan.n_local, 1)
    return pl.pallas_call(
        body, out_shape=out_shapes, in_specs=[ANY] * n_in, out_specs=[ANY] * n_out,
        scratch_shapes=[pltpu.SemaphoreType.DMA((n_sends,)), pltpu.SemaphoreType.DMA((n_sends,)),
                        pltpu.SemaphoreType.DMA((n_local,))],
        input_output_aliases=aliases, name=name,
    )(*operands)


def _plan(n_local, n_sends):
    def wrap(fn):
        fn.n_local, fn.n_sends = n_local, n_sends
        return fn
    return wrap


def _half(ref, axis, c):
    rows = ref.shape[axis] // 2
    idx = [slice(None)] * len(ref.shape)
    idx[axis] = pl.ds(pl.multiple_of(c * rows, 16), rows)
    return ref.at[tuple(idx)]


HBM = pl.BlockSpec(memory_space=pltpu.HBM)
SEM = pl.BlockSpec(memory_space=pltpu.SEMAPHORE)
EFFECT = pltpu.SideEffectType.DATAFLOW_SIDE_EFFECTING


def split_start(name, bufs, extra, n_copies, describe):
    n = len(bufs)

    def body(*refs):
        send_sems, recv_sems = refs[n + len(extra)], refs[n + len(extra) + 1]
        x, y, c = _position()
        for k, (src, dst, _, peer) in enumerate(describe(refs[:n], x, y, c)):
            pltpu.make_async_remote_copy(src_ref=src, dst_ref=dst, send_sem=send_sems.at[k], recv_sem=recv_sems.at[k],
                                         device_id=peer, device_id_type=MESH).start()
        token = refs[-1]
        token[...] = jnp.zeros_like(token)

    out = pl.pallas_call(
        body, name=name,
        out_shape=(pltpu.SemaphoreType.DMA((n_copies,)), pltpu.SemaphoreType.DMA((n_copies,)),
                   *[pltpu.HBM(b.shape, b.dtype) for b in bufs], jax.ShapeDtypeStruct((8, 128), F32)),
        in_specs=[HBM] * n + [ANY] * len(extra),
        out_specs=(SEM, SEM, *[HBM] * n, pl.BlockSpec(memory_space=pltpu.VMEM)),
        input_output_aliases={k: 2 + k for k in range(n)},
        compiler_params=pltpu.CompilerParams(has_side_effects=EFFECT),
    )(*[pltpu.with_memory_space_constraint(b, pltpu.HBM) for b in bufs], *extra)
    return out[0], out[1], list(out[2:2 + n]), out[-1]


def split_wait(name, bufs, send_sems, recv_sems, after, describe):
    n = len(bufs)

    def body(*refs):
        send, recv = refs[n], refs[n + 1]
        x, y, c = _position()
        for k, (src, _, dst, peer) in enumerate(describe(refs[:n], x, y, c)):
            copy = pltpu.make_async_remote_copy(src_ref=src, dst_ref=dst, send_sem=send.at[k], recv_sem=recv.at[k],
                                                device_id=peer, device_id_type=MESH)
            copy.wait_send()
            copy.wait_recv()

    out = pl.pallas_call(
        body, name=name, out_shape=[pltpu.HBM(b.shape, b.dtype) for b in bufs],
        in_specs=[HBM] * n + [SEM, SEM] + [ANY] * len(after), out_specs=[HBM] * n,
        input_output_aliases={k: k for k in range(n)},
        compiler_params=pltpu.CompilerParams(has_side_effects=EFFECT),
    )(*bufs, send_sems, recv_sems, *after)
    return list(out)


def _row_tile(rows, cols):
    best = 16
    for t in range(16, rows + 1, 16):
        if rows % t == 0 and t * cols <= 256 * 1024:
            best = t
    return best


def cast_into_slot(name, param, index, chip):
    rows, cols = param.shape[-2:]
    tr = _row_tile(rows, cols)
    lead = (None,) * len(index)

    def body(chip_ref, s_ref, o_ref):
        del chip_ref
        o_ref[...] = s_ref[...].astype(BF16)

    return pl.pallas_call(
        body, out_shape=jax.ShapeDtypeStruct((N_CHIPS, rows, cols), BF16),
        grid_spec=pltpu.PrefetchScalarGridSpec(
            num_scalar_prefetch=1, grid=(rows // tr,),
            in_specs=[pl.BlockSpec(lead + (tr, cols), lambda r, chip_ref: index + (r, 0))],
            out_specs=pl.BlockSpec((None, tr, cols), lambda r, chip_ref: (chip_ref[0], r, 0))),
        name=name, compiler_params=_params(("parallel",)),
    )(chip, param)


GATHER_STAGES = ((0,), (1, 2), (3, 4, 5))
REDUCE_STAGES = ((5, 4, 3), (2, 1), (0,))


def _gather_copies(slots, x, y, c):
    me = 2 * x + y
    out = []
    for s in slots:
        for fx, fy in CHIP_FLIPS:
            mine = _half(s.at[me], 0, c)
            out.append((mine, mine, _half(s.at[2 * (x ^ fx) + (y ^ fy)], 0, c), (x ^ fx, y ^ fy, c)))
    return out


class WeightStream:
    def __init__(self, shards, chip):
        self.pending, self.ready = {}, {}
        token = ()
        for si, units in enumerate(GATHER_STAGES):
            slots = [cast_into_slot(f"cast_{u}_{t}", p, idx, chip) for u in units for t, (p, idx) in enumerate(shards[u])]
            send, recv, slots, tok = split_start(f"gather_start_{si}", slots, token, 3 * len(slots), _gather_copies)
            self.pending[si] = (send, recv, slots)
            token = (tok,)
        self.token = token

    def get(self, u, after):
        if u not in self.ready:
            si = next(k for k, units in enumerate(GATHER_STAGES) if u in units)
            send, recv, slots = self.pending.pop(si)
            slots = split_wait(f"gather_wait_{si}", slots, send, recv, (after,) + self.token, _gather_copies)
            self.token = ()

            @_plan(0, 3 * len(slots))
            def to_sibling(ins, outs, x, y, c):
                sends, recvs = [], []
                for o in outs:
                    for fx, fy in CHIP_FLIPS:
                        slab = o.at[2 * (x ^ fx) + (y ^ fy)]
                        sends.append((_half(slab, 0, c), _half(slab, 0, c), (x, y, 1 - c)))
                        recvs.append(_half(slab, 0, 1 - c))
                return [], sends, recvs

            shapes = [jax.ShapeDtypeStruct(s.shape, BF16) for s in slots]
            slots = exchange(f"gather_sibling_{si}", slots, shapes, {k: k for k in range(len(slots))}, to_sibling)
            for i, v in enumerate(GATHER_STAGES[si]):
                self.ready[v] = tuple(slots[3 * i:3 * i + 3])
        return self.ready[u]


def _reduce_copies(bufs, x, y, c):
    n = len(bufs) // 2
    out = []
    for s, land in zip(bufs[:n], bufs[n:]):
        for k, (fx, fy) in enumerate(CHIP_FLIPS):
            out.append((s.at[2 * (x ^ fx) + (y ^ fy)], land.at[k], land.at[k], (x ^ fx, y ^ fy, c)))
    return out


GRAD_SLOTS = {"gate": (2 * DEPTH, D_MODEL, FF_SHARD), "up": (2 * DEPTH, D_MODEL, FF_SHARD),
              "down": (2 * DEPTH, FF_SHARD, D_MODEL), "in": (DEPTH, D_MODEL, IN_SHARD),
              "br": (DEPTH, BR_ROWS, D_SHARD), "out": (DEPTH, D_SHARD, D_MODEL)}


def _unit_tensors(u):
    l, j = divmod(u, 3)
    if j == 1:
        return [("in", l), ("br", l), ("out", l)]
    return [(k, 2 * l + j // 2) for k in ("gate", "up", "down")]


class GradStream:
    def __init__(self, chip, core):
        self.core = core
        self.place = jnp.concatenate([chip, core])
        self.held, self.flying = {}, []
        self.full = {k: lax.empty(shape, F32) for k, shape in GRAD_SLOTS.items()}

    def put(self, u, grads):
        self.held[u] = grads
        si = len(self.flying)
        units = REDUCE_STAGES[si]
        if not all(v in self.held for v in units):
            return jnp.zeros((1, 1), F32)
        gs = [g for v in units for g in self.held[v]]

        @_plan(0, len(gs))
        def swap_halves(ins, outs, x, y, c):
            sends = [(_half(g, 1, 1 - c), o, (x, y, 1 - c)) for g, o in zip(ins, outs)]
            return [], sends, list(outs)

        half_shapes = [jax.ShapeDtypeStruct((N_CHIPS, g.shape[1] // 2, g.shape[2]), BF16) for g in gs]
        landed = exchange(f"reduce_swap_{si}", gs, half_shapes, {}, swap_halves)
        sums = [_add_halves(g, la, self.core) for g, la in zip(gs, landed)]
        landing = [lax.empty((3,) + s.shape[1:], BF16) for s in sums]
        send, recv, bufs, token = split_start(f"reduce_start_{si}", sums + landing, (), 3 * len(sums), _reduce_copies)
        self.flying.append((send, recv, bufs, [t for v in units for t in _unit_tensors(v)]))
        return token[0:1, 0:1]

    def finish(self, after):
        for si, (send, recv, bufs, tensors) in enumerate(self.flying):
            bufs = split_wait(f"reduce_wait_{si}", bufs, send, recv, (after,), _reduce_copies)
            n = len(tensors)
            for (name, slot), s, land in zip(tensors, bufs[:n], bufs[n:]):
                self.full[name] = _add_chips(s, land, self.place, self.full[name], slot)
        names = list(self.full)

        @_plan(0, len(names))
        def share_halves(ins, outs, x, y, c):
            sends = [(_half(o, 1, c), _half(o, 1, c), (x, y, 1 - c)) for o in outs]
            return [], sends, [_half(o, 1, 1 - c) for o in outs]

        shapes = [jax.ShapeDtypeStruct(self.full[k].shape, F32) for k in names]
        out = exchange("reduce_share_halves", [self.full[k] for k in names], shapes, {k: k for k in range(len(names))},
                       share_halves)
        return dict(zip(names, out))


def _add_halves(g, landed, core):
    _, rh, cols = landed.shape
    tr = _row_tile(rh, cols)
    per_half = rh // tr

    def body(core_ref, g_ref, la_ref, o_ref):
        del core_ref
        o_ref[...] = (g_ref[...].astype(F32) + la_ref[...].astype(F32)).astype(BF16)

    blk = (None, tr, cols)
    return pl.pallas_call(
        body, out_shape=jax.ShapeDtypeStruct(landed.shape, BF16),
        grid_spec=pltpu.PrefetchScalarGridSpec(
            num_scalar_prefetch=1, grid=(N_CHIPS, per_half),
            in_specs=[pl.BlockSpec(blk, lambda j, r, core_ref: (j, core_ref[0] * per_half + r, 0)),
                      pl.BlockSpec(blk, lambda j, r, core_ref: (j, r, 0))],
            out_specs=pl.BlockSpec(blk, lambda j, r, core_ref: (j, r, 0))),
        name="reduce_add_halves", compiler_params=_params(("parallel", "parallel")),
    )(core, g, landed)


def _add_chips(sums, landed, place, full, slot):
    _, rh, cols = sums.shape
    tr = _row_tile(rh, cols)
    per_half = rh // tr

    def body(place_ref, s_ref, la_ref, full_in, o_ref):
        del place_ref, full_in
        o_ref[...] = ((s_ref[...].astype(F32) + la_ref[0].astype(F32)) + la_ref[1].astype(F32)) + la_ref[2].astype(F32)

    return pl.pallas_call(
        body, out_shape=jax.ShapeDtypeStruct(full.shape, F32),
        grid_spec=pltpu.PrefetchScalarGridSpec(
            num_scalar_prefetch=1, grid=(per_half,),
            in_specs=[pl.BlockSpec((None, tr, cols), lambda r, place_ref: (place_ref[0], r, 0)),
                      pl.BlockSpec((3, tr, cols), lambda r, place_ref: (0, r, 0)), ANY],
            out_specs=pl.BlockSpec((None, tr, cols), lambda r, place_ref: (slot, place_ref[1] * per_half + r, 0))),
        input_output_aliases={3: 0}, name="reduce_add_chips", compiler_params=_params(("parallel",)),
    )(place, sums, landed, full)


def sum_devices(parts):
    def body(p_ref, o_ref):
        acc = p_ref[0]
        for d in range(1, N_DEV):
            acc = acc + p_ref[d]
        o_ref[...] = acc

    return pl.pallas_call(body, out_shape=jax.ShapeDtypeStruct(parts.shape[1:], F32), name="sum_devices")(parts)


ADA_SHARD = 9 * D_MODEL // N_CHIPS
ADA_TILE = 768
ADA_ROWS = 16


def ada_forward(c_rows, w_ada, b_shard):
    def body(c_ref, w_ref, b_ref, o_ref):
        cv = c_ref[...]
        o_ref[...] = _dot((cv * _sigmoid(cv)).astype(BF16), w_ref[...].astype(BF16), NN) + b_ref[...]

    return pl.pallas_call(
        body, grid=(DEPTH, ADA_SHARD // ADA_TILE),
        in_specs=[pl.BlockSpec((ADA_ROWS, D_MODEL), lambda l, n: (0, 0)),
                  pl.BlockSpec((None, D_MODEL, ADA_TILE), lambda l, n: (l, 0, n)),
                  pl.BlockSpec((None, 1, ADA_TILE), lambda l, n: (l, 0, n))],
        out_specs=pl.BlockSpec((None, ADA_ROWS, ADA_TILE), lambda l, n: (l, 0, n)),
        out_shape=jax.ShapeDtypeStruct((DEPTH, ADA_ROWS, ADA_SHARD), F32),
        name="ada_forward", compiler_params=_params(("parallel", "parallel")),
    )(c_rows, w_ada, b_shard)


def ada_backward(c_rows, dmod_rows):
    def body(c_ref, d_ref, o_ref):
        cv = c_ref[...]
        o_ref[...] = _dot((cv * _sigmoid(cv)).astype(BF16), d_ref[...].astype(BF16), TN)

    return pl.pallas_call(
        body, grid=(DEPTH, ADA_SHARD // ADA_TILE),
        in_specs=[pl.BlockSpec((ADA_ROWS, D_MODEL), lambda l, n: (0, 0)),
                  pl.BlockSpec((None, ADA_ROWS, ADA_TILE), lambda l, n: (l, 0, n))],
        out_specs=pl.BlockSpec((None, D_MODEL, ADA_TILE), lambda l, n: (l, 0, n)),
        out_shape=jax.ShapeDtypeStruct((DEPTH, D_MODEL, ADA_SHARD), F32),
        name="ada_backward", compiler_params=_params(("parallel", "parallel")),
    )(c_rows, dmod_rows)


def adamw(name, w, g, m, v):
    shape = w.shape
    cols = shape[-1]
    rows = w.size // cols
    tr = _row_tile(rows, cols) if rows % 16 == 0 else rows
    c1 = 1.0 / (1.0 - ADAM_B1 ** ADAM_STEP)
    c2 = 1.0 / (1.0 - ADAM_B2 ** ADAM_STEP)

    def body(w_ref, g_ref, m_ref, v_ref, d_ref, mo_ref, vo_ref):
        gv = g_ref[...]
        mn = ADAM_B1 * m_ref[...] + (1.0 - ADAM_B1) * gv
        vn = ADAM_B2 * v_ref[...] + (1.0 - ADAM_B2) * (gv * gv)
        mo_ref[...] = mn
        vo_ref[...] = vn
        d_ref[...] = -ADAM_LR * ((mn * c1) / (jnp.sqrt(vn * c2) + ADAM_EPS) + ADAM_WD * w_ref[...])

    spec = pl.BlockSpec((tr, cols), lambda i: (i, 0))
    out = jax.ShapeDtypeStruct((rows, cols), F32)
    res = pl.pallas_call(
        body, grid=(rows // tr,), in_specs=[spec] * 4, out_specs=[spec] * 3, out_shape=[out] * 3,
        name=name, compiler_params=_params(("parallel",)),
    )(*[t.reshape(rows, cols) for t in (w, g, m, v)])
    return tuple(r.reshape(shape) for r in res)


def _pack(parts, rows):
    flat = jnp.concatenate([p.reshape(-1) for p in parts])
    return jnp.pad(flat, (0, rows * 128 - flat.size)).reshape(rows, 128)


def _unpack(flat, shapes):
    out, at = [], 0
    for s in shapes:
        n = math.prod(s)
        out.append(flat[at:at + n].reshape(s))
        at += n
    return out


def kernel(x, c, w_ada, b_ada, norm_gain, w_ffn_gate, w_ffn_up, w_ffn_down, w_in, w_br_sb, w_br_dil, w_br_swa, w_out, sinks, rel_bias, final_gain, loss_target, m_w_ada, m_b_ada, m_norm_gain, m_w_ffn_gate, m_w_ffn_up, m_w_ffn_down, m_w_in, m_w_br_sb, m_w_br_dil, m_w_br_swa, m_w_out, m_sinks, m_rel_bias, m_final_gain, v_w_ada, v_b_ada, v_norm_gain, v_w_ffn_gate, v_w_ffn_up, v_w_ffn_down, v_w_in, v_w_br_sb, v_w_br_dil, v_w_br_swa, v_w_out, v_sinks, v_rel_bias, v_final_gain):
    xi, yi, ci = _position()
    chip = 2 * xi + yi
    dev = 2 * chip + ci

    c_all = all_gather_small("gather_c", c.reshape(8, 128)).reshape(N_DEV, D_MODEL)
    c_rows = jnp.pad(c_all, ((0, ADA_ROWS - N_DEV), (0, 0)))
    b_shard = lax.dynamic_slice_in_dim(b_ada, chip * ADA_SHARD, ADA_SHARD, axis=1).reshape(DEPTH, 1, ADA_SHARD)
    mod_shard = ada_forward(c_rows, w_ada, b_shard)[:, :N_DEV]
    n_mod = DEPTH * N_DEV * ADA_SHARD
    gathered = all_gather_small("gather_mod", _pack([mod_shard, norm_gain], 304))[::2].reshape(N_CHIPS, -1)
    mod_all = gathered[:, :n_mod].reshape(N_CHIPS, DEPTH, N_DEV, ADA_SHARD)
    mod = lax.dynamic_index_in_dim(mod_all, dev, axis=2, keepdims=False)
    mod = mod.transpose(1, 0, 2).reshape(DEPTH, 3, 3, D_MODEL)
    gains = gathered[:, n_mod:n_mod + DEPTH * 3 * D_SHARD].reshape(N_CHIPS, DEPTH, 3, D_SHARD)
    gains = gains.transpose(1, 2, 0, 3).reshape(DEPTH, 3, D_MODEL)

    chip_i, core_i = chip.astype(jnp.int32).reshape(1), ci.astype(jnp.int32).reshape(1)
    w_br = jnp.concatenate([w_br_sb, w_br_dil, w_br_swa], axis=1)
    shards = []
    for l in range(DEPTH):
        ffn = [[(w_ffn_gate, (l, f)), (w_ffn_up, (l, f)), (w_ffn_down, (l, f))] for f in range(2)]
        shards += [ffn[0], [(w_in, (l,)), (w_br, (l,)), (w_out, (l,))], ffn[1]]
    weights_in = WeightStream(shards, chip_i)
    grads_out = GradStream(chip_i, core_i)

    loss, dx, dmod, dgains, dfinal, dsinks, drel = device_step(
        x[0], loss_target[0], mod, gains, final_gain, sinks, rel_bias, weights_in.get, grads_out.put)

    small_shapes = [(DEPTH, 9 * D_MODEL), (DEPTH, 3, D_MODEL), (D_MODEL,), (DEPTH, H_SWA_Q), (N_BUCKETS, 12), (1,)]
    small_all = all_gather_small("gather_small_grads", _pack([dmod, dgains, dfinal, dsinks, drel, loss[0, 0:1]], 208))
    g_b_ada, g_gain_full, g_final, g_sinks, g_rel, loss_sum = _unpack(sum_devices(small_all).reshape(-1), small_shapes)
    g_gain = lax.dynamic_slice_in_dim(g_gain_full, chip * D_SHARD, D_SHARD, axis=2)
    dmod_all = small_all.reshape(N_DEV, -1)[:, :DEPTH * 9 * D_MODEL].reshape(N_DEV, DEPTH, 9 * D_MODEL)
    dmod_rows = lax.dynamic_slice_in_dim(dmod_all, chip * ADA_SHARD, ADA_SHARD, axis=2).transpose(1, 0, 2)
    g_w_ada = ada_backward(c_rows, jnp.pad(dmod_rows, ((0, 0), (0, ADA_ROWS - N_DEV), (0, 0))))

    g = grads_out.finish(dx)
    g_br = g["br"]
    grads = [
        g_w_ada, g_b_ada, g_gain, g["gate"].reshape(w_ffn_gate.shape), g["up"].reshape(w_ffn_up.shape),
        g["down"].reshape(w_ffn_down.shape), g["in"], g_br[:, 0:256], g_br[:, 256:384], g_br[:, 384:768], g["out"],
        g_sinks, g_rel, g_final]
    weights = [w_ada, b_ada, norm_gain, w_ffn_gate, w_ffn_up, w_ffn_down, w_in, w_br_sb, w_br_dil, w_br_swa, w_out,
               sinks, rel_bias, final_gain]
    ms = [m_w_ada, m_b_ada, m_norm_gain, m_w_ffn_gate, m_w_ffn_up, m_w_ffn_down, m_w_in, m_w_br_sb, m_w_br_dil,
          m_w_br_swa, m_w_out, m_sinks, m_rel_bias, m_final_gain]
    vs = [v_w_ada, v_b_ada, v_norm_gain, v_w_ffn_gate, v_w_ffn_up, v_w_ffn_down, v_w_in, v_w_br_sb, v_w_br_dil,
          v_w_br_swa, v_w_out, v_sinks, v_rel_bias, v_final_gain]

    small = (1, 2, 11, 12, 13)
    deltas, new_ms, new_vs = [None] * 14, [None] * 14, [None] * 14
    for k in range(14):
        if k not in small:
            deltas[k], new_ms[k], new_vs[k] = adamw(f"adamw_{k}", weights[k], grads[k], ms[k], vs[k])
    shapes = [weights[k].shape for k in small]
    packed = [_pack([t[k] for k in small], 168) for t in (weights, grads, ms, vs)]
    for dst, res in zip((deltas, new_ms, new_vs), adamw("adamw_small", *packed)):
        for k, t in zip(small, _unpack(res.reshape(-1), shapes)):
            dst[k] = t
    return (loss_sum[0], dx[None], *grads, *deltas, *new_ms, *new_vs)
```

```python
import functools
import math

import jax
import jax.numpy as jnp
from jax import lax
from jax.experimental import pallas as pl
from jax.experimental.pallas import tpu as pltpu

F32 = jnp.float32
BF16 = jnp.bfloat16

D_MODEL = 1024
SEQ = 2048
DEPTH = 2
HEAD_DIM = 64
BLK = 128
H_SB = 4
DIL_PATTERNS = ((128, 1), (512, 4), (2048, 16))
H_PER_DIL = 2
H_DIL = 6
H_SWA_Q = 6
H_SWA_KV = 2
SWA_WINDOW = 128
N_BUCKETS = 32
MAX_REL_DIST = 2048
D_FF = 2816
RMS_EPS = 1e-6
N_CHIPS = 4
N_DEV = 8
FF_SHARD = D_FF // N_CHIPS
D_QKV = 2560
D_IN = D_QKV + 3 * D_MODEL
IN_SHARD = D_IN // N_CHIPS
D_SHARD = D_MODEL // N_CHIPS
BR_ROWS = 768
NEG = -1e30
QK_SCALE = HEAD_DIM ** -0.5

ADAM_LR = 0.001
ADAM_B1 = 0.9
ADAM_B2 = 0.999
ADAM_EPS = 1e-08
ADAM_WD = 0.01
ADAM_STEP = 10

VMEM_LIMIT = 48 * 1024 * 1024
ROW_TILE = 256
MM_TILE = 512

NN = (((1,), (0,)), ((), ()))
NT = (((1,), (1,)), ((), ()))
TN = (((0,), (0,)), ((), ()))


def _params(sem=None):
    return pltpu.CompilerParams(dimension_semantics=sem, vmem_limit_bytes=VMEM_LIMIT)


def _dot(a, b, dims):
    return lax.dot_general(a, b, dims, preferred_element_type=F32)


def _sigmoid(x):
    return 1.0 / (1.0 + jnp.exp(-x))


def _matmul(name, grid, nk, k_axis, dims, n_pairs, in_specs, out_specs, out_shape, acc_shape, epilogue,
            operands, sem, aliases=None, prologue=None):
    n_in = len(in_specs)
    n_out = len(out_specs)

    def partial(ins):
        tot = None
        for p in range(n_pairs):
            a = ins[2 * p][...]
            if prologue is not None:
                a = prologue(p, a, ins)
            d = _dot(a, ins[2 * p + 1][...], dims)
            tot = d if tot is None else tot + d
        return tot

    def body(*refs):
        ins, outs = refs[:n_in], refs[n_in:n_in + n_out]
        ids = tuple(pl.program_id(a) for a in range(len(grid)))
        if nk == 1:
            epilogue(partial(ins), ins, outs, ids)
            return
        acc = refs[n_in + n_out]
        k = ids[k_axis]

        @pl.when(k == 0)
        def _():
            acc[...] = partial(ins)

        @pl.when(k > 0)
        def _():
            acc[...] += partial(ins)

        @pl.when(k == nk - 1)
        def _():
            epilogue(acc[...], ins, outs, ids)

    return pl.pallas_call(
        body, grid=grid, in_specs=in_specs, out_specs=out_specs, out_shape=out_shape,
        scratch_shapes=[] if nk == 1 else [pltpu.VMEM(acc_shape, F32)],
        input_output_aliases=aliases or {}, name=name, compiler_params=_params(sem),
    )(*operands)


def _row_spec(width=D_MODEL):
    return pl.BlockSpec((ROW_TILE, width), lambda i: (i, 0))


def _vec_spec(rows=1, width=D_MODEL):
    return pl.BlockSpec((rows, width), lambda i: (0, 0))


def prenorm(x, gain, scale, shift):
    def body(x_ref, g_ref, sc_ref, sh_ref, h_ref):
        xv = x_ref[...]
        r = lax.rsqrt(jnp.mean(xv * xv, axis=-1, keepdims=True) + RMS_EPS)
        h_ref[...] = (((xv * r) * g_ref[...]) * (1.0 + sc_ref[...]) + sh_ref[...]).astype(BF16)

    return pl.pallas_call(
        body, grid=(SEQ // ROW_TILE,), in_specs=[_row_spec(), _vec_spec(), _vec_spec(), _vec_spec()],
        out_specs=_row_spec(), out_shape=jax.ShapeDtypeStruct((SEQ, D_MODEL), BF16),
        name="prenorm", compiler_params=_params(("parallel",)),
    )(x, gain, scale, shift)


def resid_bwd(dxo, f, coef, mult):
    def body(dx_ref, f_ref, c_ref, df_ref, dc_ref):
        dx = dx_ref[...]
        df_ref[...] = (dx * (mult * c_ref[...])).astype(BF16)
        part = mult * jnp.sum(dx * f_ref[...], axis=0, keepdims=True)

        @pl.when(pl.program_id(0) == 0)
        def _():
            dc_ref[...] = jnp.zeros_like(dc_ref)

        dc_ref[0:1, :] += part

    return pl.pallas_call(
        body, grid=(SEQ // ROW_TILE,), in_specs=[_row_spec(), _row_spec(), _vec_spec()],
        out_specs=[_row_spec(), _vec_spec(8)],
        out_shape=[jax.ShapeDtypeStruct((SEQ, D_MODEL), BF16), jax.ShapeDtypeStruct((8, D_MODEL), F32)],
        name="resid_bwd", compiler_params=_params(("arbitrary",)),
    )(dxo, f, coef)


def final_loss(x, gain, target):
    def body(x_ref, g_ref, t_ref, loss_ref, dx_ref, dg_ref):
        xv = x_ref[...]
        g = g_ref[...]
        r = lax.rsqrt(jnp.mean(xv * xv, axis=-1, keepdims=True) + RMS_EPS)
        xh = xv * r
        e = xh * g - t_ref[...]
        part = 0.5 * jnp.sum(jnp.mean(e * e, axis=-1, keepdims=True), axis=0, keepdims=True)
        dy = e * (1.0 / D_MODEL)
        dyg = dy * g
        dx_ref[...] = r * (dyg - xh * jnp.mean(dyg * xh, axis=-1, keepdims=True))

        @pl.when(pl.program_id(0) == 0)
        def _():
            loss_ref[...] = jnp.zeros_like(loss_ref)
            dg_ref[...] = jnp.zeros_like(dg_ref)

        loss_ref[...] += jnp.broadcast_to(part, loss_ref.shape)
        dg_ref[0:1, :] += jnp.sum(dy * xh, axis=0, keepdims=True)

    return pl.pallas_call(
        body, grid=(SEQ // ROW_TILE,), in_specs=[_row_spec(), _vec_spec(), _row_spec()],
        out_specs=[_vec_spec(8, 128), _row_spec(), _vec_spec(8)],
        out_shape=[jax.ShapeDtypeStruct((8, 128), F32), jax.ShapeDtypeStruct((SEQ, D_MODEL), F32),
                   jax.ShapeDtypeStruct((8, D_MODEL), F32)],
        name="final_loss", compiler_params=_params(("arbitrary",)),
    )(x, gain, target)


def _prenorm_bwd_epilogue(dh, x_ref, dxo_ref, g_ref, sc_ref, dx_ref, stats_ref, first):
    xv = x_ref[...]
    g = g_ref[...]
    r = lax.rsqrt(jnp.mean(xv * xv, axis=-1, keepdims=True) + RMS_EPS)
    xh = xv * r
    dn = dh * (1.0 + sc_ref[...])
    dxh = dn * g
    dx_ref[...] = dxo_ref[...] + r * (dxh - xh * jnp.mean(dxh * xh, axis=-1, keepdims=True))

    @pl.when(first)
    def _():
        stats_ref[...] = jnp.zeros_like(stats_ref)

    stats_ref[0:1, :] += jnp.sum(dh, axis=0, keepdims=True)
    stats_ref[1:2, :] += jnp.sum(dh * (xh * g), axis=0, keepdims=True)
    stats_ref[2:3, :] += jnp.sum(dn * xh, axis=0, keepdims=True)


def ffn_up(h, wg_all, wu_all):
    def body(h_ref, wg_ref, wu_ref, a_ref, b_ref, s_ref):
        hv = h_ref[...]
        a = _dot(hv, wg_ref[...], NN)
        b = _dot(hv, wu_ref[...], NN)
        a_ref[...] = a
        b_ref[...] = b
        s_ref[...] = (a * _sigmoid(a) * b).astype(BF16)

    w_spec = pl.BlockSpec((None, D_MODEL, FF_SHARD), lambda j, i: (j, 0, 0))
    o_spec = pl.BlockSpec((None, MM_TILE, FF_SHARD), lambda j, i: (j, i, 0))
    hid = (N_CHIPS, SEQ, FF_SHARD)
    return pl.pallas_call(
        body, grid=(N_CHIPS, SEQ // MM_TILE),
        in_specs=[pl.BlockSpec((MM_TILE, D_MODEL), lambda j, i: (i, 0)), w_spec, w_spec],
        out_specs=[o_spec, o_spec, o_spec],
        out_shape=[jax.ShapeDtypeStruct(hid, F32), jax.ShapeDtypeStruct(hid, F32), jax.ShapeDtypeStruct(hid, BF16)],
        name="ffn_up", compiler_params=_params(("parallel", "parallel")),
    )(h, wg_all, wu_all)


def matmul_residual(name, a, a_spec, w_all, w_spec, x, coef, mult):
    def epilogue(acc, ins, outs, ids):
        outs[0][...] = acc
        outs[1][...] = ins[2][...] + (mult * ins[3][...]) * acc

    row = pl.BlockSpec((MM_TILE, D_MODEL), lambda i, j: (i, 0))
    return _matmul(
        name, (SEQ // MM_TILE, N_CHIPS), N_CHIPS, 1, NN, 1,
        [a_spec, w_spec, row, pl.BlockSpec((1, D_MODEL), lambda i, j: (0, 0))], [row, row],
        [jax.ShapeDtypeStruct((SEQ, D_MODEL), F32)] * 2, (MM_TILE, D_MODEL), epilogue,
        (a, w_all, x, coef), ("parallel", "arbitrary"))


def ffn_down(s, wd_all, x, gate):
    return matmul_residual(
        "ffn_down", s, pl.BlockSpec((None, MM_TILE, FF_SHARD), lambda i, j: (j, i, 0)),
        wd_all, pl.BlockSpec((None, FF_SHARD, D_MODEL), lambda i, j: (j, 0, 0)), x, gate, 0.5)


def ffn_bwd_hidden(df, wd_all, a, b):
    def epilogue(ds, ins, outs, ids):
        av, bv = ins[2][...], ins[3][...]
        sig = _sigmoid(av)
        outs[0][...] = (ds * bv * (sig * (1.0 + av * (1.0 - sig)))).astype(BF16)
        outs[1][...] = (ds * (av * sig)).astype(BF16)

    hid_spec = pl.BlockSpec((None, MM_TILE, FF_SHARD), lambda j, i: (j, i, 0))
    hid = jax.ShapeDtypeStruct((N_CHIPS, SEQ, FF_SHARD), BF16)
    return _matmul(
        "ffn_bwd_hidden", (N_CHIPS, SEQ // MM_TILE), 1, None, NT, 1,
        [pl.BlockSpec((MM_TILE, D_MODEL), lambda j, i: (i, 0)),
         pl.BlockSpec((None, FF_SHARD, D_MODEL), lambda j, i: (j, 0, 0)), hid_spec, hid_spec],
        [hid_spec, hid_spec], [hid, hid], None, epilogue, (df, wd_all, a, b), ("parallel", "parallel"))


def grad_weight(name, lhs, lhs_spec, rhs, rhs_spec, shape):
    def epilogue(acc, ins, outs, ids):
        outs[0][...] = acc.astype(BF16)

    return _matmul(
        name, (N_CHIPS, SEQ // MM_TILE), SEQ // MM_TILE, 1, TN, 1,
        [lhs_spec, rhs_spec], [pl.BlockSpec((None,) + shape, lambda j, k: (j, 0, 0))],
        [jax.ShapeDtypeStruct((N_CHIPS,) + shape, BF16)], shape, epilogue, (lhs, rhs), ("parallel", "arbitrary"))[0]


def ffn_grad_weights(h, s, df, da, db):
    tok = pl.BlockSpec((MM_TILE, D_MODEL), lambda j, k: (k, 0))
    hid = pl.BlockSpec((None, MM_TILE, FF_SHARD), lambda j, k: (j, k, 0))
    return (grad_weight("grad_w_gate", h, tok, da, hid, (D_MODEL, FF_SHARD)),
            grad_weight("grad_w_up", h, tok, db, hid, (D_MODEL, FF_SHARD)),
            grad_weight("grad_w_down", s, hid, df, tok, (FF_SHARD, D_MODEL)))


def matmul_prenorm_bwd(name, pairs, pair_specs, x, dxo, gain, scale):
    n = len(pairs)

    def epilogue(dh, ins, outs, ids):
        _prenorm_bwd_epilogue(dh, ins[n], ins[n + 1], ins[n + 2], ins[n + 3], outs[0], outs[1], ids[0] == 0)

    row = pl.BlockSpec((MM_TILE, D_MODEL), lambda i, j: (i, 0))
    vec = pl.BlockSpec((1, D_MODEL), lambda i, j: (0, 0))
    return _matmul(
        name, (SEQ // MM_TILE, N_CHIPS), N_CHIPS, 1, NT, len(pairs) // 2,
        list(pair_specs) + [row, row, vec, vec], [row, pl.BlockSpec((8, D_MODEL), lambda i, j: (0, 0))],
        [jax.ShapeDtypeStruct((SEQ, D_MODEL), F32), jax.ShapeDtypeStruct((8, D_MODEL), F32)],
        (MM_TILE, D_MODEL), epilogue, tuple(pairs) + (x, dxo, gain, scale), ("arbitrary", "arbitrary"))


def ffn_bwd_input(da, db, wg_all, wu_all, x, dxo, gain, scale):
    hid = pl.BlockSpec((None, MM_TILE, FF_SHARD), lambda i, j: (j, i, 0))
    w = pl.BlockSpec((None, D_MODEL, FF_SHARD), lambda i, j: (j, 0, 0))
    return matmul_prenorm_bwd("ffn_bwd_input", (da, wg_all, db, wu_all), (hid, w, hid, w), x, dxo, gain, scale)


def in_proj(h, w_all):
    def epilogue(acc, ins, outs, ids):
        outs[0][...] = acc

    return _matmul(
        "in_proj", (N_CHIPS, SEQ // MM_TILE), 1, None, NN, 1,
        [pl.BlockSpec((MM_TILE, D_MODEL), lambda j, i: (i, 0)),
         pl.BlockSpec((None, D_MODEL, IN_SHARD), lambda j, i: (j, 0, 0))],
        [pl.BlockSpec((MM_TILE, IN_SHARD), lambda j, i: (i, j))], [jax.ShapeDtypeStruct((SEQ, D_IN), F32)],
        None, epilogue, (h, w_all), ("parallel", "parallel"))[0]


_GATE_BLOCK0 = D_QKV // D_SHARD


def _branch_products(o, w_ref):
    ob = o.astype(BF16)
    return (_dot(ob[:, 0:256], w_ref[0:256, :], NN), _dot(ob[:, 256:384], w_ref[256:384, :], NN),
            _dot(ob[:, 384:768], w_ref[384:768, :], NN))


def merge_branches(o_cat, wbr_all, proj):
    def body(o_ref, w_ref, g0_ref, g1_ref, g2_ref, m_ref):
        u = _branch_products(o_ref[...], w_ref)
        m_ref[...] = (_sigmoid(g0_ref[...]) * u[0] + _sigmoid(g1_ref[...]) * u[1]
                      + _sigmoid(g2_ref[...]) * u[2]).astype(BF16)

    def gate_spec(b):
        return pl.BlockSpec((MM_TILE, D_SHARD), lambda i, j: (i, _GATE_BLOCK0 + 4 * b + j))

    return pl.pallas_call(
        body, grid=(SEQ // MM_TILE, N_CHIPS),
        in_specs=[pl.BlockSpec((MM_TILE, BR_ROWS), lambda i, j: (i, 0)),
                  pl.BlockSpec((None, BR_ROWS, D_SHARD), lambda i, j: (j, 0, 0)),
                  gate_spec(0), gate_spec(1), gate_spec(2)],
        out_specs=pl.BlockSpec((MM_TILE, D_SHARD), lambda i, j: (i, j)),
        out_shape=jax.ShapeDtypeStruct((SEQ, D_MODEL), BF16),
        name="merge_branches", compiler_params=_params(("parallel", "parallel")),
    )(o_cat, wbr_all, proj, proj, proj)


def out_proj(merged, wout_all, x, gate):
    return matmul_residual(
        "out_proj", merged, pl.BlockSpec((MM_TILE, D_SHARD), lambda i, j: (i, j)),
        wout_all, pl.BlockSpec((None, D_SHARD, D_MODEL), lambda i, j: (j, 0, 0)), x, gate, 1.0)


def merge_bwd(dmo, wout_all, o_cat, wbr_all, proj):
    def epilogue(dm, ins, outs, ids):
        u = _branch_products(ins[2][...], ins[3])
        for b in range(3):
            sig = _sigmoid(ins[4 + b][...])
            outs[b][...] = (dm * sig).astype(BF16)
            outs[3 + b][...] = (dm * u[b] * (sig * (1.0 - sig))).astype(BF16)

    def gate_spec(b):
        return pl.BlockSpec((MM_TILE, D_SHARD), lambda j, i: (i, _GATE_BLOCK0 + 4 * b + j))

    col = pl.BlockSpec((MM_TILE, D_SHARD), lambda j, i: (i, j))
    du = jax.ShapeDtypeStruct((SEQ, D_MODEL), BF16)
    return _matmul(
        "merge_bwd", (N_CHIPS, SEQ // MM_TILE), 1, None, NT, 1,
        [pl.BlockSpec((MM_TILE, D_MODEL), lambda j, i: (i, 0)),
         pl.BlockSpec((None, D_SHARD, D_MODEL), lambda j, i: (j, 0, 0)),
         pl.BlockSpec((MM_TILE, BR_ROWS), lambda j, i: (i, 0)),
         pl.BlockSpec((None, BR_ROWS, D_SHARD), lambda j, i: (j, 0, 0)),
         gate_spec(0), gate_spec(1), gate_spec(2)],
        [col] * 6, [du] * 6,
        None, epilogue, (dmo, wout_all, o_cat, wbr_all, proj, proj, proj), ("parallel", "parallel"))


def branch_bwd_input(du, wbr_all):
    def body(d0_ref, d1_ref, d2_ref, w_ref, o_ref, acc):
        j = pl.program_id(1)
        parts = (_dot(d0_ref[...], w_ref[0:256, :], NT), _dot(d1_ref[...], w_ref[256:384, :], NT),
                 _dot(d2_ref[...], w_ref[384:768, :], NT))

        @pl.when(j == 0)
        def _():
            acc[:, 0:256], acc[:, 256:384], acc[:, 384:768] = parts

        @pl.when(j > 0)
        def _():
            acc[:, 0:256] += parts[0]
            acc[:, 256:384] += parts[1]
            acc[:, 384:768] += parts[2]

        @pl.when(j == N_CHIPS - 1)
        def _():
            o_ref[...] = acc[...]

    col = pl.BlockSpec((MM_TILE, D_SHARD), lambda i, j: (i, j))
    return pl.pallas_call(
        body, grid=(SEQ // MM_TILE, N_CHIPS),
        in_specs=[col, col, col, pl.BlockSpec((None, BR_ROWS, D_SHARD), lambda i, j: (j, 0, 0))],
        out_specs=pl.BlockSpec((MM_TILE, BR_ROWS), lambda i, j: (i, 0)),
        out_shape=jax.ShapeDtypeStruct((SEQ, BR_ROWS), F32),
        scratch_shapes=[pltpu.VMEM((MM_TILE, BR_ROWS), F32)],
        name="branch_bwd_input", compiler_params=_params(("parallel", "arbitrary")),
    )(du[0], du[1], du[2], wbr_all)


def branch_grad_weights(o_cat, du):
    def body(o_ref, d0_ref, d1_ref, d2_ref, g_ref, acc):
        k = pl.program_id(1)
        ob = o_ref[...].astype(BF16)
        parts = (_dot(ob[:, 0:256], d0_ref[...], TN), _dot(ob[:, 256:384], d1_ref[...], TN),
                 _dot(ob[:, 384:768], d2_ref[...], TN))

        @pl.when(k == 0)
        def _():
            acc[0:256, :], acc[256:384, :], acc[384:768, :] = parts

        @pl.when(k > 0)
        def _():
            acc[0:256, :] += parts[0]
            acc[256:384, :] += parts[1]
            acc[384:768, :] += parts[2]

        @pl.when(k == SEQ // MM_TILE - 1)
        def _():
            g_ref[...] = acc[...].astype(BF16)

    col = pl.BlockSpec((MM_TILE, D_SHARD), lambda j, k: (k, j))
    return pl.pallas_call(
        body, grid=(N_CHIPS, SEQ // MM_TILE),
        in_specs=[pl.BlockSpec((MM_TILE, BR_ROWS), lambda j, k: (k, 0)), col, col, col],
        out_specs=pl.BlockSpec((None, BR_ROWS, D_SHARD), lambda j, k: (j, 0, 0)),
        out_shape=jax.ShapeDtypeStruct((N_CHIPS, BR_ROWS, D_SHARD), BF16),
        scratch_shapes=[pltpu.VMEM((BR_ROWS, D_SHARD), F32)],
        name="branch_grad_weights", compiler_params=_params(("parallel", "arbitrary")),
    )(o_cat, du[0], du[1], du[2])


def mixer_bwd_input(dproj, win_all, x, dxo, gain, scale):
    return matmul_prenorm_bwd(
        "mixer_bwd_input", (dproj, win_all),
        (pl.BlockSpec((MM_TILE, IN_SHARD), lambda i, j: (i, j)),
         pl.BlockSpec((None, D_MODEL, IN_SHARD), lambda i, j: (j, 0, 0))), x, dxo, gain, scale)


BATCH_QK = (((2,), (2,)), ((0,), (0,)))
BATCH_PV = (((2,), (1,)), ((0,), (0,)))
BATCH_TN = (((1,), (1,)), ((0,), (0,)))


def _split_dot(v, tri):
    v2 = v.reshape(v.shape[0] * BLK, BLK)
    hi = v2.astype(BF16)
    lo = (v2 - hi.astype(F32)).astype(BF16)
    return (_dot(hi, tri, NN) + _dot(lo, tri, NN)).reshape(v.shape)


def _tri(cmp):
    return cmp(lax.broadcasted_iota(jnp.int32, (BLK, BLK), 0), lax.broadcasted_iota(jnp.int32, (BLK, BLK), 1)).astype(BF16)


def _sb_scores(qs, k_ref, i, j):
    rows = pl.ds(pl.multiple_of(j * BLK, BLK), BLK)
    z = _dot(qs, k_ref[:, rows, :], BATCH_QK)
    tile = (1, BLK, BLK)
    before = (j * BLK + lax.broadcasted_iota(jnp.int32, tile, 2)) < (i * BLK + lax.broadcasted_iota(jnp.int32, tile, 1))
    soft = jnp.log(1.0 + jnp.exp(-jnp.abs(z)))
    log_fail = jnp.where(before, -(jnp.maximum(z, 0.0) + soft), 0.0)
    log_hit = jnp.minimum(z, 0.0) - soft
    return rows, before, log_fail, log_hit


def sb_forward(q, k, v):
    def body(q_ref, k_ref, v_ref, o_ref, tot_ref):
        i = pl.program_id(0)
        qs = q_ref[...]
        later = _tri(lambda r, c: r > c)

        def step(t, carry):
            o, run = carry
            rows, before, log_fail, log_hit = _sb_scores(qs, k_ref, i, i - t)
            between = _split_dot(log_fail, later) + run
            w = jnp.where(before, jnp.exp(log_hit + between), 0.0)
            o = o + _dot(w.astype(BF16), v_ref[:, rows, :], BATCH_PV)
            return o, run + jnp.sum(log_fail, axis=2, keepdims=True)

        o, run = lax.fori_loop(0, i + 1, step, (jnp.zeros((H_SB, BLK, HEAD_DIM), F32), jnp.zeros((H_SB, BLK, 1), F32)))
        o_ref[...] = o
        tot_ref[...] = run

    blk = pl.BlockSpec((H_SB, BLK, HEAD_DIM), lambda i: (0, i, 0))
    col = pl.BlockSpec((H_SB, BLK, 1), lambda i: (0, i, 0))
    full = pl.BlockSpec((H_SB, SEQ, HEAD_DIM), lambda i: (0, 0, 0))
    return pl.pallas_call(
        body, grid=(SEQ // BLK,), in_specs=[blk, full, full], out_specs=[blk, col],
        out_shape=[jax.ShapeDtypeStruct((H_SB, SEQ, HEAD_DIM), F32), jax.ShapeDtypeStruct((H_SB, SEQ, 1), F32)],
        name="sb_forward", compiler_params=_params(("parallel",)),
    )(q, k, v)


def sb_backward(q, k, v, total, do):
    def body(q_ref, k_ref, v_ref, tot_ref, do_ref, dq_ref, dk_ref, dv_ref):
        i = pl.program_id(0)

        @pl.when(i == 0)
        def _():
            dk_ref[...] = jnp.zeros_like(dk_ref)
            dv_ref[...] = jnp.zeros_like(dv_ref)

        qs = q_ref[...]
        dob = do_ref[...].astype(BF16)
        total_v = tot_ref[...]
        upto = _tri(lambda r, c: r <= c)
        earlier = _tri(lambda r, c: r < c)

        def step(j, carry):
            dq, seen, g_seen = carry
            rows, before, log_fail, log_hit = _sb_scores(qs, k_ref, i, j)
            between = total_v - (seen + _split_dot(log_fail, upto))
            w = jnp.where(before, jnp.exp(log_hit + between), 0.0)
            g = _dot(dob, v_ref[:, rows, :], BATCH_QK) * w
            g_earlier = g_seen + _split_dot(g, earlier)
            sig = jnp.exp(log_hit)
            dz = jnp.where(before, g * (1.0 - sig) - g_earlier * sig, 0.0).astype(BF16)
            dq = dq + _dot(dz, k_ref[:, rows, :], BATCH_PV)
            dk_ref[:, rows, :] += _dot(dz, qs, BATCH_TN)
            dv_ref[:, rows, :] += _dot(w.astype(BF16), dob, BATCH_TN)
            return dq, seen + jnp.sum(log_fail, axis=2, keepdims=True), g_seen + jnp.sum(g, axis=2, keepdims=True)

        zero = jnp.zeros((H_SB, BLK, 1), F32)
        dq, _, _ = lax.fori_loop(0, i + 1, step, (jnp.zeros((H_SB, BLK, HEAD_DIM), F32), zero, zero))
        dq_ref[...] = dq * QK_SCALE

    blk = pl.BlockSpec((H_SB, BLK, HEAD_DIM), lambda i: (0, i, 0))
    col = pl.BlockSpec((H_SB, BLK, 1), lambda i: (0, i, 0))
    full = pl.BlockSpec((H_SB, SEQ, HEAD_DIM), lambda i: (0, 0, 0))
    shape = jax.ShapeDtypeStruct((H_SB, SEQ, HEAD_DIM), F32)
    return pl.pallas_call(
        body, grid=(SEQ // BLK,), in_specs=[blk, full, full, col, blk], out_specs=[blk, full, full],
        out_shape=[shape, shape, shape],
        name="sb_backward", compiler_params=_params(("arbitrary",)),
    )(q, k, v, total, do)


def _band_scores(q_ref, kp_ref, ko_ref, bias_ref, hb, prev_mask):
    b = pl.program_id(1)
    qs = q_ref[...]
    s_prev = _dot(qs, kp_ref[...], BATCH_QK) + bias_ref[:, :, 0:BLK]
    s_prev = jnp.concatenate(
        [jnp.where((b & prev_mask(pl.program_id(0) * hb + t)) != 0, s_prev[t:t + 1], NEG) for t in range(hb)], axis=0)
    s_own = _dot(qs, ko_ref[...], BATCH_QK) + bias_ref[:, :, BLK:2 * BLK]
    return qs, s_prev, s_own


def _band_specs(hb, rows, t_n):
    def q_spec(width):
        return pl.BlockSpec((hb, None, rows, width), lambda h, b: (h, b, 0, 0))

    own = pl.BlockSpec((hb, BLK, HEAD_DIM), lambda h, b: (h, b, 0))
    prev = pl.BlockSpec((hb, BLK, HEAD_DIM), lambda h, b: (h, jnp.maximum(b - 1, 0), 0))
    per_head = lambda r, width: pl.BlockSpec((hb, r, width), lambda h, b: (h, 0, 0))
    return q_spec, own, prev, per_head


def banded_forward(name, q, k, v, bias, sinks, hb, prev_mask):
    h_n, nb, rows, _ = q.shape

    def body(q_ref, kp_ref, ko_ref, vp_ref, vo_ref, bias_ref, sink_ref, o_ref, lse_ref):
        _, s_prev, s_own = _band_scores(q_ref, kp_ref, ko_ref, bias_ref, hb, prev_mask)
        sink = sink_ref[...]
        m = jnp.maximum(jnp.maximum(jnp.max(s_prev, axis=2, keepdims=True), jnp.max(s_own, axis=2, keepdims=True)), sink)
        p_prev = jnp.exp(s_prev - m)
        p_own = jnp.exp(s_own - m)
        denom = jnp.sum(p_prev, axis=2, keepdims=True) + jnp.sum(p_own, axis=2, keepdims=True) + jnp.exp(sink - m)
        o = _dot(p_prev.astype(BF16), vp_ref[...], BATCH_PV) + _dot(p_own.astype(BF16), vo_ref[...], BATCH_PV)
        o_ref[...] = o / denom
        lse_ref[...] = m + jnp.log(denom)

    q_spec, own, prev, per_head = _band_specs(hb, rows, k.shape[1])
    return pl.pallas_call(
        body, grid=(h_n // hb, nb),
        in_specs=[q_spec(HEAD_DIM), prev, own, prev, own, per_head(rows, 2 * BLK), per_head(rows, 1)],
        out_specs=[q_spec(HEAD_DIM), q_spec(1)],
        out_shape=[jax.ShapeDtypeStruct(q.shape, F32), jax.ShapeDtypeStruct((h_n, nb, rows, 1), F32)],
        name=name, compiler_params=_params(("parallel", "parallel")),
    )(q, k, k, v, v, bias, sinks)


def banded_backward(name, q, k, v, bias, sinks, o, lse, do, dlse, hb, prev_mask):
    h_n, nb, rows, _ = q.shape
    t_n = k.shape[1]

    def body(q_ref, kp_ref, ko_ref, vp_ref, vo_ref, bias_ref, sink_ref, o_ref, lse_ref, do_ref, dlse_ref,
             dq_ref, dk_ref, dv_ref, dbias_ref, dsink_ref):
        b = pl.program_id(1)

        @pl.when(b == 0)
        def _():
            dk_ref[...] = jnp.zeros_like(dk_ref)
            dv_ref[...] = jnp.zeros_like(dv_ref)
            dbias_ref[...] = jnp.zeros_like(dbias_ref)
            dsink_ref[...] = jnp.zeros_like(dsink_ref)

        qs, s_prev, s_own = _band_scores(q_ref, kp_ref, ko_ref, bias_ref, hb, prev_mask)
        lse_v = lse_ref[...]
        dov = do_ref[...]
        dob = dov.astype(BF16)
        shift = dlse_ref[...] - jnp.sum(dov * o_ref[...], axis=2, keepdims=True)
        p_prev = jnp.exp(s_prev - lse_v)
        p_own = jnp.exp(s_own - lse_v)
        ds_prev = p_prev * (_dot(dob, vp_ref[...], BATCH_QK) + shift)
        ds_own = p_own * (_dot(dob, vo_ref[...], BATCH_QK) + shift)
        dbias_ref[:, :, 0:BLK] += ds_prev
        dbias_ref[:, :, BLK:2 * BLK] += ds_own
        d_sink = jnp.exp(sink_ref[...] - lse_v) * shift
        for g in range(rows // BLK):
            dsink_ref[:, g:g + 1, :] += jnp.sum(d_sink[:, g * BLK:(g + 1) * BLK, :], axis=1, keepdims=True)
        ds_prev = ds_prev.astype(BF16)
        ds_own = ds_own.astype(BF16)
        dq_ref[...] = (_dot(ds_prev, kp_ref[...], BATCH_PV) + _dot(ds_own, ko_ref[...], BATCH_PV)) * QK_SCALE
        rows_prev = pl.ds(pl.multiple_of(jnp.maximum(b - 1, 0) * BLK, BLK), BLK)
        rows_own = pl.ds(pl.multiple_of(b * BLK, BLK), BLK)
        dk_ref[:, rows_prev, :] += _dot(ds_prev, qs, BATCH_TN)
        dk_ref[:, rows_own, :] += _dot(ds_own, qs, BATCH_TN)
        dv_ref[:, rows_prev, :] += _dot(p_prev.astype(BF16), dob, BATCH_TN)
        dv_ref[:, rows_own, :] += _dot(p_own.astype(BF16), dob, BATCH_TN)

    q_spec, own, prev, per_head = _band_specs(hb, rows, t_n)
    kv_full = per_head(t_n, HEAD_DIM)
    kv_shape = jax.ShapeDtypeStruct((h_n, t_n, HEAD_DIM), F32)
    return pl.pallas_call(
        body, grid=(h_n // hb, nb),
        in_specs=[q_spec(HEAD_DIM), prev, own, prev, own, per_head(rows, 2 * BLK), per_head(rows, 1),
                  q_spec(HEAD_DIM), q_spec(1), q_spec(HEAD_DIM), q_spec(1)],
        out_specs=[q_spec(HEAD_DIM), kv_full, kv_full, per_head(rows, 2 * BLK), per_head(rows // BLK, BLK)],
        out_shape=[jax.ShapeDtypeStruct(q.shape, F32), kv_shape, kv_shape,
                   jax.ShapeDtypeStruct((h_n, rows, 2 * BLK), F32), jax.ShapeDtypeStruct((h_n, rows // BLK, BLK), F32)],
        name=name, compiler_params=_params(("parallel", "arbitrary")),
    )(q, k, k, v, v, bias, sinks, o, lse, do, dlse)


def _dil_prev_mask(head):
    group = head // H_PER_DIL
    return jnp.where(group == 0, 15, jnp.where(group == 1, 3, 0))


def _swa_prev_mask(head):
    del head
    return 15


DIL_HEADS_PER_STEP = 3
SWA_GROUP = H_SWA_Q // H_SWA_KV
N_BLK = SEQ // BLK


def dilated_merge(o, lse):
    def body(o_ref, l_ref, out_ref):
        lv = l_ref[...]
        m = jnp.max(lv, axis=0, keepdims=True)
        e = jnp.exp(lv - m)
        alpha = e / jnp.sum(e, axis=0, keepdims=True)
        out_ref[...] = jnp.sum(alpha * o_ref[...], axis=0)

    return pl.pallas_call(
        body, grid=(H_PER_DIL, SEQ // ROW_TILE),
        in_specs=[pl.BlockSpec((3, None, ROW_TILE, HEAD_DIM), lambda h, i: (0, h, i, 0)),
                  pl.BlockSpec((3, None, ROW_TILE, 1), lambda h, i: (0, h, i, 0))],
        out_specs=pl.BlockSpec((None, ROW_TILE, HEAD_DIM), lambda h, i: (h, i, 0)),
        out_shape=jax.ShapeDtypeStruct((H_PER_DIL, SEQ, HEAD_DIM), F32),
        name="dilated_merge", compiler_params=_params(("parallel", "parallel")),
    )(o, lse)


def dilated_merge_bwd(o, lse, dout):
    def body(o_ref, l_ref, d_ref, do_ref, dl_ref):
        lv = l_ref[...]
        m = jnp.max(lv, axis=0, keepdims=True)
        e = jnp.exp(lv - m)
        alpha = e / jnp.sum(e, axis=0, keepdims=True)
        dv = d_ref[...][None]
        do_ref[...] = alpha * dv
        dalpha = jnp.sum(dv * o_ref[...], axis=-1, keepdims=True)
        dl_ref[...] = alpha * (dalpha - jnp.sum(alpha * dalpha, axis=0, keepdims=True))

    o_spec = pl.BlockSpec((3, None, ROW_TILE, HEAD_DIM), lambda h, i: (0, h, i, 0))
    l_spec = pl.BlockSpec((3, None, ROW_TILE, 1), lambda h, i: (0, h, i, 0))
    return pl.pallas_call(
        body, grid=(H_PER_DIL, SEQ // ROW_TILE),
        in_specs=[o_spec, l_spec, pl.BlockSpec((None, ROW_TILE, HEAD_DIM), lambda h, i: (h, i, 0))],
        out_specs=[o_spec, l_spec],
        out_shape=[jax.ShapeDtypeStruct(o.shape, F32), jax.ShapeDtypeStruct(lse.shape, F32)],
        name="dilated_merge_bwd", compiler_params=_params(("parallel", "parallel")),
    )(o, lse, dout)


def rel_bias_reduce(dbias0, dbias1, bucket):
    def body(d0_ref, d1_ref, b_ref, o_ref):
        dv, bv = d0_ref[...] + d1_ref[...], b_ref[...]
        lane = lax.broadcasted_iota(jnp.int32, (1, BLK), 1)
        acc = jnp.zeros((1, BLK), F32)
        for bkt in range(N_BUCKETS):
            acc = acc + jnp.where(lane == bkt, jnp.sum(jnp.where(bv == bkt, dv, 0.0)), 0.0)
        o_ref[...] = acc

    tile = pl.BlockSpec((None, BLK, 2 * BLK), lambda h: (h, 0, 0))
    return pl.pallas_call(
        body, grid=(dbias0.shape[0],), in_specs=[tile, tile, tile],
        out_specs=pl.BlockSpec((None, 1, BLK), lambda h: (h, 0, 0)),
        out_shape=jax.ShapeDtypeStruct((dbias0.shape[0], 1, BLK), F32),
        name="rel_bias_reduce", compiler_params=_params(("parallel",)),
    )(dbias0, dbias1, bucket)


def _heads(t):
    return t.reshape(SEQ, -1, HEAD_DIM).transpose(1, 0, 2)


def _unheads(t):
    return t.transpose(1, 0, 2).reshape(SEQ, -1)


def _dilate(t):
    parts = []
    for g, (_, d) in enumerate(DIL_PATTERNS):
        tg = t[:, 128 * g:128 * (g + 1)].reshape(SEQ // d, d, H_PER_DIL, HEAD_DIM).transpose(2, 1, 0, 3)
        parts.append(tg.reshape(H_PER_DIL, SEQ, HEAD_DIM))
    return jnp.concatenate(parts, axis=0)


def _undilate(t):
    outs = []
    for g, (_, d) in enumerate(DIL_PATTERNS):
        tg = t[2 * g:2 * g + 2].reshape(H_PER_DIL, d, SEQ // d, -1).transpose(0, 2, 1, 3)
        outs.append(tg.reshape(H_PER_DIL, SEQ, -1))
    return jnp.stack(outs)


def _redilate(t):
    parts = []
    for g, (_, d) in enumerate(DIL_PATTERNS):
        tg = t[g].reshape(H_PER_DIL, SEQ // d, d, -1).transpose(0, 2, 1, 3)
        parts.append(tg.reshape(H_PER_DIL, SEQ, -1))
    return jnp.concatenate(parts, axis=0)


def _t5_bucket(n):
    max_exact = N_BUCKETS // 2
    nf = jnp.maximum(n, 1).astype(F32)
    large = max_exact + (jnp.log(nf / max_exact) / math.log(MAX_REL_DIST / max_exact)
                         * (N_BUCKETS - max_exact)).astype(jnp.int32)
    large = jnp.minimum(large, N_BUCKETS - 1)
    return jnp.where(n < max_exact, n, large)


def band_tables(rel_bias):
    rel = jnp.arange(BLK)[:, None] + BLK - jnp.arange(2 * BLK)[None, :]
    buckets = []
    patterns = [(d, w // d) for w, d in DIL_PATTERNS for _ in range(H_PER_DIL)] + [(1, SWA_WINDOW - 1)] * H_SWA_Q
    for d, max_dist in patterns:
        band = (rel >= 0) & (rel <= max_dist)
        buckets.append(jnp.where(band, _t5_bucket(jnp.maximum(rel, 0) * d), -1))
    buckets = jnp.stack(buckets).astype(jnp.int32)

    def body(table_ref, b_ref, o_ref):
        h = pl.program_id(0)
        bv = b_ref[...]
        tile = jnp.full(bv.shape, NEG, F32)
        for bkt in range(N_BUCKETS):
            tile = jnp.where(bv == bkt, table_ref[h, bkt], tile)
        o_ref[...] = tile

    spec = pl.BlockSpec((None, BLK, 2 * BLK), lambda h: (h, 0, 0))
    tiles = pl.pallas_call(
        body, grid=(len(patterns),), in_specs=[pl.BlockSpec(memory_space=pltpu.SMEM), spec], out_specs=spec,
        out_shape=jax.ShapeDtypeStruct(buckets.shape, F32), name="band_tables", compiler_params=_params(("parallel",)),
    )(rel_bias.T, buckets)
    return tiles[:H_DIL], tiles[H_DIL:], buckets


def _swa_rows(t):
    t = t.reshape(N_BLK, BLK, H_SWA_KV, SWA_GROUP, HEAD_DIM).transpose(2, 0, 3, 1, 4)
    return t.reshape(H_SWA_KV, N_BLK, SWA_GROUP * BLK, HEAD_DIM)


def _swa_tokens(t):
    t = t.reshape(H_SWA_KV, N_BLK, SWA_GROUP, BLK, HEAD_DIM).transpose(1, 3, 0, 2, 4)
    return t.reshape(SEQ, H_SWA_Q * HEAD_DIM)


def _sink_rows(sinks):
    return jnp.broadcast_to(sinks.reshape(H_SWA_KV, SWA_GROUP, 1, 1), (H_SWA_KV, SWA_GROUP, BLK, 1)).reshape(
        H_SWA_KV, SWA_GROUP * BLK, 1)


def _no_sinks():
    return jnp.full((H_DIL, BLK, 1), NEG, F32)


def _vec(v):
    return v.reshape(1, D_MODEL)


def ffn_forward(x, gain, mod, w):
    h = prenorm(x, _vec(gain), _vec(mod[1]), _vec(mod[0]))
    a, b, s = ffn_up(h, w[0], w[1])
    f, xo = ffn_down(s, w[2], x, _vec(mod[2]))
    return xo, (x, h, a, b, s, f)


def ffn_backward(dxo, saved, gain, mod, w):
    x, h, a, b, s, f = saved
    df, dgate = resid_bwd(dxo, f, _vec(mod[2]), 0.5)
    da, db = ffn_bwd_hidden(df, w[2], a, b)
    grads = ffn_grad_weights(h, s, df, da, db)
    dx, stats = ffn_bwd_input(da, db, w[0], w[1], x, dxo, _vec(gain), _vec(mod[1]))
    return dx, jnp.stack([stats[0], stats[1], dgate[0]]), stats[2], grads


def mixer_forward(x, gain, mod, sinks, bias_dil, bias_swa, w):
    h = prenorm(x, _vec(gain), _vec(mod[1]), _vec(mod[0]))
    proj = in_proj(h, w[0])
    qkv = proj[:, :D_QKV].astype(BF16)
    q_sb, k_sb, v_sb = _heads(qkv[:, 0:256] * QK_SCALE), _heads(qkv[:, 256:512]), _heads(qkv[:, 512:768])
    q_dil, k_dil, v_dil = _dilate(qkv[:, 768:1152] * QK_SCALE), _dilate(qkv[:, 1152:1536]), _dilate(qkv[:, 1536:1920])
    q_swa, k_swa, v_swa = _swa_rows(qkv[:, 1920:2304] * QK_SCALE), _heads(qkv[:, 2304:2432]), _heads(qkv[:, 2432:2560])
    o_sb, total_sb = sb_forward(q_sb, k_sb, v_sb)
    q_dil = q_dil.reshape(H_DIL, N_BLK, BLK, HEAD_DIM)
    o_dd, lse_dd = banded_forward("dilated_forward", q_dil, k_dil, v_dil, bias_dil, _no_sinks(), DIL_HEADS_PER_STEP,
                                  _dil_prev_mask)
    o_dt, lse_dt = _undilate(o_dd.reshape(H_DIL, SEQ, HEAD_DIM)), _undilate(lse_dd.reshape(H_DIL, SEQ, 1))
    o_dil = dilated_merge(o_dt, lse_dt)
    bias_swa = bias_swa.reshape(H_SWA_KV, SWA_GROUP * BLK, 2 * BLK)
    o_swa, lse_swa = banded_forward("swa_forward", q_swa, k_swa, v_swa, bias_swa, _sink_rows(sinks), 1, _swa_prev_mask)
    o_cat = jnp.concatenate([_unheads(o_sb), _unheads(o_dil), _swa_tokens(o_swa)], axis=1)
    merged = merge_branches(o_cat, w[1], proj)
    mo, xo = out_proj(merged, w[2], x, _vec(mod[2]))
    saved = (x, h, proj, (q_sb, k_sb, v_sb, total_sb), (q_dil, k_dil, v_dil, o_dd, lse_dd, o_dt, lse_dt),
             (q_swa, k_swa, v_swa, o_swa, lse_swa), o_cat, merged, mo)
    return xo, saved


def mixer_backward(dxo, saved, gain, mod, sinks, bias_dil, bias_swa, w):
    x, h, proj, sb, dil, swa, o_cat, merged, mo = saved
    dmo, dgate = resid_bwd(dxo, mo, _vec(mod[2]), 1.0)
    tok = pl.BlockSpec((MM_TILE, D_MODEL), lambda j, k: (k, 0))
    g_out = grad_weight("grad_w_out", merged, pl.BlockSpec((MM_TILE, D_SHARD), lambda j, k: (k, j)), dmo, tok,
                        (D_SHARD, D_MODEL))
    du0, du1, du2, dg0, dg1, dg2 = merge_bwd(dmo, w[2], o_cat, w[1], proj)
    du = (du0, du1, du2)
    do_cat = branch_bwd_input(du, w[1])
    g_br = branch_grad_weights(o_cat, du)

    q_sb, k_sb, v_sb, total_sb = sb
    dq_sb, dk_sb, dv_sb = sb_backward(q_sb, k_sb, v_sb, total_sb, _heads(do_cat[:, 0:256]))

    q_dil, k_dil, v_dil, o_dd, lse_dd, o_dt, lse_dt = dil
    do_dt, dlse_dt = dilated_merge_bwd(o_dt, lse_dt, _heads(do_cat[:, 256:384]))
    dq_dil, dk_dil, dv_dil, dbias_dil, _ = banded_backward(
        "dilated_backward", q_dil, k_dil, v_dil, bias_dil, _no_sinks(), o_dd, lse_dd,
        _redilate(do_dt).reshape(q_dil.shape), _redilate(dlse_dt).reshape(lse_dd.shape), DIL_HEADS_PER_STEP, _dil_prev_mask)

    q_swa, k_swa, v_swa, o_swa, lse_swa = swa
    bias_swa = bias_swa.reshape(H_SWA_KV, SWA_GROUP * BLK, 2 * BLK)
    dq_swa, dk_swa, dv_swa, dbias_swa, dsinks = banded_backward(
        "swa_backward", q_swa, k_swa, v_swa, bias_swa, _sink_rows(sinks), o_swa, lse_swa, _swa_rows(do_cat[:, 384:768]),
        jnp.zeros_like(lse_swa), 1, _swa_prev_mask)
    dbias_swa = dbias_swa.reshape(H_SWA_Q, BLK, 2 * BLK)

    def tokens(t):
        return _undilate(t).transpose(2, 0, 1, 3).reshape(SEQ, -1)

    dproj = jnp.concatenate(
        [_unheads(dq_sb), _unheads(dk_sb), _unheads(dv_sb), tokens(dq_dil.reshape(H_DIL, SEQ, HEAD_DIM)), tokens(dk_dil),
         tokens(dv_dil), _swa_tokens(dq_swa), _unheads(dk_swa), _unheads(dv_swa)], axis=1).astype(BF16)
    dproj = jnp.concatenate([dproj, dg0, dg1, dg2], axis=1)
    g_in = grad_weight("grad_w_in", h, tok, dproj, pl.BlockSpec((MM_TILE, IN_SHARD), lambda j, k: (k, j)),
                       (D_MODEL, IN_SHARD))
    dx, stats = mixer_bwd_input(dproj, w[0], x, dxo, _vec(gain), _vec(mod[1]))
    dmod = jnp.stack([stats[0], stats[1], dgate[0]])
    dbias = jnp.concatenate([dbias_dil, dbias_swa], axis=0)
    return dx, dmod, stats[2], dbias, dsinks[:, :, 0].reshape(H_SWA_Q), (g_in, g_br, g_out)


N_UNITS = 3 * DEPTH


def device_step(x, target, mod, gains, final_gain, sinks, rel_bias, get_weights, put_grads):
    bias_dil, bias_swa, bucket = band_tables(rel_bias)
    saved, weights = [], []
    for u in range(N_UNITS):
        l, j = divmod(u, 3)
        w = get_weights(u, x)
        if j == 1:
            x, s = mixer_forward(x, gains[l, 1], mod[l, 1], sinks[l], bias_dil, bias_swa, w)
        else:
            x, s = ffn_forward(x, gains[l, j], mod[l, j], w)
        saved.append(s)
        weights.append(w)
    loss, dx, dfinal = final_loss(x, _vec(final_gain), target)

    dmod = [[None] * 3 for _ in range(DEPTH)]
    dgains = [[None] * 3 for _ in range(DEPTH)]
    dbias, dsinks = [None] * DEPTH, [None] * DEPTH
    zero = jnp.zeros((1, 1), F32)
    for u in reversed(range(N_UNITS)):
        l, j = divmod(u, 3)
        gain = gains[l, j] + zero[0]
        if j == 1:
            dx, dmod[l][j], dgains[l][j], dbias[l], dsinks[l], grads = mixer_backward(
                dx, saved[u], gain, mod[l, 1], sinks[l], bias_dil, bias_swa, weights[u])
        else:
            dx, dmod[l][j], dgains[l][j], grads = ffn_backward(dx, saved[u], gain, mod[l, j], weights[u])
        zero = put_grads(u, grads)
    drel = rel_bias_reduce(dbias[0], dbias[1], bucket)[:, 0, :N_BUCKETS].T
    stack2 = lambda t: jnp.stack([jnp.stack(r) for r in t])
    return loss, dx, stack2(dmod), stack2(dgains), dfinal[0], jnp.stack(dsinks), drel


MESH = pl.DeviceIdType.MESH
CHIP_FLIPS = ((1, 0), (0, 1), (1, 1))
ANY = pl.BlockSpec(memory_space=pl.ANY)


def _position():
    return lax.axis_index("x"), lax.axis_index("y"), lax.axis_index("c")


def all_gather_small(name, piece):
    def body(x_ref, out_ref, send_sems, recv_sems, local_sem):
        x, y, c = _position()
        me, sibling = (x, y, c), (x, y, 1 - c)
        chips = [(x ^ fx, y ^ fy) for fx, fy in CHIP_FLIPS]

        def rows(px, py, pc):
            return out_ref.at[4 * px + 2 * py + pc]

        def copy(k, block, to, src=None):
            return pltpu.make_async_remote_copy(
                src_ref=rows(*block) if src is None else src, dst_ref=rows(*block),
                send_sem=send_sems.at[k], recv_sem=recv_sems.at[k], device_id=to, device_id_type=MESH)

        mine = pltpu.make_async_copy(x_ref, rows(*me), local_sem)
        mine.start()
        first = [copy(0, me, sibling, src=x_ref)]
        first += [copy(1 + j, me, (*chip, c), src=x_ref) for j, chip in enumerate(chips)]
        for cp in first:
            cp.start()
        passed = [copy(4 + j, (*chip, c), sibling) for j, chip in enumerate(chips)]
        for j, chip in enumerate(chips):
            copy(1 + j, (*chip, c), me).wait_recv()
            passed[j].start()
        copy(0, sibling, me).wait_recv()
        for j, chip in enumerate(chips):
            copy(4 + j, (*chip, 1 - c), me).wait_recv()
        for cp in first + passed:
            cp.wait_send()
        mine.wait()

    return pl.pallas_call(
        body, out_shape=jax.ShapeDtypeStruct((N_DEV,) + piece.shape, piece.dtype),
        in_specs=[pl.BlockSpec(memory_space=pltpu.VMEM)], out_specs=pl.BlockSpec(memory_space=pltpu.VMEM),
        scratch_shapes=[pltpu.SemaphoreType.DMA((7,)), pltpu.SemaphoreType.DMA((7,)), pltpu.SemaphoreType.DMA],
        name=name,
    )(piece)


def exchange(name, operands, out_shapes, aliases, plan):
    n_in, n_out = len(operands), len(out_shapes)

    def body(*refs):
        ins, outs = refs[:n_in], refs[n_in:n_in + n_out]
        send_sems, recv_sems, local_sems = refs[n_in + n_out:]
        x, y, c = _position()
        local, sends, recvs = plan(ins, outs, x, y, c)
        local = [pltpu.make_async_copy(s, d, local_sems.at[k]) for k, (s, d) in enumerate(local)]
        for cp in local:
            cp.start()
        remote = [pltpu.make_async_remote_copy(src_ref=s, dst_ref=d, send_sem=send_sems.at[k], recv_sem=recv_sems.at[k],
                                               device_id=dev, device_id_type=MESH)
                  for k, (s, d, dev) in enumerate(sends)]
        for cp in remote:
            cp.start()
        for k, r in enumerate(recvs):
            pltpu.make_async_remote_copy(src_ref=r, dst_ref=r, send_sem=send_sems.at[k], recv_sem=recv_sems.at[k],
                                         device_id=(x, y, c), device_id_type=MESH).wait_recv()
        for cp in remote:
            cp.wait_send()
        for cp in local:
            cp.wait()

    n_sends, n_local = plan.n_sends, max(plan.n_local, 1)
    return pl.pallas_call(
        body, out_shape=out_shapes, in_specs=[ANY] * n_in, out_specs=[ANY] * n_out,
        scratch_shapes=[pltpu.SemaphoreType.DMA((n_sends,)), pltpu.SemaphoreType.DMA((n_sends,)),
                        pltpu.SemaphoreType.DMA((n_local,))],
        input_output_aliases=aliases, name=name,
    )(*operands)


def _plan(n_local, n_sends):
    def wrap(fn):
        fn.n_local, fn.n_sends = n_local, n_sends
        return fn
    return wrap


def _half(ref, axis, c):
    rows = ref.shape[axis] // 2
    idx = [slice(None)] * len(ref.shape)
    idx[axis] = pl.ds(pl.multiple_of(c * rows, 16), rows)
    return ref.at[tuple(idx)]


HBM = pl.BlockSpec(memory_space=pltpu.HBM)
SEM = pl.BlockSpec(memory_space=pltpu.SEMAPHORE)
EFFECT = pltpu.SideEffectType.DATAFLOW_SIDE_EFFECTING


def split_start(name, bufs, extra, n_copies, describe):
    n = len(bufs)

    def body(*refs):
        send_sems, recv_sems = refs[n + len(extra)], refs[n + len(extra) + 1]
        x, y, c = _position()
        for k, (src, dst, _, peer) in enumerate(describe(refs[:n], x, y, c)):
            pltpu.make_async_remote_copy(src_ref=src, dst_ref=dst, send_sem=send_sems.at[k], recv_sem=recv_sems.at[k],
                                         device_id=peer, device_id_type=MESH).start()
        token = refs[-1]
        token[...] = jnp.zeros_like(token)

    out = pl.pallas_call(
        body, name=name,
        out_shape=(pltpu.SemaphoreType.DMA((n_copies,)), pltpu.SemaphoreType.DMA((n_copies,)),
                   *[pltpu.HBM(b.shape, b.dtype) for b in bufs], jax.ShapeDtypeStruct((8, 128), F32)),
        in_specs=[HBM] * n + [ANY] * len(extra),
        out_specs=(SEM, SEM, *[HBM] * n, pl.BlockSpec(memory_space=pltpu.VMEM)),
        input_output_aliases={k: 2 + k for k in range(n)},
        compiler_params=pltpu.CompilerParams(has_side_effects=EFFECT),
    )(*[pltpu.with_memory_space_constraint(b, pltpu.HBM) for b in bufs], *extra)
    return out[0], out[1], list(out[2:2 + n]), out[-1]


def split_wait(name, bufs, send_sems, recv_sems, after, describe):
    n = len(bufs)

    def body(*refs):
        send, recv = refs[n], refs[n + 1]
        x, y, c = _position()
        for k, (src, _, dst, peer) in enumerate(describe(refs[:n], x, y, c)):
            copy = pltpu.make_async_remote_copy(src_ref=src, dst_ref=dst, send_sem=send.at[k], recv_sem=recv.at[k],
                                                device_id=peer, device_id_type=MESH)
            copy.wait_send()
            copy.wait_recv()

    out = pl.pallas_call(
        body, name=name, out_shape=[pltpu.HBM(b.shape, b.dtype) for b in bufs],
        in_specs=[HBM] * n + [SEM, SEM] + [ANY] * len(after), out_specs=[HBM] * n,
        input_output_aliases={k: k for k in range(n)},
        compiler_params=pltpu.CompilerParams(has_side_effects=EFFECT),
    )(*bufs, send_sems, recv_sems, *after)
    return list(out)


def _row_tile(rows, cols):
    best = 16
    for t in range(16, rows + 1, 16):
        if rows % t == 0 and t * cols <= 256 * 1024:
            best = t
    return best


def cast_into_slot(name, param, index, chip):
    rows, cols = param.shape[-2:]
    tr = _row_tile(rows, cols)
    lead = (None,) * len(index)

    def body(chip_ref, s_ref, o_ref):
        del chip_ref
        o_ref[...] = s_ref[...].astype(BF16)

    return pl.pallas_call(
        body, out_shape=jax.ShapeDtypeStruct((N_CHIPS, rows, cols), BF16),
        grid_spec=pltpu.PrefetchScalarGridSpec(
            num_scalar_prefetch=1, grid=(rows // tr,),
            in_specs=[pl.BlockSpec(lead + (tr, cols), lambda r, chip_ref: index + (r, 0))],
            out_specs=pl.BlockSpec((None, tr, cols), lambda r, chip_ref: (chip_ref[0], r, 0))),
        name=name, compiler_params=_params(("parallel",)),
    )(chip, param)


GATHER_STAGES = ((0,), (1,), (2,), (3, 4, 5))
REDUCE_STAGES = ((5, 4, 3), (2, 1), (0,))


def _gather_copies(slots, x, y, c):
    me = 2 * x + y
    out = []
    for s in slots:
        for fx, fy in CHIP_FLIPS:
            mine = _half(s.at[me], 0, c)
            out.append((mine, mine, _half(s.at[2 * (x ^ fx) + (y ^ fy)], 0, c), (x ^ fx, y ^ fy, c)))
    return out


class WeightStream:
    def __init__(self, shards, chip):
        self.pending, self.ready = {}, {}
        token = ()
        for si, units in enumerate(GATHER_STAGES):
            slots = [cast_into_slot(f"cast_{u}_{t}", p, idx, chip) for u in units for t, (p, idx) in enumerate(shards[u])]
            send, recv, slots, tok = split_start(f"gather_start_{si}", slots, token, 3 * len(slots), _gather_copies)
            self.pending[si] = (send, recv, slots)
            token = (tok,)
        self.token = token

    def started(self):
        return self.token[0][0:1, 0:1]

    def get(self, u, after):
        if u not in self.ready:
            si = next(k for k, units in enumerate(GATHER_STAGES) if u in units)
            send, recv, slots = self.pending.pop(si)
            slots = split_wait(f"gather_wait_{si}", slots, send, recv, (after,) + self.token, _gather_copies)
            self.token = ()

            @_plan(0, 3 * len(slots))
            def to_sibling(ins, outs, x, y, c):
                sends, recvs = [], []
                for o in outs:
                    for fx, fy in CHIP_FLIPS:
                        slab = o.at[2 * (x ^ fx) + (y ^ fy)]
                        sends.append((_half(slab, 0, c), _half(slab, 0, c), (x, y, 1 - c)))
                        recvs.append(_half(slab, 0, 1 - c))
                return [], sends, recvs

            shapes = [jax.ShapeDtypeStruct(s.shape, BF16) for s in slots]
            slots = exchange(f"gather_sibling_{si}", slots, shapes, {k: k for k in range(len(slots))}, to_sibling)
            for i, v in enumerate(GATHER_STAGES[si]):
                self.ready[v] = tuple(slots[3 * i:3 * i + 3])
        return self.ready[u]


def _reduce_copies(bufs, x, y, c):
    n = len(bufs) // 2
    out = []
    for s, land in zip(bufs[:n], bufs[n:]):
        for k, (fx, fy) in enumerate(CHIP_FLIPS):
            out.append((s.at[2 * (x ^ fx) + (y ^ fy)], land.at[k], land.at[k], (x ^ fx, y ^ fy, c)))
    return out


GRAD_SLOTS = {"gate": (2 * DEPTH, D_MODEL, FF_SHARD), "up": (2 * DEPTH, D_MODEL, FF_SHARD),
              "down": (2 * DEPTH, FF_SHARD, D_MODEL), "in": (DEPTH, D_MODEL, IN_SHARD),
              "br": (DEPTH, BR_ROWS, D_SHARD), "out": (DEPTH, D_SHARD, D_MODEL)}


def _unit_tensors(u):
    l, j = divmod(u, 3)
    if j == 1:
        return [("in", l), ("br", l), ("out", l)]
    return [(k, 2 * l + j // 2) for k in ("gate", "up", "down")]


class GradStream:
    def __init__(self, chip, core):
        self.core = core
        self.place = jnp.concatenate([chip, core])
        self.held, self.flying = {}, []
        self.full = {k: lax.empty(shape, F32) for k, shape in GRAD_SLOTS.items()}

    def put(self, u, grads):
        self.held[u] = grads
        si = len(self.flying)
        units = REDUCE_STAGES[si]
        if not all(v in self.held for v in units):
            return jnp.zeros((1, 1), F32)
        gs = [g for v in units for g in self.held[v]]

        @_plan(0, len(gs))
        def swap_halves(ins, outs, x, y, c):
            sends = [(_half(g, 1, 1 - c), o, (x, y, 1 - c)) for g, o in zip(ins, outs)]
            return [], sends, list(outs)

        half_shapes = [jax.ShapeDtypeStruct((N_CHIPS, g.shape[1] // 2, g.shape[2]), BF16) for g in gs]
        landed = exchange(f"reduce_swap_{si}", gs, half_shapes, {}, swap_halves)
        sums = [_add_halves(g, la, self.core) for g, la in zip(gs, landed)]
        landing = [lax.empty((3,) + s.shape[1:], BF16) for s in sums]
        send, recv, bufs, token = split_start(f"reduce_start_{si}", sums + landing, (), 3 * len(sums), _reduce_copies)
        self.flying.append((send, recv, bufs, [t for v in units for t in _unit_tensors(v)]))
        return token[0:1, 0:1]

    def finish(self, after):
        for si, (send, recv, bufs, tensors) in enumerate(self.flying):
            bufs = split_wait(f"reduce_wait_{si}", bufs, send, recv, tuple(after), _reduce_copies)
            n = len(tensors)
            for (name, slot), s, land in zip(tensors, bufs[:n], bufs[n:]):
                self.full[name] = _add_chips(s, land, self.place, self.full[name], slot)
        names = list(self.full)

        @_plan(0, len(names))
        def share_halves(ins, outs, x, y, c):
            sends = [(_half(o, 1, c), _half(o, 1, c), (x, y, 1 - c)) for o in outs]
            return [], sends, [_half(o, 1, 1 - c) for o in outs]

        shapes = [jax.ShapeDtypeStruct(self.full[k].shape, F32) for k in names]
        out = exchange("reduce_share_halves", [self.full[k] for k in names], shapes, {k: k for k in range(len(names))},
                       share_halves)
        return dict(zip(names, out))


def _add_halves(g, landed, core):
    _, rh, cols = landed.shape
    tr = _row_tile(rh, cols)
    per_half = rh // tr

    def body(core_ref, g_ref, la_ref, o_ref):
        del core_ref
        o_ref[...] = (g_ref[...].astype(F32) + la_ref[...].astype(F32)).astype(BF16)

    blk = (None, tr, cols)
    return pl.pallas_call(
        body, out_shape=jax.ShapeDtypeStruct(landed.shape, BF16),
        grid_spec=pltpu.PrefetchScalarGridSpec(
            num_scalar_prefetch=1, grid=(N_CHIPS, per_half),
            in_specs=[pl.BlockSpec(blk, lambda j, r, core_ref: (j, core_ref[0] * per_half + r, 0)),
                      pl.BlockSpec(blk, lambda j, r, core_ref: (j, r, 0))],
            out_specs=pl.BlockSpec(blk, lambda j, r, core_ref: (j, r, 0))),
        name="reduce_add_halves", compiler_params=_params(("parallel", "parallel")),
    )(core, g, landed)


def _add_chips(sums, landed, place, full, slot):
    _, rh, cols = sums.shape
    tr = _row_tile(rh, cols)
    per_half = rh // tr

    def body(place_ref, s_ref, la_ref, full_in, o_ref):
        del place_ref, full_in
        o_ref[...] = ((s_ref[...].astype(F32) + la_ref[0].astype(F32)) + la_ref[1].astype(F32)) + la_ref[2].astype(F32)

    return pl.pallas_call(
        body, out_shape=jax.ShapeDtypeStruct(full.shape, F32),
        grid_spec=pltpu.PrefetchScalarGridSpec(
            num_scalar_prefetch=1, grid=(per_half,),
            in_specs=[pl.BlockSpec((None, tr, cols), lambda r, place_ref: (place_ref[0], r, 0)),
                      pl.BlockSpec((3, tr, cols), lambda r, place_ref: (0, r, 0)), ANY],
            out_specs=pl.BlockSpec((None, tr, cols), lambda r, place_ref: (slot, place_ref[1] * per_half + r, 0))),
        input_output_aliases={3: 0}, name="reduce_add_chips", compiler_params=_params(("parallel",)),
    )(place, sums, landed, full)


def sum_devices(parts):
    def body(p_ref, o_ref):
        acc = p_ref[0]
        for d in range(1, N_DEV):
            acc = acc + p_ref[d]
        o_ref[...] = acc

    return pl.pallas_call(body, out_shape=jax.ShapeDtypeStruct(parts.shape[1:], F32), name="sum_devices")(parts)


ADA_SHARD = 9 * D_MODEL // N_CHIPS
ADA_TILE = 768
ADA_ROWS = 16


def ada_forward(c_rows, w_ada, b_shard):
    def body(c_ref, w_ref, b_ref, o_ref):
        cv = c_ref[...]
        o_ref[...] = _dot((cv * _sigmoid(cv)).astype(BF16), w_ref[...].astype(BF16), NN) + b_ref[...]

    return pl.pallas_call(
        body, grid=(DEPTH, ADA_SHARD // ADA_TILE),
        in_specs=[pl.BlockSpec((ADA_ROWS, D_MODEL), lambda l, n: (0, 0)),
                  pl.BlockSpec((None, D_MODEL, ADA_TILE), lambda l, n: (l, 0, n)),
                  pl.BlockSpec((None, 1, ADA_TILE), lambda l, n: (l, 0, n))],
        out_specs=pl.BlockSpec((None, ADA_ROWS, ADA_TILE), lambda l, n: (l, 0, n)),
        out_shape=jax.ShapeDtypeStruct((DEPTH, ADA_ROWS, ADA_SHARD), F32),
        name="ada_forward", compiler_params=_params(("parallel", "parallel")),
    )(c_rows, w_ada, b_shard)


def ada_backward(c_rows, dmod_rows):
    def body(c_ref, d_ref, o_ref):
        cv = c_ref[...]
        o_ref[...] = _dot((cv * _sigmoid(cv)).astype(BF16), d_ref[...].astype(BF16), TN)

    return pl.pallas_call(
        body, grid=(DEPTH, ADA_SHARD // ADA_TILE),
        in_specs=[pl.BlockSpec((ADA_ROWS, D_MODEL), lambda l, n: (0, 0)),
                  pl.BlockSpec((None, ADA_ROWS, ADA_TILE), lambda l, n: (l, 0, n))],
        out_specs=pl.BlockSpec((None, D_MODEL, ADA_TILE), lambda l, n: (l, 0, n)),
        out_shape=jax.ShapeDtypeStruct((DEPTH, D_MODEL, ADA_SHARD), F32),
        name="ada_backward", compiler_params=_params(("parallel", "parallel")),
    )(c_rows, dmod_rows)


def adamw(name, w, g, m, v):
    shape = w.shape
    cols = shape[-1]
    rows = w.size // cols
    tr = _row_tile(rows, cols) if rows % 16 == 0 else rows
    c1 = 1.0 / (1.0 - ADAM_B1 ** ADAM_STEP)
    c2 = 1.0 / (1.0 - ADAM_B2 ** ADAM_STEP)

    def body(w_ref, g_ref, m_ref, v_ref, go_ref, d_ref, mo_ref, vo_ref):
        gv = g_ref[...]
        mn = ADAM_B1 * m_ref[...] + (1.0 - ADAM_B1) * gv
        vn = ADAM_B2 * v_ref[...] + (1.0 - ADAM_B2) * (gv * gv)
        go_ref[...] = gv
        mo_ref[...] = mn
        vo_ref[...] = vn
        d_ref[...] = -ADAM_LR * ((mn * c1) / (jnp.sqrt(vn * c2) + ADAM_EPS) + ADAM_WD * w_ref[...])

    spec = pl.BlockSpec((tr, cols), lambda i: (i, 0))
    out = jax.ShapeDtypeStruct((rows, cols), F32)
    res = pl.pallas_call(
        body, grid=(rows // tr,), in_specs=[spec] * 4, out_specs=[spec] * 4, out_shape=[out] * 4,
        name=name, compiler_params=_params(("parallel",)),
    )(*[t.reshape(rows, cols) for t in (w, g, m, v)])
    return tuple(r.reshape(shape) for r in res)


def _pack(parts, rows):
    flat = jnp.concatenate([p.reshape(-1) for p in parts])
    return jnp.pad(flat, (0, rows * 128 - flat.size)).reshape(rows, 128)


def _unpack(flat, shapes):
    out, at = [], 0
    for s in shapes:
        n = math.prod(s)
        out.append(flat[at:at + n].reshape(s))
        at += n
    return out


def kernel(x, c, w_ada, b_ada, norm_gain, w_ffn_gate, w_ffn_up, w_ffn_down, w_in, w_br_sb, w_br_dil, w_br_swa, w_out, sinks, rel_bias, final_gain, loss_target, m_w_ada, m_b_ada, m_norm_gain, m_w_ffn_gate, m_w_ffn_up, m_w_ffn_down, m_w_in, m_w_br_sb, m_w_br_dil, m_w_br_swa, m_w_out, m_sinks, m_rel_bias, m_final_gain, v_w_ada, v_b_ada, v_norm_gain, v_w_ffn_gate, v_w_ffn_up, v_w_ffn_down, v_w_in, v_w_br_sb, v_w_br_dil, v_w_br_swa, v_w_out, v_sinks, v_rel_bias, v_final_gain):
    xi, yi, ci = _position()
    chip = 2 * xi + yi
    dev = 2 * chip + ci

    chip_i, core_i = chip.astype(jnp.int32).reshape(1), ci.astype(jnp.int32).reshape(1)
    w_br = jnp.concatenate([w_br_sb, w_br_dil, w_br_swa], axis=1)
    shards = []
    for l in range(DEPTH):
        ffn = [[(w_ffn_gate, (l, f)), (w_ffn_up, (l, f)), (w_ffn_down, (l, f))] for f in range(2)]
        shards += [ffn[0], [(w_in, (l,)), (w_br, (l,)), (w_out, (l,))], ffn[1]]
    weights_in = WeightStream(shards, chip_i)
    grads_out = GradStream(chip_i, core_i)
    c = c + weights_in.started()

    c_all = all_gather_small("gather_c", c.reshape(8, 128)).reshape(N_DEV, D_MODEL)
    c_rows = jnp.pad(c_all, ((0, ADA_ROWS - N_DEV), (0, 0)))
    b_shard = lax.dynamic_slice_in_dim(b_ada, chip * ADA_SHARD, ADA_SHARD, axis=1).reshape(DEPTH, 1, ADA_SHARD)
    mod_shard = ada_forward(c_rows, w_ada, b_shard)[:, :N_DEV]
    n_mod = DEPTH * N_DEV * ADA_SHARD
    gathered = all_gather_small("gather_mod", _pack([mod_shard, norm_gain], 304))[::2].reshape(N_CHIPS, -1)
    mod_all = gathered[:, :n_mod].reshape(N_CHIPS, DEPTH, N_DEV, ADA_SHARD)
    mod = lax.dynamic_index_in_dim(mod_all, dev, axis=2, keepdims=False)
    mod = mod.transpose(1, 0, 2).reshape(DEPTH, 3, 3, D_MODEL)
    gains = gathered[:, n_mod:n_mod + DEPTH * 3 * D_SHARD].reshape(N_CHIPS, DEPTH, 3, D_SHARD)
    gains = gains.transpose(1, 2, 0, 3).reshape(DEPTH, 3, D_MODEL)

    loss, dx, dmod, dgains, dfinal, dsinks, drel = device_step(
        x[0], loss_target[0], mod, gains, final_gain, sinks, rel_bias, weights_in.get, grads_out.put)

    small_shapes = [(DEPTH, 9 * D_MODEL), (DEPTH, 3, D_MODEL), (D_MODEL,), (DEPTH, H_SWA_Q), (N_BUCKETS, 12), (1,)]
    small_all = all_gather_small("gather_small_grads", _pack([dmod, dgains, dfinal, dsinks, drel, loss[0, 0:1]], 208))
    g_b_ada, g_gain_full, g_final, g_sinks, g_rel, loss_sum = _unpack(sum_devices(small_all).reshape(-1), small_shapes)
    g_gain = lax.dynamic_slice_in_dim(g_gain_full, chip * D_SHARD, D_SHARD, axis=2)
    dmod_all = small_all.reshape(N_DEV, -1)[:, :DEPTH * 9 * D_MODEL].reshape(N_DEV, DEPTH, 9 * D_MODEL)
    dmod_rows = lax.dynamic_slice_in_dim(dmod_all, chip * ADA_SHARD, ADA_SHARD, axis=2).transpose(1, 0, 2)
    g_w_ada = ada_backward(c_rows, jnp.pad(dmod_rows, ((0, 0), (0, ADA_ROWS - N_DEV), (0, 0))))

    weights = [w_ada, b_ada, norm_gain, w_ffn_gate, w_ffn_up, w_ffn_down, w_in, w_br_sb, w_br_dil, w_br_swa, w_out,
               sinks, rel_bias, final_gain]
    ms = [m_w_ada, m_b_ada, m_norm_gain, m_w_ffn_gate, m_w_ffn_up, m_w_ffn_down, m_w_in, m_w_br_sb, m_w_br_dil,
          m_w_br_swa, m_w_out, m_sinks, m_rel_bias, m_final_gain]
    vs = [v_w_ada, v_b_ada, v_norm_gain, v_w_ffn_gate, v_w_ffn_up, v_w_ffn_down, v_w_in, v_w_br_sb, v_w_br_dil,
          v_w_br_swa, v_w_out, v_sinks, v_rel_bias, v_final_gain]
    grads = [g_w_ada, g_b_ada, g_gain] + [None] * 8 + [g_sinks, g_rel, g_final]

    small = (1, 2, 11, 12, 13)
    deltas, new_ms, new_vs = [None] * 14, [None] * 14, [None] * 14
    _, deltas[0], new_ms[0], new_vs[0] = adamw("adamw_0", weights[0], grads[0], ms[0], vs[0])
    shapes = [weights[k].shape for k in small]
    packed = [_pack([t[k] for k in small], 168) for t in (weights, grads, ms, vs)]
    for dst, res in zip((deltas, new_ms, new_vs), adamw("adamw_small", *packed)[1:]):
        for k, t in zip(small, _unpack(res.reshape(-1), shapes)):
            dst[k] = t

    g = grads_out.finish((dx, deltas[0], deltas[1]))
    g_br = g["br"]
    grads[3:11] = [g["gate"].reshape(w_ffn_gate.shape), g["up"].reshape(w_ffn_up.shape),
                   g["down"].reshape(w_ffn_down.shape), g["in"], g_br[:, 0:256], g_br[:, 256:384], g_br[:, 384:768],
                   g["out"]]
    for k in range(3, 11):
        grads[k], deltas[k], new_ms[k], new_vs[k] = adamw(f"adamw_{k}", weights[k], grads[k], ms[k], vs[k])
    return (loss_sum[0], dx[None], *grads, *deltas, *new_ms, *new_vs)
```

```python
import functools
import math

import jax
import jax.numpy as jnp
from jax import lax
from jax.experimental import pallas as pl
from jax.experimental.pallas import tpu as pltpu

F32 = jnp.float32
BF16 = jnp.bfloat16

D_MODEL = 1024
SEQ = 2048
DEPTH = 2
HEAD_DIM = 64
BLK = 128
H_SB = 4
DIL_PATTERNS = ((128, 1), (512, 4), (2048, 16))
H_PER_DIL = 2
H_DIL = 6
H_SWA_Q = 6
H_SWA_KV = 2
SWA_WINDOW = 128
N_BUCKETS = 32
MAX_REL_DIST = 2048
D_FF = 2816
RMS_EPS = 1e-6
N_CHIPS = 4
N_DEV = 8
FF_SHARD = D_FF // N_CHIPS
D_QKV = 2560
D_IN = D_QKV + 3 * D_MODEL
IN_SHARD = D_IN // N_CHIPS
D_SHARD = D_MODEL // N_CHIPS
BR_ROWS = 768
NEG = -1e30
QK_SCALE = HEAD_DIM ** -0.5

ADAM_LR = 0.001
ADAM_B1 = 0.9
ADAM_B2 = 0.999
ADAM_EPS = 1e-08
ADAM_WD = 0.01
ADAM_STEP = 10

VMEM_LIMIT = 48 * 1024 * 1024
ROW_TILE = 256
MM_TILE = 512

NN = (((1,), (0,)), ((), ()))
NT = (((1,), (1,)), ((), ()))
TN = (((0,), (0,)), ((), ()))


def _params(sem=None):
    return pltpu.CompilerParams(dimension_semantics=sem, vmem_limit_bytes=VMEM_LIMIT)


def _dot(a, b, dims):
    return lax.dot_general(a, b, dims, preferred_element_type=F32)


def _sigmoid(x):
    return 1.0 / (1.0 + jnp.exp(-x))


def _matmul(name, grid, nk, k_axis, dims, n_pairs, in_specs, out_specs, out_shape, acc_shape, epilogue,
            operands, sem, aliases=None, prologue=None):
    n_in = len(in_specs)
    n_out = len(out_specs)

    def partial(ins):
        tot = None
        for p in range(n_pairs):
            a = ins[2 * p][...]
            if prologue is not None:
                a = prologue(p, a, ins)
            d = _dot(a, ins[2 * p + 1][...], dims)
            tot = d if tot is None else tot + d
        return tot

    def body(*refs):
        ins, outs = refs[:n_in], refs[n_in:n_in + n_out]
        ids = tuple(pl.program_id(a) for a in range(len(grid)))
        if nk == 1:
            epilogue(partial(ins), ins, outs, ids)
            return
        acc = refs[n_in + n_out]
        k = ids[k_axis]

        @pl.when(k == 0)
        def _():
            acc[...] = partial(ins)

        @pl.when(k > 0)
        def _():
            acc[...] += partial(ins)

        @pl.when(k == nk - 1)
        def _():
            epilogue(acc[...], ins, outs, ids)

    return pl.pallas_call(
        body, grid=grid, in_specs=in_specs, out_specs=out_specs, out_shape=out_shape,
        scratch_shapes=[] if nk == 1 else [pltpu.VMEM(acc_shape, F32)],
        input_output_aliases=aliases or {}, name=name, compiler_params=_params(sem),
    )(*operands)


def _row_spec(width=D_MODEL):
    return pl.BlockSpec((ROW_TILE, width), lambda i: (i, 0))


def _vec_spec(rows=1, width=D_MODEL):
    return pl.BlockSpec((rows, width), lambda i: (0, 0))


def prenorm(x, gain, scale, shift):
    def body(x_ref, g_ref, sc_ref, sh_ref, h_ref):
        xv = x_ref[...]
        r = lax.rsqrt(jnp.mean(xv * xv, axis=-1, keepdims=True) + RMS_EPS)
        h_ref[...] = (((xv * r) * g_ref[...]) * (1.0 + sc_ref[...]) + sh_ref[...]).astype(BF16)

    return pl.pallas_call(
        body, grid=(SEQ // ROW_TILE,), in_specs=[_row_spec(), _vec_spec(), _vec_spec(), _vec_spec()],
        out_specs=_row_spec(), out_shape=jax.ShapeDtypeStruct((SEQ, D_MODEL), BF16),
        name="prenorm", compiler_params=_params(("parallel",)),
    )(x, gain, scale, shift)


def resid_bwd(dxo, f, coef, mult):
    def body(dx_ref, f_ref, c_ref, df_ref, dc_ref):
        dx = dx_ref[...]
        df_ref[...] = (dx * (mult * c_ref[...])).astype(BF16)
        part = mult * jnp.sum(dx * f_ref[...], axis=0, keepdims=True)

        @pl.when(pl.program_id(0) == 0)
        def _():
            dc_ref[...] = jnp.zeros_like(dc_ref)

        dc_ref[0:1, :] += part

    return pl.pallas_call(
        body, grid=(SEQ // ROW_TILE,), in_specs=[_row_spec(), _row_spec(), _vec_spec()],
        out_specs=[_row_spec(), _vec_spec(8)],
        out_shape=[jax.ShapeDtypeStruct((SEQ, D_MODEL), BF16), jax.ShapeDtypeStruct((8, D_MODEL), F32)],
        name="resid_bwd", compiler_params=_params(("arbitrary",)),
    )(dxo, f, coef)


def final_loss(x, gain, target):
    def body(x_ref, g_ref, t_ref, loss_ref, dx_ref, dg_ref):
        xv = x_ref[...]
        g = g_ref[...]
        r = lax.rsqrt(jnp.mean(xv * xv, axis=-1, keepdims=True) + RMS_EPS)
        xh = xv * r
        e = xh * g - t_ref[...]
        part = 0.5 * jnp.sum(jnp.mean(e * e, axis=-1, keepdims=True), axis=0, keepdims=True)
        dy = e * (1.0 / D_MODEL)
        dyg = dy * g
        dx_ref[...] = r * (dyg - xh * jnp.mean(dyg * xh, axis=-1, keepdims=True))

        @pl.when(pl.program_id(0) == 0)
        def _():
            loss_ref[...] = jnp.zeros_like(loss_ref)
            dg_ref[...] = jnp.zeros_like(dg_ref)

        loss_ref[...] += jnp.broadcast_to(part, loss_ref.shape)
        dg_ref[0:1, :] += jnp.sum(dy * xh, axis=0, keepdims=True)

    return pl.pallas_call(
        body, grid=(SEQ // ROW_TILE,), in_specs=[_row_spec(), _vec_spec(), _row_spec()],
        out_specs=[_vec_spec(8, 128), _row_spec(), _vec_spec(8)],
        out_shape=[jax.ShapeDtypeStruct((8, 128), F32), jax.ShapeDtypeStruct((SEQ, D_MODEL), F32),
                   jax.ShapeDtypeStruct((8, D_MODEL), F32)],
        name="final_loss", compiler_params=_params(("arbitrary",)),
    )(x, gain, target)


def _prenorm_bwd_epilogue(dh, x_ref, dxo_ref, g_ref, sc_ref, dx_ref, stats_ref, first):
    xv = x_ref[...]
    g = g_ref[...]
    r = lax.rsqrt(jnp.mean(xv * xv, axis=-1, keepdims=True) + RMS_EPS)
    xh = xv * r
    dn = dh * (1.0 + sc_ref[...])
    dxh = dn * g
    dx_ref[...] = dxo_ref[...] + r * (dxh - xh * jnp.mean(dxh * xh, axis=-1, keepdims=True))

    @pl.when(first)
    def _():
        stats_ref[...] = jnp.zeros_like(stats_ref)

    stats_ref[0:1, :] += jnp.sum(dh, axis=0, keepdims=True)
    stats_ref[1:2, :] += jnp.sum(dh * (xh * g), axis=0, keepdims=True)
    stats_ref[2:3, :] += jnp.sum(dn * xh, axis=0, keepdims=True)


def ffn_up(h, wg_all, wu_all):
    def body(h_ref, wg_ref, wu_ref, a_ref, b_ref, s_ref):
        hv = h_ref[...]
        a = _dot(hv, wg_ref[...], NN)
        b = _dot(hv, wu_ref[...], NN)
        a_ref[...] = a
        b_ref[...] = b
        s_ref[...] = (a * _sigmoid(a) * b).astype(BF16)

    w_spec = pl.BlockSpec((None, D_MODEL, FF_SHARD), lambda j, i: (j, 0, 0))
    o_spec = pl.BlockSpec((None, MM_TILE, FF_SHARD), lambda j, i: (j, i, 0))
    hid = (N_CHIPS, SEQ, FF_SHARD)
    return pl.pallas_call(
        body, grid=(N_CHIPS, SEQ // MM_TILE),
        in_specs=[pl.BlockSpec((MM_TILE, D_MODEL), lambda j, i: (i, 0)), w_spec, w_spec],
        out_specs=[o_spec, o_spec, o_spec],
        out_shape=[jax.ShapeDtypeStruct(hid, F32), jax.ShapeDtypeStruct(hid, F32), jax.ShapeDtypeStruct(hid, BF16)],
        name="ffn_up", compiler_params=_params(("parallel", "parallel")),
    )(h, wg_all, wu_all)


def matmul_residual(name, a, a_spec, w_all, w_spec, x, coef, mult):
    def epilogue(acc, ins, outs, ids):
        outs[0][...] = acc
        outs[1][...] = ins[2][...] + (mult * ins[3][...]) * acc

    row = pl.BlockSpec((MM_TILE, D_MODEL), lambda i, j: (i, 0))
    return _matmul(
        name, (SEQ // MM_TILE, N_CHIPS), N_CHIPS, 1, NN, 1,
        [a_spec, w_spec, row, pl.BlockSpec((1, D_MODEL), lambda i, j: (0, 0))], [row, row],
        [jax.ShapeDtypeStruct((SEQ, D_MODEL), F32)] * 2, (MM_TILE, D_MODEL), epilogue,
        (a, w_all, x, coef), ("parallel", "arbitrary"))


def ffn_down(s, wd_all, x, gate):
    return matmul_residual(
        "ffn_down", s, pl.BlockSpec((None, MM_TILE, FF_SHARD), lambda i, j: (j, i, 0)),
        wd_all, pl.BlockSpec((None, FF_SHARD, D_MODEL), lambda i, j: (j, 0, 0)), x, gate, 0.5)


def ffn_bwd_hidden(df, wd_all, a, b):
    def epilogue(ds, ins, outs, ids):
        av, bv = ins[2][...], ins[3][...]
        sig = _sigmoid(av)
        outs[0][...] = (ds * bv * (sig * (1.0 + av * (1.0 - sig)))).astype(BF16)
        outs[1][...] = (ds * (av * sig)).astype(BF16)

    hid_spec = pl.BlockSpec((None, MM_TILE, FF_SHARD), lambda j, i: (j, i, 0))
    hid = jax.ShapeDtypeStruct((N_CHIPS, SEQ, FF_SHARD), BF16)
    return _matmul(
        "ffn_bwd_hidden", (N_CHIPS, SEQ // MM_TILE), 1, None, NT, 1,
        [pl.BlockSpec((MM_TILE, D_MODEL), lambda j, i: (i, 0)),
         pl.BlockSpec((None, FF_SHARD, D_MODEL), lambda j, i: (j, 0, 0)), hid_spec, hid_spec],
        [hid_spec, hid_spec], [hid, hid], None, epilogue, (df, wd_all, a, b), ("parallel", "parallel"))


def grad_weight(name, lhs, lhs_spec, rhs, rhs_spec, shape):
    def epilogue(acc, ins, outs, ids):
        outs[0][...] = acc.astype(BF16)

    return _matmul(
        name, (N_CHIPS, SEQ // MM_TILE), SEQ // MM_TILE, 1, TN, 1,
        [lhs_spec, rhs_spec], [pl.BlockSpec((None,) + shape, lambda j, k: (j, 0, 0))],
        [jax.ShapeDtypeStruct((N_CHIPS,) + shape, BF16)], shape, epilogue, (lhs, rhs), ("parallel", "arbitrary"))[0]


def ffn_grad_weights(h, s, df, da, db):
    tok = pl.BlockSpec((MM_TILE, D_MODEL), lambda j, k: (k, 0))
    hid = pl.BlockSpec((None, MM_TILE, FF_SHARD), lambda j, k: (j, k, 0))
    return (grad_weight("grad_w_gate", h, tok, da, hid, (D_MODEL, FF_SHARD)),
            grad_weight("grad_w_up", h, tok, db, hid, (D_MODEL, FF_SHARD)),
            grad_weight("grad_w_down", s, hid, df, tok, (FF_SHARD, D_MODEL)))


def matmul_prenorm_bwd(name, pairs, pair_specs, x, dxo, gain, scale):
    n = len(pairs)

    def epilogue(dh, ins, outs, ids):
        _prenorm_bwd_epilogue(dh, ins[n], ins[n + 1], ins[n + 2], ins[n + 3], outs[0], outs[1], ids[0] == 0)

    row = pl.BlockSpec((MM_TILE, D_MODEL), lambda i, j: (i, 0))
    vec = pl.BlockSpec((1, D_MODEL), lambda i, j: (0, 0))
    return _matmul(
        name, (SEQ // MM_TILE, N_CHIPS), N_CHIPS, 1, NT, len(pairs) // 2,
        list(pair_specs) + [row, row, vec, vec], [row, pl.BlockSpec((8, D_MODEL), lambda i, j: (0, 0))],
        [jax.ShapeDtypeStruct((SEQ, D_MODEL), F32), jax.ShapeDtypeStruct((8, D_MODEL), F32)],
        (MM_TILE, D_MODEL), epilogue, tuple(pairs) + (x, dxo, gain, scale), ("arbitrary", "arbitrary"))


def ffn_bwd_input(da, db, wg_all, wu_all, x, dxo, gain, scale):
    hid = pl.BlockSpec((None, MM_TILE, FF_SHARD), lambda i, j: (j, i, 0))
    w = pl.BlockSpec((None, D_MODEL, FF_SHARD), lambda i, j: (j, 0, 0))
    return matmul_prenorm_bwd("ffn_bwd_input", (da, wg_all, db, wu_all), (hid, w, hid, w), x, dxo, gain, scale)


def in_proj(h, w_all):
    def epilogue(acc, ins, outs, ids):
        outs[0][...] = acc

    return _matmul(
        "in_proj", (N_CHIPS, SEQ // MM_TILE), 1, None, NN, 1,
        [pl.BlockSpec((MM_TILE, D_MODEL), lambda j, i: (i, 0)),
         pl.BlockSpec((None, D_MODEL, IN_SHARD), lambda j, i: (j, 0, 0))],
        [pl.BlockSpec((MM_TILE, IN_SHARD), lambda j, i: (i, j))], [jax.ShapeDtypeStruct((SEQ, D_IN), F32)],
        None, epilogue, (h, w_all), ("parallel", "parallel"))[0]


_GATE_BLOCK0 = D_QKV // D_SHARD


def _branch_products(o, w_ref):
    ob = o.astype(BF16)
    return (_dot(ob[:, 0:256], w_ref[0:256, :], NN), _dot(ob[:, 256:384], w_ref[256:384, :], NN),
            _dot(ob[:, 384:768], w_ref[384:768, :], NN))


def merge_branches(o_cat, wbr_all, proj):
    def body(o_ref, w_ref, g0_ref, g1_ref, g2_ref, m_ref):
        u = _branch_products(o_ref[...], w_ref)
        m_ref[...] = (_sigmoid(g0_ref[...]) * u[0] + _sigmoid(g1_ref[...]) * u[1]
                      + _sigmoid(g2_ref[...]) * u[2]).astype(BF16)

    def gate_spec(b):
        return pl.BlockSpec((MM_TILE, D_SHARD), lambda i, j: (i, _GATE_BLOCK0 + 4 * b + j))

    return pl.pallas_call(
        body, grid=(SEQ // MM_TILE, N_CHIPS),
        in_specs=[pl.BlockSpec((MM_TILE, BR_ROWS), lambda i, j: (i, 0)),
                  pl.BlockSpec((None, BR_ROWS, D_SHARD), lambda i, j: (j, 0, 0)),
                  gate_spec(0), gate_spec(1), gate_spec(2)],
        out_specs=pl.BlockSpec((MM_TILE, D_SHARD), lambda i, j: (i, j)),
        out_shape=jax.ShapeDtypeStruct((SEQ, D_MODEL), BF16),
        name="merge_branches", compiler_params=_params(("parallel", "parallel")),
    )(o_cat, wbr_all, proj, proj, proj)


def out_proj(merged, wout_all, x, gate):
    return matmul_residual(
        "out_proj", merged, pl.BlockSpec((MM_TILE, D_SHARD), lambda i, j: (i, j)),
        wout_all, pl.BlockSpec((None, D_SHARD, D_MODEL), lambda i, j: (j, 0, 0)), x, gate, 1.0)


def merge_bwd(dmo, wout_all, o_cat, wbr_all, proj):
    def epilogue(dm, ins, outs, ids):
        u = _branch_products(ins[2][...], ins[3])
        for b in range(3):
            sig = _sigmoid(ins[4 + b][...])
            outs[b][...] = (dm * sig).astype(BF16)
            outs[3 + b][...] = (dm * u[b] * (sig * (1.0 - sig))).astype(BF16)

    def gate_spec(b):
        return pl.BlockSpec((MM_TILE, D_SHARD), lambda j, i: (i, _GATE_BLOCK0 + 4 * b + j))

    col = pl.BlockSpec((MM_TILE, D_SHARD), lambda j, i: (i, j))
    du = jax.ShapeDtypeStruct((SEQ, D_MODEL), BF16)
    return _matmul(
        "merge_bwd", (N_CHIPS, SEQ // MM_TILE), 1, None, NT, 1,
        [pl.BlockSpec((MM_TILE, D_MODEL), lambda j, i: (i, 0)),
         pl.BlockSpec((None, D_SHARD, D_MODEL), lambda j, i: (j, 0, 0)),
         pl.BlockSpec((MM_TILE, BR_ROWS), lambda j, i: (i, 0)),
         pl.BlockSpec((None, BR_ROWS, D_SHARD), lambda j, i: (j, 0, 0)),
         gate_spec(0), gate_spec(1), gate_spec(2)],
        [col] * 6, [du] * 6,
        None, epilogue, (dmo, wout_all, o_cat, wbr_all, proj, proj, proj), ("parallel", "parallel"))


def branch_bwd_input(du, wbr_all):
    def body(d0_ref, d1_ref, d2_ref, w_ref, o_ref, acc):
        j = pl.program_id(1)
        parts = (_dot(d0_ref[...], w_ref[0:256, :], NT), _dot(d1_ref[...], w_ref[256:384, :], NT),
                 _dot(d2_ref[...], w_ref[384:768, :], NT))

        @pl.when(j == 0)
        def _():
            acc[:, 0:256], acc[:, 256:384], acc[:, 384:768] = parts

        @pl.when(j > 0)
        def _():
            acc[:, 0:256] += parts[0]
            acc[:, 256:384] += parts[1]
            acc[:, 384:768] += parts[2]

        @pl.when(j == N_CHIPS - 1)
        def _():
            o_ref[...] = acc[...]

    col = pl.BlockSpec((MM_TILE, D_SHARD), lambda i, j: (i, j))
    return pl.pallas_call(
        body, grid=(SEQ // MM_TILE, N_CHIPS),
        in_specs=[col, col, col, pl.BlockSpec((None, BR_ROWS, D_SHARD), lambda i, j: (j, 0, 0))],
        out_specs=pl.BlockSpec((MM_TILE, BR_ROWS), lambda i, j: (i, 0)),
        out_shape=jax.ShapeDtypeStruct((SEQ, BR_ROWS), F32),
        scratch_shapes=[pltpu.VMEM((MM_TILE, BR_ROWS), F32)],
        name="branch_bwd_input", compiler_params=_params(("parallel", "arbitrary")),
    )(du[0], du[1], du[2], wbr_all)


def branch_grad_weights(o_cat, du):
    def body(o_ref, d0_ref, d1_ref, d2_ref, g_ref, acc):
        k = pl.program_id(1)
        ob = o_ref[...].astype(BF16)
        parts = (_dot(ob[:, 0:256], d0_ref[...], TN), _dot(ob[:, 256:384], d1_ref[...], TN),
                 _dot(ob[:, 384:768], d2_ref[...], TN))

        @pl.when(k == 0)
        def _():
            acc[0:256, :], acc[256:384, :], acc[384:768, :] = parts

        @pl.when(k > 0)
        def _():
            acc[0:256, :] += parts[0]
            acc[256:384, :] += parts[1]
            acc[384:768, :] += parts[2]

        @pl.when(k == SEQ // MM_TILE - 1)
        def _():
            g_ref[...] = acc[...].astype(BF16)

    col = pl.BlockSpec((MM_TILE, D_SHARD), lambda j, k: (k, j))
    return pl.pallas_call(
        body, grid=(N_CHIPS, SEQ // MM_TILE),
        in_specs=[pl.BlockSpec((MM_TILE, BR_ROWS), lambda j, k: (k, 0)), col, col, col],
        out_specs=pl.BlockSpec((None, BR_ROWS, D_SHARD), lambda j, k: (j, 0, 0)),
        out_shape=jax.ShapeDtypeStruct((N_CHIPS, BR_ROWS, D_SHARD), BF16),
        scratch_shapes=[pltpu.VMEM((BR_ROWS, D_SHARD), F32)],
        name="branch_grad_weights", compiler_params=_params(("parallel", "arbitrary")),
    )(o_cat, du[0], du[1], du[2])


def mixer_bwd_input(dproj, win_all, x, dxo, gain, scale):
    return matmul_prenorm_bwd(
        "mixer_bwd_input", (dproj, win_all),
        (pl.BlockSpec((MM_TILE, IN_SHARD), lambda i, j: (i, j)),
         pl.BlockSpec((None, D_MODEL, IN_SHARD), lambda i, j: (j, 0, 0))), x, dxo, gain, scale)


BATCH_QK = (((2,), (2,)), ((0,), (0,)))
BATCH_PV = (((2,), (1,)), ((0,), (0,)))
BATCH_TN = (((1,), (1,)), ((0,), (0,)))


def _split_dot(v, tri):
    v2 = v.reshape(v.shape[0] * BLK, BLK)
    hi = v2.astype(BF16)
    lo = (v2 - hi.astype(F32)).astype(BF16)
    return (_dot(hi, tri, NN) + _dot(lo, tri, NN)).reshape(v.shape)


def _tri(cmp):
    return cmp(lax.broadcasted_iota(jnp.int32, (BLK, BLK), 0), lax.broadcasted_iota(jnp.int32, (BLK, BLK), 1)).astype(BF16)


def _sb_scores(qs, k_ref, i, j):
    rows = pl.ds(pl.multiple_of(j * BLK, BLK), BLK)
    z = _dot(qs, k_ref[:, rows, :], BATCH_QK)
    tile = (1, BLK, BLK)
    before = (j * BLK + lax.broadcasted_iota(jnp.int32, tile, 2)) < (i * BLK + lax.broadcasted_iota(jnp.int32, tile, 1))
    soft = jnp.log(1.0 + jnp.exp(-jnp.abs(z)))
    log_fail = jnp.where(before, -(jnp.maximum(z, 0.0) + soft), 0.0)
    log_hit = jnp.minimum(z, 0.0) - soft
    return rows, before, log_fail, log_hit


def sb_forward(q, k, v):
    def body(q_ref, k_ref, v_ref, o_ref, tot_ref):
        i = pl.program_id(0)
        qs = q_ref[...]
        later = _tri(lambda r, c: r > c)

        def step(t, carry):
            o, run = carry
            rows, before, log_fail, log_hit = _sb_scores(qs, k_ref, i, i - t)
            between = _split_dot(log_fail, later) + run
            w = jnp.where(before, jnp.exp(log_hit + between), 0.0)
            o = o + _dot(w.astype(BF16), v_ref[:, rows, :], BATCH_PV)
            return o, run + jnp.sum(log_fail, axis=2, keepdims=True)

        o, run = lax.fori_loop(0, i + 1, step, (jnp.zeros((H_SB, BLK, HEAD_DIM), F32), jnp.zeros((H_SB, BLK, 1), F32)))
        o_ref[...] = o
        tot_ref[...] = run

    blk = pl.BlockSpec((H_SB, BLK, HEAD_DIM), lambda i: (0, i, 0))
    col = pl.BlockSpec((H_SB, BLK, 1), lambda i: (0, i, 0))
    full = pl.BlockSpec((H_SB, SEQ, HEAD_DIM), lambda i: (0, 0, 0))
    return pl.pallas_call(
        body, grid=(SEQ // BLK,), in_specs=[blk, full, full], out_specs=[blk, col],
        out_shape=[jax.ShapeDtypeStruct((H_SB, SEQ, HEAD_DIM), F32), jax.ShapeDtypeStruct((H_SB, SEQ, 1), F32)],
        name="sb_forward", compiler_params=_params(("parallel",)),
    )(q, k, v)


def sb_backward(q, k, v, total, do):
    def body(q_ref, k_ref, v_ref, tot_ref, do_ref, dq_ref, dk_ref, dv_ref):
        i = pl.program_id(0)

        @pl.when(i == 0)
        def _():
            dk_ref[...] = jnp.zeros_like(dk_ref)
            dv_ref[...] = jnp.zeros_like(dv_ref)

        qs = q_ref[...]
        dob = do_ref[...].astype(BF16)
        total_v = tot_ref[...]
        upto = _tri(lambda r, c: r <= c)
        earlier = _tri(lambda r, c: r < c)

        def step(j, carry):
            dq, seen, g_seen = carry
            rows, before, log_fail, log_hit = _sb_scores(qs, k_ref, i, j)
            between = total_v - (seen + _split_dot(log_fail, upto))
            w = jnp.where(before, jnp.exp(log_hit + between), 0.0)
            g = _dot(dob, v_ref[:, rows, :], BATCH_QK) * w
            g_earlier = g_seen + _split_dot(g, earlier)
            sig = jnp.exp(log_hit)
            dz = jnp.where(before, g * (1.0 - sig) - g_earlier * sig, 0.0).astype(BF16)
            dq = dq + _dot(dz, k_ref[:, rows, :], BATCH_PV)
            dk_ref[:, rows, :] += _dot(dz, qs, BATCH_TN)
            dv_ref[:, rows, :] += _dot(w.astype(BF16), dob, BATCH_TN)
            return dq, seen + jnp.sum(log_fail, axis=2, keepdims=True), g_seen + jnp.sum(g, axis=2, keepdims=True)

        zero = jnp.zeros((H_SB, BLK, 1), F32)
        dq, _, _ = lax.fori_loop(0, i + 1, step, (jnp.zeros((H_SB, BLK, HEAD_DIM), F32), zero, zero))
        dq_ref[...] = dq * QK_SCALE

    blk = pl.BlockSpec((H_SB, BLK, HEAD_DIM), lambda i: (0, i, 0))
    col = pl.BlockSpec((H_SB, BLK, 1), lambda i: (0, i, 0))
    full = pl.BlockSpec((H_SB, SEQ, HEAD_DIM), lambda i: (0, 0, 0))
    shape = jax.ShapeDtypeStruct((H_SB, SEQ, HEAD_DIM), F32)
    return pl.pallas_call(
        body, grid=(SEQ // BLK,), in_specs=[blk, full, full, col, blk], out_specs=[blk, full, full],
        out_shape=[shape, shape, shape],
        name="sb_backward", compiler_params=_params(("arbitrary",)),
    )(q, k, v, total, do)


def _band_scores(q_ref, kp_ref, ko_ref, bias_ref, hb, prev_mask):
    b = pl.program_id(1)
    qs = q_ref[...]
    s_prev = _dot(qs, kp_ref[...], BATCH_QK) + bias_ref[:, :, 0:BLK]
    s_prev = jnp.concatenate(
        [jnp.where((b & prev_mask(pl.program_id(0) * hb + t)) != 0, s_prev[t:t + 1], NEG) for t in range(hb)], axis=0)
    s_own = _dot(qs, ko_ref[...], BATCH_QK) + bias_ref[:, :, BLK:2 * BLK]
    return qs, s_prev, s_own


def _band_specs(hb, rows, t_n):
    def q_spec(width):
        return pl.BlockSpec((hb, None, rows, width), lambda h, b: (h, b, 0, 0))

    own = pl.BlockSpec((hb, BLK, HEAD_DIM), lambda h, b: (h, b, 0))
    prev = pl.BlockSpec((hb, BLK, HEAD_DIM), lambda h, b: (h, jnp.maximum(b - 1, 0), 0))
    per_head = lambda r, width: pl.BlockSpec((hb, r, width), lambda h, b: (h, 0, 0))
    return q_spec, own, prev, per_head


def banded_forward(name, q, k, v, bias, sinks, hb, prev_mask):
    h_n, nb, rows, _ = q.shape

    def body(q_ref, kp_ref, ko_ref, vp_ref, vo_ref, bias_ref, sink_ref, o_ref, lse_ref):
        _, s_prev, s_own = _band_scores(q_ref, kp_ref, ko_ref, bias_ref, hb, prev_mask)
        sink = sink_ref[...]
        m = jnp.maximum(jnp.maximum(jnp.max(s_prev, axis=2, keepdims=True), jnp.max(s_own, axis=2, keepdims=True)), sink)
        p_prev = jnp.exp(s_prev - m)
        p_own = jnp.exp(s_own - m)
        denom = jnp.sum(p_prev, axis=2, keepdims=True) + jnp.sum(p_own, axis=2, keepdims=True) + jnp.exp(sink - m)
        o = _dot(p_prev.astype(BF16), vp_ref[...], BATCH_PV) + _dot(p_own.astype(BF16), vo_ref[...], BATCH_PV)
        o_ref[...] = o / denom
        lse_ref[...] = m + jnp.log(denom)

    q_spec, own, prev, per_head = _band_specs(hb, rows, k.shape[1])
    return pl.pallas_call(
        body, grid=(h_n // hb, nb),
        in_specs=[q_spec(HEAD_DIM), prev, own, prev, own, per_head(rows, 2 * BLK), per_head(rows, 1)],
        out_specs=[q_spec(HEAD_DIM), q_spec(1)],
        out_shape=[jax.ShapeDtypeStruct(q.shape, F32), jax.ShapeDtypeStruct((h_n, nb, rows, 1), F32)],
        name=name, compiler_params=_params(("parallel", "parallel")),
    )(q, k, k, v, v, bias, sinks)


def banded_backward(name, q, k, v, bias, sinks, o, lse, do, dlse, hb, prev_mask):
    h_n, nb, rows, _ = q.shape
    t_n = k.shape[1]

    def body(q_ref, kp_ref, ko_ref, vp_ref, vo_ref, bias_ref, sink_ref, o_ref, lse_ref, do_ref, dlse_ref,
             dq_ref, dk_ref, dv_ref, dbias_ref, dsink_ref):
        b = pl.program_id(1)

        @pl.when(b == 0)
        def _():
            dk_ref[...] = jnp.zeros_like(dk_ref)
            dv_ref[...] = jnp.zeros_like(dv_ref)
            dbias_ref[...] = jnp.zeros_like(dbias_ref)
            dsink_ref[...] = jnp.zeros_like(dsink_ref)

        qs, s_prev, s_own = _band_scores(q_ref, kp_ref, ko_ref, bias_ref, hb, prev_mask)
        lse_v = lse_ref[...]
        dov = do_ref[...]
        dob = dov.astype(BF16)
        shift = dlse_ref[...] - jnp.sum(dov * o_ref[...], axis=2, keepdims=True)
        p_prev = jnp.exp(s_prev - lse_v)
        p_own = jnp.exp(s_own - lse_v)
        ds_prev = p_prev * (_dot(dob, vp_ref[...], BATCH_QK) + shift)
        ds_own = p_own * (_dot(dob, vo_ref[...], BATCH_QK) + shift)
        dbias_ref[:, :, 0:BLK] += ds_prev
        dbias_ref[:, :, BLK:2 * BLK] += ds_own
        d_sink = jnp.exp(sink_ref[...] - lse_v) * shift
        for g in range(rows // BLK):
            dsink_ref[:, g:g + 1, :] += jnp.sum(d_sink[:, g * BLK:(g + 1) * BLK, :], axis=1, keepdims=True)
        ds_prev = ds_prev.astype(BF16)
        ds_own = ds_own.astype(BF16)
        dq_ref[...] = (_dot(ds_prev, kp_ref[...], BATCH_PV) + _dot(ds_own, ko_ref[...], BATCH_PV)) * QK_SCALE
        rows_prev = pl.ds(pl.multiple_of(jnp.maximum(b - 1, 0) * BLK, BLK), BLK)
        rows_own = pl.ds(pl.multiple_of(b * BLK, BLK), BLK)
        dk_ref[:, rows_prev, :] += _dot(ds_prev, qs, BATCH_TN)
        dk_ref[:, rows_own, :] += _dot(ds_own, qs, BATCH_TN)
        dv_ref[:, rows_prev, :] += _dot(p_prev.astype(BF16), dob, BATCH_TN)
        dv_ref[:, rows_own, :] += _dot(p_own.astype(BF16), dob, BATCH_TN)

    q_spec, own, prev, per_head = _band_specs(hb, rows, t_n)
    kv_full = per_head(t_n, HEAD_DIM)
    kv_shape = jax.ShapeDtypeStruct((h_n, t_n, HEAD_DIM), F32)
    return pl.pallas_call(
        body, grid=(h_n // hb, nb),
        in_specs=[q_spec(HEAD_DIM), prev, own, prev, own, per_head(rows, 2 * BLK), per_head(rows, 1),
                  q_spec(HEAD_DIM), q_spec(1), q_spec(HEAD_DIM), q_spec(1)],
        out_specs=[q_spec(HEAD_DIM), kv_full, kv_full, per_head(rows, 2 * BLK), per_head(rows // BLK, BLK)],
        out_shape=[jax.ShapeDtypeStruct(q.shape, F32), kv_shape, kv_shape,
                   jax.ShapeDtypeStruct((h_n, rows, 2 * BLK), F32), jax.ShapeDtypeStruct((h_n, rows // BLK, BLK), F32)],
        name=name, compiler_params=_params(("parallel", "arbitrary")),
    )(q, k, k, v, v, bias, sinks, o, lse, do, dlse)


def _dil_prev_mask(head):
    group = head // H_PER_DIL
    return jnp.where(group == 0, 15, jnp.where(group == 1, 3, 0))


def _swa_prev_mask(head):
    del head
    return 15


DIL_HEADS_PER_STEP = 3
SWA_GROUP = H_SWA_Q // H_SWA_KV
N_BLK = SEQ // BLK


def dilated_merge(o, lse):
    def body(o_ref, l_ref, out_ref):
        lv = l_ref[...]
        m = jnp.max(lv, axis=0, keepdims=True)
        e = jnp.exp(lv - m)
        alpha = e / jnp.sum(e, axis=0, keepdims=True)
        out_ref[...] = jnp.sum(alpha * o_ref[...], axis=0)

    return pl.pallas_call(
        body, grid=(H_PER_DIL, SEQ // ROW_TILE),
        in_specs=[pl.BlockSpec((3, None, ROW_TILE, HEAD_DIM), lambda h, i: (0, h, i, 0)),
                  pl.BlockSpec((3, None, ROW_TILE, 1), lambda h, i: (0, h, i, 0))],
        out_specs=pl.BlockSpec((None, ROW_TILE, HEAD_DIM), lambda h, i: (h, i, 0)),
        out_shape=jax.ShapeDtypeStruct((H_PER_DIL, SEQ, HEAD_DIM), F32),
        name="dilated_merge", compiler_params=_params(("parallel", "parallel")),
    )(o, lse)


def dilated_merge_bwd(o, lse, dout):
    def body(o_ref, l_ref, d_ref, do_ref, dl_ref):
        lv = l_ref[...]
        m = jnp.max(lv, axis=0, keepdims=True)
        e = jnp.exp(lv - m)
        alpha = e / jnp.sum(e, axis=0, keepdims=True)
        dv = d_ref[...][None]
        do_ref[...] = alpha * dv
        dalpha = jnp.sum(dv * o_ref[...], axis=-1, keepdims=True)
        dl_ref[...] = alpha * (dalpha - jnp.sum(alpha * dalpha, axis=0, keepdims=True))

    o_spec = pl.BlockSpec((3, None, ROW_TILE, HEAD_DIM), lambda h, i: (0, h, i, 0))
    l_spec = pl.BlockSpec((3, None, ROW_TILE, 1), lambda h, i: (0, h, i, 0))
    return pl.pallas_call(
        body, grid=(H_PER_DIL, SEQ // ROW_TILE),
        in_specs=[o_spec, l_spec, pl.BlockSpec((None, ROW_TILE, HEAD_DIM), lambda h, i: (h, i, 0))],
        out_specs=[o_spec, l_spec],
        out_shape=[jax.ShapeDtypeStruct(o.shape, F32), jax.ShapeDtypeStruct(lse.shape, F32)],
        name="dilated_merge_bwd", compiler_params=_params(("parallel", "parallel")),
    )(o, lse, dout)


def rel_bias_reduce(dbias0, dbias1, bucket):
    def body(d0_ref, d1_ref, b_ref, o_ref):
        dv, bv = d0_ref[...] + d1_ref[...], b_ref[...]
        lane = lax.broadcasted_iota(jnp.int32, (1, BLK), 1)
        acc = jnp.zeros((1, BLK), F32)
        for bkt in range(N_BUCKETS):
            acc = acc + jnp.where(lane == bkt, jnp.sum(jnp.where(bv == bkt, dv, 0.0)), 0.0)
        o_ref[...] = acc

    tile = pl.BlockSpec((None, BLK, 2 * BLK), lambda h: (h, 0, 0))
    return pl.pallas_call(
        body, grid=(dbias0.shape[0],), in_specs=[tile, tile, tile],
        out_specs=pl.BlockSpec((None, 1, BLK), lambda h: (h, 0, 0)),
        out_shape=jax.ShapeDtypeStruct((dbias0.shape[0], 1, BLK), F32),
        name="rel_bias_reduce", compiler_params=_params(("parallel",)),
    )(dbias0, dbias1, bucket)


def _heads(t):
    return t.reshape(SEQ, -1, HEAD_DIM).transpose(1, 0, 2)


def _unheads(t):
    return t.transpose(1, 0, 2).reshape(SEQ, -1)


def _dilate(t):
    parts = []
    for g, (_, d) in enumerate(DIL_PATTERNS):
        tg = t[:, 128 * g:128 * (g + 1)].reshape(SEQ // d, d, H_PER_DIL, HEAD_DIM).transpose(2, 1, 0, 3)
        parts.append(tg.reshape(H_PER_DIL, SEQ, HEAD_DIM))
    return jnp.concatenate(parts, axis=0)


def _undilate(t):
    outs = []
    for g, (_, d) in enumerate(DIL_PATTERNS):
        tg = t[2 * g:2 * g + 2].reshape(H_PER_DIL, d, SEQ // d, -1).transpose(0, 2, 1, 3)
        outs.append(tg.reshape(H_PER_DIL, SEQ, -1))
    return jnp.stack(outs)


def _redilate(t):
    parts = []
    for g, (_, d) in enumerate(DIL_PATTERNS):
        tg = t[g].reshape(H_PER_DIL, SEQ // d, d, -1).transpose(0, 2, 1, 3)
        parts.append(tg.reshape(H_PER_DIL, SEQ, -1))
    return jnp.concatenate(parts, axis=0)


def _t5_bucket(n):
    max_exact = N_BUCKETS // 2
    nf = jnp.maximum(n, 1).astype(F32)
    large = max_exact + (jnp.log(nf / max_exact) / math.log(MAX_REL_DIST / max_exact)
                         * (N_BUCKETS - max_exact)).astype(jnp.int32)
    large = jnp.minimum(large, N_BUCKETS - 1)
    return jnp.where(n < max_exact, n, large)


def band_tables(rel_bias):
    rel = jnp.arange(BLK)[:, None] + BLK - jnp.arange(2 * BLK)[None, :]
    buckets = []
    patterns = [(d, w // d) for w, d in DIL_PATTERNS for _ in range(H_PER_DIL)] + [(1, SWA_WINDOW - 1)] * H_SWA_Q
    for d, max_dist in patterns:
        band = (rel >= 0) & (rel <= max_dist)
        buckets.append(jnp.where(band, _t5_bucket(jnp.maximum(rel, 0) * d), -1))
    buckets = jnp.stack(buckets).astype(jnp.int32)

    def body(table_ref, b_ref, o_ref):
        h = pl.program_id(0)
        bv = b_ref[...]
        tile = jnp.full(bv.shape, NEG, F32)
        for bkt in range(N_BUCKETS):
            tile = jnp.where(bv == bkt, table_ref[h, bkt], tile)
        o_ref[...] = tile

    spec = pl.BlockSpec((None, BLK, 2 * BLK), lambda h: (h, 0, 0))
    tiles = pl.pallas_call(
        body, grid=(len(patterns),), in_specs=[pl.BlockSpec(memory_space=pltpu.SMEM), spec], out_specs=spec,
        out_shape=jax.ShapeDtypeStruct(buckets.shape, F32), name="band_tables", compiler_params=_params(("parallel",)),
    )(rel_bias.T, buckets)
    return tiles[:H_DIL], tiles[H_DIL:], buckets


def _swa_rows(t):
    t = t.reshape(N_BLK, BLK, H_SWA_KV, SWA_GROUP, HEAD_DIM).transpose(2, 0, 3, 1, 4)
    return t.reshape(H_SWA_KV, N_BLK, SWA_GROUP * BLK, HEAD_DIM)


def _swa_tokens(t):
    t = t.reshape(H_SWA_KV, N_BLK, SWA_GROUP, BLK, HEAD_DIM).transpose(1, 3, 0, 2, 4)
    return t.reshape(SEQ, H_SWA_Q * HEAD_DIM)


def _sink_rows(sinks):
    return jnp.broadcast_to(sinks.reshape(H_SWA_KV, SWA_GROUP, 1, 1), (H_SWA_KV, SWA_GROUP, BLK, 1)).reshape(
        H_SWA_KV, SWA_GROUP * BLK, 1)


def _no_sinks():
    return jnp.full((H_DIL, BLK, 1), NEG, F32)


def _vec(v):
    return v.reshape(1, D_MODEL)


def ffn_forward(x, gain, mod, w):
    h = prenorm(x, _vec(gain), _vec(mod[1]), _vec(mod[0]))
    a, b, s = ffn_up(h, w[0], w[1])
    f, xo = ffn_down(s, w[2], x, _vec(mod[2]))
    return xo, (x, h, a, b, s, f)


def ffn_backward(dxo, saved, gain, mod, w):
    x, h, a, b, s, f = saved
    df, dgate = resid_bwd(dxo, f, _vec(mod[2]), 0.5)
    da, db = ffn_bwd_hidden(df, w[2], a, b)
    grads = ffn_grad_weights(h, s, df, da, db)
    dx, stats = ffn_bwd_input(da, db, w[0], w[1], x, dxo, _vec(gain), _vec(mod[1]))
    return dx, jnp.stack([stats[0], stats[1], dgate[0]]), stats[2], grads


def mixer_forward(x, gain, mod, sinks, bias_dil, bias_swa, w):
    h = prenorm(x, _vec(gain), _vec(mod[1]), _vec(mod[0]))
    proj = in_proj(h, w[0])
    qkv = proj[:, :D_QKV].astype(BF16)
    q_sb, k_sb, v_sb = _heads(qkv[:, 0:256] * QK_SCALE), _heads(qkv[:, 256:512]), _heads(qkv[:, 512:768])
    q_dil, k_dil, v_dil = _dilate(qkv[:, 768:1152] * QK_SCALE), _dilate(qkv[:, 1152:1536]), _dilate(qkv[:, 1536:1920])
    q_swa, k_swa, v_swa = _swa_rows(qkv[:, 1920:2304] * QK_SCALE), _heads(qkv[:, 2304:2432]), _heads(qkv[:, 2432:2560])
    o_sb, total_sb = sb_forward(q_sb, k_sb, v_sb)
    q_dil = q_dil.reshape(H_DIL, N_BLK, BLK, HEAD_DIM)
    o_dd, lse_dd = banded_forward("dilated_forward", q_dil, k_dil, v_dil, bias_dil, _no_sinks(), DIL_HEADS_PER_STEP,
                                  _dil_prev_mask)
    o_dt, lse_dt = _undilate(o_dd.reshape(H_DIL, SEQ, HEAD_DIM)), _undilate(lse_dd.reshape(H_DIL, SEQ, 1))
    o_dil = dilated_merge(o_dt, lse_dt)
    bias_swa = bias_swa.reshape(H_SWA_KV, SWA_GROUP * BLK, 2 * BLK)
    o_swa, lse_swa = banded_forward("swa_forward", q_swa, k_swa, v_swa, bias_swa, _sink_rows(sinks), 1, _swa_prev_mask)
    o_cat = jnp.concatenate([_unheads(o_sb), _unheads(o_dil), _swa_tokens(o_swa)], axis=1)
    merged = merge_branches(o_cat, w[1], proj)
    mo, xo = out_proj(merged, w[2], x, _vec(mod[2]))
    saved = (x, h, proj, (q_sb, k_sb, v_sb, total_sb), (q_dil, k_dil, v_dil, o_dd, lse_dd, o_dt, lse_dt),
             (q_swa, k_swa, v_swa, o_swa, lse_swa), o_cat, merged, mo)
    return xo, saved


def mixer_backward(dxo, saved, gain, mod, sinks, bias_dil, bias_swa, w):
    x, h, proj, sb, dil, swa, o_cat, merged, mo = saved
    dmo, dgate = resid_bwd(dxo, mo, _vec(mod[2]), 1.0)
    tok = pl.BlockSpec((MM_TILE, D_MODEL), lambda j, k: (k, 0))
    g_out = grad_weight("grad_w_out", merged, pl.BlockSpec((MM_TILE, D_SHARD), lambda j, k: (k, j)), dmo, tok,
                        (D_SHARD, D_MODEL))
    du0, du1, du2, dg0, dg1, dg2 = merge_bwd(dmo, w[2], o_cat, w[1], proj)
    du = (du0, du1, du2)
    do_cat = branch_bwd_input(du, w[1])
    g_br = branch_grad_weights(o_cat, du)

    q_sb, k_sb, v_sb, total_sb = sb
    dq_sb, dk_sb, dv_sb = sb_backward(q_sb, k_sb, v_sb, total_sb, _heads(do_cat[:, 0:256]))

    q_dil, k_dil, v_dil, o_dd, lse_dd, o_dt, lse_dt = dil
    do_dt, dlse_dt = dilated_merge_bwd(o_dt, lse_dt, _heads(do_cat[:, 256:384]))
    dq_dil, dk_dil, dv_dil, dbias_dil, _ = banded_backward(
        "dilated_backward", q_dil, k_dil, v_dil, bias_dil, _no_sinks(), o_dd, lse_dd,
        _redilate(do_dt).reshape(q_dil.shape), _redilate(dlse_dt).reshape(lse_dd.shape), DIL_HEADS_PER_STEP, _dil_prev_mask)

    q_swa, k_swa, v_swa, o_swa, lse_swa = swa
    bias_swa = bias_swa.reshape(H_SWA_KV, SWA_GROUP * BLK, 2 * BLK)
    dq_swa, dk_swa, dv_swa, dbias_swa, dsinks = banded_backward(
        "swa_backward", q_swa, k_swa, v_swa, bias_swa, _sink_rows(sinks), o_swa, lse_swa, _swa_rows(do_cat[:, 384:768]),
        jnp.zeros_like(lse_swa), 1, _swa_prev_mask)
    dbias_swa = dbias_swa.reshape(H_SWA_Q, BLK, 2 * BLK)

    def tokens(t):
        return _undilate(t).transpose(2, 0, 1, 3).reshape(SEQ, -1)

    dproj = jnp.concatenate(
        [_unheads(dq_sb), _unheads(dk_sb), _unheads(dv_sb), tokens(dq_dil.reshape(H_DIL, SEQ, HEAD_DIM)), tokens(dk_dil),
         tokens(dv_dil), _swa_tokens(dq_swa), _unheads(dk_swa), _unheads(dv_swa)], axis=1).astype(BF16)
    dproj = jnp.concatenate([dproj, dg0, dg1, dg2], axis=1)
    g_in = grad_weight("grad_w_in", h, tok, dproj, pl.BlockSpec((MM_TILE, IN_SHARD), lambda j, k: (k, j)),
                       (D_MODEL, IN_SHARD))
    dx, stats = mixer_bwd_input(dproj, w[0], x, dxo, _vec(gain), _vec(mod[1]))
    dmod = jnp.stack([stats[0], stats[1], dgate[0]])
    dbias = jnp.concatenate([dbias_dil, dbias_swa], axis=0)
    return dx, dmod, stats[2], dbias, dsinks[:, :, 0].reshape(H_SWA_Q), (g_in, g_br, g_out)


N_UNITS = 3 * DEPTH


def device_step(x, target, mod, gains, final_gain, sinks, rel_bias, get_weights, put_grads):
    bias_dil, bias_swa, bucket = band_tables(rel_bias)
    saved, weights = [], []
    for u in range(N_UNITS):
        l, j = divmod(u, 3)
        w = get_weights(u, x)
        if j == 1:
            x, s = mixer_forward(x, gains[l, 1], mod[l, 1], sinks[l], bias_dil, bias_swa, w)
        else:
            x, s = ffn_forward(x, gains[l, j], mod[l, j], w)
        saved.append(s)
        weights.append(w)
    loss, dx, dfinal = final_loss(x, _vec(final_gain), target)

    dmod = [[None] * 3 for _ in range(DEPTH)]
    dgains = [[None] * 3 for _ in range(DEPTH)]
    dbias, dsinks = [None] * DEPTH, [None] * DEPTH
    zero = jnp.zeros((1, 1), F32)
    for u in reversed(range(N_UNITS)):
        l, j = divmod(u, 3)
        gain = gains[l, j] + zero[0]
        if j == 1:
            dx, dmod[l][j], dgains[l][j], dbias[l], dsinks[l], grads = mixer_backward(
                dx, saved[u], gain, mod[l, 1], sinks[l], bias_dil, bias_swa, weights[u])
        else:
            dx, dmod[l][j], dgains[l][j], grads = ffn_backward(dx, saved[u], gain, mod[l, j], weights[u])
        if u > 0:
            zero = put_grads(u, grads)
    drel = rel_bias_reduce(dbias[0], dbias[1], bucket)[:, 0, :N_BUCKETS].T
    stack2 = lambda t: jnp.stack([jnp.stack(r) for r in t])
    return loss, dx, stack2(dmod), stack2(dgains), dfinal[0], jnp.stack(dsinks), drel, grads


MESH = pl.DeviceIdType.MESH
CHIP_FLIPS = ((1, 0), (0, 1), (1, 1))
ANY = pl.BlockSpec(memory_space=pl.ANY)


def _position():
    return lax.axis_index("x"), lax.axis_index("y"), lax.axis_index("c")


def all_gather_small(name, piece):
    def body(x_ref, out_ref, send_sems, recv_sems, local_sem):
        x, y, c = _position()
        me, sibling = (x, y, c), (x, y, 1 - c)
        chips = [(x ^ fx, y ^ fy) for fx, fy in CHIP_FLIPS]

        def rows(px, py, pc):
            return out_ref.at[4 * px + 2 * py + pc]

        def copy(k, block, to, src=None):
            return pltpu.make_async_remote_copy(
                src_ref=rows(*block) if src is None else src, dst_ref=rows(*block),
                send_sem=send_sems.at[k], recv_sem=recv_sems.at[k], device_id=to, device_id_type=MESH)

        mine = pltpu.make_async_copy(x_ref, rows(*me), local_sem)
        mine.start()
        first = [copy(0, me, sibling, src=x_ref)]
        first += [copy(1 + j, me, (*chip, c), src=x_ref) for j, chip in enumerate(chips)]
        for cp in first:
            cp.start()
        passed = [copy(4 + j, (*chip, c), sibling) for j, chip in enumerate(chips)]
        for j, chip in enumerate(chips):
            copy(1 + j, (*chip, c), me).wait_recv()
            passed[j].start()
        copy(0, sibling, me).wait_recv()
        for j, chip in enumerate(chips):
            copy(4 + j, (*chip, 1 - c), me).wait_recv()
        for cp in first + passed:
            cp.wait_send()
        mine.wait()

    return pl.pallas_call(
        body, out_shape=jax.ShapeDtypeStruct((N_DEV,) + piece.shape, piece.dtype),
        in_specs=[pl.BlockSpec(memory_space=pltpu.VMEM)], out_specs=pl.BlockSpec(memory_space=pltpu.VMEM),
        scratch_shapes=[pltpu.SemaphoreType.DMA((7,)), pltpu.SemaphoreType.DMA((7,)), pltpu.SemaphoreType.DMA],
        name=name,
    )(piece)


def exchange(name, operands, out_shapes, aliases, plan):
    n_in, n_out = len(operands), len(out_shapes)

    def body(*refs):
        ins, outs = refs[:n_in], refs[n_in:n_in + n_out]
        send_sems, recv_sems, local_sems = refs[n_in + n_out:]
        x, y, c = _position()
        local, sends, recvs = plan(ins, outs, x, y, c)
        local = [pltpu.make_async_copy(s, d, local_sems.at[k]) for k, (s, d) in enumerate(local)]
        for cp in local:
            cp.start()
        remote = [pltpu.make_async_remote_copy(src_ref=s, dst_ref=d, send_sem=send_sems.at[k], recv_sem=recv_sems.at[k],
                                               device_id=dev, device_id_type=MESH)
                  for k, (s, d, dev) in enumerate(sends)]
        for cp in remote:
            cp.start()
        for k, r in enumerate(recvs):
            pltpu.make_async_remote_copy(src_ref=r, dst_ref=r, send_sem=send_sems.at[k], recv_sem=recv_sems.at[k],
                                         device_id=(x, y, c), device_id_type=MESH).wait_recv()
        for cp in remote:
            cp.wait_send()
        for cp in local:
            cp.wait()

    n_sends, n_local = plan.n_sends, max(plan.n_local, 1)
    return pl.pallas_call(
        body, out_shape=out_shapes, in_specs=[ANY] * n_in, out_specs=[ANY] * n_out,
        scratch_shapes=[pltpu.SemaphoreType.DMA((n_sends,)), pltpu.SemaphoreType.DMA((n_sends,)),
                        pltpu.SemaphoreType.DMA((n_local,))],
        input_output_aliases=aliases, name=name,
    )(*operands)


def _plan(n_local, n_sends):
    def wrap(fn):
        fn.n_local, fn.n_sends = n_local, n_sends
        return fn
    return wrap


def _half(ref, axis, c):
    rows = ref.shape[axis] // 2
    idx = [slice(None)] * len(ref.shape)
    idx[axis] = pl.ds(pl.multiple_of(c * rows, 16), rows)
    return ref.at[tuple(idx)]


HBM = pl.BlockSpec(memory_space=pltpu.HBM)
SEM = pl.BlockSpec(memory_space=pltpu.SEMAPHORE)
EFFECT = pltpu.SideEffectType.DATAFLOW_SIDE_EFFECTING


def split_start(name, bufs, extra, n_copies, describe):
    n = len(bufs)

    def body(*refs):
        send_sems, recv_sems = refs[n + len(extra)], refs[n + len(extra) + 1]
        x, y, c = _position()
        for k, (src, dst, _, peer) in enumerate(describe(refs[:n], x, y, c)):
            pltpu.make_async_remote_copy(src_ref=src, dst_ref=dst, send_sem=send_sems.at[k], recv_sem=recv_sems.at[k],
                                         device_id=peer, device_id_type=MESH).start()
        token = refs[-1]
        token[...] = jnp.zeros_like(token)

    out = pl.pallas_call(
        body, name=name,
        out_shape=(pltpu.SemaphoreType.DMA((n_copies,)), pltpu.SemaphoreType.DMA((n_copies,)),
                   *[pltpu.HBM(b.shape, b.dtype) for b in bufs], jax.ShapeDtypeStruct((8, 128), F32)),
        in_specs=[HBM] * n + [ANY] * len(extra),
        out_specs=(SEM, SEM, *[HBM] * n, pl.BlockSpec(memory_space=pltpu.VMEM)),
        input_output_aliases={k: 2 + k for k in range(n)},
        compiler_params=pltpu.CompilerParams(has_side_effects=EFFECT),
    )(*[pltpu.with_memory_space_constraint(b, pltpu.HBM) for b in bufs], *extra)
    return out[0], out[1], list(out[2:2 + n]), out[-1]


def split_wait(name, bufs, send_sems, recv_sems, after, describe):
    n = len(bufs)

    def body(*refs):
        send, recv = refs[n], refs[n + 1]
        x, y, c = _position()
        for k, (src, _, dst, peer) in enumerate(describe(refs[:n], x, y, c)):
            copy = pltpu.make_async_remote_copy(src_ref=src, dst_ref=dst, send_sem=send.at[k], recv_sem=recv.at[k],
                                                device_id=peer, device_id_type=MESH)
            copy.wait_send()
            copy.wait_recv()

    out = pl.pallas_call(
        body, name=name, out_shape=[pltpu.HBM(b.shape, b.dtype) for b in bufs],
        in_specs=[HBM] * n + [SEM, SEM] + [ANY] * len(after), out_specs=[HBM] * n,
        input_output_aliases={k: k for k in range(n)},
        compiler_params=pltpu.CompilerParams(has_side_effects=EFFECT),
    )(*bufs, send_sems, recv_sems, *after)
    return list(out)


def _row_tile(rows, cols):
    best = 16
    for t in range(16, rows + 1, 16):
        if rows % t == 0 and t * cols <= 256 * 1024:
            best = t
    return best


def cast_into_slot(name, param, index, chip):
    rows, cols = param.shape[-2:]
    tr = _row_tile(rows, cols)
    lead = (None,) * len(index)

    def body(chip_ref, s_ref, o_ref):
        del chip_ref
        o_ref[...] = s_ref[...].astype(BF16)

    return pl.pallas_call(
        body, out_shape=jax.ShapeDtypeStruct((N_CHIPS, rows, cols), BF16),
        grid_spec=pltpu.PrefetchScalarGridSpec(
            num_scalar_prefetch=1, grid=(rows // tr,),
            in_specs=[pl.BlockSpec(lead + (tr, cols), lambda r, chip_ref: index + (r, 0))],
            out_specs=pl.BlockSpec((None, tr, cols), lambda r, chip_ref: (chip_ref[0], r, 0))),
        name=name, compiler_params=_params(("parallel",)),
    )(chip, param)


GATHER_STAGES = ((0,), (1,), (2,), (3, 4, 5))
REDUCE_STAGES = ((5, 4, 3), (2, 1), (0,))


def _gather_copies(slots, x, y, c):
    me = 2 * x + y
    out = []
    for s in slots:
        for fx, fy in CHIP_FLIPS:
            mine = _half(s.at[me], 0, c)
            out.append((mine, mine, _half(s.at[2 * (x ^ fx) + (y ^ fy)], 0, c), (x ^ fx, y ^ fy, c)))
    return out


class WeightStream:
    def __init__(self, shards, chip, after=()):
        self.pending, self.ready = {}, {}
        token = tuple(after)
        for si, units in enumerate(GATHER_STAGES):
            slots = [cast_into_slot(f"cast_{u}_{t}", p, idx, chip) for u in units for t, (p, idx) in enumerate(shards[u])]
            send, recv, slots, tok = split_start(f"gather_start_{si}", slots, token, 3 * len(slots), _gather_copies)
            self.pending[si] = (send, recv, slots)
            token = (tok,)
        self.token = token

    def get(self, u, after):
        if u not in self.ready:
            si = next(k for k, units in enumerate(GATHER_STAGES) if u in units)
            send, recv, slots = self.pending.pop(si)
            slots = split_wait(f"gather_wait_{si}", slots, send, recv, (after,) + self.token, _gather_copies)
            self.token = ()

            @_plan(0, 3 * len(slots))
            def to_sibling(ins, outs, x, y, c):
                sends, recvs = [], []
                for o in outs:
                    for fx, fy in CHIP_FLIPS:
                        slab = o.at[2 * (x ^ fx) + (y ^ fy)]
                        sends.append((_half(slab, 0, c), _half(slab, 0, c), (x, y, 1 - c)))
                        recvs.append(_half(slab, 0, 1 - c))
                return [], sends, recvs

            shapes = [jax.ShapeDtypeStruct(s.shape, BF16) for s in slots]
            slots = exchange(f"gather_sibling_{si}", slots, shapes, {k: k for k in range(len(slots))}, to_sibling)
            for i, v in enumerate(GATHER_STAGES[si]):
                self.ready[v] = tuple(slots[3 * i:3 * i + 3])
        return self.ready[u]


def _reduce_copies(bufs, x, y, c):
    n = len(bufs) // 2
    out = []
    for s, land in zip(bufs[:n], bufs[n:]):
        for k, (fx, fy) in enumerate(CHIP_FLIPS):
            out.append((s.at[2 * (x ^ fx) + (y ^ fy)], land.at[k], land.at[k], (x ^ fx, y ^ fy, c)))
    return out


GRAD_SLOTS = {"gate": (2 * DEPTH, D_MODEL, FF_SHARD), "up": (2 * DEPTH, D_MODEL, FF_SHARD),
              "down": (2 * DEPTH, FF_SHARD, D_MODEL), "in": (DEPTH, D_MODEL, IN_SHARD),
              "br": (DEPTH, BR_ROWS, D_SHARD), "out": (DEPTH, D_SHARD, D_MODEL)}


def _unit_tensors(u):
    l, j = divmod(u, 3)
    if j == 1:
        return [("in", l), ("br", l), ("out", l)]
    return [(k, 2 * l + j // 2) for k in ("gate", "up", "down")]


class GradStream:
    def __init__(self, chip, core):
        self.core = core
        self.place = jnp.concatenate([chip, core])
        self.held, self.flying = {}, []
        self.full = {k: lax.empty(shape, F32) for k, shape in GRAD_SLOTS.items()}

    def put(self, u, grads, after=()):
        self.held[u] = grads
        si = len(self.flying)
        units = REDUCE_STAGES[si]
        if not all(v in self.held for v in units):
            return jnp.zeros((1, 1), F32)
        gs = [g for v in units for g in self.held[v]]

        @_plan(0, len(gs))
        def swap_halves(ins, outs, x, y, c):
            sends = [(_half(g, 1, 1 - c), o, (x, y, 1 - c)) for g, o in zip(ins, outs)]
            return [], sends, list(outs)

        half_shapes = [jax.ShapeDtypeStruct((N_CHIPS, g.shape[1] // 2, g.shape[2]), BF16) for g in gs]
        landed = exchange(f"reduce_swap_{si}", gs + list(after), half_shapes, {}, swap_halves)
        sums = [_add_halves(g, la, self.core) for g, la in zip(gs, landed)]
        landing = [lax.empty((3,) + s.shape[1:], BF16) for s in sums]
        send, recv, bufs, token = split_start(f"reduce_start_{si}", sums + landing, (), 3 * len(sums), _reduce_copies)
        self.flying.append((send, recv, bufs, [t for v in units for t in _unit_tensors(v)]))
        return token[0:1, 0:1]

    def finish(self, after):
        for si, (send, recv, bufs, tensors) in enumerate(self.flying):
            bufs = split_wait(f"reduce_wait_{si}", bufs, send, recv, tuple(after), _reduce_copies)
            n = len(tensors)
            for (name, slot), s, land in zip(tensors, bufs[:n], bufs[n:]):
                self.full[name] = _add_chips(s, land, self.place, self.full[name], slot)
        names = list(self.full)

        @_plan(0, len(names))
        def share_halves(ins, outs, x, y, c):
            sends = [(_half(o, 1, c), _half(o, 1, c), (x, y, 1 - c)) for o in outs]
            return [], sends, [_half(o, 1, 1 - c) for o in outs]

        shapes = [jax.ShapeDtypeStruct(self.full[k].shape, F32) for k in names]
        out = exchange("reduce_share_halves", [self.full[k] for k in names], shapes, {k: k for k in range(len(names))},
                       share_halves)
        return dict(zip(names, out))


def _add_halves(g, landed, core):
    _, rh, cols = landed.shape
    tr = _row_tile(rh, cols)
    per_half = rh // tr

    def body(core_ref, g_ref, la_ref, o_ref):
        del core_ref
        o_ref[...] = (g_ref[...].astype(F32) + la_ref[...].astype(F32)).astype(BF16)

    blk = (None, tr, cols)
    return pl.pallas_call(
        body, out_shape=jax.ShapeDtypeStruct(landed.shape, BF16),
        grid_spec=pltpu.PrefetchScalarGridSpec(
            num_scalar_prefetch=1, grid=(N_CHIPS, per_half),
            in_specs=[pl.BlockSpec(blk, lambda j, r, core_ref: (j, core_ref[0] * per_half + r, 0)),
                      pl.BlockSpec(blk, lambda j, r, core_ref: (j, r, 0))],
            out_specs=pl.BlockSpec(blk, lambda j, r, core_ref: (j, r, 0))),
        name="reduce_add_halves", compiler_params=_params(("parallel", "parallel")),
    )(core, g, landed)


def _add_chips(sums, landed, place, full, slot):
    _, rh, cols = sums.shape
    tr = _row_tile(rh, cols)
    per_half = rh // tr

    def body(place_ref, s_ref, la_ref, full_in, o_ref):
        del place_ref, full_in
        o_ref[...] = ((s_ref[...].astype(F32) + la_ref[0].astype(F32)) + la_ref[1].astype(F32)) + la_ref[2].astype(F32)

    return pl.pallas_call(
        body, out_shape=jax.ShapeDtypeStruct(full.shape, F32),
        grid_spec=pltpu.PrefetchScalarGridSpec(
            num_scalar_prefetch=1, grid=(per_half,),
            in_specs=[pl.BlockSpec((None, tr, cols), lambda r, place_ref: (place_ref[0], r, 0)),
                      pl.BlockSpec((3, tr, cols), lambda r, place_ref: (0, r, 0)), ANY],
            out_specs=pl.BlockSpec((None, tr, cols), lambda r, place_ref: (slot, place_ref[1] * per_half + r, 0))),
        input_output_aliases={3: 0}, name="reduce_add_chips", compiler_params=_params(("parallel",)),
    )(place, sums, landed, full)


def sum_devices(parts):
    def body(p_ref, o_ref):
        acc = p_ref[0]
        for d in range(1, N_DEV):
            acc = acc + p_ref[d]
        o_ref[...] = acc

    return pl.pallas_call(body, out_shape=jax.ShapeDtypeStruct(parts.shape[1:], F32), name="sum_devices")(parts)


ADA_SHARD = 9 * D_MODEL // N_CHIPS
ADA_TILE = 768
ADA_ROWS = 16


def ada_forward(c_rows, w_ada, b_shard):
    def body(c_ref, w_ref, b_ref, o_ref):
        cv = c_ref[...]
        o_ref[...] = _dot((cv * _sigmoid(cv)).astype(BF16), w_ref[...].astype(BF16), NN) + b_ref[...]

    return pl.pallas_call(
        body, grid=(DEPTH, ADA_SHARD // ADA_TILE),
        in_specs=[pl.BlockSpec((ADA_ROWS, D_MODEL), lambda l, n: (0, 0)),
                  pl.BlockSpec((None, D_MODEL, ADA_TILE), lambda l, n: (l, 0, n)),
                  pl.BlockSpec((None, 1, ADA_TILE), lambda l, n: (l, 0, n))],
        out_specs=pl.BlockSpec((None, ADA_ROWS, ADA_TILE), lambda l, n: (l, 0, n)),
        out_shape=jax.ShapeDtypeStruct((DEPTH, ADA_ROWS, ADA_SHARD), F32),
        name="ada_forward", compiler_params=_params(("parallel", "parallel")),
    )(c_rows, w_ada, b_shard)


def ada_backward(c_rows, dmod_rows):
    def body(c_ref, d_ref, o_ref):
        cv = c_ref[...]
        o_ref[...] = _dot((cv * _sigmoid(cv)).astype(BF16), d_ref[...].astype(BF16), TN)

    return pl.pallas_call(
        body, grid=(DEPTH, ADA_SHARD // ADA_TILE),
        in_specs=[pl.BlockSpec((ADA_ROWS, D_MODEL), lambda l, n: (0, 0)),
                  pl.BlockSpec((None, ADA_ROWS, ADA_TILE), lambda l, n: (l, 0, n))],
        out_specs=pl.BlockSpec((None, D_MODEL, ADA_TILE), lambda l, n: (l, 0, n)),
        out_shape=jax.ShapeDtypeStruct((DEPTH, D_MODEL, ADA_SHARD), F32),
        name="ada_backward", compiler_params=_params(("parallel", "parallel")),
    )(c_rows, dmod_rows)


def adamw(name, w, g, m, v):
    shape = w.shape
    cols = shape[-1]
    rows = w.size // cols
    tr = _row_tile(rows, cols) if rows % 16 == 0 else rows
    c1 = 1.0 / (1.0 - ADAM_B1 ** ADAM_STEP)
    c2 = 1.0 / (1.0 - ADAM_B2 ** ADAM_STEP)

    def body(w_ref, g_ref, m_ref, v_ref, go_ref, d_ref, mo_ref, vo_ref):
        gv = g_ref[...]
        mn = ADAM_B1 * m_ref[...] + (1.0 - ADAM_B1) * gv
        vn = ADAM_B2 * v_ref[...] + (1.0 - ADAM_B2) * (gv * gv)
        go_ref[...] = gv
        mo_ref[...] = mn
        vo_ref[...] = vn
        d_ref[...] = -ADAM_LR * ((mn * c1) / (jnp.sqrt(vn * c2) + ADAM_EPS) + ADAM_WD * w_ref[...])

    spec = pl.BlockSpec((tr, cols), lambda i: (i, 0))
    out = jax.ShapeDtypeStruct((rows, cols), F32)
    res = pl.pallas_call(
        body, grid=(rows // tr,), in_specs=[spec] * 4, out_specs=[spec] * 4, out_shape=[out] * 4,
        name=name, compiler_params=_params(("parallel",)),
    )(*[t.reshape(rows, cols) for t in (w, g, m, v)])
    return tuple(r.reshape(shape) for r in res)


def _pack(parts, rows):
    flat = jnp.concatenate([p.reshape(-1) for p in parts])
    return jnp.pad(flat, (0, rows * 128 - flat.size)).reshape(rows, 128)


def _unpack(flat, shapes):
    out, at = [], 0
    for s in shapes:
        n = math.prod(s)
        out.append(flat[at:at + n].reshape(s))
        at += n
    return out


def kernel(x, c, w_ada, b_ada, norm_gain, w_ffn_gate, w_ffn_up, w_ffn_down, w_in, w_br_sb, w_br_dil, w_br_swa, w_out, sinks, rel_bias, final_gain, loss_target, m_w_ada, m_b_ada, m_norm_gain, m_w_ffn_gate, m_w_ffn_up, m_w_ffn_down, m_w_in, m_w_br_sb, m_w_br_dil, m_w_br_swa, m_w_out, m_sinks, m_rel_bias, m_final_gain, v_w_ada, v_b_ada, v_norm_gain, v_w_ffn_gate, v_w_ffn_up, v_w_ffn_down, v_w_in, v_w_br_sb, v_w_br_dil, v_w_br_swa, v_w_out, v_sinks, v_rel_bias, v_final_gain):
    xi, yi, ci = _position()
    chip = 2 * xi + yi
    dev = 2 * chip + ci

    c_all = all_gather_small("gather_c", c.reshape(8, 128)).reshape(N_DEV, D_MODEL)
    c_rows = jnp.pad(c_all, ((0, ADA_ROWS - N_DEV), (0, 0)))
    b_shard = lax.dynamic_slice_in_dim(b_ada, chip * ADA_SHARD, ADA_SHARD, axis=1).reshape(DEPTH, 1, ADA_SHARD)
    mod_shard = ada_forward(c_rows, w_ada, b_shard)[:, :N_DEV]
    n_mod = DEPTH * N_DEV * ADA_SHARD
    gathered = all_gather_small("gather_mod", _pack([mod_shard, norm_gain], 304))[::2].reshape(N_CHIPS, -1)
    mod_all = gathered[:, :n_mod].reshape(N_CHIPS, DEPTH, N_DEV, ADA_SHARD)
    mod = lax.dynamic_index_in_dim(mod_all, dev, axis=2, keepdims=False)
    mod = mod.transpose(1, 0, 2).reshape(DEPTH, 3, 3, D_MODEL)
    gains = gathered[:, n_mod:n_mod + DEPTH * 3 * D_SHARD].reshape(N_CHIPS, DEPTH, 3, D_SHARD)
    gains = gains.transpose(1, 2, 0, 3).reshape(DEPTH, 3, D_MODEL)

    chip_i, core_i = chip.astype(jnp.int32).reshape(1), ci.astype(jnp.int32).reshape(1)
    w_br = jnp.concatenate([w_br_sb, w_br_dil, w_br_swa], axis=1)
    shards = []
    for l in range(DEPTH):
        ffn = [[(w_ffn_gate, (l, f)), (w_ffn_up, (l, f)), (w_ffn_down, (l, f))] for f in range(2)]
        shards += [ffn[0], [(w_in, (l,)), (w_br, (l,)), (w_out, (l,))], ffn[1]]
    weights_in = WeightStream(shards, chip_i, (gathered,))
    grads_out = GradStream(chip_i, core_i)

    loss, dx, dmod, dgains, dfinal, dsinks, drel, last_grads = device_step(
        x[0], loss_target[0], mod, gains, final_gain, sinks, rel_bias, weights_in.get, grads_out.put)

    small_shapes = [(DEPTH, 9 * D_MODEL), (DEPTH, 3, D_MODEL), (D_MODEL,), (DEPTH, H_SWA_Q), (N_BUCKETS, 12), (1,)]
    small_all = all_gather_small("gather_small_grads", _pack([dmod, dgains, dfinal, dsinks, drel, loss[0, 0:1]], 208))
    started = grads_out.put(0, last_grads, after=(small_all,))
    small_all = small_all + started
    g_b_ada, g_gain_full, g_final, g_sinks, g_rel, loss_sum = _unpack(sum_devices(small_all).reshape(-1), small_shapes)
    g_gain = lax.dynamic_slice_in_dim(g_gain_full, chip * D_SHARD, D_SHARD, axis=2)
    dmod_all = small_all.reshape(N_DEV, -1)[:, :DEPTH * 9 * D_MODEL].reshape(N_DEV, DEPTH, 9 * D_MODEL)
    dmod_rows = lax.dynamic_slice_in_dim(dmod_all, chip * ADA_SHARD, ADA_SHARD, axis=2).transpose(1, 0, 2)
    g_w_ada = ada_backward(c_rows, jnp.pad(dmod_rows, ((0, 0), (0, ADA_ROWS - N_DEV), (0, 0))))

    weights = [w_ada, b_ada, norm_gain, w_ffn_gate, w_ffn_up, w_ffn_down, w_in, w_br_sb, w_br_dil, w_br_swa, w_out,
               sinks, rel_bias, final_gain]
    ms = [m_w_ada, m_b_ada, m_norm_gain, m_w_ffn_gate, m_w_ffn_up, m_w_ffn_down, m_w_in, m_w_br_sb, m_w_br_dil,
          m_w_br_swa, m_w_out, m_sinks, m_rel_bias, m_final_gain]
    vs = [v_w_ada, v_b_ada, v_norm_gain, v_w_ffn_gate, v_w_ffn_up, v_w_ffn_down, v_w_in, v_w_br_sb, v_w_br_dil,
          v_w_br_swa, v_w_out, v_sinks, v_rel_bias, v_final_gain]
    grads = [g_w_ada, g_b_ada, g_gain] + [None] * 8 + [g_sinks, g_rel, g_final]

    small = (1, 2, 11, 12, 13)
    deltas, new_ms, new_vs = [None] * 14, [None] * 14, [None] * 14
    _, deltas[0], new_ms[0], new_vs[0] = adamw("adamw_0", weights[0], grads[0], ms[0], vs[0])
    shapes = [weights[k].shape for k in small]
    packed = [_pack([t[k] for k in small], 168) for t in (weights, grads, ms, vs)]
    for dst, res in zip((deltas, new_ms, new_vs), adamw("adamw_small", *packed)[1:]):
        for k, t in zip(small, _unpack(res.reshape(-1), shapes)):
            dst[k] = t

    g = grads_out.finish((dx, deltas[0], deltas[1]))
    g_br = g["br"]
    grads[3:11] = [g["gate"].reshape(w_ffn_gate.shape), g["up"].reshape(w_ffn_up.shape),
                   g["down"].reshape(w_ffn_down.shape), g["in"], g_br[:, 0:256], g_br[:, 256:384], g_br[:, 384:768],
                   g["out"]]
    for k in range(3, 11):
        grads[k], deltas[k], new_ms[k], new_vs[k] = adamw(f"adamw_{k}", weights[k], grads[k], ms[k], vs[k])
    return (loss_sum[0], dx[None], *grads, *deltas, *new_ms, *new_vs)
```

```python
import functools
import math

import jax
import jax.numpy as jnp
from jax import lax
from jax.experimental import pallas as pl
from jax.experimental.pallas import tpu as pltpu

F32 = jnp.float32
BF16 = jnp.bfloat16

D_MODEL = 1024
SEQ = 2048
DEPTH = 2
HEAD_DIM = 64
BLK = 128
H_SB = 4
DIL_PATTERNS = ((128, 1), (512, 4), (2048, 16))
H_PER_DIL = 2
H_DIL = 6
H_SWA_Q = 6
H_SWA_KV = 2
SWA_WINDOW = 128
N_BUCKETS = 32
MAX_REL_DIST = 2048
D_FF = 2816
RMS_EPS = 1e-6
N_CHIPS = 4
N_DEV = 8
FF_SHARD = D_FF // N_CHIPS
D_QKV = 2560
D_IN = D_QKV + 3 * D_MODEL
IN_SHARD = D_IN // N_CHIPS
D_SHARD = D_MODEL // N_CHIPS
BR_ROWS = 768
NEG = -1e30
QK_SCALE = HEAD_DIM ** -0.5

ADAM_LR = 0.001
ADAM_B1 = 0.9
ADAM_B2 = 0.999
ADAM_EPS = 1e-08
ADAM_WD = 0.01
ADAM_STEP = 10

VMEM_LIMIT = 48 * 1024 * 1024
ROW_TILE = 256
MM_TILE = 512

NN = (((1,), (0,)), ((), ()))
NT = (((1,), (1,)), ((), ()))
TN = (((0,), (0,)), ((), ()))


def _params(sem=None):
    return pltpu.CompilerParams(dimension_semantics=sem, vmem_limit_bytes=VMEM_LIMIT)


def _dot(a, b, dims):
    return lax.dot_general(a, b, dims, preferred_element_type=F32)


def _sigmoid(x):
    return 1.0 / (1.0 + jnp.exp(-x))


def _matmul(name, grid, nk, k_axis, dims, n_pairs, in_specs, out_specs, out_shape, acc_shape, epilogue,
            operands, sem, aliases=None, prologue=None):
    n_in = len(in_specs)
    n_out = len(out_specs)

    def partial(ins):
        tot = None
        for p in range(n_pairs):
            a = ins[2 * p][...]
            if prologue is not None:
                a = prologue(p, a, ins)
            d = _dot(a, ins[2 * p + 1][...], dims)
            tot = d if tot is None else tot + d
        return tot

    def body(*refs):
        ins, outs = refs[:n_in], refs[n_in:n_in + n_out]
        ids = tuple(pl.program_id(a) for a in range(len(grid)))
        if nk == 1:
            epilogue(partial(ins), ins, outs, ids)
            return
        acc = refs[n_in + n_out]
        k = ids[k_axis]

        @pl.when(k == 0)
        def _():
            acc[...] = partial(ins)

        @pl.when(k > 0)
        def _():
            acc[...] += partial(ins)

        @pl.when(k == nk - 1)
        def _():
            epilogue(acc[...], ins, outs, ids)

    return pl.pallas_call(
        body, grid=grid, in_specs=in_specs, out_specs=out_specs, out_shape=out_shape,
        scratch_shapes=[] if nk == 1 else [pltpu.VMEM(acc_shape, F32)],
        input_output_aliases=aliases or {}, name=name, compiler_params=_params(sem),
    )(*operands)


def _row_spec(width=D_MODEL):
    return pl.BlockSpec((ROW_TILE, width), lambda i: (i, 0))


def _vec_spec(rows=1, width=D_MODEL):
    return pl.BlockSpec((rows, width), lambda i: (0, 0))


def prenorm(x, gain, scale, shift):
    def body(x_ref, g_ref, sc_ref, sh_ref, h_ref):
        xv = x_ref[...]
        r = lax.rsqrt(jnp.mean(xv * xv, axis=-1, keepdims=True) + RMS_EPS)
        h_ref[...] = (((xv * r) * g_ref[...]) * (1.0 + sc_ref[...]) + sh_ref[...]).astype(BF16)

    return pl.pallas_call(
        body, grid=(SEQ // ROW_TILE,), in_specs=[_row_spec(), _vec_spec(), _vec_spec(), _vec_spec()],
        out_specs=_row_spec(), out_shape=jax.ShapeDtypeStruct((SEQ, D_MODEL), BF16),
        name="prenorm", compiler_params=_params(("parallel",)),
    )(x, gain, scale, shift)


def resid_bwd(dxo, f, coef, mult):
    def body(dx_ref, f_ref, c_ref, df_ref, dc_ref):
        dx = dx_ref[...]
        df_ref[...] = (dx * (mult * c_ref[...])).astype(BF16)
        part = mult * jnp.sum(dx * f_ref[...], axis=0, keepdims=True)

        @pl.when(pl.program_id(0) == 0)
        def _():
            dc_ref[...] = jnp.zeros_like(dc_ref)

        dc_ref[0:1, :] += part

    return pl.pallas_call(
        body, grid=(SEQ // ROW_TILE,), in_specs=[_row_spec(), _row_spec(), _vec_spec()],
        out_specs=[_row_spec(), _vec_spec(8)],
        out_shape=[jax.ShapeDtypeStruct((SEQ, D_MODEL), BF16), jax.ShapeDtypeStruct((8, D_MODEL), F32)],
        name="resid_bwd", compiler_params=_params(("arbitrary",)),
    )(dxo, f, coef)


def final_loss(x, gain, target):
    def body(x_ref, g_ref, t_ref, loss_ref, dx_ref, dg_ref):
        xv = x_ref[...]
        g = g_ref[...]
        r = lax.rsqrt(jnp.mean(xv * xv, axis=-1, keepdims=True) + RMS_EPS)
        xh = xv * r
        e = xh * g - t_ref[...]
        part = 0.5 * jnp.sum(jnp.mean(e * e, axis=-1, keepdims=True), axis=0, keepdims=True)
        dy = e * (1.0 / D_MODEL)
        dyg = dy * g
        dx_ref[...] = r * (dyg - xh * jnp.mean(dyg * xh, axis=-1, keepdims=True))

        @pl.when(pl.program_id(0) == 0)
        def _():
            loss_ref[...] = jnp.zeros_like(loss_ref)
            dg_ref[...] = jnp.zeros_like(dg_ref)

        loss_ref[...] += jnp.broadcast_to(part, loss_ref.shape)
        dg_ref[0:1, :] += jnp.sum(dy * xh, axis=0, keepdims=True)

    return pl.pallas_call(
        body, grid=(SEQ // ROW_TILE,), in_specs=[_row_spec(), _vec_spec(), _row_spec()],
        out_specs=[_vec_spec(8, 128), _row_spec(), _vec_spec(8)],
        out_shape=[jax.ShapeDtypeStruct((8, 128), F32), jax.ShapeDtypeStruct((SEQ, D_MODEL), F32),
                   jax.ShapeDtypeStruct((8, D_MODEL), F32)],
        name="final_loss", compiler_params=_params(("arbitrary",)),
    )(x, gain, target)


def _prenorm_bwd_epilogue(dh, x_ref, dxo_ref, g_ref, sc_ref, dx_ref, stats_ref, first):
    xv = x_ref[...]
    g = g_ref[...]
    r = lax.rsqrt(jnp.mean(xv * xv, axis=-1, keepdims=True) + RMS_EPS)
    xh = xv * r
    dn = dh * (1.0 + sc_ref[...])
    dxh = dn * g
    dx_ref[...] = dxo_ref[...] + r * (dxh - xh * jnp.mean(dxh * xh, axis=-1, keepdims=True))

    @pl.when(first)
    def _():
        stats_ref[...] = jnp.zeros_like(stats_ref)

    stats_ref[0:1, :] += jnp.sum(dh, axis=0, keepdims=True)
    stats_ref[1:2, :] += jnp.sum(dh * (xh * g), axis=0, keepdims=True)
    stats_ref[2:3, :] += jnp.sum(dn * xh, axis=0, keepdims=True)


def ffn_up(h, wg_all, wu_all):
    def body(h_ref, wg_ref, wu_ref, a_ref, b_ref, s_ref):
        hv = h_ref[...]
        a = _dot(hv, wg_ref[...], NN)
        b = _dot(hv, wu_ref[...], NN)
        a_ref[...] = a
        b_ref[...] = b
        s_ref[...] = (a * _sigmoid(a) * b).astype(BF16)

    w_spec = pl.BlockSpec((None, D_MODEL, FF_SHARD), lambda j, i: (j, 0, 0))
    o_spec = pl.BlockSpec((None, MM_TILE, FF_SHARD), lambda j, i: (j, i, 0))
    hid = (N_CHIPS, SEQ, FF_SHARD)
    return pl.pallas_call(
        body, grid=(N_CHIPS, SEQ // MM_TILE),
        in_specs=[pl.BlockSpec((MM_TILE, D_MODEL), lambda j, i: (i, 0)), w_spec, w_spec],
        out_specs=[o_spec, o_spec, o_spec],
        out_shape=[jax.ShapeDtypeStruct(hid, F32), jax.ShapeDtypeStruct(hid, F32), jax.ShapeDtypeStruct(hid, BF16)],
        name="ffn_up", compiler_params=_params(("parallel", "parallel")),
    )(h, wg_all, wu_all)


def matmul_residual(name, a, a_spec, w_all, w_spec, x, coef, mult):
    def epilogue(acc, ins, outs, ids):
        outs[0][...] = acc
        outs[1][...] = ins[2][...] + (mult * ins[3][...]) * acc

    row = pl.BlockSpec((MM_TILE, D_MODEL), lambda i, j: (i, 0))
    return _matmul(
        name, (SEQ // MM_TILE, N_CHIPS), N_CHIPS, 1, NN, 1,
        [a_spec, w_spec, row, pl.BlockSpec((1, D_MODEL), lambda i, j: (0, 0))], [row, row],
        [jax.ShapeDtypeStruct((SEQ, D_MODEL), F32)] * 2, (MM_TILE, D_MODEL), epilogue,
        (a, w_all, x, coef), ("parallel", "arbitrary"))


def ffn_down(s, wd_all, x, gate):
    return matmul_residual(
        "ffn_down", s, pl.BlockSpec((None, MM_TILE, FF_SHARD), lambda i, j: (j, i, 0)),
        wd_all, pl.BlockSpec((None, FF_SHARD, D_MODEL), lambda i, j: (j, 0, 0)), x, gate, 0.5)


def ffn_bwd_hidden(df, wd_all, a, b):
    def epilogue(ds, ins, outs, ids):
        av, bv = ins[2][...], ins[3][...]
        sig = _sigmoid(av)
        outs[0][...] = (ds * bv * (sig * (1.0 + av * (1.0 - sig)))).astype(BF16)
        outs[1][...] = (ds * (av * sig)).astype(BF16)

    hid_spec = pl.BlockSpec((None, MM_TILE, FF_SHARD), lambda j, i: (j, i, 0))
    hid = jax.ShapeDtypeStruct((N_CHIPS, SEQ, FF_SHARD), BF16)
    return _matmul(
        "ffn_bwd_hidden", (N_CHIPS, SEQ // MM_TILE), 1, None, NT, 1,
        [pl.BlockSpec((MM_TILE, D_MODEL), lambda j, i: (i, 0)),
         pl.BlockSpec((None, FF_SHARD, D_MODEL), lambda j, i: (j, 0, 0)), hid_spec, hid_spec],
        [hid_spec, hid_spec], [hid, hid], None, epilogue, (df, wd_all, a, b), ("parallel", "parallel"))


def grad_weight(name, lhs, lhs_spec, rhs, rhs_spec, shape):
    def epilogue(acc, ins, outs, ids):
        outs[0][...] = acc.astype(BF16)

    return _matmul(
        name, (N_CHIPS, SEQ // MM_TILE), SEQ // MM_TILE, 1, TN, 1,
        [lhs_spec, rhs_spec], [pl.BlockSpec((None,) + shape, lambda j, k: (j, 0, 0))],
        [jax.ShapeDtypeStruct((N_CHIPS,) + shape, BF16)], shape, epilogue, (lhs, rhs), ("parallel", "arbitrary"))[0]


def ffn_grad_weights(h, s, df, da, db):
    tok = pl.BlockSpec((MM_TILE, D_MODEL), lambda j, k: (k, 0))
    hid = pl.BlockSpec((None, MM_TILE, FF_SHARD), lambda j, k: (j, k, 0))
    return (grad_weight("grad_w_gate", h, tok, da, hid, (D_MODEL, FF_SHARD)),
            grad_weight("grad_w_up", h, tok, db, hid, (D_MODEL, FF_SHARD)),
            grad_weight("grad_w_down", s, hid, df, tok, (FF_SHARD, D_MODEL)))


def matmul_prenorm_bwd(name, pairs, pair_specs, x, dxo, gain, scale):
    n = len(pairs)

    def epilogue(dh, ins, outs, ids):
        _prenorm_bwd_epilogue(dh, ins[n], ins[n + 1], ins[n + 2], ins[n + 3], outs[0], outs[1], ids[0] == 0)

    row = pl.BlockSpec((MM_TILE, D_MODEL), lambda i, j: (i, 0))
    vec = pl.BlockSpec((1, D_MODEL), lambda i, j: (0, 0))
    return _matmul(
        name, (SEQ // MM_TILE, N_CHIPS), N_CHIPS, 1, NT, len(pairs) // 2,
        list(pair_specs) + [row, row, vec, vec], [row, pl.BlockSpec((8, D_MODEL), lambda i, j: (0, 0))],
        [jax.ShapeDtypeStruct((SEQ, D_MODEL), F32), jax.ShapeDtypeStruct((8, D_MODEL), F32)],
        (MM_TILE, D_MODEL), epilogue, tuple(pairs) + (x, dxo, gain, scale), ("arbitrary", "arbitrary"))


def ffn_bwd_input(da, db, wg_all, wu_all, x, dxo, gain, scale):
    hid = pl.BlockSpec((None, MM_TILE, FF_SHARD), lambda i, j: (j, i, 0))
    w = pl.BlockSpec((None, D_MODEL, FF_SHARD), lambda i, j: (j, 0, 0))
    return matmul_prenorm_bwd("ffn_bwd_input", (da, wg_all, db, wu_all), (hid, w, hid, w), x, dxo, gain, scale)


def in_proj(h, w_all):
    def epilogue(acc, ins, outs, ids):
        outs[0][...] = acc

    return _matmul(
        "in_proj", (N_CHIPS, SEQ // MM_TILE), 1, None, NN, 1,
        [pl.BlockSpec((MM_TILE, D_MODEL), lambda j, i: (i, 0)),
         pl.BlockSpec((None, D_MODEL, IN_SHARD), lambda j, i: (j, 0, 0))],
        [pl.BlockSpec((MM_TILE, IN_SHARD), lambda j, i: (i, j))], [jax.ShapeDtypeStruct((SEQ, D_IN), F32)],
        None, epilogue, (h, w_all), ("parallel", "parallel"))[0]


_GATE_BLOCK0 = D_QKV // D_SHARD


def _branch_products(o, w_ref):
    ob = o.astype(BF16)
    return (_dot(ob[:, 0:256], w_ref[0:256, :], NN), _dot(ob[:, 256:384], w_ref[256:384, :], NN),
            _dot(ob[:, 384:768], w_ref[384:768, :], NN))


def merge_branches(o_cat, wbr_all, proj):
    def body(o_ref, w_ref, g0_ref, g1_ref, g2_ref, m_ref):
        u = _branch_products(o_ref[...], w_ref)
        m_ref[...] = (_sigmoid(g0_ref[...]) * u[0] + _sigmoid(g1_ref[...]) * u[1]
                      + _sigmoid(g2_ref[...]) * u[2]).astype(BF16)

    def gate_spec(b):
        return pl.BlockSpec((MM_TILE, D_SHARD), lambda i, j: (i, _GATE_BLOCK0 + 4 * b + j))

    return pl.pallas_call(
        body, grid=(SEQ // MM_TILE, N_CHIPS),
        in_specs=[pl.BlockSpec((MM_TILE, BR_ROWS), lambda i, j: (i, 0)),
                  pl.BlockSpec((None, BR_ROWS, D_SHARD), lambda i, j: (j, 0, 0)),
                  gate_spec(0), gate_spec(1), gate_spec(2)],
        out_specs=pl.BlockSpec((MM_TILE, D_SHARD), lambda i, j: (i, j)),
        out_shape=jax.ShapeDtypeStruct((SEQ, D_MODEL), BF16),
        name="merge_branches", compiler_params=_params(("parallel", "parallel")),
    )(o_cat, wbr_all, proj, proj, proj)


def out_proj(merged, wout_all, x, gate):
    return matmul_residual(
        "out_proj", merged, pl.BlockSpec((MM_TILE, D_SHARD), lambda i, j: (i, j)),
        wout_all, pl.BlockSpec((None, D_SHARD, D_MODEL), lambda i, j: (j, 0, 0)), x, gate, 1.0)


def merge_bwd(dmo, wout_all, o_cat, wbr_all, proj):
    def epilogue(dm, ins, outs, ids):
        u = _branch_products(ins[2][...], ins[3])
        for b in range(3):
            sig = _sigmoid(ins[4 + b][...])
            outs[b][...] = (dm * sig).astype(BF16)
            outs[3 + b][...] = (dm * u[b] * (sig * (1.0 - sig))).astype(BF16)

    def gate_spec(b):
        return pl.BlockSpec((MM_TILE, D_SHARD), lambda j, i: (i, _GATE_BLOCK0 + 4 * b + j))

    col = pl.BlockSpec((MM_TILE, D_SHARD), lambda j, i: (i, j))
    du = jax.ShapeDtypeStruct((SEQ, D_MODEL), BF16)
    return _matmul(
        "merge_bwd", (N_CHIPS, SEQ // MM_TILE), 1, None, NT, 1,
        [pl.BlockSpec((MM_TILE, D_MODEL), lambda j, i: (i, 0)),
         pl.BlockSpec((None, D_SHARD, D_MODEL), lambda j, i: (j, 0, 0)),
         pl.BlockSpec((MM_TILE, BR_ROWS), lambda j, i: (i, 0)),
         pl.BlockSpec((None, BR_ROWS, D_SHARD), lambda j, i: (j, 0, 0)),
         gate_spec(0), gate_spec(1), gate_spec(2)],
        [col] * 6, [du] * 6,
        None, epilogue, (dmo, wout_all, o_cat, wbr_all, proj, proj, proj), ("parallel", "parallel"))


def branch_bwd_input(du, wbr_all):
    def body(d0_ref, d1_ref, d2_ref, w_ref, o_ref, acc):
        j = pl.program_id(1)
        parts = (_dot(d0_ref[...], w_ref[0:256, :], NT), _dot(d1_ref[...], w_ref[256:384, :], NT),
                 _dot(d2_ref[...], w_ref[384:768, :], NT))

        @pl.when(j == 0)
        def _():
            acc[:, 0:256], acc[:, 256:384], acc[:, 384:768] = parts

        @pl.when(j > 0)
        def _():
            acc[:, 0:256] += parts[0]
            acc[:, 256:384] += parts[1]
            acc[:, 384:768] += parts[2]

        @pl.when(j == N_CHIPS - 1)
        def _():
            o_ref[...] = acc[...]

    col = pl.BlockSpec((MM_TILE, D_SHARD), lambda i, j: (i, j))
    return pl.pallas_call(
        body, grid=(SEQ // MM_TILE, N_CHIPS),
        in_specs=[col, col, col, pl.BlockSpec((None, BR_ROWS, D_SHARD), lambda i, j: (j, 0, 0))],
        out_specs=pl.BlockSpec((MM_TILE, BR_ROWS), lambda i, j: (i, 0)),
        out_shape=jax.ShapeDtypeStruct((SEQ, BR_ROWS), F32),
        scratch_shapes=[pltpu.VMEM((MM_TILE, BR_ROWS), F32)],
        name="branch_bwd_input", compiler_params=_params(("parallel", "arbitrary")),
    )(du[0], du[1], du[2], wbr_all)


def branch_grad_weights(o_cat, du):
    def body(o_ref, d0_ref, d1_ref, d2_ref, g_ref, acc):
        k = pl.program_id(1)
        ob = o_ref[...].astype(BF16)
        parts = (_dot(ob[:, 0:256], d0_ref[...], TN), _dot(ob[:, 256:384], d1_ref[...], TN),
                 _dot(ob[:, 384:768], d2_ref[...], TN))

        @pl.when(k == 0)
        def _():
            acc[0:256, :], acc[256:384, :], acc[384:768, :] = parts

        @pl.when(k > 0)
        def _():
            acc[0:256, :] += parts[0]
            acc[256:384, :] += parts[1]
            acc[384:768, :] += parts[2]

        @pl.when(k == SEQ // MM_TILE - 1)
        def _():
            g_ref[...] = acc[...].astype(BF16)

    col = pl.BlockSpec((MM_TILE, D_SHARD), lambda j, k: (k, j))
    return pl.pallas_call(
        body, grid=(N_CHIPS, SEQ // MM_TILE),
        in_specs=[pl.BlockSpec((MM_TILE, BR_ROWS), lambda j, k: (k, 0)), col, col, col],
        out_specs=pl.BlockSpec((None, BR_ROWS, D_SHARD), lambda j, k: (j, 0, 0)),
        out_shape=jax.ShapeDtypeStruct((N_CHIPS, BR_ROWS, D_SHARD), BF16),
        scratch_shapes=[pltpu.VMEM((BR_ROWS, D_SHARD), F32)],
        name="branch_grad_weights", compiler_params=_params(("parallel", "arbitrary")),
    )(o_cat, du[0], du[1], du[2])


def mixer_bwd_input(dproj, win_all, x, dxo, gain, scale):
    return matmul_prenorm_bwd(
        "mixer_bwd_input", (dproj, win_all),
        (pl.BlockSpec((MM_TILE, IN_SHARD), lambda i, j: (i, j)),
         pl.BlockSpec((None, D_MODEL, IN_SHARD), lambda i, j: (j, 0, 0))), x, dxo, gain, scale)


BATCH_QK = (((2,), (2,)), ((0,), (0,)))
BATCH_PV = (((2,), (1,)), ((0,), (0,)))
BATCH_TN = (((1,), (1,)), ((0,), (0,)))


SB_WIDTH = H_SB * HEAD_DIM
SB_ROWS = H_SB * BLK


def _split_dot(v, tri):
    hi = v.astype(BF16)
    lo = (v - hi.astype(F32)).astype(BF16)
    return _dot(hi, tri, NN) + _dot(lo, tri, NN)


def _tri(cmp):
    return cmp(lax.broadcasted_iota(jnp.int32, (BLK, BLK), 0), lax.broadcasted_iota(jnp.int32, (BLK, BLK), 1)).astype(BF16)


def _head_masks():
    lane = lax.broadcasted_iota(jnp.int32, (1, SB_WIDTH), 1) // HEAD_DIM
    return [lane == h for h in range(H_SB)]


def _stack_heads(x, masks):
    return jnp.concatenate([jnp.where(m, x, jnp.zeros_like(x)) for m in masks], axis=0)


def _merge_heads(y, masks):
    out = jnp.where(masks[0], y[0:BLK], 0.0)
    for h in range(1, H_SB):
        out = jnp.where(masks[h], y[h * BLK:(h + 1) * BLK], out)
    return out


def _sb_scores(q4, k_ref, i, j):
    rows = pl.ds(pl.multiple_of(j * BLK, BLK), BLK)
    z = _dot(q4, k_ref[rows, :], NT)
    tile = (SB_ROWS, BLK)
    q_pos = i * BLK + (lax.broadcasted_iota(jnp.int32, tile, 0) & (BLK - 1))
    before = (j * BLK + lax.broadcasted_iota(jnp.int32, tile, 1)) < q_pos
    soft = jnp.log(1.0 + jnp.exp(-jnp.abs(z)))
    log_fail = jnp.where(before, -(jnp.maximum(z, 0.0) + soft), 0.0)
    log_hit = jnp.minimum(z, 0.0) - soft
    return rows, before, log_fail, log_hit


def sb_forward(qkv):
    def body(q_ref, k_ref, v_ref, o_ref, tot_ref):
        i = pl.program_id(0)
        masks = _head_masks()
        q4 = _stack_heads(q_ref[...] * QK_SCALE, masks)
        later = _tri(lambda r, c: r > c)

        def step(t, carry):
            o, run = carry
            rows, before, log_fail, log_hit = _sb_scores(q4, k_ref, i, i - t)
            between = _split_dot(log_fail, later) + run
            w = jnp.where(before, jnp.exp(log_hit + between), 0.0)
            o = o + _merge_heads(_dot(w.astype(BF16), v_ref[rows, :], NN), masks)
            return o, run + jnp.sum(log_fail, axis=1, keepdims=True)

        o, run = lax.fori_loop(0, i + 1, step, (jnp.zeros((BLK, SB_WIDTH), F32), jnp.zeros((SB_ROWS, 1), F32)))
        o_ref[...] = o
        tot_ref[...] = run

    return pl.pallas_call(
        body, grid=(N_BLK,),
        in_specs=[pl.BlockSpec((BLK, SB_WIDTH), lambda i: (i, 0)), pl.BlockSpec((SEQ, SB_WIDTH), lambda i: (0, 1)),
                  pl.BlockSpec((SEQ, SB_WIDTH), lambda i: (0, 2))],
        out_specs=[pl.BlockSpec((BLK, SB_WIDTH), lambda i: (i, 0)), pl.BlockSpec((None, SB_ROWS, 1), lambda i: (i, 0, 0))],
        out_shape=[jax.ShapeDtypeStruct((SEQ, SB_WIDTH), F32), jax.ShapeDtypeStruct((N_BLK, SB_ROWS, 1), F32)],
        name="sb_forward", compiler_params=_params(("parallel",)),
    )(qkv, qkv, qkv)


def sb_backward(qkv, total, do_cat):
    def body(q_ref, k_ref, v_ref, tot_ref, do_ref, dq_ref, dk_ref, dv_ref):
        i = pl.program_id(0)

        @pl.when(i == 0)
        def _():
            dk_ref[...] = jnp.zeros_like(dk_ref)
            dv_ref[...] = jnp.zeros_like(dv_ref)

        masks = _head_masks()
        q4 = _stack_heads(q_ref[...] * QK_SCALE, masks)
        do4 = _stack_heads(do_ref[...].astype(BF16), masks)
        total_v = tot_ref[...]
        upto = _tri(lambda r, c: r <= c)
        earlier = _tri(lambda r, c: r < c)

        def step(j, carry):
            dq, seen, g_seen = carry
            rows, before, log_fail, log_hit = _sb_scores(q4, k_ref, i, j)
            between = total_v - (seen + _split_dot(log_fail, upto))
            w = jnp.where(before, jnp.exp(log_hit + between), 0.0)
            g = _dot(do4, v_ref[rows, :], NT) * w
            g_earlier = g_seen + _split_dot(g, earlier)
            sig = jnp.exp(log_hit)
            dz = jnp.where(before, g * (1.0 - sig) - g_earlier * sig, 0.0).astype(BF16)
            dq = dq + _merge_heads(_dot(dz, k_ref[rows, :], NN), masks)
            dk_ref[rows, :] += _dot(dz, q4, TN)
            dv_ref[rows, :] += _dot(w.astype(BF16), do4, TN)
            return dq, seen + jnp.sum(log_fail, axis=1, keepdims=True), g_seen + jnp.sum(g, axis=1, keepdims=True)

        zero = jnp.zeros((SB_ROWS, 1), F32)
        dq, _, _ = lax.fori_loop(0, i + 1, step, (jnp.zeros((BLK, SB_WIDTH), F32), zero, zero))
        dq_ref[...] = dq * QK_SCALE

    blk = pl.BlockSpec((BLK, SB_WIDTH), lambda i: (i, 0))
    full = pl.BlockSpec((SEQ, SB_WIDTH), lambda i: (0, 0))
    shape = jax.ShapeDtypeStruct((SEQ, SB_WIDTH), F32)
    return pl.pallas_call(
        body, grid=(N_BLK,),
        in_specs=[blk, pl.BlockSpec((SEQ, SB_WIDTH), lambda i: (0, 1)), pl.BlockSpec((SEQ, SB_WIDTH), lambda i: (0, 2)),
                  pl.BlockSpec((None, SB_ROWS, 1), lambda i: (i, 0, 0)), blk],
        out_specs=[blk, full, full], out_shape=[shape, shape, shape],
        name="sb_backward", compiler_params=_params(("arbitrary",)),
    )(qkv, qkv, qkv, total, do_cat)


def _band_scores(q_ref, kp_ref, ko_ref, bias_ref, hb, prev_mask):
    b = pl.program_id(1)
    qs = q_ref[...]
    s_prev = _dot(qs, kp_ref[...], BATCH_QK) + bias_ref[:, :, 0:BLK]
    s_prev = jnp.concatenate(
        [jnp.where((b & prev_mask(pl.program_id(0) * hb + t)) != 0, s_prev[t:t + 1], NEG) for t in range(hb)], axis=0)
    s_own = _dot(qs, ko_ref[...], BATCH_QK) + bias_ref[:, :, BLK:2 * BLK]
    return qs, s_prev, s_own


def _band_specs(hb, rows, t_n):
    def q_spec(width):
        return pl.BlockSpec((hb, None, rows, width), lambda h, b: (h, b, 0, 0))

    own = pl.BlockSpec((hb, BLK, HEAD_DIM), lambda h, b: (h, b, 0))
    prev = pl.BlockSpec((hb, BLK, HEAD_DIM), lambda h, b: (h, jnp.maximum(b - 1, 0), 0))
    per_head = lambda r, width: pl.BlockSpec((hb, r, width), lambda h, b: (h, 0, 0))
    return q_spec, own, prev, per_head


def banded_forward(name, q, k, v, bias, sinks, hb, prev_mask):
    h_n, nb, rows, _ = q.shape

    def body(q_ref, kp_ref, ko_ref, vp_ref, vo_ref, bias_ref, sink_ref, o_ref, lse_ref):
        _, s_prev, s_own = _band_scores(q_ref, kp_ref, ko_ref, bias_ref, hb, prev_mask)
        sink = sink_ref[...]
        m = jnp.maximum(jnp.maximum(jnp.max(s_prev, axis=2, keepdims=True), jnp.max(s_own, axis=2, keepdims=True)), sink)
        p_prev = jnp.exp(s_prev - m)
        p_own = jnp.exp(s_own - m)
        denom = jnp.sum(p_prev, axis=2, keepdims=True) + jnp.sum(p_own, axis=2, keepdims=True) + jnp.exp(sink - m)
        o = _dot(p_prev.astype(BF16), vp_ref[...], BATCH_PV) + _dot(p_own.astype(BF16), vo_ref[...], BATCH_PV)
        o_ref[...] = o / denom
        lse_ref[...] = m + jnp.log(denom)

    q_spec, own, prev, per_head = _band_specs(hb, rows, k.shape[1])
    return pl.pallas_call(
        body, grid=(h_n // hb, nb),
        in_specs=[q_spec(HEAD_DIM), prev, own, prev, own, per_head(rows, 2 * BLK), per_head(rows, 1)],
        out_specs=[q_spec(HEAD_DIM), q_spec(1)],
        out_shape=[jax.ShapeDtypeStruct(q.shape, F32), jax.ShapeDtypeStruct((h_n, nb, rows, 1), F32)],
        name=name, compiler_params=_params(("parallel", "parallel")),
    )(q, k, k, v, v, bias, sinks)


def banded_backward(name, q, k, v, bias, sinks, o, lse, do, dlse, hb, prev_mask):
    h_n, nb, rows, _ = q.shape
    t_n = k.shape[1]

    def body(q_ref, kp_ref, ko_ref, vp_ref, vo_ref, bias_ref, sink_ref, o_ref, lse_ref, do_ref, dlse_ref,
             dq_ref, dk_ref, dv_ref, dbias_ref, dsink_ref):
        b = pl.program_id(1)

        @pl.when(b == 0)
        def _():
            dk_ref[...] = jnp.zeros_like(dk_ref)
            dv_ref[...] = jnp.zeros_like(dv_ref)
            dbias_ref[...] = jnp.zeros_like(dbias_ref)
            dsink_ref[...] = jnp.zeros_like(dsink_ref)

        qs, s_prev, s_own = _band_scores(q_ref, kp_ref, ko_ref, bias_ref, hb, prev_mask)
        lse_v = lse_ref[...]
        dov = do_ref[...]
        dob = dov.astype(BF16)
        shift = dlse_ref[...] - jnp.sum(dov * o_ref[...], axis=2, keepdims=True)
        p_prev = jnp.exp(s_prev - lse_v)
        p_own = jnp.exp(s_own - lse_v)
        ds_prev = p_prev * (_dot(dob, vp_ref[...], BATCH_QK) + shift)
        ds_own = p_own * (_dot(dob, vo_ref[...], BATCH_QK) + shift)
        dbias_ref[:, :, 0:BLK] += ds_prev
        dbias_ref[:, :, BLK:2 * BLK] += ds_own
        d_sink = jnp.exp(sink_ref[...] - lse_v) * shift
        for g in range(rows // BLK):
            dsink_ref[:, g:g + 1, :] += jnp.sum(d_sink[:, g * BLK:(g + 1) * BLK, :], axis=1, keepdims=True)
        ds_prev = ds_prev.astype(BF16)
        ds_own = ds_own.astype(BF16)
        dq_ref[...] = (_dot(ds_prev, kp_ref[...], BATCH_PV) + _dot(ds_own, ko_ref[...], BATCH_PV)) * QK_SCALE
        rows_prev = pl.ds(pl.multiple_of(jnp.maximum(b - 1, 0) * BLK, BLK), BLK)
        rows_own = pl.ds(pl.multiple_of(b * BLK, BLK), BLK)
        dk_ref[:, rows_prev, :] += _dot(ds_prev, qs, BATCH_TN)
        dk_ref[:, rows_own, :] += _dot(ds_own, qs, BATCH_TN)
        dv_ref[:, rows_prev, :] += _dot(p_prev.astype(BF16), dob, BATCH_TN)
        dv_ref[:, rows_own, :] += _dot(p_own.astype(BF16), dob, BATCH_TN)

    q_spec, own, prev, per_head = _band_specs(hb, rows, t_n)
    kv_full = per_head(t_n, HEAD_DIM)
    kv_shape = jax.ShapeDtypeStruct((h_n, t_n, HEAD_DIM), F32)
    return pl.pallas_call(
        body, grid=(h_n // hb, nb),
        in_specs=[q_spec(HEAD_DIM), prev, own, prev, own, per_head(rows, 2 * BLK), per_head(rows, 1),
                  q_spec(HEAD_DIM), q_spec(1), q_spec(HEAD_DIM), q_spec(1)],
        out_specs=[q_spec(HEAD_DIM), kv_full, kv_full, per_head(rows, 2 * BLK), per_head(rows // BLK, BLK)],
        out_shape=[jax.ShapeDtypeStruct(q.shape, F32), kv_shape, kv_shape,
                   jax.ShapeDtypeStruct((h_n, rows, 2 * BLK), F32), jax.ShapeDtypeStruct((h_n, rows // BLK, BLK), F32)],
        name=name, compiler_params=_params(("parallel", "arbitrary")),
    )(q, k, k, v, v, bias, sinks, o, lse, do, dlse)


def _dil_prev_mask(head):
    group = head // H_PER_DIL
    return jnp.where(group == 0, 15, jnp.where(group == 1, 3, 0))


def _swa_prev_mask(head):
    del head
    return 15


DIL_HEADS_PER_STEP = 3
SWA_GROUP = H_SWA_Q // H_SWA_KV
N_BLK = SEQ // BLK


def dilated_merge(o, lse):
    def body(o_ref, l_ref, out_ref):
        lv = l_ref[...]
        m = jnp.max(lv, axis=0, keepdims=True)
        e = jnp.exp(lv - m)
        alpha = e / jnp.sum(e, axis=0, keepdims=True)
        out_ref[...] = jnp.sum(alpha * o_ref[...], axis=0)

    return pl.pallas_call(
        body, grid=(H_PER_DIL, SEQ // ROW_TILE),
        in_specs=[pl.BlockSpec((3, None, ROW_TILE, HEAD_DIM), lambda h, i: (0, h, i, 0)),
                  pl.BlockSpec((3, None, ROW_TILE, 1), lambda h, i: (0, h, i, 0))],
        out_specs=pl.BlockSpec((None, ROW_TILE, HEAD_DIM), lambda h, i: (h, i, 0)),
        out_shape=jax.ShapeDtypeStruct((H_PER_DIL, SEQ, HEAD_DIM), F32),
        name="dilated_merge", compiler_params=_params(("parallel", "parallel")),
    )(o, lse)


def dilated_merge_bwd(o, lse, dout):
    def body(o_ref, l_ref, d_ref, do_ref, dl_ref):
        lv = l_ref[...]
        m = jnp.max(lv, axis=0, keepdims=True)
        e = jnp.exp(lv - m)
        alpha = e / jnp.sum(e, axis=0, keepdims=True)
        dv = d_ref[...][None]
        do_ref[...] = alpha * dv
        dalpha = jnp.sum(dv * o_ref[...], axis=-1, keepdims=True)
        dl_ref[...] = alpha * (dalpha - jnp.sum(alpha * dalpha, axis=0, keepdims=True))

    o_spec = pl.BlockSpec((3, None, ROW_TILE, HEAD_DIM), lambda h, i: (0, h, i, 0))
    l_spec = pl.BlockSpec((3, None, ROW_TILE, 1), lambda h, i: (0, h, i, 0))
    return pl.pallas_call(
        body, grid=(H_PER_DIL, SEQ // ROW_TILE),
        in_specs=[o_spec, l_spec, pl.BlockSpec((None, ROW_TILE, HEAD_DIM), lambda h, i: (h, i, 0))],
        out_specs=[o_spec, l_spec],
        out_shape=[jax.ShapeDtypeStruct(o.shape, F32), jax.ShapeDtypeStruct(lse.shape, F32)],
        name="dilated_merge_bwd", compiler_params=_params(("parallel", "parallel")),
    )(o, lse, dout)


def rel_bias_reduce(dbias0, dbias1, bucket):
    def body(d0_ref, d1_ref, b_ref, o_ref):
        dv, bv = d0_ref[...] + d1_ref[...], b_ref[...]
        lane = lax.broadcasted_iota(jnp.int32, (1, BLK), 1)
        acc = jnp.zeros((1, BLK), F32)
        for bkt in range(N_BUCKETS):
            acc = acc + jnp.where(lane == bkt, jnp.sum(jnp.where(bv == bkt, dv, 0.0)), 0.0)
        o_ref[...] = acc

    tile = pl.BlockSpec((None, BLK, 2 * BLK), lambda h: (h, 0, 0))
    return pl.pallas_call(
        body, grid=(dbias0.shape[0],), in_specs=[tile, tile, tile],
        out_specs=pl.BlockSpec((None, 1, BLK), lambda h: (h, 0, 0)),
        out_shape=jax.ShapeDtypeStruct((dbias0.shape[0], 1, BLK), F32),
        name="rel_bias_reduce", compiler_params=_params(("parallel",)),
    )(dbias0, dbias1, bucket)


def _heads(t):
    return t.reshape(SEQ, -1, HEAD_DIM).transpose(1, 0, 2)


def _unheads(t):
    return t.transpose(1, 0, 2).reshape(SEQ, -1)


def _dilate(t):
    parts = []
    for g, (_, d) in enumerate(DIL_PATTERNS):
        tg = t[:, 128 * g:128 * (g + 1)].reshape(SEQ // d, d, H_PER_DIL, HEAD_DIM).transpose(2, 1, 0, 3)
        parts.append(tg.reshape(H_PER_DIL, SEQ, HEAD_DIM))
    return jnp.concatenate(parts, axis=0)


def _undilate(t):
    outs = []
    for g, (_, d) in enumerate(DIL_PATTERNS):
        tg = t[2 * g:2 * g + 2].reshape(H_PER_DIL, d, SEQ // d, -1).transpose(0, 2, 1, 3)
        outs.append(tg.reshape(H_PER_DIL, SEQ, -1))
    return jnp.stack(outs)


def _redilate(t):
    parts = []
    for g, (_, d) in enumerate(DIL_PATTERNS):
        tg = t[g].reshape(H_PER_DIL, SEQ // d, d, -1).transpose(0, 2, 1, 3)
        parts.append(tg.reshape(H_PER_DIL, SEQ, -1))
    return jnp.concatenate(parts, axis=0)


def _t5_bucket(n):
    max_exact = N_BUCKETS // 2
    nf = jnp.maximum(n, 1).astype(F32)
    large = max_exact + (jnp.log(nf / max_exact) / math.log(MAX_REL_DIST / max_exact)
                         * (N_BUCKETS - max_exact)).astype(jnp.int32)
    large = jnp.minimum(large, N_BUCKETS - 1)
    return jnp.where(n < max_exact, n, large)


def band_tables(rel_bias):
    rel = jnp.arange(BLK)[:, None] + BLK - jnp.arange(2 * BLK)[None, :]
    buckets = []
    patterns = [(d, w // d) for w, d in DIL_PATTERNS for _ in range(H_PER_DIL)] + [(1, SWA_WINDOW - 1)] * H_SWA_Q
    for d, max_dist in patterns:
        band = (rel >= 0) & (rel <= max_dist)
        buckets.append(jnp.where(band, _t5_bucket(jnp.maximum(rel, 0) * d), -1))
    buckets = jnp.stack(buckets).astype(jnp.int32)

    def body(table_ref, b_ref, o_ref):
        h = pl.program_id(0)
        bv = b_ref[...]
        tile = jnp.full(bv.shape, NEG, F32)
        for bkt in range(N_BUCKETS):
            tile = jnp.where(bv == bkt, table_ref[h, bkt], tile)
        o_ref[...] = tile

    spec = pl.BlockSpec((None, BLK, 2 * BLK), lambda h: (h, 0, 0))
    tiles = pl.pallas_call(
        body, grid=(len(patterns),), in_specs=[pl.BlockSpec(memory_space=pltpu.SMEM), spec], out_specs=spec,
        out_shape=jax.ShapeDtypeStruct(buckets.shape, F32), name="band_tables", compiler_params=_params(("parallel",)),
    )(rel_bias.T, buckets)
    return tiles[:H_DIL], tiles[H_DIL:], buckets


def _swa_rows(t):
    t = t.reshape(N_BLK, BLK, H_SWA_KV, SWA_GROUP, HEAD_DIM).transpose(2, 0, 3, 1, 4)
    return t.reshape(H_SWA_KV, N_BLK, SWA_GROUP * BLK, HEAD_DIM)


def _swa_tokens(t):
    t = t.reshape(H_SWA_KV, N_BLK, SWA_GROUP, BLK, HEAD_DIM).transpose(1, 3, 0, 2, 4)
    return t.reshape(SEQ, H_SWA_Q * HEAD_DIM)


def _sink_rows(sinks):
    return jnp.broadcast_to(sinks.reshape(H_SWA_KV, SWA_GROUP, 1, 1), (H_SWA_KV, SWA_GROUP, BLK, 1)).reshape(
        H_SWA_KV, SWA_GROUP * BLK, 1)


def _no_sinks():
    return jnp.full((H_DIL, BLK, 1), NEG, F32)


def _vec(v):
    return v.reshape(1, D_MODEL)


def ffn_forward(x, gain, mod, w):
    h = prenorm(x, _vec(gain), _vec(mod[1]), _vec(mod[0]))
    a, b, s = ffn_up(h, w[0], w[1])
    f, xo = ffn_down(s, w[2], x, _vec(mod[2]))
    return xo, (x, h, a, b, s, f)


def ffn_backward(dxo, saved, gain, mod, w):
    x, h, a, b, s, f = saved
    df, dgate = resid_bwd(dxo, f, _vec(mod[2]), 0.5)
    da, db = ffn_bwd_hidden(df, w[2], a, b)
    grads = ffn_grad_weights(h, s, df, da, db)
    dx, stats = ffn_bwd_input(da, db, w[0], w[1], x, dxo, _vec(gain), _vec(mod[1]))
    return dx, jnp.stack([stats[0], stats[1], dgate[0]]), stats[2], grads


def mixer_forward(x, gain, mod, sinks, bias_dil, bias_swa, w):
    h = prenorm(x, _vec(gain), _vec(mod[1]), _vec(mod[0]))
    proj = in_proj(h, w[0])
    qkv = proj[:, :D_QKV].astype(BF16)
    q_dil, k_dil, v_dil = _dilate(qkv[:, 768:1152] * QK_SCALE), _dilate(qkv[:, 1152:1536]), _dilate(qkv[:, 1536:1920])
    q_swa, k_swa, v_swa = _swa_rows(qkv[:, 1920:2304] * QK_SCALE), _heads(qkv[:, 2304:2432]), _heads(qkv[:, 2432:2560])
    o_sb, total_sb = sb_forward(qkv)
    q_dil = q_dil.reshape(H_DIL, N_BLK, BLK, HEAD_DIM)
    o_dd, lse_dd = banded_forward("dilated_forward", q_dil, k_dil, v_dil, bias_dil, _no_sinks(), DIL_HEADS_PER_STEP,
                                  _dil_prev_mask)
    o_dt, lse_dt = _undilate(o_dd.reshape(H_DIL, SEQ, HEAD_DIM)), _undilate(lse_dd.reshape(H_DIL, SEQ, 1))
    o_dil = dilated_merge(o_dt, lse_dt)
    bias_swa = bias_swa.reshape(H_SWA_KV, SWA_GROUP * BLK, 2 * BLK)
    o_swa, lse_swa = banded_forward("swa_forward", q_swa, k_swa, v_swa, bias_swa, _sink_rows(sinks), 1, _swa_prev_mask)
    o_cat = jnp.concatenate([o_sb, _unheads(o_dil), _swa_tokens(o_swa)], axis=1)
    merged = merge_branches(o_cat, w[1], proj)
    mo, xo = out_proj(merged, w[2], x, _vec(mod[2]))
    saved = (x, h, proj, (qkv, total_sb), (q_dil, k_dil, v_dil, o_dd, lse_dd, o_dt, lse_dt),
             (q_swa, k_swa, v_swa, o_swa, lse_swa), o_cat, merged, mo)
    return xo, saved


def mixer_backward(dxo, saved, gain, mod, sinks, bias_dil, bias_swa, w):
    x, h, proj, sb, dil, swa, o_cat, merged, mo = saved
    dmo, dgate = resid_bwd(dxo, mo, _vec(mod[2]), 1.0)
    tok = pl.BlockSpec((MM_TILE, D_MODEL), lambda j, k: (k, 0))
    g_out = grad_weight("grad_w_out", merged, pl.BlockSpec((MM_TILE, D_SHARD), lambda j, k: (k, j)), dmo, tok,
                        (D_SHARD, D_MODEL))
    du0, du1, du2, dg0, dg1, dg2 = merge_bwd(dmo, w[2], o_cat, w[1], proj)
    du = (du0, du1, du2)
    do_cat = branch_bwd_input(du, w[1])
    g_br = branch_grad_weights(o_cat, du)

    qkv, total_sb = sb
    dq_sb, dk_sb, dv_sb = sb_backward(qkv, total_sb, do_cat)

    q_dil, k_dil, v_dil, o_dd, lse_dd, o_dt, lse_dt = dil
    do_dt, dlse_dt = dilated_merge_bwd(o_dt, lse_dt, _heads(do_cat[:, 256:384]))
    dq_dil, dk_dil, dv_dil, dbias_dil, _ = banded_backward(
        "dilated_backward", q_dil, k_dil, v_dil, bias_dil, _no_sinks(), o_dd, lse_dd,
        _redilate(do_dt).reshape(q_dil.shape), _redilate(dlse_dt).reshape(lse_dd.shape), DIL_HEADS_PER_STEP, _dil_prev_mask)

    q_swa, k_swa, v_swa, o_swa, lse_swa = swa
    bias_swa = bias_swa.reshape(H_SWA_KV, SWA_GROUP * BLK, 2 * BLK)
    dq_swa, dk_swa, dv_swa, dbias_swa, dsinks = banded_backward(
        "swa_backward", q_swa, k_swa, v_swa, bias_swa, _sink_rows(sinks), o_swa, lse_swa, _swa_rows(do_cat[:, 384:768]),
        jnp.zeros_like(lse_swa), 1, _swa_prev_mask)
    dbias_swa = dbias_swa.reshape(H_SWA_Q, BLK, 2 * BLK)

    def tokens(t):
        return _undilate(t).transpose(2, 0, 1, 3).reshape(SEQ, -1)

    dproj = jnp.concatenate(
        [dq_sb, dk_sb, dv_sb, tokens(dq_dil.reshape(H_DIL, SEQ, HEAD_DIM)), tokens(dk_dil),
         tokens(dv_dil), _swa_tokens(dq_swa), _unheads(dk_swa), _unheads(dv_swa)], axis=1).astype(BF16)
    dproj = jnp.concatenate([dproj, dg0, dg1, dg2], axis=1)
    g_in = grad_weight("grad_w_in", h, tok, dproj, pl.BlockSpec((MM_TILE, IN_SHARD), lambda j, k: (k, j)),
                       (D_MODEL, IN_SHARD))
    dx, stats = mixer_bwd_input(dproj, w[0], x, dxo, _vec(gain), _vec(mod[1]))
    dmod = jnp.stack([stats[0], stats[1], dgate[0]])
    dbias = jnp.concatenate([dbias_dil, dbias_swa], axis=0)
    return dx, dmod, stats[2], dbias, dsinks[:, :, 0].reshape(H_SWA_Q), (g_in, g_br, g_out)


N_UNITS = 3 * DEPTH


def device_step(x, target, mod, gains, final_gain, sinks, rel_bias, get_weights, put_grads):
    bias_dil, bias_swa, bucket = band_tables(rel_bias)
    saved, weights = [], []
    for u in range(N_UNITS):
        l, j = divmod(u, 3)
        w = get_weights(u, x)
        if j == 1:
            x, s = mixer_forward(x, gains[l, 1], mod[l, 1], sinks[l], bias_dil, bias_swa, w)
        else:
            x, s = ffn_forward(x, gains[l, j], mod[l, j], w)
        saved.append(s)
        weights.append(w)
    loss, dx, dfinal = final_loss(x, _vec(final_gain), target)

    dmod = [[None] * 3 for _ in range(DEPTH)]
    dgains = [[None] * 3 for _ in range(DEPTH)]
    dbias, dsinks = [None] * DEPTH, [None] * DEPTH
    zero = jnp.zeros((1, 1), F32)
    for u in reversed(range(N_UNITS)):
        l, j = divmod(u, 3)
        gain = gains[l, j] + zero[0]
        if j == 1:
            dx, dmod[l][j], dgains[l][j], dbias[l], dsinks[l], grads = mixer_backward(
                dx, saved[u], gain, mod[l, 1], sinks[l], bias_dil, bias_swa, weights[u])
        else:
            dx, dmod[l][j], dgains[l][j], grads = ffn_backward(dx, saved[u], gain, mod[l, j], weights[u])
        if u > 0:
            zero = put_grads(u, grads)
    drel = rel_bias_reduce(dbias[0], dbias[1], bucket)[:, 0, :N_BUCKETS].T
    stack2 = lambda t: jnp.stack([jnp.stack(r) for r in t])
    return loss, dx, stack2(dmod), stack2(dgains), dfinal[0], jnp.stack(dsinks), drel, grads


MESH = pl.DeviceIdType.MESH
CHIP_FLIPS = ((1, 0), (0, 1), (1, 1))
ANY = pl.BlockSpec(memory_space=pl.ANY)


def _position():
    return lax.axis_index("x"), lax.axis_index("y"), lax.axis_index("c")


def all_gather_small(name, piece):
    def body(x_ref, out_ref, send_sems, recv_sems, local_sem):
        x, y, c = _position()
        me, sibling = (x, y, c), (x, y, 1 - c)
        chips = [(x ^ fx, y ^ fy) for fx, fy in CHIP_FLIPS]

        def rows(px, py, pc):
            return out_ref.at[4 * px + 2 * py + pc]

        def copy(k, block, to, src=None):
            return pltpu.make_async_remote_copy(
                src_ref=rows(*block) if src is None else src, dst_ref=rows(*block),
                send_sem=send_sems.at[k], recv_sem=recv_sems.at[k], device_id=to, device_id_type=MESH)

        mine = pltpu.make_async_copy(x_ref, rows(*me), local_sem)
        mine.start()
        first = [copy(0, me, sibling, src=x_ref)]
        first += [copy(1 + j, me, (*chip, c), src=x_ref) for j, chip in enumerate(chips)]
        for cp in first:
            cp.start()
        passed = [copy(4 + j, (*chip, c), sibling) for j, chip in enumerate(chips)]
        for j, chip in enumerate(chips):
            copy(1 + j, (*chip, c), me).wait_recv()
            passed[j].start()
        copy(0, sibling, me).wait_recv()
        for j, chip in enumerate(chips):
            copy(4 + j, (*chip, 1 - c), me).wait_recv()
        for cp in first + passed:
            cp.wait_send()
        mine.wait()

    return pl.pallas_call(
        body, out_shape=jax.ShapeDtypeStruct((N_DEV,) + piece.shape, piece.dtype),
        in_specs=[pl.BlockSpec(memory_space=pltpu.VMEM)], out_specs=pl.BlockSpec(memory_space=pltpu.VMEM),
        scratch_shapes=[pltpu.SemaphoreType.DMA((7,)), pltpu.SemaphoreType.DMA((7,)), pltpu.SemaphoreType.DMA],
        name=name,
    )(piece)


def exchange(name, operands, out_shapes, aliases, plan):
    n_in, n_out = len(operands), len(out_shapes)

    def body(*refs):
        ins, outs = refs[:n_in], refs[n_in:n_in + n_out]
        send_sems, recv_sems, local_sems = refs[n_in + n_out:]
        x, y, c = _position()
        local, sends, recvs = plan(ins, outs, x, y, c)
        local = [pltpu.make_async_copy(s, d, local_sems.at[k]) for k, (s, d) in enumerate(local)]
        for cp in local:
            cp.start()
        remote = [pltpu.make_async_remote_copy(src_ref=s, dst_ref=d, send_sem=send_sems.at[k], recv_sem=recv_sems.at[k],
                                               device_id=dev, device_id_type=MESH)
                  for k, (s, d, dev) in enumerate(sends)]
        for cp in remote:
            cp.start()
        for k, r in enumerate(recvs):
            pltpu.make_async_remote_copy(src_ref=r, dst_ref=r, send_sem=send_sems.at[k], recv_sem=recv_sems.at[k],
                                         device_id=(x, y, c), device_id_type=MESH).wait_recv()
        for cp in remote:
            cp.wait_send()
        for cp in local:
            cp.wait()

    n_sends, n_local = plan.n_sends, max(plan.n_local, 1)
    return pl.pallas_call(
        body, out_shape=out_shapes, in_specs=[ANY] * n_in, out_specs=[ANY] * n_out,
        scratch_shapes=[pltpu.SemaphoreType.DMA((n_sends,)), pltpu.SemaphoreType.DMA((n_sends,)),
                        pltpu.SemaphoreType.DMA((n_local,))],
        input_output_aliases=aliases, name=name,
    )(*operands)


def _plan(n_local, n_sends):
    def wrap(fn):
        fn.n_local, fn.n_sends = n_local, n_sends
        return fn
    return wrap


def _half(ref, axis, c):
    rows = ref.shape[axis] // 2
    idx = [slice(None)] * len(ref.shape)
    idx[axis] = pl.ds(pl.multiple_of(c * rows, 16), rows)
    return ref.at[tuple(idx)]


HBM = pl.BlockSpec(memory_space=pltpu.HBM)
SEM = pl.BlockSpec(memory_space=pltpu.SEMAPHORE)
EFFECT = pltpu.SideEffectType.DATAFLOW_SIDE_EFFECTING


def split_start(name, bufs, extra, n_copies, describe):
    n = len(bufs)

    def body(*refs):
        send_sems, recv_sems = refs[n + len(extra)], refs[n + len(extra) + 1]
        x, y, c = _position()
        for k, (src, dst, _, peer) in enumerate(describe(refs[:n], x, y, c)):
            pltpu.make_async_remote_copy(src_ref=src, dst_ref=dst, send_sem=send_sems.at[k], recv_sem=recv_sems.at[k],
                                         device_id=peer, device_id_type=MESH).start()
        token = refs[-1]
        token[...] = jnp.zeros_like(token)

    out = pl.pallas_call(
        body, name=name,
        out_shape=(pltpu.SemaphoreType.DMA((n_copies,)), pltpu.SemaphoreType.DMA((n_copies,)),
                   *[pltpu.HBM(b.shape, b.dtype) for b in bufs], jax.ShapeDtypeStruct((8, 128), F32)),
        in_specs=[HBM] * n + [ANY] * len(extra),
        out_specs=(SEM, SEM, *[HBM] * n, pl.BlockSpec(memory_space=pltpu.VMEM)),
        input_output_aliases={k: 2 + k for k in range(n)},
        compiler_params=pltpu.CompilerParams(has_side_effects=EFFECT),
    )(*[pltpu.with_memory_space_constraint(b, pltpu.HBM) for b in bufs], *extra)
    return out[0], out[1], list(out[2:2 + n]), out[-1]


def split_wait(name, bufs, send_sems, recv_sems, after, describe):
    n = len(bufs)

    def body(*refs):
        send, recv = refs[n], refs[n + 1]
        x, y, c = _position()
        for k, (src, _, dst, peer) in enumerate(describe(refs[:n], x, y, c)):
            copy = pltpu.make_async_remote_copy(src_ref=src, dst_ref=dst, send_sem=send.at[k], recv_sem=recv.at[k],
                                                device_id=peer, device_id_type=MESH)
            copy.wait_send()
            copy.wait_recv()

    out = pl.pallas_call(
        body, name=name, out_shape=[pltpu.HBM(b.shape, b.dtype) for b in bufs],
        in_specs=[HBM] * n + [SEM, SEM] + [ANY] * len(after), out_specs=[HBM] * n,
        input_output_aliases={k: k for k in range(n)},
        compiler_params=pltpu.CompilerParams(has_side_effects=EFFECT),
    )(*bufs, send_sems, recv_sems, *after)
    return list(out)


def _row_tile(rows, cols):
    best = 16
    for t in range(16, rows + 1, 16):
        if rows % t == 0 and t * cols <= 256 * 1024:
            best = t
    return best


def cast_into_slot(name, param, index, chip):
    rows, cols = param.shape[-2:]
    tr = _row_tile(rows, cols)
    lead = (None,) * len(index)

    def body(chip_ref, s_ref, o_ref):
        del chip_ref
        o_ref[...] = s_ref[...].astype(BF16)

    return pl.pallas_call(
        body, out_shape=jax.ShapeDtypeStruct((N_CHIPS, rows, cols), BF16),
        grid_spec=pltpu.PrefetchScalarGridSpec(
            num_scalar_prefetch=1, grid=(rows // tr,),
            in_specs=[pl.BlockSpec(lead + (tr, cols), lambda r, chip_ref: index + (r, 0))],
            out_specs=pl.BlockSpec((None, tr, cols), lambda r, chip_ref: (chip_ref[0], r, 0))),
        name=name, compiler_params=_params(("parallel",)),
    )(chip, param)


GATHER_STAGES = ((0,), (1,), (2,), (3, 4, 5))
REDUCE_STAGES = ((5, 4, 3), (2, 1), (0,))


def _gather_copies(slots, x, y, c):
    me = 2 * x + y
    out = []
    for s in slots:
        for fx, fy in CHIP_FLIPS:
            mine = _half(s.at[me], 0, c)
            out.append((mine, mine, _half(s.at[2 * (x ^ fx) + (y ^ fy)], 0, c), (x ^ fx, y ^ fy, c)))
    return out


class WeightStream:
    def __init__(self, shards, chip, after=()):
        self.pending, self.ready = {}, {}
        token = tuple(after)
        for si, units in enumerate(GATHER_STAGES):
            slots = [cast_into_slot(f"cast_{u}_{t}", p, idx, chip) for u in units for t, (p, idx) in enumerate(shards[u])]
            send, recv, slots, tok = split_start(f"gather_start_{si}", slots, token, 3 * len(slots), _gather_copies)
            self.pending[si] = (send, recv, slots)
            token = (tok,)
        self.token = token

    def get(self, u, after):
        if u not in self.ready:
            si = next(k for k, units in enumerate(GATHER_STAGES) if u in units)
            send, recv, slots = self.pending.pop(si)
            slots = split_wait(f"gather_wait_{si}", slots, send, recv, (after,) + self.token, _gather_copies)
            self.token = ()

            @_plan(0, 3 * len(slots))
            def to_sibling(ins, outs, x, y, c):
                sends, recvs = [], []
                for o in outs:
                    for fx, fy in CHIP_FLIPS:
                        slab = o.at[2 * (x ^ fx) + (y ^ fy)]
                        sends.append((_half(slab, 0, c), _half(slab, 0, c), (x, y, 1 - c)))
                        recvs.append(_half(slab, 0, 1 - c))
                return [], sends, recvs

            shapes = [jax.ShapeDtypeStruct(s.shape, BF16) for s in slots]
            slots = exchange(f"gather_sibling_{si}", slots, shapes, {k: k for k in range(len(slots))}, to_sibling)
            for i, v in enumerate(GATHER_STAGES[si]):
                self.ready[v] = tuple(slots[3 * i:3 * i + 3])
        return self.ready[u]


def _reduce_copies(bufs, x, y, c):
    n = len(bufs) // 2
    out = []
    for s, land in zip(bufs[:n], bufs[n:]):
        for k, (fx, fy) in enumerate(CHIP_FLIPS):
            out.append((s.at[2 * (x ^ fx) + (y ^ fy)], land.at[k], land.at[k], (x ^ fx, y ^ fy, c)))
    return out


GRAD_SLOTS = {"gate": (2 * DEPTH, D_MODEL, FF_SHARD), "up": (2 * DEPTH, D_MODEL, FF_SHARD),
              "down": (2 * DEPTH, FF_SHARD, D_MODEL), "in": (DEPTH, D_MODEL, IN_SHARD),
              "br": (DEPTH, BR_ROWS, D_SHARD), "out": (DEPTH, D_SHARD, D_MODEL)}


def _unit_tensors(u):
    l, j = divmod(u, 3)
    if j == 1:
        return [("in", l), ("br", l), ("out", l)]
    return [(k, 2 * l + j // 2) for k in ("gate", "up", "down")]


class GradStream:
    def __init__(self, chip, core):
        self.core = core
        self.place = jnp.concatenate([chip, core])
        self.held, self.flying = {}, []
        self.full = {k: lax.empty(shape, F32) for k, shape in GRAD_SLOTS.items()}

    def put(self, u, grads, after=()):
        self.held[u] = grads
        si = len(self.flying)
        units = REDUCE_STAGES[si]
        if not all(v in self.held for v in units):
            return jnp.zeros((1, 1), F32)
        gs = [g for v in units for g in self.held[v]]

        @_plan(0, len(gs))
        def swap_halves(ins, outs, x, y, c):
            sends = [(_half(g, 1, 1 - c), o, (x, y, 1 - c)) for g, o in zip(ins, outs)]
            return [], sends, list(outs)

        half_shapes = [jax.ShapeDtypeStruct((N_CHIPS, g.shape[1] // 2, g.shape[2]), BF16) for g in gs]
        landed = exchange(f"reduce_swap_{si}", gs + list(after), half_shapes, {}, swap_halves)
        sums = [_add_halves(g, la, self.core) for g, la in zip(gs, landed)]
        landing = [lax.empty((3,) + s.shape[1:], BF16) for s in sums]
        send, recv, bufs, token = split_start(f"reduce_start_{si}", sums + landing, (), 3 * len(sums), _reduce_copies)
        self.flying.append((send, recv, bufs, [t for v in units for t in _unit_tensors(v)]))
        return token[0:1, 0:1]

    def finish(self, after):
        for si, (send, recv, bufs, tensors) in enumerate(self.flying):
            bufs = split_wait(f"reduce_wait_{si}", bufs, send, recv, tuple(after), _reduce_copies)
            n = len(tensors)
            for (name, slot), s, land in zip(tensors, bufs[:n], bufs[n:]):
                self.full[name] = _add_chips(s, land, self.place, self.full[name], slot)
        names = list(self.full)

        @_plan(0, len(names))
        def share_halves(ins, outs, x, y, c):
            sends = [(_half(o, 1, c), _half(o, 1, c), (x, y, 1 - c)) for o in outs]
            return [], sends, [_half(o, 1, 1 - c) for o in outs]

        shapes = [jax.ShapeDtypeStruct(self.full[k].shape, F32) for k in names]
        out = exchange("reduce_share_halves", [self.full[k] for k in names], shapes, {k: k for k in range(len(names))},
                       share_halves)
        return dict(zip(names, out))


def _add_halves(g, landed, core):
    _, rh, cols = landed.shape
    tr = _row_tile(rh, cols)
    per_half = rh // tr

    def body(core_ref, g_ref, la_ref, o_ref):
        del core_ref
        o_ref[...] = (g_ref[...].astype(F32) + la_ref[...].astype(F32)).astype(BF16)

    blk = (None, tr, cols)
    return pl.pallas_call(
        body, out_shape=jax.ShapeDtypeStruct(landed.shape, BF16),
        grid_spec=pltpu.PrefetchScalarGridSpec(
            num_scalar_prefetch=1, grid=(N_CHIPS, per_half),
            in_specs=[pl.BlockSpec(blk, lambda j, r, core_ref: (j, core_ref[0] * per_half + r, 0)),
                      pl.BlockSpec(blk, lambda j, r, core_ref: (j, r, 0))],
            out_specs=pl.BlockSpec(blk, lambda j, r, core_ref: (j, r, 0))),
        name="reduce_add_halves", compiler_params=_params(("parallel", "parallel")),
    )(core, g, landed)


def _add_chips(sums, landed, place, full, slot):
    _, rh, cols = sums.shape
    tr = _row_tile(rh, cols)
    per_half = rh // tr

    def body(place_ref, s_ref, la_ref, full_in, o_ref):
        del place_ref, full_in
        o_ref[...] = ((s_ref[...].astype(F32) + la_ref[0].astype(F32)) + la_ref[1].astype(F32)) + la_ref[2].astype(F32)

    return pl.pallas_call(
        body, out_shape=jax.ShapeDtypeStruct(full.shape, F32),
        grid_spec=pltpu.PrefetchScalarGridSpec(
            num_scalar_prefetch=1, grid=(per_half,),
            in_specs=[pl.BlockSpec((None, tr, cols), lambda r, place_ref: (place_ref[0], r, 0)),
                      pl.BlockSpec((3, tr, cols), lambda r, place_ref: (0, r, 0)), ANY],
            out_specs=pl.BlockSpec((None, tr, cols), lambda r, place_ref: (slot, place_ref[1] * per_half + r, 0))),
        input_output_aliases={3: 0}, name="reduce_add_chips", compiler_params=_params(("parallel",)),
    )(place, sums, landed, full)


def sum_devices(parts):
    def body(p_ref, o_ref):
        acc = p_ref[0]
        for d in range(1, N_DEV):
            acc = acc + p_ref[d]
        o_ref[...] = acc

    return pl.pallas_call(body, out_shape=jax.ShapeDtypeStruct(parts.shape[1:], F32), name="sum_devices")(parts)


ADA_SHARD = 9 * D_MODEL // N_CHIPS
ADA_TILE = 768
ADA_ROWS = 16


def ada_forward(c_rows, w_ada, b_shard):
    def body(c_ref, w_ref, b_ref, o_ref):
        cv = c_ref[...]
        o_ref[...] = _dot((cv * _sigmoid(cv)).astype(BF16), w_ref[...].astype(BF16), NN) + b_ref[...]

    return pl.pallas_call(
        body, grid=(DEPTH, ADA_SHARD // ADA_TILE),
        in_specs=[pl.BlockSpec((ADA_ROWS, D_MODEL), lambda l, n: (0, 0)),
                  pl.BlockSpec((None, D_MODEL, ADA_TILE), lambda l, n: (l, 0, n)),
                  pl.BlockSpec((None, 1, ADA_TILE), lambda l, n: (l, 0, n))],
        out_specs=pl.BlockSpec((None, ADA_ROWS, ADA_TILE), lambda l, n: (l, 0, n)),
        out_shape=jax.ShapeDtypeStruct((DEPTH, ADA_ROWS, ADA_SHARD), F32),
        name="ada_forward", compiler_params=_params(("parallel", "parallel")),
    )(c_rows, w_ada, b_shard)


def ada_backward(c_rows, dmod_rows):
    def body(c_ref, d_ref, o_ref):
        cv = c_ref[...]
        o_ref[...] = _dot((cv * _sigmoid(cv)).astype(BF16), d_ref[...].astype(BF16), TN)

    return pl.pallas_call(
        body, grid=(DEPTH, ADA_SHARD // ADA_TILE),
        in_specs=[pl.BlockSpec((ADA_ROWS, D_MODEL), lambda l, n: (0, 0)),
                  pl.BlockSpec((None, ADA_ROWS, ADA_TILE), lambda l, n: (l, 0, n))],
        out_specs=pl.BlockSpec((None, D_MODEL, ADA_TILE), lambda l, n: (l, 0, n)),
        out_shape=jax.ShapeDtypeStruct((DEPTH, D_MODEL, ADA_SHARD), F32),
        name="ada_backward", compiler_params=_params(("parallel", "parallel")),
    )(c_rows, dmod_rows)


def adamw(name, w, g, m, v):
    shape = w.shape
    cols = shape[-1]
    rows = w.size // cols
    tr = _row_tile(rows, cols) if rows % 16 == 0 else rows
    c1 = 1.0 / (1.0 - ADAM_B1 ** ADAM_STEP)
    c2 = 1.0 / (1.0 - ADAM_B2 ** ADAM_STEP)

    def body(w_ref, g_ref, m_ref, v_ref, go_ref, d_ref, mo_ref, vo_ref):
        gv = g_ref[...]
        mn = ADAM_B1 * m_ref[...] + (1.0 - ADAM_B1) * gv
        vn = ADAM_B2 * v_ref[...] + (1.0 - ADAM_B2) * (gv * gv)
        go_ref[...] = gv
        mo_ref[...] = mn
        vo_ref[...] = vn
        d_ref[...] = -ADAM_LR * ((mn * c1) / (jnp.sqrt(vn * c2) + ADAM_EPS) + ADAM_WD * w_ref[...])

    spec = pl.BlockSpec((tr, cols), lambda i: (i, 0))
    out = jax.ShapeDtypeStruct((rows, cols), F32)
    res = pl.pallas_call(
        body, grid=(rows // tr,), in_specs=[spec] * 4, out_specs=[spec] * 4, out_shape=[out] * 4,
        name=name, compiler_params=_params(("parallel",)),
    )(*[t.reshape(rows, cols) for t in (w, g, m, v)])
    return tuple(r.reshape(shape) for r in res)


def _pack(parts, rows):
    flat = jnp.concatenate([p.reshape(-1) for p in parts])
    return jnp.pad(flat, (0, rows * 128 - flat.size)).reshape(rows, 128)


def _unpack(flat, shapes):
    out, at = [], 0
    for s in shapes:
        n = math.prod(s)
        out.append(flat[at:at + n].reshape(s))
        at += n
    return out


def kernel(x, c, w_ada, b_ada, norm_gain, w_ffn_gate, w_ffn_up, w_ffn_down, w_in, w_br_sb, w_br_dil, w_br_swa, w_out, sinks, rel_bias, final_gain, loss_target, m_w_ada, m_b_ada, m_norm_gain, m_w_ffn_gate, m_w_ffn_up, m_w_ffn_down, m_w_in, m_w_br_sb, m_w_br_dil, m_w_br_swa, m_w_out, m_sinks, m_rel_bias, m_final_gain, v_w_ada, v_b_ada, v_norm_gain, v_w_ffn_gate, v_w_ffn_up, v_w_ffn_down, v_w_in, v_w_br_sb, v_w_br_dil, v_w_br_swa, v_w_out, v_sinks, v_rel_bias, v_final_gain):
    xi, yi, ci = _position()
    chip = 2 * xi + yi
    dev = 2 * chip + ci

    c_all = all_gather_small("gather_c", c.reshape(8, 128)).reshape(N_DEV, D_MODEL)
    c_rows = jnp.pad(c_all, ((0, ADA_ROWS - N_DEV), (0, 0)))
    b_shard = lax.dynamic_slice_in_dim(b_ada, chip * ADA_SHARD, ADA_SHARD, axis=1).reshape(DEPTH, 1, ADA_SHARD)
    mod_shard = ada_forward(c_rows, w_ada, b_shard)[:, :N_DEV]
    n_mod = DEPTH * N_DEV * ADA_SHARD
    gathered = all_gather_small("gather_mod", _pack([mod_shard, norm_gain], 304))[::2].reshape(N_CHIPS, -1)
    mod_all = gathered[:, :n_mod].reshape(N_CHIPS, DEPTH, N_DEV, ADA_SHARD)
    mod = lax.dynamic_index_in_dim(mod_all, dev, axis=2, keepdims=False)
    mod = mod.transpose(1, 0, 2).reshape(DEPTH, 3, 3, D_MODEL)
    gains = gathered[:, n_mod:n_mod + DEPTH * 3 * D_SHARD].reshape(N_CHIPS, DEPTH, 3, D_SHARD)
    gains = gains.transpose(1, 2, 0, 3).reshape(DEPTH, 3, D_MODEL)

    chip_i, core_i = chip.astype(jnp.int32).reshape(1), ci.astype(jnp.int32).reshape(1)
    w_br = jnp.concatenate([w_br_sb, w_br_dil, w_br_swa], axis=1)
    shards = []
    for l in range(DEPTH):
        ffn = [[(w_ffn_gate, (l, f)), (w_ffn_up, (l, f)), (w_ffn_down, (l, f))] for f in range(2)]
        shards += [ffn[0], [(w_in, (l,)), (w_br, (l,)), (w_out, (l,))], ffn[1]]
    weights_in = WeightStream(shards, chip_i, (gathered,))
    grads_out = GradStream(chip_i, core_i)

    loss, dx, dmod, dgains, dfinal, dsinks, drel, last_grads = device_step(
        x[0], loss_target[0], mod, gains, final_gain, sinks, rel_bias, weights_in.get, grads_out.put)

    small_shapes = [(DEPTH, 9 * D_MODEL), (DEPTH, 3, D_MODEL), (D_MODEL,), (DEPTH, H_SWA_Q), (N_BUCKETS, 12), (1,)]
    small_all = all_gather_small("gather_small_grads", _pack([dmod, dgains, dfinal, dsinks, drel, loss[0, 0:1]], 208))
    started = grads_out.put(0, last_grads, after=(small_all,))
    small_all = small_all + started
    g_b_ada, g_gain_full, g_final, g_sinks, g_rel, loss_sum = _unpack(sum_devices(small_all).reshape(-1), small_shapes)
    g_gain = lax.dynamic_slice_in_dim(g_gain_full, chip * D_SHARD, D_SHARD, axis=2)
    dmod_all = small_all.reshape(N_DEV, -1)[:, :DEPTH * 9 * D_MODEL].reshape(N_DEV, DEPTH, 9 * D_MODEL)
    dmod_rows = lax.dynamic_slice_in_dim(dmod_all, chip * ADA_SHARD, ADA_SHARD, axis=2).transpose(1, 0, 2)
    g_w_ada = ada_backward(c_rows, jnp.pad(dmod_rows, ((0, 0), (0, ADA_ROWS - N_DEV), (0, 0))))

    weights = [w_ada, b_ada, norm_gain, w_ffn_gate, w_ffn_up, w_ffn_down, w_in, w_br_sb, w_br_dil, w_br_swa, w_out,
               sinks, rel_bias, final_gain]
    ms = [m_w_ada, m_b_ada, m_norm_gain, m_w_ffn_gate, m_w_ffn_up, m_w_ffn_down, m_w_in, m_w_br_sb, m_w_br_dil,
          m_w_br_swa, m_w_out, m_sinks, m_rel_bias, m_final_gain]
    vs = [v_w_ada, v_b_ada, v_norm_gain, v_w_ffn_gate, v_w_ffn_up, v_w_ffn_down, v_w_in, v_w_br_sb, v_w_br_dil,
          v_w_br_swa, v_w_out, v_sinks, v_rel_bias, v_final_gain]
    grads = [g_w_ada, g_b_ada, g_gain] + [None] * 8 + [g_sinks, g_rel, g_final]

    small = (1, 2, 11, 12, 13)
    deltas, new_ms, new_vs = [None] * 14, [None] * 14, [None] * 14
    _, deltas[0], new_ms[0], new_vs[0] = adamw("adamw_0", weights[0], grads[0], ms[0], vs[0])
    shapes = [weights[k].shape for k in small]
    packed = [_pack([t[k] for k in small], 168) for t in (weights, grads, ms, vs)]
    for dst, res in zip((deltas, new_ms, new_vs), adamw("adamw_small", *packed)[1:]):
        for k, t in zip(small, _unpack(res.reshape(-1), shapes)):
            dst[k] = t

    g = grads_out.finish((dx, deltas[0], deltas[1]))
    g_br = g["br"]
    grads[3:11] = [g["gate"].reshape(w_ffn_gate.shape), g["up"].reshape(w_ffn_up.shape),
                   g["down"].reshape(w_ffn_down.shape), g["in"], g_br[:, 0:256], g_br[:, 256:384], g_br[:, 384:768],
                   g["out"]]
    for k in range(3, 11):
        grads[k], deltas[k], new_ms[k], new_vs[k] = adamw(f"adamw_{k}", weights[k], grads[k], ms[k], vs[k])
    return (loss_sum[0], dx[None], *grads, *deltas, *new_ms, *new_vs)
```

```python
import functools
import math

import jax
import jax.numpy as jnp
from jax import lax
from jax.experimental import pallas as pl
from jax.experimental.pallas import tpu as pltpu

F32 = jnp.float32
BF16 = jnp.bfloat16

D_MODEL = 1024
SEQ = 2048
DEPTH = 2
HEAD_DIM = 64
BLK = 128
H_SB = 4
DIL_PATTERNS = ((128, 1), (512, 4), (2048, 16))
H_PER_DIL = 2
H_DIL = 6
H_SWA_Q = 6
H_SWA_KV = 2
SWA_WINDOW = 128
N_BUCKETS = 32
MAX_REL_DIST = 2048
D_FF = 2816
RMS_EPS = 1e-6
N_CHIPS = 4
N_DEV = 8
FF_SHARD = D_FF // N_CHIPS
D_QKV = 2560
D_IN = D_QKV + 3 * D_MODEL
IN_SHARD = D_IN // N_CHIPS
D_SHARD = D_MODEL // N_CHIPS
BR_ROWS = 768
NEG = -1e30
QK_SCALE = HEAD_DIM ** -0.5

ADAM_LR = 0.001
ADAM_B1 = 0.9
ADAM_B2 = 0.999
ADAM_EPS = 1e-08
ADAM_WD = 0.01
ADAM_STEP = 10

VMEM_LIMIT = 48 * 1024 * 1024
ROW_TILE = 256
MM_TILE = 512

NN = (((1,), (0,)), ((), ()))
NT = (((1,), (1,)), ((), ()))
TN = (((0,), (0,)), ((), ()))


def _params(sem=None):
    return pltpu.CompilerParams(dimension_semantics=sem, vmem_limit_bytes=VMEM_LIMIT)


def _dot(a, b, dims):
    return lax.dot_general(a, b, dims, preferred_element_type=F32)


def _sigmoid(x):
    return 1.0 / (1.0 + jnp.exp(-x))


def _matmul(name, grid, nk, k_axis, dims, n_pairs, in_specs, out_specs, out_shape, acc_shape, epilogue,
            operands, sem, aliases=None, prologue=None):
    n_in = len(in_specs)
    n_out = len(out_specs)

    def partial(ins):
        tot = None
        for p in range(n_pairs):
            a = ins[2 * p][...]
            if prologue is not None:
                a = prologue(p, a, ins)
            d = _dot(a, ins[2 * p + 1][...], dims)
            tot = d if tot is None else tot + d
        return tot

    def body(*refs):
        ins, outs = refs[:n_in], refs[n_in:n_in + n_out]
        ids = tuple(pl.program_id(a) for a in range(len(grid)))
        if nk == 1:
            epilogue(partial(ins), ins, outs, ids)
            return
        acc = refs[n_in + n_out]
        k = ids[k_axis]

        @pl.when(k == 0)
        def _():
            acc[...] = partial(ins)

        @pl.when(k > 0)
        def _():
            acc[...] += partial(ins)

        @pl.when(k == nk - 1)
        def _():
            epilogue(acc[...], ins, outs, ids)

    return pl.pallas_call(
        body, grid=grid, in_specs=in_specs, out_specs=out_specs, out_shape=out_shape,
        scratch_shapes=[] if nk == 1 else [pltpu.VMEM(acc_shape, F32)],
        input_output_aliases=aliases or {}, name=name, compiler_params=_params(sem),
    )(*operands)


def _row_spec(width=D_MODEL):
    return pl.BlockSpec((ROW_TILE, width), lambda i: (i, 0))


def _vec_spec(rows=1, width=D_MODEL):
    return pl.BlockSpec((rows, width), lambda i: (0, 0))


def prenorm(x, gain, scale, shift):
    def body(x_ref, g_ref, sc_ref, sh_ref, h_ref):
        xv = x_ref[...]
        r = lax.rsqrt(jnp.mean(xv * xv, axis=-1, keepdims=True) + RMS_EPS)
        h_ref[...] = (((xv * r) * g_ref[...]) * (1.0 + sc_ref[...]) + sh_ref[...]).astype(BF16)

    return pl.pallas_call(
        body, grid=(SEQ // ROW_TILE,), in_specs=[_row_spec(), _vec_spec(), _vec_spec(), _vec_spec()],
        out_specs=_row_spec(), out_shape=jax.ShapeDtypeStruct((SEQ, D_MODEL), BF16),
        name="prenorm", compiler_params=_params(("parallel",)),
    )(x, gain, scale, shift)


def resid_bwd(dxo, f, coef, mult):
    def body(dx_ref, f_ref, c_ref, df_ref, dc_ref):
        dx = dx_ref[...]
        df_ref[...] = (dx * (mult * c_ref[...])).astype(BF16)
        part = mult * jnp.sum(dx * f_ref[...], axis=0, keepdims=True)

        @pl.when(pl.program_id(0) == 0)
        def _():
            dc_ref[...] = jnp.zeros_like(dc_ref)

        dc_ref[0:1, :] += part

    return pl.pallas_call(
        body, grid=(SEQ // ROW_TILE,), in_specs=[_row_spec(), _row_spec(), _vec_spec()],
        out_specs=[_row_spec(), _vec_spec(8)],
        out_shape=[jax.ShapeDtypeStruct((SEQ, D_MODEL), BF16), jax.ShapeDtypeStruct((8, D_MODEL), F32)],
        name="resid_bwd", compiler_params=_params(("arbitrary",)),
    )(dxo, f, coef)


def final_loss(x, gain, target):
    def body(x_ref, g_ref, t_ref, loss_ref, dx_ref, dg_ref):
        xv = x_ref[...]
        g = g_ref[...]
        r = lax.rsqrt(jnp.mean(xv * xv, axis=-1, keepdims=True) + RMS_EPS)
        xh = xv * r
        e = xh * g - t_ref[...]
        part = 0.5 * jnp.sum(jnp.mean(e * e, axis=-1, keepdims=True), axis=0, keepdims=True)
        dy = e * (1.0 / D_MODEL)
        dyg = dy * g
        dx_ref[...] = r * (dyg - xh * jnp.mean(dyg * xh, axis=-1, keepdims=True))

        @pl.when(pl.program_id(0) == 0)
        def _():
            loss_ref[...] = jnp.zeros_like(loss_ref)
            dg_ref[...] = jnp.zeros_like(dg_ref)

        loss_ref[...] += jnp.broadcast_to(part, loss_ref.shape)
        dg_ref[0:1, :] += jnp.sum(dy * xh, axis=0, keepdims=True)

    return pl.pallas_call(
        body, grid=(SEQ // ROW_TILE,), in_specs=[_row_spec(), _vec_spec(), _row_spec()],
        out_specs=[_vec_spec(8, 128), _row_spec(), _vec_spec(8)],
        out_shape=[jax.ShapeDtypeStruct((8, 128), F32), jax.ShapeDtypeStruct((SEQ, D_MODEL), F32),
                   jax.ShapeDtypeStruct((8, D_MODEL), F32)],
        name="final_loss", compiler_params=_params(("arbitrary",)),
    )(x, gain, target)


def _prenorm_bwd_epilogue(dh, x_ref, dxo_ref, g_ref, sc_ref, dx_ref, stats_ref, first):
    xv = x_ref[...]
    g = g_ref[...]
    r = lax.rsqrt(jnp.mean(xv * xv, axis=-1, keepdims=True) + RMS_EPS)
    xh = xv * r
    dn = dh * (1.0 + sc_ref[...])
    dxh = dn * g
    dx_ref[...] = dxo_ref[...] + r * (dxh - xh * jnp.mean(dxh * xh, axis=-1, keepdims=True))

    @pl.when(first)
    def _():
        stats_ref[...] = jnp.zeros_like(stats_ref)

    stats_ref[0:1, :] += jnp.sum(dh, axis=0, keepdims=True)
    stats_ref[1:2, :] += jnp.sum(dh * (xh * g), axis=0, keepdims=True)
    stats_ref[2:3, :] += jnp.sum(dn * xh, axis=0, keepdims=True)


def ffn_up(h, wg_all, wu_all):
    def body(h_ref, wg_ref, wu_ref, a_ref, b_ref, s_ref):
        hv = h_ref[...]
        a = _dot(hv, wg_ref[...], NT)
        b = _dot(hv, wu_ref[...], NT)
        a_ref[...] = a
        b_ref[...] = b
        s_ref[...] = (a * _sigmoid(a) * b).astype(BF16)

    w_spec = pl.BlockSpec((None, FF_SHARD, D_MODEL), lambda j, i: (j, 0, 0))
    o_spec = pl.BlockSpec((None, MM_TILE, FF_SHARD), lambda j, i: (j, i, 0))
    hid = (N_CHIPS, SEQ, FF_SHARD)
    return pl.pallas_call(
        body, grid=(N_CHIPS, SEQ // MM_TILE),
        in_specs=[pl.BlockSpec((MM_TILE, D_MODEL), lambda j, i: (i, 0)), w_spec, w_spec],
        out_specs=[o_spec, o_spec, o_spec],
        out_shape=[jax.ShapeDtypeStruct(hid, F32), jax.ShapeDtypeStruct(hid, F32), jax.ShapeDtypeStruct(hid, BF16)],
        name="ffn_up", compiler_params=_params(("parallel", "parallel")),
    )(h, wg_all, wu_all)


def matmul_residual(name, a, a_spec, w_all, w_spec, x, coef, mult):
    def epilogue(acc, ins, outs, ids):
        outs[0][...] = acc
        outs[1][...] = ins[2][...] + (mult * ins[3][...]) * acc

    row = pl.BlockSpec((MM_TILE, D_MODEL), lambda i, j: (i, 0))
    return _matmul(
        name, (SEQ // MM_TILE, N_CHIPS), N_CHIPS, 1, NN, 1,
        [a_spec, w_spec, row, pl.BlockSpec((1, D_MODEL), lambda i, j: (0, 0))], [row, row],
        [jax.ShapeDtypeStruct((SEQ, D_MODEL), F32)] * 2, (MM_TILE, D_MODEL), epilogue,
        (a, w_all, x, coef), ("parallel", "arbitrary"))


def ffn_down(s, wd_all, x, gate):
    return matmul_residual(
        "ffn_down", s, pl.BlockSpec((None, MM_TILE, FF_SHARD), lambda i, j: (j, i, 0)),
        wd_all, pl.BlockSpec((None, FF_SHARD, D_MODEL), lambda i, j: (j, 0, 0)), x, gate, 0.5)


def ffn_bwd_hidden(df, wd_all, a, b):
    def epilogue(ds, ins, outs, ids):
        av, bv = ins[2][...], ins[3][...]
        sig = _sigmoid(av)
        outs[0][...] = (ds * bv * (sig * (1.0 + av * (1.0 - sig)))).astype(BF16)
        outs[1][...] = (ds * (av * sig)).astype(BF16)

    hid_spec = pl.BlockSpec((None, MM_TILE, FF_SHARD), lambda j, i: (j, i, 0))
    hid = jax.ShapeDtypeStruct((N_CHIPS, SEQ, FF_SHARD), BF16)
    return _matmul(
        "ffn_bwd_hidden", (N_CHIPS, SEQ // MM_TILE), 1, None, NT, 1,
        [pl.BlockSpec((MM_TILE, D_MODEL), lambda j, i: (i, 0)),
         pl.BlockSpec((None, FF_SHARD, D_MODEL), lambda j, i: (j, 0, 0)), hid_spec, hid_spec],
        [hid_spec, hid_spec], [hid, hid], None, epilogue, (df, wd_all, a, b), ("parallel", "parallel"))


def grad_weight(name, lhs, lhs_spec, rhs, rhs_spec, shape):
    def epilogue(acc, ins, outs, ids):
        outs[0][...] = acc.astype(BF16)

    return _matmul(
        name, (N_CHIPS, SEQ // MM_TILE), SEQ // MM_TILE, 1, TN, 1,
        [lhs_spec, rhs_spec], [pl.BlockSpec((None,) + shape, lambda j, k: (j, 0, 0))],
        [jax.ShapeDtypeStruct((N_CHIPS,) + shape, BF16)], shape, epilogue, (lhs, rhs), ("parallel", "arbitrary"))[0]


def ffn_grad_weights(h, s, df, da, db):
    tok = pl.BlockSpec((MM_TILE, D_MODEL), lambda j, k: (k, 0))
    hid = pl.BlockSpec((None, MM_TILE, FF_SHARD), lambda j, k: (j, k, 0))
    return (grad_weight("grad_w_gate", da, hid, h, tok, (FF_SHARD, D_MODEL)),
            grad_weight("grad_w_up", db, hid, h, tok, (FF_SHARD, D_MODEL)),
            grad_weight("grad_w_down", s, hid, df, tok, (FF_SHARD, D_MODEL)))


def matmul_prenorm_bwd(name, dims, pairs, pair_specs, x, dxo, gain, scale):
    n = len(pairs)

    def epilogue(dh, ins, outs, ids):
        _prenorm_bwd_epilogue(dh, ins[n], ins[n + 1], ins[n + 2], ins[n + 3], outs[0], outs[1], ids[0] == 0)

    row = pl.BlockSpec((MM_TILE, D_MODEL), lambda i, j: (i, 0))
    vec = pl.BlockSpec((1, D_MODEL), lambda i, j: (0, 0))
    return _matmul(
        name, (SEQ // MM_TILE, N_CHIPS), N_CHIPS, 1, dims, len(pairs) // 2,
        list(pair_specs) + [row, row, vec, vec], [row, pl.BlockSpec((8, D_MODEL), lambda i, j: (0, 0))],
        [jax.ShapeDtypeStruct((SEQ, D_MODEL), F32), jax.ShapeDtypeStruct((8, D_MODEL), F32)],
        (MM_TILE, D_MODEL), epilogue, tuple(pairs) + (x, dxo, gain, scale), ("arbitrary", "arbitrary"))


def ffn_bwd_input(da, db, wg_all, wu_all, x, dxo, gain, scale):
    hid = pl.BlockSpec((None, MM_TILE, FF_SHARD), lambda i, j: (j, i, 0))
    w = pl.BlockSpec((None, FF_SHARD, D_MODEL), lambda i, j: (j, 0, 0))
    return matmul_prenorm_bwd("ffn_bwd_input", NN, (da, wg_all, db, wu_all), (hid, w, hid, w), x, dxo, gain, scale)


def in_proj(h, w_all):
    def epilogue(acc, ins, outs, ids):
        outs[0][...] = acc

    return _matmul(
        "in_proj", (N_CHIPS, SEQ // MM_TILE), 1, None, NN, 1,
        [pl.BlockSpec((MM_TILE, D_MODEL), lambda j, i: (i, 0)),
         pl.BlockSpec((None, D_MODEL, IN_SHARD), lambda j, i: (j, 0, 0))],
        [pl.BlockSpec((MM_TILE, IN_SHARD), lambda j, i: (i, j))], [jax.ShapeDtypeStruct((SEQ, D_IN), F32)],
        None, epilogue, (h, w_all), ("parallel", "parallel"))[0]


_GATE_BLOCK0 = D_QKV // D_SHARD


def _branch_products(o, w_ref):
    ob = o.astype(BF16)
    return (_dot(ob[:, 0:256], w_ref[0:256, :], NN), _dot(ob[:, 256:384], w_ref[256:384, :], NN),
            _dot(ob[:, 384:768], w_ref[384:768, :], NN))


def merge_branches(o_cat, wbr_all, proj):
    def body(o_ref, w_ref, g0_ref, g1_ref, g2_ref, m_ref):
        u = _branch_products(o_ref[...], w_ref)
        m_ref[...] = (_sigmoid(g0_ref[...]) * u[0] + _sigmoid(g1_ref[...]) * u[1]
                      + _sigmoid(g2_ref[...]) * u[2]).astype(BF16)

    def gate_spec(b):
        return pl.BlockSpec((MM_TILE, D_SHARD), lambda i, j: (i, _GATE_BLOCK0 + 4 * b + j))

    return pl.pallas_call(
        body, grid=(SEQ // MM_TILE, N_CHIPS),
        in_specs=[pl.BlockSpec((MM_TILE, BR_ROWS), lambda i, j: (i, 0)),
                  pl.BlockSpec((None, BR_ROWS, D_SHARD), lambda i, j: (j, 0, 0)),
                  gate_spec(0), gate_spec(1), gate_spec(2)],
        out_specs=pl.BlockSpec((MM_TILE, D_SHARD), lambda i, j: (i, j)),
        out_shape=jax.ShapeDtypeStruct((SEQ, D_MODEL), BF16),
        name="merge_branches", compiler_params=_params(("parallel", "parallel")),
    )(o_cat, wbr_all, proj, proj, proj)


def out_proj(merged, wout_all, x, gate):
    return matmul_residual(
        "out_proj", merged, pl.BlockSpec((MM_TILE, D_SHARD), lambda i, j: (i, j)),
        wout_all, pl.BlockSpec((None, D_SHARD, D_MODEL), lambda i, j: (j, 0, 0)), x, gate, 1.0)


def merge_bwd(dmo, wout_all, o_cat, wbr_all, proj):
    def epilogue(dm, ins, outs, ids):
        u = _branch_products(ins[2][...], ins[3])
        for b in range(3):
            sig = _sigmoid(ins[4 + b][...])
            outs[b][...] = (dm * sig).astype(BF16)
            outs[3 + b][...] = (dm * u[b] * (sig * (1.0 - sig))).astype(BF16)

    def gate_spec(b):
        return pl.BlockSpec((MM_TILE, D_SHARD), lambda j, i: (i, _GATE_BLOCK0 + 4 * b + j))

    col = pl.BlockSpec((MM_TILE, D_SHARD), lambda j, i: (i, j))
    du = jax.ShapeDtypeStruct((SEQ, D_MODEL), BF16)
    return _matmul(
        "merge_bwd", (N_CHIPS, SEQ // MM_TILE), 1, None, NT, 1,
        [pl.BlockSpec((MM_TILE, D_MODEL), lambda j, i: (i, 0)),
         pl.BlockSpec((None, D_SHARD, D_MODEL), lambda j, i: (j, 0, 0)),
         pl.BlockSpec((MM_TILE, BR_ROWS), lambda j, i: (i, 0)),
         pl.BlockSpec((None, BR_ROWS, D_SHARD), lambda j, i: (j, 0, 0)),
         gate_spec(0), gate_spec(1), gate_spec(2)],
        [col] * 6, [du] * 6,
        None, epilogue, (dmo, wout_all, o_cat, wbr_all, proj, proj, proj), ("parallel", "parallel"))


def branch_bwd_input(du, wbr_all):
    def body(d0_ref, d1_ref, d2_ref, w_ref, o_ref, acc):
        j = pl.program_id(1)
        parts = (_dot(d0_ref[...], w_ref[0:256, :], NT), _dot(d1_ref[...], w_ref[256:384, :], NT),
                 _dot(d2_ref[...], w_ref[384:768, :], NT))

        @pl.when(j == 0)
        def _():
            acc[:, 0:256], acc[:, 256:384], acc[:, 384:768] = parts

        @pl.when(j > 0)
        def _():
            acc[:, 0:256] += parts[0]
            acc[:, 256:384] += parts[1]
            acc[:, 384:768] += parts[2]

        @pl.when(j == N_CHIPS - 1)
        def _():
            o_ref[...] = acc[...]

    col = pl.BlockSpec((MM_TILE, D_SHARD), lambda i, j: (i, j))
    return pl.pallas_call(
        body, grid=(SEQ // MM_TILE, N_CHIPS),
        in_specs=[col, col, col, pl.BlockSpec((None, BR_ROWS, D_SHARD), lambda i, j: (j, 0, 0))],
        out_specs=pl.BlockSpec((MM_TILE, BR_ROWS), lambda i, j: (i, 0)),
        out_shape=jax.ShapeDtypeStruct((SEQ, BR_ROWS), F32),
        scratch_shapes=[pltpu.VMEM((MM_TILE, BR_ROWS), F32)],
        name="branch_bwd_input", compiler_params=_params(("parallel", "arbitrary")),
    )(du[0], du[1], du[2], wbr_all)


def branch_grad_weights(o_cat, du):
    def body(o_ref, d0_ref, d1_ref, d2_ref, g_ref, acc):
        k = pl.program_id(1)
        ob = o_ref[...].astype(BF16)
        parts = (_dot(ob[:, 0:256], d0_ref[...], TN), _dot(ob[:, 256:384], d1_ref[...], TN),
                 _dot(ob[:, 384:768], d2_ref[...], TN))

        @pl.when(k == 0)
        def _():
            acc[0:256, :], acc[256:384, :], acc[384:768, :] = parts

        @pl.when(k > 0)
        def _():
            acc[0:256, :] += parts[0]
            acc[256:384, :] += parts[1]
            acc[384:768, :] += parts[2]

        @pl.when(k == SEQ // MM_TILE - 1)
        def _():
            g_ref[...] = acc[...].astype(BF16)

    col = pl.BlockSpec((MM_TILE, D_SHARD), lambda j, k: (k, j))
    return pl.pallas_call(
        body, grid=(N_CHIPS, SEQ // MM_TILE),
        in_specs=[pl.BlockSpec((MM_TILE, BR_ROWS), lambda j, k: (k, 0)), col, col, col],
        out_specs=pl.BlockSpec((None, BR_ROWS, D_SHARD), lambda j, k: (j, 0, 0)),
        out_shape=jax.ShapeDtypeStruct((N_CHIPS, BR_ROWS, D_SHARD), BF16),
        scratch_shapes=[pltpu.VMEM((BR_ROWS, D_SHARD), F32)],
        name="branch_grad_weights", compiler_params=_params(("parallel", "arbitrary")),
    )(o_cat, du[0], du[1], du[2])


def mixer_bwd_input(dproj, win_all, x, dxo, gain, scale):
    return matmul_prenorm_bwd(
        "mixer_bwd_input", NT, (dproj, win_all),
        (pl.BlockSpec((MM_TILE, IN_SHARD), lambda i, j: (i, j)),
         pl.BlockSpec((None, D_MODEL, IN_SHARD), lambda i, j: (j, 0, 0))), x, dxo, gain, scale)


BATCH_QK = (((2,), (2,)), ((0,), (0,)))
BATCH_PV = (((2,), (1,)), ((0,), (0,)))
BATCH_TN = (((1,), (1,)), ((0,), (0,)))


SB_WIDTH = H_SB * HEAD_DIM
SB_ROWS = H_SB * BLK


def _split_dot(v, tri):
    hi = v.astype(BF16)
    lo = (v - hi.astype(F32)).astype(BF16)
    return _dot(hi, tri, NN) + _dot(lo, tri, NN)


def _tri(cmp):
    return cmp(lax.broadcasted_iota(jnp.int32, (BLK, BLK), 0), lax.broadcasted_iota(jnp.int32, (BLK, BLK), 1)).astype(BF16)


def _head_masks():
    lane = lax.broadcasted_iota(jnp.int32, (1, SB_WIDTH), 1) // HEAD_DIM
    return [lane == h for h in range(H_SB)]


def _stack_heads(x, masks):
    return jnp.concatenate([jnp.where(m, x, jnp.zeros_like(x)) for m in masks], axis=0)


def _merge_heads(y, masks):
    out = jnp.where(masks[0], y[0:BLK], 0.0)
    for h in range(1, H_SB):
        out = jnp.where(masks[h], y[h * BLK:(h + 1) * BLK], out)
    return out


def _sb_scores(q4, k_ref, i, j):
    rows = pl.ds(pl.multiple_of(j * BLK, BLK), BLK)
    z = _dot(q4, k_ref[rows, :], NT)
    tile = (SB_ROWS, BLK)
    q_pos = i * BLK + (lax.broadcasted_iota(jnp.int32, tile, 0) & (BLK - 1))
    before = (j * BLK + lax.broadcasted_iota(jnp.int32, tile, 1)) < q_pos
    soft = jnp.log(1.0 + jnp.exp(-jnp.abs(z)))
    log_fail = jnp.where(before, -(jnp.maximum(z, 0.0) + soft), 0.0)
    log_hit = jnp.minimum(z, 0.0) - soft
    return rows, before, log_fail, log_hit


def sb_forward(qkv):
    def body(q_ref, k_ref, v_ref, o_ref, tot_ref):
        i = pl.program_id(0)
        masks = _head_masks()
        q4 = _stack_heads(q_ref[...] * QK_SCALE, masks)
        later = _tri(lambda r, c: r > c)

        def step(t, carry):
            o, run = carry
            rows, before, log_fail, log_hit = _sb_scores(q4, k_ref, i, i - t)
            between = _split_dot(log_fail, later) + run
            w = jnp.where(before, jnp.exp(log_hit + between), 0.0)
            o = o + _merge_heads(_dot(w.astype(BF16), v_ref[rows, :], NN), masks)
            return o, run + jnp.sum(log_fail, axis=1, keepdims=True)

        o, run = lax.fori_loop(0, i + 1, step, (jnp.zeros((BLK, SB_WIDTH), F32), jnp.zeros((SB_ROWS, 1), F32)))
        o_ref[...] = o
        tot_ref[...] = run

    return pl.pallas_call(
        body, grid=(N_BLK,),
        in_specs=[pl.BlockSpec((BLK, SB_WIDTH), lambda i: (i, 0)), pl.BlockSpec((SEQ, SB_WIDTH), lambda i: (0, 1)),
                  pl.BlockSpec((SEQ, SB_WIDTH), lambda i: (0, 2))],
        out_specs=[pl.BlockSpec((BLK, SB_WIDTH), lambda i: (i, 0)), pl.BlockSpec((None, SB_ROWS, 1), lambda i: (i, 0, 0))],
        out_shape=[jax.ShapeDtypeStruct((SEQ, SB_WIDTH), F32), jax.ShapeDtypeStruct((N_BLK, SB_ROWS, 1), F32)],
        name="sb_forward", compiler_params=_params(("parallel",)),
    )(qkv, qkv, qkv)


def sb_backward(qkv, total, do_cat):
    def body(q_ref, k_ref, v_ref, tot_ref, do_ref, dq_ref, dk_ref, dv_ref):
        i = pl.program_id(0)

        @pl.when(i == 0)
        def _():
            dk_ref[...] = jnp.zeros_like(dk_ref)
            dv_ref[...] = jnp.zeros_like(dv_ref)

        masks = _head_masks()
        q4 = _stack_heads(q_ref[...] * QK_SCALE, masks)
        do4 = _stack_heads(do_ref[...].astype(BF16), masks)
        total_v = tot_ref[...]
        upto = _tri(lambda r, c: r <= c)
        earlier = _tri(lambda r, c: r < c)

        def step(j, carry):
            dq, seen, g_seen = carry
            rows, before, log_fail, log_hit = _sb_scores(q4, k_ref, i, j)
            between = total_v - (seen + _split_dot(log_fail, upto))
            w = jnp.where(before, jnp.exp(log_hit + between), 0.0)
            g = _dot(do4, v_ref[rows, :], NT) * w
            g_earlier = g_seen + _split_dot(g, earlier)
            sig = jnp.exp(log_hit)
            dz = jnp.where(before, g * (1.0 - sig) - g_earlier * sig, 0.0).astype(BF16)
            dq = dq + _merge_heads(_dot(dz, k_ref[rows, :], NN), masks)
            dk_ref[rows, :] += _dot(dz, q4, TN)
            dv_ref[rows, :] += _dot(w.astype(BF16), do4, TN)
            return dq, seen + jnp.sum(log_fail, axis=1, keepdims=True), g_seen + jnp.sum(g, axis=1, keepdims=True)

        zero = jnp.zeros((SB_ROWS, 1), F32)
        dq, _, _ = lax.fori_loop(0, i + 1, step, (jnp.zeros((BLK, SB_WIDTH), F32), zero, zero))
        dq_ref[...] = dq * QK_SCALE

    blk = pl.BlockSpec((BLK, SB_WIDTH), lambda i: (i, 0))
    full = pl.BlockSpec((SEQ, SB_WIDTH), lambda i: (0, 0))
    shape = jax.ShapeDtypeStruct((SEQ, SB_WIDTH), F32)
    return pl.pallas_call(
        body, grid=(N_BLK,),
        in_specs=[blk, pl.BlockSpec((SEQ, SB_WIDTH), lambda i: (0, 1)), pl.BlockSpec((SEQ, SB_WIDTH), lambda i: (0, 2)),
                  pl.BlockSpec((None, SB_ROWS, 1), lambda i: (i, 0, 0)), blk],
        out_specs=[blk, full, full], out_shape=[shape, shape, shape],
        name="sb_backward", compiler_params=_params(("arbitrary",)),
    )(qkv, qkv, qkv, total, do_cat)


def _band_scores(q_ref, kp_ref, ko_ref, bias_ref, hb, prev_mask):
    b = pl.program_id(1)
    qs = q_ref[...]
    s_prev = _dot(qs, kp_ref[...], BATCH_QK) + bias_ref[:, :, 0:BLK]
    s_prev = jnp.concatenate(
        [jnp.where((b & prev_mask(pl.program_id(0) * hb + t)) != 0, s_prev[t:t + 1], NEG) for t in range(hb)], axis=0)
    s_own = _dot(qs, ko_ref[...], BATCH_QK) + bias_ref[:, :, BLK:2 * BLK]
    return qs, s_prev, s_own


def _band_specs(hb, rows, t_n):
    def q_spec(width):
        return pl.BlockSpec((hb, None, rows, width), lambda h, b: (h, b, 0, 0))

    own = pl.BlockSpec((hb, BLK, HEAD_DIM), lambda h, b: (h, b, 0))
    prev = pl.BlockSpec((hb, BLK, HEAD_DIM), lambda h, b: (h, jnp.maximum(b - 1, 0), 0))
    per_head = lambda r, width: pl.BlockSpec((hb, r, width), lambda h, b: (h, 0, 0))
    return q_spec, own, prev, per_head


def banded_forward(name, q, k, v, bias, sinks, hb, prev_mask):
    h_n, nb, rows, _ = q.shape

    def body(q_ref, kp_ref, ko_ref, vp_ref, vo_ref, bias_ref, sink_ref, o_ref, lse_ref):
        _, s_prev, s_own = _band_scores(q_ref, kp_ref, ko_ref, bias_ref, hb, prev_mask)
        sink = sink_ref[...]
        m = jnp.maximum(jnp.maximum(jnp.max(s_prev, axis=2, keepdims=True), jnp.max(s_own, axis=2, keepdims=True)), sink)
        p_prev = jnp.exp(s_prev - m)
        p_own = jnp.exp(s_own - m)
        denom = jnp.sum(p_prev, axis=2, keepdims=True) + jnp.sum(p_own, axis=2, keepdims=True) + jnp.exp(sink - m)
        o = _dot(p_prev.astype(BF16), vp_ref[...], BATCH_PV) + _dot(p_own.astype(BF16), vo_ref[...], BATCH_PV)
        o_ref[...] = o / denom
        lse_ref[...] = m + jnp.log(denom)

    q_spec, own, prev, per_head = _band_specs(hb, rows, k.shape[1])
    return pl.pallas_call(
        body, grid=(h_n // hb, nb),
        in_specs=[q_spec(HEAD_DIM), prev, own, prev, own, per_head(rows, 2 * BLK), per_head(rows, 1)],
        out_specs=[q_spec(HEAD_DIM), q_spec(1)],
        out_shape=[jax.ShapeDtypeStruct(q.shape, F32), jax.ShapeDtypeStruct((h_n, nb, rows, 1), F32)],
        name=name, compiler_params=_params(("parallel", "parallel")),
    )(q, k, k, v, v, bias, sinks)


def banded_backward(name, q, k, v, bias, sinks, o, lse, do, dlse, hb, prev_mask):
    h_n, nb, rows, _ = q.shape
    t_n = k.shape[1]

    def body(q_ref, kp_ref, ko_ref, vp_ref, vo_ref, bias_ref, sink_ref, o_ref, lse_ref, do_ref, dlse_ref,
             dq_ref, dk_ref, dv_ref, dbias_ref, dsink_ref):
        b = pl.program_id(1)

        @pl.when(b == 0)
        def _():
            dk_ref[...] = jnp.zeros_like(dk_ref)
            dv_ref[...] = jnp.zeros_like(dv_ref)
            dbias_ref[...] = jnp.zeros_like(dbias_ref)
            dsink_ref[...] = jnp.zeros_like(dsink_ref)

        qs, s_prev, s_own = _band_scores(q_ref, kp_ref, ko_ref, bias_ref, hb, prev_mask)
        lse_v = lse_ref[...]
        dov = do_ref[...]
        dob = dov.astype(BF16)
        shift = dlse_ref[...] - jnp.sum(dov * o_ref[...], axis=2, keepdims=True)
        p_prev = jnp.exp(s_prev - lse_v)
        p_own = jnp.exp(s_own - lse_v)
        ds_prev = p_prev * (_dot(dob, vp_ref[...], BATCH_QK) + shift)
        ds_own = p_own * (_dot(dob, vo_ref[...], BATCH_QK) + shift)
        dbias_ref[:, :, 0:BLK] += ds_prev
        dbias_ref[:, :, BLK:2 * BLK] += ds_own
        d_sink = jnp.exp(sink_ref[...] - lse_v) * shift
        for g in range(rows // BLK):
            dsink_ref[:, g:g + 1, :] += jnp.sum(d_sink[:, g * BLK:(g + 1) * BLK, :], axis=1, keepdims=True)
        ds_prev = ds_prev.astype(BF16)
        ds_own = ds_own.astype(BF16)
        dq_ref[...] = (_dot(ds_prev, kp_ref[...], BATCH_PV) + _dot(ds_own, ko_ref[...], BATCH_PV)) * QK_SCALE
        rows_prev = pl.ds(pl.multiple_of(jnp.maximum(b - 1, 0) * BLK, BLK), BLK)
        rows_own = pl.ds(pl.multiple_of(b * BLK, BLK), BLK)
        dk_ref[:, rows_prev, :] += _dot(ds_prev, qs, BATCH_TN)
        dk_ref[:, rows_own, :] += _dot(ds_own, qs, BATCH_TN)
        dv_ref[:, rows_prev, :] += _dot(p_prev.astype(BF16), dob, BATCH_TN)
        dv_ref[:, rows_own, :] += _dot(p_own.astype(BF16), dob, BATCH_TN)

    q_spec, own, prev, per_head = _band_specs(hb, rows, t_n)
    kv_full = per_head(t_n, HEAD_DIM)
    kv_shape = jax.ShapeDtypeStruct((h_n, t_n, HEAD_DIM), F32)
    return pl.pallas_call(
        body, grid=(h_n // hb, nb),
        in_specs=[q_spec(HEAD_DIM), prev, own, prev, own, per_head(rows, 2 * BLK), per_head(rows, 1),
                  q_spec(HEAD_DIM), q_spec(1), q_spec(HEAD_DIM), q_spec(1)],
        out_specs=[q_spec(HEAD_DIM), kv_full, kv_full, per_head(rows, 2 * BLK), per_head(rows // BLK, BLK)],
        out_shape=[jax.ShapeDtypeStruct(q.shape, F32), kv_shape, kv_shape,
                   jax.ShapeDtypeStruct((h_n, rows, 2 * BLK), F32), jax.ShapeDtypeStruct((h_n, rows // BLK, BLK), F32)],
        name=name, compiler_params=_params(("parallel", "arbitrary")),
    )(q, k, k, v, v, bias, sinks, o, lse, do, dlse)


def _dil_prev_mask(head):
    group = head // H_PER_DIL
    return jnp.where(group == 0, 15, jnp.where(group == 1, 3, 0))


def _swa_prev_mask(head):
    del head
    return 15


DIL_HEADS_PER_STEP = 3
SWA_GROUP = H_SWA_Q // H_SWA_KV
N_BLK = SEQ // BLK


def dilated_merge(o, lse):
    def body(o_ref, l_ref, out_ref):
        lv = l_ref[...]
        m = jnp.max(lv, axis=0, keepdims=True)
        e = jnp.exp(lv - m)
        alpha = e / jnp.sum(e, axis=0, keepdims=True)
        out_ref[...] = jnp.sum(alpha * o_ref[...], axis=0)

    return pl.pallas_call(
        body, grid=(H_PER_DIL, SEQ // ROW_TILE),
        in_specs=[pl.BlockSpec((3, None, ROW_TILE, HEAD_DIM), lambda h, i: (0, h, i, 0)),
                  pl.BlockSpec((3, None, ROW_TILE, 1), lambda h, i: (0, h, i, 0))],
        out_specs=pl.BlockSpec((None, ROW_TILE, HEAD_DIM), lambda h, i: (h, i, 0)),
        out_shape=jax.ShapeDtypeStruct((H_PER_DIL, SEQ, HEAD_DIM), F32),
        name="dilated_merge", compiler_params=_params(("parallel", "parallel")),
    )(o, lse)


def dilated_merge_bwd(o, lse, dout):
    def body(o_ref, l_ref, d_ref, do_ref, dl_ref):
        lv = l_ref[...]
        m = jnp.max(lv, axis=0, keepdims=True)
        e = jnp.exp(lv - m)
        alpha = e / jnp.sum(e, axis=0, keepdims=True)
        dv = d_ref[...][None]
        do_ref[...] = alpha * dv
        dalpha = jnp.sum(dv * o_ref[...], axis=-1, keepdims=True)
        dl_ref[...] = alpha * (dalpha - jnp.sum(alpha * dalpha, axis=0, keepdims=True))

    o_spec = pl.BlockSpec((3, None, ROW_TILE, HEAD_DIM), lambda h, i: (0, h, i, 0))
    l_spec = pl.BlockSpec((3, None, ROW_TILE, 1), lambda h, i: (0, h, i, 0))
    return pl.pallas_call(
        body, grid=(H_PER_DIL, SEQ // ROW_TILE),
        in_specs=[o_spec, l_spec, pl.BlockSpec((None, ROW_TILE, HEAD_DIM), lambda h, i: (h, i, 0))],
        out_specs=[o_spec, l_spec],
        out_shape=[jax.ShapeDtypeStruct(o.shape, F32), jax.ShapeDtypeStruct(lse.shape, F32)],
        name="dilated_merge_bwd", compiler_params=_params(("parallel", "parallel")),
    )(o, lse, dout)


def rel_bias_reduce(dbias0, dbias1, bucket):
    def body(d0_ref, d1_ref, b_ref, o_ref):
        dv, bv = d0_ref[...] + d1_ref[...], b_ref[...]
        lane = lax.broadcasted_iota(jnp.int32, (1, BLK), 1)
        acc = jnp.zeros((1, BLK), F32)
        for bkt in range(N_BUCKETS):
            acc = acc + jnp.where(lane == bkt, jnp.sum(jnp.where(bv == bkt, dv, 0.0)), 0.0)
        o_ref[...] = acc

    tile = pl.BlockSpec((None, BLK, 2 * BLK), lambda h: (h, 0, 0))
    return pl.pallas_call(
        body, grid=(dbias0.shape[0],), in_specs=[tile, tile, tile],
        out_specs=pl.BlockSpec((None, 1, BLK), lambda h: (h, 0, 0)),
        out_shape=jax.ShapeDtypeStruct((dbias0.shape[0], 1, BLK), F32),
        name="rel_bias_reduce", compiler_params=_params(("parallel",)),
    )(dbias0, dbias1, bucket)


def _heads(t):
    return t.reshape(SEQ, -1, HEAD_DIM).transpose(1, 0, 2)


def _unheads(t):
    return t.transpose(1, 0, 2).reshape(SEQ, -1)


def _dilate(t):
    parts = []
    for g, (_, d) in enumerate(DIL_PATTERNS):
        tg = t[:, 128 * g:128 * (g + 1)].reshape(SEQ // d, d, H_PER_DIL, HEAD_DIM).transpose(2, 1, 0, 3)
        parts.append(tg.reshape(H_PER_DIL, SEQ, HEAD_DIM))
    return jnp.concatenate(parts, axis=0)


def _undilate(t):
    outs = []
    for g, (_, d) in enumerate(DIL_PATTERNS):
        tg = t[2 * g:2 * g + 2].reshape(H_PER_DIL, d, SEQ // d, -1).transpose(0, 2, 1, 3)
        outs.append(tg.reshape(H_PER_DIL, SEQ, -1))
    return jnp.stack(outs)


def _redilate(t):
    parts = []
    for g, (_, d) in enumerate(DIL_PATTERNS):
        tg = t[g].reshape(H_PER_DIL, SEQ // d, d, -1).transpose(0, 2, 1, 3)
        parts.append(tg.reshape(H_PER_DIL, SEQ, -1))
    return jnp.concatenate(parts, axis=0)


def _t5_bucket(n):
    max_exact = N_BUCKETS // 2
    nf = jnp.maximum(n, 1).astype(F32)
    large = max_exact + (jnp.log(nf / max_exact) / math.log(MAX_REL_DIST / max_exact)
                         * (N_BUCKETS - max_exact)).astype(jnp.int32)
    large = jnp.minimum(large, N_BUCKETS - 1)
    return jnp.where(n < max_exact, n, large)


def band_tables(rel_bias):
    rel = jnp.arange(BLK)[:, None] + BLK - jnp.arange(2 * BLK)[None, :]
    buckets = []
    patterns = [(d, w // d) for w, d in DIL_PATTERNS for _ in range(H_PER_DIL)] + [(1, SWA_WINDOW - 1)] * H_SWA_Q
    for d, max_dist in patterns:
        band = (rel >= 0) & (rel <= max_dist)
        buckets.append(jnp.where(band, _t5_bucket(jnp.maximum(rel, 0) * d), -1))
    buckets = jnp.stack(buckets).astype(jnp.int32)

    def body(table_ref, b_ref, o_ref):
        h = pl.program_id(0)
        bv = b_ref[...]
        tile = jnp.full(bv.shape, NEG, F32)
        for bkt in range(N_BUCKETS):
            tile = jnp.where(bv == bkt, table_ref[h, bkt], tile)
        o_ref[...] = tile

    spec = pl.BlockSpec((None, BLK, 2 * BLK), lambda h: (h, 0, 0))
    tiles = pl.pallas_call(
        body, grid=(len(patterns),), in_specs=[pl.BlockSpec(memory_space=pltpu.SMEM), spec], out_specs=spec,
        out_shape=jax.ShapeDtypeStruct(buckets.shape, F32), name="band_tables", compiler_params=_params(("parallel",)),
    )(rel_bias.T, buckets)
    return tiles[:H_DIL], tiles[H_DIL:], buckets


def _swa_rows(t):
    t = t.reshape(N_BLK, BLK, H_SWA_KV, SWA_GROUP, HEAD_DIM).transpose(2, 0, 3, 1, 4)
    return t.reshape(H_SWA_KV, N_BLK, SWA_GROUP * BLK, HEAD_DIM)


def _swa_tokens(t):
    t = t.reshape(H_SWA_KV, N_BLK, SWA_GROUP, BLK, HEAD_DIM).transpose(1, 3, 0, 2, 4)
    return t.reshape(SEQ, H_SWA_Q * HEAD_DIM)


def _sink_rows(sinks):
    return jnp.broadcast_to(sinks.reshape(H_SWA_KV, SWA_GROUP, 1, 1), (H_SWA_KV, SWA_GROUP, BLK, 1)).reshape(
        H_SWA_KV, SWA_GROUP * BLK, 1)


def _no_sinks():
    return jnp.full((H_DIL, BLK, 1), NEG, F32)


def _vec(v):
    return v.reshape(1, D_MODEL)


def ffn_forward(x, gain, mod, w):
    h = prenorm(x, _vec(gain), _vec(mod[1]), _vec(mod[0]))
    a, b, s = ffn_up(h, w[0], w[1])
    f, xo = ffn_down(s, w[2], x, _vec(mod[2]))
    return xo, (x, h, a, b, s, f)


def ffn_backward(dxo, saved, gain, mod, w):
    x, h, a, b, s, f = saved
    df, dgate = resid_bwd(dxo, f, _vec(mod[2]), 0.5)
    da, db = ffn_bwd_hidden(df, w[2], a, b)
    grads = ffn_grad_weights(h, s, df, da, db)
    dx, stats = ffn_bwd_input(da, db, w[0], w[1], x, dxo, _vec(gain), _vec(mod[1]))
    return dx, jnp.stack([stats[0], stats[1], dgate[0]]), stats[2], grads


def mixer_forward(x, gain, mod, sinks, bias_dil, bias_swa, w):
    h = prenorm(x, _vec(gain), _vec(mod[1]), _vec(mod[0]))
    proj = in_proj(h, w[0])
    qkv = proj[:, :D_QKV].astype(BF16)
    q_dil, k_dil, v_dil = _dilate(qkv[:, 768:1152] * QK_SCALE), _dilate(qkv[:, 1152:1536]), _dilate(qkv[:, 1536:1920])
    q_swa, k_swa, v_swa = _swa_rows(qkv[:, 1920:2304] * QK_SCALE), _heads(qkv[:, 2304:2432]), _heads(qkv[:, 2432:2560])
    o_sb, total_sb = sb_forward(qkv)
    q_dil = q_dil.reshape(H_DIL, N_BLK, BLK, HEAD_DIM)
    o_dd, lse_dd = banded_forward("dilated_forward", q_dil, k_dil, v_dil, bias_dil, _no_sinks(), DIL_HEADS_PER_STEP,
                                  _dil_prev_mask)
    o_dt, lse_dt = _undilate(o_dd.reshape(H_DIL, SEQ, HEAD_DIM)), _undilate(lse_dd.reshape(H_DIL, SEQ, 1))
    o_dil = dilated_merge(o_dt, lse_dt)
    bias_swa = bias_swa.reshape(H_SWA_KV, SWA_GROUP * BLK, 2 * BLK)
    o_swa, lse_swa = banded_forward("swa_forward", q_swa, k_swa, v_swa, bias_swa, _sink_rows(sinks), 1, _swa_prev_mask)
    o_cat = jnp.concatenate([o_sb, _unheads(o_dil), _swa_tokens(o_swa)], axis=1)
    merged = merge_branches(o_cat, w[1], proj)
    mo, xo = out_proj(merged, w[2], x, _vec(mod[2]))
    saved = (x, h, proj, (qkv, total_sb), (q_dil, k_dil, v_dil, o_dd, lse_dd, o_dt, lse_dt),
             (q_swa, k_swa, v_swa, o_swa, lse_swa), o_cat, merged, mo)
    return xo, saved


def mixer_backward(dxo, saved, gain, mod, sinks, bias_dil, bias_swa, w):
    x, h, proj, sb, dil, swa, o_cat, merged, mo = saved
    dmo, dgate = resid_bwd(dxo, mo, _vec(mod[2]), 1.0)
    tok = pl.BlockSpec((MM_TILE, D_MODEL), lambda j, k: (k, 0))
    g_out = grad_weight("grad_w_out", merged, pl.BlockSpec((MM_TILE, D_SHARD), lambda j, k: (k, j)), dmo, tok,
                        (D_SHARD, D_MODEL))
    du0, du1, du2, dg0, dg1, dg2 = merge_bwd(dmo, w[2], o_cat, w[1], proj)
    du = (du0, du1, du2)
    do_cat = branch_bwd_input(du, w[1])
    g_br = branch_grad_weights(o_cat, du)

    qkv, total_sb = sb
    dq_sb, dk_sb, dv_sb = sb_backward(qkv, total_sb, do_cat)

    q_dil, k_dil, v_dil, o_dd, lse_dd, o_dt, lse_dt = dil
    do_dt, dlse_dt = dilated_merge_bwd(o_dt, lse_dt, _heads(do_cat[:, 256:384]))
    dq_dil, dk_dil, dv_dil, dbias_dil, _ = banded_backward(
        "dilated_backward", q_dil, k_dil, v_dil, bias_dil, _no_sinks(), o_dd, lse_dd,
        _redilate(do_dt).reshape(q_dil.shape), _redilate(dlse_dt).reshape(lse_dd.shape), DIL_HEADS_PER_STEP, _dil_prev_mask)

    q_swa, k_swa, v_swa, o_swa, lse_swa = swa
    bias_swa = bias_swa.reshape(H_SWA_KV, SWA_GROUP * BLK, 2 * BLK)
    dq_swa, dk_swa, dv_swa, dbias_swa, dsinks = banded_backward(
        "swa_backward", q_swa, k_swa, v_swa, bias_swa, _sink_rows(sinks), o_swa, lse_swa, _swa_rows(do_cat[:, 384:768]),
        jnp.zeros_like(lse_swa), 1, _swa_prev_mask)
    dbias_swa = dbias_swa.reshape(H_SWA_Q, BLK, 2 * BLK)

    def tokens(t):
        return _undilate(t).transpose(2, 0, 1, 3).reshape(SEQ, -1)

    dproj = jnp.concatenate(
        [dq_sb, dk_sb, dv_sb, tokens(dq_dil.reshape(H_DIL, SEQ, HEAD_DIM)), tokens(dk_dil),
         tokens(dv_dil), _swa_tokens(dq_swa), _unheads(dk_swa), _unheads(dv_swa)], axis=1).astype(BF16)
    dproj = jnp.concatenate([dproj, dg0, dg1, dg2], axis=1)
    g_in = grad_weight("grad_w_in", h, tok, dproj, pl.BlockSpec((MM_TILE, IN_SHARD), lambda j, k: (k, j)),
                       (D_MODEL, IN_SHARD))
    dx, stats = mixer_bwd_input(dproj, w[0], x, dxo, _vec(gain), _vec(mod[1]))
    dmod = jnp.stack([stats[0], stats[1], dgate[0]])
    dbias = jnp.concatenate([dbias_dil, dbias_swa], axis=0)
    return dx, dmod, stats[2], dbias, dsinks[:, :, 0].reshape(H_SWA_Q), (g_in, g_br, g_out)


N_UNITS = 3 * DEPTH


def device_step(x, target, mod, gains, final_gain, sinks, rel_bias, get_weights, put_grads):
    bias_dil, bias_swa, bucket = band_tables(rel_bias)
    saved, weights = [], []
    for u in range(N_UNITS):
        l, j = divmod(u, 3)
        w = get_weights(u, x)
        if j == 1:
            x, s = mixer_forward(x, gains[l, 1], mod[l, 1], sinks[l], bias_dil, bias_swa, w)
        else:
            x, s = ffn_forward(x, gains[l, j], mod[l, j], w)
        saved.append(s)
        weights.append(w)
    loss, dx, dfinal = final_loss(x, _vec(final_gain), target)

    dmod = [[None] * 3 for _ in range(DEPTH)]
    dgains = [[None] * 3 for _ in range(DEPTH)]
    dbias, dsinks = [None] * DEPTH, [None] * DEPTH
    zero = jnp.zeros((1, 1), F32)
    for u in reversed(range(N_UNITS)):
        l, j = divmod(u, 3)
        gain = gains[l, j] + zero[0]
        if j == 1:
            dx, dmod[l][j], dgains[l][j], dbias[l], dsinks[l], grads = mixer_backward(
                dx, saved[u], gain, mod[l, 1], sinks[l], bias_dil, bias_swa, weights[u])
        else:
            dx, dmod[l][j], dgains[l][j], grads = ffn_backward(dx, saved[u], gain, mod[l, j], weights[u])
        if u > 0:
            zero = put_grads(u, grads)
    drel = rel_bias_reduce(dbias[0], dbias[1], bucket)[:, 0, :N_BUCKETS].T
    stack2 = lambda t: jnp.stack([jnp.stack(r) for r in t])
    return loss, dx, stack2(dmod), stack2(dgains), dfinal[0], jnp.stack(dsinks), drel, grads


MESH = pl.DeviceIdType.MESH
CHIP_FLIPS = ((1, 0), (0, 1), (1, 1))
ANY = pl.BlockSpec(memory_space=pl.ANY)


def _position():
    return lax.axis_index("x"), lax.axis_index("y"), lax.axis_index("c")


def all_gather_small(name, piece):
    def body(x_ref, out_ref, send_sems, recv_sems, local_sem):
        x, y, c = _position()
        me, sibling = (x, y, c), (x, y, 1 - c)
        chips = [(x ^ fx, y ^ fy) for fx, fy in CHIP_FLIPS]

        def rows(px, py, pc):
            return out_ref.at[4 * px + 2 * py + pc]

        def copy(k, block, to, src=None):
            return pltpu.make_async_remote_copy(
                src_ref=rows(*block) if src is None else src, dst_ref=rows(*block),
                send_sem=send_sems.at[k], recv_sem=recv_sems.at[k], device_id=to, device_id_type=MESH)

        mine = pltpu.make_async_copy(x_ref, rows(*me), local_sem)
        mine.start()
        first = [copy(0, me, sibling, src=x_ref)]
        first += [copy(1 + j, me, (*chip, c), src=x_ref) for j, chip in enumerate(chips)]
        for cp in first:
            cp.start()
        passed = [copy(4 + j, (*chip, c), sibling) for j, chip in enumerate(chips)]
        for j, chip in enumerate(chips):
            copy(1 + j, (*chip, c), me).wait_recv()
            passed[j].start()
        copy(0, sibling, me).wait_recv()
        for j, chip in enumerate(chips):
            copy(4 + j, (*chip, 1 - c), me).wait_recv()
        for cp in first + passed:
            cp.wait_send()
        mine.wait()

    return pl.pallas_call(
        body, out_shape=jax.ShapeDtypeStruct((N_DEV,) + piece.shape, piece.dtype),
        in_specs=[pl.BlockSpec(memory_space=pltpu.VMEM)], out_specs=pl.BlockSpec(memory_space=pltpu.VMEM),
        scratch_shapes=[pltpu.SemaphoreType.DMA((7,)), pltpu.SemaphoreType.DMA((7,)), pltpu.SemaphoreType.DMA],
        name=name,
    )(piece)


def exchange(name, operands, out_shapes, aliases, plan):
    n_in, n_out = len(operands), len(out_shapes)

    def body(*refs):
        ins, outs = refs[:n_in], refs[n_in:n_in + n_out]
        send_sems, recv_sems, local_sems = refs[n_in + n_out:]
        x, y, c = _position()
        local, sends, recvs = plan(ins, outs, x, y, c)
        local = [pltpu.make_async_copy(s, d, local_sems.at[k]) for k, (s, d) in enumerate(local)]
        for cp in local:
            cp.start()
        remote = [pltpu.make_async_remote_copy(src_ref=s, dst_ref=d, send_sem=send_sems.at[k], recv_sem=recv_sems.at[k],
                                               device_id=dev, device_id_type=MESH)
                  for k, (s, d, dev) in enumerate(sends)]
        for cp in remote:
            cp.start()
        for k, r in enumerate(recvs):
            pltpu.make_async_remote_copy(src_ref=r, dst_ref=r, send_sem=send_sems.at[k], recv_sem=recv_sems.at[k],
                                         device_id=(x, y, c), device_id_type=MESH).wait_recv()
        for cp in remote:
            cp.wait_send()
        for cp in local:
            cp.wait()

    n_sends, n_local = plan.n_sends, max(plan.n_local, 1)
    return pl.pallas_call(
        body, out_shape=out_shapes, in_specs=[ANY] * n_in, out_specs=[ANY] * n_out,
        scratch_shapes=[pltpu.SemaphoreType.DMA((n_sends,)), pltpu.SemaphoreType.DMA((n_sends,)),
                        pltpu.SemaphoreType.DMA((n_local,))],
        input_output_aliases=aliases, name=name,
    )(*operands)


def _plan(n_local, n_sends):
    def wrap(fn):
        fn.n_local, fn.n_sends = n_local, n_sends
        return fn
    return wrap


def _half(ref, axis, c):
    rows = ref.shape[axis] // 2
    idx = [slice(None)] * len(ref.shape)
    idx[axis] = pl.ds(pl.multiple_of(c * rows, 16), rows)
    return ref.at[tuple(idx)]


HBM = pl.BlockSpec(memory_space=pltpu.HBM)
SEM = pl.BlockSpec(memory_space=pltpu.SEMAPHORE)
EFFECT = pltpu.SideEffectType.DATAFLOW_SIDE_EFFECTING


def split_start(name, bufs, extra, n_copies, describe):
    n = len(bufs)

    def body(*refs):
        send_sems, recv_sems = refs[n + len(extra)], refs[n + len(extra) + 1]
        x, y, c = _position()
        for k, (src, dst, _, peer) in enumerate(describe(refs[:n], x, y, c)):
            pltpu.make_async_remote_copy(src_ref=src, dst_ref=dst, send_sem=send_sems.at[k], recv_sem=recv_sems.at[k],
                                         device_id=peer, device_id_type=MESH).start()
        token = refs[-1]
        token[...] = jnp.zeros_like(token)

    out = pl.pallas_call(
        body, name=name,
        out_shape=(pltpu.SemaphoreType.DMA((n_copies,)), pltpu.SemaphoreType.DMA((n_copies,)),
                   *[pltpu.HBM(b.shape, b.dtype) for b in bufs], jax.ShapeDtypeStruct((8, 128), F32)),
        in_specs=[HBM] * n + [ANY] * len(extra),
        out_specs=(SEM, SEM, *[HBM] * n, pl.BlockSpec(memory_space=pltpu.VMEM)),
        input_output_aliases={k: 2 + k for k in range(n)},
        compiler_params=pltpu.CompilerParams(has_side_effects=EFFECT),
    )(*[pltpu.with_memory_space_constraint(b, pltpu.HBM) for b in bufs], *extra)
    return out[0], out[1], list(out[2:2 + n]), out[-1]


def split_wait(name, bufs, send_sems, recv_sems, after, describe):
    n = len(bufs)

    def body(*refs):
        send, recv = refs[n], refs[n + 1]
        x, y, c = _position()
        for k, (src, _, dst, peer) in enumerate(describe(refs[:n], x, y, c)):
            copy = pltpu.make_async_remote_copy(src_ref=src, dst_ref=dst, send_sem=send.at[k], recv_sem=recv.at[k],
                                                device_id=peer, device_id_type=MESH)
            copy.wait_send()
            copy.wait_recv()

    out = pl.pallas_call(
        body, name=name, out_shape=[pltpu.HBM(b.shape, b.dtype) for b in bufs],
        in_specs=[HBM] * n + [SEM, SEM] + [ANY] * len(after), out_specs=[HBM] * n,
        input_output_aliases={k: k for k in range(n)},
        compiler_params=pltpu.CompilerParams(has_side_effects=EFFECT),
    )(*bufs, send_sems, recv_sems, *after)
    return list(out)


def _row_tile(rows, cols):
    best = 16
    for t in range(16, rows + 1, 16):
        if rows % t == 0 and t * cols <= 256 * 1024:
            best = t
    return best


def cast_into_slot(name, param, index, chip):
    rows, cols = param.shape[-2:]
    tr = _row_tile(rows, cols)
    lead = (None,) * len(index)

    def body(chip_ref, s_ref, o_ref):
        del chip_ref
        o_ref[...] = s_ref[...].astype(BF16)

    return pl.pallas_call(
        body, out_shape=jax.ShapeDtypeStruct((N_CHIPS, rows, cols), BF16),
        grid_spec=pltpu.PrefetchScalarGridSpec(
            num_scalar_prefetch=1, grid=(rows // tr,),
            in_specs=[pl.BlockSpec(lead + (tr, cols), lambda r, chip_ref: index + (r, 0))],
            out_specs=pl.BlockSpec((None, tr, cols), lambda r, chip_ref: (chip_ref[0], r, 0))),
        name=name, compiler_params=_params(("parallel",)),
    )(chip, param)


GATHER_STAGES = ((0,), (1,), (2,), (3, 4, 5))
REDUCE_STAGES = ((5, 4, 3), (2, 1), (0,))


def _gather_copies(slots, x, y, c):
    me = 2 * x + y
    out = []
    for s in slots:
        for fx, fy in CHIP_FLIPS:
            mine = _half(s.at[me], 0, c)
            out.append((mine, mine, _half(s.at[2 * (x ^ fx) + (y ^ fy)], 0, c), (x ^ fx, y ^ fy, c)))
    return out


class WeightStream:
    def __init__(self, shards, chip, after=()):
        self.pending, self.ready = {}, {}
        token = tuple(after)
        for si, units in enumerate(GATHER_STAGES):
            slots = [cast_into_slot(f"cast_{u}_{t}", p, idx, chip) for u in units for t, (p, idx) in enumerate(shards[u])]
            send, recv, slots, tok = split_start(f"gather_start_{si}", slots, token, 3 * len(slots), _gather_copies)
            self.pending[si] = (send, recv, slots)
            token = (tok,)
        self.token = token

    def get(self, u, after):
        if u not in self.ready:
            si = next(k for k, units in enumerate(GATHER_STAGES) if u in units)
            send, recv, slots = self.pending.pop(si)
            slots = split_wait(f"gather_wait_{si}", slots, send, recv, (after,) + self.token, _gather_copies)
            self.token = ()

            @_plan(0, 3 * len(slots))
            def to_sibling(ins, outs, x, y, c):
                sends, recvs = [], []
                for o in outs:
                    for fx, fy in CHIP_FLIPS:
                        slab = o.at[2 * (x ^ fx) + (y ^ fy)]
                        sends.append((_half(slab, 0, c), _half(slab, 0, c), (x, y, 1 - c)))
                        recvs.append(_half(slab, 0, 1 - c))
                return [], sends, recvs

            shapes = [jax.ShapeDtypeStruct(s.shape, BF16) for s in slots]
            slots = exchange(f"gather_sibling_{si}", slots, shapes, {k: k for k in range(len(slots))}, to_sibling)
            for i, v in enumerate(GATHER_STAGES[si]):
                self.ready[v] = tuple(slots[3 * i:3 * i + 3])
        return self.ready[u]


def _reduce_copies(bufs, x, y, c):
    n = len(bufs) // 2
    out = []
    for s, land in zip(bufs[:n], bufs[n:]):
        for k, (fx, fy) in enumerate(CHIP_FLIPS):
            out.append((s.at[2 * (x ^ fx) + (y ^ fy)], land.at[k], land.at[k], (x ^ fx, y ^ fy, c)))
    return out


GRAD_SLOTS = {"gate": (2 * DEPTH, FF_SHARD, D_MODEL), "up": (2 * DEPTH, FF_SHARD, D_MODEL),
              "down": (2 * DEPTH, FF_SHARD, D_MODEL), "in": (DEPTH, D_MODEL, IN_SHARD),
              "br": (DEPTH, BR_ROWS, D_SHARD), "out": (DEPTH, D_SHARD, D_MODEL)}


def _unit_tensors(u):
    l, j = divmod(u, 3)
    if j == 1:
        return [("in", l), ("br", l), ("out", l)]
    return [(k, 2 * l + j // 2) for k in ("gate", "up", "down")]


class GradStream:
    def __init__(self, chip, core):
        self.core = core
        self.place = jnp.concatenate([chip, core])
        self.held, self.flying = {}, []
        self.full = {k: lax.empty(shape, F32) for k, shape in GRAD_SLOTS.items()}

    def put(self, u, grads, after=()):
        self.held[u] = grads
        si = len(self.flying)
        units = REDUCE_STAGES[si]
        if not all(v in self.held for v in units):
            return jnp.zeros((1, 1), F32)
        gs = [g for v in units for g in self.held[v]]

        @_plan(0, len(gs))
        def swap_halves(ins, outs, x, y, c):
            sends = [(_half(g, 1, 1 - c), o, (x, y, 1 - c)) for g, o in zip(ins, outs)]
            return [], sends, list(outs)

        half_shapes = [jax.ShapeDtypeStruct((N_CHIPS, g.shape[1] // 2, g.shape[2]), BF16) for g in gs]
        landed = exchange(f"reduce_swap_{si}", gs + list(after), half_shapes, {}, swap_halves)
        sums = [_add_halves(g, la, self.core) for g, la in zip(gs, landed)]
        landing = [lax.empty((3,) + s.shape[1:], BF16) for s in sums]
        send, recv, bufs, token = split_start(f"reduce_start_{si}", sums + landing, (), 3 * len(sums), _reduce_copies)
        self.flying.append((send, recv, bufs, [t for v in units for t in _unit_tensors(v)]))
        return token[0:1, 0:1]

    def finish(self, after):
        for si, (send, recv, bufs, tensors) in enumerate(self.flying):
            bufs = split_wait(f"reduce_wait_{si}", bufs, send, recv, tuple(after), _reduce_copies)
            n = len(tensors)
            for (name, slot), s, land in zip(tensors, bufs[:n], bufs[n:]):
                self.full[name] = _add_chips(s, land, self.place, self.full[name], slot)
        names = list(self.full)

        @_plan(0, len(names))
        def share_halves(ins, outs, x, y, c):
            sends = [(_half(o, 1, c), _half(o, 1, c), (x, y, 1 - c)) for o in outs]
            return [], sends, [_half(o, 1, 1 - c) for o in outs]

        shapes = [jax.ShapeDtypeStruct(self.full[k].shape, F32) for k in names]
        out = exchange("reduce_share_halves", [self.full[k] for k in names], shapes, {k: k for k in range(len(names))},
                       share_halves)
        return dict(zip(names, out))


def _add_halves(g, landed, core):
    _, rh, cols = landed.shape
    tr = _row_tile(rh, cols)
    per_half = rh // tr

    def body(core_ref, g_ref, la_ref, o_ref):
        del core_ref
        o_ref[...] = (g_ref[...].astype(F32) + la_ref[...].astype(F32)).astype(BF16)

    blk = (None, tr, cols)
    return pl.pallas_call(
        body, out_shape=jax.ShapeDtypeStruct(landed.shape, BF16),
        grid_spec=pltpu.PrefetchScalarGridSpec(
            num_scalar_prefetch=1, grid=(N_CHIPS, per_half),
            in_specs=[pl.BlockSpec(blk, lambda j, r, core_ref: (j, core_ref[0] * per_half + r, 0)),
                      pl.BlockSpec(blk, lambda j, r, core_ref: (j, r, 0))],
            out_specs=pl.BlockSpec(blk, lambda j, r, core_ref: (j, r, 0))),
        name="reduce_add_halves", compiler_params=_params(("parallel", "parallel")),
    )(core, g, landed)


def _add_chips(sums, landed, place, full, slot):
    _, rh, cols = sums.shape
    tr = _row_tile(rh, cols)
    per_half = rh // tr

    def body(place_ref, s_ref, la_ref, full_in, o_ref):
        del place_ref, full_in
        o_ref[...] = ((s_ref[...].astype(F32) + la_ref[0].astype(F32)) + la_ref[1].astype(F32)) + la_ref[2].astype(F32)

    return pl.pallas_call(
        body, out_shape=jax.ShapeDtypeStruct(full.shape, F32),
        grid_spec=pltpu.PrefetchScalarGridSpec(
            num_scalar_prefetch=1, grid=(per_half,),
            in_specs=[pl.BlockSpec((None, tr, cols), lambda r, place_ref: (place_ref[0], r, 0)),
                      pl.BlockSpec((3, tr, cols), lambda r, place_ref: (0, r, 0)), ANY],
            out_specs=pl.BlockSpec((None, tr, cols), lambda r, place_ref: (slot, place_ref[1] * per_half + r, 0))),
        input_output_aliases={3: 0}, name="reduce_add_chips", compiler_params=_params(("parallel",)),
    )(place, sums, landed, full)


def sum_devices(parts):
    def body(p_ref, o_ref):
        acc = p_ref[0]
        for d in range(1, N_DEV):
            acc = acc + p_ref[d]
        o_ref[...] = acc

    return pl.pallas_call(body, out_shape=jax.ShapeDtypeStruct(parts.shape[1:], F32), name="sum_devices")(parts)


ADA_SHARD = 9 * D_MODEL // N_CHIPS
ADA_TILE = 768
ADA_ROWS = 16


def ada_forward(c_rows, w_ada, b_shard):
    def body(c_ref, w_ref, b_ref, o_ref):
        cv = c_ref[...]
        o_ref[...] = _dot((cv * _sigmoid(cv)).astype(BF16), w_ref[...].astype(BF16), NN) + b_ref[...]

    return pl.pallas_call(
        body, grid=(DEPTH, ADA_SHARD // ADA_TILE),
        in_specs=[pl.BlockSpec((ADA_ROWS, D_MODEL), lambda l, n: (0, 0)),
                  pl.BlockSpec((None, D_MODEL, ADA_TILE), lambda l, n: (l, 0, n)),
                  pl.BlockSpec((None, 1, ADA_TILE), lambda l, n: (l, 0, n))],
        out_specs=pl.BlockSpec((None, ADA_ROWS, ADA_TILE), lambda l, n: (l, 0, n)),
        out_shape=jax.ShapeDtypeStruct((DEPTH, ADA_ROWS, ADA_SHARD), F32),
        name="ada_forward", compiler_params=_params(("parallel", "parallel")),
    )(c_rows, w_ada, b_shard)


def ada_backward(c_rows, dmod_rows):
    def body(c_ref, d_ref, o_ref):
        cv = c_ref[...]
        o_ref[...] = _dot((cv * _sigmoid(cv)).astype(BF16), d_ref[...].astype(BF16), TN)

    return pl.pallas_call(
        body, grid=(DEPTH, ADA_SHARD // ADA_TILE),
        in_specs=[pl.BlockSpec((ADA_ROWS, D_MODEL), lambda l, n: (0, 0)),
                  pl.BlockSpec((None, ADA_ROWS, ADA_TILE), lambda l, n: (l, 0, n))],
        out_specs=pl.BlockSpec((None, D_MODEL, ADA_TILE), lambda l, n: (l, 0, n)),
        out_shape=jax.ShapeDtypeStruct((DEPTH, D_MODEL, ADA_SHARD), F32),
        name="ada_backward", compiler_params=_params(("parallel", "parallel")),
    )(c_rows, dmod_rows)


def adamw(name, w, g, m, v):
    shape = w.shape
    cols = shape[-1]
    rows = w.size // cols
    tr = _row_tile(rows, cols) if rows % 16 == 0 else rows
    c1 = 1.0 / (1.0 - ADAM_B1 ** ADAM_STEP)
    c2 = 1.0 / (1.0 - ADAM_B2 ** ADAM_STEP)

    def body(w_ref, g_ref, m_ref, v_ref, go_ref, d_ref, mo_ref, vo_ref):
        gv = g_ref[...]
        mn = ADAM_B1 * m_ref[...] + (1.0 - ADAM_B1) * gv
        vn = ADAM_B2 * v_ref[...] + (1.0 - ADAM_B2) * (gv * gv)
        go_ref[...] = gv
        mo_ref[...] = mn
        vo_ref[...] = vn
        d_ref[...] = -ADAM_LR * ((mn * c1) / (jnp.sqrt(vn * c2) + ADAM_EPS) + ADAM_WD * w_ref[...])

    spec = pl.BlockSpec((tr, cols), lambda i: (i, 0))
    out = jax.ShapeDtypeStruct((rows, cols), F32)
    res = pl.pallas_call(
        body, grid=(rows // tr,), in_specs=[spec] * 4, out_specs=[spec] * 4, out_shape=[out] * 4,
        name=name, compiler_params=_params(("parallel",)),
    )(*[t.reshape(rows, cols) for t in (w, g, m, v)])
    return tuple(r.reshape(shape) for r in res)


def _pack(parts, rows):
    flat = jnp.concatenate([p.reshape(-1) for p in parts])
    return jnp.pad(flat, (0, rows * 128 - flat.size)).reshape(rows, 128)


def _unpack(flat, shapes):
    out, at = [], 0
    for s in shapes:
        n = math.prod(s)
        out.append(flat[at:at + n].reshape(s))
        at += n
    return out


def kernel(x, c, w_ada, b_ada, norm_gain, w_ffn_gate, w_ffn_up, w_ffn_down, w_in, w_br_sb, w_br_dil, w_br_swa, w_out, sinks, rel_bias, final_gain, loss_target, m_w_ada, m_b_ada, m_norm_gain, m_w_ffn_gate, m_w_ffn_up, m_w_ffn_down, m_w_in, m_w_br_sb, m_w_br_dil, m_w_br_swa, m_w_out, m_sinks, m_rel_bias, m_final_gain, v_w_ada, v_b_ada, v_norm_gain, v_w_ffn_gate, v_w_ffn_up, v_w_ffn_down, v_w_in, v_w_br_sb, v_w_br_dil, v_w_br_swa, v_w_out, v_sinks, v_rel_bias, v_final_gain):
    xi, yi, ci = _position()
    chip = 2 * xi + yi
    dev = 2 * chip + ci

    c_all = all_gather_small("gather_c", c.reshape(8, 128)).reshape(N_DEV, D_MODEL)
    c_rows = jnp.pad(c_all, ((0, ADA_ROWS - N_DEV), (0, 0)))
    b_shard = lax.dynamic_slice_in_dim(b_ada, chip * ADA_SHARD, ADA_SHARD, axis=1).reshape(DEPTH, 1, ADA_SHARD)
    mod_shard = ada_forward(c_rows, w_ada, b_shard)[:, :N_DEV]
    n_mod = DEPTH * N_DEV * ADA_SHARD
    gathered = all_gather_small("gather_mod", _pack([mod_shard, norm_gain], 304))[::2].reshape(N_CHIPS, -1)
    mod_all = gathered[:, :n_mod].reshape(N_CHIPS, DEPTH, N_DEV, ADA_SHARD)
    mod = lax.dynamic_index_in_dim(mod_all, dev, axis=2, keepdims=False)
    mod = mod.transpose(1, 0, 2).reshape(DEPTH, 3, 3, D_MODEL)
    gains = gathered[:, n_mod:n_mod + DEPTH * 3 * D_SHARD].reshape(N_CHIPS, DEPTH, 3, D_SHARD)
    gains = gains.transpose(1, 2, 0, 3).reshape(DEPTH, 3, D_MODEL)

    chip_i, core_i = chip.astype(jnp.int32).reshape(1), ci.astype(jnp.int32).reshape(1)
    w_br = jnp.concatenate([w_br_sb, w_br_dil, w_br_swa], axis=1)
    transposed = (3, 4)
    w_gate_t, w_up_t = jnp.swapaxes(w_ffn_gate, 2, 3), jnp.swapaxes(w_ffn_up, 2, 3)
    shards = []
    for l in range(DEPTH):
        ffn = [[(w_gate_t, (l, f)), (w_up_t, (l, f)), (w_ffn_down, (l, f))] for f in range(2)]
        shards += [ffn[0], [(w_in, (l,)), (w_br, (l,)), (w_out, (l,))], ffn[1]]
    weights_in = WeightStream(shards, chip_i, (gathered,))
    grads_out = GradStream(chip_i, core_i)

    loss, dx, dmod, dgains, dfinal, dsinks, drel, last_grads = device_step(
        x[0], loss_target[0], mod, gains, final_gain, sinks, rel_bias, weights_in.get, grads_out.put)

    small_shapes = [(DEPTH, 9 * D_MODEL), (DEPTH, 3, D_MODEL), (D_MODEL,), (DEPTH, H_SWA_Q), (N_BUCKETS, 12), (1,)]
    small_all = all_gather_small("gather_small_grads", _pack([dmod, dgains, dfinal, dsinks, drel, loss[0, 0:1]], 208))
    started = grads_out.put(0, last_grads, after=(small_all,))
    small_all = small_all + started
    g_b_ada, g_gain_full, g_final, g_sinks, g_rel, loss_sum = _unpack(sum_devices(small_all).reshape(-1), small_shapes)
    g_gain = lax.dynamic_slice_in_dim(g_gain_full, chip * D_SHARD, D_SHARD, axis=2)
    dmod_all = small_all.reshape(N_DEV, -1)[:, :DEPTH * 9 * D_MODEL].reshape(N_DEV, DEPTH, 9 * D_MODEL)
    dmod_rows = lax.dynamic_slice_in_dim(dmod_all, chip * ADA_SHARD, ADA_SHARD, axis=2).transpose(1, 0, 2)
    g_w_ada = ada_backward(c_rows, jnp.pad(dmod_rows, ((0, 0), (0, ADA_ROWS - N_DEV), (0, 0))))

    weights = [w_ada, b_ada, norm_gain, w_ffn_gate, w_ffn_up, w_ffn_down, w_in, w_br_sb, w_br_dil, w_br_swa, w_out,
               sinks, rel_bias, final_gain]
    ms = [m_w_ada, m_b_ada, m_norm_gain, m_w_ffn_gate, m_w_ffn_up, m_w_ffn_down, m_w_in, m_w_br_sb, m_w_br_dil,
          m_w_br_swa, m_w_out, m_sinks, m_rel_bias, m_final_gain]
    vs = [v_w_ada, v_b_ada, v_norm_gain, v_w_ffn_gate, v_w_ffn_up, v_w_ffn_down, v_w_in, v_w_br_sb, v_w_br_dil,
          v_w_br_swa, v_w_out, v_sinks, v_rel_bias, v_final_gain]
    grads = [g_w_ada, g_b_ada, g_gain] + [None] * 8 + [g_sinks, g_rel, g_final]

    small = (1, 2, 11, 12, 13)
    deltas, new_ms, new_vs = [None] * 14, [None] * 14, [None] * 14
    _, deltas[0], new_ms[0], new_vs[0] = adamw("adamw_0", weights[0], grads[0], ms[0], vs[0])
    shapes = [weights[k].shape for k in small]
    packed = [_pack([t[k] for k in small], 168) for t in (weights, grads, ms, vs)]
    for dst, res in zip((deltas, new_ms, new_vs), adamw("adamw_small", *packed)[1:]):
        for k, t in zip(small, _unpack(res.reshape(-1), shapes)):
            dst[k] = t

    g = grads_out.finish((dx, deltas[0], deltas[1]))
    g_br = g["br"]
    grads[3:11] = [g["gate"].reshape(w_gate_t.shape), g["up"].reshape(w_up_t.shape),
                   g["down"].reshape(w_ffn_down.shape), g["in"], g_br[:, 0:256], g_br[:, 256:384], g_br[:, 384:768],
                   g["out"]]
    for k in range(3, 11):
        state = [weights[k], ms[k], vs[k]]
        if k in transposed:
            state = [jnp.swapaxes(t, 2, 3) for t in state]
        out = adamw(f"adamw_{k}", state[0], grads[k], state[1], state[2])
        if k in transposed:
            out = [jnp.swapaxes(t, 2, 3) for t in out]
        grads[k], deltas[k], new_ms[k], new_vs[k] = out
    return (loss_sum[0], dx[None], *grads, *deltas, *new_ms, *new_vs)
```

```python
import functools
import math

import jax
import jax.numpy as jnp
from jax import lax
from jax.experimental import pallas as pl
from jax.experimental.pallas import tpu as pltpu

F32 = jnp.float32
BF16 = jnp.bfloat16

D_MODEL = 1024
SEQ = 2048
DEPTH = 2
HEAD_DIM = 64
BLK = 128
H_SB = 4
DIL_PATTERNS = ((128, 1), (512, 4), (2048, 16))
H_PER_DIL = 2
H_DIL = 6
H_SWA_Q = 6
H_SWA_KV = 2
SWA_WINDOW = 128
N_BUCKETS = 32
MAX_REL_DIST = 2048
D_FF = 2816
RMS_EPS = 1e-6
N_CHIPS = 4
N_DEV = 8
FF_SHARD = D_FF // N_CHIPS
D_QKV = 2560
D_IN = D_QKV + 3 * D_MODEL
IN_SHARD = D_IN // N_CHIPS
D_SHARD = D_MODEL // N_CHIPS
BR_ROWS = 768
NEG = -1e30
QK_SCALE = HEAD_DIM ** -0.5

ADAM_LR = 0.001
ADAM_B1 = 0.9
ADAM_B2 = 0.999
ADAM_EPS = 1e-08
ADAM_WD = 0.01
ADAM_STEP = 10

VMEM_LIMIT = 48 * 1024 * 1024
ROW_TILE = 256
MM_TILE = 512

NN = (((1,), (0,)), ((), ()))
NT = (((1,), (1,)), ((), ()))
TN = (((0,), (0,)), ((), ()))


def _params(sem=None):
    return pltpu.CompilerParams(dimension_semantics=sem, vmem_limit_bytes=VMEM_LIMIT)


def _dot(a, b, dims):
    return lax.dot_general(a, b, dims, preferred_element_type=F32)


def _sigmoid(x):
    return 1.0 / (1.0 + jnp.exp(-x))


def _matmul(name, grid, nk, k_axis, dims, n_pairs, in_specs, out_specs, out_shape, acc_shape, epilogue,
            operands, sem, aliases=None, prologue=None):
    n_in = len(in_specs)
    n_out = len(out_specs)

    def partial(ins):
        tot = None
        for p in range(n_pairs):
            a = ins[2 * p][...]
            if prologue is not None:
                a = prologue(p, a, ins)
            d = _dot(a, ins[2 * p + 1][...], dims)
            tot = d if tot is None else tot + d
        return tot

    def body(*refs):
        ins, outs = refs[:n_in], refs[n_in:n_in + n_out]
        ids = tuple(pl.program_id(a) for a in range(len(grid)))
        if nk == 1:
            epilogue(partial(ins), ins, outs, ids)
            return
        acc = refs[n_in + n_out]
        k = ids[k_axis]

        @pl.when(k == 0)
        def _():
            acc[...] = partial(ins)

        @pl.when(k > 0)
        def _():
            acc[...] += partial(ins)

        @pl.when(k == nk - 1)
        def _():
            epilogue(acc[...], ins, outs, ids)

    return pl.pallas_call(
        body, grid=grid, in_specs=in_specs, out_specs=out_specs, out_shape=out_shape,
        scratch_shapes=[] if nk == 1 else [pltpu.VMEM(acc_shape, F32)],
        input_output_aliases=aliases or {}, name=name, compiler_params=_params(sem),
    )(*operands)


def _row_spec(width=D_MODEL):
    return pl.BlockSpec((ROW_TILE, width), lambda i: (i, 0))


def _vec_spec(rows=1, width=D_MODEL):
    return pl.BlockSpec((rows, width), lambda i: (0, 0))


def prenorm(x, gain, scale, shift):
    def body(x_ref, g_ref, sc_ref, sh_ref, h_ref):
        xv = x_ref[...]
        r = lax.rsqrt(jnp.mean(xv * xv, axis=-1, keepdims=True) + RMS_EPS)
        h_ref[...] = (((xv * r) * g_ref[...]) * (1.0 + sc_ref[...]) + sh_ref[...]).astype(BF16)

    return pl.pallas_call(
        body, grid=(SEQ // ROW_TILE,), in_specs=[_row_spec(), _vec_spec(), _vec_spec(), _vec_spec()],
        out_specs=_row_spec(), out_shape=jax.ShapeDtypeStruct((SEQ, D_MODEL), BF16),
        name="prenorm", compiler_params=_params(("parallel",)),
    )(x, gain, scale, shift)


def resid_bwd(dxo, f, coef, mult):
    def body(dx_ref, f_ref, c_ref, df_ref, dc_ref):
        dx = dx_ref[...]
        df_ref[...] = (dx * (mult * c_ref[...])).astype(BF16)
        part = mult * jnp.sum(dx * f_ref[...], axis=0, keepdims=True)

        @pl.when(pl.program_id(0) == 0)
        def _():
            dc_ref[...] = jnp.zeros_like(dc_ref)

        dc_ref[0:1, :] += part

    return pl.pallas_call(
        body, grid=(SEQ // ROW_TILE,), in_specs=[_row_spec(), _row_spec(), _vec_spec()],
        out_specs=[_row_spec(), _vec_spec(8)],
        out_shape=[jax.ShapeDtypeStruct((SEQ, D_MODEL), BF16), jax.ShapeDtypeStruct((8, D_MODEL), F32)],
        name="resid_bwd", compiler_params=_params(("arbitrary",)),
    )(dxo, f, coef)


def final_loss(x, gain, target):
    def body(x_ref, g_ref, t_ref, loss_ref, dx_ref, dg_ref):
        xv = x_ref[...]
        g = g_ref[...]
        r = lax.rsqrt(jnp.mean(xv * xv, axis=-1, keepdims=True) + RMS_EPS)
        xh = xv * r
        e = xh * g - t_ref[...]
        part = 0.5 * jnp.sum(jnp.mean(e * e, axis=-1, keepdims=True), axis=0, keepdims=True)
        dy = e * (1.0 / D_MODEL)
        dyg = dy * g
        dx_ref[...] = r * (dyg - xh * jnp.mean(dyg * xh, axis=-1, keepdims=True))

        @pl.when(pl.program_id(0) == 0)
        def _():
            loss_ref[...] = jnp.zeros_like(loss_ref)
            dg_ref[...] = jnp.zeros_like(dg_ref)

        loss_ref[...] += jnp.broadcast_to(part, loss_ref.shape)
        dg_ref[0:1, :] += jnp.sum(dy * xh, axis=0, keepdims=True)

    return pl.pallas_call(
        body, grid=(SEQ // ROW_TILE,), in_specs=[_row_spec(), _vec_spec(), _row_spec()],
        out_specs=[_vec_spec(8, 128), _row_spec(), _vec_spec(8)],
        out_shape=[jax.ShapeDtypeStruct((8, 128), F32), jax.ShapeDtypeStruct((SEQ, D_MODEL), F32),
                   jax.ShapeDtypeStruct((8, D_MODEL), F32)],
        name="final_loss", compiler_params=_params(("arbitrary",)),
    )(x, gain, target)


def _prenorm_bwd_epilogue(dh, x_ref, dxo_ref, g_ref, sc_ref, dx_ref, stats_ref, first):
    xv = x_ref[...]
    g = g_ref[...]
    r = lax.rsqrt(jnp.mean(xv * xv, axis=-1, keepdims=True) + RMS_EPS)
    xh = xv * r
    dn = dh * (1.0 + sc_ref[...])
    dxh = dn * g
    dx_ref[...] = dxo_ref[...] + r * (dxh - xh * jnp.mean(dxh * xh, axis=-1, keepdims=True))

    @pl.when(first)
    def _():
        stats_ref[...] = jnp.zeros_like(stats_ref)

    stats_ref[0:1, :] += jnp.sum(dh, axis=0, keepdims=True)
    stats_ref[1:2, :] += jnp.sum(dh * (xh * g), axis=0, keepdims=True)
    stats_ref[2:3, :] += jnp.sum(dn * xh, axis=0, keepdims=True)


def ffn_up(h, wg_all, wu_all):
    def body(h_ref, wg_ref, wu_ref, a_ref, b_ref, s_ref):
        hv = h_ref[...]
        a = _dot(hv, wg_ref[...], NT)
        b = _dot(hv, wu_ref[...], NT)
        a_ref[...] = a
        b_ref[...] = b
        s_ref[...] = (a * _sigmoid(a) * b).astype(BF16)

    w_spec = pl.BlockSpec((None, FF_SHARD, D_MODEL), lambda j, i: (j, 0, 0))
    o_spec = pl.BlockSpec((None, MM_TILE, FF_SHARD), lambda j, i: (j, i, 0))
    hid = (N_CHIPS, SEQ, FF_SHARD)
    return pl.pallas_call(
        body, grid=(N_CHIPS, SEQ // MM_TILE),
        in_specs=[pl.BlockSpec((MM_TILE, D_MODEL), lambda j, i: (i, 0)), w_spec, w_spec],
        out_specs=[o_spec, o_spec, o_spec],
        out_shape=[jax.ShapeDtypeStruct(hid, F32), jax.ShapeDtypeStruct(hid, F32), jax.ShapeDtypeStruct(hid, BF16)],
        name="ffn_up", compiler_params=_params(("parallel", "parallel")),
    )(h, wg_all, wu_all)


def matmul_residual(name, a, a_spec, w_all, w_spec, x, coef, mult):
    def epilogue(acc, ins, outs, ids):
        outs[0][...] = acc
        outs[1][...] = ins[2][...] + (mult * ins[3][...]) * acc

    row = pl.BlockSpec((MM_TILE, D_MODEL), lambda i, j: (i, 0))
    return _matmul(
        name, (SEQ // MM_TILE, N_CHIPS), N_CHIPS, 1, NN, 1,
        [a_spec, w_spec, row, pl.BlockSpec((1, D_MODEL), lambda i, j: (0, 0))], [row, row],
        [jax.ShapeDtypeStruct((SEQ, D_MODEL), F32)] * 2, (MM_TILE, D_MODEL), epilogue,
        (a, w_all, x, coef), ("parallel", "arbitrary"))


def ffn_down(s, wd_all, x, gate):
    return matmul_residual(
        "ffn_down", s, pl.BlockSpec((None, MM_TILE, FF_SHARD), lambda i, j: (j, i, 0)),
        wd_all, pl.BlockSpec((None, FF_SHARD, D_MODEL), lambda i, j: (j, 0, 0)), x, gate, 0.5)


def ffn_bwd_hidden(df, wd_all, a, b):
    def epilogue(ds, ins, outs, ids):
        av, bv = ins[2][...], ins[3][...]
        sig = _sigmoid(av)
        outs[0][...] = (ds * bv * (sig * (1.0 + av * (1.0 - sig)))).astype(BF16)
        outs[1][...] = (ds * (av * sig)).astype(BF16)

    hid_spec = pl.BlockSpec((None, MM_TILE, FF_SHARD), lambda j, i: (j, i, 0))
    hid = jax.ShapeDtypeStruct((N_CHIPS, SEQ, FF_SHARD), BF16)
    return _matmul(
        "ffn_bwd_hidden", (N_CHIPS, SEQ // MM_TILE), 1, None, NT, 1,
        [pl.BlockSpec((MM_TILE, D_MODEL), lambda j, i: (i, 0)),
         pl.BlockSpec((None, FF_SHARD, D_MODEL), lambda j, i: (j, 0, 0)), hid_spec, hid_spec],
        [hid_spec, hid_spec], [hid, hid], None, epilogue, (df, wd_all, a, b), ("parallel", "parallel"))


def grad_weight(name, lhs, lhs_spec, rhs, rhs_spec, shape):
    def epilogue(acc, ins, outs, ids):
        outs[0][...] = acc.astype(BF16)

    return _matmul(
        name, (N_CHIPS, SEQ // MM_TILE), SEQ // MM_TILE, 1, TN, 1,
        [lhs_spec, rhs_spec], [pl.BlockSpec((None,) + shape, lambda j, k: (j, 0, 0))],
        [jax.ShapeDtypeStruct((N_CHIPS,) + shape, BF16)], shape, epilogue, (lhs, rhs), ("parallel", "arbitrary"))[0]


def ffn_grad_weights(h, s, df, da, db):
    tok = pl.BlockSpec((MM_TILE, D_MODEL), lambda j, k: (k, 0))
    hid = pl.BlockSpec((None, MM_TILE, FF_SHARD), lambda j, k: (j, k, 0))
    return (grad_weight("grad_w_gate", da, hid, h, tok, (FF_SHARD, D_MODEL)),
            grad_weight("grad_w_up", db, hid, h, tok, (FF_SHARD, D_MODEL)),
            grad_weight("grad_w_down", s, hid, df, tok, (FF_SHARD, D_MODEL)))


def matmul_prenorm_bwd(name, dims, pairs, pair_specs, x, dxo, gain, scale):
    n = len(pairs)

    def epilogue(dh, ins, outs, ids):
        _prenorm_bwd_epilogue(dh, ins[n], ins[n + 1], ins[n + 2], ins[n + 3], outs[0], outs[1], ids[0] == 0)

    row = pl.BlockSpec((MM_TILE, D_MODEL), lambda i, j: (i, 0))
    vec = pl.BlockSpec((1, D_MODEL), lambda i, j: (0, 0))
    return _matmul(
        name, (SEQ // MM_TILE, N_CHIPS), N_CHIPS, 1, dims, len(pairs) // 2,
        list(pair_specs) + [row, row, vec, vec], [row, pl.BlockSpec((8, D_MODEL), lambda i, j: (0, 0))],
        [jax.ShapeDtypeStruct((SEQ, D_MODEL), F32), jax.ShapeDtypeStruct((8, D_MODEL), F32)],
        (MM_TILE, D_MODEL), epilogue, tuple(pairs) + (x, dxo, gain, scale), ("arbitrary", "arbitrary"))


def ffn_bwd_input(da, db, wg_all, wu_all, x, dxo, gain, scale):
    hid = pl.BlockSpec((None, MM_TILE, FF_SHARD), lambda i, j: (j, i, 0))
    w = pl.BlockSpec((None, FF_SHARD, D_MODEL), lambda i, j: (j, 0, 0))
    return matmul_prenorm_bwd("ffn_bwd_input", NN, (da, wg_all, db, wu_all), (hid, w, hid, w), x, dxo, gain, scale)


def in_proj(h, w_all):
    def epilogue(acc, ins, outs, ids):
        outs[0][...] = acc

    return _matmul(
        "in_proj", (N_CHIPS, SEQ // MM_TILE), 1, None, NN, 1,
        [pl.BlockSpec((MM_TILE, D_MODEL), lambda j, i: (i, 0)),
         pl.BlockSpec((None, D_MODEL, IN_SHARD), lambda j, i: (j, 0, 0))],
        [pl.BlockSpec((MM_TILE, IN_SHARD), lambda j, i: (i, j))], [jax.ShapeDtypeStruct((SEQ, D_IN), F32)],
        None, epilogue, (h, w_all), ("parallel", "parallel"))[0]


_GATE_BLOCK0 = D_QKV // D_SHARD


def _branch_products(o, w_ref):
    ob = o.astype(BF16)
    return (_dot(ob[:, 0:256], w_ref[0:256, :], NN), _dot(ob[:, 256:384], w_ref[256:384, :], NN),
            _dot(ob[:, 384:768], w_ref[384:768, :], NN))


def merge_branches(o_cat, wbr_all, proj):
    def body(o_ref, w_ref, g0_ref, g1_ref, g2_ref, m_ref):
        u = _branch_products(o_ref[...], w_ref)
        m_ref[...] = (_sigmoid(g0_ref[...]) * u[0] + _sigmoid(g1_ref[...]) * u[1]
                      + _sigmoid(g2_ref[...]) * u[2]).astype(BF16)

    def gate_spec(b):
        return pl.BlockSpec((MM_TILE, D_SHARD), lambda i, j: (i, _GATE_BLOCK0 + 4 * b + j))

    return pl.pallas_call(
        body, grid=(SEQ // MM_TILE, N_CHIPS),
        in_specs=[pl.BlockSpec((MM_TILE, BR_ROWS), lambda i, j: (i, 0)),
                  pl.BlockSpec((None, BR_ROWS, D_SHARD), lambda i, j: (j, 0, 0)),
                  gate_spec(0), gate_spec(1), gate_spec(2)],
        out_specs=pl.BlockSpec((MM_TILE, D_SHARD), lambda i, j: (i, j)),
        out_shape=jax.ShapeDtypeStruct((SEQ, D_MODEL), BF16),
        name="merge_branches", compiler_params=_params(("parallel", "parallel")),
    )(o_cat, wbr_all, proj, proj, proj)


def out_proj(merged, wout_all, x, gate):
    return matmul_residual(
        "out_proj", merged, pl.BlockSpec((MM_TILE, D_SHARD), lambda i, j: (i, j)),
        wout_all, pl.BlockSpec((None, D_SHARD, D_MODEL), lambda i, j: (j, 0, 0)), x, gate, 1.0)


def merge_bwd(dmo, wout_all, o_cat, wbr_all, proj):
    def epilogue(dm, ins, outs, ids):
        u = _branch_products(ins[2][...], ins[3])
        for b in range(3):
            sig = _sigmoid(ins[4 + b][...])
            outs[b][...] = (dm * sig).astype(BF16)
            outs[3 + b][...] = (dm * u[b] * (sig * (1.0 - sig))).astype(BF16)

    def gate_spec(b):
        return pl.BlockSpec((MM_TILE, D_SHARD), lambda j, i: (i, _GATE_BLOCK0 + 4 * b + j))

    col = pl.BlockSpec((MM_TILE, D_SHARD), lambda j, i: (i, j))
    du = jax.ShapeDtypeStruct((SEQ, D_MODEL), BF16)
    return _matmul(
        "merge_bwd", (N_CHIPS, SEQ // MM_TILE), 1, None, NT, 1,
        [pl.BlockSpec((MM_TILE, D_MODEL), lambda j, i: (i, 0)),
         pl.BlockSpec((None, D_SHARD, D_MODEL), lambda j, i: (j, 0, 0)),
         pl.BlockSpec((MM_TILE, BR_ROWS), lambda j, i: (i, 0)),
         pl.BlockSpec((None, BR_ROWS, D_SHARD), lambda j, i: (j, 0, 0)),
         gate_spec(0), gate_spec(1), gate_spec(2)],
        [col] * 6, [du] * 6,
        None, epilogue, (dmo, wout_all, o_cat, wbr_all, proj, proj, proj), ("parallel", "parallel"))


def branch_bwd_input(du, wbr_all):
    def body(d0_ref, d1_ref, d2_ref, w_ref, o_ref, acc):
        j = pl.program_id(1)
        parts = (_dot(d0_ref[...], w_ref[0:256, :], NT), _dot(d1_ref[...], w_ref[256:384, :], NT),
                 _dot(d2_ref[...], w_ref[384:768, :], NT))

        @pl.when(j == 0)
        def _():
            acc[:, 0:256], acc[:, 256:384], acc[:, 384:768] = parts

        @pl.when(j > 0)
        def _():
            acc[:, 0:256] += parts[0]
            acc[:, 256:384] += parts[1]
            acc[:, 384:768] += parts[2]

        @pl.when(j == N_CHIPS - 1)
        def _():
            o_ref[...] = acc[...]

    col = pl.BlockSpec((MM_TILE, D_SHARD), lambda i, j: (i, j))
    return pl.pallas_call(
        body, grid=(SEQ // MM_TILE, N_CHIPS),
        in_specs=[col, col, col, pl.BlockSpec((None, BR_ROWS, D_SHARD), lambda i, j: (j, 0, 0))],
        out_specs=pl.BlockSpec((MM_TILE, BR_ROWS), lambda i, j: (i, 0)),
        out_shape=jax.ShapeDtypeStruct((SEQ, BR_ROWS), F32),
        scratch_shapes=[pltpu.VMEM((MM_TILE, BR_ROWS), F32)],
        name="branch_bwd_input", compiler_params=_params(("parallel", "arbitrary")),
    )(du[0], du[1], du[2], wbr_all)


def branch_grad_weights(o_cat, du):
    def body(o_ref, d0_ref, d1_ref, d2_ref, g_ref, acc):
        k = pl.program_id(1)
        ob = o_ref[...].astype(BF16)
        parts = (_dot(ob[:, 0:256], d0_ref[...], TN), _dot(ob[:, 256:384], d1_ref[...], TN),
                 _dot(ob[:, 384:768], d2_ref[...], TN))

        @pl.when(k == 0)
        def _():
            acc[0:256, :], acc[256:384, :], acc[384:768, :] = parts

        @pl.when(k > 0)
        def _():
            acc[0:256, :] += parts[0]
            acc[256:384, :] += parts[1]
            acc[384:768, :] += parts[2]

        @pl.when(k == SEQ // MM_TILE - 1)
        def _():
            g_ref[...] = acc[...].astype(BF16)

    col = pl.BlockSpec((MM_TILE, D_SHARD), lambda j, k: (k, j))
    return pl.pallas_call(
        body, grid=(N_CHIPS, SEQ // MM_TILE),
        in_specs=[pl.BlockSpec((MM_TILE, BR_ROWS), lambda j, k: (k, 0)), col, col, col],
        out_specs=pl.BlockSpec((None, BR_ROWS, D_SHARD), lambda j, k: (j, 0, 0)),
        out_shape=jax.ShapeDtypeStruct((N_CHIPS, BR_ROWS, D_SHARD), BF16),
        scratch_shapes=[pltpu.VMEM((BR_ROWS, D_SHARD), F32)],
        name="branch_grad_weights", compiler_params=_params(("parallel", "arbitrary")),
    )(o_cat, du[0], du[1], du[2])


def mixer_bwd_input(dproj, win_all, x, dxo, gain, scale):
    return matmul_prenorm_bwd(
        "mixer_bwd_input", NT, (dproj, win_all),
        (pl.BlockSpec((MM_TILE, IN_SHARD), lambda i, j: (i, j)),
         pl.BlockSpec((None, D_MODEL, IN_SHARD), lambda i, j: (j, 0, 0))), x, dxo, gain, scale)


BATCH_QK = (((2,), (2,)), ((0,), (0,)))
BATCH_PV = (((2,), (1,)), ((0,), (0,)))
BATCH_TN = (((1,), (1,)), ((0,), (0,)))


SB_WIDTH = H_SB * HEAD_DIM
SB_ROWS = H_SB * BLK


def _split_dot(v, tri):
    hi = v.astype(BF16)
    lo = (v - hi.astype(F32)).astype(BF16)
    return _dot(hi, tri, NN) + _dot(lo, tri, NN)


def _tri(cmp):
    return cmp(lax.broadcasted_iota(jnp.int32, (BLK, BLK), 0), lax.broadcasted_iota(jnp.int32, (BLK, BLK), 1)).astype(BF16)


def _head_masks():
    lane = lax.broadcasted_iota(jnp.int32, (1, SB_WIDTH), 1) // HEAD_DIM
    return [lane == h for h in range(H_SB)]


def _stack_heads(x, masks):
    return jnp.concatenate([jnp.where(m, x, jnp.zeros_like(x)) for m in masks], axis=0)


def _merge_heads(y, masks):
    out = jnp.where(masks[0], y[0:BLK], 0.0)
    for h in range(1, H_SB):
        out = jnp.where(masks[h], y[h * BLK:(h + 1) * BLK], out)
    return out


def _sb_scores(q4, k_ref, j, diagonal):
    rows = pl.ds(pl.multiple_of(j * BLK, BLK), BLK)
    z = _dot(q4, k_ref[rows, :], NT)
    log_fail = -(jnp.maximum(z, 0.0) + jnp.log(1.0 + jnp.exp(-jnp.abs(z))))
    log_hit = z + log_fail
    before = None
    if diagonal:
        tile = (SB_ROWS, BLK)
        before = lax.broadcasted_iota(jnp.int32, tile, 1) < (lax.broadcasted_iota(jnp.int32, tile, 0) & (BLK - 1))
        log_fail = jnp.where(before, log_fail, 0.0)
    return rows, before, log_fail, log_hit


def _keep(before, x):
    return x if before is None else jnp.where(before, x, 0.0)


def sb_forward(qkv):
    def body(q_ref, k_ref, v_ref, o_ref, tot_ref):
        i = pl.program_id(0)
        masks = _head_masks()
        q4 = _stack_heads(q_ref[...] * QK_SCALE, masks)
        later = _tri(lambda r, c: r > c)

        def tile(j, carry, diagonal):
            o, run = carry
            rows, before, log_fail, log_hit = _sb_scores(q4, k_ref, j, diagonal)
            between = _split_dot(log_fail, later) + run
            w = _keep(before, jnp.exp(log_hit + between))
            o = o + _merge_heads(_dot(w.astype(BF16), v_ref[rows, :], NN), masks)
            return o, run + jnp.sum(log_fail, axis=1, keepdims=True)

        carry = tile(i, (jnp.zeros((BLK, SB_WIDTH), F32), jnp.zeros((SB_ROWS, 1), F32)), True)
        o, run = lax.fori_loop(0, i, lambda t, c: tile(i - 1 - t, c, False), carry)
        o_ref[...] = o
        tot_ref[...] = run

    return pl.pallas_call(
        body, grid=(N_BLK,),
        in_specs=[pl.BlockSpec((BLK, SB_WIDTH), lambda i: (i, 0)), pl.BlockSpec((SEQ, SB_WIDTH), lambda i: (0, 1)),
                  pl.BlockSpec((SEQ, SB_WIDTH), lambda i: (0, 2))],
        out_specs=[pl.BlockSpec((BLK, SB_WIDTH), lambda i: (i, 0)), pl.BlockSpec((None, SB_ROWS, 1), lambda i: (i, 0, 0))],
        out_shape=[jax.ShapeDtypeStruct((SEQ, SB_WIDTH), F32), jax.ShapeDtypeStruct((N_BLK, SB_ROWS, 1), F32)],
        name="sb_forward", compiler_params=_params(("parallel",)),
    )(qkv, qkv, qkv)


def sb_backward(qkv, total, do_cat):
    def body(q_ref, k_ref, v_ref, tot_ref, do_ref, dq_ref, dk_ref, dv_ref):
        i = pl.program_id(0)

        @pl.when(i == 0)
        def _():
            dk_ref[...] = jnp.zeros_like(dk_ref)
            dv_ref[...] = jnp.zeros_like(dv_ref)

        masks = _head_masks()
        q4 = _stack_heads(q_ref[...] * QK_SCALE, masks)
        do4 = _stack_heads(do_ref[...].astype(BF16), masks)
        total_v = tot_ref[...]
        upto = _tri(lambda r, c: r <= c)
        earlier = _tri(lambda r, c: r < c)

        def tile(j, carry, diagonal):
            dq, seen, g_seen = carry
            rows, before, log_fail, log_hit = _sb_scores(q4, k_ref, j, diagonal)
            between = total_v - (seen + _split_dot(log_fail, upto))
            w = _keep(before, jnp.exp(log_hit + between))
            g = _dot(do4, v_ref[rows, :], NT) * w
            g_earlier = g_seen + _split_dot(g, earlier)
            sig = jnp.exp(log_hit)
            dz = _keep(before, g * (1.0 - sig) - g_earlier * sig).astype(BF16)
            dq = dq + _merge_heads(_dot(dz, k_ref[rows, :], NN), masks)
            dk_ref[rows, :] += _dot(dz, q4, TN)
            dv_ref[rows, :] += _dot(w.astype(BF16), do4, TN)
            return dq, seen + jnp.sum(log_fail, axis=1, keepdims=True), g_seen + jnp.sum(g, axis=1, keepdims=True)

        zero = jnp.zeros((SB_ROWS, 1), F32)
        carry = lax.fori_loop(0, i, lambda j, c: tile(j, c, False), (jnp.zeros((BLK, SB_WIDTH), F32), zero, zero))
        dq, _, _ = tile(i, carry, True)
        dq_ref[...] = dq * QK_SCALE

    blk = pl.BlockSpec((BLK, SB_WIDTH), lambda i: (i, 0))
    full = pl.BlockSpec((SEQ, SB_WIDTH), lambda i: (0, 0))
    shape = jax.ShapeDtypeStruct((SEQ, SB_WIDTH), F32)
    return pl.pallas_call(
        body, grid=(N_BLK,),
        in_specs=[blk, pl.BlockSpec((SEQ, SB_WIDTH), lambda i: (0, 1)), pl.BlockSpec((SEQ, SB_WIDTH), lambda i: (0, 2)),
                  pl.BlockSpec((None, SB_ROWS, 1), lambda i: (i, 0, 0)), blk],
        out_specs=[blk, full, full], out_shape=[shape, shape, shape],
        name="sb_backward", compiler_params=_params(("arbitrary",)),
    )(qkv, qkv, qkv, total, do_cat)


def _band_scores(q_ref, kp_ref, ko_ref, bias_ref, hb, prev_mask):
    b = pl.program_id(1)
    qs = q_ref[...]
    s_prev = _dot(qs, kp_ref[...], BATCH_QK) + bias_ref[:, :, 0:BLK]
    s_prev = jnp.concatenate(
        [jnp.where((b & prev_mask(pl.program_id(0) * hb + t)) != 0, s_prev[t:t + 1], NEG) for t in range(hb)], axis=0)
    s_own = _dot(qs, ko_ref[...], BATCH_QK) + bias_ref[:, :, BLK:2 * BLK]
    return qs, s_prev, s_own


def _band_specs(hb, rows, t_n):
    def q_spec(width):
        return pl.BlockSpec((hb, None, rows, width), lambda h, b: (h, b, 0, 0))

    own = pl.BlockSpec((hb, BLK, HEAD_DIM), lambda h, b: (h, b, 0))
    prev = pl.BlockSpec((hb, BLK, HEAD_DIM), lambda h, b: (h, jnp.maximum(b - 1, 0), 0))
    per_head = lambda r, width: pl.BlockSpec((hb, r, width), lambda h, b: (h, 0, 0))
    return q_spec, own, prev, per_head


def banded_forward(name, q, k, v, bias, sinks, hb, prev_mask):
    h_n, nb, rows, _ = q.shape

    def body(q_ref, kp_ref, ko_ref, vp_ref, vo_ref, bias_ref, sink_ref, o_ref, lse_ref):
        _, s_prev, s_own = _band_scores(q_ref, kp_ref, ko_ref, bias_ref, hb, prev_mask)
        sink = sink_ref[...]
        m = jnp.maximum(jnp.maximum(jnp.max(s_prev, axis=2, keepdims=True), jnp.max(s_own, axis=2, keepdims=True)), sink)
        p_prev = jnp.exp(s_prev - m)
        p_own = jnp.exp(s_own - m)
        denom = jnp.sum(p_prev, axis=2, keepdims=True) + jnp.sum(p_own, axis=2, keepdims=True) + jnp.exp(sink - m)
        o = _dot(p_prev.astype(BF16), vp_ref[...], BATCH_PV) + _dot(p_own.astype(BF16), vo_ref[...], BATCH_PV)
        o_ref[...] = o / denom
        lse_ref[...] = m + jnp.log(denom)

    q_spec, own, prev, per_head = _band_specs(hb, rows, k.shape[1])
    return pl.pallas_call(
        body, grid=(h_n // hb, nb),
        in_specs=[q_spec(HEAD_DIM), prev, own, prev, own, per_head(rows, 2 * BLK), per_head(rows, 1)],
        out_specs=[q_spec(HEAD_DIM), q_spec(1)],
        out_shape=[jax.ShapeDtypeStruct(q.shape, F32), jax.ShapeDtypeStruct((h_n, nb, rows, 1), F32)],
        name=name, compiler_params=_params(("parallel", "parallel")),
    )(q, k, k, v, v, bias, sinks)


def banded_backward(name, q, k, v, bias, sinks, o, lse, do, dlse, hb, prev_mask):
    h_n, nb, rows, _ = q.shape
    t_n = k.shape[1]

    def body(q_ref, kp_ref, ko_ref, vp_ref, vo_ref, bias_ref, sink_ref, o_ref, lse_ref, do_ref, dlse_ref,
             dq_ref, dk_ref, dv_ref, dbias_ref, dsink_ref):
        b = pl.program_id(1)

        @pl.when(b == 0)
        def _():
            dk_ref[...] = jnp.zeros_like(dk_ref)
            dv_ref[...] = jnp.zeros_like(dv_ref)
            dbias_ref[...] = jnp.zeros_like(dbias_ref)
            dsink_ref[...] = jnp.zeros_like(dsink_ref)

        qs, s_prev, s_own = _band_scores(q_ref, kp_ref, ko_ref, bias_ref, hb, prev_mask)
        lse_v = lse_ref[...]
        dov = do_ref[...]
        dob = dov.astype(BF16)
        shift = dlse_ref[...] - jnp.sum(dov * o_ref[...], axis=2, keepdims=True)
        p_prev = jnp.exp(s_prev - lse_v)
        p_own = jnp.exp(s_own - lse_v)
        ds_prev = p_prev * (_dot(dob, vp_ref[...], BATCH_QK) + shift)
        ds_own = p_own * (_dot(dob, vo_ref[...], BATCH_QK) + shift)
        dbias_ref[:, :, 0:BLK] += ds_prev
        dbias_ref[:, :, BLK:2 * BLK] += ds_own
        d_sink = jnp.exp(sink_ref[...] - lse_v) * shift
        for g in range(rows // BLK):
            dsink_ref[:, g:g + 1, :] += jnp.sum(d_sink[:, g * BLK:(g + 1) * BLK, :], axis=1, keepdims=True)
        ds_prev = ds_prev.astype(BF16)
        ds_own = ds_own.astype(BF16)
        dq_ref[...] = (_dot(ds_prev, kp_ref[...], BATCH_PV) + _dot(ds_own, ko_ref[...], BATCH_PV)) * QK_SCALE
        rows_prev = pl.ds(pl.multiple_of(jnp.maximum(b - 1, 0) * BLK, BLK), BLK)
        rows_own = pl.ds(pl.multiple_of(b * BLK, BLK), BLK)
        dk_ref[:, rows_prev, :] += _dot(ds_prev, qs, BATCH_TN)
        dk_ref[:, rows_own, :] += _dot(ds_own, qs, BATCH_TN)
        dv_ref[:, rows_prev, :] += _dot(p_prev.astype(BF16), dob, BATCH_TN)
        dv_ref[:, rows_own, :] += _dot(p_own.astype(BF16), dob, BATCH_TN)

    q_spec, own, prev, per_head = _band_specs(hb, rows, t_n)
    kv_full = per_head(t_n, HEAD_DIM)
    kv_shape = jax.ShapeDtypeStruct((h_n, t_n, HEAD_DIM), F32)
    return pl.pallas_call(
        body, grid=(h_n // hb, nb),
        in_specs=[q_spec(HEAD_DIM), prev, own, prev, own, per_head(rows, 2 * BLK), per_head(rows, 1),
                  q_spec(HEAD_DIM), q_spec(1), q_spec(HEAD_DIM), q_spec(1)],
        out_specs=[q_spec(HEAD_DIM), kv_full, kv_full, per_head(rows, 2 * BLK), per_head(rows // BLK, BLK)],
        out_shape=[jax.ShapeDtypeStruct(q.shape, F32), kv_shape, kv_shape,
                   jax.ShapeDtypeStruct((h_n, rows, 2 * BLK), F32), jax.ShapeDtypeStruct((h_n, rows // BLK, BLK), F32)],
        name=name, compiler_params=_params(("parallel", "arbitrary")),
    )(q, k, k, v, v, bias, sinks, o, lse, do, dlse)


def _dil_prev_mask(head):
    group = head // H_PER_DIL
    return jnp.where(group == 0, 15, jnp.where(group == 1, 3, 0))


def _swa_prev_mask(head):
    del head
    return 15


DIL_HEADS_PER_STEP = 3
SWA_GROUP = H_SWA_Q // H_SWA_KV
N_BLK = SEQ // BLK


def dilated_merge(o, lse):
    def body(o_ref, l_ref, out_ref):
        lv = l_ref[...]
        m = jnp.max(lv, axis=0, keepdims=True)
        e = jnp.exp(lv - m)
        alpha = e / jnp.sum(e, axis=0, keepdims=True)
        out_ref[...] = jnp.sum(alpha * o_ref[...], axis=0)

    return pl.pallas_call(
        body, grid=(H_PER_DIL, SEQ // ROW_TILE),
        in_specs=[pl.BlockSpec((3, None, ROW_TILE, HEAD_DIM), lambda h, i: (0, h, i, 0)),
                  pl.BlockSpec((3, None, ROW_TILE, 1), lambda h, i: (0, h, i, 0))],
        out_specs=pl.BlockSpec((None, ROW_TILE, HEAD_DIM), lambda h, i: (h, i, 0)),
        out_shape=jax.ShapeDtypeStruct((H_PER_DIL, SEQ, HEAD_DIM), F32),
        name="dilated_merge", compiler_params=_params(("parallel", "parallel")),
    )(o, lse)


def dilated_merge_bwd(o, lse, dout):
    def body(o_ref, l_ref, d_ref, do_ref, dl_ref):
        lv = l_ref[...]
        m = jnp.max(lv, axis=0, keepdims=True)
        e = jnp.exp(lv - m)
        alpha = e / jnp.sum(e, axis=0, keepdims=True)
        dv = d_ref[...][None]
        do_ref[...] = alpha * dv
        dalpha = jnp.sum(dv * o_ref[...], axis=-1, keepdims=True)
        dl_ref[...] = alpha * (dalpha - jnp.sum(alpha * dalpha, axis=0, keepdims=True))

    o_spec = pl.BlockSpec((3, None, ROW_TILE, HEAD_DIM), lambda h, i: (0, h, i, 0))
    l_spec = pl.BlockSpec((3, None, ROW_TILE, 1), lambda h, i: (0, h, i, 0))
    return pl.pallas_call(
        body, grid=(H_PER_DIL, SEQ // ROW_TILE),
        in_specs=[o_spec, l_spec, pl.BlockSpec((None, ROW_TILE, HEAD_DIM), lambda h, i: (h, i, 0))],
        out_specs=[o_spec, l_spec],
        out_shape=[jax.ShapeDtypeStruct(o.shape, F32), jax.ShapeDtypeStruct(lse.shape, F32)],
        name="dilated_merge_bwd", compiler_params=_params(("parallel", "parallel")),
    )(o, lse, dout)


def rel_bias_reduce(dbias0, dbias1, bucket):
    def body(d0_ref, d1_ref, b_ref, o_ref):
        dv, bv = d0_ref[...] + d1_ref[...], b_ref[...]
        lane = lax.broadcasted_iota(jnp.int32, (1, BLK), 1)
        acc = jnp.zeros((1, BLK), F32)
        for bkt in range(N_BUCKETS):
            acc = acc + jnp.where(lane == bkt, jnp.sum(jnp.where(bv == bkt, dv, 0.0)), 0.0)
        o_ref[...] = acc

    tile = pl.BlockSpec((None, BLK, 2 * BLK), lambda h: (h, 0, 0))
    return pl.pallas_call(
        body, grid=(dbias0.shape[0],), in_specs=[tile, tile, tile],
        out_specs=pl.BlockSpec((None, 1, BLK), lambda h: (h, 0, 0)),
        out_shape=jax.ShapeDtypeStruct((dbias0.shape[0], 1, BLK), F32),
        name="rel_bias_reduce", compiler_params=_params(("parallel",)),
    )(dbias0, dbias1, bucket)


def _heads(t):
    return t.reshape(SEQ, -1, HEAD_DIM).transpose(1, 0, 2)


def _unheads(t):
    return t.transpose(1, 0, 2).reshape(SEQ, -1)


def _dilate(t):
    parts = []
    for g, (_, d) in enumerate(DIL_PATTERNS):
        tg = t[:, 128 * g:128 * (g + 1)].reshape(SEQ // d, d, H_PER_DIL, HEAD_DIM).transpose(2, 1, 0, 3)
        parts.append(tg.reshape(H_PER_DIL, SEQ, HEAD_DIM))
    return jnp.concatenate(parts, axis=0)


def _undilate(t):
    outs = []
    for g, (_, d) in enumerate(DIL_PATTERNS):
        tg = t[2 * g:2 * g + 2].reshape(H_PER_DIL, d, SEQ // d, -1).transpose(0, 2, 1, 3)
        outs.append(tg.reshape(H_PER_DIL, SEQ, -1))
    return jnp.stack(outs)


def _redilate(t):
    parts = []
    for g, (_, d) in enumerate(DIL_PATTERNS):
        tg = t[g].reshape(H_PER_DIL, SEQ // d, d, -1).transpose(0, 2, 1, 3)
        parts.append(tg.reshape(H_PER_DIL, SEQ, -1))
    return jnp.concatenate(parts, axis=0)


def _t5_bucket(n):
    max_exact = N_BUCKETS // 2
    nf = jnp.maximum(n, 1).astype(F32)
    large = max_exact + (jnp.log(nf / max_exact) / math.log(MAX_REL_DIST / max_exact)
                         * (N_BUCKETS - max_exact)).astype(jnp.int32)
    large = jnp.minimum(large, N_BUCKETS - 1)
    return jnp.where(n < max_exact, n, large)


def band_tables(rel_bias):
    rel = jnp.arange(BLK)[:, None] + BLK - jnp.arange(2 * BLK)[None, :]
    buckets = []
    patterns = [(d, w // d) for w, d in DIL_PATTERNS for _ in range(H_PER_DIL)] + [(1, SWA_WINDOW - 1)] * H_SWA_Q
    for d, max_dist in patterns:
        band = (rel >= 0) & (rel <= max_dist)
        buckets.append(jnp.where(band, _t5_bucket(jnp.maximum(rel, 0) * d), -1))
    buckets = jnp.stack(buckets).astype(jnp.int32)

    def body(table_ref, b_ref, o_ref):
        h = pl.program_id(0)
        bv = b_ref[...]
        tile = jnp.full(bv.shape, NEG, F32)
        for bkt in range(N_BUCKETS):
            tile = jnp.where(bv == bkt, table_ref[h, bkt], tile)
        o_ref[...] = tile

    spec = pl.BlockSpec((None, BLK, 2 * BLK), lambda h: (h, 0, 0))
    tiles = pl.pallas_call(
        body, grid=(len(patterns),), in_specs=[pl.BlockSpec(memory_space=pltpu.SMEM), spec], out_specs=spec,
        out_shape=jax.ShapeDtypeStruct(buckets.shape, F32), name="band_tables", compiler_params=_params(("parallel",)),
    )(rel_bias.T, buckets)
    return tiles[:H_DIL], tiles[H_DIL:], buckets


def _swa_rows(t):
    t = t.reshape(N_BLK, BLK, H_SWA_KV, SWA_GROUP, HEAD_DIM).transpose(2, 0, 3, 1, 4)
    return t.reshape(H_SWA_KV, N_BLK, SWA_GROUP * BLK, HEAD_DIM)


def _swa_tokens(t):
    t = t.reshape(H_SWA_KV, N_BLK, SWA_GROUP, BLK, HEAD_DIM).transpose(1, 3, 0, 2, 4)
    return t.reshape(SEQ, H_SWA_Q * HEAD_DIM)


def _sink_rows(sinks):
    return jnp.broadcast_to(sinks.reshape(H_SWA_KV, SWA_GROUP, 1, 1), (H_SWA_KV, SWA_GROUP, BLK, 1)).reshape(
        H_SWA_KV, SWA_GROUP * BLK, 1)


def _no_sinks():
    return jnp.full((H_DIL, BLK, 1), NEG, F32)


def _vec(v):
    return v.reshape(1, D_MODEL)


def ffn_forward(x, gain, mod, w):
    h = prenorm(x, _vec(gain), _vec(mod[1]), _vec(mod[0]))
    a, b, s = ffn_up(h, w[0], w[1])
    f, xo = ffn_down(s, w[2], x, _vec(mod[2]))
    return xo, (x, h, a, b, s, f)


def ffn_backward(dxo, saved, gain, mod, w):
    x, h, a, b, s, f = saved
    df, dgate = resid_bwd(dxo, f, _vec(mod[2]), 0.5)
    da, db = ffn_bwd_hidden(df, w[2], a, b)
    grads = ffn_grad_weights(h, s, df, da, db)
    dx, stats = ffn_bwd_input(da, db, w[0], w[1], x, dxo, _vec(gain), _vec(mod[1]))
    return dx, jnp.stack([stats[0], stats[1], dgate[0]]), stats[2], grads


def mixer_forward(x, gain, mod, sinks, bias_dil, bias_swa, w):
    h = prenorm(x, _vec(gain), _vec(mod[1]), _vec(mod[0]))
    proj = in_proj(h, w[0])
    qkv = proj[:, :D_QKV].astype(BF16)
    q_dil, k_dil, v_dil = _dilate(qkv[:, 768:1152] * QK_SCALE), _dilate(qkv[:, 1152:1536]), _dilate(qkv[:, 1536:1920])
    q_swa, k_swa, v_swa = _swa_rows(qkv[:, 1920:2304] * QK_SCALE), _heads(qkv[:, 2304:2432]), _heads(qkv[:, 2432:2560])
    o_sb, total_sb = sb_forward(qkv)
    q_dil = q_dil.reshape(H_DIL, N_BLK, BLK, HEAD_DIM)
    o_dd, lse_dd = banded_forward("dilated_forward", q_dil, k_dil, v_dil, bias_dil, _no_sinks(), DIL_HEADS_PER_STEP,
                                  _dil_prev_mask)
    o_dt, lse_dt = _undilate(o_dd.reshape(H_DIL, SEQ, HEAD_DIM)), _undilate(lse_dd.reshape(H_DIL, SEQ, 1))
    o_dil = dilated_merge(o_dt, lse_dt)
    bias_swa = bias_swa.reshape(H_SWA_KV, SWA_GROUP * BLK, 2 * BLK)
    o_swa, lse_swa = banded_forward("swa_forward", q_swa, k_swa, v_swa, bias_swa, _sink_rows(sinks), 1, _swa_prev_mask)
    o_cat = jnp.concatenate([o_sb, _unheads(o_dil), _swa_tokens(o_swa)], axis=1)
    merged = merge_branches(o_cat, w[1], proj)
    mo, xo = out_proj(merged, w[2], x, _vec(mod[2]))
    saved = (x, h, proj, (qkv, total_sb), (q_dil, k_dil, v_dil, o_dd, lse_dd, o_dt, lse_dt),
             (q_swa, k_swa, v_swa, o_swa, lse_swa), o_cat, merged, mo)
    return xo, saved


def mixer_backward(dxo, saved, gain, mod, sinks, bias_dil, bias_swa, w):
    x, h, proj, sb, dil, swa, o_cat, merged, mo = saved
    dmo, dgate = resid_bwd(dxo, mo, _vec(mod[2]), 1.0)
    tok = pl.BlockSpec((MM_TILE, D_MODEL), lambda j, k: (k, 0))
    g_out = grad_weight("grad_w_out", merged, pl.BlockSpec((MM_TILE, D_SHARD), lambda j, k: (k, j)), dmo, tok,
                        (D_SHARD, D_MODEL))
    du0, du1, du2, dg0, dg1, dg2 = merge_bwd(dmo, w[2], o_cat, w[1], proj)
    du = (du0, du1, du2)
    do_cat = branch_bwd_input(du, w[1])
    g_br = branch_grad_weights(o_cat, du)

    qkv, total_sb = sb
    dq_sb, dk_sb, dv_sb = sb_backward(qkv, total_sb, do_cat)

    q_dil, k_dil, v_dil, o_dd, lse_dd, o_dt, lse_dt = dil
    do_dt, dlse_dt = dilated_merge_bwd(o_dt, lse_dt, _heads(do_cat[:, 256:384]))
    dq_dil, dk_dil, dv_dil, dbias_dil, _ = banded_backward(
        "dilated_backward", q_dil, k_dil, v_dil, bias_dil, _no_sinks(), o_dd, lse_dd,
        _redilate(do_dt).reshape(q_dil.shape), _redilate(dlse_dt).reshape(lse_dd.shape), DIL_HEADS_PER_STEP, _dil_prev_mask)

    q_swa, k_swa, v_swa, o_swa, lse_swa = swa
    bias_swa = bias_swa.reshape(H_SWA_KV, SWA_GROUP * BLK, 2 * BLK)
    dq_swa, dk_swa, dv_swa, dbias_swa, dsinks = banded_backward(
        "swa_backward", q_swa, k_swa, v_swa, bias_swa, _sink_rows(sinks), o_swa, lse_swa, _swa_rows(do_cat[:, 384:768]),
        jnp.zeros_like(lse_swa), 1, _swa_prev_mask)
    dbias_swa = dbias_swa.reshape(H_SWA_Q, BLK, 2 * BLK)

    def tokens(t):
        return _undilate(t).transpose(2, 0, 1, 3).reshape(SEQ, -1)

    dproj = jnp.concatenate(
        [dq_sb, dk_sb, dv_sb, tokens(dq_dil.reshape(H_DIL, SEQ, HEAD_DIM)), tokens(dk_dil),
         tokens(dv_dil), _swa_tokens(dq_swa), _unheads(dk_swa), _unheads(dv_swa)], axis=1).astype(BF16)
    dproj = jnp.concatenate([dproj, dg0, dg1, dg2], axis=1)
    g_in = grad_weight("grad_w_in", h, tok, dproj, pl.BlockSpec((MM_TILE, IN_SHARD), lambda j, k: (k, j)),
                       (D_MODEL, IN_SHARD))
    dx, stats = mixer_bwd_input(dproj, w[0], x, dxo, _vec(gain), _vec(mod[1]))
    dmod = jnp.stack([stats[0], stats[1], dgate[0]])
    dbias = jnp.concatenate([dbias_dil, dbias_swa], axis=0)
    return dx, dmod, stats[2], dbias, dsinks[:, :, 0].reshape(H_SWA_Q), (g_in, g_br, g_out)


N_UNITS = 3 * DEPTH


def device_step(x, target, mod, gains, final_gain, sinks, rel_bias, get_weights, put_grads):
    bias_dil, bias_swa, bucket = band_tables(rel_bias)
    saved, weights = [], []
    for u in range(N_UNITS):
        l, j = divmod(u, 3)
        w = get_weights(u, x)
        if j == 1:
            x, s = mixer_forward(x, gains[l, 1], mod[l, 1], sinks[l], bias_dil, bias_swa, w)
        else:
            x, s = ffn_forward(x, gains[l, j], mod[l, j], w)
        saved.append(s)
        weights.append(w)
    loss, dx, dfinal = final_loss(x, _vec(final_gain), target)

    dmod = [[None] * 3 for _ in range(DEPTH)]
    dgains = [[None] * 3 for _ in range(DEPTH)]
    dbias, dsinks = [None] * DEPTH, [None] * DEPTH
    zero = jnp.zeros((1, 1), F32)
    for u in reversed(range(N_UNITS)):
        l, j = divmod(u, 3)
        gain = gains[l, j] + zero[0]
        if j == 1:
            dx, dmod[l][j], dgains[l][j], dbias[l], dsinks[l], grads = mixer_backward(
                dx, saved[u], gain, mod[l, 1], sinks[l], bias_dil, bias_swa, weights[u])
        else:
            dx, dmod[l][j], dgains[l][j], grads = ffn_backward(dx, saved[u], gain, mod[l, j], weights[u])
        if u > 0:
            zero = put_grads(u, grads)
    drel = rel_bias_reduce(dbias[0], dbias[1], bucket)[:, 0, :N_BUCKETS].T
    stack2 = lambda t: jnp.stack([jnp.stack(r) for r in t])
    return loss, dx, stack2(dmod), stack2(dgains), dfinal[0], jnp.stack(dsinks), drel, grads


MESH = pl.DeviceIdType.MESH
CHIP_FLIPS = ((1, 0), (0, 1), (1, 1))
ANY = pl.BlockSpec(memory_space=pl.ANY)


def _position():
    return lax.axis_index("x"), lax.axis_index("y"), lax.axis_index("c")


def all_gather_small(name, piece):
    def body(x_ref, out_ref, send_sems, recv_sems, local_sem):
        x, y, c = _position()
        me, sibling = (x, y, c), (x, y, 1 - c)
        chips = [(x ^ fx, y ^ fy) for fx, fy in CHIP_FLIPS]

        def rows(px, py, pc):
            return out_ref.at[4 * px + 2 * py + pc]

        def copy(k, block, to, src=None):
            return pltpu.make_async_remote_copy(
                src_ref=rows(*block) if src is None else src, dst_ref=rows(*block),
                send_sem=send_sems.at[k], recv_sem=recv_sems.at[k], device_id=to, device_id_type=MESH)

        mine = pltpu.make_async_copy(x_ref, rows(*me), local_sem)
        mine.start()
        first = [copy(0, me, sibling, src=x_ref)]
        first += [copy(1 + j, me, (*chip, c), src=x_ref) for j, chip in enumerate(chips)]
        for cp in first:
            cp.start()
        passed = [copy(4 + j, (*chip, c), sibling) for j, chip in enumerate(chips)]
        for j, chip in enumerate(chips):
            copy(1 + j, (*chip, c), me).wait_recv()
            passed[j].start()
        copy(0, sibling, me).wait_recv()
        for j, chip in enumerate(chips):
            copy(4 + j, (*chip, 1 - c), me).wait_recv()
        for cp in first + passed:
            cp.wait_send()
        mine.wait()

    return pl.pallas_call(
        body, out_shape=jax.ShapeDtypeStruct((N_DEV,) + piece.shape, piece.dtype),
        in_specs=[pl.BlockSpec(memory_space=pltpu.VMEM)], out_specs=pl.BlockSpec(memory_space=pltpu.VMEM),
        scratch_shapes=[pltpu.SemaphoreType.DMA((7,)), pltpu.SemaphoreType.DMA((7,)), pltpu.SemaphoreType.DMA],
        name=name,
    )(piece)


def exchange(name, operands, out_shapes, aliases, plan):
    n_in, n_out = len(operands), len(out_shapes)

    def body(*refs):
        ins, outs = refs[:n_in], refs[n_in:n_in + n_out]
        send_sems, recv_sems, local_sems = refs[n_in + n_out:]
        x, y, c = _position()
        local, sends, recvs = plan(ins, outs, x, y, c)
        local = [pltpu.make_async_copy(s, d, local_sems.at[k]) for k, (s, d) in enumerate(local)]
        for cp in local:
            cp.start()
        remote = [pltpu.make_async_remote_copy(src_ref=s, dst_ref=d, send_sem=send_sems.at[k], recv_sem=recv_sems.at[k],
                                               device_id=dev, device_id_type=MESH)
                  for k, (s, d, dev) in enumerate(sends)]
        for cp in remote:
            cp.start()
        for k, r in enumerate(recvs):
            pltpu.make_async_remote_copy(src_ref=r, dst_ref=r, send_sem=send_sems.at[k], recv_sem=recv_sems.at[k],
                                         device_id=(x, y, c), device_id_type=MESH).wait_recv()
        for cp in remote:
            cp.wait_send()
        for cp in local:
            cp.wait()

    n_sends, n_local = plan.n_sends, max(plan.n_local, 1)
    return pl.pallas_call(
        body, out_shape=out_shapes, in_specs=[ANY] * n_in, out_specs=[ANY] * n_out,
        scratch_shapes=[pltpu.SemaphoreType.DMA((n_sends,)), pltpu.SemaphoreType.DMA((n_sends,)),
                        pltpu.SemaphoreType.DMA((n_local,))],
        input_output_aliases=aliases, name=name,
    )(*operands)


def _plan(n_local, n_sends):
    def wrap(fn):
        fn.n_local, fn.n_sends = n_local, n_sends
        return fn
    return wrap


def _half(ref, axis, c):
    rows = ref.shape[axis] // 2
    idx = [slice(None)] * len(ref.shape)
    idx[axis] = pl.ds(pl.multiple_of(c * rows, 16), rows)
    return ref.at[tuple(idx)]


HBM = pl.BlockSpec(memory_space=pltpu.HBM)
SEM = pl.BlockSpec(memory_space=pltpu.SEMAPHORE)
EFFECT = pltpu.SideEffectType.DATAFLOW_SIDE_EFFECTING


def split_start(name, bufs, extra, n_copies, describe):
    n = len(bufs)

    def body(*refs):
        send_sems, recv_sems = refs[n + len(extra)], refs[n + len(extra) + 1]
        x, y, c = _position()
        for k, (src, dst, _, peer) in enumerate(describe(refs[:n], x, y, c)):
            pltpu.make_async_remote_copy(src_ref=src, dst_ref=dst, send_sem=send_sems.at[k], recv_sem=recv_sems.at[k],
                                         device_id=peer, device_id_type=MESH).start()
        token = refs[-1]
        token[...] = jnp.zeros_like(token)

    out = pl.pallas_call(
        body, name=name,
        out_shape=(pltpu.SemaphoreType.DMA((n_copies,)), pltpu.SemaphoreType.DMA((n_copies,)),
                   *[pltpu.HBM(b.shape, b.dtype) for b in bufs], jax.ShapeDtypeStruct((8, 128), F32)),
        in_specs=[HBM] * n + [ANY] * len(extra),
        out_specs=(SEM, SEM, *[HBM] * n, pl.BlockSpec(memory_space=pltpu.VMEM)),
        input_output_aliases={k: 2 + k for k in range(n)},
        compiler_params=pltpu.CompilerParams(has_side_effects=EFFECT),
    )(*[pltpu.with_memory_space_constraint(b, pltpu.HBM) for b in bufs], *extra)
    return out[0], out[1], list(out[2:2 + n]), out[-1]


def split_wait(name, bufs, send_sems, recv_sems, after, describe):
    n = len(bufs)

    def body(*refs):
        send, recv = refs[n], refs[n + 1]
        x, y, c = _position()
        for k, (src, _, dst, peer) in enumerate(describe(refs[:n], x, y, c)):
            copy = pltpu.make_async_remote_copy(src_ref=src, dst_ref=dst, send_sem=send.at[k], recv_sem=recv.at[k],
                                                device_id=peer, device_id_type=MESH)
            copy.wait_send()
            copy.wait_recv()

    out = pl.pallas_call(
        body, name=name, out_shape=[pltpu.HBM(b.shape, b.dtype) for b in bufs],
        in_specs=[HBM] * n + [SEM, SEM] + [ANY] * len(after), out_specs=[HBM] * n,
        input_output_aliases={k: k for k in range(n)},
        compiler_params=pltpu.CompilerParams(has_side_effects=EFFECT),
    )(*bufs, send_sems, recv_sems, *after)
    return list(out)


def _row_tile(rows, cols):
    best = 16
    for t in range(16, rows + 1, 16):
        if rows % t == 0 and t * cols <= 256 * 1024:
            best = t
    return best


def cast_into_slot(name, param, index, chip):
    rows, cols = param.shape[-2:]
    tr = _row_tile(rows, cols)
    lead = (None,) * len(index)

    def body(chip_ref, s_ref, o_ref):
        del chip_ref
        o_ref[...] = s_ref[...].astype(BF16)

    return pl.pallas_call(
        body, out_shape=jax.ShapeDtypeStruct((N_CHIPS, rows, cols), BF16),
        grid_spec=pltpu.PrefetchScalarGridSpec(
            num_scalar_prefetch=1, grid=(rows // tr,),
            in_specs=[pl.BlockSpec(lead + (tr, cols), lambda r, chip_ref: index + (r, 0))],
            out_specs=pl.BlockSpec((None, tr, cols), lambda r, chip_ref: (chip_ref[0], r, 0))),
        name=name, compiler_params=_params(("parallel",)),
    )(chip, param)


GATHER_STAGES = ((0,), (1,), (2,), (3, 4, 5))
REDUCE_STAGES = ((5, 4, 3), (2,), (1,), (0,))


def _gather_copies(slots, x, y, c):
    me = 2 * x + y
    out = []
    for s in slots:
        for fx, fy in CHIP_FLIPS:
            mine = _half(s.at[me], 0, c)
            out.append((mine, mine, _half(s.at[2 * (x ^ fx) + (y ^ fy)], 0, c), (x ^ fx, y ^ fy, c)))
    return out


class WeightStream:
    def __init__(self, shards, chip, after=()):
        self.pending, self.ready = {}, {}
        token = tuple(after)
        for si, units in enumerate(GATHER_STAGES):
            slots = [cast_into_slot(f"cast_{u}_{t}", p, idx, chip) for u in units for t, (p, idx) in enumerate(shards[u])]
            send, recv, slots, tok = split_start(f"gather_start_{si}", slots, token, 3 * len(slots), _gather_copies)
            self.pending[si] = (send, recv, slots)
            token = (tok,)
        self.token = token

    def get(self, u, after):
        if u not in self.ready:
            si = next(k for k, units in enumerate(GATHER_STAGES) if u in units)
            send, recv, slots = self.pending.pop(si)
            slots = split_wait(f"gather_wait_{si}", slots, send, recv, (after,) + self.token, _gather_copies)
            self.token = ()

            @_plan(0, 3 * len(slots))
            def to_sibling(ins, outs, x, y, c):
                sends, recvs = [], []
                for o in outs:
                    for fx, fy in CHIP_FLIPS:
                        slab = o.at[2 * (x ^ fx) + (y ^ fy)]
                        sends.append((_half(slab, 0, c), _half(slab, 0, c), (x, y, 1 - c)))
                        recvs.append(_half(slab, 0, 1 - c))
                return [], sends, recvs

            shapes = [jax.ShapeDtypeStruct(s.shape, BF16) for s in slots]
            slots = exchange(f"gather_sibling_{si}", slots, shapes, {k: k for k in range(len(slots))}, to_sibling)
            for i, v in enumerate(GATHER_STAGES[si]):
                self.ready[v] = tuple(slots[3 * i:3 * i + 3])
        return self.ready[u]


def _reduce_copies(bufs, x, y, c):
    n = len(bufs) // 2
    out = []
    for s, land in zip(bufs[:n], bufs[n:]):
        for k, (fx, fy) in enumerate(CHIP_FLIPS):
            out.append((s.at[2 * (x ^ fx) + (y ^ fy)], land.at[k], land.at[k], (x ^ fx, y ^ fy, c)))
    return out


GRAD_SLOTS = {"gate": (2 * DEPTH, FF_SHARD, D_MODEL), "up": (2 * DEPTH, FF_SHARD, D_MODEL),
              "down": (2 * DEPTH, FF_SHARD, D_MODEL), "in": (DEPTH, D_MODEL, IN_SHARD),
              "br": (DEPTH, BR_ROWS, D_SHARD), "out": (DEPTH, D_SHARD, D_MODEL)}


def _unit_tensors(u):
    l, j = divmod(u, 3)
    if j == 1:
        return [("in", l), ("br", l), ("out", l)]
    return [(k, 2 * l + j // 2) for k in ("gate", "up", "down")]


class GradStream:
    def __init__(self, chip, core):
        self.core = core
        self.place = jnp.concatenate([chip, core])
        self.held, self.flying = {}, []
        self.full = {k: lax.empty(shape, F32) for k, shape in GRAD_SLOTS.items()}

    def put(self, u, grads, after=()):
        self.held[u] = grads
        si = len(self.flying)
        units = REDUCE_STAGES[si]
        if not all(v in self.held for v in units):
            return jnp.zeros((1, 1), F32)
        gs = [g for v in units for g in self.held[v]]

        @_plan(0, len(gs))
        def swap_halves(ins, outs, x, y, c):
            sends = [(_half(g, 1, 1 - c), o, (x, y, 1 - c)) for g, o in zip(ins, outs)]
            return [], sends, list(outs)

        half_shapes = [jax.ShapeDtypeStruct((N_CHIPS, g.shape[1] // 2, g.shape[2]), BF16) for g in gs]
        landed = exchange(f"reduce_swap_{si}", gs + list(after), half_shapes, {}, swap_halves)
        sums = [_add_halves(g, la, self.core) for g, la in zip(gs, landed)]
        landing = [lax.empty((3,) + s.shape[1:], BF16) for s in sums]
        send, recv, bufs, token = split_start(f"reduce_start_{si}", sums + landing, (), 3 * len(sums), _reduce_copies)
        self.flying.append((send, recv, bufs, [t for v in units for t in _unit_tensors(v)]))
        return token[0:1, 0:1]

    def finish(self, after):
        for si, (send, recv, bufs, tensors) in enumerate(self.flying):
            bufs = split_wait(f"reduce_wait_{si}", bufs, send, recv, tuple(after), _reduce_copies)
            n = len(tensors)
            for (name, slot), s, land in zip(tensors, bufs[:n], bufs[n:]):
                self.full[name] = _add_chips(s, land, self.place, self.full[name], slot)
        names = list(self.full)

        @_plan(0, len(names))
        def share_halves(ins, outs, x, y, c):
            sends = [(_half(o, 1, c), _half(o, 1, c), (x, y, 1 - c)) for o in outs]
            return [], sends, [_half(o, 1, 1 - c) for o in outs]

        shapes = [jax.ShapeDtypeStruct(self.full[k].shape, F32) for k in names]
        out = exchange("reduce_share_halves", [self.full[k] for k in names], shapes, {k: k for k in range(len(names))},
                       share_halves)
        return dict(zip(names, out))


def _add_halves(g, landed, core):
    _, rh, cols = landed.shape
    tr = _row_tile(rh, cols)
    per_half = rh // tr

    def body(core_ref, g_ref, la_ref, o_ref):
        del core_ref
        o_ref[...] = (g_ref[...].astype(F32) + la_ref[...].astype(F32)).astype(BF16)

    blk = (None, tr, cols)
    return pl.pallas_call(
        body, out_shape=jax.ShapeDtypeStruct(landed.shape, BF16),
        grid_spec=pltpu.PrefetchScalarGridSpec(
            num_scalar_prefetch=1, grid=(N_CHIPS, per_half),
            in_specs=[pl.BlockSpec(blk, lambda j, r, core_ref: (j, core_ref[0] * per_half + r, 0)),
                      pl.BlockSpec(blk, lambda j, r, core_ref: (j, r, 0))],
            out_specs=pl.BlockSpec(blk, lambda j, r, core_ref: (j, r, 0))),
        name="reduce_add_halves", compiler_params=_params(("parallel", "parallel")),
    )(core, g, landed)


def _add_chips(sums, landed, place, full, slot):
    _, rh, cols = sums.shape
    tr = _row_tile(rh, cols)
    per_half = rh // tr

    def body(place_ref, s_ref, la_ref, full_in, o_ref):
        del place_ref, full_in
        o_ref[...] = ((s_ref[...].astype(F32) + la_ref[0].astype(F32)) + la_ref[1].astype(F32)) + la_ref[2].astype(F32)

    return pl.pallas_call(
        body, out_shape=jax.ShapeDtypeStruct(full.shape, F32),
        grid_spec=pltpu.PrefetchScalarGridSpec(
            num_scalar_prefetch=1, grid=(per_half,),
            in_specs=[pl.BlockSpec((None, tr, cols), lambda r, place_ref: (place_ref[0], r, 0)),
                      pl.BlockSpec((3, tr, cols), lambda r, place_ref: (0, r, 0)), ANY],
            out_specs=pl.BlockSpec((None, tr, cols), lambda r, place_ref: (slot, place_ref[1] * per_half + r, 0))),
        input_output_aliases={3: 0}, name="reduce_add_chips", compiler_params=_params(("parallel",)),
    )(place, sums, landed, full)


def sum_devices(parts):
    def body(p_ref, o_ref):
        acc = p_ref[0]
        for d in range(1, N_DEV):
            acc = acc + p_ref[d]
        o_ref[...] = acc

    return pl.pallas_call(body, out_shape=jax.ShapeDtypeStruct(parts.shape[1:], F32), name="sum_devices")(parts)


ADA_SHARD = 9 * D_MODEL // N_CHIPS
ADA_TILE = 768
ADA_ROWS = 16


def ada_forward(c_rows, w_ada, b_shard):
    def body(c_ref, w_ref, b_ref, o_ref):
        cv = c_ref[...]
        o_ref[...] = _dot((cv * _sigmoid(cv)).astype(BF16), w_ref[...].astype(BF16), NN) + b_ref[...]

    return pl.pallas_call(
        body, grid=(DEPTH, ADA_SHARD // ADA_TILE),
        in_specs=[pl.BlockSpec((ADA_ROWS, D_MODEL), lambda l, n: (0, 0)),
                  pl.BlockSpec((None, D_MODEL, ADA_TILE), lambda l, n: (l, 0, n)),
                  pl.BlockSpec((None, 1, ADA_TILE), lambda l, n: (l, 0, n))],
        out_specs=pl.BlockSpec((None, ADA_ROWS, ADA_TILE), lambda l, n: (l, 0, n)),
        out_shape=jax.ShapeDtypeStruct((DEPTH, ADA_ROWS, ADA_SHARD), F32),
        name="ada_forward", compiler_params=_params(("parallel", "parallel")),
    )(c_rows, w_ada, b_shard)


def ada_backward(c_rows, dmod_rows):
    def body(c_ref, d_ref, o_ref):
        cv = c_ref[...]
        o_ref[...] = _dot((cv * _sigmoid(cv)).astype(BF16), d_ref[...].astype(BF16), TN)

    return pl.pallas_call(
        body, grid=(DEPTH, ADA_SHARD // ADA_TILE),
        in_specs=[pl.BlockSpec((ADA_ROWS, D_MODEL), lambda l, n: (0, 0)),
                  pl.BlockSpec((None, ADA_ROWS, ADA_TILE), lambda l, n: (l, 0, n))],
        out_specs=pl.BlockSpec((None, D_MODEL, ADA_TILE), lambda l, n: (l, 0, n)),
        out_shape=jax.ShapeDtypeStruct((DEPTH, D_MODEL, ADA_SHARD), F32),
        name="ada_backward", compiler_params=_params(("parallel", "parallel")),
    )(c_rows, dmod_rows)


def adamw(name, w, g, m, v):
    shape = w.shape
    cols = shape[-1]
    rows = w.size // cols
    tr = _row_tile(rows, cols) if rows % 16 == 0 else rows
    c1 = 1.0 / (1.0 - ADAM_B1 ** ADAM_STEP)
    c2 = 1.0 / (1.0 - ADAM_B2 ** ADAM_STEP)

    def body(w_ref, g_ref, m_ref, v_ref, go_ref, d_ref, mo_ref, vo_ref):
        gv = g_ref[...]
        mn = ADAM_B1 * m_ref[...] + (1.0 - ADAM_B1) * gv
        vn = ADAM_B2 * v_ref[...] + (1.0 - ADAM_B2) * (gv * gv)
        go_ref[...] = gv
        mo_ref[...] = mn
        vo_ref[...] = vn
        d_ref[...] = -ADAM_LR * ((mn * c1) / (jnp.sqrt(vn * c2) + ADAM_EPS) + ADAM_WD * w_ref[...])

    spec = pl.BlockSpec((tr, cols), lambda i: (i, 0))
    out = jax.ShapeDtypeStruct((rows, cols), F32)
    res = pl.pallas_call(
        body, grid=(rows // tr,), in_specs=[spec] * 4, out_specs=[spec] * 4, out_shape=[out] * 4,
        name=name, compiler_params=_params(("parallel",)),
    )(*[t.reshape(rows, cols) for t in (w, g, m, v)])
    return tuple(r.reshape(shape) for r in res)


def _pack(parts, rows):
    flat = jnp.concatenate([p.reshape(-1) for p in parts])
    return jnp.pad(flat, (0, rows * 128 - flat.size)).reshape(rows, 128)


def _unpack(flat, shapes):
    out, at = [], 0
    for s in shapes:
        n = math.prod(s)
        out.append(flat[at:at + n].reshape(s))
        at += n
    return out


def kernel(x, c, w_ada, b_ada, norm_gain, w_ffn_gate, w_ffn_up, w_ffn_down, w_in, w_br_sb, w_br_dil, w_br_swa, w_out, sinks, rel_bias, final_gain, loss_target, m_w_ada, m_b_ada, m_norm_gain, m_w_ffn_gate, m_w_ffn_up, m_w_ffn_down, m_w_in, m_w_br_sb, m_w_br_dil, m_w_br_swa, m_w_out, m_sinks, m_rel_bias, m_final_gain, v_w_ada, v_b_ada, v_norm_gain, v_w_ffn_gate, v_w_ffn_up, v_w_ffn_down, v_w_in, v_w_br_sb, v_w_br_dil, v_w_br_swa, v_w_out, v_sinks, v_rel_bias, v_final_gain):
    xi, yi, ci = _position()
    chip = 2 * xi + yi
    dev = 2 * chip + ci

    c_all = all_gather_small("gather_c", c.reshape(8, 128)).reshape(N_DEV, D_MODEL)
    c_rows = jnp.pad(c_all, ((0, ADA_ROWS - N_DEV), (0, 0)))
    b_shard = lax.dynamic_slice_in_dim(b_ada, chip * ADA_SHARD, ADA_SHARD, axis=1).reshape(DEPTH, 1, ADA_SHARD)
    mod_shard = ada_forward(c_rows, w_ada, b_shard)[:, :N_DEV]
    n_mod = DEPTH * N_DEV * ADA_SHARD
    gathered = all_gather_small("gather_mod", _pack([mod_shard, norm_gain], 304))[::2].reshape(N_CHIPS, -1)
    mod_all = gathered[:, :n_mod].reshape(N_CHIPS, DEPTH, N_DEV, ADA_SHARD)
    mod = lax.dynamic_index_in_dim(mod_all, dev, axis=2, keepdims=False)
    mod = mod.transpose(1, 0, 2).reshape(DEPTH, 3, 3, D_MODEL)
    gains = gathered[:, n_mod:n_mod + DEPTH * 3 * D_SHARD].reshape(N_CHIPS, DEPTH, 3, D_SHARD)
    gains = gains.transpose(1, 2, 0, 3).reshape(DEPTH, 3, D_MODEL)

    chip_i, core_i = chip.astype(jnp.int32).reshape(1), ci.astype(jnp.int32).reshape(1)
    w_br = jnp.concatenate([w_br_sb, w_br_dil, w_br_swa], axis=1)
    transposed = (3, 4)
    w_gate_t, w_up_t = jnp.swapaxes(w_ffn_gate, 2, 3), jnp.swapaxes(w_ffn_up, 2, 3)
    shards = []
    for l in range(DEPTH):
        ffn = [[(w_gate_t, (l, f)), (w_up_t, (l, f)), (w_ffn_down, (l, f))] for f in range(2)]
        shards += [ffn[0], [(w_in, (l,)), (w_br, (l,)), (w_out, (l,))], ffn[1]]
    weights_in = WeightStream(shards, chip_i, (gathered,))
    grads_out = GradStream(chip_i, core_i)

    loss, dx, dmod, dgains, dfinal, dsinks, drel, last_grads = device_step(
        x[0], loss_target[0], mod, gains, final_gain, sinks, rel_bias, weights_in.get, grads_out.put)

    small_shapes = [(DEPTH, 9 * D_MODEL), (DEPTH, 3, D_MODEL), (D_MODEL,), (DEPTH, H_SWA_Q), (N_BUCKETS, 12), (1,)]
    small_all = all_gather_small("gather_small_grads", _pack([dmod, dgains, dfinal, dsinks, drel, loss[0, 0:1]], 208))
    started = grads_out.put(0, last_grads, after=(small_all,))
    small_all = small_all + started
    g_b_ada, g_gain_full, g_final, g_sinks, g_rel, loss_sum = _unpack(sum_devices(small_all).reshape(-1), small_shapes)
    g_gain = lax.dynamic_slice_in_dim(g_gain_full, chip * D_SHARD, D_SHARD, axis=2)
    dmod_all = small_all.reshape(N_DEV, -1)[:, :DEPTH * 9 * D_MODEL].reshape(N_DEV, DEPTH, 9 * D_MODEL)
    dmod_rows = lax.dynamic_slice_in_dim(dmod_all, chip * ADA_SHARD, ADA_SHARD, axis=2).transpose(1, 0, 2)
    g_w_ada = ada_backward(c_rows, jnp.pad(dmod_rows, ((0, 0), (0, ADA_ROWS - N_DEV), (0, 0))))

    weights = [w_ada, b_ada, norm_gain, w_ffn_gate, w_ffn_up, w_ffn_down, w_in, w_br_sb, w_br_dil, w_br_swa, w_out,
               sinks, rel_bias, final_gain]
    ms = [m_w_ada, m_b_ada, m_norm_gain, m_w_ffn_gate, m_w_ffn_up, m_w_ffn_down, m_w_in, m_w_br_sb, m_w_br_dil,
          m_w_br_swa, m_w_out, m_sinks, m_rel_bias, m_final_gain]
    vs = [v_w_ada, v_b_ada, v_norm_gain, v_w_ffn_gate, v_w_ffn_up, v_w_ffn_down, v_w_in, v_w_br_sb, v_w_br_dil,
          v_w_br_swa, v_w_out, v_sinks, v_rel_bias, v_final_gain]
    grads = [g_w_ada, g_b_ada, g_gain] + [None] * 8 + [g_sinks, g_rel, g_final]

    small = (1, 2, 11, 12, 13)
    deltas, new_ms, new_vs = [None] * 14, [None] * 14, [None] * 14
    _, deltas[0], new_ms[0], new_vs[0] = adamw("adamw_0", weights[0], grads[0], ms[0], vs[0])
    shapes = [weights[k].shape for k in small]
    packed = [_pack([t[k] for k in small], 168) for t in (weights, grads, ms, vs)]
    for dst, res in zip((deltas, new_ms, new_vs), adamw("adamw_small", *packed)[1:]):
        for k, t in zip(small, _unpack(res.reshape(-1), shapes)):
            dst[k] = t

    g = grads_out.finish((dx, deltas[0], deltas[1]))
    g_br = g["br"]
    grads[3:11] = [g["gate"].reshape(w_gate_t.shape), g["up"].reshape(w_up_t.shape),
                   g["down"].reshape(w_ffn_down.shape), g["in"], g_br[:, 0:256], g_br[:, 256:384], g_br[:, 384:768],
                   g["out"]]
    for k in range(3, 11):
        state = [weights[k], ms[k], vs[k]]
        if k in transposed:
            state = [jnp.swapaxes(t, 2, 3) for t in state]
        out = adamw(f"adamw_{k}", state[0], grads[k], state[1], state[2])
        if k in transposed:
            out = [jnp.swapaxes(t, 2, 3) for t in out]
        grads[k], deltas[k], new_ms[k], new_vs[k] = out
    return (loss_sum[0], dx[None], *grads, *deltas, *new_ms, *new_vs)
```

```python
import functools
import math

import jax
import jax.numpy as jnp
from jax import lax
from jax.experimental import pallas as pl
from jax.experimental.pallas import tpu as pltpu

F32 = jnp.float32
BF16 = jnp.bfloat16

D_MODEL = 1024
SEQ = 2048
DEPTH = 2
HEAD_DIM = 64
BLK = 128
H_SB = 4
DIL_PATTERNS = ((128, 1), (512, 4), (2048, 16))
H_PER_DIL = 2
H_DIL = 6
H_SWA_Q = 6
H_SWA_KV = 2
SWA_WINDOW = 128
N_BUCKETS = 32
MAX_REL_DIST = 2048
D_FF = 2816
RMS_EPS = 1e-6
N_CHIPS = 4
N_DEV = 8
FF_SHARD = D_FF // N_CHIPS
D_QKV = 2560
D_IN = D_QKV + 3 * D_MODEL
IN_SHARD = D_IN // N_CHIPS
D_SHARD = D_MODEL // N_CHIPS
BR_ROWS = 768
NEG = -1e30
QK_SCALE = HEAD_DIM ** -0.5

ADAM_LR = 0.001
ADAM_B1 = 0.9
ADAM_B2 = 0.999
ADAM_EPS = 1e-08
ADAM_WD = 0.01
ADAM_STEP = 10

VMEM_LIMIT = 56 * 1024 * 1024
ROW_TILE = 256
MM_TILE = 1024

NN = (((1,), (0,)), ((), ()))
NT = (((1,), (1,)), ((), ()))
TN = (((0,), (0,)), ((), ()))


def _params(sem=None):
    return pltpu.CompilerParams(dimension_semantics=sem, vmem_limit_bytes=VMEM_LIMIT)


def _dot(a, b, dims):
    return lax.dot_general(a, b, dims, preferred_element_type=F32)


def _sigmoid(x):
    return 1.0 / (1.0 + jnp.exp(-x))


def _matmul(name, grid, nk, k_axis, dims, n_pairs, in_specs, out_specs, out_shape, acc_shape, epilogue,
            operands, sem, aliases=None, prologue=None):
    n_in = len(in_specs)
    n_out = len(out_specs)

    def partial(ins):
        tot = None
        for p in range(n_pairs):
            a = ins[2 * p][...]
            if prologue is not None:
                a = prologue(p, a, ins)
            d = _dot(a, ins[2 * p + 1][...], dims)
            tot = d if tot is None else tot + d
        return tot

    def body(*refs):
        ins, outs = refs[:n_in], refs[n_in:n_in + n_out]
        ids = tuple(pl.program_id(a) for a in range(len(grid)))
        if nk == 1:
            epilogue(partial(ins), ins, outs, ids)
            return
        acc = refs[n_in + n_out]
        k = ids[k_axis]

        @pl.when(k == 0)
        def _():
            acc[...] = partial(ins)

        @pl.when(k > 0)
        def _():
            acc[...] += partial(ins)

        @pl.when(k == nk - 1)
        def _():
            epilogue(acc[...], ins, outs, ids)

    return pl.pallas_call(
        body, grid=grid, in_specs=in_specs, out_specs=out_specs, out_shape=out_shape,
        scratch_shapes=[] if nk == 1 else [pltpu.VMEM(acc_shape, F32)],
        input_output_aliases=aliases or {}, name=name, compiler_params=_params(sem),
    )(*operands)


def _row_spec(width=D_MODEL):
    return pl.BlockSpec((ROW_TILE, width), lambda i: (i, 0))


def _vec_spec(rows=1, width=D_MODEL):
    return pl.BlockSpec((rows, width), lambda i: (0, 0))


def prenorm(x, gain, scale, shift):
    def body(x_ref, g_ref, sc_ref, sh_ref, h_ref):
        xv = x_ref[...]
        r = lax.rsqrt(jnp.mean(xv * xv, axis=-1, keepdims=True) + RMS_EPS)
        h_ref[...] = (((xv * r) * g_ref[...]) * (1.0 + sc_ref[...]) + sh_ref[...]).astype(BF16)

    return pl.pallas_call(
        body, grid=(SEQ // ROW_TILE,), in_specs=[_row_spec(), _vec_spec(), _vec_spec(), _vec_spec()],
        out_specs=_row_spec(), out_shape=jax.ShapeDtypeStruct((SEQ, D_MODEL), BF16),
        name="prenorm", compiler_params=_params(("parallel",)),
    )(x, gain, scale, shift)


def resid_bwd(dxo, f, coef, mult):
    def body(dx_ref, f_ref, c_ref, df_ref, dc_ref):
        dx = dx_ref[...]
        df_ref[...] = (dx * (mult * c_ref[...])).astype(BF16)
        part = mult * jnp.sum(dx * f_ref[...], axis=0, keepdims=True)

        @pl.when(pl.program_id(0) == 0)
        def _():
            dc_ref[...] = jnp.zeros_like(dc_ref)

        dc_ref[0:1, :] += part

    return pl.pallas_call(
        body, grid=(SEQ // ROW_TILE,), in_specs=[_row_spec(), _row_spec(), _vec_spec()],
        out_specs=[_row_spec(), _vec_spec(8)],
        out_shape=[jax.ShapeDtypeStruct((SEQ, D_MODEL), BF16), jax.ShapeDtypeStruct((8, D_MODEL), F32)],
        name="resid_bwd", compiler_params=_params(("arbitrary",)),
    )(dxo, f, coef)


def final_loss(x, gain, target):
    def body(x_ref, g_ref, t_ref, loss_ref, dx_ref, dg_ref):
        xv = x_ref[...]
        g = g_ref[...]
        r = lax.rsqrt(jnp.mean(xv * xv, axis=-1, keepdims=True) + RMS_EPS)
        xh = xv * r
        e = xh * g - t_ref[...]
        part = 0.5 * jnp.sum(jnp.mean(e * e, axis=-1, keepdims=True), axis=0, keepdims=True)
        dy = e * (1.0 / D_MODEL)
        dyg = dy * g
        dx_ref[...] = r * (dyg - xh * jnp.mean(dyg * xh, axis=-1, keepdims=True))

        @pl.when(pl.program_id(0) == 0)
        def _():
            loss_ref[...] = jnp.zeros_like(loss_ref)
            dg_ref[...] = jnp.zeros_like(dg_ref)

        loss_ref[...] += jnp.broadcast_to(part, loss_ref.shape)
        dg_ref[0:1, :] += jnp.sum(dy * xh, axis=0, keepdims=True)

    return pl.pallas_call(
        body, grid=(SEQ // ROW_TILE,), in_specs=[_row_spec(), _vec_spec(), _row_spec()],
        out_specs=[_vec_spec(8, 128), _row_spec(), _vec_spec(8)],
        out_shape=[jax.ShapeDtypeStruct((8, 128), F32), jax.ShapeDtypeStruct((SEQ, D_MODEL), F32),
                   jax.ShapeDtypeStruct((8, D_MODEL), F32)],
        name="final_loss", compiler_params=_params(("arbitrary",)),
    )(x, gain, target)


def _prenorm_bwd_epilogue(dh, x_ref, dxo_ref, g_ref, sc_ref, dx_ref, stats_ref, first):
    xv = x_ref[...]
    g = g_ref[...]
    r = lax.rsqrt(jnp.mean(xv * xv, axis=-1, keepdims=True) + RMS_EPS)
    xh = xv * r
    dn = dh * (1.0 + sc_ref[...])
    dxh = dn * g
    dx_ref[...] = dxo_ref[...] + r * (dxh - xh * jnp.mean(dxh * xh, axis=-1, keepdims=True))

    @pl.when(first)
    def _():
        stats_ref[...] = jnp.zeros_like(stats_ref)

    stats_ref[0:1, :] += jnp.sum(dh, axis=0, keepdims=True)
    stats_ref[1:2, :] += jnp.sum(dh * (xh * g), axis=0, keepdims=True)
    stats_ref[2:3, :] += jnp.sum(dn * xh, axis=0, keepdims=True)


def ffn_up(h, wg_all, wu_all):
    def body(h_ref, wg_ref, wu_ref, a_ref, b_ref, s_ref):
        hv = h_ref[...]
        a = _dot(hv, wg_ref[...], NT)
        b = _dot(hv, wu_ref[...], NT)
        a_ref[...] = a
        b_ref[...] = b
        s_ref[...] = (a * _sigmoid(a) * b).astype(BF16)

    w_spec = pl.BlockSpec((None, FF_SHARD, D_MODEL), lambda j, i: (j, 0, 0))
    o_spec = pl.BlockSpec((None, MM_TILE, FF_SHARD), lambda j, i: (j, i, 0))
    hid = (N_CHIPS, SEQ, FF_SHARD)
    return pl.pallas_call(
        body, grid=(N_CHIPS, SEQ // MM_TILE),
        in_specs=[pl.BlockSpec((MM_TILE, D_MODEL), lambda j, i: (i, 0)), w_spec, w_spec],
        out_specs=[o_spec, o_spec, o_spec],
        out_shape=[jax.ShapeDtypeStruct(hid, F32), jax.ShapeDtypeStruct(hid, F32), jax.ShapeDtypeStruct(hid, BF16)],
        name="ffn_up", compiler_params=_params(("parallel", "parallel")),
    )(h, wg_all, wu_all)


def matmul_residual(name, a, a_spec, w_all, w_spec, x, coef, mult):
    def epilogue(acc, ins, outs, ids):
        outs[0][...] = acc
        outs[1][...] = ins[2][...] + (mult * ins[3][...]) * acc

    row = pl.BlockSpec((MM_TILE, D_MODEL), lambda i, j: (i, 0))
    return _matmul(
        name, (SEQ // MM_TILE, N_CHIPS), N_CHIPS, 1, NN, 1,
        [a_spec, w_spec, row, pl.BlockSpec((1, D_MODEL), lambda i, j: (0, 0))], [row, row],
        [jax.ShapeDtypeStruct((SEQ, D_MODEL), F32)] * 2, (MM_TILE, D_MODEL), epilogue,
        (a, w_all, x, coef), ("parallel", "arbitrary"))


def ffn_down(s, wd_all, x, gate):
    return matmul_residual(
        "ffn_down", s, pl.BlockSpec((None, MM_TILE, FF_SHARD), lambda i, j: (j, i, 0)),
        wd_all, pl.BlockSpec((None, FF_SHARD, D_MODEL), lambda i, j: (j, 0, 0)), x, gate, 0.5)


def ffn_bwd_hidden(df, wd_all, a, b):
    def epilogue(ds, ins, outs, ids):
        av, bv = ins[2][...], ins[3][...]
        sig = _sigmoid(av)
        outs[0][...] = (ds * bv * (sig * (1.0 + av * (1.0 - sig)))).astype(BF16)
        outs[1][...] = (ds * (av * sig)).astype(BF16)

    hid_spec = pl.BlockSpec((None, MM_TILE, FF_SHARD), lambda j, i: (j, i, 0))
    hid = jax.ShapeDtypeStruct((N_CHIPS, SEQ, FF_SHARD), BF16)
    return _matmul(
        "ffn_bwd_hidden", (N_CHIPS, SEQ // MM_TILE), 1, None, NT, 1,
        [pl.BlockSpec((MM_TILE, D_MODEL), lambda j, i: (i, 0)),
         pl.BlockSpec((None, FF_SHARD, D_MODEL), lambda j, i: (j, 0, 0)), hid_spec, hid_spec],
        [hid_spec, hid_spec], [hid, hid], None, epilogue, (df, wd_all, a, b), ("parallel", "parallel"))


def grad_weight(name, lhs, lhs_spec, rhs, rhs_spec, shape):
    def epilogue(acc, ins, outs, ids):
        outs[0][...] = acc.astype(BF16)

    return _matmul(
        name, (N_CHIPS, SEQ // MM_TILE), SEQ // MM_TILE, 1, TN, 1,
        [lhs_spec, rhs_spec], [pl.BlockSpec((None,) + shape, lambda j, k: (j, 0, 0))],
        [jax.ShapeDtypeStruct((N_CHIPS,) + shape, BF16)], shape, epilogue, (lhs, rhs), ("parallel", "arbitrary"))[0]


def ffn_grad_weights(h, s, df, da, db):
    tok = pl.BlockSpec((MM_TILE, D_MODEL), lambda j, k: (k, 0))
    hid = pl.BlockSpec((None, MM_TILE, FF_SHARD), lambda j, k: (j, k, 0))
    return (grad_weight("grad_w_gate", da, hid, h, tok, (FF_SHARD, D_MODEL)),
            grad_weight("grad_w_up", db, hid, h, tok, (FF_SHARD, D_MODEL)),
            grad_weight("grad_w_down", s, hid, df, tok, (FF_SHARD, D_MODEL)))


def matmul_prenorm_bwd(name, dims, pairs, pair_specs, x, dxo, gain, scale):
    n = len(pairs)

    def epilogue(dh, ins, outs, ids):
        _prenorm_bwd_epilogue(dh, ins[n], ins[n + 1], ins[n + 2], ins[n + 3], outs[0], outs[1], ids[0] == 0)

    row = pl.BlockSpec((MM_TILE, D_MODEL), lambda i, j: (i, 0))
    vec = pl.BlockSpec((1, D_MODEL), lambda i, j: (0, 0))
    return _matmul(
        name, (SEQ // MM_TILE, N_CHIPS), N_CHIPS, 1, dims, len(pairs) // 2,
        list(pair_specs) + [row, row, vec, vec], [row, pl.BlockSpec((8, D_MODEL), lambda i, j: (0, 0))],
        [jax.ShapeDtypeStruct((SEQ, D_MODEL), F32), jax.ShapeDtypeStruct((8, D_MODEL), F32)],
        (MM_TILE, D_MODEL), epilogue, tuple(pairs) + (x, dxo, gain, scale), ("arbitrary", "arbitrary"))


def ffn_bwd_input(da, db, wg_all, wu_all, x, dxo, gain, scale):
    hid = pl.BlockSpec((None, MM_TILE, FF_SHARD), lambda i, j: (j, i, 0))
    w = pl.BlockSpec((None, FF_SHARD, D_MODEL), lambda i, j: (j, 0, 0))
    return matmul_prenorm_bwd("ffn_bwd_input", NN, (da, wg_all, db, wu_all), (hid, w, hid, w), x, dxo, gain, scale)


def in_proj(h, w_all):
    def epilogue(acc, ins, outs, ids):
        outs[0][...] = acc

    return _matmul(
        "in_proj", (N_CHIPS, SEQ // MM_TILE), 1, None, NN, 1,
        [pl.BlockSpec((MM_TILE, D_MODEL), lambda j, i: (i, 0)),
         pl.BlockSpec((None, D_MODEL, IN_SHARD), lambda j, i: (j, 0, 0))],
        [pl.BlockSpec((MM_TILE, IN_SHARD), lambda j, i: (i, j))], [jax.ShapeDtypeStruct((SEQ, D_IN), F32)],
        None, epilogue, (h, w_all), ("parallel", "parallel"))[0]


_GATE_BLOCK0 = D_QKV // D_SHARD


def _branch_products(o, w_ref):
    ob = o.astype(BF16)
    return (_dot(ob[:, 0:256], w_ref[0:256, :], NN), _dot(ob[:, 256:384], w_ref[256:384, :], NN),
            _dot(ob[:, 384:768], w_ref[384:768, :], NN))


def merge_branches(o_cat, wbr_all, proj):
    def body(o_ref, w_ref, g0_ref, g1_ref, g2_ref, m_ref):
        u = _branch_products(o_ref[...], w_ref)
        m_ref[...] = (_sigmoid(g0_ref[...]) * u[0] + _sigmoid(g1_ref[...]) * u[1]
                      + _sigmoid(g2_ref[...]) * u[2]).astype(BF16)

    def gate_spec(b):
        return pl.BlockSpec((MM_TILE, D_SHARD), lambda i, j: (i, _GATE_BLOCK0 + 4 * b + j))

    return pl.pallas_call(
        body, grid=(SEQ // MM_TILE, N_CHIPS),
        in_specs=[pl.BlockSpec((MM_TILE, BR_ROWS), lambda i, j: (i, 0)),
                  pl.BlockSpec((None, BR_ROWS, D_SHARD), lambda i, j: (j, 0, 0)),
                  gate_spec(0), gate_spec(1), gate_spec(2)],
        out_specs=pl.BlockSpec((MM_TILE, D_SHARD), lambda i, j: (i, j)),
        out_shape=jax.ShapeDtypeStruct((SEQ, D_MODEL), BF16),
        name="merge_branches", compiler_params=_params(("parallel", "parallel")),
    )(o_cat, wbr_all, proj, proj, proj)


def out_proj(merged, wout_all, x, gate):
    return matmul_residual(
        "out_proj", merged, pl.BlockSpec((MM_TILE, D_SHARD), lambda i, j: (i, j)),
        wout_all, pl.BlockSpec((None, D_SHARD, D_MODEL), lambda i, j: (j, 0, 0)), x, gate, 1.0)


def merge_bwd(dmo, wout_all, o_cat, wbr_all, proj):
    def epilogue(dm, ins, outs, ids):
        u = _branch_products(ins[2][...], ins[3])
        for b in range(3):
            sig = _sigmoid(ins[4 + b][...])
            outs[b][...] = (dm * sig).astype(BF16)
            outs[3 + b][...] = (dm * u[b] * (sig * (1.0 - sig))).astype(BF16)

    def gate_spec(b):
        return pl.BlockSpec((MM_TILE, D_SHARD), lambda j, i: (i, _GATE_BLOCK0 + 4 * b + j))

    col = pl.BlockSpec((MM_TILE, D_SHARD), lambda j, i: (i, j))
    du = jax.ShapeDtypeStruct((SEQ, D_MODEL), BF16)
    return _matmul(
        "merge_bwd", (N_CHIPS, SEQ // MM_TILE), 1, None, NT, 1,
        [pl.BlockSpec((MM_TILE, D_MODEL), lambda j, i: (i, 0)),
         pl.BlockSpec((None, D_SHARD, D_MODEL), lambda j, i: (j, 0, 0)),
         pl.BlockSpec((MM_TILE, BR_ROWS), lambda j, i: (i, 0)),
         pl.BlockSpec((None, BR_ROWS, D_SHARD), lambda j, i: (j, 0, 0)),
         gate_spec(0), gate_spec(1), gate_spec(2)],
        [col] * 6, [du] * 6,
        None, epilogue, (dmo, wout_all, o_cat, wbr_all, proj, proj, proj), ("parallel", "parallel"))


def branch_bwd_input(du, wbr_all):
    def body(d0_ref, d1_ref, d2_ref, w_ref, o_ref, acc):
        j = pl.program_id(1)
        parts = (_dot(d0_ref[...], w_ref[0:256, :], NT), _dot(d1_ref[...], w_ref[256:384, :], NT),
                 _dot(d2_ref[...], w_ref[384:768, :], NT))

        @pl.when(j == 0)
        def _():
            acc[:, 0:256], acc[:, 256:384], acc[:, 384:768] = parts

        @pl.when(j > 0)
        def _():
            acc[:, 0:256] += parts[0]
            acc[:, 256:384] += parts[1]
            acc[:, 384:768] += parts[2]

        @pl.when(j == N_CHIPS - 1)
        def _():
            o_ref[...] = acc[...]

    col = pl.BlockSpec((MM_TILE, D_SHARD), lambda i, j: (i, j))
    return pl.pallas_call(
        body, grid=(SEQ // MM_TILE, N_CHIPS),
        in_specs=[col, col, col, pl.BlockSpec((None, BR_ROWS, D_SHARD), lambda i, j: (j, 0, 0))],
        out_specs=pl.BlockSpec((MM_TILE, BR_ROWS), lambda i, j: (i, 0)),
        out_shape=jax.ShapeDtypeStruct((SEQ, BR_ROWS), F32),
        scratch_shapes=[pltpu.VMEM((MM_TILE, BR_ROWS), F32)],
        name="branch_bwd_input", compiler_params=_params(("parallel", "arbitrary")),
    )(du[0], du[1], du[2], wbr_all)


def branch_grad_weights(o_cat, du):
    def body(o_ref, d0_ref, d1_ref, d2_ref, g_ref, acc):
        k = pl.program_id(1)
        ob = o_ref[...].astype(BF16)
        parts = (_dot(ob[:, 0:256], d0_ref[...], TN), _dot(ob[:, 256:384], d1_ref[...], TN),
                 _dot(ob[:, 384:768], d2_ref[...], TN))

        @pl.when(k == 0)
        def _():
            acc[0:256, :], acc[256:384, :], acc[384:768, :] = parts

        @pl.when(k > 0)
        def _():
            acc[0:256, :] += parts[0]
            acc[256:384, :] += parts[1]
            acc[384:768, :] += parts[2]

        @pl.when(k == SEQ // MM_TILE - 1)
        def _():
            g_ref[...] = acc[...].astype(BF16)

    col = pl.BlockSpec((MM_TILE, D_SHARD), lambda j, k: (k, j))
    return pl.pallas_call(
        body, grid=(N_CHIPS, SEQ // MM_TILE),
        in_specs=[pl.BlockSpec((MM_TILE, BR_ROWS), lambda j, k: (k, 0)), col, col, col],
        out_specs=pl.BlockSpec((None, BR_ROWS, D_SHARD), lambda j, k: (j, 0, 0)),
        out_shape=jax.ShapeDtypeStruct((N_CHIPS, BR_ROWS, D_SHARD), BF16),
        scratch_shapes=[pltpu.VMEM((BR_ROWS, D_SHARD), F32)],
        name="branch_grad_weights", compiler_params=_params(("parallel", "arbitrary")),
    )(o_cat, du[0], du[1], du[2])


def mixer_bwd_input(dproj, win_all, x, dxo, gain, scale):
    return matmul_prenorm_bwd(
        "mixer_bwd_input", NT, (dproj, win_all),
        (pl.BlockSpec((MM_TILE, IN_SHARD), lambda i, j: (i, j)),
         pl.BlockSpec((None, D_MODEL, IN_SHARD), lambda i, j: (j, 0, 0))), x, dxo, gain, scale)


BATCH_QK = (((2,), (2,)), ((0,), (0,)))
BATCH_PV = (((2,), (1,)), ((0,), (0,)))
BATCH_TN = (((1,), (1,)), ((0,), (0,)))


SB_WIDTH = H_SB * HEAD_DIM
SB_ROWS = H_SB * BLK


def _split_dot(v, tri):
    hi = v.astype(BF16)
    lo = (v - hi.astype(F32)).astype(BF16)
    return _dot(hi, tri, NN) + _dot(lo, tri, NN)


def _tri(cmp):
    return cmp(lax.broadcasted_iota(jnp.int32, (BLK, BLK), 0), lax.broadcasted_iota(jnp.int32, (BLK, BLK), 1)).astype(BF16)


def _head_masks():
    lane = lax.broadcasted_iota(jnp.int32, (1, SB_WIDTH), 1) // HEAD_DIM
    return [lane == h for h in range(H_SB)]


def _stack_heads(x, masks):
    return jnp.concatenate([jnp.where(m, x, jnp.zeros_like(x)) for m in masks], axis=0)


def _merge_heads(y, masks):
    out = jnp.where(masks[0], y[0:BLK], 0.0)
    for h in range(1, H_SB):
        out = jnp.where(masks[h], y[h * BLK:(h + 1) * BLK], out)
    return out


def _sb_scores(q4, k_ref, j, diagonal):
    rows = pl.ds(pl.multiple_of(j * BLK, BLK), BLK)
    z = _dot(q4, k_ref[rows, :], NT)
    log_fail = -(jnp.maximum(z, 0.0) + jnp.log(1.0 + jnp.exp(-jnp.abs(z))))
    log_hit = z + log_fail
    before = None
    if diagonal:
        tile = (SB_ROWS, BLK)
        before = lax.broadcasted_iota(jnp.int32, tile, 1) < (lax.broadcasted_iota(jnp.int32, tile, 0) & (BLK - 1))
        log_fail = jnp.where(before, log_fail, 0.0)
    return rows, before, log_fail, log_hit


def _keep(before, x):
    return x if before is None else jnp.where(before, x, 0.0)


def sb_forward(qkv):
    def body(q_ref, k_ref, v_ref, o_ref, tot_ref):
        i = pl.program_id(0)
        masks = _head_masks()
        q4 = _stack_heads(q_ref[...] * QK_SCALE, masks)
        later = _tri(lambda r, c: r > c)

        def tile(j, carry, diagonal):
            o, run = carry
            rows, before, log_fail, log_hit = _sb_scores(q4, k_ref, j, diagonal)
            between = _split_dot(log_fail, later) + run
            w = _keep(before, jnp.exp(log_hit + between))
            o = o + _merge_heads(_dot(w.astype(BF16), v_ref[rows, :], NN), masks)
            return o, run + jnp.sum(log_fail, axis=1, keepdims=True)

        carry = tile(i, (jnp.zeros((BLK, SB_WIDTH), F32), jnp.zeros((SB_ROWS, 1), F32)), True)
        o, run = lax.fori_loop(0, i, lambda t, c: tile(i - 1 - t, c, False), carry)
        o_ref[...] = o
        tot_ref[...] = run

    return pl.pallas_call(
        body, grid=(N_BLK,),
        in_specs=[pl.BlockSpec((BLK, SB_WIDTH), lambda i: (i, 0)), pl.BlockSpec((SEQ, SB_WIDTH), lambda i: (0, 1)),
                  pl.BlockSpec((SEQ, SB_WIDTH), lambda i: (0, 2))],
        out_specs=[pl.BlockSpec((BLK, SB_WIDTH), lambda i: (i, 0)), pl.BlockSpec((None, SB_ROWS, 1), lambda i: (i, 0, 0))],
        out_shape=[jax.ShapeDtypeStruct((SEQ, SB_WIDTH), F32), jax.ShapeDtypeStruct((N_BLK, SB_ROWS, 1), F32)],
        name="sb_forward", compiler_params=_params(("parallel",)),
    )(qkv, qkv, qkv)


def sb_backward(qkv, total, do_cat):
    def body(q_ref, k_ref, v_ref, tot_ref, do_ref, dq_ref, dk_ref, dv_ref):
        i = pl.program_id(0)

        @pl.when(i == 0)
        def _():
            dk_ref[...] = jnp.zeros_like(dk_ref)
            dv_ref[...] = jnp.zeros_like(dv_ref)

        masks = _head_masks()
        q4 = _stack_heads(q_ref[...] * QK_SCALE, masks)
        do4 = _stack_heads(do_ref[...].astype(BF16), masks)
        total_v = tot_ref[...]
        upto = _tri(lambda r, c: r <= c)
        earlier = _tri(lambda r, c: r < c)

        def tile(j, carry, diagonal):
            dq, seen, g_seen = carry
            rows, before, log_fail, log_hit = _sb_scores(q4, k_ref, j, diagonal)
            between = total_v - (seen + _split_dot(log_fail, upto))
            w = _keep(before, jnp.exp(log_hit + between))
            g = _dot(do4, v_ref[rows, :], NT) * w
            g_earlier = g_seen + _split_dot(g, earlier)
            sig = jnp.exp(log_hit)
            dz = _keep(before, g * (1.0 - sig) - g_earlier * sig).astype(BF16)
            dq = dq + _merge_heads(_dot(dz, k_ref[rows, :], NN), masks)
            dk_ref[rows, :] += _dot(dz, q4, TN)
            dv_ref[rows, :] += _dot(w.astype(BF16), do4, TN)
            return dq, seen + jnp.sum(log_fail, axis=1, keepdims=True), g_seen + jnp.sum(g, axis=1, keepdims=True)

        zero = jnp.zeros((SB_ROWS, 1), F32)
        carry = lax.fori_loop(0, i, lambda j, c: tile(j, c, False), (jnp.zeros((BLK, SB_WIDTH), F32), zero, zero))
        dq, _, _ = tile(i, carry, True)
        dq_ref[...] = dq * QK_SCALE

    blk = pl.BlockSpec((BLK, SB_WIDTH), lambda i: (i, 0))
    full = pl.BlockSpec((SEQ, SB_WIDTH), lambda i: (0, 0))
    shape = jax.ShapeDtypeStruct((SEQ, SB_WIDTH), F32)
    return pl.pallas_call(
        body, grid=(N_BLK,),
        in_specs=[blk, pl.BlockSpec((SEQ, SB_WIDTH), lambda i: (0, 1)), pl.BlockSpec((SEQ, SB_WIDTH), lambda i: (0, 2)),
                  pl.BlockSpec((None, SB_ROWS, 1), lambda i: (i, 0, 0)), blk],
        out_specs=[blk, full, full], out_shape=[shape, shape, shape],
        name="sb_backward", compiler_params=_params(("arbitrary",)),
    )(qkv, qkv, qkv, total, do_cat)


def _band_scores(q_ref, kp_ref, ko_ref, bias_ref, hb, prev_mask):
    b = pl.program_id(1)
    qs = q_ref[...]
    s_prev = _dot(qs, kp_ref[...], BATCH_QK) + bias_ref[:, :, 0:BLK]
    s_prev = jnp.concatenate(
        [jnp.where((b & prev_mask(pl.program_id(0) * hb + t)) != 0, s_prev[t:t + 1], NEG) for t in range(hb)], axis=0)
    s_own = _dot(qs, ko_ref[...], BATCH_QK) + bias_ref[:, :, BLK:2 * BLK]
    return qs, s_prev, s_own


def _band_specs(hb, rows, t_n):
    def q_spec(width):
        return pl.BlockSpec((hb, None, rows, width), lambda h, b: (h, b, 0, 0))

    own = pl.BlockSpec((hb, BLK, HEAD_DIM), lambda h, b: (h, b, 0))
    prev = pl.BlockSpec((hb, BLK, HEAD_DIM), lambda h, b: (h, jnp.maximum(b - 1, 0), 0))
    per_head = lambda r, width: pl.BlockSpec((hb, r, width), lambda h, b: (h, 0, 0))
    return q_spec, own, prev, per_head


def banded_forward(name, q, k, v, bias, sinks, hb, prev_mask):
    h_n, nb, rows, _ = q.shape

    def body(q_ref, kp_ref, ko_ref, vp_ref, vo_ref, bias_ref, sink_ref, o_ref, lse_ref):
        _, s_prev, s_own = _band_scores(q_ref, kp_ref, ko_ref, bias_ref, hb, prev_mask)
        sink = sink_ref[...]
        m = jnp.maximum(jnp.maximum(jnp.max(s_prev, axis=2, keepdims=True), jnp.max(s_own, axis=2, keepdims=True)), sink)
        p_prev = jnp.exp(s_prev - m)
        p_own = jnp.exp(s_own - m)
        denom = jnp.sum(p_prev, axis=2, keepdims=True) + jnp.sum(p_own, axis=2, keepdims=True) + jnp.exp(sink - m)
        o = _dot(p_prev.astype(BF16), vp_ref[...], BATCH_PV) + _dot(p_own.astype(BF16), vo_ref[...], BATCH_PV)
        o_ref[...] = o / denom
        lse_ref[...] = m + jnp.log(denom)

    q_spec, own, prev, per_head = _band_specs(hb, rows, k.shape[1])
    return pl.pallas_call(
        body, grid=(h_n // hb, nb),
        in_specs=[q_spec(HEAD_DIM), prev, own, prev, own, per_head(rows, 2 * BLK), per_head(rows, 1)],
        out_specs=[q_spec(HEAD_DIM), q_spec(1)],
        out_shape=[jax.ShapeDtypeStruct(q.shape, F32), jax.ShapeDtypeStruct((h_n, nb, rows, 1), F32)],
        name=name, compiler_params=_params(("parallel", "parallel")),
    )(q, k, k, v, v, bias, sinks)


def banded_backward(name, q, k, v, bias, sinks, o, lse, do, dlse, hb, prev_mask):
    h_n, nb, rows, _ = q.shape
    t_n = k.shape[1]

    def body(q_ref, kp_ref, ko_ref, vp_ref, vo_ref, bias_ref, sink_ref, o_ref, lse_ref, do_ref, dlse_ref,
             dq_ref, dk_ref, dv_ref, dbias_ref, dsink_ref):
        b = pl.program_id(1)

        @pl.when(b == 0)
        def _():
            dk_ref[...] = jnp.zeros_like(dk_ref)
            dv_ref[...] = jnp.zeros_like(dv_ref)
            dbias_ref[...] = jnp.zeros_like(dbias_ref)
            dsink_ref[...] = jnp.zeros_like(dsink_ref)

        qs, s_prev, s_own = _band_scores(q_ref, kp_ref, ko_ref, bias_ref, hb, prev_mask)
        lse_v = lse_ref[...]
        dov = do_ref[...]
        dob = dov.astype(BF16)
        shift = dlse_ref[...] - jnp.sum(dov * o_ref[...], axis=2, keepdims=True)
        p_prev = jnp.exp(s_prev - lse_v)
        p_own = jnp.exp(s_own - lse_v)
        ds_prev = p_prev * (_dot(dob, vp_ref[...], BATCH_QK) + shift)
        ds_own = p_own * (_dot(dob, vo_ref[...], BATCH_QK) + shift)
        dbias_ref[:, :, 0:BLK] += ds_prev
        dbias_ref[:, :, BLK:2 * BLK] += ds_own
        d_sink = jnp.exp(sink_ref[...] - lse_v) * shift
        for g in range(rows // BLK):
            dsink_ref[:, g:g + 1, :] += jnp.sum(d_sink[:, g * BLK:(g + 1) * BLK, :], axis=1, keepdims=True)
        ds_prev = ds_prev.astype(BF16)
        ds_own = ds_own.astype(BF16)
        dq_ref[...] = (_dot(ds_prev, kp_ref[...], BATCH_PV) + _dot(ds_own, ko_ref[...], BATCH_PV)) * QK_SCALE
        rows_prev = pl.ds(pl.multiple_of(jnp.maximum(b - 1, 0) * BLK, BLK), BLK)
        rows_own = pl.ds(pl.multiple_of(b * BLK, BLK), BLK)
        dk_ref[:, rows_prev, :] += _dot(ds_prev, qs, BATCH_TN)
        dk_ref[:, rows_own, :] += _dot(ds_own, qs, BATCH_TN)
        dv_ref[:, rows_prev, :] += _dot(p_prev.astype(BF16), dob, BATCH_TN)
        dv_ref[:, rows_own, :] += _dot(p_own.astype(BF16), dob, BATCH_TN)

    q_spec, own, prev, per_head = _band_specs(hb, rows, t_n)
    kv_full = per_head(t_n, HEAD_DIM)
    kv_shape = jax.ShapeDtypeStruct((h_n, t_n, HEAD_DIM), F32)
    return pl.pallas_call(
        body, grid=(h_n // hb, nb),
        in_specs=[q_spec(HEAD_DIM), prev, own, prev, own, per_head(rows, 2 * BLK), per_head(rows, 1),
                  q_spec(HEAD_DIM), q_spec(1), q_spec(HEAD_DIM), q_spec(1)],
        out_specs=[q_spec(HEAD_DIM), kv_full, kv_full, per_head(rows, 2 * BLK), per_head(rows // BLK, BLK)],
        out_shape=[jax.ShapeDtypeStruct(q.shape, F32), kv_shape, kv_shape,
                   jax.ShapeDtypeStruct((h_n, rows, 2 * BLK), F32), jax.ShapeDtypeStruct((h_n, rows // BLK, BLK), F32)],
        name=name, compiler_params=_params(("parallel", "arbitrary")),
    )(q, k, k, v, v, bias, sinks, o, lse, do, dlse)


def _dil_prev_mask(head):
    group = head // H_PER_DIL
    return jnp.where(group == 0, 15, jnp.where(group == 1, 3, 0))


def _swa_prev_mask(head):
    del head
    return 15


DIL_HEADS_PER_STEP = 3
SWA_GROUP = H_SWA_Q // H_SWA_KV
N_BLK = SEQ // BLK


def dilated_merge(o, lse):
    def body(o_ref, l_ref, out_ref):
        lv = l_ref[...]
        m = jnp.max(lv, axis=0, keepdims=True)
        e = jnp.exp(lv - m)
        alpha = e / jnp.sum(e, axis=0, keepdims=True)
        out_ref[...] = jnp.sum(alpha * o_ref[...], axis=0)

    return pl.pallas_call(
        body, grid=(H_PER_DIL, SEQ // ROW_TILE),
        in_specs=[pl.BlockSpec((3, None, ROW_TILE, HEAD_DIM), lambda h, i: (0, h, i, 0)),
                  pl.BlockSpec((3, None, ROW_TILE, 1), lambda h, i: (0, h, i, 0))],
        out_specs=pl.BlockSpec((None, ROW_TILE, HEAD_DIM), lambda h, i: (h, i, 0)),
        out_shape=jax.ShapeDtypeStruct((H_PER_DIL, SEQ, HEAD_DIM), F32),
        name="dilated_merge", compiler_params=_params(("parallel", "parallel")),
    )(o, lse)


def dilated_merge_bwd(o, lse, dout):
    def body(o_ref, l_ref, d_ref, do_ref, dl_ref):
        lv = l_ref[...]
        m = jnp.max(lv, axis=0, keepdims=True)
        e = jnp.exp(lv - m)
        alpha = e / jnp.sum(e, axis=0, keepdims=True)
        dv = d_ref[...][None]
        do_ref[...] = alpha * dv
        dalpha = jnp.sum(dv * o_ref[...], axis=-1, keepdims=True)
        dl_ref[...] = alpha * (dalpha - jnp.sum(alpha * dalpha, axis=0, keepdims=True))

    o_spec = pl.BlockSpec((3, None, ROW_TILE, HEAD_DIM), lambda h, i: (0, h, i, 0))
    l_spec = pl.BlockSpec((3, None, ROW_TILE, 1), lambda h, i: (0, h, i, 0))
    return pl.pallas_call(
        body, grid=(H_PER_DIL, SEQ // ROW_TILE),
        in_specs=[o_spec, l_spec, pl.BlockSpec((None, ROW_TILE, HEAD_DIM), lambda h, i: (h, i, 0))],
        out_specs=[o_spec, l_spec],
        out_shape=[jax.ShapeDtypeStruct(o.shape, F32), jax.ShapeDtypeStruct(lse.shape, F32)],
        name="dilated_merge_bwd", compiler_params=_params(("parallel", "parallel")),
    )(o, lse, dout)


def rel_bias_reduce(dbias0, dbias1, bucket):
    def body(d0_ref, d1_ref, b_ref, o_ref):
        dv, bv = d0_ref[...] + d1_ref[...], b_ref[...]
        lane = lax.broadcasted_iota(jnp.int32, (1, BLK), 1)
        acc = jnp.zeros((1, BLK), F32)
        for bkt in range(N_BUCKETS):
            acc = acc + jnp.where(lane == bkt, jnp.sum(jnp.where(bv == bkt, dv, 0.0)), 0.0)
        o_ref[...] = acc

    tile = pl.BlockSpec((None, BLK, 2 * BLK), lambda h: (h, 0, 0))
    return pl.pallas_call(
        body, grid=(dbias0.shape[0],), in_specs=[tile, tile, tile],
        out_specs=pl.BlockSpec((None, 1, BLK), lambda h: (h, 0, 0)),
        out_shape=jax.ShapeDtypeStruct((dbias0.shape[0], 1, BLK), F32),
        name="rel_bias_reduce", compiler_params=_params(("parallel",)),
    )(dbias0, dbias1, bucket)


def _heads(t):
    return t.reshape(SEQ, -1, HEAD_DIM).transpose(1, 0, 2)


def _unheads(t):
    return t.transpose(1, 0, 2).reshape(SEQ, -1)


def _dilate(t):
    parts = []
    for g, (_, d) in enumerate(DIL_PATTERNS):
        tg = t[:, 128 * g:128 * (g + 1)].reshape(SEQ // d, d, H_PER_DIL, HEAD_DIM).transpose(2, 1, 0, 3)
        parts.append(tg.reshape(H_PER_DIL, SEQ, HEAD_DIM))
    return jnp.concatenate(parts, axis=0)


def _undilate(t):
    outs = []
    for g, (_, d) in enumerate(DIL_PATTERNS):
        tg = t[2 * g:2 * g + 2].reshape(H_PER_DIL, d, SEQ // d, -1).transpose(0, 2, 1, 3)
        outs.append(tg.reshape(H_PER_DIL, SEQ, -1))
    return jnp.stack(outs)


def _redilate(t):
    parts = []
    for g, (_, d) in enumerate(DIL_PATTERNS):
        tg = t[g].reshape(H_PER_DIL, SEQ // d, d, -1).transpose(0, 2, 1, 3)
        parts.append(tg.reshape(H_PER_DIL, SEQ, -1))
    return jnp.concatenate(parts, axis=0)


def _t5_bucket(n):
    max_exact = N_BUCKETS // 2
    nf = jnp.maximum(n, 1).astype(F32)
    large = max_exact + (jnp.log(nf / max_exact) / math.log(MAX_REL_DIST / max_exact)
                         * (N_BUCKETS - max_exact)).astype(jnp.int32)
    large = jnp.minimum(large, N_BUCKETS - 1)
    return jnp.where(n < max_exact, n, large)


def band_tables(rel_bias):
    rel = jnp.arange(BLK)[:, None] + BLK - jnp.arange(2 * BLK)[None, :]
    buckets = []
    patterns = [(d, w // d) for w, d in DIL_PATTERNS for _ in range(H_PER_DIL)] + [(1, SWA_WINDOW - 1)] * H_SWA_Q
    for d, max_dist in patterns:
        band = (rel >= 0) & (rel <= max_dist)
        buckets.append(jnp.where(band, _t5_bucket(jnp.maximum(rel, 0) * d), -1))
    buckets = jnp.stack(buckets).astype(jnp.int32)

    def body(table_ref, b_ref, o_ref):
        h = pl.program_id(0)
        bv = b_ref[...]
        tile = jnp.full(bv.shape, NEG, F32)
        for bkt in range(N_BUCKETS):
            tile = jnp.where(bv == bkt, table_ref[h, bkt], tile)
        o_ref[...] = tile

    spec = pl.BlockSpec((None, BLK, 2 * BLK), lambda h: (h, 0, 0))
    tiles = pl.pallas_call(
        body, grid=(len(patterns),), in_specs=[pl.BlockSpec(memory_space=pltpu.SMEM), spec], out_specs=spec,
        out_shape=jax.ShapeDtypeStruct(buckets.shape, F32), name="band_tables", compiler_params=_params(("parallel",)),
    )(rel_bias.T, buckets)
    return tiles[:H_DIL], tiles[H_DIL:], buckets


def _swa_rows(t):
    t = t.reshape(N_BLK, BLK, H_SWA_KV, SWA_GROUP, HEAD_DIM).transpose(2, 0, 3, 1, 4)
    return t.reshape(H_SWA_KV, N_BLK, SWA_GROUP * BLK, HEAD_DIM)


def _swa_tokens(t):
    t = t.reshape(H_SWA_KV, N_BLK, SWA_GROUP, BLK, HEAD_DIM).transpose(1, 3, 0, 2, 4)
    return t.reshape(SEQ, H_SWA_Q * HEAD_DIM)


def _sink_rows(sinks):
    return jnp.broadcast_to(sinks.reshape(H_SWA_KV, SWA_GROUP, 1, 1), (H_SWA_KV, SWA_GROUP, BLK, 1)).reshape(
        H_SWA_KV, SWA_GROUP * BLK, 1)


def _no_sinks():
    return jnp.full((H_DIL, BLK, 1), NEG, F32)


def _vec(v):
    return v.reshape(1, D_MODEL)


def ffn_forward(x, gain, mod, w):
    h = prenorm(x, _vec(gain), _vec(mod[1]), _vec(mod[0]))
    a, b, s = ffn_up(h, w[0], w[1])
    f, xo = ffn_down(s, w[2], x, _vec(mod[2]))
    return xo, (x, h, a, b, s, f)


def ffn_backward(dxo, saved, gain, mod, w):
    x, h, a, b, s, f = saved
    df, dgate = resid_bwd(dxo, f, _vec(mod[2]), 0.5)
    da, db = ffn_bwd_hidden(df, w[2], a, b)
    grads = ffn_grad_weights(h, s, df, da, db)
    dx, stats = ffn_bwd_input(da, db, w[0], w[1], x, dxo, _vec(gain), _vec(mod[1]))
    return dx, jnp.stack([stats[0], stats[1], dgate[0]]), stats[2], grads


def mixer_forward(x, gain, mod, sinks, bias_dil, bias_swa, w):
    h = prenorm(x, _vec(gain), _vec(mod[1]), _vec(mod[0]))
    proj = in_proj(h, w[0])
    qkv = proj[:, :D_QKV].astype(BF16)
    q_dil, k_dil, v_dil = _dilate(qkv[:, 768:1152] * QK_SCALE), _dilate(qkv[:, 1152:1536]), _dilate(qkv[:, 1536:1920])
    q_swa, k_swa, v_swa = _swa_rows(qkv[:, 1920:2304] * QK_SCALE), _heads(qkv[:, 2304:2432]), _heads(qkv[:, 2432:2560])
    o_sb, total_sb = sb_forward(qkv)
    q_dil = q_dil.reshape(H_DIL, N_BLK, BLK, HEAD_DIM)
    o_dd, lse_dd = banded_forward("dilated_forward", q_dil, k_dil, v_dil, bias_dil, _no_sinks(), DIL_HEADS_PER_STEP,
                                  _dil_prev_mask)
    o_dt, lse_dt = _undilate(o_dd.reshape(H_DIL, SEQ, HEAD_DIM)), _undilate(lse_dd.reshape(H_DIL, SEQ, 1))
    o_dil = dilated_merge(o_dt, lse_dt)
    bias_swa = bias_swa.reshape(H_SWA_KV, SWA_GROUP * BLK, 2 * BLK)
    o_swa, lse_swa = banded_forward("swa_forward", q_swa, k_swa, v_swa, bias_swa, _sink_rows(sinks), 1, _swa_prev_mask)
    o_cat = jnp.concatenate([o_sb, _unheads(o_dil), _swa_tokens(o_swa)], axis=1)
    merged = merge_branches(o_cat, w[1], proj)
    mo, xo = out_proj(merged, w[2], x, _vec(mod[2]))
    saved = (x, h, proj, (qkv, total_sb), (q_dil, k_dil, v_dil, o_dd, lse_dd, o_dt, lse_dt),
             (q_swa, k_swa, v_swa, o_swa, lse_swa), o_cat, merged, mo)
    return xo, saved


def mixer_backward(dxo, saved, gain, mod, sinks, bias_dil, bias_swa, w):
    x, h, proj, sb, dil, swa, o_cat, merged, mo = saved
    dmo, dgate = resid_bwd(dxo, mo, _vec(mod[2]), 1.0)
    tok = pl.BlockSpec((MM_TILE, D_MODEL), lambda j, k: (k, 0))
    g_out = grad_weight("grad_w_out", merged, pl.BlockSpec((MM_TILE, D_SHARD), lambda j, k: (k, j)), dmo, tok,
                        (D_SHARD, D_MODEL))
    du0, du1, du2, dg0, dg1, dg2 = merge_bwd(dmo, w[2], o_cat, w[1], proj)
    du = (du0, du1, du2)
    do_cat = branch_bwd_input(du, w[1])
    g_br = branch_grad_weights(o_cat, du)

    qkv, total_sb = sb
    dq_sb, dk_sb, dv_sb = sb_backward(qkv, total_sb, do_cat)

    q_dil, k_dil, v_dil, o_dd, lse_dd, o_dt, lse_dt = dil
    do_dt, dlse_dt = dilated_merge_bwd(o_dt, lse_dt, _heads(do_cat[:, 256:384]))
    dq_dil, dk_dil, dv_dil, dbias_dil, _ = banded_backward(
        "dilated_backward", q_dil, k_dil, v_dil, bias_dil, _no_sinks(), o_dd, lse_dd,
        _redilate(do_dt).reshape(q_dil.shape), _redilate(dlse_dt).reshape(lse_dd.shape), DIL_HEADS_PER_STEP, _dil_prev_mask)

    q_swa, k_swa, v_swa, o_swa, lse_swa = swa
    bias_swa = bias_swa.reshape(H_SWA_KV, SWA_GROUP * BLK, 2 * BLK)
    dq_swa, dk_swa, dv_swa, dbias_swa, dsinks = banded_backward(
        "swa_backward", q_swa, k_swa, v_swa, bias_swa, _sink_rows(sinks), o_swa, lse_swa, _swa_rows(do_cat[:, 384:768]),
        jnp.zeros_like(lse_swa), 1, _swa_prev_mask)
    dbias_swa = dbias_swa.reshape(H_SWA_Q, BLK, 2 * BLK)

    def tokens(t):
        return _undilate(t).transpose(2, 0, 1, 3).reshape(SEQ, -1)

    dproj = jnp.concatenate(
        [dq_sb, dk_sb, dv_sb, tokens(dq_dil.reshape(H_DIL, SEQ, HEAD_DIM)), tokens(dk_dil),
         tokens(dv_dil), _swa_tokens(dq_swa), _unheads(dk_swa), _unheads(dv_swa)], axis=1).astype(BF16)
    dproj = jnp.concatenate([dproj, dg0, dg1, dg2], axis=1)
    g_in = grad_weight("grad_w_in", h, tok, dproj, pl.BlockSpec((MM_TILE, IN_SHARD), lambda j, k: (k, j)),
                       (D_MODEL, IN_SHARD))
    dx, stats = mixer_bwd_input(dproj, w[0], x, dxo, _vec(gain), _vec(mod[1]))
    dmod = jnp.stack([stats[0], stats[1], dgate[0]])
    dbias = jnp.concatenate([dbias_dil, dbias_swa], axis=0)
    return dx, dmod, stats[2], dbias, dsinks[:, :, 0].reshape(H_SWA_Q), (g_in, g_br, g_out)


N_UNITS = 3 * DEPTH


def device_step(x, target, mod, gains, final_gain, sinks, rel_bias, get_weights, put_grads):
    bias_dil, bias_swa, bucket = band_tables(rel_bias)
    saved, weights = [], []
    for u in range(N_UNITS):
        l, j = divmod(u, 3)
        w = get_weights(u, x)
        if j == 1:
            x, s = mixer_forward(x, gains[l, 1], mod[l, 1], sinks[l], bias_dil, bias_swa, w)
        else:
            x, s = ffn_forward(x, gains[l, j], mod[l, j], w)
        saved.append(s)
        weights.append(w)
    loss, dx, dfinal = final_loss(x, _vec(final_gain), target)

    dmod = [[None] * 3 for _ in range(DEPTH)]
    dgains = [[None] * 3 for _ in range(DEPTH)]
    dbias, dsinks = [None] * DEPTH, [None] * DEPTH
    zero = jnp.zeros((1, 1), F32)
    for u in reversed(range(N_UNITS)):
        l, j = divmod(u, 3)
        gain = gains[l, j] + zero[0]
        if j == 1:
            dx, dmod[l][j], dgains[l][j], dbias[l], dsinks[l], grads = mixer_backward(
                dx, saved[u], gain, mod[l, 1], sinks[l], bias_dil, bias_swa, weights[u])
        else:
            dx, dmod[l][j], dgains[l][j], grads = ffn_backward(dx, saved[u], gain, mod[l, j], weights[u])
        if u > 0:
            zero = put_grads(u, grads)
    drel = rel_bias_reduce(dbias[0], dbias[1], bucket)[:, 0, :N_BUCKETS].T
    stack2 = lambda t: jnp.stack([jnp.stack(r) for r in t])
    return loss, dx, stack2(dmod), stack2(dgains), dfinal[0], jnp.stack(dsinks), drel, grads


MESH = pl.DeviceIdType.MESH
CHIP_FLIPS = ((1, 0), (0, 1), (1, 1))
ANY = pl.BlockSpec(memory_space=pl.ANY)


def _position():
    return lax.axis_index("x"), lax.axis_index("y"), lax.axis_index("c")


def all_gather_small(name, piece):
    def body(x_ref, out_ref, send_sems, recv_sems, local_sem):
        x, y, c = _position()
        me, sibling = (x, y, c), (x, y, 1 - c)
        chips = [(x ^ fx, y ^ fy) for fx, fy in CHIP_FLIPS]

        def rows(px, py, pc):
            return out_ref.at[4 * px + 2 * py + pc]

        def copy(k, block, to, src=None):
            return pltpu.make_async_remote_copy(
                src_ref=rows(*block) if src is None else src, dst_ref=rows(*block),
                send_sem=send_sems.at[k], recv_sem=recv_sems.at[k], device_id=to, device_id_type=MESH)

        mine = pltpu.make_async_copy(x_ref, rows(*me), local_sem)
        mine.start()
        first = [copy(0, me, sibling, src=x_ref)]
        first += [copy(1 + j, me, (*chip, c), src=x_ref) for j, chip in enumerate(chips)]
        for cp in first:
            cp.start()
        passed = [copy(4 + j, (*chip, c), sibling) for j, chip in enumerate(chips)]
        for j, chip in enumerate(chips):
            copy(1 + j, (*chip, c), me).wait_recv()
            passed[j].start()
        copy(0, sibling, me).wait_recv()
        for j, chip in enumerate(chips):
            copy(4 + j, (*chip, 1 - c), me).wait_recv()
        for cp in first + passed:
            cp.wait_send()
        mine.wait()

    return pl.pallas_call(
        body, out_shape=jax.ShapeDtypeStruct((N_DEV,) + piece.shape, piece.dtype),
        in_specs=[pl.BlockSpec(memory_space=pltpu.VMEM)], out_specs=pl.BlockSpec(memory_space=pltpu.VMEM),
        scratch_shapes=[pltpu.SemaphoreType.DMA((7,)), pltpu.SemaphoreType.DMA((7,)), pltpu.SemaphoreType.DMA],
        name=name,
    )(piece)


def exchange(name, operands, out_shapes, aliases, plan):
    n_in, n_out = len(operands), len(out_shapes)

    def body(*refs):
        ins, outs = refs[:n_in], refs[n_in:n_in + n_out]
        send_sems, recv_sems, local_sems = refs[n_in + n_out:]
        x, y, c = _position()
        local, sends, recvs = plan(ins, outs, x, y, c)
        local = [pltpu.make_async_copy(s, d, local_sems.at[k]) for k, (s, d) in enumerate(local)]
        for cp in local:
            cp.start()
        remote = [pltpu.make_async_remote_copy(src_ref=s, dst_ref=d, send_sem=send_sems.at[k], recv_sem=recv_sems.at[k],
                                               device_id=dev, device_id_type=MESH)
                  for k, (s, d, dev) in enumerate(sends)]
        for cp in remote:
            cp.start()
        for k, r in enumerate(recvs):
            pltpu.make_async_remote_copy(src_ref=r, dst_ref=r, send_sem=send_sems.at[k], recv_sem=recv_sems.at[k],
                                         device_id=(x, y, c), device_id_type=MESH).wait_recv()
        for cp in remote:
            cp.wait_send()
        for cp in local:
            cp.wait()

    n_sends, n_local = plan.n_sends, max(plan.n_local, 1)
    return pl.pallas_call(
        body, out_shape=out_shapes, in_specs=[ANY] * n_in, out_specs=[ANY] * n_out,
        scratch_shapes=[pltpu.SemaphoreType.DMA((n_sends,)), pltpu.SemaphoreType.DMA((n_sends,)),
                        pltpu.SemaphoreType.DMA((n_local,))],
        input_output_aliases=aliases, name=name,
    )(*operands)


def _plan(n_local, n_sends):
    def wrap(fn):
        fn.n_local, fn.n_sends = n_local, n_sends
        return fn
    return wrap


def _half(ref, axis, c):
    rows = ref.shape[axis] // 2
    idx = [slice(None)] * len(ref.shape)
    idx[axis] = pl.ds(pl.multiple_of(c * rows, 16), rows)
    return ref.at[tuple(idx)]


HBM = pl.BlockSpec(memory_space=pltpu.HBM)
SEM = pl.BlockSpec(memory_space=pltpu.SEMAPHORE)
EFFECT = pltpu.SideEffectType.DATAFLOW_SIDE_EFFECTING


def split_start(name, bufs, extra, n_copies, describe):
    n = len(bufs)

    def body(*refs):
        send_sems, recv_sems = refs[n + len(extra)], refs[n + len(extra) + 1]
        x, y, c = _position()
        for k, (src, dst, _, peer) in enumerate(describe(refs[:n], x, y, c)):
            pltpu.make_async_remote_copy(src_ref=src, dst_ref=dst, send_sem=send_sems.at[k], recv_sem=recv_sems.at[k],
                                         device_id=peer, device_id_type=MESH).start()
        token = refs[-1]
        token[...] = jnp.zeros_like(token)

    out = pl.pallas_call(
        body, name=name,
        out_shape=(pltpu.SemaphoreType.DMA((n_copies,)), pltpu.SemaphoreType.DMA((n_copies,)),
                   *[pltpu.HBM(b.shape, b.dtype) for b in bufs], jax.ShapeDtypeStruct((8, 128), F32)),
        in_specs=[HBM] * n + [ANY] * len(extra),
        out_specs=(SEM, SEM, *[HBM] * n, pl.BlockSpec(memory_space=pltpu.VMEM)),
        input_output_aliases={k: 2 + k for k in range(n)},
        compiler_params=pltpu.CompilerParams(has_side_effects=EFFECT),
    )(*[pltpu.with_memory_space_constraint(b, pltpu.HBM) for b in bufs], *extra)
    return out[0], out[1], list(out[2:2 + n]), out[-1]


def split_wait(name, bufs, send_sems, recv_sems, after, describe):
    n = len(bufs)

    def body(*refs):
        send, recv = refs[n], refs[n + 1]
        x, y, c = _position()
        for k, (src, _, dst, peer) in enumerate(describe(refs[:n], x, y, c)):
            copy = pltpu.make_async_remote_copy(src_ref=src, dst_ref=dst, send_sem=send.at[k], recv_sem=recv.at[k],
                                                device_id=peer, device_id_type=MESH)
            copy.wait_send()
            copy.wait_recv()

    out = pl.pallas_call(
        body, name=name, out_shape=[pltpu.HBM(b.shape, b.dtype) for b in bufs],
        in_specs=[HBM] * n + [SEM, SEM] + [ANY] * len(after), out_specs=[HBM] * n,
        input_output_aliases={k: k for k in range(n)},
        compiler_params=pltpu.CompilerParams(has_side_effects=EFFECT),
    )(*bufs, send_sems, recv_sems, *after)
    return list(out)


def _row_tile(rows, cols):
    best = 16
    for t in range(16, rows + 1, 16):
        if rows % t == 0 and t * cols <= 256 * 1024:
            best = t
    return best


def cast_into_slot(name, param, index, chip):
    rows, cols = param.shape[-2:]
    tr = _row_tile(rows, cols)
    lead = (None,) * len(index)

    def body(chip_ref, s_ref, o_ref):
        del chip_ref
        o_ref[...] = s_ref[...].astype(BF16)

    return pl.pallas_call(
        body, out_shape=jax.ShapeDtypeStruct((N_CHIPS, rows, cols), BF16),
        grid_spec=pltpu.PrefetchScalarGridSpec(
            num_scalar_prefetch=1, grid=(rows // tr,),
            in_specs=[pl.BlockSpec(lead + (tr, cols), lambda r, chip_ref: index + (r, 0))],
            out_specs=pl.BlockSpec((None, tr, cols), lambda r, chip_ref: (chip_ref[0], r, 0))),
        name=name, compiler_params=_params(("parallel",)),
    )(chip, param)


GATHER_STAGES = ((0,), (1,), (2,), (3, 4, 5))
REDUCE_STAGES = ((5, 4, 3), (2,), (1,), (0,))


def _gather_copies(slots, x, y, c):
    me = 2 * x + y
    out = []
    for s in slots:
        for fx, fy in CHIP_FLIPS:
            mine = _half(s.at[me], 0, c)
            out.append((mine, mine, _half(s.at[2 * (x ^ fx) + (y ^ fy)], 0, c), (x ^ fx, y ^ fy, c)))
    return out


class WeightStream:
    def __init__(self, shards, chip, after=()):
        self.pending, self.ready = {}, {}
        token = tuple(after)
        for si, units in enumerate(GATHER_STAGES):
            slots = [cast_into_slot(f"cast_{u}_{t}", p, idx, chip) for u in units for t, (p, idx) in enumerate(shards[u])]
            send, recv, slots, tok = split_start(f"gather_start_{si}", slots, token, 3 * len(slots), _gather_copies)
            self.pending[si] = (send, recv, slots)
            token = (tok,)
        self.token = token

    def get(self, u, after):
        if u not in self.ready:
            si = next(k for k, units in enumerate(GATHER_STAGES) if u in units)
            send, recv, slots = self.pending.pop(si)
            slots = split_wait(f"gather_wait_{si}", slots, send, recv, (after,) + self.token, _gather_copies)
            self.token = ()

            @_plan(0, 3 * len(slots))
            def to_sibling(ins, outs, x, y, c):
                sends, recvs = [], []
                for o in outs:
                    for fx, fy in CHIP_FLIPS:
                        slab = o.at[2 * (x ^ fx) + (y ^ fy)]
                        sends.append((_half(slab, 0, c), _half(slab, 0, c), (x, y, 1 - c)))
                        recvs.append(_half(slab, 0, 1 - c))
                return [], sends, recvs

            shapes = [jax.ShapeDtypeStruct(s.shape, BF16) for s in slots]
            slots = exchange(f"gather_sibling_{si}", slots, shapes, {k: k for k in range(len(slots))}, to_sibling)
            for i, v in enumerate(GATHER_STAGES[si]):
                self.ready[v] = tuple(slots[3 * i:3 * i + 3])
        return self.ready[u]


def _reduce_copies(bufs, x, y, c):
    n = len(bufs) // 2
    out = []
    for s, land in zip(bufs[:n], bufs[n:]):
        for k, (fx, fy) in enumerate(CHIP_FLIPS):
            out.append((s.at[2 * (x ^ fx) + (y ^ fy)], land.at[k], land.at[k], (x ^ fx, y ^ fy, c)))
    return out


GRAD_SLOTS = {"gate": (2 * DEPTH, FF_SHARD, D_MODEL), "up": (2 * DEPTH, FF_SHARD, D_MODEL),
              "down": (2 * DEPTH, FF_SHARD, D_MODEL), "in": (DEPTH, D_MODEL, IN_SHARD),
              "br": (DEPTH, BR_ROWS, D_SHARD), "out": (DEPTH, D_SHARD, D_MODEL)}


def _unit_tensors(u):
    l, j = divmod(u, 3)
    if j == 1:
        return [("in", l), ("br", l), ("out", l)]
    return [(k, 2 * l + j // 2) for k in ("gate", "up", "down")]


class GradStream:
    def __init__(self, chip, core):
        self.core = core
        self.place = jnp.concatenate([chip, core])
        self.held, self.flying = {}, []
        self.full = {k: lax.empty(shape, F32) for k, shape in GRAD_SLOTS.items()}

    def put(self, u, grads, after=()):
        self.held[u] = grads
        si = len(self.flying)
        units = REDUCE_STAGES[si]
        if not all(v in self.held for v in units):
            return jnp.zeros((1, 1), F32)
        gs = [g for v in units for g in self.held[v]]

        @_plan(0, len(gs))
        def swap_halves(ins, outs, x, y, c):
            sends = [(_half(g, 1, 1 - c), o, (x, y, 1 - c)) for g, o in zip(ins, outs)]
            return [], sends, list(outs)

        half_shapes = [jax.ShapeDtypeStruct((N_CHIPS, g.shape[1] // 2, g.shape[2]), BF16) for g in gs]
        landed = exchange(f"reduce_swap_{si}", gs + list(after), half_shapes, {}, swap_halves)
        sums = [_add_halves(g, la, self.core) for g, la in zip(gs, landed)]
        landing = [lax.empty((3,) + s.shape[1:], BF16) for s in sums]
        send, recv, bufs, token = split_start(f"reduce_start_{si}", sums + landing, (), 3 * len(sums), _reduce_copies)
        self.flying.append((send, recv, bufs, [t for v in units for t in _unit_tensors(v)]))
        return token[0:1, 0:1]

    def finish(self, after):
        for si, (send, recv, bufs, tensors) in enumerate(self.flying):
            bufs = split_wait(f"reduce_wait_{si}", bufs, send, recv, tuple(after), _reduce_copies)
            n = len(tensors)
            for (name, slot), s, land in zip(tensors, bufs[:n], bufs[n:]):
                self.full[name] = _add_chips(s, land, self.place, self.full[name], slot)
        names = list(self.full)

        @_plan(0, len(names))
        def share_halves(ins, outs, x, y, c):
            sends = [(_half(o, 1, c), _half(o, 1, c), (x, y, 1 - c)) for o in outs]
            return [], sends, [_half(o, 1, 1 - c) for o in outs]

        shapes = [jax.ShapeDtypeStruct(self.full[k].shape, F32) for k in names]
        out = exchange("reduce_share_halves", [self.full[k] for k in names], shapes, {k: k for k in range(len(names))},
                       share_halves)
        return dict(zip(names, out))


def _add_halves(g, landed, core):
    _, rh, cols = landed.shape
    tr = _row_tile(rh, cols)
    per_half = rh // tr

    def body(core_ref, g_ref, la_ref, o_ref):
        del core_ref
        o_ref[...] = (g_ref[...].astype(F32) + la_ref[...].astype(F32)).astype(BF16)

    blk = (None, tr, cols)
    return pl.pallas_call(
        body, out_shape=jax.ShapeDtypeStruct(landed.shape, BF16),
        grid_spec=pltpu.PrefetchScalarGridSpec(
            num_scalar_prefetch=1, grid=(N_CHIPS, per_half),
            in_specs=[pl.BlockSpec(blk, lambda j, r, core_ref: (j, core_ref[0] * per_half + r, 0)),
                      pl.BlockSpec(blk, lambda j, r, core_ref: (j, r, 0))],
            out_specs=pl.BlockSpec(blk, lambda j, r, core_ref: (j, r, 0))),
        name="reduce_add_halves", compiler_params=_params(("parallel", "parallel")),
    )(core, g, landed)


def _add_chips(sums, landed, place, full, slot):
    _, rh, cols = sums.shape
    tr = _row_tile(rh, cols)
    per_half = rh // tr

    def body(place_ref, s_ref, la_ref, full_in, o_ref):
        del place_ref, full_in
        o_ref[...] = ((s_ref[...].astype(F32) + la_ref[0].astype(F32)) + la_ref[1].astype(F32)) + la_ref[2].astype(F32)

    return pl.pallas_call(
        body, out_shape=jax.ShapeDtypeStruct(full.shape, F32),
        grid_spec=pltpu.PrefetchScalarGridSpec(
            num_scalar_prefetch=1, grid=(per_half,),
            in_specs=[pl.BlockSpec((None, tr, cols), lambda r, place_ref: (place_ref[0], r, 0)),
                      pl.BlockSpec((3, tr, cols), lambda r, place_ref: (0, r, 0)), ANY],
            out_specs=pl.BlockSpec((None, tr, cols), lambda r, place_ref: (slot, place_ref[1] * per_half + r, 0))),
        input_output_aliases={3: 0}, name="reduce_add_chips", compiler_params=_params(("parallel",)),
    )(place, sums, landed, full)


def sum_devices(parts):
    def body(p_ref, o_ref):
        acc = p_ref[0]
        for d in range(1, N_DEV):
            acc = acc + p_ref[d]
        o_ref[...] = acc

    return pl.pallas_call(body, out_shape=jax.ShapeDtypeStruct(parts.shape[1:], F32), name="sum_devices")(parts)


ADA_SHARD = 9 * D_MODEL // N_CHIPS
ADA_TILE = 768
ADA_ROWS = 16


def ada_forward(c_rows, w_ada, b_shard):
    def body(c_ref, w_ref, b_ref, o_ref):
        cv = c_ref[...]
        o_ref[...] = _dot((cv * _sigmoid(cv)).astype(BF16), w_ref[...].astype(BF16), NN) + b_ref[...]

    return pl.pallas_call(
        body, grid=(DEPTH, ADA_SHARD // ADA_TILE),
        in_specs=[pl.BlockSpec((ADA_ROWS, D_MODEL), lambda l, n: (0, 0)),
                  pl.BlockSpec((None, D_MODEL, ADA_TILE), lambda l, n: (l, 0, n)),
                  pl.BlockSpec((None, 1, ADA_TILE), lambda l, n: (l, 0, n))],
        out_specs=pl.BlockSpec((None, ADA_ROWS, ADA_TILE), lambda l, n: (l, 0, n)),
        out_shape=jax.ShapeDtypeStruct((DEPTH, ADA_ROWS, ADA_SHARD), F32),
        name="ada_forward", compiler_params=_params(("parallel", "parallel")),
    )(c_rows, w_ada, b_shard)


def ada_backward(c_rows, dmod_rows):
    def body(c_ref, d_ref, o_ref):
        cv = c_ref[...]
        o_ref[...] = _dot((cv * _sigmoid(cv)).astype(BF16), d_ref[...].astype(BF16), TN)

    return pl.pallas_call(
        body, grid=(DEPTH, ADA_SHARD // ADA_TILE),
        in_specs=[pl.BlockSpec((ADA_ROWS, D_MODEL), lambda l, n: (0, 0)),
                  pl.BlockSpec((None, ADA_ROWS, ADA_TILE), lambda l, n: (l, 0, n))],
        out_specs=pl.BlockSpec((None, D_MODEL, ADA_TILE), lambda l, n: (l, 0, n)),
        out_shape=jax.ShapeDtypeStruct((DEPTH, D_MODEL, ADA_SHARD), F32),
        name="ada_backward", compiler_params=_params(("parallel", "parallel")),
    )(c_rows, dmod_rows)


def adamw(name, w, g, m, v):
    shape = w.shape
    cols = shape[-1]
    rows = w.size // cols
    tr = _row_tile(rows, cols) if rows % 16 == 0 else rows
    c1 = 1.0 / (1.0 - ADAM_B1 ** ADAM_STEP)
    c2 = 1.0 / (1.0 - ADAM_B2 ** ADAM_STEP)

    def body(w_ref, g_ref, m_ref, v_ref, go_ref, d_ref, mo_ref, vo_ref):
        gv = g_ref[...]
        mn = ADAM_B1 * m_ref[...] + (1.0 - ADAM_B1) * gv
        vn = ADAM_B2 * v_ref[...] + (1.0 - ADAM_B2) * (gv * gv)
        go_ref[...] = gv
        mo_ref[...] = mn
        vo_ref[...] = vn
        d_ref[...] = -ADAM_LR * ((mn * c1) / (jnp.sqrt(vn * c2) + ADAM_EPS) + ADAM_WD * w_ref[...])

    spec = pl.BlockSpec((tr, cols), lambda i: (i, 0))
    out = jax.ShapeDtypeStruct((rows, cols), F32)
    res = pl.pallas_call(
        body, grid=(rows // tr,), in_specs=[spec] * 4, out_specs=[spec] * 4, out_shape=[out] * 4,
        name=name, compiler_params=_params(("parallel",)),
    )(*[t.reshape(rows, cols) for t in (w, g, m, v)])
    return tuple(r.reshape(shape) for r in res)


def _pack(parts, rows):
    flat = jnp.concatenate([p.reshape(-1) for p in parts])
    return jnp.pad(flat, (0, rows * 128 - flat.size)).reshape(rows, 128)


def _unpack(flat, shapes):
    out, at = [], 0
    for s in shapes:
        n = math.prod(s)
        out.append(flat[at:at + n].reshape(s))
        at += n
    return out


def kernel(x, c, w_ada, b_ada, norm_gain, w_ffn_gate, w_ffn_up, w_ffn_down, w_in, w_br_sb, w_br_dil, w_br_swa, w_out, sinks, rel_bias, final_gain, loss_target, m_w_ada, m_b_ada, m_norm_gain, m_w_ffn_gate, m_w_ffn_up, m_w_ffn_down, m_w_in, m_w_br_sb, m_w_br_dil, m_w_br_swa, m_w_out, m_sinks, m_rel_bias, m_final_gain, v_w_ada, v_b_ada, v_norm_gain, v_w_ffn_gate, v_w_ffn_up, v_w_ffn_down, v_w_in, v_w_br_sb, v_w_br_dil, v_w_br_swa, v_w_out, v_sinks, v_rel_bias, v_final_gain):
    xi, yi, ci = _position()
    chip = 2 * xi + yi
    dev = 2 * chip + ci

    c_all = all_gather_small("gather_c", c.reshape(8, 128)).reshape(N_DEV, D_MODEL)
    c_rows = jnp.pad(c_all, ((0, ADA_ROWS - N_DEV), (0, 0)))
    b_shard = lax.dynamic_slice_in_dim(b_ada, chip * ADA_SHARD, ADA_SHARD, axis=1).reshape(DEPTH, 1, ADA_SHARD)
    mod_shard = ada_forward(c_rows, w_ada, b_shard)[:, :N_DEV]
    n_mod = DEPTH * N_DEV * ADA_SHARD
    gathered = all_gather_small("gather_mod", _pack([mod_shard, norm_gain], 304))[::2].reshape(N_CHIPS, -1)
    mod_all = gathered[:, :n_mod].reshape(N_CHIPS, DEPTH, N_DEV, ADA_SHARD)
    mod = lax.dynamic_index_in_dim(mod_all, dev, axis=2, keepdims=False)
    mod = mod.transpose(1, 0, 2).reshape(DEPTH, 3, 3, D_MODEL)
    gains = gathered[:, n_mod:n_mod + DEPTH * 3 * D_SHARD].reshape(N_CHIPS, DEPTH, 3, D_SHARD)
    gains = gains.transpose(1, 2, 0, 3).reshape(DEPTH, 3, D_MODEL)

    chip_i, core_i = chip.astype(jnp.int32).reshape(1), ci.astype(jnp.int32).reshape(1)
    w_br = jnp.concatenate([w_br_sb, w_br_dil, w_br_swa], axis=1)
    transposed = (3, 4)
    w_gate_t, w_up_t = jnp.swapaxes(w_ffn_gate, 2, 3), jnp.swapaxes(w_ffn_up, 2, 3)
    shards = []
    for l in range(DEPTH):
        ffn = [[(w_gate_t, (l, f)), (w_up_t, (l, f)), (w_ffn_down, (l, f))] for f in range(2)]
        shards += [ffn[0], [(w_in, (l,)), (w_br, (l,)), (w_out, (l,))], ffn[1]]
    weights_in = WeightStream(shards, chip_i, (gathered,))
    grads_out = GradStream(chip_i, core_i)

    loss, dx, dmod, dgains, dfinal, dsinks, drel, last_grads = device_step(
        x[0], loss_target[0], mod, gains, final_gain, sinks, rel_bias, weights_in.get, grads_out.put)

    small_shapes = [(DEPTH, 9 * D_MODEL), (DEPTH, 3, D_MODEL), (D_MODEL,), (DEPTH, H_SWA_Q), (N_BUCKETS, 12), (1,)]
    small_all = all_gather_small("gather_small_grads", _pack([dmod, dgains, dfinal, dsinks, drel, loss[0, 0:1]], 208))
    started = grads_out.put(0, last_grads, after=(small_all,))
    small_all = small_all + started
    g_b_ada, g_gain_full, g_final, g_sinks, g_rel, loss_sum = _unpack(sum_devices(small_all).reshape(-1), small_shapes)
    g_gain = lax.dynamic_slice_in_dim(g_gain_full, chip * D_SHARD, D_SHARD, axis=2)
    dmod_all = small_all.reshape(N_DEV, -1)[:, :DEPTH * 9 * D_MODEL].reshape(N_DEV, DEPTH, 9 * D_MODEL)
    dmod_rows = lax.dynamic_slice_in_dim(dmod_all, chip * ADA_SHARD, ADA_SHARD, axis=2).transpose(1, 0, 2)
    g_w_ada = ada_backward(c_rows, jnp.pad(dmod_rows, ((0, 0), (0, ADA_ROWS - N_DEV), (0, 0))))

    weights = [w_ada, b_ada, norm_gain, w_ffn_gate, w_ffn_up, w_ffn_down, w_in, w_br_sb, w_br_dil, w_br_swa, w_out,
               sinks, rel_bias, final_gain]
    ms = [m_w_ada, m_b_ada, m_norm_gain, m_w_ffn_gate, m_w_ffn_up, m_w_ffn_down, m_w_in, m_w_br_sb, m_w_br_dil,
          m_w_br_swa, m_w_out, m_sinks, m_rel_bias, m_final_gain]
    vs = [v_w_ada, v_b_ada, v_norm_gain, v_w_ffn_gate, v_w_ffn_up, v_w_ffn_down, v_w_in, v_w_br_sb, v_w_br_dil,
          v_w_br_swa, v_w_out, v_sinks, v_rel_bias, v_final_gain]
    grads = [g_w_ada, g_b_ada, g_gain] + [None] * 8 + [g_sinks, g_rel, g_final]

    small = (1, 2, 11, 12, 13)
    deltas, new_ms, new_vs = [None] * 14, [None] * 14, [None] * 14
    _, deltas[0], new_ms[0], new_vs[0] = adamw("adamw_0", weights[0], grads[0], ms[0], vs[0])
    shapes = [weights[k].shape for k in small]
    packed = [_pack([t[k] for k in small], 168) for t in (weights, grads, ms, vs)]
    for dst, res in zip((deltas, new_ms, new_vs), adamw("adamw_small", *packed)[1:]):
        for k, t in zip(small, _unpack(res.reshape(-1), shapes)):
            dst[k] = t

    g = grads_out.finish((dx, deltas[0], deltas[1]))
    g_br = g["br"]
    grads[3:11] = [g["gate"].reshape(w_gate_t.shape), g["up"].reshape(w_up_t.shape),
                   g["down"].reshape(w_ffn_down.shape), g["in"], g_br[:, 0:256], g_br[:, 256:384], g_br[:, 384:768],
                   g["out"]]
    for k in range(3, 11):
        state = [weights[k], ms[k], vs[k]]
        if k in transposed:
            state = [jnp.swapaxes(t, 2, 3) for t in state]
        out = adamw(f"adamw_{k}", state[0], grads[k], state[1], state[2])
        if k in transposed:
            out = [jnp.swapaxes(t, 2, 3) for t in out]
        grads[k], deltas[k], new_ms[k], new_vs[k] = out
    return (loss_sum[0], dx[None], *grads, *deltas, *new_ms, *new_vs)
```

```python
import functools
import math

import jax
import jax.numpy as jnp
from jax import lax
from jax.experimental import pallas as pl
from jax.experimental.pallas import tpu as pltpu

F32 = jnp.float32
BF16 = jnp.bfloat16

D_MODEL = 1024
SEQ = 2048
DEPTH = 2
HEAD_DIM = 64
BLK = 128
H_SB = 4
DIL_PATTERNS = ((128, 1), (512, 4), (2048, 16))
H_PER_DIL = 2
H_DIL = 6
H_SWA_Q = 6
H_SWA_KV = 2
SWA_WINDOW = 128
N_BUCKETS = 32
MAX_REL_DIST = 2048
D_FF = 2816
RMS_EPS = 1e-6
N_CHIPS = 4
N_DEV = 8
FF_SHARD = D_FF // N_CHIPS
D_QKV = 2560
D_IN = D_QKV + 3 * D_MODEL
IN_SHARD = D_IN // N_CHIPS
D_SHARD = D_MODEL // N_CHIPS
BR_ROWS = 768
NEG = -1e30
QK_SCALE = HEAD_DIM ** -0.5

ADAM_LR = 0.001
ADAM_B1 = 0.9
ADAM_B2 = 0.999
ADAM_EPS = 1e-08
ADAM_WD = 0.01
ADAM_STEP = 10

VMEM_LIMIT = 56 * 1024 * 1024
ROW_TILE = 256
MM_TILE = 1024

NN = (((1,), (0,)), ((), ()))
NT = (((1,), (1,)), ((), ()))
TN = (((0,), (0,)), ((), ()))


def _params(sem=None):
    return pltpu.CompilerParams(dimension_semantics=sem, vmem_limit_bytes=VMEM_LIMIT)


def _dot(a, b, dims):
    return lax.dot_general(a, b, dims, preferred_element_type=F32)


def _sigmoid(x):
    return 1.0 / (1.0 + jnp.exp(-x))


def _matmul(name, grid, nk, k_axis, dims, n_pairs, in_specs, out_specs, out_shape, acc_shape, epilogue,
            operands, sem, aliases=None, prologue=None):
    n_in = len(in_specs)
    n_out = len(out_specs)

    def partial(ins):
        tot = None
        for p in range(n_pairs):
            a = ins[2 * p][...]
            if prologue is not None:
                a = prologue(p, a, ins)
            d = _dot(a, ins[2 * p + 1][...], dims)
            tot = d if tot is None else tot + d
        return tot

    def body(*refs):
        ins, outs = refs[:n_in], refs[n_in:n_in + n_out]
        ids = tuple(pl.program_id(a) for a in range(len(grid)))
        if nk == 1:
            epilogue(partial(ins), ins, outs, ids)
            return
        acc = refs[n_in + n_out]
        k = ids[k_axis]

        @pl.when(k == 0)
        def _():
            acc[...] = partial(ins)

        @pl.when(k > 0)
        def _():
            acc[...] += partial(ins)

        @pl.when(k == nk - 1)
        def _():
            epilogue(acc[...], ins, outs, ids)

    return pl.pallas_call(
        body, grid=grid, in_specs=in_specs, out_specs=out_specs, out_shape=out_shape,
        scratch_shapes=[] if nk == 1 else [pltpu.VMEM(acc_shape, F32)],
        input_output_aliases=aliases or {}, name=name, compiler_params=_params(sem),
    )(*operands)


def _row_spec(width=D_MODEL):
    return pl.BlockSpec((ROW_TILE, width), lambda i: (i, 0))


def _vec_spec(rows=1, width=D_MODEL):
    return pl.BlockSpec((rows, width), lambda i: (0, 0))


class Row:
    def __init__(self, table, index):
        self.table, self.index = table, index

    def spec(self):
        index = self.index
        return pl.BlockSpec((None, 1, D_MODEL), lambda *ids: (index, 0, 0))


def _slot_spec(u):
    return pl.BlockSpec((8, D_MODEL), lambda *ids: (u, 0))


def prenorm(x, gain, scale, shift):
    def body(x_ref, g_ref, sc_ref, sh_ref, h_ref):
        xv = x_ref[...]
        r = lax.rsqrt(jnp.mean(xv * xv, axis=-1, keepdims=True) + RMS_EPS)
        h_ref[...] = (((xv * r) * g_ref[...]) * (1.0 + sc_ref[...]) + sh_ref[...]).astype(BF16)

    return pl.pallas_call(
        body, grid=(SEQ // ROW_TILE,), in_specs=[_row_spec(), gain.spec(), scale.spec(), shift.spec()],
        out_specs=_row_spec(), out_shape=jax.ShapeDtypeStruct((SEQ, D_MODEL), BF16),
        name="prenorm", compiler_params=_params(("parallel",)),
    )(x, gain.table, scale.table, shift.table)


def resid_bwd(dxo, f, coef, mult, sums, u):
    def body(dx_ref, f_ref, c_ref, sums_in, df_ref, dc_ref):
        del sums_in
        dx = dx_ref[...]
        df_ref[...] = (dx * (mult * c_ref[...])).astype(BF16)
        part = mult * jnp.sum(dx * f_ref[...], axis=0, keepdims=True)

        @pl.when(pl.program_id(0) == 0)
        def _():
            dc_ref[...] = jnp.zeros_like(dc_ref)

        dc_ref[0:1, :] += part

    return pl.pallas_call(
        body, grid=(SEQ // ROW_TILE,),
        in_specs=[_row_spec(), _row_spec(), coef.spec(), pl.BlockSpec(memory_space=pl.ANY)],
        out_specs=[_row_spec(), _slot_spec(u)],
        out_shape=[jax.ShapeDtypeStruct((SEQ, D_MODEL), BF16), jax.ShapeDtypeStruct(sums.shape, F32)],
        input_output_aliases={3: 1}, name="resid_bwd", compiler_params=_params(("arbitrary",)),
    )(dxo, f, coef.table, sums)


def final_loss(x, gain, target):
    def body(x_ref, g_ref, t_ref, loss_ref, dx_ref, dg_ref):
        xv = x_ref[...]
        g = g_ref[...]
        r = lax.rsqrt(jnp.mean(xv * xv, axis=-1, keepdims=True) + RMS_EPS)
        xh = xv * r
        e = xh * g - t_ref[...]
        part = 0.5 * jnp.sum(jnp.mean(e * e, axis=-1, keepdims=True), axis=0, keepdims=True)
        dy = e * (1.0 / D_MODEL)
        dyg = dy * g
        dx_ref[...] = r * (dyg - xh * jnp.mean(dyg * xh, axis=-1, keepdims=True))

        @pl.when(pl.program_id(0) == 0)
        def _():
            loss_ref[...] = jnp.zeros_like(loss_ref)
            dg_ref[...] = jnp.zeros_like(dg_ref)

        loss_ref[...] += jnp.broadcast_to(part, loss_ref.shape)
        dg_ref[0:1, :] += jnp.sum(dy * xh, axis=0, keepdims=True)

    return pl.pallas_call(
        body, grid=(SEQ // ROW_TILE,), in_specs=[_row_spec(), _vec_spec(), _row_spec()],
        out_specs=[_vec_spec(8, 128), _row_spec(), _vec_spec(8)],
        out_shape=[jax.ShapeDtypeStruct((8, 128), F32), jax.ShapeDtypeStruct((SEQ, D_MODEL), F32),
                   jax.ShapeDtypeStruct((8, D_MODEL), F32)],
        name="final_loss", compiler_params=_params(("arbitrary",)),
    )(x, gain, target)


def _prenorm_bwd_epilogue(dh, x_ref, dxo_ref, g_ref, sc_ref, dx_ref, stats_ref, first):
    xv = x_ref[...]
    g = g_ref[...]
    r = lax.rsqrt(jnp.mean(xv * xv, axis=-1, keepdims=True) + RMS_EPS)
    xh = xv * r
    dn = dh * (1.0 + sc_ref[...])
    dxh = dn * g
    dx_ref[...] = dxo_ref[...] + r * (dxh - xh * jnp.mean(dxh * xh, axis=-1, keepdims=True))

    @pl.when(first)
    def _():
        stats_ref[...] = jnp.zeros_like(stats_ref)

    stats_ref[0:1, :] += jnp.sum(dh, axis=0, keepdims=True)
    stats_ref[1:2, :] += jnp.sum(dh * (xh * g), axis=0, keepdims=True)
    stats_ref[2:3, :] += jnp.sum(dn * xh, axis=0, keepdims=True)


def ffn_up(h, wg_all, wu_all):
    def body(h_ref, wg_ref, wu_ref, a_ref, b_ref, s_ref):
        hv = h_ref[...]
        a = _dot(hv, wg_ref[...], NT)
        b = _dot(hv, wu_ref[...], NT)
        a_ref[...] = a
        b_ref[...] = b
        s_ref[...] = (a * _sigmoid(a) * b).astype(BF16)

    w_spec = pl.BlockSpec((None, FF_SHARD, D_MODEL), lambda j, i: (j, 0, 0))
    o_spec = pl.BlockSpec((None, MM_TILE, FF_SHARD), lambda j, i: (j, i, 0))
    hid = (N_CHIPS, SEQ, FF_SHARD)
    return pl.pallas_call(
        body, grid=(N_CHIPS, SEQ // MM_TILE),
        in_specs=[pl.BlockSpec((MM_TILE, D_MODEL), lambda j, i: (i, 0)), w_spec, w_spec],
        out_specs=[o_spec, o_spec, o_spec],
        out_shape=[jax.ShapeDtypeStruct(hid, F32), jax.ShapeDtypeStruct(hid, F32), jax.ShapeDtypeStruct(hid, BF16)],
        name="ffn_up", compiler_params=_params(("parallel", "parallel")),
    )(h, wg_all, wu_all)


def matmul_residual(name, a, a_spec, w_all, w_spec, x, coef, mult):
    def epilogue(acc, ins, outs, ids):
        outs[0][...] = acc
        outs[1][...] = ins[2][...] + (mult * ins[3][...]) * acc

    row = pl.BlockSpec((MM_TILE, D_MODEL), lambda i, j: (i, 0))
    return _matmul(
        name, (SEQ // MM_TILE, N_CHIPS), N_CHIPS, 1, NN, 1,
        [a_spec, w_spec, row, coef.spec()], [row, row],
        [jax.ShapeDtypeStruct((SEQ, D_MODEL), F32)] * 2, (MM_TILE, D_MODEL), epilogue,
        (a, w_all, x, coef.table), ("parallel", "arbitrary"))


def ffn_down(s, wd_all, x, gate):
    return matmul_residual(
        "ffn_down", s, pl.BlockSpec((None, MM_TILE, FF_SHARD), lambda i, j: (j, i, 0)),
        wd_all, pl.BlockSpec((None, FF_SHARD, D_MODEL), lambda i, j: (j, 0, 0)), x, gate, 0.5)


def ffn_bwd_hidden(df, wd_all, a, b):
    def epilogue(ds, ins, outs, ids):
        av, bv = ins[2][...], ins[3][...]
        sig = _sigmoid(av)
        outs[0][...] = (ds * bv * (sig * (1.0 + av * (1.0 - sig)))).astype(BF16)
        outs[1][...] = (ds * (av * sig)).astype(BF16)

    hid_spec = pl.BlockSpec((None, MM_TILE, FF_SHARD), lambda j, i: (j, i, 0))
    hid = jax.ShapeDtypeStruct((N_CHIPS, SEQ, FF_SHARD), BF16)
    return _matmul(
        "ffn_bwd_hidden", (N_CHIPS, SEQ // MM_TILE), 1, None, NT, 1,
        [pl.BlockSpec((MM_TILE, D_MODEL), lambda j, i: (i, 0)),
         pl.BlockSpec((None, FF_SHARD, D_MODEL), lambda j, i: (j, 0, 0)), hid_spec, hid_spec],
        [hid_spec, hid_spec], [hid, hid], None, epilogue, (df, wd_all, a, b), ("parallel", "parallel"))


def grad_weight(name, lhs, lhs_spec, rhs, rhs_spec, shape):
    def epilogue(acc, ins, outs, ids):
        outs[0][...] = acc.astype(BF16)

    return _matmul(
        name, (N_CHIPS, SEQ // MM_TILE), SEQ // MM_TILE, 1, TN, 1,
        [lhs_spec, rhs_spec], [pl.BlockSpec((None,) + shape, lambda j, k: (j, 0, 0))],
        [jax.ShapeDtypeStruct((N_CHIPS,) + shape, BF16)], shape, epilogue, (lhs, rhs), ("parallel", "arbitrary"))[0]


def ffn_grad_weights(h, s, df, da, db):
    tok = pl.BlockSpec((MM_TILE, D_MODEL), lambda j, k: (k, 0))
    hid = pl.BlockSpec((None, MM_TILE, FF_SHARD), lambda j, k: (j, k, 0))
    return (grad_weight("grad_w_gate", da, hid, h, tok, (FF_SHARD, D_MODEL)),
            grad_weight("grad_w_up", db, hid, h, tok, (FF_SHARD, D_MODEL)),
            grad_weight("grad_w_down", s, hid, df, tok, (FF_SHARD, D_MODEL)))


def matmul_prenorm_bwd(name, dims, pairs, pair_specs, x, dxo, gain, scale, sums, u):
    n = len(pairs)

    def epilogue(dh, ins, outs, ids):
        _prenorm_bwd_epilogue(dh, ins[n], ins[n + 1], ins[n + 2], ins[n + 3], outs[0], outs[1], ids[0] == 0)

    row = pl.BlockSpec((MM_TILE, D_MODEL), lambda i, j: (i, 0))
    return _matmul(
        name, (SEQ // MM_TILE, N_CHIPS), N_CHIPS, 1, dims, len(pairs) // 2,
        list(pair_specs) + [row, row, gain.spec(), scale.spec(), pl.BlockSpec(memory_space=pl.ANY)],
        [row, _slot_spec(u)], [jax.ShapeDtypeStruct((SEQ, D_MODEL), F32), jax.ShapeDtypeStruct(sums.shape, F32)],
        (MM_TILE, D_MODEL), epilogue, tuple(pairs) + (x, dxo, gain.table, scale.table, sums),
        ("arbitrary", "arbitrary"), aliases={n + 4: 1})


def ffn_bwd_input(da, db, wg_all, wu_all, x, dxo, gain, scale, sums, u):
    hid = pl.BlockSpec((None, MM_TILE, FF_SHARD), lambda i, j: (j, i, 0))
    w = pl.BlockSpec((None, FF_SHARD, D_MODEL), lambda i, j: (j, 0, 0))
    return matmul_prenorm_bwd("ffn_bwd_input", NN, (da, wg_all, db, wu_all), (hid, w, hid, w), x, dxo, gain, scale,
                              sums, u)


def in_proj(h, w_all):
    def epilogue(acc, ins, outs, ids):
        outs[0][...] = acc
        outs[1][...] = acc.astype(BF16)

    out = pl.BlockSpec((MM_TILE, IN_SHARD), lambda j, i: (i, j))
    return _matmul(
        "in_proj", (N_CHIPS, SEQ // MM_TILE), 1, None, NN, 1,
        [pl.BlockSpec((MM_TILE, D_MODEL), lambda j, i: (i, 0)),
         pl.BlockSpec((None, D_MODEL, IN_SHARD), lambda j, i: (j, 0, 0))],
        [out, out], [jax.ShapeDtypeStruct((SEQ, D_IN), F32), jax.ShapeDtypeStruct((SEQ, D_IN), BF16)],
        None, epilogue, (h, w_all), ("parallel", "parallel"))


_GATE_BLOCK0 = D_QKV // D_SHARD


def _branch_products(o, w_ref):
    ob = o.astype(BF16)
    return (_dot(ob[:, 0:256], w_ref[0:256, :], NN), _dot(ob[:, 256:384], w_ref[256:384, :], NN),
            _dot(ob[:, 384:768], w_ref[384:768, :], NN))


def merge_branches(o_cat, wbr_all, proj):
    def body(o_ref, w_ref, g0_ref, g1_ref, g2_ref, m_ref):
        u = _branch_products(o_ref[...], w_ref)
        m_ref[...] = (_sigmoid(g0_ref[...]) * u[0] + _sigmoid(g1_ref[...]) * u[1]
                      + _sigmoid(g2_ref[...]) * u[2]).astype(BF16)

    def gate_spec(b):
        return pl.BlockSpec((MM_TILE, D_SHARD), lambda i, j: (i, _GATE_BLOCK0 + 4 * b + j))

    return pl.pallas_call(
        body, grid=(SEQ // MM_TILE, N_CHIPS),
        in_specs=[pl.BlockSpec((MM_TILE, BR_ROWS), lambda i, j: (i, 0)),
                  pl.BlockSpec((None, BR_ROWS, D_SHARD), lambda i, j: (j, 0, 0)),
                  gate_spec(0), gate_spec(1), gate_spec(2)],
        out_specs=pl.BlockSpec((MM_TILE, D_SHARD), lambda i, j: (i, j)),
        out_shape=jax.ShapeDtypeStruct((SEQ, D_MODEL), BF16),
        name="merge_branches", compiler_params=_params(("parallel", "parallel")),
    )(o_cat, wbr_all, proj, proj, proj)


def out_proj(merged, wout_all, x, gate):
    return matmul_residual(
        "out_proj", merged, pl.BlockSpec((MM_TILE, D_SHARD), lambda i, j: (i, j)),
        wout_all, pl.BlockSpec((None, D_SHARD, D_MODEL), lambda i, j: (j, 0, 0)), x, gate, 1.0)


def merge_bwd(dmo, wout_all, o_cat, wbr_all, proj):
    def epilogue(dm, ins, outs, ids):
        u = _branch_products(ins[2][...], ins[3])
        for b in range(3):
            sig = _sigmoid(ins[4 + b][...])
            outs[b][...] = (dm * sig).astype(BF16)
            outs[3 + b][...] = (dm * u[b] * (sig * (1.0 - sig))).astype(BF16)

    def gate_spec(b):
        return pl.BlockSpec((MM_TILE, D_SHARD), lambda j, i: (i, _GATE_BLOCK0 + 4 * b + j))

    col = pl.BlockSpec((MM_TILE, D_SHARD), lambda j, i: (i, j))
    du = jax.ShapeDtypeStruct((SEQ, D_MODEL), BF16)
    return _matmul(
        "merge_bwd", (N_CHIPS, SEQ // MM_TILE), 1, None, NT, 1,
        [pl.BlockSpec((MM_TILE, D_MODEL), lambda j, i: (i, 0)),
         pl.BlockSpec((None, D_SHARD, D_MODEL), lambda j, i: (j, 0, 0)),
         pl.BlockSpec((MM_TILE, BR_ROWS), lambda j, i: (i, 0)),
         pl.BlockSpec((None, BR_ROWS, D_SHARD), lambda j, i: (j, 0, 0)),
         gate_spec(0), gate_spec(1), gate_spec(2)],
        [col] * 6, [du] * 6,
        None, epilogue, (dmo, wout_all, o_cat, wbr_all, proj, proj, proj), ("parallel", "parallel"))


def branch_bwd_input(du, wbr_all):
    def body(d0_ref, d1_ref, d2_ref, w_ref, o_ref, acc):
        j = pl.program_id(1)
        parts = (_dot(d0_ref[...], w_ref[0:256, :], NT), _dot(d1_ref[...], w_ref[256:384, :], NT),
                 _dot(d2_ref[...], w_ref[384:768, :], NT))

        @pl.when(j == 0)
        def _():
            acc[:, 0:256], acc[:, 256:384], acc[:, 384:768] = parts

        @pl.when(j > 0)
        def _():
            acc[:, 0:256] += parts[0]
            acc[:, 256:384] += parts[1]
            acc[:, 384:768] += parts[2]

        @pl.when(j == N_CHIPS - 1)
        def _():
            o_ref[...] = acc[...]

    col = pl.BlockSpec((MM_TILE, D_SHARD), lambda i, j: (i, j))
    return pl.pallas_call(
        body, grid=(SEQ // MM_TILE, N_CHIPS),
        in_specs=[col, col, col, pl.BlockSpec((None, BR_ROWS, D_SHARD), lambda i, j: (j, 0, 0))],
        out_specs=pl.BlockSpec((MM_TILE, BR_ROWS), lambda i, j: (i, 0)),
        out_shape=jax.ShapeDtypeStruct((SEQ, BR_ROWS), F32),
        scratch_shapes=[pltpu.VMEM((MM_TILE, BR_ROWS), F32)],
        name="branch_bwd_input", compiler_params=_params(("parallel", "arbitrary")),
    )(du[0], du[1], du[2], wbr_all)


def branch_grad_weights(o_cat, du):
    def body(o_ref, d0_ref, d1_ref, d2_ref, g_ref, acc):
        k = pl.program_id(1)
        ob = o_ref[...].astype(BF16)
        parts = (_dot(ob[:, 0:256], d0_ref[...], TN), _dot(ob[:, 256:384], d1_ref[...], TN),
                 _dot(ob[:, 384:768], d2_ref[...], TN))

        @pl.when(k == 0)
        def _():
            acc[0:256, :], acc[256:384, :], acc[384:768, :] = parts

        @pl.when(k > 0)
        def _():
            acc[0:256, :] += parts[0]
            acc[256:384, :] += parts[1]
            acc[384:768, :] += parts[2]

        @pl.when(k == SEQ // MM_TILE - 1)
        def _():
            g_ref[...] = acc[...].astype(BF16)

    col = pl.BlockSpec((MM_TILE, D_SHARD), lambda j, k: (k, j))
    return pl.pallas_call(
        body, grid=(N_CHIPS, SEQ // MM_TILE),
        in_specs=[pl.BlockSpec((MM_TILE, BR_ROWS), lambda j, k: (k, 0)), col, col, col],
        out_specs=pl.BlockSpec((None, BR_ROWS, D_SHARD), lambda j, k: (j, 0, 0)),
        out_shape=jax.ShapeDtypeStruct((N_CHIPS, BR_ROWS, D_SHARD), BF16),
        scratch_shapes=[pltpu.VMEM((BR_ROWS, D_SHARD), F32)],
        name="branch_grad_weights", compiler_params=_params(("parallel", "arbitrary")),
    )(o_cat, du[0], du[1], du[2])


def mixer_bwd_input(dproj, win_all, x, dxo, gain, scale, sums, u):
    return matmul_prenorm_bwd(
        "mixer_bwd_input", NT, (dproj, win_all),
        (pl.BlockSpec((MM_TILE, IN_SHARD), lambda i, j: (i, j)),
         pl.BlockSpec((None, D_MODEL, IN_SHARD), lambda i, j: (j, 0, 0))), x, dxo, gain, scale, sums, u)


BATCH_QK = (((2,), (2,)), ((0,), (0,)))
BATCH_PV = (((2,), (1,)), ((0,), (0,)))
BATCH_TN = (((1,), (1,)), ((0,), (0,)))


SB_WIDTH = H_SB * HEAD_DIM
SB_ROWS = H_SB * BLK


def _split_dot(v, tri):
    hi = v.astype(BF16)
    lo = (v - hi.astype(F32)).astype(BF16)
    return _dot(hi, tri, NN) + _dot(lo, tri, NN)


def _tri(cmp):
    return cmp(lax.broadcasted_iota(jnp.int32, (BLK, BLK), 0), lax.broadcasted_iota(jnp.int32, (BLK, BLK), 1)).astype(BF16)


def _head_masks():
    lane = lax.broadcasted_iota(jnp.int32, (1, SB_WIDTH), 1) // HEAD_DIM
    return [lane == h for h in range(H_SB)]


def _stack_heads(x, masks):
    return jnp.concatenate([jnp.where(m, x, jnp.zeros_like(x)) for m in masks], axis=0)


def _merge_heads(y, masks):
    out = jnp.where(masks[0], y[0:BLK], 0.0)
    for h in range(1, H_SB):
        out = jnp.where(masks[h], y[h * BLK:(h + 1) * BLK], out)
    return out


def _sb_scores(q4, k_ref, j, diagonal):
    rows = pl.ds(pl.multiple_of(j * BLK, BLK), BLK)
    z = _dot(q4, k_ref[rows, :], NT)
    log_fail = -(jnp.maximum(z, 0.0) + jnp.log(1.0 + jnp.exp(-jnp.abs(z))))
    log_hit = z + log_fail
    before = None
    if diagonal:
        tile = (SB_ROWS, BLK)
        before = lax.broadcasted_iota(jnp.int32, tile, 1) < (lax.broadcasted_iota(jnp.int32, tile, 0) & (BLK - 1))
        log_fail = jnp.where(before, log_fail, 0.0)
    return rows, before, log_fail, log_hit


def _keep(before, x):
    return x if before is None else jnp.where(before, x, 0.0)


def sb_forward(qkv):
    def body(q_ref, k_ref, v_ref, o_ref, tot_ref):
        i = pl.program_id(0)
        masks = _head_masks()
        q4 = _stack_heads(q_ref[...] * QK_SCALE, masks)
        later = _tri(lambda r, c: r > c)

        def tile(j, carry, diagonal):
            o, run = carry
            rows, before, log_fail, log_hit = _sb_scores(q4, k_ref, j, diagonal)
            between = _split_dot(log_fail, later) + run
            w = _keep(before, jnp.exp(log_hit + between))
            o = o + _merge_heads(_dot(w.astype(BF16), v_ref[rows, :], NN), masks)
            return o, run + jnp.sum(log_fail, axis=1, keepdims=True)

        carry = tile(i, (jnp.zeros((BLK, SB_WIDTH), F32), jnp.zeros((SB_ROWS, 1), F32)), True)
        o, run = lax.fori_loop(0, i, lambda t, c: tile(i - 1 - t, c, False), carry)
        o_ref[...] = o
        tot_ref[...] = run

    return pl.pallas_call(
        body, grid=(N_BLK,),
        in_specs=[pl.BlockSpec((BLK, SB_WIDTH), lambda i: (i, 0)), pl.BlockSpec((SEQ, SB_WIDTH), lambda i: (0, 1)),
                  pl.BlockSpec((SEQ, SB_WIDTH), lambda i: (0, 2))],
        out_specs=[pl.BlockSpec((BLK, SB_WIDTH), lambda i: (i, 0)), pl.BlockSpec((None, SB_ROWS, 1), lambda i: (i, 0, 0))],
        out_shape=[jax.ShapeDtypeStruct((SEQ, SB_WIDTH), F32), jax.ShapeDtypeStruct((N_BLK, SB_ROWS, 1), F32)],
        name="sb_forward", compiler_params=_params(("parallel",)),
    )(qkv, qkv, qkv)


def sb_backward(qkv, total, do_cat):
    def body(q_ref, k_ref, v_ref, tot_ref, do_ref, dq_ref, dk_ref, dv_ref):
        i = pl.program_id(0)

        @pl.when(i == 0)
        def _():
            dk_ref[...] = jnp.zeros_like(dk_ref)
            dv_ref[...] = jnp.zeros_like(dv_ref)

        masks = _head_masks()
        q4 = _stack_heads(q_ref[...] * QK_SCALE, masks)
        do4 = _stack_heads(do_ref[...].astype(BF16), masks)
        total_v = tot_ref[...]
        upto = _tri(lambda r, c: r <= c)
        earlier = _tri(lambda r, c: r < c)

        def tile(j, carry, diagonal):
            dq, seen, g_seen = carry
            rows, before, log_fail, log_hit = _sb_scores(q4, k_ref, j, diagonal)
            between = total_v - (seen + _split_dot(log_fail, upto))
            w = _keep(before, jnp.exp(log_hit + between))
            g = _dot(do4, v_ref[rows, :], NT) * w
            g_earlier = g_seen + _split_dot(g, earlier)
            sig = jnp.exp(log_hit)
            dz = _keep(before, g * (1.0 - sig) - g_earlier * sig).astype(BF16)
            dq = dq + _merge_heads(_dot(dz, k_ref[rows, :], NN), masks)
            dk_ref[rows, :] += _dot(dz, q4, TN)
            dv_ref[rows, :] += _dot(w.astype(BF16), do4, TN)
            return dq, seen + jnp.sum(log_fail, axis=1, keepdims=True), g_seen + jnp.sum(g, axis=1, keepdims=True)

        zero = jnp.zeros((SB_ROWS, 1), F32)
        carry = lax.fori_loop(0, i, lambda j, c: tile(j, c, False), (jnp.zeros((BLK, SB_WIDTH), F32), zero, zero))
        dq, _, _ = tile(i, carry, True)
        dq_ref[...] = dq * QK_SCALE

    blk = pl.BlockSpec((BLK, SB_WIDTH), lambda i: (i, 0))
    full = pl.BlockSpec((SEQ, SB_WIDTH), lambda i: (0, 0))
    shape = jax.ShapeDtypeStruct((SEQ, SB_WIDTH), F32)
    return pl.pallas_call(
        body, grid=(N_BLK,),
        in_specs=[blk, pl.BlockSpec((SEQ, SB_WIDTH), lambda i: (0, 1)), pl.BlockSpec((SEQ, SB_WIDTH), lambda i: (0, 2)),
                  pl.BlockSpec((None, SB_ROWS, 1), lambda i: (i, 0, 0)), blk],
        out_specs=[blk, full, full], out_shape=[shape, shape, shape],
        name="sb_backward", compiler_params=_params(("arbitrary",)),
    )(qkv, qkv, qkv, total, do_cat)


def _band_scores(q_ref, kp_ref, ko_ref, bias_ref, hb, prev_mask):
    b = pl.program_id(1)
    qs = q_ref[...]
    s_prev = _dot(qs, kp_ref[...], BATCH_QK) + bias_ref[:, :, 0:BLK]
    s_prev = jnp.concatenate(
        [jnp.where((b & prev_mask(pl.program_id(0) * hb + t)) != 0, s_prev[t:t + 1], NEG) for t in range(hb)], axis=0)
    s_own = _dot(qs, ko_ref[...], BATCH_QK) + bias_ref[:, :, BLK:2 * BLK]
    return qs, s_prev, s_own


def _band_specs(hb, rows, t_n):
    def q_spec(width):
        return pl.BlockSpec((hb, None, rows, width), lambda h, b: (h, b, 0, 0))

    own = pl.BlockSpec((hb, BLK, HEAD_DIM), lambda h, b: (h, b, 0))
    prev = pl.BlockSpec((hb, BLK, HEAD_DIM), lambda h, b: (h, jnp.maximum(b - 1, 0), 0))
    per_head = lambda r, width: pl.BlockSpec((hb, r, width), lambda h, b: (h, 0, 0))
    return q_spec, own, prev, per_head


def banded_forward(name, q, k, v, bias, sinks, hb, prev_mask):
    h_n, nb, rows, _ = q.shape

    def body(q_ref, kp_ref, ko_ref, vp_ref, vo_ref, bias_ref, sink_ref, o_ref, lse_ref):
        _, s_prev, s_own = _band_scores(q_ref, kp_ref, ko_ref, bias_ref, hb, prev_mask)
        sink = sink_ref[...]
        m = jnp.maximum(jnp.maximum(jnp.max(s_prev, axis=2, keepdims=True), jnp.max(s_own, axis=2, keepdims=True)), sink)
        p_prev = jnp.exp(s_prev - m)
        p_own = jnp.exp(s_own - m)
        denom = jnp.sum(p_prev, axis=2, keepdims=True) + jnp.sum(p_own, axis=2, keepdims=True) + jnp.exp(sink - m)
        o = _dot(p_prev.astype(BF16), vp_ref[...], BATCH_PV) + _dot(p_own.astype(BF16), vo_ref[...], BATCH_PV)
        o_ref[...] = o / denom
        lse_ref[...] = m + jnp.log(denom)

    q_spec, own, prev, per_head = _band_specs(hb, rows, k.shape[1])
    return pl.pallas_call(
        body, grid=(h_n // hb, nb),
        in_specs=[q_spec(HEAD_DIM), prev, own, prev, own, per_head(rows, 2 * BLK), per_head(rows, 1)],
        out_specs=[q_spec(HEAD_DIM), q_spec(1)],
        out_shape=[jax.ShapeDtypeStruct(q.shape, F32), jax.ShapeDtypeStruct((h_n, nb, rows, 1), F32)],
        name=name, compiler_params=_params(("parallel", "parallel")),
    )(q, k, k, v, v, bias, sinks)


def banded_backward(name, q, k, v, bias, sinks, o, lse, do, dlse, hb, prev_mask):
    h_n, nb, rows, _ = q.shape
    t_n = k.shape[1]

    def body(q_ref, kp_ref, ko_ref, vp_ref, vo_ref, bias_ref, sink_ref, o_ref, lse_ref, do_ref, dlse_ref,
             dq_ref, dk_ref, dv_ref, dbias_ref, dsink_ref):
        b = pl.program_id(1)

        @pl.when(b == 0)
        def _():
            dk_ref[...] = jnp.zeros_like(dk_ref)
            dv_ref[...] = jnp.zeros_like(dv_ref)
            dbias_ref[...] = jnp.zeros_like(dbias_ref)
            dsink_ref[...] = jnp.zeros_like(dsink_ref)

        qs, s_prev, s_own = _band_scores(q_ref, kp_ref, ko_ref, bias_ref, hb, prev_mask)
        lse_v = lse_ref[...]
        dov = do_ref[...]
        dob = dov.astype(BF16)
        shift = dlse_ref[...] - jnp.sum(dov * o_ref[...], axis=2, keepdims=True)
        p_prev = jnp.exp(s_prev - lse_v)
        p_own = jnp.exp(s_own - lse_v)
        ds_prev = p_prev * (_dot(dob, vp_ref[...], BATCH_QK) + shift)
        ds_own = p_own * (_dot(dob, vo_ref[...], BATCH_QK) + shift)
        dbias_ref[:, :, 0:BLK] += ds_prev
        dbias_ref[:, :, BLK:2 * BLK] += ds_own
        d_sink = jnp.exp(sink_ref[...] - lse_v) * shift
        for g in range(rows // BLK):
            dsink_ref[:, g:g + 1, :] += jnp.sum(d_sink[:, g * BLK:(g + 1) * BLK, :], axis=1, keepdims=True)
        ds_prev = ds_prev.astype(BF16)
        ds_own = ds_own.astype(BF16)
        dq_ref[...] = (_dot(ds_prev, kp_ref[...], BATCH_PV) + _dot(ds_own, ko_ref[...], BATCH_PV)) * QK_SCALE
        rows_prev = pl.ds(pl.multiple_of(jnp.maximum(b - 1, 0) * BLK, BLK), BLK)
        rows_own = pl.ds(pl.multiple_of(b * BLK, BLK), BLK)
        dk_ref[:, rows_prev, :] += _dot(ds_prev, qs, BATCH_TN)
        dk_ref[:, rows_own, :] += _dot(ds_own, qs, BATCH_TN)
        dv_ref[:, rows_prev, :] += _dot(p_prev.astype(BF16), dob, BATCH_TN)
        dv_ref[:, rows_own, :] += _dot(p_own.astype(BF16), dob, BATCH_TN)

    q_spec, own, prev, per_head = _band_specs(hb, rows, t_n)
    kv_full = per_head(t_n, HEAD_DIM)
    kv_shape = jax.ShapeDtypeStruct((h_n, t_n, HEAD_DIM), F32)
    return pl.pallas_call(
        body, grid=(h_n // hb, nb),
        in_specs=[q_spec(HEAD_DIM), prev, own, prev, own, per_head(rows, 2 * BLK), per_head(rows, 1),
                  q_spec(HEAD_DIM), q_spec(1), q_spec(HEAD_DIM), q_spec(1)],
        out_specs=[q_spec(HEAD_DIM), kv_full, kv_full, per_head(rows, 2 * BLK), per_head(rows // BLK, BLK)],
        out_shape=[jax.ShapeDtypeStruct(q.shape, F32), kv_shape, kv_shape,
                   jax.ShapeDtypeStruct((h_n, rows, 2 * BLK), F32), jax.ShapeDtypeStruct((h_n, rows // BLK, BLK), F32)],
        name=name, compiler_params=_params(("parallel", "arbitrary")),
    )(q, k, k, v, v, bias, sinks, o, lse, do, dlse)


def _dil_prev_mask(head):
    group = head // H_PER_DIL
    return jnp.where(group == 0, 15, jnp.where(group == 1, 3, 0))


def _swa_prev_mask(head):
    del head
    return 15


DIL_HEADS_PER_STEP = 3
SWA_GROUP = H_SWA_Q // H_SWA_KV
N_BLK = SEQ // BLK


def dilated_merge(o, lse):
    def body(o_ref, l_ref, out_ref):
        lv = l_ref[...]
        m = jnp.max(lv, axis=0, keepdims=True)
        e = jnp.exp(lv - m)
        alpha = e / jnp.sum(e, axis=0, keepdims=True)
        out_ref[...] = jnp.sum(alpha * o_ref[...], axis=0)

    return pl.pallas_call(
        body, grid=(H_PER_DIL, SEQ // ROW_TILE),
        in_specs=[pl.BlockSpec((3, None, ROW_TILE, HEAD_DIM), lambda h, i: (0, h, i, 0)),
                  pl.BlockSpec((3, None, ROW_TILE, 1), lambda h, i: (0, h, i, 0))],
        out_specs=pl.BlockSpec((None, ROW_TILE, HEAD_DIM), lambda h, i: (h, i, 0)),
        out_shape=jax.ShapeDtypeStruct((H_PER_DIL, SEQ, HEAD_DIM), F32),
        name="dilated_merge", compiler_params=_params(("parallel", "parallel")),
    )(o, lse)


def dilated_merge_bwd(o, lse, dout):
    def body(o_ref, l_ref, d_ref, do_ref, dl_ref):
        lv = l_ref[...]
        m = jnp.max(lv, axis=0, keepdims=True)
        e = jnp.exp(lv - m)
        alpha = e / jnp.sum(e, axis=0, keepdims=True)
        dv = d_ref[...][None]
        do_ref[...] = alpha * dv
        dalpha = jnp.sum(dv * o_ref[...], axis=-1, keepdims=True)
        dl_ref[...] = alpha * (dalpha - jnp.sum(alpha * dalpha, axis=0, keepdims=True))

    o_spec = pl.BlockSpec((3, None, ROW_TILE, HEAD_DIM), lambda h, i: (0, h, i, 0))
    l_spec = pl.BlockSpec((3, None, ROW_TILE, 1), lambda h, i: (0, h, i, 0))
    return pl.pallas_call(
        body, grid=(H_PER_DIL, SEQ // ROW_TILE),
        in_specs=[o_spec, l_spec, pl.BlockSpec((None, ROW_TILE, HEAD_DIM), lambda h, i: (h, i, 0))],
        out_specs=[o_spec, l_spec],
        out_shape=[jax.ShapeDtypeStruct(o.shape, F32), jax.ShapeDtypeStruct(lse.shape, F32)],
        name="dilated_merge_bwd", compiler_params=_params(("parallel", "parallel")),
    )(o, lse, dout)


def rel_bias_reduce(dbias0, dbias1, bucket):
    def body(d0_ref, d1_ref, b_ref, o_ref):
        dv, bv = d0_ref[...] + d1_ref[...], b_ref[...]
        lane = lax.broadcasted_iota(jnp.int32, (1, BLK), 1)
        acc = jnp.zeros((1, BLK), F32)
        for bkt in range(N_BUCKETS):
            acc = acc + jnp.where(lane == bkt, jnp.sum(jnp.where(bv == bkt, dv, 0.0)), 0.0)
        o_ref[...] = acc

    tile = pl.BlockSpec((None, BLK, 2 * BLK), lambda h: (h, 0, 0))
    return pl.pallas_call(
        body, grid=(dbias0.shape[0],), in_specs=[tile, tile, tile],
        out_specs=pl.BlockSpec((None, 1, BLK), lambda h: (h, 0, 0)),
        out_shape=jax.ShapeDtypeStruct((dbias0.shape[0], 1, BLK), F32),
        name="rel_bias_reduce", compiler_params=_params(("parallel",)),
    )(dbias0, dbias1, bucket)


def _heads(t):
    return t.reshape(SEQ, -1, HEAD_DIM).transpose(1, 0, 2)


def _unheads(t):
    return t.transpose(1, 0, 2).reshape(SEQ, -1)


def _dilate(t):
    parts = []
    for g, (_, d) in enumerate(DIL_PATTERNS):
        tg = t[:, 128 * g:128 * (g + 1)].reshape(SEQ // d, d, H_PER_DIL, HEAD_DIM).transpose(2, 1, 0, 3)
        parts.append(tg.reshape(H_PER_DIL, SEQ, HEAD_DIM))
    return jnp.concatenate(parts, axis=0)


def _undilate(t):
    outs = []
    for g, (_, d) in enumerate(DIL_PATTERNS):
        tg = t[2 * g:2 * g + 2].reshape(H_PER_DIL, d, SEQ // d, -1).transpose(0, 2, 1, 3)
        outs.append(tg.reshape(H_PER_DIL, SEQ, -1))
    return jnp.stack(outs)


def _redilate(t):
    parts = []
    for g, (_, d) in enumerate(DIL_PATTERNS):
        tg = t[g].reshape(H_PER_DIL, SEQ // d, d, -1).transpose(0, 2, 1, 3)
        parts.append(tg.reshape(H_PER_DIL, SEQ, -1))
    return jnp.concatenate(parts, axis=0)


def _t5_bucket(n):
    max_exact = N_BUCKETS // 2
    nf = jnp.maximum(n, 1).astype(F32)
    large = max_exact + (jnp.log(nf / max_exact) / math.log(MAX_REL_DIST / max_exact)
                         * (N_BUCKETS - max_exact)).astype(jnp.int32)
    large = jnp.minimum(large, N_BUCKETS - 1)
    return jnp.where(n < max_exact, n, large)


def band_tables(rel_bias):
    rel = jnp.arange(BLK)[:, None] + BLK - jnp.arange(2 * BLK)[None, :]
    buckets = []
    patterns = [(d, w // d) for w, d in DIL_PATTERNS for _ in range(H_PER_DIL)] + [(1, SWA_WINDOW - 1)] * H_SWA_Q
    for d, max_dist in patterns:
        band = (rel >= 0) & (rel <= max_dist)
        buckets.append(jnp.where(band, _t5_bucket(jnp.maximum(rel, 0) * d), -1))
    buckets = jnp.stack(buckets).astype(jnp.int32)

    def body(table_ref, b_ref, o_ref):
        h = pl.program_id(0)
        bv = b_ref[...]
        tile = jnp.full(bv.shape, NEG, F32)
        for bkt in range(N_BUCKETS):
            tile = jnp.where(bv == bkt, table_ref[h, bkt], tile)
        o_ref[...] = tile

    spec = pl.BlockSpec((None, BLK, 2 * BLK), lambda h: (h, 0, 0))
    tiles = pl.pallas_call(
        body, grid=(len(patterns),), in_specs=[pl.BlockSpec(memory_space=pltpu.SMEM), spec], out_specs=spec,
        out_shape=jax.ShapeDtypeStruct(buckets.shape, F32), name="band_tables", compiler_params=_params(("parallel",)),
    )(rel_bias.T, buckets)
    return tiles[:H_DIL], tiles[H_DIL:], buckets


def _swa_rows(t):
    t = t.reshape(N_BLK, BLK, H_SWA_KV, SWA_GROUP, HEAD_DIM).transpose(2, 0, 3, 1, 4)
    return t.reshape(H_SWA_KV, N_BLK, SWA_GROUP * BLK, HEAD_DIM)


def _swa_tokens(t):
    t = t.reshape(H_SWA_KV, N_BLK, SWA_GROUP, BLK, HEAD_DIM).transpose(1, 3, 0, 2, 4)
    return t.reshape(SEQ, H_SWA_Q * HEAD_DIM)


def _sink_rows(sinks):
    return jnp.broadcast_to(sinks.reshape(H_SWA_KV, SWA_GROUP, 1, 1), (H_SWA_KV, SWA_GROUP, BLK, 1)).reshape(
        H_SWA_KV, SWA_GROUP * BLK, 1)


def _no_sinks():
    return jnp.full((H_DIL, BLK, 1), NEG, F32)


def _vec(v):
    return v.reshape(1, D_MODEL)


class UnitRows:
    def __init__(self, u, mod_table, gain_table):
        self.shift, self.scale, self.gate = (Row(mod_table, 3 * u + t) for t in range(3))
        self.gain = Row(gain_table, u)


def ffn_forward(x, rows, w):
    h = prenorm(x, rows.gain, rows.scale, rows.shift)
    a, b, s = ffn_up(h, w[0], w[1])
    f, xo = ffn_down(s, w[2], x, rows.gate)
    return xo, (x, h, a, b, s, f)


def ffn_backward(u, dxo, saved, rows, w, sums):
    x, h, a, b, s, f = saved
    df, gate_sums = resid_bwd(dxo, f, rows.gate, 0.5, sums[1], u)
    da, db = ffn_bwd_hidden(df, w[2], a, b)
    grads = ffn_grad_weights(h, s, df, da, db)
    dx, norm_sums = ffn_bwd_input(da, db, w[0], w[1], x, dxo, rows.gain, rows.scale, sums[0], u)
    return dx, (norm_sums, gate_sums), grads


def mixer_forward(x, rows, sinks, bias_dil, bias_swa, w):
    h = prenorm(x, rows.gain, rows.scale, rows.shift)
    proj, qkv = in_proj(h, w[0])
    q_dil, k_dil, v_dil = _dilate(qkv[:, 768:1152] * QK_SCALE), _dilate(qkv[:, 1152:1536]), _dilate(qkv[:, 1536:1920])
    q_swa, k_swa, v_swa = _swa_rows(qkv[:, 1920:2304] * QK_SCALE), _heads(qkv[:, 2304:2432]), _heads(qkv[:, 2432:2560])
    o_sb, total_sb = sb_forward(qkv)
    q_dil = q_dil.reshape(H_DIL, N_BLK, BLK, HEAD_DIM)
    o_dd, lse_dd = banded_forward("dilated_forward", q_dil, k_dil, v_dil, bias_dil, _no_sinks(), DIL_HEADS_PER_STEP,
                                  _dil_prev_mask)
    o_dt, lse_dt = _undilate(o_dd.reshape(H_DIL, SEQ, HEAD_DIM)), _undilate(lse_dd.reshape(H_DIL, SEQ, 1))
    o_dil = dilated_merge(o_dt, lse_dt)
    bias_swa = bias_swa.reshape(H_SWA_KV, SWA_GROUP * BLK, 2 * BLK)
    o_swa, lse_swa = banded_forward("swa_forward", q_swa, k_swa, v_swa, bias_swa, _sink_rows(sinks), 1, _swa_prev_mask)
    o_cat = jnp.concatenate([o_sb, _unheads(o_dil), _swa_tokens(o_swa)], axis=1)
    merged = merge_branches(o_cat, w[1], proj)
    mo, xo = out_proj(merged, w[2], x, rows.gate)
    saved = (x, h, proj, (qkv, total_sb), (q_dil, k_dil, v_dil, o_dd, lse_dd, o_dt, lse_dt),
             (q_swa, k_swa, v_swa, o_swa, lse_swa), o_cat, merged, mo)
    return xo, saved


def mixer_backward(u, dxo, saved, rows, sinks, bias_dil, bias_swa, w, sums):
    x, h, proj, sb, dil, swa, o_cat, merged, mo = saved
    dmo, gate_sums = resid_bwd(dxo, mo, rows.gate, 1.0, sums[1], u)
    tok = pl.BlockSpec((MM_TILE, D_MODEL), lambda j, k: (k, 0))
    g_out = grad_weight("grad_w_out", merged, pl.BlockSpec((MM_TILE, D_SHARD), lambda j, k: (k, j)), dmo, tok,
                        (D_SHARD, D_MODEL))
    du0, du1, du2, dg0, dg1, dg2 = merge_bwd(dmo, w[2], o_cat, w[1], proj)
    du = (du0, du1, du2)
    do_cat = branch_bwd_input(du, w[1])
    g_br = branch_grad_weights(o_cat, du)

    qkv, total_sb = sb
    dq_sb, dk_sb, dv_sb = sb_backward(qkv, total_sb, do_cat)

    q_dil, k_dil, v_dil, o_dd, lse_dd, o_dt, lse_dt = dil
    do_dt, dlse_dt = dilated_merge_bwd(o_dt, lse_dt, _heads(do_cat[:, 256:384]))
    dq_dil, dk_dil, dv_dil, dbias_dil, _ = banded_backward(
        "dilated_backward", q_dil, k_dil, v_dil, bias_dil, _no_sinks(), o_dd, lse_dd,
        _redilate(do_dt).reshape(q_dil.shape), _redilate(dlse_dt).reshape(lse_dd.shape), DIL_HEADS_PER_STEP, _dil_prev_mask)

    q_swa, k_swa, v_swa, o_swa, lse_swa = swa
    bias_swa = bias_swa.reshape(H_SWA_KV, SWA_GROUP * BLK, 2 * BLK)
    dq_swa, dk_swa, dv_swa, dbias_swa, dsinks = banded_backward(
        "swa_backward", q_swa, k_swa, v_swa, bias_swa, _sink_rows(sinks), o_swa, lse_swa, _swa_rows(do_cat[:, 384:768]),
        jnp.zeros_like(lse_swa), 1, _swa_prev_mask)
    dbias_swa = dbias_swa.reshape(H_SWA_Q, BLK, 2 * BLK)

    def tokens(t):
        return _undilate(t).transpose(2, 0, 1, 3).reshape(SEQ, -1)

    dproj = jnp.concatenate(
        [dq_sb, dk_sb, dv_sb, tokens(dq_dil.reshape(H_DIL, SEQ, HEAD_DIM)), tokens(dk_dil),
         tokens(dv_dil), _swa_tokens(dq_swa), _unheads(dk_swa), _unheads(dv_swa)], axis=1).astype(BF16)
    dproj = jnp.concatenate([dproj, dg0, dg1, dg2], axis=1)
    g_in = grad_weight("grad_w_in", h, tok, dproj, pl.BlockSpec((MM_TILE, IN_SHARD), lambda j, k: (k, j)),
                       (D_MODEL, IN_SHARD))
    dx, norm_sums = mixer_bwd_input(dproj, w[0], x, dxo, rows.gain, rows.scale, sums[0], u)
    dbias = jnp.concatenate([dbias_dil, dbias_swa], axis=0)
    return dx, (norm_sums, gate_sums), dbias, dsinks[:, :, 0].reshape(H_SWA_Q), (g_in, g_br, g_out)


N_UNITS = 3 * DEPTH


def device_step(x, target, mod, gains, final_gain, sinks, rel_bias, get_weights, put_grads):
    bias_dil, bias_swa, bucket = band_tables(rel_bias)
    mod_table = mod.reshape(3 * N_UNITS, 1, D_MODEL)
    gain_table = gains.reshape(N_UNITS, 1, D_MODEL)
    saved, weights = [], []
    for u in range(N_UNITS):
        l, j = divmod(u, 3)
        w = get_weights(u, x)
        rows = UnitRows(u, mod_table, gain_table)
        if j == 1:
            x, s = mixer_forward(x, rows, sinks[l], bias_dil, bias_swa, w)
        else:
            x, s = ffn_forward(x, rows, w)
        saved.append(s)
        weights.append(w)
    loss, dx, dfinal = final_loss(x, _vec(final_gain), target)

    sums = (lax.empty((8 * N_UNITS, D_MODEL), F32), lax.empty((8 * N_UNITS, D_MODEL), F32))
    dbias, dsinks = [None] * DEPTH, [None] * DEPTH
    zero = jnp.zeros((1, 1), F32)
    for u in reversed(range(N_UNITS)):
        l, j = divmod(u, 3)
        rows = UnitRows(u, mod_table, gain_table + zero)
        if j == 1:
            dx, sums, dbias[l], dsinks[l], grads = mixer_backward(
                u, dx, saved[u], rows, sinks[l], bias_dil, bias_swa, weights[u], sums)
        else:
            dx, sums, grads = ffn_backward(u, dx, saved[u], rows, weights[u], sums)
        if u > 0:
            zero = put_grads(u, grads)
    drel = rel_bias_reduce(dbias[0], dbias[1], bucket)[:, 0, :N_BUCKETS].T
    norm_sums, gate_sums = (t.reshape(DEPTH, 3, 8, D_MODEL) for t in sums)
    dmod = jnp.stack([norm_sums[:, :, 0], norm_sums[:, :, 1], gate_sums[:, :, 0]], axis=2)
    return loss, dx, dmod, norm_sums[:, :, 2], dfinal[0], jnp.stack(dsinks), drel, grads


MESH = pl.DeviceIdType.MESH
CHIP_FLIPS = ((1, 0), (0, 1), (1, 1))
ANY = pl.BlockSpec(memory_space=pl.ANY)


def _position():
    return lax.axis_index("x"), lax.axis_index("y"), lax.axis_index("c")


def all_gather_small(name, piece):
    def body(x_ref, out_ref, send_sems, recv_sems, local_sem):
        x, y, c = _position()
        me, sibling = (x, y, c), (x, y, 1 - c)
        chips = [(x ^ fx, y ^ fy) for fx, fy in CHIP_FLIPS]

        def rows(px, py, pc):
            return out_ref.at[4 * px + 2 * py + pc]

        def copy(k, block, to, src=None):
            return pltpu.make_async_remote_copy(
                src_ref=rows(*block) if src is None else src, dst_ref=rows(*block),
                send_sem=send_sems.at[k], recv_sem=recv_sems.at[k], device_id=to, device_id_type=MESH)

        mine = pltpu.make_async_copy(x_ref, rows(*me), local_sem)
        mine.start()
        first = [copy(0, me, sibling, src=x_ref)]
        first += [copy(1 + j, me, (*chip, c), src=x_ref) for j, chip in enumerate(chips)]
        for cp in first:
            cp.start()
        passed = [copy(4 + j, (*chip, c), sibling) for j, chip in enumerate(chips)]
        for j, chip in enumerate(chips):
            copy(1 + j, (*chip, c), me).wait_recv()
            passed[j].start()
        copy(0, sibling, me).wait_recv()
        for j, chip in enumerate(chips):
            copy(4 + j, (*chip, 1 - c), me).wait_recv()
        for cp in first + passed:
            cp.wait_send()
        mine.wait()

    return pl.pallas_call(
        body, out_shape=jax.ShapeDtypeStruct((N_DEV,) + piece.shape, piece.dtype),
        in_specs=[pl.BlockSpec(memory_space=pltpu.VMEM)], out_specs=pl.BlockSpec(memory_space=pltpu.VMEM),
        scratch_shapes=[pltpu.SemaphoreType.DMA((7,)), pltpu.SemaphoreType.DMA((7,)), pltpu.SemaphoreType.DMA],
        name=name,
    )(piece)


def exchange(name, operands, out_shapes, aliases, plan):
    n_in, n_out = len(operands), len(out_shapes)

    def body(*refs):
        ins, outs = refs[:n_in], refs[n_in:n_in + n_out]
        send_sems, recv_sems, local_sems = refs[n_in + n_out:]
        x, y, c = _position()
        local, sends, recvs = plan(ins, outs, x, y, c)
        local = [pltpu.make_async_copy(s, d, local_sems.at[k]) for k, (s, d) in enumerate(local)]
        for cp in local:
            cp.start()
        remote = [pltpu.make_async_remote_copy(src_ref=s, dst_ref=d, send_sem=send_sems.at[k], recv_sem=recv_sems.at[k],
                                               device_id=dev, device_id_type=MESH)
                  for k, (s, d, dev) in enumerate(sends)]
        for cp in remote:
            cp.start()
        for k, r in enumerate(recvs):
            pltpu.make_async_remote_copy(src_ref=r, dst_ref=r, send_sem=send_sems.at[k], recv_sem=recv_sems.at[k],
                                         device_id=(x, y, c), device_id_type=MESH).wait_recv()
        for cp in remote:
            cp.wait_send()
        for cp in local:
            cp.wait()

    n_sends, n_local = plan.n_sends, max(plan.n_local, 1)
    return pl.pallas_call(
        body, out_shape=out_shapes, in_specs=[ANY] * n_in, out_specs=[ANY] * n_out,
        scratch_shapes=[pltpu.SemaphoreType.DMA((n_sends,)), pltpu.SemaphoreType.DMA((n_sends,)),
                        pltpu.SemaphoreType.DMA((n_local,))],
        input_output_aliases=aliases, name=name,
    )(*operands)


def _plan(n_local, n_sends):
    def wrap(fn):
        fn.n_local, fn.n_sends = n_local, n_sends
        return fn
    return wrap


def _half(ref, axis, c):
    rows = ref.shape[axis] // 2
    idx = [slice(None)] * len(ref.shape)
    idx[axis] = pl.ds(pl.multiple_of(c * rows, 16), rows)
    return ref.at[tuple(idx)]


HBM = pl.BlockSpec(memory_space=pltpu.HBM)
SEM = pl.BlockSpec(memory_space=pltpu.SEMAPHORE)
EFFECT = pltpu.SideEffectType.DATAFLOW_SIDE_EFFECTING


def split_start(name, bufs, extra, n_copies, describe):
    n = len(bufs)

    def body(*refs):
        send_sems, recv_sems = refs[n + len(extra)], refs[n + len(extra) + 1]
        x, y, c = _position()
        for k, (src, dst, _, peer) in enumerate(describe(refs[:n], x, y, c)):
            pltpu.make_async_remote_copy(src_ref=src, dst_ref=dst, send_sem=send_sems.at[k], recv_sem=recv_sems.at[k],
                                         device_id=peer, device_id_type=MESH).start()
        token = refs[-1]
        token[...] = jnp.zeros_like(token)

    out = pl.pallas_call(
        body, name=name,
        out_shape=(pltpu.SemaphoreType.DMA((n_copies,)), pltpu.SemaphoreType.DMA((n_copies,)),
                   *[pltpu.HBM(b.shape, b.dtype) for b in bufs], jax.ShapeDtypeStruct((8, 128), F32)),
        in_specs=[HBM] * n + [ANY] * len(extra),
        out_specs=(SEM, SEM, *[HBM] * n, pl.BlockSpec(memory_space=pltpu.VMEM)),
        input_output_aliases={k: 2 + k for k in range(n)},
        compiler_params=pltpu.CompilerParams(has_side_effects=EFFECT),
    )(*[pltpu.with_memory_space_constraint(b, pltpu.HBM) for b in bufs], *extra)
    return out[0], out[1], list(out[2:2 + n]), out[-1]


def split_wait(name, bufs, send_sems, recv_sems, after, describe):
    n = len(bufs)

    def body(*refs):
        send, recv = refs[n], refs[n + 1]
        x, y, c = _position()
        for k, (src, _, dst, peer) in enumerate(describe(refs[:n], x, y, c)):
            copy = pltpu.make_async_remote_copy(src_ref=src, dst_ref=dst, send_sem=send.at[k], recv_sem=recv.at[k],
                                                device_id=peer, device_id_type=MESH)
            copy.wait_send()
            copy.wait_recv()

    out = pl.pallas_call(
        body, name=name, out_shape=[pltpu.HBM(b.shape, b.dtype) for b in bufs],
        in_specs=[HBM] * n + [SEM, SEM] + [ANY] * len(after), out_specs=[HBM] * n,
        input_output_aliases={k: k for k in range(n)},
        compiler_params=pltpu.CompilerParams(has_side_effects=EFFECT),
    )(*bufs, send_sems, recv_sems, *after)
    return list(out)


def _row_tile(rows, cols):
    best = 16
    for t in range(16, rows + 1, 16):
        if rows % t == 0 and t * cols <= 256 * 1024:
            best = t
    return best


def cast_into_slot(name, param, index, chip):
    rows, cols = param.shape[-2:]
    tr = _row_tile(rows, cols)
    lead = (None,) * len(index)

    def body(chip_ref, s_ref, o_ref):
        del chip_ref
        o_ref[...] = s_ref[...].astype(BF16)

    return pl.pallas_call(
        body, out_shape=jax.ShapeDtypeStruct((N_CHIPS, rows, cols), BF16),
        grid_spec=pltpu.PrefetchScalarGridSpec(
            num_scalar_prefetch=1, grid=(rows // tr,),
            in_specs=[pl.BlockSpec(lead + (tr, cols), lambda r, chip_ref: index + (r, 0))],
            out_specs=pl.BlockSpec((None, tr, cols), lambda r, chip_ref: (chip_ref[0], r, 0))),
        name=name, compiler_params=_params(("parallel",)),
    )(chip, param)


GATHER_STAGES = ((0,), (1,), (2,), (3, 4, 5))
REDUCE_STAGES = ((5, 4, 3), (2,), (1,), (0,))


def _gather_copies(slots, x, y, c):
    me = 2 * x + y
    out = []
    for s in slots:
        for fx, fy in CHIP_FLIPS:
            mine = _half(s.at[me], 0, c)
            out.append((mine, mine, _half(s.at[2 * (x ^ fx) + (y ^ fy)], 0, c), (x ^ fx, y ^ fy, c)))
    return out


class WeightStream:
    def __init__(self, shards, chip, after=()):
        self.pending, self.ready = {}, {}
        token = tuple(after)
        for si, units in enumerate(GATHER_STAGES):
            slots = [cast_into_slot(f"cast_{u}_{t}", p, idx, chip) for u in units for t, (p, idx) in enumerate(shards[u])]
            send, recv, slots, tok = split_start(f"gather_start_{si}", slots, token, 3 * len(slots), _gather_copies)
            self.pending[si] = (send, recv, slots)
            token = (tok,)
        self.token = token

    def get(self, u, after):
        if u not in self.ready:
            si = next(k for k, units in enumerate(GATHER_STAGES) if u in units)
            send, recv, slots = self.pending.pop(si)
            slots = split_wait(f"gather_wait_{si}", slots, send, recv, (after,) + self.token, _gather_copies)
            self.token = ()

            @_plan(0, 3 * len(slots))
            def to_sibling(ins, outs, x, y, c):
                sends, recvs = [], []
                for o in outs:
                    for fx, fy in CHIP_FLIPS:
                        slab = o.at[2 * (x ^ fx) + (y ^ fy)]
                        sends.append((_half(slab, 0, c), _half(slab, 0, c), (x, y, 1 - c)))
                        recvs.append(_half(slab, 0, 1 - c))
                return [], sends, recvs

            shapes = [jax.ShapeDtypeStruct(s.shape, BF16) for s in slots]
            slots = exchange(f"gather_sibling_{si}", slots, shapes, {k: k for k in range(len(slots))}, to_sibling)
            for i, v in enumerate(GATHER_STAGES[si]):
                self.ready[v] = tuple(slots[3 * i:3 * i + 3])
        return self.ready[u]


def _reduce_copies(bufs, x, y, c):
    n = len(bufs) // 2
    out = []
    for s, land in zip(bufs[:n], bufs[n:]):
        for k, (fx, fy) in enumerate(CHIP_FLIPS):
            out.append((s.at[2 * (x ^ fx) + (y ^ fy)], land.at[k], land.at[k], (x ^ fx, y ^ fy, c)))
    return out


GRAD_SLOTS = {"gate": (2 * DEPTH, FF_SHARD, D_MODEL), "up": (2 * DEPTH, FF_SHARD, D_MODEL),
              "down": (2 * DEPTH, FF_SHARD, D_MODEL), "in": (DEPTH, D_MODEL, IN_SHARD),
              "br": (DEPTH, BR_ROWS, D_SHARD), "out": (DEPTH, D_SHARD, D_MODEL)}


def _unit_tensors(u):
    l, j = divmod(u, 3)
    if j == 1:
        return [("in", l), ("br", l), ("out", l)]
    return [(k, 2 * l + j // 2) for k in ("gate", "up", "down")]


class GradStream:
    def __init__(self, chip, core):
        self.core = core
        self.place = jnp.concatenate([chip, core])
        self.held, self.flying = {}, []
        self.full = {k: lax.empty(shape, F32) for k, shape in GRAD_SLOTS.items()}

    def put(self, u, grads, after=()):
        self.held[u] = grads
        si = len(self.flying)
        units = REDUCE_STAGES[si]
        if not all(v in self.held for v in units):
            return jnp.zeros((1, 1), F32)
        gs = [g for v in units for g in self.held[v]]

        @_plan(0, len(gs))
        def swap_halves(ins, outs, x, y, c):
            sends = [(_half(g, 1, 1 - c), o, (x, y, 1 - c)) for g, o in zip(ins, outs)]
            return [], sends, list(outs)

        half_shapes = [jax.ShapeDtypeStruct((N_CHIPS, g.shape[1] // 2, g.shape[2]), BF16) for g in gs]
        landed = exchange(f"reduce_swap_{si}", gs + list(after), half_shapes, {}, swap_halves)
        sums = [_add_halves(g, la, self.core) for g, la in zip(gs, landed)]
        landing = [lax.empty((3,) + s.shape[1:], BF16) for s in sums]
        send, recv, bufs, token = split_start(f"reduce_start_{si}", sums + landing, (), 3 * len(sums), _reduce_copies)
        self.flying.append((send, recv, bufs, [t for v in units for t in _unit_tensors(v)]))
        return token[0:1, 0:1]

    def finish(self, after):
        for si, (send, recv, bufs, tensors) in enumerate(self.flying):
            bufs = split_wait(f"reduce_wait_{si}", bufs, send, recv, tuple(after), _reduce_copies)
            n = len(tensors)
            for (name, slot), s, land in zip(tensors, bufs[:n], bufs[n:]):
                self.full[name] = _add_chips(s, land, self.place, self.full[name], slot)
        names = list(self.full)

        @_plan(0, len(names))
        def share_halves(ins, outs, x, y, c):
            sends = [(_half(o, 1, c), _half(o, 1, c), (x, y, 1 - c)) for o in outs]
            return [], sends, [_half(o, 1, 1 - c) for o in outs]

        shapes = [jax.ShapeDtypeStruct(self.full[k].shape, F32) for k in names]
        out = exchange("reduce_share_halves", [self.full[k] for k in names], shapes, {k: k for k in range(len(names))},
                       share_halves)
        return dict(zip(names, out))


def _add_halves(g, landed, core):
    _, rh, cols = landed.shape
    tr = _row_tile(rh, cols)
    per_half = rh // tr

    def body(core_ref, g_ref, la_ref, o_ref):
        del core_ref
        o_ref[...] = (g_ref[...].astype(F32) + la_ref[...].astype(F32)).astype(BF16)

    blk = (None, tr, cols)
    return pl.pallas_call(
        body, out_shape=jax.ShapeDtypeStruct(landed.shape, BF16),
        grid_spec=pltpu.PrefetchScalarGridSpec(
            num_scalar_prefetch=1, grid=(N_CHIPS, per_half),
            in_specs=[pl.BlockSpec(blk, lambda j, r, core_ref: (j, core_ref[0] * per_half + r, 0)),
                      pl.BlockSpec(blk, lambda j, r, core_ref: (j, r, 0))],
            out_specs=pl.BlockSpec(blk, lambda j, r, core_ref: (j, r, 0))),
        name="reduce_add_halves", compiler_params=_params(("parallel", "parallel")),
    )(core, g, landed)


def _add_chips(sums, landed, place, full, slot):
    _, rh, cols = sums.shape
    tr = _row_tile(rh, cols)
    per_half = rh // tr

    def body(place_ref, s_ref, la_ref, full_in, o_ref):
        del place_ref, full_in
        o_ref[...] = ((s_ref[...].astype(F32) + la_ref[0].astype(F32)) + la_ref[1].astype(F32)) + la_ref[2].astype(F32)

    return pl.pallas_call(
        body, out_shape=jax.ShapeDtypeStruct(full.shape, F32),
        grid_spec=pltpu.PrefetchScalarGridSpec(
            num_scalar_prefetch=1, grid=(per_half,),
            in_specs=[pl.BlockSpec((None, tr, cols), lambda r, place_ref: (place_ref[0], r, 0)),
                      pl.BlockSpec((3, tr, cols), lambda r, place_ref: (0, r, 0)), ANY],
            out_specs=pl.BlockSpec((None, tr, cols), lambda r, place_ref: (slot, place_ref[1] * per_half + r, 0))),
        input_output_aliases={3: 0}, name="reduce_add_chips", compiler_params=_params(("parallel",)),
    )(place, sums, landed, full)


def sum_devices(parts):
    def body(p_ref, o_ref):
        acc = p_ref[0]
        for d in range(1, N_DEV):
            acc = acc + p_ref[d]
        o_ref[...] = acc

    return pl.pallas_call(body, out_shape=jax.ShapeDtypeStruct(parts.shape[1:], F32), name="sum_devices")(parts)


ADA_SHARD = 9 * D_MODEL // N_CHIPS
ADA_TILE = 768
ADA_ROWS = 16


def ada_forward(c_rows, w_ada, b_shard):
    def body(c_ref, w_ref, b_ref, o_ref):
        cv = c_ref[...]
        o_ref[...] = _dot((cv * _sigmoid(cv)).astype(BF16), w_ref[...].astype(BF16), NN) + b_ref[...]

    return pl.pallas_call(
        body, grid=(DEPTH, ADA_SHARD // ADA_TILE),
        in_specs=[pl.BlockSpec((ADA_ROWS, D_MODEL), lambda l, n: (0, 0)),
                  pl.BlockSpec((None, D_MODEL, ADA_TILE), lambda l, n: (l, 0, n)),
                  pl.BlockSpec((None, 1, ADA_TILE), lambda l, n: (l, 0, n))],
        out_specs=pl.BlockSpec((None, ADA_ROWS, ADA_TILE), lambda l, n: (l, 0, n)),
        out_shape=jax.ShapeDtypeStruct((DEPTH, ADA_ROWS, ADA_SHARD), F32),
        name="ada_forward", compiler_params=_params(("parallel", "parallel")),
    )(c_rows, w_ada, b_shard)


def ada_backward(c_rows, dmod_rows):
    def body(c_ref, d_ref, o_ref):
        cv = c_ref[...]
        o_ref[...] = _dot((cv * _sigmoid(cv)).astype(BF16), d_ref[...].astype(BF16), TN)

    return pl.pallas_call(
        body, grid=(DEPTH, ADA_SHARD // ADA_TILE),
        in_specs=[pl.BlockSpec((ADA_ROWS, D_MODEL), lambda l, n: (0, 0)),
                  pl.BlockSpec((None, ADA_ROWS, ADA_TILE), lambda l, n: (l, 0, n))],
        out_specs=pl.BlockSpec((None, D_MODEL, ADA_TILE), lambda l, n: (l, 0, n)),
        out_shape=jax.ShapeDtypeStruct((DEPTH, D_MODEL, ADA_SHARD), F32),
        name="ada_backward", compiler_params=_params(("parallel", "parallel")),
    )(c_rows, dmod_rows)


def adamw(name, w, g, m, v):
    shape = w.shape
    cols = shape[-1]
    rows = w.size // cols
    tr = _row_tile(rows, cols) if rows % 16 == 0 else rows
    c1 = 1.0 / (1.0 - ADAM_B1 ** ADAM_STEP)
    c2 = 1.0 / (1.0 - ADAM_B2 ** ADAM_STEP)

    def body(w_ref, g_ref, m_ref, v_ref, go_ref, d_ref, mo_ref, vo_ref):
        gv = g_ref[...]
        mn = ADAM_B1 * m_ref[...] + (1.0 - ADAM_B1) * gv
        vn = ADAM_B2 * v_ref[...] + (1.0 - ADAM_B2) * (gv * gv)
        go_ref[...] = gv
        mo_ref[...] = mn
        vo_ref[...] = vn
        d_ref[...] = -ADAM_LR * ((mn * c1) / (jnp.sqrt(vn * c2) + ADAM_EPS) + ADAM_WD * w_ref[...])

    spec = pl.BlockSpec((tr, cols), lambda i: (i, 0))
    out = jax.ShapeDtypeStruct((rows, cols), F32)
    res = pl.pallas_call(
        body, grid=(rows // tr,), in_specs=[spec] * 4, out_specs=[spec] * 4, out_shape=[out] * 4,
        name=name, compiler_params=_params(("parallel",)),
    )(*[t.reshape(rows, cols) for t in (w, g, m, v)])
    return tuple(r.reshape(shape) for r in res)


def _pack(parts, rows):
    flat = jnp.concatenate([p.reshape(-1) for p in parts])
    return jnp.pad(flat, (0, rows * 128 - flat.size)).reshape(rows, 128)


def _unpack(flat, shapes):
    out, at = [], 0
    for s in shapes:
        n = math.prod(s)
        out.append(flat[at:at + n].reshape(s))
        at += n
    return out


def kernel(x, c, w_ada, b_ada, norm_gain, w_ffn_gate, w_ffn_up, w_ffn_down, w_in, w_br_sb, w_br_dil, w_br_swa, w_out, sinks, rel_bias, final_gain, loss_target, m_w_ada, m_b_ada, m_norm_gain, m_w_ffn_gate, m_w_ffn_up, m_w_ffn_down, m_w_in, m_w_br_sb, m_w_br_dil, m_w_br_swa, m_w_out, m_sinks, m_rel_bias, m_final_gain, v_w_ada, v_b_ada, v_norm_gain, v_w_ffn_gate, v_w_ffn_up, v_w_ffn_down, v_w_in, v_w_br_sb, v_w_br_dil, v_w_br_swa, v_w_out, v_sinks, v_rel_bias, v_final_gain):
    xi, yi, ci = _position()
    chip = 2 * xi + yi
    dev = 2 * chip + ci

    c_all = all_gather_small("gather_c", c.reshape(8, 128)).reshape(N_DEV, D_MODEL)
    c_rows = jnp.pad(c_all, ((0, ADA_ROWS - N_DEV), (0, 0)))
    b_shard = lax.dynamic_slice_in_dim(b_ada, chip * ADA_SHARD, ADA_SHARD, axis=1).reshape(DEPTH, 1, ADA_SHARD)
    mod_shard = ada_forward(c_rows, w_ada, b_shard)[:, :N_DEV]
    n_mod = DEPTH * N_DEV * ADA_SHARD
    gathered = all_gather_small("gather_mod", _pack([mod_shard, norm_gain], 304))[::2].reshape(N_CHIPS, -1)
    mod_all = gathered[:, :n_mod].reshape(N_CHIPS, DEPTH, N_DEV, ADA_SHARD)
    mod = lax.dynamic_index_in_dim(mod_all, dev, axis=2, keepdims=False)
    mod = mod.transpose(1, 0, 2).reshape(DEPTH, 3, 3, D_MODEL)
    gains = gathered[:, n_mod:n_mod + DEPTH * 3 * D_SHARD].reshape(N_CHIPS, DEPTH, 3, D_SHARD)
    gains = gains.transpose(1, 2, 0, 3).reshape(DEPTH, 3, D_MODEL)

    chip_i, core_i = chip.astype(jnp.int32).reshape(1), ci.astype(jnp.int32).reshape(1)
    w_br = jnp.concatenate([w_br_sb, w_br_dil, w_br_swa], axis=1)
    transposed = (3, 4)
    w_gate_t, w_up_t = jnp.swapaxes(w_ffn_gate, 2, 3), jnp.swapaxes(w_ffn_up, 2, 3)
    shards = []
    for l in range(DEPTH):
        ffn = [[(w_gate_t, (l, f)), (w_up_t, (l, f)), (w_ffn_down, (l, f))] for f in range(2)]
        shards += [ffn[0], [(w_in, (l,)), (w_br, (l,)), (w_out, (l,))], ffn[1]]
    weights_in = WeightStream(shards, chip_i, (gathered,))
    grads_out = GradStream(chip_i, core_i)

    loss, dx, dmod, dgains, dfinal, dsinks, drel, last_grads = device_step(
        x[0], loss_target[0], mod, gains, final_gain, sinks, rel_bias, weights_in.get, grads_out.put)

    small_shapes = [(DEPTH, 9 * D_MODEL), (DEPTH, 3, D_MODEL), (D_MODEL,), (DEPTH, H_SWA_Q), (N_BUCKETS, 12), (1,)]
    small_all = all_gather_small("gather_small_grads", _pack([dmod, dgains, dfinal, dsinks, drel, loss[0, 0:1]], 208))
    started = grads_out.put(0, last_grads, after=(small_all,))
    small_all = small_all + started
    g_b_ada, g_gain_full, g_final, g_sinks, g_rel, loss_sum = _unpack(sum_devices(small_all).reshape(-1), small_shapes)
    g_gain = lax.dynamic_slice_in_dim(g_gain_full, chip * D_SHARD, D_SHARD, axis=2)
    dmod_all = small_all.reshape(N_DEV, -1)[:, :DEPTH * 9 * D_MODEL].reshape(N_DEV, DEPTH, 9 * D_MODEL)
    dmod_rows = lax.dynamic_slice_in_dim(dmod_all, chip * ADA_SHARD, ADA_SHARD, axis=2).transpose(1, 0, 2)
    g_w_ada = ada_backward(c_rows, jnp.pad(dmod_rows, ((0, 0), (0, ADA_ROWS - N_DEV), (0, 0))))

    weights = [w_ada, b_ada, norm_gain, w_ffn_gate, w_ffn_up, w_ffn_down, w_in, w_br_sb, w_br_dil, w_br_swa, w_out,
               sinks, rel_bias, final_gain]
    ms = [m_w_ada, m_b_ada, m_norm_gain, m_w_ffn_gate, m_w_ffn_up, m_w_ffn_down, m_w_in, m_w_br_sb, m_w_br_dil,
          m_w_br_swa, m_w_out, m_sinks, m_rel_bias, m_final_gain]
    vs = [v_w_ada, v_b_ada, v_norm_gain, v_w_ffn_gate, v_w_ffn_up, v_w_ffn_down, v_w_in, v_w_br_sb, v_w_br_dil,
          v_w_br_swa, v_w_out, v_sinks, v_rel_bias, v_final_gain]
    grads = [g_w_ada, g_b_ada, g_gain] + [None] * 8 + [g_sinks, g_rel, g_final]

    small = (1, 2, 11, 12, 13)
    deltas, new_ms, new_vs = [None] * 14, [None] * 14, [None] * 14
    _, deltas[0], new_ms[0], new_vs[0] = adamw("adamw_0", weights[0], grads[0], ms[0], vs[0])
    shapes = [weights[k].shape for k in small]
    packed = [_pack([t[k] for k in small], 168) for t in (weights, grads, ms, vs)]
    for dst, res in zip((deltas, new_ms, new_vs), adamw("adamw_small", *packed)[1:]):
        for k, t in zip(small, _unpack(res.reshape(-1), shapes)):
            dst[k] = t

    g = grads_out.finish((dx, deltas[0], deltas[1]))
    g_br = g["br"]
    grads[3:11] = [g["gate"].reshape(w_gate_t.shape), g["up"].reshape(w_up_t.shape),
                   g["down"].reshape(w_ffn_down.shape), g["in"], g_br[:, 0:256], g_br[:, 256:384], g_br[:, 384:768],
                   g["out"]]
    for k in range(3, 11):
        state = [weights[k], ms[k], vs[k]]
        if k in transposed:
            state = [jnp.swapaxes(t, 2, 3) for t in state]
        out = adamw(f"adamw_{k}", state[0], grads[k], state[1], state[2])
        if k in transposed:
            out = [jnp.swapaxes(t, 2, 3) for t in out]
        grads[k], deltas[k], new_ms[k], new_vs[k] = out
    return (loss_sum[0], dx[None], *grads, *deltas, *new_ms, *new_vs)
```

```python
import functools
import math

import jax
import jax.numpy as jnp
from jax import lax
from jax.experimental import pallas as pl
from jax.experimental.pallas import tpu as pltpu

F32 = jnp.float32
BF16 = jnp.bfloat16

D_MODEL = 1024
SEQ = 2048
DEPTH = 2
HEAD_DIM = 64
BLK = 128
H_SB = 4
DIL_PATTERNS = ((128, 1), (512, 4), (2048, 16))
H_PER_DIL = 2
H_DIL = 6
H_SWA_Q = 6
H_SWA_KV = 2
SWA_WINDOW = 128
N_BUCKETS = 32
MAX_REL_DIST = 2048
D_FF = 2816
RMS_EPS = 1e-6
N_CHIPS = 4
N_DEV = 8
FF_SHARD = D_FF // N_CHIPS
D_QKV = 2560
D_IN = D_QKV + 3 * D_MODEL
IN_SHARD = D_IN // N_CHIPS
D_SHARD = D_MODEL // N_CHIPS
BR_ROWS = 768
NEG = -1e30
QK_SCALE = HEAD_DIM ** -0.5

ADAM_LR = 0.001
ADAM_B1 = 0.9
ADAM_B2 = 0.999
ADAM_EPS = 1e-08
ADAM_WD = 0.01
ADAM_STEP = 10

VMEM_LIMIT = 56 * 1024 * 1024
ROW_TILE = 256
MM_TILE = 1024

NN = (((1,), (0,)), ((), ()))
NT = (((1,), (1,)), ((), ()))
TN = (((0,), (0,)), ((), ()))


def _params(sem=None):
    return pltpu.CompilerParams(dimension_semantics=sem, vmem_limit_bytes=VMEM_LIMIT)


def _dot(a, b, dims):
    return lax.dot_general(a, b, dims, preferred_element_type=F32)


def _sigmoid(x):
    return 1.0 / (1.0 + jnp.exp(-x))


def _matmul(name, grid, nk, k_axis, dims, n_pairs, in_specs, out_specs, out_shape, acc_shape, epilogue,
            operands, sem, aliases=None, prologue=None):
    n_in = len(in_specs)
    n_out = len(out_specs)

    def partial(ins):
        tot = None
        for p in range(n_pairs):
            a = ins[2 * p][...]
            if prologue is not None:
                a = prologue(p, a, ins)
            d = _dot(a, ins[2 * p + 1][...], dims)
            tot = d if tot is None else tot + d
        return tot

    def body(*refs):
        ins, outs = refs[:n_in], refs[n_in:n_in + n_out]
        ids = tuple(pl.program_id(a) for a in range(len(grid)))
        if nk == 1:
            epilogue(partial(ins), ins, outs, ids)
            return
        acc = refs[n_in + n_out]
        k = ids[k_axis]

        @pl.when(k == 0)
        def _():
            acc[...] = partial(ins)

        @pl.when(k > 0)
        def _():
            acc[...] += partial(ins)

        @pl.when(k == nk - 1)
        def _():
            epilogue(acc[...], ins, outs, ids)

    return pl.pallas_call(
        body, grid=grid, in_specs=in_specs, out_specs=out_specs, out_shape=out_shape,
        scratch_shapes=[] if nk == 1 else [pltpu.VMEM(acc_shape, F32)],
        input_output_aliases=aliases or {}, name=name, compiler_params=_params(sem),
    )(*operands)


def _row_spec(width=D_MODEL):
    return pl.BlockSpec((ROW_TILE, width), lambda i: (i, 0))


def _vec_spec(rows=1, width=D_MODEL):
    return pl.BlockSpec((rows, width), lambda i: (0, 0))


class Row:
    def __init__(self, table, index):
        self.table, self.index = table, index

    def spec(self):
        index = self.index
        return pl.BlockSpec((None, 1, D_MODEL), lambda *ids: (index, 0, 0))


def _slot_spec(u):
    return pl.BlockSpec((8, D_MODEL), lambda *ids: (u, 0))


def prenorm(x, gain, scale, shift):
    def body(x_ref, g_ref, sc_ref, sh_ref, h_ref):
        xv = x_ref[...]
        r = lax.rsqrt(jnp.mean(xv * xv, axis=-1, keepdims=True) + RMS_EPS)
        h_ref[...] = (((xv * r) * g_ref[...]) * (1.0 + sc_ref[...]) + sh_ref[...]).astype(BF16)

    return pl.pallas_call(
        body, grid=(SEQ // ROW_TILE,), in_specs=[_row_spec(), gain.spec(), scale.spec(), shift.spec()],
        out_specs=_row_spec(), out_shape=jax.ShapeDtypeStruct((SEQ, D_MODEL), BF16),
        name="prenorm", compiler_params=_params(("parallel",)),
    )(x, gain.table, scale.table, shift.table)


def resid_bwd(dxo, f, coef, mult, sums, u):
    def body(dx_ref, f_ref, c_ref, sums_in, df_ref, dc_ref):
        del sums_in
        dx = dx_ref[...]
        df_ref[...] = (dx * (mult * c_ref[...])).astype(BF16)
        part = mult * jnp.sum(dx * f_ref[...], axis=0, keepdims=True)

        @pl.when(pl.program_id(0) == 0)
        def _():
            dc_ref[...] = jnp.zeros_like(dc_ref)

        dc_ref[0:1, :] += part

    return pl.pallas_call(
        body, grid=(SEQ // ROW_TILE,),
        in_specs=[_row_spec(), _row_spec(), coef.spec(), pl.BlockSpec(memory_space=pl.ANY)],
        out_specs=[_row_spec(), _slot_spec(u)],
        out_shape=[jax.ShapeDtypeStruct((SEQ, D_MODEL), BF16), jax.ShapeDtypeStruct(sums.shape, F32)],
        input_output_aliases={3: 1}, name="resid_bwd", compiler_params=_params(("arbitrary",)),
    )(dxo, f, coef.table, sums)


def final_loss(x, gain, target):
    def body(x_ref, g_ref, t_ref, loss_ref, dx_ref, dg_ref):
        xv = x_ref[...]
        g = g_ref[...]
        r = lax.rsqrt(jnp.mean(xv * xv, axis=-1, keepdims=True) + RMS_EPS)
        xh = xv * r
        e = xh * g - t_ref[...]
        part = 0.5 * jnp.sum(jnp.mean(e * e, axis=-1, keepdims=True), axis=0, keepdims=True)
        dy = e * (1.0 / D_MODEL)
        dyg = dy * g
        dx_ref[...] = r * (dyg - xh * jnp.mean(dyg * xh, axis=-1, keepdims=True))

        @pl.when(pl.program_id(0) == 0)
        def _():
            loss_ref[...] = jnp.zeros_like(loss_ref)
            dg_ref[...] = jnp.zeros_like(dg_ref)

        loss_ref[...] += jnp.broadcast_to(part, loss_ref.shape)
        dg_ref[0:1, :] += jnp.sum(dy * xh, axis=0, keepdims=True)

    return pl.pallas_call(
        body, grid=(SEQ // ROW_TILE,), in_specs=[_row_spec(), _vec_spec(), _row_spec()],
        out_specs=[_vec_spec(8, 128), _row_spec(), _vec_spec(8)],
        out_shape=[jax.ShapeDtypeStruct((8, 128), F32), jax.ShapeDtypeStruct((SEQ, D_MODEL), F32),
                   jax.ShapeDtypeStruct((8, D_MODEL), F32)],
        name="final_loss", compiler_params=_params(("arbitrary",)),
    )(x, gain, target)


def _prenorm_bwd_epilogue(dh, x_ref, dxo_ref, g_ref, sc_ref, dx_ref, stats_ref, first):
    xv = x_ref[...]
    g = g_ref[...]
    r = lax.rsqrt(jnp.mean(xv * xv, axis=-1, keepdims=True) + RMS_EPS)
    xh = xv * r
    dn = dh * (1.0 + sc_ref[...])
    dxh = dn * g
    dx_ref[...] = dxo_ref[...] + r * (dxh - xh * jnp.mean(dxh * xh, axis=-1, keepdims=True))

    @pl.when(first)
    def _():
        stats_ref[...] = jnp.zeros_like(stats_ref)

    stats_ref[0:1, :] += jnp.sum(dh, axis=0, keepdims=True)
    stats_ref[1:2, :] += jnp.sum(dh * (xh * g), axis=0, keepdims=True)
    stats_ref[2:3, :] += jnp.sum(dn * xh, axis=0, keepdims=True)


def ffn_up(h, wg_all, wu_all):
    def body(h_ref, wg_ref, wu_ref, a_ref, b_ref, s_ref):
        hv = h_ref[...]
        a = _dot(hv, wg_ref[...], NT)
        b = _dot(hv, wu_ref[...], NT)
        a_ref[...] = a
        b_ref[...] = b
        s_ref[...] = (a * _sigmoid(a) * b).astype(BF16)

    w_spec = pl.BlockSpec((None, FF_SHARD, D_MODEL), lambda j, i: (j, 0, 0))
    o_spec = pl.BlockSpec((None, MM_TILE, FF_SHARD), lambda j, i: (j, i, 0))
    hid = (N_CHIPS, SEQ, FF_SHARD)
    return pl.pallas_call(
        body, grid=(N_CHIPS, SEQ // MM_TILE),
        in_specs=[pl.BlockSpec((MM_TILE, D_MODEL), lambda j, i: (i, 0)), w_spec, w_spec],
        out_specs=[o_spec, o_spec, o_spec],
        out_shape=[jax.ShapeDtypeStruct(hid, F32), jax.ShapeDtypeStruct(hid, F32), jax.ShapeDtypeStruct(hid, BF16)],
        name="ffn_up", compiler_params=_params(("parallel", "parallel")),
    )(h, wg_all, wu_all)


def matmul_residual(name, a, a_spec, w_all, w_spec, x, coef, mult):
    def epilogue(acc, ins, outs, ids):
        outs[0][...] = acc
        outs[1][...] = ins[2][...] + (mult * ins[3][...]) * acc

    row = pl.BlockSpec((MM_TILE, D_MODEL), lambda i, j: (i, 0))
    return _matmul(
        name, (SEQ // MM_TILE, N_CHIPS), N_CHIPS, 1, NN, 1,
        [a_spec, w_spec, row, coef.spec()], [row, row],
        [jax.ShapeDtypeStruct((SEQ, D_MODEL), F32)] * 2, (MM_TILE, D_MODEL), epilogue,
        (a, w_all, x, coef.table), ("parallel", "arbitrary"))


def ffn_down(s, wd_all, x, gate):
    return matmul_residual(
        "ffn_down", s, pl.BlockSpec((None, MM_TILE, FF_SHARD), lambda i, j: (j, i, 0)),
        wd_all, pl.BlockSpec((None, FF_SHARD, D_MODEL), lambda i, j: (j, 0, 0)), x, gate, 0.5)


def ffn_bwd_hidden(df, wd_all, a, b):
    def epilogue(ds, ins, outs, ids):
        av, bv = ins[2][...], ins[3][...]
        sig = _sigmoid(av)
        outs[0][...] = (ds * bv * (sig * (1.0 + av * (1.0 - sig)))).astype(BF16)
        outs[1][...] = (ds * (av * sig)).astype(BF16)

    hid_spec = pl.BlockSpec((None, MM_TILE, FF_SHARD), lambda j, i: (j, i, 0))
    hid = jax.ShapeDtypeStruct((N_CHIPS, SEQ, FF_SHARD), BF16)
    return _matmul(
        "ffn_bwd_hidden", (N_CHIPS, SEQ // MM_TILE), 1, None, NT, 1,
        [pl.BlockSpec((MM_TILE, D_MODEL), lambda j, i: (i, 0)),
         pl.BlockSpec((None, FF_SHARD, D_MODEL), lambda j, i: (j, 0, 0)), hid_spec, hid_spec],
        [hid_spec, hid_spec], [hid, hid], None, epilogue, (df, wd_all, a, b), ("parallel", "parallel"))


def grad_weight(name, lhs, lhs_spec, rhs, rhs_spec, shape):
    def epilogue(acc, ins, outs, ids):
        outs[0][...] = acc.astype(BF16)

    return _matmul(
        name, (N_CHIPS, SEQ // MM_TILE), SEQ // MM_TILE, 1, TN, 1,
        [lhs_spec, rhs_spec], [pl.BlockSpec((None,) + shape, lambda j, k: (j, 0, 0))],
        [jax.ShapeDtypeStruct((N_CHIPS,) + shape, BF16)], shape, epilogue, (lhs, rhs), ("parallel", "arbitrary"))[0]


def ffn_grad_weights(h, s, df, da, db):
    tok = pl.BlockSpec((MM_TILE, D_MODEL), lambda j, k: (k, 0))
    hid = pl.BlockSpec((None, MM_TILE, FF_SHARD), lambda j, k: (j, k, 0))
    return (grad_weight("grad_w_gate", da, hid, h, tok, (FF_SHARD, D_MODEL)),
            grad_weight("grad_w_up", db, hid, h, tok, (FF_SHARD, D_MODEL)),
            grad_weight("grad_w_down", s, hid, df, tok, (FF_SHARD, D_MODEL)))


def matmul_prenorm_bwd(name, dims, pairs, pair_specs, x, dxo, gain, scale, sums, u):
    n = len(pairs)

    def epilogue(dh, ins, outs, ids):
        _prenorm_bwd_epilogue(dh, ins[n], ins[n + 1], ins[n + 2], ins[n + 3], outs[0], outs[1], ids[0] == 0)

    row = pl.BlockSpec((MM_TILE, D_MODEL), lambda i, j: (i, 0))
    return _matmul(
        name, (SEQ // MM_TILE, N_CHIPS), N_CHIPS, 1, dims, len(pairs) // 2,
        list(pair_specs) + [row, row, gain.spec(), scale.spec(), pl.BlockSpec(memory_space=pl.ANY)],
        [row, _slot_spec(u)], [jax.ShapeDtypeStruct((SEQ, D_MODEL), F32), jax.ShapeDtypeStruct(sums.shape, F32)],
        (MM_TILE, D_MODEL), epilogue, tuple(pairs) + (x, dxo, gain.table, scale.table, sums),
        ("arbitrary", "arbitrary"), aliases={n + 4: 1})


def ffn_bwd_input(da, db, wg_all, wu_all, x, dxo, gain, scale, sums, u):
    hid = pl.BlockSpec((None, MM_TILE, FF_SHARD), lambda i, j: (j, i, 0))
    w = pl.BlockSpec((None, FF_SHARD, D_MODEL), lambda i, j: (j, 0, 0))
    return matmul_prenorm_bwd("ffn_bwd_input", NN, (da, wg_all, db, wu_all), (hid, w, hid, w), x, dxo, gain, scale,
                              sums, u)


def in_proj(h, w_all):
    def epilogue(acc, ins, outs, ids):
        outs[0][...] = acc
        outs[1][...] = acc.astype(BF16)

    out = pl.BlockSpec((MM_TILE, IN_SHARD), lambda j, i: (i, j))
    return _matmul(
        "in_proj", (N_CHIPS, SEQ // MM_TILE), 1, None, NN, 1,
        [pl.BlockSpec((MM_TILE, D_MODEL), lambda j, i: (i, 0)),
         pl.BlockSpec((None, D_MODEL, IN_SHARD), lambda j, i: (j, 0, 0))],
        [out, out], [jax.ShapeDtypeStruct((SEQ, D_IN), F32), jax.ShapeDtypeStruct((SEQ, D_IN), BF16)],
        None, epilogue, (h, w_all), ("parallel", "parallel"))


_GATE_BLOCK0 = D_QKV // D_SHARD


def _branch_products(o, w_ref):
    ob = o.astype(BF16)
    return (_dot(ob[:, 0:256], w_ref[0:256, :], NN), _dot(ob[:, 256:384], w_ref[256:384, :], NN),
            _dot(ob[:, 384:768], w_ref[384:768, :], NN))


def merge_branches(o_cat, wbr_all, proj):
    def body(o_ref, w_ref, g0_ref, g1_ref, g2_ref, m_ref):
        u = _branch_products(o_ref[...], w_ref)
        m_ref[...] = (_sigmoid(g0_ref[...]) * u[0] + _sigmoid(g1_ref[...]) * u[1]
                      + _sigmoid(g2_ref[...]) * u[2]).astype(BF16)

    def gate_spec(b):
        return pl.BlockSpec((MM_TILE, D_SHARD), lambda i, j: (i, _GATE_BLOCK0 + 4 * b + j))

    return pl.pallas_call(
        body, grid=(SEQ // MM_TILE, N_CHIPS),
        in_specs=[pl.BlockSpec((MM_TILE, BR_ROWS), lambda i, j: (i, 0)),
                  pl.BlockSpec((None, BR_ROWS, D_SHARD), lambda i, j: (j, 0, 0)),
                  gate_spec(0), gate_spec(1), gate_spec(2)],
        out_specs=pl.BlockSpec((MM_TILE, D_SHARD), lambda i, j: (i, j)),
        out_shape=jax.ShapeDtypeStruct((SEQ, D_MODEL), BF16),
        name="merge_branches", compiler_params=_params(("parallel", "parallel")),
    )(o_cat, wbr_all, proj, proj, proj)


def out_proj(merged, wout_all, x, gate):
    return matmul_residual(
        "out_proj", merged, pl.BlockSpec((MM_TILE, D_SHARD), lambda i, j: (i, j)),
        wout_all, pl.BlockSpec((None, D_SHARD, D_MODEL), lambda i, j: (j, 0, 0)), x, gate, 1.0)


def merge_bwd(dmo, wout_all, o_cat, wbr_all, proj):
    def epilogue(dm, ins, outs, ids):
        u = _branch_products(ins[2][...], ins[3])
        for b in range(3):
            sig = _sigmoid(ins[4 + b][...])
            outs[b][...] = (dm * sig).astype(BF16)
            outs[3 + b][...] = (dm * u[b] * (sig * (1.0 - sig))).astype(BF16)

    def gate_spec(b):
        return pl.BlockSpec((MM_TILE, D_SHARD), lambda j, i: (i, _GATE_BLOCK0 + 4 * b + j))

    col = pl.BlockSpec((MM_TILE, D_SHARD), lambda j, i: (i, j))
    du = jax.ShapeDtypeStruct((SEQ, D_MODEL), BF16)
    return _matmul(
        "merge_bwd", (N_CHIPS, SEQ // MM_TILE), 1, None, NT, 1,
        [pl.BlockSpec((MM_TILE, D_MODEL), lambda j, i: (i, 0)),
         pl.BlockSpec((None, D_SHARD, D_MODEL), lambda j, i: (j, 0, 0)),
         pl.BlockSpec((MM_TILE, BR_ROWS), lambda j, i: (i, 0)),
         pl.BlockSpec((None, BR_ROWS, D_SHARD), lambda j, i: (j, 0, 0)),
         gate_spec(0), gate_spec(1), gate_spec(2)],
        [col] * 6, [du] * 6,
        None, epilogue, (dmo, wout_all, o_cat, wbr_all, proj, proj, proj), ("parallel", "parallel"))


def branch_bwd_input(du, wbr_all):
    def body(d0_ref, d1_ref, d2_ref, w_ref, o_ref, acc):
        j = pl.program_id(1)
        parts = (_dot(d0_ref[...], w_ref[0:256, :], NT), _dot(d1_ref[...], w_ref[256:384, :], NT),
                 _dot(d2_ref[...], w_ref[384:768, :], NT))

        @pl.when(j == 0)
        def _():
            acc[:, 0:256], acc[:, 256:384], acc[:, 384:768] = parts

        @pl.when(j > 0)
        def _():
            acc[:, 0:256] += parts[0]
            acc[:, 256:384] += parts[1]
            acc[:, 384:768] += parts[2]

        @pl.when(j == N_CHIPS - 1)
        def _():
            o_ref[...] = acc[...]

    col = pl.BlockSpec((MM_TILE, D_SHARD), lambda i, j: (i, j))
    return pl.pallas_call(
        body, grid=(SEQ // MM_TILE, N_CHIPS),
        in_specs=[col, col, col, pl.BlockSpec((None, BR_ROWS, D_SHARD), lambda i, j: (j, 0, 0))],
        out_specs=pl.BlockSpec((MM_TILE, BR_ROWS), lambda i, j: (i, 0)),
        out_shape=jax.ShapeDtypeStruct((SEQ, BR_ROWS), F32),
        scratch_shapes=[pltpu.VMEM((MM_TILE, BR_ROWS), F32)],
        name="branch_bwd_input", compiler_params=_params(("parallel", "arbitrary")),
    )(du[0], du[1], du[2], wbr_all)


def branch_grad_weights(o_cat, du):
    def body(o_ref, d0_ref, d1_ref, d2_ref, g_ref, acc):
        k = pl.program_id(1)
        ob = o_ref[...].astype(BF16)
        parts = (_dot(ob[:, 0:256], d0_ref[...], TN), _dot(ob[:, 256:384], d1_ref[...], TN),
                 _dot(ob[:, 384:768], d2_ref[...], TN))

        @pl.when(k == 0)
        def _():
            acc[0:256, :], acc[256:384, :], acc[384:768, :] = parts

        @pl.when(k > 0)
        def _():
            acc[0:256, :] += parts[0]
            acc[256:384, :] += parts[1]
            acc[384:768, :] += parts[2]

        @pl.when(k == SEQ // MM_TILE - 1)
        def _():
            g_ref[...] = acc[...].astype(BF16)

    col = pl.BlockSpec((MM_TILE, D_SHARD), lambda j, k: (k, j))
    return pl.pallas_call(
        body, grid=(N_CHIPS, SEQ // MM_TILE),
        in_specs=[pl.BlockSpec((MM_TILE, BR_ROWS), lambda j, k: (k, 0)), col, col, col],
        out_specs=pl.BlockSpec((None, BR_ROWS, D_SHARD), lambda j, k: (j, 0, 0)),
        out_shape=jax.ShapeDtypeStruct((N_CHIPS, BR_ROWS, D_SHARD), BF16),
        scratch_shapes=[pltpu.VMEM((BR_ROWS, D_SHARD), F32)],
        name="branch_grad_weights", compiler_params=_params(("parallel", "arbitrary")),
    )(o_cat, du[0], du[1], du[2])


def mixer_bwd_input(dproj, win_all, x, dxo, gain, scale, sums, u):
    return matmul_prenorm_bwd(
        "mixer_bwd_input", NT, (dproj, win_all),
        (pl.BlockSpec((MM_TILE, IN_SHARD), lambda i, j: (i, j)),
         pl.BlockSpec((None, D_MODEL, IN_SHARD), lambda i, j: (j, 0, 0))), x, dxo, gain, scale, sums, u)


BATCH_QK = (((2,), (2,)), ((0,), (0,)))
BATCH_PV = (((2,), (1,)), ((0,), (0,)))
BATCH_TN = (((1,), (1,)), ((0,), (0,)))


SB_WIDTH = H_SB * HEAD_DIM
SB_ROWS = H_SB * BLK


def _split_dot(v, tri):
    hi = v.astype(BF16)
    lo = (v - hi.astype(F32)).astype(BF16)
    return _dot(hi, tri, NN) + _dot(lo, tri, NN)


def _tri(cmp):
    return cmp(lax.broadcasted_iota(jnp.int32, (BLK, BLK), 0), lax.broadcasted_iota(jnp.int32, (BLK, BLK), 1)).astype(BF16)


def _head_masks():
    lane = lax.broadcasted_iota(jnp.int32, (1, SB_WIDTH), 1) // HEAD_DIM
    return [lane == h for h in range(H_SB)]


def _stack_heads(x, masks):
    return jnp.concatenate([jnp.where(m, x, jnp.zeros_like(x)) for m in masks], axis=0)


def _merge_heads(y, masks):
    out = jnp.where(masks[0], y[0:BLK], 0.0)
    for h in range(1, H_SB):
        out = jnp.where(masks[h], y[h * BLK:(h + 1) * BLK], out)
    return out


def _sb_scores(q4, k_ref, j, diagonal):
    rows = pl.ds(pl.multiple_of(j * BLK, BLK), BLK)
    z = _dot(q4, k_ref[rows, :], NT)
    log_fail = -(jnp.maximum(z, 0.0) + jnp.log(1.0 + jnp.exp(-jnp.abs(z))))
    log_hit = z + log_fail
    before = None
    if diagonal:
        tile = (SB_ROWS, BLK)
        before = lax.broadcasted_iota(jnp.int32, tile, 1) < (lax.broadcasted_iota(jnp.int32, tile, 0) & (BLK - 1))
        log_fail = jnp.where(before, log_fail, 0.0)
    return rows, before, log_fail, log_hit


def _keep(before, x):
    return x if before is None else jnp.where(before, x, 0.0)


def sb_forward(qkv):
    def body(q_ref, k_ref, v_ref, o_ref, tot_ref):
        i = pl.program_id(0)
        masks = _head_masks()
        q4 = _stack_heads(q_ref[...] * QK_SCALE, masks)
        later = _tri(lambda r, c: r > c)

        def tiles(js, carry, diagonal):
            o, run = carry
            scores = [_sb_scores(q4, k_ref, j, diagonal) for j in js]
            acc = None
            for rows, before, log_fail, log_hit in scores:
                between = _split_dot(log_fail, later) + run
                w = _keep(before, jnp.exp(log_hit + between))
                part = _dot(w.astype(BF16), v_ref[rows, :], NN)
                acc = part if acc is None else acc + part
                run = run + jnp.sum(log_fail, axis=1, keepdims=True)
            return o + _merge_heads(acc, masks), run

        carry = tiles([i], (jnp.zeros((BLK, SB_WIDTH), F32), jnp.zeros((SB_ROWS, 1), F32)), True)
        carry = lax.cond((i & 1) == 1, lambda c: tiles([i - 1], c, False), lambda c: c, carry)
        first = i - 1 - (i & 1)
        o, run = lax.fori_loop(0, i // 2, lambda t, c: tiles([first - 2 * t, first - 2 * t - 1], c, False), carry)
        o_ref[...] = o
        tot_ref[...] = run

    return pl.pallas_call(
        body, grid=(N_BLK,),
        in_specs=[pl.BlockSpec((BLK, SB_WIDTH), lambda i: (i, 0)), pl.BlockSpec((SEQ, SB_WIDTH), lambda i: (0, 1)),
                  pl.BlockSpec((SEQ, SB_WIDTH), lambda i: (0, 2))],
        out_specs=[pl.BlockSpec((BLK, SB_WIDTH), lambda i: (i, 0)), pl.BlockSpec((None, SB_ROWS, 1), lambda i: (i, 0, 0))],
        out_shape=[jax.ShapeDtypeStruct((SEQ, SB_WIDTH), F32), jax.ShapeDtypeStruct((N_BLK, SB_ROWS, 1), F32)],
        name="sb_forward", compiler_params=_params(("parallel",)),
    )(qkv, qkv, qkv)


def sb_backward(qkv, total, do_cat):
    def body(q_ref, k_ref, v_ref, tot_ref, do_ref, dq_ref, dk_ref, dv_ref):
        i = pl.program_id(0)

        @pl.when(i == 0)
        def _():
            dk_ref[...] = jnp.zeros_like(dk_ref)
            dv_ref[...] = jnp.zeros_like(dv_ref)

        masks = _head_masks()
        q4 = _stack_heads(q_ref[...] * QK_SCALE, masks)
        do4 = _stack_heads(do_ref[...].astype(BF16), masks)
        total_v = tot_ref[...]
        upto = _tri(lambda r, c: r <= c)
        earlier = _tri(lambda r, c: r < c)

        def tiles(js, carry, diagonal):
            dq, seen, g_seen = carry
            scores = [_sb_scores(q4, k_ref, j, diagonal) for j in js]
            acc = None
            for rows, before, log_fail, log_hit in scores:
                between = total_v - (seen + _split_dot(log_fail, upto))
                w = _keep(before, jnp.exp(log_hit + between))
                g = _dot(do4, v_ref[rows, :], NT) * w
                g_earlier = g_seen + _split_dot(g, earlier)
                sig = jnp.exp(log_hit)
                dz = _keep(before, g * (1.0 - sig) - g_earlier * sig).astype(BF16)
                part = _dot(dz, k_ref[rows, :], NN)
                acc = part if acc is None else acc + part
                dk_ref[rows, :] += _dot(dz, q4, TN)
                dv_ref[rows, :] += _dot(w.astype(BF16), do4, TN)
                seen = seen + jnp.sum(log_fail, axis=1, keepdims=True)
                g_seen = g_seen + jnp.sum(g, axis=1, keepdims=True)
            return dq + _merge_heads(acc, masks), seen, g_seen

        zero = jnp.zeros((SB_ROWS, 1), F32)
        carry = lax.fori_loop(0, i // 2, lambda t, c: tiles([2 * t, 2 * t + 1], c, False),
                              (jnp.zeros((BLK, SB_WIDTH), F32), zero, zero))
        carry = lax.cond((i & 1) == 1, lambda c: tiles([i - 1], c, False), lambda c: c, carry)
        dq, _, _ = tiles([i], carry, True)
        dq_ref[...] = dq * QK_SCALE

    blk = pl.BlockSpec((BLK, SB_WIDTH), lambda i: (i, 0))
    full = pl.BlockSpec((SEQ, SB_WIDTH), lambda i: (0, 0))
    shape = jax.ShapeDtypeStruct((SEQ, SB_WIDTH), F32)
    return pl.pallas_call(
        body, grid=(N_BLK,),
        in_specs=[blk, pl.BlockSpec((SEQ, SB_WIDTH), lambda i: (0, 1)), pl.BlockSpec((SEQ, SB_WIDTH), lambda i: (0, 2)),
                  pl.BlockSpec((None, SB_ROWS, 1), lambda i: (i, 0, 0)), blk],
        out_specs=[blk, full, full], out_shape=[shape, shape, shape],
        name="sb_backward", compiler_params=_params(("arbitrary",)),
    )(qkv, qkv, qkv, total, do_cat)


def _band_scores(q_ref, kp_ref, ko_ref, bias_ref, hb, prev_mask):
    b = pl.program_id(1)
    qs = q_ref[...]
    s_prev = _dot(qs, kp_ref[...], BATCH_QK) + bias_ref[:, :, 0:BLK]
    s_prev = jnp.concatenate(
        [jnp.where((b & prev_mask(pl.program_id(0) * hb + t)) != 0, s_prev[t:t + 1], NEG) for t in range(hb)], axis=0)
    s_own = _dot(qs, ko_ref[...], BATCH_QK) + bias_ref[:, :, BLK:2 * BLK]
    return qs, s_prev, s_own


def _band_specs(hb, rows, t_n):
    def q_spec(width):
        return pl.BlockSpec((hb, None, rows, width), lambda h, b: (h, b, 0, 0))

    own = pl.BlockSpec((hb, BLK, HEAD_DIM), lambda h, b: (h, b, 0))
    prev = pl.BlockSpec((hb, BLK, HEAD_DIM), lambda h, b: (h, jnp.maximum(b - 1, 0), 0))
    per_head = lambda r, width: pl.BlockSpec((hb, r, width), lambda h, b: (h, 0, 0))
    return q_spec, own, prev, per_head


def banded_forward(name, q, k, v, bias, sinks, hb, prev_mask):
    h_n, nb, rows, _ = q.shape

    def body(q_ref, kp_ref, ko_ref, vp_ref, vo_ref, bias_ref, sink_ref, o_ref, lse_ref):
        _, s_prev, s_own = _band_scores(q_ref, kp_ref, ko_ref, bias_ref, hb, prev_mask)
        sink = sink_ref[...]
        m = jnp.maximum(jnp.maximum(jnp.max(s_prev, axis=2, keepdims=True), jnp.max(s_own, axis=2, keepdims=True)), sink)
        p_prev = jnp.exp(s_prev - m)
        p_own = jnp.exp(s_own - m)
        denom = jnp.sum(p_prev, axis=2, keepdims=True) + jnp.sum(p_own, axis=2, keepdims=True) + jnp.exp(sink - m)
        o = _dot(p_prev.astype(BF16), vp_ref[...], BATCH_PV) + _dot(p_own.astype(BF16), vo_ref[...], BATCH_PV)
        o_ref[...] = o / denom
        lse_ref[...] = m + jnp.log(denom)

    q_spec, own, prev, per_head = _band_specs(hb, rows, k.shape[1])
    return pl.pallas_call(
        body, grid=(h_n // hb, nb),
        in_specs=[q_spec(HEAD_DIM), prev, own, prev, own, per_head(rows, 2 * BLK), per_head(rows, 1)],
        out_specs=[q_spec(HEAD_DIM), q_spec(1)],
        out_shape=[jax.ShapeDtypeStruct(q.shape, F32), jax.ShapeDtypeStruct((h_n, nb, rows, 1), F32)],
        name=name, compiler_params=_params(("parallel", "parallel")),
    )(q, k, k, v, v, bias, sinks)


def banded_backward(name, q, k, v, bias, sinks, o, lse, do, dlse, hb, prev_mask):
    h_n, nb, rows, _ = q.shape
    t_n = k.shape[1]

    def body(q_ref, kp_ref, ko_ref, vp_ref, vo_ref, bias_ref, sink_ref, o_ref, lse_ref, do_ref, dlse_ref,
             dq_ref, dk_ref, dv_ref, dbias_ref, dsink_ref):
        b = pl.program_id(1)

        @pl.when(b == 0)
        def _():
            dk_ref[...] = jnp.zeros_like(dk_ref)
            dv_ref[...] = jnp.zeros_like(dv_ref)
            dbias_ref[...] = jnp.zeros_like(dbias_ref)
            dsink_ref[...] = jnp.zeros_like(dsink_ref)

        qs, s_prev, s_own = _band_scores(q_ref, kp_ref, ko_ref, bias_ref, hb, prev_mask)
        lse_v = lse_ref[...]
        dov = do_ref[...]
        dob = dov.astype(BF16)
        shift = dlse_ref[...] - jnp.sum(dov * o_ref[...], axis=2, keepdims=True)
        p_prev = jnp.exp(s_prev - lse_v)
        p_own = jnp.exp(s_own - lse_v)
        ds_prev = p_prev * (_dot(dob, vp_ref[...], BATCH_QK) + shift)
        ds_own = p_own * (_dot(dob, vo_ref[...], BATCH_QK) + shift)
        dbias_ref[:, :, 0:BLK] += ds_prev
        dbias_ref[:, :, BLK:2 * BLK] += ds_own
        d_sink = jnp.exp(sink_ref[...] - lse_v) * shift
        for g in range(rows // BLK):
            dsink_ref[:, g:g + 1, :] += jnp.sum(d_sink[:, g * BLK:(g + 1) * BLK, :], axis=1, keepdims=True)
        ds_prev = ds_prev.astype(BF16)
        ds_own = ds_own.astype(BF16)
        dq_ref[...] = (_dot(ds_prev, kp_ref[...], BATCH_PV) + _dot(ds_own, ko_ref[...], BATCH_PV)) * QK_SCALE
        rows_prev = pl.ds(pl.multiple_of(jnp.maximum(b - 1, 0) * BLK, BLK), BLK)
        rows_own = pl.ds(pl.multiple_of(b * BLK, BLK), BLK)
        dk_ref[:, rows_prev, :] += _dot(ds_prev, qs, BATCH_TN)
        dk_ref[:, rows_own, :] += _dot(ds_own, qs, BATCH_TN)
        dv_ref[:, rows_prev, :] += _dot(p_prev.astype(BF16), dob, BATCH_TN)
        dv_ref[:, rows_own, :] += _dot(p_own.astype(BF16), dob, BATCH_TN)

    q_spec, own, prev, per_head = _band_specs(hb, rows, t_n)
    kv_full = per_head(t_n, HEAD_DIM)
    kv_shape = jax.ShapeDtypeStruct((h_n, t_n, HEAD_DIM), F32)
    return pl.pallas_call(
        body, grid=(h_n // hb, nb),
        in_specs=[q_spec(HEAD_DIM), prev, own, prev, own, per_head(rows, 2 * BLK), per_head(rows, 1),
                  q_spec(HEAD_DIM), q_spec(1), q_spec(HEAD_DIM), q_spec(1)],
        out_specs=[q_spec(HEAD_DIM), kv_full, kv_full, per_head(rows, 2 * BLK), per_head(rows // BLK, BLK)],
        out_shape=[jax.ShapeDtypeStruct(q.shape, F32), kv_shape, kv_shape,
                   jax.ShapeDtypeStruct((h_n, rows, 2 * BLK), F32), jax.ShapeDtypeStruct((h_n, rows // BLK, BLK), F32)],
        name=name, compiler_params=_params(("parallel", "arbitrary")),
    )(q, k, k, v, v, bias, sinks, o, lse, do, dlse)


def _dil_prev_mask(head):
    group = head // H_PER_DIL
    return jnp.where(group == 0, 15, jnp.where(group == 1, 3, 0))


def _swa_prev_mask(head):
    del head
    return 15


DIL_HEADS_PER_STEP = 3
SWA_GROUP = H_SWA_Q // H_SWA_KV
N_BLK = SEQ // BLK


def dilated_merge(o, lse):
    def body(o_ref, l_ref, out_ref):
        lv = l_ref[...]
        m = jnp.max(lv, axis=0, keepdims=True)
        e = jnp.exp(lv - m)
        alpha = e / jnp.sum(e, axis=0, keepdims=True)
        out_ref[...] = jnp.sum(alpha * o_ref[...], axis=0)

    return pl.pallas_call(
        body, grid=(H_PER_DIL, SEQ // ROW_TILE),
        in_specs=[pl.BlockSpec((3, None, ROW_TILE, HEAD_DIM), lambda h, i: (0, h, i, 0)),
                  pl.BlockSpec((3, None, ROW_TILE, 1), lambda h, i: (0, h, i, 0))],
        out_specs=pl.BlockSpec((None, ROW_TILE, HEAD_DIM), lambda h, i: (h, i, 0)),
        out_shape=jax.ShapeDtypeStruct((H_PER_DIL, SEQ, HEAD_DIM), F32),
        name="dilated_merge", compiler_params=_params(("parallel", "parallel")),
    )(o, lse)


def dilated_merge_bwd(o, lse, dout):
    def body(o_ref, l_ref, d_ref, do_ref, dl_ref):
        lv = l_ref[...]
        m = jnp.max(lv, axis=0, keepdims=True)
        e = jnp.exp(lv - m)
        alpha = e / jnp.sum(e, axis=0, keepdims=True)
        dv = d_ref[...][None]
        do_ref[...] = alpha * dv
        dalpha = jnp.sum(dv * o_ref[...], axis=-1, keepdims=True)
        dl_ref[...] = alpha * (dalpha - jnp.sum(alpha * dalpha, axis=0, keepdims=True))

    o_spec = pl.BlockSpec((3, None, ROW_TILE, HEAD_DIM), lambda h, i: (0, h, i, 0))
    l_spec = pl.BlockSpec((3, None, ROW_TILE, 1), lambda h, i: (0, h, i, 0))
    return pl.pallas_call(
        body, grid=(H_PER_DIL, SEQ // ROW_TILE),
        in_specs=[o_spec, l_spec, pl.BlockSpec((None, ROW_TILE, HEAD_DIM), lambda h, i: (h, i, 0))],
        out_specs=[o_spec, l_spec],
        out_shape=[jax.ShapeDtypeStruct(o.shape, F32), jax.ShapeDtypeStruct(lse.shape, F32)],
        name="dilated_merge_bwd", compiler_params=_params(("parallel", "parallel")),
    )(o, lse, dout)


def rel_bias_reduce(dbias0, dbias1, bucket):
    def body(d0_ref, d1_ref, b_ref, o_ref):
        dv, bv = d0_ref[...] + d1_ref[...], b_ref[...]
        lane = lax.broadcasted_iota(jnp.int32, (1, BLK), 1)
        acc = jnp.zeros((1, BLK), F32)
        for bkt in range(N_BUCKETS):
            acc = acc + jnp.where(lane == bkt, jnp.sum(jnp.where(bv == bkt, dv, 0.0)), 0.0)
        o_ref[...] = acc

    tile = pl.BlockSpec((None, BLK, 2 * BLK), lambda h: (h, 0, 0))
    return pl.pallas_call(
        body, grid=(dbias0.shape[0],), in_specs=[tile, tile, tile],
        out_specs=pl.BlockSpec((None, 1, BLK), lambda h: (h, 0, 0)),
        out_shape=jax.ShapeDtypeStruct((dbias0.shape[0], 1, BLK), F32),
        name="rel_bias_reduce", compiler_params=_params(("parallel",)),
    )(dbias0, dbias1, bucket)


def _heads(t):
    return t.reshape(SEQ, -1, HEAD_DIM).transpose(1, 0, 2)


def _unheads(t):
    return t.transpose(1, 0, 2).reshape(SEQ, -1)


def _dilate(t):
    parts = []
    for g, (_, d) in enumerate(DIL_PATTERNS):
        tg = t[:, 128 * g:128 * (g + 1)].reshape(SEQ // d, d, H_PER_DIL, HEAD_DIM).transpose(2, 1, 0, 3)
        parts.append(tg.reshape(H_PER_DIL, SEQ, HEAD_DIM))
    return jnp.concatenate(parts, axis=0)


def _undilate(t):
    outs = []
    for g, (_, d) in enumerate(DIL_PATTERNS):
        tg = t[2 * g:2 * g + 2].reshape(H_PER_DIL, d, SEQ // d, -1).transpose(0, 2, 1, 3)
        outs.append(tg.reshape(H_PER_DIL, SEQ, -1))
    return jnp.stack(outs)


def _redilate(t):
    parts = []
    for g, (_, d) in enumerate(DIL_PATTERNS):
        tg = t[g].reshape(H_PER_DIL, SEQ // d, d, -1).transpose(0, 2, 1, 3)
        parts.append(tg.reshape(H_PER_DIL, SEQ, -1))
    return jnp.concatenate(parts, axis=0)


def _t5_bucket(n):
    max_exact = N_BUCKETS // 2
    nf = jnp.maximum(n, 1).astype(F32)
    large = max_exact + (jnp.log(nf / max_exact) / math.log(MAX_REL_DIST / max_exact)
                         * (N_BUCKETS - max_exact)).astype(jnp.int32)
    large = jnp.minimum(large, N_BUCKETS - 1)
    return jnp.where(n < max_exact, n, large)


def band_tables(rel_bias):
    rel = jnp.arange(BLK)[:, None] + BLK - jnp.arange(2 * BLK)[None, :]
    buckets = []
    patterns = [(d, w // d) for w, d in DIL_PATTERNS for _ in range(H_PER_DIL)] + [(1, SWA_WINDOW - 1)] * H_SWA_Q
    for d, max_dist in patterns:
        band = (rel >= 0) & (rel <= max_dist)
        buckets.append(jnp.where(band, _t5_bucket(jnp.maximum(rel, 0) * d), -1))
    buckets = jnp.stack(buckets).astype(jnp.int32)

    def body(table_ref, b_ref, o_ref):
        h = pl.program_id(0)
        bv = b_ref[...]
        tile = jnp.full(bv.shape, NEG, F32)
        for bkt in range(N_BUCKETS):
            tile = jnp.where(bv == bkt, table_ref[h, bkt], tile)
        o_ref[...] = tile

    spec = pl.BlockSpec((None, BLK, 2 * BLK), lambda h: (h, 0, 0))
    tiles = pl.pallas_call(
        body, grid=(len(patterns),), in_specs=[pl.BlockSpec(memory_space=pltpu.SMEM), spec], out_specs=spec,
        out_shape=jax.ShapeDtypeStruct(buckets.shape, F32), name="band_tables", compiler_params=_params(("parallel",)),
    )(rel_bias.T, buckets)
    return tiles[:H_DIL], tiles[H_DIL:], buckets


def _swa_rows(t):
    t = t.reshape(N_BLK, BLK, H_SWA_KV, SWA_GROUP, HEAD_DIM).transpose(2, 0, 3, 1, 4)
    return t.reshape(H_SWA_KV, N_BLK, SWA_GROUP * BLK, HEAD_DIM)


def _swa_tokens(t):
    t = t.reshape(H_SWA_KV, N_BLK, SWA_GROUP, BLK, HEAD_DIM).transpose(1, 3, 0, 2, 4)
    return t.reshape(SEQ, H_SWA_Q * HEAD_DIM)


def _sink_rows(sinks):
    return jnp.broadcast_to(sinks.reshape(H_SWA_KV, SWA_GROUP, 1, 1), (H_SWA_KV, SWA_GROUP, BLK, 1)).reshape(
        H_SWA_KV, SWA_GROUP * BLK, 1)


def _no_sinks():
    return jnp.full((H_DIL, BLK, 1), NEG, F32)


def _vec(v):
    return v.reshape(1, D_MODEL)


class UnitRows:
    def __init__(self, u, mod_table, gain_table):
        self.shift, self.scale, self.gate = (Row(mod_table, 3 * u + t) for t in range(3))
        self.gain = Row(gain_table, u)


def ffn_forward(x, rows, w):
    h = prenorm(x, rows.gain, rows.scale, rows.shift)
    a, b, s = ffn_up(h, w[0], w[1])
    f, xo = ffn_down(s, w[2], x, rows.gate)
    return xo, (x, h, a, b, s, f)


def ffn_backward(u, dxo, saved, rows, w, sums):
    x, h, a, b, s, f = saved
    df, gate_sums = resid_bwd(dxo, f, rows.gate, 0.5, sums[1], u)
    da, db = ffn_bwd_hidden(df, w[2], a, b)
    grads = ffn_grad_weights(h, s, df, da, db)
    dx, norm_sums = ffn_bwd_input(da, db, w[0], w[1], x, dxo, rows.gain, rows.scale, sums[0], u)
    return dx, (norm_sums, gate_sums), grads


def mixer_forward(x, rows, sinks, bias_dil, bias_swa, w):
    h = prenorm(x, rows.gain, rows.scale, rows.shift)
    proj, qkv = in_proj(h, w[0])
    q_dil, k_dil, v_dil = _dilate(qkv[:, 768:1152] * QK_SCALE), _dilate(qkv[:, 1152:1536]), _dilate(qkv[:, 1536:1920])
    q_swa, k_swa, v_swa = _swa_rows(qkv[:, 1920:2304] * QK_SCALE), _heads(qkv[:, 2304:2432]), _heads(qkv[:, 2432:2560])
    o_sb, total_sb = sb_forward(qkv)
    q_dil = q_dil.reshape(H_DIL, N_BLK, BLK, HEAD_DIM)
    o_dd, lse_dd = banded_forward("dilated_forward", q_dil, k_dil, v_dil, bias_dil, _no_sinks(), DIL_HEADS_PER_STEP,
                                  _dil_prev_mask)
    o_dt, lse_dt = _undilate(o_dd.reshape(H_DIL, SEQ, HEAD_DIM)), _undilate(lse_dd.reshape(H_DIL, SEQ, 1))
    o_dil = dilated_merge(o_dt, lse_dt)
    bias_swa = bias_swa.reshape(H_SWA_KV, SWA_GROUP * BLK, 2 * BLK)
    o_swa, lse_swa = banded_forward("swa_forward", q_swa, k_swa, v_swa, bias_swa, _sink_rows(sinks), 1, _swa_prev_mask)
    o_cat = jnp.concatenate([o_sb, _unheads(o_dil), _swa_tokens(o_swa)], axis=1)
    merged = merge_branches(o_cat, w[1], proj)
    mo, xo = out_proj(merged, w[2], x, rows.gate)
    saved = (x, h, proj, (qkv, total_sb), (q_dil, k_dil, v_dil, o_dd, lse_dd, o_dt, lse_dt),
             (q_swa, k_swa, v_swa, o_swa, lse_swa), o_cat, merged, mo)
    return xo, saved


def mixer_backward(u, dxo, saved, rows, sinks, bias_dil, bias_swa, w, sums):
    x, h, proj, sb, dil, swa, o_cat, merged, mo = saved
    dmo, gate_sums = resid_bwd(dxo, mo, rows.gate, 1.0, sums[1], u)
    tok = pl.BlockSpec((MM_TILE, D_MODEL), lambda j, k: (k, 0))
    g_out = grad_weight("grad_w_out", merged, pl.BlockSpec((MM_TILE, D_SHARD), lambda j, k: (k, j)), dmo, tok,
                        (D_SHARD, D_MODEL))
    du0, du1, du2, dg0, dg1, dg2 = merge_bwd(dmo, w[2], o_cat, w[1], proj)
    du = (du0, du1, du2)
    do_cat = branch_bwd_input(du, w[1])
    g_br = branch_grad_weights(o_cat, du)

    qkv, total_sb = sb
    dq_sb, dk_sb, dv_sb = sb_backward(qkv, total_sb, do_cat)

    q_dil, k_dil, v_dil, o_dd, lse_dd, o_dt, lse_dt = dil
    do_dt, dlse_dt = dilated_merge_bwd(o_dt, lse_dt, _heads(do_cat[:, 256:384]))
    dq_dil, dk_dil, dv_dil, dbias_dil, _ = banded_backward(
        "dilated_backward", q_dil, k_dil, v_dil, bias_dil, _no_sinks(), o_dd, lse_dd,
        _redilate(do_dt).reshape(q_dil.shape), _redilate(dlse_dt).reshape(lse_dd.shape), DIL_HEADS_PER_STEP, _dil_prev_mask)

    q_swa, k_swa, v_swa, o_swa, lse_swa = swa
    bias_swa = bias_swa.reshape(H_SWA_KV, SWA_GROUP * BLK, 2 * BLK)
    dq_swa, dk_swa, dv_swa, dbias_swa, dsinks = banded_backward(
        "swa_backward", q_swa, k_swa, v_swa, bias_swa, _sink_rows(sinks), o_swa, lse_swa, _swa_rows(do_cat[:, 384:768]),
        jnp.zeros_like(lse_swa), 1, _swa_prev_mask)
    dbias_swa = dbias_swa.reshape(H_SWA_Q, BLK, 2 * BLK)

    def tokens(t):
        return _undilate(t).transpose(2, 0, 1, 3).reshape(SEQ, -1)

    dproj = jnp.concatenate(
        [dq_sb, dk_sb, dv_sb, tokens(dq_dil.reshape(H_DIL, SEQ, HEAD_DIM)), tokens(dk_dil),
         tokens(dv_dil), _swa_tokens(dq_swa), _unheads(dk_swa), _unheads(dv_swa)], axis=1).astype(BF16)
    dproj = jnp.concatenate([dproj, dg0, dg1, dg2], axis=1)
    g_in = grad_weight("grad_w_in", h, tok, dproj, pl.BlockSpec((MM_TILE, IN_SHARD), lambda j, k: (k, j)),
                       (D_MODEL, IN_SHARD))
    dx, norm_sums = mixer_bwd_input(dproj, w[0], x, dxo, rows.gain, rows.scale, sums[0], u)
    dbias = jnp.concatenate([dbias_dil, dbias_swa], axis=0)
    return dx, (norm_sums, gate_sums), dbias, dsinks[:, :, 0].reshape(H_SWA_Q), (g_in, g_br, g_out)


N_UNITS = 3 * DEPTH


def device_step(x, target, mod, gains, final_gain, sinks, rel_bias, get_weights, put_grads):
    bias_dil, bias_swa, bucket = band_tables(rel_bias)
    mod_table = mod.reshape(3 * N_UNITS, 1, D_MODEL)
    gain_table = gains.reshape(N_UNITS, 1, D_MODEL)
    saved, weights = [], []
    for u in range(N_UNITS):
        l, j = divmod(u, 3)
        w = get_weights(u, x)
        rows = UnitRows(u, mod_table, gain_table)
        if j == 1:
            x, s = mixer_forward(x, rows, sinks[l], bias_dil, bias_swa, w)
        else:
            x, s = ffn_forward(x, rows, w)
        saved.append(s)
        weights.append(w)
    loss, dx, dfinal = final_loss(x, _vec(final_gain), target)

    sums = (lax.empty((8 * N_UNITS, D_MODEL), F32), lax.empty((8 * N_UNITS, D_MODEL), F32))
    dbias, dsinks = [None] * DEPTH, [None] * DEPTH
    zero = jnp.zeros((1, 1), F32)
    for u in reversed(range(N_UNITS)):
        l, j = divmod(u, 3)
        rows = UnitRows(u, mod_table, gain_table + zero)
        if j == 1:
            dx, sums, dbias[l], dsinks[l], grads = mixer_backward(
                u, dx, saved[u], rows, sinks[l], bias_dil, bias_swa, weights[u], sums)
        else:
            dx, sums, grads = ffn_backward(u, dx, saved[u], rows, weights[u], sums)
        if u > 0:
            zero = put_grads(u, grads)
    drel = rel_bias_reduce(dbias[0], dbias[1], bucket)[:, 0, :N_BUCKETS].T
    norm_sums, gate_sums = (t.reshape(DEPTH, 3, 8, D_MODEL) for t in sums)
    dmod = jnp.stack([norm_sums[:, :, 0], norm_sums[:, :, 1], gate_sums[:, :, 0]], axis=2)
    return loss, dx, dmod, norm_sums[:, :, 2], dfinal[0], jnp.stack(dsinks), drel, grads


MESH = pl.DeviceIdType.MESH
CHIP_FLIPS = ((1, 0), (0, 1), (1, 1))
ANY = pl.BlockSpec(memory_space=pl.ANY)


def _position():
    return lax.axis_index("x"), lax.axis_index("y"), lax.axis_index("c")


def all_gather_small(name, piece):
    def body(x_ref, out_ref, send_sems, recv_sems, local_sem):
        x, y, c = _position()
        me, sibling = (x, y, c), (x, y, 1 - c)
        chips = [(x ^ fx, y ^ fy) for fx, fy in CHIP_FLIPS]

        def rows(px, py, pc):
            return out_ref.at[4 * px + 2 * py + pc]

        def copy(k, block, to, src=None):
            return pltpu.make_async_remote_copy(
                src_ref=rows(*block) if src is None else src, dst_ref=rows(*block),
                send_sem=send_sems.at[k], recv_sem=recv_sems.at[k], device_id=to, device_id_type=MESH)

        mine = pltpu.make_async_copy(x_ref, rows(*me), local_sem)
        mine.start()
        first = [copy(0, me, sibling, src=x_ref)]
        first += [copy(1 + j, me, (*chip, c), src=x_ref) for j, chip in enumerate(chips)]
        for cp in first:
            cp.start()
        passed = [copy(4 + j, (*chip, c), sibling) for j, chip in enumerate(chips)]
        for j, chip in enumerate(chips):
            copy(1 + j, (*chip, c), me).wait_recv()
            passed[j].start()
        copy(0, sibling, me).wait_recv()
        for j, chip in enumerate(chips):
            copy(4 + j, (*chip, 1 - c), me).wait_recv()
        for cp in first + passed:
            cp.wait_send()
        mine.wait()

    return pl.pallas_call(
        body, out_shape=jax.ShapeDtypeStruct((N_DEV,) + piece.shape, piece.dtype),
        in_specs=[pl.BlockSpec(memory_space=pltpu.VMEM)], out_specs=pl.BlockSpec(memory_space=pltpu.VMEM),
        scratch_shapes=[pltpu.SemaphoreType.DMA((7,)), pltpu.SemaphoreType.DMA((7,)), pltpu.SemaphoreType.DMA],
        name=name,
    )(piece)


def exchange(name, operands, out_shapes, aliases, plan):
    n_in, n_out = len(operands), len(out_shapes)

    def body(*refs):
        ins, outs = refs[:n_in], refs[n_in:n_in + n_out]
        send_sems, recv_sems, local_sems = refs[n_in + n_out:]
        x, y, c = _position()
        local, sends, recvs = plan(ins, outs, x, y, c)
        local = [pltpu.make_async_copy(s, d, local_sems.at[k]) for k, (s, d) in enumerate(local)]
        for cp in local:
            cp.start()
        remote = [pltpu.make_async_remote_copy(src_ref=s, dst_ref=d, send_sem=send_sems.at[k], recv_sem=recv_sems.at[k],
                                               device_id=dev, device_id_type=MESH)
                  for k, (s, d, dev) in enumerate(sends)]
        for cp in remote:
            cp.start()
        for k, r in enumerate(recvs):
            pltpu.make_async_remote_copy(src_ref=r, dst_ref=r, send_sem=send_sems.at[k], recv_sem=recv_sems.at[k],
                                         device_id=(x, y, c), device_id_type=MESH).wait_recv()
        for cp in remote:
            cp.wait_send()
        for cp in local:
            cp.wait()

    n_sends, n_local = plan.n_sends, max(plan.n_local, 1)
    return pl.pallas_call(
        body, out_shape=out_shapes, in_specs=[ANY] * n_in, out_specs=[ANY] * n_out,
        scratch_shapes=[pltpu.SemaphoreType.DMA((n_sends,)), pltpu.SemaphoreType.DMA((n_sends,)),
                        pltpu.SemaphoreType.DMA((n_local,))],
        input_output_aliases=aliases, name=name,
    )(*operands)


def _plan(n_local, n_sends):
    def wrap(fn):
        fn.n_local, fn.n_sends = n_local, n_sends
        return fn
    return wrap


def _half(ref, axis, c):
    rows = ref.shape[axis] // 2
    idx = [slice(None)] * len(ref.shape)
    idx[axis] = pl.ds(pl.multiple_of(c * rows, 16), rows)
    return ref.at[tuple(idx)]


HBM = pl.BlockSpec(memory_space=pltpu.HBM)
SEM = pl.BlockSpec(memory_space=pltpu.SEMAPHORE)
EFFECT = pltpu.SideEffectType.DATAFLOW_SIDE_EFFECTING


def split_start(name, bufs, extra, n_copies, describe):
    n = len(bufs)

    def body(*refs):
        send_sems, recv_sems = refs[n + len(extra)], refs[n + len(extra) + 1]
        x, y, c = _position()
        for k, (src, dst, _, peer) in enumerate(describe(refs[:n], x, y, c)):
            pltpu.make_async_remote_copy(src_ref=src, dst_ref=dst, send_sem=send_sems.at[k], recv_sem=recv_sems.at[k],
                                         device_id=peer, device_id_type=MESH).start()
        token = refs[-1]
        token[...] = jnp.zeros_like(token)

    out = pl.pallas_call(
        body, name=name,
        out_shape=(pltpu.SemaphoreType.DMA((n_copies,)), pltpu.SemaphoreType.DMA((n_copies,)),
                   *[pltpu.HBM(b.shape, b.dtype) for b in bufs], jax.ShapeDtypeStruct((8, 128), F32)),
        in_specs=[HBM] * n + [ANY] * len(extra),
        out_specs=(SEM, SEM, *[HBM] * n, pl.BlockSpec(memory_space=pltpu.VMEM)),
        input_output_aliases={k: 2 + k for k in range(n)},
        compiler_params=pltpu.CompilerParams(has_side_effects=EFFECT),
    )(*[pltpu.with_memory_space_constraint(b, pltpu.HBM) for b in bufs], *extra)
    return out[0], out[1], list(out[2:2 + n]), out[-1]


def split_wait(name, bufs, send_sems, recv_sems, after, describe):
    n = len(bufs)

    def body(*refs):
        send, recv = refs[n], refs[n + 1]
        x, y, c = _position()
        for k, (src, _, dst, peer) in enumerate(describe(refs[:n], x, y, c)):
            copy = pltpu.make_async_remote_copy(src_ref=src, dst_ref=dst, send_sem=send.at[k], recv_sem=recv.at[k],
                                                device_id=peer, device_id_type=MESH)
            copy.wait_send()
            copy.wait_recv()

    out = pl.pallas_call(
        body, name=name, out_shape=[pltpu.HBM(b.shape, b.dtype) for b in bufs],
        in_specs=[HBM] * n + [SEM, SEM] + [ANY] * len(after), out_specs=[HBM] * n,
        input_output_aliases={k: k for k in range(n)},
        compiler_params=pltpu.CompilerParams(has_side_effects=EFFECT),
    )(*bufs, send_sems, recv_sems, *after)
    return list(out)


def _row_tile(rows, cols, max_elements=256 * 1024):
    best = 16
    for t in range(16, rows + 1, 16):
        if rows % t == 0 and t * cols <= max_elements:
            best = t
    return best


def cast_into_slot(name, param, index, chip):
    rows, cols = param.shape[-2:]
    tr = _row_tile(rows, cols)
    lead = (None,) * len(index)

    def body(chip_ref, s_ref, o_ref):
        del chip_ref
        o_ref[...] = s_ref[...].astype(BF16)

    return pl.pallas_call(
        body, out_shape=jax.ShapeDtypeStruct((N_CHIPS, rows, cols), BF16),
        grid_spec=pltpu.PrefetchScalarGridSpec(
            num_scalar_prefetch=1, grid=(rows // tr,),
            in_specs=[pl.BlockSpec(lead + (tr, cols), lambda r, chip_ref: index + (r, 0))],
            out_specs=pl.BlockSpec((None, tr, cols), lambda r, chip_ref: (chip_ref[0], r, 0))),
        name=name, compiler_params=_params(("parallel",)),
    )(chip, param)


GATHER_STAGES = ((0,), (1,), (2,), (3, 4, 5))
REDUCE_STAGES = ((5, 4, 3), (2,), (1,), (0,))


def _gather_copies(slots, x, y, c):
    me = 2 * x + y
    out = []
    for s in slots:
        for fx, fy in CHIP_FLIPS:
            mine = _half(s.at[me], 0, c)
            out.append((mine, mine, _half(s.at[2 * (x ^ fx) + (y ^ fy)], 0, c), (x ^ fx, y ^ fy, c)))
    return out


class WeightStream:
    def __init__(self, shards, chip, after=()):
        self.pending, self.ready = {}, {}
        token = tuple(after)
        for si, units in enumerate(GATHER_STAGES):
            slots = [cast_into_slot(f"cast_{u}_{t}", p, idx, chip) for u in units for t, (p, idx) in enumerate(shards[u])]
            send, recv, slots, tok = split_start(f"gather_start_{si}", slots, token, 3 * len(slots), _gather_copies)
            self.pending[si] = (send, recv, slots)
            token = (tok,)
        self.token = token

    def get(self, u, after):
        if u not in self.ready:
            si = next(k for k, units in enumerate(GATHER_STAGES) if u in units)
            send, recv, slots = self.pending.pop(si)
            slots = split_wait(f"gather_wait_{si}", slots, send, recv, (after,) + self.token, _gather_copies)
            self.token = ()

            @_plan(0, 3 * len(slots))
            def to_sibling(ins, outs, x, y, c):
                sends, recvs = [], []
                for o in outs:
                    for fx, fy in CHIP_FLIPS:
                        slab = o.at[2 * (x ^ fx) + (y ^ fy)]
                        sends.append((_half(slab, 0, c), _half(slab, 0, c), (x, y, 1 - c)))
                        recvs.append(_half(slab, 0, 1 - c))
                return [], sends, recvs

            shapes = [jax.ShapeDtypeStruct(s.shape, BF16) for s in slots]
            slots = exchange(f"gather_sibling_{si}", slots, shapes, {k: k for k in range(len(slots))}, to_sibling)
            for i, v in enumerate(GATHER_STAGES[si]):
                self.ready[v] = tuple(slots[3 * i:3 * i + 3])
        return self.ready[u]


def _reduce_copies(bufs, x, y, c):
    n = len(bufs) // 2
    out = []
    for s, land in zip(bufs[:n], bufs[n:]):
        for k, (fx, fy) in enumerate(CHIP_FLIPS):
            out.append((s.at[2 * (x ^ fx) + (y ^ fy)], land.at[k], land.at[k], (x ^ fx, y ^ fy, c)))
    return out


GRAD_SLOTS = {"gate": (2 * DEPTH, FF_SHARD, D_MODEL), "up": (2 * DEPTH, FF_SHARD, D_MODEL),
              "down": (2 * DEPTH, FF_SHARD, D_MODEL), "in": (DEPTH, D_MODEL, IN_SHARD),
              "br": (DEPTH, BR_ROWS, D_SHARD), "out": (DEPTH, D_SHARD, D_MODEL)}


def _unit_tensors(u):
    l, j = divmod(u, 3)
    if j == 1:
        return [("in", l), ("br", l), ("out", l)]
    return [(k, 2 * l + j // 2) for k in ("gate", "up", "down")]


class GradStream:
    def __init__(self, chip, core):
        self.core = core
        self.place = jnp.concatenate([chip, core])
        self.held, self.flying = {}, []
        self.full = {k: lax.empty(shape, F32) for k, shape in GRAD_SLOTS.items()}

    def put(self, u, grads, after=()):
        self.held[u] = grads
        si = len(self.flying)
        units = REDUCE_STAGES[si]
        if not all(v in self.held for v in units):
            return jnp.zeros((1, 1), F32)
        gs = [g for v in units for g in self.held[v]]

        @_plan(0, len(gs))
        def swap_halves(ins, outs, x, y, c):
            sends = [(_half(g, 1, 1 - c), o, (x, y, 1 - c)) for g, o in zip(ins, outs)]
            return [], sends, list(outs)

        half_shapes = [jax.ShapeDtypeStruct((N_CHIPS, g.shape[1] // 2, g.shape[2]), BF16) for g in gs]
        landed = exchange(f"reduce_swap_{si}", gs + list(after), half_shapes, {}, swap_halves)
        sums = [_add_halves(g, la, self.core) for g, la in zip(gs, landed)]
        landing = [lax.empty((3,) + s.shape[1:], BF16) for s in sums]
        send, recv, bufs, token = split_start(f"reduce_start_{si}", sums + landing, (), 3 * len(sums), _reduce_copies)
        self.flying.append((send, recv, bufs, [t for v in units for t in _unit_tensors(v)]))
        return token[0:1, 0:1]

    def finish(self, after):
        for si, (send, recv, bufs, tensors) in enumerate(self.flying):
            bufs = split_wait(f"reduce_wait_{si}", bufs, send, recv, tuple(after), _reduce_copies)
            n = len(tensors)
            for (name, slot), s, land in zip(tensors, bufs[:n], bufs[n:]):
                self.full[name] = _add_chips(s, land, self.place, self.full[name], slot)
        names = list(self.full)

        @_plan(0, len(names))
        def share_halves(ins, outs, x, y, c):
            sends = [(_half(o, 1, c), _half(o, 1, c), (x, y, 1 - c)) for o in outs]
            return [], sends, [_half(o, 1, 1 - c) for o in outs]

        shapes = [jax.ShapeDtypeStruct(self.full[k].shape, F32) for k in names]
        out = exchange("reduce_share_halves", [self.full[k] for k in names], shapes, {k: k for k in range(len(names))},
                       share_halves)
        return dict(zip(names, out))


def _add_halves(g, landed, core):
    _, rh, cols = landed.shape
    tr = _row_tile(rh, cols, 1024 * 1024)
    per_half = rh // tr

    def body(core_ref, g_ref, la_ref, o_ref):
        del core_ref
        o_ref[...] = (g_ref[...].astype(F32) + la_ref[...].astype(F32)).astype(BF16)

    blk = (None, tr, cols)
    return pl.pallas_call(
        body, out_shape=jax.ShapeDtypeStruct(landed.shape, BF16),
        grid_spec=pltpu.PrefetchScalarGridSpec(
            num_scalar_prefetch=1, grid=(N_CHIPS, per_half),
            in_specs=[pl.BlockSpec(blk, lambda j, r, core_ref: (j, core_ref[0] * per_half + r, 0)),
                      pl.BlockSpec(blk, lambda j, r, core_ref: (j, r, 0))],
            out_specs=pl.BlockSpec(blk, lambda j, r, core_ref: (j, r, 0))),
        name="reduce_add_halves", compiler_params=_params(("parallel", "parallel")),
    )(core, g, landed)


def _add_chips(sums, landed, place, full, slot):
    _, rh, cols = sums.shape
    tr = _row_tile(rh, cols, 1024 * 1024)
    per_half = rh // tr

    def body(place_ref, s_ref, la_ref, full_in, o_ref):
        del place_ref, full_in
        o_ref[...] = ((s_ref[...].astype(F32) + la_ref[0].astype(F32)) + la_ref[1].astype(F32)) + la_ref[2].astype(F32)

    return pl.pallas_call(
        body, out_shape=jax.ShapeDtypeStruct(full.shape, F32),
        grid_spec=pltpu.PrefetchScalarGridSpec(
            num_scalar_prefetch=1, grid=(per_half,),
            in_specs=[pl.BlockSpec((None, tr, cols), lambda r, place_ref: (place_ref[0], r, 0)),
                      pl.BlockSpec((3, tr, cols), lambda r, place_ref: (0, r, 0)), ANY],
            out_specs=pl.BlockSpec((None, tr, cols), lambda r, place_ref: (slot, place_ref[1] * per_half + r, 0))),
        input_output_aliases={3: 0}, name="reduce_add_chips", compiler_params=_params(("parallel",)),
    )(place, sums, landed, full)


def sum_devices(parts):
    def body(p_ref, o_ref):
        acc = p_ref[0]
        for d in range(1, N_DEV):
            acc = acc + p_ref[d]
        o_ref[...] = acc

    return pl.pallas_call(body, out_shape=jax.ShapeDtypeStruct(parts.shape[1:], F32), name="sum_devices")(parts)


ADA_SHARD = 9 * D_MODEL // N_CHIPS
ADA_TILE = 768
ADA_ROWS = 16


def ada_forward(c_rows, w_ada, b_shard):
    def body(c_ref, w_ref, b_ref, o_ref):
        cv = c_ref[...]
        o_ref[...] = _dot((cv * _sigmoid(cv)).astype(BF16), w_ref[...].astype(BF16), NN) + b_ref[...]

    return pl.pallas_call(
        body, grid=(DEPTH, ADA_SHARD // ADA_TILE),
        in_specs=[pl.BlockSpec((ADA_ROWS, D_MODEL), lambda l, n: (0, 0)),
                  pl.BlockSpec((None, D_MODEL, ADA_TILE), lambda l, n: (l, 0, n)),
                  pl.BlockSpec((None, 1, ADA_TILE), lambda l, n: (l, 0, n))],
        out_specs=pl.BlockSpec((None, ADA_ROWS, ADA_TILE), lambda l, n: (l, 0, n)),
        out_shape=jax.ShapeDtypeStruct((DEPTH, ADA_ROWS, ADA_SHARD), F32),
        name="ada_forward", compiler_params=_params(("parallel", "parallel")),
    )(c_rows, w_ada, b_shard)


def ada_backward(c_rows, dmod_rows):
    def body(c_ref, d_ref, o_ref):
        cv = c_ref[...]
        o_ref[...] = _dot((cv * _sigmoid(cv)).astype(BF16), d_ref[...].astype(BF16), TN)

    return pl.pallas_call(
        body, grid=(DEPTH, ADA_SHARD // ADA_TILE),
        in_specs=[pl.BlockSpec((ADA_ROWS, D_MODEL), lambda l, n: (0, 0)),
                  pl.BlockSpec((None, ADA_ROWS, ADA_TILE), lambda l, n: (l, 0, n))],
        out_specs=pl.BlockSpec((None, D_MODEL, ADA_TILE), lambda l, n: (l, 0, n)),
        out_shape=jax.ShapeDtypeStruct((DEPTH, D_MODEL, ADA_SHARD), F32),
        name="ada_backward", compiler_params=_params(("parallel", "parallel")),
    )(c_rows, dmod_rows)


def adamw(name, w, g, m, v):
    shape = w.shape
    cols = shape[-1]
    rows = w.size // cols
    tr = _row_tile(rows, cols) if rows % 16 == 0 else rows
    c1 = 1.0 / (1.0 - ADAM_B1 ** ADAM_STEP)
    c2 = 1.0 / (1.0 - ADAM_B2 ** ADAM_STEP)

    def body(w_ref, g_ref, m_ref, v_ref, go_ref, d_ref, mo_ref, vo_ref):
        gv = g_ref[...]
        mn = ADAM_B1 * m_ref[...] + (1.0 - ADAM_B1) * gv
        vn = ADAM_B2 * v_ref[...] + (1.0 - ADAM_B2) * (gv * gv)
        go_ref[...] = gv
        mo_ref[...] = mn
        vo_ref[...] = vn
        d_ref[...] = -ADAM_LR * ((mn * c1) / (jnp.sqrt(vn * c2) + ADAM_EPS) + ADAM_WD * w_ref[...])

    spec = pl.BlockSpec((tr, cols), lambda i: (i, 0))
    out = jax.ShapeDtypeStruct((rows, cols), F32)
    res = pl.pallas_call(
        body, grid=(rows // tr,), in_specs=[spec] * 4, out_specs=[spec] * 4, out_shape=[out] * 4,
        name=name, compiler_params=_params(("parallel",)),
    )(*[t.reshape(rows, cols) for t in (w, g, m, v)])
    return tuple(r.reshape(shape) for r in res)


def _pack(parts, rows):
    flat = jnp.concatenate([p.reshape(-1) for p in parts])
    return jnp.pad(flat, (0, rows * 128 - flat.size)).reshape(rows, 128)


def _unpack(flat, shapes):
    out, at = [], 0
    for s in shapes:
        n = math.prod(s)
        out.append(flat[at:at + n].reshape(s))
        at += n
    return out


def kernel(x, c, w_ada, b_ada, norm_gain, w_ffn_gate, w_ffn_up, w_ffn_down, w_in, w_br_sb, w_br_dil, w_br_swa, w_out, sinks, rel_bias, final_gain, loss_target, m_w_ada, m_b_ada, m_norm_gain, m_w_ffn_gate, m_w_ffn_up, m_w_ffn_down, m_w_in, m_w_br_sb, m_w_br_dil, m_w_br_swa, m_w_out, m_sinks, m_rel_bias, m_final_gain, v_w_ada, v_b_ada, v_norm_gain, v_w_ffn_gate, v_w_ffn_up, v_w_ffn_down, v_w_in, v_w_br_sb, v_w_br_dil, v_w_br_swa, v_w_out, v_sinks, v_rel_bias, v_final_gain):
    xi, yi, ci = _position()
    chip = 2 * xi + yi
    dev = 2 * chip + ci

    c_all = all_gather_small("gather_c", c.reshape(8, 128)).reshape(N_DEV, D_MODEL)
    c_rows = jnp.pad(c_all, ((0, ADA_ROWS - N_DEV), (0, 0)))
    b_shard = lax.dynamic_slice_in_dim(b_ada, chip * ADA_SHARD, ADA_SHARD, axis=1).reshape(DEPTH, 1, ADA_SHARD)
    mod_shard = ada_forward(c_rows, w_ada, b_shard)[:, :N_DEV]
    n_mod = DEPTH * N_DEV * ADA_SHARD
    gathered = all_gather_small("gather_mod", _pack([mod_shard, norm_gain], 304))[::2].reshape(N_CHIPS, -1)
    mod_all = gathered[:, :n_mod].reshape(N_CHIPS, DEPTH, N_DEV, ADA_SHARD)
    mod = lax.dynamic_index_in_dim(mod_all, dev, axis=2, keepdims=False)
    mod = mod.transpose(1, 0, 2).reshape(DEPTH, 3, 3, D_MODEL)
    gains = gathered[:, n_mod:n_mod + DEPTH * 3 * D_SHARD].reshape(N_CHIPS, DEPTH, 3, D_SHARD)
    gains = gains.transpose(1, 2, 0, 3).reshape(DEPTH, 3, D_MODEL)

    chip_i, core_i = chip.astype(jnp.int32).reshape(1), ci.astype(jnp.int32).reshape(1)
    w_br = jnp.concatenate([w_br_sb, w_br_dil, w_br_swa], axis=1)
    transposed = (3, 4)
    w_gate_t, w_up_t = jnp.swapaxes(w_ffn_gate, 2, 3), jnp.swapaxes(w_ffn_up, 2, 3)
    shards = []
    for l in range(DEPTH):
        ffn = [[(w_gate_t, (l, f)), (w_up_t, (l, f)), (w_ffn_down, (l, f))] for f in range(2)]
        shards += [ffn[0], [(w_in, (l,)), (w_br, (l,)), (w_out, (l,))], ffn[1]]
    weights_in = WeightStream(shards, chip_i, (gathered,))
    grads_out = GradStream(chip_i, core_i)

    loss, dx, dmod, dgains, dfinal, dsinks, drel, last_grads = device_step(
        x[0], loss_target[0], mod, gains, final_gain, sinks, rel_bias, weights_in.get, grads_out.put)

    small_shapes = [(DEPTH, 9 * D_MODEL), (DEPTH, 3, D_MODEL), (D_MODEL,), (DEPTH, H_SWA_Q), (N_BUCKETS, 12), (1,)]
    small_all = all_gather_small("gather_small_grads", _pack([dmod, dgains, dfinal, dsinks, drel, loss[0, 0:1]], 208))
    started = grads_out.put(0, last_grads, after=(small_all,))
    small_all = small_all + started
    g_b_ada, g_gain_full, g_final, g_sinks, g_rel, loss_sum = _unpack(sum_devices(small_all).reshape(-1), small_shapes)
    g_gain = lax.dynamic_slice_in_dim(g_gain_full, chip * D_SHARD, D_SHARD, axis=2)
    dmod_all = small_all.reshape(N_DEV, -1)[:, :DEPTH * 9 * D_MODEL].reshape(N_DEV, DEPTH, 9 * D_MODEL)
    dmod_rows = lax.dynamic_slice_in_dim(dmod_all, chip * ADA_SHARD, ADA_SHARD, axis=2).transpose(1, 0, 2)
    g_w_ada = ada_backward(c_rows, jnp.pad(dmod_rows, ((0, 0), (0, ADA_ROWS - N_DEV), (0, 0))))

    weights = [w_ada, b_ada, norm_gain, w_ffn_gate, w_ffn_up, w_ffn_down, w_in, w_br_sb, w_br_dil, w_br_swa, w_out,
               sinks, rel_bias, final_gain]
    ms = [m_w_ada, m_b_ada, m_norm_gain, m_w_ffn_gate, m_w_ffn_up, m_w_ffn_down, m_w_in, m_w_br_sb, m_w_br_dil,
          m_w_br_swa, m_w_out, m_sinks, m_rel_bias, m_final_gain]
    vs = [v_w_ada, v_b_ada, v_norm_gain, v_w_ffn_gate, v_w_ffn_up, v_w_ffn_down, v_w_in, v_w_br_sb, v_w_br_dil,
          v_w_br_swa, v_w_out, v_sinks, v_rel_bias, v_final_gain]
    grads = [g_w_ada, g_b_ada, g_gain] + [None] * 8 + [g_sinks, g_rel, g_final]

    small = (1, 2, 11, 12, 13)
    deltas, new_ms, new_vs = [None] * 14, [None] * 14, [None] * 14
    _, deltas[0], new_ms[0], new_vs[0] = adamw("adamw_0", weights[0], grads[0], ms[0], vs[0])
    shapes = [weights[k].shape for k in small]
    packed = [_pack([t[k] for k in small], 168) for t in (weights, grads, ms, vs)]
    for dst, res in zip((deltas, new_ms, new_vs), adamw("adamw_small", *packed)[1:]):
        for k, t in zip(small, _unpack(res.reshape(-1), shapes)):
            dst[k] = t

    g = grads_out.finish((dx, deltas[0], deltas[1]))
    g_br = g["br"]
    grads[3:11] = [g["gate"].reshape(w_gate_t.shape), g["up"].reshape(w_up_t.shape),
                   g["down"].reshape(w_ffn_down.shape), g["in"], g_br[:, 0:256], g_br[:, 256:384], g_br[:, 384:768],
                   g["out"]]
    for k in range(3, 11):
        state = [weights[k], ms[k], vs[k]]
        if k in transposed:
            state = [jnp.swapaxes(t, 2, 3) for t in state]
        out = adamw(f"adamw_{k}", state[0], grads[k], state[1], state[2])
        if k in transposed:
            out = [jnp.swapaxes(t, 2, 3) for t in out]
        grads[k], deltas[k], new_ms[k], new_vs[k] = out
    return (loss_sum[0], dx[None], *grads, *deltas, *new_ms, *new_vs)
```

```python
import functools
import math

import jax
import jax.numpy as jnp
from jax import lax
from jax.experimental import pallas as pl
from jax.experimental.pallas import tpu as pltpu

F32 = jnp.float32
BF16 = jnp.bfloat16

D_MODEL = 1024
SEQ = 2048
DEPTH = 2
HEAD_DIM = 64
BLK = 128
H_SB = 4
DIL_PATTERNS = ((128, 1), (512, 4), (2048, 16))
H_PER_DIL = 2
H_DIL = 6
H_SWA_Q = 6
H_SWA_KV = 2
SWA_WINDOW = 128
N_BUCKETS = 32
MAX_REL_DIST = 2048
D_FF = 2816
RMS_EPS = 1e-6
N_CHIPS = 4
N_DEV = 8
FF_SHARD = D_FF // N_CHIPS
D_QKV = 2560
D_IN = D_QKV + 3 * D_MODEL
IN_SHARD = D_IN // N_CHIPS
D_SHARD = D_MODEL // N_CHIPS
BR_ROWS = 768
NEG = -1e30
QK_SCALE = HEAD_DIM ** -0.5

ADAM_LR = 0.001
ADAM_B1 = 0.9
ADAM_B2 = 0.999
ADAM_EPS = 1e-08
ADAM_WD = 0.01
ADAM_STEP = 10

VMEM_LIMIT = 56 * 1024 * 1024
ROW_TILE = 256
MM_TILE = 1024

NN = (((1,), (0,)), ((), ()))
NT = (((1,), (1,)), ((), ()))
TN = (((0,), (0,)), ((), ()))


def _params(sem=None):
    return pltpu.CompilerParams(dimension_semantics=sem, vmem_limit_bytes=VMEM_LIMIT)


def _dot(a, b, dims):
    return lax.dot_general(a, b, dims, preferred_element_type=F32)


def _sigmoid(x):
    return 1.0 / (1.0 + jnp.exp(-x))


def _matmul(name, grid, nk, k_axis, dims, n_pairs, in_specs, out_specs, out_shape, acc_shape, epilogue,
            operands, sem, aliases=None, prologue=None):
    n_in = len(in_specs)
    n_out = len(out_specs)

    def partial(ins):
        tot = None
        for p in range(n_pairs):
            a = ins[2 * p][...]
            if prologue is not None:
                a = prologue(p, a, ins)
            d = _dot(a, ins[2 * p + 1][...], dims)
            tot = d if tot is None else tot + d
        return tot

    def body(*refs):
        ins, outs = refs[:n_in], refs[n_in:n_in + n_out]
        ids = tuple(pl.program_id(a) for a in range(len(grid)))
        if nk == 1:
            epilogue(partial(ins), ins, outs, ids)
            return
        acc = refs[n_in + n_out]
        k = ids[k_axis]

        @pl.when(k == 0)
        def _():
            acc[...] = partial(ins)

        @pl.when(k > 0)
        def _():
            acc[...] += partial(ins)

        @pl.when(k == nk - 1)
        def _():
            epilogue(acc[...], ins, outs, ids)

    return pl.pallas_call(
        body, grid=grid, in_specs=in_specs, out_specs=out_specs, out_shape=out_shape,
        scratch_shapes=[] if nk == 1 else [pltpu.VMEM(acc_shape, F32)],
        input_output_aliases=aliases or {}, name=name, compiler_params=_params(sem),
    )(*operands)


def _row_spec(width=D_MODEL):
    return pl.BlockSpec((ROW_TILE, width), lambda i: (i, 0))


def _vec_spec(rows=1, width=D_MODEL):
    return pl.BlockSpec((rows, width), lambda i: (0, 0))


class Row:
    def __init__(self, table, index):
        self.table, self.index = table, index

    def spec(self):
        index = self.index
        return pl.BlockSpec((None, 1, D_MODEL), lambda *ids: (index, 0, 0))


def _slot_spec(u):
    return pl.BlockSpec((8, D_MODEL), lambda *ids: (u, 0))


def prenorm(x, gain, scale, shift):
    def body(x_ref, g_ref, sc_ref, sh_ref, h_ref):
        xv = x_ref[...]
        r = lax.rsqrt(jnp.mean(xv * xv, axis=-1, keepdims=True) + RMS_EPS)
        h_ref[...] = (((xv * r) * g_ref[...]) * (1.0 + sc_ref[...]) + sh_ref[...]).astype(BF16)

    return pl.pallas_call(
        body, grid=(SEQ // ROW_TILE,), in_specs=[_row_spec(), gain.spec(), scale.spec(), shift.spec()],
        out_specs=_row_spec(), out_shape=jax.ShapeDtypeStruct((SEQ, D_MODEL), BF16),
        name="prenorm", compiler_params=_params(("parallel",)),
    )(x, gain.table, scale.table, shift.table)


def resid_bwd(dxo, f, coef, mult, sums, u):
    def body(dx_ref, f_ref, c_ref, sums_in, df_ref, dc_ref):
        del sums_in
        dx = dx_ref[...]
        df_ref[...] = (dx * (mult * c_ref[...])).astype(BF16)
        part = mult * jnp.sum(dx * f_ref[...], axis=0, keepdims=True)

        @pl.when(pl.program_id(0) == 0)
        def _():
            dc_ref[...] = jnp.zeros_like(dc_ref)

        dc_ref[0:1, :] += part

    return pl.pallas_call(
        body, grid=(SEQ // ROW_TILE,),
        in_specs=[_row_spec(), _row_spec(), coef.spec(), pl.BlockSpec(memory_space=pl.ANY)],
        out_specs=[_row_spec(), _slot_spec(u)],
        out_shape=[jax.ShapeDtypeStruct((SEQ, D_MODEL), BF16), jax.ShapeDtypeStruct(sums.shape, F32)],
        input_output_aliases={3: 1}, name="resid_bwd", compiler_params=_params(("arbitrary",)),
    )(dxo, f, coef.table, sums)


def final_loss(x, gain, target):
    def body(x_ref, g_ref, t_ref, loss_ref, dx_ref, dg_ref):
        xv = x_ref[...]
        g = g_ref[...]
        r = lax.rsqrt(jnp.mean(xv * xv, axis=-1, keepdims=True) + RMS_EPS)
        xh = xv * r
        e = xh * g - t_ref[...]
        part = 0.5 * jnp.sum(jnp.mean(e * e, axis=-1, keepdims=True), axis=0, keepdims=True)
        dy = e * (1.0 / D_MODEL)
        dyg = dy * g
        dx_ref[...] = r * (dyg - xh * jnp.mean(dyg * xh, axis=-1, keepdims=True))

        @pl.when(pl.program_id(0) == 0)
        def _():
            loss_ref[...] = jnp.zeros_like(loss_ref)
            dg_ref[...] = jnp.zeros_like(dg_ref)

        loss_ref[...] += jnp.broadcast_to(part, loss_ref.shape)
        dg_ref[0:1, :] += jnp.sum(dy * xh, axis=0, keepdims=True)

    return pl.pallas_call(
        body, grid=(SEQ // ROW_TILE,), in_specs=[_row_spec(), _vec_spec(), _row_spec()],
        out_specs=[_vec_spec(8, 128), _row_spec(), _vec_spec(8)],
        out_shape=[jax.ShapeDtypeStruct((8, 128), F32), jax.ShapeDtypeStruct((SEQ, D_MODEL), F32),
                   jax.ShapeDtypeStruct((8, D_MODEL), F32)],
        name="final_loss", compiler_params=_params(("arbitrary",)),
    )(x, gain, target)


def _prenorm_bwd_epilogue(dh, x_ref, dxo_ref, g_ref, sc_ref, dx_ref, stats_ref, first):
    xv = x_ref[...]
    g = g_ref[...]
    r = lax.rsqrt(jnp.mean(xv * xv, axis=-1, keepdims=True) + RMS_EPS)
    xh = xv * r
    dn = dh * (1.0 + sc_ref[...])
    dxh = dn * g
    dx_ref[...] = dxo_ref[...] + r * (dxh - xh * jnp.mean(dxh * xh, axis=-1, keepdims=True))

    @pl.when(first)
    def _():
        stats_ref[...] = jnp.zeros_like(stats_ref)

    stats_ref[0:1, :] += jnp.sum(dh, axis=0, keepdims=True)
    stats_ref[1:2, :] += jnp.sum(dh * (xh * g), axis=0, keepdims=True)
    stats_ref[2:3, :] += jnp.sum(dn * xh, axis=0, keepdims=True)


def ffn_up(h, wg_all, wu_all):
    def body(h_ref, wg_ref, wu_ref, a_ref, b_ref, s_ref):
        hv = h_ref[...]
        a = _dot(hv, wg_ref[...], NT)
        b = _dot(hv, wu_ref[...], NT)
        a_ref[...] = a
        b_ref[...] = b
        s_ref[...] = (a * _sigmoid(a) * b).astype(BF16)

    w_spec = pl.BlockSpec((None, FF_SHARD, D_MODEL), lambda j, i: (j, 0, 0))
    o_spec = pl.BlockSpec((None, MM_TILE, FF_SHARD), lambda j, i: (j, i, 0))
    hid = (N_CHIPS, SEQ, FF_SHARD)
    return pl.pallas_call(
        body, grid=(N_CHIPS, SEQ // MM_TILE),
        in_specs=[pl.BlockSpec((MM_TILE, D_MODEL), lambda j, i: (i, 0)), w_spec, w_spec],
        out_specs=[o_spec, o_spec, o_spec],
        out_shape=[jax.ShapeDtypeStruct(hid, F32), jax.ShapeDtypeStruct(hid, F32), jax.ShapeDtypeStruct(hid, BF16)],
        name="ffn_up", compiler_params=_params(("parallel", "parallel")),
    )(h, wg_all, wu_all)


def matmul_residual(name, a, a_spec, w_all, w_spec, x, coef, mult):
    def epilogue(acc, ins, outs, ids):
        outs[0][...] = acc
        outs[1][...] = ins[2][...] + (mult * ins[3][...]) * acc

    row = pl.BlockSpec((MM_TILE, D_MODEL), lambda i, j: (i, 0))
    return _matmul(
        name, (SEQ // MM_TILE, N_CHIPS), N_CHIPS, 1, NN, 1,
        [a_spec, w_spec, row, coef.spec()], [row, row],
        [jax.ShapeDtypeStruct((SEQ, D_MODEL), F32)] * 2, (MM_TILE, D_MODEL), epilogue,
        (a, w_all, x, coef.table), ("parallel", "arbitrary"))


def ffn_down(s, wd_all, x, gate):
    return matmul_residual(
        "ffn_down", s, pl.BlockSpec((None, MM_TILE, FF_SHARD), lambda i, j: (j, i, 0)),
        wd_all, pl.BlockSpec((None, FF_SHARD, D_MODEL), lambda i, j: (j, 0, 0)), x, gate, 0.5)


def ffn_bwd_hidden(df, wd_all, a, b):
    def epilogue(ds, ins, outs, ids):
        av, bv = ins[2][...], ins[3][...]
        sig = _sigmoid(av)
        outs[0][...] = (ds * bv * (sig * (1.0 + av * (1.0 - sig)))).astype(BF16)
        outs[1][...] = (ds * (av * sig)).astype(BF16)

    hid_spec = pl.BlockSpec((None, MM_TILE, FF_SHARD), lambda j, i: (j, i, 0))
    hid = jax.ShapeDtypeStruct((N_CHIPS, SEQ, FF_SHARD), BF16)
    return _matmul(
        "ffn_bwd_hidden", (N_CHIPS, SEQ // MM_TILE), 1, None, NT, 1,
        [pl.BlockSpec((MM_TILE, D_MODEL), lambda j, i: (i, 0)),
         pl.BlockSpec((None, FF_SHARD, D_MODEL), lambda j, i: (j, 0, 0)), hid_spec, hid_spec],
        [hid_spec, hid_spec], [hid, hid], None, epilogue, (df, wd_all, a, b), ("parallel", "parallel"))


def grad_weight(name, lhs, lhs_spec, rhs, rhs_spec, shape):
    def epilogue(acc, ins, outs, ids):
        outs[0][...] = acc.astype(BF16)

    return _matmul(
        name, (N_CHIPS, SEQ // MM_TILE), SEQ // MM_TILE, 1, TN, 1,
        [lhs_spec, rhs_spec], [pl.BlockSpec((None,) + shape, lambda j, k: (j, 0, 0))],
        [jax.ShapeDtypeStruct((N_CHIPS,) + shape, BF16)], shape, epilogue, (lhs, rhs), ("parallel", "arbitrary"))[0]


def ffn_grad_weights(h, s, df, da, db):
    tok = pl.BlockSpec((MM_TILE, D_MODEL), lambda j, k: (k, 0))
    hid = pl.BlockSpec((None, MM_TILE, FF_SHARD), lambda j, k: (j, k, 0))
    return (grad_weight("grad_w_gate", da, hid, h, tok, (FF_SHARD, D_MODEL)),
            grad_weight("grad_w_up", db, hid, h, tok, (FF_SHARD, D_MODEL)),
            grad_weight("grad_w_down", s, hid, df, tok, (FF_SHARD, D_MODEL)))


def matmul_prenorm_bwd(name, dims, pairs, pair_specs, x, dxo, gain, scale, sums, u):
    n = len(pairs)

    def epilogue(dh, ins, outs, ids):
        _prenorm_bwd_epilogue(dh, ins[n], ins[n + 1], ins[n + 2], ins[n + 3], outs[0], outs[1], ids[0] == 0)

    row = pl.BlockSpec((MM_TILE, D_MODEL), lambda i, j: (i, 0))
    return _matmul(
        name, (SEQ // MM_TILE, N_CHIPS), N_CHIPS, 1, dims, len(pairs) // 2,
        list(pair_specs) + [row, row, gain.spec(), scale.spec(), pl.BlockSpec(memory_space=pl.ANY)],
        [row, _slot_spec(u)], [jax.ShapeDtypeStruct((SEQ, D_MODEL), F32), jax.ShapeDtypeStruct(sums.shape, F32)],
        (MM_TILE, D_MODEL), epilogue, tuple(pairs) + (x, dxo, gain.table, scale.table, sums),
        ("arbitrary", "arbitrary"), aliases={n + 4: 1})


def ffn_bwd_input(da, db, wg_all, wu_all, x, dxo, gain, scale, sums, u):
    hid = pl.BlockSpec((None, MM_TILE, FF_SHARD), lambda i, j: (j, i, 0))
    w = pl.BlockSpec((None, FF_SHARD, D_MODEL), lambda i, j: (j, 0, 0))
    return matmul_prenorm_bwd("ffn_bwd_input", NN, (da, wg_all, db, wu_all), (hid, w, hid, w), x, dxo, gain, scale,
                              sums, u)


def in_proj(h, w_all):
    def epilogue(acc, ins, outs, ids):
        outs[0][...] = acc
        outs[1][...] = acc.astype(BF16)

    out = pl.BlockSpec((MM_TILE, IN_SHARD), lambda j, i: (i, j))
    return _matmul(
        "in_proj", (N_CHIPS, SEQ // MM_TILE), 1, None, NN, 1,
        [pl.BlockSpec((MM_TILE, D_MODEL), lambda j, i: (i, 0)),
         pl.BlockSpec((None, D_MODEL, IN_SHARD), lambda j, i: (j, 0, 0))],
        [out, out], [jax.ShapeDtypeStruct((SEQ, D_IN), F32), jax.ShapeDtypeStruct((SEQ, D_IN), BF16)],
        None, epilogue, (h, w_all), ("parallel", "parallel"))


_GATE_BLOCK0 = D_QKV // D_SHARD


def _branch_products(o, w_ref):
    ob = o.astype(BF16)
    return (_dot(ob[:, 0:256], w_ref[0:256, :], NN), _dot(ob[:, 256:384], w_ref[256:384, :], NN),
            _dot(ob[:, 384:768], w_ref[384:768, :], NN))


def merge_branches(o_cat, wbr_all, proj):
    def body(o_ref, w_ref, g0_ref, g1_ref, g2_ref, m_ref):
        u = _branch_products(o_ref[...], w_ref)
        m_ref[...] = (_sigmoid(g0_ref[...]) * u[0] + _sigmoid(g1_ref[...]) * u[1]
                      + _sigmoid(g2_ref[...]) * u[2]).astype(BF16)

    def gate_spec(b):
        return pl.BlockSpec((MM_TILE, D_SHARD), lambda i, j: (i, _GATE_BLOCK0 + 4 * b + j))

    return pl.pallas_call(
        body, grid=(SEQ // MM_TILE, N_CHIPS),
        in_specs=[pl.BlockSpec((MM_TILE, BR_ROWS), lambda i, j: (i, 0)),
                  pl.BlockSpec((None, BR_ROWS, D_SHARD), lambda i, j: (j, 0, 0)),
                  gate_spec(0), gate_spec(1), gate_spec(2)],
        out_specs=pl.BlockSpec((MM_TILE, D_SHARD), lambda i, j: (i, j)),
        out_shape=jax.ShapeDtypeStruct((SEQ, D_MODEL), BF16),
        name="merge_branches", compiler_params=_params(("parallel", "parallel")),
    )(o_cat, wbr_all, proj, proj, proj)


def out_proj(merged, wout_all, x, gate):
    return matmul_residual(
        "out_proj", merged, pl.BlockSpec((MM_TILE, D_SHARD), lambda i, j: (i, j)),
        wout_all, pl.BlockSpec((None, D_SHARD, D_MODEL), lambda i, j: (j, 0, 0)), x, gate, 1.0)


def merge_bwd(dmo, wout_all, o_cat, wbr_all, proj):
    def epilogue(dm, ins, outs, ids):
        u = _branch_products(ins[2][...], ins[3])
        for b in range(3):
            sig = _sigmoid(ins[4 + b][...])
            outs[b][...] = (dm * sig).astype(BF16)
            outs[3 + b][...] = (dm * u[b] * (sig * (1.0 - sig))).astype(BF16)

    def gate_spec(b):
        return pl.BlockSpec((MM_TILE, D_SHARD), lambda j, i: (i, _GATE_BLOCK0 + 4 * b + j))

    col = pl.BlockSpec((MM_TILE, D_SHARD), lambda j, i: (i, j))
    du = jax.ShapeDtypeStruct((SEQ, D_MODEL), BF16)
    return _matmul(
        "merge_bwd", (N_CHIPS, SEQ // MM_TILE), 1, None, NT, 1,
        [pl.BlockSpec((MM_TILE, D_MODEL), lambda j, i: (i, 0)),
         pl.BlockSpec((None, D_SHARD, D_MODEL), lambda j, i: (j, 0, 0)),
         pl.BlockSpec((MM_TILE, BR_ROWS), lambda j, i: (i, 0)),
         pl.BlockSpec((None, BR_ROWS, D_SHARD), lambda j, i: (j, 0, 0)),
         gate_spec(0), gate_spec(1), gate_spec(2)],
        [col] * 6, [du] * 6,
        None, epilogue, (dmo, wout_all, o_cat, wbr_all, proj, proj, proj), ("parallel", "parallel"))


def branch_bwd_input(du, wbr_all):
    def body(d0_ref, d1_ref, d2_ref, w_ref, o_ref, acc):
        j = pl.program_id(1)
        parts = (_dot(d0_ref[...], w_ref[0:256, :], NT), _dot(d1_ref[...], w_ref[256:384, :], NT),
                 _dot(d2_ref[...], w_ref[384:768, :], NT))

        @pl.when(j == 0)
        def _():
            acc[:, 0:256], acc[:, 256:384], acc[:, 384:768] = parts

        @pl.when(j > 0)
        def _():
            acc[:, 0:256] += parts[0]
            acc[:, 256:384] += parts[1]
            acc[:, 384:768] += parts[2]

        @pl.when(j == N_CHIPS - 1)
        def _():
            o_ref[...] = acc[...]

    col = pl.BlockSpec((MM_TILE, D_SHARD), lambda i, j: (i, j))
    return pl.pallas_call(
        body, grid=(SEQ // MM_TILE, N_CHIPS),
        in_specs=[col, col, col, pl.BlockSpec((None, BR_ROWS, D_SHARD), lambda i, j: (j, 0, 0))],
        out_specs=pl.BlockSpec((MM_TILE, BR_ROWS), lambda i, j: (i, 0)),
        out_shape=jax.ShapeDtypeStruct((SEQ, BR_ROWS), F32),
        scratch_shapes=[pltpu.VMEM((MM_TILE, BR_ROWS), F32)],
        name="branch_bwd_input", compiler_params=_params(("parallel", "arbitrary")),
    )(du[0], du[1], du[2], wbr_all)


def branch_grad_weights(o_cat, du):
    def body(o_ref, d0_ref, d1_ref, d2_ref, g_ref, acc):
        k = pl.program_id(1)
        ob = o_ref[...].astype(BF16)
        parts = (_dot(ob[:, 0:256], d0_ref[...], TN), _dot(ob[:, 256:384], d1_ref[...], TN),
                 _dot(ob[:, 384:768], d2_ref[...], TN))

        @pl.when(k == 0)
        def _():
            acc[0:256, :], acc[256:384, :], acc[384:768, :] = parts

        @pl.when(k > 0)
        def _():
            acc[0:256, :] += parts[0]
            acc[256:384, :] += parts[1]
            acc[384:768, :] += parts[2]

        @pl.when(k == SEQ // MM_TILE - 1)
        def _():
            g_ref[...] = acc[...].astype(BF16)

    col = pl.BlockSpec((MM_TILE, D_SHARD), lambda j, k: (k, j))
    return pl.pallas_call(
        body, grid=(N_CHIPS, SEQ // MM_TILE),
        in_specs=[pl.BlockSpec((MM_TILE, BR_ROWS), lambda j, k: (k, 0)), col, col, col],
        out_specs=pl.BlockSpec((None, BR_ROWS, D_SHARD), lambda j, k: (j, 0, 0)),
        out_shape=jax.ShapeDtypeStruct((N_CHIPS, BR_ROWS, D_SHARD), BF16),
        scratch_shapes=[pltpu.VMEM((BR_ROWS, D_SHARD), F32)],
        name="branch_grad_weights", compiler_params=_params(("parallel", "arbitrary")),
    )(o_cat, du[0], du[1], du[2])


def mixer_bwd_input(dproj, win_all, x, dxo, gain, scale, sums, u):
    return matmul_prenorm_bwd(
        "mixer_bwd_input", NT, (dproj, win_all),
        (pl.BlockSpec((MM_TILE, IN_SHARD), lambda i, j: (i, j)),
         pl.BlockSpec((None, D_MODEL, IN_SHARD), lambda i, j: (j, 0, 0))), x, dxo, gain, scale, sums, u)


BATCH_QK = (((2,), (2,)), ((0,), (0,)))
BATCH_PV = (((2,), (1,)), ((0,), (0,)))
BATCH_TN = (((1,), (1,)), ((0,), (0,)))


SB_WIDTH = H_SB * HEAD_DIM
SB_ROWS = H_SB * BLK


def _split_dot(v, tri):
    hi = v.astype(BF16)
    lo = (v - hi.astype(F32)).astype(BF16)
    return _dot(hi, tri, NN) + _dot(lo, tri, NN)


def _tri(cmp):
    return cmp(lax.broadcasted_iota(jnp.int32, (BLK, BLK), 0), lax.broadcasted_iota(jnp.int32, (BLK, BLK), 1)).astype(BF16)


def _head_masks():
    lane = lax.broadcasted_iota(jnp.int32, (1, SB_WIDTH), 1) // HEAD_DIM
    return [lane == h for h in range(H_SB)]


def _stack_heads(x, masks):
    return jnp.concatenate([jnp.where(m, x, jnp.zeros_like(x)) for m in masks], axis=0)


def _merge_heads(y, masks):
    out = jnp.where(masks[0], y[0:BLK], 0.0)
    for h in range(1, H_SB):
        out = jnp.where(masks[h], y[h * BLK:(h + 1) * BLK], out)
    return out


def _sb_scores(q4, k_ref, j, diagonal):
    rows = pl.ds(pl.multiple_of(j * BLK, BLK), BLK)
    z = _dot(q4, k_ref[rows, :], NT)
    log_fail = -(jnp.maximum(z, 0.0) + jnp.log(1.0 + jnp.exp(-jnp.abs(z))))
    log_hit = z + log_fail
    before = None
    if diagonal:
        tile = (SB_ROWS, BLK)
        before = lax.broadcasted_iota(jnp.int32, tile, 1) < (lax.broadcasted_iota(jnp.int32, tile, 0) & (BLK - 1))
        log_fail = jnp.where(before, log_fail, 0.0)
    return rows, before, log_fail, log_hit


def _keep(before, x):
    return x if before is None else jnp.where(before, x, 0.0)


def sb_forward(qkv):
    def body(q_ref, k_ref, v_ref, o_ref, tot_ref):
        i = pl.program_id(0)
        masks = _head_masks()
        q4 = _stack_heads(q_ref[...] * QK_SCALE, masks)
        later = _tri(lambda r, c: r > c)

        def tiles(js, carry, diagonal):
            o, run = carry
            scores = [_sb_scores(q4, k_ref, j, diagonal) for j in js]
            acc = None
            for rows, before, log_fail, log_hit in scores:
                between = _split_dot(log_fail, later) + run
                w = _keep(before, jnp.exp(log_hit + between))
                part = _dot(w.astype(BF16), v_ref[rows, :], NN)
                acc = part if acc is None else acc + part
                run = run + jnp.sum(log_fail, axis=1, keepdims=True)
            return o + _merge_heads(acc, masks), run

        carry = tiles([i], (jnp.zeros((BLK, SB_WIDTH), F32), jnp.zeros((SB_ROWS, 1), F32)), True)
        carry = lax.cond((i & 1) != 0, lambda c: tiles([i - 1], c, False), lambda c: c, carry)
        at = i - 1 - (i & 1)
        carry = lax.cond((i & 2) != 0, lambda c: tiles([at, at - 1], c, False), lambda c: c, carry)
        at = at - (i & 2)
        o, run = lax.fori_loop(0, i // 4, lambda t, c: tiles([at - 4 * t - n for n in range(4)], c, False), carry)
        o_ref[...] = o
        tot_ref[...] = run

    return pl.pallas_call(
        body, grid=(N_BLK,),
        in_specs=[pl.BlockSpec((BLK, SB_WIDTH), lambda i: (i, 0)), pl.BlockSpec((SEQ, SB_WIDTH), lambda i: (0, 1)),
                  pl.BlockSpec((SEQ, SB_WIDTH), lambda i: (0, 2))],
        out_specs=[pl.BlockSpec((BLK, SB_WIDTH), lambda i: (i, 0)), pl.BlockSpec((None, SB_ROWS, 1), lambda i: (i, 0, 0))],
        out_shape=[jax.ShapeDtypeStruct((SEQ, SB_WIDTH), F32), jax.ShapeDtypeStruct((N_BLK, SB_ROWS, 1), F32)],
        name="sb_forward", compiler_params=_params(("parallel",)),
    )(qkv, qkv, qkv)


def sb_backward(qkv, total, do_cat):
    def body(q_ref, k_ref, v_ref, tot_ref, do_ref, dq_ref, dk_ref, dv_ref):
        i = pl.program_id(0)

        @pl.when(i == 0)
        def _():
            dk_ref[...] = jnp.zeros_like(dk_ref)
            dv_ref[...] = jnp.zeros_like(dv_ref)

        masks = _head_masks()
        q4 = _stack_heads(q_ref[...] * QK_SCALE, masks)
        do4 = _stack_heads(do_ref[...].astype(BF16), masks)
        total_v = tot_ref[...]
        upto = _tri(lambda r, c: r <= c)
        earlier = _tri(lambda r, c: r < c)

        def tiles(js, carry, diagonal):
            dq, seen, g_seen = carry
            scores = [_sb_scores(q4, k_ref, j, diagonal) for j in js]
            acc = None
            for rows, before, log_fail, log_hit in scores:
                between = total_v - (seen + _split_dot(log_fail, upto))
                w = _keep(before, jnp.exp(log_hit + between))
                g = _dot(do4, v_ref[rows, :], NT) * w
                g_earlier = g_seen + _split_dot(g, earlier)
                sig = jnp.exp(log_hit)
                dz = _keep(before, g * (1.0 - sig) - g_earlier * sig).astype(BF16)
                part = _dot(dz, k_ref[rows, :], NN)
                acc = part if acc is None else acc + part
                dk_ref[rows, :] += _dot(dz, q4, TN)
                dv_ref[rows, :] += _dot(w.astype(BF16), do4, TN)
                seen = seen + jnp.sum(log_fail, axis=1, keepdims=True)
                g_seen = g_seen + jnp.sum(g, axis=1, keepdims=True)
            return dq + _merge_heads(acc, masks), seen, g_seen

        zero = jnp.zeros((SB_ROWS, 1), F32)
        carry = lax.fori_loop(0, i // 4, lambda t, c: tiles([4 * t + n for n in range(4)], c, False),
                              (jnp.zeros((BLK, SB_WIDTH), F32), zero, zero))
        at = i - (i & 3)
        carry = lax.cond((i & 2) != 0, lambda c: tiles([at, at + 1], c, False), lambda c: c, carry)
        carry = lax.cond((i & 1) != 0, lambda c: tiles([i - 1], c, False), lambda c: c, carry)
        dq, _, _ = tiles([i], carry, True)
        dq_ref[...] = dq * QK_SCALE

    blk = pl.BlockSpec((BLK, SB_WIDTH), lambda i: (i, 0))
    full = pl.BlockSpec((SEQ, SB_WIDTH), lambda i: (0, 0))
    shape = jax.ShapeDtypeStruct((SEQ, SB_WIDTH), F32)
    return pl.pallas_call(
        body, grid=(N_BLK,),
        in_specs=[blk, pl.BlockSpec((SEQ, SB_WIDTH), lambda i: (0, 1)), pl.BlockSpec((SEQ, SB_WIDTH), lambda i: (0, 2)),
                  pl.BlockSpec((None, SB_ROWS, 1), lambda i: (i, 0, 0)), blk],
        out_specs=[blk, full, full], out_shape=[shape, shape, shape],
        name="sb_backward", compiler_params=_params(("arbitrary",)),
    )(qkv, qkv, qkv, total, do_cat)


def _band_scores(q_ref, kp_ref, ko_ref, bias_ref, hb, prev_mask):
    b = pl.program_id(1)
    qs = q_ref[...]
    s_prev = _dot(qs, kp_ref[...], BATCH_QK) + bias_ref[:, :, 0:BLK]
    s_prev = jnp.concatenate(
        [jnp.where((b & prev_mask(pl.program_id(0) * hb + t)) != 0, s_prev[t:t + 1], NEG) for t in range(hb)], axis=0)
    s_own = _dot(qs, ko_ref[...], BATCH_QK) + bias_ref[:, :, BLK:2 * BLK]
    return qs, s_prev, s_own


def _band_specs(hb, rows, t_n):
    def q_spec(width):
        return pl.BlockSpec((hb, None, rows, width), lambda h, b: (h, b, 0, 0))

    own = pl.BlockSpec((hb, BLK, HEAD_DIM), lambda h, b: (h, b, 0))
    prev = pl.BlockSpec((hb, BLK, HEAD_DIM), lambda h, b: (h, jnp.maximum(b - 1, 0), 0))
    per_head = lambda r, width: pl.BlockSpec((hb, r, width), lambda h, b: (h, 0, 0))
    return q_spec, own, prev, per_head


def banded_forward(name, q, k, v, bias, sinks, hb, prev_mask):
    h_n, nb, rows, _ = q.shape

    def body(q_ref, kp_ref, ko_ref, vp_ref, vo_ref, bias_ref, sink_ref, o_ref, lse_ref):
        _, s_prev, s_own = _band_scores(q_ref, kp_ref, ko_ref, bias_ref, hb, prev_mask)
        sink = sink_ref[...]
        m = jnp.maximum(jnp.maximum(jnp.max(s_prev, axis=2, keepdims=True), jnp.max(s_own, axis=2, keepdims=True)), sink)
        p_prev = jnp.exp(s_prev - m)
        p_own = jnp.exp(s_own - m)
        denom = jnp.sum(p_prev, axis=2, keepdims=True) + jnp.sum(p_own, axis=2, keepdims=True) + jnp.exp(sink - m)
        o = _dot(p_prev.astype(BF16), vp_ref[...], BATCH_PV) + _dot(p_own.astype(BF16), vo_ref[...], BATCH_PV)
        o_ref[...] = o / denom
        lse_ref[...] = m + jnp.log(denom)

    q_spec, own, prev, per_head = _band_specs(hb, rows, k.shape[1])
    return pl.pallas_call(
        body, grid=(h_n // hb, nb),
        in_specs=[q_spec(HEAD_DIM), prev, own, prev, own, per_head(rows, 2 * BLK), per_head(rows, 1)],
        out_specs=[q_spec(HEAD_DIM), q_spec(1)],
        out_shape=[jax.ShapeDtypeStruct(q.shape, F32), jax.ShapeDtypeStruct((h_n, nb, rows, 1), F32)],
        name=name, compiler_params=_params(("parallel", "parallel")),
    )(q, k, k, v, v, bias, sinks)


def banded_backward(name, q, k, v, bias, sinks, o, lse, do, dlse, hb, prev_mask):
    h_n, nb, rows, _ = q.shape
    t_n = k.shape[1]

    def body(q_ref, kp_ref, ko_ref, vp_ref, vo_ref, bias_ref, sink_ref, o_ref, lse_ref, do_ref, dlse_ref,
             dq_ref, dk_ref, dv_ref, dbias_ref, dsink_ref):
        b = pl.program_id(1)

        @pl.when(b == 0)
        def _():
            dk_ref[...] = jnp.zeros_like(dk_ref)
            dv_ref[...] = jnp.zeros_like(dv_ref)
            dbias_ref[...] = jnp.zeros_like(dbias_ref)
            dsink_ref[...] = jnp.zeros_like(dsink_ref)

        qs, s_prev, s_own = _band_scores(q_ref, kp_ref, ko_ref, bias_ref, hb, prev_mask)
        lse_v = lse_ref[...]
        dov = do_ref[...]
        dob = dov.astype(BF16)
        shift = dlse_ref[...] - jnp.sum(dov * o_ref[...], axis=2, keepdims=True)
        p_prev = jnp.exp(s_prev - lse_v)
        p_own = jnp.exp(s_own - lse_v)
        ds_prev = p_prev * (_dot(dob, vp_ref[...], BATCH_QK) + shift)
        ds_own = p_own * (_dot(dob, vo_ref[...], BATCH_QK) + shift)
        dbias_ref[:, :, 0:BLK] += ds_prev
        dbias_ref[:, :, BLK:2 * BLK] += ds_own
        d_sink = jnp.exp(sink_ref[...] - lse_v) * shift
        for g in range(rows // BLK):
            dsink_ref[:, g:g + 1, :] += jnp.sum(d_sink[:, g * BLK:(g + 1) * BLK, :], axis=1, keepdims=True)
        ds_prev = ds_prev.astype(BF16)
        ds_own = ds_own.astype(BF16)
        dq_ref[...] = (_dot(ds_prev, kp_ref[...], BATCH_PV) + _dot(ds_own, ko_ref[...], BATCH_PV)) * QK_SCALE
        rows_prev = pl.ds(pl.multiple_of(jnp.maximum(b - 1, 0) * BLK, BLK), BLK)
        rows_own = pl.ds(pl.multiple_of(b * BLK, BLK), BLK)
        dk_ref[:, rows_prev, :] += _dot(ds_prev, qs, BATCH_TN)
        dk_ref[:, rows_own, :] += _dot(ds_own, qs, BATCH_TN)
        dv_ref[:, rows_prev, :] += _dot(p_prev.astype(BF16), dob, BATCH_TN)
        dv_ref[:, rows_own, :] += _dot(p_own.astype(BF16), dob, BATCH_TN)

    q_spec, own, prev, per_head = _band_specs(hb, rows, t_n)
    kv_full = per_head(t_n, HEAD_DIM)
    kv_shape = jax.ShapeDtypeStruct((h_n, t_n, HEAD_DIM), F32)
    return pl.pallas_call(
        body, grid=(h_n // hb, nb),
        in_specs=[q_spec(HEAD_DIM), prev, own, prev, own, per_head(rows, 2 * BLK), per_head(rows, 1),
                  q_spec(HEAD_DIM), q_spec(1), q_spec(HEAD_DIM), q_spec(1)],
        out_specs=[q_spec(HEAD_DIM), kv_full, kv_full, per_head(rows, 2 * BLK), per_head(rows // BLK, BLK)],
        out_shape=[jax.ShapeDtypeStruct(q.shape, F32), kv_shape, kv_shape,
                   jax.ShapeDtypeStruct((h_n, rows, 2 * BLK), F32), jax.ShapeDtypeStruct((h_n, rows // BLK, BLK), F32)],
        name=name, compiler_params=_params(("parallel", "arbitrary")),
    )(q, k, k, v, v, bias, sinks, o, lse, do, dlse)


def _dil_prev_mask(head):
    group = head // H_PER_DIL
    return jnp.where(group == 0, 15, jnp.where(group == 1, 3, 0))


def _swa_prev_mask(head):
    del head
    return 15


DIL_HEADS_PER_STEP = 3
SWA_GROUP = H_SWA_Q // H_SWA_KV
N_BLK = SEQ // BLK


def dilated_merge(o, lse):
    def body(o_ref, l_ref, out_ref):
        lv = l_ref[...]
        m = jnp.max(lv, axis=0, keepdims=True)
        e = jnp.exp(lv - m)
        alpha = e / jnp.sum(e, axis=0, keepdims=True)
        out_ref[...] = jnp.sum(alpha * o_ref[...], axis=0)

    return pl.pallas_call(
        body, grid=(H_PER_DIL, SEQ // ROW_TILE),
        in_specs=[pl.BlockSpec((3, None, ROW_TILE, HEAD_DIM), lambda h, i: (0, h, i, 0)),
                  pl.BlockSpec((3, None, ROW_TILE, 1), lambda h, i: (0, h, i, 0))],
        out_specs=pl.BlockSpec((None, ROW_TILE, HEAD_DIM), lambda h, i: (h, i, 0)),
        out_shape=jax.ShapeDtypeStruct((H_PER_DIL, SEQ, HEAD_DIM), F32),
        name="dilated_merge", compiler_params=_params(("parallel", "parallel")),
    )(o, lse)


def dilated_merge_bwd(o, lse, dout):
    def body(o_ref, l_ref, d_ref, do_ref, dl_ref):
        lv = l_ref[...]
        m = jnp.max(lv, axis=0, keepdims=True)
        e = jnp.exp(lv - m)
        alpha = e / jnp.sum(e, axis=0, keepdims=True)
        dv = d_ref[...][None]
        do_ref[...] = alpha * dv
        dalpha = jnp.sum(dv * o_ref[...], axis=-1, keepdims=True)
        dl_ref[...] = alpha * (dalpha - jnp.sum(alpha * dalpha, axis=0, keepdims=True))

    o_spec = pl.BlockSpec((3, None, ROW_TILE, HEAD_DIM), lambda h, i: (0, h, i, 0))
    l_spec = pl.BlockSpec((3, None, ROW_TILE, 1), lambda h, i: (0, h, i, 0))
    return pl.pallas_call(
        body, grid=(H_PER_DIL, SEQ // ROW_TILE),
        in_specs=[o_spec, l_spec, pl.BlockSpec((None, ROW_TILE, HEAD_DIM), lambda h, i: (h, i, 0))],
        out_specs=[o_spec, l_spec],
        out_shape=[jax.ShapeDtypeStruct(o.shape, F32), jax.ShapeDtypeStruct(lse.shape, F32)],
        name="dilated_merge_bwd", compiler_params=_params(("parallel", "parallel")),
    )(o, lse, dout)


def rel_bias_reduce(dbias0, dbias1, bucket):
    def body(d0_ref, d1_ref, b_ref, o_ref):
        dv, bv = d0_ref[...] + d1_ref[...], b_ref[...]
        lane = lax.broadcasted_iota(jnp.int32, (1, BLK), 1)
        acc = jnp.zeros((1, BLK), F32)
        for bkt in range(N_BUCKETS):
            acc = acc + jnp.where(lane == bkt, jnp.sum(jnp.where(bv == bkt, dv, 0.0)), 0.0)
        o_ref[...] = acc

    tile = pl.BlockSpec((None, BLK, 2 * BLK), lambda h: (h, 0, 0))
    return pl.pallas_call(
        body, grid=(dbias0.shape[0],), in_specs=[tile, tile, tile],
        out_specs=pl.BlockSpec((None, 1, BLK), lambda h: (h, 0, 0)),
        out_shape=jax.ShapeDtypeStruct((dbias0.shape[0], 1, BLK), F32),
        name="rel_bias_reduce", compiler_params=_params(("parallel",)),
    )(dbias0, dbias1, bucket)


def _heads(t):
    return t.reshape(SEQ, -1, HEAD_DIM).transpose(1, 0, 2)


def _unheads(t):
    return t.transpose(1, 0, 2).reshape(SEQ, -1)


def _dilate(t):
    parts = []
    for g, (_, d) in enumerate(DIL_PATTERNS):
        tg = t[:, 128 * g:128 * (g + 1)].reshape(SEQ // d, d, H_PER_DIL, HEAD_DIM).transpose(2, 1, 0, 3)
        parts.append(tg.reshape(H_PER_DIL, SEQ, HEAD_DIM))
    return jnp.concatenate(parts, axis=0)


def _undilate(t):
    outs = []
    for g, (_, d) in enumerate(DIL_PATTERNS):
        tg = t[2 * g:2 * g + 2].reshape(H_PER_DIL, d, SEQ // d, -1).transpose(0, 2, 1, 3)
        outs.append(tg.reshape(H_PER_DIL, SEQ, -1))
    return jnp.stack(outs)


def _redilate(t):
    parts = []
    for g, (_, d) in enumerate(DIL_PATTERNS):
        tg = t[g].reshape(H_PER_DIL, SEQ // d, d, -1).transpose(0, 2, 1, 3)
        parts.append(tg.reshape(H_PER_DIL, SEQ, -1))
    return jnp.concatenate(parts, axis=0)


def _t5_bucket(n):
    max_exact = N_BUCKETS // 2
    nf = jnp.maximum(n, 1).astype(F32)
    large = max_exact + (jnp.log(nf / max_exact) / math.log(MAX_REL_DIST / max_exact)
                         * (N_BUCKETS - max_exact)).astype(jnp.int32)
    large = jnp.minimum(large, N_BUCKETS - 1)
    return jnp.where(n < max_exact, n, large)


def band_tables(rel_bias):
    rel = jnp.arange(BLK)[:, None] + BLK - jnp.arange(2 * BLK)[None, :]
    buckets = []
    patterns = [(d, w // d) for w, d in DIL_PATTERNS for _ in range(H_PER_DIL)] + [(1, SWA_WINDOW - 1)] * H_SWA_Q
    for d, max_dist in patterns:
        band = (rel >= 0) & (rel <= max_dist)
        buckets.append(jnp.where(band, _t5_bucket(jnp.maximum(rel, 0) * d), -1))
    buckets = jnp.stack(buckets).astype(jnp.int32)

    def body(table_ref, b_ref, o_ref):
        h = pl.program_id(0)
        bv = b_ref[...]
        tile = jnp.full(bv.shape, NEG, F32)
        for bkt in range(N_BUCKETS):
            tile = jnp.where(bv == bkt, table_ref[h, bkt], tile)
        o_ref[...] = tile

    spec = pl.BlockSpec((None, BLK, 2 * BLK), lambda h: (h, 0, 0))
    tiles = pl.pallas_call(
        body, grid=(len(patterns),), in_specs=[pl.BlockSpec(memory_space=pltpu.SMEM), spec], out_specs=spec,
        out_shape=jax.ShapeDtypeStruct(buckets.shape, F32), name="band_tables", compiler_params=_params(("parallel",)),
    )(rel_bias.T, buckets)
    return tiles[:H_DIL], tiles[H_DIL:], buckets


def _swa_rows(t):
    t = t.reshape(N_BLK, BLK, H_SWA_KV, SWA_GROUP, HEAD_DIM).transpose(2, 0, 3, 1, 4)
    return t.reshape(H_SWA_KV, N_BLK, SWA_GROUP * BLK, HEAD_DIM)


def _swa_tokens(t):
    t = t.reshape(H_SWA_KV, N_BLK, SWA_GROUP, BLK, HEAD_DIM).transpose(1, 3, 0, 2, 4)
    return t.reshape(SEQ, H_SWA_Q * HEAD_DIM)


def _sink_rows(sinks):
    return jnp.broadcast_to(sinks.reshape(H_SWA_KV, SWA_GROUP, 1, 1), (H_SWA_KV, SWA_GROUP, BLK, 1)).reshape(
        H_SWA_KV, SWA_GROUP * BLK, 1)


def _no_sinks():
    return jnp.full((H_DIL, BLK, 1), NEG, F32)


def _vec(v):
    return v.reshape(1, D_MODEL)


class UnitRows:
    def __init__(self, u, mod_table, gain_table):
        self.shift, self.scale, self.gate = (Row(mod_table, 3 * u + t) for t in range(3))
        self.gain = Row(gain_table, u)


def ffn_forward(x, rows, w):
    h = prenorm(x, rows.gain, rows.scale, rows.shift)
    a, b, s = ffn_up(h, w[0], w[1])
    f, xo = ffn_down(s, w[2], x, rows.gate)
    return xo, (x, h, a, b, s, f)


def ffn_backward(u, dxo, saved, rows, w, sums):
    x, h, a, b, s, f = saved
    df, gate_sums = resid_bwd(dxo, f, rows.gate, 0.5, sums[1], u)
    da, db = ffn_bwd_hidden(df, w[2], a, b)
    grads = ffn_grad_weights(h, s, df, da, db)
    dx, norm_sums = ffn_bwd_input(da, db, w[0], w[1], x, dxo, rows.gain, rows.scale, sums[0], u)
    return dx, (norm_sums, gate_sums), grads


def mixer_forward(x, rows, sinks, bias_dil, bias_swa, w):
    h = prenorm(x, rows.gain, rows.scale, rows.shift)
    proj, qkv = in_proj(h, w[0])
    q_dil, k_dil, v_dil = _dilate(qkv[:, 768:1152] * QK_SCALE), _dilate(qkv[:, 1152:1536]), _dilate(qkv[:, 1536:1920])
    q_swa, k_swa, v_swa = _swa_rows(qkv[:, 1920:2304] * QK_SCALE), _heads(qkv[:, 2304:2432]), _heads(qkv[:, 2432:2560])
    o_sb, total_sb = sb_forward(qkv)
    q_dil = q_dil.reshape(H_DIL, N_BLK, BLK, HEAD_DIM)
    o_dd, lse_dd = banded_forward("dilated_forward", q_dil, k_dil, v_dil, bias_dil, _no_sinks(), DIL_HEADS_PER_STEP,
                                  _dil_prev_mask)
    o_dt, lse_dt = _undilate(o_dd.reshape(H_DIL, SEQ, HEAD_DIM)), _undilate(lse_dd.reshape(H_DIL, SEQ, 1))
    o_dil = dilated_merge(o_dt, lse_dt)
    bias_swa = bias_swa.reshape(H_SWA_KV, SWA_GROUP * BLK, 2 * BLK)
    o_swa, lse_swa = banded_forward("swa_forward", q_swa, k_swa, v_swa, bias_swa, _sink_rows(sinks), 1, _swa_prev_mask)
    o_cat = jnp.concatenate([o_sb, _unheads(o_dil), _swa_tokens(o_swa)], axis=1)
    merged = merge_branches(o_cat, w[1], proj)
    mo, xo = out_proj(merged, w[2], x, rows.gate)
    saved = (x, h, proj, (qkv, total_sb), (q_dil, k_dil, v_dil, o_dd, lse_dd, o_dt, lse_dt),
             (q_swa, k_swa, v_swa, o_swa, lse_swa), o_cat, merged, mo)
    return xo, saved


def mixer_backward(u, dxo, saved, rows, sinks, bias_dil, bias_swa, w, sums):
    x, h, proj, sb, dil, swa, o_cat, merged, mo = saved
    dmo, gate_sums = resid_bwd(dxo, mo, rows.gate, 1.0, sums[1], u)
    tok = pl.BlockSpec((MM_TILE, D_MODEL), lambda j, k: (k, 0))
    g_out = grad_weight("grad_w_out", merged, pl.BlockSpec((MM_TILE, D_SHARD), lambda j, k: (k, j)), dmo, tok,
                        (D_SHARD, D_MODEL))
    du0, du1, du2, dg0, dg1, dg2 = merge_bwd(dmo, w[2], o_cat, w[1], proj)
    du = (du0, du1, du2)
    do_cat = branch_bwd_input(du, w[1])
    g_br = branch_grad_weights(o_cat, du)

    qkv, total_sb = sb
    dq_sb, dk_sb, dv_sb = sb_backward(qkv, total_sb, do_cat)

    q_dil, k_dil, v_dil, o_dd, lse_dd, o_dt, lse_dt = dil
    do_dt, dlse_dt = dilated_merge_bwd(o_dt, lse_dt, _heads(do_cat[:, 256:384]))
    dq_dil, dk_dil, dv_dil, dbias_dil, _ = banded_backward(
        "dilated_backward", q_dil, k_dil, v_dil, bias_dil, _no_sinks(), o_dd, lse_dd,
        _redilate(do_dt).reshape(q_dil.shape), _redilate(dlse_dt).reshape(lse_dd.shape), DIL_HEADS_PER_STEP, _dil_prev_mask)

    q_swa, k_swa, v_swa, o_swa, lse_swa = swa
    bias_swa = bias_swa.reshape(H_SWA_KV, SWA_GROUP * BLK, 2 * BLK)
    dq_swa, dk_swa, dv_swa, dbias_swa, dsinks = banded_backward(
        "swa_backward", q_swa, k_swa, v_swa, bias_swa, _sink_rows(sinks), o_swa, lse_swa, _swa_rows(do_cat[:, 384:768]),
        jnp.zeros_like(lse_swa), 1, _swa_prev_mask)
    dbias_swa = dbias_swa.reshape(H_SWA_Q, BLK, 2 * BLK)

    def tokens(t):
        return _undilate(t).transpose(2, 0, 1, 3).reshape(SEQ, -1)

    dproj = jnp.concatenate(
        [dq_sb, dk_sb, dv_sb, tokens(dq_dil.reshape(H_DIL, SEQ, HEAD_DIM)), tokens(dk_dil),
         tokens(dv_dil), _swa_tokens(dq_swa), _unheads(dk_swa), _unheads(dv_swa)], axis=1).astype(BF16)
    dproj = jnp.concatenate([dproj, dg0, dg1, dg2], axis=1)
    g_in = grad_weight("grad_w_in", h, tok, dproj, pl.BlockSpec((MM_TILE, IN_SHARD), lambda j, k: (k, j)),
                       (D_MODEL, IN_SHARD))
    dx, norm_sums = mixer_bwd_input(dproj, w[0], x, dxo, rows.gain, rows.scale, sums[0], u)
    dbias = jnp.concatenate([dbias_dil, dbias_swa], axis=0)
    return dx, (norm_sums, gate_sums), dbias, dsinks[:, :, 0].reshape(H_SWA_Q), (g_in, g_br, g_out)


N_UNITS = 3 * DEPTH


def device_step(x, target, mod, gains, final_gain, sinks, rel_bias, get_weights, put_grads):
    bias_dil, bias_swa, bucket = band_tables(rel_bias)
    mod_table = mod.reshape(3 * N_UNITS, 1, D_MODEL)
    gain_table = gains.reshape(N_UNITS, 1, D_MODEL)
    saved, weights = [], []
    for u in range(N_UNITS):
        l, j = divmod(u, 3)
        w = get_weights(u, x)
        rows = UnitRows(u, mod_table, gain_table)
        if j == 1:
            x, s = mixer_forward(x, rows, sinks[l], bias_dil, bias_swa, w)
        else:
            x, s = ffn_forward(x, rows, w)
        saved.append(s)
        weights.append(w)
    loss, dx, dfinal = final_loss(x, _vec(final_gain), target)

    sums = (lax.empty((8 * N_UNITS, D_MODEL), F32), lax.empty((8 * N_UNITS, D_MODEL), F32))
    dbias, dsinks = [None] * DEPTH, [None] * DEPTH
    zero = jnp.zeros((1, 1), F32)
    for u in reversed(range(N_UNITS)):
        l, j = divmod(u, 3)
        rows = UnitRows(u, mod_table, gain_table + zero)
        if j == 1:
            dx, sums, dbias[l], dsinks[l], grads = mixer_backward(
                u, dx, saved[u], rows, sinks[l], bias_dil, bias_swa, weights[u], sums)
        else:
            dx, sums, grads = ffn_backward(u, dx, saved[u], rows, weights[u], sums)
        if u > 0:
            zero = put_grads(u, grads)
    drel = rel_bias_reduce(dbias[0], dbias[1], bucket)[:, 0, :N_BUCKETS].T
    norm_sums, gate_sums = (t.reshape(DEPTH, 3, 8, D_MODEL) for t in sums)
    dmod = jnp.stack([norm_sums[:, :, 0], norm_sums[:, :, 1], gate_sums[:, :, 0]], axis=2)
    return loss, dx, dmod, norm_sums[:, :, 2], dfinal[0], jnp.stack(dsinks), drel, grads


MESH = pl.DeviceIdType.MESH
CHIP_FLIPS = ((1, 0), (0, 1), (1, 1))
ANY = pl.BlockSpec(memory_space=pl.ANY)


def _position():
    return lax.axis_index("x"), lax.axis_index("y"), lax.axis_index("c")


def all_gather_small(name, piece):
    def body(x_ref, out_ref, send_sems, recv_sems, local_sem):
        x, y, c = _position()
        me, sibling = (x, y, c), (x, y, 1 - c)
        chips = [(x ^ fx, y ^ fy) for fx, fy in CHIP_FLIPS]

        def rows(px, py, pc):
            return out_ref.at[4 * px + 2 * py + pc]

        def copy(k, block, to, src=None):
            return pltpu.make_async_remote_copy(
                src_ref=rows(*block) if src is None else src, dst_ref=rows(*block),
                send_sem=send_sems.at[k], recv_sem=recv_sems.at[k], device_id=to, device_id_type=MESH)

        mine = pltpu.make_async_copy(x_ref, rows(*me), local_sem)
        mine.start()
        first = [copy(0, me, sibling, src=x_ref)]
        first += [copy(1 + j, me, (*chip, c), src=x_ref) for j, chip in enumerate(chips)]
        for cp in first:
            cp.start()
        passed = [copy(4 + j, (*chip, c), sibling) for j, chip in enumerate(chips)]
        for j, chip in enumerate(chips):
            copy(1 + j, (*chip, c), me).wait_recv()
            passed[j].start()
        copy(0, sibling, me).wait_recv()
        for j, chip in enumerate(chips):
            copy(4 + j, (*chip, 1 - c), me).wait_recv()
        for cp in first + passed:
            cp.wait_send()
        mine.wait()

    return pl.pallas_call(
        body, out_shape=jax.ShapeDtypeStruct((N_DEV,) + piece.shape, piece.dtype),
        in_specs=[pl.BlockSpec(memory_space=pltpu.VMEM)], out_specs=pl.BlockSpec(memory_space=pltpu.VMEM),
        scratch_shapes=[pltpu.SemaphoreType.DMA((7,)), pltpu.SemaphoreType.DMA((7,)), pltpu.SemaphoreType.DMA],
        name=name,
    )(piece)


def exchange(name, operands, out_shapes, aliases, plan):
    n_in, n_out = len(operands), len(out_shapes)

    def body(*refs):
        ins, outs = refs[:n_in], refs[n_in:n_in + n_out]
        send_sems, recv_sems, local_sems = refs[n_in + n_out:]
        x, y, c = _position()
        local, sends, recvs = plan(ins, outs, x, y, c)
        local = [pltpu.make_async_copy(s, d, local_sems.at[k]) for k, (s, d) in enumerate(local)]
        for cp in local:
            cp.start()
        remote = [pltpu.make_async_remote_copy(src_ref=s, dst_ref=d, send_sem=send_sems.at[k], recv_sem=recv_sems.at[k],
                                               device_id=dev, device_id_type=MESH)
                  for k, (s, d, dev) in enumerate(sends)]
        for cp in remote:
            cp.start()
        for k, r in enumerate(recvs):
            pltpu.make_async_remote_copy(src_ref=r, dst_ref=r, send_sem=send_sems.at[k], recv_sem=recv_sems.at[k],
                                         device_id=(x, y, c), device_id_type=MESH).wait_recv()
        for cp in remote:
            cp.wait_send()
        for cp in local:
            cp.wait()

    n_sends, n_local = plan.n_sends, max(plan.n_local, 1)
    return pl.pallas_call(
        body, out_shape=out_shapes, in_specs=[ANY] * n_in, out_specs=[ANY] * n_out,
        scratch_shapes=[pltpu.SemaphoreType.DMA((n_sends,)), pltpu.SemaphoreType.DMA((n_sends,)),
                        pltpu.SemaphoreType.DMA((n_local,))],
        input_output_aliases=aliases, name=name,
    )(*operands)


def _plan(n_local, n_sends):
    def wrap(fn):
        fn.n_local, fn.n_sends = n_local, n_sends
        return fn
    return wrap


def _half(ref, axis, c):
    rows = ref.shape[axis] // 2
    idx = [slice(None)] * len(ref.shape)
    idx[axis] = pl.ds(pl.multiple_of(c * rows, 16), rows)
    return ref.at[tuple(idx)]


HBM = pl.BlockSpec(memory_space=pltpu.HBM)
SEM = pl.BlockSpec(memory_space=pltpu.SEMAPHORE)
EFFECT = pltpu.SideEffectType.DATAFLOW_SIDE_EFFECTING


def split_start(name, bufs, extra, n_copies, describe):
    n = len(bufs)

    def body(*refs):
        send_sems, recv_sems = refs[n + len(extra)], refs[n + len(extra) + 1]
        x, y, c = _position()
        for k, (src, dst, _, peer) in enumerate(describe(refs[:n], x, y, c)):
            pltpu.make_async_remote_copy(src_ref=src, dst_ref=dst, send_sem=send_sems.at[k], recv_sem=recv_sems.at[k],
                                         device_id=peer, device_id_type=MESH).start()
        token = refs[-1]
        token[...] = jnp.zeros_like(token)

    out = pl.pallas_call(
        body, name=name,
        out_shape=(pltpu.SemaphoreType.DMA((n_copies,)), pltpu.SemaphoreType.DMA((n_copies,)),
                   *[pltpu.HBM(b.shape, b.dtype) for b in bufs], jax.ShapeDtypeStruct((8, 128), F32)),
        in_specs=[HBM] * n + [ANY] * len(extra),
        out_specs=(SEM, SEM, *[HBM] * n, pl.BlockSpec(memory_space=pltpu.VMEM)),
        input_output_aliases={k: 2 + k for k in range(n)},
        compiler_params=pltpu.CompilerParams(has_side_effects=EFFECT),
    )(*[pltpu.with_memory_space_constraint(b, pltpu.HBM) for b in bufs], *extra)
    return out[0], out[1], list(out[2:2 + n]), out[-1]


def split_wait(name, bufs, send_sems, recv_sems, after, describe):
    n = len(bufs)

    def body(*refs):
        send, recv = refs[n], refs[n + 1]
        x, y, c = _position()
        for k, (src, _, dst, peer) in enumerate(describe(refs[:n], x, y, c)):
            copy = pltpu.make_async_remote_copy(src_ref=src, dst_ref=dst, send_sem=send.at[k], recv_sem=recv.at[k],
                                                device_id=peer, device_id_type=MESH)
            copy.wait_send()
            copy.wait_recv()

    out = pl.pallas_call(
        body, name=name, out_shape=[pltpu.HBM(b.shape, b.dtype) for b in bufs],
        in_specs=[HBM] * n + [SEM, SEM] + [ANY] * len(after), out_specs=[HBM] * n,
        input_output_aliases={k: k for k in range(n)},
        compiler_params=pltpu.CompilerParams(has_side_effects=EFFECT),
    )(*bufs, send_sems, recv_sems, *after)
    return list(out)


def _row_tile(rows, cols, max_elements=256 * 1024):
    best = 16
    for t in range(16, rows + 1, 16):
        if rows % t == 0 and t * cols <= max_elements:
            best = t
    return best


def cast_into_slot(name, param, index, chip):
    rows, cols = param.shape[-2:]
    tr = _row_tile(rows, cols)
    lead = (None,) * len(index)

    def body(chip_ref, s_ref, o_ref):
        del chip_ref
        o_ref[...] = s_ref[...].astype(BF16)

    return pl.pallas_call(
        body, out_shape=jax.ShapeDtypeStruct((N_CHIPS, rows, cols), BF16),
        grid_spec=pltpu.PrefetchScalarGridSpec(
            num_scalar_prefetch=1, grid=(rows // tr,),
            in_specs=[pl.BlockSpec(lead + (tr, cols), lambda r, chip_ref: index + (r, 0))],
            out_specs=pl.BlockSpec((None, tr, cols), lambda r, chip_ref: (chip_ref[0], r, 0))),
        name=name, compiler_params=_params(("parallel",)),
    )(chip, param)


GATHER_STAGES = ((0,), (1,), (2,), (3, 4, 5))
REDUCE_STAGES = ((5, 4, 3), (2,), (1,), (0,))


def _gather_copies(slots, x, y, c):
    me = 2 * x + y
    out = []
    for s in slots:
        for fx, fy in CHIP_FLIPS:
            mine = _half(s.at[me], 0, c)
            out.append((mine, mine, _half(s.at[2 * (x ^ fx) + (y ^ fy)], 0, c), (x ^ fx, y ^ fy, c)))
    return out


class WeightStream:
    def __init__(self, shards, chip, after=()):
        self.pending, self.ready = {}, {}
        token = tuple(after)
        for si, units in enumerate(GATHER_STAGES):
            slots = [cast_into_slot(f"cast_{u}_{t}", p, idx, chip) for u in units for t, (p, idx) in enumerate(shards[u])]
            send, recv, slots, tok = split_start(f"gather_start_{si}", slots, token, 3 * len(slots), _gather_copies)
            self.pending[si] = (send, recv, slots)
            token = (tok,)
        self.token = token

    def get(self, u, after):
        if u not in self.ready:
            si = next(k for k, units in enumerate(GATHER_STAGES) if u in units)
            send, recv, slots = self.pending.pop(si)
            slots = split_wait(f"gather_wait_{si}", slots, send, recv, (after,) + self.token, _gather_copies)
            self.token = ()

            @_plan(0, 3 * len(slots))
            def to_sibling(ins, outs, x, y, c):
                sends, recvs = [], []
                for o in outs:
                    for fx, fy in CHIP_FLIPS:
                        slab = o.at[2 * (x ^ fx) + (y ^ fy)]
                        sends.append((_half(slab, 0, c), _half(slab, 0, c), (x, y, 1 - c)))
                        recvs.append(_half(slab, 0, 1 - c))
                return [], sends, recvs

            shapes = [jax.ShapeDtypeStruct(s.shape, BF16) for s in slots]
            slots = exchange(f"gather_sibling_{si}", slots, shapes, {k: k for k in range(len(slots))}, to_sibling)
            for i, v in enumerate(GATHER_STAGES[si]):
                self.ready[v] = tuple(slots[3 * i:3 * i + 3])
        return self.ready[u]


def _reduce_copies(bufs, x, y, c):
    n = len(bufs) // 2
    out = []
    for s, land in zip(bufs[:n], bufs[n:]):
        for k, (fx, fy) in enumerate(CHIP_FLIPS):
            out.append((s.at[2 * (x ^ fx) + (y ^ fy)], land.at[k], land.at[k], (x ^ fx, y ^ fy, c)))
    return out


GRAD_SLOTS = {"gate": (2 * DEPTH, FF_SHARD, D_MODEL), "up": (2 * DEPTH, FF_SHARD, D_MODEL),
              "down": (2 * DEPTH, FF_SHARD, D_MODEL), "in": (DEPTH, D_MODEL, IN_SHARD),
              "br": (DEPTH, BR_ROWS, D_SHARD), "out": (DEPTH, D_SHARD, D_MODEL)}


def _unit_tensors(u):
    l, j = divmod(u, 3)
    if j == 1:
        return [("in", l), ("br", l), ("out", l)]
    return [(k, 2 * l + j // 2) for k in ("gate", "up", "down")]


class GradStream:
    def __init__(self, chip, core):
        self.core = core
        self.place = jnp.concatenate([chip, core])
        self.held, self.flying = {}, []
        self.full = {k: lax.empty(shape, F32) for k, shape in GRAD_SLOTS.items()}

    def put(self, u, grads, after=()):
        self.held[u] = grads
        si = len(self.flying)
        units = REDUCE_STAGES[si]
        if not all(v in self.held for v in units):
            return jnp.zeros((1, 1), F32)
        gs = [g for v in units for g in self.held[v]]

        @_plan(0, len(gs))
        def swap_halves(ins, outs, x, y, c):
            sends = [(_half(g, 1, 1 - c), o, (x, y, 1 - c)) for g, o in zip(ins, outs)]
            return [], sends, list(outs)

        half_shapes = [jax.ShapeDtypeStruct((N_CHIPS, g.shape[1] // 2, g.shape[2]), BF16) for g in gs]
        landed = exchange(f"reduce_swap_{si}", gs + list(after), half_shapes, {}, swap_halves)
        sums = [_add_halves(g, la, self.core) for g, la in zip(gs, landed)]
        landing = [lax.empty((3,) + s.shape[1:], BF16) for s in sums]
        send, recv, bufs, token = split_start(f"reduce_start_{si}", sums + landing, (), 3 * len(sums), _reduce_copies)
        self.flying.append((send, recv, bufs, [t for v in units for t in _unit_tensors(v)]))
        return token[0:1, 0:1]

    def finish(self, after):
        for si, (send, recv, bufs, tensors) in enumerate(self.flying):
            bufs = split_wait(f"reduce_wait_{si}", bufs, send, recv, tuple(after), _reduce_copies)
            n = len(tensors)
            for (name, slot), s, land in zip(tensors, bufs[:n], bufs[n:]):
                self.full[name] = _add_chips(s, land, self.place, self.full[name], slot)
        names = list(self.full)

        @_plan(0, len(names))
        def share_halves(ins, outs, x, y, c):
            sends = [(_half(o, 1, c), _half(o, 1, c), (x, y, 1 - c)) for o in outs]
            return [], sends, [_half(o, 1, 1 - c) for o in outs]

        shapes = [jax.ShapeDtypeStruct(self.full[k].shape, F32) for k in names]
        out = exchange("reduce_share_halves", [self.full[k] for k in names], shapes, {k: k for k in range(len(names))},
                       share_halves)
        return dict(zip(names, out))


def _add_halves(g, landed, core):
    _, rh, cols = landed.shape
    tr = _row_tile(rh, cols, 1024 * 1024)
    per_half = rh // tr

    def body(core_ref, g_ref, la_ref, o_ref):
        del core_ref
        o_ref[...] = (g_ref[...].astype(F32) + la_ref[...].astype(F32)).astype(BF16)

    blk = (None, tr, cols)
    return pl.pallas_call(
        body, out_shape=jax.ShapeDtypeStruct(landed.shape, BF16),
        grid_spec=pltpu.PrefetchScalarGridSpec(
            num_scalar_prefetch=1, grid=(N_CHIPS, per_half),
            in_specs=[pl.BlockSpec(blk, lambda j, r, core_ref: (j, core_ref[0] * per_half + r, 0)),
                      pl.BlockSpec(blk, lambda j, r, core_ref: (j, r, 0))],
            out_specs=pl.BlockSpec(blk, lambda j, r, core_ref: (j, r, 0))),
        name="reduce_add_halves", compiler_params=_params(("parallel", "parallel")),
    )(core, g, landed)


def _add_chips(sums, landed, place, full, slot):
    _, rh, cols = sums.shape
    tr = _row_tile(rh, cols, 1024 * 1024)
    per_half = rh // tr

    def body(place_ref, s_ref, la_ref, full_in, o_ref):
        del place_ref, full_in
        o_ref[...] = ((s_ref[...].astype(F32) + la_ref[0].astype(F32)) + la_ref[1].astype(F32)) + la_ref[2].astype(F32)

    return pl.pallas_call(
        body, out_shape=jax.ShapeDtypeStruct(full.shape, F32),
        grid_spec=pltpu.PrefetchScalarGridSpec(
            num_scalar_prefetch=1, grid=(per_half,),
            in_specs=[pl.BlockSpec((None, tr, cols), lambda r, place_ref: (place_ref[0], r, 0)),
                      pl.BlockSpec((3, tr, cols), lambda r, place_ref: (0, r, 0)), ANY],
            out_specs=pl.BlockSpec((None, tr, cols), lambda r, place_ref: (slot, place_ref[1] * per_half + r, 0))),
        input_output_aliases={3: 0}, name="reduce_add_chips", compiler_params=_params(("parallel",)),
    )(place, sums, landed, full)


def sum_devices(parts):
    def body(p_ref, o_ref):
        acc = p_ref[0]
        for d in range(1, N_DEV):
            acc = acc + p_ref[d]
        o_ref[...] = acc

    return pl.pallas_call(body, out_shape=jax.ShapeDtypeStruct(parts.shape[1:], F32), name="sum_devices")(parts)


ADA_SHARD = 9 * D_MODEL // N_CHIPS
ADA_TILE = 768
ADA_ROWS = 16


def ada_forward(c_rows, w_ada, b_shard):
    def body(c_ref, w_ref, b_ref, o_ref):
        cv = c_ref[...]
        o_ref[...] = _dot((cv * _sigmoid(cv)).astype(BF16), w_ref[...].astype(BF16), NN) + b_ref[...]

    return pl.pallas_call(
        body, grid=(DEPTH, ADA_SHARD // ADA_TILE),
        in_specs=[pl.BlockSpec((ADA_ROWS, D_MODEL), lambda l, n: (0, 0)),
                  pl.BlockSpec((None, D_MODEL, ADA_TILE), lambda l, n: (l, 0, n)),
                  pl.BlockSpec((None, 1, ADA_TILE), lambda l, n: (l, 0, n))],
        out_specs=pl.BlockSpec((None, ADA_ROWS, ADA_TILE), lambda l, n: (l, 0, n)),
        out_shape=jax.ShapeDtypeStruct((DEPTH, ADA_ROWS, ADA_SHARD), F32),
        name="ada_forward", compiler_params=_params(("parallel", "parallel")),
    )(c_rows, w_ada, b_shard)


def ada_backward(c_rows, dmod_rows):
    def body(c_ref, d_ref, o_ref):
        cv = c_ref[...]
        o_ref[...] = _dot((cv * _sigmoid(cv)).astype(BF16), d_ref[...].astype(BF16), TN)

    return pl.pallas_call(
        body, grid=(DEPTH, ADA_SHARD // ADA_TILE),
        in_specs=[pl.BlockSpec((ADA_ROWS, D_MODEL), lambda l, n: (0, 0)),
                  pl.BlockSpec((None, ADA_ROWS, ADA_TILE), lambda l, n: (l, 0, n))],
        out_specs=pl.BlockSpec((None, D_MODEL, ADA_TILE), lambda l, n: (l, 0, n)),
        out_shape=jax.ShapeDtypeStruct((DEPTH, D_MODEL, ADA_SHARD), F32),
        name="ada_backward", compiler_params=_params(("parallel", "parallel")),
    )(c_rows, dmod_rows)


def adamw(name, w, g, m, v):
    shape = w.shape
    cols = shape[-1]
    rows = w.size // cols
    tr = _row_tile(rows, cols) if rows % 16 == 0 else rows
    c1 = 1.0 / (1.0 - ADAM_B1 ** ADAM_STEP)
    c2 = 1.0 / (1.0 - ADAM_B2 ** ADAM_STEP)

    def body(w_ref, g_ref, m_ref, v_ref, go_ref, d_ref, mo_ref, vo_ref):
        gv = g_ref[...]
        mn = ADAM_B1 * m_ref[...] + (1.0 - ADAM_B1) * gv
        vn = ADAM_B2 * v_ref[...] + (1.0 - ADAM_B2) * (gv * gv)
        go_ref[...] = gv
        mo_ref[...] = mn
        vo_ref[...] = vn
        d_ref[...] = -ADAM_LR * ((mn * c1) / (jnp.sqrt(vn * c2) + ADAM_EPS) + ADAM_WD * w_ref[...])

    spec = pl.BlockSpec((tr, cols), lambda i: (i, 0))
    out = jax.ShapeDtypeStruct((rows, cols), F32)
    res = pl.pallas_call(
        body, grid=(rows // tr,), in_specs=[spec] * 4, out_specs=[spec] * 4, out_shape=[out] * 4,
        name=name, compiler_params=_params(("parallel",)),
    )(*[t.reshape(rows, cols) for t in (w, g, m, v)])
    return tuple(r.reshape(shape) for r in res)


def _pack(parts, rows):
    flat = jnp.concatenate([p.reshape(-1) for p in parts])
    return jnp.pad(flat, (0, rows * 128 - flat.size)).reshape(rows, 128)


def _unpack(flat, shapes):
    out, at = [], 0
    for s in shapes:
        n = math.prod(s)
        out.append(flat[at:at + n].reshape(s))
        at += n
    return out


def kernel(x, c, w_ada, b_ada, norm_gain, w_ffn_gate, w_ffn_up, w_ffn_down, w_in, w_br_sb, w_br_dil, w_br_swa, w_out, sinks, rel_bias, final_gain, loss_target, m_w_ada, m_b_ada, m_norm_gain, m_w_ffn_gate, m_w_ffn_up, m_w_ffn_down, m_w_in, m_w_br_sb, m_w_br_dil, m_w_br_swa, m_w_out, m_sinks, m_rel_bias, m_final_gain, v_w_ada, v_b_ada, v_norm_gain, v_w_ffn_gate, v_w_ffn_up, v_w_ffn_down, v_w_in, v_w_br_sb, v_w_br_dil, v_w_br_swa, v_w_out, v_sinks, v_rel_bias, v_final_gain):
    xi, yi, ci = _position()
    chip = 2 * xi + yi
    dev = 2 * chip + ci

    c_all = all_gather_small("gather_c", c.reshape(8, 128)).reshape(N_DEV, D_MODEL)
    c_rows = jnp.pad(c_all, ((0, ADA_ROWS - N_DEV), (0, 0)))
    b_shard = lax.dynamic_slice_in_dim(b_ada, chip * ADA_SHARD, ADA_SHARD, axis=1).reshape(DEPTH, 1, ADA_SHARD)
    mod_shard = ada_forward(c_rows, w_ada, b_shard)[:, :N_DEV]
    n_mod = DEPTH * N_DEV * ADA_SHARD
    gathered = all_gather_small("gather_mod", _pack([mod_shard, norm_gain], 304))[::2].reshape(N_CHIPS, -1)
    mod_all = gathered[:, :n_mod].reshape(N_CHIPS, DEPTH, N_DEV, ADA_SHARD)
    mod = lax.dynamic_index_in_dim(mod_all, dev, axis=2, keepdims=False)
    mod = mod.transpose(1, 0, 2).reshape(DEPTH, 3, 3, D_MODEL)
    gains = gathered[:, n_mod:n_mod + DEPTH * 3 * D_SHARD].reshape(N_CHIPS, DEPTH, 3, D_SHARD)
    gains = gains.transpose(1, 2, 0, 3).reshape(DEPTH, 3, D_MODEL)

    chip_i, core_i = chip.astype(jnp.int32).reshape(1), ci.astype(jnp.int32).reshape(1)
    w_br = jnp.concatenate([w_br_sb, w_br_dil, w_br_swa], axis=1)
    transposed = (3, 4)
    w_gate_t, w_up_t = jnp.swapaxes(w_ffn_gate, 2, 3), jnp.swapaxes(w_ffn_up, 2, 3)
    shards = []
    for l in range(DEPTH):
        ffn = [[(w_gate_t, (l, f)), (w_up_t, (l, f)), (w_ffn_down, (l, f))] for f in range(2)]
        shards += [ffn[0], [(w_in, (l,)), (w_br, (l,)), (w_out, (l,))], ffn[1]]
    weights_in = WeightStream(shards, chip_i, (gathered,))
    grads_out = GradStream(chip_i, core_i)

    loss, dx, dmod, dgains, dfinal, dsinks, drel, last_grads = device_step(
        x[0], loss_target[0], mod, gains, final_gain, sinks, rel_bias, weights_in.get, grads_out.put)

    small_shapes = [(DEPTH, 9 * D_MODEL), (DEPTH, 3, D_MODEL), (D_MODEL,), (DEPTH, H_SWA_Q), (N_BUCKETS, 12), (1,)]
    small_all = all_gather_small("gather_small_grads", _pack([dmod, dgains, dfinal, dsinks, drel, loss[0, 0:1]], 208))
    started = grads_out.put(0, last_grads, after=(small_all,))
    small_all = small_all + started
    g_b_ada, g_gain_full, g_final, g_sinks, g_rel, loss_sum = _unpack(sum_devices(small_all).reshape(-1), small_shapes)
    g_gain = lax.dynamic_slice_in_dim(g_gain_full, chip * D_SHARD, D_SHARD, axis=2)
    dmod_all = small_all.reshape(N_DEV, -1)[:, :DEPTH * 9 * D_MODEL].reshape(N_DEV, DEPTH, 9 * D_MODEL)
    dmod_rows = lax.dynamic_slice_in_dim(dmod_all, chip * ADA_SHARD, ADA_SHARD, axis=2).transpose(1, 0, 2)
    g_w_ada = ada_backward(c_rows, jnp.pad(dmod_rows, ((0, 0), (0, ADA_ROWS - N_DEV), (0, 0))))

    weights = [w_ada, b_ada, norm_gain, w_ffn_gate, w_ffn_up, w_ffn_down, w_in, w_br_sb, w_br_dil, w_br_swa, w_out,
               sinks, rel_bias, final_gain]
    ms = [m_w_ada, m_b_ada, m_norm_gain, m_w_ffn_gate, m_w_ffn_up, m_w_ffn_down, m_w_in, m_w_br_sb, m_w_br_dil,
          m_w_br_swa, m_w_out, m_sinks, m_rel_bias, m_final_gain]
    vs = [v_w_ada, v_b_ada, v_norm_gain, v_w_ffn_gate, v_w_ffn_up, v_w_ffn_down, v_w_in, v_w_br_sb, v_w_br_dil,
          v_w_br_swa, v_w_out, v_sinks, v_rel_bias, v_final_gain]
    grads = [g_w_ada, g_b_ada, g_gain] + [None] * 8 + [g_sinks, g_rel, g_final]

    small = (1, 2, 11, 12, 13)
    deltas, new_ms, new_vs = [None] * 14, [None] * 14, [None] * 14
    _, deltas[0], new_ms[0], new_vs[0] = adamw("adamw_0", weights[0], grads[0], ms[0], vs[0])
    shapes = [weights[k].shape for k in small]
    packed = [_pack([t[k] for k in small], 168) for t in (weights, grads, ms, vs)]
    for dst, res in zip((deltas, new_ms, new_vs), adamw("adamw_small", *packed)[1:]):
        for k, t in zip(small, _unpack(res.reshape(-1), shapes)):
            dst[k] = t

    g = grads_out.finish((dx, deltas[0], deltas[1]))
    g_br = g["br"]
    grads[3:11] = [g["gate"].reshape(w_gate_t.shape), g["up"].reshape(w_up_t.shape),
                   g["down"].reshape(w_ffn_down.shape), g["in"], g_br[:, 0:256], g_br[:, 256:384], g_br[:, 384:768],
                   g["out"]]
    for k in range(3, 11):
        state = [weights[k], ms[k], vs[k]]
        if k in transposed:
            state = [jnp.swapaxes(t, 2, 3) for t in state]
        out = adamw(f"adamw_{k}", state[0], grads[k], state[1], state[2])
        if k in transposed:
            out = [jnp.swapaxes(t, 2, 3) for t in out]
        grads[k], deltas[k], new_ms[k], new_vs[k] = out
    return (loss_sum[0], dx[None], *grads, *deltas, *new_ms, *new_vs)
```

```python
import functools
import math

import jax
import jax.numpy as jnp
from jax import lax
from jax.experimental import pallas as pl
from jax.experimental.pallas import tpu as pltpu

F32 = jnp.float32
BF16 = jnp.bfloat16

D_MODEL = 1024
SEQ = 2048
DEPTH = 2
HEAD_DIM = 64
BLK = 128
H_SB = 4
DIL_PATTERNS = ((128, 1), (512, 4), (2048, 16))
H_PER_DIL = 2
H_DIL = 6
H_SWA_Q = 6
H_SWA_KV = 2
SWA_WINDOW = 128
N_BUCKETS = 32
MAX_REL_DIST = 2048
D_FF = 2816
RMS_EPS = 1e-6
N_CHIPS = 4
N_DEV = 8
FF_SHARD = D_FF // N_CHIPS
D_QKV = 2560
D_IN = D_QKV + 3 * D_MODEL
IN_SHARD = D_IN // N_CHIPS
D_SHARD = D_MODEL // N_CHIPS
BR_ROWS = 768
NEG = -1e30
QK_SCALE = HEAD_DIM ** -0.5

ADAM_LR = 0.001
ADAM_B1 = 0.9
ADAM_B2 = 0.999
ADAM_EPS = 1e-08
ADAM_WD = 0.01
ADAM_STEP = 10

VMEM_LIMIT = 56 * 1024 * 1024
ROW_TILE = 256
MM_TILE = 1024

NN = (((1,), (0,)), ((), ()))
NT = (((1,), (1,)), ((), ()))
TN = (((0,), (0,)), ((), ()))


def _params(sem=None):
    return pltpu.CompilerParams(dimension_semantics=sem, vmem_limit_bytes=VMEM_LIMIT)


def _dot(a, b, dims):
    return lax.dot_general(a, b, dims, preferred_element_type=F32)


def _sigmoid(x):
    return 1.0 / (1.0 + jnp.exp(-x))


def _matmul(name, grid, nk, k_axis, dims, n_pairs, in_specs, out_specs, out_shape, acc_shape, epilogue,
            operands, sem, aliases=None, prologue=None):
    n_in = len(in_specs)
    n_out = len(out_specs)

    def partial(ins):
        tot = None
        for p in range(n_pairs):
            a = ins[2 * p][...]
            if prologue is not None:
                a = prologue(p, a, ins)
            d = _dot(a, ins[2 * p + 1][...], dims)
            tot = d if tot is None else tot + d
        return tot

    def body(*refs):
        ins, outs = refs[:n_in], refs[n_in:n_in + n_out]
        ids = tuple(pl.program_id(a) for a in range(len(grid)))
        if nk == 1:
            epilogue(partial(ins), ins, outs, ids)
            return
        acc = refs[n_in + n_out]
        k = ids[k_axis]

        @pl.when(k == 0)
        def _():
            acc[...] = partial(ins)

        @pl.when(k > 0)
        def _():
            acc[...] += partial(ins)

        @pl.when(k == nk - 1)
        def _():
            epilogue(acc[...], ins, outs, ids)

    return pl.pallas_call(
        body, grid=grid, in_specs=in_specs, out_specs=out_specs, out_shape=out_shape,
        scratch_shapes=[] if nk == 1 else [pltpu.VMEM(acc_shape, F32)],
        input_output_aliases=aliases or {}, name=name, compiler_params=_params(sem),
    )(*operands)


def _row_spec(width=D_MODEL):
    return pl.BlockSpec((ROW_TILE, width), lambda i: (i, 0))


def _vec_spec(rows=1, width=D_MODEL):
    return pl.BlockSpec((rows, width), lambda i: (0, 0))


class Row:
    def __init__(self, table, index):
        self.table, self.index = table, index

    def spec(self):
        index = self.index
        return pl.BlockSpec((None, 1, D_MODEL), lambda *ids: (index, 0, 0))


def _slot_spec(u):
    return pl.BlockSpec((8, D_MODEL), lambda *ids: (u, 0))


def prenorm(x, gain, scale, shift):
    def body(x_ref, g_ref, sc_ref, sh_ref, h_ref):
        xv = x_ref[...]
        r = lax.rsqrt(jnp.mean(xv * xv, axis=-1, keepdims=True) + RMS_EPS)
        h_ref[...] = (((xv * r) * g_ref[...]) * (1.0 + sc_ref[...]) + sh_ref[...]).astype(BF16)

    return pl.pallas_call(
        body, grid=(SEQ // ROW_TILE,), in_specs=[_row_spec(), gain.spec(), scale.spec(), shift.spec()],
        out_specs=_row_spec(), out_shape=jax.ShapeDtypeStruct((SEQ, D_MODEL), BF16),
        name="prenorm", compiler_params=_params(("parallel",)),
    )(x, gain.table, scale.table, shift.table)


def resid_bwd(dxo, f, coef, mult, sums, u):
    def body(dx_ref, f_ref, c_ref, sums_in, df_ref, dc_ref):
        del sums_in
        dx = dx_ref[...]
        df_ref[...] = (dx * (mult * c_ref[...])).astype(BF16)
        part = mult * jnp.sum(dx * f_ref[...], axis=0, keepdims=True)

        @pl.when(pl.program_id(0) == 0)
        def _():
            dc_ref[...] = jnp.zeros_like(dc_ref)

        dc_ref[0:1, :] += part

    return pl.pallas_call(
        body, grid=(SEQ // ROW_TILE,),
        in_specs=[_row_spec(), _row_spec(), coef.spec(), pl.BlockSpec(memory_space=pl.ANY)],
        out_specs=[_row_spec(), _slot_spec(u)],
        out_shape=[jax.ShapeDtypeStruct((SEQ, D_MODEL), BF16), jax.ShapeDtypeStruct(sums.shape, F32)],
        input_output_aliases={3: 1}, name="resid_bwd", compiler_params=_params(("arbitrary",)),
    )(dxo, f, coef.table, sums)


def final_loss(x, gain, target):
    def body(x_ref, g_ref, t_ref, loss_ref, dx_ref, dg_ref):
        xv = x_ref[...]
        g = g_ref[...]
        r = lax.rsqrt(jnp.mean(xv * xv, axis=-1, keepdims=True) + RMS_EPS)
        xh = xv * r
        e = xh * g - t_ref[...]
        part = 0.5 * jnp.sum(jnp.mean(e * e, axis=-1, keepdims=True), axis=0, keepdims=True)
        dy = e * (1.0 / D_MODEL)
        dyg = dy * g
        dx_ref[...] = r * (dyg - xh * jnp.mean(dyg * xh, axis=-1, keepdims=True))

        @pl.when(pl.program_id(0) == 0)
        def _():
            loss_ref[...] = jnp.zeros_like(loss_ref)
            dg_ref[...] = jnp.zeros_like(dg_ref)

        loss_ref[...] += jnp.broadcast_to(part, loss_ref.shape)
        dg_ref[0:1, :] += jnp.sum(dy * xh, axis=0, keepdims=True)

    return pl.pallas_call(
        body, grid=(SEQ // ROW_TILE,), in_specs=[_row_spec(), _vec_spec(), _row_spec()],
        out_specs=[_vec_spec(8, 128), _row_spec(), _vec_spec(8)],
        out_shape=[jax.ShapeDtypeStruct((8, 128), F32), jax.ShapeDtypeStruct((SEQ, D_MODEL), F32),
                   jax.ShapeDtypeStruct((8, D_MODEL), F32)],
        name="final_loss", compiler_params=_params(("arbitrary",)),
    )(x, gain, target)


def _prenorm_bwd_epilogue(dh, x_ref, dxo_ref, g_ref, sc_ref, dx_ref, stats_ref, first):
    xv = x_ref[...]
    g = g_ref[...]
    r = lax.rsqrt(jnp.mean(xv * xv, axis=-1, keepdims=True) + RMS_EPS)
    xh = xv * r
    dn = dh * (1.0 + sc_ref[...])
    dxh = dn * g
    dx_ref[...] = dxo_ref[...] + r * (dxh - xh * jnp.mean(dxh * xh, axis=-1, keepdims=True))

    @pl.when(first)
    def _():
        stats_ref[...] = jnp.zeros_like(stats_ref)

    stats_ref[0:1, :] += jnp.sum(dh, axis=0, keepdims=True)
    stats_ref[1:2, :] += jnp.sum(dh * (xh * g), axis=0, keepdims=True)
    stats_ref[2:3, :] += jnp.sum(dn * xh, axis=0, keepdims=True)


def ffn_up(h, wg_all, wu_all):
    def body(h_ref, wg_ref, wu_ref, a_ref, b_ref, s_ref):
        hv = h_ref[...]
        a = _dot(hv, wg_ref[...], NT)
        b = _dot(hv, wu_ref[...], NT)
        a_ref[...] = a.astype(BF16)
        b_ref[...] = b.astype(BF16)
        s_ref[...] = (a * _sigmoid(a) * b).astype(BF16)

    w_spec = pl.BlockSpec((None, FF_SHARD, D_MODEL), lambda j, i: (j, 0, 0))
    o_spec = pl.BlockSpec((None, MM_TILE, FF_SHARD), lambda j, i: (j, i, 0))
    hid = (N_CHIPS, SEQ, FF_SHARD)
    return pl.pallas_call(
        body, grid=(N_CHIPS, SEQ // MM_TILE),
        in_specs=[pl.BlockSpec((MM_TILE, D_MODEL), lambda j, i: (i, 0)), w_spec, w_spec],
        out_specs=[o_spec, o_spec, o_spec],
        out_shape=[jax.ShapeDtypeStruct(hid, BF16)] * 3,
        name="ffn_up", compiler_params=_params(("parallel", "parallel")),
    )(h, wg_all, wu_all)


def matmul_residual(name, a, a_spec, w_all, w_spec, x, coef, mult):
    def epilogue(acc, ins, outs, ids):
        outs[0][...] = acc
        outs[1][...] = ins[2][...] + (mult * ins[3][...]) * acc

    row = pl.BlockSpec((MM_TILE, D_MODEL), lambda i, j: (i, 0))
    return _matmul(
        name, (SEQ // MM_TILE, N_CHIPS), N_CHIPS, 1, NN, 1,
        [a_spec, w_spec, row, coef.spec()], [row, row],
        [jax.ShapeDtypeStruct((SEQ, D_MODEL), F32)] * 2, (MM_TILE, D_MODEL), epilogue,
        (a, w_all, x, coef.table), ("parallel", "arbitrary"))


def ffn_down(s, wd_all, x, gate):
    return matmul_residual(
        "ffn_down", s, pl.BlockSpec((None, MM_TILE, FF_SHARD), lambda i, j: (j, i, 0)),
        wd_all, pl.BlockSpec((None, FF_SHARD, D_MODEL), lambda i, j: (j, 0, 0)), x, gate, 0.5)


def ffn_bwd_hidden(df, wd_all, a, b):
    def epilogue(ds, ins, outs, ids):
        av, bv = ins[2][...].astype(F32), ins[3][...].astype(F32)
        sig = _sigmoid(av)
        outs[0][...] = (ds * bv * (sig * (1.0 + av * (1.0 - sig)))).astype(BF16)
        outs[1][...] = (ds * (av * sig)).astype(BF16)

    hid_spec = pl.BlockSpec((None, MM_TILE, FF_SHARD), lambda j, i: (j, i, 0))
    hid = jax.ShapeDtypeStruct((N_CHIPS, SEQ, FF_SHARD), BF16)
    return _matmul(
        "ffn_bwd_hidden", (N_CHIPS, SEQ // MM_TILE), 1, None, NT, 1,
        [pl.BlockSpec((MM_TILE, D_MODEL), lambda j, i: (i, 0)),
         pl.BlockSpec((None, FF_SHARD, D_MODEL), lambda j, i: (j, 0, 0)), hid_spec, hid_spec],
        [hid_spec, hid_spec], [hid, hid], None, epilogue, (df, wd_all, a, b), ("parallel", "parallel"))


def grad_weight(name, lhs, lhs_spec, rhs, rhs_spec, shape):
    def epilogue(acc, ins, outs, ids):
        outs[0][...] = acc.astype(BF16)

    return _matmul(
        name, (N_CHIPS, SEQ // MM_TILE), SEQ // MM_TILE, 1, TN, 1,
        [lhs_spec, rhs_spec], [pl.BlockSpec((None,) + shape, lambda j, k: (j, 0, 0))],
        [jax.ShapeDtypeStruct((N_CHIPS,) + shape, BF16)], shape, epilogue, (lhs, rhs), ("parallel", "arbitrary"))[0]


def ffn_grad_weights(h, s, df, da, db):
    tok = pl.BlockSpec((MM_TILE, D_MODEL), lambda j, k: (k, 0))
    hid = pl.BlockSpec((None, MM_TILE, FF_SHARD), lambda j, k: (j, k, 0))
    return (grad_weight("grad_w_gate", da, hid, h, tok, (FF_SHARD, D_MODEL)),
            grad_weight("grad_w_up", db, hid, h, tok, (FF_SHARD, D_MODEL)),
            grad_weight("grad_w_down", s, hid, df, tok, (FF_SHARD, D_MODEL)))


def matmul_prenorm_bwd(name, dims, pairs, pair_specs, x, dxo, gain, scale, sums, u):
    n = len(pairs)

    def epilogue(dh, ins, outs, ids):
        _prenorm_bwd_epilogue(dh, ins[n], ins[n + 1], ins[n + 2], ins[n + 3], outs[0], outs[1], ids[0] == 0)

    row = pl.BlockSpec((MM_TILE, D_MODEL), lambda i, j: (i, 0))
    return _matmul(
        name, (SEQ // MM_TILE, N_CHIPS), N_CHIPS, 1, dims, len(pairs) // 2,
        list(pair_specs) + [row, row, gain.spec(), scale.spec(), pl.BlockSpec(memory_space=pl.ANY)],
        [row, _slot_spec(u)], [jax.ShapeDtypeStruct((SEQ, D_MODEL), F32), jax.ShapeDtypeStruct(sums.shape, F32)],
        (MM_TILE, D_MODEL), epilogue, tuple(pairs) + (x, dxo, gain.table, scale.table, sums),
        ("arbitrary", "arbitrary"), aliases={n + 4: 1})


def ffn_bwd_input(da, db, wg_all, wu_all, x, dxo, gain, scale, sums, u):
    hid = pl.BlockSpec((None, MM_TILE, FF_SHARD), lambda i, j: (j, i, 0))
    w = pl.BlockSpec((None, FF_SHARD, D_MODEL), lambda i, j: (j, 0, 0))
    return matmul_prenorm_bwd("ffn_bwd_input", NN, (da, wg_all, db, wu_all), (hid, w, hid, w), x, dxo, gain, scale,
                              sums, u)


def in_proj(h, w_all):
    def epilogue(acc, ins, outs, ids):
        outs[0][...] = acc
        outs[1][...] = acc.astype(BF16)

    out = pl.BlockSpec((MM_TILE, IN_SHARD), lambda j, i: (i, j))
    return _matmul(
        "in_proj", (N_CHIPS, SEQ // MM_TILE), 1, None, NN, 1,
        [pl.BlockSpec((MM_TILE, D_MODEL), lambda j, i: (i, 0)),
         pl.BlockSpec((None, D_MODEL, IN_SHARD), lambda j, i: (j, 0, 0))],
        [out, out], [jax.ShapeDtypeStruct((SEQ, D_IN), F32), jax.ShapeDtypeStruct((SEQ, D_IN), BF16)],
        None, epilogue, (h, w_all), ("parallel", "parallel"))


_GATE_BLOCK0 = D_QKV // D_SHARD


def _branch_products(o, w_ref):
    ob = o.astype(BF16)
    return (_dot(ob[:, 0:256], w_ref[0:256, :], NN), _dot(ob[:, 256:384], w_ref[256:384, :], NN),
            _dot(ob[:, 384:768], w_ref[384:768, :], NN))


def merge_branches(o_cat, wbr_all, proj):
    def body(o_ref, w_ref, g0_ref, g1_ref, g2_ref, m_ref):
        u = _branch_products(o_ref[...], w_ref)
        m_ref[...] = (_sigmoid(g0_ref[...]) * u[0] + _sigmoid(g1_ref[...]) * u[1]
                      + _sigmoid(g2_ref[...]) * u[2]).astype(BF16)

    def gate_spec(b):
        return pl.BlockSpec((MM_TILE, D_SHARD), lambda i, j: (i, _GATE_BLOCK0 + 4 * b + j))

    return pl.pallas_call(
        body, grid=(SEQ // MM_TILE, N_CHIPS),
        in_specs=[pl.BlockSpec((MM_TILE, BR_ROWS), lambda i, j: (i, 0)),
                  pl.BlockSpec((None, BR_ROWS, D_SHARD), lambda i, j: (j, 0, 0)),
                  gate_spec(0), gate_spec(1), gate_spec(2)],
        out_specs=pl.BlockSpec((MM_TILE, D_SHARD), lambda i, j: (i, j)),
        out_shape=jax.ShapeDtypeStruct((SEQ, D_MODEL), BF16),
        name="merge_branches", compiler_params=_params(("parallel", "parallel")),
    )(o_cat, wbr_all, proj, proj, proj)


def out_proj(merged, wout_all, x, gate):
    return matmul_residual(
        "out_proj", merged, pl.BlockSpec((MM_TILE, D_SHARD), lambda i, j: (i, j)),
        wout_all, pl.BlockSpec((None, D_SHARD, D_MODEL), lambda i, j: (j, 0, 0)), x, gate, 1.0)


def merge_bwd(dmo, wout_all, o_cat, wbr_all, proj):
    def epilogue(dm, ins, outs, ids):
        u = _branch_products(ins[2][...], ins[3])
        for b in range(3):
            sig = _sigmoid(ins[4 + b][...])
            outs[b][...] = (dm * sig).astype(BF16)
            outs[3 + b][...] = (dm * u[b] * (sig * (1.0 - sig))).astype(BF16)

    def gate_spec(b):
        return pl.BlockSpec((MM_TILE, D_SHARD), lambda j, i: (i, _GATE_BLOCK0 + 4 * b + j))

    col = pl.BlockSpec((MM_TILE, D_SHARD), lambda j, i: (i, j))
    du = jax.ShapeDtypeStruct((SEQ, D_MODEL), BF16)
    return _matmul(
        "merge_bwd", (N_CHIPS, SEQ // MM_TILE), 1, None, NT, 1,
        [pl.BlockSpec((MM_TILE, D_MODEL), lambda j, i: (i, 0)),
         pl.BlockSpec((None, D_SHARD, D_MODEL), lambda j, i: (j, 0, 0)),
         pl.BlockSpec((MM_TILE, BR_ROWS), lambda j, i: (i, 0)),
         pl.BlockSpec((None, BR_ROWS, D_SHARD), lambda j, i: (j, 0, 0)),
         gate_spec(0), gate_spec(1), gate_spec(2)],
        [col] * 6, [du] * 6,
        None, epilogue, (dmo, wout_all, o_cat, wbr_all, proj, proj, proj), ("parallel", "parallel"))


def branch_bwd_input(du, wbr_all):
    def body(d0_ref, d1_ref, d2_ref, w_ref, o_ref, acc):
        j = pl.program_id(1)
        parts = (_dot(d0_ref[...], w_ref[0:256, :], NT), _dot(d1_ref[...], w_ref[256:384, :], NT),
                 _dot(d2_ref[...], w_ref[384:768, :], NT))

        @pl.when(j == 0)
        def _():
            acc[:, 0:256], acc[:, 256:384], acc[:, 384:768] = parts

        @pl.when(j > 0)
        def _():
            acc[:, 0:256] += parts[0]
            acc[:, 256:384] += parts[1]
            acc[:, 384:768] += parts[2]

        @pl.when(j == N_CHIPS - 1)
        def _():
            o_ref[...] = acc[...]

    col = pl.BlockSpec((MM_TILE, D_SHARD), lambda i, j: (i, j))
    return pl.pallas_call(
        body, grid=(SEQ // MM_TILE, N_CHIPS),
        in_specs=[col, col, col, pl.BlockSpec((None, BR_ROWS, D_SHARD), lambda i, j: (j, 0, 0))],
        out_specs=pl.BlockSpec((MM_TILE, BR_ROWS), lambda i, j: (i, 0)),
        out_shape=jax.ShapeDtypeStruct((SEQ, BR_ROWS), F32),
        scratch_shapes=[pltpu.VMEM((MM_TILE, BR_ROWS), F32)],
        name="branch_bwd_input", compiler_params=_params(("parallel", "arbitrary")),
    )(du[0], du[1], du[2], wbr_all)


def branch_grad_weights(o_cat, du):
    def body(o_ref, d0_ref, d1_ref, d2_ref, g_ref, acc):
        k = pl.program_id(1)
        ob = o_ref[...].astype(BF16)
        parts = (_dot(ob[:, 0:256], d0_ref[...], TN), _dot(ob[:, 256:384], d1_ref[...], TN),
                 _dot(ob[:, 384:768], d2_ref[...], TN))

        @pl.when(k == 0)
        def _():
            acc[0:256, :], acc[256:384, :], acc[384:768, :] = parts

        @pl.when(k > 0)
        def _():
            acc[0:256, :] += parts[0]
            acc[256:384, :] += parts[1]
            acc[384:768, :] += parts[2]

        @pl.when(k == SEQ // MM_TILE - 1)
        def _():
            g_ref[...] = acc[...].astype(BF16)

    col = pl.BlockSpec((MM_TILE, D_SHARD), lambda j, k: (k, j))
    return pl.pallas_call(
        body, grid=(N_CHIPS, SEQ // MM_TILE),
        in_specs=[pl.BlockSpec((MM_TILE, BR_ROWS), lambda j, k: (k, 0)), col, col, col],
        out_specs=pl.BlockSpec((None, BR_ROWS, D_SHARD), lambda j, k: (j, 0, 0)),
        out_shape=jax.ShapeDtypeStruct((N_CHIPS, BR_ROWS, D_SHARD), BF16),
        scratch_shapes=[pltpu.VMEM((BR_ROWS, D_SHARD), F32)],
        name="branch_grad_weights", compiler_params=_params(("parallel", "arbitrary")),
    )(o_cat, du[0], du[1], du[2])


def mixer_bwd_input(dproj, win_all, x, dxo, gain, scale, sums, u):
    return matmul_prenorm_bwd(
        "mixer_bwd_input", NT, (dproj, win_all),
        (pl.BlockSpec((MM_TILE, IN_SHARD), lambda i, j: (i, j)),
         pl.BlockSpec((None, D_MODEL, IN_SHARD), lambda i, j: (j, 0, 0))), x, dxo, gain, scale, sums, u)


BATCH_QK = (((2,), (2,)), ((0,), (0,)))
BATCH_PV = (((2,), (1,)), ((0,), (0,)))
BATCH_TN = (((1,), (1,)), ((0,), (0,)))


SB_WIDTH = H_SB * HEAD_DIM
SB_ROWS = H_SB * BLK


def _split_dot(v, tri):
    hi = v.astype(BF16)
    lo = (v - hi.astype(F32)).astype(BF16)
    return _dot(hi, tri, NN) + _dot(lo, tri, NN)


def _tri(cmp):
    return cmp(lax.broadcasted_iota(jnp.int32, (BLK, BLK), 0), lax.broadcasted_iota(jnp.int32, (BLK, BLK), 1)).astype(BF16)


def _head_masks():
    lane = lax.broadcasted_iota(jnp.int32, (1, SB_WIDTH), 1) // HEAD_DIM
    return [lane == h for h in range(H_SB)]


def _stack_heads(x, masks):
    return jnp.concatenate([jnp.where(m, x, jnp.zeros_like(x)) for m in masks], axis=0)


def _merge_heads(y, masks):
    out = jnp.where(masks[0], y[0:BLK], 0.0)
    for h in range(1, H_SB):
        out = jnp.where(masks[h], y[h * BLK:(h + 1) * BLK], out)
    return out


def _sb_scores(q4, k_ref, j, diagonal):
    rows = pl.ds(pl.multiple_of(j * BLK, BLK), BLK)
    z = _dot(q4, k_ref[rows, :], NT)
    log_fail = -(jnp.maximum(z, 0.0) + jnp.log(1.0 + jnp.exp(-jnp.abs(z))))
    log_hit = z + log_fail
    before = None
    if diagonal:
        tile = (SB_ROWS, BLK)
        before = lax.broadcasted_iota(jnp.int32, tile, 1) < (lax.broadcasted_iota(jnp.int32, tile, 0) & (BLK - 1))
        log_fail = jnp.where(before, log_fail, 0.0)
    return rows, before, log_fail, log_hit


def _keep(before, x):
    return x if before is None else jnp.where(before, x, 0.0)


def sb_forward(qkv):
    def body(q_ref, k_ref, v_ref, o_ref, tot_ref):
        i = pl.program_id(0)
        masks = _head_masks()
        q4 = _stack_heads(q_ref[...] * QK_SCALE, masks)
        later = _tri(lambda r, c: r > c)

        def tiles(js, carry, diagonal):
            o, run = carry
            scores = [_sb_scores(q4, k_ref, j, diagonal) for j in js]
            acc = None
            for rows, before, log_fail, log_hit in scores:
                between = _split_dot(log_fail, later) + run
                w = _keep(before, jnp.exp(log_hit + between))
                part = _dot(w.astype(BF16), v_ref[rows, :], NN)
                acc = part if acc is None else acc + part
                run = run + jnp.sum(log_fail, axis=1, keepdims=True)
            return o + _merge_heads(acc, masks), run

        carry = tiles([i], (jnp.zeros((BLK, SB_WIDTH), F32), jnp.zeros((SB_ROWS, 1), F32)), True)
        carry = lax.cond((i & 1) != 0, lambda c: tiles([i - 1], c, False), lambda c: c, carry)
        at = i - 1 - (i & 1)
        carry = lax.cond((i & 2) != 0, lambda c: tiles([at, at - 1], c, False), lambda c: c, carry)
        at = at - (i & 2)
        o, run = lax.fori_loop(0, i // 4, lambda t, c: tiles([at - 4 * t - n for n in range(4)], c, False), carry)
        o_ref[...] = o
        tot_ref[...] = run

    return pl.pallas_call(
        body, grid=(N_BLK,),
        in_specs=[pl.BlockSpec((BLK, SB_WIDTH), lambda i: (i, 0)), pl.BlockSpec((SEQ, SB_WIDTH), lambda i: (0, 1)),
                  pl.BlockSpec((SEQ, SB_WIDTH), lambda i: (0, 2))],
        out_specs=[pl.BlockSpec((BLK, SB_WIDTH), lambda i: (i, 0)), pl.BlockSpec((None, SB_ROWS, 1), lambda i: (i, 0, 0))],
        out_shape=[jax.ShapeDtypeStruct((SEQ, SB_WIDTH), F32), jax.ShapeDtypeStruct((N_BLK, SB_ROWS, 1), F32)],
        name="sb_forward", compiler_params=_params(("parallel",)),
    )(qkv, qkv, qkv)


def sb_backward(qkv, total, do_cat):
    def body(q_ref, k_ref, v_ref, tot_ref, do_ref, dq_ref, dk_ref, dv_ref):
        i = pl.program_id(0)

        @pl.when(i == 0)
        def _():
            dk_ref[...] = jnp.zeros_like(dk_ref)
            dv_ref[...] = jnp.zeros_like(dv_ref)

        masks = _head_masks()
        q4 = _stack_heads(q_ref[...] * QK_SCALE, masks)
        do4 = _stack_heads(do_ref[...].astype(BF16), masks)
        total_v = tot_ref[...]
        upto = _tri(lambda r, c: r <= c)
        earlier = _tri(lambda r, c: r < c)

        def tiles(js, carry, diagonal):
            dq, seen, g_seen = carry
            scores = [_sb_scores(q4, k_ref, j, diagonal) for j in js]
            acc = None
            for rows, before, log_fail, log_hit in scores:
                between = total_v - (seen + _split_dot(log_fail, upto))
                w = _keep(before, jnp.exp(log_hit + between))
                g = _dot(do4, v_ref[rows, :], NT) * w
                g_earlier = g_seen + _split_dot(g, earlier)
                sig = jnp.exp(log_hit)
                dz = _keep(before, g * (1.0 - sig) - g_earlier * sig).astype(BF16)
                part = _dot(dz, k_ref[rows, :], NN)
                acc = part if acc is None else acc + part
                dk_ref[rows, :] += _dot(dz, q4, TN)
                dv_ref[rows, :] += _dot(w.astype(BF16), do4, TN)
                seen = seen + jnp.sum(log_fail, axis=1, keepdims=True)
                g_seen = g_seen + jnp.sum(g, axis=1, keepdims=True)
            return dq + _merge_heads(acc, masks), seen, g_seen

        zero = jnp.zeros((SB_ROWS, 1), F32)
        carry = lax.fori_loop(0, i // 4, lambda t, c: tiles([4 * t + n for n in range(4)], c, False),
                              (jnp.zeros((BLK, SB_WIDTH), F32), zero, zero))
        at = i - (i & 3)
        carry = lax.cond((i & 2) != 0, lambda c: tiles([at, at + 1], c, False), lambda c: c, carry)
        carry = lax.cond((i & 1) != 0, lambda c: tiles([i - 1], c, False), lambda c: c, carry)
        dq, _, _ = tiles([i], carry, True)
        dq_ref[...] = dq * QK_SCALE

    blk = pl.BlockSpec((BLK, SB_WIDTH), lambda i: (i, 0))
    full = pl.BlockSpec((SEQ, SB_WIDTH), lambda i: (0, 0))
    shape = jax.ShapeDtypeStruct((SEQ, SB_WIDTH), F32)
    return pl.pallas_call(
        body, grid=(N_BLK,),
        in_specs=[blk, pl.BlockSpec((SEQ, SB_WIDTH), lambda i: (0, 1)), pl.BlockSpec((SEQ, SB_WIDTH), lambda i: (0, 2)),
                  pl.BlockSpec((None, SB_ROWS, 1), lambda i: (i, 0, 0)), blk],
        out_specs=[blk, full, full], out_shape=[shape, shape, shape],
        name="sb_backward", compiler_params=_params(("arbitrary",)),
    )(qkv, qkv, qkv, total, do_cat)


def _band_scores(q_ref, kp_ref, ko_ref, bias_ref, hb, prev_mask):
    b = pl.program_id(1)
    qs = q_ref[...]
    s_prev = _dot(qs, kp_ref[...], BATCH_QK) + bias_ref[:, :, 0:BLK]
    s_prev = jnp.concatenate(
        [jnp.where((b & prev_mask(pl.program_id(0) * hb + t)) != 0, s_prev[t:t + 1], NEG) for t in range(hb)], axis=0)
    s_own = _dot(qs, ko_ref[...], BATCH_QK) + bias_ref[:, :, BLK:2 * BLK]
    return qs, s_prev, s_own


def _band_specs(hb, rows, t_n):
    def q_spec(width):
        return pl.BlockSpec((hb, None, rows, width), lambda h, b: (h, b, 0, 0))

    own = pl.BlockSpec((hb, BLK, HEAD_DIM), lambda h, b: (h, b, 0))
    prev = pl.BlockSpec((hb, BLK, HEAD_DIM), lambda h, b: (h, jnp.maximum(b - 1, 0), 0))
    per_head = lambda r, width: pl.BlockSpec((hb, r, width), lambda h, b: (h, 0, 0))
    return q_spec, own, prev, per_head


def banded_forward(name, q, k, v, bias, sinks, hb, prev_mask):
    h_n, nb, rows, _ = q.shape

    def body(q_ref, kp_ref, ko_ref, vp_ref, vo_ref, bias_ref, sink_ref, o_ref, lse_ref):
        _, s_prev, s_own = _band_scores(q_ref, kp_ref, ko_ref, bias_ref, hb, prev_mask)
        sink = sink_ref[...]
        m = jnp.maximum(jnp.maximum(jnp.max(s_prev, axis=2, keepdims=True), jnp.max(s_own, axis=2, keepdims=True)), sink)
        p_prev = jnp.exp(s_prev - m)
        p_own = jnp.exp(s_own - m)
        denom = jnp.sum(p_prev, axis=2, keepdims=True) + jnp.sum(p_own, axis=2, keepdims=True) + jnp.exp(sink - m)
        o = _dot(p_prev.astype(BF16), vp_ref[...], BATCH_PV) + _dot(p_own.astype(BF16), vo_ref[...], BATCH_PV)
        o_ref[...] = o / denom
        lse_ref[...] = m + jnp.log(denom)

    q_spec, own, prev, per_head = _band_specs(hb, rows, k.shape[1])
    return pl.pallas_call(
        body, grid=(h_n // hb, nb),
        in_specs=[q_spec(HEAD_DIM), prev, own, prev, own, per_head(rows, 2 * BLK), per_head(rows, 1)],
        out_specs=[q_spec(HEAD_DIM), q_spec(1)],
        out_shape=[jax.ShapeDtypeStruct(q.shape, F32), jax.ShapeDtypeStruct((h_n, nb, rows, 1), F32)],
        name=name, compiler_params=_params(("parallel", "parallel")),
    )(q, k, k, v, v, bias, sinks)


def banded_backward(name, q, k, v, bias, sinks, o, lse, do, dlse, hb, prev_mask):
    h_n, nb, rows, _ = q.shape
    t_n = k.shape[1]

    def body(q_ref, kp_ref, ko_ref, vp_ref, vo_ref, bias_ref, sink_ref, o_ref, lse_ref, do_ref, dlse_ref,
             dq_ref, dk_ref, dv_ref, dbias_ref, dsink_ref):
        b = pl.program_id(1)

        @pl.when(b == 0)
        def _():
            dk_ref[...] = jnp.zeros_like(dk_ref)
            dv_ref[...] = jnp.zeros_like(dv_ref)
            dbias_ref[...] = jnp.zeros_like(dbias_ref)
            dsink_ref[...] = jnp.zeros_like(dsink_ref)

        qs, s_prev, s_own = _band_scores(q_ref, kp_ref, ko_ref, bias_ref, hb, prev_mask)
        lse_v = lse_ref[...]
        dov = do_ref[...]
        dob = dov.astype(BF16)
        shift = dlse_ref[...] - jnp.sum(dov * o_ref[...], axis=2, keepdims=True)
        p_prev = jnp.exp(s_prev - lse_v)
        p_own = jnp.exp(s_own - lse_v)
        ds_prev = p_prev * (_dot(dob, vp_ref[...], BATCH_QK) + shift)
        ds_own = p_own * (_dot(dob, vo_ref[...], BATCH_QK) + shift)
        dbias_ref[:, :, 0:BLK] += ds_prev
        dbias_ref[:, :, BLK:2 * BLK] += ds_own
        d_sink = jnp.exp(sink_ref[...] - lse_v) * shift
        for g in range(rows // BLK):
            dsink_ref[:, g:g + 1, :] += jnp.sum(d_sink[:, g * BLK:(g + 1) * BLK, :], axis=1, keepdims=True)
        ds_prev = ds_prev.astype(BF16)
        ds_own = ds_own.astype(BF16)
        dq_ref[...] = (_dot(ds_prev, kp_ref[...], BATCH_PV) + _dot(ds_own, ko_ref[...], BATCH_PV)) * QK_SCALE
        rows_prev = pl.ds(pl.multiple_of(jnp.maximum(b - 1, 0) * BLK, BLK), BLK)
        rows_own = pl.ds(pl.multiple_of(b * BLK, BLK), BLK)
        dk_ref[:, rows_prev, :] += _dot(ds_prev, qs, BATCH_TN)
        dk_ref[:, rows_own, :] += _dot(ds_own, qs, BATCH_TN)
        dv_ref[:, rows_prev, :] += _dot(p_prev.astype(BF16), dob, BATCH_TN)
        dv_ref[:, rows_own, :] += _dot(p_own.astype(BF16), dob, BATCH_TN)

    q_spec, own, prev, per_head = _band_specs(hb, rows, t_n)
    kv_full = per_head(t_n, HEAD_DIM)
    kv_shape = jax.ShapeDtypeStruct((h_n, t_n, HEAD_DIM), F32)
    return pl.pallas_call(
        body, grid=(h_n // hb, nb),
        in_specs=[q_spec(HEAD_DIM), prev, own, prev, own, per_head(rows, 2 * BLK), per_head(rows, 1),
                  q_spec(HEAD_DIM), q_spec(1), q_spec(HEAD_DIM), q_spec(1)],
        out_specs=[q_spec(HEAD_DIM), kv_full, kv_full, per_head(rows, 2 * BLK), per_head(rows // BLK, BLK)],
        out_shape=[jax.ShapeDtypeStruct(q.shape, F32), kv_shape, kv_shape,
                   jax.ShapeDtypeStruct((h_n, rows, 2 * BLK), F32), jax.ShapeDtypeStruct((h_n, rows // BLK, BLK), F32)],
        name=name, compiler_params=_params(("parallel", "arbitrary")),
    )(q, k, k, v, v, bias, sinks, o, lse, do, dlse)


def _dil_prev_mask(head):
    group = head // H_PER_DIL
    return jnp.where(group == 0, 15, jnp.where(group == 1, 3, 0))


def _swa_prev_mask(head):
    del head
    return 15


DIL_HEADS_PER_STEP = 6
SWA_HEADS_PER_STEP = 2
SWA_GROUP = H_SWA_Q // H_SWA_KV
N_BLK = SEQ // BLK


def dilated_merge(o, lse):
    def body(o_ref, l_ref, out_ref):
        lv = l_ref[...]
        m = jnp.max(lv, axis=0, keepdims=True)
        e = jnp.exp(lv - m)
        alpha = e / jnp.sum(e, axis=0, keepdims=True)
        out_ref[...] = jnp.sum(alpha * o_ref[...], axis=0)

    return pl.pallas_call(
        body, grid=(H_PER_DIL, SEQ // ROW_TILE),
        in_specs=[pl.BlockSpec((3, None, ROW_TILE, HEAD_DIM), lambda h, i: (0, h, i, 0)),
                  pl.BlockSpec((3, None, ROW_TILE, 1), lambda h, i: (0, h, i, 0))],
        out_specs=pl.BlockSpec((None, ROW_TILE, HEAD_DIM), lambda h, i: (h, i, 0)),
        out_shape=jax.ShapeDtypeStruct((H_PER_DIL, SEQ, HEAD_DIM), F32),
        name="dilated_merge", compiler_params=_params(("parallel", "parallel")),
    )(o, lse)


def dilated_merge_bwd(o, lse, dout):
    def body(o_ref, l_ref, d_ref, do_ref, dl_ref):
        lv = l_ref[...]
        m = jnp.max(lv, axis=0, keepdims=True)
        e = jnp.exp(lv - m)
        alpha = e / jnp.sum(e, axis=0, keepdims=True)
        dv = d_ref[...][None]
        do_ref[...] = alpha * dv
        dalpha = jnp.sum(dv * o_ref[...], axis=-1, keepdims=True)
        dl_ref[...] = alpha * (dalpha - jnp.sum(alpha * dalpha, axis=0, keepdims=True))

    o_spec = pl.BlockSpec((3, None, ROW_TILE, HEAD_DIM), lambda h, i: (0, h, i, 0))
    l_spec = pl.BlockSpec((3, None, ROW_TILE, 1), lambda h, i: (0, h, i, 0))
    return pl.pallas_call(
        body, grid=(H_PER_DIL, SEQ // ROW_TILE),
        in_specs=[o_spec, l_spec, pl.BlockSpec((None, ROW_TILE, HEAD_DIM), lambda h, i: (h, i, 0))],
        out_specs=[o_spec, l_spec],
        out_shape=[jax.ShapeDtypeStruct(o.shape, F32), jax.ShapeDtypeStruct(lse.shape, F32)],
        name="dilated_merge_bwd", compiler_params=_params(("parallel", "parallel")),
    )(o, lse, dout)


def rel_bias_reduce(dbias0, dbias1, bucket):
    def body(d0_ref, d1_ref, b_ref, o_ref):
        dv, bv = d0_ref[...] + d1_ref[...], b_ref[...]
        lane = lax.broadcasted_iota(jnp.int32, (1, BLK), 1)
        acc = jnp.zeros((1, BLK), F32)
        for bkt in range(N_BUCKETS):
            acc = acc + jnp.where(lane == bkt, jnp.sum(jnp.where(bv == bkt, dv, 0.0)), 0.0)
        o_ref[...] = acc

    tile = pl.BlockSpec((None, BLK, 2 * BLK), lambda h: (h, 0, 0))
    return pl.pallas_call(
        body, grid=(dbias0.shape[0],), in_specs=[tile, tile, tile],
        out_specs=pl.BlockSpec((None, 1, BLK), lambda h: (h, 0, 0)),
        out_shape=jax.ShapeDtypeStruct((dbias0.shape[0], 1, BLK), F32),
        name="rel_bias_reduce", compiler_params=_params(("parallel",)),
    )(dbias0, dbias1, bucket)


def _heads(t):
    return t.reshape(SEQ, -1, HEAD_DIM).transpose(1, 0, 2)


def _unheads(t):
    return t.transpose(1, 0, 2).reshape(SEQ, -1)


def _dilate(t):
    parts = []
    for g, (_, d) in enumerate(DIL_PATTERNS):
        tg = t[:, 128 * g:128 * (g + 1)].reshape(SEQ // d, d, H_PER_DIL, HEAD_DIM).transpose(2, 1, 0, 3)
        parts.append(tg.reshape(H_PER_DIL, SEQ, HEAD_DIM))
    return jnp.concatenate(parts, axis=0)


def _undilate(t):
    outs = []
    for g, (_, d) in enumerate(DIL_PATTERNS):
        tg = t[2 * g:2 * g + 2].reshape(H_PER_DIL, d, SEQ // d, -1).transpose(0, 2, 1, 3)
        outs.append(tg.reshape(H_PER_DIL, SEQ, -1))
    return jnp.stack(outs)


def _redilate(t):
    parts = []
    for g, (_, d) in enumerate(DIL_PATTERNS):
        tg = t[g].reshape(H_PER_DIL, SEQ // d, d, -1).transpose(0, 2, 1, 3)
        parts.append(tg.reshape(H_PER_DIL, SEQ, -1))
    return jnp.concatenate(parts, axis=0)


def _t5_bucket(n):
    max_exact = N_BUCKETS // 2
    nf = jnp.maximum(n, 1).astype(F32)
    large = max_exact + (jnp.log(nf / max_exact) / math.log(MAX_REL_DIST / max_exact)
                         * (N_BUCKETS - max_exact)).astype(jnp.int32)
    large = jnp.minimum(large, N_BUCKETS - 1)
    return jnp.where(n < max_exact, n, large)


def band_tables(rel_bias):
    rel = jnp.arange(BLK)[:, None] + BLK - jnp.arange(2 * BLK)[None, :]
    buckets = []
    patterns = [(d, w // d) for w, d in DIL_PATTERNS for _ in range(H_PER_DIL)] + [(1, SWA_WINDOW - 1)] * H_SWA_Q
    for d, max_dist in patterns:
        band = (rel >= 0) & (rel <= max_dist)
        buckets.append(jnp.where(band, _t5_bucket(jnp.maximum(rel, 0) * d), -1))
    buckets = jnp.stack(buckets).astype(jnp.int32)

    def body(table_ref, b_ref, o_ref):
        h = pl.program_id(0)
        bv = b_ref[...]
        tile = jnp.full(bv.shape, NEG, F32)
        for bkt in range(N_BUCKETS):
            tile = jnp.where(bv == bkt, table_ref[h, bkt], tile)
        o_ref[...] = tile

    spec = pl.BlockSpec((None, BLK, 2 * BLK), lambda h: (h, 0, 0))
    tiles = pl.pallas_call(
        body, grid=(len(patterns),), in_specs=[pl.BlockSpec(memory_space=pltpu.SMEM), spec], out_specs=spec,
        out_shape=jax.ShapeDtypeStruct(buckets.shape, F32), name="band_tables", compiler_params=_params(("parallel",)),
    )(rel_bias.T, buckets)
    return tiles[:H_DIL], tiles[H_DIL:], buckets


def _swa_rows(t):
    t = t.reshape(N_BLK, BLK, H_SWA_KV, SWA_GROUP, HEAD_DIM).transpose(2, 0, 3, 1, 4)
    return t.reshape(H_SWA_KV, N_BLK, SWA_GROUP * BLK, HEAD_DIM)


def _swa_tokens(t):
    t = t.reshape(H_SWA_KV, N_BLK, SWA_GROUP, BLK, HEAD_DIM).transpose(1, 3, 0, 2, 4)
    return t.reshape(SEQ, H_SWA_Q * HEAD_DIM)


def _sink_rows(sinks):
    return jnp.broadcast_to(sinks.reshape(H_SWA_KV, SWA_GROUP, 1, 1), (H_SWA_KV, SWA_GROUP, BLK, 1)).reshape(
        H_SWA_KV, SWA_GROUP * BLK, 1)


def _no_sinks():
    return jnp.full((H_DIL, BLK, 1), NEG, F32)


def _vec(v):
    return v.reshape(1, D_MODEL)


class UnitRows:
    def __init__(self, u, mod_table, gain_table):
        self.shift, self.scale, self.gate = (Row(mod_table, 3 * u + t) for t in range(3))
        self.gain = Row(gain_table, u)


def ffn_forward(x, rows, w):
    h = prenorm(x, rows.gain, rows.scale, rows.shift)
    a, b, s = ffn_up(h, w[0], w[1])
    f, xo = ffn_down(s, w[2], x, rows.gate)
    return xo, (x, h, a, b, s, f)


def ffn_backward(u, dxo, saved, rows, w, sums):
    x, h, a, b, s, f = saved
    df, gate_sums = resid_bwd(dxo, f, rows.gate, 0.5, sums[1], u)
    da, db = ffn_bwd_hidden(df, w[2], a, b)
    grads = ffn_grad_weights(h, s, df, da, db)
    dx, norm_sums = ffn_bwd_input(da, db, w[0], w[1], x, dxo, rows.gain, rows.scale, sums[0], u)
    return dx, (norm_sums, gate_sums), grads


def mixer_forward(x, rows, sinks, bias_dil, bias_swa, w):
    h = prenorm(x, rows.gain, rows.scale, rows.shift)
    proj, qkv = in_proj(h, w[0])
    q_dil, k_dil, v_dil = _dilate(qkv[:, 768:1152] * QK_SCALE), _dilate(qkv[:, 1152:1536]), _dilate(qkv[:, 1536:1920])
    q_swa, k_swa, v_swa = _swa_rows(qkv[:, 1920:2304] * QK_SCALE), _heads(qkv[:, 2304:2432]), _heads(qkv[:, 2432:2560])
    o_sb, total_sb = sb_forward(qkv)
    q_dil = q_dil.reshape(H_DIL, N_BLK, BLK, HEAD_DIM)
    o_dd, lse_dd = banded_forward("dilated_forward", q_dil, k_dil, v_dil, bias_dil, _no_sinks(), DIL_HEADS_PER_STEP,
                                  _dil_prev_mask)
    o_dt, lse_dt = _undilate(o_dd.reshape(H_DIL, SEQ, HEAD_DIM)), _undilate(lse_dd.reshape(H_DIL, SEQ, 1))
    o_dil = dilated_merge(o_dt, lse_dt)
    bias_swa = bias_swa.reshape(H_SWA_KV, SWA_GROUP * BLK, 2 * BLK)
    o_swa, lse_swa = banded_forward("swa_forward", q_swa, k_swa, v_swa, bias_swa, _sink_rows(sinks), SWA_HEADS_PER_STEP,
                                    _swa_prev_mask)
    o_cat = jnp.concatenate([o_sb, _unheads(o_dil), _swa_tokens(o_swa)], axis=1)
    merged = merge_branches(o_cat, w[1], proj)
    mo, xo = out_proj(merged, w[2], x, rows.gate)
    saved = (x, h, proj, (qkv, total_sb), (q_dil, k_dil, v_dil, o_dd, lse_dd, o_dt, lse_dt),
             (q_swa, k_swa, v_swa, o_swa, lse_swa), o_cat, merged, mo)
    return xo, saved


def mixer_backward(u, dxo, saved, rows, sinks, bias_dil, bias_swa, w, sums):
    x, h, proj, sb, dil, swa, o_cat, merged, mo = saved
    dmo, gate_sums = resid_bwd(dxo, mo, rows.gate, 1.0, sums[1], u)
    tok = pl.BlockSpec((MM_TILE, D_MODEL), lambda j, k: (k, 0))
    g_out = grad_weight("grad_w_out", merged, pl.BlockSpec((MM_TILE, D_SHARD), lambda j, k: (k, j)), dmo, tok,
                        (D_SHARD, D_MODEL))
    du0, du1, du2, dg0, dg1, dg2 = merge_bwd(dmo, w[2], o_cat, w[1], proj)
    du = (du0, du1, du2)
    do_cat = branch_bwd_input(du, w[1])
    g_br = branch_grad_weights(o_cat, du)

    qkv, total_sb = sb
    dq_sb, dk_sb, dv_sb = sb_backward(qkv, total_sb, do_cat)

    q_dil, k_dil, v_dil, o_dd, lse_dd, o_dt, lse_dt = dil
    do_dt, dlse_dt = dilated_merge_bwd(o_dt, lse_dt, _heads(do_cat[:, 256:384]))
    dq_dil, dk_dil, dv_dil, dbias_dil, _ = banded_backward(
        "dilated_backward", q_dil, k_dil, v_dil, bias_dil, _no_sinks(), o_dd, lse_dd,
        _redilate(do_dt).reshape(q_dil.shape), _redilate(dlse_dt).reshape(lse_dd.shape), DIL_HEADS_PER_STEP, _dil_prev_mask)

    q_swa, k_swa, v_swa, o_swa, lse_swa = swa
    bias_swa = bias_swa.reshape(H_SWA_KV, SWA_GROUP * BLK, 2 * BLK)
    dq_swa, dk_swa, dv_swa, dbias_swa, dsinks = banded_backward(
        "swa_backward", q_swa, k_swa, v_swa, bias_swa, _sink_rows(sinks), o_swa, lse_swa, _swa_rows(do_cat[:, 384:768]),
        jnp.zeros_like(lse_swa), SWA_HEADS_PER_STEP, _swa_prev_mask)
    dbias_swa = dbias_swa.reshape(H_SWA_Q, BLK, 2 * BLK)

    def tokens(t):
        return _undilate(t).transpose(2, 0, 1, 3).reshape(SEQ, -1)

    dproj = jnp.concatenate(
        [dq_sb, dk_sb, dv_sb, tokens(dq_dil.reshape(H_DIL, SEQ, HEAD_DIM)), tokens(dk_dil),
         tokens(dv_dil), _swa_tokens(dq_swa), _unheads(dk_swa), _unheads(dv_swa)], axis=1).astype(BF16)
    dproj = jnp.concatenate([dproj, dg0, dg1, dg2], axis=1)
    g_in = grad_weight("grad_w_in", h, tok, dproj, pl.BlockSpec((MM_TILE, IN_SHARD), lambda j, k: (k, j)),
                       (D_MODEL, IN_SHARD))
    dx, norm_sums = mixer_bwd_input(dproj, w[0], x, dxo, rows.gain, rows.scale, sums[0], u)
    dbias = jnp.concatenate([dbias_dil, dbias_swa], axis=0)
    return dx, (norm_sums, gate_sums), dbias, dsinks[:, :, 0].reshape(H_SWA_Q), (g_in, g_br, g_out)


N_UNITS = 3 * DEPTH


def device_step(x, target, mod, gains, final_gain, sinks, rel_bias, get_weights, put_grads):
    bias_dil, bias_swa, bucket = band_tables(rel_bias)
    mod_table = mod.reshape(3 * N_UNITS, 1, D_MODEL)
    gain_table = gains.reshape(N_UNITS, 1, D_MODEL)
    saved, weights = [], []
    for u in range(N_UNITS):
        l, j = divmod(u, 3)
        w = get_weights(u, x)
        rows = UnitRows(u, mod_table, gain_table)
        if j == 1:
            x, s = mixer_forward(x, rows, sinks[l], bias_dil, bias_swa, w)
        else:
            x, s = ffn_forward(x, rows, w)
        saved.append(s)
        weights.append(w)
    loss, dx, dfinal = final_loss(x, _vec(final_gain), target)

    sums = (lax.empty((8 * N_UNITS, D_MODEL), F32), lax.empty((8 * N_UNITS, D_MODEL), F32))
    dbias, dsinks = [None] * DEPTH, [None] * DEPTH
    zero = jnp.zeros((1, 1), F32)
    for u in reversed(range(N_UNITS)):
        l, j = divmod(u, 3)
        rows = UnitRows(u, mod_table, gain_table + zero)
        if j == 1:
            dx, sums, dbias[l], dsinks[l], grads = mixer_backward(
                u, dx, saved[u], rows, sinks[l], bias_dil, bias_swa, weights[u], sums)
        else:
            dx, sums, grads = ffn_backward(u, dx, saved[u], rows, weights[u], sums)
        if u > 0:
            zero = put_grads(u, grads)
    drel = rel_bias_reduce(dbias[0], dbias[1], bucket)[:, 0, :N_BUCKETS].T
    norm_sums, gate_sums = (t.reshape(DEPTH, 3, 8, D_MODEL) for t in sums)
    dmod = jnp.stack([norm_sums[:, :, 0], norm_sums[:, :, 1], gate_sums[:, :, 0]], axis=2)
    return loss, dx, dmod, norm_sums[:, :, 2], dfinal[0], jnp.stack(dsinks), drel, grads


MESH = pl.DeviceIdType.MESH
CHIP_FLIPS = ((1, 0), (0, 1), (1, 1))
ANY = pl.BlockSpec(memory_space=pl.ANY)


def _position():
    return lax.axis_index("x"), lax.axis_index("y"), lax.axis_index("c")


def all_gather_small(name, piece):
    def body(x_ref, out_ref, send_sems, recv_sems, local_sem):
        x, y, c = _position()
        me, sibling = (x, y, c), (x, y, 1 - c)
        chips = [(x ^ fx, y ^ fy) for fx, fy in CHIP_FLIPS]

        def rows(px, py, pc):
            return out_ref.at[4 * px + 2 * py + pc]

        def copy(k, block, to, src=None):
            return pltpu.make_async_remote_copy(
                src_ref=rows(*block) if src is None else src, dst_ref=rows(*block),
                send_sem=send_sems.at[k], recv_sem=recv_sems.at[k], device_id=to, device_id_type=MESH)

        mine = pltpu.make_async_copy(x_ref, rows(*me), local_sem)
        mine.start()
        first = [copy(0, me, sibling, src=x_ref)]
        first += [copy(1 + j, me, (*chip, c), src=x_ref) for j, chip in enumerate(chips)]
        for cp in first:
            cp.start()
        passed = [copy(4 + j, (*chip, c), sibling) for j, chip in enumerate(chips)]
        for j, chip in enumerate(chips):
            copy(1 + j, (*chip, c), me).wait_recv()
            passed[j].start()
        copy(0, sibling, me).wait_recv()
        for j, chip in enumerate(chips):
            copy(4 + j, (*chip, 1 - c), me).wait_recv()
        for cp in first + passed:
            cp.wait_send()
        mine.wait()

    return pl.pallas_call(
        body, out_shape=jax.ShapeDtypeStruct((N_DEV,) + piece.shape, piece.dtype),
        in_specs=[pl.BlockSpec(memory_space=pltpu.VMEM)], out_specs=pl.BlockSpec(memory_space=pltpu.VMEM),
        scratch_shapes=[pltpu.SemaphoreType.DMA((7,)), pltpu.SemaphoreType.DMA((7,)), pltpu.SemaphoreType.DMA],
        name=name,
    )(piece)


def exchange(name, operands, out_shapes, aliases, plan):
    n_in, n_out = len(operands), len(out_shapes)

    def body(*refs):
        ins, outs = refs[:n_in], refs[n_in:n_in + n_out]
        send_sems, recv_sems, local_sems = refs[n_in + n_out:]
        x, y, c = _position()
        local, sends, recvs = plan(ins, outs, x, y, c)
        local = [pltpu.make_async_copy(s, d, local_sems.at[k]) for k, (s, d) in enumerate(local)]
        for cp in local:
            cp.start()
        remote = [pltpu.make_async_remote_copy(src_ref=s, dst_ref=d, send_sem=send_sems.at[k], recv_sem=recv_sems.at[k],
                                               device_id=dev, device_id_type=MESH)
                  for k, (s, d, dev) in enumerate(sends)]
        for cp in remote:
            cp.start()
        for k, r in enumerate(recvs):
            pltpu.make_async_remote_copy(src_ref=r, dst_ref=r, send_sem=send_sems.at[k], recv_sem=recv_sems.at[k],
                                         device_id=(x, y, c), device_id_type=MESH).wait_recv()
        for cp in remote:
            cp.wait_send()
        for cp in local:
            cp.wait()

    n_sends, n_local = plan.n_sends, max(plan.n_local, 1)
    return pl.pallas_call(
        body, out_shape=out_shapes, in_specs=[ANY] * n_in, out_specs=[ANY] * n_out,
        scratch_shapes=[pltpu.SemaphoreType.DMA((n_sends,)), pltpu.SemaphoreType.DMA((n_sends,)),
                        pltpu.SemaphoreType.DMA((n_local,))],
        input_output_aliases=aliases, name=name,
    )(*operands)


def _plan(n_local, n_sends):
    def wrap(fn):
        fn.n_local, fn.n_sends = n_local, n_sends
        return fn
    return wrap


def _half(ref, axis, c):
    rows = ref.shape[axis] // 2
    idx = [slice(None)] * len(ref.shape)
    idx[axis] = pl.ds(pl.multiple_of(c * rows, 16), rows)
    return ref.at[tuple(idx)]


HBM = pl.BlockSpec(memory_space=pltpu.HBM)
SEM = pl.BlockSpec(memory_space=pltpu.SEMAPHORE)
EFFECT = pltpu.SideEffectType.DATAFLOW_SIDE_EFFECTING


def split_start(name, bufs, extra, n_copies, describe):
    n = len(bufs)

    def body(*refs):
        send_sems, recv_sems = refs[n + len(extra)], refs[n + len(extra) + 1]
        x, y, c = _position()
        for k, (src, dst, _, peer) in enumerate(describe(refs[:n], x, y, c)):
            pltpu.make_async_remote_copy(src_ref=src, dst_ref=dst, send_sem=send_sems.at[k], recv_sem=recv_sems.at[k],
                                         device_id=peer, device_id_type=MESH).start()
        token = refs[-1]
        token[...] = jnp.zeros_like(token)

    out = pl.pallas_call(
        body, name=name,
        out_shape=(pltpu.SemaphoreType.DMA((n_copies,)), pltpu.SemaphoreType.DMA((n_copies,)),
                   *[pltpu.HBM(b.shape, b.dtype) for b in bufs], jax.ShapeDtypeStruct((8, 128), F32)),
        in_specs=[HBM] * n + [ANY] * len(extra),
        out_specs=(SEM, SEM, *[HBM] * n, pl.BlockSpec(memory_space=pltpu.VMEM)),
        input_output_aliases={k: 2 + k for k in range(n)},
        compiler_params=pltpu.CompilerParams(has_side_effects=EFFECT),
    )(*[pltpu.with_memory_space_constraint(b, pltpu.HBM) for b in bufs], *extra)
    return out[0], out[1], list(out[2:2 + n]), out[-1]


def split_wait(name, bufs, send_sems, recv_sems, after, describe):
    n = len(bufs)

    def body(*refs):
        send, recv = refs[n], refs[n + 1]
        x, y, c = _position()
        for k, (src, _, dst, peer) in enumerate(describe(refs[:n], x, y, c)):
            copy = pltpu.make_async_remote_copy(src_ref=src, dst_ref=dst, send_sem=send.at[k], recv_sem=recv.at[k],
                                                device_id=peer, device_id_type=MESH)
            copy.wait_send()
            copy.wait_recv()

    out = pl.pallas_call(
        body, name=name, out_shape=[pltpu.HBM(b.shape, b.dtype) for b in bufs],
        in_specs=[HBM] * n + [SEM, SEM] + [ANY] * len(after), out_specs=[HBM] * n,
        input_output_aliases={k: k for k in range(n)},
        compiler_params=pltpu.CompilerParams(has_side_effects=EFFECT),
    )(*bufs, send_sems, recv_sems, *after)
    return list(out)


def _row_tile(rows, cols, max_elements=256 * 1024):
    best = 16
    for t in range(16, rows + 1, 16):
        if rows % t == 0 and t * cols <= max_elements:
            best = t
    return best


def cast_into_slot(name, param, index, chip):
    rows, cols = param.shape[-2:]
    tr = _row_tile(rows, cols)
    lead = (None,) * len(index)

    def body(chip_ref, s_ref, o_ref):
        del chip_ref
        o_ref[...] = s_ref[...].astype(BF16)

    return pl.pallas_call(
        body, out_shape=jax.ShapeDtypeStruct((N_CHIPS, rows, cols), BF16),
        grid_spec=pltpu.PrefetchScalarGridSpec(
            num_scalar_prefetch=1, grid=(rows // tr,),
            in_specs=[pl.BlockSpec(lead + (tr, cols), lambda r, chip_ref: index + (r, 0))],
            out_specs=pl.BlockSpec((None, tr, cols), lambda r, chip_ref: (chip_ref[0], r, 0))),
        name=name, compiler_params=_params(("parallel",)),
    )(chip, param)


GATHER_STAGES = ((0,), (1,), (2,), (3, 4, 5))
REDUCE_STAGES = ((5, 4, 3), (2,), (1,), (0,))


def _gather_copies(slots, x, y, c):
    me = 2 * x + y
    out = []
    for s in slots:
        for fx, fy in CHIP_FLIPS:
            mine = _half(s.at[me], 0, c)
            out.append((mine, mine, _half(s.at[2 * (x ^ fx) + (y ^ fy)], 0, c), (x ^ fx, y ^ fy, c)))
    return out


class WeightStream:
    def __init__(self, shards, chip, after=()):
        self.pending, self.ready = {}, {}
        token = tuple(after)
        for si, units in enumerate(GATHER_STAGES):
            slots = [cast_into_slot(f"cast_{u}_{t}", p, idx, chip) for u in units for t, (p, idx) in enumerate(shards[u])]
            send, recv, slots, tok = split_start(f"gather_start_{si}", slots, token, 3 * len(slots), _gather_copies)
            self.pending[si] = (send, recv, slots)
            token = (tok,)
        self.token = token

    def get(self, u, after):
        if u not in self.ready:
            si = next(k for k, units in enumerate(GATHER_STAGES) if u in units)
            send, recv, slots = self.pending.pop(si)
            slots = split_wait(f"gather_wait_{si}", slots, send, recv, (after,) + self.token, _gather_copies)
            self.token = ()

            @_plan(0, 3 * len(slots))
            def to_sibling(ins, outs, x, y, c):
                sends, recvs = [], []
                for o in outs:
                    for fx, fy in CHIP_FLIPS:
                        slab = o.at[2 * (x ^ fx) + (y ^ fy)]
                        sends.append((_half(slab, 0, c), _half(slab, 0, c), (x, y, 1 - c)))
                        recvs.append(_half(slab, 0, 1 - c))
                return [], sends, recvs

            shapes = [jax.ShapeDtypeStruct(s.shape, BF16) for s in slots]
            slots = exchange(f"gather_sibling_{si}", slots, shapes, {k: k for k in range(len(slots))}, to_sibling)
            for i, v in enumerate(GATHER_STAGES[si]):
                self.ready[v] = tuple(slots[3 * i:3 * i + 3])
        return self.ready[u]


def _reduce_copies(bufs, x, y, c):
    n = len(bufs) // 2
    out = []
    for s, land in zip(bufs[:n], bufs[n:]):
        for k, (fx, fy) in enumerate(CHIP_FLIPS):
            out.append((s.at[2 * (x ^ fx) + (y ^ fy)], land.at[k], land.at[k], (x ^ fx, y ^ fy, c)))
    return out


GRAD_SLOTS = {"gate": (2 * DEPTH, FF_SHARD, D_MODEL), "up": (2 * DEPTH, FF_SHARD, D_MODEL),
              "down": (2 * DEPTH, FF_SHARD, D_MODEL), "in": (DEPTH, D_MODEL, IN_SHARD),
              "br": (DEPTH, BR_ROWS, D_SHARD), "out": (DEPTH, D_SHARD, D_MODEL)}


def _unit_tensors(u):
    l, j = divmod(u, 3)
    if j == 1:
        return [("in", l), ("br", l), ("out", l)]
    return [(k, 2 * l + j // 2) for k in ("gate", "up", "down")]


class GradStream:
    def __init__(self, chip, core):
        self.core = core
        self.place = jnp.concatenate([chip, core])
        self.held, self.flying = {}, []
        self.full = {k: lax.empty(shape, F32) for k, shape in GRAD_SLOTS.items()}

    def put(self, u, grads, after=()):
        self.held[u] = grads
        si = len(self.flying)
        units = REDUCE_STAGES[si]
        if not all(v in self.held for v in units):
            return jnp.zeros((1, 1), F32)
        gs = [g for v in units for g in self.held[v]]

        @_plan(0, len(gs))
        def swap_halves(ins, outs, x, y, c):
            sends = [(_half(g, 1, 1 - c), o, (x, y, 1 - c)) for g, o in zip(ins, outs)]
            return [], sends, list(outs)

        half_shapes = [jax.ShapeDtypeStruct((N_CHIPS, g.shape[1] // 2, g.shape[2]), BF16) for g in gs]
        landed = exchange(f"reduce_swap_{si}", gs + list(after), half_shapes, {}, swap_halves)
        sums = [_add_halves(g, la, self.core) for g, la in zip(gs, landed)]
        landing = [lax.empty((3,) + s.shape[1:], BF16) for s in sums]
        send, recv, bufs, token = split_start(f"reduce_start_{si}", sums + landing, (), 3 * len(sums), _reduce_copies)
        self.flying.append((send, recv, bufs, [t for v in units for t in _unit_tensors(v)]))
        return token[0:1, 0:1]

    def finish(self, after):
        for si, (send, recv, bufs, tensors) in enumerate(self.flying):
            bufs = split_wait(f"reduce_wait_{si}", bufs, send, recv, tuple(after), _reduce_copies)
            n = len(tensors)
            for (name, slot), s, land in zip(tensors, bufs[:n], bufs[n:]):
                self.full[name] = _add_chips(s, land, self.place, self.full[name], slot)
        names = list(self.full)

        @_plan(0, len(names))
        def share_halves(ins, outs, x, y, c):
            sends = [(_half(o, 1, c), _half(o, 1, c), (x, y, 1 - c)) for o in outs]
            return [], sends, [_half(o, 1, 1 - c) for o in outs]

        shapes = [jax.ShapeDtypeStruct(self.full[k].shape, F32) for k in names]
        out = exchange("reduce_share_halves", [self.full[k] for k in names], shapes, {k: k for k in range(len(names))},
                       share_halves)
        return dict(zip(names, out))


def _add_halves(g, landed, core):
    _, rh, cols = landed.shape
    tr = _row_tile(rh, cols, 1024 * 1024)
    per_half = rh // tr

    def body(core_ref, g_ref, la_ref, o_ref):
        del core_ref
        o_ref[...] = (g_ref[...].astype(F32) + la_ref[...].astype(F32)).astype(BF16)

    blk = (None, tr, cols)
    return pl.pallas_call(
        body, out_shape=jax.ShapeDtypeStruct(landed.shape, BF16),
        grid_spec=pltpu.PrefetchScalarGridSpec(
            num_scalar_prefetch=1, grid=(N_CHIPS, per_half),
            in_specs=[pl.BlockSpec(blk, lambda j, r, core_ref: (j, core_ref[0] * per_half + r, 0)),
                      pl.BlockSpec(blk, lambda j, r, core_ref: (j, r, 0))],
            out_specs=pl.BlockSpec(blk, lambda j, r, core_ref: (j, r, 0))),
        name="reduce_add_halves", compiler_params=_params(("parallel", "parallel")),
    )(core, g, landed)


def _add_chips(sums, landed, place, full, slot):
    _, rh, cols = sums.shape
    tr = _row_tile(rh, cols, 1024 * 1024)
    per_half = rh // tr

    def body(place_ref, s_ref, la_ref, full_in, o_ref):
        del place_ref, full_in
        o_ref[...] = ((s_ref[...].astype(F32) + la_ref[0].astype(F32)) + la_ref[1].astype(F32)) + la_ref[2].astype(F32)

    return pl.pallas_call(
        body, out_shape=jax.ShapeDtypeStruct(full.shape, F32),
        grid_spec=pltpu.PrefetchScalarGridSpec(
            num_scalar_prefetch=1, grid=(per_half,),
            in_specs=[pl.BlockSpec((None, tr, cols), lambda r, place_ref: (place_ref[0], r, 0)),
                      pl.BlockSpec((3, tr, cols), lambda r, place_ref: (0, r, 0)), ANY],
            out_specs=pl.BlockSpec((None, tr, cols), lambda r, place_ref: (slot, place_ref[1] * per_half + r, 0))),
        input_output_aliases={3: 0}, name="reduce_add_chips", compiler_params=_params(("parallel",)),
    )(place, sums, landed, full)


def sum_devices(parts):
    def body(p_ref, o_ref):
        acc = p_ref[0]
        for d in range(1, N_DEV):
            acc = acc + p_ref[d]
        o_ref[...] = acc

    return pl.pallas_call(body, out_shape=jax.ShapeDtypeStruct(parts.shape[1:], F32), name="sum_devices")(parts)


ADA_SHARD = 9 * D_MODEL // N_CHIPS
ADA_TILE = 768
ADA_ROWS = 16


def ada_forward(c_rows, w_ada, b_shard):
    def body(c_ref, w_ref, b_ref, o_ref):
        cv = c_ref[...]
        o_ref[...] = _dot((cv * _sigmoid(cv)).astype(BF16), w_ref[...].astype(BF16), NN) + b_ref[...]

    return pl.pallas_call(
        body, grid=(DEPTH, ADA_SHARD // ADA_TILE),
        in_specs=[pl.BlockSpec((ADA_ROWS, D_MODEL), lambda l, n: (0, 0)),
                  pl.BlockSpec((None, D_MODEL, ADA_TILE), lambda l, n: (l, 0, n)),
                  pl.BlockSpec((None, 1, ADA_TILE), lambda l, n: (l, 0, n))],
        out_specs=pl.BlockSpec((None, ADA_ROWS, ADA_TILE), lambda l, n: (l, 0, n)),
        out_shape=jax.ShapeDtypeStruct((DEPTH, ADA_ROWS, ADA_SHARD), F32),
        name="ada_forward", compiler_params=_params(("parallel", "parallel")),
    )(c_rows, w_ada, b_shard)


def ada_backward(c_rows, dmod_rows):
    def body(c_ref, d_ref, o_ref):
        cv = c_ref[...]
        o_ref[...] = _dot((cv * _sigmoid(cv)).astype(BF16), d_ref[...].astype(BF16), TN)

    return pl.pallas_call(
        body, grid=(DEPTH, ADA_SHARD // ADA_TILE),
        in_specs=[pl.BlockSpec((ADA_ROWS, D_MODEL), lambda l, n: (0, 0)),
                  pl.BlockSpec((None, ADA_ROWS, ADA_TILE), lambda l, n: (l, 0, n))],
        out_specs=pl.BlockSpec((None, D_MODEL, ADA_TILE), lambda l, n: (l, 0, n)),
        out_shape=jax.ShapeDtypeStruct((DEPTH, D_MODEL, ADA_SHARD), F32),
        name="ada_backward", compiler_params=_params(("parallel", "parallel")),
    )(c_rows, dmod_rows)


def adamw(name, w, g, m, v):
    shape = w.shape
    cols = shape[-1]
    rows = w.size // cols
    tr = _row_tile(rows, cols) if rows % 16 == 0 else rows
    c1 = 1.0 / (1.0 - ADAM_B1 ** ADAM_STEP)
    c2 = 1.0 / (1.0 - ADAM_B2 ** ADAM_STEP)

    def body(w_ref, g_ref, m_ref, v_ref, go_ref, d_ref, mo_ref, vo_ref):
        gv = g_ref[...]
        mn = ADAM_B1 * m_ref[...] + (1.0 - ADAM_B1) * gv
        vn = ADAM_B2 * v_ref[...] + (1.0 - ADAM_B2) * (gv * gv)
        go_ref[...] = gv
        mo_ref[...] = mn
        vo_ref[...] = vn
        d_ref[...] = -ADAM_LR * ((mn * c1) / (jnp.sqrt(vn * c2) + ADAM_EPS) + ADAM_WD * w_ref[...])

    spec = pl.BlockSpec((tr, cols), lambda i: (i, 0))
    out = jax.ShapeDtypeStruct((rows, cols), F32)
    res = pl.pallas_call(
        body, grid=(rows // tr,), in_specs=[spec] * 4, out_specs=[spec] * 4, out_shape=[out] * 4,
        name=name, compiler_params=_params(("parallel",)),
    )(*[t.reshape(rows, cols) for t in (w, g, m, v)])
    return tuple(r.reshape(shape) for r in res)


def _pack(parts, rows):
    flat = jnp.concatenate([p.reshape(-1) for p in parts])
    return jnp.pad(flat, (0, rows * 128 - flat.size)).reshape(rows, 128)


def _unpack(flat, shapes):
    out, at = [], 0
    for s in shapes:
        n = math.prod(s)
        out.append(flat[at:at + n].reshape(s))
        at += n
    return out


def kernel(x, c, w_ada, b_ada, norm_gain, w_ffn_gate, w_ffn_up, w_ffn_down, w_in, w_br_sb, w_br_dil, w_br_swa, w_out, sinks, rel_bias, final_gain, loss_target, m_w_ada, m_b_ada, m_norm_gain, m_w_ffn_gate, m_w_ffn_up, m_w_ffn_down, m_w_in, m_w_br_sb, m_w_br_dil, m_w_br_swa, m_w_out, m_sinks, m_rel_bias, m_final_gain, v_w_ada, v_b_ada, v_norm_gain, v_w_ffn_gate, v_w_ffn_up, v_w_ffn_down, v_w_in, v_w_br_sb, v_w_br_dil, v_w_br_swa, v_w_out, v_sinks, v_rel_bias, v_final_gain):
    xi, yi, ci = _position()
    chip = 2 * xi + yi
    dev = 2 * chip + ci

    c_all = all_gather_small("gather_c", c.reshape(8, 128)).reshape(N_DEV, D_MODEL)
    c_rows = jnp.pad(c_all, ((0, ADA_ROWS - N_DEV), (0, 0)))
    b_shard = lax.dynamic_slice_in_dim(b_ada, chip * ADA_SHARD, ADA_SHARD, axis=1).reshape(DEPTH, 1, ADA_SHARD)
    mod_shard = ada_forward(c_rows, w_ada, b_shard)[:, :N_DEV]
    n_mod = DEPTH * N_DEV * ADA_SHARD
    gathered = all_gather_small("gather_mod", _pack([mod_shard, norm_gain], 304))[::2].reshape(N_CHIPS, -1)
    mod_all = gathered[:, :n_mod].reshape(N_CHIPS, DEPTH, N_DEV, ADA_SHARD)
    mod = lax.dynamic_index_in_dim(mod_all, dev, axis=2, keepdims=False)
    mod = mod.transpose(1, 0, 2).reshape(DEPTH, 3, 3, D_MODEL)
    gains = gathered[:, n_mod:n_mod + DEPTH * 3 * D_SHARD].reshape(N_CHIPS, DEPTH, 3, D_SHARD)
    gains = gains.transpose(1, 2, 0, 3).reshape(DEPTH, 3, D_MODEL)

    chip_i, core_i = chip.astype(jnp.int32).reshape(1), ci.astype(jnp.int32).reshape(1)
    w_br = jnp.concatenate([w_br_sb, w_br_dil, w_br_swa], axis=1)
    transposed = (3, 4)
    w_gate_t, w_up_t = jnp.swapaxes(w_ffn_gate, 2, 3), jnp.swapaxes(w_ffn_up, 2, 3)
    shards = []
    for l in range(DEPTH):
        ffn = [[(w_gate_t, (l, f)), (w_up_t, (l, f)), (w_ffn_down, (l, f))] for f in range(2)]
        shards += [ffn[0], [(w_in, (l,)), (w_br, (l,)), (w_out, (l,))], ffn[1]]
    weights_in = WeightStream(shards, chip_i, (gathered,))
    grads_out = GradStream(chip_i, core_i)

    loss, dx, dmod, dgains, dfinal, dsinks, drel, last_grads = device_step(
        x[0], loss_target[0], mod, gains, final_gain, sinks, rel_bias, weights_in.get, grads_out.put)

    small_shapes = [(DEPTH, 9 * D_MODEL), (DEPTH, 3, D_MODEL), (D_MODEL,), (DEPTH, H_SWA_Q), (N_BUCKETS, 12), (1,)]
    small_all = all_gather_small("gather_small_grads", _pack([dmod, dgains, dfinal, dsinks, drel, loss[0, 0:1]], 208))
    started = grads_out.put(0, last_grads, after=(small_all,))
    small_all = small_all + started
    g_b_ada, g_gain_full, g_final, g_sinks, g_rel, loss_sum = _unpack(sum_devices(small_all).reshape(-1), small_shapes)
    g_gain = lax.dynamic_slice_in_dim(g_gain_full, chip * D_SHARD, D_SHARD, axis=2)
    dmod_all = small_all.reshape(N_DEV, -1)[:, :DEPTH * 9 * D_MODEL].reshape(N_DEV, DEPTH, 9 * D_MODEL)
    dmod_rows = lax.dynamic_slice_in_dim(dmod_all, chip * ADA_SHARD, ADA_SHARD, axis=2).transpose(1, 0, 2)
    g_w_ada = ada_backward(c_rows, jnp.pad(dmod_rows, ((0, 0), (0, ADA_ROWS - N_DEV), (0, 0))))

    weights = [w_ada, b_ada, norm_gain, w_ffn_gate, w_ffn_up, w_ffn_down, w_in, w_br_sb, w_br_dil, w_br_swa, w_out,
               sinks, rel_bias, final_gain]
    ms = [m_w_ada, m_b_ada, m_norm_gain, m_w_ffn_gate, m_w_ffn_up, m_w_ffn_down, m_w_in, m_w_br_sb, m_w_br_dil,
          m_w_br_swa, m_w_out, m_sinks, m_rel_bias, m_final_gain]
    vs = [v_w_ada, v_b_ada, v_norm_gain, v_w_ffn_gate, v_w_ffn_up, v_w_ffn_down, v_w_in, v_w_br_sb, v_w_br_dil,
          v_w_br_swa, v_w_out, v_sinks, v_rel_bias, v_final_gain]
    grads = [g_w_ada, g_b_ada, g_gain] + [None] * 8 + [g_sinks, g_rel, g_final]

    small = (1, 2, 11, 12, 13)
    deltas, new_ms, new_vs = [None] * 14, [None] * 14, [None] * 14
    _, deltas[0], new_ms[0], new_vs[0] = adamw("adamw_0", weights[0], grads[0], ms[0], vs[0])
    shapes = [weights[k].shape for k in small]
    packed = [_pack([t[k] for k in small], 168) for t in (weights, grads, ms, vs)]
    for dst, res in zip((deltas, new_ms, new_vs), adamw("adamw_small", *packed)[1:]):
        for k, t in zip(small, _unpack(res.reshape(-1), shapes)):
            dst[k] = t

    g = grads_out.finish((dx, deltas[0], deltas[1]))
    g_br = g["br"]
    grads[3:11] = [g["gate"].reshape(w_gate_t.shape), g["up"].reshape(w_up_t.shape),
                   g["down"].reshape(w_ffn_down.shape), g["in"], g_br[:, 0:256], g_br[:, 256:384], g_br[:, 384:768],
                   g["out"]]
    for k in range(3, 11):
        state = [weights[k], ms[k], vs[k]]
        if k in transposed:
            state = [jnp.swapaxes(t, 2, 3) for t in state]
        out = adamw(f"adamw_{k}", state[0], grads[k], state[1], state[2])
        if k in transposed:
            out = [jnp.swapaxes(t, 2, 3) for t in out]
        grads[k], deltas[k], new_ms[k], new_vs[k] = out
    return (loss_sum[0], dx[None], *grads, *deltas, *new_ms, *new_vs)
```

```python
import functools
import math

import jax
import jax.numpy as jnp
from jax import lax
from jax.experimental import pallas as pl
from jax.experimental.pallas import tpu as pltpu

F32 = jnp.float32
BF16 = jnp.bfloat16

D_MODEL = 1024
SEQ = 2048
DEPTH = 2
HEAD_DIM = 64
BLK = 128
H_SB = 4
DIL_PATTERNS = ((128, 1), (512, 4), (2048, 16))
H_PER_DIL = 2
H_DIL = 6
H_SWA_Q = 6
H_SWA_KV = 2
SWA_WINDOW = 128
N_BUCKETS = 32
MAX_REL_DIST = 2048
D_FF = 2816
RMS_EPS = 1e-6
N_CHIPS = 4
N_DEV = 8
FF_SHARD = D_FF // N_CHIPS
D_QKV = 2560
D_IN = D_QKV + 3 * D_MODEL
IN_SHARD = D_IN // N_CHIPS
D_SHARD = D_MODEL // N_CHIPS
BR_ROWS = 768
NEG = -1e30
QK_SCALE = HEAD_DIM ** -0.5

ADAM_LR = 0.001
ADAM_B1 = 0.9
ADAM_B2 = 0.999
ADAM_EPS = 1e-08
ADAM_WD = 0.01
ADAM_STEP = 10

VMEM_LIMIT = 56 * 1024 * 1024
ROW_TILE = 256
MM_TILE = 1024

NN = (((1,), (0,)), ((), ()))
NT = (((1,), (1,)), ((), ()))
TN = (((0,), (0,)), ((), ()))


def _params(sem=None):
    return pltpu.CompilerParams(dimension_semantics=sem, vmem_limit_bytes=VMEM_LIMIT)


def _dot(a, b, dims):
    return lax.dot_general(a, b, dims, preferred_element_type=F32)


def _sigmoid(x):
    return 1.0 / (1.0 + jnp.exp(-x))


def _matmul(name, grid, nk, k_axis, dims, n_pairs, in_specs, out_specs, out_shape, acc_shape, epilogue,
            operands, sem, aliases=None, prologue=None):
    n_in = len(in_specs)
    n_out = len(out_specs)

    def partial(ins):
        tot = None
        for p in range(n_pairs):
            a = ins[2 * p][...]
            if prologue is not None:
                a = prologue(p, a, ins)
            d = _dot(a, ins[2 * p + 1][...], dims)
            tot = d if tot is None else tot + d
        return tot

    def body(*refs):
        ins, outs = refs[:n_in], refs[n_in:n_in + n_out]
        ids = tuple(pl.program_id(a) for a in range(len(grid)))
        if nk == 1:
            epilogue(partial(ins), ins, outs, ids)
            return
        acc = refs[n_in + n_out]
        k = ids[k_axis]

        @pl.when(k == 0)
        def _():
            acc[...] = partial(ins)

        @pl.when(k > 0)
        def _():
            acc[...] += partial(ins)

        @pl.when(k == nk - 1)
        def _():
            epilogue(acc[...], ins, outs, ids)

    return pl.pallas_call(
        body, grid=grid, in_specs=in_specs, out_specs=out_specs, out_shape=out_shape,
        scratch_shapes=[] if nk == 1 else [pltpu.VMEM(acc_shape, F32)],
        input_output_aliases=aliases or {}, name=name, compiler_params=_params(sem),
    )(*operands)


def _row_spec(width=D_MODEL):
    return pl.BlockSpec((ROW_TILE, width), lambda i: (i, 0))


def _vec_spec(rows=1, width=D_MODEL):
    return pl.BlockSpec((rows, width), lambda i: (0, 0))


class Row:
    def __init__(self, table, index):
        self.table, self.index = table, index

    def spec(self):
        index = self.index
        return pl.BlockSpec((None, 1, D_MODEL), lambda *ids: (index, 0, 0))


def _slot_spec(u):
    return pl.BlockSpec((8, D_MODEL), lambda *ids: (u, 0))


def prenorm(x, gain, scale, shift):
    def body(x_ref, g_ref, sc_ref, sh_ref, h_ref):
        xv = x_ref[...]
        r = lax.rsqrt(jnp.mean(xv * xv, axis=-1, keepdims=True) + RMS_EPS)
        h_ref[...] = (((xv * r) * g_ref[...]) * (1.0 + sc_ref[...]) + sh_ref[...]).astype(BF16)

    return pl.pallas_call(
        body, grid=(SEQ // ROW_TILE,), in_specs=[_row_spec(), gain.spec(), scale.spec(), shift.spec()],
        out_specs=_row_spec(), out_shape=jax.ShapeDtypeStruct((SEQ, D_MODEL), BF16),
        name="prenorm", compiler_params=_params(("parallel",)),
    )(x, gain.table, scale.table, shift.table)


def resid_bwd(dxo, f, coef, mult, sums, u):
    def body(dx_ref, f_ref, c_ref, sums_in, df_ref, dc_ref):
        del sums_in
        dx = dx_ref[...]
        df_ref[...] = (dx * (mult * c_ref[...])).astype(BF16)
        part = mult * jnp.sum(dx * f_ref[...], axis=0, keepdims=True)

        @pl.when(pl.program_id(0) == 0)
        def _():
            dc_ref[...] = jnp.zeros_like(dc_ref)

        dc_ref[0:1, :] += part

    return pl.pallas_call(
        body, grid=(SEQ // ROW_TILE,),
        in_specs=[_row_spec(), _row_spec(), coef.spec(), pl.BlockSpec(memory_space=pl.ANY)],
        out_specs=[_row_spec(), _slot_spec(u)],
        out_shape=[jax.ShapeDtypeStruct((SEQ, D_MODEL), BF16), jax.ShapeDtypeStruct(sums.shape, F32)],
        input_output_aliases={3: 1}, name="resid_bwd", compiler_params=_params(("arbitrary",)),
    )(dxo, f, coef.table, sums)


def final_loss(x, gain, target):
    def body(x_ref, g_ref, t_ref, loss_ref, dx_ref, dg_ref):
        xv = x_ref[...]
        g = g_ref[...]
        r = lax.rsqrt(jnp.mean(xv * xv, axis=-1, keepdims=True) + RMS_EPS)
        xh = xv * r
        e = xh * g - t_ref[...]
        part = 0.5 * jnp.sum(jnp.mean(e * e, axis=-1, keepdims=True), axis=0, keepdims=True)
        dy = e * (1.0 / D_MODEL)
        dyg = dy * g
        dx_ref[...] = r * (dyg - xh * jnp.mean(dyg * xh, axis=-1, keepdims=True))

        @pl.when(pl.program_id(0) == 0)
        def _():
            loss_ref[...] = jnp.zeros_like(loss_ref)
            dg_ref[...] = jnp.zeros_like(dg_ref)

        loss_ref[...] += jnp.broadcast_to(part, loss_ref.shape)
        dg_ref[0:1, :] += jnp.sum(dy * xh, axis=0, keepdims=True)

    return pl.pallas_call(
        body, grid=(SEQ // ROW_TILE,), in_specs=[_row_spec(), _vec_spec(), _row_spec()],
        out_specs=[_vec_spec(8, 128), _row_spec(), _vec_spec(8)],
        out_shape=[jax.ShapeDtypeStruct((8, 128), F32), jax.ShapeDtypeStruct((SEQ, D_MODEL), F32),
                   jax.ShapeDtypeStruct((8, D_MODEL), F32)],
        name="final_loss", compiler_params=_params(("arbitrary",)),
    )(x, gain, target)


def _prenorm_bwd_epilogue(dh, x_ref, dxo_ref, g_ref, sc_ref, dx_ref, stats_ref, first):
    xv = x_ref[...]
    g = g_ref[...]
    r = lax.rsqrt(jnp.mean(xv * xv, axis=-1, keepdims=True) + RMS_EPS)
    xh = xv * r
    dn = dh * (1.0 + sc_ref[...])
    dxh = dn * g
    dx = dxo_ref[...] + r * (dxh - xh * jnp.mean(dxh * xh, axis=-1, keepdims=True))
    dx_ref[...] = dx

    @pl.when(first)
    def _():
        stats_ref[...] = jnp.zeros_like(stats_ref)

    stats_ref[0:1, :] += jnp.sum(dh, axis=0, keepdims=True)
    stats_ref[1:2, :] += jnp.sum(dh * (xh * g), axis=0, keepdims=True)
    stats_ref[2:3, :] += jnp.sum(dn * xh, axis=0, keepdims=True)
    return dx


def _resid_bwd_epilogue(dx, f_ref, c_ref, mult, df_ref, dc_ref, first):
    df_ref[...] = (dx * (mult * c_ref[...])).astype(BF16)

    @pl.when(first)
    def _():
        dc_ref[...] = jnp.zeros_like(dc_ref)

    dc_ref[0:1, :] += mult * jnp.sum(dx * f_ref[...], axis=0, keepdims=True)


def ffn_up(h, wg_all, wu_all):
    def body(h_ref, wg_ref, wu_ref, a_ref, b_ref, s_ref):
        hv = h_ref[...]
        a = _dot(hv, wg_ref[...], NT)
        b = _dot(hv, wu_ref[...], NT)
        a_ref[...] = a.astype(BF16)
        b_ref[...] = b.astype(BF16)
        s_ref[...] = (a * _sigmoid(a) * b).astype(BF16)

    w_spec = pl.BlockSpec((None, FF_SHARD, D_MODEL), lambda j, i: (j, 0, 0))
    o_spec = pl.BlockSpec((None, MM_TILE, FF_SHARD), lambda j, i: (j, i, 0))
    hid = (N_CHIPS, SEQ, FF_SHARD)
    return pl.pallas_call(
        body, grid=(N_CHIPS, SEQ // MM_TILE),
        in_specs=[pl.BlockSpec((MM_TILE, D_MODEL), lambda j, i: (i, 0)), w_spec, w_spec],
        out_specs=[o_spec, o_spec, o_spec],
        out_shape=[jax.ShapeDtypeStruct(hid, BF16)] * 3,
        name="ffn_up", compiler_params=_params(("parallel", "parallel")),
    )(h, wg_all, wu_all)


def matmul_residual(name, a, a_spec, w_all, w_spec, x, coef, mult, then=None):
    def epilogue(acc, ins, outs, ids):
        outs[0][...] = acc
        xo = ins[2][...] + (mult * ins[3][...]) * acc
        outs[1][...] = xo
        if then is not None:
            r = lax.rsqrt(jnp.mean(xo * xo, axis=-1, keepdims=True) + RMS_EPS)
            outs[2][...] = (((xo * r) * ins[4][...]) * (1.0 + ins[5][...]) + ins[6][...]).astype(BF16)

    row = pl.BlockSpec((MM_TILE, D_MODEL), lambda i, j: (i, 0))
    f32 = jax.ShapeDtypeStruct((SEQ, D_MODEL), F32)
    extra = [] if then is None else [then.gain, then.scale, then.shift]
    return _matmul(
        name, (SEQ // MM_TILE, N_CHIPS), N_CHIPS, 1, NN, 1,
        [a_spec, w_spec, row, coef.spec()] + [t.spec() for t in extra], [row] * (2 + bool(extra)),
        [f32, f32] + [jax.ShapeDtypeStruct((SEQ, D_MODEL), BF16)] * bool(extra), (MM_TILE, D_MODEL), epilogue,
        (a, w_all, x, coef.table) + tuple(t.table for t in extra), ("parallel", "arbitrary"))


def ffn_down(s, wd_all, x, gate, then):
    return matmul_residual(
        "ffn_down", s, pl.BlockSpec((None, MM_TILE, FF_SHARD), lambda i, j: (j, i, 0)),
        wd_all, pl.BlockSpec((None, FF_SHARD, D_MODEL), lambda i, j: (j, 0, 0)), x, gate, 0.5, then)


def ffn_bwd_hidden(df, wd_all, a, b):
    def epilogue(ds, ins, outs, ids):
        av, bv = ins[2][...].astype(F32), ins[3][...].astype(F32)
        sig = _sigmoid(av)
        outs[0][...] = (ds * bv * (sig * (1.0 + av * (1.0 - sig)))).astype(BF16)
        outs[1][...] = (ds * (av * sig)).astype(BF16)

    hid_spec = pl.BlockSpec((None, MM_TILE, FF_SHARD), lambda j, i: (j, i, 0))
    hid = jax.ShapeDtypeStruct((N_CHIPS, SEQ, FF_SHARD), BF16)
    return _matmul(
        "ffn_bwd_hidden", (N_CHIPS, SEQ // MM_TILE), 1, None, NT, 1,
        [pl.BlockSpec((MM_TILE, D_MODEL), lambda j, i: (i, 0)),
         pl.BlockSpec((None, FF_SHARD, D_MODEL), lambda j, i: (j, 0, 0)), hid_spec, hid_spec],
        [hid_spec, hid_spec], [hid, hid], None, epilogue, (df, wd_all, a, b), ("parallel", "parallel"))


def grad_weight(name, lhs, lhs_spec, rhs, rhs_spec, shape):
    def epilogue(acc, ins, outs, ids):
        outs[0][...] = acc.astype(BF16)

    return _matmul(
        name, (N_CHIPS, SEQ // MM_TILE), SEQ // MM_TILE, 1, TN, 1,
        [lhs_spec, rhs_spec], [pl.BlockSpec((None,) + shape, lambda j, k: (j, 0, 0))],
        [jax.ShapeDtypeStruct((N_CHIPS,) + shape, BF16)], shape, epilogue, (lhs, rhs), ("parallel", "arbitrary"))[0]


def ffn_grad_weights(h, s, df, da, db):
    tok = pl.BlockSpec((MM_TILE, D_MODEL), lambda j, k: (k, 0))
    hid = pl.BlockSpec((None, MM_TILE, FF_SHARD), lambda j, k: (j, k, 0))
    return (grad_weight("grad_w_gate", da, hid, h, tok, (FF_SHARD, D_MODEL)),
            grad_weight("grad_w_up", db, hid, h, tok, (FF_SHARD, D_MODEL)),
            grad_weight("grad_w_down", s, hid, df, tok, (FF_SHARD, D_MODEL)))


BWD_TILE = 512


def matmul_prenorm_bwd(name, dims, pairs, pair_specs, x, dxo, gain, scale, sums, u, below):
    n = len(pairs)

    def epilogue(dh, ins, outs, ids):
        first = ids[0] == 0
        dx = _prenorm_bwd_epilogue(dh, ins[n], ins[n + 1], ins[n + 2], ins[n + 3], outs[0], outs[1], first)
        if below is not None:
            _resid_bwd_epilogue(dx, ins[n + 5], ins[n + 6], below[2], outs[2], outs[3], first)

    row = pl.BlockSpec((BWD_TILE, D_MODEL), lambda i, j: (i, 0))
    any_spec = pl.BlockSpec(memory_space=pl.ANY)
    f32 = jax.ShapeDtypeStruct((SEQ, D_MODEL), F32)
    in_specs = list(pair_specs) + [row, row, gain.spec(), scale.spec(), any_spec]
    operands = tuple(pairs) + (x, dxo, gain.table, scale.table, sums[0])
    out_specs, out_shape, aliases = [row, _slot_spec(u)], [f32, jax.ShapeDtypeStruct(sums[0].shape, F32)], {n + 4: 1}
    if below is not None:
        in_specs += [row, below[1].spec(), any_spec]
        operands += (below[0], below[1].table, sums[1])
        out_specs += [row, _slot_spec(u - 1)]
        out_shape += [jax.ShapeDtypeStruct((SEQ, D_MODEL), BF16), jax.ShapeDtypeStruct(sums[1].shape, F32)]
        aliases[n + 7] = 3
    out = _matmul(name, (SEQ // BWD_TILE, N_CHIPS), N_CHIPS, 1, dims, len(pairs) // 2, in_specs, out_specs, out_shape,
                  (BWD_TILE, D_MODEL), epilogue, operands, ("arbitrary", "arbitrary"), aliases=aliases)
    if below is None:
        return out[0], (out[1], sums[1])
    return out[0], (out[1], out[3]), out[2]


def ffn_bwd_input(da, db, wg_all, wu_all, x, dxo, gain, scale, sums, u, below):
    hid = pl.BlockSpec((None, BWD_TILE, FF_SHARD), lambda i, j: (j, i, 0))
    w = pl.BlockSpec((None, FF_SHARD, D_MODEL), lambda i, j: (j, 0, 0))
    return matmul_prenorm_bwd("ffn_bwd_input", NN, (da, wg_all, db, wu_all), (hid, w, hid, w), x, dxo, gain, scale,
                              sums, u, below)


def in_proj(h, w_all):
    def epilogue(acc, ins, outs, ids):
        outs[0][...] = acc
        outs[1][...] = acc.astype(BF16)

    out = pl.BlockSpec((MM_TILE, IN_SHARD), lambda j, i: (i, j))
    return _matmul(
        "in_proj", (N_CHIPS, SEQ // MM_TILE), 1, None, NN, 1,
        [pl.BlockSpec((MM_TILE, D_MODEL), lambda j, i: (i, 0)),
         pl.BlockSpec((None, D_MODEL, IN_SHARD), lambda j, i: (j, 0, 0))],
        [out, out], [jax.ShapeDtypeStruct((SEQ, D_IN), F32), jax.ShapeDtypeStruct((SEQ, D_IN), BF16)],
        None, epilogue, (h, w_all), ("parallel", "parallel"))


_GATE_BLOCK0 = D_QKV // D_SHARD


def _branch_products(o, w_ref):
    ob = o.astype(BF16)
    return (_dot(ob[:, 0:256], w_ref[0:256, :], NN), _dot(ob[:, 256:384], w_ref[256:384, :], NN),
            _dot(ob[:, 384:768], w_ref[384:768, :], NN))


def merge_branches(o_cat, wbr_all, proj):
    def body(o_ref, w_ref, g0_ref, g1_ref, g2_ref, m_ref):
        u = _branch_products(o_ref[...], w_ref)
        m_ref[...] = (_sigmoid(g0_ref[...]) * u[0] + _sigmoid(g1_ref[...]) * u[1]
                      + _sigmoid(g2_ref[...]) * u[2]).astype(BF16)

    def gate_spec(b):
        return pl.BlockSpec((MM_TILE, D_SHARD), lambda i, j: (i, _GATE_BLOCK0 + 4 * b + j))

    return pl.pallas_call(
        body, grid=(SEQ // MM_TILE, N_CHIPS),
        in_specs=[pl.BlockSpec((MM_TILE, BR_ROWS), lambda i, j: (i, 0)),
                  pl.BlockSpec((None, BR_ROWS, D_SHARD), lambda i, j: (j, 0, 0)),
                  gate_spec(0), gate_spec(1), gate_spec(2)],
        out_specs=pl.BlockSpec((MM_TILE, D_SHARD), lambda i, j: (i, j)),
        out_shape=jax.ShapeDtypeStruct((SEQ, D_MODEL), BF16),
        name="merge_branches", compiler_params=_params(("parallel", "parallel")),
    )(o_cat, wbr_all, proj, proj, proj)


def out_proj(merged, wout_all, x, gate, then):
    return matmul_residual(
        "out_proj", merged, pl.BlockSpec((MM_TILE, D_SHARD), lambda i, j: (i, j)),
        wout_all, pl.BlockSpec((None, D_SHARD, D_MODEL), lambda i, j: (j, 0, 0)), x, gate, 1.0, then)


def merge_bwd(dmo, wout_all, o_cat, wbr_all, proj):
    def epilogue(dm, ins, outs, ids):
        u = _branch_products(ins[2][...], ins[3])
        for b in range(3):
            sig = _sigmoid(ins[4 + b][...])
            outs[b][...] = (dm * sig).astype(BF16)
            outs[3 + b][...] = (dm * u[b] * (sig * (1.0 - sig))).astype(BF16)

    def gate_spec(b):
        return pl.BlockSpec((MM_TILE, D_SHARD), lambda j, i: (i, _GATE_BLOCK0 + 4 * b + j))

    col = pl.BlockSpec((MM_TILE, D_SHARD), lambda j, i: (i, j))
    du = jax.ShapeDtypeStruct((SEQ, D_MODEL), BF16)
    return _matmul(
        "merge_bwd", (N_CHIPS, SEQ // MM_TILE), 1, None, NT, 1,
        [pl.BlockSpec((MM_TILE, D_MODEL), lambda j, i: (i, 0)),
         pl.BlockSpec((None, D_SHARD, D_MODEL), lambda j, i: (j, 0, 0)),
         pl.BlockSpec((MM_TILE, BR_ROWS), lambda j, i: (i, 0)),
         pl.BlockSpec((None, BR_ROWS, D_SHARD), lambda j, i: (j, 0, 0)),
         gate_spec(0), gate_spec(1), gate_spec(2)],
        [col] * 6, [du] * 6,
        None, epilogue, (dmo, wout_all, o_cat, wbr_all, proj, proj, proj), ("parallel", "parallel"))


def branch_bwd_input(du, wbr_all):
    def body(d0_ref, d1_ref, d2_ref, w_ref, o_ref, acc):
        j = pl.program_id(1)
        parts = (_dot(d0_ref[...], w_ref[0:256, :], NT), _dot(d1_ref[...], w_ref[256:384, :], NT),
                 _dot(d2_ref[...], w_ref[384:768, :], NT))

        @pl.when(j == 0)
        def _():
            acc[:, 0:256], acc[:, 256:384], acc[:, 384:768] = parts

        @pl.when(j > 0)
        def _():
            acc[:, 0:256] += parts[0]
            acc[:, 256:384] += parts[1]
            acc[:, 384:768] += parts[2]

        @pl.when(j == N_CHIPS - 1)
        def _():
            o_ref[...] = acc[...]

    col = pl.BlockSpec((MM_TILE, D_SHARD), lambda i, j: (i, j))
    return pl.pallas_call(
        body, grid=(SEQ // MM_TILE, N_CHIPS),
        in_specs=[col, col, col, pl.BlockSpec((None, BR_ROWS, D_SHARD), lambda i, j: (j, 0, 0))],
        out_specs=pl.BlockSpec((MM_TILE, BR_ROWS), lambda i, j: (i, 0)),
        out_shape=jax.ShapeDtypeStruct((SEQ, BR_ROWS), F32),
        scratch_shapes=[pltpu.VMEM((MM_TILE, BR_ROWS), F32)],
        name="branch_bwd_input", compiler_params=_params(("parallel", "arbitrary")),
    )(du[0], du[1], du[2], wbr_all)


def branch_grad_weights(o_cat, du):
    def body(o_ref, d0_ref, d1_ref, d2_ref, g_ref, acc):
        k = pl.program_id(1)
        ob = o_ref[...].astype(BF16)
        parts = (_dot(ob[:, 0:256], d0_ref[...], TN), _dot(ob[:, 256:384], d1_ref[...], TN),
                 _dot(ob[:, 384:768], d2_ref[...], TN))

        @pl.when(k == 0)
        def _():
            acc[0:256, :], acc[256:384, :], acc[384:768, :] = parts

        @pl.when(k > 0)
        def _():
            acc[0:256, :] += parts[0]
            acc[256:384, :] += parts[1]
            acc[384:768, :] += parts[2]

        @pl.when(k == SEQ // MM_TILE - 1)
        def _():
            g_ref[...] = acc[...].astype(BF16)

    col = pl.BlockSpec((MM_TILE, D_SHARD), lambda j, k: (k, j))
    return pl.pallas_call(
        body, grid=(N_CHIPS, SEQ // MM_TILE),
        in_specs=[pl.BlockSpec((MM_TILE, BR_ROWS), lambda j, k: (k, 0)), col, col, col],
        out_specs=pl.BlockSpec((None, BR_ROWS, D_SHARD), lambda j, k: (j, 0, 0)),
        out_shape=jax.ShapeDtypeStruct((N_CHIPS, BR_ROWS, D_SHARD), BF16),
        scratch_shapes=[pltpu.VMEM((BR_ROWS, D_SHARD), F32)],
        name="branch_grad_weights", compiler_params=_params(("parallel", "arbitrary")),
    )(o_cat, du[0], du[1], du[2])


def mixer_bwd_input(dproj, win_all, x, dxo, gain, scale, sums, u, below):
    return matmul_prenorm_bwd(
        "mixer_bwd_input", NT, (dproj, win_all),
        (pl.BlockSpec((BWD_TILE, IN_SHARD), lambda i, j: (i, j)),
         pl.BlockSpec((None, D_MODEL, IN_SHARD), lambda i, j: (j, 0, 0))), x, dxo, gain, scale, sums, u, below)


BATCH_QK = (((2,), (2,)), ((0,), (0,)))
BATCH_PV = (((2,), (1,)), ((0,), (0,)))
BATCH_TN = (((1,), (1,)), ((0,), (0,)))


SB_WIDTH = H_SB * HEAD_DIM
SB_ROWS = H_SB * BLK


def _split_dot(v, tri):
    hi = v.astype(BF16)
    lo = (v - hi.astype(F32)).astype(BF16)
    return _dot(hi, tri, NN) + _dot(lo, tri, NN)


def _tri(cmp):
    return cmp(lax.broadcasted_iota(jnp.int32, (BLK, BLK), 0), lax.broadcasted_iota(jnp.int32, (BLK, BLK), 1)).astype(BF16)


def _head_masks():
    lane = lax.broadcasted_iota(jnp.int32, (1, SB_WIDTH), 1) // HEAD_DIM
    return [lane == h for h in range(H_SB)]


def _stack_heads(x, masks):
    return jnp.concatenate([jnp.where(m, x, jnp.zeros_like(x)) for m in masks], axis=0)


def _merge_heads(y, masks):
    out = jnp.where(masks[0], y[0:BLK], 0.0)
    for h in range(1, H_SB):
        out = jnp.where(masks[h], y[h * BLK:(h + 1) * BLK], out)
    return out


def _sb_scores(q4, k_ref, j, diagonal):
    rows = pl.ds(pl.multiple_of(j * BLK, BLK), BLK)
    z = _dot(q4, k_ref[rows, :], NT)
    log_fail = -(jnp.maximum(z, 0.0) + jnp.log(1.0 + jnp.exp(-jnp.abs(z))))
    log_hit = z + log_fail
    before = None
    if diagonal:
        tile = (SB_ROWS, BLK)
        before = lax.broadcasted_iota(jnp.int32, tile, 1) < (lax.broadcasted_iota(jnp.int32, tile, 0) & (BLK - 1))
        log_fail = jnp.where(before, log_fail, 0.0)
    return rows, before, log_fail, log_hit


def _keep(before, x):
    return x if before is None else jnp.where(before, x, 0.0)


def sb_forward(qkv):
    def body(q_ref, k_ref, v_ref, o_ref, tot_ref):
        i = pl.program_id(0)
        masks = _head_masks()
        q4 = _stack_heads(q_ref[...] * QK_SCALE, masks)
        later = _tri(lambda r, c: r > c)

        def tiles(js, carry, diagonal):
            o, run = carry
            scores = [_sb_scores(q4, k_ref, j, diagonal) for j in js]
            acc = None
            for rows, before, log_fail, log_hit in scores:
                between = _split_dot(log_fail, later) + run
                w = _keep(before, jnp.exp(log_hit + between))
                part = _dot(w.astype(BF16), v_ref[rows, :], NN)
                acc = part if acc is None else acc + part
                run = run + jnp.sum(log_fail, axis=1, keepdims=True)
            return o + _merge_heads(acc, masks), run

        carry = tiles([i], (jnp.zeros((BLK, SB_WIDTH), F32), jnp.zeros((SB_ROWS, 1), F32)), True)
        carry = lax.cond((i & 1) != 0, lambda c: tiles([i - 1], c, False), lambda c: c, carry)
        at = i - 1 - (i & 1)
        carry = lax.cond((i & 2) != 0, lambda c: tiles([at, at - 1], c, False), lambda c: c, carry)
        at = at - (i & 2)
        o, run = lax.fori_loop(0, i // 4, lambda t, c: tiles([at - 4 * t - n for n in range(4)], c, False), carry)
        o_ref[...] = o
        tot_ref[...] = run

    return pl.pallas_call(
        body, grid=(N_BLK,),
        in_specs=[pl.BlockSpec((BLK, SB_WIDTH), lambda i: (i, 0)), pl.BlockSpec((SEQ, SB_WIDTH), lambda i: (0, 1)),
                  pl.BlockSpec((SEQ, SB_WIDTH), lambda i: (0, 2))],
        out_specs=[pl.BlockSpec((BLK, SB_WIDTH), lambda i: (i, 0)), pl.BlockSpec((None, SB_ROWS, 1), lambda i: (i, 0, 0))],
        out_shape=[jax.ShapeDtypeStruct((SEQ, SB_WIDTH), F32), jax.ShapeDtypeStruct((N_BLK, SB_ROWS, 1), F32)],
        name="sb_forward", compiler_params=_params(("parallel",)),
    )(qkv, qkv, qkv)


def sb_backward(qkv, total, do_cat):
    def body(q_ref, k_ref, v_ref, tot_ref, do_ref, dq_ref, dk_ref, dv_ref):
        i = pl.program_id(0)

        @pl.when(i == 0)
        def _():
            dk_ref[...] = jnp.zeros_like(dk_ref)
            dv_ref[...] = jnp.zeros_like(dv_ref)

        masks = _head_masks()
        q4 = _stack_heads(q_ref[...] * QK_SCALE, masks)
        do4 = _stack_heads(do_ref[...].astype(BF16), masks)
        total_v = tot_ref[...]
        upto = _tri(lambda r, c: r <= c)
        earlier = _tri(lambda r, c: r < c)

        def tiles(js, carry, diagonal):
            dq, seen, g_seen = carry
            scores = [_sb_scores(q4, k_ref, j, diagonal) for j in js]
            acc = None
            for rows, before, log_fail, log_hit in scores:
                between = total_v - (seen + _split_dot(log_fail, upto))
                w = _keep(before, jnp.exp(log_hit + between))
                g = _dot(do4, v_ref[rows, :], NT) * w
                g_earlier = g_seen + _split_dot(g, earlier)
                sig = jnp.exp(log_hit)
                dz = _keep(before, g * (1.0 - sig) - g_earlier * sig).astype(BF16)
                part = _dot(dz, k_ref[rows, :], NN)
                acc = part if acc is None else acc + part
                dk_ref[rows, :] += _dot(dz, q4, TN)
                dv_ref[rows, :] += _dot(w.astype(BF16), do4, TN)
                seen = seen + jnp.sum(log_fail, axis=1, keepdims=True)
                g_seen = g_seen + jnp.sum(g, axis=1, keepdims=True)
            return dq + _merge_heads(acc, masks), seen, g_seen

        zero = jnp.zeros((SB_ROWS, 1), F32)
        carry = lax.fori_loop(0, i // 4, lambda t, c: tiles([4 * t + n for n in range(4)], c, False),
                              (jnp.zeros((BLK, SB_WIDTH), F32), zero, zero))
        at = i - (i & 3)
        carry = lax.cond((i & 2) != 0, lambda c: tiles([at, at + 1], c, False), lambda c: c, carry)
        carry = lax.cond((i & 1) != 0, lambda c: tiles([i - 1], c, False), lambda c: c, carry)
        dq, _, _ = tiles([i], carry, True)
        dq_ref[...] = dq * QK_SCALE

    blk = pl.BlockSpec((BLK, SB_WIDTH), lambda i: (i, 0))
    full = pl.BlockSpec((SEQ, SB_WIDTH), lambda i: (0, 0))
    shape = jax.ShapeDtypeStruct((SEQ, SB_WIDTH), F32)
    return pl.pallas_call(
        body, grid=(N_BLK,),
        in_specs=[blk, pl.BlockSpec((SEQ, SB_WIDTH), lambda i: (0, 1)), pl.BlockSpec((SEQ, SB_WIDTH), lambda i: (0, 2)),
                  pl.BlockSpec((None, SB_ROWS, 1), lambda i: (i, 0, 0)), blk],
        out_specs=[blk, full, full], out_shape=[shape, shape, shape],
        name="sb_backward", compiler_params=_params(("arbitrary",)),
    )(qkv, qkv, qkv, total, do_cat)


def _band_scores(q_ref, kp_ref, ko_ref, bias_ref, hb, prev_mask):
    b = pl.program_id(1)
    qs = q_ref[...]
    s_prev = _dot(qs, kp_ref[...], BATCH_QK) + bias_ref[:, :, 0:BLK]
    s_prev = jnp.concatenate(
        [jnp.where((b & prev_mask(pl.program_id(0) * hb + t)) != 0, s_prev[t:t + 1], NEG) for t in range(hb)], axis=0)
    s_own = _dot(qs, ko_ref[...], BATCH_QK) + bias_ref[:, :, BLK:2 * BLK]
    return qs, s_prev, s_own


def _band_specs(hb, rows, t_n):
    def q_spec(width):
        return pl.BlockSpec((hb, None, rows, width), lambda h, b: (h, b, 0, 0))

    own = pl.BlockSpec((hb, BLK, HEAD_DIM), lambda h, b: (h, b, 0))
    prev = pl.BlockSpec((hb, BLK, HEAD_DIM), lambda h, b: (h, jnp.maximum(b - 1, 0), 0))
    per_head = lambda r, width: pl.BlockSpec((hb, r, width), lambda h, b: (h, 0, 0))
    return q_spec, own, prev, per_head


def banded_forward(name, q, k, v, bias, sinks, hb, prev_mask):
    h_n, nb, rows, _ = q.shape

    def body(q_ref, kp_ref, ko_ref, vp_ref, vo_ref, bias_ref, sink_ref, o_ref, lse_ref):
        _, s_prev, s_own = _band_scores(q_ref, kp_ref, ko_ref, bias_ref, hb, prev_mask)
        sink = sink_ref[...]
        m = jnp.maximum(jnp.maximum(jnp.max(s_prev, axis=2, keepdims=True), jnp.max(s_own, axis=2, keepdims=True)), sink)
        p_prev = jnp.exp(s_prev - m)
        p_own = jnp.exp(s_own - m)
        denom = jnp.sum(p_prev, axis=2, keepdims=True) + jnp.sum(p_own, axis=2, keepdims=True) + jnp.exp(sink - m)
        o = _dot(p_prev.astype(BF16), vp_ref[...], BATCH_PV) + _dot(p_own.astype(BF16), vo_ref[...], BATCH_PV)
        o_ref[...] = o / denom
        lse_ref[...] = m + jnp.log(denom)

    q_spec, own, prev, per_head = _band_specs(hb, rows, k.shape[1])
    return pl.pallas_call(
        body, grid=(h_n // hb, nb),
        in_specs=[q_spec(HEAD_DIM), prev, own, prev, own, per_head(rows, 2 * BLK), per_head(rows, 1)],
        out_specs=[q_spec(HEAD_DIM), q_spec(1)],
        out_shape=[jax.ShapeDtypeStruct(q.shape, F32), jax.ShapeDtypeStruct((h_n, nb, rows, 1), F32)],
        name=name, compiler_params=_params(("parallel", "parallel")),
    )(q, k, k, v, v, bias, sinks)


def banded_backward(name, q, k, v, bias, sinks, o, lse, do, dlse, hb, prev_mask):
    h_n, nb, rows, _ = q.shape
    t_n = k.shape[1]

    def body(q_ref, kp_ref, ko_ref, vp_ref, vo_ref, bias_ref, sink_ref, o_ref, lse_ref, do_ref, dlse_ref,
             dq_ref, dk_ref, dv_ref, dbias_ref, dsink_ref):
        b = pl.program_id(1)

        @pl.when(b == 0)
        def _():
            dk_ref[...] = jnp.zeros_like(dk_ref)
            dv_ref[...] = jnp.zeros_like(dv_ref)
            dbias_ref[...] = jnp.zeros_like(dbias_ref)
            dsink_ref[...] = jnp.zeros_like(dsink_ref)

        qs, s_prev, s_own = _band_scores(q_ref, kp_ref, ko_ref, bias_ref, hb, prev_mask)
        lse_v = lse_ref[...]
        dov = do_ref[...]
        dob = dov.astype(BF16)
        shift = dlse_ref[...] - jnp.sum(dov * o_ref[...], axis=2, keepdims=True)
        p_prev = jnp.exp(s_prev - lse_v)
        p_own = jnp.exp(s_own - lse_v)
        ds_prev = p_prev * (_dot(dob, vp_ref[...], BATCH_QK) + shift)
        ds_own = p_own * (_dot(dob, vo_ref[...], BATCH_QK) + shift)
        dbias_ref[:, :, 0:BLK] += ds_prev
        dbias_ref[:, :, BLK:2 * BLK] += ds_own
        d_sink = jnp.exp(sink_ref[...] - lse_v) * shift
        for g in range(rows // BLK):
            dsink_ref[:, g:g + 1, :] += jnp.sum(d_sink[:, g * BLK:(g + 1) * BLK, :], axis=1, keepdims=True)
        ds_prev = ds_prev.astype(BF16)
        ds_own = ds_own.astype(BF16)
        dq_ref[...] = (_dot(ds_prev, kp_ref[...], BATCH_PV) + _dot(ds_own, ko_ref[...], BATCH_PV)) * QK_SCALE
        rows_prev = pl.ds(pl.multiple_of(jnp.maximum(b - 1, 0) * BLK, BLK), BLK)
        rows_own = pl.ds(pl.multiple_of(b * BLK, BLK), BLK)
        dk_ref[:, rows_prev, :] += _dot(ds_prev, qs, BATCH_TN)
        dk_ref[:, rows_own, :] += _dot(ds_own, qs, BATCH_TN)
        dv_ref[:, rows_prev, :] += _dot(p_prev.astype(BF16), dob, BATCH_TN)
        dv_ref[:, rows_own, :] += _dot(p_own.astype(BF16), dob, BATCH_TN)

    q_spec, own, prev, per_head = _band_specs(hb, rows, t_n)
    kv_full = per_head(t_n, HEAD_DIM)
    kv_shape = jax.ShapeDtypeStruct((h_n, t_n, HEAD_DIM), F32)
    return pl.pallas_call(
        body, grid=(h_n // hb, nb),
        in_specs=[q_spec(HEAD_DIM), prev, own, prev, own, per_head(rows, 2 * BLK), per_head(rows, 1),
                  q_spec(HEAD_DIM), q_spec(1), q_spec(HEAD_DIM), q_spec(1)],
        out_specs=[q_spec(HEAD_DIM), kv_full, kv_full, per_head(rows, 2 * BLK), per_head(rows // BLK, BLK)],
        out_shape=[jax.ShapeDtypeStruct(q.shape, F32), kv_shape, kv_shape,
                   jax.ShapeDtypeStruct((h_n, rows, 2 * BLK), F32), jax.ShapeDtypeStruct((h_n, rows // BLK, BLK), F32)],
        name=name, compiler_params=_params(("parallel", "arbitrary")),
    )(q, k, k, v, v, bias, sinks, o, lse, do, dlse)


def _dil_prev_mask(head):
    group = head // H_PER_DIL
    return jnp.where(group == 0, 15, jnp.where(group == 1, 3, 0))


def _swa_prev_mask(head):
    del head
    return 15


DIL_HEADS_PER_STEP = 6
SWA_HEADS_PER_STEP = 2
SWA_GROUP = H_SWA_Q // H_SWA_KV
N_BLK = SEQ // BLK


def dilated_merge(o, lse):
    def body(o_ref, l_ref, out_ref):
        lv = l_ref[...]
        m = jnp.max(lv, axis=0, keepdims=True)
        e = jnp.exp(lv - m)
        alpha = e / jnp.sum(e, axis=0, keepdims=True)
        out_ref[...] = jnp.sum(alpha * o_ref[...], axis=0)

    return pl.pallas_call(
        body, grid=(H_PER_DIL, SEQ // ROW_TILE),
        in_specs=[pl.BlockSpec((3, None, ROW_TILE, HEAD_DIM), lambda h, i: (0, h, i, 0)),
                  pl.BlockSpec((3, None, ROW_TILE, 1), lambda h, i: (0, h, i, 0))],
        out_specs=pl.BlockSpec((None, ROW_TILE, HEAD_DIM), lambda h, i: (h, i, 0)),
        out_shape=jax.ShapeDtypeStruct((H_PER_DIL, SEQ, HEAD_DIM), F32),
        name="dilated_merge", compiler_params=_params(("parallel", "parallel")),
    )(o, lse)


def dilated_merge_bwd(o, lse, dout):
    def body(o_ref, l_ref, d_ref, do_ref, dl_ref):
        lv = l_ref[...]
        m = jnp.max(lv, axis=0, keepdims=True)
        e = jnp.exp(lv - m)
        alpha = e / jnp.sum(e, axis=0, keepdims=True)
        dv = d_ref[...][None]
        do_ref[...] = alpha * dv
        dalpha = jnp.sum(dv * o_ref[...], axis=-1, keepdims=True)
        dl_ref[...] = alpha * (dalpha - jnp.sum(alpha * dalpha, axis=0, keepdims=True))

    o_spec = pl.BlockSpec((3, None, ROW_TILE, HEAD_DIM), lambda h, i: (0, h, i, 0))
    l_spec = pl.BlockSpec((3, None, ROW_TILE, 1), lambda h, i: (0, h, i, 0))
    return pl.pallas_call(
        body, grid=(H_PER_DIL, SEQ // ROW_TILE),
        in_specs=[o_spec, l_spec, pl.BlockSpec((None, ROW_TILE, HEAD_DIM), lambda h, i: (h, i, 0))],
        out_specs=[o_spec, l_spec],
        out_shape=[jax.ShapeDtypeStruct(o.shape, F32), jax.ShapeDtypeStruct(lse.shape, F32)],
        name="dilated_merge_bwd", compiler_params=_params(("parallel", "parallel")),
    )(o, lse, dout)


def rel_bias_reduce(dbias0, dbias1, bucket):
    def body(d0_ref, d1_ref, b_ref, o_ref):
        dv, bv = d0_ref[...] + d1_ref[...], b_ref[...]
        lane = lax.broadcasted_iota(jnp.int32, (1, BLK), 1)
        acc = jnp.zeros((1, BLK), F32)
        for bkt in range(N_BUCKETS):
            acc = acc + jnp.where(lane == bkt, jnp.sum(jnp.where(bv == bkt, dv, 0.0)), 0.0)
        o_ref[...] = acc

    tile = pl.BlockSpec((None, BLK, 2 * BLK), lambda h: (h, 0, 0))
    return pl.pallas_call(
        body, grid=(dbias0.shape[0],), in_specs=[tile, tile, tile],
        out_specs=pl.BlockSpec((None, 1, BLK), lambda h: (h, 0, 0)),
        out_shape=jax.ShapeDtypeStruct((dbias0.shape[0], 1, BLK), F32),
        name="rel_bias_reduce", compiler_params=_params(("parallel",)),
    )(dbias0, dbias1, bucket)


def _heads(t):
    return t.reshape(SEQ, -1, HEAD_DIM).transpose(1, 0, 2)


def _unheads(t):
    return t.transpose(1, 0, 2).reshape(SEQ, -1)


def _dilate(t):
    parts = []
    for g, (_, d) in enumerate(DIL_PATTERNS):
        tg = t[:, 128 * g:128 * (g + 1)].reshape(SEQ // d, d, H_PER_DIL, HEAD_DIM).transpose(2, 1, 0, 3)
        parts.append(tg.reshape(H_PER_DIL, SEQ, HEAD_DIM))
    return jnp.concatenate(parts, axis=0)


def _undilate(t):
    outs = []
    for g, (_, d) in enumerate(DIL_PATTERNS):
        tg = t[2 * g:2 * g + 2].reshape(H_PER_DIL, d, SEQ // d, -1).transpose(0, 2, 1, 3)
        outs.append(tg.reshape(H_PER_DIL, SEQ, -1))
    return jnp.stack(outs)


def _redilate(t):
    parts = []
    for g, (_, d) in enumerate(DIL_PATTERNS):
        tg = t[g].reshape(H_PER_DIL, SEQ // d, d, -1).transpose(0, 2, 1, 3)
        parts.append(tg.reshape(H_PER_DIL, SEQ, -1))
    return jnp.concatenate(parts, axis=0)


def _t5_bucket(n):
    max_exact = N_BUCKETS // 2
    nf = jnp.maximum(n, 1).astype(F32)
    large = max_exact + (jnp.log(nf / max_exact) / math.log(MAX_REL_DIST / max_exact)
                         * (N_BUCKETS - max_exact)).astype(jnp.int32)
    large = jnp.minimum(large, N_BUCKETS - 1)
    return jnp.where(n < max_exact, n, large)


def band_tables(rel_bias):
    rel = jnp.arange(BLK)[:, None] + BLK - jnp.arange(2 * BLK)[None, :]
    buckets = []
    patterns = [(d, w // d) for w, d in DIL_PATTERNS for _ in range(H_PER_DIL)] + [(1, SWA_WINDOW - 1)] * H_SWA_Q
    for d, max_dist in patterns:
        band = (rel >= 0) & (rel <= max_dist)
        buckets.append(jnp.where(band, _t5_bucket(jnp.maximum(rel, 0) * d), -1))
    buckets = jnp.stack(buckets).astype(jnp.int32)

    def body(table_ref, b_ref, o_ref):
        h = pl.program_id(0)
        bv = b_ref[...]
        tile = jnp.full(bv.shape, NEG, F32)
        for bkt in range(N_BUCKETS):
            tile = jnp.where(bv == bkt, table_ref[h, bkt], tile)
        o_ref[...] = tile

    spec = pl.BlockSpec((None, BLK, 2 * BLK), lambda h: (h, 0, 0))
    tiles = pl.pallas_call(
        body, grid=(len(patterns),), in_specs=[pl.BlockSpec(memory_space=pltpu.SMEM), spec], out_specs=spec,
        out_shape=jax.ShapeDtypeStruct(buckets.shape, F32), name="band_tables", compiler_params=_params(("parallel",)),
    )(rel_bias.T, buckets)
    return tiles[:H_DIL], tiles[H_DIL:], buckets


def _swa_rows(t):
    t = t.reshape(N_BLK, BLK, H_SWA_KV, SWA_GROUP, HEAD_DIM).transpose(2, 0, 3, 1, 4)
    return t.reshape(H_SWA_KV, N_BLK, SWA_GROUP * BLK, HEAD_DIM)


def _swa_tokens(t):
    t = t.reshape(H_SWA_KV, N_BLK, SWA_GROUP, BLK, HEAD_DIM).transpose(1, 3, 0, 2, 4)
    return t.reshape(SEQ, H_SWA_Q * HEAD_DIM)


def _sink_rows(sinks):
    return jnp.broadcast_to(sinks.reshape(H_SWA_KV, SWA_GROUP, 1, 1), (H_SWA_KV, SWA_GROUP, BLK, 1)).reshape(
        H_SWA_KV, SWA_GROUP * BLK, 1)


def _no_sinks():
    return jnp.full((H_DIL, BLK, 1), NEG, F32)


def _vec(v):
    return v.reshape(1, D_MODEL)


class UnitRows:
    def __init__(self, u, mod_table, gain_table):
        self.shift, self.scale, self.gate = (Row(mod_table, 3 * u + t) for t in range(3))
        self.gain = Row(gain_table, u)


def ffn_forward(x, h, rows, then, w):
    a, b, s = ffn_up(h, w[0], w[1])
    f, xo, *h_next = ffn_down(s, w[2], x, rows.gate, then)
    return xo, (h_next or [None])[0], (x, h, a, b, s, f)


def ffn_backward(u, dxo, df, saved, rows, w, sums, below):
    x, h, a, b, s, _ = saved
    da, db = ffn_bwd_hidden(df, w[2], a, b)
    grads = ffn_grad_weights(h, s, df, da, db)
    dx, sums, *df_below = ffn_bwd_input(da, db, w[0], w[1], x, dxo, rows.gain, rows.scale, sums, u, below)
    return dx, sums, df_below, grads


def mixer_forward(x, h, rows, then, sinks, bias_dil, bias_swa, w):
    proj, qkv = in_proj(h, w[0])
    q_dil, k_dil, v_dil = _dilate(qkv[:, 768:1152] * QK_SCALE), _dilate(qkv[:, 1152:1536]), _dilate(qkv[:, 1536:1920])
    q_swa, k_swa, v_swa = _swa_rows(qkv[:, 1920:2304] * QK_SCALE), _heads(qkv[:, 2304:2432]), _heads(qkv[:, 2432:2560])
    o_sb, total_sb = sb_forward(qkv)
    q_dil = q_dil.reshape(H_DIL, N_BLK, BLK, HEAD_DIM)
    o_dd, lse_dd = banded_forward("dilated_forward", q_dil, k_dil, v_dil, bias_dil, _no_sinks(), DIL_HEADS_PER_STEP,
                                  _dil_prev_mask)
    o_dt, lse_dt = _undilate(o_dd.reshape(H_DIL, SEQ, HEAD_DIM)), _undilate(lse_dd.reshape(H_DIL, SEQ, 1))
    o_dil = dilated_merge(o_dt, lse_dt)
    bias_swa = bias_swa.reshape(H_SWA_KV, SWA_GROUP * BLK, 2 * BLK)
    o_swa, lse_swa = banded_forward("swa_forward", q_swa, k_swa, v_swa, bias_swa, _sink_rows(sinks), SWA_HEADS_PER_STEP,
                                    _swa_prev_mask)
    o_cat = jnp.concatenate([o_sb, _unheads(o_dil), _swa_tokens(o_swa)], axis=1)
    merged = merge_branches(o_cat, w[1], proj)
    mo, xo, *h_next = out_proj(merged, w[2], x, rows.gate, then)
    saved = (x, h, proj, (qkv, total_sb), (q_dil, k_dil, v_dil, o_dd, lse_dd, o_dt, lse_dt),
             (q_swa, k_swa, v_swa, o_swa, lse_swa), o_cat, merged, mo)
    return xo, (h_next or [None])[0], saved


def mixer_backward(u, dxo, dmo, saved, rows, sinks, bias_dil, bias_swa, w, sums, below):
    x, h, proj, sb, dil, swa, o_cat, merged, _ = saved
    tok = pl.BlockSpec((MM_TILE, D_MODEL), lambda j, k: (k, 0))
    g_out = grad_weight("grad_w_out", merged, pl.BlockSpec((MM_TILE, D_SHARD), lambda j, k: (k, j)), dmo, tok,
                        (D_SHARD, D_MODEL))
    du0, du1, du2, dg0, dg1, dg2 = merge_bwd(dmo, w[2], o_cat, w[1], proj)
    du = (du0, du1, du2)
    do_cat = branch_bwd_input(du, w[1])
    g_br = branch_grad_weights(o_cat, du)

    qkv, total_sb = sb
    dq_sb, dk_sb, dv_sb = sb_backward(qkv, total_sb, do_cat)

    q_dil, k_dil, v_dil, o_dd, lse_dd, o_dt, lse_dt = dil
    do_dt, dlse_dt = dilated_merge_bwd(o_dt, lse_dt, _heads(do_cat[:, 256:384]))
    dq_dil, dk_dil, dv_dil, dbias_dil, _ = banded_backward(
        "dilated_backward", q_dil, k_dil, v_dil, bias_dil, _no_sinks(), o_dd, lse_dd,
        _redilate(do_dt).reshape(q_dil.shape), _redilate(dlse_dt).reshape(lse_dd.shape), DIL_HEADS_PER_STEP, _dil_prev_mask)

    q_swa, k_swa, v_swa, o_swa, lse_swa = swa
    bias_swa = bias_swa.reshape(H_SWA_KV, SWA_GROUP * BLK, 2 * BLK)
    dq_swa, dk_swa, dv_swa, dbias_swa, dsinks = banded_backward(
        "swa_backward", q_swa, k_swa, v_swa, bias_swa, _sink_rows(sinks), o_swa, lse_swa, _swa_rows(do_cat[:, 384:768]),
        jnp.zeros_like(lse_swa), SWA_HEADS_PER_STEP, _swa_prev_mask)
    dbias_swa = dbias_swa.reshape(H_SWA_Q, BLK, 2 * BLK)

    def tokens(t):
        return _undilate(t).transpose(2, 0, 1, 3).reshape(SEQ, -1)

    dproj = jnp.concatenate(
        [dq_sb, dk_sb, dv_sb, tokens(dq_dil.reshape(H_DIL, SEQ, HEAD_DIM)), tokens(dk_dil),
         tokens(dv_dil), _swa_tokens(dq_swa), _unheads(dk_swa), _unheads(dv_swa)], axis=1).astype(BF16)
    dproj = jnp.concatenate([dproj, dg0, dg1, dg2], axis=1)
    g_in = grad_weight("grad_w_in", h, tok, dproj, pl.BlockSpec((MM_TILE, IN_SHARD), lambda j, k: (k, j)),
                       (D_MODEL, IN_SHARD))
    dx, sums, *df_below = mixer_bwd_input(dproj, w[0], x, dxo, rows.gain, rows.scale, sums, u, below)
    dbias = jnp.concatenate([dbias_dil, dbias_swa], axis=0)
    return dx, sums, df_below, dbias, dsinks[:, :, 0].reshape(H_SWA_Q), (g_in, g_br, g_out)


N_UNITS = 3 * DEPTH


def device_step(x, target, mod, gains, final_gain, sinks, rel_bias, get_weights, put_grads):
    bias_dil, bias_swa, bucket = band_tables(rel_bias)
    mod_table = mod.reshape(3 * N_UNITS, 1, D_MODEL)
    gain_table = gains.reshape(N_UNITS, 1, D_MODEL)
    saved, weights = [], []
    units = [UnitRows(u, mod_table, gain_table) for u in range(N_UNITS)]
    h = prenorm(x, units[0].gain, units[0].scale, units[0].shift)
    for u in range(N_UNITS):
        l, j = divmod(u, 3)
        w = get_weights(u, x)
        then = units[u + 1] if u + 1 < N_UNITS else None
        if j == 1:
            x, h, s = mixer_forward(x, h, units[u], then, sinks[l], bias_dil, bias_swa, w)
        else:
            x, h, s = ffn_forward(x, h, units[u], then, w)
        saved.append(s)
        weights.append(w)
    loss, dx, dfinal = final_loss(x, _vec(final_gain), target)

    sums = (lax.empty((8 * N_UNITS, D_MODEL), F32), lax.empty((8 * N_UNITS, D_MODEL), F32))
    dbias, dsinks = [None] * DEPTH, [None] * DEPTH
    zero = jnp.zeros((1, 1), F32)
    top = N_UNITS - 1
    df, gate_sums = resid_bwd(dx, saved[top][-1], units[top].gate, 0.5, sums[1], top)
    sums = (sums[0], gate_sums)
    for u in reversed(range(N_UNITS)):
        l, j = divmod(u, 3)
        rows = UnitRows(u, mod_table, gain_table + zero)
        below = (saved[u - 1][-1], units[u - 1].gate, 1.0 if (u - 1) % 3 == 1 else 0.5) if u > 0 else None
        if j == 1:
            dx, sums, df, dbias[l], dsinks[l], grads = mixer_backward(
                u, dx, df, saved[u], rows, sinks[l], bias_dil, bias_swa, weights[u], sums, below)
        else:
            dx, sums, df, grads = ffn_backward(u, dx, df, saved[u], rows, weights[u], sums, below)
        df = df[0] if df else None
        if u > 0:
            zero = put_grads(u, grads)
    drel = rel_bias_reduce(dbias[0], dbias[1], bucket)[:, 0, :N_BUCKETS].T
    norm_sums, gate_sums = (t.reshape(DEPTH, 3, 8, D_MODEL) for t in sums)
    dmod = jnp.stack([norm_sums[:, :, 0], norm_sums[:, :, 1], gate_sums[:, :, 0]], axis=2)
    return loss, dx, dmod, norm_sums[:, :, 2], dfinal[0], jnp.stack(dsinks), drel, grads


MESH = pl.DeviceIdType.MESH
CHIP_FLIPS = ((1, 0), (0, 1), (1, 1))
ANY = pl.BlockSpec(memory_space=pl.ANY)


def _position():
    return lax.axis_index("x"), lax.axis_index("y"), lax.axis_index("c")


def all_gather_small(name, piece):
    def body(x_ref, out_ref, send_sems, recv_sems, local_sem):
        x, y, c = _position()
        me, sibling = (x, y, c), (x, y, 1 - c)
        chips = [(x ^ fx, y ^ fy) for fx, fy in CHIP_FLIPS]

        def rows(px, py, pc):
            return out_ref.at[4 * px + 2 * py + pc]

        def copy(k, block, to, src=None):
            return pltpu.make_async_remote_copy(
                src_ref=rows(*block) if src is None else src, dst_ref=rows(*block),
                send_sem=send_sems.at[k], recv_sem=recv_sems.at[k], device_id=to, device_id_type=MESH)

        mine = pltpu.make_async_copy(x_ref, rows(*me), local_sem)
        mine.start()
        first = [copy(0, me, sibling, src=x_ref)]
        first += [copy(1 + j, me, (*chip, c), src=x_ref) for j, chip in enumerate(chips)]
        for cp in first:
            cp.start()
        passed = [copy(4 + j, (*chip, c), sibling) for j, chip in enumerate(chips)]
        for j, chip in enumerate(chips):
            copy(1 + j, (*chip, c), me).wait_recv()
            passed[j].start()
        copy(0, sibling, me).wait_recv()
        for j, chip in enumerate(chips):
            copy(4 + j, (*chip, 1 - c), me).wait_recv()
        for cp in first + passed:
            cp.wait_send()
        mine.wait()

    return pl.pallas_call(
        body, out_shape=jax.ShapeDtypeStruct((N_DEV,) + piece.shape, piece.dtype),
        in_specs=[pl.BlockSpec(memory_space=pltpu.VMEM)], out_specs=pl.BlockSpec(memory_space=pltpu.VMEM),
        scratch_shapes=[pltpu.SemaphoreType.DMA((7,)), pltpu.SemaphoreType.DMA((7,)), pltpu.SemaphoreType.DMA],
        name=name,
    )(piece)


def exchange(name, operands, out_shapes, aliases, plan):
    n_in, n_out = len(operands), len(out_shapes)

    def body(*refs):
        ins, outs = refs[:n_in], refs[n_in:n_in + n_out]
        send_sems, recv_sems, local_sems = refs[n_in + n_out:]
        x, y, c = _position()
        local, sends, recvs = plan(ins, outs, x, y, c)
        local = [pltpu.make_async_copy(s, d, local_sems.at[k]) for k, (s, d) in enumerate(local)]
        for cp in local:
            cp.start()
        remote = [pltpu.make_async_remote_copy(src_ref=s, dst_ref=d, send_sem=send_sems.at[k], recv_sem=recv_sems.at[k],
                                               device_id=dev, device_id_type=MESH)
                  for k, (s, d, dev) in enumerate(sends)]
        for cp in remote:
            cp.start()
        for k, r in enumerate(recvs):
            pltpu.make_async_remote_copy(src_ref=r, dst_ref=r, send_sem=send_sems.at[k], recv_sem=recv_sems.at[k],
                                         device_id=(x, y, c), device_id_type=MESH).wait_recv()
        for cp in remote:
            cp.wait_send()
        for cp in local:
            cp.wait()

    n_sends, n_local = plan.n_sends, max(plan.n_local, 1)
    return pl.pallas_call(
        body, out_shape=out_shapes, in_specs=[ANY] * n_in, out_specs=[ANY] * n_out,
        scratch_shapes=[pltpu.SemaphoreType.DMA((n_sends,)), pltpu.SemaphoreType.DMA((n_sends,)),
                        pltpu.SemaphoreType.DMA((n_local,))],
        input_output_aliases=aliases, name=name,
    )(*operands)


def _plan(n_local, n_sends):
    def wrap(fn):
        fn.n_local, fn.n_sends = n_local, n_sends
        return fn
    return wrap


def _half(ref, axis, c):
    rows = ref.shape[axis] // 2
    idx = [slice(None)] * len(ref.shape)
    idx[axis] = pl.ds(pl.multiple_of(c * rows, 16), rows)
    return ref.at[tuple(idx)]


HBM = pl.BlockSpec(memory_space=pltpu.HBM)
SEM = pl.BlockSpec(memory_space=pltpu.SEMAPHORE)
EFFECT = pltpu.SideEffectType.DATAFLOW_SIDE_EFFECTING


def split_start(name, bufs, extra, n_copies, describe):
    n = len(bufs)

    def body(*refs):
        send_sems, recv_sems = refs[n + len(extra)], refs[n + len(extra) + 1]
        x, y, c = _position()
        for k, (src, dst, _, peer) in enumerate(describe(refs[:n], x, y, c)):
            pltpu.make_async_remote_copy(src_ref=src, dst_ref=dst, send_sem=send_sems.at[k], recv_sem=recv_sems.at[k],
                                         device_id=peer, device_id_type=MESH).start()
        token = refs[-1]
        token[...] = jnp.zeros_like(token)

    out = pl.pallas_call(
        body, name=name,
        out_shape=(pltpu.SemaphoreType.DMA((n_copies,)), pltpu.SemaphoreType.DMA((n_copies,)),
                   *[pltpu.HBM(b.shape, b.dtype) for b in bufs], jax.ShapeDtypeStruct((8, 128), F32)),
        in_specs=[HBM] * n + [ANY] * len(extra),
        out_specs=(SEM, SEM, *[HBM] * n, pl.BlockSpec(memory_space=pltpu.VMEM)),
        input_output_aliases={k: 2 + k for k in range(n)},
        compiler_params=pltpu.CompilerParams(has_side_effects=EFFECT),
    )(*[pltpu.with_memory_space_constraint(b, pltpu.HBM) for b in bufs], *extra)
    return out[0], out[1], list(out[2:2 + n]), out[-1]


def split_wait(name, bufs, send_sems, recv_sems, after, describe):
    n = len(bufs)

    def body(*refs):
        send, recv = refs[n], refs[n + 1]
        x, y, c = _position()
        for k, (src, _, dst, peer) in enumerate(describe(refs[:n], x, y, c)):
            copy = pltpu.make_async_remote_copy(src_ref=src, dst_ref=dst, send_sem=send.at[k], recv_sem=recv.at[k],
                                                device_id=peer, device_id_type=MESH)
            copy.wait_send()
            copy.wait_recv()

    out = pl.pallas_call(
        body, name=name, out_shape=[pltpu.HBM(b.shape, b.dtype) for b in bufs],
        in_specs=[HBM] * n + [SEM, SEM] + [ANY] * len(after), out_specs=[HBM] * n,
        input_output_aliases={k: k for k in range(n)},
        compiler_params=pltpu.CompilerParams(has_side_effects=EFFECT),
    )(*bufs, send_sems, recv_sems, *after)
    return list(out)


def _row_tile(rows, cols, max_elements=256 * 1024):
    best = 16
    for t in range(16, rows + 1, 16):
        if rows % t == 0 and t * cols <= max_elements:
            best = t
    return best


def cast_into_slot(name, param, index, chip):
    rows, cols = param.shape[-2:]
    tr = _row_tile(rows, cols)
    lead = (None,) * len(index)

    def body(chip_ref, s_ref, o_ref):
        del chip_ref
        o_ref[...] = s_ref[...].astype(BF16)

    return pl.pallas_call(
        body, out_shape=jax.ShapeDtypeStruct((N_CHIPS, rows, cols), BF16),
        grid_spec=pltpu.PrefetchScalarGridSpec(
            num_scalar_prefetch=1, grid=(rows // tr,),
            in_specs=[pl.BlockSpec(lead + (tr, cols), lambda r, chip_ref: index + (r, 0))],
            out_specs=pl.BlockSpec((None, tr, cols), lambda r, chip_ref: (chip_ref[0], r, 0))),
        name=name, compiler_params=_params(("parallel",)),
    )(chip, param)


GATHER_STAGES = ((0,), (1,), (2,), (3, 4, 5))
REDUCE_STAGES = ((5, 4, 3), (2,), (1,), (0,))


def _gather_copies(slots, x, y, c):
    me = 2 * x + y
    out = []
    for s in slots:
        for fx, fy in CHIP_FLIPS:
            mine = _half(s.at[me], 0, c)
            out.append((mine, mine, _half(s.at[2 * (x ^ fx) + (y ^ fy)], 0, c), (x ^ fx, y ^ fy, c)))
    return out


class WeightStream:
    def __init__(self, shards, chip, after=()):
        self.pending, self.ready = {}, {}
        token = tuple(after)
        for si, units in enumerate(GATHER_STAGES):
            slots = [cast_into_slot(f"cast_{u}_{t}", p, idx, chip) for u in units for t, (p, idx) in enumerate(shards[u])]
            send, recv, slots, tok = split_start(f"gather_start_{si}", slots, token, 3 * len(slots), _gather_copies)
            self.pending[si] = (send, recv, slots)
            token = (tok,)
        self.token = token

    def get(self, u, after):
        if u not in self.ready:
            si = next(k for k, units in enumerate(GATHER_STAGES) if u in units)
            send, recv, slots = self.pending.pop(si)
            slots = split_wait(f"gather_wait_{si}", slots, send, recv, (after,) + self.token, _gather_copies)
            self.token = ()

            @_plan(0, 3 * len(slots))
            def to_sibling(ins, outs, x, y, c):
                sends, recvs = [], []
                for o in outs:
                    for fx, fy in CHIP_FLIPS:
                        slab = o.at[2 * (x ^ fx) + (y ^ fy)]
                        sends.append((_half(slab, 0, c), _half(slab, 0, c), (x, y, 1 - c)))
                        recvs.append(_half(slab, 0, 1 - c))
                return [], sends, recvs

            shapes = [jax.ShapeDtypeStruct(s.shape, BF16) for s in slots]
            slots = exchange(f"gather_sibling_{si}", slots, shapes, {k: k for k in range(len(slots))}, to_sibling)
            for i, v in enumerate(GATHER_STAGES[si]):
                self.ready[v] = tuple(slots[3 * i:3 * i + 3])
        return self.ready[u]


def _reduce_copies(bufs, x, y, c):
    n = len(bufs) // 2
    out = []
    for s, land in zip(bufs[:n], bufs[n:]):
        for k, (fx, fy) in enumerate(CHIP_FLIPS):
            out.append((s.at[2 * (x ^ fx) + (y ^ fy)], land.at[k], land.at[k], (x ^ fx, y ^ fy, c)))
    return out


GRAD_SLOTS = {"gate": (2 * DEPTH, FF_SHARD, D_MODEL), "up": (2 * DEPTH, FF_SHARD, D_MODEL),
              "down": (2 * DEPTH, FF_SHARD, D_MODEL), "in": (DEPTH, D_MODEL, IN_SHARD),
              "br": (DEPTH, BR_ROWS, D_SHARD), "out": (DEPTH, D_SHARD, D_MODEL)}


def _unit_tensors(u):
    l, j = divmod(u, 3)
    if j == 1:
        return [("in", l), ("br", l), ("out", l)]
    return [(k, 2 * l + j // 2) for k in ("gate", "up", "down")]


class GradStream:
    def __init__(self, chip, core):
        self.core = core
        self.place = jnp.concatenate([chip, core])
        self.held, self.flying = {}, []
        self.full = {k: lax.empty(shape, F32) for k, shape in GRAD_SLOTS.items()}

    def put(self, u, grads, after=()):
        self.held[u] = grads
        si = len(self.flying)
        units = REDUCE_STAGES[si]
        if not all(v in self.held for v in units):
            return jnp.zeros((1, 1), F32)
        gs = [g for v in units for g in self.held[v]]

        @_plan(0, len(gs))
        def swap_halves(ins, outs, x, y, c):
            sends = [(_half(g, 1, 1 - c), o, (x, y, 1 - c)) for g, o in zip(ins, outs)]
            return [], sends, list(outs)

        half_shapes = [jax.ShapeDtypeStruct((N_CHIPS, g.shape[1] // 2, g.shape[2]), BF16) for g in gs]
        landed = exchange(f"reduce_swap_{si}", gs + list(after), half_shapes, {}, swap_halves)
        sums = [_add_halves(g, la, self.core) for g, la in zip(gs, landed)]
        landing = [lax.empty((3,) + s.shape[1:], BF16) for s in sums]
        send, recv, bufs, token = split_start(f"reduce_start_{si}", sums + landing, (), 3 * len(sums), _reduce_copies)
        self.flying.append((send, recv, bufs, [t for v in units for t in _unit_tensors(v)]))
        return token[0:1, 0:1]

    def finish(self, after):
        for si, (send, recv, bufs, tensors) in enumerate(self.flying):
            bufs = split_wait(f"reduce_wait_{si}", bufs, send, recv, tuple(after), _reduce_copies)
            n = len(tensors)
            for (name, slot), s, land in zip(tensors, bufs[:n], bufs[n:]):
                self.full[name] = _add_chips(s, land, self.place, self.full[name], slot)
        names = list(self.full)

        @_plan(0, len(names))
        def share_halves(ins, outs, x, y, c):
            sends = [(_half(o, 1, c), _half(o, 1, c), (x, y, 1 - c)) for o in outs]
            return [], sends, [_half(o, 1, 1 - c) for o in outs]

        shapes = [jax.ShapeDtypeStruct(self.full[k].shape, F32) for k in names]
        out = exchange("reduce_share_halves", [self.full[k] for k in names], shapes, {k: k for k in range(len(names))},
                       share_halves)
        return dict(zip(names, out))


def _add_halves(g, landed, core):
    _, rh, cols = landed.shape
    tr = _row_tile(rh, cols, 1024 * 1024)
    per_half = rh // tr

    def body(core_ref, g_ref, la_ref, o_ref):
        del core_ref
        o_ref[...] = (g_ref[...].astype(F32) + la_ref[...].astype(F32)).astype(BF16)

    blk = (None, tr, cols)
    return pl.pallas_call(
        body, out_shape=jax.ShapeDtypeStruct(landed.shape, BF16),
        grid_spec=pltpu.PrefetchScalarGridSpec(
            num_scalar_prefetch=1, grid=(N_CHIPS, per_half),
            in_specs=[pl.BlockSpec(blk, lambda j, r, core_ref: (j, core_ref[0] * per_half + r, 0)),
                      pl.BlockSpec(blk, lambda j, r, core_ref: (j, r, 0))],
            out_specs=pl.BlockSpec(blk, lambda j, r, core_ref: (j, r, 0))),
        name="reduce_add_halves", compiler_params=_params(("parallel", "parallel")),
    )(core, g, landed)


def _add_chips(sums, landed, place, full, slot):
    _, rh, cols = sums.shape
    tr = _row_tile(rh, cols, 1024 * 1024)
    per_half = rh // tr

    def body(place_ref, s_ref, la_ref, full_in, o_ref):
        del place_ref, full_in
        o_ref[...] = ((s_ref[...].astype(F32) + la_ref[0].astype(F32)) + la_ref[1].astype(F32)) + la_ref[2].astype(F32)

    return pl.pallas_call(
        body, out_shape=jax.ShapeDtypeStruct(full.shape, F32),
        grid_spec=pltpu.PrefetchScalarGridSpec(
            num_scalar_prefetch=1, grid=(per_half,),
            in_specs=[pl.BlockSpec((None, tr, cols), lambda r, place_ref: (place_ref[0], r, 0)),
                      pl.BlockSpec((3, tr, cols), lambda r, place_ref: (0, r, 0)), ANY],
            out_specs=pl.BlockSpec((None, tr, cols), lambda r, place_ref: (slot, place_ref[1] * per_half + r, 0))),
        input_output_aliases={3: 0}, name="reduce_add_chips", compiler_params=_params(("parallel",)),
    )(place, sums, landed, full)


def sum_devices(parts):
    def body(p_ref, o_ref):
        acc = p_ref[0]
        for d in range(1, N_DEV):
            acc = acc + p_ref[d]
        o_ref[...] = acc

    return pl.pallas_call(body, out_shape=jax.ShapeDtypeStruct(parts.shape[1:], F32), name="sum_devices")(parts)


ADA_SHARD = 9 * D_MODEL // N_CHIPS
ADA_TILE = 768
ADA_ROWS = 16


def ada_forward(c_rows, w_ada, b_shard):
    def body(c_ref, w_ref, b_ref, o_ref):
        cv = c_ref[...]
        o_ref[...] = _dot((cv * _sigmoid(cv)).astype(BF16), w_ref[...].astype(BF16), NN) + b_ref[...]

    return pl.pallas_call(
        body, grid=(DEPTH, ADA_SHARD // ADA_TILE),
        in_specs=[pl.BlockSpec((ADA_ROWS, D_MODEL), lambda l, n: (0, 0)),
                  pl.BlockSpec((None, D_MODEL, ADA_TILE), lambda l, n: (l, 0, n)),
                  pl.BlockSpec((None, 1, ADA_TILE), lambda l, n: (l, 0, n))],
        out_specs=pl.BlockSpec((None, ADA_ROWS, ADA_TILE), lambda l, n: (l, 0, n)),
        out_shape=jax.ShapeDtypeStruct((DEPTH, ADA_ROWS, ADA_SHARD), F32),
        name="ada_forward", compiler_params=_params(("parallel", "parallel")),
    )(c_rows, w_ada, b_shard)


def ada_backward(c_rows, dmod_rows):
    def body(c_ref, d_ref, o_ref):
        cv = c_ref[...]
        o_ref[...] = _dot((cv * _sigmoid(cv)).astype(BF16), d_ref[...].astype(BF16), TN)

    return pl.pallas_call(
        body, grid=(DEPTH, ADA_SHARD // ADA_TILE),
        in_specs=[pl.BlockSpec((ADA_ROWS, D_MODEL), lambda l, n: (0, 0)),
                  pl.BlockSpec((None, ADA_ROWS, ADA_TILE), lambda l, n: (l, 0, n))],
        out_specs=pl.BlockSpec((None, D_MODEL, ADA_TILE), lambda l, n: (l, 0, n)),
        out_shape=jax.ShapeDtypeStruct((DEPTH, D_MODEL, ADA_SHARD), F32),
        name="ada_backward", compiler_params=_params(("parallel", "parallel")),
    )(c_rows, dmod_rows)


def adamw(name, w, g, m, v):
    shape = w.shape
    cols = shape[-1]
    rows = w.size // cols
    tr = _row_tile(rows, cols) if rows % 16 == 0 else rows
    c1 = 1.0 / (1.0 - ADAM_B1 ** ADAM_STEP)
    c2 = 1.0 / (1.0 - ADAM_B2 ** ADAM_STEP)

    def body(w_ref, g_ref, m_ref, v_ref, go_ref, d_ref, mo_ref, vo_ref):
        gv = g_ref[...]
        mn = ADAM_B1 * m_ref[...] + (1.0 - ADAM_B1) * gv
        vn = ADAM_B2 * v_ref[...] + (1.0 - ADAM_B2) * (gv * gv)
        go_ref[...] = gv
        mo_ref[...] = mn
        vo_ref[...] = vn
        d_ref[...] = -ADAM_LR * ((mn * c1) / (jnp.sqrt(vn * c2) + ADAM_EPS) + ADAM_WD * w_ref[...])

    spec = pl.BlockSpec((tr, cols), lambda i: (i, 0))
    out = jax.ShapeDtypeStruct((rows, cols), F32)
    res = pl.pallas_call(
        body, grid=(rows // tr,), in_specs=[spec] * 4, out_specs=[spec] * 4, out_shape=[out] * 4,
        name=name, compiler_params=_params(("parallel",)),
    )(*[t.reshape(rows, cols) for t in (w, g, m, v)])
    return tuple(r.reshape(shape) for r in res)


def _pack(parts, rows):
    flat = jnp.concatenate([p.reshape(-1) for p in parts])
    return jnp.pad(flat, (0, rows * 128 - flat.size)).reshape(rows, 128)


def _unpack(flat, shapes):
    out, at = [], 0
    for s in shapes:
        n = math.prod(s)
        out.append(flat[at:at + n].reshape(s))
        at += n
    return out


def kernel(x, c, w_ada, b_ada, norm_gain, w_ffn_gate, w_ffn_up, w_ffn_down, w_in, w_br_sb, w_br_dil, w_br_swa, w_out, sinks, rel_bias, final_gain, loss_target, m_w_ada, m_b_ada, m_norm_gain, m_w_ffn_gate, m_w_ffn_up, m_w_ffn_down, m_w_in, m_w_br_sb, m_w_br_dil, m_w_br_swa, m_w_out, m_sinks, m_rel_bias, m_final_gain, v_w_ada, v_b_ada, v_norm_gain, v_w_ffn_gate, v_w_ffn_up, v_w_ffn_down, v_w_in, v_w_br_sb, v_w_br_dil, v_w_br_swa, v_w_out, v_sinks, v_rel_bias, v_final_gain):
    xi, yi, ci = _position()
    chip = 2 * xi + yi
    dev = 2 * chip + ci

    c_all = all_gather_small("gather_c", c.reshape(8, 128)).reshape(N_DEV, D_MODEL)
    c_rows = jnp.pad(c_all, ((0, ADA_ROWS - N_DEV), (0, 0)))
    b_shard = lax.dynamic_slice_in_dim(b_ada, chip * ADA_SHARD, ADA_SHARD, axis=1).reshape(DEPTH, 1, ADA_SHARD)
    mod_shard = ada_forward(c_rows, w_ada, b_shard)[:, :N_DEV]
    n_mod = DEPTH * N_DEV * ADA_SHARD
    gathered = all_gather_small("gather_mod", _pack([mod_shard, norm_gain], 304))[::2].reshape(N_CHIPS, -1)
    mod_all = gathered[:, :n_mod].reshape(N_CHIPS, DEPTH, N_DEV, ADA_SHARD)
    mod = lax.dynamic_index_in_dim(mod_all, dev, axis=2, keepdims=False)
    mod = mod.transpose(1, 0, 2).reshape(DEPTH, 3, 3, D_MODEL)
    gains = gathered[:, n_mod:n_mod + DEPTH * 3 * D_SHARD].reshape(N_CHIPS, DEPTH, 3, D_SHARD)
    gains = gains.transpose(1, 2, 0, 3).reshape(DEPTH, 3, D_MODEL)

    chip_i, core_i = chip.astype(jnp.int32).reshape(1), ci.astype(jnp.int32).reshape(1)
    w_br = jnp.concatenate([w_br_sb, w_br_dil, w_br_swa], axis=1)
    transposed = (3, 4)
    w_gate_t, w_up_t = jnp.swapaxes(w_ffn_gate, 2, 3), jnp.swapaxes(w_ffn_up, 2, 3)
    shards = []
    for l in range(DEPTH):
        ffn = [[(w_gate_t, (l, f)), (w_up_t, (l, f)), (w_ffn_down, (l, f))] for f in range(2)]
        shards += [ffn[0], [(w_in, (l,)), (w_br, (l,)), (w_out, (l,))], ffn[1]]
    weights_in = WeightStream(shards, chip_i, (gathered,))
    grads_out = GradStream(chip_i, core_i)

    loss, dx, dmod, dgains, dfinal, dsinks, drel, last_grads = device_step(
        x[0], loss_target[0], mod, gains, final_gain, sinks, rel_bias, weights_in.get, grads_out.put)

    small_shapes = [(DEPTH, 9 * D_MODEL), (DEPTH, 3, D_MODEL), (D_MODEL,), (DEPTH, H_SWA_Q), (N_BUCKETS, 12), (1,)]
    small_all = all_gather_small("gather_small_grads", _pack([dmod, dgains, dfinal, dsinks, drel, loss[0, 0:1]], 208))
    started = grads_out.put(0, last_grads, after=(small_all,))
    small_all = small_all + started
    g_b_ada, g_gain_full, g_final, g_sinks, g_rel, loss_sum = _unpack(sum_devices(small_all).reshape(-1), small_shapes)
    g_gain = lax.dynamic_slice_in_dim(g_gain_full, chip * D_SHARD, D_SHARD, axis=2)
    dmod_all = small_all.reshape(N_DEV, -1)[:, :DEPTH * 9 * D_MODEL].reshape(N_DEV, DEPTH, 9 * D_MODEL)
    dmod_rows = lax.dynamic_slice_in_dim(dmod_all, chip * ADA_SHARD, ADA_SHARD, axis=2).transpose(1, 0, 2)
    g_w_ada = ada_backward(c_rows, jnp.pad(dmod_rows, ((0, 0), (0, ADA_ROWS - N_DEV), (0, 0))))

    weights = [w_ada, b_ada, norm_gain, w_ffn_gate, w_ffn_up, w_ffn_down, w_in, w_br_sb, w_br_dil, w_br_swa, w_out,
               sinks, rel_bias, final_gain]
    ms = [m_w_ada, m_b_ada, m_norm_gain, m_w_ffn_gate, m_w_ffn_up, m_w_ffn_down, m_w_in, m_w_br_sb, m_w_br_dil,
          m_w_br_swa, m_w_out, m_sinks, m_rel_bias, m_final_gain]
    vs = [v_w_ada, v_b_ada, v_norm_gain, v_w_ffn_gate, v_w_ffn_up, v_w_ffn_down, v_w_in, v_w_br_sb, v_w_br_dil,
          v_w_br_swa, v_w_out, v_sinks, v_rel_bias, v_final_gain]
    grads = [g_w_ada, g_b_ada, g_gain] + [None] * 8 + [g_sinks, g_rel, g_final]

    small = (1, 2, 11, 12, 13)
    deltas, new_ms, new_vs = [None] * 14, [None] * 14, [None] * 14
    _, deltas[0], new_ms[0], new_vs[0] = adamw("adamw_0", weights[0], grads[0], ms[0], vs[0])
    shapes = [weights[k].shape for k in small]
    packed = [_pack([t[k] for k in small], 168) for t in (weights, grads, ms, vs)]
    for dst, res in zip((deltas, new_ms, new_vs), adamw("adamw_small", *packed)[1:]):
        for k, t in zip(small, _unpack(res.reshape(-1), shapes)):
            dst[k] = t

    g = grads_out.finish((dx, deltas[0], deltas[1]))
    g_br = g["br"]
    grads[3:11] = [g["gate"].reshape(w_gate_t.shape), g["up"].reshape(w_up_t.shape),
                   g["down"].reshape(w_ffn_down.shape), g["in"], g_br[:, 0:256], g_br[:, 256:384], g_br[:, 384:768],
                   g["out"]]
    for k in range(3, 11):
        state = [weights[k], ms[k], vs[k]]
        if k in transposed:
            state = [jnp.swapaxes(t, 2, 3) for t in state]
        out = adamw(f"adamw_{k}", state[0], grads[k], state[1], state[2])
        if k in transposed:
            out = [jnp.swapaxes(t, 2, 3) for t in out]
        grads[k], deltas[k], new_ms[k], new_vs[k] = out
    return (loss_sum[0], dx[None], *grads, *deltas, *new_ms, *new_vs)
```

```python
import functools
import math

import jax
import jax.numpy as jnp
from jax import lax
from jax.experimental import pallas as pl
from jax.experimental.pallas import tpu as pltpu

F32 = jnp.float32
BF16 = jnp.bfloat16

D_MODEL = 1024
SEQ = 2048
DEPTH = 2
HEAD_DIM = 64
BLK = 128
H_SB = 4
DIL_PATTERNS = ((128, 1), (512, 4), (2048, 16))
H_PER_DIL = 2
H_DIL = 6
H_SWA_Q = 6
H_SWA_KV = 2
SWA_WINDOW = 128
N_BUCKETS = 32
MAX_REL_DIST = 2048
D_FF = 2816
RMS_EPS = 1e-6
N_CHIPS = 4
N_DEV = 8
FF_SHARD = D_FF // N_CHIPS
D_QKV = 2560
D_IN = D_QKV + 3 * D_MODEL
IN_SHARD = D_IN // N_CHIPS
D_SHARD = D_MODEL // N_CHIPS
BR_ROWS = 768
NEG = -1e30
QK_SCALE = HEAD_DIM ** -0.5

ADAM_LR = 0.001
ADAM_B1 = 0.9
ADAM_B2 = 0.999
ADAM_EPS = 1e-08
ADAM_WD = 0.01
ADAM_STEP = 10

VMEM_LIMIT = 56 * 1024 * 1024
ROW_TILE = 256
MM_TILE = 1024

NN = (((1,), (0,)), ((), ()))
NT = (((1,), (1,)), ((), ()))
TN = (((0,), (0,)), ((), ()))


def _params(sem=None):
    return pltpu.CompilerParams(dimension_semantics=sem, vmem_limit_bytes=VMEM_LIMIT)


def _dot(a, b, dims):
    return lax.dot_general(a, b, dims, preferred_element_type=F32)


def _sigmoid(x):
    return 1.0 / (1.0 + jnp.exp(-x))


def _matmul(name, grid, nk, k_axis, dims, n_pairs, in_specs, out_specs, out_shape, acc_shape, epilogue,
            operands, sem, aliases=None, prologue=None):
    n_in = len(in_specs)
    n_out = len(out_specs)

    def partial(ins):
        tot = None
        for p in range(n_pairs):
            a = ins[2 * p][...]
            if prologue is not None:
                a = prologue(p, a, ins)
            d = _dot(a, ins[2 * p + 1][...], dims)
            tot = d if tot is None else tot + d
        return tot

    def body(*refs):
        ins, outs = refs[:n_in], refs[n_in:n_in + n_out]
        ids = tuple(pl.program_id(a) for a in range(len(grid)))
        if nk == 1:
            epilogue(partial(ins), ins, outs, ids)
            return
        acc = refs[n_in + n_out]
        k = ids[k_axis]

        @pl.when(k == 0)
        def _():
            acc[...] = partial(ins)

        @pl.when(k > 0)
        def _():
            acc[...] += partial(ins)

        @pl.when(k == nk - 1)
        def _():
            epilogue(acc[...], ins, outs, ids)

    return pl.pallas_call(
        body, grid=grid, in_specs=in_specs, out_specs=out_specs, out_shape=out_shape,
        scratch_shapes=[] if nk == 1 else [pltpu.VMEM(acc_shape, F32)],
        input_output_aliases=aliases or {}, name=name, compiler_params=_params(sem),
    )(*operands)


def _row_spec(width=D_MODEL):
    return pl.BlockSpec((ROW_TILE, width), lambda i: (i, 0))


def _vec_spec(rows=1, width=D_MODEL):
    return pl.BlockSpec((rows, width), lambda i: (0, 0))


class Row:
    def __init__(self, table, index):
        self.table, self.index = table, index

    def spec(self):
        index = self.index
        return pl.BlockSpec((None, 1, D_MODEL), lambda *ids: (index, 0, 0))


def _slot_spec(u):
    return pl.BlockSpec((8, D_MODEL), lambda *ids: (u, 0))


def prenorm(x, gain, scale, shift):
    def body(x_ref, g_ref, sc_ref, sh_ref, h_ref):
        xv = x_ref[...]
        r = lax.rsqrt(jnp.mean(xv * xv, axis=-1, keepdims=True) + RMS_EPS)
        h_ref[...] = (((xv * r) * g_ref[...]) * (1.0 + sc_ref[...]) + sh_ref[...]).astype(BF16)

    return pl.pallas_call(
        body, grid=(SEQ // ROW_TILE,), in_specs=[_row_spec(), gain.spec(), scale.spec(), shift.spec()],
        out_specs=_row_spec(), out_shape=jax.ShapeDtypeStruct((SEQ, D_MODEL), BF16),
        name="prenorm", compiler_params=_params(("parallel",)),
    )(x, gain.table, scale.table, shift.table)


def resid_bwd(dxo, f, coef, mult, sums, u):
    def body(dx_ref, f_ref, c_ref, sums_in, df_ref, dc_ref):
        del sums_in
        dx = dx_ref[...]
        df_ref[...] = (dx * (mult * c_ref[...])).astype(BF16)
        part = mult * jnp.sum(dx * f_ref[...], axis=0, keepdims=True)

        @pl.when(pl.program_id(0) == 0)
        def _():
            dc_ref[...] = jnp.zeros_like(dc_ref)

        dc_ref[0:1, :] += part

    return pl.pallas_call(
        body, grid=(SEQ // ROW_TILE,),
        in_specs=[_row_spec(), _row_spec(), coef.spec(), pl.BlockSpec(memory_space=pl.ANY)],
        out_specs=[_row_spec(), _slot_spec(u)],
        out_shape=[jax.ShapeDtypeStruct((SEQ, D_MODEL), BF16), jax.ShapeDtypeStruct(sums.shape, F32)],
        input_output_aliases={3: 1}, name="resid_bwd", compiler_params=_params(("arbitrary",)),
    )(dxo, f, coef.table, sums)


def final_loss(x, gain, target):
    def body(x_ref, g_ref, t_ref, loss_ref, dx_ref, dg_ref):
        xv = x_ref[...]
        g = g_ref[...]
        r = lax.rsqrt(jnp.mean(xv * xv, axis=-1, keepdims=True) + RMS_EPS)
        xh = xv * r
        e = xh * g - t_ref[...]
        part = 0.5 * jnp.sum(jnp.mean(e * e, axis=-1, keepdims=True), axis=0, keepdims=True)
        dy = e * (1.0 / D_MODEL)
        dyg = dy * g
        dx_ref[...] = r * (dyg - xh * jnp.mean(dyg * xh, axis=-1, keepdims=True))

        @pl.when(pl.program_id(0) == 0)
        def _():
            loss_ref[...] = jnp.zeros_like(loss_ref)
            dg_ref[...] = jnp.zeros_like(dg_ref)

        loss_ref[...] += jnp.broadcast_to(part, loss_ref.shape)
        dg_ref[0:1, :] += jnp.sum(dy * xh, axis=0, keepdims=True)

    return pl.pallas_call(
        body, grid=(SEQ // ROW_TILE,), in_specs=[_row_spec(), _vec_spec(), _row_spec()],
        out_specs=[_vec_spec(8, 128), _row_spec(), _vec_spec(8)],
        out_shape=[jax.ShapeDtypeStruct((8, 128), F32), jax.ShapeDtypeStruct((SEQ, D_MODEL), F32),
                   jax.ShapeDtypeStruct((8, D_MODEL), F32)],
        name="final_loss", compiler_params=_params(("arbitrary",)),
    )(x, gain, target)


def _prenorm_bwd_epilogue(dh, x_ref, dxo_ref, g_ref, sc_ref, dx_ref, stats_ref, first):
    xv = x_ref[...]
    g = g_ref[...]
    r = lax.rsqrt(jnp.mean(xv * xv, axis=-1, keepdims=True) + RMS_EPS)
    xh = xv * r
    dn = dh * (1.0 + sc_ref[...])
    dxh = dn * g
    dx = dxo_ref[...] + r * (dxh - xh * jnp.mean(dxh * xh, axis=-1, keepdims=True))
    dx_ref[...] = dx

    @pl.when(first)
    def _():
        stats_ref[...] = jnp.zeros_like(stats_ref)

    stats_ref[0:1, :] += jnp.sum(dh, axis=0, keepdims=True)
    stats_ref[1:2, :] += jnp.sum(dh * (xh * g), axis=0, keepdims=True)
    stats_ref[2:3, :] += jnp.sum(dn * xh, axis=0, keepdims=True)
    return dx


def _resid_bwd_epilogue(dx, f_ref, c_ref, mult, df_ref, dc_ref, first):
    df_ref[...] = (dx * (mult * c_ref[...])).astype(BF16)

    @pl.when(first)
    def _():
        dc_ref[...] = jnp.zeros_like(dc_ref)

    dc_ref[0:1, :] += mult * jnp.sum(dx * f_ref[...], axis=0, keepdims=True)


def ffn_up(h, wg_all, wu_all):
    def body(h_ref, wg_ref, wu_ref, a_ref, b_ref, s_ref):
        hv = h_ref[...]
        a = _dot(hv, wg_ref[...], NT)
        b = _dot(hv, wu_ref[...], NT)
        a_ref[...] = a.astype(BF16)
        b_ref[...] = b.astype(BF16)
        s_ref[...] = (a * _sigmoid(a) * b).astype(BF16)

    w_spec = pl.BlockSpec((None, FF_SHARD, D_MODEL), lambda j, i: (j, 0, 0))
    o_spec = pl.BlockSpec((None, MM_TILE, FF_SHARD), lambda j, i: (j, i, 0))
    hid = (N_CHIPS, SEQ, FF_SHARD)
    return pl.pallas_call(
        body, grid=(N_CHIPS, SEQ // MM_TILE),
        in_specs=[pl.BlockSpec((MM_TILE, D_MODEL), lambda j, i: (i, 0)), w_spec, w_spec],
        out_specs=[o_spec, o_spec, o_spec],
        out_shape=[jax.ShapeDtypeStruct(hid, BF16)] * 3,
        name="ffn_up", compiler_params=_params(("parallel", "parallel")),
    )(h, wg_all, wu_all)


def matmul_residual(name, a, a_spec, w_all, w_spec, x, coef, mult, then=None):
    def epilogue(acc, ins, outs, ids):
        outs[0][...] = acc
        xo = ins[2][...] + (mult * ins[3][...]) * acc
        outs[1][...] = xo
        if then is not None:
            r = lax.rsqrt(jnp.mean(xo * xo, axis=-1, keepdims=True) + RMS_EPS)
            outs[2][...] = (((xo * r) * ins[4][...]) * (1.0 + ins[5][...]) + ins[6][...]).astype(BF16)

    row = pl.BlockSpec((MM_TILE, D_MODEL), lambda i, j: (i, 0))
    f32 = jax.ShapeDtypeStruct((SEQ, D_MODEL), F32)
    extra = [] if then is None else [then.gain, then.scale, then.shift]
    return _matmul(
        name, (SEQ // MM_TILE, N_CHIPS), N_CHIPS, 1, NN, 1,
        [a_spec, w_spec, row, coef.spec()] + [t.spec() for t in extra], [row] * (2 + bool(extra)),
        [f32, f32] + [jax.ShapeDtypeStruct((SEQ, D_MODEL), BF16)] * bool(extra), (MM_TILE, D_MODEL), epilogue,
        (a, w_all, x, coef.table) + tuple(t.table for t in extra), ("parallel", "arbitrary"))


def ffn_down(s, wd_all, x, gate, then):
    return matmul_residual(
        "ffn_down", s, pl.BlockSpec((None, MM_TILE, FF_SHARD), lambda i, j: (j, i, 0)),
        wd_all, pl.BlockSpec((None, FF_SHARD, D_MODEL), lambda i, j: (j, 0, 0)), x, gate, 0.5, then)


def ffn_bwd_hidden(df, wd_all, a, b):
    def epilogue(ds, ins, outs, ids):
        av, bv = ins[2][...].astype(F32), ins[3][...].astype(F32)
        sig = _sigmoid(av)
        outs[0][...] = (ds * bv * (sig * (1.0 + av * (1.0 - sig)))).astype(BF16)
        outs[1][...] = (ds * (av * sig)).astype(BF16)

    hid_spec = pl.BlockSpec((None, MM_TILE, FF_SHARD), lambda j, i: (j, i, 0))
    hid = jax.ShapeDtypeStruct((N_CHIPS, SEQ, FF_SHARD), BF16)
    return _matmul(
        "ffn_bwd_hidden", (N_CHIPS, SEQ // MM_TILE), 1, None, NT, 1,
        [pl.BlockSpec((MM_TILE, D_MODEL), lambda j, i: (i, 0)),
         pl.BlockSpec((None, FF_SHARD, D_MODEL), lambda j, i: (j, 0, 0)), hid_spec, hid_spec],
        [hid_spec, hid_spec], [hid, hid], None, epilogue, (df, wd_all, a, b), ("parallel", "parallel"))


def grad_weight(name, lhs, lhs_spec, rhs, rhs_spec, shape):
    def epilogue(acc, ins, outs, ids):
        outs[0][...] = acc.astype(BF16)

    return _matmul(
        name, (N_CHIPS, SEQ // MM_TILE), SEQ // MM_TILE, 1, TN, 1,
        [lhs_spec, rhs_spec], [pl.BlockSpec((None,) + shape, lambda j, k: (j, 0, 0))],
        [jax.ShapeDtypeStruct((N_CHIPS,) + shape, BF16)], shape, epilogue, (lhs, rhs), ("parallel", "arbitrary"))[0]


def ffn_grad_weights(h, s, df, da, db):
    tok = pl.BlockSpec((MM_TILE, D_MODEL), lambda j, k: (k, 0))
    hid = pl.BlockSpec((None, MM_TILE, FF_SHARD), lambda j, k: (j, k, 0))
    return (grad_weight("grad_w_gate", da, hid, h, tok, (FF_SHARD, D_MODEL)),
            grad_weight("grad_w_up", db, hid, h, tok, (FF_SHARD, D_MODEL)),
            grad_weight("grad_w_down", s, hid, df, tok, (FF_SHARD, D_MODEL)))


BWD_TILE = 512


def matmul_prenorm_bwd(name, dims, pairs, pair_specs, x, dxo, gain, scale, sums, u, below):
    n = len(pairs)

    def epilogue(dh, ins, outs, ids):
        first = ids[0] == 0
        dx = _prenorm_bwd_epilogue(dh, ins[n], ins[n + 1], ins[n + 2], ins[n + 3], outs[0], outs[1], first)
        if below is not None:
            _resid_bwd_epilogue(dx, ins[n + 5], ins[n + 6], below[2], outs[2], outs[3], first)

    row = pl.BlockSpec((BWD_TILE, D_MODEL), lambda i, j: (i, 0))
    any_spec = pl.BlockSpec(memory_space=pl.ANY)
    f32 = jax.ShapeDtypeStruct((SEQ, D_MODEL), F32)
    in_specs = list(pair_specs) + [row, row, gain.spec(), scale.spec(), any_spec]
    operands = tuple(pairs) + (x, dxo, gain.table, scale.table, sums[0])
    out_specs, out_shape, aliases = [row, _slot_spec(u)], [f32, jax.ShapeDtypeStruct(sums[0].shape, F32)], {n + 4: 1}
    if below is not None:
        in_specs += [row, below[1].spec(), any_spec]
        operands += (below[0], below[1].table, sums[1])
        out_specs += [row, _slot_spec(u - 1)]
        out_shape += [jax.ShapeDtypeStruct((SEQ, D_MODEL), BF16), jax.ShapeDtypeStruct(sums[1].shape, F32)]
        aliases[n + 7] = 3
    out = _matmul(name, (SEQ // BWD_TILE, N_CHIPS), N_CHIPS, 1, dims, len(pairs) // 2, in_specs, out_specs, out_shape,
                  (BWD_TILE, D_MODEL), epilogue, operands, ("arbitrary", "arbitrary"), aliases=aliases)
    if below is None:
        return out[0], (out[1], sums[1])
    return out[0], (out[1], out[3]), out[2]


def ffn_bwd_input(da, db, wg_all, wu_all, x, dxo, gain, scale, sums, u, below):
    hid = pl.BlockSpec((None, BWD_TILE, FF_SHARD), lambda i, j: (j, i, 0))
    w = pl.BlockSpec((None, FF_SHARD, D_MODEL), lambda i, j: (j, 0, 0))
    return matmul_prenorm_bwd("ffn_bwd_input", NN, (da, wg_all, db, wu_all), (hid, w, hid, w), x, dxo, gain, scale,
                              sums, u, below)


def in_proj(h, w_all):
    def epilogue(acc, ins, outs, ids):
        outs[0][...] = acc
        outs[1][...] = acc.astype(BF16)

    out = pl.BlockSpec((MM_TILE, IN_SHARD), lambda j, i: (i, j))
    return _matmul(
        "in_proj", (N_CHIPS, SEQ // MM_TILE), 1, None, NN, 1,
        [pl.BlockSpec((MM_TILE, D_MODEL), lambda j, i: (i, 0)),
         pl.BlockSpec((None, D_MODEL, IN_SHARD), lambda j, i: (j, 0, 0))],
        [out, out], [jax.ShapeDtypeStruct((SEQ, D_IN), F32), jax.ShapeDtypeStruct((SEQ, D_IN), BF16)],
        None, epilogue, (h, w_all), ("parallel", "parallel"))


_GATE_BLOCK0 = D_QKV // D_SHARD


def _branch_products(o, w_ref):
    ob = o.astype(BF16)
    return (_dot(ob[:, 0:256], w_ref[0:256, :], NN), _dot(ob[:, 256:384], w_ref[256:384, :], NN),
            _dot(ob[:, 384:768], w_ref[384:768, :], NN))


def merge_branches(o_cat, wbr_all, proj):
    def body(o_ref, w_ref, g0_ref, g1_ref, g2_ref, m_ref):
        u = _branch_products(o_ref[...], w_ref)
        m_ref[...] = (_sigmoid(g0_ref[...]) * u[0] + _sigmoid(g1_ref[...]) * u[1]
                      + _sigmoid(g2_ref[...]) * u[2]).astype(BF16)

    def gate_spec(b):
        return pl.BlockSpec((MM_TILE, D_SHARD), lambda i, j: (i, _GATE_BLOCK0 + 4 * b + j))

    return pl.pallas_call(
        body, grid=(SEQ // MM_TILE, N_CHIPS),
        in_specs=[pl.BlockSpec((MM_TILE, BR_ROWS), lambda i, j: (i, 0)),
                  pl.BlockSpec((None, BR_ROWS, D_SHARD), lambda i, j: (j, 0, 0)),
                  gate_spec(0), gate_spec(1), gate_spec(2)],
        out_specs=pl.BlockSpec((MM_TILE, D_SHARD), lambda i, j: (i, j)),
        out_shape=jax.ShapeDtypeStruct((SEQ, D_MODEL), BF16),
        name="merge_branches", compiler_params=_params(("parallel", "parallel")),
    )(o_cat, wbr_all, proj, proj, proj)


def out_proj(merged, wout_all, x, gate, then):
    return matmul_residual(
        "out_proj", merged, pl.BlockSpec((MM_TILE, D_SHARD), lambda i, j: (i, j)),
        wout_all, pl.BlockSpec((None, D_SHARD, D_MODEL), lambda i, j: (j, 0, 0)), x, gate, 1.0, then)


def merge_bwd(dmo, wout_all, o_cat, wbr_all, proj):
    def epilogue(dm, ins, outs, ids):
        u = _branch_products(ins[2][...], ins[3])
        for b in range(3):
            sig = _sigmoid(ins[4 + b][...])
            outs[b][...] = (dm * sig).astype(BF16)
            outs[3 + b][...] = (dm * u[b] * (sig * (1.0 - sig))).astype(BF16)

    def gate_spec(b):
        return pl.BlockSpec((MM_TILE, D_SHARD), lambda j, i: (i, _GATE_BLOCK0 + 4 * b + j))

    col = pl.BlockSpec((MM_TILE, D_SHARD), lambda j, i: (i, j))
    du = jax.ShapeDtypeStruct((SEQ, D_MODEL), BF16)
    return _matmul(
        "merge_bwd", (N_CHIPS, SEQ // MM_TILE), 1, None, NT, 1,
        [pl.BlockSpec((MM_TILE, D_MODEL), lambda j, i: (i, 0)),
         pl.BlockSpec((None, D_SHARD, D_MODEL), lambda j, i: (j, 0, 0)),
         pl.BlockSpec((MM_TILE, BR_ROWS), lambda j, i: (i, 0)),
         pl.BlockSpec((None, BR_ROWS, D_SHARD), lambda j, i: (j, 0, 0)),
         gate_spec(0), gate_spec(1), gate_spec(2)],
        [col] * 6, [du] * 6,
        None, epilogue, (dmo, wout_all, o_cat, wbr_all, proj, proj, proj), ("parallel", "parallel"))


def branch_bwd_input(du, wbr_all):
    def body(d0_ref, d1_ref, d2_ref, w_ref, o_ref, acc):
        j = pl.program_id(1)
        parts = (_dot(d0_ref[...], w_ref[0:256, :], NT), _dot(d1_ref[...], w_ref[256:384, :], NT),
                 _dot(d2_ref[...], w_ref[384:768, :], NT))

        @pl.when(j == 0)
        def _():
            acc[:, 0:256], acc[:, 256:384], acc[:, 384:768] = parts

        @pl.when(j > 0)
        def _():
            acc[:, 0:256] += parts[0]
            acc[:, 256:384] += parts[1]
            acc[:, 384:768] += parts[2]

        @pl.when(j == N_CHIPS - 1)
        def _():
            o_ref[...] = acc[...]

    col = pl.BlockSpec((MM_TILE, D_SHARD), lambda i, j: (i, j))
    return pl.pallas_call(
        body, grid=(SEQ // MM_TILE, N_CHIPS),
        in_specs=[col, col, col, pl.BlockSpec((None, BR_ROWS, D_SHARD), lambda i, j: (j, 0, 0))],
        out_specs=pl.BlockSpec((MM_TILE, BR_ROWS), lambda i, j: (i, 0)),
        out_shape=jax.ShapeDtypeStruct((SEQ, BR_ROWS), F32),
        scratch_shapes=[pltpu.VMEM((MM_TILE, BR_ROWS), F32)],
        name="branch_bwd_input", compiler_params=_params(("parallel", "arbitrary")),
    )(du[0], du[1], du[2], wbr_all)


def branch_grad_weights(o_cat, du):
    def body(o_ref, d0_ref, d1_ref, d2_ref, g_ref, acc):
        k = pl.program_id(1)
        ob = o_ref[...].astype(BF16)
        parts = (_dot(ob[:, 0:256], d0_ref[...], TN), _dot(ob[:, 256:384], d1_ref[...], TN),
                 _dot(ob[:, 384:768], d2_ref[...], TN))

        @pl.when(k == 0)
        def _():
            acc[0:256, :], acc[256:384, :], acc[384:768, :] = parts

        @pl.when(k > 0)
        def _():
            acc[0:256, :] += parts[0]
            acc[256:384, :] += parts[1]
            acc[384:768, :] += parts[2]

        @pl.when(k == SEQ // MM_TILE - 1)
        def _():
            g_ref[...] = acc[...].astype(BF16)

    col = pl.BlockSpec((MM_TILE, D_SHARD), lambda j, k: (k, j))
    return pl.pallas_call(
        body, grid=(N_CHIPS, SEQ // MM_TILE),
        in_specs=[pl.BlockSpec((MM_TILE, BR_ROWS), lambda j, k: (k, 0)), col, col, col],
        out_specs=pl.BlockSpec((None, BR_ROWS, D_SHARD), lambda j, k: (j, 0, 0)),
        out_shape=jax.ShapeDtypeStruct((N_CHIPS, BR_ROWS, D_SHARD), BF16),
        scratch_shapes=[pltpu.VMEM((BR_ROWS, D_SHARD), F32)],
        name="branch_grad_weights", compiler_params=_params(("parallel", "arbitrary")),
    )(o_cat, du[0], du[1], du[2])


def mixer_bwd_input(dproj, win_all, x, dxo, gain, scale, sums, u, below):
    return matmul_prenorm_bwd(
        "mixer_bwd_input", NT, (dproj, win_all),
        (pl.BlockSpec((BWD_TILE, IN_SHARD), lambda i, j: (i, j)),
         pl.BlockSpec((None, D_MODEL, IN_SHARD), lambda i, j: (j, 0, 0))), x, dxo, gain, scale, sums, u, below)


BATCH_QK = (((2,), (2,)), ((0,), (0,)))
BATCH_PV = (((2,), (1,)), ((0,), (0,)))
BATCH_TN = (((1,), (1,)), ((0,), (0,)))


SB_WIDTH = H_SB * HEAD_DIM
SB_ROWS = H_SB * BLK


def _split_dot(v, tri):
    hi = v.astype(BF16)
    lo = (v - hi.astype(F32)).astype(BF16)
    return _dot(hi, tri, NN) + _dot(lo, tri, NN)


def _tri(cmp):
    return cmp(lax.broadcasted_iota(jnp.int32, (BLK, BLK), 0), lax.broadcasted_iota(jnp.int32, (BLK, BLK), 1)).astype(BF16)


def _head_masks():
    lane = lax.broadcasted_iota(jnp.int32, (1, SB_WIDTH), 1) // HEAD_DIM
    return [lane == h for h in range(H_SB)]


def _stack_heads(x, masks):
    return jnp.concatenate([jnp.where(m, x, jnp.zeros_like(x)) for m in masks], axis=0)


def _merge_heads(y, masks):
    out = jnp.where(masks[0], y[0:BLK], 0.0)
    for h in range(1, H_SB):
        out = jnp.where(masks[h], y[h * BLK:(h + 1) * BLK], out)
    return out


def _sb_scores(q4, k_ref, j, diagonal):
    rows = pl.ds(pl.multiple_of(j * BLK, BLK), BLK)
    z = _dot(q4, k_ref[rows, :], NT)
    log_fail = -(jnp.maximum(z, 0.0) + jnp.log(1.0 + jnp.exp(-jnp.abs(z))))
    log_hit = z + log_fail
    before = None
    if diagonal:
        tile = (SB_ROWS, BLK)
        before = lax.broadcasted_iota(jnp.int32, tile, 1) < (lax.broadcasted_iota(jnp.int32, tile, 0) & (BLK - 1))
        log_fail = jnp.where(before, log_fail, 0.0)
    return rows, before, log_fail, log_hit


def _keep(before, x):
    return x if before is None else jnp.where(before, x, 0.0)


def sb_forward(qkv):
    def body(q_ref, k_ref, v_ref, o_ref, tot_ref):
        i = pl.program_id(0)
        masks = _head_masks()
        q4 = _stack_heads(q_ref[...] * QK_SCALE, masks)
        later = _tri(lambda r, c: r > c)

        def tiles(js, carry, diagonal):
            o, run = carry
            scores = [_sb_scores(q4, k_ref, j, diagonal) for j in js]
            acc = None
            for rows, before, log_fail, log_hit in scores:
                between = _split_dot(log_fail, later) + run
                w = _keep(before, jnp.exp(log_hit + between))
                part = _dot(w.astype(BF16), v_ref[rows, :], NN)
                acc = part if acc is None else acc + part
                run = run + jnp.sum(log_fail, axis=1, keepdims=True)
            return o + _merge_heads(acc, masks), run

        carry = tiles([i], (jnp.zeros((BLK, SB_WIDTH), F32), jnp.zeros((SB_ROWS, 1), F32)), True)
        carry = lax.cond((i & 1) != 0, lambda c: tiles([i - 1], c, False), lambda c: c, carry)
        at = i - 1 - (i & 1)
        carry = lax.cond((i & 2) != 0, lambda c: tiles([at, at - 1], c, False), lambda c: c, carry)
        at = at - (i & 2)
        o, run = lax.fori_loop(0, i // 4, lambda t, c: tiles([at - 4 * t - n for n in range(4)], c, False), carry)
        o_ref[...] = o
        tot_ref[...] = run

    return pl.pallas_call(
        body, grid=(N_BLK,),
        in_specs=[pl.BlockSpec((BLK, SB_WIDTH), lambda i: (i, 0)), pl.BlockSpec((SEQ, SB_WIDTH), lambda i: (0, 1)),
                  pl.BlockSpec((SEQ, SB_WIDTH), lambda i: (0, 2))],
        out_specs=[pl.BlockSpec((BLK, SB_WIDTH), lambda i: (i, 0)), pl.BlockSpec((None, SB_ROWS, 1), lambda i: (i, 0, 0))],
        out_shape=[jax.ShapeDtypeStruct((SEQ, SB_WIDTH), F32), jax.ShapeDtypeStruct((N_BLK, SB_ROWS, 1), F32)],
        name="sb_forward", compiler_params=_params(("parallel",)),
    )(qkv, qkv, qkv)


def sb_backward(qkv, total, do_cat):
    def body(q_ref, k_ref, v_ref, tot_ref, do_ref, dq_ref, dk_ref, dv_ref):
        i = pl.program_id(0)

        @pl.when(i == 0)
        def _():
            dk_ref[...] = jnp.zeros_like(dk_ref)
            dv_ref[...] = jnp.zeros_like(dv_ref)

        masks = _head_masks()
        q4 = _stack_heads(q_ref[...] * QK_SCALE, masks)
        do4 = _stack_heads(do_ref[...].astype(BF16), masks)
        total_v = tot_ref[...]
        upto = _tri(lambda r, c: r <= c)
        earlier = _tri(lambda r, c: r < c)

        def tiles(js, carry, diagonal):
            dq, seen, g_seen = carry
            scores = [_sb_scores(q4, k_ref, j, diagonal) for j in js]
            acc = None
            for rows, before, log_fail, log_hit in scores:
                between = total_v - (seen + _split_dot(log_fail, upto))
                w = _keep(before, jnp.exp(log_hit + between))
                g = _dot(do4, v_ref[rows, :], NT) * w
                g_earlier = g_seen + _split_dot(g, earlier)
                sig = jnp.exp(log_hit)
                dz = _keep(before, g * (1.0 - sig) - g_earlier * sig).astype(BF16)
                part = _dot(dz, k_ref[rows, :], NN)
                acc = part if acc is None else acc + part
                dk_ref[rows, :] += _dot(dz, q4, TN)
                dv_ref[rows, :] += _dot(w.astype(BF16), do4, TN)
                seen = seen + jnp.sum(log_fail, axis=1, keepdims=True)
                g_seen = g_seen + jnp.sum(g, axis=1, keepdims=True)
            return dq + _merge_heads(acc, masks), seen, g_seen

        zero = jnp.zeros((SB_ROWS, 1), F32)
        carry = lax.fori_loop(0, i // 4, lambda t, c: tiles([4 * t + n for n in range(4)], c, False),
                              (jnp.zeros((BLK, SB_WIDTH), F32), zero, zero))
        at = i - (i & 3)
        carry = lax.cond((i & 2) != 0, lambda c: tiles([at, at + 1], c, False), lambda c: c, carry)
        carry = lax.cond((i & 1) != 0, lambda c: tiles([i - 1], c, False), lambda c: c, carry)
        dq, _, _ = tiles([i], carry, True)
        dq_ref[...] = dq * QK_SCALE

    blk = pl.BlockSpec((BLK, SB_WIDTH), lambda i: (i, 0))
    full = pl.BlockSpec((SEQ, SB_WIDTH), lambda i: (0, 0))
    shape = jax.ShapeDtypeStruct((SEQ, SB_WIDTH), F32)
    return pl.pallas_call(
        body, grid=(N_BLK,),
        in_specs=[blk, pl.BlockSpec((SEQ, SB_WIDTH), lambda i: (0, 1)), pl.BlockSpec((SEQ, SB_WIDTH), lambda i: (0, 2)),
                  pl.BlockSpec((None, SB_ROWS, 1), lambda i: (i, 0, 0)), blk],
        out_specs=[blk, full, full], out_shape=[shape, shape, shape],
        name="sb_backward", compiler_params=_params(("arbitrary",)),
    )(qkv, qkv, qkv, total, do_cat)


def _band_scores(q_ref, kp_ref, ko_ref, bias_ref, hb, prev_mask):
    b = pl.program_id(1)
    qs = q_ref[...]
    s_prev = _dot(qs, kp_ref[...], BATCH_QK) + bias_ref[:, :, 0:BLK]
    s_prev = jnp.concatenate(
        [jnp.where((b & prev_mask(pl.program_id(0) * hb + t)) != 0, s_prev[t:t + 1], NEG) for t in range(hb)], axis=0)
    s_own = _dot(qs, ko_ref[...], BATCH_QK) + bias_ref[:, :, BLK:2 * BLK]
    return qs, s_prev, s_own


def _band_specs(hb, rows, t_n):
    def q_spec(width):
        return pl.BlockSpec((hb, None, rows, width), lambda h, b: (h, b, 0, 0))

    own = pl.BlockSpec((hb, BLK, HEAD_DIM), lambda h, b: (h, b, 0))
    prev = pl.BlockSpec((hb, BLK, HEAD_DIM), lambda h, b: (h, jnp.maximum(b - 1, 0), 0))
    per_head = lambda r, width: pl.BlockSpec((hb, r, width), lambda h, b: (h, 0, 0))
    return q_spec, own, prev, per_head


def banded_forward(name, q, k, v, bias, sinks, hb, prev_mask):
    h_n, nb, rows, _ = q.shape

    def body(q_ref, kp_ref, ko_ref, vp_ref, vo_ref, bias_ref, sink_ref, o_ref, lse_ref):
        _, s_prev, s_own = _band_scores(q_ref, kp_ref, ko_ref, bias_ref, hb, prev_mask)
        sink = sink_ref[...]
        m = jnp.maximum(jnp.maximum(jnp.max(s_prev, axis=2, keepdims=True), jnp.max(s_own, axis=2, keepdims=True)), sink)
        p_prev = jnp.exp(s_prev - m)
        p_own = jnp.exp(s_own - m)
        denom = jnp.sum(p_prev, axis=2, keepdims=True) + jnp.sum(p_own, axis=2, keepdims=True) + jnp.exp(sink - m)
        o = _dot(p_prev.astype(BF16), vp_ref[...], BATCH_PV) + _dot(p_own.astype(BF16), vo_ref[...], BATCH_PV)
        o_ref[...] = o / denom
        lse_ref[...] = m + jnp.log(denom)

    q_spec, own, prev, per_head = _band_specs(hb, rows, k.shape[1])
    return pl.pallas_call(
        body, grid=(h_n // hb, nb),
        in_specs=[q_spec(HEAD_DIM), prev, own, prev, own, per_head(rows, 2 * BLK), per_head(rows, 1)],
        out_specs=[q_spec(HEAD_DIM), q_spec(1)],
        out_shape=[jax.ShapeDtypeStruct(q.shape, F32), jax.ShapeDtypeStruct((h_n, nb, rows, 1), F32)],
        name=name, compiler_params=_params(("parallel", "parallel")),
    )(q, k, k, v, v, bias, sinks)


def banded_backward(name, q, k, v, bias, sinks, o, lse, do, dlse, hb, prev_mask):
    h_n, nb, rows, _ = q.shape
    t_n = k.shape[1]

    def body(q_ref, kp_ref, ko_ref, vp_ref, vo_ref, bias_ref, sink_ref, o_ref, lse_ref, do_ref, dlse_ref,
             dq_ref, dk_ref, dv_ref, dbias_ref, dsink_ref):
        b = pl.program_id(1)

        @pl.when(b == 0)
        def _():
            dk_ref[...] = jnp.zeros_like(dk_ref)
            dv_ref[...] = jnp.zeros_like(dv_ref)
            dbias_ref[...] = jnp.zeros_like(dbias_ref)
            dsink_ref[...] = jnp.zeros_like(dsink_ref)

        qs, s_prev, s_own = _band_scores(q_ref, kp_ref, ko_ref, bias_ref, hb, prev_mask)
        lse_v = lse_ref[...]
        dov = do_ref[...]
        dob = dov.astype(BF16)
        shift = dlse_ref[...] - jnp.sum(dov * o_ref[...], axis=2, keepdims=True)
        p_prev = jnp.exp(s_prev - lse_v)
        p_own = jnp.exp(s_own - lse_v)
        ds_prev = p_prev * (_dot(dob, vp_ref[...], BATCH_QK) + shift)
        ds_own = p_own * (_dot(dob, vo_ref[...], BATCH_QK) + shift)
        dbias_ref[:, :, 0:BLK] += ds_prev
        dbias_ref[:, :, BLK:2 * BLK] += ds_own
        d_sink = jnp.exp(sink_ref[...] - lse_v) * shift
        for g in range(rows // BLK):
            dsink_ref[:, g:g + 1, :] += jnp.sum(d_sink[:, g * BLK:(g + 1) * BLK, :], axis=1, keepdims=True)
        ds_prev = ds_prev.astype(BF16)
        ds_own = ds_own.astype(BF16)
        dq_ref[...] = (_dot(ds_prev, kp_ref[...], BATCH_PV) + _dot(ds_own, ko_ref[...], BATCH_PV)) * QK_SCALE
        rows_prev = pl.ds(pl.multiple_of(jnp.maximum(b - 1, 0) * BLK, BLK), BLK)
        rows_own = pl.ds(pl.multiple_of(b * BLK, BLK), BLK)
        dk_ref[:, rows_prev, :] += _dot(ds_prev, qs, BATCH_TN)
        dk_ref[:, rows_own, :] += _dot(ds_own, qs, BATCH_TN)
        dv_ref[:, rows_prev, :] += _dot(p_prev.astype(BF16), dob, BATCH_TN)
        dv_ref[:, rows_own, :] += _dot(p_own.astype(BF16), dob, BATCH_TN)

    q_spec, own, prev, per_head = _band_specs(hb, rows, t_n)
    kv_full = per_head(t_n, HEAD_DIM)
    kv_shape = jax.ShapeDtypeStruct((h_n, t_n, HEAD_DIM), F32)
    return pl.pallas_call(
        body, grid=(h_n // hb, nb),
        in_specs=[q_spec(HEAD_DIM), prev, own, prev, own, per_head(rows, 2 * BLK), per_head(rows, 1),
                  q_spec(HEAD_DIM), q_spec(1), q_spec(HEAD_DIM), q_spec(1)],
        out_specs=[q_spec(HEAD_DIM), kv_full, kv_full, per_head(rows, 2 * BLK), per_head(rows // BLK, BLK)],
        out_shape=[jax.ShapeDtypeStruct(q.shape, F32), kv_shape, kv_shape,
                   jax.ShapeDtypeStruct((h_n, rows, 2 * BLK), F32), jax.ShapeDtypeStruct((h_n, rows // BLK, BLK), F32)],
        name=name, compiler_params=_params(("parallel", "arbitrary")),
    )(q, k, k, v, v, bias, sinks, o, lse, do, dlse)


def _swa_prev_mask(head):
    del head
    return 15


SWA_HEADS_PER_STEP = 2
SWA_GROUP = H_SWA_Q // H_SWA_KV
N_BLK = SEQ // BLK


GROUP_W = H_PER_DIL * HEAD_DIM
LANE_BLOCKS = D_IN // GROUP_W
DIL_Q_BLOCK, DIL_K_BLOCK, DIL_V_BLOCK = 6, 9, 12
N_GROUPS = len(DIL_PATTERNS)


def _dil_view(t, d):
    return t.reshape(SEQ // d, d * t.shape[1])


def _dil_tile(n, d):
    per_class = N_BLK // d
    return n // per_class, n % per_class


def _dil_spec(d, lane_block, lane_blocks, shift=0):
    def index(n):
        r, m = _dil_tile(n, d)
        m = jnp.clip(m + shift, 0, N_BLK // d - 1)
        return m, r * lane_blocks + lane_block
    return pl.BlockSpec((BLK, GROUP_W), index)


def _two_heads(x, first):
    zero = jnp.zeros_like(x)
    return jnp.concatenate([jnp.where(first, x, zero), jnp.where(first, zero, x)], axis=0)


def _per_head(col, first):
    return jnp.where(first, col[0:BLK], col[BLK:2 * BLK])


def _head_rows(tile, first):
    pick = lambda keep: jnp.max(jnp.where(keep, tile, -jnp.inf), axis=1, keepdims=True)
    return jnp.concatenate([pick(first), pick(jnp.logical_not(first))], axis=0)


def _dil_scores(q_ref, kp_ref, ko_ref, bias, has_prev, first):
    q2 = _two_heads(q_ref[...] * QK_SCALE, first)
    k2 = jnp.concatenate([kp_ref[...], ko_ref[...]], axis=0)
    s = _dot(q2, k2, NT) + bias
    key = lax.broadcasted_iota(jnp.int32, s.shape, 1)
    return q2, k2, jnp.where(jnp.logical_or(has_prev, key >= BLK), s, NEG)


def dilated_forward(qkv, bias):
    def body(*refs):
        ins, bias_ref, outs = refs[:5 * N_GROUPS], refs[5 * N_GROUPS], refs[5 * N_GROUPS + 1:]
        n = pl.program_id(0)
        first = lax.broadcasted_iota(jnp.int32, (1, GROUP_W), 1) < HEAD_DIM
        for g, (_, d) in enumerate(DIL_PATTERNS):
            q_ref, kp_ref, ko_ref, vp_ref, vo_ref = ins[5 * g:5 * g + 5]
            has_prev = _dil_tile(n, d)[1] > 0
            _, _, s = _dil_scores(q_ref, kp_ref, ko_ref, bias_ref[g], has_prev, first)
            m = jnp.max(s, axis=1, keepdims=True)
            p = jnp.exp(s - m)
            denom = jnp.sum(p, axis=1, keepdims=True)
            v2 = jnp.concatenate([vp_ref[...], vo_ref[...]], axis=0)
            o2 = _dot(p.astype(BF16), v2, NN) / denom
            outs[2 * g][...] = _per_head(o2, first)
            outs[2 * g + 1][...] = _per_head(m + jnp.log(denom), first)

    operands, in_specs, out_specs, out_shape = [], [], [], []
    for g, (_, d) in enumerate(DIL_PATTERNS):
        view = _dil_view(qkv, d)
        operands += [view] * 5
        in_specs += [_dil_spec(d, DIL_Q_BLOCK + g, LANE_BLOCKS), _dil_spec(d, DIL_K_BLOCK + g, LANE_BLOCKS, -1),
                     _dil_spec(d, DIL_K_BLOCK + g, LANE_BLOCKS), _dil_spec(d, DIL_V_BLOCK + g, LANE_BLOCKS, -1),
                     _dil_spec(d, DIL_V_BLOCK + g, LANE_BLOCKS)]
        out_specs += [_dil_spec(d, 0, 1)] * 2
        out_shape += [jax.ShapeDtypeStruct((SEQ // d, d * GROUP_W), F32)] * 2
    out = pl.pallas_call(
        body, grid=(N_BLK,), in_specs=in_specs + [pl.BlockSpec((N_GROUPS, 2 * BLK, 2 * BLK), lambda n: (0, 0, 0))],
        out_specs=out_specs, out_shape=out_shape, name="dilated_forward", compiler_params=_params(("parallel",)),
    )(*operands, bias)
    out = [t.reshape(SEQ, GROUP_W) for t in out]
    return out[0::2], out[1::2]


def _group_softmax(lses):
    m = jnp.maximum(jnp.maximum(lses[0], lses[1]), lses[2])
    e = [jnp.exp(l - m) for l in lses]
    total = e[0] + e[1] + e[2]
    return [t / total for t in e]


def dilated_merge(o, lse):
    def body(*refs):
        alpha = _group_softmax([r[...] for r in refs[N_GROUPS:2 * N_GROUPS]])
        refs[-1][...] = alpha[0] * refs[0][...] + alpha[1] * refs[1][...] + alpha[2] * refs[2][...]

    spec = pl.BlockSpec((ROW_TILE, GROUP_W), lambda i: (i, 0))
    return pl.pallas_call(
        body, grid=(SEQ // ROW_TILE,), in_specs=[spec] * (2 * N_GROUPS), out_specs=spec,
        out_shape=jax.ShapeDtypeStruct((SEQ, GROUP_W), F32), name="dilated_merge", compiler_params=_params(("parallel",)),
    )(*o, *lse)


def dilated_merge_bwd(o, lse, do_cat):
    def body(*refs):
        o_v = [r[...] for r in refs[:N_GROUPS]]
        alpha = _group_softmax([r[...] for r in refs[N_GROUPS:2 * N_GROUPS]])
        dout = refs[2 * N_GROUPS][...]
        outs = refs[2 * N_GROUPS + 1:]
        first = lax.broadcasted_iota(jnp.int32, (1, GROUP_W), 1) < HEAD_DIM

        def head_sum(x):
            a = jnp.sum(jnp.where(first, x, 0.0), axis=1, keepdims=True)
            b = jnp.sum(jnp.where(first, 0.0, x), axis=1, keepdims=True)
            return jnp.where(first, a, b)

        dalpha = [head_sum(dout * o_g) for o_g in o_v]
        mean = alpha[0] * dalpha[0] + alpha[1] * dalpha[1] + alpha[2] * dalpha[2]
        for g in range(N_GROUPS):
            outs[g][...] = alpha[g] * dout
            outs[N_GROUPS + g][...] = alpha[g] * (dalpha[g] - mean)

    spec = pl.BlockSpec((ROW_TILE, GROUP_W), lambda i: (i, 0))
    shape = jax.ShapeDtypeStruct((SEQ, GROUP_W), F32)
    out = pl.pallas_call(
        body, grid=(SEQ // ROW_TILE,), in_specs=[spec] * (2 * N_GROUPS) + [pl.BlockSpec((ROW_TILE, GROUP_W), lambda i: (i, 2))],
        out_specs=[spec] * (2 * N_GROUPS), out_shape=[shape] * (2 * N_GROUPS),
        name="dilated_merge_bwd", compiler_params=_params(("parallel",)),
    )(*o, *lse, do_cat)
    return out[:N_GROUPS], out[N_GROUPS:]


def dilated_backward(qkv, bias, o, lse, do, dlse):
    n_in = 9

    def body(*refs):
        ins, bias_ref = refs[:n_in * N_GROUPS], refs[n_in * N_GROUPS]
        outs, dbias_ref = refs[n_in * N_GROUPS + 1:-1], refs[-1]
        n = pl.program_id(0)

        @pl.when(n == 0)
        def _():
            dbias_ref[...] = jnp.zeros_like(dbias_ref)

        first = lax.broadcasted_iota(jnp.int32, (1, GROUP_W), 1) < HEAD_DIM
        for g, (_, d) in enumerate(DIL_PATTERNS):
            q_ref, kp_ref, ko_ref, vp_ref, vo_ref, o_ref, lse_ref, do_ref, dlse_ref = ins[n_in * g:n_in * (g + 1)]
            has_prev = _dil_tile(n, d)[1] > 0
            q2, k2, s = _dil_scores(q_ref, kp_ref, ko_ref, bias_ref[g], has_prev, first)
            dov = do_ref[...]
            do2 = _two_heads(dov.astype(BF16), first)
            prod = dov * o_ref[...]
            delta = jnp.concatenate([jnp.sum(jnp.where(first, prod, 0.0), axis=1, keepdims=True),
                                     jnp.sum(jnp.where(first, 0.0, prod), axis=1, keepdims=True)], axis=0)
            shift = _head_rows(dlse_ref[...], first) - delta
            p = jnp.exp(s - _head_rows(lse_ref[...], first))
            v2 = jnp.concatenate([vp_ref[...], vo_ref[...]], axis=0)
            ds = p * (_dot(do2, v2, NT) + shift)
            dbias_ref[g] += ds
            ds = ds.astype(BF16)
            dq2 = _dot(ds, k2, NN) * QK_SCALE
            dk2 = _dot(ds, q2, TN)
            dv2 = _dot(p.astype(BF16), do2, TN)
            base = 5 * g
            outs[base][...] = jnp.where(first, dq2[0:BLK], dq2[BLK:2 * BLK])
            outs[base + 1][...] = dk2[BLK:2 * BLK]
            outs[base + 2][...] = dk2[0:BLK]
            outs[base + 3][...] = dv2[BLK:2 * BLK]
            outs[base + 4][...] = dv2[0:BLK]

    operands, in_specs, out_specs, out_shape = [], [], [], []
    for g, (_, d) in enumerate(DIL_PATTERNS):
        view = _dil_view(qkv, d)
        own = _dil_spec(d, 0, 1)
        operands += [view] * 5 + [_dil_view(t[g], d) for t in (o, lse, do, dlse)]
        in_specs += [_dil_spec(d, DIL_Q_BLOCK + g, LANE_BLOCKS), _dil_spec(d, DIL_K_BLOCK + g, LANE_BLOCKS, -1),
                     _dil_spec(d, DIL_K_BLOCK + g, LANE_BLOCKS), _dil_spec(d, DIL_V_BLOCK + g, LANE_BLOCKS, -1),
                     _dil_spec(d, DIL_V_BLOCK + g, LANE_BLOCKS)] + [own] * 4
        out_specs += [own] * 5
        out_shape += [jax.ShapeDtypeStruct((SEQ // d, d * GROUP_W), F32)] * 5
    tiles = pl.BlockSpec((N_GROUPS, 2 * BLK, 2 * BLK), lambda n: (0, 0, 0))
    out = pl.pallas_call(
        body, grid=(N_BLK,), in_specs=in_specs + [tiles], out_specs=out_specs + [tiles],
        out_shape=out_shape + [jax.ShapeDtypeStruct((N_GROUPS, 2 * BLK, 2 * BLK), F32)],
        name="dilated_backward", compiler_params=_params(("arbitrary",)),
    )(*operands, bias)
    return [out[5 * g:5 * g + 5] for g in range(N_GROUPS)], out[-1]


def dilated_key_grads(parts):
    def body(*refs):
        ins, outs = refs[:4 * N_GROUPS], refs[4 * N_GROUPS:]
        n = pl.program_id(0)
        for g, (_, d) in enumerate(DIL_PATTERNS):
            has_next = _dil_tile(n, d)[1] < N_BLK // d - 1
            own_k, next_k, own_v, next_v = ins[4 * g:4 * g + 4]
            outs[2 * g][...] = own_k[...] + jnp.where(has_next, next_k[...], 0.0)
            outs[2 * g + 1][...] = own_v[...] + jnp.where(has_next, next_v[...], 0.0)

    operands, in_specs, out_specs, out_shape = [], [], [], []
    for g, (_, d) in enumerate(DIL_PATTERNS):
        _, dk_own, dk_prev, dv_own, dv_prev = parts[g]
        operands += [dk_own, dk_prev, dv_own, dv_prev]
        in_specs += [_dil_spec(d, 0, 1), _dil_spec(d, 0, 1, 1)] * 2
        out_specs += [_dil_spec(d, 0, 1)] * 2
        out_shape += [jax.ShapeDtypeStruct((SEQ // d, d * GROUP_W), F32)] * 2
    out = pl.pallas_call(
        body, grid=(N_BLK,), in_specs=in_specs, out_specs=out_specs, out_shape=out_shape,
        name="dilated_key_grads", compiler_params=_params(("parallel",)),
    )(*operands)
    tok = lambda ts: jnp.concatenate([t.reshape(SEQ, GROUP_W) for t in ts], axis=1)
    return tok([parts[g][0] for g in range(N_GROUPS)]), tok(out[0::2]), tok(out[1::2])


def rel_bias_reduce(dbias0, dbias1, bucket):
    def body(d0_ref, d1_ref, b_ref, o_ref):
        dv, bv = d0_ref[...] + d1_ref[...], b_ref[...]
        lane = lax.broadcasted_iota(jnp.int32, (1, BLK), 1)
        acc = jnp.zeros((1, BLK), F32)
        for bkt in range(N_BUCKETS):
            acc = acc + jnp.where(lane == bkt, jnp.sum(jnp.where(bv == bkt, dv, 0.0)), 0.0)
        o_ref[...] = acc

    tile = pl.BlockSpec((None, BLK, 2 * BLK), lambda h: (h, 0, 0))
    return pl.pallas_call(
        body, grid=(dbias0.shape[0],), in_specs=[tile, tile, tile],
        out_specs=pl.BlockSpec((None, 1, BLK), lambda h: (h, 0, 0)),
        out_shape=jax.ShapeDtypeStruct((dbias0.shape[0], 1, BLK), F32),
        name="rel_bias_reduce", compiler_params=_params(("parallel",)),
    )(dbias0, dbias1, bucket)


def _heads(t):
    return t.reshape(SEQ, -1, HEAD_DIM).transpose(1, 0, 2)


def _unheads(t):
    return t.transpose(1, 0, 2).reshape(SEQ, -1)


def _t5_bucket(n):
    max_exact = N_BUCKETS // 2
    nf = jnp.maximum(n, 1).astype(F32)
    large = max_exact + (jnp.log(nf / max_exact) / math.log(MAX_REL_DIST / max_exact)
                         * (N_BUCKETS - max_exact)).astype(jnp.int32)
    large = jnp.minimum(large, N_BUCKETS - 1)
    return jnp.where(n < max_exact, n, large)


def band_tables(rel_bias):
    rel = jnp.arange(BLK)[:, None] + BLK - jnp.arange(2 * BLK)[None, :]
    buckets = []
    patterns = [(d, w // d) for w, d in DIL_PATTERNS for _ in range(H_PER_DIL)] + [(1, SWA_WINDOW - 1)] * H_SWA_Q
    for d, max_dist in patterns:
        band = (rel >= 0) & (rel <= max_dist)
        buckets.append(jnp.where(band, _t5_bucket(jnp.maximum(rel, 0) * d), -1))
    buckets = jnp.stack(buckets).astype(jnp.int32)

    def body(table_ref, b_ref, o_ref):
        h = pl.program_id(0)
        bv = b_ref[...]
        tile = jnp.full(bv.shape, NEG, F32)
        for bkt in range(N_BUCKETS):
            tile = jnp.where(bv == bkt, table_ref[h, bkt], tile)
        o_ref[...] = tile

    spec = pl.BlockSpec((None, BLK, 2 * BLK), lambda h: (h, 0, 0))
    tiles = pl.pallas_call(
        body, grid=(len(patterns),), in_specs=[pl.BlockSpec(memory_space=pltpu.SMEM), spec], out_specs=spec,
        out_shape=jax.ShapeDtypeStruct(buckets.shape, F32), name="band_tables", compiler_params=_params(("parallel",)),
    )(rel_bias.T, buckets)
    return tiles[:H_DIL], tiles[H_DIL:], buckets


def _swa_rows(t):
    t = t.reshape(N_BLK, BLK, H_SWA_KV, SWA_GROUP, HEAD_DIM).transpose(2, 0, 3, 1, 4)
    return t.reshape(H_SWA_KV, N_BLK, SWA_GROUP * BLK, HEAD_DIM)


def _swa_tokens(t):
    t = t.reshape(H_SWA_KV, N_BLK, SWA_GROUP, BLK, HEAD_DIM).transpose(1, 3, 0, 2, 4)
    return t.reshape(SEQ, H_SWA_Q * HEAD_DIM)


def _sink_rows(sinks):
    return jnp.broadcast_to(sinks.reshape(H_SWA_KV, SWA_GROUP, 1, 1), (H_SWA_KV, SWA_GROUP, BLK, 1)).reshape(
        H_SWA_KV, SWA_GROUP * BLK, 1)


def _vec(v):
    return v.reshape(1, D_MODEL)


class UnitRows:
    def __init__(self, u, mod_table, gain_table):
        self.shift, self.scale, self.gate = (Row(mod_table, 3 * u + t) for t in range(3))
        self.gain = Row(gain_table, u)


def ffn_forward(x, h, rows, then, w):
    a, b, s = ffn_up(h, w[0], w[1])
    f, xo, *h_next = ffn_down(s, w[2], x, rows.gate, then)
    return xo, (h_next or [None])[0], (x, h, a, b, s, f)


def ffn_backward(u, dxo, df, saved, rows, w, sums, below):
    x, h, a, b, s, _ = saved
    da, db = ffn_bwd_hidden(df, w[2], a, b)
    grads = ffn_grad_weights(h, s, df, da, db)
    dx, sums, *df_below = ffn_bwd_input(da, db, w[0], w[1], x, dxo, rows.gain, rows.scale, sums, u, below)
    return dx, sums, df_below, grads


def mixer_forward(x, h, rows, then, sinks, bias_dil, bias_swa, w):
    proj, qkv = in_proj(h, w[0])
    q_swa, k_swa, v_swa = _swa_rows(qkv[:, 1920:2304] * QK_SCALE), _heads(qkv[:, 2304:2432]), _heads(qkv[:, 2432:2560])
    o_sb, total_sb = sb_forward(qkv)
    bias_dil = bias_dil.reshape(N_GROUPS, 2 * BLK, 2 * BLK)
    o_groups, lse_groups = dilated_forward(qkv, bias_dil)
    o_dil = dilated_merge(o_groups, lse_groups)
    bias_swa = bias_swa.reshape(H_SWA_KV, SWA_GROUP * BLK, 2 * BLK)
    o_swa, lse_swa = banded_forward("swa_forward", q_swa, k_swa, v_swa, bias_swa, _sink_rows(sinks), SWA_HEADS_PER_STEP,
                                    _swa_prev_mask)
    o_cat = jnp.concatenate([o_sb, o_dil, _swa_tokens(o_swa)], axis=1)
    merged = merge_branches(o_cat, w[1], proj)
    mo, xo, *h_next = out_proj(merged, w[2], x, rows.gate, then)
    saved = (x, h, proj, (qkv, total_sb), (o_groups, lse_groups),
             (q_swa, k_swa, v_swa, o_swa, lse_swa), o_cat, merged, mo)
    return xo, (h_next or [None])[0], saved


def mixer_backward(u, dxo, dmo, saved, rows, sinks, bias_dil, bias_swa, w, sums, below):
    x, h, proj, sb, dil, swa, o_cat, merged, _ = saved
    tok = pl.BlockSpec((MM_TILE, D_MODEL), lambda j, k: (k, 0))
    g_out = grad_weight("grad_w_out", merged, pl.BlockSpec((MM_TILE, D_SHARD), lambda j, k: (k, j)), dmo, tok,
                        (D_SHARD, D_MODEL))
    du0, du1, du2, dg0, dg1, dg2 = merge_bwd(dmo, w[2], o_cat, w[1], proj)
    du = (du0, du1, du2)
    do_cat = branch_bwd_input(du, w[1])
    g_br = branch_grad_weights(o_cat, du)

    qkv, total_sb = sb
    dq_sb, dk_sb, dv_sb = sb_backward(qkv, total_sb, do_cat)

    o_groups, lse_groups = dil
    bias_dil = bias_dil.reshape(N_GROUPS, 2 * BLK, 2 * BLK)
    do_groups, dlse_groups = dilated_merge_bwd(o_groups, lse_groups, do_cat)
    parts, dbias_dil = dilated_backward(qkv, bias_dil, o_groups, lse_groups, do_groups, dlse_groups)
    dq_dil, dk_dil, dv_dil = dilated_key_grads(parts)
    dbias_dil = dbias_dil.reshape(H_DIL, BLK, 2 * BLK)

    q_swa, k_swa, v_swa, o_swa, lse_swa = swa
    bias_swa = bias_swa.reshape(H_SWA_KV, SWA_GROUP * BLK, 2 * BLK)
    dq_swa, dk_swa, dv_swa, dbias_swa, dsinks = banded_backward(
        "swa_backward", q_swa, k_swa, v_swa, bias_swa, _sink_rows(sinks), o_swa, lse_swa, _swa_rows(do_cat[:, 384:768]),
        jnp.zeros_like(lse_swa), SWA_HEADS_PER_STEP, _swa_prev_mask)
    dbias_swa = dbias_swa.reshape(H_SWA_Q, BLK, 2 * BLK)

    dproj = jnp.concatenate(
        [dq_sb, dk_sb, dv_sb, dq_dil, dk_dil, dv_dil, _swa_tokens(dq_swa), _unheads(dk_swa), _unheads(dv_swa)],
        axis=1).astype(BF16)
    dproj = jnp.concatenate([dproj, dg0, dg1, dg2], axis=1)
    g_in = grad_weight("grad_w_in", h, tok, dproj, pl.BlockSpec((MM_TILE, IN_SHARD), lambda j, k: (k, j)),
                       (D_MODEL, IN_SHARD))
    dx, sums, *df_below = mixer_bwd_input(dproj, w[0], x, dxo, rows.gain, rows.scale, sums, u, below)
    dbias = jnp.concatenate([dbias_dil, dbias_swa], axis=0)
    return dx, sums, df_below, dbias, dsinks[:, :, 0].reshape(H_SWA_Q), (g_in, g_br, g_out)


N_UNITS = 3 * DEPTH


def device_step(x, target, mod, gains, final_gain, sinks, rel_bias, get_weights, put_grads):
    bias_dil, bias_swa, bucket = band_tables(rel_bias)
    mod_table = mod.reshape(3 * N_UNITS, 1, D_MODEL)
    gain_table = gains.reshape(N_UNITS, 1, D_MODEL)
    saved, weights = [], []
    units = [UnitRows(u, mod_table, gain_table) for u in range(N_UNITS)]
    h = prenorm(x, units[0].gain, units[0].scale, units[0].shift)
    for u in range(N_UNITS):
        l, j = divmod(u, 3)
        w = get_weights(u, x)
        then = units[u + 1] if u + 1 < N_UNITS else None
        if j == 1:
            x, h, s = mixer_forward(x, h, units[u], then, sinks[l], bias_dil, bias_swa, w)
        else:
            x, h, s = ffn_forward(x, h, units[u], then, w)
        saved.append(s)
        weights.append(w)
    loss, dx, dfinal = final_loss(x, _vec(final_gain), target)

    sums = (lax.empty((8 * N_UNITS, D_MODEL), F32), lax.empty((8 * N_UNITS, D_MODEL), F32))
    dbias, dsinks = [None] * DEPTH, [None] * DEPTH
    zero = jnp.zeros((1, 1), F32)
    top = N_UNITS - 1
    df, gate_sums = resid_bwd(dx, saved[top][-1], units[top].gate, 0.5, sums[1], top)
    sums = (sums[0], gate_sums)
    for u in reversed(range(N_UNITS)):
        l, j = divmod(u, 3)
        rows = UnitRows(u, mod_table, gain_table + zero)
        below = (saved[u - 1][-1], units[u - 1].gate, 1.0 if (u - 1) % 3 == 1 else 0.5) if u > 0 else None
        if j == 1:
            dx, sums, df, dbias[l], dsinks[l], grads = mixer_backward(
                u, dx, df, saved[u], rows, sinks[l], bias_dil, bias_swa, weights[u], sums, below)
        else:
            dx, sums, df, grads = ffn_backward(u, dx, df, saved[u], rows, weights[u], sums, below)
        df = df[0] if df else None
        if u > 0:
            zero = put_grads(u, grads)
    drel = rel_bias_reduce(dbias[0], dbias[1], bucket)[:, 0, :N_BUCKETS].T
    norm_sums, gate_sums = (t.reshape(DEPTH, 3, 8, D_MODEL) for t in sums)
    dmod = jnp.stack([norm_sums[:, :, 0], norm_sums[:, :, 1], gate_sums[:, :, 0]], axis=2)
    return loss, dx, dmod, norm_sums[:, :, 2], dfinal[0], jnp.stack(dsinks), drel, grads


MESH = pl.DeviceIdType.MESH
CHIP_FLIPS = ((1, 0), (0, 1), (1, 1))
ANY = pl.BlockSpec(memory_space=pl.ANY)


def _position():
    return lax.axis_index("x"), lax.axis_index("y"), lax.axis_index("c")


def all_gather_small(name, piece):
    def body(x_ref, out_ref, send_sems, recv_sems, local_sem):
        x, y, c = _position()
        me, sibling = (x, y, c), (x, y, 1 - c)
        chips = [(x ^ fx, y ^ fy) for fx, fy in CHIP_FLIPS]

        def rows(px, py, pc):
            return out_ref.at[4 * px + 2 * py + pc]

        def copy(k, block, to, src=None):
            return pltpu.make_async_remote_copy(
                src_ref=rows(*block) if src is None else src, dst_ref=rows(*block),
                send_sem=send_sems.at[k], recv_sem=recv_sems.at[k], device_id=to, device_id_type=MESH)

        mine = pltpu.make_async_copy(x_ref, rows(*me), local_sem)
        mine.start()
        first = [copy(0, me, sibling, src=x_ref)]
        first += [copy(1 + j, me, (*chip, c), src=x_ref) for j, chip in enumerate(chips)]
        for cp in first:
            cp.start()
        passed = [copy(4 + j, (*chip, c), sibling) for j, chip in enumerate(chips)]
        for j, chip in enumerate(chips):
            copy(1 + j, (*chip, c), me).wait_recv()
            passed[j].start()
        copy(0, sibling, me).wait_recv()
        for j, chip in enumerate(chips):
            copy(4 + j, (*chip, 1 - c), me).wait_recv()
        for cp in first + passed:
            cp.wait_send()
        mine.wait()

    return pl.pallas_call(
        body, out_shape=jax.ShapeDtypeStruct((N_DEV,) + piece.shape, piece.dtype),
        in_specs=[pl.BlockSpec(memory_space=pltpu.VMEM)], out_specs=pl.BlockSpec(memory_space=pltpu.VMEM),
        scratch_shapes=[pltpu.SemaphoreType.DMA((7,)), pltpu.SemaphoreType.DMA((7,)), pltpu.SemaphoreType.DMA],
        name=name,
    )(piece)


def exchange(name, operands, out_shapes, aliases, plan):
    n_in, n_out = len(operands), len(out_shapes)

    def body(*refs):
        ins, outs = refs[:n_in], refs[n_in:n_in + n_out]
        send_sems, recv_sems, local_sems = refs[n_in + n_out:]
        x, y, c = _position()
        local, sends, recvs = plan(ins, outs, x, y, c)
        local = [pltpu.make_async_copy(s, d, local_sems.at[k]) for k, (s, d) in enumerate(local)]
        for cp in local:
            cp.start()
        remote = [pltpu.make_async_remote_copy(src_ref=s, dst_ref=d, send_sem=send_sems.at[k], recv_sem=recv_sems.at[k],
                                               device_id=dev, device_id_type=MESH)
                  for k, (s, d, dev) in enumerate(sends)]
        for cp in remote:
            cp.start()
        for k, r in enumerate(recvs):
            pltpu.make_async_remote_copy(src_ref=r, dst_ref=r, send_sem=send_sems.at[k], recv_sem=recv_sems.at[k],
                                         device_id=(x, y, c), device_id_type=MESH).wait_recv()
        for cp in remote:
            cp.wait_send()
        for cp in local:
            cp.wait()

    n_sends, n_local = plan.n_sends, max(plan.n_local, 1)
    return pl.pallas_call(
        body, out_shape=out_shapes, in_specs=[ANY] * n_in, out_specs=[ANY] * n_out,
        scratch_shapes=[pltpu.SemaphoreType.DMA((n_sends,)), pltpu.SemaphoreType.DMA((n_sends,)),
                        pltpu.SemaphoreType.DMA((n_local,))],
        input_output_aliases=aliases, name=name,
    )(*operands)


def _plan(n_local, n_sends):
    def wrap(fn):
        fn.n_local, fn.n_sends = n_local, n_sends
        return fn
    return wrap


def _half(ref, axis, c):
    rows = ref.shape[axis] // 2
    idx = [slice(None)] * len(ref.shape)
    idx[axis] = pl.ds(pl.multiple_of(c * rows, 16), rows)
    return ref.at[tuple(idx)]


HBM = pl.BlockSpec(memory_space=pltpu.HBM)
SEM = pl.BlockSpec(memory_space=pltpu.SEMAPHORE)
EFFECT = pltpu.SideEffectType.DATAFLOW_SIDE_EFFECTING


def split_start(name, bufs, extra, n_copies, describe):
    n = len(bufs)

    def body(*refs):
        send_sems, recv_sems = refs[n + len(extra)], refs[n + len(extra) + 1]
        x, y, c = _position()
        for k, (src, dst, _, peer) in enumerate(describe(refs[:n], x, y, c)):
            pltpu.make_async_remote_copy(src_ref=src, dst_ref=dst, send_sem=send_sems.at[k], recv_sem=recv_sems.at[k],
                                         device_id=peer, device_id_type=MESH).start()
        token = refs[-1]
        token[...] = jnp.zeros_like(token)

    out = pl.pallas_call(
        body, name=name,
        out_shape=(pltpu.SemaphoreType.DMA((n_copies,)), pltpu.SemaphoreType.DMA((n_copies,)),
                   *[pltpu.HBM(b.shape, b.dtype) for b in bufs], jax.ShapeDtypeStruct((8, 128), F32)),
        in_specs=[HBM] * n + [ANY] * len(extra),
        out_specs=(SEM, SEM, *[HBM] * n, pl.BlockSpec(memory_space=pltpu.VMEM)),
        input_output_aliases={k: 2 + k for k in range(n)},
        compiler_params=pltpu.CompilerParams(has_side_effects=EFFECT),
    )(*[pltpu.with_memory_space_constraint(b, pltpu.HBM) for b in bufs], *extra)
    return out[0], out[1], list(out[2:2 + n]), out[-1]


def split_wait(name, bufs, send_sems, recv_sems, after, describe):
    n = len(bufs)

    def body(*refs):
        send, recv = refs[n], refs[n + 1]
        x, y, c = _position()
        for k, (src, _, dst, peer) in enumerate(describe(refs[:n], x, y, c)):
            copy = pltpu.make_async_remote_copy(src_ref=src, dst_ref=dst, send_sem=send.at[k], recv_sem=recv.at[k],
                                                device_id=peer, device_id_type=MESH)
            copy.wait_send()
            copy.wait_recv()

    out = pl.pallas_call(
        body, name=name, out_shape=[pltpu.HBM(b.shape, b.dtype) for b in bufs],
        in_specs=[HBM] * n + [SEM, SEM] + [ANY] * len(after), out_specs=[HBM] * n,
        input_output_aliases={k: k for k in range(n)},
        compiler_params=pltpu.CompilerParams(has_side_effects=EFFECT),
    )(*bufs, send_sems, recv_sems, *after)
    return list(out)


def _row_tile(rows, cols, max_elements=256 * 1024):
    best = 16
    for t in range(16, rows + 1, 16):
        if rows % t == 0 and t * cols <= max_elements:
            best = t
    return best


def cast_into_slot(name, param, index, chip):
    rows, cols = param.shape[-2:]
    tr = _row_tile(rows, cols)
    lead = (None,) * len(index)

    def body(chip_ref, s_ref, o_ref):
        del chip_ref
        o_ref[...] = s_ref[...].astype(BF16)

    return pl.pallas_call(
        body, out_shape=jax.ShapeDtypeStruct((N_CHIPS, rows, cols), BF16),
        grid_spec=pltpu.PrefetchScalarGridSpec(
            num_scalar_prefetch=1, grid=(rows // tr,),
            in_specs=[pl.BlockSpec(lead + (tr, cols), lambda r, chip_ref: index + (r, 0))],
            out_specs=pl.BlockSpec((None, tr, cols), lambda r, chip_ref: (chip_ref[0], r, 0))),
        name=name, compiler_params=_params(("parallel",)),
    )(chip, param)


GATHER_STAGES = ((0,), (1,), (2,), (3, 4, 5))
REDUCE_STAGES = ((5, 4, 3), (2,), (1,), (0,))


def _gather_copies(slots, x, y, c):
    me = 2 * x + y
    out = []
    for s in slots:
        for fx, fy in CHIP_FLIPS:
            mine = _half(s.at[me], 0, c)
            out.append((mine, mine, _half(s.at[2 * (x ^ fx) + (y ^ fy)], 0, c), (x ^ fx, y ^ fy, c)))
    return out


class WeightStream:
    def __init__(self, shards, chip, after=()):
        self.pending, self.ready = {}, {}
        token = tuple(after)
        for si, units in enumerate(GATHER_STAGES):
            slots = [cast_into_slot(f"cast_{u}_{t}", p, idx, chip) for u in units for t, (p, idx) in enumerate(shards[u])]
            send, recv, slots, tok = split_start(f"gather_start_{si}", slots, token, 3 * len(slots), _gather_copies)
            self.pending[si] = (send, recv, slots)
            token = (tok,)
        self.token = token

    def get(self, u, after):
        if u not in self.ready:
            si = next(k for k, units in enumerate(GATHER_STAGES) if u in units)
            send, recv, slots = self.pending.pop(si)
            slots = split_wait(f"gather_wait_{si}", slots, send, recv, (after,) + self.token, _gather_copies)
            self.token = ()

            @_plan(0, 3 * len(slots))
            def to_sibling(ins, outs, x, y, c):
                sends, recvs = [], []
                for o in outs:
                    for fx, fy in CHIP_FLIPS:
                        slab = o.at[2 * (x ^ fx) + (y ^ fy)]
                        sends.append((_half(slab, 0, c), _half(slab, 0, c), (x, y, 1 - c)))
                        recvs.append(_half(slab, 0, 1 - c))
                return [], sends, recvs

            shapes = [jax.ShapeDtypeStruct(s.shape, BF16) for s in slots]
            slots = exchange(f"gather_sibling_{si}", slots, shapes, {k: k for k in range(len(slots))}, to_sibling)
            for i, v in enumerate(GATHER_STAGES[si]):
                self.ready[v] = tuple(slots[3 * i:3 * i + 3])
        return self.ready[u]


def _reduce_copies(bufs, x, y, c):
    n = len(bufs) // 2
    out = []
    for s, land in zip(bufs[:n], bufs[n:]):
        for k, (fx, fy) in enumerate(CHIP_FLIPS):
            out.append((s.at[2 * (x ^ fx) + (y ^ fy)], land.at[k], land.at[k], (x ^ fx, y ^ fy, c)))
    return out


GRAD_SLOTS = {"gate": (2 * DEPTH, FF_SHARD, D_MODEL), "up": (2 * DEPTH, FF_SHARD, D_MODEL),
              "down": (2 * DEPTH, FF_SHARD, D_MODEL), "in": (DEPTH, D_MODEL, IN_SHARD),
              "br": (DEPTH, BR_ROWS, D_SHARD), "out": (DEPTH, D_SHARD, D_MODEL)}


def _unit_tensors(u):
    l, j = divmod(u, 3)
    if j == 1:
        return [("in", l), ("br", l), ("out", l)]
    return [(k, 2 * l + j // 2) for k in ("gate", "up", "down")]


class GradStream:
    def __init__(self, chip, core):
        self.core = core
        self.place = jnp.concatenate([chip, core])
        self.held, self.flying = {}, []
        self.full = {k: lax.empty(shape, F32) for k, shape in GRAD_SLOTS.items()}

    def put(self, u, grads, after=()):
        self.held[u] = grads
        si = len(self.flying)
        units = REDUCE_STAGES[si]
        if not all(v in self.held for v in units):
            return jnp.zeros((1, 1), F32)
        gs = [g for v in units for g in self.held[v]]

        @_plan(0, len(gs))
        def swap_halves(ins, outs, x, y, c):
            sends = [(_half(g, 1, 1 - c), o, (x, y, 1 - c)) for g, o in zip(ins, outs)]
            return [], sends, list(outs)

        half_shapes = [jax.ShapeDtypeStruct((N_CHIPS, g.shape[1] // 2, g.shape[2]), BF16) for g in gs]
        landed = exchange(f"reduce_swap_{si}", gs + list(after), half_shapes, {}, swap_halves)
        sums = [_add_halves(g, la, self.core) for g, la in zip(gs, landed)]
        landing = [lax.empty((3,) + s.shape[1:], BF16) for s in sums]
        send, recv, bufs, token = split_start(f"reduce_start_{si}", sums + landing, (), 3 * len(sums), _reduce_copies)
        self.flying.append((send, recv, bufs, [t for v in units for t in _unit_tensors(v)]))
        return token[0:1, 0:1]

    def finish(self, after):
        for si, (send, recv, bufs, tensors) in enumerate(self.flying):
            bufs = split_wait(f"reduce_wait_{si}", bufs, send, recv, tuple(after), _reduce_copies)
            n = len(tensors)
            for (name, slot), s, land in zip(tensors, bufs[:n], bufs[n:]):
                self.full[name] = _add_chips(s, land, self.place, self.full[name], slot)
        names = list(self.full)

        @_plan(0, len(names))
        def share_halves(ins, outs, x, y, c):
            sends = [(_half(o, 1, c), _half(o, 1, c), (x, y, 1 - c)) for o in outs]
            return [], sends, [_half(o, 1, 1 - c) for o in outs]

        shapes = [jax.ShapeDtypeStruct(self.full[k].shape, F32) for k in names]
        out = exchange("reduce_share_halves", [self.full[k] for k in names], shapes, {k: k for k in range(len(names))},
                       share_halves)
        return dict(zip(names, out))


def _add_halves(g, landed, core):
    _, rh, cols = landed.shape
    tr = _row_tile(rh, cols, 1024 * 1024)
    per_half = rh // tr

    def body(core_ref, g_ref, la_ref, o_ref):
        del core_ref
        o_ref[...] = (g_ref[...].astype(F32) + la_ref[...].astype(F32)).astype(BF16)

    blk = (None, tr, cols)
    return pl.pallas_call(
        body, out_shape=jax.ShapeDtypeStruct(landed.shape, BF16),
        grid_spec=pltpu.PrefetchScalarGridSpec(
            num_scalar_prefetch=1, grid=(N_CHIPS, per_half),
            in_specs=[pl.BlockSpec(blk, lambda j, r, core_ref: (j, core_ref[0] * per_half + r, 0)),
                      pl.BlockSpec(blk, lambda j, r, core_ref: (j, r, 0))],
            out_specs=pl.BlockSpec(blk, lambda j, r, core_ref: (j, r, 0))),
        name="reduce_add_halves", compiler_params=_params(("parallel", "parallel")),
    )(core, g, landed)


def _add_chips(sums, landed, place, full, slot):
    _, rh, cols = sums.shape
    tr = _row_tile(rh, cols, 1024 * 1024)
    per_half = rh // tr

    def body(place_ref, s_ref, la_ref, full_in, o_ref):
        del place_ref, full_in
        o_ref[...] = ((s_ref[...].astype(F32) + la_ref[0].astype(F32)) + la_ref[1].astype(F32)) + la_ref[2].astype(F32)

    return pl.pallas_call(
        body, out_shape=jax.ShapeDtypeStruct(full.shape, F32),
        grid_spec=pltpu.PrefetchScalarGridSpec(
            num_scalar_prefetch=1, grid=(per_half,),
            in_specs=[pl.BlockSpec((None, tr, cols), lambda r, place_ref: (place_ref[0], r, 0)),
                      pl.BlockSpec((3, tr, cols), lambda r, place_ref: (0, r, 0)), ANY],
            out_specs=pl.BlockSpec((None, tr, cols), lambda r, place_ref: (slot, place_ref[1] * per_half + r, 0))),
        input_output_aliases={3: 0}, name="reduce_add_chips", compiler_params=_params(("parallel",)),
    )(place, sums, landed, full)


def sum_devices(parts):
    def body(p_ref, o_ref):
        acc = p_ref[0]
        for d in range(1, N_DEV):
            acc = acc + p_ref[d]
        o_ref[...] = acc

    return pl.pallas_call(body, out_shape=jax.ShapeDtypeStruct(parts.shape[1:], F32), name="sum_devices")(parts)


ADA_SHARD = 9 * D_MODEL // N_CHIPS
ADA_TILE = 768
ADA_ROWS = 16


def ada_forward(c_rows, w_ada, b_shard):
    def body(c_ref, w_ref, b_ref, o_ref):
        cv = c_ref[...]
        o_ref[...] = _dot((cv * _sigmoid(cv)).astype(BF16), w_ref[...].astype(BF16), NN) + b_ref[...]

    return pl.pallas_call(
        body, grid=(DEPTH, ADA_SHARD // ADA_TILE),
        in_specs=[pl.BlockSpec((ADA_ROWS, D_MODEL), lambda l, n: (0, 0)),
                  pl.BlockSpec((None, D_MODEL, ADA_TILE), lambda l, n: (l, 0, n)),
                  pl.BlockSpec((None, 1, ADA_TILE), lambda l, n: (l, 0, n))],
        out_specs=pl.BlockSpec((None, ADA_ROWS, ADA_TILE), lambda l, n: (l, 0, n)),
        out_shape=jax.ShapeDtypeStruct((DEPTH, ADA_ROWS, ADA_SHARD), F32),
        name="ada_forward", compiler_params=_params(("parallel", "parallel")),
    )(c_rows, w_ada, b_shard)


def ada_backward(c_rows, dmod_rows):
    def body(c_ref, d_ref, o_ref):
        cv = c_ref[...]
        o_ref[...] = _dot((cv * _sigmoid(cv)).astype(BF16), d_ref[...].astype(BF16), TN)

    return pl.pallas_call(
        body, grid=(DEPTH, ADA_SHARD // ADA_TILE),
        in_specs=[pl.BlockSpec((ADA_ROWS, D_MODEL), lambda l, n: (0, 0)),
                  pl.BlockSpec((None, ADA_ROWS, ADA_TILE), lambda l, n: (l, 0, n))],
        out_specs=pl.BlockSpec((None, D_MODEL, ADA_TILE), lambda l, n: (l, 0, n)),
        out_shape=jax.ShapeDtypeStruct((DEPTH, D_MODEL, ADA_SHARD), F32),
        name="ada_backward", compiler_params=_params(("parallel", "parallel")),
    )(c_rows, dmod_rows)


def adamw(name, w, g, m, v):
    shape = w.shape
    cols = shape[-1]
    rows = w.size // cols
    tr = _row_tile(rows, cols) if rows % 16 == 0 else rows
    c1 = 1.0 / (1.0 - ADAM_B1 ** ADAM_STEP)
    c2 = 1.0 / (1.0 - ADAM_B2 ** ADAM_STEP)

    def body(w_ref, g_ref, m_ref, v_ref, go_ref, d_ref, mo_ref, vo_ref):
        gv = g_ref[...]
        mn = ADAM_B1 * m_ref[...] + (1.0 - ADAM_B1) * gv
        vn = ADAM_B2 * v_ref[...] + (1.0 - ADAM_B2) * (gv * gv)
        go_ref[...] = gv
        mo_ref[...] = mn
        vo_ref[...] = vn
        d_ref[...] = -ADAM_LR * ((mn * c1) / (jnp.sqrt(vn * c2) + ADAM_EPS) + ADAM_WD * w_ref[...])

    spec = pl.BlockSpec((tr, cols), lambda i: (i, 0))
    out = jax.ShapeDtypeStruct((rows, cols), F32)
    res = pl.pallas_call(
        body, grid=(rows // tr,), in_specs=[spec] * 4, out_specs=[spec] * 4, out_shape=[out] * 4,
        name=name, compiler_params=_params(("parallel",)),
    )(*[t.reshape(rows, cols) for t in (w, g, m, v)])
    return tuple(r.reshape(shape) for r in res)


def _pack(parts, rows):
    flat = jnp.concatenate([p.reshape(-1) for p in parts])
    return jnp.pad(flat, (0, rows * 128 - flat.size)).reshape(rows, 128)


def _unpack(flat, shapes):
    out, at = [], 0
    for s in shapes:
        n = math.prod(s)
        out.append(flat[at:at + n].reshape(s))
        at += n
    return out


def kernel(x, c, w_ada, b_ada, norm_gain, w_ffn_gate, w_ffn_up, w_ffn_down, w_in, w_br_sb, w_br_dil, w_br_swa, w_out, sinks, rel_bias, final_gain, loss_target, m_w_ada, m_b_ada, m_norm_gain, m_w_ffn_gate, m_w_ffn_up, m_w_ffn_down, m_w_in, m_w_br_sb, m_w_br_dil, m_w_br_swa, m_w_out, m_sinks, m_rel_bias, m_final_gain, v_w_ada, v_b_ada, v_norm_gain, v_w_ffn_gate, v_w_ffn_up, v_w_ffn_down, v_w_in, v_w_br_sb, v_w_br_dil, v_w_br_swa, v_w_out, v_sinks, v_rel_bias, v_final_gain):
    xi, yi, ci = _position()
    chip = 2 * xi + yi
    dev = 2 * chip + ci

    c_all = all_gather_small("gather_c", c.reshape(8, 128)).reshape(N_DEV, D_MODEL)
    c_rows = jnp.pad(c_all, ((0, ADA_ROWS - N_DEV), (0, 0)))
    b_shard = lax.dynamic_slice_in_dim(b_ada, chip * ADA_SHARD, ADA_SHARD, axis=1).reshape(DEPTH, 1, ADA_SHARD)
    mod_shard = ada_forward(c_rows, w_ada, b_shard)[:, :N_DEV]
    n_mod = DEPTH * N_DEV * ADA_SHARD
    gathered = all_gather_small("gather_mod", _pack([mod_shard, norm_gain], 304))[::2].reshape(N_CHIPS, -1)
    mod_all = gathered[:, :n_mod].reshape(N_CHIPS, DEPTH, N_DEV, ADA_SHARD)
    mod = lax.dynamic_index_in_dim(mod_all, dev, axis=2, keepdims=False)
    mod = mod.transpose(1, 0, 2).reshape(DEPTH, 3, 3, D_MODEL)
    gains = gathered[:, n_mod:n_mod + DEPTH * 3 * D_SHARD].reshape(N_CHIPS, DEPTH, 3, D_SHARD)
    gains = gains.transpose(1, 2, 0, 3).reshape(DEPTH, 3, D_MODEL)

    chip_i, core_i = chip.astype(jnp.int32).reshape(1), ci.astype(jnp.int32).reshape(1)
    w_br = jnp.concatenate([w_br_sb, w_br_dil, w_br_swa], axis=1)
    transposed = (3, 4)
    w_gate_t, w_up_t = jnp.swapaxes(w_ffn_gate, 2, 3), jnp.swapaxes(w_ffn_up, 2, 3)
    shards = []
    for l in range(DEPTH):
        ffn = [[(w_gate_t, (l, f)), (w_up_t, (l, f)), (w_ffn_down, (l, f))] for f in range(2)]
        shards += [ffn[0], [(w_in, (l,)), (w_br, (l,)), (w_out, (l,))], ffn[1]]
    weights_in = WeightStream(shards, chip_i, (gathered,))
    grads_out = GradStream(chip_i, core_i)

    loss, dx, dmod, dgains, dfinal, dsinks, drel, last_grads = device_step(
        x[0], loss_target[0], mod, gains, final_gain, sinks, rel_bias, weights_in.get, grads_out.put)

    small_shapes = [(DEPTH, 9 * D_MODEL), (DEPTH, 3, D_MODEL), (D_MODEL,), (DEPTH, H_SWA_Q), (N_BUCKETS, 12), (1,)]
    small_all = all_gather_small("gather_small_grads", _pack([dmod, dgains, dfinal, dsinks, drel, loss[0, 0:1]], 208))
    started = grads_out.put(0, last_grads, after=(small_all,))
    small_all = small_all + started
    g_b_ada, g_gain_full, g_final, g_sinks, g_rel, loss_sum = _unpack(sum_devices(small_all).reshape(-1), small_shapes)
    g_gain = lax.dynamic_slice_in_dim(g_gain_full, chip * D_SHARD, D_SHARD, axis=2)
    dmod_all = small_all.reshape(N_DEV, -1)[:, :DEPTH * 9 * D_MODEL].reshape(N_DEV, DEPTH, 9 * D_MODEL)
    dmod_rows = lax.dynamic_slice_in_dim(dmod_all, chip * ADA_SHARD, ADA_SHARD, axis=2).transpose(1, 0, 2)
    g_w_ada = ada_backward(c_rows, jnp.pad(dmod_rows, ((0, 0), (0, ADA_ROWS - N_DEV), (0, 0))))

    weights = [w_ada, b_ada, norm_gain, w_ffn_gate, w_ffn_up, w_ffn_down, w_in, w_br_sb, w_br_dil, w_br_swa, w_out,
               sinks, rel_bias, final_gain]
    ms = [m_w_ada, m_b_ada, m_norm_gain, m_w_ffn_gate, m_w_ffn_up, m_w_ffn_down, m_w_in, m_w_br_sb, m_w_br_dil,
          m_w_br_swa, m_w_out, m_sinks, m_rel_bias, m_final_gain]
    vs = [v_w_ada, v_b_ada, v_norm_gain, v_w_ffn_gate, v_w_ffn_up, v_w_ffn_down, v_w_in, v_w_br_sb, v_w_br_dil,
          v_w_br_swa, v_w_out, v_sinks, v_rel_bias, v_final_gain]
    grads = [g_w_ada, g_b_ada, g_gain] + [None] * 8 + [g_sinks, g_rel, g_final]

    small = (1, 2, 11, 12, 13)
    deltas, new_ms, new_vs = [None] * 14, [None] * 14, [None] * 14
    _, deltas[0], new_ms[0], new_vs[0] = adamw("adamw_0", weights[0], grads[0], ms[0], vs[0])
    shapes = [weights[k].shape for k in small]
    packed = [_pack([t[k] for k in small], 168) for t in (weights, grads, ms, vs)]
    for dst, res in zip((deltas, new_ms, new_vs), adamw("adamw_small", *packed)[1:]):
        for k, t in zip(small, _unpack(res.reshape(-1), shapes)):
            dst[k] = t

    g = grads_out.finish((dx, deltas[0], deltas[1]))
    g_br = g["br"]
    grads[3:11] = [g["gate"].reshape(w_gate_t.shape), g["up"].reshape(w_up_t.shape),
                   g["down"].reshape(w_ffn_down.shape), g["in"], g_br[:, 0:256], g_br[:, 256:384], g_br[:, 384:768],
                   g["out"]]
    for k in range(3, 11):
        state = [weights[k], ms[k], vs[k]]
        if k in transposed:
            state = [jnp.swapaxes(t, 2, 3) for t in state]
        out = adamw(f"adamw_{k}", state[0], grads[k], state[1], state[2])
        if k in transposed:
            out = [jnp.swapaxes(t, 2, 3) for t in out]
        grads[k], deltas[k], new_ms[k], new_vs[k] = out
    return (loss_sum[0], dx[None], *grads, *deltas, *new_ms, *new_vs)
```

```python
import functools
import math

import jax
import jax.numpy as jnp
from jax import lax
from jax.experimental import pallas as pl
from jax.experimental.pallas import tpu as pltpu

F32 = jnp.float32
BF16 = jnp.bfloat16

D_MODEL = 1024
SEQ = 2048
DEPTH = 2
HEAD_DIM = 64
BLK = 128
H_SB = 4
DIL_PATTERNS = ((128, 1), (512, 4), (2048, 16))
H_PER_DIL = 2
H_DIL = 6
H_SWA_Q = 6
H_SWA_KV = 2
SWA_WINDOW = 128
N_BUCKETS = 32
MAX_REL_DIST = 2048
D_FF = 2816
RMS_EPS = 1e-6
N_CHIPS = 4
N_DEV = 8
FF_SHARD = D_FF // N_CHIPS
D_QKV = 2560
D_IN = D_QKV + 3 * D_MODEL
IN_SHARD = D_IN // N_CHIPS
D_SHARD = D_MODEL // N_CHIPS
BR_ROWS = 768
NEG = -1e30
QK_SCALE = HEAD_DIM ** -0.5

ADAM_LR = 0.001
ADAM_B1 = 0.9
ADAM_B2 = 0.999
ADAM_EPS = 1e-08
ADAM_WD = 0.01
ADAM_STEP = 10

VMEM_LIMIT = 56 * 1024 * 1024
ROW_TILE = 256
MM_TILE = 1024

NN = (((1,), (0,)), ((), ()))
NT = (((1,), (1,)), ((), ()))
TN = (((0,), (0,)), ((), ()))


def _params(sem=None):
    return pltpu.CompilerParams(dimension_semantics=sem, vmem_limit_bytes=VMEM_LIMIT)


def _dot(a, b, dims):
    return lax.dot_general(a, b, dims, preferred_element_type=F32)


def _sigmoid(x):
    return 1.0 / (1.0 + jnp.exp(-x))


def _matmul(name, grid, nk, k_axis, dims, n_pairs, in_specs, out_specs, out_shape, acc_shape, epilogue,
            operands, sem, aliases=None, prologue=None):
    n_in = len(in_specs)
    n_out = len(out_specs)

    def partial(ins):
        tot = None
        for p in range(n_pairs):
            a = ins[2 * p][...]
            if prologue is not None:
                a = prologue(p, a, ins)
            d = _dot(a, ins[2 * p + 1][...], dims)
            tot = d if tot is None else tot + d
        return tot

    def body(*refs):
        ins, outs = refs[:n_in], refs[n_in:n_in + n_out]
        ids = tuple(pl.program_id(a) for a in range(len(grid)))
        if nk == 1:
            epilogue(partial(ins), ins, outs, ids)
            return
        acc = refs[n_in + n_out]
        k = ids[k_axis]

        @pl.when(k == 0)
        def _():
            acc[...] = partial(ins)

        @pl.when(k > 0)
        def _():
            acc[...] += partial(ins)

        @pl.when(k == nk - 1)
        def _():
            epilogue(acc[...], ins, outs, ids)

    return pl.pallas_call(
        body, grid=grid, in_specs=in_specs, out_specs=out_specs, out_shape=out_shape,
        scratch_shapes=[] if nk == 1 else [pltpu.VMEM(acc_shape, F32)],
        input_output_aliases=aliases or {}, name=name, compiler_params=_params(sem),
    )(*operands)


def _row_spec(width=D_MODEL):
    return pl.BlockSpec((ROW_TILE, width), lambda i: (i, 0))


def _vec_spec(rows=1, width=D_MODEL):
    return pl.BlockSpec((rows, width), lambda i: (0, 0))


class Row:
    def __init__(self, table, index):
        self.table, self.index = table, index

    def spec(self):
        index = self.index
        return pl.BlockSpec((None, 1, D_MODEL), lambda *ids: (index, 0, 0))


def _slot_spec(u):
    return pl.BlockSpec((8, D_MODEL), lambda *ids: (u, 0))


def prenorm(x, gain, scale, shift):
    def body(x_ref, g_ref, sc_ref, sh_ref, h_ref):
        xv = x_ref[...]
        r = lax.rsqrt(jnp.mean(xv * xv, axis=-1, keepdims=True) + RMS_EPS)
        h_ref[...] = (((xv * r) * g_ref[...]) * (1.0 + sc_ref[...]) + sh_ref[...]).astype(BF16)

    return pl.pallas_call(
        body, grid=(SEQ // ROW_TILE,), in_specs=[_row_spec(), gain.spec(), scale.spec(), shift.spec()],
        out_specs=_row_spec(), out_shape=jax.ShapeDtypeStruct((SEQ, D_MODEL), BF16),
        name="prenorm", compiler_params=_params(("parallel",)),
    )(x, gain.table, scale.table, shift.table)


def resid_bwd(dxo, f, coef, mult, sums, u):
    def body(dx_ref, f_ref, c_ref, sums_in, df_ref, dc_ref):
        del sums_in
        dx = dx_ref[...]
        df_ref[...] = (dx * (mult * c_ref[...])).astype(BF16)
        part = mult * jnp.sum(dx * f_ref[...], axis=0, keepdims=True)

        @pl.when(pl.program_id(0) == 0)
        def _():
            dc_ref[...] = jnp.zeros_like(dc_ref)

        dc_ref[0:1, :] += part

    return pl.pallas_call(
        body, grid=(SEQ // ROW_TILE,),
        in_specs=[_row_spec(), _row_spec(), coef.spec(), pl.BlockSpec(memory_space=pl.ANY)],
        out_specs=[_row_spec(), _slot_spec(u)],
        out_shape=[jax.ShapeDtypeStruct((SEQ, D_MODEL), BF16), jax.ShapeDtypeStruct(sums.shape, F32)],
        input_output_aliases={3: 1}, name="resid_bwd", compiler_params=_params(("arbitrary",)),
    )(dxo, f, coef.table, sums)


def final_loss(x, gain, target):
    def body(x_ref, g_ref, t_ref, loss_ref, dx_ref, dg_ref):
        xv = x_ref[...]
        g = g_ref[...]
        r = lax.rsqrt(jnp.mean(xv * xv, axis=-1, keepdims=True) + RMS_EPS)
        xh = xv * r
        e = xh * g - t_ref[...]
        part = 0.5 * jnp.sum(jnp.mean(e * e, axis=-1, keepdims=True), axis=0, keepdims=True)
        dy = e * (1.0 / D_MODEL)
        dyg = dy * g
        dx_ref[...] = r * (dyg - xh * jnp.mean(dyg * xh, axis=-1, keepdims=True))

        @pl.when(pl.program_id(0) == 0)
        def _():
            loss_ref[...] = jnp.zeros_like(loss_ref)
            dg_ref[...] = jnp.zeros_like(dg_ref)

        loss_ref[...] += jnp.broadcast_to(part, loss_ref.shape)
        dg_ref[0:1, :] += jnp.sum(dy * xh, axis=0, keepdims=True)

    return pl.pallas_call(
        body, grid=(SEQ // ROW_TILE,), in_specs=[_row_spec(), _vec_spec(), _row_spec()],
        out_specs=[_vec_spec(8, 128), _row_spec(), _vec_spec(8)],
        out_shape=[jax.ShapeDtypeStruct((8, 128), F32), jax.ShapeDtypeStruct((SEQ, D_MODEL), F32),
                   jax.ShapeDtypeStruct((8, D_MODEL), F32)],
        name="final_loss", compiler_params=_params(("arbitrary",)),
    )(x, gain, target)


def _prenorm_bwd_epilogue(dh, x_ref, dxo_ref, g_ref, sc_ref, dx_ref, stats_ref, first):
    xv = x_ref[...]
    g = g_ref[...]
    r = lax.rsqrt(jnp.mean(xv * xv, axis=-1, keepdims=True) + RMS_EPS)
    xh = xv * r
    dn = dh * (1.0 + sc_ref[...])
    dxh = dn * g
    dx = dxo_ref[...] + r * (dxh - xh * jnp.mean(dxh * xh, axis=-1, keepdims=True))
    dx_ref[...] = dx

    @pl.when(first)
    def _():
        stats_ref[...] = jnp.zeros_like(stats_ref)

    stats_ref[0:1, :] += jnp.sum(dh, axis=0, keepdims=True)
    stats_ref[1:2, :] += jnp.sum(dh * (xh * g), axis=0, keepdims=True)
    stats_ref[2:3, :] += jnp.sum(dn * xh, axis=0, keepdims=True)
    return dx


def _resid_bwd_epilogue(dx, f_ref, c_ref, mult, df_ref, dc_ref, first):
    df_ref[...] = (dx * (mult * c_ref[...])).astype(BF16)

    @pl.when(first)
    def _():
        dc_ref[...] = jnp.zeros_like(dc_ref)

    dc_ref[0:1, :] += mult * jnp.sum(dx * f_ref[...], axis=0, keepdims=True)


def ffn_up(h, wg_all, wu_all):
    def body(h_ref, wg_ref, wu_ref, a_ref, b_ref, s_ref):
        hv = h_ref[...]
        a = _dot(hv, wg_ref[...], NT)
        b = _dot(hv, wu_ref[...], NT)
        a_ref[...] = a.astype(BF16)
        b_ref[...] = b.astype(BF16)
        s_ref[...] = (a * _sigmoid(a) * b).astype(BF16)

    w_spec = pl.BlockSpec((None, FF_SHARD, D_MODEL), lambda j, i: (j, 0, 0))
    o_spec = pl.BlockSpec((None, MM_TILE, FF_SHARD), lambda j, i: (j, i, 0))
    hid = (N_CHIPS, SEQ, FF_SHARD)
    return pl.pallas_call(
        body, grid=(N_CHIPS, SEQ // MM_TILE),
        in_specs=[pl.BlockSpec((MM_TILE, D_MODEL), lambda j, i: (i, 0)), w_spec, w_spec],
        out_specs=[o_spec, o_spec, o_spec],
        out_shape=[jax.ShapeDtypeStruct(hid, BF16)] * 3,
        name="ffn_up", compiler_params=_params(("parallel", "parallel")),
    )(h, wg_all, wu_all)


def matmul_residual(name, a, a_spec, w_all, w_spec, x, coef, mult, then=None):
    def epilogue(acc, ins, outs, ids):
        outs[0][...] = acc
        xo = ins[2][...] + (mult * ins[3][...]) * acc
        outs[1][...] = xo
        if then is not None:
            r = lax.rsqrt(jnp.mean(xo * xo, axis=-1, keepdims=True) + RMS_EPS)
            outs[2][...] = (((xo * r) * ins[4][...]) * (1.0 + ins[5][...]) + ins[6][...]).astype(BF16)

    row = pl.BlockSpec((MM_TILE, D_MODEL), lambda i, j: (i, 0))
    f32 = jax.ShapeDtypeStruct((SEQ, D_MODEL), F32)
    extra = [] if then is None else [then.gain, then.scale, then.shift]
    return _matmul(
        name, (SEQ // MM_TILE, N_CHIPS), N_CHIPS, 1, NN, 1,
        [a_spec, w_spec, row, coef.spec()] + [t.spec() for t in extra], [row] * (2 + bool(extra)),
        [f32, f32] + [jax.ShapeDtypeStruct((SEQ, D_MODEL), BF16)] * bool(extra), (MM_TILE, D_MODEL), epilogue,
        (a, w_all, x, coef.table) + tuple(t.table for t in extra), ("parallel", "arbitrary"))


def ffn_down(s, wd_all, x, gate, then):
    return matmul_residual(
        "ffn_down", s, pl.BlockSpec((None, MM_TILE, FF_SHARD), lambda i, j: (j, i, 0)),
        wd_all, pl.BlockSpec((None, FF_SHARD, D_MODEL), lambda i, j: (j, 0, 0)), x, gate, 0.5, then)


def ffn_bwd_hidden(df, wd_all, a, b):
    def epilogue(ds, ins, outs, ids):
        av, bv = ins[2][...].astype(F32), ins[3][...].astype(F32)
        sig = _sigmoid(av)
        outs[0][...] = (ds * bv * (sig * (1.0 + av * (1.0 - sig)))).astype(BF16)
        outs[1][...] = (ds * (av * sig)).astype(BF16)

    hid_spec = pl.BlockSpec((None, MM_TILE, FF_SHARD), lambda j, i: (j, i, 0))
    hid = jax.ShapeDtypeStruct((N_CHIPS, SEQ, FF_SHARD), BF16)
    return _matmul(
        "ffn_bwd_hidden", (N_CHIPS, SEQ // MM_TILE), 1, None, NT, 1,
        [pl.BlockSpec((MM_TILE, D_MODEL), lambda j, i: (i, 0)),
         pl.BlockSpec((None, FF_SHARD, D_MODEL), lambda j, i: (j, 0, 0)), hid_spec, hid_spec],
        [hid_spec, hid_spec], [hid, hid], None, epilogue, (df, wd_all, a, b), ("parallel", "parallel"))


def grad_weight(name, lhs, lhs_spec, rhs, rhs_spec, shape):
    def epilogue(acc, ins, outs, ids):
        outs[0][...] = acc.astype(BF16)

    return _matmul(
        name, (N_CHIPS, SEQ // MM_TILE), SEQ // MM_TILE, 1, TN, 1,
        [lhs_spec, rhs_spec], [pl.BlockSpec((None,) + shape, lambda j, k: (j, 0, 0))],
        [jax.ShapeDtypeStruct((N_CHIPS,) + shape, BF16)], shape, epilogue, (lhs, rhs), ("parallel", "arbitrary"))[0]


def ffn_grad_weights(h, s, df, da, db):
    tok = pl.BlockSpec((MM_TILE, D_MODEL), lambda j, k: (k, 0))
    hid = pl.BlockSpec((None, MM_TILE, FF_SHARD), lambda j, k: (j, k, 0))
    return (grad_weight("grad_w_gate", da, hid, h, tok, (FF_SHARD, D_MODEL)),
            grad_weight("grad_w_up", db, hid, h, tok, (FF_SHARD, D_MODEL)),
            grad_weight("grad_w_down", s, hid, df, tok, (FF_SHARD, D_MODEL)))


BWD_TILE = 512


def matmul_prenorm_bwd(name, dims, pairs, pair_specs, x, dxo, gain, scale, sums, u, below):
    n = len(pairs)

    def epilogue(dh, ins, outs, ids):
        first = ids[0] == 0
        dx = _prenorm_bwd_epilogue(dh, ins[n], ins[n + 1], ins[n + 2], ins[n + 3], outs[0], outs[1], first)
        if below is not None:
            _resid_bwd_epilogue(dx, ins[n + 5], ins[n + 6], below[2], outs[2], outs[3], first)

    row = pl.BlockSpec((BWD_TILE, D_MODEL), lambda i, j: (i, 0))
    any_spec = pl.BlockSpec(memory_space=pl.ANY)
    f32 = jax.ShapeDtypeStruct((SEQ, D_MODEL), F32)
    in_specs = list(pair_specs) + [row, row, gain.spec(), scale.spec(), any_spec]
    operands = tuple(pairs) + (x, dxo, gain.table, scale.table, sums[0])
    out_specs, out_shape, aliases = [row, _slot_spec(u)], [f32, jax.ShapeDtypeStruct(sums[0].shape, F32)], {n + 4: 1}
    if below is not None:
        in_specs += [row, below[1].spec(), any_spec]
        operands += (below[0], below[1].table, sums[1])
        out_specs += [row, _slot_spec(u - 1)]
        out_shape += [jax.ShapeDtypeStruct((SEQ, D_MODEL), BF16), jax.ShapeDtypeStruct(sums[1].shape, F32)]
        aliases[n + 7] = 3
    out = _matmul(name, (SEQ // BWD_TILE, N_CHIPS), N_CHIPS, 1, dims, len(pairs) // 2, in_specs, out_specs, out_shape,
                  (BWD_TILE, D_MODEL), epilogue, operands, ("arbitrary", "arbitrary"), aliases=aliases)
    if below is None:
        return out[0], (out[1], sums[1])
    return out[0], (out[1], out[3]), out[2]


def ffn_bwd_input(da, db, wg_all, wu_all, x, dxo, gain, scale, sums, u, below):
    hid = pl.BlockSpec((None, BWD_TILE, FF_SHARD), lambda i, j: (j, i, 0))
    w = pl.BlockSpec((None, FF_SHARD, D_MODEL), lambda i, j: (j, 0, 0))
    return matmul_prenorm_bwd("ffn_bwd_input", NN, (da, wg_all, db, wu_all), (hid, w, hid, w), x, dxo, gain, scale,
                              sums, u, below)


def in_proj(h, w_all):
    def epilogue(acc, ins, outs, ids):
        outs[0][...] = acc
        outs[1][...] = acc.astype(BF16)

    out = pl.BlockSpec((MM_TILE, IN_SHARD), lambda j, i: (i, j))
    return _matmul(
        "in_proj", (N_CHIPS, SEQ // MM_TILE), 1, None, NN, 1,
        [pl.BlockSpec((MM_TILE, D_MODEL), lambda j, i: (i, 0)),
         pl.BlockSpec((None, D_MODEL, IN_SHARD), lambda j, i: (j, 0, 0))],
        [out, out], [jax.ShapeDtypeStruct((SEQ, D_IN), F32), jax.ShapeDtypeStruct((SEQ, D_IN), BF16)],
        None, epilogue, (h, w_all), ("parallel", "parallel"))


_GATE_BLOCK0 = D_QKV // D_SHARD


def _branch_products(o, w_ref):
    ob = o.astype(BF16)
    return (_dot(ob[:, 0:256], w_ref[0:256, :], NN), _dot(ob[:, 256:384], w_ref[256:384, :], NN),
            _dot(ob[:, 384:768], w_ref[384:768, :], NN))


def merge_branches(o_cat, wbr_all, proj):
    def body(o_ref, w_ref, g0_ref, g1_ref, g2_ref, m_ref):
        u = _branch_products(o_ref[...], w_ref)
        m_ref[...] = (_sigmoid(g0_ref[...]) * u[0] + _sigmoid(g1_ref[...]) * u[1]
                      + _sigmoid(g2_ref[...]) * u[2]).astype(BF16)

    def gate_spec(b):
        return pl.BlockSpec((MM_TILE, D_SHARD), lambda i, j: (i, _GATE_BLOCK0 + 4 * b + j))

    return pl.pallas_call(
        body, grid=(SEQ // MM_TILE, N_CHIPS),
        in_specs=[pl.BlockSpec((MM_TILE, BR_ROWS), lambda i, j: (i, 0)),
                  pl.BlockSpec((None, BR_ROWS, D_SHARD), lambda i, j: (j, 0, 0)),
                  gate_spec(0), gate_spec(1), gate_spec(2)],
        out_specs=pl.BlockSpec((MM_TILE, D_SHARD), lambda i, j: (i, j)),
        out_shape=jax.ShapeDtypeStruct((SEQ, D_MODEL), BF16),
        name="merge_branches", compiler_params=_params(("parallel", "parallel")),
    )(o_cat, wbr_all, proj, proj, proj)


def out_proj(merged, wout_all, x, gate, then):
    return matmul_residual(
        "out_proj", merged, pl.BlockSpec((MM_TILE, D_SHARD), lambda i, j: (i, j)),
        wout_all, pl.BlockSpec((None, D_SHARD, D_MODEL), lambda i, j: (j, 0, 0)), x, gate, 1.0, then)


def merge_bwd(dmo, wout_all, o_cat, wbr_all, proj):
    def epilogue(dm, ins, outs, ids):
        u = _branch_products(ins[2][...], ins[3])
        for b in range(3):
            sig = _sigmoid(ins[4 + b][...])
            outs[b][...] = (dm * sig).astype(BF16)
            outs[3 + b][...] = (dm * u[b] * (sig * (1.0 - sig))).astype(BF16)

    def gate_spec(b):
        return pl.BlockSpec((MM_TILE, D_SHARD), lambda j, i: (i, _GATE_BLOCK0 + 4 * b + j))

    col = pl.BlockSpec((MM_TILE, D_SHARD), lambda j, i: (i, j))
    du = jax.ShapeDtypeStruct((SEQ, D_MODEL), BF16)
    return _matmul(
        "merge_bwd", (N_CHIPS, SEQ // MM_TILE), 1, None, NT, 1,
        [pl.BlockSpec((MM_TILE, D_MODEL), lambda j, i: (i, 0)),
         pl.BlockSpec((None, D_SHARD, D_MODEL), lambda j, i: (j, 0, 0)),
         pl.BlockSpec((MM_TILE, BR_ROWS), lambda j, i: (i, 0)),
         pl.BlockSpec((None, BR_ROWS, D_SHARD), lambda j, i: (j, 0, 0)),
         gate_spec(0), gate_spec(1), gate_spec(2)],
        [col] * 6, [du] * 6,
        None, epilogue, (dmo, wout_all, o_cat, wbr_all, proj, proj, proj), ("parallel", "parallel"))


def branch_bwd_input(du, wbr_all):
    def body(d0_ref, d1_ref, d2_ref, w_ref, o_ref, acc):
        j = pl.program_id(1)
        parts = (_dot(d0_ref[...], w_ref[0:256, :], NT), _dot(d1_ref[...], w_ref[256:384, :], NT),
                 _dot(d2_ref[...], w_ref[384:768, :], NT))

        @pl.when(j == 0)
        def _():
            acc[:, 0:256], acc[:, 256:384], acc[:, 384:768] = parts

        @pl.when(j > 0)
        def _():
            acc[:, 0:256] += parts[0]
            acc[:, 256:384] += parts[1]
            acc[:, 384:768] += parts[2]

        @pl.when(j == N_CHIPS - 1)
        def _():
            o_ref[...] = acc[...]

    col = pl.BlockSpec((MM_TILE, D_SHARD), lambda i, j: (i, j))
    return pl.pallas_call(
        body, grid=(SEQ // MM_TILE, N_CHIPS),
        in_specs=[col, col, col, pl.BlockSpec((None, BR_ROWS, D_SHARD), lambda i, j: (j, 0, 0))],
        out_specs=pl.BlockSpec((MM_TILE, BR_ROWS), lambda i, j: (i, 0)),
        out_shape=jax.ShapeDtypeStruct((SEQ, BR_ROWS), F32),
        scratch_shapes=[pltpu.VMEM((MM_TILE, BR_ROWS), F32)],
        name="branch_bwd_input", compiler_params=_params(("parallel", "arbitrary")),
    )(du[0], du[1], du[2], wbr_all)


def branch_grad_weights(o_cat, du):
    def body(o_ref, d0_ref, d1_ref, d2_ref, g_ref, acc):
        k = pl.program_id(1)
        ob = o_ref[...].astype(BF16)
        parts = (_dot(ob[:, 0:256], d0_ref[...], TN), _dot(ob[:, 256:384], d1_ref[...], TN),
                 _dot(ob[:, 384:768], d2_ref[...], TN))

        @pl.when(k == 0)
        def _():
            acc[0:256, :], acc[256:384, :], acc[384:768, :] = parts

        @pl.when(k > 0)
        def _():
            acc[0:256, :] += parts[0]
            acc[256:384, :] += parts[1]
            acc[384:768, :] += parts[2]

        @pl.when(k == SEQ // MM_TILE - 1)
        def _():
            g_ref[...] = acc[...].astype(BF16)

    col = pl.BlockSpec((MM_TILE, D_SHARD), lambda j, k: (k, j))
    return pl.pallas_call(
        body, grid=(N_CHIPS, SEQ // MM_TILE),
        in_specs=[pl.BlockSpec((MM_TILE, BR_ROWS), lambda j, k: (k, 0)), col, col, col],
        out_specs=pl.BlockSpec((None, BR_ROWS, D_SHARD), lambda j, k: (j, 0, 0)),
        out_shape=jax.ShapeDtypeStruct((N_CHIPS, BR_ROWS, D_SHARD), BF16),
        scratch_shapes=[pltpu.VMEM((BR_ROWS, D_SHARD), F32)],
        name="branch_grad_weights", compiler_params=_params(("parallel", "arbitrary")),
    )(o_cat, du[0], du[1], du[2])


def mixer_bwd_input(dproj, win_all, x, dxo, gain, scale, sums, u, below):
    return matmul_prenorm_bwd(
        "mixer_bwd_input", NT, (dproj, win_all),
        (pl.BlockSpec((BWD_TILE, IN_SHARD), lambda i, j: (i, j)),
         pl.BlockSpec((None, D_MODEL, IN_SHARD), lambda i, j: (j, 0, 0))), x, dxo, gain, scale, sums, u, below)


BATCH_QK = (((2,), (2,)), ((0,), (0,)))
BATCH_PV = (((2,), (1,)), ((0,), (0,)))
BATCH_TN = (((1,), (1,)), ((0,), (0,)))


SB_WIDTH = H_SB * HEAD_DIM
SB_ROWS = H_SB * BLK


def _split_dot(v, tri):
    hi = v.astype(BF16)
    lo = (v - hi.astype(F32)).astype(BF16)
    return _dot(hi, tri, NN) + _dot(lo, tri, NN)


def _tri(cmp):
    return cmp(lax.broadcasted_iota(jnp.int32, (BLK, BLK), 0), lax.broadcasted_iota(jnp.int32, (BLK, BLK), 1)).astype(BF16)


def _head_masks():
    lane = lax.broadcasted_iota(jnp.int32, (1, SB_WIDTH), 1) // HEAD_DIM
    return [lane == h for h in range(H_SB)]


def _stack_heads(x, masks):
    return jnp.concatenate([jnp.where(m, x, jnp.zeros_like(x)) for m in masks], axis=0)


def _merge_heads(y, masks):
    out = jnp.where(masks[0], y[0:BLK], 0.0)
    for h in range(1, H_SB):
        out = jnp.where(masks[h], y[h * BLK:(h + 1) * BLK], out)
    return out


def _sb_scores(q4, k_ref, j, diagonal):
    rows = pl.ds(pl.multiple_of(j * BLK, BLK), BLK)
    z = _dot(q4, k_ref[rows, :], NT)
    log_fail = -(jnp.maximum(z, 0.0) + jnp.log(1.0 + jnp.exp(-jnp.abs(z))))
    log_hit = z + log_fail
    before = None
    if diagonal:
        tile = (SB_ROWS, BLK)
        before = lax.broadcasted_iota(jnp.int32, tile, 1) < (lax.broadcasted_iota(jnp.int32, tile, 0) & (BLK - 1))
        log_fail = jnp.where(before, log_fail, 0.0)
    return rows, before, log_fail, log_hit


def _keep(before, x):
    return x if before is None else jnp.where(before, x, 0.0)


def sb_forward(qkv):
    def body(q_ref, k_ref, v_ref, o_ref, tot_ref):
        i = pl.program_id(0)
        masks = _head_masks()
        q4 = _stack_heads(q_ref[...] * QK_SCALE, masks)
        later = _tri(lambda r, c: r > c)

        def tiles(js, carry, diagonal):
            o, run = carry
            scores = [_sb_scores(q4, k_ref, j, diagonal) for j in js]
            acc = None
            for rows, before, log_fail, log_hit in scores:
                between = _split_dot(log_fail, later) + run
                w = _keep(before, jnp.exp(log_hit + between))
                part = _dot(w.astype(BF16), v_ref[rows, :], NN)
                acc = part if acc is None else acc + part
                run = run + jnp.sum(log_fail, axis=1, keepdims=True)
            return o + _merge_heads(acc, masks), run

        carry = tiles([i], (jnp.zeros((BLK, SB_WIDTH), F32), jnp.zeros((SB_ROWS, 1), F32)), True)
        carry = lax.cond((i & 1) != 0, lambda c: tiles([i - 1], c, False), lambda c: c, carry)
        at = i - 1 - (i & 1)
        carry = lax.cond((i & 2) != 0, lambda c: tiles([at, at - 1], c, False), lambda c: c, carry)
        at = at - (i & 2)
        o, run = lax.fori_loop(0, i // 4, lambda t, c: tiles([at - 4 * t - n for n in range(4)], c, False), carry)
        o_ref[...] = o
        tot_ref[...] = run

    return pl.pallas_call(
        body, grid=(N_BLK,),
        in_specs=[pl.BlockSpec((BLK, SB_WIDTH), lambda i: (i, 0)), pl.BlockSpec((SEQ, SB_WIDTH), lambda i: (0, 1)),
                  pl.BlockSpec((SEQ, SB_WIDTH), lambda i: (0, 2))],
        out_specs=[pl.BlockSpec((BLK, SB_WIDTH), lambda i: (i, 0)), pl.BlockSpec((None, SB_ROWS, 1), lambda i: (i, 0, 0))],
        out_shape=[jax.ShapeDtypeStruct((SEQ, SB_WIDTH), F32), jax.ShapeDtypeStruct((N_BLK, SB_ROWS, 1), F32)],
        name="sb_forward", compiler_params=_params(("parallel",)),
    )(qkv, qkv, qkv)


def sb_backward(qkv, total, do_cat):
    def body(q_ref, k_ref, v_ref, tot_ref, do_ref, dq_ref, dk_ref, dv_ref):
        i = pl.program_id(0)

        @pl.when(i == 0)
        def _():
            dk_ref[...] = jnp.zeros_like(dk_ref)
            dv_ref[...] = jnp.zeros_like(dv_ref)

        masks = _head_masks()
        q4 = _stack_heads(q_ref[...] * QK_SCALE, masks)
        do4 = _stack_heads(do_ref[...].astype(BF16), masks)
        total_v = tot_ref[...]
        upto = _tri(lambda r, c: r <= c)
        earlier = _tri(lambda r, c: r < c)

        def tiles(js, carry, diagonal):
            dq, seen, g_seen = carry
            scores = [_sb_scores(q4, k_ref, j, diagonal) for j in js]
            acc = None
            for rows, before, log_fail, log_hit in scores:
                between = total_v - (seen + _split_dot(log_fail, upto))
                w = _keep(before, jnp.exp(log_hit + between))
                g = _dot(do4, v_ref[rows, :], NT) * w
                g_earlier = g_seen + _split_dot(g, earlier)
                sig = jnp.exp(log_hit)
                dz = _keep(before, g * (1.0 - sig) - g_earlier * sig).astype(BF16)
                part = _dot(dz, k_ref[rows, :], NN)
                acc = part if acc is None else acc + part
                dk_ref[rows, :] += _dot(dz, q4, TN)
                dv_ref[rows, :] += _dot(w.astype(BF16), do4, TN)
                seen = seen + jnp.sum(log_fail, axis=1, keepdims=True)
                g_seen = g_seen + jnp.sum(g, axis=1, keepdims=True)
            return dq + _merge_heads(acc, masks), seen, g_seen

        zero = jnp.zeros((SB_ROWS, 1), F32)
        carry = lax.fori_loop(0, i // 4, lambda t, c: tiles([4 * t + n for n in range(4)], c, False),
                              (jnp.zeros((BLK, SB_WIDTH), F32), zero, zero))
        at = i - (i & 3)
        carry = lax.cond((i & 2) != 0, lambda c: tiles([at, at + 1], c, False), lambda c: c, carry)
        carry = lax.cond((i & 1) != 0, lambda c: tiles([i - 1], c, False), lambda c: c, carry)
        dq, _, _ = tiles([i], carry, True)
        dq_ref[...] = dq * QK_SCALE

    blk = pl.BlockSpec((BLK, SB_WIDTH), lambda i: (i, 0))
    full = pl.BlockSpec((SEQ, SB_WIDTH), lambda i: (0, 0))
    shape = jax.ShapeDtypeStruct((SEQ, SB_WIDTH), F32)
    return pl.pallas_call(
        body, grid=(N_BLK,),
        in_specs=[blk, pl.BlockSpec((SEQ, SB_WIDTH), lambda i: (0, 1)), pl.BlockSpec((SEQ, SB_WIDTH), lambda i: (0, 2)),
                  pl.BlockSpec((None, SB_ROWS, 1), lambda i: (i, 0, 0)), blk],
        out_specs=[blk, full, full], out_shape=[shape, shape, shape],
        name="sb_backward", compiler_params=_params(("arbitrary",)),
    )(qkv, qkv, qkv, total, do_cat)


def _band_scores(q_ref, kp_ref, ko_ref, bias_ref, hb, prev_mask):
    b = pl.program_id(1)
    qs = q_ref[...]
    s_prev = _dot(qs, kp_ref[...], BATCH_QK) + bias_ref[:, :, 0:BLK]
    s_prev = jnp.concatenate(
        [jnp.where((b & prev_mask(pl.program_id(0) * hb + t)) != 0, s_prev[t:t + 1], NEG) for t in range(hb)], axis=0)
    s_own = _dot(qs, ko_ref[...], BATCH_QK) + bias_ref[:, :, BLK:2 * BLK]
    return qs, s_prev, s_own


def _band_specs(hb, rows, t_n):
    def q_spec(width):
        return pl.BlockSpec((hb, None, rows, width), lambda h, b: (h, b, 0, 0))

    own = pl.BlockSpec((hb, BLK, HEAD_DIM), lambda h, b: (h, b, 0))
    prev = pl.BlockSpec((hb, BLK, HEAD_DIM), lambda h, b: (h, jnp.maximum(b - 1, 0), 0))
    per_head = lambda r, width: pl.BlockSpec((hb, r, width), lambda h, b: (h, 0, 0))
    return q_spec, own, prev, per_head


def banded_forward(name, q, k, v, bias, sinks, hb, prev_mask):
    h_n, nb, rows, _ = q.shape

    def body(q_ref, kp_ref, ko_ref, vp_ref, vo_ref, bias_ref, sink_ref, o_ref, lse_ref):
        _, s_prev, s_own = _band_scores(q_ref, kp_ref, ko_ref, bias_ref, hb, prev_mask)
        sink = sink_ref[...]
        m = jnp.maximum(jnp.maximum(jnp.max(s_prev, axis=2, keepdims=True), jnp.max(s_own, axis=2, keepdims=True)), sink)
        p_prev = jnp.exp(s_prev - m)
        p_own = jnp.exp(s_own - m)
        denom = jnp.sum(p_prev, axis=2, keepdims=True) + jnp.sum(p_own, axis=2, keepdims=True) + jnp.exp(sink - m)
        o = _dot(p_prev.astype(BF16), vp_ref[...], BATCH_PV) + _dot(p_own.astype(BF16), vo_ref[...], BATCH_PV)
        o_ref[...] = o / denom
        lse_ref[...] = m + jnp.log(denom)

    q_spec, own, prev, per_head = _band_specs(hb, rows, k.shape[1])
    return pl.pallas_call(
        body, grid=(h_n // hb, nb),
        in_specs=[q_spec(HEAD_DIM), prev, own, prev, own, per_head(rows, 2 * BLK), per_head(rows, 1)],
        out_specs=[q_spec(HEAD_DIM), q_spec(1)],
        out_shape=[jax.ShapeDtypeStruct(q.shape, F32), jax.ShapeDtypeStruct((h_n, nb, rows, 1), F32)],
        name=name, compiler_params=_params(("parallel", "parallel")),
    )(q, k, k, v, v, bias, sinks)


def banded_backward(name, q, k, v, bias, sinks, o, lse, do, dlse, hb, prev_mask):
    h_n, nb, rows, _ = q.shape
    t_n = k.shape[1]

    def body(q_ref, kp_ref, ko_ref, vp_ref, vo_ref, bias_ref, sink_ref, o_ref, lse_ref, do_ref, dlse_ref,
             dq_ref, dk_ref, dv_ref, dbias_ref, dsink_ref):
        b = pl.program_id(1)

        @pl.when(b == 0)
        def _():
            dk_ref[...] = jnp.zeros_like(dk_ref)
            dv_ref[...] = jnp.zeros_like(dv_ref)
            dbias_ref[...] = jnp.zeros_like(dbias_ref)
            dsink_ref[...] = jnp.zeros_like(dsink_ref)

        qs, s_prev, s_own = _band_scores(q_ref, kp_ref, ko_ref, bias_ref, hb, prev_mask)
        lse_v = lse_ref[...]
        dov = do_ref[...]
        dob = dov.astype(BF16)
        shift = dlse_ref[...] - jnp.sum(dov * o_ref[...], axis=2, keepdims=True)
        p_prev = jnp.exp(s_prev - lse_v)
        p_own = jnp.exp(s_own - lse_v)
        ds_prev = p_prev * (_dot(dob, vp_ref[...], BATCH_QK) + shift)
        ds_own = p_own * (_dot(dob, vo_ref[...], BATCH_QK) + shift)
        dbias_ref[:, :, 0:BLK] += ds_prev
        dbias_ref[:, :, BLK:2 * BLK] += ds_own
        d_sink = jnp.exp(sink_ref[...] - lse_v) * shift
        for g in range(rows // BLK):
            dsink_ref[:, g:g + 1, :] += jnp.sum(d_sink[:, g * BLK:(g + 1) * BLK, :], axis=1, keepdims=True)
        ds_prev = ds_prev.astype(BF16)
        ds_own = ds_own.astype(BF16)
        dq_ref[...] = (_dot(ds_prev, kp_ref[...], BATCH_PV) + _dot(ds_own, ko_ref[...], BATCH_PV)) * QK_SCALE
        rows_prev = pl.ds(pl.multiple_of(jnp.maximum(b - 1, 0) * BLK, BLK), BLK)
        rows_own = pl.ds(pl.multiple_of(b * BLK, BLK), BLK)
        dk_ref[:, rows_prev, :] += _dot(ds_prev, qs, BATCH_TN)
        dk_ref[:, rows_own, :] += _dot(ds_own, qs, BATCH_TN)
        dv_ref[:, rows_prev, :] += _dot(p_prev.astype(BF16), dob, BATCH_TN)
        dv_ref[:, rows_own, :] += _dot(p_own.astype(BF16), dob, BATCH_TN)

    q_spec, own, prev, per_head = _band_specs(hb, rows, t_n)
    kv_full = per_head(t_n, HEAD_DIM)
    kv_shape = jax.ShapeDtypeStruct((h_n, t_n, HEAD_DIM), F32)
    return pl.pallas_call(
        body, grid=(h_n // hb, nb),
        in_specs=[q_spec(HEAD_DIM), prev, own, prev, own, per_head(rows, 2 * BLK), per_head(rows, 1),
                  q_spec(HEAD_DIM), q_spec(1), q_spec(HEAD_DIM), q_spec(1)],
        out_specs=[q_spec(HEAD_DIM), kv_full, kv_full, per_head(rows, 2 * BLK), per_head(rows // BLK, BLK)],
        out_shape=[jax.ShapeDtypeStruct(q.shape, F32), kv_shape, kv_shape,
                   jax.ShapeDtypeStruct((h_n, rows, 2 * BLK), F32), jax.ShapeDtypeStruct((h_n, rows // BLK, BLK), F32)],
        name=name, compiler_params=_params(("parallel", "arbitrary")),
    )(q, k, k, v, v, bias, sinks, o, lse, do, dlse)


def _swa_prev_mask(head):
    del head
    return 15


SWA_HEADS_PER_STEP = 2
SWA_GROUP = H_SWA_Q // H_SWA_KV
N_BLK = SEQ // BLK


GROUP_W = H_PER_DIL * HEAD_DIM
DIL_COLUMNS = (768, 1920)
LANE_BLOCKS = (DIL_COLUMNS[1] - DIL_COLUMNS[0]) // GROUP_W
DIL_Q_BLOCK, DIL_K_BLOCK, DIL_V_BLOCK = 0, 3, 6
N_GROUPS = len(DIL_PATTERNS)


def dilated_views(qkv):
    cols = qkv[:, DIL_COLUMNS[0]:DIL_COLUMNS[1]]
    return [_dil_view(cols, d) for _, d in DIL_PATTERNS]


def _dil_view(t, d):
    return t.reshape(SEQ // d, d * t.shape[1])


def _dil_tile(n, d):
    per_class = N_BLK // d
    return n // per_class, n % per_class


def _dil_spec(d, lane_block, lane_blocks, shift=0):
    def index(n):
        r, m = _dil_tile(n, d)
        m = jnp.clip(m + shift, 0, N_BLK // d - 1)
        return m, r * lane_blocks + lane_block
    return pl.BlockSpec((BLK, GROUP_W), index)


def _two_heads(x, first):
    zero = jnp.zeros_like(x)
    return jnp.concatenate([jnp.where(first, x, zero), jnp.where(first, zero, x)], axis=0)


def _per_head(col, first):
    return jnp.where(first, col[0:BLK], col[BLK:2 * BLK])


def _head_rows(tile, first):
    pick = lambda keep: jnp.max(jnp.where(keep, tile, -jnp.inf), axis=1, keepdims=True)
    return jnp.concatenate([pick(first), pick(jnp.logical_not(first))], axis=0)


def _dil_scores(q_ref, kp_ref, ko_ref, bias, has_prev, first):
    q2 = _two_heads(q_ref[...] * QK_SCALE, first)
    k2 = jnp.concatenate([kp_ref[...], ko_ref[...]], axis=0)
    s = _dot(q2, k2, NT) + bias
    key = lax.broadcasted_iota(jnp.int32, s.shape, 1)
    return q2, k2, jnp.where(jnp.logical_or(has_prev, key >= BLK), s, NEG)


def dilated_forward(views, bias):
    def body(*refs):
        ins, bias_ref, outs = refs[:5 * N_GROUPS], refs[5 * N_GROUPS], refs[5 * N_GROUPS + 1:]
        n = pl.program_id(0)
        first = lax.broadcasted_iota(jnp.int32, (1, GROUP_W), 1) < HEAD_DIM
        for g, (_, d) in enumerate(DIL_PATTERNS):
            q_ref, kp_ref, ko_ref, vp_ref, vo_ref = ins[5 * g:5 * g + 5]
            has_prev = _dil_tile(n, d)[1] > 0
            _, _, s = _dil_scores(q_ref, kp_ref, ko_ref, bias_ref[g], has_prev, first)
            m = jnp.max(s, axis=1, keepdims=True)
            p = jnp.exp(s - m)
            denom = jnp.sum(p, axis=1, keepdims=True)
            v2 = jnp.concatenate([vp_ref[...], vo_ref[...]], axis=0)
            o2 = _dot(p.astype(BF16), v2, NN) / denom
            outs[2 * g][...] = _per_head(o2, first)
            outs[2 * g + 1][...] = _per_head(m + jnp.log(denom), first)

    operands, in_specs, out_specs, out_shape = [], [], [], []
    for g, (_, d) in enumerate(DIL_PATTERNS):
        operands += [views[g]] * 5
        in_specs += [_dil_spec(d, DIL_Q_BLOCK + g, LANE_BLOCKS), _dil_spec(d, DIL_K_BLOCK + g, LANE_BLOCKS, -1),
                     _dil_spec(d, DIL_K_BLOCK + g, LANE_BLOCKS), _dil_spec(d, DIL_V_BLOCK + g, LANE_BLOCKS, -1),
                     _dil_spec(d, DIL_V_BLOCK + g, LANE_BLOCKS)]
        out_specs += [_dil_spec(d, 0, 1)] * 2
        out_shape += [jax.ShapeDtypeStruct((SEQ // d, d * GROUP_W), F32)] * 2
    out = pl.pallas_call(
        body, grid=(N_BLK,), in_specs=in_specs + [pl.BlockSpec((N_GROUPS, 2 * BLK, 2 * BLK), lambda n: (0, 0, 0))],
        out_specs=out_specs, out_shape=out_shape, name="dilated_forward", compiler_params=_params(("parallel",)),
    )(*operands, bias)
    out = [t.reshape(SEQ, GROUP_W) for t in out]
    return out[0::2], out[1::2]


def _group_softmax(lses):
    m = jnp.maximum(jnp.maximum(lses[0], lses[1]), lses[2])
    e = [jnp.exp(l - m) for l in lses]
    total = e[0] + e[1] + e[2]
    return [t / total for t in e]


def dilated_merge(o, lse):
    def body(*refs):
        alpha = _group_softmax([r[...] for r in refs[N_GROUPS:2 * N_GROUPS]])
        refs[-1][...] = alpha[0] * refs[0][...] + alpha[1] * refs[1][...] + alpha[2] * refs[2][...]

    spec = pl.BlockSpec((ROW_TILE, GROUP_W), lambda i: (i, 0))
    return pl.pallas_call(
        body, grid=(SEQ // ROW_TILE,), in_specs=[spec] * (2 * N_GROUPS), out_specs=spec,
        out_shape=jax.ShapeDtypeStruct((SEQ, GROUP_W), F32), name="dilated_merge", compiler_params=_params(("parallel",)),
    )(*o, *lse)


def dilated_merge_bwd(o, lse, do_cat):
    def body(*refs):
        o_v = [r[...] for r in refs[:N_GROUPS]]
        alpha = _group_softmax([r[...] for r in refs[N_GROUPS:2 * N_GROUPS]])
        dout = refs[2 * N_GROUPS][...]
        outs = refs[2 * N_GROUPS + 1:]
        first = lax.broadcasted_iota(jnp.int32, (1, GROUP_W), 1) < HEAD_DIM

        def head_sum(x):
            a = jnp.sum(jnp.where(first, x, 0.0), axis=1, keepdims=True)
            b = jnp.sum(jnp.where(first, 0.0, x), axis=1, keepdims=True)
            return jnp.where(first, a, b)

        dalpha = [head_sum(dout * o_g) for o_g in o_v]
        mean = alpha[0] * dalpha[0] + alpha[1] * dalpha[1] + alpha[2] * dalpha[2]
        for g in range(N_GROUPS):
            outs[g][...] = alpha[g] * dout
            outs[N_GROUPS + g][...] = alpha[g] * (dalpha[g] - mean)

    spec = pl.BlockSpec((ROW_TILE, GROUP_W), lambda i: (i, 0))
    shape = jax.ShapeDtypeStruct((SEQ, GROUP_W), F32)
    out = pl.pallas_call(
        body, grid=(SEQ // ROW_TILE,), in_specs=[spec] * (2 * N_GROUPS) + [pl.BlockSpec((ROW_TILE, GROUP_W), lambda i: (i, 2))],
        out_specs=[spec] * (2 * N_GROUPS), out_shape=[shape] * (2 * N_GROUPS),
        name="dilated_merge_bwd", compiler_params=_params(("parallel",)),
    )(*o, *lse, do_cat)
    return out[:N_GROUPS], out[N_GROUPS:]


def dilated_backward(views, bias, o, lse, do, dlse):
    n_in = 9

    def body(*refs):
        ins, bias_ref = refs[:n_in * N_GROUPS], refs[n_in * N_GROUPS]
        outs, dbias_ref = refs[n_in * N_GROUPS + 1:-1], refs[-1]
        n = pl.program_id(0)

        @pl.when(n == 0)
        def _():
            dbias_ref[...] = jnp.zeros_like(dbias_ref)

        first = lax.broadcasted_iota(jnp.int32, (1, GROUP_W), 1) < HEAD_DIM
        for g, (_, d) in enumerate(DIL_PATTERNS):
            q_ref, kp_ref, ko_ref, vp_ref, vo_ref, o_ref, lse_ref, do_ref, dlse_ref = ins[n_in * g:n_in * (g + 1)]
            has_prev = _dil_tile(n, d)[1] > 0
            q2, k2, s = _dil_scores(q_ref, kp_ref, ko_ref, bias_ref[g], has_prev, first)
            dov = do_ref[...]
            do2 = _two_heads(dov.astype(BF16), first)
            prod = dov * o_ref[...]
            delta = jnp.concatenate([jnp.sum(jnp.where(first, prod, 0.0), axis=1, keepdims=True),
                                     jnp.sum(jnp.where(first, 0.0, prod), axis=1, keepdims=True)], axis=0)
            shift = _head_rows(dlse_ref[...], first) - delta
            p = jnp.exp(s - _head_rows(lse_ref[...], first))
            v2 = jnp.concatenate([vp_ref[...], vo_ref[...]], axis=0)
            ds = p * (_dot(do2, v2, NT) + shift)
            dbias_ref[g] += ds
            ds = ds.astype(BF16)
            dq2 = _dot(ds, k2, NN) * QK_SCALE
            dk2 = _dot(ds, q2, TN)
            dv2 = _dot(p.astype(BF16), do2, TN)
            base = 5 * g
            outs[base][...] = jnp.where(first, dq2[0:BLK], dq2[BLK:2 * BLK])
            outs[base + 1][...] = dk2[BLK:2 * BLK]
            outs[base + 2][...] = dk2[0:BLK]
            outs[base + 3][...] = dv2[BLK:2 * BLK]
            outs[base + 4][...] = dv2[0:BLK]

    operands, in_specs, out_specs, out_shape = [], [], [], []
    for g, (_, d) in enumerate(DIL_PATTERNS):
        own = _dil_spec(d, 0, 1)
        operands += [views[g]] * 5 + [_dil_view(t[g], d) for t in (o, lse, do, dlse)]
        in_specs += [_dil_spec(d, DIL_Q_BLOCK + g, LANE_BLOCKS), _dil_spec(d, DIL_K_BLOCK + g, LANE_BLOCKS, -1),
                     _dil_spec(d, DIL_K_BLOCK + g, LANE_BLOCKS), _dil_spec(d, DIL_V_BLOCK + g, LANE_BLOCKS, -1),
                     _dil_spec(d, DIL_V_BLOCK + g, LANE_BLOCKS)] + [own] * 4
        out_specs += [own] * 5
        out_shape += [jax.ShapeDtypeStruct((SEQ // d, d * GROUP_W), F32)] * 5
    tiles = pl.BlockSpec((N_GROUPS, 2 * BLK, 2 * BLK), lambda n: (0, 0, 0))
    out = pl.pallas_call(
        body, grid=(N_BLK,), in_specs=in_specs + [tiles], out_specs=out_specs + [tiles],
        out_shape=out_shape + [jax.ShapeDtypeStruct((N_GROUPS, 2 * BLK, 2 * BLK), F32)],
        name="dilated_backward", compiler_params=_params(("arbitrary",)),
    )(*operands, bias)
    return [out[5 * g:5 * g + 5] for g in range(N_GROUPS)], out[-1]


def dilated_key_grads(parts):
    def body(*refs):
        ins, outs = refs[:4 * N_GROUPS], refs[4 * N_GROUPS:]
        n = pl.program_id(0)
        for g, (_, d) in enumerate(DIL_PATTERNS):
            has_next = _dil_tile(n, d)[1] < N_BLK // d - 1
            own_k, next_k, own_v, next_v = ins[4 * g:4 * g + 4]
            outs[2 * g][...] = own_k[...] + jnp.where(has_next, next_k[...], 0.0)
            outs[2 * g + 1][...] = own_v[...] + jnp.where(has_next, next_v[...], 0.0)

    operands, in_specs, out_specs, out_shape = [], [], [], []
    for g, (_, d) in enumerate(DIL_PATTERNS):
        _, dk_own, dk_prev, dv_own, dv_prev = parts[g]
        operands += [dk_own, dk_prev, dv_own, dv_prev]
        in_specs += [_dil_spec(d, 0, 1), _dil_spec(d, 0, 1, 1)] * 2
        out_specs += [_dil_spec(d, 0, 1)] * 2
        out_shape += [jax.ShapeDtypeStruct((SEQ // d, d * GROUP_W), F32)] * 2
    out = pl.pallas_call(
        body, grid=(N_BLK,), in_specs=in_specs, out_specs=out_specs, out_shape=out_shape,
        name="dilated_key_grads", compiler_params=_params(("parallel",)),
    )(*operands)
    tok = lambda ts: jnp.concatenate([t.reshape(SEQ, GROUP_W) for t in ts], axis=1)
    return tok([parts[g][0] for g in range(N_GROUPS)]), tok(out[0::2]), tok(out[1::2])


def rel_bias_reduce(dbias0, dbias1, bucket):
    def body(d0_ref, d1_ref, b_ref, o_ref):
        dv, bv = d0_ref[...] + d1_ref[...], b_ref[...]
        lane = lax.broadcasted_iota(jnp.int32, (1, BLK), 1)
        acc = jnp.zeros((1, BLK), F32)
        for bkt in range(N_BUCKETS):
            acc = acc + jnp.where(lane == bkt, jnp.sum(jnp.where(bv == bkt, dv, 0.0)), 0.0)
        o_ref[...] = acc

    tile = pl.BlockSpec((None, BLK, 2 * BLK), lambda h: (h, 0, 0))
    return pl.pallas_call(
        body, grid=(dbias0.shape[0],), in_specs=[tile, tile, tile],
        out_specs=pl.BlockSpec((None, 1, BLK), lambda h: (h, 0, 0)),
        out_shape=jax.ShapeDtypeStruct((dbias0.shape[0], 1, BLK), F32),
        name="rel_bias_reduce", compiler_params=_params(("parallel",)),
    )(dbias0, dbias1, bucket)


def _heads(t):
    return t.reshape(SEQ, -1, HEAD_DIM).transpose(1, 0, 2)


def _unheads(t):
    return t.transpose(1, 0, 2).reshape(SEQ, -1)


def _t5_bucket(n):
    max_exact = N_BUCKETS // 2
    nf = jnp.maximum(n, 1).astype(F32)
    large = max_exact + (jnp.log(nf / max_exact) / math.log(MAX_REL_DIST / max_exact)
                         * (N_BUCKETS - max_exact)).astype(jnp.int32)
    large = jnp.minimum(large, N_BUCKETS - 1)
    return jnp.where(n < max_exact, n, large)


def band_tables(rel_bias):
    rel = jnp.arange(BLK)[:, None] + BLK - jnp.arange(2 * BLK)[None, :]
    buckets = []
    patterns = [(d, w // d) for w, d in DIL_PATTERNS for _ in range(H_PER_DIL)] + [(1, SWA_WINDOW - 1)] * H_SWA_Q
    for d, max_dist in patterns:
        band = (rel >= 0) & (rel <= max_dist)
        buckets.append(jnp.where(band, _t5_bucket(jnp.maximum(rel, 0) * d), -1))
    buckets = jnp.stack(buckets).astype(jnp.int32)

    def body(table_ref, b_ref, o_ref):
        h = pl.program_id(0)
        bv = b_ref[...]
        tile = jnp.full(bv.shape, NEG, F32)
        for bkt in range(N_BUCKETS):
            tile = jnp.where(bv == bkt, table_ref[h, bkt], tile)
        o_ref[...] = tile

    spec = pl.BlockSpec((None, BLK, 2 * BLK), lambda h: (h, 0, 0))
    tiles = pl.pallas_call(
        body, grid=(len(patterns),), in_specs=[pl.BlockSpec(memory_space=pltpu.SMEM), spec], out_specs=spec,
        out_shape=jax.ShapeDtypeStruct(buckets.shape, F32), name="band_tables", compiler_params=_params(("parallel",)),
    )(rel_bias.T, buckets)
    return tiles[:H_DIL], tiles[H_DIL:], buckets


def _swa_rows(t):
    t = t.reshape(N_BLK, BLK, H_SWA_KV, SWA_GROUP, HEAD_DIM).transpose(2, 0, 3, 1, 4)
    return t.reshape(H_SWA_KV, N_BLK, SWA_GROUP * BLK, HEAD_DIM)


def _swa_tokens(t):
    t = t.reshape(H_SWA_KV, N_BLK, SWA_GROUP, BLK, HEAD_DIM).transpose(1, 3, 0, 2, 4)
    return t.reshape(SEQ, H_SWA_Q * HEAD_DIM)


def _sink_rows(sinks):
    return jnp.broadcast_to(sinks.reshape(H_SWA_KV, SWA_GROUP, 1, 1), (H_SWA_KV, SWA_GROUP, BLK, 1)).reshape(
        H_SWA_KV, SWA_GROUP * BLK, 1)


def _vec(v):
    return v.reshape(1, D_MODEL)


class UnitRows:
    def __init__(self, u, mod_table, gain_table):
        self.shift, self.scale, self.gate = (Row(mod_table, 3 * u + t) for t in range(3))
        self.gain = Row(gain_table, u)


def ffn_forward(x, h, rows, then, w):
    a, b, s = ffn_up(h, w[0], w[1])
    f, xo, *h_next = ffn_down(s, w[2], x, rows.gate, then)
    return xo, (h_next or [None])[0], (x, h, a, b, s, f)


def ffn_backward(u, dxo, df, saved, rows, w, sums, below):
    x, h, a, b, s, _ = saved
    da, db = ffn_bwd_hidden(df, w[2], a, b)
    grads = ffn_grad_weights(h, s, df, da, db)
    dx, sums, *df_below = ffn_bwd_input(da, db, w[0], w[1], x, dxo, rows.gain, rows.scale, sums, u, below)
    return dx, sums, df_below, grads


def mixer_forward(x, h, rows, then, sinks, bias_dil, bias_swa, w):
    proj, qkv = in_proj(h, w[0])
    q_swa, k_swa, v_swa = _swa_rows(qkv[:, 1920:2304] * QK_SCALE), _heads(qkv[:, 2304:2432]), _heads(qkv[:, 2432:2560])
    o_sb, total_sb = sb_forward(qkv)
    bias_dil = bias_dil.reshape(N_GROUPS, 2 * BLK, 2 * BLK)
    views = dilated_views(qkv)
    o_groups, lse_groups = dilated_forward(views, bias_dil)
    o_dil = dilated_merge(o_groups, lse_groups)
    bias_swa = bias_swa.reshape(H_SWA_KV, SWA_GROUP * BLK, 2 * BLK)
    o_swa, lse_swa = banded_forward("swa_forward", q_swa, k_swa, v_swa, bias_swa, _sink_rows(sinks), SWA_HEADS_PER_STEP,
                                    _swa_prev_mask)
    o_cat = jnp.concatenate([o_sb, o_dil, _swa_tokens(o_swa)], axis=1)
    merged = merge_branches(o_cat, w[1], proj)
    mo, xo, *h_next = out_proj(merged, w[2], x, rows.gate, then)
    saved = (x, h, proj, (qkv, total_sb), (views, o_groups, lse_groups),
             (q_swa, k_swa, v_swa, o_swa, lse_swa), o_cat, merged, mo)
    return xo, (h_next or [None])[0], saved


def mixer_backward(u, dxo, dmo, saved, rows, sinks, bias_dil, bias_swa, w, sums, below):
    x, h, proj, sb, dil, swa, o_cat, merged, _ = saved
    tok = pl.BlockSpec((MM_TILE, D_MODEL), lambda j, k: (k, 0))
    g_out = grad_weight("grad_w_out", merged, pl.BlockSpec((MM_TILE, D_SHARD), lambda j, k: (k, j)), dmo, tok,
                        (D_SHARD, D_MODEL))
    du0, du1, du2, dg0, dg1, dg2 = merge_bwd(dmo, w[2], o_cat, w[1], proj)
    du = (du0, du1, du2)
    do_cat = branch_bwd_input(du, w[1])
    g_br = branch_grad_weights(o_cat, du)

    qkv, total_sb = sb
    dq_sb, dk_sb, dv_sb = sb_backward(qkv, total_sb, do_cat)

    views, o_groups, lse_groups = dil
    bias_dil = bias_dil.reshape(N_GROUPS, 2 * BLK, 2 * BLK)
    do_groups, dlse_groups = dilated_merge_bwd(o_groups, lse_groups, do_cat)
    parts, dbias_dil = dilated_backward(views, bias_dil, o_groups, lse_groups, do_groups, dlse_groups)
    dq_dil, dk_dil, dv_dil = dilated_key_grads(parts)
    dbias_dil = dbias_dil.reshape(H_DIL, BLK, 2 * BLK)

    q_swa, k_swa, v_swa, o_swa, lse_swa = swa
    bias_swa = bias_swa.reshape(H_SWA_KV, SWA_GROUP * BLK, 2 * BLK)
    dq_swa, dk_swa, dv_swa, dbias_swa, dsinks = banded_backward(
        "swa_backward", q_swa, k_swa, v_swa, bias_swa, _sink_rows(sinks), o_swa, lse_swa, _swa_rows(do_cat[:, 384:768]),
        jnp.zeros_like(lse_swa), SWA_HEADS_PER_STEP, _swa_prev_mask)
    dbias_swa = dbias_swa.reshape(H_SWA_Q, BLK, 2 * BLK)

    dproj = jnp.concatenate(
        [dq_sb, dk_sb, dv_sb, dq_dil, dk_dil, dv_dil, _swa_tokens(dq_swa), _unheads(dk_swa), _unheads(dv_swa)],
        axis=1).astype(BF16)
    dproj = jnp.concatenate([dproj, dg0, dg1, dg2], axis=1)
    g_in = grad_weight("grad_w_in", h, tok, dproj, pl.BlockSpec((MM_TILE, IN_SHARD), lambda j, k: (k, j)),
                       (D_MODEL, IN_SHARD))
    dx, sums, *df_below = mixer_bwd_input(dproj, w[0], x, dxo, rows.gain, rows.scale, sums, u, below)
    dbias = jnp.concatenate([dbias_dil, dbias_swa], axis=0)
    return dx, sums, df_below, dbias, dsinks[:, :, 0].reshape(H_SWA_Q), (g_in, g_br, g_out)


N_UNITS = 3 * DEPTH


def device_step(x, target, mod, gains, final_gain, sinks, rel_bias, get_weights, put_grads):
    bias_dil, bias_swa, bucket = band_tables(rel_bias)
    mod_table = mod.reshape(3 * N_UNITS, 1, D_MODEL)
    gain_table = gains.reshape(N_UNITS, 1, D_MODEL)
    saved, weights = [], []
    units = [UnitRows(u, mod_table, gain_table) for u in range(N_UNITS)]
    h = prenorm(x, units[0].gain, units[0].scale, units[0].shift)
    for u in range(N_UNITS):
        l, j = divmod(u, 3)
        w = get_weights(u, x)
        then = units[u + 1] if u + 1 < N_UNITS else None
        if j == 1:
            x, h, s = mixer_forward(x, h, units[u], then, sinks[l], bias_dil, bias_swa, w)
        else:
            x, h, s = ffn_forward(x, h, units[u], then, w)
        saved.append(s)
        weights.append(w)
    loss, dx, dfinal = final_loss(x, _vec(final_gain), target)

    sums = (lax.empty((8 * N_UNITS, D_MODEL), F32), lax.empty((8 * N_UNITS, D_MODEL), F32))
    dbias, dsinks = [None] * DEPTH, [None] * DEPTH
    zero = jnp.zeros((1, 1), F32)
    top = N_UNITS - 1
    df, gate_sums = resid_bwd(dx, saved[top][-1], units[top].gate, 0.5, sums[1], top)
    sums = (sums[0], gate_sums)
    for u in reversed(range(N_UNITS)):
        l, j = divmod(u, 3)
        rows = UnitRows(u, mod_table, gain_table + zero)
        below = (saved[u - 1][-1], units[u - 1].gate, 1.0 if (u - 1) % 3 == 1 else 0.5) if u > 0 else None
        if j == 1:
            dx, sums, df, dbias[l], dsinks[l], grads = mixer_backward(
                u, dx, df, saved[u], rows, sinks[l], bias_dil, bias_swa, weights[u], sums, below)
        else:
            dx, sums, df, grads = ffn_backward(u, dx, df, saved[u], rows, weights[u], sums, below)
        df = df[0] if df else None
        if u > 0:
            zero = put_grads(u, grads)
    drel = rel_bias_reduce(dbias[0], dbias[1], bucket)[:, 0, :N_BUCKETS].T
    norm_sums, gate_sums = (t.reshape(DEPTH, 3, 8, D_MODEL) for t in sums)
    dmod = jnp.stack([norm_sums[:, :, 0], norm_sums[:, :, 1], gate_sums[:, :, 0]], axis=2)
    return loss, dx, dmod, norm_sums[:, :, 2], dfinal[0], jnp.stack(dsinks), drel, grads


MESH = pl.DeviceIdType.MESH
CHIP_FLIPS = ((1, 0), (0, 1), (1, 1))
ANY = pl.BlockSpec(memory_space=pl.ANY)


def _position():
    return lax.axis_index("x"), lax.axis_index("y"), lax.axis_index("c")


def all_gather_small(name, piece):
    def body(x_ref, out_ref, send_sems, recv_sems, local_sem):
        x, y, c = _position()
        me, sibling = (x, y, c), (x, y, 1 - c)
        chips = [(x ^ fx, y ^ fy) for fx, fy in CHIP_FLIPS]

        def rows(px, py, pc):
            return out_ref.at[4 * px + 2 * py + pc]

        def copy(k, block, to, src=None):
            return pltpu.make_async_remote_copy(
                src_ref=rows(*block) if src is None else src, dst_ref=rows(*block),
                send_sem=send_sems.at[k], recv_sem=recv_sems.at[k], device_id=to, device_id_type=MESH)

        mine = pltpu.make_async_copy(x_ref, rows(*me), local_sem)
        mine.start()
        first = [copy(0, me, sibling, src=x_ref)]
        first += [copy(1 + j, me, (*chip, c), src=x_ref) for j, chip in enumerate(chips)]
        for cp in first:
            cp.start()
        passed = [copy(4 + j, (*chip, c), sibling) for j, chip in enumerate(chips)]
        for j, chip in enumerate(chips):
            copy(1 + j, (*chip, c), me).wait_recv()
            passed[j].start()
        copy(0, sibling, me).wait_recv()
        for j, chip in enumerate(chips):
            copy(4 + j, (*chip, 1 - c), me).wait_recv()
        for cp in first + passed:
            cp.wait_send()
        mine.wait()

    return pl.pallas_call(
        body, out_shape=jax.ShapeDtypeStruct((N_DEV,) + piece.shape, piece.dtype),
        in_specs=[pl.BlockSpec(memory_space=pltpu.VMEM)], out_specs=pl.BlockSpec(memory_space=pltpu.VMEM),
        scratch_shapes=[pltpu.SemaphoreType.DMA((7,)), pltpu.SemaphoreType.DMA((7,)), pltpu.SemaphoreType.DMA],
        name=name,
    )(piece)


def exchange(name, operands, out_shapes, aliases, plan):
    n_in, n_out = len(operands), len(out_shapes)

    def body(*refs):
        ins, outs = refs[:n_in], refs[n_in:n_in + n_out]
        send_sems, recv_sems, local_sems = refs[n_in + n_out:]
        x, y, c = _position()
        local, sends, recvs = plan(ins, outs, x, y, c)
        local = [pltpu.make_async_copy(s, d, local_sems.at[k]) for k, (s, d) in enumerate(local)]
        for cp in local:
            cp.start()
        remote = [pltpu.make_async_remote_copy(src_ref=s, dst_ref=d, send_sem=send_sems.at[k], recv_sem=recv_sems.at[k],
                                               device_id=dev, device_id_type=MESH)
                  for k, (s, d, dev) in enumerate(sends)]
        for cp in remote:
            cp.start()
        for k, r in enumerate(recvs):
            pltpu.make_async_remote_copy(src_ref=r, dst_ref=r, send_sem=send_sems.at[k], recv_sem=recv_sems.at[k],
                                         device_id=(x, y, c), device_id_type=MESH).wait_recv()
        for cp in remote:
            cp.wait_send()
        for cp in local:
            cp.wait()

    n_sends, n_local = plan.n_sends, max(plan.n_local, 1)
    return pl.pallas_call(
        body, out_shape=out_shapes, in_specs=[ANY] * n_in, out_specs=[ANY] * n_out,
        scratch_shapes=[pltpu.SemaphoreType.DMA((n_sends,)), pltpu.SemaphoreType.DMA((n_sends,)),
                        pltpu.SemaphoreType.DMA((n_local,))],
        input_output_aliases=aliases, name=name,
    )(*operands)


def _plan(n_local, n_sends):
    def wrap(fn):
        fn.n_local, fn.n_sends = n_local, n_sends
        return fn
    return wrap


def _half(ref, axis, c):
    rows = ref.shape[axis] // 2
    idx = [slice(None)] * len(ref.shape)
    idx[axis] = pl.ds(pl.multiple_of(c * rows, 16), rows)
    return ref.at[tuple(idx)]


HBM = pl.BlockSpec(memory_space=pltpu.HBM)
SEM = pl.BlockSpec(memory_space=pltpu.SEMAPHORE)
EFFECT = pltpu.SideEffectType.DATAFLOW_SIDE_EFFECTING


def split_start(name, bufs, extra, n_copies, describe):
    n = len(bufs)

    def body(*refs):
        send_sems, recv_sems = refs[n + len(extra)], refs[n + len(extra) + 1]
        x, y, c = _position()
        for k, (src, dst, _, peer) in enumerate(describe(refs[:n], x, y, c)):
            pltpu.make_async_remote_copy(src_ref=src, dst_ref=dst, send_sem=send_sems.at[k], recv_sem=recv_sems.at[k],
                                         device_id=peer, device_id_type=MESH).start()
        token = refs[-1]
        token[...] = jnp.zeros_like(token)

    out = pl.pallas_call(
        body, name=name,
        out_shape=(pltpu.SemaphoreType.DMA((n_copies,)), pltpu.SemaphoreType.DMA((n_copies,)),
                   *[pltpu.HBM(b.shape, b.dtype) for b in bufs], jax.ShapeDtypeStruct((8, 128), F32)),
        in_specs=[HBM] * n + [ANY] * len(extra),
        out_specs=(SEM, SEM, *[HBM] * n, pl.BlockSpec(memory_space=pltpu.VMEM)),
        input_output_aliases={k: 2 + k for k in range(n)},
        compiler_params=pltpu.CompilerParams(has_side_effects=EFFECT),
    )(*[pltpu.with_memory_space_constraint(b, pltpu.HBM) for b in bufs], *extra)
    return out[0], out[1], list(out[2:2 + n]), out[-1]


def split_wait(name, bufs, send_sems, recv_sems, after, describe):
    n = len(bufs)

    def body(*refs):
        send, recv = refs[n], refs[n + 1]
        x, y, c = _position()
        for k, (src, _, dst, peer) in enumerate(describe(refs[:n], x, y, c)):
            copy = pltpu.make_async_remote_copy(src_ref=src, dst_ref=dst, send_sem=send.at[k], recv_sem=recv.at[k],
                                                device_id=peer, device_id_type=MESH)
            copy.wait_send()
            copy.wait_recv()

    out = pl.pallas_call(
        body, name=name, out_shape=[pltpu.HBM(b.shape, b.dtype) for b in bufs],
        in_specs=[HBM] * n + [SEM, SEM] + [ANY] * len(after), out_specs=[HBM] * n,
        input_output_aliases={k: k for k in range(n)},
        compiler_params=pltpu.CompilerParams(has_side_effects=EFFECT),
    )(*bufs, send_sems, recv_sems, *after)
    return list(out)


def _row_tile(rows, cols, max_elements=256 * 1024):
    best = 16
    for t in range(16, rows + 1, 16):
        if rows % t == 0 and t * cols <= max_elements:
            best = t
    return best


def cast_into_slot(name, param, index, chip):
    rows, cols = param.shape[-2:]
    tr = _row_tile(rows, cols)
    lead = (None,) * len(index)

    def body(chip_ref, s_ref, o_ref):
        del chip_ref
        o_ref[...] = s_ref[...].astype(BF16)

    return pl.pallas_call(
        body, out_shape=jax.ShapeDtypeStruct((N_CHIPS, rows, cols), BF16),
        grid_spec=pltpu.PrefetchScalarGridSpec(
            num_scalar_prefetch=1, grid=(rows // tr,),
            in_specs=[pl.BlockSpec(lead + (tr, cols), lambda r, chip_ref: index + (r, 0))],
            out_specs=pl.BlockSpec((None, tr, cols), lambda r, chip_ref: (chip_ref[0], r, 0))),
        name=name, compiler_params=_params(("parallel",)),
    )(chip, param)


GATHER_STAGES = ((0,), (1,), (2,), (3, 4, 5))
REDUCE_STAGES = ((5, 4, 3), (2,), (1,), (0,))


def _gather_copies(slots, x, y, c):
    me = 2 * x + y
    out = []
    for s in slots:
        for fx, fy in CHIP_FLIPS:
            mine = _half(s.at[me], 0, c)
            out.append((mine, mine, _half(s.at[2 * (x ^ fx) + (y ^ fy)], 0, c), (x ^ fx, y ^ fy, c)))
    return out


class WeightStream:
    def __init__(self, shards, chip, after=()):
        self.pending, self.ready = {}, {}
        token = tuple(after)
        for si, units in enumerate(GATHER_STAGES):
            slots = [cast_into_slot(f"cast_{u}_{t}", p, idx, chip) for u in units for t, (p, idx) in enumerate(shards[u])]
            send, recv, slots, tok = split_start(f"gather_start_{si}", slots, token, 3 * len(slots), _gather_copies)
            self.pending[si] = (send, recv, slots)
            token = (tok,)
        self.token = token

    def get(self, u, after):
        if u not in self.ready:
            si = next(k for k, units in enumerate(GATHER_STAGES) if u in units)
            send, recv, slots = self.pending.pop(si)
            slots = split_wait(f"gather_wait_{si}", slots, send, recv, (after,) + self.token, _gather_copies)
            self.token = ()

            @_plan(0, 3 * len(slots))
            def to_sibling(ins, outs, x, y, c):
                sends, recvs = [], []
                for o in outs:
                    for fx, fy in CHIP_FLIPS:
                        slab = o.at[2 * (x ^ fx) + (y ^ fy)]
                        sends.append((_half(slab, 0, c), _half(slab, 0, c), (x, y, 1 - c)))
                        recvs.append(_half(slab, 0, 1 - c))
                return [], sends, recvs

            shapes = [jax.ShapeDtypeStruct(s.shape, BF16) for s in slots]
            slots = exchange(f"gather_sibling_{si}", slots, shapes, {k: k for k in range(len(slots))}, to_sibling)
            for i, v in enumerate(GATHER_STAGES[si]):
                self.ready[v] = tuple(slots[3 * i:3 * i + 3])
        return self.ready[u]


def _reduce_copies(bufs, x, y, c):
    n = len(bufs) // 2
    out = []
    for s, land in zip(bufs[:n], bufs[n:]):
        for k, (fx, fy) in enumerate(CHIP_FLIPS):
            out.append((s.at[2 * (x ^ fx) + (y ^ fy)], land.at[k], land.at[k], (x ^ fx, y ^ fy, c)))
    return out


GRAD_SLOTS = {"gate": (2 * DEPTH, FF_SHARD, D_MODEL), "up": (2 * DEPTH, FF_SHARD, D_MODEL),
              "down": (2 * DEPTH, FF_SHARD, D_MODEL), "in": (DEPTH, D_MODEL, IN_SHARD),
              "br": (DEPTH, BR_ROWS, D_SHARD), "out": (DEPTH, D_SHARD, D_MODEL)}


def _unit_tensors(u):
    l, j = divmod(u, 3)
    if j == 1:
        return [("in", l), ("br", l), ("out", l)]
    return [(k, 2 * l + j // 2) for k in ("gate", "up", "down")]


class GradStream:
    def __init__(self, chip, core):
        self.core = core
        self.place = jnp.concatenate([chip, core])
        self.held, self.flying = {}, []
        self.full = {k: lax.empty(shape, F32) for k, shape in GRAD_SLOTS.items()}

    def put(self, u, grads, after=()):
        self.held[u] = grads
        si = len(self.flying)
        units = REDUCE_STAGES[si]
        if not all(v in self.held for v in units):
            return jnp.zeros((1, 1), F32)
        gs = [g for v in units for g in self.held[v]]

        @_plan(0, len(gs))
        def swap_halves(ins, outs, x, y, c):
            sends = [(_half(g, 1, 1 - c), o, (x, y, 1 - c)) for g, o in zip(ins, outs)]
            return [], sends, list(outs)

        half_shapes = [jax.ShapeDtypeStruct((N_CHIPS, g.shape[1] // 2, g.shape[2]), BF16) for g in gs]
        landed = exchange(f"reduce_swap_{si}", gs + list(after), half_shapes, {}, swap_halves)
        sums = [_add_halves(g, la, self.core) for g, la in zip(gs, landed)]
        landing = [lax.empty((3,) + s.shape[1:], BF16) for s in sums]
        send, recv, bufs, token = split_start(f"reduce_start_{si}", sums + landing, (), 3 * len(sums), _reduce_copies)
        self.flying.append((send, recv, bufs, [t for v in units for t in _unit_tensors(v)]))
        return token[0:1, 0:1]

    def finish(self, after):
        for si, (send, recv, bufs, tensors) in enumerate(self.flying):
            bufs = split_wait(f"reduce_wait_{si}", bufs, send, recv, tuple(after), _reduce_copies)
            n = len(tensors)
            for (name, slot), s, land in zip(tensors, bufs[:n], bufs[n:]):
                self.full[name] = _add_chips(s, land, self.place, self.full[name], slot)
        names = list(self.full)

        @_plan(0, len(names))
        def share_halves(ins, outs, x, y, c):
            sends = [(_half(o, 1, c), _half(o, 1, c), (x, y, 1 - c)) for o in outs]
            return [], sends, [_half(o, 1, 1 - c) for o in outs]

        shapes = [jax.ShapeDtypeStruct(self.full[k].shape, F32) for k in names]
        out = exchange("reduce_share_halves", [self.full[k] for k in names], shapes, {k: k for k in range(len(names))},
                       share_halves)
        return dict(zip(names, out))


def _add_halves(g, landed, core):
    _, rh, cols = landed.shape
    tr = _row_tile(rh, cols, 1024 * 1024)
    per_half = rh // tr

    def body(core_ref, g_ref, la_ref, o_ref):
        del core_ref
        o_ref[...] = (g_ref[...].astype(F32) + la_ref[...].astype(F32)).astype(BF16)

    blk = (None, tr, cols)
    return pl.pallas_call(
        body, out_shape=jax.ShapeDtypeStruct(landed.shape, BF16),
        grid_spec=pltpu.PrefetchScalarGridSpec(
            num_scalar_prefetch=1, grid=(N_CHIPS, per_half),
            in_specs=[pl.BlockSpec(blk, lambda j, r, core_ref: (j, core_ref[0] * per_half + r, 0)),
                      pl.BlockSpec(blk, lambda j, r, core_ref: (j, r, 0))],
            out_specs=pl.BlockSpec(blk, lambda j, r, core_ref: (j, r, 0))),
        name="reduce_add_halves", compiler_params=_params(("parallel", "parallel")),
    )(core, g, landed)


def _add_chips(sums, landed, place, full, slot):
    _, rh, cols = sums.shape
    tr = _row_tile(rh, cols, 1024 * 1024)
    per_half = rh // tr

    def body(place_ref, s_ref, la_ref, full_in, o_ref):
        del place_ref, full_in
        o_ref[...] = ((s_ref[...].astype(F32) + la_ref[0].astype(F32)) + la_ref[1].astype(F32)) + la_ref[2].astype(F32)

    return pl.pallas_call(
        body, out_shape=jax.ShapeDtypeStruct(full.shape, F32),
        grid_spec=pltpu.PrefetchScalarGridSpec(
            num_scalar_prefetch=1, grid=(per_half,),
            in_specs=[pl.BlockSpec((None, tr, cols), lambda r, place_ref: (place_ref[0], r, 0)),
                      pl.BlockSpec((3, tr, cols), lambda r, place_ref: (0, r, 0)), ANY],
            out_specs=pl.BlockSpec((None, tr, cols), lambda r, place_ref: (slot, place_ref[1] * per_half + r, 0))),
        input_output_aliases={3: 0}, name="reduce_add_chips", compiler_params=_params(("parallel",)),
    )(place, sums, landed, full)


def sum_devices(parts):
    def body(p_ref, o_ref):
        acc = p_ref[0]
        for d in range(1, N_DEV):
            acc = acc + p_ref[d]
        o_ref[...] = acc

    return pl.pallas_call(body, out_shape=jax.ShapeDtypeStruct(parts.shape[1:], F32), name="sum_devices")(parts)


ADA_SHARD = 9 * D_MODEL // N_CHIPS
ADA_TILE = 768
ADA_ROWS = 16


def ada_forward(c_rows, w_ada, b_shard):
    def body(c_ref, w_ref, b_ref, o_ref):
        cv = c_ref[...]
        o_ref[...] = _dot((cv * _sigmoid(cv)).astype(BF16), w_ref[...].astype(BF16), NN) + b_ref[...]

    return pl.pallas_call(
        body, grid=(DEPTH, ADA_SHARD // ADA_TILE),
        in_specs=[pl.BlockSpec((ADA_ROWS, D_MODEL), lambda l, n: (0, 0)),
                  pl.BlockSpec((None, D_MODEL, ADA_TILE), lambda l, n: (l, 0, n)),
                  pl.BlockSpec((None, 1, ADA_TILE), lambda l, n: (l, 0, n))],
        out_specs=pl.BlockSpec((None, ADA_ROWS, ADA_TILE), lambda l, n: (l, 0, n)),
        out_shape=jax.ShapeDtypeStruct((DEPTH, ADA_ROWS, ADA_SHARD), F32),
        name="ada_forward", compiler_params=_params(("parallel", "parallel")),
    )(c_rows, w_ada, b_shard)


def ada_backward(c_rows, dmod_rows):
    def body(c_ref, d_ref, o_ref):
        cv = c_ref[...]
        o_ref[...] = _dot((cv * _sigmoid(cv)).astype(BF16), d_ref[...].astype(BF16), TN)

    return pl.pallas_call(
        body, grid=(DEPTH, ADA_SHARD // ADA_TILE),
        in_specs=[pl.BlockSpec((ADA_ROWS, D_MODEL), lambda l, n: (0, 0)),
                  pl.BlockSpec((None, ADA_ROWS, ADA_TILE), lambda l, n: (l, 0, n))],
        out_specs=pl.BlockSpec((None, D_MODEL, ADA_TILE), lambda l, n: (l, 0, n)),
        out_shape=jax.ShapeDtypeStruct((DEPTH, D_MODEL, ADA_SHARD), F32),
        name="ada_backward", compiler_params=_params(("parallel", "parallel")),
    )(c_rows, dmod_rows)


def adamw(name, w, g, m, v):
    shape = w.shape
    cols = shape[-1]
    rows = w.size // cols
    tr = _row_tile(rows, cols) if rows % 16 == 0 else rows
    c1 = 1.0 / (1.0 - ADAM_B1 ** ADAM_STEP)
    c2 = 1.0 / (1.0 - ADAM_B2 ** ADAM_STEP)

    def body(w_ref, g_ref, m_ref, v_ref, go_ref, d_ref, mo_ref, vo_ref):
        gv = g_ref[...]
        mn = ADAM_B1 * m_ref[...] + (1.0 - ADAM_B1) * gv
        vn = ADAM_B2 * v_ref[...] + (1.0 - ADAM_B2) * (gv * gv)
        go_ref[...] = gv
        mo_ref[...] = mn
        vo_ref[...] = vn
        d_ref[...] = -ADAM_LR * ((mn * c1) / (jnp.sqrt(vn * c2) + ADAM_EPS) + ADAM_WD * w_ref[...])

    spec = pl.BlockSpec((tr, cols), lambda i: (i, 0))
    out = jax.ShapeDtypeStruct((rows, cols), F32)
    res = pl.pallas_call(
        body, grid=(rows // tr,), in_specs=[spec] * 4, out_specs=[spec] * 4, out_shape=[out] * 4,
        name=name, compiler_params=_params(("parallel",)),
    )(*[t.reshape(rows, cols) for t in (w, g, m, v)])
    return tuple(r.reshape(shape) for r in res)


def _pack(parts, rows):
    flat = jnp.concatenate([p.reshape(-1) for p in parts])
    return jnp.pad(flat, (0, rows * 128 - flat.size)).reshape(rows, 128)


def _unpack(flat, shapes):
    out, at = [], 0
    for s in shapes:
        n = math.prod(s)
        out.append(flat[at:at + n].reshape(s))
        at += n
    return out


def kernel(x, c, w_ada, b_ada, norm_gain, w_ffn_gate, w_ffn_up, w_ffn_down, w_in, w_br_sb, w_br_dil, w_br_swa, w_out, sinks, rel_bias, final_gain, loss_target, m_w_ada, m_b_ada, m_norm_gain, m_w_ffn_gate, m_w_ffn_up, m_w_ffn_down, m_w_in, m_w_br_sb, m_w_br_dil, m_w_br_swa, m_w_out, m_sinks, m_rel_bias, m_final_gain, v_w_ada, v_b_ada, v_norm_gain, v_w_ffn_gate, v_w_ffn_up, v_w_ffn_down, v_w_in, v_w_br_sb, v_w_br_dil, v_w_br_swa, v_w_out, v_sinks, v_rel_bias, v_final_gain):
    xi, yi, ci = _position()
    chip = 2 * xi + yi
    dev = 2 * chip + ci

    c_all = all_gather_small("gather_c", c.reshape(8, 128)).reshape(N_DEV, D_MODEL)
    c_rows = jnp.pad(c_all, ((0, ADA_ROWS - N_DEV), (0, 0)))
    b_shard = lax.dynamic_slice_in_dim(b_ada, chip * ADA_SHARD, ADA_SHARD, axis=1).reshape(DEPTH, 1, ADA_SHARD)
    mod_shard = ada_forward(c_rows, w_ada, b_shard)[:, :N_DEV]
    n_mod = DEPTH * N_DEV * ADA_SHARD
    gathered = all_gather_small("gather_mod", _pack([mod_shard, norm_gain], 304))[::2].reshape(N_CHIPS, -1)
    mod_all = gathered[:, :n_mod].reshape(N_CHIPS, DEPTH, N_DEV, ADA_SHARD)
    mod = lax.dynamic_index_in_dim(mod_all, dev, axis=2, keepdims=False)
    mod = mod.transpose(1, 0, 2).reshape(DEPTH, 3, 3, D_MODEL)
    gains = gathered[:, n_mod:n_mod + DEPTH * 3 * D_SHARD].reshape(N_CHIPS, DEPTH, 3, D_SHARD)
    gains = gains.transpose(1, 2, 0, 3).reshape(DEPTH, 3, D_MODEL)

    chip_i, core_i = chip.astype(jnp.int32).reshape(1), ci.astype(jnp.int32).reshape(1)
    w_br = jnp.concatenate([w_br_sb, w_br_dil, w_br_swa], axis=1)
    transposed = (3, 4)
    w_gate_t, w_up_t = jnp.swapaxes(w_ffn_gate, 2, 3), jnp.swapaxes(w_ffn_up, 2, 3)
    shards = []
    for l in range(DEPTH):
        ffn = [[(w_gate_t, (l, f)), (w_up_t, (l, f)), (w_ffn_down, (l, f))] for f in range(2)]
        shards += [ffn[0], [(w_in, (l,)), (w_br, (l,)), (w_out, (l,))], ffn[1]]
    weights_in = WeightStream(shards, chip_i, (gathered,))
    grads_out = GradStream(chip_i, core_i)

    loss, dx, dmod, dgains, dfinal, dsinks, drel, last_grads = device_step(
        x[0], loss_target[0], mod, gains, final_gain, sinks, rel_bias, weights_in.get, grads_out.put)

    small_shapes = [(DEPTH, 9 * D_MODEL), (DEPTH, 3, D_MODEL), (D_MODEL,), (DEPTH, H_SWA_Q), (N_BUCKETS, 12), (1,)]
    small_all = all_gather_small("gather_small_grads", _pack([dmod, dgains, dfinal, dsinks, drel, loss[0, 0:1]], 208))
    started = grads_out.put(0, last_grads, after=(small_all,))
    small_all = small_all + started
    g_b_ada, g_gain_full, g_final, g_sinks, g_rel, loss_sum = _unpack(sum_devices(small_all).reshape(-1), small_shapes)
    g_gain = lax.dynamic_slice_in_dim(g_gain_full, chip * D_SHARD, D_SHARD, axis=2)
    dmod_all = small_all.reshape(N_DEV, -1)[:, :DEPTH * 9 * D_MODEL].reshape(N_DEV, DEPTH, 9 * D_MODEL)
    dmod_rows = lax.dynamic_slice_in_dim(dmod_all, chip * ADA_SHARD, ADA_SHARD, axis=2).transpose(1, 0, 2)
    g_w_ada = ada_backward(c_rows, jnp.pad(dmod_rows, ((0, 0), (0, ADA_ROWS - N_DEV), (0, 0))))

    weights = [w_ada, b_ada, norm_gain, w_ffn_gate, w_ffn_up, w_ffn_down, w_in, w_br_sb, w_br_dil, w_br_swa, w_out,
               sinks, rel_bias, final_gain]
    ms = [m_w_ada, m_b_ada, m_norm_gain, m_w_ffn_gate, m_w_ffn_up, m_w_ffn_down, m_w_in, m_w_br_sb, m_w_br_dil,
          m_w_br_swa, m_w_out, m_sinks, m_rel_bias, m_final_gain]
    vs = [v_w_ada, v_b_ada, v_norm_gain, v_w_ffn_gate, v_w_ffn_up, v_w_ffn_down, v_w_in, v_w_br_sb, v_w_br_dil,
          v_w_br_swa, v_w_out, v_sinks, v_rel_bias, v_final_gain]
    grads = [g_w_ada, g_b_ada, g_gain] + [None] * 8 + [g_sinks, g_rel, g_final]

    small = (1, 2, 11, 12, 13)
    deltas, new_ms, new_vs = [None] * 14, [None] * 14, [None] * 14
    _, deltas[0], new_ms[0], new_vs[0] = adamw("adamw_0", weights[0], grads[0], ms[0], vs[0])
    shapes = [weights[k].shape for k in small]
    packed = [_pack([t[k] for k in small], 168) for t in (weights, grads, ms, vs)]
    for dst, res in zip((deltas, new_ms, new_vs), adamw("adamw_small", *packed)[1:]):
        for k, t in zip(small, _unpack(res.reshape(-1), shapes)):
            dst[k] = t

    g = grads_out.finish((dx, deltas[0], deltas[1]))
    g_br = g["br"]
    grads[3:11] = [g["gate"].reshape(w_gate_t.shape), g["up"].reshape(w_up_t.shape),
                   g["down"].reshape(w_ffn_down.shape), g["in"], g_br[:, 0:256], g_br[:, 256:384], g_br[:, 384:768],
                   g["out"]]
    for k in range(3, 11):
        state = [weights[k], ms[k], vs[k]]
        if k in transposed:
            state = [jnp.swapaxes(t, 2, 3) for t in state]
        out = adamw(f"adamw_{k}", state[0], grads[k], state[1], state[2])
        if k in transposed:
            out = [jnp.swapaxes(t, 2, 3) for t in out]
        grads[k], deltas[k], new_ms[k], new_vs[k] = out
    return (loss_sum[0], dx[None], *grads, *deltas, *new_ms, *new_vs)
```

```python
import functools
import math

import jax
import jax.numpy as jnp
from jax import lax
from jax.experimental import pallas as pl
from jax.experimental.pallas import tpu as pltpu

F32 = jnp.float32
BF16 = jnp.bfloat16

D_MODEL = 1024
SEQ = 2048
DEPTH = 2
HEAD_DIM = 64
BLK = 128
H_SB = 4
DIL_PATTERNS = ((128, 1), (512, 4), (2048, 16))
H_PER_DIL = 2
H_DIL = 6
H_SWA_Q = 6
H_SWA_KV = 2
SWA_WINDOW = 128
N_BUCKETS = 32
MAX_REL_DIST = 2048
D_FF = 2816
RMS_EPS = 1e-6
N_CHIPS = 4
N_DEV = 8
FF_SHARD = D_FF // N_CHIPS
D_QKV = 2560
D_IN = D_QKV + 3 * D_MODEL
IN_SHARD = D_IN // N_CHIPS
D_SHARD = D_MODEL // N_CHIPS
BR_ROWS = 768
NEG = -1e30
QK_SCALE = HEAD_DIM ** -0.5

ADAM_LR = 0.001
ADAM_B1 = 0.9
ADAM_B2 = 0.999
ADAM_EPS = 1e-08
ADAM_WD = 0.01
ADAM_STEP = 10

VMEM_LIMIT = 56 * 1024 * 1024
ROW_TILE = 256
MM_TILE = 1024

NN = (((1,), (0,)), ((), ()))
NT = (((1,), (1,)), ((), ()))
TN = (((0,), (0,)), ((), ()))


def _params(sem=None):
    return pltpu.CompilerParams(dimension_semantics=sem, vmem_limit_bytes=VMEM_LIMIT)


def _dot(a, b, dims):
    return lax.dot_general(a, b, dims, preferred_element_type=F32)


def _sigmoid(x):
    return 1.0 / (1.0 + jnp.exp(-x))


def _matmul(name, grid, nk, k_axis, dims, n_pairs, in_specs, out_specs, out_shape, acc_shape, epilogue,
            operands, sem, aliases=None, prologue=None):
    n_in = len(in_specs)
    n_out = len(out_specs)

    def partial(ins):
        tot = None
        for p in range(n_pairs):
            a = ins[2 * p][...]
            if prologue is not None:
                a = prologue(p, a, ins)
            d = _dot(a, ins[2 * p + 1][...], dims)
            tot = d if tot is None else tot + d
        return tot

    def body(*refs):
        ins, outs = refs[:n_in], refs[n_in:n_in + n_out]
        ids = tuple(pl.program_id(a) for a in range(len(grid)))
        if nk == 1:
            epilogue(partial(ins), ins, outs, ids)
            return
        acc = refs[n_in + n_out]
        k = ids[k_axis]

        @pl.when(k == 0)
        def _():
            acc[...] = partial(ins)

        @pl.when(k > 0)
        def _():
            acc[...] += partial(ins)

        @pl.when(k == nk - 1)
        def _():
            epilogue(acc[...], ins, outs, ids)

    return pl.pallas_call(
        body, grid=grid, in_specs=in_specs, out_specs=out_specs, out_shape=out_shape,
        scratch_shapes=[] if nk == 1 else [pltpu.VMEM(acc_shape, F32)],
        input_output_aliases=aliases or {}, name=name, compiler_params=_params(sem),
    )(*operands)


def _row_spec(width=D_MODEL):
    return pl.BlockSpec((ROW_TILE, width), lambda i: (i, 0))


def _vec_spec(rows=1, width=D_MODEL):
    return pl.BlockSpec((rows, width), lambda i: (0, 0))


class Row:
    def __init__(self, table, index):
        self.table, self.index = table, index

    def spec(self):
        index = self.index
        return pl.BlockSpec((None, 1, D_MODEL), lambda *ids: (index, 0, 0))


def _slot_spec(u):
    return pl.BlockSpec((8, D_MODEL), lambda *ids: (u, 0))


def prenorm(x, gain, scale, shift):
    def body(x_ref, g_ref, sc_ref, sh_ref, h_ref):
        xv = x_ref[...]
        r = lax.rsqrt(jnp.mean(xv * xv, axis=-1, keepdims=True) + RMS_EPS)
        h_ref[...] = (((xv * r) * g_ref[...]) * (1.0 + sc_ref[...]) + sh_ref[...]).astype(BF16)

    return pl.pallas_call(
        body, grid=(SEQ // ROW_TILE,), in_specs=[_row_spec(), gain.spec(), scale.spec(), shift.spec()],
        out_specs=_row_spec(), out_shape=jax.ShapeDtypeStruct((SEQ, D_MODEL), BF16),
        name="prenorm", compiler_params=_params(("parallel",)),
    )(x, gain.table, scale.table, shift.table)


def resid_bwd(dxo, f, coef, mult, sums, u):
    def body(dx_ref, f_ref, c_ref, sums_in, df_ref, dc_ref):
        del sums_in
        dx = dx_ref[...]
        df_ref[...] = (dx * (mult * c_ref[...])).astype(BF16)
        part = mult * jnp.sum(dx * f_ref[...], axis=0, keepdims=True)

        @pl.when(pl.program_id(0) == 0)
        def _():
            dc_ref[...] = jnp.zeros_like(dc_ref)

        dc_ref[0:1, :] += part

    return pl.pallas_call(
        body, grid=(SEQ // ROW_TILE,),
        in_specs=[_row_spec(), _row_spec(), coef.spec(), pl.BlockSpec(memory_space=pl.ANY)],
        out_specs=[_row_spec(), _slot_spec(u)],
        out_shape=[jax.ShapeDtypeStruct((SEQ, D_MODEL), BF16), jax.ShapeDtypeStruct(sums.shape, F32)],
        input_output_aliases={3: 1}, name="resid_bwd", compiler_params=_params(("arbitrary",)),
    )(dxo, f, coef.table, sums)


def final_loss(x, gain, target):
    def body(x_ref, g_ref, t_ref, loss_ref, dx_ref, dg_ref):
        xv = x_ref[...]
        g = g_ref[...]
        r = lax.rsqrt(jnp.mean(xv * xv, axis=-1, keepdims=True) + RMS_EPS)
        xh = xv * r
        e = xh * g - t_ref[...]
        part = 0.5 * jnp.sum(jnp.mean(e * e, axis=-1, keepdims=True), axis=0, keepdims=True)
        dy = e * (1.0 / D_MODEL)
        dyg = dy * g
        dx_ref[...] = r * (dyg - xh * jnp.mean(dyg * xh, axis=-1, keepdims=True))

        @pl.when(pl.program_id(0) == 0)
        def _():
            loss_ref[...] = jnp.zeros_like(loss_ref)
            dg_ref[...] = jnp.zeros_like(dg_ref)

        loss_ref[...] += jnp.broadcast_to(part, loss_ref.shape)
        dg_ref[0:1, :] += jnp.sum(dy * xh, axis=0, keepdims=True)

    return pl.pallas_call(
        body, grid=(SEQ // ROW_TILE,), in_specs=[_row_spec(), _vec_spec(), _row_spec()],
        out_specs=[_vec_spec(8, 128), _row_spec(), _vec_spec(8)],
        out_shape=[jax.ShapeDtypeStruct((8, 128), F32), jax.ShapeDtypeStruct((SEQ, D_MODEL), F32),
                   jax.ShapeDtypeStruct((8, D_MODEL), F32)],
        name="final_loss", compiler_params=_params(("arbitrary",)),
    )(x, gain, target)


def _prenorm_bwd_epilogue(dh, x_ref, dxo_ref, g_ref, sc_ref, dx_ref, stats_ref, first):
    xv = x_ref[...]
    g = g_ref[...]
    r = lax.rsqrt(jnp.mean(xv * xv, axis=-1, keepdims=True) + RMS_EPS)
    xh = xv * r
    dn = dh * (1.0 + sc_ref[...])
    dxh = dn * g
    dx = dxo_ref[...] + r * (dxh - xh * jnp.mean(dxh * xh, axis=-1, keepdims=True))
    dx_ref[...] = dx

    @pl.when(first)
    def _():
        stats_ref[...] = jnp.zeros_like(stats_ref)

    stats_ref[0:1, :] += jnp.sum(dh, axis=0, keepdims=True)
    stats_ref[1:2, :] += jnp.sum(dh * (xh * g), axis=0, keepdims=True)
    stats_ref[2:3, :] += jnp.sum(dn * xh, axis=0, keepdims=True)
    return dx


def _resid_bwd_epilogue(dx, f_ref, c_ref, mult, df_ref, dc_ref, first):
    df_ref[...] = (dx * (mult * c_ref[...])).astype(BF16)

    @pl.when(first)
    def _():
        dc_ref[...] = jnp.zeros_like(dc_ref)

    dc_ref[0:1, :] += mult * jnp.sum(dx * f_ref[...], axis=0, keepdims=True)


def ffn_up(h, wg_all, wu_all):
    def body(h_ref, wg_ref, wu_ref, a_ref, b_ref, s_ref):
        hv = h_ref[...]
        a = _dot(hv, wg_ref[...], NT)
        b = _dot(hv, wu_ref[...], NT)
        a_ref[...] = a.astype(BF16)
        b_ref[...] = b.astype(BF16)
        s_ref[...] = (a * _sigmoid(a) * b).astype(BF16)

    w_spec = pl.BlockSpec((None, FF_SHARD, D_MODEL), lambda j, i: (j, 0, 0))
    o_spec = pl.BlockSpec((None, MM_TILE, FF_SHARD), lambda j, i: (j, i, 0))
    hid = (N_CHIPS, SEQ, FF_SHARD)
    return pl.pallas_call(
        body, grid=(N_CHIPS, SEQ // MM_TILE),
        in_specs=[pl.BlockSpec((MM_TILE, D_MODEL), lambda j, i: (i, 0)), w_spec, w_spec],
        out_specs=[o_spec, o_spec, o_spec],
        out_shape=[jax.ShapeDtypeStruct(hid, BF16)] * 3,
        name="ffn_up", compiler_params=_params(("parallel", "parallel")),
    )(h, wg_all, wu_all)


def matmul_residual(name, a, a_spec, w_all, w_spec, x, coef, mult, then=None):
    def epilogue(acc, ins, outs, ids):
        outs[0][...] = acc
        xo = ins[2][...] + (mult * ins[3][...]) * acc
        outs[1][...] = xo
        if then is not None:
            r = lax.rsqrt(jnp.mean(xo * xo, axis=-1, keepdims=True) + RMS_EPS)
            outs[2][...] = (((xo * r) * ins[4][...]) * (1.0 + ins[5][...]) + ins[6][...]).astype(BF16)

    row = pl.BlockSpec((MM_TILE, D_MODEL), lambda i, j: (i, 0))
    f32 = jax.ShapeDtypeStruct((SEQ, D_MODEL), F32)
    extra = [] if then is None else [then.gain, then.scale, then.shift]
    return _matmul(
        name, (SEQ // MM_TILE, N_CHIPS), N_CHIPS, 1, NN, 1,
        [a_spec, w_spec, row, coef.spec()] + [t.spec() for t in extra], [row] * (2 + bool(extra)),
        [f32, f32] + [jax.ShapeDtypeStruct((SEQ, D_MODEL), BF16)] * bool(extra), (MM_TILE, D_MODEL), epilogue,
        (a, w_all, x, coef.table) + tuple(t.table for t in extra), ("parallel", "arbitrary"))


def ffn_down(s, wd_all, x, gate, then):
    return matmul_residual(
        "ffn_down", s, pl.BlockSpec((None, MM_TILE, FF_SHARD), lambda i, j: (j, i, 0)),
        wd_all, pl.BlockSpec((None, FF_SHARD, D_MODEL), lambda i, j: (j, 0, 0)), x, gate, 0.5, then)


def ffn_bwd_hidden(df, wd_all, a, b):
    def epilogue(ds, ins, outs, ids):
        av, bv = ins[2][...].astype(F32), ins[3][...].astype(F32)
        sig = _sigmoid(av)
        outs[0][...] = (ds * bv * (sig * (1.0 + av * (1.0 - sig)))).astype(BF16)
        outs[1][...] = (ds * (av * sig)).astype(BF16)

    hid_spec = pl.BlockSpec((None, MM_TILE, FF_SHARD), lambda j, i: (j, i, 0))
    hid = jax.ShapeDtypeStruct((N_CHIPS, SEQ, FF_SHARD), BF16)
    return _matmul(
        "ffn_bwd_hidden", (N_CHIPS, SEQ // MM_TILE), 1, None, NT, 1,
        [pl.BlockSpec((MM_TILE, D_MODEL), lambda j, i: (i, 0)),
         pl.BlockSpec((None, FF_SHARD, D_MODEL), lambda j, i: (j, 0, 0)), hid_spec, hid_spec],
        [hid_spec, hid_spec], [hid, hid], None, epilogue, (df, wd_all, a, b), ("parallel", "parallel"))


def grad_weight(name, lhs, lhs_spec, rhs, rhs_spec, shape):
    def epilogue(acc, ins, outs, ids):
        outs[0][...] = acc.astype(BF16)

    return _matmul(
        name, (N_CHIPS, SEQ // MM_TILE), SEQ // MM_TILE, 1, TN, 1,
        [lhs_spec, rhs_spec], [pl.BlockSpec((None,) + shape, lambda j, k: (j, 0, 0))],
        [jax.ShapeDtypeStruct((N_CHIPS,) + shape, BF16)], shape, epilogue, (lhs, rhs), ("parallel", "arbitrary"))[0]


def ffn_grad_weights(h, s, df, da, db):
    tok = pl.BlockSpec((MM_TILE, D_MODEL), lambda j, k: (k, 0))
    hid = pl.BlockSpec((None, MM_TILE, FF_SHARD), lambda j, k: (j, k, 0))
    return (grad_weight("grad_w_gate", da, hid, h, tok, (FF_SHARD, D_MODEL)),
            grad_weight("grad_w_up", db, hid, h, tok, (FF_SHARD, D_MODEL)),
            grad_weight("grad_w_down", s, hid, df, tok, (FF_SHARD, D_MODEL)))


BWD_TILE = 512


def matmul_prenorm_bwd(name, dims, pairs, pair_specs, x, dxo, gain, scale, sums, u, below):
    n = len(pairs)

    def epilogue(dh, ins, outs, ids):
        first = ids[0] == 0
        dx = _prenorm_bwd_epilogue(dh, ins[n], ins[n + 1], ins[n + 2], ins[n + 3], outs[0], outs[1], first)
        if below is not None:
            _resid_bwd_epilogue(dx, ins[n + 5], ins[n + 6], below[2], outs[2], outs[3], first)

    row = pl.BlockSpec((BWD_TILE, D_MODEL), lambda i, j: (i, 0))
    any_spec = pl.BlockSpec(memory_space=pl.ANY)
    f32 = jax.ShapeDtypeStruct((SEQ, D_MODEL), F32)
    in_specs = list(pair_specs) + [row, row, gain.spec(), scale.spec(), any_spec]
    operands = tuple(pairs) + (x, dxo, gain.table, scale.table, sums[0])
    out_specs, out_shape, aliases = [row, _slot_spec(u)], [f32, jax.ShapeDtypeStruct(sums[0].shape, F32)], {n + 4: 1}
    if below is not None:
        in_specs += [row, below[1].spec(), any_spec]
        operands += (below[0], below[1].table, sums[1])
        out_specs += [row, _slot_spec(u - 1)]
        out_shape += [jax.ShapeDtypeStruct((SEQ, D_MODEL), BF16), jax.ShapeDtypeStruct(sums[1].shape, F32)]
        aliases[n + 7] = 3
    out = _matmul(name, (SEQ // BWD_TILE, N_CHIPS), N_CHIPS, 1, dims, len(pairs) // 2, in_specs, out_specs, out_shape,
                  (BWD_TILE, D_MODEL), epilogue, operands, ("arbitrary", "arbitrary"), aliases=aliases)
    if below is None:
        return out[0], (out[1], sums[1])
    return out[0], (out[1], out[3]), out[2]


def ffn_bwd_input(da, db, wg_all, wu_all, x, dxo, gain, scale, sums, u, below):
    hid = pl.BlockSpec((None, BWD_TILE, FF_SHARD), lambda i, j: (j, i, 0))
    w = pl.BlockSpec((None, FF_SHARD, D_MODEL), lambda i, j: (j, 0, 0))
    return matmul_prenorm_bwd("ffn_bwd_input", NN, (da, wg_all, db, wu_all), (hid, w, hid, w), x, dxo, gain, scale,
                              sums, u, below)


def in_proj(h, w_all):
    def epilogue(acc, ins, outs, ids):
        outs[0][...] = acc
        outs[1][...] = acc.astype(BF16)

    out = pl.BlockSpec((MM_TILE, IN_SHARD), lambda j, i: (i, j))
    return _matmul(
        "in_proj", (N_CHIPS, SEQ // MM_TILE), 1, None, NN, 1,
        [pl.BlockSpec((MM_TILE, D_MODEL), lambda j, i: (i, 0)),
         pl.BlockSpec((None, D_MODEL, IN_SHARD), lambda j, i: (j, 0, 0))],
        [out, out], [jax.ShapeDtypeStruct((SEQ, D_IN), F32), jax.ShapeDtypeStruct((SEQ, D_IN), BF16)],
        None, epilogue, (h, w_all), ("parallel", "parallel"))


_GATE_BLOCK0 = D_QKV // D_SHARD


def _branch_products(o, w_ref):
    ob = o.astype(BF16)
    return (_dot(ob[:, 0:256], w_ref[0:256, :], NN), _dot(ob[:, 256:384], w_ref[256:384, :], NN),
            _dot(ob[:, 384:768], w_ref[384:768, :], NN))


def merge_branches(o_cat, wbr_all, proj):
    def body(o_ref, w_ref, g0_ref, g1_ref, g2_ref, m_ref):
        u = _branch_products(o_ref[...], w_ref)
        m_ref[...] = (_sigmoid(g0_ref[...]) * u[0] + _sigmoid(g1_ref[...]) * u[1]
                      + _sigmoid(g2_ref[...]) * u[2]).astype(BF16)

    def gate_spec(b):
        return pl.BlockSpec((MM_TILE, D_SHARD), lambda i, j: (i, _GATE_BLOCK0 + 4 * b + j))

    return pl.pallas_call(
        body, grid=(SEQ // MM_TILE, N_CHIPS),
        in_specs=[pl.BlockSpec((MM_TILE, BR_ROWS), lambda i, j: (i, 0)),
                  pl.BlockSpec((None, BR_ROWS, D_SHARD), lambda i, j: (j, 0, 0)),
                  gate_spec(0), gate_spec(1), gate_spec(2)],
        out_specs=pl.BlockSpec((MM_TILE, D_SHARD), lambda i, j: (i, j)),
        out_shape=jax.ShapeDtypeStruct((SEQ, D_MODEL), BF16),
        name="merge_branches", compiler_params=_params(("parallel", "parallel")),
    )(o_cat, wbr_all, proj, proj, proj)


def out_proj(merged, wout_all, x, gate, then):
    return matmul_residual(
        "out_proj", merged, pl.BlockSpec((MM_TILE, D_SHARD), lambda i, j: (i, j)),
        wout_all, pl.BlockSpec((None, D_SHARD, D_MODEL), lambda i, j: (j, 0, 0)), x, gate, 1.0, then)


def merge_bwd(dmo, wout_all, o_cat, wbr_all, proj):
    def epilogue(dm, ins, outs, ids):
        u = _branch_products(ins[2][...], ins[3])
        for b in range(3):
            sig = _sigmoid(ins[4 + b][...])
            outs[b][...] = (dm * sig).astype(BF16)
            outs[3 + b][...] = (dm * u[b] * (sig * (1.0 - sig))).astype(BF16)

    def gate_spec(b):
        return pl.BlockSpec((MM_TILE, D_SHARD), lambda j, i: (i, _GATE_BLOCK0 + 4 * b + j))

    col = pl.BlockSpec((MM_TILE, D_SHARD), lambda j, i: (i, j))
    du = jax.ShapeDtypeStruct((SEQ, D_MODEL), BF16)
    return _matmul(
        "merge_bwd", (N_CHIPS, SEQ // MM_TILE), 1, None, NT, 1,
        [pl.BlockSpec((MM_TILE, D_MODEL), lambda j, i: (i, 0)),
         pl.BlockSpec((None, D_SHARD, D_MODEL), lambda j, i: (j, 0, 0)),
         pl.BlockSpec((MM_TILE, BR_ROWS), lambda j, i: (i, 0)),
         pl.BlockSpec((None, BR_ROWS, D_SHARD), lambda j, i: (j, 0, 0)),
         gate_spec(0), gate_spec(1), gate_spec(2)],
        [col] * 6, [du] * 6,
        None, epilogue, (dmo, wout_all, o_cat, wbr_all, proj, proj, proj), ("parallel", "parallel"))


def branch_bwd_input(du, wbr_all):
    def body(d0_ref, d1_ref, d2_ref, w_ref, o_ref, acc):
        j = pl.program_id(1)
        parts = (_dot(d0_ref[...], w_ref[0:256, :], NT), _dot(d1_ref[...], w_ref[256:384, :], NT),
                 _dot(d2_ref[...], w_ref[384:768, :], NT))

        @pl.when(j == 0)
        def _():
            acc[:, 0:256], acc[:, 256:384], acc[:, 384:768] = parts

        @pl.when(j > 0)
        def _():
            acc[:, 0:256] += parts[0]
            acc[:, 256:384] += parts[1]
            acc[:, 384:768] += parts[2]

        @pl.when(j == N_CHIPS - 1)
        def _():
            o_ref[...] = acc[...]

    col = pl.BlockSpec((MM_TILE, D_SHARD), lambda i, j: (i, j))
    return pl.pallas_call(
        body, grid=(SEQ // MM_TILE, N_CHIPS),
        in_specs=[col, col, col, pl.BlockSpec((None, BR_ROWS, D_SHARD), lambda i, j: (j, 0, 0))],
        out_specs=pl.BlockSpec((MM_TILE, BR_ROWS), lambda i, j: (i, 0)),
        out_shape=jax.ShapeDtypeStruct((SEQ, BR_ROWS), F32),
        scratch_shapes=[pltpu.VMEM((MM_TILE, BR_ROWS), F32)],
        name="branch_bwd_input", compiler_params=_params(("parallel", "arbitrary")),
    )(du[0], du[1], du[2], wbr_all)


def branch_grad_weights(o_cat, du):
    def body(o_ref, d0_ref, d1_ref, d2_ref, g_ref, acc):
        k = pl.program_id(1)
        ob = o_ref[...].astype(BF16)
        parts = (_dot(ob[:, 0:256], d0_ref[...], TN), _dot(ob[:, 256:384], d1_ref[...], TN),
                 _dot(ob[:, 384:768], d2_ref[...], TN))

        @pl.when(k == 0)
        def _():
            acc[0:256, :], acc[256:384, :], acc[384:768, :] = parts

        @pl.when(k > 0)
        def _():
            acc[0:256, :] += parts[0]
            acc[256:384, :] += parts[1]
            acc[384:768, :] += parts[2]

        @pl.when(k == SEQ // MM_TILE - 1)
        def _():
            g_ref[...] = acc[...].astype(BF16)

    col = pl.BlockSpec((MM_TILE, D_SHARD), lambda j, k: (k, j))
    return pl.pallas_call(
        body, grid=(N_CHIPS, SEQ // MM_TILE),
        in_specs=[pl.BlockSpec((MM_TILE, BR_ROWS), lambda j, k: (k, 0)), col, col, col],
        out_specs=pl.BlockSpec((None, BR_ROWS, D_SHARD), lambda j, k: (j, 0, 0)),
        out_shape=jax.ShapeDtypeStruct((N_CHIPS, BR_ROWS, D_SHARD), BF16),
        scratch_shapes=[pltpu.VMEM((BR_ROWS, D_SHARD), F32)],
        name="branch_grad_weights", compiler_params=_params(("parallel", "arbitrary")),
    )(o_cat, du[0], du[1], du[2])


def mixer_bwd_input(dproj, win_all, x, dxo, gain, scale, sums, u, below):
    return matmul_prenorm_bwd(
        "mixer_bwd_input", NT, (dproj, win_all),
        (pl.BlockSpec((BWD_TILE, IN_SHARD), lambda i, j: (i, j)),
         pl.BlockSpec((None, D_MODEL, IN_SHARD), lambda i, j: (j, 0, 0))), x, dxo, gain, scale, sums, u, below)


BATCH_QK = (((2,), (2,)), ((0,), (0,)))
BATCH_PV = (((2,), (1,)), ((0,), (0,)))
BATCH_TN = (((1,), (1,)), ((0,), (0,)))


SB_WIDTH = H_SB * HEAD_DIM
SB_ROWS = H_SB * BLK


def _split_dot(v, tri):
    hi = v.astype(BF16)
    lo = (v - hi.astype(F32)).astype(BF16)
    return _dot(hi, tri, NN) + _dot(lo, tri, NN)


def _tri(cmp):
    return cmp(lax.broadcasted_iota(jnp.int32, (BLK, BLK), 0), lax.broadcasted_iota(jnp.int32, (BLK, BLK), 1)).astype(BF16)


def _head_masks():
    lane = lax.broadcasted_iota(jnp.int32, (1, SB_WIDTH), 1) // HEAD_DIM
    return [lane == h for h in range(H_SB)]


def _stack_heads(x, masks):
    return jnp.concatenate([jnp.where(m, x, jnp.zeros_like(x)) for m in masks], axis=0)


def _merge_heads(y, masks):
    out = jnp.where(masks[0], y[0:BLK], 0.0)
    for h in range(1, H_SB):
        out = jnp.where(masks[h], y[h * BLK:(h + 1) * BLK], out)
    return out


def _sb_scores(q4, k_ref, j, diagonal):
    rows = pl.ds(pl.multiple_of(j * BLK, BLK), BLK)
    z = _dot(q4, k_ref[rows, :], NT)
    log_fail = -(jnp.maximum(z, 0.0) + jnp.log(1.0 + jnp.exp(-jnp.abs(z))))
    log_hit = z + log_fail
    before = None
    if diagonal:
        tile = (SB_ROWS, BLK)
        before = lax.broadcasted_iota(jnp.int32, tile, 1) < (lax.broadcasted_iota(jnp.int32, tile, 0) & (BLK - 1))
        log_fail = jnp.where(before, log_fail, 0.0)
    return rows, before, log_fail, log_hit


def _keep(before, x):
    return x if before is None else jnp.where(before, x, 0.0)


def sb_forward(qkv):
    def body(q_ref, k_ref, v_ref, o_ref, tot_ref):
        i = pl.program_id(0)
        masks = _head_masks()
        q4 = _stack_heads(q_ref[...] * QK_SCALE, masks)
        later = _tri(lambda r, c: r > c)

        def tiles(js, carry, diagonal):
            o, run = carry
            scores = [_sb_scores(q4, k_ref, j, diagonal) for j in js]
            acc = None
            for rows, before, log_fail, log_hit in scores:
                between = _split_dot(log_fail, later) + run
                w = _keep(before, jnp.exp(log_hit + between))
                part = _dot(w.astype(BF16), v_ref[rows, :], NN)
                acc = part if acc is None else acc + part
                run = run + jnp.sum(log_fail, axis=1, keepdims=True)
            return o + _merge_heads(acc, masks), run

        carry = tiles([i], (jnp.zeros((BLK, SB_WIDTH), F32), jnp.zeros((SB_ROWS, 1), F32)), True)
        carry = lax.cond((i & 1) != 0, lambda c: tiles([i - 1], c, False), lambda c: c, carry)
        at = i - 1 - (i & 1)
        carry = lax.cond((i & 2) != 0, lambda c: tiles([at, at - 1], c, False), lambda c: c, carry)
        at = at - (i & 2)
        o, run = lax.fori_loop(0, i // 4, lambda t, c: tiles([at - 4 * t - n for n in range(4)], c, False), carry)
        o_ref[...] = o
        tot_ref[...] = run

    return pl.pallas_call(
        body, grid=(N_BLK,),
        in_specs=[pl.BlockSpec((BLK, SB_WIDTH), lambda i: (i, 0)), pl.BlockSpec((SEQ, SB_WIDTH), lambda i: (0, 1)),
                  pl.BlockSpec((SEQ, SB_WIDTH), lambda i: (0, 2))],
        out_specs=[pl.BlockSpec((BLK, SB_WIDTH), lambda i: (i, 0)), pl.BlockSpec((None, SB_ROWS, 1), lambda i: (i, 0, 0))],
        out_shape=[jax.ShapeDtypeStruct((SEQ, SB_WIDTH), F32), jax.ShapeDtypeStruct((N_BLK, SB_ROWS, 1), F32)],
        name="sb_forward", compiler_params=_params(("parallel",)),
    )(qkv, qkv, qkv)


def sb_backward(qkv, total, do_cat):
    def body(q_ref, k_ref, v_ref, tot_ref, do_ref, dq_ref, dk_ref, dv_ref):
        i = pl.program_id(0)

        @pl.when(i == 0)
        def _():
            dk_ref[...] = jnp.zeros_like(dk_ref)
            dv_ref[...] = jnp.zeros_like(dv_ref)

        masks = _head_masks()
        q4 = _stack_heads(q_ref[...] * QK_SCALE, masks)
        do4 = _stack_heads(do_ref[...].astype(BF16), masks)
        total_v = tot_ref[...]
        upto = _tri(lambda r, c: r <= c)
        earlier = _tri(lambda r, c: r < c)

        def tiles(js, carry, diagonal):
            dq, seen, g_seen = carry
            scores = [_sb_scores(q4, k_ref, j, diagonal) for j in js]
            acc = None
            for rows, before, log_fail, log_hit in scores:
                between = total_v - (seen + _split_dot(log_fail, upto))
                w = _keep(before, jnp.exp(log_hit + between))
                g = _dot(do4, v_ref[rows, :], NT) * w
                g_earlier = g_seen + _split_dot(g, earlier)
                sig = jnp.exp(log_hit)
                dz = _keep(before, g * (1.0 - sig) - g_earlier * sig).astype(BF16)
                part = _dot(dz, k_ref[rows, :], NN)
                acc = part if acc is None else acc + part
                dk_ref[rows, :] += _dot(dz, q4, TN)
                dv_ref[rows, :] += _dot(w.astype(BF16), do4, TN)
                seen = seen + jnp.sum(log_fail, axis=1, keepdims=True)
                g_seen = g_seen + jnp.sum(g, axis=1, keepdims=True)
            return dq + _merge_heads(acc, masks), seen, g_seen

        zero = jnp.zeros((SB_ROWS, 1), F32)
        carry = lax.fori_loop(0, i // 4, lambda t, c: tiles([4 * t + n for n in range(4)], c, False),
                              (jnp.zeros((BLK, SB_WIDTH), F32), zero, zero))
        at = i - (i & 3)
        carry = lax.cond((i & 2) != 0, lambda c: tiles([at, at + 1], c, False), lambda c: c, carry)
        carry = lax.cond((i & 1) != 0, lambda c: tiles([i - 1], c, False), lambda c: c, carry)
        dq, _, _ = tiles([i], carry, True)
        dq_ref[...] = dq * QK_SCALE

    blk = pl.BlockSpec((BLK, SB_WIDTH), lambda i: (i, 0))
    full = pl.BlockSpec((SEQ, SB_WIDTH), lambda i: (0, 0))
    shape = jax.ShapeDtypeStruct((SEQ, SB_WIDTH), F32)
    return pl.pallas_call(
        body, grid=(N_BLK,),
        in_specs=[blk, pl.BlockSpec((SEQ, SB_WIDTH), lambda i: (0, 1)), pl.BlockSpec((SEQ, SB_WIDTH), lambda i: (0, 2)),
                  pl.BlockSpec((None, SB_ROWS, 1), lambda i: (i, 0, 0)), blk],
        out_specs=[blk, full, full], out_shape=[shape, shape, shape],
        name="sb_backward", compiler_params=_params(("arbitrary",)),
    )(qkv, qkv, qkv, total, do_cat)


def _band_scores(q_ref, kp_ref, ko_ref, bias_ref, hb, prev_mask):
    b = pl.program_id(1)
    qs = q_ref[...]
    s_prev = _dot(qs, kp_ref[...], BATCH_QK) + bias_ref[:, :, 0:BLK]
    s_prev = jnp.concatenate(
        [jnp.where((b & prev_mask(pl.program_id(0) * hb + t)) != 0, s_prev[t:t + 1], NEG) for t in range(hb)], axis=0)
    s_own = _dot(qs, ko_ref[...], BATCH_QK) + bias_ref[:, :, BLK:2 * BLK]
    return qs, s_prev, s_own


def _band_specs(hb, rows, t_n):
    def q_spec(width):
        return pl.BlockSpec((hb, None, rows, width), lambda h, b: (h, b, 0, 0))

    own = pl.BlockSpec((hb, BLK, HEAD_DIM), lambda h, b: (h, b, 0))
    prev = pl.BlockSpec((hb, BLK, HEAD_DIM), lambda h, b: (h, jnp.maximum(b - 1, 0), 0))
    per_head = lambda r, width: pl.BlockSpec((hb, r, width), lambda h, b: (h, 0, 0))
    return q_spec, own, prev, per_head


def banded_forward(name, q, k, v, bias, sinks, hb, prev_mask):
    h_n, nb, rows, _ = q.shape

    def body(q_ref, kp_ref, ko_ref, vp_ref, vo_ref, bias_ref, sink_ref, o_ref, lse_ref):
        _, s_prev, s_own = _band_scores(q_ref, kp_ref, ko_ref, bias_ref, hb, prev_mask)
        sink = sink_ref[...]
        m = jnp.maximum(jnp.maximum(jnp.max(s_prev, axis=2, keepdims=True), jnp.max(s_own, axis=2, keepdims=True)), sink)
        p_prev = jnp.exp(s_prev - m)
        p_own = jnp.exp(s_own - m)
        denom = jnp.sum(p_prev, axis=2, keepdims=True) + jnp.sum(p_own, axis=2, keepdims=True) + jnp.exp(sink - m)
        o = _dot(p_prev.astype(BF16), vp_ref[...], BATCH_PV) + _dot(p_own.astype(BF16), vo_ref[...], BATCH_PV)
        o_ref[...] = o / denom
        lse_ref[...] = m + jnp.log(denom)

    q_spec, own, prev, per_head = _band_specs(hb, rows, k.shape[1])
    return pl.pallas_call(
        body, grid=(h_n // hb, nb),
        in_specs=[q_spec(HEAD_DIM), prev, own, prev, own, per_head(rows, 2 * BLK), per_head(rows, 1)],
        out_specs=[q_spec(HEAD_DIM), q_spec(1)],
        out_shape=[jax.ShapeDtypeStruct(q.shape, F32), jax.ShapeDtypeStruct((h_n, nb, rows, 1), F32)],
        name=name, compiler_params=_params(("parallel", "parallel")),
    )(q, k, k, v, v, bias, sinks)


def banded_backward(name, q, k, v, bias, sinks, o, lse, do, dlse, hb, prev_mask):
    h_n, nb, rows, _ = q.shape
    t_n = k.shape[1]

    def body(q_ref, kp_ref, ko_ref, vp_ref, vo_ref, bias_ref, sink_ref, o_ref, lse_ref, do_ref, dlse_ref,
             dq_ref, dk_ref, dv_ref, dbias_ref, dsink_ref):
        b = pl.program_id(1)

        @pl.when(b == 0)
        def _():
            dk_ref[...] = jnp.zeros_like(dk_ref)
            dv_ref[...] = jnp.zeros_like(dv_ref)
            dbias_ref[...] = jnp.zeros_like(dbias_ref)
            dsink_ref[...] = jnp.zeros_like(dsink_ref)

        qs, s_prev, s_own = _band_scores(q_ref, kp_ref, ko_ref, bias_ref, hb, prev_mask)
        lse_v = lse_ref[...]
        dov = do_ref[...]
        dob = dov.astype(BF16)
        shift = dlse_ref[...] - jnp.sum(dov * o_ref[...], axis=2, keepdims=True)
        p_prev = jnp.exp(s_prev - lse_v)
        p_own = jnp.exp(s_own - lse_v)
        ds_prev = p_prev * (_dot(dob, vp_ref[...], BATCH_QK) + shift)
        ds_own = p_own * (_dot(dob, vo_ref[...], BATCH_QK) + shift)
        dbias_ref[:, :, 0:BLK] += ds_prev
        dbias_ref[:, :, BLK:2 * BLK] += ds_own
        d_sink = jnp.exp(sink_ref[...] - lse_v) * shift
        for g in range(rows // BLK):
            dsink_ref[:, g:g + 1, :] += jnp.sum(d_sink[:, g * BLK:(g + 1) * BLK, :], axis=1, keepdims=True)
        ds_prev = ds_prev.astype(BF16)
        ds_own = ds_own.astype(BF16)
        dq_ref[...] = (_dot(ds_prev, kp_ref[...], BATCH_PV) + _dot(ds_own, ko_ref[...], BATCH_PV)) * QK_SCALE
        rows_prev = pl.ds(pl.multiple_of(jnp.maximum(b - 1, 0) * BLK, BLK), BLK)
        rows_own = pl.ds(pl.multiple_of(b * BLK, BLK), BLK)
        dk_ref[:, rows_prev, :] += _dot(ds_prev, qs, BATCH_TN)
        dk_ref[:, rows_own, :] += _dot(ds_own, qs, BATCH_TN)
        dv_ref[:, rows_prev, :] += _dot(p_prev.astype(BF16), dob, BATCH_TN)
        dv_ref[:, rows_own, :] += _dot(p_own.astype(BF16), dob, BATCH_TN)

    q_spec, own, prev, per_head = _band_specs(hb, rows, t_n)
    kv_full = per_head(t_n, HEAD_DIM)
    kv_shape = jax.ShapeDtypeStruct((h_n, t_n, HEAD_DIM), F32)
    return pl.pallas_call(
        body, grid=(h_n // hb, nb),
        in_specs=[q_spec(HEAD_DIM), prev, own, prev, own, per_head(rows, 2 * BLK), per_head(rows, 1),
                  q_spec(HEAD_DIM), q_spec(1), q_spec(HEAD_DIM), q_spec(1)],
        out_specs=[q_spec(HEAD_DIM), kv_full, kv_full, per_head(rows, 2 * BLK), per_head(rows // BLK, BLK)],
        out_shape=[jax.ShapeDtypeStruct(q.shape, F32), kv_shape, kv_shape,
                   jax.ShapeDtypeStruct((h_n, rows, 2 * BLK), F32), jax.ShapeDtypeStruct((h_n, rows // BLK, BLK), F32)],
        name=name, compiler_params=_params(("parallel", "arbitrary")),
    )(q, k, k, v, v, bias, sinks, o, lse, do, dlse)


def _swa_prev_mask(head):
    del head
    return 15


SWA_HEADS_PER_STEP = 2
SWA_GROUP = H_SWA_Q // H_SWA_KV
N_BLK = SEQ // BLK


GROUP_W = H_PER_DIL * HEAD_DIM
DIL_COLUMNS = (768, 1920)
LANE_BLOCKS = (DIL_COLUMNS[1] - DIL_COLUMNS[0]) // GROUP_W
DIL_Q_BLOCK, DIL_K_BLOCK, DIL_V_BLOCK = 0, 3, 6
N_GROUPS = len(DIL_PATTERNS)


def dilated_views(qkv):
    cols = qkv[:, DIL_COLUMNS[0]:DIL_COLUMNS[1]]
    return [_dil_view(cols, d) for _, d in DIL_PATTERNS]


def _dil_view(t, d):
    return t.reshape(SEQ // d, d * t.shape[1])


def _dil_tile(n, d):
    per_class = N_BLK // d
    return n // per_class, n % per_class


def _dil_spec(d, lane_block, lane_blocks, shift=0):
    def index(n):
        r, m = _dil_tile(n, d)
        m = jnp.clip(m + shift, 0, N_BLK // d - 1)
        return m, r * lane_blocks + lane_block
    return pl.BlockSpec((BLK, GROUP_W), index)


def _two_heads(x, first):
    zero = jnp.zeros_like(x)
    return jnp.concatenate([jnp.where(first, x, zero), jnp.where(first, zero, x)], axis=0)


def _per_head(col, first):
    return jnp.where(first, col[0:BLK], col[BLK:2 * BLK])


def _head_rows(tile, first):
    pick = lambda keep: jnp.max(jnp.where(keep, tile, -jnp.inf), axis=1, keepdims=True)
    return jnp.concatenate([pick(first), pick(jnp.logical_not(first))], axis=0)


def _dil_scores(q_ref, kp_ref, ko_ref, bias, has_prev, first):
    q2 = _two_heads(q_ref[...] * QK_SCALE, first)
    k2 = jnp.concatenate([kp_ref[...], ko_ref[...]], axis=0)
    s = _dot(q2, k2, NT) + bias
    key = lax.broadcasted_iota(jnp.int32, s.shape, 1)
    return q2, k2, jnp.where(jnp.logical_or(has_prev, key >= BLK), s, NEG)


def dilated_forward(views, bias):
    def body(*refs):
        ins, bias_ref, outs = refs[:5 * N_GROUPS], refs[5 * N_GROUPS], refs[5 * N_GROUPS + 1:]
        n = pl.program_id(0)
        first = lax.broadcasted_iota(jnp.int32, (1, GROUP_W), 1) < HEAD_DIM
        for g, (_, d) in enumerate(DIL_PATTERNS):
            q_ref, kp_ref, ko_ref, vp_ref, vo_ref = ins[5 * g:5 * g + 5]
            has_prev = _dil_tile(n, d)[1] > 0
            _, _, s = _dil_scores(q_ref, kp_ref, ko_ref, bias_ref[g], has_prev, first)
            m = jnp.max(s, axis=1, keepdims=True)
            p = jnp.exp(s - m)
            denom = jnp.sum(p, axis=1, keepdims=True)
            v2 = jnp.concatenate([vp_ref[...], vo_ref[...]], axis=0)
            o2 = _dot(p.astype(BF16), v2, NN) / denom
            outs[2 * g][...] = _per_head(o2, first)
            outs[2 * g + 1][...] = _per_head(m + jnp.log(denom), first)

    operands, in_specs, out_specs, out_shape = [], [], [], []
    for g, (_, d) in enumerate(DIL_PATTERNS):
        operands += [views[g]] * 5
        in_specs += [_dil_spec(d, DIL_Q_BLOCK + g, LANE_BLOCKS), _dil_spec(d, DIL_K_BLOCK + g, LANE_BLOCKS, -1),
                     _dil_spec(d, DIL_K_BLOCK + g, LANE_BLOCKS), _dil_spec(d, DIL_V_BLOCK + g, LANE_BLOCKS, -1),
                     _dil_spec(d, DIL_V_BLOCK + g, LANE_BLOCKS)]
        out_specs += [_dil_spec(d, 0, 1)] * 2
        out_shape += [jax.ShapeDtypeStruct((SEQ // d, d * GROUP_W), F32)] * 2
    out = pl.pallas_call(
        body, grid=(N_BLK,), in_specs=in_specs + [pl.BlockSpec((N_GROUPS, 2 * BLK, 2 * BLK), lambda n: (0, 0, 0))],
        out_specs=out_specs, out_shape=out_shape, name="dilated_forward", compiler_params=_params(("parallel",)),
    )(*operands, bias)
    out = [t.reshape(SEQ, GROUP_W) for t in out]
    return out[0::2], out[1::2]


def _group_softmax(lses):
    m = jnp.maximum(jnp.maximum(lses[0], lses[1]), lses[2])
    e = [jnp.exp(l - m) for l in lses]
    total = e[0] + e[1] + e[2]
    return [t / total for t in e]


def dilated_merge(o, lse):
    def body(*refs):
        alpha = _group_softmax([r[...] for r in refs[N_GROUPS:2 * N_GROUPS]])
        refs[-1][...] = alpha[0] * refs[0][...] + alpha[1] * refs[1][...] + alpha[2] * refs[2][...]

    spec = pl.BlockSpec((ROW_TILE, GROUP_W), lambda i: (i, 0))
    return pl.pallas_call(
        body, grid=(SEQ // ROW_TILE,), in_specs=[spec] * (2 * N_GROUPS), out_specs=spec,
        out_shape=jax.ShapeDtypeStruct((SEQ, GROUP_W), F32), name="dilated_merge", compiler_params=_params(("parallel",)),
    )(*o, *lse)


def dilated_merge_bwd(o, lse, do_cat):
    def body(*refs):
        o_v = [r[...] for r in refs[:N_GROUPS]]
        alpha = _group_softmax([r[...] for r in refs[N_GROUPS:2 * N_GROUPS]])
        dout = refs[2 * N_GROUPS][...]
        outs = refs[2 * N_GROUPS + 1:]
        first = lax.broadcasted_iota(jnp.int32, (1, GROUP_W), 1) < HEAD_DIM

        def head_sum(x):
            a = jnp.sum(jnp.where(first, x, 0.0), axis=1, keepdims=True)
            b = jnp.sum(jnp.where(first, 0.0, x), axis=1, keepdims=True)
            return jnp.where(first, a, b)

        dalpha = [head_sum(dout * o_g) for o_g in o_v]
        mean = alpha[0] * dalpha[0] + alpha[1] * dalpha[1] + alpha[2] * dalpha[2]
        for g in range(N_GROUPS):
            outs[g][...] = alpha[g] * dout
            outs[N_GROUPS + g][...] = alpha[g] * (dalpha[g] - mean)

    spec = pl.BlockSpec((ROW_TILE, GROUP_W), lambda i: (i, 0))
    shape = jax.ShapeDtypeStruct((SEQ, GROUP_W), F32)
    out = pl.pallas_call(
        body, grid=(SEQ // ROW_TILE,), in_specs=[spec] * (2 * N_GROUPS) + [pl.BlockSpec((ROW_TILE, GROUP_W), lambda i: (i, 2))],
        out_specs=[spec] * (2 * N_GROUPS), out_shape=[shape] * (2 * N_GROUPS),
        name="dilated_merge_bwd", compiler_params=_params(("parallel",)),
    )(*o, *lse, do_cat)
    return out[:N_GROUPS], out[N_GROUPS:]


def dilated_backward(views, bias, o, lse, do, dlse):
    n_in = 9

    def body(*refs):
        ins, bias_ref = refs[:n_in * N_GROUPS], refs[n_in * N_GROUPS]
        outs, dbias_ref = refs[n_in * N_GROUPS + 1:-1], refs[-1]
        n = pl.program_id(0)

        @pl.when(n == 0)
        def _():
            dbias_ref[...] = jnp.zeros_like(dbias_ref)

        first = lax.broadcasted_iota(jnp.int32, (1, GROUP_W), 1) < HEAD_DIM
        for g, (_, d) in enumerate(DIL_PATTERNS):
            q_ref, kp_ref, ko_ref, vp_ref, vo_ref, o_ref, lse_ref, do_ref, dlse_ref = ins[n_in * g:n_in * (g + 1)]
            has_prev = _dil_tile(n, d)[1] > 0
            q2, k2, s = _dil_scores(q_ref, kp_ref, ko_ref, bias_ref[g], has_prev, first)
            dov = do_ref[...]
            do2 = _two_heads(dov.astype(BF16), first)
            prod = dov * o_ref[...]
            delta = jnp.concatenate([jnp.sum(jnp.where(first, prod, 0.0), axis=1, keepdims=True),
                                     jnp.sum(jnp.where(first, 0.0, prod), axis=1, keepdims=True)], axis=0)
            shift = _head_rows(dlse_ref[...], first) - delta
            p = jnp.exp(s - _head_rows(lse_ref[...], first))
            v2 = jnp.concatenate([vp_ref[...], vo_ref[...]], axis=0)
            ds = p * (_dot(do2, v2, NT) + shift)
            dbias_ref[g] += ds
            ds = ds.astype(BF16)
            dq2 = _dot(ds, k2, NN) * QK_SCALE
            dk2 = _dot(ds, q2, TN)
            dv2 = _dot(p.astype(BF16), do2, TN)
            base = 5 * g
            outs[base][...] = jnp.where(first, dq2[0:BLK], dq2[BLK:2 * BLK])
            outs[base + 1][...] = dk2[BLK:2 * BLK]
            outs[base + 2][...] = dk2[0:BLK]
            outs[base + 3][...] = dv2[BLK:2 * BLK]
            outs[base + 4][...] = dv2[0:BLK]

    operands, in_specs, out_specs, out_shape = [], [], [], []
    for g, (_, d) in enumerate(DIL_PATTERNS):
        own = _dil_spec(d, 0, 1)
        operands += [views[g]] * 5 + [_dil_view(t[g], d) for t in (o, lse, do, dlse)]
        in_specs += [_dil_spec(d, DIL_Q_BLOCK + g, LANE_BLOCKS), _dil_spec(d, DIL_K_BLOCK + g, LANE_BLOCKS, -1),
                     _dil_spec(d, DIL_K_BLOCK + g, LANE_BLOCKS), _dil_spec(d, DIL_V_BLOCK + g, LANE_BLOCKS, -1),
                     _dil_spec(d, DIL_V_BLOCK + g, LANE_BLOCKS)] + [own] * 4
        out_specs += [own] * 5
        out_shape += [jax.ShapeDtypeStruct((SEQ // d, d * GROUP_W), F32)] * 5
    tiles = pl.BlockSpec((N_GROUPS, 2 * BLK, 2 * BLK), lambda n: (0, 0, 0))
    out = pl.pallas_call(
        body, grid=(N_BLK,), in_specs=in_specs + [tiles], out_specs=out_specs + [tiles],
        out_shape=out_shape + [jax.ShapeDtypeStruct((N_GROUPS, 2 * BLK, 2 * BLK), F32)],
        name="dilated_backward", compiler_params=_params(("arbitrary",)),
    )(*operands, bias)
    return [out[5 * g:5 * g + 5] for g in range(N_GROUPS)], out[-1]


def dilated_key_grads(parts):
    def body(*refs):
        ins, outs = refs[:4 * N_GROUPS], refs[4 * N_GROUPS:]
        n = pl.program_id(0)
        for g, (_, d) in enumerate(DIL_PATTERNS):
            has_next = _dil_tile(n, d)[1] < N_BLK // d - 1
            own_k, next_k, own_v, next_v = ins[4 * g:4 * g + 4]
            outs[2 * g][...] = own_k[...] + jnp.where(has_next, next_k[...], 0.0)
            outs[2 * g + 1][...] = own_v[...] + jnp.where(has_next, next_v[...], 0.0)

    operands, in_specs, out_specs, out_shape = [], [], [], []
    for g, (_, d) in enumerate(DIL_PATTERNS):
        _, dk_own, dk_prev, dv_own, dv_prev = parts[g]
        operands += [dk_own, dk_prev, dv_own, dv_prev]
        in_specs += [_dil_spec(d, 0, 1), _dil_spec(d, 0, 1, 1)] * 2
        out_specs += [_dil_spec(d, 0, 1)] * 2
        out_shape += [jax.ShapeDtypeStruct((SEQ // d, d * GROUP_W), F32)] * 2
    out = pl.pallas_call(
        body, grid=(N_BLK,), in_specs=in_specs, out_specs=out_specs, out_shape=out_shape,
        name="dilated_key_grads", compiler_params=_params(("parallel",)),
    )(*operands)
    tok = lambda ts: jnp.concatenate([t.reshape(SEQ, GROUP_W) for t in ts], axis=1)
    return tok([parts[g][0] for g in range(N_GROUPS)]), tok(out[0::2]), tok(out[1::2])


def rel_bias_reduce(dbias0, dbias1, bucket):
    def body(d0_ref, d1_ref, b_ref, o_ref):
        dv, bv = d0_ref[...] + d1_ref[...], b_ref[...]
        lane = lax.broadcasted_iota(jnp.int32, (1, BLK), 1)
        acc = jnp.zeros((1, BLK), F32)
        for bkt in range(N_BUCKETS):
            acc = acc + jnp.where(lane == bkt, jnp.sum(jnp.where(bv == bkt, dv, 0.0)), 0.0)
        o_ref[...] = acc

    tile = pl.BlockSpec((None, BLK, 2 * BLK), lambda h: (h, 0, 0))
    return pl.pallas_call(
        body, grid=(dbias0.shape[0],), in_specs=[tile, tile, tile],
        out_specs=pl.BlockSpec((None, 1, BLK), lambda h: (h, 0, 0)),
        out_shape=jax.ShapeDtypeStruct((dbias0.shape[0], 1, BLK), F32),
        name="rel_bias_reduce", compiler_params=_params(("parallel",)),
    )(dbias0, dbias1, bucket)


def _heads(t):
    return t.reshape(SEQ, -1, HEAD_DIM).transpose(1, 0, 2)


def _unheads(t):
    return t.transpose(1, 0, 2).reshape(SEQ, -1)


def _t5_bucket(n):
    max_exact = N_BUCKETS // 2
    nf = jnp.maximum(n, 1).astype(F32)
    large = max_exact + (jnp.log(nf / max_exact) / math.log(MAX_REL_DIST / max_exact)
                         * (N_BUCKETS - max_exact)).astype(jnp.int32)
    large = jnp.minimum(large, N_BUCKETS - 1)
    return jnp.where(n < max_exact, n, large)


def band_tables(rel_bias):
    rel = jnp.arange(BLK)[:, None] + BLK - jnp.arange(2 * BLK)[None, :]
    buckets = []
    patterns = [(d, w // d) for w, d in DIL_PATTERNS for _ in range(H_PER_DIL)] + [(1, SWA_WINDOW - 1)] * H_SWA_Q
    for d, max_dist in patterns:
        band = (rel >= 0) & (rel <= max_dist)
        buckets.append(jnp.where(band, _t5_bucket(jnp.maximum(rel, 0) * d), -1))
    buckets = jnp.stack(buckets).astype(jnp.int32)

    def body(table_ref, b_ref, o_ref):
        h = pl.program_id(0)
        bv = b_ref[...]
        tile = jnp.full(bv.shape, NEG, F32)
        for bkt in range(N_BUCKETS):
            tile = jnp.where(bv == bkt, table_ref[h, bkt], tile)
        o_ref[...] = tile

    spec = pl.BlockSpec((None, BLK, 2 * BLK), lambda h: (h, 0, 0))
    tiles = pl.pallas_call(
        body, grid=(len(patterns),), in_specs=[pl.BlockSpec(memory_space=pltpu.SMEM), spec], out_specs=spec,
        out_shape=jax.ShapeDtypeStruct(buckets.shape, F32), name="band_tables", compiler_params=_params(("parallel",)),
    )(rel_bias.T, buckets)
    return tiles[:H_DIL], tiles[H_DIL:], buckets


def _swa_rows(t):
    t = t.reshape(N_BLK, BLK, H_SWA_KV, SWA_GROUP, HEAD_DIM).transpose(2, 0, 3, 1, 4)
    return t.reshape(H_SWA_KV, N_BLK, SWA_GROUP * BLK, HEAD_DIM)


def _swa_tokens(t):
    t = t.reshape(H_SWA_KV, N_BLK, SWA_GROUP, BLK, HEAD_DIM).transpose(1, 3, 0, 2, 4)
    return t.reshape(SEQ, H_SWA_Q * HEAD_DIM)


def _sink_rows(sinks):
    return jnp.broadcast_to(sinks.reshape(H_SWA_KV, SWA_GROUP, 1, 1), (H_SWA_KV, SWA_GROUP, BLK, 1)).reshape(
        H_SWA_KV, SWA_GROUP * BLK, 1)


def _vec(v):
    return v.reshape(1, D_MODEL)


class UnitRows:
    def __init__(self, u, mod_table, gain_table):
        self.shift, self.scale, self.gate = (Row(mod_table, 3 * u + t) for t in range(3))
        self.gain = Row(gain_table, u)


def ffn_forward(x, h, rows, then, w):
    a, b, s = ffn_up(h, w[0], w[1])
    f, xo, *h_next = ffn_down(s, w[2], x, rows.gate, then)
    return xo, (h_next or [None])[0], (x, h, a, b, s, f)


def ffn_backward(u, dxo, df, saved, rows, w, sums, below):
    x, h, a, b, s, _ = saved
    da, db = ffn_bwd_hidden(df, w[2], a, b)
    grads = ffn_grad_weights(h, s, df, da, db)
    dx, sums, *df_below = ffn_bwd_input(da, db, w[0], w[1], x, dxo, rows.gain, rows.scale, sums, u, below)
    return dx, sums, df_below, grads


def mixer_forward(x, h, rows, then, sinks, bias_dil, bias_swa, w):
    proj, qkv = in_proj(h, w[0])
    q_swa, k_swa, v_swa = _swa_rows(qkv[:, 1920:2304] * QK_SCALE), _heads(qkv[:, 2304:2432]), _heads(qkv[:, 2432:2560])
    o_sb, total_sb = sb_forward(qkv)
    bias_dil = bias_dil.reshape(N_GROUPS, 2 * BLK, 2 * BLK)
    views = dilated_views(qkv)
    o_groups, lse_groups = dilated_forward(views, bias_dil)
    o_dil = dilated_merge(o_groups, lse_groups)
    bias_swa = bias_swa.reshape(H_SWA_KV, SWA_GROUP * BLK, 2 * BLK)
    o_swa, lse_swa = banded_forward("swa_forward", q_swa, k_swa, v_swa, bias_swa, _sink_rows(sinks), SWA_HEADS_PER_STEP,
                                    _swa_prev_mask)
    o_cat = jnp.concatenate([o_sb, o_dil, _swa_tokens(o_swa)], axis=1)
    merged = merge_branches(o_cat, w[1], proj)
    mo, xo, *h_next = out_proj(merged, w[2], x, rows.gate, then)
    saved = (x, h, proj, (qkv, total_sb), (views, o_groups, lse_groups),
             (q_swa, k_swa, v_swa, o_swa, lse_swa), o_cat, merged, mo)
    return xo, (h_next or [None])[0], saved


def mixer_backward(u, dxo, dmo, saved, rows, sinks, bias_dil, bias_swa, w, sums, below):
    x, h, proj, sb, dil, swa, o_cat, merged, _ = saved
    tok = pl.BlockSpec((MM_TILE, D_MODEL), lambda j, k: (k, 0))
    g_out = grad_weight("grad_w_out", merged, pl.BlockSpec((MM_TILE, D_SHARD), lambda j, k: (k, j)), dmo, tok,
                        (D_SHARD, D_MODEL))
    du0, du1, du2, dg0, dg1, dg2 = merge_bwd(dmo, w[2], o_cat, w[1], proj)
    du = (du0, du1, du2)
    do_cat = branch_bwd_input(du, w[1])
    g_br = branch_grad_weights(o_cat, du)

    qkv, total_sb = sb
    dq_sb, dk_sb, dv_sb = sb_backward(qkv, total_sb, do_cat)

    views, o_groups, lse_groups = dil
    bias_dil = bias_dil.reshape(N_GROUPS, 2 * BLK, 2 * BLK)
    do_groups, dlse_groups = dilated_merge_bwd(o_groups, lse_groups, do_cat)
    parts, dbias_dil = dilated_backward(views, bias_dil, o_groups, lse_groups, do_groups, dlse_groups)
    dq_dil, dk_dil, dv_dil = dilated_key_grads(parts)
    dbias_dil = dbias_dil.reshape(H_DIL, BLK, 2 * BLK)

    q_swa, k_swa, v_swa, o_swa, lse_swa = swa
    bias_swa = bias_swa.reshape(H_SWA_KV, SWA_GROUP * BLK, 2 * BLK)
    dq_swa, dk_swa, dv_swa, dbias_swa, dsinks = banded_backward(
        "swa_backward", q_swa, k_swa, v_swa, bias_swa, _sink_rows(sinks), o_swa, lse_swa, _swa_rows(do_cat[:, 384:768]),
        jnp.zeros_like(lse_swa), SWA_HEADS_PER_STEP, _swa_prev_mask)
    dbias_swa = dbias_swa.reshape(H_SWA_Q, BLK, 2 * BLK)

    dproj = jnp.concatenate(
        [dq_sb, dk_sb, dv_sb, dq_dil, dk_dil, dv_dil, _swa_tokens(dq_swa), _unheads(dk_swa), _unheads(dv_swa)],
        axis=1).astype(BF16)
    dproj = jnp.concatenate([dproj, dg0, dg1, dg2], axis=1)
    g_in = grad_weight("grad_w_in", h, tok, dproj, pl.BlockSpec((MM_TILE, IN_SHARD), lambda j, k: (k, j)),
                       (D_MODEL, IN_SHARD))
    dx, sums, *df_below = mixer_bwd_input(dproj, w[0], x, dxo, rows.gain, rows.scale, sums, u, below)
    dbias = jnp.concatenate([dbias_dil, dbias_swa], axis=0)
    return dx, sums, df_below, dbias, dsinks[:, :, 0].reshape(H_SWA_Q), (g_in, g_br, g_out)


N_UNITS = 3 * DEPTH


def device_step(x, target, mod, gains, final_gain, sinks, rel_bias, get_weights, put_grads):
    bias_dil, bias_swa, bucket = band_tables(rel_bias)
    mod_table = mod.reshape(3 * N_UNITS, 1, D_MODEL)
    gain_table = gains.reshape(N_UNITS, 1, D_MODEL)
    saved, weights = [], []
    units = [UnitRows(u, mod_table, gain_table) for u in range(N_UNITS)]
    h = prenorm(x, units[0].gain, units[0].scale, units[0].shift)
    for u in range(N_UNITS):
        l, j = divmod(u, 3)
        w = get_weights(u, x)
        then = units[u + 1] if u + 1 < N_UNITS else None
        if j == 1:
            x, h, s = mixer_forward(x, h, units[u], then, sinks[l], bias_dil, bias_swa, w)
        else:
            x, h, s = ffn_forward(x, h, units[u], then, w)
        saved.append(s)
        weights.append(w)
    loss, dx, dfinal = final_loss(x, _vec(final_gain), target)

    sums = (lax.empty((8 * N_UNITS, D_MODEL), F32), lax.empty((8 * N_UNITS, D_MODEL), F32))
    dbias, dsinks = [None] * DEPTH, [None] * DEPTH
    zero = jnp.zeros((1, 1), F32)
    top = N_UNITS - 1
    df, gate_sums = resid_bwd(dx, saved[top][-1], units[top].gate, 0.5, sums[1], top)
    sums = (sums[0], gate_sums)
    for u in reversed(range(N_UNITS)):
        l, j = divmod(u, 3)
        rows = UnitRows(u, mod_table, gain_table + zero)
        below = (saved[u - 1][-1], units[u - 1].gate, 1.0 if (u - 1) % 3 == 1 else 0.5) if u > 0 else None
        if j == 1:
            dx, sums, df, dbias[l], dsinks[l], grads = mixer_backward(
                u, dx, df, saved[u], rows, sinks[l], bias_dil, bias_swa, weights[u], sums, below)
        else:
            dx, sums, df, grads = ffn_backward(u, dx, df, saved[u], rows, weights[u], sums, below)
        df = df[0] if df else None
        if u > 0:
            zero = put_grads(u, grads)
    drel = rel_bias_reduce(dbias[0], dbias[1], bucket)[:, 0, :N_BUCKETS].T
    norm_sums, gate_sums = (t.reshape(DEPTH, 3, 8, D_MODEL) for t in sums)
    dmod = jnp.stack([norm_sums[:, :, 0], norm_sums[:, :, 1], gate_sums[:, :, 0]], axis=2)
    return loss, dx, dmod, norm_sums[:, :, 2], dfinal[0], jnp.stack(dsinks), drel, grads


MESH = pl.DeviceIdType.MESH
CHIP_FLIPS = ((1, 0), (0, 1), (1, 1))
ANY = pl.BlockSpec(memory_space=pl.ANY)


def _position():
    return lax.axis_index("x"), lax.axis_index("y"), lax.axis_index("c")


def all_gather_small(name, piece):
    def body(x_ref, out_ref, send_sems, recv_sems, local_sem):
        x, y, c = _position()
        me, sibling = (x, y, c), (x, y, 1 - c)
        chips = [(x ^ fx, y ^ fy) for fx, fy in CHIP_FLIPS]

        def rows(px, py, pc):
            return out_ref.at[4 * px + 2 * py + pc]

        def copy(k, block, to, src=None):
            return pltpu.make_async_remote_copy(
                src_ref=rows(*block) if src is None else src, dst_ref=rows(*block),
                send_sem=send_sems.at[k], recv_sem=recv_sems.at[k], device_id=to, device_id_type=MESH)

        mine = pltpu.make_async_copy(x_ref, rows(*me), local_sem)
        mine.start()
        first = [copy(0, me, sibling, src=x_ref)]
        first += [copy(1 + j, me, (*chip, c), src=x_ref) for j, chip in enumerate(chips)]
        for cp in first:
            cp.start()
        passed = [copy(4 + j, (*chip, c), sibling) for j, chip in enumerate(chips)]
        for j, chip in enumerate(chips):
            copy(1 + j, (*chip, c), me).wait_recv()
            passed[j].start()
        copy(0, sibling, me).wait_recv()
        for j, chip in enumerate(chips):
            copy(4 + j, (*chip, 1 - c), me).wait_recv()
        for cp in first + passed:
            cp.wait_send()
        mine.wait()

    return pl.pallas_call(
        body, out_shape=jax.ShapeDtypeStruct((N_DEV,) + piece.shape, piece.dtype),
        in_specs=[pl.BlockSpec(memory_space=pltpu.VMEM)], out_specs=pl.BlockSpec(memory_space=pltpu.VMEM),
        scratch_shapes=[pltpu.SemaphoreType.DMA((7,)), pltpu.SemaphoreType.DMA((7,)), pltpu.SemaphoreType.DMA],
        name=name,
    )(piece)


def exchange(name, operands, out_shapes, aliases, plan):
    n_in, n_out = len(operands), len(out_shapes)

    def body(*refs):
        ins, outs = refs[:n_in], refs[n_in:n_in + n_out]
        send_sems, recv_sems, local_sems = refs[n_in + n_out:]
        x, y, c = _position()
        local, sends, recvs = plan(ins, outs, x, y, c)
        local = [pltpu.make_async_copy(s, d, local_sems.at[k]) for k, (s, d) in enumerate(local)]
        for cp in local:
            cp.start()
        remote = [pltpu.make_async_remote_copy(src_ref=s, dst_ref=d, send_sem=send_sems.at[k], recv_sem=recv_sems.at[k],
                                               device_id=dev, device_id_type=MESH)
                  for k, (s, d, dev) in enumerate(sends)]
        for cp in remote:
            cp.start()
        for k, r in enumerate(recvs):
            pltpu.make_async_remote_copy(src_ref=r, dst_ref=r, send_sem=send_sems.at[k], recv_sem=recv_sems.at[k],
                                         device_id=(x, y, c), device_id_type=MESH).wait_recv()
        for cp in remote:
            cp.wait_send()
        for cp in local:
            cp.wait()

    n_sends, n_local = plan.n_sends, max(plan.n_local, 1)
    return pl.pallas_call(
        body, out_shape=out_shapes, in_specs=[ANY] * n_in, out_specs=[ANY] * n_out,
        scratch_shapes=[pltpu.SemaphoreType.DMA((n_sends,)), pltpu.SemaphoreType.DMA((n_sends,)),
                        pltpu.SemaphoreType.DMA((n_local,))],
        input_output_aliases=aliases, name=name,
    )(*operands)


def _plan(n_local, n_sends):
    def wrap(fn):
        fn.n_local, fn.n_sends = n_local, n_sends
        return fn
    return wrap


def _half(ref, axis, c):
    rows = ref.shape[axis] // 2
    idx = [slice(None)] * len(ref.shape)
    idx[axis] = pl.ds(pl.multiple_of(c * rows, 16), rows)
    return ref.at[tuple(idx)]


HBM = pl.BlockSpec(memory_space=pltpu.HBM)
SEM = pl.BlockSpec(memory_space=pltpu.SEMAPHORE)
EFFECT = pltpu.SideEffectType.DATAFLOW_SIDE_EFFECTING


def split_start(name, bufs, extra, n_copies, describe):
    n = len(bufs)

    def body(*refs):
        send_sems, recv_sems = refs[n + len(extra)], refs[n + len(extra) + 1]
        x, y, c = _position()
        for k, (src, dst, _, peer) in enumerate(describe(refs[:n], x, y, c)):
            pltpu.make_async_remote_copy(src_ref=src, dst_ref=dst, send_sem=send_sems.at[k], recv_sem=recv_sems.at[k],
                                         device_id=peer, device_id_type=MESH).start()
        token = refs[-1]
        token[...] = jnp.zeros_like(token)

    out = pl.pallas_call(
        body, name=name,
        out_shape=(pltpu.SemaphoreType.DMA((n_copies,)), pltpu.SemaphoreType.DMA((n_copies,)),
                   *[pltpu.HBM(b.shape, b.dtype) for b in bufs], jax.ShapeDtypeStruct((8, 128), F32)),
        in_specs=[HBM] * n + [ANY] * len(extra),
        out_specs=(SEM, SEM, *[HBM] * n, pl.BlockSpec(memory_space=pltpu.VMEM)),
        input_output_aliases={k: 2 + k for k in range(n)},
        compiler_params=pltpu.CompilerParams(has_side_effects=EFFECT),
    )(*[pltpu.with_memory_space_constraint(b, pltpu.HBM) for b in bufs], *extra)
    return out[0], out[1], list(out[2:2 + n]), out[-1]


def split_wait(name, bufs, send_sems, recv_sems, after, describe):
    n = len(bufs)

    def body(*refs):
        send, recv = refs[n], refs[n + 1]
        x, y, c = _position()
        for k, (src, _, dst, peer) in enumerate(describe(refs[:n], x, y, c)):
            copy = pltpu.make_async_remote_copy(src_ref=src, dst_ref=dst, send_sem=send.at[k], recv_sem=recv.at[k],
                                                device_id=peer, device_id_type=MESH)
            copy.wait_send()
            copy.wait_recv()

    out = pl.pallas_call(
        body, name=name, out_shape=[pltpu.HBM(b.shape, b.dtype) for b in bufs],
        in_specs=[HBM] * n + [SEM, SEM] + [ANY] * len(after), out_specs=[HBM] * n,
        input_output_aliases={k: k for k in range(n)},
        compiler_params=pltpu.CompilerParams(has_side_effects=EFFECT),
    )(*bufs, send_sems, recv_sems, *after)
    return list(out)


def _row_tile(rows, cols, max_elements=256 * 1024):
    best = 16
    for t in range(16, rows + 1, 16):
        if rows % t == 0 and t * cols <= max_elements:
            best = t
    return best


def cast_into_slot(name, param, index, chip):
    rows, cols = param.shape[-2:]
    tr = _row_tile(rows, cols)
    lead = (None,) * len(index)

    def body(chip_ref, s_ref, o_ref):
        del chip_ref
        o_ref[...] = s_ref[...].astype(BF16)

    return pl.pallas_call(
        body, out_shape=jax.ShapeDtypeStruct((N_CHIPS, rows, cols), BF16),
        grid_spec=pltpu.PrefetchScalarGridSpec(
            num_scalar_prefetch=1, grid=(rows // tr,),
            in_specs=[pl.BlockSpec(lead + (tr, cols), lambda r, chip_ref: index + (r, 0))],
            out_specs=pl.BlockSpec((None, tr, cols), lambda r, chip_ref: (chip_ref[0], r, 0))),
        name=name, compiler_params=_params(("parallel",)),
    )(chip, param)


GATHER_STAGES = ((0,), (1,), (2,), (3, 4, 5))
REDUCE_STAGES = ((5, 4, 3), (2,), (1,), (0,))


def _gather_copies(slots, x, y, c):
    me = 2 * x + y
    out = []
    for s in slots:
        for fx, fy in CHIP_FLIPS:
            mine = _half(s.at[me], 0, c)
            out.append((mine, mine, _half(s.at[2 * (x ^ fx) + (y ^ fy)], 0, c), (x ^ fx, y ^ fy, c)))
    return out


class WeightStream:
    def __init__(self, shards, chip, after=()):
        self.pending, self.ready = {}, {}
        token = tuple(after)
        for si, units in enumerate(GATHER_STAGES):
            slots = [cast_into_slot(f"cast_{u}_{t}", p, idx, chip) for u in units for t, (p, idx) in enumerate(shards[u])]
            send, recv, slots, tok = split_start(f"gather_start_{si}", slots, token, 3 * len(slots), _gather_copies)
            self.pending[si] = (send, recv, slots)
            token = (tok,)
        self.token = token

    def get(self, u, after):
        if u not in self.ready:
            si = next(k for k, units in enumerate(GATHER_STAGES) if u in units)
            send, recv, slots = self.pending.pop(si)
            slots = split_wait(f"gather_wait_{si}", slots, send, recv, (after,) + self.token, _gather_copies)
            self.token = ()

            @_plan(0, 3 * len(slots))
            def to_sibling(ins, outs, x, y, c):
                sends, recvs = [], []
                for o in outs:
                    for fx, fy in CHIP_FLIPS:
                        slab = o.at[2 * (x ^ fx) + (y ^ fy)]
                        sends.append((_half(slab, 0, c), _half(slab, 0, c), (x, y, 1 - c)))
                        recvs.append(_half(slab, 0, 1 - c))
                return [], sends, recvs

            shapes = [jax.ShapeDtypeStruct(s.shape, BF16) for s in slots]
            slots = exchange(f"gather_sibling_{si}", slots, shapes, {k: k for k in range(len(slots))}, to_sibling)
            for i, v in enumerate(GATHER_STAGES[si]):
                self.ready[v] = tuple(slots[3 * i:3 * i + 3])
        return self.ready[u]


def _reduce_copies(bufs, x, y, c):
    n = len(bufs) // 2
    out = []
    for s, land in zip(bufs[:n], bufs[n:]):
        for k, (fx, fy) in enumerate(CHIP_FLIPS):
            out.append((s.at[2 * (x ^ fx) + (y ^ fy)], land.at[k], land.at[k], (x ^ fx, y ^ fy, c)))
    return out


GRAD_SLOTS = {"gate": (2 * DEPTH, FF_SHARD, D_MODEL), "up": (2 * DEPTH, FF_SHARD, D_MODEL),
              "down": (2 * DEPTH, FF_SHARD, D_MODEL), "in": (DEPTH, D_MODEL, IN_SHARD),
              "br": (DEPTH, BR_ROWS, D_SHARD), "out": (DEPTH, D_SHARD, D_MODEL)}


def _unit_tensors(u):
    l, j = divmod(u, 3)
    if j == 1:
        return [("in", l), ("br", l), ("out", l)]
    return [(k, 2 * l + j // 2) for k in ("gate", "up", "down")]


class GradStream:
    def __init__(self, chip, core):
        self.core = core
        self.place = jnp.concatenate([chip, core])
        self.held, self.flying = {}, []
        self.full = {k: lax.empty(shape, F32) for k, shape in GRAD_SLOTS.items()}

    def put(self, u, grads, after=()):
        self.held[u] = grads
        si = len(self.flying)
        units = REDUCE_STAGES[si]
        if not all(v in self.held for v in units):
            return jnp.zeros((1, 1), F32)
        gs = [g for v in units for g in self.held[v]]

        @_plan(0, len(gs))
        def swap_halves(ins, outs, x, y, c):
            sends = [(_half(g, 1, 1 - c), o, (x, y, 1 - c)) for g, o in zip(ins, outs)]
            return [], sends, list(outs)

        half_shapes = [jax.ShapeDtypeStruct((N_CHIPS, g.shape[1] // 2, g.shape[2]), BF16) for g in gs]
        landed = exchange(f"reduce_swap_{si}", gs + list(after), half_shapes, {}, swap_halves)
        sums = [None] * len(gs)
        for run in _same_shape_runs(gs):
            for k, s in zip(run, _add_halves([gs[k] for k in run], [landed[k] for k in run], self.core)):
                sums[k] = s
        landing = [lax.empty((3,) + s.shape[1:], BF16) for s in sums]
        send, recv, bufs, token = split_start(f"reduce_start_{si}", sums + landing, (), 3 * len(sums), _reduce_copies)
        self.flying.append((send, recv, bufs, [t for v in units for t in _unit_tensors(v)]))
        return token[0:1, 0:1]

    def finish(self, after):
        for si, (send, recv, bufs, tensors) in enumerate(self.flying):
            bufs = split_wait(f"reduce_wait_{si}", bufs, send, recv, tuple(after), _reduce_copies)
            n = len(tensors)
            for run in _same_shape_runs(bufs[:n]):
                names = [tensors[k][0] for k in run]
                out = _add_chips([bufs[k] for k in run], [bufs[n + k] for k in run], self.place,
                                 [self.full[t] for t in names], [tensors[k][1] for k in run])
                self.full.update(zip(names, out))
        names = list(self.full)

        @_plan(0, len(names))
        def share_halves(ins, outs, x, y, c):
            sends = [(_half(o, 1, c), _half(o, 1, c), (x, y, 1 - c)) for o in outs]
            return [], sends, [_half(o, 1, 1 - c) for o in outs]

        shapes = [jax.ShapeDtypeStruct(self.full[k].shape, F32) for k in names]
        out = exchange("reduce_share_halves", [self.full[k] for k in names], shapes, {k: k for k in range(len(names))},
                       share_halves)
        return dict(zip(names, out))


def _same_shape_runs(arrays, longest=3):
    runs = []
    for k, a in enumerate(arrays):
        if runs and len(runs[-1]) < longest and arrays[runs[-1][0]].shape == a.shape:
            runs[-1].append(k)
        else:
            runs.append([k])
    return runs


def _add_halves(gs, landeds, core):
    n = len(gs)
    _, rh, cols = landeds[0].shape
    tr = _row_tile(rh, cols, 1024 * 1024)
    per_half = rh // tr

    def body(core_ref, *refs):
        del core_ref
        for k in range(n):
            refs[2 * n + k][...] = (refs[k][...].astype(F32) + refs[n + k][...].astype(F32)).astype(BF16)

    blk = (None, tr, cols)
    landed_spec = pl.BlockSpec(blk, lambda j, r, core_ref: (j, r, 0))
    return pl.pallas_call(
        body, out_shape=[jax.ShapeDtypeStruct(landeds[0].shape, BF16)] * n,
        grid_spec=pltpu.PrefetchScalarGridSpec(
            num_scalar_prefetch=1, grid=(N_CHIPS, per_half),
            in_specs=[pl.BlockSpec(blk, lambda j, r, core_ref: (j, core_ref[0] * per_half + r, 0))] * n
            + [landed_spec] * n,
            out_specs=[landed_spec] * n),
        name="reduce_add_halves", compiler_params=_params(("parallel", "parallel")),
    )(core, *gs, *landeds)


def _add_chips(sums, landeds, place, fulls, slots):
    n = len(sums)
    _, rh, cols = sums[0].shape
    tr = _row_tile(rh, cols, 1024 * 1024)
    per_half = rh // tr

    def body(place_ref, *refs):
        del place_ref
        for k in range(n):
            s_ref, la_ref, o_ref = refs[k], refs[n + k], refs[3 * n + k]
            o_ref[...] = ((s_ref[...].astype(F32) + la_ref[0].astype(F32)) + la_ref[1].astype(F32)) + la_ref[2].astype(F32)

    def out_spec(slot):
        return pl.BlockSpec((None, tr, cols), lambda r, place_ref: (slot, place_ref[1] * per_half + r, 0))

    return pl.pallas_call(
        body, out_shape=[jax.ShapeDtypeStruct(f.shape, F32) for f in fulls],
        grid_spec=pltpu.PrefetchScalarGridSpec(
            num_scalar_prefetch=1, grid=(per_half,),
            in_specs=[pl.BlockSpec((None, tr, cols), lambda r, place_ref: (place_ref[0], r, 0))] * n
            + [pl.BlockSpec((3, tr, cols), lambda r, place_ref: (0, r, 0))] * n + [ANY] * n,
            out_specs=[out_spec(slot) for slot in slots]),
        input_output_aliases={1 + 2 * n + k: k for k in range(n)}, name="reduce_add_chips",
        compiler_params=_params(("parallel",)),
    )(place, *sums, *landeds, *fulls)


def sum_devices(parts):
    def body(p_ref, o_ref):
        acc = p_ref[0]
        for d in range(1, N_DEV):
            acc = acc + p_ref[d]
        o_ref[...] = acc

    return pl.pallas_call(body, out_shape=jax.ShapeDtypeStruct(parts.shape[1:], F32), name="sum_devices")(parts)


ADA_SHARD = 9 * D_MODEL // N_CHIPS
ADA_TILE = 768
ADA_ROWS = 16


def ada_forward(c_rows, w_ada, b_shard):
    def body(c_ref, w_ref, b_ref, o_ref):
        cv = c_ref[...]
        o_ref[...] = _dot((cv * _sigmoid(cv)).astype(BF16), w_ref[...].astype(BF16), NN) + b_ref[...]

    return pl.pallas_call(
        body, grid=(DEPTH, ADA_SHARD // ADA_TILE),
        in_specs=[pl.BlockSpec((ADA_ROWS, D_MODEL), lambda l, n: (0, 0)),
                  pl.BlockSpec((None, D_MODEL, ADA_TILE), lambda l, n: (l, 0, n)),
                  pl.BlockSpec((None, 1, ADA_TILE), lambda l, n: (l, 0, n))],
        out_specs=pl.BlockSpec((None, ADA_ROWS, ADA_TILE), lambda l, n: (l, 0, n)),
        out_shape=jax.ShapeDtypeStruct((DEPTH, ADA_ROWS, ADA_SHARD), F32),
        name="ada_forward", compiler_params=_params(("parallel", "parallel")),
    )(c_rows, w_ada, b_shard)


def ada_backward(c_rows, dmod_rows):
    def body(c_ref, d_ref, o_ref):
        cv = c_ref[...]
        o_ref[...] = _dot((cv * _sigmoid(cv)).astype(BF16), d_ref[...].astype(BF16), TN)

    return pl.pallas_call(
        body, grid=(DEPTH, ADA_SHARD // ADA_TILE),
        in_specs=[pl.BlockSpec((ADA_ROWS, D_MODEL), lambda l, n: (0, 0)),
                  pl.BlockSpec((None, ADA_ROWS, ADA_TILE), lambda l, n: (l, 0, n))],
        out_specs=pl.BlockSpec((None, D_MODEL, ADA_TILE), lambda l, n: (l, 0, n)),
        out_shape=jax.ShapeDtypeStruct((DEPTH, D_MODEL, ADA_SHARD), F32),
        name="ada_backward", compiler_params=_params(("parallel", "parallel")),
    )(c_rows, dmod_rows)


def adamw(name, w, g, m, v):
    shape = w.shape
    cols = shape[-1]
    rows = w.size // cols
    tr = _row_tile(rows, cols) if rows % 16 == 0 else rows
    c1 = 1.0 / (1.0 - ADAM_B1 ** ADAM_STEP)
    c2 = 1.0 / (1.0 - ADAM_B2 ** ADAM_STEP)

    def body(w_ref, g_ref, m_ref, v_ref, go_ref, d_ref, mo_ref, vo_ref):
        gv = g_ref[...]
        mn = ADAM_B1 * m_ref[...] + (1.0 - ADAM_B1) * gv
        vn = ADAM_B2 * v_ref[...] + (1.0 - ADAM_B2) * (gv * gv)
        go_ref[...] = gv
        mo_ref[...] = mn
        vo_ref[...] = vn
        d_ref[...] = -ADAM_LR * ((mn * c1) / (jnp.sqrt(vn * c2) + ADAM_EPS) + ADAM_WD * w_ref[...])

    spec = pl.BlockSpec((tr, cols), lambda i: (i, 0))
    out = jax.ShapeDtypeStruct((rows, cols), F32)
    res = pl.pallas_call(
        body, grid=(rows // tr,), in_specs=[spec] * 4, out_specs=[spec] * 4, out_shape=[out] * 4,
        name=name, compiler_params=_params(("parallel",)),
    )(*[t.reshape(rows, cols) for t in (w, g, m, v)])
    return tuple(r.reshape(shape) for r in res)


def _pack(parts, rows):
    flat = jnp.concatenate([p.reshape(-1) for p in parts])
    return jnp.pad(flat, (0, rows * 128 - flat.size)).reshape(rows, 128)


def _unpack(flat, shapes):
    out, at = [], 0
    for s in shapes:
        n = math.prod(s)
        out.append(flat[at:at + n].reshape(s))
        at += n
    return out


def kernel(x, c, w_ada, b_ada, norm_gain, w_ffn_gate, w_ffn_up, w_ffn_down, w_in, w_br_sb, w_br_dil, w_br_swa, w_out, sinks, rel_bias, final_gain, loss_target, m_w_ada, m_b_ada, m_norm_gain, m_w_ffn_gate, m_w_ffn_up, m_w_ffn_down, m_w_in, m_w_br_sb, m_w_br_dil, m_w_br_swa, m_w_out, m_sinks, m_rel_bias, m_final_gain, v_w_ada, v_b_ada, v_norm_gain, v_w_ffn_gate, v_w_ffn_up, v_w_ffn_down, v_w_in, v_w_br_sb, v_w_br_dil, v_w_br_swa, v_w_out, v_sinks, v_rel_bias, v_final_gain):
    xi, yi, ci = _position()
    chip = 2 * xi + yi
    dev = 2 * chip + ci

    c_all = all_gather_small("gather_c", c.reshape(8, 128)).reshape(N_DEV, D_MODEL)
    c_rows = jnp.pad(c_all, ((0, ADA_ROWS - N_DEV), (0, 0)))
    b_shard = lax.dynamic_slice_in_dim(b_ada, chip * ADA_SHARD, ADA_SHARD, axis=1).reshape(DEPTH, 1, ADA_SHARD)
    mod_shard = ada_forward(c_rows, w_ada, b_shard)[:, :N_DEV]
    n_mod = DEPTH * N_DEV * ADA_SHARD
    gathered = all_gather_small("gather_mod", _pack([mod_shard, norm_gain], 304))[::2].reshape(N_CHIPS, -1)
    mod_all = gathered[:, :n_mod].reshape(N_CHIPS, DEPTH, N_DEV, ADA_SHARD)
    mod = lax.dynamic_index_in_dim(mod_all, dev, axis=2, keepdims=False)
    mod = mod.transpose(1, 0, 2).reshape(DEPTH, 3, 3, D_MODEL)
    gains = gathered[:, n_mod:n_mod + DEPTH * 3 * D_SHARD].reshape(N_CHIPS, DEPTH, 3, D_SHARD)
    gains = gains.transpose(1, 2, 0, 3).reshape(DEPTH, 3, D_MODEL)

    chip_i, core_i = chip.astype(jnp.int32).reshape(1), ci.astype(jnp.int32).reshape(1)
    w_br = jnp.concatenate([w_br_sb, w_br_dil, w_br_swa], axis=1)
    transposed = (3, 4)
    w_gate_t, w_up_t = jnp.swapaxes(w_ffn_gate, 2, 3), jnp.swapaxes(w_ffn_up, 2, 3)
    shards = []
    for l in range(DEPTH):
        ffn = [[(w_gate_t, (l, f)), (w_up_t, (l, f)), (w_ffn_down, (l, f))] for f in range(2)]
        shards += [ffn[0], [(w_in, (l,)), (w_br, (l,)), (w_out, (l,))], ffn[1]]
    weights_in = WeightStream(shards, chip_i, (gathered,))
    grads_out = GradStream(chip_i, core_i)

    loss, dx, dmod, dgains, dfinal, dsinks, drel, last_grads = device_step(
        x[0], loss_target[0], mod, gains, final_gain, sinks, rel_bias, weights_in.get, grads_out.put)

    small_shapes = [(DEPTH, 9 * D_MODEL), (DEPTH, 3, D_MODEL), (D_MODEL,), (DEPTH, H_SWA_Q), (N_BUCKETS, 12), (1,)]
    small_all = all_gather_small("gather_small_grads", _pack([dmod, dgains, dfinal, dsinks, drel, loss[0, 0:1]], 208))
    started = grads_out.put(0, last_grads, after=(small_all,))
    small_all = small_all + started
    g_b_ada, g_gain_full, g_final, g_sinks, g_rel, loss_sum = _unpack(sum_devices(small_all).reshape(-1), small_shapes)
    g_gain = lax.dynamic_slice_in_dim(g_gain_full, chip * D_SHARD, D_SHARD, axis=2)
    dmod_all = small_all.reshape(N_DEV, -1)[:, :DEPTH * 9 * D_MODEL].reshape(N_DEV, DEPTH, 9 * D_MODEL)
    dmod_rows = lax.dynamic_slice_in_dim(dmod_all, chip * ADA_SHARD, ADA_SHARD, axis=2).transpose(1, 0, 2)
    g_w_ada = ada_backward(c_rows, jnp.pad(dmod_rows, ((0, 0), (0, ADA_ROWS - N_DEV), (0, 0))))

    weights = [w_ada, b_ada, norm_gain, w_ffn_gate, w_ffn_up, w_ffn_down, w_in, w_br_sb, w_br_dil, w_br_swa, w_out,
               sinks, rel_bias, final_gain]
    ms = [m_w_ada, m_b_ada, m_norm_gain, m_w_ffn_gate, m_w_ffn_up, m_w_ffn_down, m_w_in, m_w_br_sb, m_w_br_dil,
          m_w_br_swa, m_w_out, m_sinks, m_rel_bias, m_final_gain]
    vs = [v_w_ada, v_b_ada, v_norm_gain, v_w_ffn_gate, v_w_ffn_up, v_w_ffn_down, v_w_in, v_w_br_sb, v_w_br_dil,
          v_w_br_swa, v_w_out, v_sinks, v_rel_bias, v_final_gain]
    grads = [g_w_ada, g_b_ada, g_gain] + [None] * 8 + [g_sinks, g_rel, g_final]

    deltas, new_ms, new_vs = [None] * 14, [None] * 14, [None] * 14
    for k in (0, 1, 2, 11, 12, 13):
        _, deltas[k], new_ms[k], new_vs[k] = adamw(f"adamw_{k}", weights[k], grads[k], ms[k], vs[k])

    g = grads_out.finish((dx, deltas[0], deltas[1]))
    g_br = g["br"]
    grads[3:11] = [g["gate"].reshape(w_gate_t.shape), g["up"].reshape(w_up_t.shape),
                   g["down"].reshape(w_ffn_down.shape), g["in"], g_br[:, 0:256], g_br[:, 256:384], g_br[:, 384:768],
                   g["out"]]
    for k in range(3, 11):
        state = [weights[k], ms[k], vs[k]]
        if k in transposed:
            state = [jnp.swapaxes(t, 2, 3) for t in state]
        out = adamw(f"adamw_{k}", state[0], grads[k], state[1], state[2])
        if k in transposed:
            out = [jnp.swapaxes(t, 2, 3) for t in out]
        grads[k], deltas[k], new_ms[k], new_vs[k] = out
    return (loss_sum[0], dx[None], *grads, *deltas, *new_ms, *new_vs)
```

```python
import functools
import math

import jax
import jax.numpy as jnp
from jax import lax
from jax.experimental import pallas as pl
from jax.experimental.pallas import tpu as pltpu

F32 = jnp.float32
BF16 = jnp.bfloat16

D_MODEL = 1024
SEQ = 2048
DEPTH = 2
HEAD_DIM = 64
BLK = 128
H_SB = 4
DIL_PATTERNS = ((128, 1), (512, 4), (2048, 16))
H_PER_DIL = 2
H_DIL = 6
H_SWA_Q = 6
H_SWA_KV = 2
SWA_WINDOW = 128
N_BUCKETS = 32
MAX_REL_DIST = 2048
D_FF = 2816
RMS_EPS = 1e-6
N_CHIPS = 4
N_DEV = 8
FF_SHARD = D_FF // N_CHIPS
D_QKV = 2560
D_IN = D_QKV + 3 * D_MODEL
IN_SHARD = D_IN // N_CHIPS
D_SHARD = D_MODEL // N_CHIPS
BR_ROWS = 768
NEG = -1e30
QK_SCALE = HEAD_DIM ** -0.5

ADAM_LR = 0.001
ADAM_B1 = 0.9
ADAM_B2 = 0.999
ADAM_EPS = 1e-08
ADAM_WD = 0.01
ADAM_STEP = 10

VMEM_LIMIT = 56 * 1024 * 1024
ROW_TILE = 256
MM_TILE = 1024

NN = (((1,), (0,)), ((), ()))
NT = (((1,), (1,)), ((), ()))
TN = (((0,), (0,)), ((), ()))


def _params(sem=None):
    return pltpu.CompilerParams(dimension_semantics=sem, vmem_limit_bytes=VMEM_LIMIT)


def _dot(a, b, dims):
    return lax.dot_general(a, b, dims, preferred_element_type=F32)


def _sigmoid(x):
    return 1.0 / (1.0 + jnp.exp(-x))


def _matmul(name, grid, nk, k_axis, dims, n_pairs, in_specs, out_specs, out_shape, acc_shape, epilogue,
            operands, sem, aliases=None, prologue=None):
    n_in = len(in_specs)
    n_out = len(out_specs)

    def partial(ins):
        tot = None
        for p in range(n_pairs):
            a = ins[2 * p][...]
            if prologue is not None:
                a = prologue(p, a, ins)
            d = _dot(a, ins[2 * p + 1][...], dims)
            tot = d if tot is None else tot + d
        return tot

    def body(*refs):
        ins, outs = refs[:n_in], refs[n_in:n_in + n_out]
        ids = tuple(pl.program_id(a) for a in range(len(grid)))
        if nk == 1:
            epilogue(partial(ins), ins, outs, ids)
            return
        acc = refs[n_in + n_out]
        k = ids[k_axis]

        @pl.when(k == 0)
        def _():
            acc[...] = partial(ins)

        @pl.when(k > 0)
        def _():
            acc[...] += partial(ins)

        @pl.when(k == nk - 1)
        def _():
            epilogue(acc[...], ins, outs, ids)

    return pl.pallas_call(
        body, grid=grid, in_specs=in_specs, out_specs=out_specs, out_shape=out_shape,
        scratch_shapes=[] if nk == 1 else [pltpu.VMEM(acc_shape, F32)],
        input_output_aliases=aliases or {}, name=name, compiler_params=_params(sem),
    )(*operands)


def _row_spec(width=D_MODEL):
    return pl.BlockSpec((ROW_TILE, width), lambda i: (i, 0))


def _vec_spec(rows=1, width=D_MODEL):
    return pl.BlockSpec((rows, width), lambda i: (0, 0))


class Row:
    def __init__(self, table, index):
        self.table, self.index = table, index

    def spec(self):
        index = self.index
        return pl.BlockSpec((None, 1, D_MODEL), lambda *ids: (index, 0, 0))


def _slot_spec(u):
    return pl.BlockSpec((8, D_MODEL), lambda *ids: (u, 0))


def prenorm(x, gain, scale, shift):
    def body(x_ref, g_ref, sc_ref, sh_ref, h_ref):
        xv = x_ref[...]
        r = lax.rsqrt(jnp.mean(xv * xv, axis=-1, keepdims=True) + RMS_EPS)
        h_ref[...] = (((xv * r) * g_ref[...]) * (1.0 + sc_ref[...]) + sh_ref[...]).astype(BF16)

    return pl.pallas_call(
        body, grid=(SEQ // ROW_TILE,), in_specs=[_row_spec(), gain.spec(), scale.spec(), shift.spec()],
        out_specs=_row_spec(), out_shape=jax.ShapeDtypeStruct((SEQ, D_MODEL), BF16),
        name="prenorm", compiler_params=_params(("parallel",)),
    )(x, gain.table, scale.table, shift.table)


def resid_bwd(dxo, f, coef, mult, sums, u):
    def body(dx_ref, f_ref, c_ref, sums_in, df_ref, dc_ref):
        del sums_in
        dx = dx_ref[...]
        df_ref[...] = (dx * (mult * c_ref[...])).astype(BF16)
        part = mult * jnp.sum(dx * f_ref[...], axis=0, keepdims=True)

        @pl.when(pl.program_id(0) == 0)
        def _():
            dc_ref[...] = jnp.zeros_like(dc_ref)

        dc_ref[0:1, :] += part

    return pl.pallas_call(
        body, grid=(SEQ // ROW_TILE,),
        in_specs=[_row_spec(), _row_spec(), coef.spec(), pl.BlockSpec(memory_space=pl.ANY)],
        out_specs=[_row_spec(), _slot_spec(u)],
        out_shape=[jax.ShapeDtypeStruct((SEQ, D_MODEL), BF16), jax.ShapeDtypeStruct(sums.shape, F32)],
        input_output_aliases={3: 1}, name="resid_bwd", compiler_params=_params(("arbitrary",)),
    )(dxo, f, coef.table, sums)


def final_loss(x, gain, target):
    def body(x_ref, g_ref, t_ref, loss_ref, dx_ref, dg_ref):
        xv = x_ref[...]
        g = g_ref[...]
        r = lax.rsqrt(jnp.mean(xv * xv, axis=-1, keepdims=True) + RMS_EPS)
        xh = xv * r
        e = xh * g - t_ref[...]
        part = 0.5 * jnp.sum(jnp.mean(e * e, axis=-1, keepdims=True), axis=0, keepdims=True)
        dy = e * (1.0 / D_MODEL)
        dyg = dy * g
        dx_ref[...] = r * (dyg - xh * jnp.mean(dyg * xh, axis=-1, keepdims=True))

        @pl.when(pl.program_id(0) == 0)
        def _():
            loss_ref[...] = jnp.zeros_like(loss_ref)
            dg_ref[...] = jnp.zeros_like(dg_ref)

        loss_ref[...] += jnp.broadcast_to(part, loss_ref.shape)
        dg_ref[0:1, :] += jnp.sum(dy * xh, axis=0, keepdims=True)

    return pl.pallas_call(
        body, grid=(SEQ // ROW_TILE,), in_specs=[_row_spec(), _vec_spec(), _row_spec()],
        out_specs=[_vec_spec(8, 128), _row_spec(), _vec_spec(8)],
        out_shape=[jax.ShapeDtypeStruct((8, 128), F32), jax.ShapeDtypeStruct((SEQ, D_MODEL), F32),
                   jax.ShapeDtypeStruct((8, D_MODEL), F32)],
        name="final_loss", compiler_params=_params(("arbitrary",)),
    )(x, gain, target)


def _prenorm_bwd_epilogue(dh, x_ref, dxo_ref, g_ref, sc_ref, dx_ref, stats_ref, first):
    xv = x_ref[...]
    g = g_ref[...]
    r = lax.rsqrt(jnp.mean(xv * xv, axis=-1, keepdims=True) + RMS_EPS)
    xh = xv * r
    dn = dh * (1.0 + sc_ref[...])
    dxh = dn * g
    dx = dxo_ref[...] + r * (dxh - xh * jnp.mean(dxh * xh, axis=-1, keepdims=True))
    dx_ref[...] = dx

    @pl.when(first)
    def _():
        stats_ref[...] = jnp.zeros_like(stats_ref)

    stats_ref[0:1, :] += jnp.sum(dh, axis=0, keepdims=True)
    stats_ref[1:2, :] += jnp.sum(dh * (xh * g), axis=0, keepdims=True)
    stats_ref[2:3, :] += jnp.sum(dn * xh, axis=0, keepdims=True)
    return dx


def _resid_bwd_epilogue(dx, f_ref, c_ref, mult, df_ref, dc_ref, first):
    df_ref[...] = (dx * (mult * c_ref[...])).astype(BF16)

    @pl.when(first)
    def _():
        dc_ref[...] = jnp.zeros_like(dc_ref)

    dc_ref[0:1, :] += mult * jnp.sum(dx * f_ref[...], axis=0, keepdims=True)


def ffn_up(h, wg_all, wu_all):
    def body(h_ref, wg_ref, wu_ref, a_ref, b_ref, s_ref):
        hv = h_ref[...]
        a = _dot(hv, wg_ref[...], NT)
        b = _dot(hv, wu_ref[...], NT)
        a_ref[...] = a.astype(BF16)
        b_ref[...] = b.astype(BF16)
        s_ref[...] = (a * _sigmoid(a) * b).astype(BF16)

    w_spec = pl.BlockSpec((None, FF_SHARD, D_MODEL), lambda j, i: (j, 0, 0))
    o_spec = pl.BlockSpec((None, MM_TILE, FF_SHARD), lambda j, i: (j, i, 0))
    hid = (N_CHIPS, SEQ, FF_SHARD)
    return pl.pallas_call(
        body, grid=(N_CHIPS, SEQ // MM_TILE),
        in_specs=[pl.BlockSpec((MM_TILE, D_MODEL), lambda j, i: (i, 0)), w_spec, w_spec],
        out_specs=[o_spec, o_spec, o_spec],
        out_shape=[jax.ShapeDtypeStruct(hid, BF16)] * 3,
        name="ffn_up", compiler_params=_params(("parallel", "parallel")),
    )(h, wg_all, wu_all)


def matmul_residual(name, a, a_spec, w_all, w_spec, x, coef, mult, then=None):
    def epilogue(acc, ins, outs, ids):
        outs[0][...] = acc
        xo = ins[2][...] + (mult * ins[3][...]) * acc
        outs[1][...] = xo
        if then is not None:
            r = lax.rsqrt(jnp.mean(xo * xo, axis=-1, keepdims=True) + RMS_EPS)
            outs[2][...] = (((xo * r) * ins[4][...]) * (1.0 + ins[5][...]) + ins[6][...]).astype(BF16)

    row = pl.BlockSpec((MM_TILE, D_MODEL), lambda i, j: (i, 0))
    f32 = jax.ShapeDtypeStruct((SEQ, D_MODEL), F32)
    extra = [] if then is None else [then.gain, then.scale, then.shift]
    return _matmul(
        name, (SEQ // MM_TILE, N_CHIPS), N_CHIPS, 1, NN, 1,
        [a_spec, w_spec, row, coef.spec()] + [t.spec() for t in extra], [row] * (2 + bool(extra)),
        [f32, f32] + [jax.ShapeDtypeStruct((SEQ, D_MODEL), BF16)] * bool(extra), (MM_TILE, D_MODEL), epilogue,
        (a, w_all, x, coef.table) + tuple(t.table for t in extra), ("parallel", "arbitrary"))


def ffn_down(s, wd_all, x, gate, then):
    return matmul_residual(
        "ffn_down", s, pl.BlockSpec((None, MM_TILE, FF_SHARD), lambda i, j: (j, i, 0)),
        wd_all, pl.BlockSpec((None, FF_SHARD, D_MODEL), lambda i, j: (j, 0, 0)), x, gate, 0.5, then)


def ffn_bwd_hidden(df, wd_all, a, b):
    def epilogue(ds, ins, outs, ids):
        av, bv = ins[2][...].astype(F32), ins[3][...].astype(F32)
        sig = _sigmoid(av)
        outs[0][...] = (ds * bv * (sig * (1.0 + av * (1.0 - sig)))).astype(BF16)
        outs[1][...] = (ds * (av * sig)).astype(BF16)

    hid_spec = pl.BlockSpec((None, MM_TILE, FF_SHARD), lambda j, i: (j, i, 0))
    hid = jax.ShapeDtypeStruct((N_CHIPS, SEQ, FF_SHARD), BF16)
    return _matmul(
        "ffn_bwd_hidden", (N_CHIPS, SEQ // MM_TILE), 1, None, NT, 1,
        [pl.BlockSpec((MM_TILE, D_MODEL), lambda j, i: (i, 0)),
         pl.BlockSpec((None, FF_SHARD, D_MODEL), lambda j, i: (j, 0, 0)), hid_spec, hid_spec],
        [hid_spec, hid_spec], [hid, hid], None, epilogue, (df, wd_all, a, b), ("parallel", "parallel"))


def grad_weight(name, lhs, lhs_spec, rhs, rhs_spec, shape):
    def epilogue(acc, ins, outs, ids):
        outs[0][...] = acc.astype(BF16)

    return _matmul(
        name, (N_CHIPS, SEQ // MM_TILE), SEQ // MM_TILE, 1, TN, 1,
        [lhs_spec, rhs_spec], [pl.BlockSpec((None,) + shape, lambda j, k: (j, 0, 0))],
        [jax.ShapeDtypeStruct((N_CHIPS,) + shape, BF16)], shape, epilogue, (lhs, rhs), ("parallel", "arbitrary"))[0]


def ffn_grad_weights(h, s, df, da, db):
    tok = pl.BlockSpec((MM_TILE, D_MODEL), lambda j, k: (k, 0))
    hid = pl.BlockSpec((None, MM_TILE, FF_SHARD), lambda j, k: (j, k, 0))
    n_k = SEQ // MM_TILE

    def body(da_ref, db_ref, h_ref, gg_ref, gu_ref, acc_g, acc_u):
        k = pl.program_id(1)
        hv = h_ref[...]
        parts = (_dot(da_ref[...], hv, TN), _dot(db_ref[...], hv, TN))

        @pl.when(k == 0)
        def _():
            acc_g[...], acc_u[...] = parts

        @pl.when(k > 0)
        def _():
            acc_g[...] += parts[0]
            acc_u[...] += parts[1]

        @pl.when(k == n_k - 1)
        def _():
            gg_ref[...] = acc_g[...].astype(BF16)
            gu_ref[...] = acc_u[...].astype(BF16)

    out = pl.BlockSpec((None, FF_SHARD, D_MODEL), lambda j, k: (j, 0, 0))
    shape = jax.ShapeDtypeStruct((N_CHIPS, FF_SHARD, D_MODEL), BF16)
    g_gate, g_up = pl.pallas_call(
        body, grid=(N_CHIPS, n_k), in_specs=[hid, hid, tok], out_specs=[out, out], out_shape=[shape, shape],
        scratch_shapes=[pltpu.VMEM((FF_SHARD, D_MODEL), F32)] * 2,
        name="grad_w_gate_up", compiler_params=_params(("parallel", "arbitrary")),
    )(da, db, h)
    return g_gate, g_up, grad_weight("grad_w_down", s, hid, df, tok, (FF_SHARD, D_MODEL))


BWD_TILE = 512


def matmul_prenorm_bwd(name, dims, pairs, pair_specs, x, dxo, gain, scale, sums, u, below):
    n = len(pairs)

    def epilogue(dh, ins, outs, ids):
        first = ids[0] == 0
        dx = _prenorm_bwd_epilogue(dh, ins[n], ins[n + 1], ins[n + 2], ins[n + 3], outs[0], outs[1], first)
        if below is not None:
            _resid_bwd_epilogue(dx, ins[n + 5], ins[n + 6], below[2], outs[2], outs[3], first)

    row = pl.BlockSpec((BWD_TILE, D_MODEL), lambda i, j: (i, 0))
    any_spec = pl.BlockSpec(memory_space=pl.ANY)
    f32 = jax.ShapeDtypeStruct((SEQ, D_MODEL), F32)
    in_specs = list(pair_specs) + [row, row, gain.spec(), scale.spec(), any_spec]
    operands = tuple(pairs) + (x, dxo, gain.table, scale.table, sums[0])
    out_specs, out_shape, aliases = [row, _slot_spec(u)], [f32, jax.ShapeDtypeStruct(sums[0].shape, F32)], {n + 4: 1}
    if below is not None:
        in_specs += [row, below[1].spec(), any_spec]
        operands += (below[0], below[1].table, sums[1])
        out_specs += [row, _slot_spec(u - 1)]
        out_shape += [jax.ShapeDtypeStruct((SEQ, D_MODEL), BF16), jax.ShapeDtypeStruct(sums[1].shape, F32)]
        aliases[n + 7] = 3
    out = _matmul(name, (SEQ // BWD_TILE, N_CHIPS), N_CHIPS, 1, dims, len(pairs) // 2, in_specs, out_specs, out_shape,
                  (BWD_TILE, D_MODEL), epilogue, operands, ("arbitrary", "arbitrary"), aliases=aliases)
    if below is None:
        return out[0], (out[1], sums[1])
    return out[0], (out[1], out[3]), out[2]


def ffn_bwd_input(da, db, wg_all, wu_all, x, dxo, gain, scale, sums, u, below):
    hid = pl.BlockSpec((None, BWD_TILE, FF_SHARD), lambda i, j: (j, i, 0))
    w = pl.BlockSpec((None, FF_SHARD, D_MODEL), lambda i, j: (j, 0, 0))
    return matmul_prenorm_bwd("ffn_bwd_input", NN, (da, wg_all, db, wu_all), (hid, w, hid, w), x, dxo, gain, scale,
                              sums, u, below)


def in_proj(h, w_all):
    def epilogue(acc, ins, outs, ids):
        outs[0][...] = acc
        outs[1][...] = acc.astype(BF16)

    out = pl.BlockSpec((MM_TILE, IN_SHARD), lambda j, i: (i, j))
    return _matmul(
        "in_proj", (N_CHIPS, SEQ // MM_TILE), 1, None, NN, 1,
        [pl.BlockSpec((MM_TILE, D_MODEL), lambda j, i: (i, 0)),
         pl.BlockSpec((None, D_MODEL, IN_SHARD), lambda j, i: (j, 0, 0))],
        [out, out], [jax.ShapeDtypeStruct((SEQ, D_IN), F32), jax.ShapeDtypeStruct((SEQ, D_IN), BF16)],
        None, epilogue, (h, w_all), ("parallel", "parallel"))


_GATE_BLOCK0 = D_QKV // D_SHARD


def _branch_products(o, w_ref):
    ob = o.astype(BF16)
    return (_dot(ob[:, 0:256], w_ref[0:256, :], NN), _dot(ob[:, 256:384], w_ref[256:384, :], NN),
            _dot(ob[:, 384:768], w_ref[384:768, :], NN))


def merge_branches(o_cat, wbr_all, proj):
    def body(o_ref, w_ref, g0_ref, g1_ref, g2_ref, m_ref):
        u = _branch_products(o_ref[...], w_ref)
        m_ref[...] = (_sigmoid(g0_ref[...]) * u[0] + _sigmoid(g1_ref[...]) * u[1]
                      + _sigmoid(g2_ref[...]) * u[2]).astype(BF16)

    def gate_spec(b):
        return pl.BlockSpec((MM_TILE, D_SHARD), lambda i, j: (i, _GATE_BLOCK0 + 4 * b + j))

    return pl.pallas_call(
        body, grid=(SEQ // MM_TILE, N_CHIPS),
        in_specs=[pl.BlockSpec((MM_TILE, BR_ROWS), lambda i, j: (i, 0)),
                  pl.BlockSpec((None, BR_ROWS, D_SHARD), lambda i, j: (j, 0, 0)),
                  gate_spec(0), gate_spec(1), gate_spec(2)],
        out_specs=pl.BlockSpec((MM_TILE, D_SHARD), lambda i, j: (i, j)),
        out_shape=jax.ShapeDtypeStruct((SEQ, D_MODEL), BF16),
        name="merge_branches", compiler_params=_params(("parallel", "parallel")),
    )(o_cat, wbr_all, proj, proj, proj)


def out_proj(merged, wout_all, x, gate, then):
    return matmul_residual(
        "out_proj", merged, pl.BlockSpec((MM_TILE, D_SHARD), lambda i, j: (i, j)),
        wout_all, pl.BlockSpec((None, D_SHARD, D_MODEL), lambda i, j: (j, 0, 0)), x, gate, 1.0, then)


def merge_bwd(dmo, wout_all, o_cat, wbr_all, proj):
    def epilogue(dm, ins, outs, ids):
        u = _branch_products(ins[2][...], ins[3])
        for b in range(3):
            sig = _sigmoid(ins[4 + b][...])
            outs[b][...] = (dm * sig).astype(BF16)
            outs[3 + b][...] = (dm * u[b] * (sig * (1.0 - sig))).astype(BF16)

    def gate_spec(b):
        return pl.BlockSpec((MM_TILE, D_SHARD), lambda j, i: (i, _GATE_BLOCK0 + 4 * b + j))

    col = pl.BlockSpec((MM_TILE, D_SHARD), lambda j, i: (i, j))
    du = jax.ShapeDtypeStruct((SEQ, D_MODEL), BF16)
    return _matmul(
        "merge_bwd", (N_CHIPS, SEQ // MM_TILE), 1, None, NT, 1,
        [pl.BlockSpec((MM_TILE, D_MODEL), lambda j, i: (i, 0)),
         pl.BlockSpec((None, D_SHARD, D_MODEL), lambda j, i: (j, 0, 0)),
         pl.BlockSpec((MM_TILE, BR_ROWS), lambda j, i: (i, 0)),
         pl.BlockSpec((None, BR_ROWS, D_SHARD), lambda j, i: (j, 0, 0)),
         gate_spec(0), gate_spec(1), gate_spec(2)],
        [col] * 6, [du] * 6,
        None, epilogue, (dmo, wout_all, o_cat, wbr_all, proj, proj, proj), ("parallel", "parallel"))


def branch_bwd_input(du, wbr_all):
    def body(d0_ref, d1_ref, d2_ref, w_ref, o_ref, acc):
        j = pl.program_id(1)
        parts = (_dot(d0_ref[...], w_ref[0:256, :], NT), _dot(d1_ref[...], w_ref[256:384, :], NT),
                 _dot(d2_ref[...], w_ref[384:768, :], NT))

        @pl.when(j == 0)
        def _():
            acc[:, 0:256], acc[:, 256:384], acc[:, 384:768] = parts

        @pl.when(j > 0)
        def _():
            acc[:, 0:256] += parts[0]
            acc[:, 256:384] += parts[1]
            acc[:, 384:768] += parts[2]

        @pl.when(j == N_CHIPS - 1)
        def _():
            o_ref[...] = acc[...]

    col = pl.BlockSpec((MM_TILE, D_SHARD), lambda i, j: (i, j))
    return pl.pallas_call(
        body, grid=(SEQ // MM_TILE, N_CHIPS),
        in_specs=[col, col, col, pl.BlockSpec((None, BR_ROWS, D_SHARD), lambda i, j: (j, 0, 0))],
        out_specs=pl.BlockSpec((MM_TILE, BR_ROWS), lambda i, j: (i, 0)),
        out_shape=jax.ShapeDtypeStruct((SEQ, BR_ROWS), F32),
        scratch_shapes=[pltpu.VMEM((MM_TILE, BR_ROWS), F32)],
        name="branch_bwd_input", compiler_params=_params(("parallel", "arbitrary")),
    )(du[0], du[1], du[2], wbr_all)


def branch_grad_weights(o_cat, du):
    def body(o_ref, d0_ref, d1_ref, d2_ref, g_ref, acc):
        k = pl.program_id(1)
        ob = o_ref[...].astype(BF16)
        parts = (_dot(ob[:, 0:256], d0_ref[...], TN), _dot(ob[:, 256:384], d1_ref[...], TN),
                 _dot(ob[:, 384:768], d2_ref[...], TN))

        @pl.when(k == 0)
        def _():
            acc[0:256, :], acc[256:384, :], acc[384:768, :] = parts

        @pl.when(k > 0)
        def _():
            acc[0:256, :] += parts[0]
            acc[256:384, :] += parts[1]
            acc[384:768, :] += parts[2]

        @pl.when(k == SEQ // MM_TILE - 1)
        def _():
            g_ref[...] = acc[...].astype(BF16)

    col = pl.BlockSpec((MM_TILE, D_SHARD), lambda j, k: (k, j))
    return pl.pallas_call(
        body, grid=(N_CHIPS, SEQ // MM_TILE),
        in_specs=[pl.BlockSpec((MM_TILE, BR_ROWS), lambda j, k: (k, 0)), col, col, col],
        out_specs=pl.BlockSpec((None, BR_ROWS, D_SHARD), lambda j, k: (j, 0, 0)),
        out_shape=jax.ShapeDtypeStruct((N_CHIPS, BR_ROWS, D_SHARD), BF16),
        scratch_shapes=[pltpu.VMEM((BR_ROWS, D_SHARD), F32)],
        name="branch_grad_weights", compiler_params=_params(("parallel", "arbitrary")),
    )(o_cat, du[0], du[1], du[2])


def mixer_bwd_input(dproj, win_all, x, dxo, gain, scale, sums, u, below):
    return matmul_prenorm_bwd(
        "mixer_bwd_input", NT, (dproj, win_all),
        (pl.BlockSpec((BWD_TILE, IN_SHARD), lambda i, j: (i, j)),
         pl.BlockSpec((None, D_MODEL, IN_SHARD), lambda i, j: (j, 0, 0))), x, dxo, gain, scale, sums, u, below)


BATCH_QK = (((2,), (2,)), ((0,), (0,)))
BATCH_PV = (((2,), (1,)), ((0,), (0,)))
BATCH_TN = (((1,), (1,)), ((0,), (0,)))


SB_WIDTH = H_SB * HEAD_DIM
SB_ROWS = H_SB * BLK


def _split_dot(v, tri):
    hi = v.astype(BF16)
    lo = (v - hi.astype(F32)).astype(BF16)
    return _dot(hi, tri, NN) + _dot(lo, tri, NN)


def _tri(cmp):
    return cmp(lax.broadcasted_iota(jnp.int32, (BLK, BLK), 0), lax.broadcasted_iota(jnp.int32, (BLK, BLK), 1)).astype(BF16)


def _head_masks():
    lane = lax.broadcasted_iota(jnp.int32, (1, SB_WIDTH), 1) // HEAD_DIM
    return [lane == h for h in range(H_SB)]


def _stack_heads(x, masks):
    return jnp.concatenate([jnp.where(m, x, jnp.zeros_like(x)) for m in masks], axis=0)


def _merge_heads(y, masks):
    out = jnp.where(masks[0], y[0:BLK], 0.0)
    for h in range(1, H_SB):
        out = jnp.where(masks[h], y[h * BLK:(h + 1) * BLK], out)
    return out


def _sb_scores(q4, k_ref, j, diagonal):
    rows = pl.ds(pl.multiple_of(j * BLK, BLK), BLK)
    z = _dot(q4, k_ref[rows, :], NT)
    log_fail = -(jnp.maximum(z, 0.0) + jnp.log(1.0 + jnp.exp(-jnp.abs(z))))
    log_hit = z + log_fail
    before = None
    if diagonal:
        tile = (SB_ROWS, BLK)
        before = lax.broadcasted_iota(jnp.int32, tile, 1) < (lax.broadcasted_iota(jnp.int32, tile, 0) & (BLK - 1))
        log_fail = jnp.where(before, log_fail, 0.0)
    return rows, before, log_fail, log_hit


def _keep(before, x):
    return x if before is None else jnp.where(before, x, 0.0)


def sb_forward(qkv):
    def body(q_ref, k_ref, v_ref, o_ref, tot_ref):
        i = pl.program_id(0)
        masks = _head_masks()
        q4 = _stack_heads(q_ref[...] * QK_SCALE, masks)
        later = _tri(lambda r, c: r > c)

        def tiles(js, carry, diagonal):
            o, run = carry
            scores = [_sb_scores(q4, k_ref, j, diagonal) for j in js]
            acc = None
            for rows, before, log_fail, log_hit in scores:
                between = _split_dot(log_fail, later) + run
                w = _keep(before, jnp.exp(log_hit + between))
                part = _dot(w.astype(BF16), v_ref[rows, :], NN)
                acc = part if acc is None else acc + part
                run = run + jnp.sum(log_fail, axis=1, keepdims=True)
            return o + _merge_heads(acc, masks), run

        carry = tiles([i], (jnp.zeros((BLK, SB_WIDTH), F32), jnp.zeros((SB_ROWS, 1), F32)), True)
        carry = lax.cond((i & 1) != 0, lambda c: tiles([i - 1], c, False), lambda c: c, carry)
        at = i - 1 - (i & 1)
        carry = lax.cond((i & 2) != 0, lambda c: tiles([at, at - 1], c, False), lambda c: c, carry)
        at = at - (i & 2)
        o, run = lax.fori_loop(0, i // 4, lambda t, c: tiles([at - 4 * t - n for n in range(4)], c, False), carry)
        o_ref[...] = o
        tot_ref[...] = run

    return pl.pallas_call(
        body, grid=(N_BLK,),
        in_specs=[pl.BlockSpec((BLK, SB_WIDTH), lambda i: (i, 0)), pl.BlockSpec((SEQ, SB_WIDTH), lambda i: (0, 1)),
                  pl.BlockSpec((SEQ, SB_WIDTH), lambda i: (0, 2))],
        out_specs=[pl.BlockSpec((BLK, SB_WIDTH), lambda i: (i, 0)), pl.BlockSpec((None, SB_ROWS, 1), lambda i: (i, 0, 0))],
        out_shape=[jax.ShapeDtypeStruct((SEQ, SB_WIDTH), F32), jax.ShapeDtypeStruct((N_BLK, SB_ROWS, 1), F32)],
        name="sb_forward", compiler_params=_params(("parallel",)),
    )(qkv, qkv, qkv)


def sb_backward(qkv, total, do_cat):
    def body(q_ref, k_ref, v_ref, tot_ref, do_ref, dq_ref, dk_ref, dv_ref):
        i = pl.program_id(0)

        @pl.when(i == 0)
        def _():
            dk_ref[...] = jnp.zeros_like(dk_ref)
            dv_ref[...] = jnp.zeros_like(dv_ref)

        masks = _head_masks()
        q4 = _stack_heads(q_ref[...] * QK_SCALE, masks)
        do4 = _stack_heads(do_ref[...].astype(BF16), masks)
        total_v = tot_ref[...]
        upto = _tri(lambda r, c: r <= c)
        earlier = _tri(lambda r, c: r < c)

        def tiles(js, carry, diagonal):
            dq, seen, g_seen = carry
            scores = [_sb_scores(q4, k_ref, j, diagonal) for j in js]
            acc = None
            for rows, before, log_fail, log_hit in scores:
                between = total_v - (seen + _split_dot(log_fail, upto))
                w = _keep(before, jnp.exp(log_hit + between))
                g = _dot(do4, v_ref[rows, :], NT) * w
                g_earlier = g_seen + _split_dot(g, earlier)
                sig = jnp.exp(log_hit)
                dz = _keep(before, g * (1.0 - sig) - g_earlier * sig).astype(BF16)
                part = _dot(dz, k_ref[rows, :], NN)
                acc = part if acc is None else acc + part
                dk_ref[rows, :] += _dot(dz, q4, TN)
                dv_ref[rows, :] += _dot(w.astype(BF16), do4, TN)
                seen = seen + jnp.sum(log_fail, axis=1, keepdims=True)
                g_seen = g_seen + jnp.sum(g, axis=1, keepdims=True)
            return dq + _merge_heads(acc, masks), seen, g_seen

        zero = jnp.zeros((SB_ROWS, 1), F32)
        carry = lax.fori_loop(0, i // 4, lambda t, c: tiles([4 * t + n for n in range(4)], c, False),
                              (jnp.zeros((BLK, SB_WIDTH), F32), zero, zero))
        at = i - (i & 3)
        carry = lax.cond((i & 2) != 0, lambda c: tiles([at, at + 1], c, False), lambda c: c, carry)
        carry = lax.cond((i & 1) != 0, lambda c: tiles([i - 1], c, False), lambda c: c, carry)
        dq, _, _ = tiles([i], carry, True)
        dq_ref[...] = dq * QK_SCALE

    blk = pl.BlockSpec((BLK, SB_WIDTH), lambda i: (i, 0))
    full = pl.BlockSpec((SEQ, SB_WIDTH), lambda i: (0, 0))
    shape = jax.ShapeDtypeStruct((SEQ, SB_WIDTH), F32)
    return pl.pallas_call(
        body, grid=(N_BLK,),
        in_specs=[blk, pl.BlockSpec((SEQ, SB_WIDTH), lambda i: (0, 1)), pl.BlockSpec((SEQ, SB_WIDTH), lambda i: (0, 2)),
                  pl.BlockSpec((None, SB_ROWS, 1), lambda i: (i, 0, 0)), blk],
        out_specs=[blk, full, full], out_shape=[shape, shape, shape],
        name="sb_backward", compiler_params=_params(("arbitrary",)),
    )(qkv, qkv, qkv, total, do_cat)


def _band_scores(q_ref, kp_ref, ko_ref, bias_ref, hb, prev_mask):
    b = pl.program_id(1)
    qs = q_ref[...]
    s_prev = _dot(qs, kp_ref[...], BATCH_QK) + bias_ref[:, :, 0:BLK]
    s_prev = jnp.concatenate(
        [jnp.where((b & prev_mask(pl.program_id(0) * hb + t)) != 0, s_prev[t:t + 1], NEG) for t in range(hb)], axis=0)
    s_own = _dot(qs, ko_ref[...], BATCH_QK) + bias_ref[:, :, BLK:2 * BLK]
    return qs, s_prev, s_own


def _band_specs(hb, rows, t_n):
    def q_spec(width):
        return pl.BlockSpec((hb, None, rows, width), lambda h, b: (h, b, 0, 0))

    own = pl.BlockSpec((hb, BLK, HEAD_DIM), lambda h, b: (h, b, 0))
    prev = pl.BlockSpec((hb, BLK, HEAD_DIM), lambda h, b: (h, jnp.maximum(b - 1, 0), 0))
    per_head = lambda r, width: pl.BlockSpec((hb, r, width), lambda h, b: (h, 0, 0))
    return q_spec, own, prev, per_head


def banded_forward(name, q, k, v, bias, sinks, hb, prev_mask):
    h_n, nb, rows, _ = q.shape

    def body(q_ref, kp_ref, ko_ref, vp_ref, vo_ref, bias_ref, sink_ref, o_ref, lse_ref):
        _, s_prev, s_own = _band_scores(q_ref, kp_ref, ko_ref, bias_ref, hb, prev_mask)
        sink = sink_ref[...]
        m = jnp.maximum(jnp.maximum(jnp.max(s_prev, axis=2, keepdims=True), jnp.max(s_own, axis=2, keepdims=True)), sink)
        p_prev = jnp.exp(s_prev - m)
        p_own = jnp.exp(s_own - m)
        denom = jnp.sum(p_prev, axis=2, keepdims=True) + jnp.sum(p_own, axis=2, keepdims=True) + jnp.exp(sink - m)
        o = _dot(p_prev.astype(BF16), vp_ref[...], BATCH_PV) + _dot(p_own.astype(BF16), vo_ref[...], BATCH_PV)
        o_ref[...] = o / denom
        lse_ref[...] = m + jnp.log(denom)

    q_spec, own, prev, per_head = _band_specs(hb, rows, k.shape[1])
    return pl.pallas_call(
        body, grid=(h_n // hb, nb),
        in_specs=[q_spec(HEAD_DIM), prev, own, prev, own, per_head(rows, 2 * BLK), per_head(rows, 1)],
        out_specs=[q_spec(HEAD_DIM), q_spec(1)],
        out_shape=[jax.ShapeDtypeStruct(q.shape, F32), jax.ShapeDtypeStruct((h_n, nb, rows, 1), F32)],
        name=name, compiler_params=_params(("parallel", "parallel")),
    )(q, k, k, v, v, bias, sinks)


def banded_backward(name, q, k, v, bias, sinks, o, lse, do, dlse, hb, prev_mask):
    h_n, nb, rows, _ = q.shape
    t_n = k.shape[1]

    def body(q_ref, kp_ref, ko_ref, vp_ref, vo_ref, bias_ref, sink_ref, o_ref, lse_ref, do_ref, dlse_ref,
             dq_ref, dk_ref, dv_ref, dbias_ref, dsink_ref):
        b = pl.program_id(1)

        @pl.when(b == 0)
        def _():
            dk_ref[...] = jnp.zeros_like(dk_ref)
            dv_ref[...] = jnp.zeros_like(dv_ref)
            dbias_ref[...] = jnp.zeros_like(dbias_ref)
            dsink_ref[...] = jnp.zeros_like(dsink_ref)

        qs, s_prev, s_own = _band_scores(q_ref, kp_ref, ko_ref, bias_ref, hb, prev_mask)
        lse_v = lse_ref[...]
        dov = do_ref[...]
        dob = dov.astype(BF16)
        shift = dlse_ref[...] - jnp.sum(dov * o_ref[...], axis=2, keepdims=True)
        p_prev = jnp.exp(s_prev - lse_v)
        p_own = jnp.exp(s_own - lse_v)
        ds_prev = p_prev * (_dot(dob, vp_ref[...], BATCH_QK) + shift)
        ds_own = p_own * (_dot(dob, vo_ref[...], BATCH_QK) + shift)
        dbias_ref[:, :, 0:BLK] += ds_prev
        dbias_ref[:, :, BLK:2 * BLK] += ds_own
        d_sink = jnp.exp(sink_ref[...] - lse_v) * shift
        for g in range(rows // BLK):
            dsink_ref[:, g:g + 1, :] += jnp.sum(d_sink[:, g * BLK:(g + 1) * BLK, :], axis=1, keepdims=True)
        ds_prev = ds_prev.astype(BF16)
        ds_own = ds_own.astype(BF16)
        dq_ref[...] = (_dot(ds_prev, kp_ref[...], BATCH_PV) + _dot(ds_own, ko_ref[...], BATCH_PV)) * QK_SCALE
        rows_prev = pl.ds(pl.multiple_of(jnp.maximum(b - 1, 0) * BLK, BLK), BLK)
        rows_own = pl.ds(pl.multiple_of(b * BLK, BLK), BLK)
        dk_ref[:, rows_prev, :] += _dot(ds_prev, qs, BATCH_TN)
        dk_ref[:, rows_own, :] += _dot(ds_own, qs, BATCH_TN)
        dv_ref[:, rows_prev, :] += _dot(p_prev.astype(BF16), dob, BATCH_TN)
        dv_ref[:, rows_own, :] += _dot(p_own.astype(BF16), dob, BATCH_TN)

    q_spec, own, prev, per_head = _band_specs(hb, rows, t_n)
    kv_full = per_head(t_n, HEAD_DIM)
    kv_shape = jax.ShapeDtypeStruct((h_n, t_n, HEAD_DIM), F32)
    return pl.pallas_call(
        body, grid=(h_n // hb, nb),
        in_specs=[q_spec(HEAD_DIM), prev, own, prev, own, per_head(rows, 2 * BLK), per_head(rows, 1),
                  q_spec(HEAD_DIM), q_spec(1), q_spec(HEAD_DIM), q_spec(1)],
        out_specs=[q_spec(HEAD_DIM), kv_full, kv_full, per_head(rows, 2 * BLK), per_head(rows // BLK, BLK)],
        out_shape=[jax.ShapeDtypeStruct(q.shape, F32), kv_shape, kv_shape,
                   jax.ShapeDtypeStruct((h_n, rows, 2 * BLK), F32), jax.ShapeDtypeStruct((h_n, rows // BLK, BLK), F32)],
        name=name, compiler_params=_params(("parallel", "arbitrary")),
    )(q, k, k, v, v, bias, sinks, o, lse, do, dlse)


def _swa_prev_mask(head):
    del head
    return 15


SWA_HEADS_PER_STEP = 2
SWA_GROUP = H_SWA_Q // H_SWA_KV
N_BLK = SEQ // BLK


GROUP_W = H_PER_DIL * HEAD_DIM
DIL_COLUMNS = (768, 1920)
LANE_BLOCKS = (DIL_COLUMNS[1] - DIL_COLUMNS[0]) // GROUP_W
DIL_Q_BLOCK, DIL_K_BLOCK, DIL_V_BLOCK = 0, 3, 6
N_GROUPS = len(DIL_PATTERNS)


def dilated_views(qkv):
    cols = qkv[:, DIL_COLUMNS[0]:DIL_COLUMNS[1]]
    return [_dil_view(cols, d) for _, d in DIL_PATTERNS]


def _dil_view(t, d):
    return t.reshape(SEQ // d, d * t.shape[1])


def _dil_tile(n, d):
    per_class = N_BLK // d
    return n // per_class, n % per_class


def _dil_spec(d, lane_block, lane_blocks, shift=0):
    def index(n):
        r, m = _dil_tile(n, d)
        m = jnp.clip(m + shift, 0, N_BLK // d - 1)
        return m, r * lane_blocks + lane_block
    return pl.BlockSpec((BLK, GROUP_W), index)


def _two_heads(x, first):
    zero = jnp.zeros_like(x)
    return jnp.concatenate([jnp.where(first, x, zero), jnp.where(first, zero, x)], axis=0)


def _per_head(col, first):
    return jnp.where(first, col[0:BLK], col[BLK:2 * BLK])


def _head_rows(tile, first):
    pick = lambda keep: jnp.max(jnp.where(keep, tile, -jnp.inf), axis=1, keepdims=True)
    return jnp.concatenate([pick(first), pick(jnp.logical_not(first))], axis=0)


def _dil_scores(q_ref, kp_ref, ko_ref, bias, has_prev, first):
    q2 = _two_heads(q_ref[...] * QK_SCALE, first)
    k2 = jnp.concatenate([kp_ref[...], ko_ref[...]], axis=0)
    s = _dot(q2, k2, NT) + bias
    key = lax.broadcasted_iota(jnp.int32, s.shape, 1)
    return q2, k2, jnp.where(jnp.logical_or(has_prev, key >= BLK), s, NEG)


def dilated_forward(views, bias):
    def body(*refs):
        ins, bias_ref, outs = refs[:5 * N_GROUPS], refs[5 * N_GROUPS], refs[5 * N_GROUPS + 1:]
        n = pl.program_id(0)
        first = lax.broadcasted_iota(jnp.int32, (1, GROUP_W), 1) < HEAD_DIM
        for g, (_, d) in enumerate(DIL_PATTERNS):
            q_ref, kp_ref, ko_ref, vp_ref, vo_ref = ins[5 * g:5 * g + 5]
            has_prev = _dil_tile(n, d)[1] > 0
            _, _, s = _dil_scores(q_ref, kp_ref, ko_ref, bias_ref[g], has_prev, first)
            m = jnp.max(s, axis=1, keepdims=True)
            p = jnp.exp(s - m)
            denom = jnp.sum(p, axis=1, keepdims=True)
            v2 = jnp.concatenate([vp_ref[...], vo_ref[...]], axis=0)
            o2 = _dot(p.astype(BF16), v2, NN) / denom
            outs[2 * g][...] = _per_head(o2, first)
            outs[2 * g + 1][...] = _per_head(m + jnp.log(denom), first)

    operands, in_specs, out_specs, out_shape = [], [], [], []
    for g, (_, d) in enumerate(DIL_PATTERNS):
        operands += [views[g]] * 5
        in_specs += [_dil_spec(d, DIL_Q_BLOCK + g, LANE_BLOCKS), _dil_spec(d, DIL_K_BLOCK + g, LANE_BLOCKS, -1),
                     _dil_spec(d, DIL_K_BLOCK + g, LANE_BLOCKS), _dil_spec(d, DIL_V_BLOCK + g, LANE_BLOCKS, -1),
                     _dil_spec(d, DIL_V_BLOCK + g, LANE_BLOCKS)]
        out_specs += [_dil_spec(d, 0, 1)] * 2
        out_shape += [jax.ShapeDtypeStruct((SEQ // d, d * GROUP_W), F32)] * 2
    out = pl.pallas_call(
        body, grid=(N_BLK,), in_specs=in_specs + [pl.BlockSpec((N_GROUPS, 2 * BLK, 2 * BLK), lambda n: (0, 0, 0))],
        out_specs=out_specs, out_shape=out_shape, name="dilated_forward", compiler_params=_params(("parallel",)),
    )(*operands, bias)
    out = [t.reshape(SEQ, GROUP_W) for t in out]
    return out[0::2], out[1::2]


def _group_softmax(lses):
    m = jnp.maximum(jnp.maximum(lses[0], lses[1]), lses[2])
    e = [jnp.exp(l - m) for l in lses]
    total = e[0] + e[1] + e[2]
    return [t / total for t in e]


def dilated_merge(o, lse):
    def body(*refs):
        alpha = _group_softmax([r[...] for r in refs[N_GROUPS:2 * N_GROUPS]])
        refs[-1][...] = alpha[0] * refs[0][...] + alpha[1] * refs[1][...] + alpha[2] * refs[2][...]

    spec = pl.BlockSpec((ROW_TILE, GROUP_W), lambda i: (i, 0))
    return pl.pallas_call(
        body, grid=(SEQ // ROW_TILE,), in_specs=[spec] * (2 * N_GROUPS), out_specs=spec,
        out_shape=jax.ShapeDtypeStruct((SEQ, GROUP_W), F32), name="dilated_merge", compiler_params=_params(("parallel",)),
    )(*o, *lse)


def dilated_merge_bwd(o, lse, do_cat):
    def body(*refs):
        o_v = [r[...] for r in refs[:N_GROUPS]]
        alpha = _group_softmax([r[...] for r in refs[N_GROUPS:2 * N_GROUPS]])
        dout = refs[2 * N_GROUPS][...]
        outs = refs[2 * N_GROUPS + 1:]
        first = lax.broadcasted_iota(jnp.int32, (1, GROUP_W), 1) < HEAD_DIM

        def head_sum(x):
            a = jnp.sum(jnp.where(first, x, 0.0), axis=1, keepdims=True)
            b = jnp.sum(jnp.where(first, 0.0, x), axis=1, keepdims=True)
            return jnp.where(first, a, b)

        dalpha = [head_sum(dout * o_g) for o_g in o_v]
        mean = alpha[0] * dalpha[0] + alpha[1] * dalpha[1] + alpha[2] * dalpha[2]
        for g in range(N_GROUPS):
            outs[g][...] = alpha[g] * dout
            outs[N_GROUPS + g][...] = alpha[g] * (dalpha[g] - mean)

    spec = pl.BlockSpec((ROW_TILE, GROUP_W), lambda i: (i, 0))
    shape = jax.ShapeDtypeStruct((SEQ, GROUP_W), F32)
    out = pl.pallas_call(
        body, grid=(SEQ // ROW_TILE,), in_specs=[spec] * (2 * N_GROUPS) + [pl.BlockSpec((ROW_TILE, GROUP_W), lambda i: (i, 2))],
        out_specs=[spec] * (2 * N_GROUPS), out_shape=[shape] * (2 * N_GROUPS),
        name="dilated_merge_bwd", compiler_params=_params(("parallel",)),
    )(*o, *lse, do_cat)
    return out[:N_GROUPS], out[N_GROUPS:]


def dilated_backward(views, bias, o, lse, do, dlse):
    n_in = 9

    def body(*refs):
        ins, bias_ref = refs[:n_in * N_GROUPS], refs[n_in * N_GROUPS]
        outs, dbias_ref = refs[n_in * N_GROUPS + 1:-1], refs[-1]
        n = pl.program_id(0)

        @pl.when(n == 0)
        def _():
            dbias_ref[...] = jnp.zeros_like(dbias_ref)

        first = lax.broadcasted_iota(jnp.int32, (1, GROUP_W), 1) < HEAD_DIM
        for g, (_, d) in enumerate(DIL_PATTERNS):
            q_ref, kp_ref, ko_ref, vp_ref, vo_ref, o_ref, lse_ref, do_ref, dlse_ref = ins[n_in * g:n_in * (g + 1)]
            has_prev = _dil_tile(n, d)[1] > 0
            q2, k2, s = _dil_scores(q_ref, kp_ref, ko_ref, bias_ref[g], has_prev, first)
            dov = do_ref[...]
            do2 = _two_heads(dov.astype(BF16), first)
            prod = dov * o_ref[...]
            delta = jnp.concatenate([jnp.sum(jnp.where(first, prod, 0.0), axis=1, keepdims=True),
                                     jnp.sum(jnp.where(first, 0.0, prod), axis=1, keepdims=True)], axis=0)
            shift = _head_rows(dlse_ref[...], first) - delta
            p = jnp.exp(s - _head_rows(lse_ref[...], first))
            v2 = jnp.concatenate([vp_ref[...], vo_ref[...]], axis=0)
            ds = p * (_dot(do2, v2, NT) + shift)
            dbias_ref[g] += ds
            ds = ds.astype(BF16)
            dq2 = _dot(ds, k2, NN) * QK_SCALE
            dk2 = _dot(ds, q2, TN)
            dv2 = _dot(p.astype(BF16), do2, TN)
            base = 5 * g
            outs[base][...] = jnp.where(first, dq2[0:BLK], dq2[BLK:2 * BLK])
            outs[base + 1][...] = dk2[BLK:2 * BLK]
            outs[base + 2][...] = dk2[0:BLK]
            outs[base + 3][...] = dv2[BLK:2 * BLK]
            outs[base + 4][...] = dv2[0:BLK]

    operands, in_specs, out_specs, out_shape = [], [], [], []
    for g, (_, d) in enumerate(DIL_PATTERNS):
        own = _dil_spec(d, 0, 1)
        operands += [views[g]] * 5 + [_dil_view(t[g], d) for t in (o, lse, do, dlse)]
        in_specs += [_dil_spec(d, DIL_Q_BLOCK + g, LANE_BLOCKS), _dil_spec(d, DIL_K_BLOCK + g, LANE_BLOCKS, -1),
                     _dil_spec(d, DIL_K_BLOCK + g, LANE_BLOCKS), _dil_spec(d, DIL_V_BLOCK + g, LANE_BLOCKS, -1),
                     _dil_spec(d, DIL_V_BLOCK + g, LANE_BLOCKS)] + [own] * 4
        out_specs += [own] * 5
        out_shape += [jax.ShapeDtypeStruct((SEQ // d, d * GROUP_W), F32)] * 5
    tiles = pl.BlockSpec((N_GROUPS, 2 * BLK, 2 * BLK), lambda n: (0, 0, 0))
    out = pl.pallas_call(
        body, grid=(N_BLK,), in_specs=in_specs + [tiles], out_specs=out_specs + [tiles],
        out_shape=out_shape + [jax.ShapeDtypeStruct((N_GROUPS, 2 * BLK, 2 * BLK), F32)],
        name="dilated_backward", compiler_params=_params(("arbitrary",)),
    )(*operands, bias)
    return [out[5 * g:5 * g + 5] for g in range(N_GROUPS)], out[-1]


def dilated_key_grads(parts):
    def body(*refs):
        ins, outs = refs[:4 * N_GROUPS], refs[4 * N_GROUPS:]
        n = pl.program_id(0)
        for g, (_, d) in enumerate(DIL_PATTERNS):
            has_next = _dil_tile(n, d)[1] < N_BLK // d - 1
            own_k, next_k, own_v, next_v = ins[4 * g:4 * g + 4]
            outs[2 * g][...] = own_k[...] + jnp.where(has_next, next_k[...], 0.0)
            outs[2 * g + 1][...] = own_v[...] + jnp.where(has_next, next_v[...], 0.0)

    operands, in_specs, out_specs, out_shape = [], [], [], []
    for g, (_, d) in enumerate(DIL_PATTERNS):
        _, dk_own, dk_prev, dv_own, dv_prev = parts[g]
        operands += [dk_own, dk_prev, dv_own, dv_prev]
        in_specs += [_dil_spec(d, 0, 1), _dil_spec(d, 0, 1, 1)] * 2
        out_specs += [_dil_spec(d, 0, 1)] * 2
        out_shape += [jax.ShapeDtypeStruct((SEQ // d, d * GROUP_W), F32)] * 2
    out = pl.pallas_call(
        body, grid=(N_BLK,), in_specs=in_specs, out_specs=out_specs, out_shape=out_shape,
        name="dilated_key_grads", compiler_params=_params(("parallel",)),
    )(*operands)
    tok = lambda ts: jnp.concatenate([t.reshape(SEQ, GROUP_W) for t in ts], axis=1)
    return tok([parts[g][0] for g in range(N_GROUPS)]), tok(out[0::2]), tok(out[1::2])


def rel_bias_reduce(dbias0, dbias1, bucket):
    def body(d0_ref, d1_ref, b_ref, o_ref):
        dv, bv = d0_ref[...] + d1_ref[...], b_ref[...]
        lane = lax.broadcasted_iota(jnp.int32, (1, BLK), 1)
        acc = jnp.zeros((1, BLK), F32)
        for bkt in range(N_BUCKETS):
            acc = acc + jnp.where(lane == bkt, jnp.sum(jnp.where(bv == bkt, dv, 0.0)), 0.0)
        o_ref[...] = acc

    tile = pl.BlockSpec((None, BLK, 2 * BLK), lambda h: (h, 0, 0))
    return pl.pallas_call(
        body, grid=(dbias0.shape[0],), in_specs=[tile, tile, tile],
        out_specs=pl.BlockSpec((None, 1, BLK), lambda h: (h, 0, 0)),
        out_shape=jax.ShapeDtypeStruct((dbias0.shape[0], 1, BLK), F32),
        name="rel_bias_reduce", compiler_params=_params(("parallel",)),
    )(dbias0, dbias1, bucket)


def _heads(t):
    return t.reshape(SEQ, -1, HEAD_DIM).transpose(1, 0, 2)


def _unheads(t):
    return t.transpose(1, 0, 2).reshape(SEQ, -1)


def _t5_bucket(n):
    max_exact = N_BUCKETS // 2
    nf = jnp.maximum(n, 1).astype(F32)
    large = max_exact + (jnp.log(nf / max_exact) / math.log(MAX_REL_DIST / max_exact)
                         * (N_BUCKETS - max_exact)).astype(jnp.int32)
    large = jnp.minimum(large, N_BUCKETS - 1)
    return jnp.where(n < max_exact, n, large)


def band_tables(rel_bias):
    rel = jnp.arange(BLK)[:, None] + BLK - jnp.arange(2 * BLK)[None, :]
    buckets = []
    patterns = [(d, w // d) for w, d in DIL_PATTERNS for _ in range(H_PER_DIL)] + [(1, SWA_WINDOW - 1)] * H_SWA_Q
    for d, max_dist in patterns:
        band = (rel >= 0) & (rel <= max_dist)
        buckets.append(jnp.where(band, _t5_bucket(jnp.maximum(rel, 0) * d), -1))
    buckets = jnp.stack(buckets).astype(jnp.int32)

    def body(table_ref, b_ref, o_ref):
        h = pl.program_id(0)
        bv = b_ref[...]
        tile = jnp.full(bv.shape, NEG, F32)
        for bkt in range(N_BUCKETS):
            tile = jnp.where(bv == bkt, table_ref[h, bkt], tile)
        o_ref[...] = tile

    spec = pl.BlockSpec((None, BLK, 2 * BLK), lambda h: (h, 0, 0))
    tiles = pl.pallas_call(
        body, grid=(len(patterns),), in_specs=[pl.BlockSpec(memory_space=pltpu.SMEM), spec], out_specs=spec,
        out_shape=jax.ShapeDtypeStruct(buckets.shape, F32), name="band_tables", compiler_params=_params(("parallel",)),
    )(rel_bias.T, buckets)
    return tiles[:H_DIL], tiles[H_DIL:], buckets


def _swa_rows(t):
    t = t.reshape(N_BLK, BLK, H_SWA_KV, SWA_GROUP, HEAD_DIM).transpose(2, 0, 3, 1, 4)
    return t.reshape(H_SWA_KV, N_BLK, SWA_GROUP * BLK, HEAD_DIM)


def _swa_tokens(t):
    t = t.reshape(H_SWA_KV, N_BLK, SWA_GROUP, BLK, HEAD_DIM).transpose(1, 3, 0, 2, 4)
    return t.reshape(SEQ, H_SWA_Q * HEAD_DIM)


def _sink_rows(sinks):
    return jnp.broadcast_to(sinks.reshape(H_SWA_KV, SWA_GROUP, 1, 1), (H_SWA_KV, SWA_GROUP, BLK, 1)).reshape(
        H_SWA_KV, SWA_GROUP * BLK, 1)


def _vec(v):
    return v.reshape(1, D_MODEL)


class UnitRows:
    def __init__(self, u, mod_table, gain_table):
        self.shift, self.scale, self.gate = (Row(mod_table, 3 * u + t) for t in range(3))
        self.gain = Row(gain_table, u)


def ffn_forward(x, h, rows, then, w):
    a, b, s = ffn_up(h, w[0], w[1])
    f, xo, *h_next = ffn_down(s, w[2], x, rows.gate, then)
    return xo, (h_next or [None])[0], (x, h, a, b, s, f)


def ffn_backward(u, dxo, df, saved, rows, w, sums, below):
    x, h, a, b, s, _ = saved
    da, db = ffn_bwd_hidden(df, w[2], a, b)
    grads = ffn_grad_weights(h, s, df, da, db)
    dx, sums, *df_below = ffn_bwd_input(da, db, w[0], w[1], x, dxo, rows.gain, rows.scale, sums, u, below)
    return dx, sums, df_below, grads


def mixer_forward(x, h, rows, then, sinks, bias_dil, bias_swa, w):
    proj, qkv = in_proj(h, w[0])
    q_swa, k_swa, v_swa = _swa_rows(qkv[:, 1920:2304] * QK_SCALE), _heads(qkv[:, 2304:2432]), _heads(qkv[:, 2432:2560])
    o_sb, total_sb = sb_forward(qkv)
    bias_dil = bias_dil.reshape(N_GROUPS, 2 * BLK, 2 * BLK)
    views = dilated_views(qkv)
    o_groups, lse_groups = dilated_forward(views, bias_dil)
    o_dil = dilated_merge(o_groups, lse_groups)
    bias_swa = bias_swa.reshape(H_SWA_KV, SWA_GROUP * BLK, 2 * BLK)
    o_swa, lse_swa = banded_forward("swa_forward", q_swa, k_swa, v_swa, bias_swa, _sink_rows(sinks), SWA_HEADS_PER_STEP,
                                    _swa_prev_mask)
    o_cat = jnp.concatenate([o_sb, o_dil, _swa_tokens(o_swa)], axis=1)
    merged = merge_branches(o_cat, w[1], proj)
    mo, xo, *h_next = out_proj(merged, w[2], x, rows.gate, then)
    saved = (x, h, proj, (qkv, total_sb), (views, o_groups, lse_groups),
             (q_swa, k_swa, v_swa, o_swa, lse_swa), o_cat, merged, mo)
    return xo, (h_next or [None])[0], saved


def mixer_backward(u, dxo, dmo, saved, rows, sinks, bias_dil, bias_swa, w, sums, below):
    x, h, proj, sb, dil, swa, o_cat, merged, _ = saved
    tok = pl.BlockSpec((MM_TILE, D_MODEL), lambda j, k: (k, 0))
    g_out = grad_weight("grad_w_out", merged, pl.BlockSpec((MM_TILE, D_SHARD), lambda j, k: (k, j)), dmo, tok,
                        (D_SHARD, D_MODEL))
    du0, du1, du2, dg0, dg1, dg2 = merge_bwd(dmo, w[2], o_cat, w[1], proj)
    du = (du0, du1, du2)
    do_cat = branch_bwd_input(du, w[1])
    g_br = branch_grad_weights(o_cat, du)

    qkv, total_sb = sb
    dq_sb, dk_sb, dv_sb = sb_backward(qkv, total_sb, do_cat)

    views, o_groups, lse_groups = dil
    bias_dil = bias_dil.reshape(N_GROUPS, 2 * BLK, 2 * BLK)
    do_groups, dlse_groups = dilated_merge_bwd(o_groups, lse_groups, do_cat)
    parts, dbias_dil = dilated_backward(views, bias_dil, o_groups, lse_groups, do_groups, dlse_groups)
    dq_dil, dk_dil, dv_dil = dilated_key_grads(parts)
    dbias_dil = dbias_dil.reshape(H_DIL, BLK, 2 * BLK)

    q_swa, k_swa, v_swa, o_swa, lse_swa = swa
    bias_swa = bias_swa.reshape(H_SWA_KV, SWA_GROUP * BLK, 2 * BLK)
    dq_swa, dk_swa, dv_swa, dbias_swa, dsinks = banded_backward(
        "swa_backward", q_swa, k_swa, v_swa, bias_swa, _sink_rows(sinks), o_swa, lse_swa, _swa_rows(do_cat[:, 384:768]),
        jnp.zeros_like(lse_swa), SWA_HEADS_PER_STEP, _swa_prev_mask)
    dbias_swa = dbias_swa.reshape(H_SWA_Q, BLK, 2 * BLK)

    dproj = jnp.concatenate(
        [dq_sb, dk_sb, dv_sb, dq_dil, dk_dil, dv_dil, _swa_tokens(dq_swa), _unheads(dk_swa), _unheads(dv_swa)],
        axis=1).astype(BF16)
    dproj = jnp.concatenate([dproj, dg0, dg1, dg2], axis=1)
    g_in = grad_weight("grad_w_in", h, tok, dproj, pl.BlockSpec((MM_TILE, IN_SHARD), lambda j, k: (k, j)),
                       (D_MODEL, IN_SHARD))
    dx, sums, *df_below = mixer_bwd_input(dproj, w[0], x, dxo, rows.gain, rows.scale, sums, u, below)
    dbias = jnp.concatenate([dbias_dil, dbias_swa], axis=0)
    return dx, sums, df_below, dbias, dsinks[:, :, 0].reshape(H_SWA_Q), (g_in, g_br, g_out)


N_UNITS = 3 * DEPTH


def device_step(x, target, mod, gains, final_gain, sinks, rel_bias, get_weights, put_grads):
    bias_dil, bias_swa, bucket = band_tables(rel_bias)
    mod_table = mod.reshape(3 * N_UNITS, 1, D_MODEL)
    gain_table = gains.reshape(N_UNITS, 1, D_MODEL)
    saved, weights = [], []
    units = [UnitRows(u, mod_table, gain_table) for u in range(N_UNITS)]
    h = prenorm(x, units[0].gain, units[0].scale, units[0].shift)
    for u in range(N_UNITS):
        l, j = divmod(u, 3)
        w = get_weights(u, x)
        then = units[u + 1] if u + 1 < N_UNITS else None
        if j == 1:
            x, h, s = mixer_forward(x, h, units[u], then, sinks[l], bias_dil, bias_swa, w)
        else:
            x, h, s = ffn_forward(x, h, units[u], then, w)
        saved.append(s)
        weights.append(w)
    loss, dx, dfinal = final_loss(x, _vec(final_gain), target)

    sums = (lax.empty((8 * N_UNITS, D_MODEL), F32), lax.empty((8 * N_UNITS, D_MODEL), F32))
    dbias, dsinks = [None] * DEPTH, [None] * DEPTH
    zero = jnp.zeros((1, 1), F32)
    top = N_UNITS - 1
    df, gate_sums = resid_bwd(dx, saved[top][-1], units[top].gate, 0.5, sums[1], top)
    sums = (sums[0], gate_sums)
    for u in reversed(range(N_UNITS)):
        l, j = divmod(u, 3)
        rows = UnitRows(u, mod_table, gain_table + zero)
        below = (saved[u - 1][-1], units[u - 1].gate, 1.0 if (u - 1) % 3 == 1 else 0.5) if u > 0 else None
        if j == 1:
            dx, sums, df, dbias[l], dsinks[l], grads = mixer_backward(
                u, dx, df, saved[u], rows, sinks[l], bias_dil, bias_swa, weights[u], sums, below)
        else:
            dx, sums, df, grads = ffn_backward(u, dx, df, saved[u], rows, weights[u], sums, below)
        df = df[0] if df else None
        if u > 0:
            zero = put_grads(u, grads)
    drel = rel_bias_reduce(dbias[0], dbias[1], bucket)[:, 0, :N_BUCKETS].T
    norm_sums, gate_sums = (t.reshape(DEPTH, 3, 8, D_MODEL) for t in sums)
    dmod = jnp.stack([norm_sums[:, :, 0], norm_sums[:, :, 1], gate_sums[:, :, 0]], axis=2)
    return loss, dx, dmod, norm_sums[:, :, 2], dfinal[0], jnp.stack(dsinks), drel, grads


MESH = pl.DeviceIdType.MESH
CHIP_FLIPS = ((1, 0), (0, 1), (1, 1))
ANY = pl.BlockSpec(memory_space=pl.ANY)


def _position():
    return lax.axis_index("x"), lax.axis_index("y"), lax.axis_index("c")


def all_gather_small(name, piece):
    def body(x_ref, out_ref, send_sems, recv_sems, local_sem):
        x, y, c = _position()
        me, sibling = (x, y, c), (x, y, 1 - c)
        chips = [(x ^ fx, y ^ fy) for fx, fy in CHIP_FLIPS]

        def rows(px, py, pc):
            return out_ref.at[4 * px + 2 * py + pc]

        def copy(k, block, to, src=None):
            return pltpu.make_async_remote_copy(
                src_ref=rows(*block) if src is None else src, dst_ref=rows(*block),
                send_sem=send_sems.at[k], recv_sem=recv_sems.at[k], device_id=to, device_id_type=MESH)

        mine = pltpu.make_async_copy(x_ref, rows(*me), local_sem)
        mine.start()
        first = [copy(0, me, sibling, src=x_ref)]
        first += [copy(1 + j, me, (*chip, c), src=x_ref) for j, chip in enumerate(chips)]
        for cp in first:
            cp.start()
        passed = [copy(4 + j, (*chip, c), sibling) for j, chip in enumerate(chips)]
        for j, chip in enumerate(chips):
            copy(1 + j, (*chip, c), me).wait_recv()
            passed[j].start()
        copy(0, sibling, me).wait_recv()
        for j, chip in enumerate(chips):
            copy(4 + j, (*chip, 1 - c), me).wait_recv()
        for cp in first + passed:
            cp.wait_send()
        mine.wait()

    return pl.pallas_call(
        body, out_shape=jax.ShapeDtypeStruct((N_DEV,) + piece.shape, piece.dtype),
        in_specs=[pl.BlockSpec(memory_space=pltpu.VMEM)], out_specs=pl.BlockSpec(memory_space=pltpu.VMEM),
        scratch_shapes=[pltpu.SemaphoreType.DMA((7,)), pltpu.SemaphoreType.DMA((7,)), pltpu.SemaphoreType.DMA],
        name=name,
    )(piece)


def exchange(name, operands, out_shapes, aliases, plan):
    n_in, n_out = len(operands), len(out_shapes)

    def body(*refs):
        ins, outs = refs[:n_in], refs[n_in:n_in + n_out]
        send_sems, recv_sems, local_sems = refs[n_in + n_out:]
        x, y, c = _position()
        local, sends, recvs = plan(ins, outs, x, y, c)
        local = [pltpu.make_async_copy(s, d, local_sems.at[k]) for k, (s, d) in enumerate(local)]
        for cp in local:
            cp.start()
        remote = [pltpu.make_async_remote_copy(src_ref=s, dst_ref=d, send_sem=send_sems.at[k], recv_sem=recv_sems.at[k],
                                               device_id=dev, device_id_type=MESH)
                  for k, (s, d, dev) in enumerate(sends)]
        for cp in remote:
            cp.start()
        for k, r in enumerate(recvs):
            pltpu.make_async_remote_copy(src_ref=r, dst_ref=r, send_sem=send_sems.at[k], recv_sem=recv_sems.at[k],
                                         device_id=(x, y, c), device_id_type=MESH).wait_recv()
        for cp in remote:
            cp.wait_send()
        for cp in local:
            cp.wait()

    n_sends, n_local = plan.n_sends, max(plan.n_local, 1)
    return pl.pallas_call(
        body, out_shape=out_shapes, in_specs=[ANY] * n_in, out_specs=[ANY] * n_out,
        scratch_shapes=[pltpu.SemaphoreType.DMA((n_sends,)), pltpu.SemaphoreType.DMA((n_sends,)),
                        pltpu.SemaphoreType.DMA((n_local,))],
        input_output_aliases=aliases, name=name,
    )(*operands)


def _plan(n_local, n_sends):
    def wrap(fn):
        fn.n_local, fn.n_sends = n_local, n_sends
        return fn
    return wrap


def _half(ref, axis, c):
    rows = ref.shape[axis] // 2
    idx = [slice(None)] * len(ref.shape)
    idx[axis] = pl.ds(pl.multiple_of(c * rows, 16), rows)
    return ref.at[tuple(idx)]


HBM = pl.BlockSpec(memory_space=pltpu.HBM)
SEM = pl.BlockSpec(memory_space=pltpu.SEMAPHORE)
EFFECT = pltpu.SideEffectType.DATAFLOW_SIDE_EFFECTING


def split_start(name, bufs, extra, n_copies, describe):
    n = len(bufs)

    def body(*refs):
        send_sems, recv_sems = refs[n + len(extra)], refs[n + len(extra) + 1]
        x, y, c = _position()
        for k, (src, dst, _, peer) in enumerate(describe(refs[:n], x, y, c)):
            pltpu.make_async_remote_copy(src_ref=src, dst_ref=dst, send_sem=send_sems.at[k], recv_sem=recv_sems.at[k],
                                         device_id=peer, device_id_type=MESH).start()
        token = refs[-1]
        token[...] = jnp.zeros_like(token)

    out = pl.pallas_call(
        body, name=name,
        out_shape=(pltpu.SemaphoreType.DMA((n_copies,)), pltpu.SemaphoreType.DMA((n_copies,)),
                   *[pltpu.HBM(b.shape, b.dtype) for b in bufs], jax.ShapeDtypeStruct((8, 128), F32)),
        in_specs=[HBM] * n + [ANY] * len(extra),
        out_specs=(SEM, SEM, *[HBM] * n, pl.BlockSpec(memory_space=pltpu.VMEM)),
        input_output_aliases={k: 2 + k for k in range(n)},
        compiler_params=pltpu.CompilerParams(has_side_effects=EFFECT),
    )(*[pltpu.with_memory_space_constraint(b, pltpu.HBM) for b in bufs], *extra)
    return out[0], out[1], list(out[2:2 + n]), out[-1]


def split_wait(name, bufs, send_sems, recv_sems, after, describe):
    n = len(bufs)

    def body(*refs):
        send, recv = refs[n], refs[n + 1]
        x, y, c = _position()
        for k, (src, _, dst, peer) in enumerate(describe(refs[:n], x, y, c)):
            copy = pltpu.make_async_remote_copy(src_ref=src, dst_ref=dst, send_sem=send.at[k], recv_sem=recv.at[k],
                                                device_id=peer, device_id_type=MESH)
            copy.wait_send()
            copy.wait_recv()

    out = pl.pallas_call(
        body, name=name, out_shape=[pltpu.HBM(b.shape, b.dtype) for b in bufs],
        in_specs=[HBM] * n + [SEM, SEM] + [ANY] * len(after), out_specs=[HBM] * n,
        input_output_aliases={k: k for k in range(n)},
        compiler_params=pltpu.CompilerParams(has_side_effects=EFFECT),
    )(*bufs, send_sems, recv_sems, *after)
    return list(out)


def _row_tile(rows, cols, max_elements=256 * 1024):
    best = 16
    for t in range(16, rows + 1, 16):
        if rows % t == 0 and t * cols <= max_elements:
            best = t
    return best


def cast_into_slots(name, shards, chip):
    n = len(shards)
    rows, cols = shards[0][0].shape[-2:]
    tr = _row_tile(rows, cols)

    def body(chip_ref, *refs):
        del chip_ref
        for k in range(n):
            refs[n + k][...] = refs[k][...].astype(BF16)

    def in_spec(param, index):
        return pl.BlockSpec((None,) * len(index) + (tr, cols), lambda r, chip_ref: index + (r, 0))

    return pl.pallas_call(
        body, out_shape=[jax.ShapeDtypeStruct((N_CHIPS, rows, cols), BF16)] * n,
        grid_spec=pltpu.PrefetchScalarGridSpec(
            num_scalar_prefetch=1, grid=(rows // tr,),
            in_specs=[in_spec(p, idx) for p, idx in shards],
            out_specs=[pl.BlockSpec((None, tr, cols), lambda r, chip_ref: (chip_ref[0], r, 0))] * n),
        name=name, compiler_params=_params(("parallel",)),
    )(chip, *[p for p, _ in shards])


GATHER_STAGES = ((0,), (1,), (2,), (3, 4, 5))
REDUCE_STAGES = ((5, 4, 3), (2,), (1,), (0,))


def _gather_copies(slots, x, y, c):
    me = 2 * x + y
    out = []
    for s in slots:
        for fx, fy in CHIP_FLIPS:
            mine = _half(s.at[me], 0, c)
            out.append((mine, mine, _half(s.at[2 * (x ^ fx) + (y ^ fy)], 0, c), (x ^ fx, y ^ fy, c)))
    return out


class WeightStream:
    def __init__(self, shards, chip, after=()):
        self.pending, self.ready = {}, {}
        token = tuple(after)
        for si, units in enumerate(GATHER_STAGES):
            slots = []
            for u in units:
                same = len({p.shape[-2:] for p, _ in shards[u]}) == 1
                for t, group in enumerate([shards[u]] if same else [[s] for s in shards[u]]):
                    slots += cast_into_slots(f"cast_{u}_{t}", group, chip)
            send, recv, slots, tok = split_start(f"gather_start_{si}", slots, token, 3 * len(slots), _gather_copies)
            self.pending[si] = (send, recv, slots)
            token = (tok,)
        self.token = token

    def get(self, u, after):
        if u not in self.ready:
            si = next(k for k, units in enumerate(GATHER_STAGES) if u in units)
            send, recv, slots = self.pending.pop(si)
            slots = split_wait(f"gather_wait_{si}", slots, send, recv, (after,) + self.token, _gather_copies)
            self.token = ()

            @_plan(0, 3 * len(slots))
            def to_sibling(ins, outs, x, y, c):
                sends, recvs = [], []
                for o in outs:
                    for fx, fy in CHIP_FLIPS:
                        slab = o.at[2 * (x ^ fx) + (y ^ fy)]
                        sends.append((_half(slab, 0, c), _half(slab, 0, c), (x, y, 1 - c)))
                        recvs.append(_half(slab, 0, 1 - c))
                return [], sends, recvs

            shapes = [jax.ShapeDtypeStruct(s.shape, BF16) for s in slots]
            slots = exchange(f"gather_sibling_{si}", slots, shapes, {k: k for k in range(len(slots))}, to_sibling)
            for i, v in enumerate(GATHER_STAGES[si]):
                self.ready[v] = tuple(slots[3 * i:3 * i + 3])
        return self.ready[u]


def _reduce_copies(bufs, x, y, c):
    n = len(bufs) // 2
    out = []
    for s, land in zip(bufs[:n], bufs[n:]):
        for k, (fx, fy) in enumerate(CHIP_FLIPS):
            out.append((s.at[2 * (x ^ fx) + (y ^ fy)], land.at[k], land.at[k], (x ^ fx, y ^ fy, c)))
    return out


GRAD_SLOTS = {"gate": (2 * DEPTH, FF_SHARD, D_MODEL), "up": (2 * DEPTH, FF_SHARD, D_MODEL),
              "down": (2 * DEPTH, FF_SHARD, D_MODEL), "in": (DEPTH, D_MODEL, IN_SHARD),
              "br": (DEPTH, BR_ROWS, D_SHARD), "out": (DEPTH, D_SHARD, D_MODEL)}


def _unit_tensors(u):
    l, j = divmod(u, 3)
    if j == 1:
        return [("in", l), ("br", l), ("out", l)]
    return [(k, 2 * l + j // 2) for k in ("gate", "up", "down")]


class GradStream:
    def __init__(self, chip, core):
        self.core = core
        self.place = jnp.concatenate([chip, core])
        self.held, self.flying = {}, []
        self.full = {k: lax.empty(shape, F32) for k, shape in GRAD_SLOTS.items()}

    def put(self, u, grads, after=()):
        self.held[u] = grads
        si = len(self.flying)
        units = REDUCE_STAGES[si]
        if not all(v in self.held for v in units):
            return jnp.zeros((1, 1), F32)
        gs = [g for v in units for g in self.held[v]]

        @_plan(0, len(gs))
        def swap_halves(ins, outs, x, y, c):
            sends = [(_half(g, 1, 1 - c), o, (x, y, 1 - c)) for g, o in zip(ins, outs)]
            return [], sends, list(outs)

        half_shapes = [jax.ShapeDtypeStruct((N_CHIPS, g.shape[1] // 2, g.shape[2]), BF16) for g in gs]
        landed = exchange(f"reduce_swap_{si}", gs + list(after), half_shapes, {}, swap_halves)
        sums = [None] * len(gs)
        for run in _same_shape_runs(gs):
            for k, s in zip(run, _add_halves([gs[k] for k in run], [landed[k] for k in run], self.core)):
                sums[k] = s
        landing = [lax.empty((3,) + s.shape[1:], BF16) for s in sums]
        send, recv, bufs, token = split_start(f"reduce_start_{si}", sums + landing, (), 3 * len(sums), _reduce_copies)
        self.flying.append((send, recv, bufs, [t for v in units for t in _unit_tensors(v)]))
        return token[0:1, 0:1]

    def finish(self, after):
        for si, (send, recv, bufs, tensors) in enumerate(self.flying):
            bufs = split_wait(f"reduce_wait_{si}", bufs, send, recv, tuple(after), _reduce_copies)
            n = len(tensors)
            for run in _same_shape_runs(bufs[:n]):
                names = [tensors[k][0] for k in run]
                out = _add_chips([bufs[k] for k in run], [bufs[n + k] for k in run], self.place,
                                 [self.full[t] for t in names], [tensors[k][1] for k in run])
                self.full.update(zip(names, out))
        names = list(self.full)

        @_plan(0, len(names))
        def share_halves(ins, outs, x, y, c):
            sends = [(_half(o, 1, c), _half(o, 1, c), (x, y, 1 - c)) for o in outs]
            return [], sends, [_half(o, 1, 1 - c) for o in outs]

        shapes = [jax.ShapeDtypeStruct(self.full[k].shape, F32) for k in names]
        out = exchange("reduce_share_halves", [self.full[k] for k in names], shapes, {k: k for k in range(len(names))},
                       share_halves)
        return dict(zip(names, out))


def _same_shape_runs(arrays, longest=3):
    runs = []
    for k, a in enumerate(arrays):
        if runs and len(runs[-1]) < longest and arrays[runs[-1][0]].shape == a.shape:
            runs[-1].append(k)
        else:
            runs.append([k])
    return runs


def _add_halves(gs, landeds, core):
    n = len(gs)
    _, rh, cols = landeds[0].shape
    tr = _row_tile(rh, cols, 1024 * 1024)
    per_half = rh // tr

    def body(core_ref, *refs):
        del core_ref
        for k in range(n):
            refs[2 * n + k][...] = (refs[k][...].astype(F32) + refs[n + k][...].astype(F32)).astype(BF16)

    blk = (None, tr, cols)
    landed_spec = pl.BlockSpec(blk, lambda j, r, core_ref: (j, r, 0))
    return pl.pallas_call(
        body, out_shape=[jax.ShapeDtypeStruct(landeds[0].shape, BF16)] * n,
        grid_spec=pltpu.PrefetchScalarGridSpec(
            num_scalar_prefetch=1, grid=(N_CHIPS, per_half),
            in_specs=[pl.BlockSpec(blk, lambda j, r, core_ref: (j, core_ref[0] * per_half + r, 0))] * n
            + [landed_spec] * n,
            out_specs=[landed_spec] * n),
        name="reduce_add_halves", compiler_params=_params(("parallel", "parallel")),
    )(core, *gs, *landeds)


def _add_chips(sums, landeds, place, fulls, slots):
    n = len(sums)
    _, rh, cols = sums[0].shape
    tr = _row_tile(rh, cols, 1024 * 1024)
    per_half = rh // tr

    def body(place_ref, *refs):
        del place_ref
        for k in range(n):
            s_ref, la_ref, o_ref = refs[k], refs[n + k], refs[3 * n + k]
            o_ref[...] = ((s_ref[...].astype(F32) + la_ref[0].astype(F32)) + la_ref[1].astype(F32)) + la_ref[2].astype(F32)

    def out_spec(slot):
        return pl.BlockSpec((None, tr, cols), lambda r, place_ref: (slot, place_ref[1] * per_half + r, 0))

    return pl.pallas_call(
        body, out_shape=[jax.ShapeDtypeStruct(f.shape, F32) for f in fulls],
        grid_spec=pltpu.PrefetchScalarGridSpec(
            num_scalar_prefetch=1, grid=(per_half,),
            in_specs=[pl.BlockSpec((None, tr, cols), lambda r, place_ref: (place_ref[0], r, 0))] * n
            + [pl.BlockSpec((3, tr, cols), lambda r, place_ref: (0, r, 0))] * n + [ANY] * n,
            out_specs=[out_spec(slot) for slot in slots]),
        input_output_aliases={1 + 2 * n + k: k for k in range(n)}, name="reduce_add_chips",
        compiler_params=_params(("parallel",)),
    )(place, *sums, *landeds, *fulls)


def sum_devices(parts):
    def body(p_ref, o_ref):
        acc = p_ref[0]
        for d in range(1, N_DEV):
            acc = acc + p_ref[d]
        o_ref[...] = acc

    return pl.pallas_call(body, out_shape=jax.ShapeDtypeStruct(parts.shape[1:], F32), name="sum_devices")(parts)


ADA_SHARD = 9 * D_MODEL // N_CHIPS
ADA_TILE = 768
ADA_ROWS = 16


def ada_forward(c_rows, w_ada, b_shard):
    def body(c_ref, w_ref, b_ref, o_ref):
        cv = c_ref[...]
        o_ref[...] = _dot((cv * _sigmoid(cv)).astype(BF16), w_ref[...].astype(BF16), NN) + b_ref[...]

    return pl.pallas_call(
        body, grid=(DEPTH, ADA_SHARD // ADA_TILE),
        in_specs=[pl.BlockSpec((ADA_ROWS, D_MODEL), lambda l, n: (0, 0)),
                  pl.BlockSpec((None, D_MODEL, ADA_TILE), lambda l, n: (l, 0, n)),
                  pl.BlockSpec((None, 1, ADA_TILE), lambda l, n: (l, 0, n))],
        out_specs=pl.BlockSpec((None, ADA_ROWS, ADA_TILE), lambda l, n: (l, 0, n)),
        out_shape=jax.ShapeDtypeStruct((DEPTH, ADA_ROWS, ADA_SHARD), F32),
        name="ada_forward", compiler_params=_params(("parallel", "parallel")),
    )(c_rows, w_ada, b_shard)


def ada_backward(c_rows, dmod_rows):
    def body(c_ref, d_ref, o_ref):
        cv = c_ref[...]
        o_ref[...] = _dot((cv * _sigmoid(cv)).astype(BF16), d_ref[...].astype(BF16), TN)

    return pl.pallas_call(
        body, grid=(DEPTH, ADA_SHARD // ADA_TILE),
        in_specs=[pl.BlockSpec((ADA_ROWS, D_MODEL), lambda l, n: (0, 0)),
                  pl.BlockSpec((None, ADA_ROWS, ADA_TILE), lambda l, n: (l, 0, n))],
        out_specs=pl.BlockSpec((None, D_MODEL, ADA_TILE), lambda l, n: (l, 0, n)),
        out_shape=jax.ShapeDtypeStruct((DEPTH, D_MODEL, ADA_SHARD), F32),
        name="ada_backward", compiler_params=_params(("parallel", "parallel")),
    )(c_rows, dmod_rows)


def adamw(name, w, g, m, v):
    shape = w.shape
    cols = shape[-1]
    rows = w.size // cols
    tr = _row_tile(rows, cols) if rows % 16 == 0 else rows
    c1 = 1.0 / (1.0 - ADAM_B1 ** ADAM_STEP)
    c2 = 1.0 / (1.0 - ADAM_B2 ** ADAM_STEP)

    def body(w_ref, g_ref, m_ref, v_ref, go_ref, d_ref, mo_ref, vo_ref):
        gv = g_ref[...]
        mn = ADAM_B1 * m_ref[...] + (1.0 - ADAM_B1) * gv
        vn = ADAM_B2 * v_ref[...] + (1.0 - ADAM_B2) * (gv * gv)
        go_ref[...] = gv
        mo_ref[...] = mn
        vo_ref[...] = vn
        d_ref[...] = -ADAM_LR * ((mn * c1) / (jnp.sqrt(vn * c2) + ADAM_EPS) + ADAM_WD * w_ref[...])

    spec = pl.BlockSpec((tr, cols), lambda i: (i, 0))
    out = jax.ShapeDtypeStruct((rows, cols), F32)
    res = pl.pallas_call(
        body, grid=(rows // tr,), in_specs=[spec] * 4, out_specs=[spec] * 4, out_shape=[out] * 4,
        name=name, compiler_params=_params(("parallel",)),
    )(*[t.reshape(rows, cols) for t in (w, g, m, v)])
    return tuple(r.reshape(shape) for r in res)


def _pack(parts, rows):
    flat = jnp.concatenate([p.reshape(-1) for p in parts])
    return jnp.pad(flat, (0, rows * 128 - flat.size)).reshape(rows, 128)


def _unpack(flat, shapes):
    out, at = [], 0
    for s in shapes:
        n = math.prod(s)
        out.append(flat[at:at + n].reshape(s))
        at += n
    return out


def kernel(x, c, w_ada, b_ada, norm_gain, w_ffn_gate, w_ffn_up, w_ffn_down, w_in, w_br_sb, w_br_dil, w_br_swa, w_out, sinks, rel_bias, final_gain, loss_target, m_w_ada, m_b_ada, m_norm_gain, m_w_ffn_gate, m_w_ffn_up, m_w_ffn_down, m_w_in, m_w_br_sb, m_w_br_dil, m_w_br_swa, m_w_out, m_sinks, m_rel_bias, m_final_gain, v_w_ada, v_b_ada, v_norm_gain, v_w_ffn_gate, v_w_ffn_up, v_w_ffn_down, v_w_in, v_w_br_sb, v_w_br_dil, v_w_br_swa, v_w_out, v_sinks, v_rel_bias, v_final_gain):
    xi, yi, ci = _position()
    chip = 2 * xi + yi
    dev = 2 * chip + ci

    c_all = all_gather_small("gather_c", c.reshape(8, 128)).reshape(N_DEV, D_MODEL)
    c_rows = jnp.pad(c_all, ((0, ADA_ROWS - N_DEV), (0, 0)))
    b_shard = lax.dynamic_slice_in_dim(b_ada, chip * ADA_SHARD, ADA_SHARD, axis=1).reshape(DEPTH, 1, ADA_SHARD)
    mod_shard = ada_forward(c_rows, w_ada, b_shard)[:, :N_DEV]
    n_mod = DEPTH * N_DEV * ADA_SHARD
    gathered = all_gather_small("gather_mod", _pack([mod_shard, norm_gain], 304))[::2].reshape(N_CHIPS, -1)
    mod_all = gathered[:, :n_mod].reshape(N_CHIPS, DEPTH, N_DEV, ADA_SHARD)
    mod = lax.dynamic_index_in_dim(mod_all, dev, axis=2, keepdims=False)
    mod = mod.transpose(1, 0, 2).reshape(DEPTH, 3, 3, D_MODEL)
    gains = gathered[:, n_mod:n_mod + DEPTH * 3 * D_SHARD].reshape(N_CHIPS, DEPTH, 3, D_SHARD)
    gains = gains.transpose(1, 2, 0, 3).reshape(DEPTH, 3, D_MODEL)

    chip_i, core_i = chip.astype(jnp.int32).reshape(1), ci.astype(jnp.int32).reshape(1)
    w_br = jnp.concatenate([w_br_sb, w_br_dil, w_br_swa], axis=1)
    transposed = (3, 4)
    w_gate_t, w_up_t = jnp.swapaxes(w_ffn_gate, 2, 3), jnp.swapaxes(w_ffn_up, 2, 3)
    shards = []
    for l in range(DEPTH):
        ffn = [[(w_gate_t, (l, f)), (w_up_t, (l, f)), (w_ffn_down, (l, f))] for f in range(2)]
        shards += [ffn[0], [(w_in, (l,)), (w_br, (l,)), (w_out, (l,))], ffn[1]]
    weights_in = WeightStream(shards, chip_i, (gathered,))
    grads_out = GradStream(chip_i, core_i)

    loss, dx, dmod, dgains, dfinal, dsinks, drel, last_grads = device_step(
        x[0], loss_target[0], mod, gains, final_gain, sinks, rel_bias, weights_in.get, grads_out.put)

    small_shapes = [(DEPTH, 9 * D_MODEL), (DEPTH, 3, D_MODEL), (D_MODEL,), (DEPTH, H_SWA_Q), (N_BUCKETS, 12), (1,)]
    small_all = all_gather_small("gather_small_grads", _pack([dmod, dgains, dfinal, dsinks, drel, loss[0, 0:1]], 208))
    started = grads_out.put(0, last_grads, after=(small_all,))
    small_all = small_all + started
    g_b_ada, g_gain_full, g_final, g_sinks, g_rel, loss_sum = _unpack(sum_devices(small_all).reshape(-1), small_shapes)
    g_gain = lax.dynamic_slice_in_dim(g_gain_full, chip * D_SHARD, D_SHARD, axis=2)
    dmod_all = small_all.reshape(N_DEV, -1)[:, :DEPTH * 9 * D_MODEL].reshape(N_DEV, DEPTH, 9 * D_MODEL)
    dmod_rows = lax.dynamic_slice_in_dim(dmod_all, chip * ADA_SHARD, ADA_SHARD, axis=2).transpose(1, 0, 2)
    g_w_ada = ada_backward(c_rows, jnp.pad(dmod_rows, ((0, 0), (0, ADA_ROWS - N_DEV), (0, 0))))

    weights = [w_ada, b_ada, norm_gain, w_ffn_gate, w_ffn_up, w_ffn_down, w_in, w_br_sb, w_br_dil, w_br_swa, w_out,
               sinks, rel_bias, final_gain]
    ms = [m_w_ada, m_b_ada, m_norm_gain, m_w_ffn_gate, m_w_ffn_up, m_w_ffn_down, m_w_in, m_w_br_sb, m_w_br_dil,
          m_w_br_swa, m_w_out, m_sinks, m_rel_bias, m_final_gain]
    vs = [v_w_ada, v_b_ada, v_norm_gain, v_w_ffn_gate, v_w_ffn_up, v_w_ffn_down, v_w_in, v_w_br_sb, v_w_br_dil,
          v_w_br_swa, v_w_out, v_sinks, v_rel_bias, v_final_gain]
    grads = [g_w_ada, g_b_ada, g_gain] + [None] * 8 + [g_sinks, g_rel, g_final]

    deltas, new_ms, new_vs = [None] * 14, [None] * 14, [None] * 14
    for k in (0, 1, 2, 11, 12, 13):
        _, deltas[k], new_ms[k], new_vs[k] = adamw(f"adamw_{k}", weights[k], grads[k], ms[k], vs[k])

    g = grads_out.finish((dx, deltas[0], deltas[1]))
    g_br = g["br"]
    grads[3:11] = [g["gate"].reshape(w_gate_t.shape), g["up"].reshape(w_up_t.shape),
                   g["down"].reshape(w_ffn_down.shape), g["in"], g_br[:, 0:256], g_br[:, 256:384], g_br[:, 384:768],
                   g["out"]]
    for k in range(3, 11):
        state = [weights[k], ms[k], vs[k]]
        if k in transposed:
            state = [jnp.swapaxes(t, 2, 3) for t in state]
        out = adamw(f"adamw_{k}", state[0], grads[k], state[1], state[2])
        if k in transposed:
            out = [jnp.swapaxes(t, 2, 3) for t in out]
        grads[k], deltas[k], new_ms[k], new_vs[k] = out
    return (loss_sum[0], dx[None], *grads, *deltas, *new_ms, *new_vs)
```

```python
import functools
import math

import jax
import jax.numpy as jnp
from jax import lax
from jax.experimental import pallas as pl
from jax.experimental.pallas import tpu as pltpu

F32 = jnp.float32
BF16 = jnp.bfloat16

D_MODEL = 1024
SEQ = 2048
DEPTH = 2
HEAD_DIM = 64
BLK = 128
H_SB = 4
DIL_PATTERNS = ((128, 1), (512, 4), (2048, 16))
H_PER_DIL = 2
H_DIL = 6
H_SWA_Q = 6
H_SWA_KV = 2
SWA_WINDOW = 128
N_BUCKETS = 32
MAX_REL_DIST = 2048
D_FF = 2816
RMS_EPS = 1e-6
N_CHIPS = 4
N_DEV = 8
FF_SHARD = D_FF // N_CHIPS
D_QKV = 2560
D_IN = D_QKV + 3 * D_MODEL
IN_SHARD = D_IN // N_CHIPS
D_SHARD = D_MODEL // N_CHIPS
BR_ROWS = 768
NEG = -1e30
QK_SCALE = HEAD_DIM ** -0.5

ADAM_LR = 0.001
ADAM_B1 = 0.9
ADAM_B2 = 0.999
ADAM_EPS = 1e-08
ADAM_WD = 0.01
ADAM_STEP = 10

VMEM_LIMIT = 56 * 1024 * 1024
ROW_TILE = 256
MM_TILE = 1024

NN = (((1,), (0,)), ((), ()))
NT = (((1,), (1,)), ((), ()))
TN = (((0,), (0,)), ((), ()))


def _params(sem=None):
    return pltpu.CompilerParams(dimension_semantics=sem, vmem_limit_bytes=VMEM_LIMIT)


def _dot(a, b, dims):
    return lax.dot_general(a, b, dims, preferred_element_type=F32)


def _sigmoid(x):
    return 1.0 / (1.0 + jnp.exp(-x))


def _matmul(name, grid, nk, k_axis, dims, n_pairs, in_specs, out_specs, out_shape, acc_shape, epilogue,
            operands, sem, aliases=None, prologue=None):
    n_in = len(in_specs)
    n_out = len(out_specs)

    def partial(ins):
        tot = None
        for p in range(n_pairs):
            a = ins[2 * p][...]
            if prologue is not None:
                a = prologue(p, a, ins)
            d = _dot(a, ins[2 * p + 1][...], dims)
            tot = d if tot is None else tot + d
        return tot

    def body(*refs):
        ins, outs = refs[:n_in], refs[n_in:n_in + n_out]
        ids = tuple(pl.program_id(a) for a in range(len(grid)))
        if nk == 1:
            epilogue(partial(ins), ins, outs, ids)
            return
        acc = refs[n_in + n_out]
        k = ids[k_axis]

        @pl.when(k == 0)
        def _():
            acc[...] = partial(ins)

        @pl.when(k > 0)
        def _():
            acc[...] += partial(ins)

        @pl.when(k == nk - 1)
        def _():
            epilogue(acc[...], ins, outs, ids)

    return pl.pallas_call(
        body, grid=grid, in_specs=in_specs, out_specs=out_specs, out_shape=out_shape,
        scratch_shapes=[] if nk == 1 else [pltpu.VMEM(acc_shape, F32)],
        input_output_aliases=aliases or {}, name=name, compiler_params=_params(sem),
    )(*operands)


def _row_spec(width=D_MODEL):
    return pl.BlockSpec((ROW_TILE, width), lambda i: (i, 0))


def _vec_spec(rows=1, width=D_MODEL):
    return pl.BlockSpec((rows, width), lambda i: (0, 0))


class Row:
    def __init__(self, table, index):
        self.table, self.index = table, index

    def spec(self):
        index = self.index
        return pl.BlockSpec((None, 1, D_MODEL), lambda *ids: (index, 0, 0))


def _slot_spec(u):
    return pl.BlockSpec((8, D_MODEL), lambda *ids: (u, 0))


def prenorm(x, gain, scale, shift):
    def body(x_ref, g_ref, sc_ref, sh_ref, h_ref):
        xv = x_ref[...]
        r = lax.rsqrt(jnp.mean(xv * xv, axis=-1, keepdims=True) + RMS_EPS)
        h_ref[...] = (((xv * r) * g_ref[...]) * (1.0 + sc_ref[...]) + sh_ref[...]).astype(BF16)

    return pl.pallas_call(
        body, grid=(SEQ // ROW_TILE,), in_specs=[_row_spec(), gain.spec(), scale.spec(), shift.spec()],
        out_specs=_row_spec(), out_shape=jax.ShapeDtypeStruct((SEQ, D_MODEL), BF16),
        name="prenorm", compiler_params=_params(("parallel",)),
    )(x, gain.table, scale.table, shift.table)


def resid_bwd(dxo, f, coef, mult, sums, u):
    def body(dx_ref, f_ref, c_ref, sums_in, df_ref, dc_ref):
        del sums_in
        dx = dx_ref[...]
        df_ref[...] = (dx * (mult * c_ref[...])).astype(BF16)
        part = mult * jnp.sum(dx * f_ref[...], axis=0, keepdims=True)

        @pl.when(pl.program_id(0) == 0)
        def _():
            dc_ref[...] = jnp.zeros_like(dc_ref)

        dc_ref[0:1, :] += part

    return pl.pallas_call(
        body, grid=(SEQ // ROW_TILE,),
        in_specs=[_row_spec(), _row_spec(), coef.spec(), pl.BlockSpec(memory_space=pl.ANY)],
        out_specs=[_row_spec(), _slot_spec(u)],
        out_shape=[jax.ShapeDtypeStruct((SEQ, D_MODEL), BF16), jax.ShapeDtypeStruct(sums.shape, F32)],
        input_output_aliases={3: 1}, name="resid_bwd", compiler_params=_params(("arbitrary",)),
    )(dxo, f, coef.table, sums)


def final_loss(x, gain, target):
    def body(x_ref, g_ref, t_ref, loss_ref, dx_ref, dg_ref):
        xv = x_ref[...]
        g = g_ref[...]
        r = lax.rsqrt(jnp.mean(xv * xv, axis=-1, keepdims=True) + RMS_EPS)
        xh = xv * r
        e = xh * g - t_ref[...]
        part = 0.5 * jnp.sum(jnp.mean(e * e, axis=-1, keepdims=True), axis=0, keepdims=True)
        dy = e * (1.0 / D_MODEL)
        dyg = dy * g
        dx_ref[...] = r * (dyg - xh * jnp.mean(dyg * xh, axis=-1, keepdims=True))

        @pl.when(pl.program_id(0) == 0)
        def _():
            loss_ref[...] = jnp.zeros_like(loss_ref)
            dg_ref[...] = jnp.zeros_like(dg_ref)

        loss_ref[...] += jnp.broadcast_to(part, loss_ref.shape)
        dg_ref[0:1, :] += jnp.sum(dy * xh, axis=0, keepdims=True)

    return pl.pallas_call(
        body, grid=(SEQ // ROW_TILE,), in_specs=[_row_spec(), _vec_spec(), _row_spec()],
        out_specs=[_vec_spec(8, 128), _row_spec(), _vec_spec(8)],
        out_shape=[jax.ShapeDtypeStruct((8, 128), F32), jax.ShapeDtypeStruct((SEQ, D_MODEL), F32),
                   jax.ShapeDtypeStruct((8, D_MODEL), F32)],
        name="final_loss", compiler_params=_params(("arbitrary",)),
    )(x, gain, target)


def _prenorm_bwd_epilogue(dh, x_ref, dxo_ref, g_ref, sc_ref, dx_ref, stats_ref, first):
    xv = x_ref[...]
    g = g_ref[...]
    r = lax.rsqrt(jnp.mean(xv * xv, axis=-1, keepdims=True) + RMS_EPS)
    xh = xv * r
    dn = dh * (1.0 + sc_ref[...])
    dxh = dn * g
    dx = dxo_ref[...] + r * (dxh - xh * jnp.mean(dxh * xh, axis=-1, keepdims=True))
    dx_ref[...] = dx

    @pl.when(first)
    def _():
        stats_ref[...] = jnp.zeros_like(stats_ref)

    stats_ref[0:1, :] += jnp.sum(dh, axis=0, keepdims=True)
    stats_ref[1:2, :] += jnp.sum(dh * (xh * g), axis=0, keepdims=True)
    stats_ref[2:3, :] += jnp.sum(dn * xh, axis=0, keepdims=True)
    return dx


def _resid_bwd_epilogue(dx, f_ref, c_ref, mult, df_ref, dc_ref, first):
    df_ref[...] = (dx * (mult * c_ref[...])).astype(BF16)

    @pl.when(first)
    def _():
        dc_ref[...] = jnp.zeros_like(dc_ref)

    dc_ref[0:1, :] += mult * jnp.sum(dx * f_ref[...], axis=0, keepdims=True)


def ffn_up(h, wg_all, wu_all):
    def body(h_ref, wg_ref, wu_ref, a_ref, b_ref, s_ref):
        hv = h_ref[...]
        a = _dot(hv, wg_ref[...], NT)
        b = _dot(hv, wu_ref[...], NT)
        a_ref[...] = a.astype(BF16)
        b_ref[...] = b.astype(BF16)
        s_ref[...] = (a * _sigmoid(a) * b).astype(BF16)

    w_spec = pl.BlockSpec((None, FF_SHARD, D_MODEL), lambda j, i: (j, 0, 0))
    o_spec = pl.BlockSpec((None, MM_TILE, FF_SHARD), lambda j, i: (j, i, 0))
    hid = (N_CHIPS, SEQ, FF_SHARD)
    return pl.pallas_call(
        body, grid=(N_CHIPS, SEQ // MM_TILE),
        in_specs=[pl.BlockSpec((MM_TILE, D_MODEL), lambda j, i: (i, 0)), w_spec, w_spec],
        out_specs=[o_spec, o_spec, o_spec],
        out_shape=[jax.ShapeDtypeStruct(hid, BF16)] * 3,
        name="ffn_up", compiler_params=_params(("parallel", "parallel")),
    )(h, wg_all, wu_all)


def matmul_residual(name, a, a_spec, w_all, w_spec, x, coef, mult, then=None):
    def epilogue(acc, ins, outs, ids):
        outs[0][...] = acc
        xo = ins[2][...] + (mult * ins[3][...]) * acc
        outs[1][...] = xo
        if then is not None:
            r = lax.rsqrt(jnp.mean(xo * xo, axis=-1, keepdims=True) + RMS_EPS)
            outs[2][...] = (((xo * r) * ins[4][...]) * (1.0 + ins[5][...]) + ins[6][...]).astype(BF16)

    row = pl.BlockSpec((MM_TILE, D_MODEL), lambda i, j: (i, 0))
    f32 = jax.ShapeDtypeStruct((SEQ, D_MODEL), F32)
    extra = [] if then is None else [then.gain, then.scale, then.shift]
    return _matmul(
        name, (SEQ // MM_TILE, N_CHIPS), N_CHIPS, 1, NN, 1,
        [a_spec, w_spec, row, coef.spec()] + [t.spec() for t in extra], [row] * (2 + bool(extra)),
        [f32, f32] + [jax.ShapeDtypeStruct((SEQ, D_MODEL), BF16)] * bool(extra), (MM_TILE, D_MODEL), epilogue,
        (a, w_all, x, coef.table) + tuple(t.table for t in extra), ("parallel", "arbitrary"))


def ffn_down(s, wd_all, x, gate, then):
    return matmul_residual(
        "ffn_down", s, pl.BlockSpec((None, MM_TILE, FF_SHARD), lambda i, j: (j, i, 0)),
        wd_all, pl.BlockSpec((None, FF_SHARD, D_MODEL), lambda i, j: (j, 0, 0)), x, gate, 0.5, then)


def ffn_bwd_hidden(df, wd_all, a, b):
    def epilogue(ds, ins, outs, ids):
        av, bv = ins[2][...].astype(F32), ins[3][...].astype(F32)
        sig = _sigmoid(av)
        outs[0][...] = (ds * bv * (sig * (1.0 + av * (1.0 - sig)))).astype(BF16)
        outs[1][...] = (ds * (av * sig)).astype(BF16)

    hid_spec = pl.BlockSpec((None, MM_TILE, FF_SHARD), lambda j, i: (j, i, 0))
    hid = jax.ShapeDtypeStruct((N_CHIPS, SEQ, FF_SHARD), BF16)
    return _matmul(
        "ffn_bwd_hidden", (N_CHIPS, SEQ // MM_TILE), 1, None, NT, 1,
        [pl.BlockSpec((MM_TILE, D_MODEL), lambda j, i: (i, 0)),
         pl.BlockSpec((None, FF_SHARD, D_MODEL), lambda j, i: (j, 0, 0)), hid_spec, hid_spec],
        [hid_spec, hid_spec], [hid, hid], None, epilogue, (df, wd_all, a, b), ("parallel", "parallel"))


def grad_weight(name, lhs, lhs_spec, rhs, rhs_spec, shape):
    def epilogue(acc, ins, outs, ids):
        outs[0][...] = acc.astype(BF16)

    return _matmul(
        name, (N_CHIPS, SEQ // MM_TILE), SEQ // MM_TILE, 1, TN, 1,
        [lhs_spec, rhs_spec], [pl.BlockSpec((None,) + shape, lambda j, k: (j, 0, 0))],
        [jax.ShapeDtypeStruct((N_CHIPS,) + shape, BF16)], shape, epilogue, (lhs, rhs), ("parallel", "arbitrary"))[0]


def ffn_grad_weights(h, s, df, da, db):
    tok = pl.BlockSpec((MM_TILE, D_MODEL), lambda j, k: (k, 0))
    hid = pl.BlockSpec((None, MM_TILE, FF_SHARD), lambda j, k: (j, k, 0))
    n_k = SEQ // MM_TILE

    def body(da_ref, db_ref, h_ref, gg_ref, gu_ref, acc_g, acc_u):
        k = pl.program_id(1)
        hv = h_ref[...]
        parts = (_dot(da_ref[...], hv, TN), _dot(db_ref[...], hv, TN))

        @pl.when(k == 0)
        def _():
            acc_g[...], acc_u[...] = parts

        @pl.when(k > 0)
        def _():
            acc_g[...] += parts[0]
            acc_u[...] += parts[1]

        @pl.when(k == n_k - 1)
        def _():
            gg_ref[...] = acc_g[...].astype(BF16)
            gu_ref[...] = acc_u[...].astype(BF16)

    out = pl.BlockSpec((None, FF_SHARD, D_MODEL), lambda j, k: (j, 0, 0))
    shape = jax.ShapeDtypeStruct((N_CHIPS, FF_SHARD, D_MODEL), BF16)
    g_gate, g_up = pl.pallas_call(
        body, grid=(N_CHIPS, n_k), in_specs=[hid, hid, tok], out_specs=[out, out], out_shape=[shape, shape],
        scratch_shapes=[pltpu.VMEM((FF_SHARD, D_MODEL), F32)] * 2,
        name="grad_w_gate_up", compiler_params=_params(("parallel", "arbitrary")),
    )(da, db, h)
    return g_gate, g_up, grad_weight("grad_w_down", s, hid, df, tok, (FF_SHARD, D_MODEL))


BWD_TILE = 512


def matmul_prenorm_bwd(name, dims, pairs, pair_specs, x, dxo, gain, scale, sums, u, below):
    n = len(pairs)

    def epilogue(dh, ins, outs, ids):
        first = ids[0] == 0
        dx = _prenorm_bwd_epilogue(dh, ins[n], ins[n + 1], ins[n + 2], ins[n + 3], outs[0], outs[1], first)
        if below is not None:
            _resid_bwd_epilogue(dx, ins[n + 5], ins[n + 6], below[2], outs[2], outs[3], first)

    row = pl.BlockSpec((BWD_TILE, D_MODEL), lambda i, j: (i, 0))
    any_spec = pl.BlockSpec(memory_space=pl.ANY)
    f32 = jax.ShapeDtypeStruct((SEQ, D_MODEL), F32)
    in_specs = list(pair_specs) + [row, row, gain.spec(), scale.spec(), any_spec]
    operands = tuple(pairs) + (x, dxo, gain.table, scale.table, sums[0])
    out_specs, out_shape, aliases = [row, _slot_spec(u)], [f32, jax.ShapeDtypeStruct(sums[0].shape, F32)], {n + 4: 1}
    if below is not None:
        in_specs += [row, below[1].spec(), any_spec]
        operands += (below[0], below[1].table, sums[1])
        out_specs += [row, _slot_spec(u - 1)]
        out_shape += [jax.ShapeDtypeStruct((SEQ, D_MODEL), BF16), jax.ShapeDtypeStruct(sums[1].shape, F32)]
        aliases[n + 7] = 3
    out = _matmul(name, (SEQ // BWD_TILE, N_CHIPS), N_CHIPS, 1, dims, len(pairs) // 2, in_specs, out_specs, out_shape,
                  (BWD_TILE, D_MODEL), epilogue, operands, ("arbitrary", "arbitrary"), aliases=aliases)
    if below is None:
        return out[0], (out[1], sums[1])
    return out[0], (out[1], out[3]), out[2]


def ffn_bwd_input(da, db, wg_all, wu_all, x, dxo, gain, scale, sums, u, below):
    hid = pl.BlockSpec((None, BWD_TILE, FF_SHARD), lambda i, j: (j, i, 0))
    w = pl.BlockSpec((None, FF_SHARD, D_MODEL), lambda i, j: (j, 0, 0))
    return matmul_prenorm_bwd("ffn_bwd_input", NN, (da, wg_all, db, wu_all), (hid, w, hid, w), x, dxo, gain, scale,
                              sums, u, below)


def in_proj(h, w_all):
    def epilogue(acc, ins, outs, ids):
        outs[0][...] = acc
        outs[1][...] = acc.astype(BF16)

    out = pl.BlockSpec((MM_TILE, IN_SHARD), lambda j, i: (i, j))
    return _matmul(
        "in_proj", (N_CHIPS, SEQ // MM_TILE), 1, None, NN, 1,
        [pl.BlockSpec((MM_TILE, D_MODEL), lambda j, i: (i, 0)),
         pl.BlockSpec((None, D_MODEL, IN_SHARD), lambda j, i: (j, 0, 0))],
        [out, out], [jax.ShapeDtypeStruct((SEQ, D_IN), F32), jax.ShapeDtypeStruct((SEQ, D_IN), BF16)],
        None, epilogue, (h, w_all), ("parallel", "parallel"))


_GATE_BLOCK0 = D_QKV // D_SHARD


def _branch_products(o, w_ref):
    ob = o.astype(BF16)
    return (_dot(ob[:, 0:256], w_ref[0:256, :], NN), _dot(ob[:, 256:384], w_ref[256:384, :], NN),
            _dot(ob[:, 384:768], w_ref[384:768, :], NN))


def merge_branches(o_cat, wbr_all, proj):
    def body(o_ref, w_ref, g0_ref, g1_ref, g2_ref, m_ref):
        u = _branch_products(o_ref[...], w_ref)
        m_ref[...] = (_sigmoid(g0_ref[...]) * u[0] + _sigmoid(g1_ref[...]) * u[1]
                      + _sigmoid(g2_ref[...]) * u[2]).astype(BF16)

    def gate_spec(b):
        return pl.BlockSpec((MM_TILE, D_SHARD), lambda i, j: (i, _GATE_BLOCK0 + 4 * b + j))

    return pl.pallas_call(
        body, grid=(SEQ // MM_TILE, N_CHIPS),
        in_specs=[pl.BlockSpec((MM_TILE, BR_ROWS), lambda i, j: (i, 0)),
                  pl.BlockSpec((None, BR_ROWS, D_SHARD), lambda i, j: (j, 0, 0)),
                  gate_spec(0), gate_spec(1), gate_spec(2)],
        out_specs=pl.BlockSpec((MM_TILE, D_SHARD), lambda i, j: (i, j)),
        out_shape=jax.ShapeDtypeStruct((SEQ, D_MODEL), BF16),
        name="merge_branches", compiler_params=_params(("parallel", "parallel")),
    )(o_cat, wbr_all, proj, proj, proj)


def out_proj(merged, wout_all, x, gate, then):
    return matmul_residual(
        "out_proj", merged, pl.BlockSpec((MM_TILE, D_SHARD), lambda i, j: (i, j)),
        wout_all, pl.BlockSpec((None, D_SHARD, D_MODEL), lambda i, j: (j, 0, 0)), x, gate, 1.0, then)


def merge_bwd(dmo, wout_all, o_cat, wbr_all, proj):
    def epilogue(dm, ins, outs, ids):
        u = _branch_products(ins[2][...], ins[3])
        for b in range(3):
            sig = _sigmoid(ins[4 + b][...])
            outs[b][...] = (dm * sig).astype(BF16)
            outs[3 + b][...] = (dm * u[b] * (sig * (1.0 - sig))).astype(BF16)

    def gate_spec(b):
        return pl.BlockSpec((MM_TILE, D_SHARD), lambda j, i: (i, _GATE_BLOCK0 + 4 * b + j))

    col = pl.BlockSpec((MM_TILE, D_SHARD), lambda j, i: (i, j))
    du = jax.ShapeDtypeStruct((SEQ, D_MODEL), BF16)
    return _matmul(
        "merge_bwd", (N_CHIPS, SEQ // MM_TILE), 1, None, NT, 1,
        [pl.BlockSpec((MM_TILE, D_MODEL), lambda j, i: (i, 0)),
         pl.BlockSpec((None, D_SHARD, D_MODEL), lambda j, i: (j, 0, 0)),
         pl.BlockSpec((MM_TILE, BR_ROWS), lambda j, i: (i, 0)),
         pl.BlockSpec((None, BR_ROWS, D_SHARD), lambda j, i: (j, 0, 0)),
         gate_spec(0), gate_spec(1), gate_spec(2)],
        [col] * 6, [du] * 6,
        None, epilogue, (dmo, wout_all, o_cat, wbr_all, proj, proj, proj), ("parallel", "parallel"))


def branch_bwd_input(du, wbr_all):
    def body(d0_ref, d1_ref, d2_ref, w_ref, o_ref, acc):
        j = pl.program_id(1)
        parts = (_dot(d0_ref[...], w_ref[0:256, :], NT), _dot(d1_ref[...], w_ref[256:384, :], NT),
                 _dot(d2_ref[...], w_ref[384:768, :], NT))

        @pl.when(j == 0)
        def _():
            acc[:, 0:256], acc[:, 256:384], acc[:, 384:768] = parts

        @pl.when(j > 0)
        def _():
            acc[:, 0:256] += parts[0]
            acc[:, 256:384] += parts[1]
            acc[:, 384:768] += parts[2]

        @pl.when(j == N_CHIPS - 1)
        def _():
            o_ref[...] = acc[...]

    col = pl.BlockSpec((MM_TILE, D_SHARD), lambda i, j: (i, j))
    return pl.pallas_call(
        body, grid=(SEQ // MM_TILE, N_CHIPS),
        in_specs=[col, col, col, pl.BlockSpec((None, BR_ROWS, D_SHARD), lambda i, j: (j, 0, 0))],
        out_specs=pl.BlockSpec((MM_TILE, BR_ROWS), lambda i, j: (i, 0)),
        out_shape=jax.ShapeDtypeStruct((SEQ, BR_ROWS), F32),
        scratch_shapes=[pltpu.VMEM((MM_TILE, BR_ROWS), F32)],
        name="branch_bwd_input", compiler_params=_params(("parallel", "arbitrary")),
    )(du[0], du[1], du[2], wbr_all)


def branch_grad_weights(o_cat, du):
    def body(o_ref, d0_ref, d1_ref, d2_ref, g_ref, acc):
        k = pl.program_id(1)
        ob = o_ref[...].astype(BF16)
        parts = (_dot(ob[:, 0:256], d0_ref[...], TN), _dot(ob[:, 256:384], d1_ref[...], TN),
                 _dot(ob[:, 384:768], d2_ref[...], TN))

        @pl.when(k == 0)
        def _():
            acc[0:256, :], acc[256:384, :], acc[384:768, :] = parts

        @pl.when(k > 0)
        def _():
            acc[0:256, :] += parts[0]
            acc[256:384, :] += parts[1]
            acc[384:768, :] += parts[2]

        @pl.when(k == SEQ // MM_TILE - 1)
        def _():
            g_ref[...] = acc[...].astype(BF16)

    col = pl.BlockSpec((MM_TILE, D_SHARD), lambda j, k: (k, j))
    return pl.pallas_call(
        body, grid=(N_CHIPS, SEQ // MM_TILE),
        in_specs=[pl.BlockSpec((MM_TILE, BR_ROWS), lambda j, k: (k, 0)), col, col, col],
        out_specs=pl.BlockSpec((None, BR_ROWS, D_SHARD), lambda j, k: (j, 0, 0)),
        out_shape=jax.ShapeDtypeStruct((N_CHIPS, BR_ROWS, D_SHARD), BF16),
        scratch_shapes=[pltpu.VMEM((BR_ROWS, D_SHARD), F32)],
        name="branch_grad_weights", compiler_params=_params(("parallel", "arbitrary")),
    )(o_cat, du[0], du[1], du[2])


def mixer_bwd_input(dproj, win_all, x, dxo, gain, scale, sums, u, below):
    return matmul_prenorm_bwd(
        "mixer_bwd_input", NT, (dproj, win_all),
        (pl.BlockSpec((BWD_TILE, IN_SHARD), lambda i, j: (i, j)),
         pl.BlockSpec((None, D_MODEL, IN_SHARD), lambda i, j: (j, 0, 0))), x, dxo, gain, scale, sums, u, below)


BATCH_QK = (((2,), (2,)), ((0,), (0,)))
BATCH_PV = (((2,), (1,)), ((0,), (0,)))
BATCH_TN = (((1,), (1,)), ((0,), (0,)))


SB_WIDTH = H_SB * HEAD_DIM
SB_ROWS = H_SB * BLK


def _split_dot(v, tri):
    hi = v.astype(BF16)
    lo = (v - hi.astype(F32)).astype(BF16)
    return _dot(hi, tri, NN) + _dot(lo, tri, NN)


def _tri(cmp):
    return cmp(lax.broadcasted_iota(jnp.int32, (BLK, BLK), 0), lax.broadcasted_iota(jnp.int32, (BLK, BLK), 1)).astype(BF16)


def _head_masks():
    lane = lax.broadcasted_iota(jnp.int32, (1, SB_WIDTH), 1) // HEAD_DIM
    return [lane == h for h in range(H_SB)]


def _stack_heads(x, masks):
    return jnp.concatenate([jnp.where(m, x, jnp.zeros_like(x)) for m in masks], axis=0)


def _merge_heads(y, masks):
    out = jnp.where(masks[0], y[0:BLK], 0.0)
    for h in range(1, H_SB):
        out = jnp.where(masks[h], y[h * BLK:(h + 1) * BLK], out)
    return out


def _sb_scores(q4, k_ref, j, diagonal):
    rows = pl.ds(pl.multiple_of(j * BLK, BLK), BLK)
    z = _dot(q4, k_ref[rows, :], NT)
    log_fail = -(jnp.maximum(z, 0.0) + jnp.log(1.0 + jnp.exp(-jnp.abs(z))))
    log_hit = z + log_fail
    before = None
    if diagonal:
        tile = (SB_ROWS, BLK)
        before = lax.broadcasted_iota(jnp.int32, tile, 1) < (lax.broadcasted_iota(jnp.int32, tile, 0) & (BLK - 1))
        log_fail = jnp.where(before, log_fail, 0.0)
    return rows, before, log_fail, log_hit


def _keep(before, x):
    return x if before is None else jnp.where(before, x, 0.0)


def sb_forward(qkv):
    def body(q_ref, k_ref, v_ref, o_ref, tot_ref):
        i = pl.program_id(0)
        masks = _head_masks()
        q4 = _stack_heads(q_ref[...] * QK_SCALE, masks)
        later = _tri(lambda r, c: r > c)

        def tiles(js, carry, diagonal):
            o, run = carry
            scores = [_sb_scores(q4, k_ref, j, diagonal) for j in js]
            acc = None
            for rows, before, log_fail, log_hit in scores:
                between = _split_dot(log_fail, later) + run
                w = _keep(before, jnp.exp(log_hit + between))
                part = _dot(w.astype(BF16), v_ref[rows, :], NN)
                acc = part if acc is None else acc + part
                run = run + jnp.sum(log_fail, axis=1, keepdims=True)
            return o + _merge_heads(acc, masks), run

        carry = tiles([i], (jnp.zeros((BLK, SB_WIDTH), F32), jnp.zeros((SB_ROWS, 1), F32)), True)
        carry = lax.cond((i & 1) != 0, lambda c: tiles([i - 1], c, False), lambda c: c, carry)
        at = i - 1 - (i & 1)
        carry = lax.cond((i & 2) != 0, lambda c: tiles([at, at - 1], c, False), lambda c: c, carry)
        at = at - (i & 2)
        o, run = lax.fori_loop(0, i // 4, lambda t, c: tiles([at - 4 * t - n for n in range(4)], c, False), carry)
        o_ref[...] = o
        tot_ref[...] = run

    return pl.pallas_call(
        body, grid=(N_BLK,),
        in_specs=[pl.BlockSpec((BLK, SB_WIDTH), lambda i: (i, 0)), pl.BlockSpec((SEQ, SB_WIDTH), lambda i: (0, 1)),
                  pl.BlockSpec((SEQ, SB_WIDTH), lambda i: (0, 2))],
        out_specs=[pl.BlockSpec((BLK, SB_WIDTH), lambda i: (i, 0)), pl.BlockSpec((None, SB_ROWS, 1), lambda i: (i, 0, 0))],
        out_shape=[jax.ShapeDtypeStruct((SEQ, SB_WIDTH), F32), jax.ShapeDtypeStruct((N_BLK, SB_ROWS, 1), F32)],
        name="sb_forward", compiler_params=_params(("parallel",)),
    )(qkv, qkv, qkv)


def sb_backward(qkv, total, do_cat):
    def body(q_ref, k_ref, v_ref, tot_ref, do_ref, dq_ref, dk_ref, dv_ref):
        i = pl.program_id(0)

        @pl.when(i == 0)
        def _():
            dk_ref[...] = jnp.zeros_like(dk_ref)
            dv_ref[...] = jnp.zeros_like(dv_ref)

        masks = _head_masks()
        q4 = _stack_heads(q_ref[...] * QK_SCALE, masks)
        do4 = _stack_heads(do_ref[...].astype(BF16), masks)
        total_v = tot_ref[...]
        upto = _tri(lambda r, c: r <= c)
        earlier = _tri(lambda r, c: r < c)

        def tiles(js, carry, diagonal):
            dq, seen, g_seen = carry
            scores = [_sb_scores(q4, k_ref, j, diagonal) for j in js]
            acc = None
            for rows, before, log_fail, log_hit in scores:
                between = total_v - (seen + _split_dot(log_fail, upto))
                w = _keep(before, jnp.exp(log_hit + between))
                g = _dot(do4, v_ref[rows, :], NT) * w
                g_earlier = g_seen + _split_dot(g, earlier)
                sig = jnp.exp(log_hit)
                dz = _keep(before, g * (1.0 - sig) - g_earlier * sig).astype(BF16)
                part = _dot(dz, k_ref[rows, :], NN)
                acc = part if acc is None else acc + part
                dk_ref[rows, :] += _dot(dz, q4, TN)
                dv_ref[rows, :] += _dot(w.astype(BF16), do4, TN)
                seen = seen + jnp.sum(log_fail, axis=1, keepdims=True)
                g_seen = g_seen + jnp.sum(g, axis=1, keepdims=True)
            return dq + _merge_heads(acc, masks), seen, g_seen

        zero = jnp.zeros((SB_ROWS, 1), F32)
        carry = lax.fori_loop(0, i // 4, lambda t, c: tiles([4 * t + n for n in range(4)], c, False),
                              (jnp.zeros((BLK, SB_WIDTH), F32), zero, zero))
        at = i - (i & 3)
        carry = lax.cond((i & 2) != 0, lambda c: tiles([at, at + 1], c, False), lambda c: c, carry)
        carry = lax.cond((i & 1) != 0, lambda c: tiles([i - 1], c, False), lambda c: c, carry)
        dq, _, _ = tiles([i], carry, True)
        dq_ref[...] = dq * QK_SCALE

    blk = pl.BlockSpec((BLK, SB_WIDTH), lambda i: (i, 0))
    full = pl.BlockSpec((SEQ, SB_WIDTH), lambda i: (0, 0))
    shape = jax.ShapeDtypeStruct((SEQ, SB_WIDTH), F32)
    return pl.pallas_call(
        body, grid=(N_BLK,),
        in_specs=[blk, pl.BlockSpec((SEQ, SB_WIDTH), lambda i: (0, 1)), pl.BlockSpec((SEQ, SB_WIDTH), lambda i: (0, 2)),
                  pl.BlockSpec((None, SB_ROWS, 1), lambda i: (i, 0, 0)), blk],
        out_specs=[blk, full, full], out_shape=[shape, shape, shape],
        name="sb_backward", compiler_params=_params(("arbitrary",)),
    )(qkv, qkv, qkv, total, do_cat)


def _band_scores(q_ref, kp_ref, ko_ref, bias_ref, hb, prev_mask):
    b = pl.program_id(1)
    qs = q_ref[...]
    s_prev = _dot(qs, kp_ref[...], BATCH_QK) + bias_ref[:, :, 0:BLK]
    s_prev = jnp.concatenate(
        [jnp.where((b & prev_mask(pl.program_id(0) * hb + t)) != 0, s_prev[t:t + 1], NEG) for t in range(hb)], axis=0)
    s_own = _dot(qs, ko_ref[...], BATCH_QK) + bias_ref[:, :, BLK:2 * BLK]
    return qs, s_prev, s_own


def _band_specs(hb, rows, t_n):
    def q_spec(width):
        return pl.BlockSpec((hb, None, rows, width), lambda h, b: (h, b, 0, 0))

    own = pl.BlockSpec((hb, BLK, HEAD_DIM), lambda h, b: (h, b, 0))
    prev = pl.BlockSpec((hb, BLK, HEAD_DIM), lambda h, b: (h, jnp.maximum(b - 1, 0), 0))
    per_head = lambda r, width: pl.BlockSpec((hb, r, width), lambda h, b: (h, 0, 0))
    return q_spec, own, prev, per_head


def banded_forward(name, q, k, v, bias, sinks, hb, prev_mask):
    h_n, nb, rows, _ = q.shape

    def body(q_ref, kp_ref, ko_ref, vp_ref, vo_ref, bias_ref, sink_ref, o_ref, lse_ref):
        _, s_prev, s_own = _band_scores(q_ref, kp_ref, ko_ref, bias_ref, hb, prev_mask)
        sink = sink_ref[...]
        m = jnp.maximum(jnp.maximum(jnp.max(s_prev, axis=2, keepdims=True), jnp.max(s_own, axis=2, keepdims=True)), sink)
        p_prev = jnp.exp(s_prev - m)
        p_own = jnp.exp(s_own - m)
        denom = jnp.sum(p_prev, axis=2, keepdims=True) + jnp.sum(p_own, axis=2, keepdims=True) + jnp.exp(sink - m)
        o = _dot(p_prev.astype(BF16), vp_ref[...], BATCH_PV) + _dot(p_own.astype(BF16), vo_ref[...], BATCH_PV)
        o_ref[...] = o / denom
        lse_ref[...] = m + jnp.log(denom)

    q_spec, own, prev, per_head = _band_specs(hb, rows, k.shape[1])
    return pl.pallas_call(
        body, grid=(h_n // hb, nb),
        in_specs=[q_spec(HEAD_DIM), prev, own, prev, own, per_head(rows, 2 * BLK), per_head(rows, 1)],
        out_specs=[q_spec(HEAD_DIM), q_spec(1)],
        out_shape=[jax.ShapeDtypeStruct(q.shape, F32), jax.ShapeDtypeStruct((h_n, nb, rows, 1), F32)],
        name=name, compiler_params=_params(("parallel", "parallel")),
    )(q, k, k, v, v, bias, sinks)


def banded_backward(name, q, k, v, bias, sinks, o, lse, do, dlse, hb, prev_mask):
    h_n, nb, rows, _ = q.shape
    t_n = k.shape[1]

    def body(q_ref, kp_ref, ko_ref, vp_ref, vo_ref, bias_ref, sink_ref, o_ref, lse_ref, do_ref, dlse_ref,
             dq_ref, dk_ref, dv_ref, dbias_ref, dsink_ref):
        b = pl.program_id(1)

        @pl.when(b == 0)
        def _():
            dk_ref[...] = jnp.zeros_like(dk_ref)
            dv_ref[...] = jnp.zeros_like(dv_ref)
            dbias_ref[...] = jnp.zeros_like(dbias_ref)
            dsink_ref[...] = jnp.zeros_like(dsink_ref)

        qs, s_prev, s_own = _band_scores(q_ref, kp_ref, ko_ref, bias_ref, hb, prev_mask)
        lse_v = lse_ref[...]
        dov = do_ref[...]
        dob = dov.astype(BF16)
        shift = dlse_ref[...] - jnp.sum(dov * o_ref[...], axis=2, keepdims=True)
        p_prev = jnp.exp(s_prev - lse_v)
        p_own = jnp.exp(s_own - lse_v)
        ds_prev = p_prev * (_dot(dob, vp_ref[...], BATCH_QK) + shift)
        ds_own = p_own * (_dot(dob, vo_ref[...], BATCH_QK) + shift)
        dbias_ref[:, :, 0:BLK] += ds_prev
        dbias_ref[:, :, BLK:2 * BLK] += ds_own
        d_sink = jnp.exp(sink_ref[...] - lse_v) * shift
        for g in range(rows // BLK):
            dsink_ref[:, g:g + 1, :] += jnp.sum(d_sink[:, g * BLK:(g + 1) * BLK, :], axis=1, keepdims=True)
        ds_prev = ds_prev.astype(BF16)
        ds_own = ds_own.astype(BF16)
        dq_ref[...] = (_dot(ds_prev, kp_ref[...], BATCH_PV) + _dot(ds_own, ko_ref[...], BATCH_PV)) * QK_SCALE
        rows_prev = pl.ds(pl.multiple_of(jnp.maximum(b - 1, 0) * BLK, BLK), BLK)
        rows_own = pl.ds(pl.multiple_of(b * BLK, BLK), BLK)
        dk_ref[:, rows_prev, :] += _dot(ds_prev, qs, BATCH_TN)
        dk_ref[:, rows_own, :] += _dot(ds_own, qs, BATCH_TN)
        dv_ref[:, rows_prev, :] += _dot(p_prev.astype(BF16), dob, BATCH_TN)
        dv_ref[:, rows_own, :] += _dot(p_own.astype(BF16), dob, BATCH_TN)

    q_spec, own, prev, per_head = _band_specs(hb, rows, t_n)
    kv_full = per_head(t_n, HEAD_DIM)
    kv_shape = jax.ShapeDtypeStruct((h_n, t_n, HEAD_DIM), F32)
    return pl.pallas_call(
        body, grid=(h_n // hb, nb),
        in_specs=[q_spec(HEAD_DIM), prev, own, prev, own, per_head(rows, 2 * BLK), per_head(rows, 1),
                  q_spec(HEAD_DIM), q_spec(1), q_spec(HEAD_DIM), q_spec(1)],
        out_specs=[q_spec(HEAD_DIM), kv_full, kv_full, per_head(rows, 2 * BLK), per_head(rows // BLK, BLK)],
        out_shape=[jax.ShapeDtypeStruct(q.shape, F32), kv_shape, kv_shape,
                   jax.ShapeDtypeStruct((h_n, rows, 2 * BLK), F32), jax.ShapeDtypeStruct((h_n, rows // BLK, BLK), F32)],
        name=name, compiler_params=_params(("parallel", "arbitrary")),
    )(q, k, k, v, v, bias, sinks, o, lse, do, dlse)


def _swa_prev_mask(head):
    del head
    return 15


SWA_HEADS_PER_STEP = 2
SWA_GROUP = H_SWA_Q // H_SWA_KV
N_BLK = SEQ // BLK


GROUP_W = H_PER_DIL * HEAD_DIM
DIL_COLUMNS = (768, 1920)
LANE_BLOCKS = (DIL_COLUMNS[1] - DIL_COLUMNS[0]) // GROUP_W
DIL_Q_BLOCK, DIL_K_BLOCK, DIL_V_BLOCK = 0, 3, 6
N_GROUPS = len(DIL_PATTERNS)


def dilated_views(qkv):
    cols = qkv[:, DIL_COLUMNS[0]:DIL_COLUMNS[1]]
    return [_dil_view(cols, d) for _, d in DIL_PATTERNS]


def _dil_view(t, d):
    return t.reshape(SEQ // d, d * t.shape[1])


def _dil_tile(n, d):
    per_class = N_BLK // d
    return n // per_class, n % per_class


def _dil_spec(d, lane_block, lane_blocks, shift=0):
    def index(n):
        r, m = _dil_tile(n, d)
        m = jnp.clip(m + shift, 0, N_BLK // d - 1)
        return m, r * lane_blocks + lane_block
    return pl.BlockSpec((BLK, GROUP_W), index)


def _two_heads(x, first):
    zero = jnp.zeros_like(x)
    return jnp.concatenate([jnp.where(first, x, zero), jnp.where(first, zero, x)], axis=0)


def _per_head(col, first):
    return jnp.where(first, col[0:BLK], col[BLK:2 * BLK])


def _head_rows(tile, first):
    pick = lambda keep: jnp.max(jnp.where(keep, tile, -jnp.inf), axis=1, keepdims=True)
    return jnp.concatenate([pick(first), pick(jnp.logical_not(first))], axis=0)


def _dil_scores(q_ref, kp_ref, ko_ref, bias, has_prev, first):
    q2 = _two_heads(q_ref[...] * QK_SCALE, first)
    k2 = jnp.concatenate([kp_ref[...], ko_ref[...]], axis=0)
    s = _dot(q2, k2, NT) + bias
    key = lax.broadcasted_iota(jnp.int32, s.shape, 1)
    return q2, k2, jnp.where(jnp.logical_or(has_prev, key >= BLK), s, NEG)


def dilated_forward(views, bias):
    def body(*refs):
        ins, bias_ref, outs = refs[:5 * N_GROUPS], refs[5 * N_GROUPS], refs[5 * N_GROUPS + 1:]
        n = pl.program_id(0)
        first = lax.broadcasted_iota(jnp.int32, (1, GROUP_W), 1) < HEAD_DIM
        for g, (_, d) in enumerate(DIL_PATTERNS):
            q_ref, kp_ref, ko_ref, vp_ref, vo_ref = ins[5 * g:5 * g + 5]
            has_prev = _dil_tile(n, d)[1] > 0
            _, _, s = _dil_scores(q_ref, kp_ref, ko_ref, bias_ref[g], has_prev, first)
            m = jnp.max(s, axis=1, keepdims=True)
            p = jnp.exp(s - m)
            denom = jnp.sum(p, axis=1, keepdims=True)
            v2 = jnp.concatenate([vp_ref[...], vo_ref[...]], axis=0)
            o2 = _dot(p.astype(BF16), v2, NN) / denom
            outs[2 * g][...] = _per_head(o2, first)
            outs[2 * g + 1][...] = _per_head(m + jnp.log(denom), first)

    operands, in_specs, out_specs, out_shape = [], [], [], []
    for g, (_, d) in enumerate(DIL_PATTERNS):
        operands += [views[g]] * 5
        in_specs += [_dil_spec(d, DIL_Q_BLOCK + g, LANE_BLOCKS), _dil_spec(d, DIL_K_BLOCK + g, LANE_BLOCKS, -1),
                     _dil_spec(d, DIL_K_BLOCK + g, LANE_BLOCKS), _dil_spec(d, DIL_V_BLOCK + g, LANE_BLOCKS, -1),
                     _dil_spec(d, DIL_V_BLOCK + g, LANE_BLOCKS)]
        out_specs += [_dil_spec(d, 0, 1)] * 2
        out_shape += [jax.ShapeDtypeStruct((SEQ // d, d * GROUP_W), F32)] * 2
    out = pl.pallas_call(
        body, grid=(N_BLK,), in_specs=in_specs + [pl.BlockSpec((N_GROUPS, 2 * BLK, 2 * BLK), lambda n: (0, 0, 0))],
        out_specs=out_specs, out_shape=out_shape, name="dilated_forward", compiler_params=_params(("parallel",)),
    )(*operands, bias)
    out = [t.reshape(SEQ, GROUP_W) for t in out]
    return out[0::2], out[1::2]


def _group_softmax(lses):
    m = jnp.maximum(jnp.maximum(lses[0], lses[1]), lses[2])
    e = [jnp.exp(l - m) for l in lses]
    total = e[0] + e[1] + e[2]
    return [t / total for t in e]


def dilated_merge(o, lse):
    def body(*refs):
        alpha = _group_softmax([r[...] for r in refs[N_GROUPS:2 * N_GROUPS]])
        refs[-1][...] = alpha[0] * refs[0][...] + alpha[1] * refs[1][...] + alpha[2] * refs[2][...]

    spec = pl.BlockSpec((ROW_TILE, GROUP_W), lambda i: (i, 0))
    return pl.pallas_call(
        body, grid=(SEQ // ROW_TILE,), in_specs=[spec] * (2 * N_GROUPS), out_specs=spec,
        out_shape=jax.ShapeDtypeStruct((SEQ, GROUP_W), F32), name="dilated_merge", compiler_params=_params(("parallel",)),
    )(*o, *lse)


def dilated_merge_bwd(o, lse, do_cat):
    def body(*refs):
        o_v = [r[...] for r in refs[:N_GROUPS]]
        alpha = _group_softmax([r[...] for r in refs[N_GROUPS:2 * N_GROUPS]])
        dout = refs[2 * N_GROUPS][...]
        outs = refs[2 * N_GROUPS + 1:]
        first = lax.broadcasted_iota(jnp.int32, (1, GROUP_W), 1) < HEAD_DIM

        def head_sum(x):
            a = jnp.sum(jnp.where(first, x, 0.0), axis=1, keepdims=True)
            b = jnp.sum(jnp.where(first, 0.0, x), axis=1, keepdims=True)
            return jnp.where(first, a, b)

        dalpha = [head_sum(dout * o_g) for o_g in o_v]
        mean = alpha[0] * dalpha[0] + alpha[1] * dalpha[1] + alpha[2] * dalpha[2]
        for g in range(N_GROUPS):
            outs[g][...] = alpha[g] * dout
            outs[N_GROUPS + g][...] = alpha[g] * (dalpha[g] - mean)

    spec = pl.BlockSpec((ROW_TILE, GROUP_W), lambda i: (i, 0))
    shape = jax.ShapeDtypeStruct((SEQ, GROUP_W), F32)
    out = pl.pallas_call(
        body, grid=(SEQ // ROW_TILE,), in_specs=[spec] * (2 * N_GROUPS) + [pl.BlockSpec((ROW_TILE, GROUP_W), lambda i: (i, 2))],
        out_specs=[spec] * (2 * N_GROUPS), out_shape=[shape] * (2 * N_GROUPS),
        name="dilated_merge_bwd", compiler_params=_params(("parallel",)),
    )(*o, *lse, do_cat)
    return out[:N_GROUPS], out[N_GROUPS:]


def dilated_backward(views, bias, o, lse, do, dlse):
    n_in = 9

    def body(*refs):
        ins, bias_ref = refs[:n_in * N_GROUPS], refs[n_in * N_GROUPS]
        outs, dbias_ref = refs[n_in * N_GROUPS + 1:-1], refs[-1]
        n = pl.program_id(0)

        @pl.when(n == 0)
        def _():
            dbias_ref[...] = jnp.zeros_like(dbias_ref)

        first = lax.broadcasted_iota(jnp.int32, (1, GROUP_W), 1) < HEAD_DIM
        for g, (_, d) in enumerate(DIL_PATTERNS):
            q_ref, kp_ref, ko_ref, vp_ref, vo_ref, o_ref, lse_ref, do_ref, dlse_ref = ins[n_in * g:n_in * (g + 1)]
            has_prev = _dil_tile(n, d)[1] > 0
            q2, k2, s = _dil_scores(q_ref, kp_ref, ko_ref, bias_ref[g], has_prev, first)
            dov = do_ref[...]
            do2 = _two_heads(dov.astype(BF16), first)
            prod = dov * o_ref[...]
            delta = jnp.concatenate([jnp.sum(jnp.where(first, prod, 0.0), axis=1, keepdims=True),
                                     jnp.sum(jnp.where(first, 0.0, prod), axis=1, keepdims=True)], axis=0)
            shift = _head_rows(dlse_ref[...], first) - delta
            p = jnp.exp(s - _head_rows(lse_ref[...], first))
            v2 = jnp.concatenate([vp_ref[...], vo_ref[...]], axis=0)
            ds = p * (_dot(do2, v2, NT) + shift)
            dbias_ref[g] += ds
            ds = ds.astype(BF16)
            dq2 = _dot(ds, k2, NN) * QK_SCALE
            dk2 = _dot(ds, q2, TN)
            dv2 = _dot(p.astype(BF16), do2, TN)
            base = 5 * g
            outs[base][...] = jnp.where(first, dq2[0:BLK], dq2[BLK:2 * BLK])
            outs[base + 1][...] = dk2[BLK:2 * BLK]
            outs[base + 2][...] = dk2[0:BLK]
            outs[base + 3][...] = dv2[BLK:2 * BLK]
            outs[base + 4][...] = dv2[0:BLK]

    operands, in_specs, out_specs, out_shape = [], [], [], []
    for g, (_, d) in enumerate(DIL_PATTERNS):
        own = _dil_spec(d, 0, 1)
        operands += [views[g]] * 5 + [_dil_view(t[g], d) for t in (o, lse, do, dlse)]
        in_specs += [_dil_spec(d, DIL_Q_BLOCK + g, LANE_BLOCKS), _dil_spec(d, DIL_K_BLOCK + g, LANE_BLOCKS, -1),
                     _dil_spec(d, DIL_K_BLOCK + g, LANE_BLOCKS), _dil_spec(d, DIL_V_BLOCK + g, LANE_BLOCKS, -1),
                     _dil_spec(d, DIL_V_BLOCK + g, LANE_BLOCKS)] + [own] * 4
        out_specs += [own] * 5
        out_shape += [jax.ShapeDtypeStruct((SEQ // d, d * GROUP_W), F32)] * 5
    tiles = pl.BlockSpec((N_GROUPS, 2 * BLK, 2 * BLK), lambda n: (0, 0, 0))
    out = pl.pallas_call(
        body, grid=(N_BLK,), in_specs=in_specs + [tiles], out_specs=out_specs + [tiles],
        out_shape=out_shape + [jax.ShapeDtypeStruct((N_GROUPS, 2 * BLK, 2 * BLK), F32)],
        name="dilated_backward", compiler_params=_params(("arbitrary",)),
    )(*operands, bias)
    return [out[5 * g:5 * g + 5] for g in range(N_GROUPS)], out[-1]


def dilated_key_grads(parts):
    def body(*refs):
        ins, outs = refs[:4 * N_GROUPS], refs[4 * N_GROUPS:]
        n = pl.program_id(0)
        for g, (_, d) in enumerate(DIL_PATTERNS):
            has_next = _dil_tile(n, d)[1] < N_BLK // d - 1
            own_k, next_k, own_v, next_v = ins[4 * g:4 * g + 4]
            outs[2 * g][...] = own_k[...] + jnp.where(has_next, next_k[...], 0.0)
            outs[2 * g + 1][...] = own_v[...] + jnp.where(has_next, next_v[...], 0.0)

    operands, in_specs, out_specs, out_shape = [], [], [], []
    for g, (_, d) in enumerate(DIL_PATTERNS):
        _, dk_own, dk_prev, dv_own, dv_prev = parts[g]
        operands += [dk_own, dk_prev, dv_own, dv_prev]
        in_specs += [_dil_spec(d, 0, 1), _dil_spec(d, 0, 1, 1)] * 2
        out_specs += [_dil_spec(d, 0, 1)] * 2
        out_shape += [jax.ShapeDtypeStruct((SEQ // d, d * GROUP_W), F32)] * 2
    out = pl.pallas_call(
        body, grid=(N_BLK,), in_specs=in_specs, out_specs=out_specs, out_shape=out_shape,
        name="dilated_key_grads", compiler_params=_params(("parallel",)),
    )(*operands)
    tok = lambda ts: jnp.concatenate([t.reshape(SEQ, GROUP_W) for t in ts], axis=1)
    return tok([parts[g][0] for g in range(N_GROUPS)]), tok(out[0::2]), tok(out[1::2])


def rel_bias_reduce(dbias0, dbias1, bucket):
    def body(d0_ref, d1_ref, b_ref, o_ref):
        dv, bv = d0_ref[...] + d1_ref[...], b_ref[...]
        lane = lax.broadcasted_iota(jnp.int32, (1, BLK), 1)
        acc = jnp.zeros((1, BLK), F32)
        for bkt in range(N_BUCKETS):
            acc = acc + jnp.where(lane == bkt, jnp.sum(jnp.where(bv == bkt, dv, 0.0)), 0.0)
        o_ref[...] = acc

    tile = pl.BlockSpec((None, BLK, 2 * BLK), lambda h: (h, 0, 0))
    return pl.pallas_call(
        body, grid=(dbias0.shape[0],), in_specs=[tile, tile, tile],
        out_specs=pl.BlockSpec((None, 1, BLK), lambda h: (h, 0, 0)),
        out_shape=jax.ShapeDtypeStruct((dbias0.shape[0], 1, BLK), F32),
        name="rel_bias_reduce", compiler_params=_params(("parallel",)),
    )(dbias0, dbias1, bucket)


def _heads(t):
    return t.reshape(SEQ, -1, HEAD_DIM).transpose(1, 0, 2)


def _unheads(t):
    return t.transpose(1, 0, 2).reshape(SEQ, -1)


def _t5_bucket(n):
    max_exact = N_BUCKETS // 2
    nf = jnp.maximum(n, 1).astype(F32)
    large = max_exact + (jnp.log(nf / max_exact) / math.log(MAX_REL_DIST / max_exact)
                         * (N_BUCKETS - max_exact)).astype(jnp.int32)
    large = jnp.minimum(large, N_BUCKETS - 1)
    return jnp.where(n < max_exact, n, large)


def band_tables(rel_bias):
    rel = jnp.arange(BLK)[:, None] + BLK - jnp.arange(2 * BLK)[None, :]
    patterns = [(d, w // d, H_PER_DIL) for w, d in DIL_PATTERNS] + [(1, SWA_WINDOW - 1, H_SWA_Q)]
    buckets = []
    for d, max_dist, heads in patterns:
        band = (rel >= 0) & (rel <= max_dist)
        tile = jnp.where(band, _t5_bucket(jnp.maximum(rel, 0) * d), -1).astype(jnp.int32)
        buckets.append(jnp.broadcast_to(tile, (heads,) + tile.shape))
    buckets = jnp.concatenate(buckets, axis=0)

    def body(table_ref, b_ref, o_ref):
        h = pl.program_id(0)
        bv = b_ref[...]
        tile = jnp.full(bv.shape, NEG, F32)
        for bkt in range(N_BUCKETS):
            tile = jnp.where(bv == bkt, table_ref[h, bkt], tile)
        o_ref[...] = tile

    spec = pl.BlockSpec((None, BLK, 2 * BLK), lambda h: (h, 0, 0))
    tiles = pl.pallas_call(
        body, grid=(buckets.shape[0],), in_specs=[pl.BlockSpec(memory_space=pltpu.SMEM), spec], out_specs=spec,
        out_shape=jax.ShapeDtypeStruct(buckets.shape, F32), name="band_tables", compiler_params=_params(("parallel",)),
    )(rel_bias.T, buckets)
    return tiles[:H_DIL], tiles[H_DIL:], buckets


def _swa_rows(t):
    t = t.reshape(N_BLK, BLK, H_SWA_KV, SWA_GROUP, HEAD_DIM).transpose(2, 0, 3, 1, 4)
    return t.reshape(H_SWA_KV, N_BLK, SWA_GROUP * BLK, HEAD_DIM)


def _swa_tokens(t):
    t = t.reshape(H_SWA_KV, N_BLK, SWA_GROUP, BLK, HEAD_DIM).transpose(1, 3, 0, 2, 4)
    return t.reshape(SEQ, H_SWA_Q * HEAD_DIM)


def _sink_rows(sinks):
    return jnp.broadcast_to(sinks.reshape(H_SWA_KV, SWA_GROUP, 1, 1), (H_SWA_KV, SWA_GROUP, BLK, 1)).reshape(
        H_SWA_KV, SWA_GROUP * BLK, 1)


def _vec(v):
    return v.reshape(1, D_MODEL)


class UnitRows:
    def __init__(self, u, mod_table, gain_table):
        self.shift, self.scale, self.gate = (Row(mod_table, 3 * u + t) for t in range(3))
        self.gain = Row(gain_table, u)


def ffn_forward(x, h, rows, then, w):
    a, b, s = ffn_up(h, w[0], w[1])
    f, xo, *h_next = ffn_down(s, w[2], x, rows.gate, then)
    return xo, (h_next or [None])[0], (x, h, a, b, s, f)


def ffn_backward(u, dxo, df, saved, rows, w, sums, below):
    x, h, a, b, s, _ = saved
    da, db = ffn_bwd_hidden(df, w[2], a, b)
    grads = ffn_grad_weights(h, s, df, da, db)
    dx, sums, *df_below = ffn_bwd_input(da, db, w[0], w[1], x, dxo, rows.gain, rows.scale, sums, u, below)
    return dx, sums, df_below, grads


def mixer_forward(x, h, rows, then, sinks, bias_dil, bias_swa, w):
    proj, qkv = in_proj(h, w[0])
    q_swa, k_swa, v_swa = _swa_rows(qkv[:, 1920:2304] * QK_SCALE), _heads(qkv[:, 2304:2432]), _heads(qkv[:, 2432:2560])
    o_sb, total_sb = sb_forward(qkv)
    bias_dil = bias_dil.reshape(N_GROUPS, 2 * BLK, 2 * BLK)
    views = dilated_views(qkv)
    o_groups, lse_groups = dilated_forward(views, bias_dil)
    o_dil = dilated_merge(o_groups, lse_groups)
    bias_swa = bias_swa.reshape(H_SWA_KV, SWA_GROUP * BLK, 2 * BLK)
    o_swa, lse_swa = banded_forward("swa_forward", q_swa, k_swa, v_swa, bias_swa, _sink_rows(sinks), SWA_HEADS_PER_STEP,
                                    _swa_prev_mask)
    o_cat = jnp.concatenate([o_sb, o_dil, _swa_tokens(o_swa)], axis=1)
    merged = merge_branches(o_cat, w[1], proj)
    mo, xo, *h_next = out_proj(merged, w[2], x, rows.gate, then)
    saved = (x, h, proj, (qkv, total_sb), (views, o_groups, lse_groups),
             (q_swa, k_swa, v_swa, o_swa, lse_swa), o_cat, merged, mo)
    return xo, (h_next or [None])[0], saved


def mixer_backward(u, dxo, dmo, saved, rows, sinks, bias_dil, bias_swa, w, sums, below):
    x, h, proj, sb, dil, swa, o_cat, merged, _ = saved
    tok = pl.BlockSpec((MM_TILE, D_MODEL), lambda j, k: (k, 0))
    g_out = grad_weight("grad_w_out", merged, pl.BlockSpec((MM_TILE, D_SHARD), lambda j, k: (k, j)), dmo, tok,
                        (D_SHARD, D_MODEL))
    du0, du1, du2, dg0, dg1, dg2 = merge_bwd(dmo, w[2], o_cat, w[1], proj)
    du = (du0, du1, du2)
    do_cat = branch_bwd_input(du, w[1])
    g_br = branch_grad_weights(o_cat, du)

    qkv, total_sb = sb
    dq_sb, dk_sb, dv_sb = sb_backward(qkv, total_sb, do_cat)

    views, o_groups, lse_groups = dil
    bias_dil = bias_dil.reshape(N_GROUPS, 2 * BLK, 2 * BLK)
    do_groups, dlse_groups = dilated_merge_bwd(o_groups, lse_groups, do_cat)
    parts, dbias_dil = dilated_backward(views, bias_dil, o_groups, lse_groups, do_groups, dlse_groups)
    dq_dil, dk_dil, dv_dil = dilated_key_grads(parts)
    dbias_dil = dbias_dil.reshape(H_DIL, BLK, 2 * BLK)

    q_swa, k_swa, v_swa, o_swa, lse_swa = swa
    bias_swa = bias_swa.reshape(H_SWA_KV, SWA_GROUP * BLK, 2 * BLK)
    dq_swa, dk_swa, dv_swa, dbias_swa, dsinks = banded_backward(
        "swa_backward", q_swa, k_swa, v_swa, bias_swa, _sink_rows(sinks), o_swa, lse_swa, _swa_rows(do_cat[:, 384:768]),
        jnp.zeros_like(lse_swa), SWA_HEADS_PER_STEP, _swa_prev_mask)
    dbias_swa = dbias_swa.reshape(H_SWA_Q, BLK, 2 * BLK)

    dproj = jnp.concatenate(
        [dq_sb, dk_sb, dv_sb, dq_dil, dk_dil, dv_dil, _swa_tokens(dq_swa), _unheads(dk_swa), _unheads(dv_swa)],
        axis=1).astype(BF16)
    dproj = jnp.concatenate([dproj, dg0, dg1, dg2], axis=1)
    g_in = grad_weight("grad_w_in", h, tok, dproj, pl.BlockSpec((MM_TILE, IN_SHARD), lambda j, k: (k, j)),
                       (D_MODEL, IN_SHARD))
    dx, sums, *df_below = mixer_bwd_input(dproj, w[0], x, dxo, rows.gain, rows.scale, sums, u, below)
    dbias = jnp.concatenate([dbias_dil, dbias_swa], axis=0)
    return dx, sums, df_below, dbias, dsinks[:, :, 0].reshape(H_SWA_Q), (g_in, g_br, g_out)


N_UNITS = 3 * DEPTH


def device_step(x, target, mod, gains, final_gain, sinks, rel_bias, get_weights, put_grads):
    bias_dil, bias_swa, bucket = band_tables(rel_bias)
    mod_table = mod.reshape(3 * N_UNITS, 1, D_MODEL)
    gain_table = gains.reshape(N_UNITS, 1, D_MODEL)
    saved, weights = [], []
    units = [UnitRows(u, mod_table, gain_table) for u in range(N_UNITS)]
    h = prenorm(x, units[0].gain, units[0].scale, units[0].shift)
    for u in range(N_UNITS):
        l, j = divmod(u, 3)
        w = get_weights(u, x)
        then = units[u + 1] if u + 1 < N_UNITS else None
        if j == 1:
            x, h, s = mixer_forward(x, h, units[u], then, sinks[l], bias_dil, bias_swa, w)
        else:
            x, h, s = ffn_forward(x, h, units[u], then, w)
        saved.append(s)
        weights.append(w)
    loss, dx, dfinal = final_loss(x, _vec(final_gain), target)

    sums = (lax.empty((8 * N_UNITS, D_MODEL), F32), lax.empty((8 * N_UNITS, D_MODEL), F32))
    dbias, dsinks = [None] * DEPTH, [None] * DEPTH
    zero = jnp.zeros((1, 1), F32)
    top = N_UNITS - 1
    df, gate_sums = resid_bwd(dx, saved[top][-1], units[top].gate, 0.5, sums[1], top)
    sums = (sums[0], gate_sums)
    for u in reversed(range(N_UNITS)):
        l, j = divmod(u, 3)
        rows = UnitRows(u, mod_table, gain_table + zero)
        below = (saved[u - 1][-1], units[u - 1].gate, 1.0 if (u - 1) % 3 == 1 else 0.5) if u > 0 else None
        if j == 1:
            dx, sums, df, dbias[l], dsinks[l], grads = mixer_backward(
                u, dx, df, saved[u], rows, sinks[l], bias_dil, bias_swa, weights[u], sums, below)
        else:
            dx, sums, df, grads = ffn_backward(u, dx, df, saved[u], rows, weights[u], sums, below)
        df = df[0] if df else None
        if u > 0:
            zero = put_grads(u, grads)
    drel = rel_bias_reduce(dbias[0], dbias[1], bucket)[:, 0, :N_BUCKETS].T
    norm_sums, gate_sums = (t.reshape(DEPTH, 3, 8, D_MODEL) for t in sums)
    dmod = jnp.stack([norm_sums[:, :, 0], norm_sums[:, :, 1], gate_sums[:, :, 0]], axis=2)
    return loss, dx, dmod, norm_sums[:, :, 2], dfinal[0], jnp.stack(dsinks), drel, grads


MESH = pl.DeviceIdType.MESH
CHIP_FLIPS = ((1, 0), (0, 1), (1, 1))
ANY = pl.BlockSpec(memory_space=pl.ANY)


def _position():
    return lax.axis_index("x"), lax.axis_index("y"), lax.axis_index("c")


def all_gather_small(name, piece):
    def body(x_ref, out_ref, send_sems, recv_sems, local_sem):
        x, y, c = _position()
        me, sibling = (x, y, c), (x, y, 1 - c)
        chips = [(x ^ fx, y ^ fy) for fx, fy in CHIP_FLIPS]

        def rows(px, py, pc):
            return out_ref.at[4 * px + 2 * py + pc]

        def copy(k, block, to, src=None):
            return pltpu.make_async_remote_copy(
                src_ref=rows(*block) if src is None else src, dst_ref=rows(*block),
                send_sem=send_sems.at[k], recv_sem=recv_sems.at[k], device_id=to, device_id_type=MESH)

        mine = pltpu.make_async_copy(x_ref, rows(*me), local_sem)
        mine.start()
        first = [copy(0, me, sibling, src=x_ref)]
        first += [copy(1 + j, me, (*chip, c), src=x_ref) for j, chip in enumerate(chips)]
        for cp in first:
            cp.start()
        passed = [copy(4 + j, (*chip, c), sibling) for j, chip in enumerate(chips)]
        for j, chip in enumerate(chips):
            copy(1 + j, (*chip, c), me).wait_recv()
            passed[j].start()
        copy(0, sibling, me).wait_recv()
        for j, chip in enumerate(chips):
            copy(4 + j, (*chip, 1 - c), me).wait_recv()
        for cp in first + passed:
            cp.wait_send()
        mine.wait()

    return pl.pallas_call(
        body, out_shape=jax.ShapeDtypeStruct((N_DEV,) + piece.shape, piece.dtype),
        in_specs=[pl.BlockSpec(memory_space=pltpu.VMEM)], out_specs=pl.BlockSpec(memory_space=pltpu.VMEM),
        scratch_shapes=[pltpu.SemaphoreType.DMA((7,)), pltpu.SemaphoreType.DMA((7,)), pltpu.SemaphoreType.DMA],
        name=name,
    )(piece)


def exchange(name, operands, out_shapes, aliases, plan):
    n_in, n_out = len(operands), len(out_shapes)

    def body(*refs):
        ins, outs = refs[:n_in], refs[n_in:n_in + n_out]
        send_sems, recv_sems, local_sems = refs[n_in + n_out:]
        x, y, c = _position()
        local, sends, recvs = plan(ins, outs, x, y, c)
        local = [pltpu.make_async_copy(s, d, local_sems.at[k]) for k, (s, d) in enumerate(local)]
        for cp in local:
            cp.start()
        remote = [pltpu.make_async_remote_copy(src_ref=s, dst_ref=d, send_sem=send_sems.at[k], recv_sem=recv_sems.at[k],
                                               device_id=dev, device_id_type=MESH)
                  for k, (s, d, dev) in enumerate(sends)]
        for cp in remote:
            cp.start()
        for k, r in enumerate(recvs):
            pltpu.make_async_remote_copy(src_ref=r, dst_ref=r, send_sem=send_sems.at[k], recv_sem=recv_sems.at[k],
                                         device_id=(x, y, c), device_id_type=MESH).wait_recv()
        for cp in remote:
            cp.wait_send()
        for cp in local:
            cp.wait()

    n_sends, n_local = plan.n_sends, max(plan.n_local, 1)
    return pl.pallas_call(
        body, out_shape=out_shapes, in_specs=[ANY] * n_in, out_specs=[ANY] * n_out,
        scratch_shapes=[pltpu.SemaphoreType.DMA((n_sends,)), pltpu.SemaphoreType.DMA((n_sends,)),
                        pltpu.SemaphoreType.DMA((n_local,))],
        input_output_aliases=aliases, name=name,
    )(*operands)


def _plan(n_local, n_sends):
    def wrap(fn):
        fn.n_local, fn.n_sends = n_local, n_sends
        return fn
    return wrap


def _half(ref, axis, c):
    rows = ref.shape[axis] // 2
    idx = [slice(None)] * len(ref.shape)
    idx[axis] = pl.ds(pl.multiple_of(c * rows, 16), rows)
    return ref.at[tuple(idx)]


HBM = pl.BlockSpec(memory_space=pltpu.HBM)
SEM = pl.BlockSpec(memory_space=pltpu.SEMAPHORE)
EFFECT = pltpu.SideEffectType.DATAFLOW_SIDE_EFFECTING


def split_start(name, bufs, extra, n_copies, describe):
    n = len(bufs)

    def body(*refs):
        send_sems, recv_sems = refs[n + len(extra)], refs[n + len(extra) + 1]
        x, y, c = _position()
        for k, (src, dst, _, peer) in enumerate(describe(refs[:n], x, y, c)):
            pltpu.make_async_remote_copy(src_ref=src, dst_ref=dst, send_sem=send_sems.at[k], recv_sem=recv_sems.at[k],
                                         device_id=peer, device_id_type=MESH).start()
        token = refs[-1]
        token[...] = jnp.zeros_like(token)

    out = pl.pallas_call(
        body, name=name,
        out_shape=(pltpu.SemaphoreType.DMA((n_copies,)), pltpu.SemaphoreType.DMA((n_copies,)),
                   *[pltpu.HBM(b.shape, b.dtype) for b in bufs], jax.ShapeDtypeStruct((8, 128), F32)),
        in_specs=[HBM] * n + [ANY] * len(extra),
        out_specs=(SEM, SEM, *[HBM] * n, pl.BlockSpec(memory_space=pltpu.VMEM)),
        input_output_aliases={k: 2 + k for k in range(n)},
        compiler_params=pltpu.CompilerParams(has_side_effects=EFFECT),
    )(*[pltpu.with_memory_space_constraint(b, pltpu.HBM) for b in bufs], *extra)
    return out[0], out[1], list(out[2:2 + n]), out[-1]


def split_wait(name, bufs, send_sems, recv_sems, after, describe):
    n = len(bufs)

    def body(*refs):
        send, recv = refs[n], refs[n + 1]
        x, y, c = _position()
        for k, (src, _, dst, peer) in enumerate(describe(refs[:n], x, y, c)):
            copy = pltpu.make_async_remote_copy(src_ref=src, dst_ref=dst, send_sem=send.at[k], recv_sem=recv.at[k],
                                                device_id=peer, device_id_type=MESH)
            copy.wait_send()
            copy.wait_recv()

    out = pl.pallas_call(
        body, name=name, out_shape=[pltpu.HBM(b.shape, b.dtype) for b in bufs],
        in_specs=[HBM] * n + [SEM, SEM] + [ANY] * len(after), out_specs=[HBM] * n,
        input_output_aliases={k: k for k in range(n)},
        compiler_params=pltpu.CompilerParams(has_side_effects=EFFECT),
    )(*bufs, send_sems, recv_sems, *after)
    return list(out)


def _row_tile(rows, cols, max_elements=256 * 1024):
    best = 16
    for t in range(16, rows + 1, 16):
        if rows % t == 0 and t * cols <= max_elements:
            best = t
    return best


def cast_into_slots(name, shards, chip):
    n = len(shards)
    rows, cols = shards[0][0].shape[-2:]
    tr = _row_tile(rows, cols)

    def body(chip_ref, *refs):
        del chip_ref
        for k in range(n):
            refs[n + k][...] = refs[k][...].astype(BF16)

    def in_spec(param, index):
        return pl.BlockSpec((None,) * len(index) + (tr, cols), lambda r, chip_ref: index + (r, 0))

    return pl.pallas_call(
        body, out_shape=[jax.ShapeDtypeStruct((N_CHIPS, rows, cols), BF16)] * n,
        grid_spec=pltpu.PrefetchScalarGridSpec(
            num_scalar_prefetch=1, grid=(rows // tr,),
            in_specs=[in_spec(p, idx) for p, idx in shards],
            out_specs=[pl.BlockSpec((None, tr, cols), lambda r, chip_ref: (chip_ref[0], r, 0))] * n),
        name=name, compiler_params=_params(("parallel",)),
    )(chip, *[p for p, _ in shards])


GATHER_STAGES = ((0,), (1,), (2,), (3, 4, 5))
REDUCE_STAGES = ((5, 4, 3), (2,), (1,), (0,))


def _gather_copies(slots, x, y, c):
    me = 2 * x + y
    out = []
    for s in slots:
        for fx, fy in CHIP_FLIPS:
            mine = _half(s.at[me], 0, c)
            out.append((mine, mine, _half(s.at[2 * (x ^ fx) + (y ^ fy)], 0, c), (x ^ fx, y ^ fy, c)))
    return out


class WeightStream:
    def __init__(self, shards, chip, after=()):
        self.pending, self.ready = {}, {}
        token = tuple(after)
        for si, units in enumerate(GATHER_STAGES):
            slots = []
            for u in units:
                same = len({p.shape[-2:] for p, _ in shards[u]}) == 1
                for t, group in enumerate([shards[u]] if same else [[s] for s in shards[u]]):
                    slots += cast_into_slots(f"cast_{u}_{t}", group, chip)
            send, recv, slots, tok = split_start(f"gather_start_{si}", slots, token, 3 * len(slots), _gather_copies)
            self.pending[si] = (send, recv, slots)
            token = (tok,)
        self.token = token

    def get(self, u, after):
        if u not in self.ready:
            si = next(k for k, units in enumerate(GATHER_STAGES) if u in units)
            send, recv, slots = self.pending.pop(si)
            slots = split_wait(f"gather_wait_{si}", slots, send, recv, (after,) + self.token, _gather_copies)
            self.token = ()

            @_plan(0, 3 * len(slots))
            def to_sibling(ins, outs, x, y, c):
                sends, recvs = [], []
                for o in outs:
                    for fx, fy in CHIP_FLIPS:
                        slab = o.at[2 * (x ^ fx) + (y ^ fy)]
                        sends.append((_half(slab, 0, c), _half(slab, 0, c), (x, y, 1 - c)))
                        recvs.append(_half(slab, 0, 1 - c))
                return [], sends, recvs

            shapes = [jax.ShapeDtypeStruct(s.shape, BF16) for s in slots]
            slots = exchange(f"gather_sibling_{si}", slots, shapes, {k: k for k in range(len(slots))}, to_sibling)
            for i, v in enumerate(GATHER_STAGES[si]):
                self.ready[v] = tuple(slots[3 * i:3 * i + 3])
        return self.ready[u]


def _reduce_copies(bufs, x, y, c):
    n = len(bufs) // 2
    out = []
    for s, land in zip(bufs[:n], bufs[n:]):
        for k, (fx, fy) in enumerate(CHIP_FLIPS):
            out.append((s.at[2 * (x ^ fx) + (y ^ fy)], land.at[k], land.at[k], (x ^ fx, y ^ fy, c)))
    return out


GRAD_SLOTS = {"gate": (2 * DEPTH, FF_SHARD, D_MODEL), "up": (2 * DEPTH, FF_SHARD, D_MODEL),
              "down": (2 * DEPTH, FF_SHARD, D_MODEL), "in": (DEPTH, D_MODEL, IN_SHARD),
              "br": (DEPTH, BR_ROWS, D_SHARD), "out": (DEPTH, D_SHARD, D_MODEL)}


def _unit_tensors(u):
    l, j = divmod(u, 3)
    if j == 1:
        return [("in", l), ("br", l), ("out", l)]
    return [(k, 2 * l + j // 2) for k in ("gate", "up", "down")]


class GradStream:
    def __init__(self, chip, core):
        self.core = core
        self.place = jnp.concatenate([chip, core])
        self.held, self.flying = {}, []
        self.full = {k: lax.empty(shape, F32) for k, shape in GRAD_SLOTS.items()}

    def put(self, u, grads, after=()):
        self.held[u] = grads
        si = len(self.flying)
        units = REDUCE_STAGES[si]
        if not all(v in self.held for v in units):
            return jnp.zeros((1, 1), F32)
        gs = [g for v in units for g in self.held[v]]

        @_plan(0, len(gs))
        def swap_halves(ins, outs, x, y, c):
            sends = [(_half(g, 1, 1 - c), o, (x, y, 1 - c)) for g, o in zip(ins, outs)]
            return [], sends, list(outs)

        half_shapes = [jax.ShapeDtypeStruct((N_CHIPS, g.shape[1] // 2, g.shape[2]), BF16) for g in gs]
        landed = exchange(f"reduce_swap_{si}", gs + list(after), half_shapes, {}, swap_halves)
        sums = [None] * len(gs)
        for run in _same_shape_runs(gs):
            for k, s in zip(run, _add_halves([gs[k] for k in run], [landed[k] for k in run], self.core)):
                sums[k] = s
        landing = [lax.empty((3,) + s.shape[1:], BF16) for s in sums]
        send, recv, bufs, token = split_start(f"reduce_start_{si}", sums + landing, (), 3 * len(sums), _reduce_copies)
        self.flying.append((send, recv, bufs, [t for v in units for t in _unit_tensors(v)]))
        return token[0:1, 0:1]

    def finish(self, after):
        for si, (send, recv, bufs, tensors) in enumerate(self.flying):
            bufs = split_wait(f"reduce_wait_{si}", bufs, send, recv, tuple(after), _reduce_copies)
            n = len(tensors)
            for run in _same_shape_runs(bufs[:n]):
                names = [tensors[k][0] for k in run]
                out = _add_chips([bufs[k] for k in run], [bufs[n + k] for k in run], self.place,
                                 [self.full[t] for t in names], [tensors[k][1] for k in run])
                self.full.update(zip(names, out))
        names = list(self.full)

        @_plan(0, len(names))
        def share_halves(ins, outs, x, y, c):
            sends = [(_half(o, 1, c), _half(o, 1, c), (x, y, 1 - c)) for o in outs]
            return [], sends, [_half(o, 1, 1 - c) for o in outs]

        shapes = [jax.ShapeDtypeStruct(self.full[k].shape, F32) for k in names]
        out = exchange("reduce_share_halves", [self.full[k] for k in names], shapes, {k: k for k in range(len(names))},
                       share_halves)
        return dict(zip(names, out))


def _same_shape_runs(arrays, longest=3):
    runs = []
    for k, a in enumerate(arrays):
        if runs and len(runs[-1]) < longest and arrays[runs[-1][0]].shape == a.shape:
            runs[-1].append(k)
        else:
            runs.append([k])
    return runs


def _add_halves(gs, landeds, core):
    n = len(gs)
    _, rh, cols = landeds[0].shape
    tr = _row_tile(rh, cols, 1024 * 1024)
    per_half = rh // tr

    def body(core_ref, *refs):
        del core_ref
        for k in range(n):
            refs[2 * n + k][...] = (refs[k][...].astype(F32) + refs[n + k][...].astype(F32)).astype(BF16)

    blk = (None, tr, cols)
    landed_spec = pl.BlockSpec(blk, lambda j, r, core_ref: (j, r, 0))
    return pl.pallas_call(
        body, out_shape=[jax.ShapeDtypeStruct(landeds[0].shape, BF16)] * n,
        grid_spec=pltpu.PrefetchScalarGridSpec(
            num_scalar_prefetch=1, grid=(N_CHIPS, per_half),
            in_specs=[pl.BlockSpec(blk, lambda j, r, core_ref: (j, core_ref[0] * per_half + r, 0))] * n
            + [landed_spec] * n,
            out_specs=[landed_spec] * n),
        name="reduce_add_halves", compiler_params=_params(("parallel", "parallel")),
    )(core, *gs, *landeds)


def _add_chips(sums, landeds, place, fulls, slots):
    n = len(sums)
    _, rh, cols = sums[0].shape
    tr = _row_tile(rh, cols, 1024 * 1024)
    per_half = rh // tr

    def body(place_ref, *refs):
        del place_ref
        for k in range(n):
            s_ref, la_ref, o_ref = refs[k], refs[n + k], refs[3 * n + k]
            o_ref[...] = ((s_ref[...].astype(F32) + la_ref[0].astype(F32)) + la_ref[1].astype(F32)) + la_ref[2].astype(F32)

    def out_spec(slot):
        return pl.BlockSpec((None, tr, cols), lambda r, place_ref: (slot, place_ref[1] * per_half + r, 0))

    return pl.pallas_call(
        body, out_shape=[jax.ShapeDtypeStruct(f.shape, F32) for f in fulls],
        grid_spec=pltpu.PrefetchScalarGridSpec(
            num_scalar_prefetch=1, grid=(per_half,),
            in_specs=[pl.BlockSpec((None, tr, cols), lambda r, place_ref: (place_ref[0], r, 0))] * n
            + [pl.BlockSpec((3, tr, cols), lambda r, place_ref: (0, r, 0))] * n + [ANY] * n,
            out_specs=[out_spec(slot) for slot in slots]),
        input_output_aliases={1 + 2 * n + k: k for k in range(n)}, name="reduce_add_chips",
        compiler_params=_params(("parallel",)),
    )(place, *sums, *landeds, *fulls)


def sum_devices(parts):
    def body(p_ref, o_ref):
        acc = p_ref[0]
        for d in range(1, N_DEV):
            acc = acc + p_ref[d]
        o_ref[...] = acc

    return pl.pallas_call(body, out_shape=jax.ShapeDtypeStruct(parts.shape[1:], F32), name="sum_devices")(parts)


ADA_SHARD = 9 * D_MODEL // N_CHIPS
ADA_TILE = 768
ADA_ROWS = 16


def ada_forward(c_rows, w_ada, b_shard):
    def body(c_ref, w_ref, b_ref, o_ref):
        cv = c_ref[...]
        o_ref[...] = _dot((cv * _sigmoid(cv)).astype(BF16), w_ref[...].astype(BF16), NN) + b_ref[...]

    return pl.pallas_call(
        body, grid=(DEPTH, ADA_SHARD // ADA_TILE),
        in_specs=[pl.BlockSpec((ADA_ROWS, D_MODEL), lambda l, n: (0, 0)),
                  pl.BlockSpec((None, D_MODEL, ADA_TILE), lambda l, n: (l, 0, n)),
                  pl.BlockSpec((None, 1, ADA_TILE), lambda l, n: (l, 0, n))],
        out_specs=pl.BlockSpec((None, ADA_ROWS, ADA_TILE), lambda l, n: (l, 0, n)),
        out_shape=jax.ShapeDtypeStruct((DEPTH, ADA_ROWS, ADA_SHARD), F32),
        name="ada_forward", compiler_params=_params(("parallel", "parallel")),
    )(c_rows, w_ada, b_shard)


def ada_backward(c_rows, dmod_rows):
    def body(c_ref, d_ref, o_ref):
        cv = c_ref[...]
        o_ref[...] = _dot((cv * _sigmoid(cv)).astype(BF16), d_ref[...].astype(BF16), TN)

    return pl.pallas_call(
        body, grid=(DEPTH, ADA_SHARD // ADA_TILE),
        in_specs=[pl.BlockSpec((ADA_ROWS, D_MODEL), lambda l, n: (0, 0)),
                  pl.BlockSpec((None, ADA_ROWS, ADA_TILE), lambda l, n: (l, 0, n))],
        out_specs=pl.BlockSpec((None, D_MODEL, ADA_TILE), lambda l, n: (l, 0, n)),
        out_shape=jax.ShapeDtypeStruct((DEPTH, D_MODEL, ADA_SHARD), F32),
        name="ada_backward", compiler_params=_params(("parallel", "parallel")),
    )(c_rows, dmod_rows)


def adamw(name, w, g, m, v):
    shape = w.shape
    cols = shape[-1]
    rows = w.size // cols
    tr = _row_tile(rows, cols) if rows % 16 == 0 else rows
    c1 = 1.0 / (1.0 - ADAM_B1 ** ADAM_STEP)
    c2 = 1.0 / (1.0 - ADAM_B2 ** ADAM_STEP)

    def body(w_ref, g_ref, m_ref, v_ref, go_ref, d_ref, mo_ref, vo_ref):
        gv = g_ref[...]
        mn = ADAM_B1 * m_ref[...] + (1.0 - ADAM_B1) * gv
        vn = ADAM_B2 * v_ref[...] + (1.0 - ADAM_B2) * (gv * gv)
        go_ref[...] = gv
        mo_ref[...] = mn
        vo_ref[...] = vn
        d_ref[...] = -ADAM_LR * ((mn * c1) / (jnp.sqrt(vn * c2) + ADAM_EPS) + ADAM_WD * w_ref[...])

    spec = pl.BlockSpec((tr, cols), lambda i: (i, 0))
    out = jax.ShapeDtypeStruct((rows, cols), F32)
    res = pl.pallas_call(
        body, grid=(rows // tr,), in_specs=[spec] * 4, out_specs=[spec] * 4, out_shape=[out] * 4,
        name=name, compiler_params=_params(("parallel",)),
    )(*[t.reshape(rows, cols) for t in (w, g, m, v)])
    return tuple(r.reshape(shape) for r in res)


def _pack(parts, rows):
    flat = jnp.concatenate([p.reshape(-1) for p in parts])
    return jnp.pad(flat, (0, rows * 128 - flat.size)).reshape(rows, 128)


def _unpack(flat, shapes):
    out, at = [], 0
    for s in shapes:
        n = math.prod(s)
        out.append(flat[at:at + n].reshape(s))
        at += n
    return out


def kernel(x, c, w_ada, b_ada, norm_gain, w_ffn_gate, w_ffn_up, w_ffn_down, w_in, w_br_sb, w_br_dil, w_br_swa, w_out, sinks, rel_bias, final_gain, loss_target, m_w_ada, m_b_ada, m_norm_gain, m_w_ffn_gate, m_w_ffn_up, m_w_ffn_down, m_w_in, m_w_br_sb, m_w_br_dil, m_w_br_swa, m_w_out, m_sinks, m_rel_bias, m_final_gain, v_w_ada, v_b_ada, v_norm_gain, v_w_ffn_gate, v_w_ffn_up, v_w_ffn_down, v_w_in, v_w_br_sb, v_w_br_dil, v_w_br_swa, v_w_out, v_sinks, v_rel_bias, v_final_gain):
    xi, yi, ci = _position()
    chip = 2 * xi + yi
    dev = 2 * chip + ci

    c_all = all_gather_small("gather_c", c.reshape(8, 128)).reshape(N_DEV, D_MODEL)
    c_rows = jnp.pad(c_all, ((0, ADA_ROWS - N_DEV), (0, 0)))
    b_shard = lax.dynamic_slice_in_dim(b_ada, chip * ADA_SHARD, ADA_SHARD, axis=1).reshape(DEPTH, 1, ADA_SHARD)
    mod_shard = ada_forward(c_rows, w_ada, b_shard)[:, :N_DEV]
    n_mod = DEPTH * N_DEV * ADA_SHARD
    gathered = all_gather_small("gather_mod", _pack([mod_shard, norm_gain], 304))[::2].reshape(N_CHIPS, -1)
    mod_all = gathered[:, :n_mod].reshape(N_CHIPS, DEPTH, N_DEV, ADA_SHARD)
    mod = lax.dynamic_index_in_dim(mod_all, dev, axis=2, keepdims=False)
    mod = mod.transpose(1, 0, 2).reshape(DEPTH, 3, 3, D_MODEL)
    gains = gathered[:, n_mod:n_mod + DEPTH * 3 * D_SHARD].reshape(N_CHIPS, DEPTH, 3, D_SHARD)
    gains = gains.transpose(1, 2, 0, 3).reshape(DEPTH, 3, D_MODEL)

    chip_i, core_i = chip.astype(jnp.int32).reshape(1), ci.astype(jnp.int32).reshape(1)
    w_br = jnp.concatenate([w_br_sb, w_br_dil, w_br_swa], axis=1)
    transposed = (3, 4)
    w_gate_t, w_up_t = jnp.swapaxes(w_ffn_gate, 2, 3), jnp.swapaxes(w_ffn_up, 2, 3)
    shards = []
    for l in range(DEPTH):
        ffn = [[(w_gate_t, (l, f)), (w_up_t, (l, f)), (w_ffn_down, (l, f))] for f in range(2)]
        shards += [ffn[0], [(w_in, (l,)), (w_br, (l,)), (w_out, (l,))], ffn[1]]
    weights_in = WeightStream(shards, chip_i, (gathered,))
    grads_out = GradStream(chip_i, core_i)

    loss, dx, dmod, dgains, dfinal, dsinks, drel, last_grads = device_step(
        x[0], loss_target[0], mod, gains, final_gain, sinks, rel_bias, weights_in.get, grads_out.put)

    small_shapes = [(DEPTH, 9 * D_MODEL), (DEPTH, 3, D_MODEL), (D_MODEL,), (DEPTH, H_SWA_Q), (N_BUCKETS, 12), (1,)]
    small_all = all_gather_small("gather_small_grads", _pack([dmod, dgains, dfinal, dsinks, drel, loss[0, 0:1]], 208))
    started = grads_out.put(0, last_grads, after=(small_all,))
    small_all = small_all + started
    g_b_ada, g_gain_full, g_final, g_sinks, g_rel, loss_sum = _unpack(sum_devices(small_all).reshape(-1), small_shapes)
    g_gain = lax.dynamic_slice_in_dim(g_gain_full, chip * D_SHARD, D_SHARD, axis=2)
    dmod_all = small_all.reshape(N_DEV, -1)[:, :DEPTH * 9 * D_MODEL].reshape(N_DEV, DEPTH, 9 * D_MODEL)
    dmod_rows = lax.dynamic_slice_in_dim(dmod_all, chip * ADA_SHARD, ADA_SHARD, axis=2).transpose(1, 0, 2)
    g_w_ada = ada_backward(c_rows, jnp.pad(dmod_rows, ((0, 0), (0, ADA_ROWS - N_DEV), (0, 0))))

    weights = [w_ada, b_ada, norm_gain, w_ffn_gate, w_ffn_up, w_ffn_down, w_in, w_br_sb, w_br_dil, w_br_swa, w_out,
               sinks, rel_bias, final_gain]
    ms = [m_w_ada, m_b_ada, m_norm_gain, m_w_ffn_gate, m_w_ffn_up, m_w_ffn_down, m_w_in, m_w_br_sb, m_w_br_dil,
          m_w_br_swa, m_w_out, m_sinks, m_rel_bias, m_final_gain]
    vs = [v_w_ada, v_b_ada, v_norm_gain, v_w_ffn_gate, v_w_ffn_up, v_w_ffn_down, v_w_in, v_w_br_sb, v_w_br_dil,
          v_w_br_swa, v_w_out, v_sinks, v_rel_bias, v_final_gain]
    grads = [g_w_ada, g_b_ada, g_gain] + [None] * 8 + [g_sinks, g_rel, g_final]

    deltas, new_ms, new_vs = [None] * 14, [None] * 14, [None] * 14
    for k in (0, 1, 2, 11, 12, 13):
        _, deltas[k], new_ms[k], new_vs[k] = adamw(f"adamw_{k}", weights[k], grads[k], ms[k], vs[k])

    g = grads_out.finish((dx, deltas[0], deltas[1]))
    g_br = g["br"]
    grads[3:11] = [g["gate"].reshape(w_gate_t.shape), g["up"].reshape(w_up_t.shape),
                   g["down"].reshape(w_ffn_down.shape), g["in"], g_br[:, 0:256], g_br[:, 256:384], g_br[:, 384:768],
                   g["out"]]
    for k in range(3, 11):
        state = [weights[k], ms[k], vs[k]]
        if k in transposed:
            state = [jnp.swapaxes(t, 2, 3) for t in state]
        out = adamw(f"adamw_{k}", state[0], grads[k], state[1], state[2])
        if k in transposed:
            out = [jnp.swapaxes(t, 2, 3) for t in out]
        grads[k], deltas[k], new_ms[k], new_vs[k] = out
    return (loss_sum[0], dx[None], *grads, *deltas, *new_ms, *new_vs)
```

```python
import functools
import math

import jax
import jax.numpy as jnp
from jax import lax
from jax.experimental import pallas as pl
from jax.experimental.pallas import tpu as pltpu

F32 = jnp.float32
BF16 = jnp.bfloat16

D_MODEL = 1024
SEQ = 2048
DEPTH = 2
HEAD_DIM = 64
BLK = 128
H_SB = 4
DIL_PATTERNS = ((128, 1), (512, 4), (2048, 16))
H_PER_DIL = 2
H_DIL = 6
H_SWA_Q = 6
H_SWA_KV = 2
SWA_WINDOW = 128
N_BUCKETS = 32
MAX_REL_DIST = 2048
D_FF = 2816
RMS_EPS = 1e-6
N_CHIPS = 4
N_DEV = 8
FF_SHARD = D_FF // N_CHIPS
D_QKV = 2560
D_IN = D_QKV + 3 * D_MODEL
IN_SHARD = D_IN // N_CHIPS
D_SHARD = D_MODEL // N_CHIPS
BR_ROWS = 768
NEG = -1e30
QK_SCALE = HEAD_DIM ** -0.5

ADAM_LR = 0.001
ADAM_B1 = 0.9
ADAM_B2 = 0.999
ADAM_EPS = 1e-08
ADAM_WD = 0.01
ADAM_STEP = 10

VMEM_LIMIT = 56 * 1024 * 1024
ROW_TILE = 256
MM_TILE = 1024

NN = (((1,), (0,)), ((), ()))
NT = (((1,), (1,)), ((), ()))
TN = (((0,), (0,)), ((), ()))


def _params(sem=None):
    return pltpu.CompilerParams(dimension_semantics=sem, vmem_limit_bytes=VMEM_LIMIT)


def _dot(a, b, dims):
    return lax.dot_general(a, b, dims, preferred_element_type=F32)


def _sigmoid(x):
    return 1.0 / (1.0 + jnp.exp(-x))


def _matmul(name, grid, nk, k_axis, dims, n_pairs, in_specs, out_specs, out_shape, acc_shape, epilogue,
            operands, sem, aliases=None, prologue=None):
    n_in = len(in_specs)
    n_out = len(out_specs)

    def partial(ins):
        tot = None
        for p in range(n_pairs):
            a = ins[2 * p][...]
            if prologue is not None:
                a = prologue(p, a, ins)
            d = _dot(a, ins[2 * p + 1][...], dims)
            tot = d if tot is None else tot + d
        return tot

    def body(*refs):
        ins, outs = refs[:n_in], refs[n_in:n_in + n_out]
        ids = tuple(pl.program_id(a) for a in range(len(grid)))
        if nk == 1:
            epilogue(partial(ins), ins, outs, ids)
            return
        acc = refs[n_in + n_out]
        k = ids[k_axis]

        @pl.when(k == 0)
        def _():
            acc[...] = partial(ins)

        @pl.when(k > 0)
        def _():
            acc[...] += partial(ins)

        @pl.when(k == nk - 1)
        def _():
            epilogue(acc[...], ins, outs, ids)

    return pl.pallas_call(
        body, grid=grid, in_specs=in_specs, out_specs=out_specs, out_shape=out_shape,
        scratch_shapes=[] if nk == 1 else [pltpu.VMEM(acc_shape, F32)],
        input_output_aliases=aliases or {}, name=name, compiler_params=_params(sem),
    )(*operands)


def _row_spec(width=D_MODEL):
    return pl.BlockSpec((ROW_TILE, width), lambda i: (i, 0))


def _vec_spec(rows=1, width=D_MODEL):
    return pl.BlockSpec((rows, width), lambda i: (0, 0))


class Row:
    def __init__(self, table, index):
        self.table, self.index = table, index

    def spec(self):
        index = self.index
        return pl.BlockSpec((None, 1, D_MODEL), lambda *ids: (index, 0, 0))


def _slot_spec(u):
    return pl.BlockSpec((8, D_MODEL), lambda *ids: (u, 0))


def prenorm(x, gain, scale, shift):
    def body(x_ref, g_ref, sc_ref, sh_ref, h_ref):
        xv = x_ref[...]
        r = lax.rsqrt(jnp.mean(xv * xv, axis=-1, keepdims=True) + RMS_EPS)
        h_ref[...] = (((xv * r) * g_ref[...]) * (1.0 + sc_ref[...]) + sh_ref[...]).astype(BF16)

    return pl.pallas_call(
        body, grid=(SEQ // ROW_TILE,), in_specs=[_row_spec(), gain.spec(), scale.spec(), shift.spec()],
        out_specs=_row_spec(), out_shape=jax.ShapeDtypeStruct((SEQ, D_MODEL), BF16),
        name="prenorm", compiler_params=_params(("parallel",)),
    )(x, gain.table, scale.table, shift.table)


def resid_bwd(dxo, f, coef, mult, sums, u):
    def body(dx_ref, f_ref, c_ref, sums_in, df_ref, dc_ref):
        del sums_in
        dx = dx_ref[...]
        df_ref[...] = (dx * (mult * c_ref[...])).astype(BF16)
        part = mult * jnp.sum(dx * f_ref[...], axis=0, keepdims=True)

        @pl.when(pl.program_id(0) == 0)
        def _():
            dc_ref[...] = jnp.zeros_like(dc_ref)

        dc_ref[0:1, :] += part

    return pl.pallas_call(
        body, grid=(SEQ // ROW_TILE,),
        in_specs=[_row_spec(), _row_spec(), coef.spec(), pl.BlockSpec(memory_space=pl.ANY)],
        out_specs=[_row_spec(), _slot_spec(u)],
        out_shape=[jax.ShapeDtypeStruct((SEQ, D_MODEL), BF16), jax.ShapeDtypeStruct(sums.shape, F32)],
        input_output_aliases={3: 1}, name="resid_bwd", compiler_params=_params(("arbitrary",)),
    )(dxo, f, coef.table, sums)


def final_loss(x, gain, target):
    def body(x_ref, g_ref, t_ref, loss_ref, dx_ref, dg_ref):
        xv = x_ref[...]
        g = g_ref[...]
        r = lax.rsqrt(jnp.mean(xv * xv, axis=-1, keepdims=True) + RMS_EPS)
        xh = xv * r
        e = xh * g - t_ref[...]
        part = 0.5 * jnp.sum(jnp.mean(e * e, axis=-1, keepdims=True), axis=0, keepdims=True)
        dy = e * (1.0 / D_MODEL)
        dyg = dy * g
        dx_ref[...] = r * (dyg - xh * jnp.mean(dyg * xh, axis=-1, keepdims=True))

        @pl.when(pl.program_id(0) == 0)
        def _():
            loss_ref[...] = jnp.zeros_like(loss_ref)
            dg_ref[...] = jnp.zeros_like(dg_ref)

        loss_ref[...] += jnp.broadcast_to(part, loss_ref.shape)
        dg_ref[0:1, :] += jnp.sum(dy * xh, axis=0, keepdims=True)

    return pl.pallas_call(
        body, grid=(SEQ // ROW_TILE,), in_specs=[_row_spec(), _vec_spec(), _row_spec()],
        out_specs=[_vec_spec(8, 128), _row_spec(), _vec_spec(8)],
        out_shape=[jax.ShapeDtypeStruct((8, 128), F32), jax.ShapeDtypeStruct((SEQ, D_MODEL), F32),
                   jax.ShapeDtypeStruct((8, D_MODEL), F32)],
        name="final_loss", compiler_params=_params(("arbitrary",)),
    )(x, gain, target)


def _prenorm_bwd_epilogue(dh, x_ref, dxo_ref, g_ref, sc_ref, dx_ref, stats_ref, first):
    xv = x_ref[...]
    g = g_ref[...]
    r = lax.rsqrt(jnp.mean(xv * xv, axis=-1, keepdims=True) + RMS_EPS)
    xh = xv * r
    dn = dh * (1.0 + sc_ref[...])
    dxh = dn * g
    dx = dxo_ref[...] + r * (dxh - xh * jnp.mean(dxh * xh, axis=-1, keepdims=True))
    dx_ref[...] = dx

    @pl.when(first)
    def _():
        stats_ref[...] = jnp.zeros_like(stats_ref)

    stats_ref[0:1, :] += jnp.sum(dh, axis=0, keepdims=True)
    stats_ref[1:2, :] += jnp.sum(dh * (xh * g), axis=0, keepdims=True)
    stats_ref[2:3, :] += jnp.sum(dn * xh, axis=0, keepdims=True)
    return dx


def _resid_bwd_epilogue(dx, f_ref, c_ref, mult, df_ref, dc_ref, first):
    df_ref[...] = (dx * (mult * c_ref[...])).astype(BF16)

    @pl.when(first)
    def _():
        dc_ref[...] = jnp.zeros_like(dc_ref)

    dc_ref[0:1, :] += mult * jnp.sum(dx * f_ref[...], axis=0, keepdims=True)


def ffn_up(h, wg_all, wu_all):
    def body(h_ref, wg_ref, wu_ref, a_ref, b_ref, s_ref):
        hv = h_ref[...]
        a = _dot(hv, wg_ref[...], NT)
        b = _dot(hv, wu_ref[...], NT)
        a_ref[...] = a.astype(BF16)
        b_ref[...] = b.astype(BF16)
        s_ref[...] = (a * _sigmoid(a) * b).astype(BF16)

    w_spec = pl.BlockSpec((None, FF_SHARD, D_MODEL), lambda j, i: (j, 0, 0))
    o_spec = pl.BlockSpec((None, MM_TILE, FF_SHARD), lambda j, i: (j, i, 0))
    hid = (N_CHIPS, SEQ, FF_SHARD)
    return pl.pallas_call(
        body, grid=(N_CHIPS, SEQ // MM_TILE),
        in_specs=[pl.BlockSpec((MM_TILE, D_MODEL), lambda j, i: (i, 0)), w_spec, w_spec],
        out_specs=[o_spec, o_spec, o_spec],
        out_shape=[jax.ShapeDtypeStruct(hid, BF16)] * 3,
        name="ffn_up", compiler_params=_params(("parallel", "parallel")),
    )(h, wg_all, wu_all)


def matmul_residual(name, a, a_spec, w_all, w_spec, x, coef, mult, then=None):
    def epilogue(acc, ins, outs, ids):
        outs[0][...] = acc
        xo = ins[2][...] + (mult * ins[3][...]) * acc
        outs[1][...] = xo
        if then is not None:
            r = lax.rsqrt(jnp.mean(xo * xo, axis=-1, keepdims=True) + RMS_EPS)
            outs[2][...] = (((xo * r) * ins[4][...]) * (1.0 + ins[5][...]) + ins[6][...]).astype(BF16)

    row = pl.BlockSpec((MM_TILE, D_MODEL), lambda i, j: (i, 0))
    f32 = jax.ShapeDtypeStruct((SEQ, D_MODEL), F32)
    extra = [] if then is None else [then.gain, then.scale, then.shift]
    return _matmul(
        name, (SEQ // MM_TILE, N_CHIPS), N_CHIPS, 1, NN, 1,
        [a_spec, w_spec, row, coef.spec()] + [t.spec() for t in extra], [row] * (2 + bool(extra)),
        [f32, f32] + [jax.ShapeDtypeStruct((SEQ, D_MODEL), BF16)] * bool(extra), (MM_TILE, D_MODEL), epilogue,
        (a, w_all, x, coef.table) + tuple(t.table for t in extra), ("parallel", "arbitrary"))


def ffn_down(s, wd_all, x, gate, then):
    return matmul_residual(
        "ffn_down", s, pl.BlockSpec((None, MM_TILE, FF_SHARD), lambda i, j: (j, i, 0)),
        wd_all, pl.BlockSpec((None, FF_SHARD, D_MODEL), lambda i, j: (j, 0, 0)), x, gate, 0.5, then)


def ffn_bwd_hidden(df, wd_all, a, b):
    def epilogue(ds, ins, outs, ids):
        av, bv = ins[2][...].astype(F32), ins[3][...].astype(F32)
        sig = _sigmoid(av)
        outs[0][...] = (ds * bv * (sig * (1.0 + av * (1.0 - sig)))).astype(BF16)
        outs[1][...] = (ds * (av * sig)).astype(BF16)

    hid_spec = pl.BlockSpec((None, MM_TILE, FF_SHARD), lambda j, i: (j, i, 0))
    hid = jax.ShapeDtypeStruct((N_CHIPS, SEQ, FF_SHARD), BF16)
    return _matmul(
        "ffn_bwd_hidden", (N_CHIPS, SEQ // MM_TILE), 1, None, NT, 1,
        [pl.BlockSpec((MM_TILE, D_MODEL), lambda j, i: (i, 0)),
         pl.BlockSpec((None, FF_SHARD, D_MODEL), lambda j, i: (j, 0, 0)), hid_spec, hid_spec],
        [hid_spec, hid_spec], [hid, hid], None, epilogue, (df, wd_all, a, b), ("parallel", "parallel"))


def grad_weight(name, lhs, lhs_spec, rhs, rhs_spec, shape):
    def epilogue(acc, ins, outs, ids):
        outs[0][...] = acc.astype(BF16)

    return _matmul(
        name, (N_CHIPS, SEQ // MM_TILE), SEQ // MM_TILE, 1, TN, 1,
        [lhs_spec, rhs_spec], [pl.BlockSpec((None,) + shape, lambda j, k: (j, 0, 0))],
        [jax.ShapeDtypeStruct((N_CHIPS,) + shape, BF16)], shape, epilogue, (lhs, rhs), ("parallel", "arbitrary"))[0]


def ffn_grad_weights(h, s, df, da, db):
    tok = pl.BlockSpec((MM_TILE, D_MODEL), lambda j, k: (k, 0))
    hid = pl.BlockSpec((None, MM_TILE, FF_SHARD), lambda j, k: (j, k, 0))
    n_k = SEQ // MM_TILE

    def body(da_ref, db_ref, h_ref, gg_ref, gu_ref, acc_g, acc_u):
        k = pl.program_id(1)
        hv = h_ref[...]
        parts = (_dot(da_ref[...], hv, TN), _dot(db_ref[...], hv, TN))

        @pl.when(k == 0)
        def _():
            acc_g[...], acc_u[...] = parts

        @pl.when(k > 0)
        def _():
            acc_g[...] += parts[0]
            acc_u[...] += parts[1]

        @pl.when(k == n_k - 1)
        def _():
            gg_ref[...] = acc_g[...].astype(BF16)
            gu_ref[...] = acc_u[...].astype(BF16)

    out = pl.BlockSpec((None, FF_SHARD, D_MODEL), lambda j, k: (j, 0, 0))
    shape = jax.ShapeDtypeStruct((N_CHIPS, FF_SHARD, D_MODEL), BF16)
    g_gate, g_up = pl.pallas_call(
        body, grid=(N_CHIPS, n_k), in_specs=[hid, hid, tok], out_specs=[out, out], out_shape=[shape, shape],
        scratch_shapes=[pltpu.VMEM((FF_SHARD, D_MODEL), F32)] * 2,
        name="grad_w_gate_up", compiler_params=_params(("parallel", "arbitrary")),
    )(da, db, h)
    return g_gate, g_up, grad_weight("grad_w_down", s, hid, df, tok, (FF_SHARD, D_MODEL))


BWD_TILE = 512


def matmul_prenorm_bwd(name, dims, pairs, pair_specs, x, dxo, gain, scale, sums, u, below):
    n_pairs = len(pairs) // 2
    lhs, weights = pairs[0::2], pairs[1::2]
    n_in = n_pairs * 2 + 5 + (3 if below is not None else 0)
    n_out = 2 + (2 if below is not None else 0)

    def body(*refs):
        ins, outs = refs[:n_in], refs[n_in:n_in + n_out]
        w_vmem, w_sems = refs[n_in + n_out:n_in + n_out + n_pairs], refs[-1]
        first = pl.program_id(0) == 0

        @pl.when(first)
        def _():
            copies = [pltpu.make_async_copy(ins[n_pairs + p], w_vmem[p], w_sems.at[p]) for p in range(n_pairs)]
            for cp in copies:
                cp.start()
            for cp in copies:
                cp.wait()

        dh = None
        for j in range(N_CHIPS):
            for p in range(n_pairs):
                part = _dot(pair_specs[p][1](ins[p], j), w_vmem[p][j], dims)
                dh = part if dh is None else dh + part
        k = 2 * n_pairs
        dx = _prenorm_bwd_epilogue(dh, ins[k], ins[k + 1], ins[k + 2], ins[k + 3], outs[0], outs[1], first)
        if below is not None:
            _resid_bwd_epilogue(dx, ins[k + 5], ins[k + 6], below[2], outs[2], outs[3], first)

    row = pl.BlockSpec((BWD_TILE, D_MODEL), lambda i: (i, 0))
    any_spec = pl.BlockSpec(memory_space=pl.ANY)
    f32 = jax.ShapeDtypeStruct((SEQ, D_MODEL), F32)
    in_specs = [s for s, _ in pair_specs] + [any_spec] * n_pairs + [row, row, gain.spec(), scale.spec(), any_spec]
    operands = tuple(lhs) + tuple(weights) + (x, dxo, gain.table, scale.table, sums[0])
    out_specs, out_shape = [row, _slot_spec(u)], [f32, jax.ShapeDtypeStruct(sums[0].shape, F32)]
    aliases = {2 * n_pairs + 4: 1}
    if below is not None:
        in_specs += [row, below[1].spec(), any_spec]
        operands += (below[0], below[1].table, sums[1])
        out_specs += [row, _slot_spec(u - 1)]
        out_shape += [jax.ShapeDtypeStruct((SEQ, D_MODEL), BF16), jax.ShapeDtypeStruct(sums[1].shape, F32)]
        aliases[2 * n_pairs + 7] = 3
    out = pl.pallas_call(
        body, grid=(SEQ // BWD_TILE,), in_specs=in_specs, out_specs=out_specs, out_shape=out_shape,
        scratch_shapes=[pltpu.VMEM(w.shape, w.dtype) for w in weights] + [pltpu.SemaphoreType.DMA((n_pairs,))],
        input_output_aliases=aliases, name=name, compiler_params=_params(("arbitrary",)),
    )(*operands)
    if below is None:
        return out[0], (out[1], sums[1])
    return out[0], (out[1], out[3]), out[2]


def ffn_bwd_input(da, db, wg_all, wu_all, x, dxo, gain, scale, sums, u, below):
    hid = (pl.BlockSpec((N_CHIPS, BWD_TILE, FF_SHARD), lambda i: (0, i, 0)), lambda ref, j: ref[j])
    return matmul_prenorm_bwd("ffn_bwd_input", NN, (da, wg_all, db, wu_all), (hid, hid), x, dxo, gain, scale,
                              sums, u, below)


def in_proj(h, w_all):
    def epilogue(acc, ins, outs, ids):
        outs[0][...] = acc
        outs[1][...] = acc.astype(BF16)

    out = pl.BlockSpec((MM_TILE, IN_SHARD), lambda j, i: (i, j))
    return _matmul(
        "in_proj", (N_CHIPS, SEQ // MM_TILE), 1, None, NN, 1,
        [pl.BlockSpec((MM_TILE, D_MODEL), lambda j, i: (i, 0)),
         pl.BlockSpec((None, D_MODEL, IN_SHARD), lambda j, i: (j, 0, 0))],
        [out, out], [jax.ShapeDtypeStruct((SEQ, D_IN), F32), jax.ShapeDtypeStruct((SEQ, D_IN), BF16)],
        None, epilogue, (h, w_all), ("parallel", "parallel"))


_GATE_BLOCK0 = D_QKV // D_SHARD


def _branch_products(o, w_ref):
    ob = o.astype(BF16)
    return (_dot(ob[:, 0:256], w_ref[0:256, :], NN), _dot(ob[:, 256:384], w_ref[256:384, :], NN),
            _dot(ob[:, 384:768], w_ref[384:768, :], NN))


def merge_branches(o_cat, wbr_all, proj):
    def body(o_ref, w_ref, g0_ref, g1_ref, g2_ref, m_ref):
        u = _branch_products(o_ref[...], w_ref)
        m_ref[...] = (_sigmoid(g0_ref[...]) * u[0] + _sigmoid(g1_ref[...]) * u[1]
                      + _sigmoid(g2_ref[...]) * u[2]).astype(BF16)

    def gate_spec(b):
        return pl.BlockSpec((MM_TILE, D_SHARD), lambda i, j: (i, _GATE_BLOCK0 + 4 * b + j))

    return pl.pallas_call(
        body, grid=(SEQ // MM_TILE, N_CHIPS),
        in_specs=[pl.BlockSpec((MM_TILE, BR_ROWS), lambda i, j: (i, 0)),
                  pl.BlockSpec((None, BR_ROWS, D_SHARD), lambda i, j: (j, 0, 0)),
                  gate_spec(0), gate_spec(1), gate_spec(2)],
        out_specs=pl.BlockSpec((MM_TILE, D_SHARD), lambda i, j: (i, j)),
        out_shape=jax.ShapeDtypeStruct((SEQ, D_MODEL), BF16),
        name="merge_branches", compiler_params=_params(("parallel", "parallel")),
    )(o_cat, wbr_all, proj, proj, proj)


def out_proj(merged, wout_all, x, gate, then):
    return matmul_residual(
        "out_proj", merged, pl.BlockSpec((MM_TILE, D_SHARD), lambda i, j: (i, j)),
        wout_all, pl.BlockSpec((None, D_SHARD, D_MODEL), lambda i, j: (j, 0, 0)), x, gate, 1.0, then)


def merge_bwd(dmo, wout_all, o_cat, wbr_all, proj):
    def epilogue(dm, ins, outs, ids):
        u = _branch_products(ins[2][...], ins[3])
        for b in range(3):
            sig = _sigmoid(ins[4 + b][...])
            outs[b][...] = (dm * sig).astype(BF16)
            outs[3 + b][...] = (dm * u[b] * (sig * (1.0 - sig))).astype(BF16)

    def gate_spec(b):
        return pl.BlockSpec((MM_TILE, D_SHARD), lambda j, i: (i, _GATE_BLOCK0 + 4 * b + j))

    col = pl.BlockSpec((MM_TILE, D_SHARD), lambda j, i: (i, j))
    du = jax.ShapeDtypeStruct((SEQ, D_MODEL), BF16)
    return _matmul(
        "merge_bwd", (N_CHIPS, SEQ // MM_TILE), 1, None, NT, 1,
        [pl.BlockSpec((MM_TILE, D_MODEL), lambda j, i: (i, 0)),
         pl.BlockSpec((None, D_SHARD, D_MODEL), lambda j, i: (j, 0, 0)),
         pl.BlockSpec((MM_TILE, BR_ROWS), lambda j, i: (i, 0)),
         pl.BlockSpec((None, BR_ROWS, D_SHARD), lambda j, i: (j, 0, 0)),
         gate_spec(0), gate_spec(1), gate_spec(2)],
        [col] * 6, [du] * 6,
        None, epilogue, (dmo, wout_all, o_cat, wbr_all, proj, proj, proj), ("parallel", "parallel"))


def branch_bwd_input(du, wbr_all):
    def body(d0_ref, d1_ref, d2_ref, w_ref, o_ref, acc):
        j = pl.program_id(1)
        parts = (_dot(d0_ref[...], w_ref[0:256, :], NT), _dot(d1_ref[...], w_ref[256:384, :], NT),
                 _dot(d2_ref[...], w_ref[384:768, :], NT))

        @pl.when(j == 0)
        def _():
            acc[:, 0:256], acc[:, 256:384], acc[:, 384:768] = parts

        @pl.when(j > 0)
        def _():
            acc[:, 0:256] += parts[0]
            acc[:, 256:384] += parts[1]
            acc[:, 384:768] += parts[2]

        @pl.when(j == N_CHIPS - 1)
        def _():
            o_ref[...] = acc[...]

    col = pl.BlockSpec((MM_TILE, D_SHARD), lambda i, j: (i, j))
    return pl.pallas_call(
        body, grid=(SEQ // MM_TILE, N_CHIPS),
        in_specs=[col, col, col, pl.BlockSpec((None, BR_ROWS, D_SHARD), lambda i, j: (j, 0, 0))],
        out_specs=pl.BlockSpec((MM_TILE, BR_ROWS), lambda i, j: (i, 0)),
        out_shape=jax.ShapeDtypeStruct((SEQ, BR_ROWS), F32),
        scratch_shapes=[pltpu.VMEM((MM_TILE, BR_ROWS), F32)],
        name="branch_bwd_input", compiler_params=_params(("parallel", "arbitrary")),
    )(du[0], du[1], du[2], wbr_all)


def branch_grad_weights(o_cat, du):
    def body(o_ref, d0_ref, d1_ref, d2_ref, g_ref, acc):
        k = pl.program_id(1)
        ob = o_ref[...].astype(BF16)
        parts = (_dot(ob[:, 0:256], d0_ref[...], TN), _dot(ob[:, 256:384], d1_ref[...], TN),
                 _dot(ob[:, 384:768], d2_ref[...], TN))

        @pl.when(k == 0)
        def _():
            acc[0:256, :], acc[256:384, :], acc[384:768, :] = parts

        @pl.when(k > 0)
        def _():
            acc[0:256, :] += parts[0]
            acc[256:384, :] += parts[1]
            acc[384:768, :] += parts[2]

        @pl.when(k == SEQ // MM_TILE - 1)
        def _():
            g_ref[...] = acc[...].astype(BF16)

    col = pl.BlockSpec((MM_TILE, D_SHARD), lambda j, k: (k, j))
    return pl.pallas_call(
        body, grid=(N_CHIPS, SEQ // MM_TILE),
        in_specs=[pl.BlockSpec((MM_TILE, BR_ROWS), lambda j, k: (k, 0)), col, col, col],
        out_specs=pl.BlockSpec((None, BR_ROWS, D_SHARD), lambda j, k: (j, 0, 0)),
        out_shape=jax.ShapeDtypeStruct((N_CHIPS, BR_ROWS, D_SHARD), BF16),
        scratch_shapes=[pltpu.VMEM((BR_ROWS, D_SHARD), F32)],
        name="branch_grad_weights", compiler_params=_params(("parallel", "arbitrary")),
    )(o_cat, du[0], du[1], du[2])


def mixer_bwd_input(dproj, win_all, x, dxo, gain, scale, sums, u, below):
    columns = (pl.BlockSpec((BWD_TILE, D_IN), lambda i: (i, 0)),
               lambda ref, j: ref[:, IN_SHARD * j:IN_SHARD * (j + 1)])
    return matmul_prenorm_bwd("mixer_bwd_input", NT, (dproj, win_all), (columns,), x, dxo, gain, scale, sums, u, below)


BATCH_QK = (((2,), (2,)), ((0,), (0,)))
BATCH_PV = (((2,), (1,)), ((0,), (0,)))
BATCH_TN = (((1,), (1,)), ((0,), (0,)))


SB_WIDTH = H_SB * HEAD_DIM
SB_ROWS = H_SB * BLK


def _split_dot(v, tri):
    hi = v.astype(BF16)
    lo = (v - hi.astype(F32)).astype(BF16)
    return _dot(hi, tri, NN) + _dot(lo, tri, NN)


def _tri(cmp):
    return cmp(lax.broadcasted_iota(jnp.int32, (BLK, BLK), 0), lax.broadcasted_iota(jnp.int32, (BLK, BLK), 1)).astype(BF16)


def _head_masks():
    lane = lax.broadcasted_iota(jnp.int32, (1, SB_WIDTH), 1) // HEAD_DIM
    return [lane == h for h in range(H_SB)]


def _stack_heads(x, masks):
    return jnp.concatenate([jnp.where(m, x, jnp.zeros_like(x)) for m in masks], axis=0)


def _merge_heads(y, masks):
    out = jnp.where(masks[0], y[0:BLK], 0.0)
    for h in range(1, H_SB):
        out = jnp.where(masks[h], y[h * BLK:(h + 1) * BLK], out)
    return out


def _sb_scores(q4, k_ref, j, diagonal):
    rows = pl.ds(pl.multiple_of(j * BLK, BLK), BLK)
    z = _dot(q4, k_ref[rows, :], NT)
    log_fail = -(jnp.maximum(z, 0.0) + jnp.log(1.0 + jnp.exp(-jnp.abs(z))))
    log_hit = z + log_fail
    before = None
    if diagonal:
        tile = (SB_ROWS, BLK)
        before = lax.broadcasted_iota(jnp.int32, tile, 1) < (lax.broadcasted_iota(jnp.int32, tile, 0) & (BLK - 1))
        log_fail = jnp.where(before, log_fail, 0.0)
    return rows, before, log_fail, log_hit


def _keep(before, x):
    return x if before is None else jnp.where(before, x, 0.0)


def sb_forward(qkv):
    def body(q_ref, k_ref, v_ref, o_ref, tot_ref):
        i = pl.program_id(0)
        masks = _head_masks()
        q4 = _stack_heads(q_ref[...] * QK_SCALE, masks)
        later = _tri(lambda r, c: r > c)

        def tiles(js, carry, diagonal):
            o, run = carry
            scores = [_sb_scores(q4, k_ref, j, diagonal) for j in js]
            acc = None
            for rows, before, log_fail, log_hit in scores:
                between = _split_dot(log_fail, later) + run
                w = _keep(before, jnp.exp(log_hit + between))
                part = _dot(w.astype(BF16), v_ref[rows, :], NN)
                acc = part if acc is None else acc + part
                run = run + jnp.sum(log_fail, axis=1, keepdims=True)
            return o + _merge_heads(acc, masks), run

        carry = tiles([i], (jnp.zeros((BLK, SB_WIDTH), F32), jnp.zeros((SB_ROWS, 1), F32)), True)
        carry = lax.cond((i & 1) != 0, lambda c: tiles([i - 1], c, False), lambda c: c, carry)
        at = i - 1 - (i & 1)
        carry = lax.cond((i & 2) != 0, lambda c: tiles([at, at - 1], c, False), lambda c: c, carry)
        at = at - (i & 2)
        o, run = lax.fori_loop(0, i // 4, lambda t, c: tiles([at - 4 * t - n for n in range(4)], c, False), carry)
        o_ref[...] = o
        tot_ref[...] = run

    return pl.pallas_call(
        body, grid=(N_BLK,),
        in_specs=[pl.BlockSpec((BLK, SB_WIDTH), lambda i: (i, 0)), pl.BlockSpec((SEQ, SB_WIDTH), lambda i: (0, 1)),
                  pl.BlockSpec((SEQ, SB_WIDTH), lambda i: (0, 2))],
        out_specs=[pl.BlockSpec((BLK, SB_WIDTH), lambda i: (i, 0)), pl.BlockSpec((None, SB_ROWS, 1), lambda i: (i, 0, 0))],
        out_shape=[jax.ShapeDtypeStruct((SEQ, SB_WIDTH), F32), jax.ShapeDtypeStruct((N_BLK, SB_ROWS, 1), F32)],
        name="sb_forward", compiler_params=_params(("parallel",)),
    )(qkv, qkv, qkv)


def sb_backward(qkv, total, do_cat):
    def body(q_ref, k_ref, v_ref, tot_ref, do_ref, dq_ref, dk_ref, dv_ref):
        i = pl.program_id(0)

        @pl.when(i == 0)
        def _():
            dk_ref[...] = jnp.zeros_like(dk_ref)
            dv_ref[...] = jnp.zeros_like(dv_ref)

        masks = _head_masks()
        q4 = _stack_heads(q_ref[...] * QK_SCALE, masks)
        do4 = _stack_heads(do_ref[...].astype(BF16), masks)
        total_v = tot_ref[...]
        upto = _tri(lambda r, c: r <= c)
        earlier = _tri(lambda r, c: r < c)

        def tiles(js, carry, diagonal):
            dq, seen, g_seen = carry
            scores = [_sb_scores(q4, k_ref, j, diagonal) for j in js]
            acc = None
            for rows, before, log_fail, log_hit in scores:
                between = total_v - (seen + _split_dot(log_fail, upto))
                w = _keep(before, jnp.exp(log_hit + between))
                g = _dot(do4, v_ref[rows, :], NT) * w
                g_earlier = g_seen + _split_dot(g, earlier)
                sig = jnp.exp(log_hit)
                dz = _keep(before, g * (1.0 - sig) - g_earlier * sig).astype(BF16)
                part = _dot(dz, k_ref[rows, :], NN)
                acc = part if acc is None else acc + part
                dk_ref[rows, :] += _dot(dz, q4, TN)
                dv_ref[rows, :] += _dot(w.astype(BF16), do4, TN)
                seen = seen + jnp.sum(log_fail, axis=1, keepdims=True)
                g_seen = g_seen + jnp.sum(g, axis=1, keepdims=True)
            return dq + _merge_heads(acc, masks), seen, g_seen

        zero = jnp.zeros((SB_ROWS, 1), F32)
        carry = lax.fori_loop(0, i // 4, lambda t, c: tiles([4 * t + n for n in range(4)], c, False),
                              (jnp.zeros((BLK, SB_WIDTH), F32), zero, zero))
        at = i - (i & 3)
        carry = lax.cond((i & 2) != 0, lambda c: tiles([at, at + 1], c, False), lambda c: c, carry)
        carry = lax.cond((i & 1) != 0, lambda c: tiles([i - 1], c, False), lambda c: c, carry)
        dq, _, _ = tiles([i], carry, True)
        dq_ref[...] = dq * QK_SCALE

    blk = pl.BlockSpec((BLK, SB_WIDTH), lambda i: (i, 0))
    full = pl.BlockSpec((SEQ, SB_WIDTH), lambda i: (0, 0))
    shape = jax.ShapeDtypeStruct((SEQ, SB_WIDTH), F32)
    return pl.pallas_call(
        body, grid=(N_BLK,),
        in_specs=[blk, pl.BlockSpec((SEQ, SB_WIDTH), lambda i: (0, 1)), pl.BlockSpec((SEQ, SB_WIDTH), lambda i: (0, 2)),
                  pl.BlockSpec((None, SB_ROWS, 1), lambda i: (i, 0, 0)), blk],
        out_specs=[blk, full, full], out_shape=[shape, shape, shape],
        name="sb_backward", compiler_params=_params(("arbitrary",)),
    )(qkv, qkv, qkv, total, do_cat)


def _band_scores(q_ref, kp_ref, ko_ref, bias_ref, hb, prev_mask):
    b = pl.program_id(1)
    qs = q_ref[...]
    s_prev = _dot(qs, kp_ref[...], BATCH_QK) + bias_ref[:, :, 0:BLK]
    s_prev = jnp.concatenate(
        [jnp.where((b & prev_mask(pl.program_id(0) * hb + t)) != 0, s_prev[t:t + 1], NEG) for t in range(hb)], axis=0)
    s_own = _dot(qs, ko_ref[...], BATCH_QK) + bias_ref[:, :, BLK:2 * BLK]
    return qs, s_prev, s_own


def _band_specs(hb, rows, t_n):
    def q_spec(width):
        return pl.BlockSpec((hb, None, rows, width), lambda h, b: (h, b, 0, 0))

    own = pl.BlockSpec((hb, BLK, HEAD_DIM), lambda h, b: (h, b, 0))
    prev = pl.BlockSpec((hb, BLK, HEAD_DIM), lambda h, b: (h, jnp.maximum(b - 1, 0), 0))
    per_head = lambda r, width: pl.BlockSpec((hb, r, width), lambda h, b: (h, 0, 0))
    return q_spec, own, prev, per_head


def banded_forward(name, q, k, v, bias, sinks, hb, prev_mask):
    h_n, nb, rows, _ = q.shape

    def body(q_ref, kp_ref, ko_ref, vp_ref, vo_ref, bias_ref, sink_ref, o_ref, lse_ref):
        _, s_prev, s_own = _band_scores(q_ref, kp_ref, ko_ref, bias_ref, hb, prev_mask)
        sink = sink_ref[...]
        m = jnp.maximum(jnp.maximum(jnp.max(s_prev, axis=2, keepdims=True), jnp.max(s_own, axis=2, keepdims=True)), sink)
        p_prev = jnp.exp(s_prev - m)
        p_own = jnp.exp(s_own - m)
        denom = jnp.sum(p_prev, axis=2, keepdims=True) + jnp.sum(p_own, axis=2, keepdims=True) + jnp.exp(sink - m)
        o = _dot(p_prev.astype(BF16), vp_ref[...], BATCH_PV) + _dot(p_own.astype(BF16), vo_ref[...], BATCH_PV)
        o_ref[...] = o / denom
        lse_ref[...] = m + jnp.log(denom)

    q_spec, own, prev, per_head = _band_specs(hb, rows, k.shape[1])
    return pl.pallas_call(
        body, grid=(h_n // hb, nb),
        in_specs=[q_spec(HEAD_DIM), prev, own, prev, own, per_head(rows, 2 * BLK), per_head(rows, 1)],
        out_specs=[q_spec(HEAD_DIM), q_spec(1)],
        out_shape=[jax.ShapeDtypeStruct(q.shape, F32), jax.ShapeDtypeStruct((h_n, nb, rows, 1), F32)],
        name=name, compiler_params=_params(("parallel", "parallel")),
    )(q, k, k, v, v, bias, sinks)


def banded_backward(name, q, k, v, bias, sinks, o, lse, do, dlse, hb, prev_mask):
    h_n, nb, rows, _ = q.shape
    t_n = k.shape[1]

    def body(q_ref, kp_ref, ko_ref, vp_ref, vo_ref, bias_ref, sink_ref, o_ref, lse_ref, do_ref, dlse_ref,
             dq_ref, dk_ref, dv_ref, dbias_ref, dsink_ref):
        b = pl.program_id(1)

        @pl.when(b == 0)
        def _():
            dk_ref[...] = jnp.zeros_like(dk_ref)
            dv_ref[...] = jnp.zeros_like(dv_ref)
            dbias_ref[...] = jnp.zeros_like(dbias_ref)
            dsink_ref[...] = jnp.zeros_like(dsink_ref)

        qs, s_prev, s_own = _band_scores(q_ref, kp_ref, ko_ref, bias_ref, hb, prev_mask)
        lse_v = lse_ref[...]
        dov = do_ref[...]
        dob = dov.astype(BF16)
        shift = dlse_ref[...] - jnp.sum(dov * o_ref[...], axis=2, keepdims=True)
        p_prev = jnp.exp(s_prev - lse_v)
        p_own = jnp.exp(s_own - lse_v)
        ds_prev = p_prev * (_dot(dob, vp_ref[...], BATCH_QK) + shift)
        ds_own = p_own * (_dot(dob, vo_ref[...], BATCH_QK) + shift)
        dbias_ref[:, :, 0:BLK] += ds_prev
        dbias_ref[:, :, BLK:2 * BLK] += ds_own
        d_sink = jnp.exp(sink_ref[...] - lse_v) * shift
        for g in range(rows // BLK):
            dsink_ref[:, g:g + 1, :] += jnp.sum(d_sink[:, g * BLK:(g + 1) * BLK, :], axis=1, keepdims=True)
        ds_prev = ds_prev.astype(BF16)
        ds_own = ds_own.astype(BF16)
        dq_ref[...] = (_dot(ds_prev, kp_ref[...], BATCH_PV) + _dot(ds_own, ko_ref[...], BATCH_PV)) * QK_SCALE
        rows_prev = pl.ds(pl.multiple_of(jnp.maximum(b - 1, 0) * BLK, BLK), BLK)
        rows_own = pl.ds(pl.multiple_of(b * BLK, BLK), BLK)
        dk_ref[:, rows_prev, :] += _dot(ds_prev, qs, BATCH_TN)
        dk_ref[:, rows_own, :] += _dot(ds_own, qs, BATCH_TN)
        dv_ref[:, rows_prev, :] += _dot(p_prev.astype(BF16), dob, BATCH_TN)
        dv_ref[:, rows_own, :] += _dot(p_own.astype(BF16), dob, BATCH_TN)

    q_spec, own, prev, per_head = _band_specs(hb, rows, t_n)
    kv_full = per_head(t_n, HEAD_DIM)
    kv_shape = jax.ShapeDtypeStruct((h_n, t_n, HEAD_DIM), F32)
    return pl.pallas_call(
        body, grid=(h_n // hb, nb),
        in_specs=[q_spec(HEAD_DIM), prev, own, prev, own, per_head(rows, 2 * BLK), per_head(rows, 1),
                  q_spec(HEAD_DIM), q_spec(1), q_spec(HEAD_DIM), q_spec(1)],
        out_specs=[q_spec(HEAD_DIM), kv_full, kv_full, per_head(rows, 2 * BLK), per_head(rows // BLK, BLK)],
        out_shape=[jax.ShapeDtypeStruct(q.shape, F32), kv_shape, kv_shape,
                   jax.ShapeDtypeStruct((h_n, rows, 2 * BLK), F32), jax.ShapeDtypeStruct((h_n, rows // BLK, BLK), F32)],
        name=name, compiler_params=_params(("parallel", "arbitrary")),
    )(q, k, k, v, v, bias, sinks, o, lse, do, dlse)


def _swa_prev_mask(head):
    del head
    return 15


SWA_HEADS_PER_STEP = 2
SWA_GROUP = H_SWA_Q // H_SWA_KV
N_BLK = SEQ // BLK


GROUP_W = H_PER_DIL * HEAD_DIM
DIL_COLUMNS = (768, 1920)
LANE_BLOCKS = (DIL_COLUMNS[1] - DIL_COLUMNS[0]) // GROUP_W
DIL_Q_BLOCK, DIL_K_BLOCK, DIL_V_BLOCK = 0, 3, 6
N_GROUPS = len(DIL_PATTERNS)


def dilated_views(qkv):
    cols = qkv[:, DIL_COLUMNS[0]:DIL_COLUMNS[1]]
    return [_dil_view(cols, d) for _, d in DIL_PATTERNS]


def _dil_view(t, d):
    return t.reshape(SEQ // d, d * t.shape[1])


def _dil_tile(n, d):
    per_class = N_BLK // d
    return n // per_class, n % per_class


def _dil_spec(d, lane_block, lane_blocks, shift=0):
    def index(n):
        r, m = _dil_tile(n, d)
        m = jnp.clip(m + shift, 0, N_BLK // d - 1)
        return m, r * lane_blocks + lane_block
    return pl.BlockSpec((BLK, GROUP_W), index)


def _two_heads(x, first):
    zero = jnp.zeros_like(x)
    return jnp.concatenate([jnp.where(first, x, zero), jnp.where(first, zero, x)], axis=0)


def _per_head(col, first):
    return jnp.where(first, col[0:BLK], col[BLK:2 * BLK])


def _head_rows(tile, first):
    pick = lambda keep: jnp.max(jnp.where(keep, tile, -jnp.inf), axis=1, keepdims=True)
    return jnp.concatenate([pick(first), pick(jnp.logical_not(first))], axis=0)


def _dil_scores(q_ref, kp_ref, ko_ref, bias, has_prev, first):
    q2 = _two_heads(q_ref[...] * QK_SCALE, first)
    k2 = jnp.concatenate([kp_ref[...], ko_ref[...]], axis=0)
    s = _dot(q2, k2, NT) + bias
    key = lax.broadcasted_iota(jnp.int32, s.shape, 1)
    return q2, k2, jnp.where(jnp.logical_or(has_prev, key >= BLK), s, NEG)


def dilated_forward(views, bias):
    def body(*refs):
        ins, bias_ref, outs = refs[:5 * N_GROUPS], refs[5 * N_GROUPS], refs[5 * N_GROUPS + 1:]
        n = pl.program_id(0)
        first = lax.broadcasted_iota(jnp.int32, (1, GROUP_W), 1) < HEAD_DIM
        for g, (_, d) in enumerate(DIL_PATTERNS):
            q_ref, kp_ref, ko_ref, vp_ref, vo_ref = ins[5 * g:5 * g + 5]
            has_prev = _dil_tile(n, d)[1] > 0
            _, _, s = _dil_scores(q_ref, kp_ref, ko_ref, bias_ref[g], has_prev, first)
            m = jnp.max(s, axis=1, keepdims=True)
            p = jnp.exp(s - m)
            denom = jnp.sum(p, axis=1, keepdims=True)
            v2 = jnp.concatenate([vp_ref[...], vo_ref[...]], axis=0)
            o2 = _dot(p.astype(BF16), v2, NN) / denom
            outs[2 * g][...] = _per_head(o2, first)
            outs[2 * g + 1][...] = _per_head(m + jnp.log(denom), first)

    operands, in_specs, out_specs, out_shape = [], [], [], []
    for g, (_, d) in enumerate(DIL_PATTERNS):
        operands += [views[g]] * 5
        in_specs += [_dil_spec(d, DIL_Q_BLOCK + g, LANE_BLOCKS), _dil_spec(d, DIL_K_BLOCK + g, LANE_BLOCKS, -1),
                     _dil_spec(d, DIL_K_BLOCK + g, LANE_BLOCKS), _dil_spec(d, DIL_V_BLOCK + g, LANE_BLOCKS, -1),
                     _dil_spec(d, DIL_V_BLOCK + g, LANE_BLOCKS)]
        out_specs += [_dil_spec(d, 0, 1)] * 2
        out_shape += [jax.ShapeDtypeStruct((SEQ // d, d * GROUP_W), F32)] * 2
    out = pl.pallas_call(
        body, grid=(N_BLK,), in_specs=in_specs + [pl.BlockSpec((N_GROUPS, 2 * BLK, 2 * BLK), lambda n: (0, 0, 0))],
        out_specs=out_specs, out_shape=out_shape, name="dilated_forward", compiler_params=_params(("parallel",)),
    )(*operands, bias)
    out = [t.reshape(SEQ, GROUP_W) for t in out]
    return out[0::2], out[1::2]


def _group_softmax(lses):
    m = jnp.maximum(jnp.maximum(lses[0], lses[1]), lses[2])
    e = [jnp.exp(l - m) for l in lses]
    total = e[0] + e[1] + e[2]
    return [t / total for t in e]


def dilated_merge(o, lse):
    def body(*refs):
        alpha = _group_softmax([r[...] for r in refs[N_GROUPS:2 * N_GROUPS]])
        refs[-1][...] = alpha[0] * refs[0][...] + alpha[1] * refs[1][...] + alpha[2] * refs[2][...]

    spec = pl.BlockSpec((ROW_TILE, GROUP_W), lambda i: (i, 0))
    return pl.pallas_call(
        body, grid=(SEQ // ROW_TILE,), in_specs=[spec] * (2 * N_GROUPS), out_specs=spec,
        out_shape=jax.ShapeDtypeStruct((SEQ, GROUP_W), F32), name="dilated_merge", compiler_params=_params(("parallel",)),
    )(*o, *lse)


def dilated_merge_bwd(o, lse, do_cat):
    def body(*refs):
        o_v = [r[...] for r in refs[:N_GROUPS]]
        alpha = _group_softmax([r[...] for r in refs[N_GROUPS:2 * N_GROUPS]])
        dout = refs[2 * N_GROUPS][...]
        outs = refs[2 * N_GROUPS + 1:]
        first = lax.broadcasted_iota(jnp.int32, (1, GROUP_W), 1) < HEAD_DIM

        def head_sum(x):
            a = jnp.sum(jnp.where(first, x, 0.0), axis=1, keepdims=True)
            b = jnp.sum(jnp.where(first, 0.0, x), axis=1, keepdims=True)
            return jnp.where(first, a, b)

        dalpha = [head_sum(dout * o_g) for o_g in o_v]
        mean = alpha[0] * dalpha[0] + alpha[1] * dalpha[1] + alpha[2] * dalpha[2]
        for g in range(N_GROUPS):
            outs[g][...] = alpha[g] * dout
            outs[N_GROUPS + g][...] = alpha[g] * (dalpha[g] - mean)

    spec = pl.BlockSpec((ROW_TILE, GROUP_W), lambda i: (i, 0))
    shape = jax.ShapeDtypeStruct((SEQ, GROUP_W), F32)
    out = pl.pallas_call(
        body, grid=(SEQ // ROW_TILE,), in_specs=[spec] * (2 * N_GROUPS) + [pl.BlockSpec((ROW_TILE, GROUP_W), lambda i: (i, 2))],
        out_specs=[spec] * (2 * N_GROUPS), out_shape=[shape] * (2 * N_GROUPS),
        name="dilated_merge_bwd", compiler_params=_params(("parallel",)),
    )(*o, *lse, do_cat)
    return out[:N_GROUPS], out[N_GROUPS:]


def dilated_backward(views, bias, o, lse, do, dlse):
    n_in = 9

    def body(*refs):
        ins, bias_ref = refs[:n_in * N_GROUPS], refs[n_in * N_GROUPS]
        outs, dbias_ref = refs[n_in * N_GROUPS + 1:-1], refs[-1]
        n = pl.program_id(0)

        @pl.when(n == 0)
        def _():
            dbias_ref[...] = jnp.zeros_like(dbias_ref)

        first = lax.broadcasted_iota(jnp.int32, (1, GROUP_W), 1) < HEAD_DIM
        for g, (_, d) in enumerate(DIL_PATTERNS):
            q_ref, kp_ref, ko_ref, vp_ref, vo_ref, o_ref, lse_ref, do_ref, dlse_ref = ins[n_in * g:n_in * (g + 1)]
            has_prev = _dil_tile(n, d)[1] > 0
            q2, k2, s = _dil_scores(q_ref, kp_ref, ko_ref, bias_ref[g], has_prev, first)
            dov = do_ref[...]
            do2 = _two_heads(dov.astype(BF16), first)
            prod = dov * o_ref[...]
            delta = jnp.concatenate([jnp.sum(jnp.where(first, prod, 0.0), axis=1, keepdims=True),
                                     jnp.sum(jnp.where(first, 0.0, prod), axis=1, keepdims=True)], axis=0)
            shift = _head_rows(dlse_ref[...], first) - delta
            p = jnp.exp(s - _head_rows(lse_ref[...], first))
            v2 = jnp.concatenate([vp_ref[...], vo_ref[...]], axis=0)
            ds = p * (_dot(do2, v2, NT) + shift)
            dbias_ref[g] += ds
            ds = ds.astype(BF16)
            dq2 = _dot(ds, k2, NN) * QK_SCALE
            dk2 = _dot(ds, q2, TN)
            dv2 = _dot(p.astype(BF16), do2, TN)
            base = 5 * g
            outs[base][...] = jnp.where(first, dq2[0:BLK], dq2[BLK:2 * BLK])
            outs[base + 1][...] = dk2[BLK:2 * BLK]
            outs[base + 2][...] = dk2[0:BLK]
            outs[base + 3][...] = dv2[BLK:2 * BLK]
            outs[base + 4][...] = dv2[0:BLK]

    operands, in_specs, out_specs, out_shape = [], [], [], []
    for g, (_, d) in enumerate(DIL_PATTERNS):
        own = _dil_spec(d, 0, 1)
        operands += [views[g]] * 5 + [_dil_view(t[g], d) for t in (o, lse, do, dlse)]
        in_specs += [_dil_spec(d, DIL_Q_BLOCK + g, LANE_BLOCKS), _dil_spec(d, DIL_K_BLOCK + g, LANE_BLOCKS, -1),
                     _dil_spec(d, DIL_K_BLOCK + g, LANE_BLOCKS), _dil_spec(d, DIL_V_BLOCK + g, LANE_BLOCKS, -1),
                     _dil_spec(d, DIL_V_BLOCK + g, LANE_BLOCKS)] + [own] * 4
        out_specs += [own] * 5
        out_shape += [jax.ShapeDtypeStruct((SEQ // d, d * GROUP_W), F32)] * 5
    tiles = pl.BlockSpec((N_GROUPS, 2 * BLK, 2 * BLK), lambda n: (0, 0, 0))
    out = pl.pallas_call(
        body, grid=(N_BLK,), in_specs=in_specs + [tiles], out_specs=out_specs + [tiles],
        out_shape=out_shape + [jax.ShapeDtypeStruct((N_GROUPS, 2 * BLK, 2 * BLK), F32)],
        name="dilated_backward", compiler_params=_params(("arbitrary",)),
    )(*operands, bias)
    return [out[5 * g:5 * g + 5] for g in range(N_GROUPS)], out[-1]


def dilated_key_grads(parts):
    def body(*refs):
        ins, outs = refs[:4 * N_GROUPS], refs[4 * N_GROUPS:]
        n = pl.program_id(0)
        for g, (_, d) in enumerate(DIL_PATTERNS):
            has_next = _dil_tile(n, d)[1] < N_BLK // d - 1
            own_k, next_k, own_v, next_v = ins[4 * g:4 * g + 4]
            outs[2 * g][...] = own_k[...] + jnp.where(has_next, next_k[...], 0.0)
            outs[2 * g + 1][...] = own_v[...] + jnp.where(has_next, next_v[...], 0.0)

    operands, in_specs, out_specs, out_shape = [], [], [], []
    for g, (_, d) in enumerate(DIL_PATTERNS):
        _, dk_own, dk_prev, dv_own, dv_prev = parts[g]
        operands += [dk_own, dk_prev, dv_own, dv_prev]
        in_specs += [_dil_spec(d, 0, 1), _dil_spec(d, 0, 1, 1)] * 2
        out_specs += [_dil_spec(d, 0, 1)] * 2
        out_shape += [jax.ShapeDtypeStruct((SEQ // d, d * GROUP_W), F32)] * 2
    out = pl.pallas_call(
        body, grid=(N_BLK,), in_specs=in_specs, out_specs=out_specs, out_shape=out_shape,
        name="dilated_key_grads", compiler_params=_params(("parallel",)),
    )(*operands)
    tok = lambda ts: jnp.concatenate([t.reshape(SEQ, GROUP_W) for t in ts], axis=1)
    return tok([parts[g][0] for g in range(N_GROUPS)]), tok(out[0::2]), tok(out[1::2])


def rel_bias_reduce(dbias0, dbias1, bucket):
    def body(d0_ref, d1_ref, b_ref, o_ref):
        dv, bv = d0_ref[...] + d1_ref[...], b_ref[...]
        lane = lax.broadcasted_iota(jnp.int32, (1, BLK), 1)
        acc = jnp.zeros((1, BLK), F32)
        for bkt in range(N_BUCKETS):
            acc = acc + jnp.where(lane == bkt, jnp.sum(jnp.where(bv == bkt, dv, 0.0)), 0.0)
        o_ref[...] = acc

    tile = pl.BlockSpec((None, BLK, 2 * BLK), lambda h: (h, 0, 0))
    return pl.pallas_call(
        body, grid=(dbias0.shape[0],), in_specs=[tile, tile, tile],
        out_specs=pl.BlockSpec((None, 1, BLK), lambda h: (h, 0, 0)),
        out_shape=jax.ShapeDtypeStruct((dbias0.shape[0], 1, BLK), F32),
        name="rel_bias_reduce", compiler_params=_params(("parallel",)),
    )(dbias0, dbias1, bucket)


def _heads(t):
    return t.reshape(SEQ, -1, HEAD_DIM).transpose(1, 0, 2)


def _unheads(t):
    return t.transpose(1, 0, 2).reshape(SEQ, -1)


def _t5_bucket(n):
    max_exact = N_BUCKETS // 2
    nf = jnp.maximum(n, 1).astype(F32)
    large = max_exact + (jnp.log(nf / max_exact) / math.log(MAX_REL_DIST / max_exact)
                         * (N_BUCKETS - max_exact)).astype(jnp.int32)
    large = jnp.minimum(large, N_BUCKETS - 1)
    return jnp.where(n < max_exact, n, large)


def band_tables(rel_bias):
    rel = jnp.arange(BLK)[:, None] + BLK - jnp.arange(2 * BLK)[None, :]
    patterns = [(d, w // d, H_PER_DIL) for w, d in DIL_PATTERNS] + [(1, SWA_WINDOW - 1, H_SWA_Q)]
    buckets = []
    for d, max_dist, heads in patterns:
        band = (rel >= 0) & (rel <= max_dist)
        tile = jnp.where(band, _t5_bucket(jnp.maximum(rel, 0) * d), -1).astype(jnp.int32)
        buckets.append(jnp.broadcast_to(tile, (heads,) + tile.shape))
    buckets = jnp.concatenate(buckets, axis=0)

    def body(table_ref, b_ref, o_ref):
        h = pl.program_id(0)
        bv = b_ref[...]
        tile = jnp.full(bv.shape, NEG, F32)
        for bkt in range(N_BUCKETS):
            tile = jnp.where(bv == bkt, table_ref[h, bkt], tile)
        o_ref[...] = tile

    spec = pl.BlockSpec((None, BLK, 2 * BLK), lambda h: (h, 0, 0))
    tiles = pl.pallas_call(
        body, grid=(buckets.shape[0],), in_specs=[pl.BlockSpec(memory_space=pltpu.SMEM), spec], out_specs=spec,
        out_shape=jax.ShapeDtypeStruct(buckets.shape, F32), name="band_tables", compiler_params=_params(("parallel",)),
    )(rel_bias.T, buckets)
    return tiles[:H_DIL], tiles[H_DIL:], buckets


def _swa_rows(t):
    t = t.reshape(N_BLK, BLK, H_SWA_KV, SWA_GROUP, HEAD_DIM).transpose(2, 0, 3, 1, 4)
    return t.reshape(H_SWA_KV, N_BLK, SWA_GROUP * BLK, HEAD_DIM)


def _swa_tokens(t):
    t = t.reshape(H_SWA_KV, N_BLK, SWA_GROUP, BLK, HEAD_DIM).transpose(1, 3, 0, 2, 4)
    return t.reshape(SEQ, H_SWA_Q * HEAD_DIM)


def _sink_rows(sinks):
    return jnp.broadcast_to(sinks.reshape(H_SWA_KV, SWA_GROUP, 1, 1), (H_SWA_KV, SWA_GROUP, BLK, 1)).reshape(
        H_SWA_KV, SWA_GROUP * BLK, 1)


def _vec(v):
    return v.reshape(1, D_MODEL)


class UnitRows:
    def __init__(self, u, mod_table, gain_table):
        self.shift, self.scale, self.gate = (Row(mod_table, 3 * u + t) for t in range(3))
        self.gain = Row(gain_table, u)


def ffn_forward(x, h, rows, then, w):
    a, b, s = ffn_up(h, w[0], w[1])
    f, xo, *h_next = ffn_down(s, w[2], x, rows.gate, then)
    return xo, (h_next or [None])[0], (x, h, a, b, s, f)


def ffn_backward(u, dxo, df, saved, rows, w, sums, below):
    x, h, a, b, s, _ = saved
    da, db = ffn_bwd_hidden(df, w[2], a, b)
    grads = ffn_grad_weights(h, s, df, da, db)
    dx, sums, *df_below = ffn_bwd_input(da, db, w[0], w[1], x, dxo, rows.gain, rows.scale, sums, u, below)
    return dx, sums, df_below, grads


def mixer_forward(x, h, rows, then, sinks, bias_dil, bias_swa, w):
    proj, qkv = in_proj(h, w[0])
    q_swa, k_swa, v_swa = _swa_rows(qkv[:, 1920:2304] * QK_SCALE), _heads(qkv[:, 2304:2432]), _heads(qkv[:, 2432:2560])
    o_sb, total_sb = sb_forward(qkv)
    bias_dil = bias_dil.reshape(N_GROUPS, 2 * BLK, 2 * BLK)
    views = dilated_views(qkv)
    o_groups, lse_groups = dilated_forward(views, bias_dil)
    o_dil = dilated_merge(o_groups, lse_groups)
    bias_swa = bias_swa.reshape(H_SWA_KV, SWA_GROUP * BLK, 2 * BLK)
    o_swa, lse_swa = banded_forward("swa_forward", q_swa, k_swa, v_swa, bias_swa, _sink_rows(sinks), SWA_HEADS_PER_STEP,
                                    _swa_prev_mask)
    o_cat = jnp.concatenate([o_sb, o_dil, _swa_tokens(o_swa)], axis=1)
    merged = merge_branches(o_cat, w[1], proj)
    mo, xo, *h_next = out_proj(merged, w[2], x, rows.gate, then)
    saved = (x, h, proj, (qkv, total_sb), (views, o_groups, lse_groups),
             (q_swa, k_swa, v_swa, o_swa, lse_swa), o_cat, merged, mo)
    return xo, (h_next or [None])[0], saved


def mixer_backward(u, dxo, dmo, saved, rows, sinks, bias_dil, bias_swa, w, sums, below):
    x, h, proj, sb, dil, swa, o_cat, merged, _ = saved
    tok = pl.BlockSpec((MM_TILE, D_MODEL), lambda j, k: (k, 0))
    g_out = grad_weight("grad_w_out", merged, pl.BlockSpec((MM_TILE, D_SHARD), lambda j, k: (k, j)), dmo, tok,
                        (D_SHARD, D_MODEL))
    du0, du1, du2, dg0, dg1, dg2 = merge_bwd(dmo, w[2], o_cat, w[1], proj)
    du = (du0, du1, du2)
    do_cat = branch_bwd_input(du, w[1])
    g_br = branch_grad_weights(o_cat, du)

    qkv, total_sb = sb
    dq_sb, dk_sb, dv_sb = sb_backward(qkv, total_sb, do_cat)

    views, o_groups, lse_groups = dil
    bias_dil = bias_dil.reshape(N_GROUPS, 2 * BLK, 2 * BLK)
    do_groups, dlse_groups = dilated_merge_bwd(o_groups, lse_groups, do_cat)
    parts, dbias_dil = dilated_backward(views, bias_dil, o_groups, lse_groups, do_groups, dlse_groups)
    dq_dil, dk_dil, dv_dil = dilated_key_grads(parts)
    dbias_dil = dbias_dil.reshape(H_DIL, BLK, 2 * BLK)

    q_swa, k_swa, v_swa, o_swa, lse_swa = swa
    bias_swa = bias_swa.reshape(H_SWA_KV, SWA_GROUP * BLK, 2 * BLK)
    dq_swa, dk_swa, dv_swa, dbias_swa, dsinks = banded_backward(
        "swa_backward", q_swa, k_swa, v_swa, bias_swa, _sink_rows(sinks), o_swa, lse_swa, _swa_rows(do_cat[:, 384:768]),
        jnp.zeros_like(lse_swa), SWA_HEADS_PER_STEP, _swa_prev_mask)
    dbias_swa = dbias_swa.reshape(H_SWA_Q, BLK, 2 * BLK)

    dproj = jnp.concatenate(
        [dq_sb, dk_sb, dv_sb, dq_dil, dk_dil, dv_dil, _swa_tokens(dq_swa), _unheads(dk_swa), _unheads(dv_swa)],
        axis=1).astype(BF16)
    dproj = jnp.concatenate([dproj, dg0, dg1, dg2], axis=1)
    g_in = grad_weight("grad_w_in", h, tok, dproj, pl.BlockSpec((MM_TILE, IN_SHARD), lambda j, k: (k, j)),
                       (D_MODEL, IN_SHARD))
    dx, sums, *df_below = mixer_bwd_input(dproj, w[0], x, dxo, rows.gain, rows.scale, sums, u, below)
    dbias = jnp.concatenate([dbias_dil, dbias_swa], axis=0)
    return dx, sums, df_below, dbias, dsinks[:, :, 0].reshape(H_SWA_Q), (g_in, g_br, g_out)


N_UNITS = 3 * DEPTH


def device_step(x, target, mod, gains, final_gain, sinks, rel_bias, get_weights, put_grads):
    bias_dil, bias_swa, bucket = band_tables(rel_bias)
    mod_table = mod.reshape(3 * N_UNITS, 1, D_MODEL)
    gain_table = gains.reshape(N_UNITS, 1, D_MODEL)
    saved, weights = [], []
    units = [UnitRows(u, mod_table, gain_table) for u in range(N_UNITS)]
    h = prenorm(x, units[0].gain, units[0].scale, units[0].shift)
    for u in range(N_UNITS):
        l, j = divmod(u, 3)
        w = get_weights(u, x)
        then = units[u + 1] if u + 1 < N_UNITS else None
        if j == 1:
            x, h, s = mixer_forward(x, h, units[u], then, sinks[l], bias_dil, bias_swa, w)
        else:
            x, h, s = ffn_forward(x, h, units[u], then, w)
        saved.append(s)
        weights.append(w)
    loss, dx, dfinal = final_loss(x, _vec(final_gain), target)

    sums = (lax.empty((8 * N_UNITS, D_MODEL), F32), lax.empty((8 * N_UNITS, D_MODEL), F32))
    dbias, dsinks = [None] * DEPTH, [None] * DEPTH
    zero = jnp.zeros((1, 1), F32)
    top = N_UNITS - 1
    df, gate_sums = resid_bwd(dx, saved[top][-1], units[top].gate, 0.5, sums[1], top)
    sums = (sums[0], gate_sums)
    for u in reversed(range(N_UNITS)):
        l, j = divmod(u, 3)
        rows = UnitRows(u, mod_table, gain_table + zero)
        below = (saved[u - 1][-1], units[u - 1].gate, 1.0 if (u - 1) % 3 == 1 else 0.5) if u > 0 else None
        if j == 1:
            dx, sums, df, dbias[l], dsinks[l], grads = mixer_backward(
                u, dx, df, saved[u], rows, sinks[l], bias_dil, bias_swa, weights[u], sums, below)
        else:
            dx, sums, df, grads = ffn_backward(u, dx, df, saved[u], rows, weights[u], sums, below)
        df = df[0] if df else None
        if u > 0:
            zero = put_grads(u, grads)
    drel = rel_bias_reduce(dbias[0], dbias[1], bucket)[:, 0, :N_BUCKETS].T
    norm_sums, gate_sums = (t.reshape(DEPTH, 3, 8, D_MODEL) for t in sums)
    dmod = jnp.stack([norm_sums[:, :, 0], norm_sums[:, :, 1], gate_sums[:, :, 0]], axis=2)
    return loss, dx, dmod, norm_sums[:, :, 2], dfinal[0], jnp.stack(dsinks), drel, grads


MESH = pl.DeviceIdType.MESH
CHIP_FLIPS = ((1, 0), (0, 1), (1, 1))
ANY = pl.BlockSpec(memory_space=pl.ANY)


def _position():
    return lax.axis_index("x"), lax.axis_index("y"), lax.axis_index("c")


def all_gather_small(name, piece):
    def body(x_ref, out_ref, send_sems, recv_sems, local_sem):
        x, y, c = _position()
        me, sibling = (x, y, c), (x, y, 1 - c)
        chips = [(x ^ fx, y ^ fy) for fx, fy in CHIP_FLIPS]

        def rows(px, py, pc):
            return out_ref.at[4 * px + 2 * py + pc]

        def copy(k, block, to, src=None):
            return pltpu.make_async_remote_copy(
                src_ref=rows(*block) if src is None else src, dst_ref=rows(*block),
                send_sem=send_sems.at[k], recv_sem=recv_sems.at[k], device_id=to, device_id_type=MESH)

        mine = pltpu.make_async_copy(x_ref, rows(*me), local_sem)
        mine.start()
        first = [copy(0, me, sibling, src=x_ref)]
        first += [copy(1 + j, me, (*chip, c), src=x_ref) for j, chip in enumerate(chips)]
        for cp in first:
            cp.start()
        passed = [copy(4 + j, (*chip, c), sibling) for j, chip in enumerate(chips)]
        for j, chip in enumerate(chips):
            copy(1 + j, (*chip, c), me).wait_recv()
            passed[j].start()
        copy(0, sibling, me).wait_recv()
        for j, chip in enumerate(chips):
            copy(4 + j, (*chip, 1 - c), me).wait_recv()
        for cp in first + passed:
            cp.wait_send()
        mine.wait()

    return pl.pallas_call(
        body, out_shape=jax.ShapeDtypeStruct((N_DEV,) + piece.shape, piece.dtype),
        in_specs=[pl.BlockSpec(memory_space=pltpu.VMEM)], out_specs=pl.BlockSpec(memory_space=pltpu.VMEM),
        scratch_shapes=[pltpu.SemaphoreType.DMA((7,)), pltpu.SemaphoreType.DMA((7,)), pltpu.SemaphoreType.DMA],
        name=name,
    )(piece)


def exchange(name, operands, out_shapes, aliases, plan):
    n_in, n_out = len(operands), len(out_shapes)

    def body(*refs):
        ins, outs = refs[:n_in], refs[n_in:n_in + n_out]
        send_sems, recv_sems, local_sems = refs[n_in + n_out:]
        x, y, c = _position()
        local, sends, recvs = plan(ins, outs, x, y, c)
        local = [pltpu.make_async_copy(s, d, local_sems.at[k]) for k, (s, d) in enumerate(local)]
        for cp in local:
            cp.start()
        remote = [pltpu.make_async_remote_copy(src_ref=s, dst_ref=d, send_sem=send_sems.at[k], recv_sem=recv_sems.at[k],
                                               device_id=dev, device_id_type=MESH)
                  for k, (s, d, dev) in enumerate(sends)]
        for cp in remote:
            cp.start()
        for k, r in enumerate(recvs):
            pltpu.make_async_remote_copy(src_ref=r, dst_ref=r, send_sem=send_sems.at[k], recv_sem=recv_sems.at[k],
                                         device_id=(x, y, c), device_id_type=MESH).wait_recv()
        for cp in remote:
            cp.wait_send()
        for cp in local:
            cp.wait()

    n_sends, n_local = plan.n_sends, max(plan.n_local, 1)
    return pl.pallas_call(
        body, out_shape=out_shapes, in_specs=[ANY] * n_in, out_specs=[ANY] * n_out,
        scratch_shapes=[pltpu.SemaphoreType.DMA((n_sends,)), pltpu.SemaphoreType.DMA((n_sends,)),
                        pltpu.SemaphoreType.DMA((n_local,))],
        input_output_aliases=aliases, name=name,
    )(*operands)


def _plan(n_local, n_sends):
    def wrap(fn):
        fn.n_local, fn.n_sends = n_local, n_sends
        return fn
    return wrap


def _half(ref, axis, c):
    rows = ref.shape[axis] // 2
    idx = [slice(None)] * len(ref.shape)
    idx[axis] = pl.ds(pl.multiple_of(c * rows, 16), rows)
    return ref.at[tuple(idx)]


HBM = pl.BlockSpec(memory_space=pltpu.HBM)
SEM = pl.BlockSpec(memory_space=pltpu.SEMAPHORE)
EFFECT = pltpu.SideEffectType.DATAFLOW_SIDE_EFFECTING


def split_start(name, bufs, extra, n_copies, describe):
    n = len(bufs)

    def body(*refs):
        send_sems, recv_sems = refs[n + len(extra)], refs[n + len(extra) + 1]
        x, y, c = _position()
        for k, (src, dst, _, peer) in enumerate(describe(refs[:n], x, y, c)):
            pltpu.make_async_remote_copy(src_ref=src, dst_ref=dst, send_sem=send_sems.at[k], recv_sem=recv_sems.at[k],
                                         device_id=peer, device_id_type=MESH).start()
        token = refs[-1]
        token[...] = jnp.zeros_like(token)

    out = pl.pallas_call(
        body, name=name,
        out_shape=(pltpu.SemaphoreType.DMA((n_copies,)), pltpu.SemaphoreType.DMA((n_copies,)),
                   *[pltpu.HBM(b.shape, b.dtype) for b in bufs], jax.ShapeDtypeStruct((8, 128), F32)),
        in_specs=[HBM] * n + [ANY] * len(extra),
        out_specs=(SEM, SEM, *[HBM] * n, pl.BlockSpec(memory_space=pltpu.VMEM)),
        input_output_aliases={k: 2 + k for k in range(n)},
        compiler_params=pltpu.CompilerParams(has_side_effects=EFFECT),
    )(*[pltpu.with_memory_space_constraint(b, pltpu.HBM) for b in bufs], *extra)
    return out[0], out[1], list(out[2:2 + n]), out[-1]


def split_wait(name, bufs, send_sems, recv_sems, after, describe):
    n = len(bufs)

    def body(*refs):
        send, recv = refs[n], refs[n + 1]
        x, y, c = _position()
        for k, (src, _, dst, peer) in enumerate(describe(refs[:n], x, y, c)):
            copy = pltpu.make_async_remote_copy(src_ref=src, dst_ref=dst, send_sem=send.at[k], recv_sem=recv.at[k],
                                                device_id=peer, device_id_type=MESH)
            copy.wait_send()
            copy.wait_recv()

    out = pl.pallas_call(
        body, name=name, out_shape=[pltpu.HBM(b.shape, b.dtype) for b in bufs],
        in_specs=[HBM] * n + [SEM, SEM] + [ANY] * len(after), out_specs=[HBM] * n,
        input_output_aliases={k: k for k in range(n)},
        compiler_params=pltpu.CompilerParams(has_side_effects=EFFECT),
    )(*bufs, send_sems, recv_sems, *after)
    return list(out)


def _row_tile(rows, cols, max_elements=256 * 1024):
    best = 16
    for t in range(16, rows + 1, 16):
        if rows % t == 0 and t * cols <= max_elements:
            best = t
    return best


def cast_into_slots(name, shards, chip):
    n = len(shards)
    rows, cols = shards[0][0].shape[-2:]
    tr = _row_tile(rows, cols)

    def body(chip_ref, *refs):
        del chip_ref
        for k in range(n):
            refs[n + k][...] = refs[k][...].astype(BF16)

    def in_spec(param, index):
        return pl.BlockSpec((None,) * len(index) + (tr, cols), lambda r, chip_ref: index + (r, 0))

    return pl.pallas_call(
        body, out_shape=[jax.ShapeDtypeStruct((N_CHIPS, rows, cols), BF16)] * n,
        grid_spec=pltpu.PrefetchScalarGridSpec(
            num_scalar_prefetch=1, grid=(rows // tr,),
            in_specs=[in_spec(p, idx) for p, idx in shards],
            out_specs=[pl.BlockSpec((None, tr, cols), lambda r, chip_ref: (chip_ref[0], r, 0))] * n),
        name=name, compiler_params=_params(("parallel",)),
    )(chip, *[p for p, _ in shards])


GATHER_STAGES = ((0,), (1,), (2,), (3, 4, 5))
REDUCE_STAGES = ((5, 4, 3), (2,), (1,), (0,))


def _gather_copies(slots, x, y, c):
    me = 2 * x + y
    out = []
    for s in slots:
        for fx, fy in CHIP_FLIPS:
            mine = _half(s.at[me], 0, c)
            out.append((mine, mine, _half(s.at[2 * (x ^ fx) + (y ^ fy)], 0, c), (x ^ fx, y ^ fy, c)))
    return out


class WeightStream:
    def __init__(self, shards, chip, after=()):
        self.pending, self.ready = {}, {}
        token = tuple(after)
        for si, units in enumerate(GATHER_STAGES):
            slots = []
            for u in units:
                same = len({p.shape[-2:] for p, _ in shards[u]}) == 1
                for t, group in enumerate([shards[u]] if same else [[s] for s in shards[u]]):
                    slots += cast_into_slots(f"cast_{u}_{t}", group, chip)
            send, recv, slots, tok = split_start(f"gather_start_{si}", slots, token, 3 * len(slots), _gather_copies)
            self.pending[si] = (send, recv, slots)
            token = (tok,)
        self.token = token

    def get(self, u, after):
        if u not in self.ready:
            si = next(k for k, units in enumerate(GATHER_STAGES) if u in units)
            send, recv, slots = self.pending.pop(si)
            slots = split_wait(f"gather_wait_{si}", slots, send, recv, (after,) + self.token, _gather_copies)
            self.token = ()

            @_plan(0, 3 * len(slots))
            def to_sibling(ins, outs, x, y, c):
                sends, recvs = [], []
                for o in outs:
                    for fx, fy in CHIP_FLIPS:
                        slab = o.at[2 * (x ^ fx) + (y ^ fy)]
                        sends.append((_half(slab, 0, c), _half(slab, 0, c), (x, y, 1 - c)))
                        recvs.append(_half(slab, 0, 1 - c))
                return [], sends, recvs

            shapes = [jax.ShapeDtypeStruct(s.shape, BF16) for s in slots]
            slots = exchange(f"gather_sibling_{si}", slots, shapes, {k: k for k in range(len(slots))}, to_sibling)
            for i, v in enumerate(GATHER_STAGES[si]):
                self.ready[v] = tuple(slots[3 * i:3 * i + 3])
        return self.ready[u]


def _reduce_copies(bufs, x, y, c):
    n = len(bufs) // 2
    out = []
    for s, land in zip(bufs[:n], bufs[n:]):
        for k, (fx, fy) in enumerate(CHIP_FLIPS):
            out.append((s.at[2 * (x ^ fx) + (y ^ fy)], land.at[k], land.at[k], (x ^ fx, y ^ fy, c)))
    return out


GRAD_SLOTS = {"gate": (2 * DEPTH, FF_SHARD, D_MODEL), "up": (2 * DEPTH, FF_SHARD, D_MODEL),
              "down": (2 * DEPTH, FF_SHARD, D_MODEL), "in": (DEPTH, D_MODEL, IN_SHARD),
              "br": (DEPTH, BR_ROWS, D_SHARD), "out": (DEPTH, D_SHARD, D_MODEL)}


def _unit_tensors(u):
    l, j = divmod(u, 3)
    if j == 1:
        return [("in", l), ("br", l), ("out", l)]
    return [(k, 2 * l + j // 2) for k in ("gate", "up", "down")]


class GradStream:
    def __init__(self, chip, core):
        self.core = core
        self.place = jnp.concatenate([chip, core])
        self.held, self.flying = {}, []
        self.full = {k: lax.empty(shape, F32) for k, shape in GRAD_SLOTS.items()}

    def put(self, u, grads, after=()):
        self.held[u] = grads
        si = len(self.flying)
        units = REDUCE_STAGES[si]
        if not all(v in self.held for v in units):
            return jnp.zeros((1, 1), F32)
        gs = [g for v in units for g in self.held[v]]

        @_plan(0, len(gs))
        def swap_halves(ins, outs, x, y, c):
            sends = [(_half(g, 1, 1 - c), o, (x, y, 1 - c)) for g, o in zip(ins, outs)]
            return [], sends, list(outs)

        half_shapes = [jax.ShapeDtypeStruct((N_CHIPS, g.shape[1] // 2, g.shape[2]), BF16) for g in gs]
        landed = exchange(f"reduce_swap_{si}", gs + list(after), half_shapes, {}, swap_halves)
        sums = [None] * len(gs)
        for run in _same_shape_runs(gs):
            for k, s in zip(run, _add_halves([gs[k] for k in run], [landed[k] for k in run], self.core)):
                sums[k] = s
        landing = [lax.empty((3,) + s.shape[1:], BF16) for s in sums]
        send, recv, bufs, token = split_start(f"reduce_start_{si}", sums + landing, (), 3 * len(sums), _reduce_copies)
        self.flying.append((send, recv, bufs, [t for v in units for t in _unit_tensors(v)]))
        return token[0:1, 0:1]

    def finish(self, after):
        for si, (send, recv, bufs, tensors) in enumerate(self.flying):
            bufs = split_wait(f"reduce_wait_{si}", bufs, send, recv, tuple(after), _reduce_copies)
            n = len(tensors)
            for run in _same_shape_runs(bufs[:n]):
                names = [tensors[k][0] for k in run]
                out = _add_chips([bufs[k] for k in run], [bufs[n + k] for k in run], self.place,
                                 [self.full[t] for t in names], [tensors[k][1] for k in run])
                self.full.update(zip(names, out))
        names = list(self.full)

        @_plan(0, len(names))
        def share_halves(ins, outs, x, y, c):
            sends = [(_half(o, 1, c), _half(o, 1, c), (x, y, 1 - c)) for o in outs]
            return [], sends, [_half(o, 1, 1 - c) for o in outs]

        shapes = [jax.ShapeDtypeStruct(self.full[k].shape, F32) for k in names]
        out = exchange("reduce_share_halves", [self.full[k] for k in names], shapes, {k: k for k in range(len(names))},
                       share_halves)
        return dict(zip(names, out))


def _same_shape_runs(arrays, longest=3):
    runs = []
    for k, a in enumerate(arrays):
        if runs and len(runs[-1]) < longest and arrays[runs[-1][0]].shape == a.shape:
            runs[-1].append(k)
        else:
            runs.append([k])
    return runs


def _add_halves(gs, landeds, core):
    n = len(gs)
    _, rh, cols = landeds[0].shape
    tr = _row_tile(rh, cols, 1024 * 1024)
    per_half = rh // tr

    def body(core_ref, *refs):
        del core_ref
        for k in range(n):
            refs[2 * n + k][...] = (refs[k][...].astype(F32) + refs[n + k][...].astype(F32)).astype(BF16)

    blk = (None, tr, cols)
    landed_spec = pl.BlockSpec(blk, lambda j, r, core_ref: (j, r, 0))
    return pl.pallas_call(
        body, out_shape=[jax.ShapeDtypeStruct(landeds[0].shape, BF16)] * n,
        grid_spec=pltpu.PrefetchScalarGridSpec(
            num_scalar_prefetch=1, grid=(N_CHIPS, per_half),
            in_specs=[pl.BlockSpec(blk, lambda j, r, core_ref: (j, core_ref[0] * per_half + r, 0))] * n
            + [landed_spec] * n,
            out_specs=[landed_spec] * n),
        name="reduce_add_halves", compiler_params=_params(("parallel", "parallel")),
    )(core, *gs, *landeds)


def _add_chips(sums, landeds, place, fulls, slots):
    n = len(sums)
    _, rh, cols = sums[0].shape
    tr = _row_tile(rh, cols, 1024 * 1024)
    per_half = rh // tr

    def body(place_ref, *refs):
        del place_ref
        for k in range(n):
            s_ref, la_ref, o_ref = refs[k], refs[n + k], refs[3 * n + k]
            o_ref[...] = ((s_ref[...].astype(F32) + la_ref[0].astype(F32)) + la_ref[1].astype(F32)) + la_ref[2].astype(F32)

    def out_spec(slot):
        return pl.BlockSpec((None, tr, cols), lambda r, place_ref: (slot, place_ref[1] * per_half + r, 0))

    return pl.pallas_call(
        body, out_shape=[jax.ShapeDtypeStruct(f.shape, F32) for f in fulls],
        grid_spec=pltpu.PrefetchScalarGridSpec(
            num_scalar_prefetch=1, grid=(per_half,),
            in_specs=[pl.BlockSpec((None, tr, cols), lambda r, place_ref: (place_ref[0], r, 0))] * n
            + [pl.BlockSpec((3, tr, cols), lambda r, place_ref: (0, r, 0))] * n + [ANY] * n,
            out_specs=[out_spec(slot) for slot in slots]),
        input_output_aliases={1 + 2 * n + k: k for k in range(n)}, name="reduce_add_chips",
        compiler_params=_params(("parallel",)),
    )(place, *sums, *landeds, *fulls)


def sum_devices(parts):
    def body(p_ref, o_ref):
        acc = p_ref[0]
        for d in range(1, N_DEV):
            acc = acc + p_ref[d]
        o_ref[...] = acc

    return pl.pallas_call(body, out_shape=jax.ShapeDtypeStruct(parts.shape[1:], F32), name="sum_devices")(parts)


ADA_SHARD = 9 * D_MODEL // N_CHIPS
ADA_TILE = 768
ADA_ROWS = 16


def ada_forward(c_rows, w_ada, b_shard):
    def body(c_ref, w_ref, b_ref, o_ref):
        cv = c_ref[...]
        o_ref[...] = _dot((cv * _sigmoid(cv)).astype(BF16), w_ref[...].astype(BF16), NN) + b_ref[...]

    return pl.pallas_call(
        body, grid=(DEPTH, ADA_SHARD // ADA_TILE),
        in_specs=[pl.BlockSpec((ADA_ROWS, D_MODEL), lambda l, n: (0, 0)),
                  pl.BlockSpec((None, D_MODEL, ADA_TILE), lambda l, n: (l, 0, n)),
                  pl.BlockSpec((None, 1, ADA_TILE), lambda l, n: (l, 0, n))],
        out_specs=pl.BlockSpec((None, ADA_ROWS, ADA_TILE), lambda l, n: (l, 0, n)),
        out_shape=jax.ShapeDtypeStruct((DEPTH, ADA_ROWS, ADA_SHARD), F32),
        name="ada_forward", compiler_params=_params(("parallel", "parallel")),
    )(c_rows, w_ada, b_shard)


def ada_backward(c_rows, dmod_rows):
    def body(c_ref, d_ref, o_ref):
        cv = c_ref[...]
        o_ref[...] = _dot((cv * _sigmoid(cv)).astype(BF16), d_ref[...].astype(BF16), TN)

    return pl.pallas_call(
        body, grid=(DEPTH, ADA_SHARD // ADA_TILE),
        in_specs=[pl.BlockSpec((ADA_ROWS, D_MODEL), lambda l, n: (0, 0)),
                  pl.BlockSpec((None, ADA_ROWS, ADA_TILE), lambda l, n: (l, 0, n))],
        out_specs=pl.BlockSpec((None, D_MODEL, ADA_TILE), lambda l, n: (l, 0, n)),
        out_shape=jax.ShapeDtypeStruct((DEPTH, D_MODEL, ADA_SHARD), F32),
        name="ada_backward", compiler_params=_params(("parallel", "parallel")),
    )(c_rows, dmod_rows)


def adamw(name, w, g, m, v):
    shape = w.shape
    cols = shape[-1]
    rows = w.size // cols
    tr = _row_tile(rows, cols) if rows % 16 == 0 else rows
    c1 = 1.0 / (1.0 - ADAM_B1 ** ADAM_STEP)
    c2 = 1.0 / (1.0 - ADAM_B2 ** ADAM_STEP)

    def body(w_ref, g_ref, m_ref, v_ref, go_ref, d_ref, mo_ref, vo_ref):
        gv = g_ref[...]
        mn = ADAM_B1 * m_ref[...] + (1.0 - ADAM_B1) * gv
        vn = ADAM_B2 * v_ref[...] + (1.0 - ADAM_B2) * (gv * gv)
        go_ref[...] = gv
        mo_ref[...] = mn
        vo_ref[...] = vn
        d_ref[...] = -ADAM_LR * ((mn * c1) / (jnp.sqrt(vn * c2) + ADAM_EPS) + ADAM_WD * w_ref[...])

    spec = pl.BlockSpec((tr, cols), lambda i: (i, 0))
    out = jax.ShapeDtypeStruct((rows, cols), F32)
    res = pl.pallas_call(
        body, grid=(rows // tr,), in_specs=[spec] * 4, out_specs=[spec] * 4, out_shape=[out] * 4,
        name=name, compiler_params=_params(("parallel",)),
    )(*[t.reshape(rows, cols) for t in (w, g, m, v)])
    return tuple(r.reshape(shape) for r in res)


def _pack(parts, rows):
    flat = jnp.concatenate([p.reshape(-1) for p in parts])
    return jnp.pad(flat, (0, rows * 128 - flat.size)).reshape(rows, 128)


def _unpack(flat, shapes):
    out, at = [], 0
    for s in shapes:
        n = math.prod(s)
        out.append(flat[at:at + n].reshape(s))
        at += n
    return out


def kernel(x, c, w_ada, b_ada, norm_gain, w_ffn_gate, w_ffn_up, w_ffn_down, w_in, w_br_sb, w_br_dil, w_br_swa, w_out, sinks, rel_bias, final_gain, loss_target, m_w_ada, m_b_ada, m_norm_gain, m_w_ffn_gate, m_w_ffn_up, m_w_ffn_down, m_w_in, m_w_br_sb, m_w_br_dil, m_w_br_swa, m_w_out, m_sinks, m_rel_bias, m_final_gain, v_w_ada, v_b_ada, v_norm_gain, v_w_ffn_gate, v_w_ffn_up, v_w_ffn_down, v_w_in, v_w_br_sb, v_w_br_dil, v_w_br_swa, v_w_out, v_sinks, v_rel_bias, v_final_gain):
    xi, yi, ci = _position()
    chip = 2 * xi + yi
    dev = 2 * chip + ci

    c_all = all_gather_small("gather_c", c.reshape(8, 128)).reshape(N_DEV, D_MODEL)
    c_rows = jnp.pad(c_all, ((0, ADA_ROWS - N_DEV), (0, 0)))
    b_shard = lax.dynamic_slice_in_dim(b_ada, chip * ADA_SHARD, ADA_SHARD, axis=1).reshape(DEPTH, 1, ADA_SHARD)
    mod_shard = ada_forward(c_rows, w_ada, b_shard)[:, :N_DEV]
    n_mod = DEPTH * N_DEV * ADA_SHARD
    gathered = all_gather_small("gather_mod", _pack([mod_shard, norm_gain], 304))[::2].reshape(N_CHIPS, -1)
    mod_all = gathered[:, :n_mod].reshape(N_CHIPS, DEPTH, N_DEV, ADA_SHARD)
    mod = lax.dynamic_index_in_dim(mod_all, dev, axis=2, keepdims=False)
    mod = mod.transpose(1, 0, 2).reshape(DEPTH, 3, 3, D_MODEL)
    gains = gathered[:, n_mod:n_mod + DEPTH * 3 * D_SHARD].reshape(N_CHIPS, DEPTH, 3, D_SHARD)
    gains = gains.transpose(1, 2, 0, 3).reshape(DEPTH, 3, D_MODEL)

    chip_i, core_i = chip.astype(jnp.int32).reshape(1), ci.astype(jnp.int32).reshape(1)
    w_br = jnp.concatenate([w_br_sb, w_br_dil, w_br_swa], axis=1)
    transposed = (3, 4)
    w_gate_t, w_up_t = jnp.swapaxes(w_ffn_gate, 2, 3), jnp.swapaxes(w_ffn_up, 2, 3)
    shards = []
    for l in range(DEPTH):
        ffn = [[(w_gate_t, (l, f)), (w_up_t, (l, f)), (w_ffn_down, (l, f))] for f in range(2)]
        shards += [ffn[0], [(w_in, (l,)), (w_br, (l,)), (w_out, (l,))], ffn[1]]
    weights_in = WeightStream(shards, chip_i, (gathered,))
    grads_out = GradStream(chip_i, core_i)

    loss, dx, dmod, dgains, dfinal, dsinks, drel, last_grads = device_step(
        x[0], loss_target[0], mod, gains, final_gain, sinks, rel_bias, weights_in.get, grads_out.put)

    small_shapes = [(DEPTH, 9 * D_MODEL), (DEPTH, 3, D_MODEL), (D_MODEL,), (DEPTH, H_SWA_Q), (N_BUCKETS, 12), (1,)]
    small_all = all_gather_small("gather_small_grads", _pack([dmod, dgains, dfinal, dsinks, drel, loss[0, 0:1]], 208))
    started = grads_out.put(0, last_grads, after=(small_all,))
    small_all = small_all + started
    g_b_ada, g_gain_full, g_final, g_sinks, g_rel, loss_sum = _unpack(sum_devices(small_all).reshape(-1), small_shapes)
    g_gain = lax.dynamic_slice_in_dim(g_gain_full, chip * D_SHARD, D_SHARD, axis=2)
    dmod_all = small_all.reshape(N_DEV, -1)[:, :DEPTH * 9 * D_MODEL].reshape(N_DEV, DEPTH, 9 * D_MODEL)
    dmod_rows = lax.dynamic_slice_in_dim(dmod_all, chip * ADA_SHARD, ADA_SHARD, axis=2).transpose(1, 0, 2)
    g_w_ada = ada_backward(c_rows, jnp.pad(dmod_rows, ((0, 0), (0, ADA_ROWS - N_DEV), (0, 0))))

    weights = [w_ada, b_ada, norm_gain, w_ffn_gate, w_ffn_up, w_ffn_down, w_in, w_br_sb, w_br_dil, w_br_swa, w_out,
               sinks, rel_bias, final_gain]
    ms = [m_w_ada, m_b_ada, m_norm_gain, m_w_ffn_gate, m_w_ffn_up, m_w_ffn_down, m_w_in, m_w_br_sb, m_w_br_dil,
          m_w_br_swa, m_w_out, m_sinks, m_rel_bias, m_final_gain]
    vs = [v_w_ada, v_b_ada, v_norm_gain, v_w_ffn_gate, v_w_ffn_up, v_w_ffn_down, v_w_in, v_w_br_sb, v_w_br_dil,
          v_w_br_swa, v_w_out, v_sinks, v_rel_bias, v_final_gain]
    grads = [g_w_ada, g_b_ada, g_gain] + [None] * 8 + [g_sinks, g_rel, g_final]

    deltas, new_ms, new_vs = [None] * 14, [None] * 14, [None] * 14
    for k in (0, 1, 2, 11, 12, 13):
        _, deltas[k], new_ms[k], new_vs[k] = adamw(f"adamw_{k}", weights[k], grads[k], ms[k], vs[k])

    g = grads_out.finish((dx, deltas[0], deltas[1]))
    g_br = g["br"]
    grads[3:11] = [g["gate"].reshape(w_gate_t.shape), g["up"].reshape(w_up_t.shape),
                   g["down"].reshape(w_ffn_down.shape), g["in"], g_br[:, 0:256], g_br[:, 256:384], g_br[:, 384:768],
                   g["out"]]
    for k in range(3, 11):
        state = [weights[k], ms[k], vs[k]]
        if k in transposed:
            state = [jnp.swapaxes(t, 2, 3) for t in state]
        out = adamw(f"adamw_{k}", state[0], grads[k], state[1], state[2])
        if k in transposed:
            out = [jnp.swapaxes(t, 2, 3) for t in out]
        grads[k], deltas[k], new_ms[k], new_vs[k] = out
    return (loss_sum[0], dx[None], *grads, *deltas, *new_ms, *new_vs)
```

```python
import functools
import math

import jax
import jax.numpy as jnp
from jax import lax
from jax.experimental import pallas as pl
from jax.experimental.pallas import tpu as pltpu

F32 = jnp.float32
BF16 = jnp.bfloat16

D_MODEL = 1024
SEQ = 2048
DEPTH = 2
HEAD_DIM = 64
BLK = 128
H_SB = 4
DIL_PATTERNS = ((128, 1), (512, 4), (2048, 16))
H_PER_DIL = 2
H_DIL = 6
H_SWA_Q = 6
H_SWA_KV = 2
SWA_WINDOW = 128
N_BUCKETS = 32
MAX_REL_DIST = 2048
D_FF = 2816
RMS_EPS = 1e-6
N_CHIPS = 4
N_DEV = 8
FF_SHARD = D_FF // N_CHIPS
D_QKV = 2560
D_IN = D_QKV + 3 * D_MODEL
IN_SHARD = D_IN // N_CHIPS
D_SHARD = D_MODEL // N_CHIPS
BR_ROWS = 768
NEG = -1e30
QK_SCALE = HEAD_DIM ** -0.5

ADAM_LR = 0.001
ADAM_B1 = 0.9
ADAM_B2 = 0.999
ADAM_EPS = 1e-08
ADAM_WD = 0.01
ADAM_STEP = 10

VMEM_LIMIT = 56 * 1024 * 1024
ROW_TILE = 256
MM_TILE = 1024

NN = (((1,), (0,)), ((), ()))
NT = (((1,), (1,)), ((), ()))
TN = (((0,), (0,)), ((), ()))


def _params(sem=None):
    return pltpu.CompilerParams(dimension_semantics=sem, vmem_limit_bytes=VMEM_LIMIT)


def _dot(a, b, dims):
    return lax.dot_general(a, b, dims, preferred_element_type=F32)


def _sigmoid(x):
    return 1.0 / (1.0 + jnp.exp(-x))


def _matmul(name, grid, nk, k_axis, dims, n_pairs, in_specs, out_specs, out_shape, acc_shape, epilogue,
            operands, sem, aliases=None, prologue=None):
    n_in = len(in_specs)
    n_out = len(out_specs)

    def partial(ins):
        tot = None
        for p in range(n_pairs):
            a = ins[2 * p][...]
            if prologue is not None:
                a = prologue(p, a, ins)
            d = _dot(a, ins[2 * p + 1][...], dims)
            tot = d if tot is None else tot + d
        return tot

    def body(*refs):
        ins, outs = refs[:n_in], refs[n_in:n_in + n_out]
        ids = tuple(pl.program_id(a) for a in range(len(grid)))
        if nk == 1:
            epilogue(partial(ins), ins, outs, ids)
            return
        acc = refs[n_in + n_out]
        k = ids[k_axis]

        @pl.when(k == 0)
        def _():
            acc[...] = partial(ins)

        @pl.when(k > 0)
        def _():
            acc[...] += partial(ins)

        @pl.when(k == nk - 1)
        def _():
            epilogue(acc[...], ins, outs, ids)

    return pl.pallas_call(
        body, grid=grid, in_specs=in_specs, out_specs=out_specs, out_shape=out_shape,
        scratch_shapes=[] if nk == 1 else [pltpu.VMEM(acc_shape, F32)],
        input_output_aliases=aliases or {}, name=name, compiler_params=_params(sem),
    )(*operands)


def _row_spec(width=D_MODEL):
    return pl.BlockSpec((ROW_TILE, width), lambda i: (i, 0))


def _vec_spec(rows=1, width=D_MODEL):
    return pl.BlockSpec((rows, width), lambda i: (0, 0))


class Row:
    def __init__(self, table, index):
        self.table, self.index = table, index

    def spec(self):
        index = self.index
        return pl.BlockSpec((None, 1, D_MODEL), lambda *ids: (index, 0, 0))


def _slot_spec(u):
    return pl.BlockSpec((8, D_MODEL), lambda *ids: (u, 0))


def prenorm(x, gain, scale, shift):
    def body(x_ref, g_ref, sc_ref, sh_ref, h_ref):
        xv = x_ref[...]
        r = lax.rsqrt(jnp.mean(xv * xv, axis=-1, keepdims=True) + RMS_EPS)
        h_ref[...] = (((xv * r) * g_ref[...]) * (1.0 + sc_ref[...]) + sh_ref[...]).astype(BF16)

    return pl.pallas_call(
        body, grid=(SEQ // ROW_TILE,), in_specs=[_row_spec(), gain.spec(), scale.spec(), shift.spec()],
        out_specs=_row_spec(), out_shape=jax.ShapeDtypeStruct((SEQ, D_MODEL), BF16),
        name="prenorm", compiler_params=_params(("parallel",)),
    )(x, gain.table, scale.table, shift.table)


def resid_bwd(dxo, f, coef, mult, sums, u):
    def body(dx_ref, f_ref, c_ref, sums_in, df_ref, dc_ref):
        del sums_in
        dx = dx_ref[...]
        df_ref[...] = (dx * (mult * c_ref[...])).astype(BF16)
        part = mult * jnp.sum(dx * f_ref[...], axis=0, keepdims=True)

        @pl.when(pl.program_id(0) == 0)
        def _():
            dc_ref[...] = jnp.zeros_like(dc_ref)

        dc_ref[0:1, :] += part

    return pl.pallas_call(
        body, grid=(SEQ // ROW_TILE,),
        in_specs=[_row_spec(), _row_spec(), coef.spec(), pl.BlockSpec(memory_space=pl.ANY)],
        out_specs=[_row_spec(), _slot_spec(u)],
        out_shape=[jax.ShapeDtypeStruct((SEQ, D_MODEL), BF16), jax.ShapeDtypeStruct(sums.shape, F32)],
        input_output_aliases={3: 1}, name="resid_bwd", compiler_params=_params(("arbitrary",)),
    )(dxo, f, coef.table, sums)


def final_loss(x, gain, target):
    def body(x_ref, g_ref, t_ref, loss_ref, dx_ref, dg_ref):
        xv = x_ref[...]
        g = g_ref[...]
        r = lax.rsqrt(jnp.mean(xv * xv, axis=-1, keepdims=True) + RMS_EPS)
        xh = xv * r
        e = xh * g - t_ref[...]
        part = 0.5 * jnp.sum(jnp.mean(e * e, axis=-1, keepdims=True), axis=0, keepdims=True)
        dy = e * (1.0 / D_MODEL)
        dyg = dy * g
        dx_ref[...] = r * (dyg - xh * jnp.mean(dyg * xh, axis=-1, keepdims=True))

        @pl.when(pl.program_id(0) == 0)
        def _():
            loss_ref[...] = jnp.zeros_like(loss_ref)
            dg_ref[...] = jnp.zeros_like(dg_ref)

        loss_ref[...] += jnp.broadcast_to(part, loss_ref.shape)
        dg_ref[0:1, :] += jnp.sum(dy * xh, axis=0, keepdims=True)

    return pl.pallas_call(
        body, grid=(SEQ // ROW_TILE,), in_specs=[_row_spec(), _vec_spec(), _row_spec()],
        out_specs=[_vec_spec(8, 128), _row_spec(), _vec_spec(8)],
        out_shape=[jax.ShapeDtypeStruct((8, 128), F32), jax.ShapeDtypeStruct((SEQ, D_MODEL), F32),
                   jax.ShapeDtypeStruct((8, D_MODEL), F32)],
        name="final_loss", compiler_params=_params(("arbitrary",)),
    )(x, gain, target)


def _prenorm_bwd_epilogue(dh, x_ref, dxo_ref, g_ref, sc_ref, dx_ref, stats_ref, first):
    xv = x_ref[...]
    g = g_ref[...]
    r = lax.rsqrt(jnp.mean(xv * xv, axis=-1, keepdims=True) + RMS_EPS)
    xh = xv * r
    dn = dh * (1.0 + sc_ref[...])
    dxh = dn * g
    dx = dxo_ref[...] + r * (dxh - xh * jnp.mean(dxh * xh, axis=-1, keepdims=True))
    dx_ref[...] = dx

    @pl.when(first)
    def _():
        stats_ref[...] = jnp.zeros_like(stats_ref)

    stats_ref[0:1, :] += jnp.sum(dh, axis=0, keepdims=True)
    stats_ref[1:2, :] += jnp.sum(dh * (xh * g), axis=0, keepdims=True)
    stats_ref[2:3, :] += jnp.sum(dn * xh, axis=0, keepdims=True)
    return dx


def _resid_bwd_epilogue(dx, f_ref, c_ref, mult, df_ref, dc_ref, first):
    df_ref[...] = (dx * (mult * c_ref[...])).astype(BF16)

    @pl.when(first)
    def _():
        dc_ref[...] = jnp.zeros_like(dc_ref)

    dc_ref[0:1, :] += mult * jnp.sum(dx * f_ref[...], axis=0, keepdims=True)


def ffn_up(h, wg_all, wu_all):
    def body(h_ref, wg_ref, wu_ref, a_ref, b_ref, s_ref):
        hv = h_ref[...]
        a = _dot(hv, wg_ref[...], NT)
        b = _dot(hv, wu_ref[...], NT)
        a_ref[...] = a.astype(BF16)
        b_ref[...] = b.astype(BF16)
        s_ref[...] = (a * _sigmoid(a) * b).astype(BF16)

    w_spec = pl.BlockSpec((None, FF_SHARD, D_MODEL), lambda j, i: (j, 0, 0))
    o_spec = pl.BlockSpec((None, MM_TILE, FF_SHARD), lambda j, i: (j, i, 0))
    hid = (N_CHIPS, SEQ, FF_SHARD)
    return pl.pallas_call(
        body, grid=(N_CHIPS, SEQ // MM_TILE),
        in_specs=[pl.BlockSpec((MM_TILE, D_MODEL), lambda j, i: (i, 0)), w_spec, w_spec],
        out_specs=[o_spec, o_spec, o_spec],
        out_shape=[jax.ShapeDtypeStruct(hid, BF16)] * 3,
        name="ffn_up", compiler_params=_params(("parallel", "parallel")),
    )(h, wg_all, wu_all)


def matmul_residual(name, a, a_spec, w_all, w_spec, x, coef, mult, then=None):
    def epilogue(acc, ins, outs, ids):
        outs[0][...] = acc
        xo = ins[2][...] + (mult * ins[3][...]) * acc
        outs[1][...] = xo
        if then is not None:
            r = lax.rsqrt(jnp.mean(xo * xo, axis=-1, keepdims=True) + RMS_EPS)
            outs[2][...] = (((xo * r) * ins[4][...]) * (1.0 + ins[5][...]) + ins[6][...]).astype(BF16)

    row = pl.BlockSpec((MM_TILE, D_MODEL), lambda i, j: (i, 0))
    f32 = jax.ShapeDtypeStruct((SEQ, D_MODEL), F32)
    extra = [] if then is None else [then.gain, then.scale, then.shift]
    return _matmul(
        name, (SEQ // MM_TILE, N_CHIPS), N_CHIPS, 1, NN, 1,
        [a_spec, w_spec, row, coef.spec()] + [t.spec() for t in extra], [row] * (2 + bool(extra)),
        [f32, f32] + [jax.ShapeDtypeStruct((SEQ, D_MODEL), BF16)] * bool(extra), (MM_TILE, D_MODEL), epilogue,
        (a, w_all, x, coef.table) + tuple(t.table for t in extra), ("parallel", "arbitrary"))


def ffn_down(s, wd_all, x, gate, then):
    return matmul_residual(
        "ffn_down", s, pl.BlockSpec((None, MM_TILE, FF_SHARD), lambda i, j: (j, i, 0)),
        wd_all, pl.BlockSpec((None, FF_SHARD, D_MODEL), lambda i, j: (j, 0, 0)), x, gate, 0.5, then)


def ffn_bwd_hidden(df, wd_all, a, b):
    def epilogue(ds, ins, outs, ids):
        av, bv = ins[2][...].astype(F32), ins[3][...].astype(F32)
        sig = _sigmoid(av)
        outs[0][...] = (ds * bv * (sig * (1.0 + av * (1.0 - sig)))).astype(BF16)
        outs[1][...] = (ds * (av * sig)).astype(BF16)

    hid_spec = pl.BlockSpec((None, MM_TILE, FF_SHARD), lambda j, i: (j, i, 0))
    hid = jax.ShapeDtypeStruct((N_CHIPS, SEQ, FF_SHARD), BF16)
    return _matmul(
        "ffn_bwd_hidden", (N_CHIPS, SEQ // MM_TILE), 1, None, NT, 1,
        [pl.BlockSpec((MM_TILE, D_MODEL), lambda j, i: (i, 0)),
         pl.BlockSpec((None, FF_SHARD, D_MODEL), lambda j, i: (j, 0, 0)), hid_spec, hid_spec],
        [hid_spec, hid_spec], [hid, hid], None, epilogue, (df, wd_all, a, b), ("parallel", "parallel"))


def grad_weight(name, lhs, lhs_spec, rhs, rhs_spec, shape):
    def epilogue(acc, ins, outs, ids):
        outs[0][...] = acc.astype(BF16)

    return _matmul(
        name, (N_CHIPS, SEQ // MM_TILE), SEQ // MM_TILE, 1, TN, 1,
        [lhs_spec, rhs_spec], [pl.BlockSpec((None,) + shape, lambda j, k: (j, 0, 0))],
        [jax.ShapeDtypeStruct((N_CHIPS,) + shape, BF16)], shape, epilogue, (lhs, rhs), ("parallel", "arbitrary"))[0]


def ffn_grad_weights(h, s, df, da, db):
    tok = pl.BlockSpec((MM_TILE, D_MODEL), lambda j, k: (k, 0))
    hid = pl.BlockSpec((None, MM_TILE, FF_SHARD), lambda j, k: (j, k, 0))
    n_k = SEQ // MM_TILE

    def body(da_ref, db_ref, s_ref, h_ref, df_ref, gg_ref, gu_ref, gd_ref, acc_g, acc_u, acc_d):
        k = pl.program_id(1)
        hv = h_ref[...]
        parts = (_dot(da_ref[...], hv, TN), _dot(db_ref[...], hv, TN), _dot(s_ref[...], df_ref[...], TN))

        @pl.when(k == 0)
        def _():
            acc_g[...], acc_u[...], acc_d[...] = parts

        @pl.when(k > 0)
        def _():
            acc_g[...] += parts[0]
            acc_u[...] += parts[1]
            acc_d[...] += parts[2]

        @pl.when(k == n_k - 1)
        def _():
            gg_ref[...] = acc_g[...].astype(BF16)
            gu_ref[...] = acc_u[...].astype(BF16)
            gd_ref[...] = acc_d[...].astype(BF16)

    out = pl.BlockSpec((None, FF_SHARD, D_MODEL), lambda j, k: (j, 0, 0))
    shape = jax.ShapeDtypeStruct((N_CHIPS, FF_SHARD, D_MODEL), BF16)
    return pl.pallas_call(
        body, grid=(N_CHIPS, n_k), in_specs=[hid, hid, hid, tok, tok], out_specs=[out] * 3, out_shape=[shape] * 3,
        scratch_shapes=[pltpu.VMEM((FF_SHARD, D_MODEL), F32)] * 3,
        name="ffn_grad_weights", compiler_params=_params(("parallel", "arbitrary")),
    )(da, db, s, h, df)


BWD_TILE = 512


def matmul_prenorm_bwd(name, dims, pairs, pair_specs, x, dxo, gain, scale, sums, u, below):
    n_pairs = len(pairs) // 2
    lhs, weights = pairs[0::2], pairs[1::2]
    n_in = n_pairs * 2 + 5 + (3 if below is not None else 0)
    n_out = 2 + (2 if below is not None else 0)

    def body(*refs):
        ins, outs = refs[:n_in], refs[n_in:n_in + n_out]
        w_vmem, w_sems = refs[n_in + n_out:n_in + n_out + n_pairs], refs[-1]
        first = pl.program_id(0) == 0

        @pl.when(first)
        def _():
            copies = [pltpu.make_async_copy(ins[n_pairs + p], w_vmem[p], w_sems.at[p]) for p in range(n_pairs)]
            for cp in copies:
                cp.start()
            for cp in copies:
                cp.wait()

        dh = None
        for j in range(N_CHIPS):
            for p in range(n_pairs):
                part = _dot(pair_specs[p][1](ins[p], j), w_vmem[p][j], dims)
                dh = part if dh is None else dh + part
        k = 2 * n_pairs
        dx = _prenorm_bwd_epilogue(dh, ins[k], ins[k + 1], ins[k + 2], ins[k + 3], outs[0], outs[1], first)
        if below is not None:
            _resid_bwd_epilogue(dx, ins[k + 5], ins[k + 6], below[2], outs[2], outs[3], first)

    row = pl.BlockSpec((BWD_TILE, D_MODEL), lambda i: (i, 0))
    any_spec = pl.BlockSpec(memory_space=pl.ANY)
    f32 = jax.ShapeDtypeStruct((SEQ, D_MODEL), F32)
    in_specs = [s for s, _ in pair_specs] + [any_spec] * n_pairs + [row, row, gain.spec(), scale.spec(), any_spec]
    operands = tuple(lhs) + tuple(weights) + (x, dxo, gain.table, scale.table, sums[0])
    out_specs, out_shape = [row, _slot_spec(u)], [f32, jax.ShapeDtypeStruct(sums[0].shape, F32)]
    aliases = {2 * n_pairs + 4: 1}
    if below is not None:
        in_specs += [row, below[1].spec(), any_spec]
        operands += (below[0], below[1].table, sums[1])
        out_specs += [row, _slot_spec(u - 1)]
        out_shape += [jax.ShapeDtypeStruct((SEQ, D_MODEL), BF16), jax.ShapeDtypeStruct(sums[1].shape, F32)]
        aliases[2 * n_pairs + 7] = 3
    out = pl.pallas_call(
        body, grid=(SEQ // BWD_TILE,), in_specs=in_specs, out_specs=out_specs, out_shape=out_shape,
        scratch_shapes=[pltpu.VMEM(w.shape, w.dtype) for w in weights] + [pltpu.SemaphoreType.DMA((n_pairs,))],
        input_output_aliases=aliases, name=name, compiler_params=_params(("arbitrary",)),
    )(*operands)
    if below is None:
        return out[0], (out[1], sums[1])
    return out[0], (out[1], out[3]), out[2]


def ffn_bwd_input(da, db, wg_all, wu_all, x, dxo, gain, scale, sums, u, below):
    hid = (pl.BlockSpec((N_CHIPS, BWD_TILE, FF_SHARD), lambda i: (0, i, 0)), lambda ref, j: ref[j])
    return matmul_prenorm_bwd("ffn_bwd_input", NN, (da, wg_all, db, wu_all), (hid, hid), x, dxo, gain, scale,
                              sums, u, below)


def in_proj(h, w_all):
    def epilogue(acc, ins, outs, ids):
        outs[0][...] = acc
        outs[1][...] = acc.astype(BF16)

    out = pl.BlockSpec((MM_TILE, IN_SHARD), lambda j, i: (i, j))
    return _matmul(
        "in_proj", (N_CHIPS, SEQ // MM_TILE), 1, None, NN, 1,
        [pl.BlockSpec((MM_TILE, D_MODEL), lambda j, i: (i, 0)),
         pl.BlockSpec((None, D_MODEL, IN_SHARD), lambda j, i: (j, 0, 0))],
        [out, out], [jax.ShapeDtypeStruct((SEQ, D_IN), F32), jax.ShapeDtypeStruct((SEQ, D_IN), BF16)],
        None, epilogue, (h, w_all), ("parallel", "parallel"))


_GATE_BLOCK0 = D_QKV // D_SHARD


def _branch_products(o, w_ref):
    ob = o.astype(BF16)
    return (_dot(ob[:, 0:256], w_ref[0:256, :], NN), _dot(ob[:, 256:384], w_ref[256:384, :], NN),
            _dot(ob[:, 384:768], w_ref[384:768, :], NN))


def merge_branches(o_cat, wbr_all, proj):
    def body(o_ref, w_ref, g0_ref, g1_ref, g2_ref, m_ref):
        u = _branch_products(o_ref[...], w_ref)
        m_ref[...] = (_sigmoid(g0_ref[...]) * u[0] + _sigmoid(g1_ref[...]) * u[1]
                      + _sigmoid(g2_ref[...]) * u[2]).astype(BF16)

    def gate_spec(b):
        return pl.BlockSpec((MM_TILE, D_SHARD), lambda i, j: (i, _GATE_BLOCK0 + 4 * b + j))

    return pl.pallas_call(
        body, grid=(SEQ // MM_TILE, N_CHIPS),
        in_specs=[pl.BlockSpec((MM_TILE, BR_ROWS), lambda i, j: (i, 0)),
                  pl.BlockSpec((None, BR_ROWS, D_SHARD), lambda i, j: (j, 0, 0)),
                  gate_spec(0), gate_spec(1), gate_spec(2)],
        out_specs=pl.BlockSpec((MM_TILE, D_SHARD), lambda i, j: (i, j)),
        out_shape=jax.ShapeDtypeStruct((SEQ, D_MODEL), BF16),
        name="merge_branches", compiler_params=_params(("parallel", "parallel")),
    )(o_cat, wbr_all, proj, proj, proj)


def out_proj(merged, wout_all, x, gate, then):
    return matmul_residual(
        "out_proj", merged, pl.BlockSpec((MM_TILE, D_SHARD), lambda i, j: (i, j)),
        wout_all, pl.BlockSpec((None, D_SHARD, D_MODEL), lambda i, j: (j, 0, 0)), x, gate, 1.0, then)


def merge_bwd(dmo, wout_all, o_cat, wbr_all, proj):
    def epilogue(dm, ins, outs, ids):
        u = _branch_products(ins[2][...], ins[3])
        for b in range(3):
            sig = _sigmoid(ins[4 + b][...])
            outs[b][...] = (dm * sig).astype(BF16)
            outs[3 + b][...] = (dm * u[b] * (sig * (1.0 - sig))).astype(BF16)

    def gate_spec(b):
        return pl.BlockSpec((MM_TILE, D_SHARD), lambda j, i: (i, _GATE_BLOCK0 + 4 * b + j))

    col = pl.BlockSpec((MM_TILE, D_SHARD), lambda j, i: (i, j))
    du = jax.ShapeDtypeStruct((SEQ, D_MODEL), BF16)
    return _matmul(
        "merge_bwd", (N_CHIPS, SEQ // MM_TILE), 1, None, NT, 1,
        [pl.BlockSpec((MM_TILE, D_MODEL), lambda j, i: (i, 0)),
         pl.BlockSpec((None, D_SHARD, D_MODEL), lambda j, i: (j, 0, 0)),
         pl.BlockSpec((MM_TILE, BR_ROWS), lambda j, i: (i, 0)),
         pl.BlockSpec((None, BR_ROWS, D_SHARD), lambda j, i: (j, 0, 0)),
         gate_spec(0), gate_spec(1), gate_spec(2)],
        [col] * 6, [du] * 6,
        None, epilogue, (dmo, wout_all, o_cat, wbr_all, proj, proj, proj), ("parallel", "parallel"))


def branch_bwd_input(du, wbr_all):
    def body(d0_ref, d1_ref, d2_ref, w_ref, o_ref, acc):
        j = pl.program_id(1)
        parts = (_dot(d0_ref[...], w_ref[0:256, :], NT), _dot(d1_ref[...], w_ref[256:384, :], NT),
                 _dot(d2_ref[...], w_ref[384:768, :], NT))

        @pl.when(j == 0)
        def _():
            acc[:, 0:256], acc[:, 256:384], acc[:, 384:768] = parts

        @pl.when(j > 0)
        def _():
            acc[:, 0:256] += parts[0]
            acc[:, 256:384] += parts[1]
            acc[:, 384:768] += parts[2]

        @pl.when(j == N_CHIPS - 1)
        def _():
            o_ref[...] = acc[...]

    col = pl.BlockSpec((MM_TILE, D_SHARD), lambda i, j: (i, j))
    return pl.pallas_call(
        body, grid=(SEQ // MM_TILE, N_CHIPS),
        in_specs=[col, col, col, pl.BlockSpec((None, BR_ROWS, D_SHARD), lambda i, j: (j, 0, 0))],
        out_specs=pl.BlockSpec((MM_TILE, BR_ROWS), lambda i, j: (i, 0)),
        out_shape=jax.ShapeDtypeStruct((SEQ, BR_ROWS), F32),
        scratch_shapes=[pltpu.VMEM((MM_TILE, BR_ROWS), F32)],
        name="branch_bwd_input", compiler_params=_params(("parallel", "arbitrary")),
    )(du[0], du[1], du[2], wbr_all)


def branch_grad_weights(o_cat, du):
    def body(o_ref, d0_ref, d1_ref, d2_ref, g_ref, acc):
        k = pl.program_id(1)
        ob = o_ref[...].astype(BF16)
        parts = (_dot(ob[:, 0:256], d0_ref[...], TN), _dot(ob[:, 256:384], d1_ref[...], TN),
                 _dot(ob[:, 384:768], d2_ref[...], TN))

        @pl.when(k == 0)
        def _():
            acc[0:256, :], acc[256:384, :], acc[384:768, :] = parts

        @pl.when(k > 0)
        def _():
            acc[0:256, :] += parts[0]
            acc[256:384, :] += parts[1]
            acc[384:768, :] += parts[2]

        @pl.when(k == SEQ // MM_TILE - 1)
        def _():
            g_ref[...] = acc[...].astype(BF16)

    col = pl.BlockSpec((MM_TILE, D_SHARD), lambda j, k: (k, j))
    return pl.pallas_call(
        body, grid=(N_CHIPS, SEQ // MM_TILE),
        in_specs=[pl.BlockSpec((MM_TILE, BR_ROWS), lambda j, k: (k, 0)), col, col, col],
        out_specs=pl.BlockSpec((None, BR_ROWS, D_SHARD), lambda j, k: (j, 0, 0)),
        out_shape=jax.ShapeDtypeStruct((N_CHIPS, BR_ROWS, D_SHARD), BF16),
        scratch_shapes=[pltpu.VMEM((BR_ROWS, D_SHARD), F32)],
        name="branch_grad_weights", compiler_params=_params(("parallel", "arbitrary")),
    )(o_cat, du[0], du[1], du[2])


def mixer_bwd_input(dproj, win_all, x, dxo, gain, scale, sums, u, below):
    columns = (pl.BlockSpec((BWD_TILE, D_IN), lambda i: (i, 0)),
               lambda ref, j: ref[:, IN_SHARD * j:IN_SHARD * (j + 1)])
    return matmul_prenorm_bwd("mixer_bwd_input", NT, (dproj, win_all), (columns,), x, dxo, gain, scale, sums, u, below)


BATCH_QK = (((2,), (2,)), ((0,), (0,)))
BATCH_PV = (((2,), (1,)), ((0,), (0,)))
BATCH_TN = (((1,), (1,)), ((0,), (0,)))


SB_WIDTH = H_SB * HEAD_DIM
SB_ROWS = H_SB * BLK


def _split_dot(v, tri):
    hi = v.astype(BF16)
    lo = (v - hi.astype(F32)).astype(BF16)
    return _dot(hi, tri, NN) + _dot(lo, tri, NN)


def _tri(cmp):
    return cmp(lax.broadcasted_iota(jnp.int32, (BLK, BLK), 0), lax.broadcasted_iota(jnp.int32, (BLK, BLK), 1)).astype(BF16)


def _head_masks():
    lane = lax.broadcasted_iota(jnp.int32, (1, SB_WIDTH), 1) // HEAD_DIM
    return [lane == h for h in range(H_SB)]


def _stack_heads(x, masks):
    return jnp.concatenate([jnp.where(m, x, jnp.zeros_like(x)) for m in masks], axis=0)


def _merge_heads(y, masks):
    out = jnp.where(masks[0], y[0:BLK], 0.0)
    for h in range(1, H_SB):
        out = jnp.where(masks[h], y[h * BLK:(h + 1) * BLK], out)
    return out


def _sb_scores(q4, k_ref, j, diagonal):
    rows = pl.ds(pl.multiple_of(j * BLK, BLK), BLK)
    z = _dot(q4, k_ref[rows, :], NT)
    log_fail = -(jnp.maximum(z, 0.0) + jnp.log(1.0 + jnp.exp(-jnp.abs(z))))
    log_hit = z + log_fail
    before = None
    if diagonal:
        tile = (SB_ROWS, BLK)
        before = lax.broadcasted_iota(jnp.int32, tile, 1) < (lax.broadcasted_iota(jnp.int32, tile, 0) & (BLK - 1))
        log_fail = jnp.where(before, log_fail, 0.0)
    return rows, before, log_fail, log_hit


def _keep(before, x):
    return x if before is None else jnp.where(before, x, 0.0)


def sb_forward(qkv):
    def body(q_ref, k_ref, v_ref, o_ref, tot_ref):
        i = pl.program_id(0)
        masks = _head_masks()
        q4 = _stack_heads(q_ref[...] * QK_SCALE, masks)
        later = _tri(lambda r, c: r > c)

        def tiles(js, carry, diagonal):
            o, run = carry
            scores = [_sb_scores(q4, k_ref, j, diagonal) for j in js]
            acc = None
            for rows, before, log_fail, log_hit in scores:
                between = _split_dot(log_fail, later) + run
                w = _keep(before, jnp.exp(log_hit + between))
                part = _dot(w.astype(BF16), v_ref[rows, :], NN)
                acc = part if acc is None else acc + part
                run = run + jnp.sum(log_fail, axis=1, keepdims=True)
            return o + _merge_heads(acc, masks), run

        carry = tiles([i], (jnp.zeros((BLK, SB_WIDTH), F32), jnp.zeros((SB_ROWS, 1), F32)), True)
        carry = lax.cond((i & 1) != 0, lambda c: tiles([i - 1], c, False), lambda c: c, carry)
        at = i - 1 - (i & 1)
        carry = lax.cond((i & 2) != 0, lambda c: tiles([at, at - 1], c, False), lambda c: c, carry)
        at = at - (i & 2)
        o, run = lax.fori_loop(0, i // 4, lambda t, c: tiles([at - 4 * t - n for n in range(4)], c, False), carry)
        o_ref[...] = o
        tot_ref[...] = run

    return pl.pallas_call(
        body, grid=(N_BLK,),
        in_specs=[pl.BlockSpec((BLK, SB_WIDTH), lambda i: (i, 0)), pl.BlockSpec((SEQ, SB_WIDTH), lambda i: (0, 1)),
                  pl.BlockSpec((SEQ, SB_WIDTH), lambda i: (0, 2))],
        out_specs=[pl.BlockSpec((BLK, SB_WIDTH), lambda i: (i, 0)), pl.BlockSpec((None, SB_ROWS, 1), lambda i: (i, 0, 0))],
        out_shape=[jax.ShapeDtypeStruct((SEQ, SB_WIDTH), F32), jax.ShapeDtypeStruct((N_BLK, SB_ROWS, 1), F32)],
        name="sb_forward", compiler_params=_params(("parallel",)),
    )(qkv, qkv, qkv)


def sb_backward(qkv, total, do_cat):
    def body(q_ref, k_ref, v_ref, tot_ref, do_ref, dq_ref, dk_ref, dv_ref):
        i = pl.program_id(0)

        @pl.when(i == 0)
        def _():
            dk_ref[...] = jnp.zeros_like(dk_ref)
            dv_ref[...] = jnp.zeros_like(dv_ref)

        masks = _head_masks()
        q4 = _stack_heads(q_ref[...] * QK_SCALE, masks)
        do4 = _stack_heads(do_ref[...].astype(BF16), masks)
        total_v = tot_ref[...]
        upto = _tri(lambda r, c: r <= c)
        earlier = _tri(lambda r, c: r < c)

        def tiles(js, carry, diagonal):
            dq, seen, g_seen = carry
            scores = [_sb_scores(q4, k_ref, j, diagonal) for j in js]
            acc = None
            for rows, before, log_fail, log_hit in scores:
                between = total_v - (seen + _split_dot(log_fail, upto))
                w = _keep(before, jnp.exp(log_hit + between))
                g = _dot(do4, v_ref[rows, :], NT) * w
                g_earlier = g_seen + _split_dot(g, earlier)
                sig = jnp.exp(log_hit)
                dz = _keep(before, g * (1.0 - sig) - g_earlier * sig).astype(BF16)
                part = _dot(dz, k_ref[rows, :], NN)
                acc = part if acc is None else acc + part
                dk_ref[rows, :] += _dot(dz, q4, TN)
                dv_ref[rows, :] += _dot(w.astype(BF16), do4, TN)
                seen = seen + jnp.sum(log_fail, axis=1, keepdims=True)
                g_seen = g_seen + jnp.sum(g, axis=1, keepdims=True)
            return dq + _merge_heads(acc, masks), seen, g_seen

        zero = jnp.zeros((SB_ROWS, 1), F32)
        carry = lax.fori_loop(0, i // 4, lambda t, c: tiles([4 * t + n for n in range(4)], c, False),
                              (jnp.zeros((BLK, SB_WIDTH), F32), zero, zero))
        at = i - (i & 3)
        carry = lax.cond((i & 2) != 0, lambda c: tiles([at, at + 1], c, False), lambda c: c, carry)
        carry = lax.cond((i & 1) != 0, lambda c: tiles([i - 1], c, False), lambda c: c, carry)
        dq, _, _ = tiles([i], carry, True)
        dq_ref[...] = dq * QK_SCALE

    blk = pl.BlockSpec((BLK, SB_WIDTH), lambda i: (i, 0))
    full = pl.BlockSpec((SEQ, SB_WIDTH), lambda i: (0, 0))
    shape = jax.ShapeDtypeStruct((SEQ, SB_WIDTH), F32)
    return pl.pallas_call(
        body, grid=(N_BLK,),
        in_specs=[blk, pl.BlockSpec((SEQ, SB_WIDTH), lambda i: (0, 1)), pl.BlockSpec((SEQ, SB_WIDTH), lambda i: (0, 2)),
                  pl.BlockSpec((None, SB_ROWS, 1), lambda i: (i, 0, 0)), blk],
        out_specs=[blk, full, full], out_shape=[shape, shape, shape],
        name="sb_backward", compiler_params=_params(("arbitrary",)),
    )(qkv, qkv, qkv, total, do_cat)


def _band_scores(q_ref, kp_ref, ko_ref, bias_ref, hb, prev_mask):
    b = pl.program_id(1)
    qs = q_ref[...]
    s_prev = _dot(qs, kp_ref[...], BATCH_QK) + bias_ref[:, :, 0:BLK]
    s_prev = jnp.concatenate(
        [jnp.where((b & prev_mask(pl.program_id(0) * hb + t)) != 0, s_prev[t:t + 1], NEG) for t in range(hb)], axis=0)
    s_own = _dot(qs, ko_ref[...], BATCH_QK) + bias_ref[:, :, BLK:2 * BLK]
    return qs, s_prev, s_own


def _band_specs(hb, rows, t_n):
    def q_spec(width):
        return pl.BlockSpec((hb, None, rows, width), lambda h, b: (h, b, 0, 0))

    own = pl.BlockSpec((hb, BLK, HEAD_DIM), lambda h, b: (h, b, 0))
    prev = pl.BlockSpec((hb, BLK, HEAD_DIM), lambda h, b: (h, jnp.maximum(b - 1, 0), 0))
    per_head = lambda r, width: pl.BlockSpec((hb, r, width), lambda h, b: (h, 0, 0))
    return q_spec, own, prev, per_head


def banded_forward(name, q, k, v, bias, sinks, hb, prev_mask):
    h_n, nb, rows, _ = q.shape

    def body(q_ref, kp_ref, ko_ref, vp_ref, vo_ref, bias_ref, sink_ref, o_ref, lse_ref):
        _, s_prev, s_own = _band_scores(q_ref, kp_ref, ko_ref, bias_ref, hb, prev_mask)
        sink = sink_ref[...]
        m = jnp.maximum(jnp.maximum(jnp.max(s_prev, axis=2, keepdims=True), jnp.max(s_own, axis=2, keepdims=True)), sink)
        p_prev = jnp.exp(s_prev - m)
        p_own = jnp.exp(s_own - m)
        denom = jnp.sum(p_prev, axis=2, keepdims=True) + jnp.sum(p_own, axis=2, keepdims=True) + jnp.exp(sink - m)
        o = _dot(p_prev.astype(BF16), vp_ref[...], BATCH_PV) + _dot(p_own.astype(BF16), vo_ref[...], BATCH_PV)
        o_ref[...] = o / denom
        lse_ref[...] = m + jnp.log(denom)

    q_spec, own, prev, per_head = _band_specs(hb, rows, k.shape[1])
    return pl.pallas_call(
        body, grid=(h_n // hb, nb),
        in_specs=[q_spec(HEAD_DIM), prev, own, prev, own, per_head(rows, 2 * BLK), per_head(rows, 1)],
        out_specs=[q_spec(HEAD_DIM), q_spec(1)],
        out_shape=[jax.ShapeDtypeStruct(q.shape, F32), jax.ShapeDtypeStruct((h_n, nb, rows, 1), F32)],
        name=name, compiler_params=_params(("parallel", "parallel")),
    )(q, k, k, v, v, bias, sinks)


def banded_backward(name, q, k, v, bias, sinks, o, lse, do, dlse, hb, prev_mask):
    h_n, nb, rows, _ = q.shape
    t_n = k.shape[1]

    def body(q_ref, kp_ref, ko_ref, vp_ref, vo_ref, bias_ref, sink_ref, o_ref, lse_ref, do_ref, dlse_ref,
             dq_ref, dk_ref, dv_ref, dbias_ref, dsink_ref):
        b = pl.program_id(1)

        @pl.when(b == 0)
        def _():
            dk_ref[...] = jnp.zeros_like(dk_ref)
            dv_ref[...] = jnp.zeros_like(dv_ref)
            dbias_ref[...] = jnp.zeros_like(dbias_ref)
            dsink_ref[...] = jnp.zeros_like(dsink_ref)

        qs, s_prev, s_own = _band_scores(q_ref, kp_ref, ko_ref, bias_ref, hb, prev_mask)
        lse_v = lse_ref[...]
        dov = do_ref[...]
        dob = dov.astype(BF16)
        shift = dlse_ref[...] - jnp.sum(dov * o_ref[...], axis=2, keepdims=True)
        p_prev = jnp.exp(s_prev - lse_v)
        p_own = jnp.exp(s_own - lse_v)
        ds_prev = p_prev * (_dot(dob, vp_ref[...], BATCH_QK) + shift)
        ds_own = p_own * (_dot(dob, vo_ref[...], BATCH_QK) + shift)
        dbias_ref[:, :, 0:BLK] += ds_prev
        dbias_ref[:, :, BLK:2 * BLK] += ds_own
        d_sink = jnp.exp(sink_ref[...] - lse_v) * shift
        for g in range(rows // BLK):
            dsink_ref[:, g:g + 1, :] += jnp.sum(d_sink[:, g * BLK:(g + 1) * BLK, :], axis=1, keepdims=True)
        ds_prev = ds_prev.astype(BF16)
        ds_own = ds_own.astype(BF16)
        dq_ref[...] = (_dot(ds_prev, kp_ref[...], BATCH_PV) + _dot(ds_own, ko_ref[...], BATCH_PV)) * QK_SCALE
        rows_prev = pl.ds(pl.multiple_of(jnp.maximum(b - 1, 0) * BLK, BLK), BLK)
        rows_own = pl.ds(pl.multiple_of(b * BLK, BLK), BLK)
        dk_ref[:, rows_prev, :] += _dot(ds_prev, qs, BATCH_TN)
        dk_ref[:, rows_own, :] += _dot(ds_own, qs, BATCH_TN)
        dv_ref[:, rows_prev, :] += _dot(p_prev.astype(BF16), dob, BATCH_TN)
        dv_ref[:, rows_own, :] += _dot(p_own.astype(BF16), dob, BATCH_TN)

    q_spec, own, prev, per_head = _band_specs(hb, rows, t_n)
    kv_full = per_head(t_n, HEAD_DIM)
    kv_shape = jax.ShapeDtypeStruct((h_n, t_n, HEAD_DIM), F32)
    return pl.pallas_call(
        body, grid=(h_n // hb, nb),
        in_specs=[q_spec(HEAD_DIM), prev, own, prev, own, per_head(rows, 2 * BLK), per_head(rows, 1),
                  q_spec(HEAD_DIM), q_spec(1), q_spec(HEAD_DIM), q_spec(1)],
        out_specs=[q_spec(HEAD_DIM), kv_full, kv_full, per_head(rows, 2 * BLK), per_head(rows // BLK, BLK)],
        out_shape=[jax.ShapeDtypeStruct(q.shape, F32), kv_shape, kv_shape,
                   jax.ShapeDtypeStruct((h_n, rows, 2 * BLK), F32), jax.ShapeDtypeStruct((h_n, rows // BLK, BLK), F32)],
        name=name, compiler_params=_params(("parallel", "arbitrary")),
    )(q, k, k, v, v, bias, sinks, o, lse, do, dlse)


def _swa_prev_mask(head):
    del head
    return 15


SWA_HEADS_PER_STEP = 2
SWA_GROUP = H_SWA_Q // H_SWA_KV
N_BLK = SEQ // BLK


GROUP_W = H_PER_DIL * HEAD_DIM
DIL_COLUMNS = (768, 1920)
LANE_BLOCKS = (DIL_COLUMNS[1] - DIL_COLUMNS[0]) // GROUP_W
DIL_Q_BLOCK, DIL_K_BLOCK, DIL_V_BLOCK = 0, 3, 6
N_GROUPS = len(DIL_PATTERNS)


def dilated_views(qkv):
    cols = qkv[:, DIL_COLUMNS[0]:DIL_COLUMNS[1]]
    return [_dil_view(cols, d) for _, d in DIL_PATTERNS]


def _dil_view(t, d):
    return t.reshape(SEQ // d, d * t.shape[1])


def _dil_tile(n, d):
    per_class = N_BLK // d
    return n // per_class, n % per_class


def _dil_spec(d, lane_block, lane_blocks, shift=0):
    def index(n):
        r, m = _dil_tile(n, d)
        m = jnp.clip(m + shift, 0, N_BLK // d - 1)
        return m, r * lane_blocks + lane_block
    return pl.BlockSpec((BLK, GROUP_W), index)


def _two_heads(x, first):
    zero = jnp.zeros_like(x)
    return jnp.concatenate([jnp.where(first, x, zero), jnp.where(first, zero, x)], axis=0)


def _per_head(col, first):
    return jnp.where(first, col[0:BLK], col[BLK:2 * BLK])


def _head_rows(tile, first):
    pick = lambda keep: jnp.max(jnp.where(keep, tile, -jnp.inf), axis=1, keepdims=True)
    return jnp.concatenate([pick(first), pick(jnp.logical_not(first))], axis=0)


def _dil_scores(q_ref, kp_ref, ko_ref, bias, has_prev, first):
    q2 = _two_heads(q_ref[...] * QK_SCALE, first)
    k2 = jnp.concatenate([kp_ref[...], ko_ref[...]], axis=0)
    s = _dot(q2, k2, NT) + bias
    key = lax.broadcasted_iota(jnp.int32, s.shape, 1)
    return q2, k2, jnp.where(jnp.logical_or(has_prev, key >= BLK), s, NEG)


def dilated_forward(views, bias):
    def body(*refs):
        ins, bias_ref, outs = refs[:5 * N_GROUPS], refs[5 * N_GROUPS], refs[5 * N_GROUPS + 1:]
        n = pl.program_id(0)
        first = lax.broadcasted_iota(jnp.int32, (1, GROUP_W), 1) < HEAD_DIM
        for g, (_, d) in enumerate(DIL_PATTERNS):
            q_ref, kp_ref, ko_ref, vp_ref, vo_ref = ins[5 * g:5 * g + 5]
            has_prev = _dil_tile(n, d)[1] > 0
            _, _, s = _dil_scores(q_ref, kp_ref, ko_ref, bias_ref[g], has_prev, first)
            m = jnp.max(s, axis=1, keepdims=True)
            p = jnp.exp(s - m)
            denom = jnp.sum(p, axis=1, keepdims=True)
            v2 = jnp.concatenate([vp_ref[...], vo_ref[...]], axis=0)
            o2 = _dot(p.astype(BF16), v2, NN) / denom
            outs[2 * g][...] = _per_head(o2, first)
            outs[2 * g + 1][...] = _per_head(m + jnp.log(denom), first)

    operands, in_specs, out_specs, out_shape = [], [], [], []
    for g, (_, d) in enumerate(DIL_PATTERNS):
        operands += [views[g]] * 5
        in_specs += [_dil_spec(d, DIL_Q_BLOCK + g, LANE_BLOCKS), _dil_spec(d, DIL_K_BLOCK + g, LANE_BLOCKS, -1),
                     _dil_spec(d, DIL_K_BLOCK + g, LANE_BLOCKS), _dil_spec(d, DIL_V_BLOCK + g, LANE_BLOCKS, -1),
                     _dil_spec(d, DIL_V_BLOCK + g, LANE_BLOCKS)]
        out_specs += [_dil_spec(d, 0, 1)] * 2
        out_shape += [jax.ShapeDtypeStruct((SEQ // d, d * GROUP_W), F32)] * 2
    out = pl.pallas_call(
        body, grid=(N_BLK,), in_specs=in_specs + [pl.BlockSpec((N_GROUPS, 2 * BLK, 2 * BLK), lambda n: (0, 0, 0))],
        out_specs=out_specs, out_shape=out_shape, name="dilated_forward", compiler_params=_params(("parallel",)),
    )(*operands, bias)
    out = [t.reshape(SEQ, GROUP_W) for t in out]
    return out[0::2], out[1::2]


def _group_softmax(lses):
    m = jnp.maximum(jnp.maximum(lses[0], lses[1]), lses[2])
    e = [jnp.exp(l - m) for l in lses]
    total = e[0] + e[1] + e[2]
    return [t / total for t in e]


def dilated_merge(o, lse):
    def body(*refs):
        alpha = _group_softmax([r[...] for r in refs[N_GROUPS:2 * N_GROUPS]])
        refs[-1][...] = alpha[0] * refs[0][...] + alpha[1] * refs[1][...] + alpha[2] * refs[2][...]

    spec = pl.BlockSpec((ROW_TILE, GROUP_W), lambda i: (i, 0))
    return pl.pallas_call(
        body, grid=(SEQ // ROW_TILE,), in_specs=[spec] * (2 * N_GROUPS), out_specs=spec,
        out_shape=jax.ShapeDtypeStruct((SEQ, GROUP_W), F32), name="dilated_merge", compiler_params=_params(("parallel",)),
    )(*o, *lse)


def dilated_merge_bwd(o, lse, do_cat):
    def body(*refs):
        o_v = [r[...] for r in refs[:N_GROUPS]]
        alpha = _group_softmax([r[...] for r in refs[N_GROUPS:2 * N_GROUPS]])
        dout = refs[2 * N_GROUPS][...]
        outs = refs[2 * N_GROUPS + 1:]
        first = lax.broadcasted_iota(jnp.int32, (1, GROUP_W), 1) < HEAD_DIM

        def head_sum(x):
            a = jnp.sum(jnp.where(first, x, 0.0), axis=1, keepdims=True)
            b = jnp.sum(jnp.where(first, 0.0, x), axis=1, keepdims=True)
            return jnp.where(first, a, b)

        dalpha = [head_sum(dout * o_g) for o_g in o_v]
        mean = alpha[0] * dalpha[0] + alpha[1] * dalpha[1] + alpha[2] * dalpha[2]
        for g in range(N_GROUPS):
            outs[g][...] = alpha[g] * dout
            outs[N_GROUPS + g][...] = alpha[g] * (dalpha[g] - mean)

    spec = pl.BlockSpec((ROW_TILE, GROUP_W), lambda i: (i, 0))
    shape = jax.ShapeDtypeStruct((SEQ, GROUP_W), F32)
    out = pl.pallas_call(
        body, grid=(SEQ // ROW_TILE,), in_specs=[spec] * (2 * N_GROUPS) + [pl.BlockSpec((ROW_TILE, GROUP_W), lambda i: (i, 2))],
        out_specs=[spec] * (2 * N_GROUPS), out_shape=[shape] * (2 * N_GROUPS),
        name="dilated_merge_bwd", compiler_params=_params(("parallel",)),
    )(*o, *lse, do_cat)
    return out[:N_GROUPS], out[N_GROUPS:]


def dilated_backward(views, bias, o, lse, do, dlse):
    n_in = 9

    def body(*refs):
        ins, bias_ref = refs[:n_in * N_GROUPS], refs[n_in * N_GROUPS]
        outs, dbias_ref = refs[n_in * N_GROUPS + 1:-1], refs[-1]
        n = pl.program_id(0)

        @pl.when(n == 0)
        def _():
            dbias_ref[...] = jnp.zeros_like(dbias_ref)

        first = lax.broadcasted_iota(jnp.int32, (1, GROUP_W), 1) < HEAD_DIM
        for g, (_, d) in enumerate(DIL_PATTERNS):
            q_ref, kp_ref, ko_ref, vp_ref, vo_ref, o_ref, lse_ref, do_ref, dlse_ref = ins[n_in * g:n_in * (g + 1)]
            has_prev = _dil_tile(n, d)[1] > 0
            q2, k2, s = _dil_scores(q_ref, kp_ref, ko_ref, bias_ref[g], has_prev, first)
            dov = do_ref[...]
            do2 = _two_heads(dov.astype(BF16), first)
            prod = dov * o_ref[...]
            delta = jnp.concatenate([jnp.sum(jnp.where(first, prod, 0.0), axis=1, keepdims=True),
                                     jnp.sum(jnp.where(first, 0.0, prod), axis=1, keepdims=True)], axis=0)
            shift = _head_rows(dlse_ref[...], first) - delta
            p = jnp.exp(s - _head_rows(lse_ref[...], first))
            v2 = jnp.concatenate([vp_ref[...], vo_ref[...]], axis=0)
            ds = p * (_dot(do2, v2, NT) + shift)
            dbias_ref[g] += ds
            ds = ds.astype(BF16)
            dq2 = _dot(ds, k2, NN) * QK_SCALE
            dk2 = _dot(ds, q2, TN)
            dv2 = _dot(p.astype(BF16), do2, TN)
            base = 5 * g
            outs[base][...] = jnp.where(first, dq2[0:BLK], dq2[BLK:2 * BLK])
            outs[base + 1][...] = dk2[BLK:2 * BLK]
            outs[base + 2][...] = dk2[0:BLK]
            outs[base + 3][...] = dv2[BLK:2 * BLK]
            outs[base + 4][...] = dv2[0:BLK]

    operands, in_specs, out_specs, out_shape = [], [], [], []
    for g, (_, d) in enumerate(DIL_PATTERNS):
        own = _dil_spec(d, 0, 1)
        operands += [views[g]] * 5 + [_dil_view(t[g], d) for t in (o, lse, do, dlse)]
        in_specs += [_dil_spec(d, DIL_Q_BLOCK + g, LANE_BLOCKS), _dil_spec(d, DIL_K_BLOCK + g, LANE_BLOCKS, -1),
                     _dil_spec(d, DIL_K_BLOCK + g, LANE_BLOCKS), _dil_spec(d, DIL_V_BLOCK + g, LANE_BLOCKS, -1),
                     _dil_spec(d, DIL_V_BLOCK + g, LANE_BLOCKS)] + [own] * 4
        out_specs += [own] * 5
        out_shape += [jax.ShapeDtypeStruct((SEQ // d, d * GROUP_W), F32)] * 5
    tiles = pl.BlockSpec((N_GROUPS, 2 * BLK, 2 * BLK), lambda n: (0, 0, 0))
    out = pl.pallas_call(
        body, grid=(N_BLK,), in_specs=in_specs + [tiles], out_specs=out_specs + [tiles],
        out_shape=out_shape + [jax.ShapeDtypeStruct((N_GROUPS, 2 * BLK, 2 * BLK), F32)],
        name="dilated_backward", compiler_params=_params(("arbitrary",)),
    )(*operands, bias)
    return [out[5 * g:5 * g + 5] for g in range(N_GROUPS)], out[-1]


def dilated_key_grads(parts):
    def body(*refs):
        ins, outs = refs[:4 * N_GROUPS], refs[4 * N_GROUPS:]
        n = pl.program_id(0)
        for g, (_, d) in enumerate(DIL_PATTERNS):
            has_next = _dil_tile(n, d)[1] < N_BLK // d - 1
            own_k, next_k, own_v, next_v = ins[4 * g:4 * g + 4]
            outs[2 * g][...] = own_k[...] + jnp.where(has_next, next_k[...], 0.0)
            outs[2 * g + 1][...] = own_v[...] + jnp.where(has_next, next_v[...], 0.0)

    operands, in_specs, out_specs, out_shape = [], [], [], []
    for g, (_, d) in enumerate(DIL_PATTERNS):
        _, dk_own, dk_prev, dv_own, dv_prev = parts[g]
        operands += [dk_own, dk_prev, dv_own, dv_prev]
        in_specs += [_dil_spec(d, 0, 1), _dil_spec(d, 0, 1, 1)] * 2
        out_specs += [_dil_spec(d, 0, 1)] * 2
        out_shape += [jax.ShapeDtypeStruct((SEQ // d, d * GROUP_W), F32)] * 2
    out = pl.pallas_call(
        body, grid=(N_BLK,), in_specs=in_specs, out_specs=out_specs, out_shape=out_shape,
        name="dilated_key_grads", compiler_params=_params(("parallel",)),
    )(*operands)
    tok = lambda ts: jnp.concatenate([t.reshape(SEQ, GROUP_W) for t in ts], axis=1)
    return tok([parts[g][0] for g in range(N_GROUPS)]), tok(out[0::2]), tok(out[1::2])


def rel_bias_reduce(dbias0, dbias1, bucket):
    def body(d0_ref, d1_ref, b_ref, o_ref):
        dv, bv = d0_ref[...] + d1_ref[...], b_ref[...]
        lane = lax.broadcasted_iota(jnp.int32, (1, BLK), 1)
        acc = jnp.zeros((1, BLK), F32)
        for bkt in range(N_BUCKETS):
            acc = acc + jnp.where(lane == bkt, jnp.sum(jnp.where(bv == bkt, dv, 0.0)), 0.0)
        o_ref[...] = acc

    tile = pl.BlockSpec((None, BLK, 2 * BLK), lambda h: (h, 0, 0))
    return pl.pallas_call(
        body, grid=(dbias0.shape[0],), in_specs=[tile, tile, tile],
        out_specs=pl.BlockSpec((None, 1, BLK), lambda h: (h, 0, 0)),
        out_shape=jax.ShapeDtypeStruct((dbias0.shape[0], 1, BLK), F32),
        name="rel_bias_reduce", compiler_params=_params(("parallel",)),
    )(dbias0, dbias1, bucket)


def _heads(t):
    return t.reshape(SEQ, -1, HEAD_DIM).transpose(1, 0, 2)


def _unheads(t):
    return t.transpose(1, 0, 2).reshape(SEQ, -1)


def _t5_bucket(n):
    max_exact = N_BUCKETS // 2
    nf = jnp.maximum(n, 1).astype(F32)
    large = max_exact + (jnp.log(nf / max_exact) / math.log(MAX_REL_DIST / max_exact)
                         * (N_BUCKETS - max_exact)).astype(jnp.int32)
    large = jnp.minimum(large, N_BUCKETS - 1)
    return jnp.where(n < max_exact, n, large)


def band_tables(rel_bias):
    rel = jnp.arange(BLK)[:, None] + BLK - jnp.arange(2 * BLK)[None, :]
    patterns = [(d, w // d, H_PER_DIL) for w, d in DIL_PATTERNS] + [(1, SWA_WINDOW - 1, H_SWA_Q)]
    buckets = []
    for d, max_dist, heads in patterns:
        band = (rel >= 0) & (rel <= max_dist)
        tile = jnp.where(band, _t5_bucket(jnp.maximum(rel, 0) * d), -1).astype(jnp.int32)
        buckets.append(jnp.broadcast_to(tile, (heads,) + tile.shape))
    buckets = jnp.concatenate(buckets, axis=0)

    def body(table_ref, b_ref, o_ref):
        h = pl.program_id(0)
        bv = b_ref[...]
        tile = jnp.full(bv.shape, NEG, F32)
        for bkt in range(N_BUCKETS):
            tile = jnp.where(bv == bkt, table_ref[h, bkt], tile)
        o_ref[...] = tile

    spec = pl.BlockSpec((None, BLK, 2 * BLK), lambda h: (h, 0, 0))
    tiles = pl.pallas_call(
        body, grid=(buckets.shape[0],), in_specs=[pl.BlockSpec(memory_space=pltpu.SMEM), spec], out_specs=spec,
        out_shape=jax.ShapeDtypeStruct(buckets.shape, F32), name="band_tables", compiler_params=_params(("parallel",)),
    )(rel_bias.T, buckets)
    return tiles[:H_DIL], tiles[H_DIL:], buckets


def _swa_rows(t):
    t = t.reshape(N_BLK, BLK, H_SWA_KV, SWA_GROUP, HEAD_DIM).transpose(2, 0, 3, 1, 4)
    return t.reshape(H_SWA_KV, N_BLK, SWA_GROUP * BLK, HEAD_DIM)


def _swa_tokens(t):
    t = t.reshape(H_SWA_KV, N_BLK, SWA_GROUP, BLK, HEAD_DIM).transpose(1, 3, 0, 2, 4)
    return t.reshape(SEQ, H_SWA_Q * HEAD_DIM)


def _sink_rows(sinks):
    return jnp.broadcast_to(sinks.reshape(H_SWA_KV, SWA_GROUP, 1, 1), (H_SWA_KV, SWA_GROUP, BLK, 1)).reshape(
        H_SWA_KV, SWA_GROUP * BLK, 1)


def _vec(v):
    return v.reshape(1, D_MODEL)


class UnitRows:
    def __init__(self, u, mod_table, gain_table):
        self.shift, self.scale, self.gate = (Row(mod_table, 3 * u + t) for t in range(3))
        self.gain = Row(gain_table, u)


def ffn_forward(x, h, rows, then, w):
    a, b, s = ffn_up(h, w[0], w[1])
    f, xo, *h_next = ffn_down(s, w[2], x, rows.gate, then)
    return xo, (h_next or [None])[0], (x, h, a, b, s, f)


def ffn_backward(u, dxo, df, saved, rows, w, sums, below):
    x, h, a, b, s, _ = saved
    da, db = ffn_bwd_hidden(df, w[2], a, b)
    grads = ffn_grad_weights(h, s, df, da, db)
    dx, sums, *df_below = ffn_bwd_input(da, db, w[0], w[1], x, dxo, rows.gain, rows.scale, sums, u, below)
    return dx, sums, df_below, grads


def mixer_forward(x, h, rows, then, sinks, bias_dil, bias_swa, w):
    proj, qkv = in_proj(h, w[0])
    q_swa, k_swa, v_swa = _swa_rows(qkv[:, 1920:2304] * QK_SCALE), _heads(qkv[:, 2304:2432]), _heads(qkv[:, 2432:2560])
    o_sb, total_sb = sb_forward(qkv)
    bias_dil = bias_dil.reshape(N_GROUPS, 2 * BLK, 2 * BLK)
    views = dilated_views(qkv)
    o_groups, lse_groups = dilated_forward(views, bias_dil)
    o_dil = dilated_merge(o_groups, lse_groups)
    bias_swa = bias_swa.reshape(H_SWA_KV, SWA_GROUP * BLK, 2 * BLK)
    o_swa, lse_swa = banded_forward("swa_forward", q_swa, k_swa, v_swa, bias_swa, _sink_rows(sinks), SWA_HEADS_PER_STEP,
                                    _swa_prev_mask)
    o_cat = jnp.concatenate([o_sb, o_dil, _swa_tokens(o_swa)], axis=1)
    merged = merge_branches(o_cat, w[1], proj)
    mo, xo, *h_next = out_proj(merged, w[2], x, rows.gate, then)
    saved = (x, h, proj, (qkv, total_sb), (views, o_groups, lse_groups),
             (q_swa, k_swa, v_swa, o_swa, lse_swa), o_cat, merged, mo)
    return xo, (h_next or [None])[0], saved


def mixer_backward(u, dxo, dmo, saved, rows, sinks, bias_dil, bias_swa, w, sums, below):
    x, h, proj, sb, dil, swa, o_cat, merged, _ = saved
    tok = pl.BlockSpec((MM_TILE, D_MODEL), lambda j, k: (k, 0))
    g_out = grad_weight("grad_w_out", merged, pl.BlockSpec((MM_TILE, D_SHARD), lambda j, k: (k, j)), dmo, tok,
                        (D_SHARD, D_MODEL))
    du0, du1, du2, dg0, dg1, dg2 = merge_bwd(dmo, w[2], o_cat, w[1], proj)
    du = (du0, du1, du2)
    do_cat = branch_bwd_input(du, w[1])
    g_br = branch_grad_weights(o_cat, du)

    qkv, total_sb = sb
    dq_sb, dk_sb, dv_sb = sb_backward(qkv, total_sb, do_cat)

    views, o_groups, lse_groups = dil
    bias_dil = bias_dil.reshape(N_GROUPS, 2 * BLK, 2 * BLK)
    do_groups, dlse_groups = dilated_merge_bwd(o_groups, lse_groups, do_cat)
    parts, dbias_dil = dilated_backward(views, bias_dil, o_groups, lse_groups, do_groups, dlse_groups)
    dq_dil, dk_dil, dv_dil = dilated_key_grads(parts)
    dbias_dil = dbias_dil.reshape(H_DIL, BLK, 2 * BLK)

    q_swa, k_swa, v_swa, o_swa, lse_swa = swa
    bias_swa = bias_swa.reshape(H_SWA_KV, SWA_GROUP * BLK, 2 * BLK)
    dq_swa, dk_swa, dv_swa, dbias_swa, dsinks = banded_backward(
        "swa_backward", q_swa, k_swa, v_swa, bias_swa, _sink_rows(sinks), o_swa, lse_swa, _swa_rows(do_cat[:, 384:768]),
        jnp.zeros_like(lse_swa), SWA_HEADS_PER_STEP, _swa_prev_mask)
    dbias_swa = dbias_swa.reshape(H_SWA_Q, BLK, 2 * BLK)

    dproj = jnp.concatenate(
        [dq_sb, dk_sb, dv_sb, dq_dil, dk_dil, dv_dil, _swa_tokens(dq_swa), _unheads(dk_swa), _unheads(dv_swa)],
        axis=1).astype(BF16)
    dproj = jnp.concatenate([dproj, dg0, dg1, dg2], axis=1)
    g_in = grad_weight("grad_w_in", h, tok, dproj, pl.BlockSpec((MM_TILE, IN_SHARD), lambda j, k: (k, j)),
                       (D_MODEL, IN_SHARD))
    dx, sums, *df_below = mixer_bwd_input(dproj, w[0], x, dxo, rows.gain, rows.scale, sums, u, below)
    dbias = jnp.concatenate([dbias_dil, dbias_swa], axis=0)
    return dx, sums, df_below, dbias, dsinks[:, :, 0].reshape(H_SWA_Q), (g_in, g_br, g_out)


N_UNITS = 3 * DEPTH


def device_step(x, target, mod, gains, final_gain, sinks, rel_bias, get_weights, put_grads):
    bias_dil, bias_swa, bucket = band_tables(rel_bias)
    mod_table = mod.reshape(3 * N_UNITS, 1, D_MODEL)
    gain_table = gains.reshape(N_UNITS, 1, D_MODEL)
    saved, weights = [], []
    units = [UnitRows(u, mod_table, gain_table) for u in range(N_UNITS)]
    h = prenorm(x, units[0].gain, units[0].scale, units[0].shift)
    for u in range(N_UNITS):
        l, j = divmod(u, 3)
        w = get_weights(u, x)
        then = units[u + 1] if u + 1 < N_UNITS else None
        if j == 1:
            x, h, s = mixer_forward(x, h, units[u], then, sinks[l], bias_dil, bias_swa, w)
        else:
            x, h, s = ffn_forward(x, h, units[u], then, w)
        saved.append(s)
        weights.append(w)
    loss, dx, dfinal = final_loss(x, _vec(final_gain), target)

    sums = (lax.empty((8 * N_UNITS, D_MODEL), F32), lax.empty((8 * N_UNITS, D_MODEL), F32))
    dbias, dsinks = [None] * DEPTH, [None] * DEPTH
    zero = jnp.zeros((1, 1), F32)
    top = N_UNITS - 1
    df, gate_sums = resid_bwd(dx, saved[top][-1], units[top].gate, 0.5, sums[1], top)
    sums = (sums[0], gate_sums)
    for u in reversed(range(N_UNITS)):
        l, j = divmod(u, 3)
        rows = UnitRows(u, mod_table, gain_table + zero)
        below = (saved[u - 1][-1], units[u - 1].gate, 1.0 if (u - 1) % 3 == 1 else 0.5) if u > 0 else None
        if j == 1:
            dx, sums, df, dbias[l], dsinks[l], grads = mixer_backward(
                u, dx, df, saved[u], rows, sinks[l], bias_dil, bias_swa, weights[u], sums, below)
        else:
            dx, sums, df, grads = ffn_backward(u, dx, df, saved[u], rows, weights[u], sums, below)
        df = df[0] if df else None
        if u > 0:
            zero = put_grads(u, grads)
    drel = rel_bias_reduce(dbias[0], dbias[1], bucket)[:, 0, :N_BUCKETS].T
    norm_sums, gate_sums = (t.reshape(DEPTH, 3, 8, D_MODEL) for t in sums)
    dmod = jnp.stack([norm_sums[:, :, 0], norm_sums[:, :, 1], gate_sums[:, :, 0]], axis=2)
    return loss, dx, dmod, norm_sums[:, :, 2], dfinal[0], jnp.stack(dsinks), drel, grads


MESH = pl.DeviceIdType.MESH
CHIP_FLIPS = ((1, 0), (0, 1), (1, 1))
ANY = pl.BlockSpec(memory_space=pl.ANY)


def _position():
    return lax.axis_index("x"), lax.axis_index("y"), lax.axis_index("c")


def all_gather_small(name, piece):
    def body(x_ref, out_ref, send_sems, recv_sems, local_sem):
        x, y, c = _position()
        me, sibling = (x, y, c), (x, y, 1 - c)
        chips = [(x ^ fx, y ^ fy) for fx, fy in CHIP_FLIPS]

        def rows(px, py, pc):
            return out_ref.at[4 * px + 2 * py + pc]

        def copy(k, block, to, src=None):
            return pltpu.make_async_remote_copy(
                src_ref=rows(*block) if src is None else src, dst_ref=rows(*block),
                send_sem=send_sems.at[k], recv_sem=recv_sems.at[k], device_id=to, device_id_type=MESH)

        mine = pltpu.make_async_copy(x_ref, rows(*me), local_sem)
        mine.start()
        first = [copy(0, me, sibling, src=x_ref)]
        first += [copy(1 + j, me, (*chip, c), src=x_ref) for j, chip in enumerate(chips)]
        for cp in first:
            cp.start()
        passed = [copy(4 + j, (*chip, c), sibling) for j, chip in enumerate(chips)]
        for j, chip in enumerate(chips):
            copy(1 + j, (*chip, c), me).wait_recv()
            passed[j].start()
        copy(0, sibling, me).wait_recv()
        for j, chip in enumerate(chips):
            copy(4 + j, (*chip, 1 - c), me).wait_recv()
        for cp in first + passed:
            cp.wait_send()
        mine.wait()

    return pl.pallas_call(
        body, out_shape=jax.ShapeDtypeStruct((N_DEV,) + piece.shape, piece.dtype),
        in_specs=[pl.BlockSpec(memory_space=pltpu.VMEM)], out_specs=pl.BlockSpec(memory_space=pltpu.VMEM),
        scratch_shapes=[pltpu.SemaphoreType.DMA((7,)), pltpu.SemaphoreType.DMA((7,)), pltpu.SemaphoreType.DMA],
        name=name,
    )(piece)


def exchange(name, operands, out_shapes, aliases, plan):
    n_in, n_out = len(operands), len(out_shapes)

    def body(*refs):
        ins, outs = refs[:n_in], refs[n_in:n_in + n_out]
        send_sems, recv_sems, local_sems = refs[n_in + n_out:]
        x, y, c = _position()
        local, sends, recvs = plan(ins, outs, x, y, c)
        local = [pltpu.make_async_copy(s, d, local_sems.at[k]) for k, (s, d) in enumerate(local)]
        for cp in local:
            cp.start()
        remote = [pltpu.make_async_remote_copy(src_ref=s, dst_ref=d, send_sem=send_sems.at[k], recv_sem=recv_sems.at[k],
                                               device_id=dev, device_id_type=MESH)
                  for k, (s, d, dev) in enumerate(sends)]
        for cp in remote:
            cp.start()
        for k, r in enumerate(recvs):
            pltpu.make_async_remote_copy(src_ref=r, dst_ref=r, send_sem=send_sems.at[k], recv_sem=recv_sems.at[k],
                                         device_id=(x, y, c), device_id_type=MESH).wait_recv()
        for cp in remote:
            cp.wait_send()
        for cp in local:
            cp.wait()

    n_sends, n_local = plan.n_sends, max(plan.n_local, 1)
    return pl.pallas_call(
        body, out_shape=out_shapes, in_specs=[ANY] * n_in, out_specs=[ANY] * n_out,
        scratch_shapes=[pltpu.SemaphoreType.DMA((n_sends,)), pltpu.SemaphoreType.DMA((n_sends,)),
                        pltpu.SemaphoreType.DMA((n_local,))],
        input_output_aliases=aliases, name=name,
    )(*operands)


def _plan(n_local, n_sends):
    def wrap(fn):
        fn.n_local, fn.n_sends = n_local, n_sends
        return fn
    return wrap


def _half(ref, axis, c):
    rows = ref.shape[axis] // 2
    idx = [slice(None)] * len(ref.shape)
    idx[axis] = pl.ds(pl.multiple_of(c * rows, 16), rows)
    return ref.at[tuple(idx)]


HBM = pl.BlockSpec(memory_space=pltpu.HBM)
SEM = pl.BlockSpec(memory_space=pltpu.SEMAPHORE)
EFFECT = pltpu.SideEffectType.DATAFLOW_SIDE_EFFECTING


def split_start(name, bufs, extra, n_copies, describe):
    n = len(bufs)

    def body(*refs):
        send_sems, recv_sems = refs[n + len(extra)], refs[n + len(extra) + 1]
        x, y, c = _position()
        for k, (src, dst, _, peer) in enumerate(describe(refs[:n], x, y, c)):
            pltpu.make_async_remote_copy(src_ref=src, dst_ref=dst, send_sem=send_sems.at[k], recv_sem=recv_sems.at[k],
                                         device_id=peer, device_id_type=MESH).start()
        token = refs[-1]
        token[...] = jnp.zeros_like(token)

    out = pl.pallas_call(
        body, name=name,
        out_shape=(pltpu.SemaphoreType.DMA((n_copies,)), pltpu.SemaphoreType.DMA((n_copies,)),
                   *[pltpu.HBM(b.shape, b.dtype) for b in bufs], jax.ShapeDtypeStruct((8, 128), F32)),
        in_specs=[HBM] * n + [ANY] * len(extra),
        out_specs=(SEM, SEM, *[HBM] * n, pl.BlockSpec(memory_space=pltpu.VMEM)),
        input_output_aliases={k: 2 + k for k in range(n)},
        compiler_params=pltpu.CompilerParams(has_side_effects=EFFECT),
    )(*[pltpu.with_memory_space_constraint(b, pltpu.HBM) for b in bufs], *extra)
    return out[0], out[1], list(out[2:2 + n]), out[-1]


def split_wait(name, bufs, send_sems, recv_sems, after, describe):
    n = len(bufs)

    def body(*refs):
        send, recv = refs[n], refs[n + 1]
        x, y, c = _position()
        for k, (src, _, dst, peer) in enumerate(describe(refs[:n], x, y, c)):
            copy = pltpu.make_async_remote_copy(src_ref=src, dst_ref=dst, send_sem=send.at[k], recv_sem=recv.at[k],
                                                device_id=peer, device_id_type=MESH)
            copy.wait_send()
            copy.wait_recv()

    out = pl.pallas_call(
        body, name=name, out_shape=[pltpu.HBM(b.shape, b.dtype) for b in bufs],
        in_specs=[HBM] * n + [SEM, SEM] + [ANY] * len(after), out_specs=[HBM] * n,
        input_output_aliases={k: k for k in range(n)},
        compiler_params=pltpu.CompilerParams(has_side_effects=EFFECT),
    )(*bufs, send_sems, recv_sems, *after)
    return list(out)


def _row_tile(rows, cols, max_elements=256 * 1024):
    best = 16
    for t in range(16, rows + 1, 16):
        if rows % t == 0 and t * cols <= max_elements:
            best = t
    return best


def cast_into_slots(name, shards, chip):
    n = len(shards)
    rows, cols = shards[0][0].shape[-2:]
    tr = _row_tile(rows, cols)

    def body(chip_ref, *refs):
        del chip_ref
        for k in range(n):
            refs[n + k][...] = refs[k][...].astype(BF16)

    def in_spec(param, index):
        return pl.BlockSpec((None,) * len(index) + (tr, cols), lambda r, chip_ref: index + (r, 0))

    return pl.pallas_call(
        body, out_shape=[jax.ShapeDtypeStruct((N_CHIPS, rows, cols), BF16)] * n,
        grid_spec=pltpu.PrefetchScalarGridSpec(
            num_scalar_prefetch=1, grid=(rows // tr,),
            in_specs=[in_spec(p, idx) for p, idx in shards],
            out_specs=[pl.BlockSpec((None, tr, cols), lambda r, chip_ref: (chip_ref[0], r, 0))] * n),
        name=name, compiler_params=_params(("parallel",)),
    )(chip, *[p for p, _ in shards])


GATHER_STAGES = ((0,), (1,), (2,), (3, 4, 5))
REDUCE_STAGES = ((5, 4, 3), (2,), (1,), (0,))


def _gather_copies(slots, x, y, c):
    me = 2 * x + y
    out = []
    for s in slots:
        for fx, fy in CHIP_FLIPS:
            mine = _half(s.at[me], 0, c)
            out.append((mine, mine, _half(s.at[2 * (x ^ fx) + (y ^ fy)], 0, c), (x ^ fx, y ^ fy, c)))
    return out


class WeightStream:
    def __init__(self, shards, chip, after=()):
        self.pending, self.ready = {}, {}
        token = tuple(after)
        for si, units in enumerate(GATHER_STAGES):
            slots = []
            for u in units:
                same = len({p.shape[-2:] for p, _ in shards[u]}) == 1
                for t, group in enumerate([shards[u]] if same else [[s] for s in shards[u]]):
                    slots += cast_into_slots(f"cast_{u}_{t}", group, chip)
            send, recv, slots, tok = split_start(f"gather_start_{si}", slots, token, 3 * len(slots), _gather_copies)
            self.pending[si] = (send, recv, slots)
            token = (tok,)
        self.token = token

    def get(self, u, after):
        if u not in self.ready:
            si = next(k for k, units in enumerate(GATHER_STAGES) if u in units)
            send, recv, slots = self.pending.pop(si)
            slots = split_wait(f"gather_wait_{si}", slots, send, recv, (after,) + self.token, _gather_copies)
            self.token = ()

            @_plan(0, 3 * len(slots))
            def to_sibling(ins, outs, x, y, c):
                sends, recvs = [], []
                for o in outs:
                    for fx, fy in CHIP_FLIPS:
                        slab = o.at[2 * (x ^ fx) + (y ^ fy)]
                        sends.append((_half(slab, 0, c), _half(slab, 0, c), (x, y, 1 - c)))
                        recvs.append(_half(slab, 0, 1 - c))
                return [], sends, recvs

            shapes = [jax.ShapeDtypeStruct(s.shape, BF16) for s in slots]
            slots = exchange(f"gather_sibling_{si}", slots, shapes, {k: k for k in range(len(slots))}, to_sibling)
            for i, v in enumerate(GATHER_STAGES[si]):
                self.ready[v] = tuple(slots[3 * i:3 * i + 3])
        return self.ready[u]


def _reduce_copies(bufs, x, y, c):
    n = len(bufs) // 2
    out = []
    for s, land in zip(bufs[:n], bufs[n:]):
        for k, (fx, fy) in enumerate(CHIP_FLIPS):
            out.append((s.at[2 * (x ^ fx) + (y ^ fy)], land.at[k], land.at[k], (x ^ fx, y ^ fy, c)))
    return out


GRAD_SLOTS = {"gate": (2 * DEPTH, FF_SHARD, D_MODEL), "up": (2 * DEPTH, FF_SHARD, D_MODEL),
              "down": (2 * DEPTH, FF_SHARD, D_MODEL), "in": (DEPTH, D_MODEL, IN_SHARD),
              "br": (DEPTH, BR_ROWS, D_SHARD), "out": (DEPTH, D_SHARD, D_MODEL)}


def _unit_tensors(u):
    l, j = divmod(u, 3)
    if j == 1:
        return [("in", l), ("br", l), ("out", l)]
    return [(k, 2 * l + j // 2) for k in ("gate", "up", "down")]


class GradStream:
    def __init__(self, chip, core):
        self.core = core
        self.place = jnp.concatenate([chip, core])
        self.held, self.flying = {}, []
        self.full = {k: lax.empty(shape, F32) for k, shape in GRAD_SLOTS.items()}

    def put(self, u, grads, after=()):
        self.held[u] = grads
        si = len(self.flying)
        units = REDUCE_STAGES[si]
        if not all(v in self.held for v in units):
            return jnp.zeros((1, 1), F32)
        gs = [g for v in units for g in self.held[v]]

        @_plan(0, len(gs))
        def swap_halves(ins, outs, x, y, c):
            sends = [(_half(g, 1, 1 - c), o, (x, y, 1 - c)) for g, o in zip(ins, outs)]
            return [], sends, list(outs)

        half_shapes = [jax.ShapeDtypeStruct((N_CHIPS, g.shape[1] // 2, g.shape[2]), BF16) for g in gs]
        landed = exchange(f"reduce_swap_{si}", gs + list(after), half_shapes, {}, swap_halves)
        sums = [None] * len(gs)
        for run in _same_shape_runs(gs):
            for k, s in zip(run, _add_halves([gs[k] for k in run], [landed[k] for k in run], self.core)):
                sums[k] = s
        landing = [lax.empty((3,) + s.shape[1:], BF16) for s in sums]
        send, recv, bufs, token = split_start(f"reduce_start_{si}", sums + landing, (), 3 * len(sums), _reduce_copies)
        self.flying.append((send, recv, bufs, [t for v in units for t in _unit_tensors(v)]))
        return token[0:1, 0:1]

    def finish(self, after):
        for si, (send, recv, bufs, tensors) in enumerate(self.flying):
            bufs = split_wait(f"reduce_wait_{si}", bufs, send, recv, tuple(after), _reduce_copies)
            n = len(tensors)
            for run in _same_shape_runs(bufs[:n]):
                names = [tensors[k][0] for k in run]
                out = _add_chips([bufs[k] for k in run], [bufs[n + k] for k in run], self.place,
                                 [self.full[t] for t in names], [tensors[k][1] for k in run])
                self.full.update(zip(names, out))
        names = list(self.full)

        @_plan(0, len(names))
        def share_halves(ins, outs, x, y, c):
            sends = [(_half(o, 1, c), _half(o, 1, c), (x, y, 1 - c)) for o in outs]
            return [], sends, [_half(o, 1, 1 - c) for o in outs]

        shapes = [jax.ShapeDtypeStruct(self.full[k].shape, F32) for k in names]
        out = exchange("reduce_share_halves", [self.full[k] for k in names], shapes, {k: k for k in range(len(names))},
                       share_halves)
        return dict(zip(names, out))


def _same_shape_runs(arrays, longest=3):
    runs = []
    for k, a in enumerate(arrays):
        if runs and len(runs[-1]) < longest and arrays[runs[-1][0]].shape == a.shape:
            runs[-1].append(k)
        else:
            runs.append([k])
    return runs


def _add_halves(gs, landeds, core):
    n = len(gs)
    _, rh, cols = landeds[0].shape
    tr = _row_tile(rh, cols, 1024 * 1024)
    per_half = rh // tr

    def body(core_ref, *refs):
        del core_ref
        for k in range(n):
            refs[2 * n + k][...] = (refs[k][...].astype(F32) + refs[n + k][...].astype(F32)).astype(BF16)

    blk = (None, tr, cols)
    landed_spec = pl.BlockSpec(blk, lambda j, r, core_ref: (j, r, 0))
    return pl.pallas_call(
        body, out_shape=[jax.ShapeDtypeStruct(landeds[0].shape, BF16)] * n,
        grid_spec=pltpu.PrefetchScalarGridSpec(
            num_scalar_prefetch=1, grid=(N_CHIPS, per_half),
            in_specs=[pl.BlockSpec(blk, lambda j, r, core_ref: (j, core_ref[0] * per_half + r, 0))] * n
            + [landed_spec] * n,
            out_specs=[landed_spec] * n),
        name="reduce_add_halves", compiler_params=_params(("parallel", "parallel")),
    )(core, *gs, *landeds)


def _add_chips(sums, landeds, place, fulls, slots):
    n = len(sums)
    _, rh, cols = sums[0].shape
    tr = _row_tile(rh, cols, 1024 * 1024)
    per_half = rh // tr

    def body(place_ref, *refs):
        del place_ref
        for k in range(n):
            s_ref, la_ref, o_ref = refs[k], refs[n + k], refs[3 * n + k]
            o_ref[...] = ((s_ref[...].astype(F32) + la_ref[0].astype(F32)) + la_ref[1].astype(F32)) + la_ref[2].astype(F32)

    def out_spec(slot):
        return pl.BlockSpec((None, tr, cols), lambda r, place_ref: (slot, place_ref[1] * per_half + r, 0))

    return pl.pallas_call(
        body, out_shape=[jax.ShapeDtypeStruct(f.shape, F32) for f in fulls],
        grid_spec=pltpu.PrefetchScalarGridSpec(
            num_scalar_prefetch=1, grid=(per_half,),
            in_specs=[pl.BlockSpec((None, tr, cols), lambda r, place_ref: (place_ref[0], r, 0))] * n
            + [pl.BlockSpec((3, tr, cols), lambda r, place_ref: (0, r, 0))] * n + [ANY] * n,
            out_specs=[out_spec(slot) for slot in slots]),
        input_output_aliases={1 + 2 * n + k: k for k in range(n)}, name="reduce_add_chips",
        compiler_params=_params(("parallel",)),
    )(place, *sums, *landeds, *fulls)


def sum_devices(parts):
    def body(p_ref, o_ref):
        acc = p_ref[0]
        for d in range(1, N_DEV):
            acc = acc + p_ref[d]
        o_ref[...] = acc

    return pl.pallas_call(body, out_shape=jax.ShapeDtypeStruct(parts.shape[1:], F32), name="sum_devices")(parts)


ADA_SHARD = 9 * D_MODEL // N_CHIPS
ADA_TILE = 768
ADA_ROWS = 16


def ada_forward(c_rows, w_ada, b_shard):
    def body(c_ref, w_ref, b_ref, o_ref):
        cv = c_ref[...]
        o_ref[...] = _dot((cv * _sigmoid(cv)).astype(BF16), w_ref[...].astype(BF16), NN) + b_ref[...]

    return pl.pallas_call(
        body, grid=(DEPTH, ADA_SHARD // ADA_TILE),
        in_specs=[pl.BlockSpec((ADA_ROWS, D_MODEL), lambda l, n: (0, 0)),
                  pl.BlockSpec((None, D_MODEL, ADA_TILE), lambda l, n: (l, 0, n)),
                  pl.BlockSpec((None, 1, ADA_TILE), lambda l, n: (l, 0, n))],
        out_specs=pl.BlockSpec((None, ADA_ROWS, ADA_TILE), lambda l, n: (l, 0, n)),
        out_shape=jax.ShapeDtypeStruct((DEPTH, ADA_ROWS, ADA_SHARD), F32),
        name="ada_forward", compiler_params=_params(("parallel", "parallel")),
    )(c_rows, w_ada, b_shard)


def ada_backward(c_rows, dmod_rows):
    def body(c_ref, d_ref, o_ref):
        cv = c_ref[...]
        o_ref[...] = _dot((cv * _sigmoid(cv)).astype(BF16), d_ref[...].astype(BF16), TN)

    return pl.pallas_call(
        body, grid=(DEPTH, ADA_SHARD // ADA_TILE),
        in_specs=[pl.BlockSpec((ADA_ROWS, D_MODEL), lambda l, n: (0, 0)),
                  pl.BlockSpec((None, ADA_ROWS, ADA_TILE), lambda l, n: (l, 0, n))],
        out_specs=pl.BlockSpec((None, D_MODEL, ADA_TILE), lambda l, n: (l, 0, n)),
        out_shape=jax.ShapeDtypeStruct((DEPTH, D_MODEL, ADA_SHARD), F32),
        name="ada_backward", compiler_params=_params(("parallel", "parallel")),
    )(c_rows, dmod_rows)


def adamw(name, w, g, m, v):
    shape = w.shape
    cols = shape[-1]
    rows = w.size // cols
    tr = _row_tile(rows, cols) if rows % 16 == 0 else rows
    c1 = 1.0 / (1.0 - ADAM_B1 ** ADAM_STEP)
    c2 = 1.0 / (1.0 - ADAM_B2 ** ADAM_STEP)

    def body(w_ref, g_ref, m_ref, v_ref, go_ref, d_ref, mo_ref, vo_ref):
        gv = g_ref[...]
        mn = ADAM_B1 * m_ref[...] + (1.0 - ADAM_B1) * gv
        vn = ADAM_B2 * v_ref[...] + (1.0 - ADAM_B2) * (gv * gv)
        go_ref[...] = gv
        mo_ref[...] = mn
        vo_ref[...] = vn
        d_ref[...] = -ADAM_LR * ((mn * c1) / (jnp.sqrt(vn * c2) + ADAM_EPS) + ADAM_WD * w_ref[...])

    spec = pl.BlockSpec((tr, cols), lambda i: (i, 0))
    out = jax.ShapeDtypeStruct((rows, cols), F32)
    res = pl.pallas_call(
        body, grid=(rows // tr,), in_specs=[spec] * 4, out_specs=[spec] * 4, out_shape=[out] * 4,
        name=name, compiler_params=_params(("parallel",)),
    )(*[t.reshape(rows, cols) for t in (w, g, m, v)])
    return tuple(r.reshape(shape) for r in res)


def _pack(parts, rows):
    flat = jnp.concatenate([p.reshape(-1) for p in parts])
    return jnp.pad(flat, (0, rows * 128 - flat.size)).reshape(rows, 128)


def _unpack(flat, shapes):
    out, at = [], 0
    for s in shapes:
        n = math.prod(s)
        out.append(flat[at:at + n].reshape(s))
        at += n
    return out


def kernel(x, c, w_ada, b_ada, norm_gain, w_ffn_gate, w_ffn_up, w_ffn_down, w_in, w_br_sb, w_br_dil, w_br_swa, w_out, sinks, rel_bias, final_gain, loss_target, m_w_ada, m_b_ada, m_norm_gain, m_w_ffn_gate, m_w_ffn_up, m_w_ffn_down, m_w_in, m_w_br_sb, m_w_br_dil, m_w_br_swa, m_w_out, m_sinks, m_rel_bias, m_final_gain, v_w_ada, v_b_ada, v_norm_gain, v_w_ffn_gate, v_w_ffn_up, v_w_ffn_down, v_w_in, v_w_br_sb, v_w_br_dil, v_w_br_swa, v_w_out, v_sinks, v_rel_bias, v_final_gain):
    xi, yi, ci = _position()
    chip = 2 * xi + yi
    dev = 2 * chip + ci

    c_all = all_gather_small("gather_c", c.reshape(8, 128)).reshape(N_DEV, D_MODEL)
    c_rows = jnp.pad(c_all, ((0, ADA_ROWS - N_DEV), (0, 0)))
    b_shard = lax.dynamic_slice_in_dim(b_ada, chip * ADA_SHARD, ADA_SHARD, axis=1).reshape(DEPTH, 1, ADA_SHARD)
    mod_shard = ada_forward(c_rows, w_ada, b_shard)[:, :N_DEV]
    n_mod = DEPTH * N_DEV * ADA_SHARD
    gathered = all_gather_small("gather_mod", _pack([mod_shard, norm_gain], 304))[::2].reshape(N_CHIPS, -1)
    mod_all = gathered[:, :n_mod].reshape(N_CHIPS, DEPTH, N_DEV, ADA_SHARD)
    mod = lax.dynamic_index_in_dim(mod_all, dev, axis=2, keepdims=False)
    mod = mod.transpose(1, 0, 2).reshape(DEPTH, 3, 3, D_MODEL)
    gains = gathered[:, n_mod:n_mod + DEPTH * 3 * D_SHARD].reshape(N_CHIPS, DEPTH, 3, D_SHARD)
    gains = gains.transpose(1, 2, 0, 3).reshape(DEPTH, 3, D_MODEL)

    chip_i, core_i = chip.astype(jnp.int32).reshape(1), ci.astype(jnp.int32).reshape(1)
    w_br = jnp.concatenate([w_br_sb, w_br_dil, w_br_swa], axis=1)
    transposed = (3, 4)
    w_gate_t, w_up_t = jnp.swapaxes(w_ffn_gate, 2, 3), jnp.swapaxes(w_ffn_up, 2, 3)
    shards = []
    for l in range(DEPTH):
        ffn = [[(w_gate_t, (l, f)), (w_up_t, (l, f)), (w_ffn_down, (l, f))] for f in range(2)]
        shards += [ffn[0], [(w_in, (l,)), (w_br, (l,)), (w_out, (l,))], ffn[1]]
    weights_in = WeightStream(shards, chip_i, (gathered,))
    grads_out = GradStream(chip_i, core_i)

    loss, dx, dmod, dgains, dfinal, dsinks, drel, last_grads = device_step(
        x[0], loss_target[0], mod, gains, final_gain, sinks, rel_bias, weights_in.get, grads_out.put)

    small_shapes = [(DEPTH, 9 * D_MODEL), (DEPTH, 3, D_MODEL), (D_MODEL,), (DEPTH, H_SWA_Q), (N_BUCKETS, 12), (1,)]
    small_all = all_gather_small("gather_small_grads", _pack([dmod, dgains, dfinal, dsinks, drel, loss[0, 0:1]], 208))
    started = grads_out.put(0, last_grads, after=(small_all,))
    small_all = small_all + started
    g_b_ada, g_gain_full, g_final, g_sinks, g_rel, loss_sum = _unpack(sum_devices(small_all).reshape(-1), small_shapes)
    g_gain = lax.dynamic_slice_in_dim(g_gain_full, chip * D_SHARD, D_SHARD, axis=2)
    dmod_all = small_all.reshape(N_DEV, -1)[:, :DEPTH * 9 * D_MODEL].reshape(N_DEV, DEPTH, 9 * D_MODEL)
    dmod_rows = lax.dynamic_slice_in_dim(dmod_all, chip * ADA_SHARD, ADA_SHARD, axis=2).transpose(1, 0, 2)
    g_w_ada = ada_backward(c_rows, jnp.pad(dmod_rows, ((0, 0), (0, ADA_ROWS - N_DEV), (0, 0))))

    weights = [w_ada, b_ada, norm_gain, w_ffn_gate, w_ffn_up, w_ffn_down, w_in, w_br_sb, w_br_dil, w_br_swa, w_out,
               sinks, rel_bias, final_gain]
    ms = [m_w_ada, m_b_ada, m_norm_gain, m_w_ffn_gate, m_w_ffn_up, m_w_ffn_down, m_w_in, m_w_br_sb, m_w_br_dil,
          m_w_br_swa, m_w_out, m_sinks, m_rel_bias, m_final_gain]
    vs = [v_w_ada, v_b_ada, v_norm_gain, v_w_ffn_gate, v_w_ffn_up, v_w_ffn_down, v_w_in, v_w_br_sb, v_w_br_dil,
          v_w_br_swa, v_w_out, v_sinks, v_rel_bias, v_final_gain]
    grads = [g_w_ada, g_b_ada, g_gain] + [None] * 8 + [g_sinks, g_rel, g_final]

    deltas, new_ms, new_vs = [None] * 14, [None] * 14, [None] * 14
    for k in (0, 1, 2, 11, 12, 13):
        _, deltas[k], new_ms[k], new_vs[k] = adamw(f"adamw_{k}", weights[k], grads[k], ms[k], vs[k])

    g = grads_out.finish((dx, deltas[0], deltas[1]))
    g_br = g["br"]
    grads[3:11] = [g["gate"].reshape(w_gate_t.shape), g["up"].reshape(w_up_t.shape),
                   g["down"].reshape(w_ffn_down.shape), g["in"], g_br[:, 0:256], g_br[:, 256:384], g_br[:, 384:768],
                   g["out"]]
    for k in range(3, 11):
        state = [weights[k], ms[k], vs[k]]
        if k in transposed:
            state = [jnp.swapaxes(t, 2, 3) for t in state]
        out = adamw(f"adamw_{k}", state[0], grads[k], state[1], state[2])
        if k in transposed:
            out = [jnp.swapaxes(t, 2, 3) for t in out]
        grads[k], deltas[k], new_ms[k], new_vs[k] = out
    return (loss_sum[0], dx[None], *grads, *deltas, *new_ms, *new_vs)
```

```python
import functools
import math

import jax
import jax.numpy as jnp
from jax import lax
from jax.experimental import pallas as pl
from jax.experimental.pallas import tpu as pltpu

F32 = jnp.float32
BF16 = jnp.bfloat16

D_MODEL = 1024
SEQ = 2048
DEPTH = 2
HEAD_DIM = 64
BLK = 128
H_SB = 4
DIL_PATTERNS = ((128, 1), (512, 4), (2048, 16))
H_PER_DIL = 2
H_DIL = 6
H_SWA_Q = 6
H_SWA_KV = 2
SWA_WINDOW = 128
N_BUCKETS = 32
MAX_REL_DIST = 2048
D_FF = 2816
RMS_EPS = 1e-6
N_CHIPS = 4
N_DEV = 8
FF_SHARD = D_FF // N_CHIPS
D_QKV = 2560
D_IN = D_QKV + 3 * D_MODEL
IN_SHARD = D_IN // N_CHIPS
D_SHARD = D_MODEL // N_CHIPS
BR_ROWS = 768
NEG = -1e30
QK_SCALE = HEAD_DIM ** -0.5

ADAM_LR = 0.001
ADAM_B1 = 0.9
ADAM_B2 = 0.999
ADAM_EPS = 1e-08
ADAM_WD = 0.01
ADAM_STEP = 10

VMEM_LIMIT = 56 * 1024 * 1024
ROW_TILE = 256
MM_TILE = 1024

NN = (((1,), (0,)), ((), ()))
NT = (((1,), (1,)), ((), ()))
TN = (((0,), (0,)), ((), ()))


def _params(sem=None):
    return pltpu.CompilerParams(dimension_semantics=sem, vmem_limit_bytes=VMEM_LIMIT)


def _dot(a, b, dims):
    return lax.dot_general(a, b, dims, preferred_element_type=F32)


def _sigmoid(x):
    return 1.0 / (1.0 + jnp.exp(-x))


def _matmul(name, grid, nk, k_axis, dims, n_pairs, in_specs, out_specs, out_shape, acc_shape, epilogue,
            operands, sem, aliases=None, prologue=None):
    n_in = len(in_specs)
    n_out = len(out_specs)

    def partial(ins):
        tot = None
        for p in range(n_pairs):
            a = ins[2 * p][...]
            if prologue is not None:
                a = prologue(p, a, ins)
            d = _dot(a, ins[2 * p + 1][...], dims)
            tot = d if tot is None else tot + d
        return tot

    def body(*refs):
        ins, outs = refs[:n_in], refs[n_in:n_in + n_out]
        ids = tuple(pl.program_id(a) for a in range(len(grid)))
        if nk == 1:
            epilogue(partial(ins), ins, outs, ids)
            return
        acc = refs[n_in + n_out]
        k = ids[k_axis]

        @pl.when(k == 0)
        def _():
            acc[...] = partial(ins)

        @pl.when(k > 0)
        def _():
            acc[...] += partial(ins)

        @pl.when(k == nk - 1)
        def _():
            epilogue(acc[...], ins, outs, ids)

    return pl.pallas_call(
        body, grid=grid, in_specs=in_specs, out_specs=out_specs, out_shape=out_shape,
        scratch_shapes=[] if nk == 1 else [pltpu.VMEM(acc_shape, F32)],
        input_output_aliases=aliases or {}, name=name, compiler_params=_params(sem),
    )(*operands)


def _row_spec(width=D_MODEL):
    return pl.BlockSpec((ROW_TILE, width), lambda i: (i, 0))


def _vec_spec(rows=1, width=D_MODEL):
    return pl.BlockSpec((rows, width), lambda i: (0, 0))


class Row:
    def __init__(self, table, index):
        self.table, self.index = table, index

    def spec(self):
        index = self.index
        return pl.BlockSpec((None, 1, D_MODEL), lambda *ids: (index, 0, 0))


def _slot_spec(u):
    return pl.BlockSpec((8, D_MODEL), lambda *ids: (u, 0))


def prenorm(x, gain, scale, shift):
    def body(x_ref, g_ref, sc_ref, sh_ref, h_ref):
        xv = x_ref[...]
        r = lax.rsqrt(jnp.mean(xv * xv, axis=-1, keepdims=True) + RMS_EPS)
        h_ref[...] = (((xv * r) * g_ref[...]) * (1.0 + sc_ref[...]) + sh_ref[...]).astype(BF16)

    return pl.pallas_call(
        body, grid=(SEQ // ROW_TILE,), in_specs=[_row_spec(), gain.spec(), scale.spec(), shift.spec()],
        out_specs=_row_spec(), out_shape=jax.ShapeDtypeStruct((SEQ, D_MODEL), BF16),
        name="prenorm", compiler_params=_params(("parallel",)),
    )(x, gain.table, scale.table, shift.table)


def resid_bwd(dxo, f, coef, mult, sums, u):
    def body(dx_ref, f_ref, c_ref, sums_in, df_ref, dc_ref):
        del sums_in
        dx = dx_ref[...]
        df_ref[...] = (dx * (mult * c_ref[...])).astype(BF16)
        part = mult * jnp.sum(dx * f_ref[...], axis=0, keepdims=True)

        @pl.when(pl.program_id(0) == 0)
        def _():
            dc_ref[...] = jnp.zeros_like(dc_ref)

        dc_ref[0:1, :] += part

    return pl.pallas_call(
        body, grid=(SEQ // ROW_TILE,),
        in_specs=[_row_spec(), _row_spec(), coef.spec(), pl.BlockSpec(memory_space=pl.ANY)],
        out_specs=[_row_spec(), _slot_spec(u)],
        out_shape=[jax.ShapeDtypeStruct((SEQ, D_MODEL), BF16), jax.ShapeDtypeStruct(sums.shape, F32)],
        input_output_aliases={3: 1}, name="resid_bwd", compiler_params=_params(("arbitrary",)),
    )(dxo, f, coef.table, sums)


def final_loss(x, gain, target):
    def body(x_ref, g_ref, t_ref, loss_ref, dx_ref, dg_ref):
        xv = x_ref[...]
        g = g_ref[...]
        r = lax.rsqrt(jnp.mean(xv * xv, axis=-1, keepdims=True) + RMS_EPS)
        xh = xv * r
        e = xh * g - t_ref[...]
        part = 0.5 * jnp.sum(jnp.mean(e * e, axis=-1, keepdims=True), axis=0, keepdims=True)
        dy = e * (1.0 / D_MODEL)
        dyg = dy * g
        dx_ref[...] = r * (dyg - xh * jnp.mean(dyg * xh, axis=-1, keepdims=True))

        @pl.when(pl.program_id(0) == 0)
        def _():
            loss_ref[...] = jnp.zeros_like(loss_ref)
            dg_ref[...] = jnp.zeros_like(dg_ref)

        loss_ref[...] += jnp.broadcast_to(part, loss_ref.shape)
        dg_ref[0:1, :] += jnp.sum(dy * xh, axis=0, keepdims=True)

    return pl.pallas_call(
        body, grid=(SEQ // ROW_TILE,), in_specs=[_row_spec(), _vec_spec(), _row_spec()],
        out_specs=[_vec_spec(8, 128), _row_spec(), _vec_spec(8)],
        out_shape=[jax.ShapeDtypeStruct((8, 128), F32), jax.ShapeDtypeStruct((SEQ, D_MODEL), F32),
                   jax.ShapeDtypeStruct((8, D_MODEL), F32)],
        name="final_loss", compiler_params=_params(("arbitrary",)),
    )(x, gain, target)


def _prenorm_bwd_epilogue(dh, x_ref, dxo_ref, g_ref, sc_ref, dx_ref, stats_ref, first):
    xv = x_ref[...]
    g = g_ref[...]
    r = lax.rsqrt(jnp.mean(xv * xv, axis=-1, keepdims=True) + RMS_EPS)
    xh = xv * r
    dn = dh * (1.0 + sc_ref[...])
    dxh = dn * g
    dx = dxo_ref[...] + r * (dxh - xh * jnp.mean(dxh * xh, axis=-1, keepdims=True))
    dx_ref[...] = dx

    @pl.when(first)
    def _():
        stats_ref[...] = jnp.zeros_like(stats_ref)

    stats_ref[0:1, :] += jnp.sum(dh, axis=0, keepdims=True)
    stats_ref[1:2, :] += jnp.sum(dh * (xh * g), axis=0, keepdims=True)
    stats_ref[2:3, :] += jnp.sum(dn * xh, axis=0, keepdims=True)
    return dx


def _resid_bwd_epilogue(dx, f_ref, c_ref, mult, df_ref, dc_ref, first):
    df_ref[...] = (dx * (mult * c_ref[...])).astype(BF16)

    @pl.when(first)
    def _():
        dc_ref[...] = jnp.zeros_like(dc_ref)

    dc_ref[0:1, :] += mult * jnp.sum(dx * f_ref[...], axis=0, keepdims=True)


def ffn_up(h, wg_all, wu_all):
    def body(h_ref, wg_ref, wu_ref, a_ref, b_ref, s_ref):
        hv = h_ref[...]
        a = _dot(hv, wg_ref[...], NT)
        b = _dot(hv, wu_ref[...], NT)
        a_ref[...] = a.astype(BF16)
        b_ref[...] = b.astype(BF16)
        s_ref[...] = (a * _sigmoid(a) * b).astype(BF16)

    w_spec = pl.BlockSpec((None, FF_SHARD, D_MODEL), lambda j, i: (j, 0, 0))
    o_spec = pl.BlockSpec((None, MM_TILE, FF_SHARD), lambda j, i: (j, i, 0))
    hid = (N_CHIPS, SEQ, FF_SHARD)
    return pl.pallas_call(
        body, grid=(N_CHIPS, SEQ // MM_TILE),
        in_specs=[pl.BlockSpec((MM_TILE, D_MODEL), lambda j, i: (i, 0)), w_spec, w_spec],
        out_specs=[o_spec, o_spec, o_spec],
        out_shape=[jax.ShapeDtypeStruct(hid, BF16)] * 3,
        name="ffn_up", compiler_params=_params(("parallel", "parallel")),
    )(h, wg_all, wu_all)


def matmul_residual(name, a, a_spec, take, w_all, x, coef, mult, then=None):
    n_rows = 0 if then is None else 3

    def body(a_ref, w_hbm, x_ref, c_ref, *rest):
        rows, (f_ref, xo_ref), w_vmem, sem = rest[:n_rows], rest[n_rows:n_rows + 2], rest[-2], rest[-1]
        h_ref = rest[n_rows + 2] if then is not None else None

        @pl.when(pl.program_id(0) == 0)
        def _():
            copy = pltpu.make_async_copy(w_hbm, w_vmem, sem)
            copy.start()
            copy.wait()

        acc = None
        for j in range(N_CHIPS):
            part = _dot(take(a_ref, j), w_vmem[j], NN)
            acc = part if acc is None else acc + part
        f_ref[...] = acc
        xo = x_ref[...] + (mult * c_ref[...]) * acc
        xo_ref[...] = xo
        if then is not None:
            r = lax.rsqrt(jnp.mean(xo * xo, axis=-1, keepdims=True) + RMS_EPS)
            h_ref[...] = (((xo * r) * rows[0][...]) * (1.0 + rows[1][...]) + rows[2][...]).astype(BF16)

    row = pl.BlockSpec((MM_TILE, D_MODEL), lambda i: (i, 0))
    f32 = jax.ShapeDtypeStruct((SEQ, D_MODEL), F32)
    extra = [] if then is None else [then.gain, then.scale, then.shift]
    return pl.pallas_call(
        body, grid=(SEQ // MM_TILE,),
        in_specs=[a_spec, pl.BlockSpec(memory_space=pl.ANY), row, coef.spec()] + [t.spec() for t in extra],
        out_specs=[row] * (2 + bool(extra)),
        out_shape=[f32, f32] + [jax.ShapeDtypeStruct((SEQ, D_MODEL), BF16)] * bool(extra),
        scratch_shapes=[pltpu.VMEM(w_all.shape, w_all.dtype), pltpu.SemaphoreType.DMA],
        name=name, compiler_params=_params(("arbitrary",)),
    )(a, w_all, x, coef.table, *[t.table for t in extra])


def ffn_down(s, wd_all, x, gate, then):
    return matmul_residual(
        "ffn_down", s, pl.BlockSpec((N_CHIPS, MM_TILE, FF_SHARD), lambda i: (0, i, 0)), lambda ref, j: ref[j],
        wd_all, x, gate, 0.5, then)


def ffn_bwd_hidden(df, wd_all, a, b):
    def epilogue(ds, ins, outs, ids):
        av, bv = ins[2][...].astype(F32), ins[3][...].astype(F32)
        sig = _sigmoid(av)
        outs[0][...] = (ds * bv * (sig * (1.0 + av * (1.0 - sig)))).astype(BF16)
        outs[1][...] = (ds * (av * sig)).astype(BF16)

    hid_spec = pl.BlockSpec((None, MM_TILE, FF_SHARD), lambda j, i: (j, i, 0))
    hid = jax.ShapeDtypeStruct((N_CHIPS, SEQ, FF_SHARD), BF16)
    return _matmul(
        "ffn_bwd_hidden", (N_CHIPS, SEQ // MM_TILE), 1, None, NT, 1,
        [pl.BlockSpec((MM_TILE, D_MODEL), lambda j, i: (i, 0)),
         pl.BlockSpec((None, FF_SHARD, D_MODEL), lambda j, i: (j, 0, 0)), hid_spec, hid_spec],
        [hid_spec, hid_spec], [hid, hid], None, epilogue, (df, wd_all, a, b), ("parallel", "parallel"))


def grad_weight(name, lhs, lhs_spec, rhs, rhs_spec, shape):
    def epilogue(acc, ins, outs, ids):
        outs[0][...] = acc.astype(BF16)

    return _matmul(
        name, (N_CHIPS, SEQ // MM_TILE), SEQ // MM_TILE, 1, TN, 1,
        [lhs_spec, rhs_spec], [pl.BlockSpec((None,) + shape, lambda j, k: (j, 0, 0))],
        [jax.ShapeDtypeStruct((N_CHIPS,) + shape, BF16)], shape, epilogue, (lhs, rhs), ("parallel", "arbitrary"))[0]


def ffn_grad_weights(h, s, df, da, db):
    tok = pl.BlockSpec((MM_TILE, D_MODEL), lambda j, k: (k, 0))
    hid = pl.BlockSpec((None, MM_TILE, FF_SHARD), lambda j, k: (j, k, 0))
    n_k = SEQ // MM_TILE

    def body(da_ref, db_ref, s_ref, h_ref, df_ref, gg_ref, gu_ref, gd_ref, acc_g, acc_u, acc_d):
        k = pl.program_id(1)
        hv = h_ref[...]
        parts = (_dot(da_ref[...], hv, TN), _dot(db_ref[...], hv, TN), _dot(s_ref[...], df_ref[...], TN))

        @pl.when(k == 0)
        def _():
            acc_g[...], acc_u[...], acc_d[...] = parts

        @pl.when(k > 0)
        def _():
            acc_g[...] += parts[0]
            acc_u[...] += parts[1]
            acc_d[...] += parts[2]

        @pl.when(k == n_k - 1)
        def _():
            gg_ref[...] = acc_g[...].astype(BF16)
            gu_ref[...] = acc_u[...].astype(BF16)
            gd_ref[...] = acc_d[...].astype(BF16)

    out = pl.BlockSpec((None, FF_SHARD, D_MODEL), lambda j, k: (j, 0, 0))
    shape = jax.ShapeDtypeStruct((N_CHIPS, FF_SHARD, D_MODEL), BF16)
    return pl.pallas_call(
        body, grid=(N_CHIPS, n_k), in_specs=[hid, hid, hid, tok, tok], out_specs=[out] * 3, out_shape=[shape] * 3,
        scratch_shapes=[pltpu.VMEM((FF_SHARD, D_MODEL), F32)] * 3,
        name="ffn_grad_weights", compiler_params=_params(("parallel", "arbitrary")),
    )(da, db, s, h, df)


BWD_TILE = 512


def matmul_prenorm_bwd(name, dims, pairs, pair_specs, x, dxo, gain, scale, sums, u, below):
    n_pairs = len(pairs) // 2
    lhs, weights = pairs[0::2], pairs[1::2]
    n_in = n_pairs * 2 + 5 + (3 if below is not None else 0)
    n_out = 2 + (2 if below is not None else 0)

    def body(*refs):
        ins, outs = refs[:n_in], refs[n_in:n_in + n_out]
        w_vmem, w_sems = refs[n_in + n_out:n_in + n_out + n_pairs], refs[-1]
        first = pl.program_id(0) == 0

        @pl.when(first)
        def _():
            copies = [pltpu.make_async_copy(ins[n_pairs + p], w_vmem[p], w_sems.at[p]) for p in range(n_pairs)]
            for cp in copies:
                cp.start()
            for cp in copies:
                cp.wait()

        dh = None
        for j in range(N_CHIPS):
            for p in range(n_pairs):
                part = _dot(pair_specs[p][1](ins[p], j), w_vmem[p][j], dims)
                dh = part if dh is None else dh + part
        k = 2 * n_pairs
        dx = _prenorm_bwd_epilogue(dh, ins[k], ins[k + 1], ins[k + 2], ins[k + 3], outs[0], outs[1], first)
        if below is not None:
            _resid_bwd_epilogue(dx, ins[k + 5], ins[k + 6], below[2], outs[2], outs[3], first)

    row = pl.BlockSpec((BWD_TILE, D_MODEL), lambda i: (i, 0))
    any_spec = pl.BlockSpec(memory_space=pl.ANY)
    f32 = jax.ShapeDtypeStruct((SEQ, D_MODEL), F32)
    in_specs = [s for s, _ in pair_specs] + [any_spec] * n_pairs + [row, row, gain.spec(), scale.spec(), any_spec]
    operands = tuple(lhs) + tuple(weights) + (x, dxo, gain.table, scale.table, sums[0])
    out_specs, out_shape = [row, _slot_spec(u)], [f32, jax.ShapeDtypeStruct(sums[0].shape, F32)]
    aliases = {2 * n_pairs + 4: 1}
    if below is not None:
        in_specs += [row, below[1].spec(), any_spec]
        operands += (below[0], below[1].table, sums[1])
        out_specs += [row, _slot_spec(u - 1)]
        out_shape += [jax.ShapeDtypeStruct((SEQ, D_MODEL), BF16), jax.ShapeDtypeStruct(sums[1].shape, F32)]
        aliases[2 * n_pairs + 7] = 3
    out = pl.pallas_call(
        body, grid=(SEQ // BWD_TILE,), in_specs=in_specs, out_specs=out_specs, out_shape=out_shape,
        scratch_shapes=[pltpu.VMEM(w.shape, w.dtype) for w in weights] + [pltpu.SemaphoreType.DMA((n_pairs,))],
        input_output_aliases=aliases, name=name, compiler_params=_params(("arbitrary",)),
    )(*operands)
    if below is None:
        return out[0], (out[1], sums[1])
    return out[0], (out[1], out[3]), out[2]


def ffn_bwd_input(da, db, wg_all, wu_all, x, dxo, gain, scale, sums, u, below):
    hid = (pl.BlockSpec((N_CHIPS, BWD_TILE, FF_SHARD), lambda i: (0, i, 0)), lambda ref, j: ref[j])
    return matmul_prenorm_bwd("ffn_bwd_input", NN, (da, wg_all, db, wu_all), (hid, hid), x, dxo, gain, scale,
                              sums, u, below)


def in_proj(h, w_all):
    def epilogue(acc, ins, outs, ids):
        outs[0][...] = acc
        outs[1][...] = acc.astype(BF16)

    out = pl.BlockSpec((MM_TILE, IN_SHARD), lambda j, i: (i, j))
    return _matmul(
        "in_proj", (N_CHIPS, SEQ // MM_TILE), 1, None, NN, 1,
        [pl.BlockSpec((MM_TILE, D_MODEL), lambda j, i: (i, 0)),
         pl.BlockSpec((None, D_MODEL, IN_SHARD), lambda j, i: (j, 0, 0))],
        [out, out], [jax.ShapeDtypeStruct((SEQ, D_IN), F32), jax.ShapeDtypeStruct((SEQ, D_IN), BF16)],
        None, epilogue, (h, w_all), ("parallel", "parallel"))


_GATE_BLOCK0 = D_QKV // D_SHARD


def _branch_products(o, w_ref):
    ob = o.astype(BF16)
    return (_dot(ob[:, 0:256], w_ref[0:256, :], NN), _dot(ob[:, 256:384], w_ref[256:384, :], NN),
            _dot(ob[:, 384:768], w_ref[384:768, :], NN))


def merge_branches(o_cat, wbr_all, proj):
    def body(o_ref, w_ref, g0_ref, g1_ref, g2_ref, m_ref):
        u = _branch_products(o_ref[...], w_ref)
        m_ref[...] = (_sigmoid(g0_ref[...]) * u[0] + _sigmoid(g1_ref[...]) * u[1]
                      + _sigmoid(g2_ref[...]) * u[2]).astype(BF16)

    def gate_spec(b):
        return pl.BlockSpec((MM_TILE, D_SHARD), lambda i, j: (i, _GATE_BLOCK0 + 4 * b + j))

    return pl.pallas_call(
        body, grid=(SEQ // MM_TILE, N_CHIPS),
        in_specs=[pl.BlockSpec((MM_TILE, BR_ROWS), lambda i, j: (i, 0)),
                  pl.BlockSpec((None, BR_ROWS, D_SHARD), lambda i, j: (j, 0, 0)),
                  gate_spec(0), gate_spec(1), gate_spec(2)],
        out_specs=pl.BlockSpec((MM_TILE, D_SHARD), lambda i, j: (i, j)),
        out_shape=jax.ShapeDtypeStruct((SEQ, D_MODEL), BF16),
        name="merge_branches", compiler_params=_params(("parallel", "parallel")),
    )(o_cat, wbr_all, proj, proj, proj)


def out_proj(merged, wout_all, x, gate, then):
    return matmul_residual(
        "out_proj", merged, pl.BlockSpec((MM_TILE, D_MODEL), lambda i: (i, 0)),
        lambda ref, j: ref[:, D_SHARD * j:D_SHARD * (j + 1)], wout_all, x, gate, 1.0, then)


def merge_bwd(dmo, wout_all, o_cat, wbr_all, proj):
    def epilogue(dm, ins, outs, ids):
        u = _branch_products(ins[2][...], ins[3])
        for b in range(3):
            sig = _sigmoid(ins[4 + b][...])
            outs[b][...] = (dm * sig).astype(BF16)
            outs[3 + b][...] = (dm * u[b] * (sig * (1.0 - sig))).astype(BF16)

    def gate_spec(b):
        return pl.BlockSpec((MM_TILE, D_SHARD), lambda j, i: (i, _GATE_BLOCK0 + 4 * b + j))

    col = pl.BlockSpec((MM_TILE, D_SHARD), lambda j, i: (i, j))
    du = jax.ShapeDtypeStruct((SEQ, D_MODEL), BF16)
    return _matmul(
        "merge_bwd", (N_CHIPS, SEQ // MM_TILE), 1, None, NT, 1,
        [pl.BlockSpec((MM_TILE, D_MODEL), lambda j, i: (i, 0)),
         pl.BlockSpec((None, D_SHARD, D_MODEL), lambda j, i: (j, 0, 0)),
         pl.BlockSpec((MM_TILE, BR_ROWS), lambda j, i: (i, 0)),
         pl.BlockSpec((None, BR_ROWS, D_SHARD), lambda j, i: (j, 0, 0)),
         gate_spec(0), gate_spec(1), gate_spec(2)],
        [col] * 6, [du] * 6,
        None, epilogue, (dmo, wout_all, o_cat, wbr_all, proj, proj, proj), ("parallel", "parallel"))


def branch_bwd_input(du, wbr_all):
    def body(d0_ref, d1_ref, d2_ref, w_ref, o_ref, acc):
        j = pl.program_id(1)
        parts = (_dot(d0_ref[...], w_ref[0:256, :], NT), _dot(d1_ref[...], w_ref[256:384, :], NT),
                 _dot(d2_ref[...], w_ref[384:768, :], NT))

        @pl.when(j == 0)
        def _():
            acc[:, 0:256], acc[:, 256:384], acc[:, 384:768] = parts

        @pl.when(j > 0)
        def _():
            acc[:, 0:256] += parts[0]
            acc[:, 256:384] += parts[1]
            acc[:, 384:768] += parts[2]

        @pl.when(j == N_CHIPS - 1)
        def _():
            o_ref[...] = acc[...]

    col = pl.BlockSpec((MM_TILE, D_SHARD), lambda i, j: (i, j))
    return pl.pallas_call(
        body, grid=(SEQ // MM_TILE, N_CHIPS),
        in_specs=[col, col, col, pl.BlockSpec((None, BR_ROWS, D_SHARD), lambda i, j: (j, 0, 0))],
        out_specs=pl.BlockSpec((MM_TILE, BR_ROWS), lambda i, j: (i, 0)),
        out_shape=jax.ShapeDtypeStruct((SEQ, BR_ROWS), F32),
        scratch_shapes=[pltpu.VMEM((MM_TILE, BR_ROWS), F32)],
        name="branch_bwd_input", compiler_params=_params(("parallel", "arbitrary")),
    )(du[0], du[1], du[2], wbr_all)


def branch_grad_weights(o_cat, du):
    def body(o_ref, d0_ref, d1_ref, d2_ref, g_ref, acc):
        k = pl.program_id(1)
        ob = o_ref[...].astype(BF16)
        parts = (_dot(ob[:, 0:256], d0_ref[...], TN), _dot(ob[:, 256:384], d1_ref[...], TN),
                 _dot(ob[:, 384:768], d2_ref[...], TN))

        @pl.when(k == 0)
        def _():
            acc[0:256, :], acc[256:384, :], acc[384:768, :] = parts

        @pl.when(k > 0)
        def _():
            acc[0:256, :] += parts[0]
            acc[256:384, :] += parts[1]
            acc[384:768, :] += parts[2]

        @pl.when(k == SEQ // MM_TILE - 1)
        def _():
            g_ref[...] = acc[...].astype(BF16)

    col = pl.BlockSpec((MM_TILE, D_SHARD), lambda j, k: (k, j))
    return pl.pallas_call(
        body, grid=(N_CHIPS, SEQ // MM_TILE),
        in_specs=[pl.BlockSpec((MM_TILE, BR_ROWS), lambda j, k: (k, 0)), col, col, col],
        out_specs=pl.BlockSpec((None, BR_ROWS, D_SHARD), lambda j, k: (j, 0, 0)),
        out_shape=jax.ShapeDtypeStruct((N_CHIPS, BR_ROWS, D_SHARD), BF16),
        scratch_shapes=[pltpu.VMEM((BR_ROWS, D_SHARD), F32)],
        name="branch_grad_weights", compiler_params=_params(("parallel", "arbitrary")),
    )(o_cat, du[0], du[1], du[2])


def mixer_bwd_input(dproj, win_all, x, dxo, gain, scale, sums, u, below):
    columns = (pl.BlockSpec((BWD_TILE, D_IN), lambda i: (i, 0)),
               lambda ref, j: ref[:, IN_SHARD * j:IN_SHARD * (j + 1)])
    return matmul_prenorm_bwd("mixer_bwd_input", NT, (dproj, win_all), (columns,), x, dxo, gain, scale, sums, u, below)


BATCH_QK = (((2,), (2,)), ((0,), (0,)))
BATCH_PV = (((2,), (1,)), ((0,), (0,)))
BATCH_TN = (((1,), (1,)), ((0,), (0,)))


SB_WIDTH = H_SB * HEAD_DIM
SB_ROWS = H_SB * BLK


def _split_dot(v, tri):
    hi = v.astype(BF16)
    lo = (v - hi.astype(F32)).astype(BF16)
    return _dot(hi, tri, NN) + _dot(lo, tri, NN)


def _tri(cmp):
    return cmp(lax.broadcasted_iota(jnp.int32, (BLK, BLK), 0), lax.broadcasted_iota(jnp.int32, (BLK, BLK), 1)).astype(BF16)


def _head_masks():
    lane = lax.broadcasted_iota(jnp.int32, (1, SB_WIDTH), 1) // HEAD_DIM
    return [lane == h for h in range(H_SB)]


def _stack_heads(x, masks):
    return jnp.concatenate([jnp.where(m, x, jnp.zeros_like(x)) for m in masks], axis=0)


def _merge_heads(y, masks):
    out = jnp.where(masks[0], y[0:BLK], 0.0)
    for h in range(1, H_SB):
        out = jnp.where(masks[h], y[h * BLK:(h + 1) * BLK], out)
    return out


def _sb_scores(q4, k_ref, j, diagonal):
    rows = pl.ds(pl.multiple_of(j * BLK, BLK), BLK)
    z = _dot(q4, k_ref[rows, :], NT)
    log_fail = -(jnp.maximum(z, 0.0) + jnp.log(1.0 + jnp.exp(-jnp.abs(z))))
    log_hit = z + log_fail
    before = None
    if diagonal:
        tile = (SB_ROWS, BLK)
        before = lax.broadcasted_iota(jnp.int32, tile, 1) < (lax.broadcasted_iota(jnp.int32, tile, 0) & (BLK - 1))
        log_fail = jnp.where(before, log_fail, 0.0)
    return rows, before, log_fail, log_hit


def _keep(before, x):
    return x if before is None else jnp.where(before, x, 0.0)


def sb_forward(qkv):
    def body(q_ref, k_ref, v_ref, o_ref, tot_ref):
        i = pl.program_id(0)
        masks = _head_masks()
        q4 = _stack_heads(q_ref[...] * QK_SCALE, masks)
        later = _tri(lambda r, c: r > c)

        def tiles(js, carry, diagonal):
            o, run = carry
            scores = [_sb_scores(q4, k_ref, j, diagonal) for j in js]
            acc = None
            for rows, before, log_fail, log_hit in scores:
                between = _split_dot(log_fail, later) + run
                w = _keep(before, jnp.exp(log_hit + between))
                part = _dot(w.astype(BF16), v_ref[rows, :], NN)
                acc = part if acc is None else acc + part
                run = run + jnp.sum(log_fail, axis=1, keepdims=True)
            return o + _merge_heads(acc, masks), run

        carry = tiles([i], (jnp.zeros((BLK, SB_WIDTH), F32), jnp.zeros((SB_ROWS, 1), F32)), True)
        carry = lax.cond((i & 1) != 0, lambda c: tiles([i - 1], c, False), lambda c: c, carry)
        at = i - 1 - (i & 1)
        carry = lax.cond((i & 2) != 0, lambda c: tiles([at, at - 1], c, False), lambda c: c, carry)
        at = at - (i & 2)
        o, run = lax.fori_loop(0, i // 4, lambda t, c: tiles([at - 4 * t - n for n in range(4)], c, False), carry)
        o_ref[...] = o
        tot_ref[...] = run

    return pl.pallas_call(
        body, grid=(N_BLK,),
        in_specs=[pl.BlockSpec((BLK, SB_WIDTH), lambda i: (i, 0)), pl.BlockSpec((SEQ, SB_WIDTH), lambda i: (0, 1)),
                  pl.BlockSpec((SEQ, SB_WIDTH), lambda i: (0, 2))],
        out_specs=[pl.BlockSpec((BLK, SB_WIDTH), lambda i: (i, 0)), pl.BlockSpec((None, SB_ROWS, 1), lambda i: (i, 0, 0))],
        out_shape=[jax.ShapeDtypeStruct((SEQ, SB_WIDTH), F32), jax.ShapeDtypeStruct((N_BLK, SB_ROWS, 1), F32)],
        name="sb_forward", compiler_params=_params(("parallel",)),
    )(qkv, qkv, qkv)


def sb_backward(qkv, total, do_cat):
    def body(q_ref, k_ref, v_ref, tot_ref, do_ref, dq_ref, dk_ref, dv_ref):
        i = pl.program_id(0)

        @pl.when(i == 0)
        def _():
            dk_ref[...] = jnp.zeros_like(dk_ref)
            dv_ref[...] = jnp.zeros_like(dv_ref)

        masks = _head_masks()
        q4 = _stack_heads(q_ref[...] * QK_SCALE, masks)
        do4 = _stack_heads(do_ref[...].astype(BF16), masks)
        total_v = tot_ref[...]
        upto = _tri(lambda r, c: r <= c)
        earlier = _tri(lambda r, c: r < c)

        def tiles(js, carry, diagonal):
            dq, seen, g_seen = carry
            scores = [_sb_scores(q4, k_ref, j, diagonal) for j in js]
            acc = None
            for rows, before, log_fail, log_hit in scores:
                between = total_v - (seen + _split_dot(log_fail, upto))
                w = _keep(before, jnp.exp(log_hit + between))
                g = _dot(do4, v_ref[rows, :], NT) * w
                g_earlier = g_seen + _split_dot(g, earlier)
                sig = jnp.exp(log_hit)
                dz = _keep(before, g * (1.0 - sig) - g_earlier * sig).astype(BF16)
                part = _dot(dz, k_ref[rows, :], NN)
                acc = part if acc is None else acc + part
                dk_ref[rows, :] += _dot(dz, q4, TN)
                dv_ref[rows, :] += _dot(w.astype(BF16), do4, TN)
                seen = seen + jnp.sum(log_fail, axis=1, keepdims=True)
                g_seen = g_seen + jnp.sum(g, axis=1, keepdims=True)
            return dq + _merge_heads(acc, masks), seen, g_seen

        zero = jnp.zeros((SB_ROWS, 1), F32)
        carry = lax.fori_loop(0, i // 4, lambda t, c: tiles([4 * t + n for n in range(4)], c, False),
                              (jnp.zeros((BLK, SB_WIDTH), F32), zero, zero))
        at = i - (i & 3)
        carry = lax.cond((i & 2) != 0, lambda c: tiles([at, at + 1], c, False), lambda c: c, carry)
        carry = lax.cond((i & 1) != 0, lambda c: tiles([i - 1], c, False), lambda c: c, carry)
        dq, _, _ = tiles([i], carry, True)
        dq_ref[...] = dq * QK_SCALE

    blk = pl.BlockSpec((BLK, SB_WIDTH), lambda i: (i, 0))
    full = pl.BlockSpec((SEQ, SB_WIDTH), lambda i: (0, 0))
    shape = jax.ShapeDtypeStruct((SEQ, SB_WIDTH), F32)
    return pl.pallas_call(
        body, grid=(N_BLK,),
        in_specs=[blk, pl.BlockSpec((SEQ, SB_WIDTH), lambda i: (0, 1)), pl.BlockSpec((SEQ, SB_WIDTH), lambda i: (0, 2)),
                  pl.BlockSpec((None, SB_ROWS, 1), lambda i: (i, 0, 0)), blk],
        out_specs=[blk, full, full], out_shape=[shape, shape, shape],
        name="sb_backward", compiler_params=_params(("arbitrary",)),
    )(qkv, qkv, qkv, total, do_cat)


def _band_scores(q_ref, kp_ref, ko_ref, bias_ref, hb, prev_mask):
    b = pl.program_id(1)
    qs = q_ref[...]
    s_prev = _dot(qs, kp_ref[...], BATCH_QK) + bias_ref[:, :, 0:BLK]
    s_prev = jnp.concatenate(
        [jnp.where((b & prev_mask(pl.program_id(0) * hb + t)) != 0, s_prev[t:t + 1], NEG) for t in range(hb)], axis=0)
    s_own = _dot(qs, ko_ref[...], BATCH_QK) + bias_ref[:, :, BLK:2 * BLK]
    return qs, s_prev, s_own


def _band_specs(hb, rows, t_n):
    def q_spec(width):
        return pl.BlockSpec((hb, None, rows, width), lambda h, b: (h, b, 0, 0))

    own = pl.BlockSpec((hb, BLK, HEAD_DIM), lambda h, b: (h, b, 0))
    prev = pl.BlockSpec((hb, BLK, HEAD_DIM), lambda h, b: (h, jnp.maximum(b - 1, 0), 0))
    per_head = lambda r, width: pl.BlockSpec((hb, r, width), lambda h, b: (h, 0, 0))
    return q_spec, own, prev, per_head


def banded_forward(name, q, k, v, bias, sinks, hb, prev_mask):
    h_n, nb, rows, _ = q.shape

    def body(q_ref, kp_ref, ko_ref, vp_ref, vo_ref, bias_ref, sink_ref, o_ref, lse_ref):
        _, s_prev, s_own = _band_scores(q_ref, kp_ref, ko_ref, bias_ref, hb, prev_mask)
        sink = sink_ref[...]
        m = jnp.maximum(jnp.maximum(jnp.max(s_prev, axis=2, keepdims=True), jnp.max(s_own, axis=2, keepdims=True)), sink)
        p_prev = jnp.exp(s_prev - m)
        p_own = jnp.exp(s_own - m)
        denom = jnp.sum(p_prev, axis=2, keepdims=True) + jnp.sum(p_own, axis=2, keepdims=True) + jnp.exp(sink - m)
        o = _dot(p_prev.astype(BF16), vp_ref[...], BATCH_PV) + _dot(p_own.astype(BF16), vo_ref[...], BATCH_PV)
        o_ref[...] = o / denom
        lse_ref[...] = m + jnp.log(denom)

    q_spec, own, prev, per_head = _band_specs(hb, rows, k.shape[1])
    return pl.pallas_call(
        body, grid=(h_n // hb, nb),
        in_specs=[q_spec(HEAD_DIM), prev, own, prev, own, per_head(rows, 2 * BLK), per_head(rows, 1)],
        out_specs=[q_spec(HEAD_DIM), q_spec(1)],
        out_shape=[jax.ShapeDtypeStruct(q.shape, F32), jax.ShapeDtypeStruct((h_n, nb, rows, 1), F32)],
        name=name, compiler_params=_params(("parallel", "parallel")),
    )(q, k, k, v, v, bias, sinks)


def banded_backward(name, q, k, v, bias, sinks, o, lse, do, dlse, hb, prev_mask):
    h_n, nb, rows, _ = q.shape
    t_n = k.shape[1]

    def body(q_ref, kp_ref, ko_ref, vp_ref, vo_ref, bias_ref, sink_ref, o_ref, lse_ref, do_ref, dlse_ref,
             dq_ref, dk_ref, dv_ref, dbias_ref, dsink_ref):
        b = pl.program_id(1)

        @pl.when(b == 0)
        def _():
            dk_ref[...] = jnp.zeros_like(dk_ref)
            dv_ref[...] = jnp.zeros_like(dv_ref)
            dbias_ref[...] = jnp.zeros_like(dbias_ref)
            dsink_ref[...] = jnp.zeros_like(dsink_ref)

        qs, s_prev, s_own = _band_scores(q_ref, kp_ref, ko_ref, bias_ref, hb, prev_mask)
        lse_v = lse_ref[...]
        dov = do_ref[...]
        dob = dov.astype(BF16)
        shift = dlse_ref[...] - jnp.sum(dov * o_ref[...], axis=2, keepdims=True)
        p_prev = jnp.exp(s_prev - lse_v)
        p_own = jnp.exp(s_own - lse_v)
        ds_prev = p_prev * (_dot(dob, vp_ref[...], BATCH_QK) + shift)
        ds_own = p_own * (_dot(dob, vo_ref[...], BATCH_QK) + shift)
        dbias_ref[:, :, 0:BLK] += ds_prev
        dbias_ref[:, :, BLK:2 * BLK] += ds_own
        d_sink = jnp.exp(sink_ref[...] - lse_v) * shift
        for g in range(rows // BLK):
            dsink_ref[:, g:g + 1, :] += jnp.sum(d_sink[:, g * BLK:(g + 1) * BLK, :], axis=1, keepdims=True)
        ds_prev = ds_prev.astype(BF16)
        ds_own = ds_own.astype(BF16)
        dq_ref[...] = (_dot(ds_prev, kp_ref[...], BATCH_PV) + _dot(ds_own, ko_ref[...], BATCH_PV)) * QK_SCALE
        rows_prev = pl.ds(pl.multiple_of(jnp.maximum(b - 1, 0) * BLK, BLK), BLK)
        rows_own = pl.ds(pl.multiple_of(b * BLK, BLK), BLK)
        dk_ref[:, rows_prev, :] += _dot(ds_prev, qs, BATCH_TN)
        dk_ref[:, rows_own, :] += _dot(ds_own, qs, BATCH_TN)
        dv_ref[:, rows_prev, :] += _dot(p_prev.astype(BF16), dob, BATCH_TN)
        dv_ref[:, rows_own, :] += _dot(p_own.astype(BF16), dob, BATCH_TN)

    q_spec, own, prev, per_head = _band_specs(hb, rows, t_n)
    kv_full = per_head(t_n, HEAD_DIM)
    kv_shape = jax.ShapeDtypeStruct((h_n, t_n, HEAD_DIM), F32)
    return pl.pallas_call(
        body, grid=(h_n // hb, nb),
        in_specs=[q_spec(HEAD_DIM), prev, own, prev, own, per_head(rows, 2 * BLK), per_head(rows, 1),
                  q_spec(HEAD_DIM), q_spec(1), q_spec(HEAD_DIM), q_spec(1)],
        out_specs=[q_spec(HEAD_DIM), kv_full, kv_full, per_head(rows, 2 * BLK), per_head(rows // BLK, BLK)],
        out_shape=[jax.ShapeDtypeStruct(q.shape, F32), kv_shape, kv_shape,
                   jax.ShapeDtypeStruct((h_n, rows, 2 * BLK), F32), jax.ShapeDtypeStruct((h_n, rows // BLK, BLK), F32)],
        name=name, compiler_params=_params(("parallel", "arbitrary")),
    )(q, k, k, v, v, bias, sinks, o, lse, do, dlse)


def _swa_prev_mask(head):
    del head
    return 15


SWA_HEADS_PER_STEP = 2
SWA_GROUP = H_SWA_Q // H_SWA_KV
N_BLK = SEQ // BLK


GROUP_W = H_PER_DIL * HEAD_DIM
DIL_COLUMNS = (768, 1920)
LANE_BLOCKS = (DIL_COLUMNS[1] - DIL_COLUMNS[0]) // GROUP_W
DIL_Q_BLOCK, DIL_K_BLOCK, DIL_V_BLOCK = 0, 3, 6
N_GROUPS = len(DIL_PATTERNS)


def dilated_views(qkv):
    cols = qkv[:, DIL_COLUMNS[0]:DIL_COLUMNS[1]]
    return [_dil_view(cols, d) for _, d in DIL_PATTERNS]


def _dil_view(t, d):
    return t.reshape(SEQ // d, d * t.shape[1])


def _dil_tile(n, d):
    per_class = N_BLK // d
    return n // per_class, n % per_class


def _dil_spec(d, lane_block, lane_blocks, shift=0):
    def index(n):
        r, m = _dil_tile(n, d)
        m = jnp.clip(m + shift, 0, N_BLK // d - 1)
        return m, r * lane_blocks + lane_block
    return pl.BlockSpec((BLK, GROUP_W), index)


def _two_heads(x, first):
    zero = jnp.zeros_like(x)
    return jnp.concatenate([jnp.where(first, x, zero), jnp.where(first, zero, x)], axis=0)


def _per_head(col, first):
    return jnp.where(first, col[0:BLK], col[BLK:2 * BLK])


def _head_rows(tile, first):
    pick = lambda keep: jnp.max(jnp.where(keep, tile, -jnp.inf), axis=1, keepdims=True)
    return jnp.concatenate([pick(first), pick(jnp.logical_not(first))], axis=0)


def _dil_scores(q_ref, kp_ref, ko_ref, bias, has_prev, first):
    q2 = _two_heads(q_ref[...] * QK_SCALE, first)
    k2 = jnp.concatenate([kp_ref[...], ko_ref[...]], axis=0)
    s = _dot(q2, k2, NT) + bias
    key = lax.broadcasted_iota(jnp.int32, s.shape, 1)
    return q2, k2, jnp.where(jnp.logical_or(has_prev, key >= BLK), s, NEG)


def dilated_forward(views, bias):
    def body(*refs):
        ins, bias_ref, outs = refs[:5 * N_GROUPS], refs[5 * N_GROUPS], refs[5 * N_GROUPS + 1:]
        n = pl.program_id(0)
        first = lax.broadcasted_iota(jnp.int32, (1, GROUP_W), 1) < HEAD_DIM
        for g, (_, d) in enumerate(DIL_PATTERNS):
            q_ref, kp_ref, ko_ref, vp_ref, vo_ref = ins[5 * g:5 * g + 5]
            has_prev = _dil_tile(n, d)[1] > 0
            _, _, s = _dil_scores(q_ref, kp_ref, ko_ref, bias_ref[g], has_prev, first)
            m = jnp.max(s, axis=1, keepdims=True)
            p = jnp.exp(s - m)
            denom = jnp.sum(p, axis=1, keepdims=True)
            v2 = jnp.concatenate([vp_ref[...], vo_ref[...]], axis=0)
            o2 = _dot(p.astype(BF16), v2, NN) / denom
            outs[2 * g][...] = _per_head(o2, first)
            outs[2 * g + 1][...] = _per_head(m + jnp.log(denom), first)

    operands, in_specs, out_specs, out_shape = [], [], [], []
    for g, (_, d) in enumerate(DIL_PATTERNS):
        operands += [views[g]] * 5
        in_specs += [_dil_spec(d, DIL_Q_BLOCK + g, LANE_BLOCKS), _dil_spec(d, DIL_K_BLOCK + g, LANE_BLOCKS, -1),
                     _dil_spec(d, DIL_K_BLOCK + g, LANE_BLOCKS), _dil_spec(d, DIL_V_BLOCK + g, LANE_BLOCKS, -1),
                     _dil_spec(d, DIL_V_BLOCK + g, LANE_BLOCKS)]
        out_specs += [_dil_spec(d, 0, 1)] * 2
        out_shape += [jax.ShapeDtypeStruct((SEQ // d, d * GROUP_W), F32)] * 2
    out = pl.pallas_call(
        body, grid=(N_BLK,), in_specs=in_specs + [pl.BlockSpec((N_GROUPS, 2 * BLK, 2 * BLK), lambda n: (0, 0, 0))],
        out_specs=out_specs, out_shape=out_shape, name="dilated_forward", compiler_params=_params(("parallel",)),
    )(*operands, bias)
    out = [t.reshape(SEQ, GROUP_W) for t in out]
    return out[0::2], out[1::2]


def _group_softmax(lses):
    m = jnp.maximum(jnp.maximum(lses[0], lses[1]), lses[2])
    e = [jnp.exp(l - m) for l in lses]
    total = e[0] + e[1] + e[2]
    return [t / total for t in e]


def dilated_merge(o, lse):
    def body(*refs):
        alpha = _group_softmax([r[...] for r in refs[N_GROUPS:2 * N_GROUPS]])
        refs[-1][...] = alpha[0] * refs[0][...] + alpha[1] * refs[1][...] + alpha[2] * refs[2][...]

    spec = pl.BlockSpec((ROW_TILE, GROUP_W), lambda i: (i, 0))
    return pl.pallas_call(
        body, grid=(SEQ // ROW_TILE,), in_specs=[spec] * (2 * N_GROUPS), out_specs=spec,
        out_shape=jax.ShapeDtypeStruct((SEQ, GROUP_W), F32), name="dilated_merge", compiler_params=_params(("parallel",)),
    )(*o, *lse)


def dilated_merge_bwd(o, lse, do_cat):
    def body(*refs):
        o_v = [r[...] for r in refs[:N_GROUPS]]
        alpha = _group_softmax([r[...] for r in refs[N_GROUPS:2 * N_GROUPS]])
        dout = refs[2 * N_GROUPS][...]
        outs = refs[2 * N_GROUPS + 1:]
        first = lax.broadcasted_iota(jnp.int32, (1, GROUP_W), 1) < HEAD_DIM

        def head_sum(x):
            a = jnp.sum(jnp.where(first, x, 0.0), axis=1, keepdims=True)
            b = jnp.sum(jnp.where(first, 0.0, x), axis=1, keepdims=True)
            return jnp.where(first, a, b)

        dalpha = [head_sum(dout * o_g) for o_g in o_v]
        mean = alpha[0] * dalpha[0] + alpha[1] * dalpha[1] + alpha[2] * dalpha[2]
        for g in range(N_GROUPS):
            outs[g][...] = alpha[g] * dout
            outs[N_GROUPS + g][...] = alpha[g] * (dalpha[g] - mean)

    spec = pl.BlockSpec((ROW_TILE, GROUP_W), lambda i: (i, 0))
    shape = jax.ShapeDtypeStruct((SEQ, GROUP_W), F32)
    out = pl.pallas_call(
        body, grid=(SEQ // ROW_TILE,), in_specs=[spec] * (2 * N_GROUPS) + [pl.BlockSpec((ROW_TILE, GROUP_W), lambda i: (i, 2))],
        out_specs=[spec] * (2 * N_GROUPS), out_shape=[shape] * (2 * N_GROUPS),
        name="dilated_merge_bwd", compiler_params=_params(("parallel",)),
    )(*o, *lse, do_cat)
    return out[:N_GROUPS], out[N_GROUPS:]


def dilated_backward(views, bias, o, lse, do, dlse):
    n_in = 9

    def body(*refs):
        ins, bias_ref = refs[:n_in * N_GROUPS], refs[n_in * N_GROUPS]
        outs, dbias_ref = refs[n_in * N_GROUPS + 1:-1], refs[-1]
        n = pl.program_id(0)

        @pl.when(n == 0)
        def _():
            dbias_ref[...] = jnp.zeros_like(dbias_ref)

        first = lax.broadcasted_iota(jnp.int32, (1, GROUP_W), 1) < HEAD_DIM
        for g, (_, d) in enumerate(DIL_PATTERNS):
            q_ref, kp_ref, ko_ref, vp_ref, vo_ref, o_ref, lse_ref, do_ref, dlse_ref = ins[n_in * g:n_in * (g + 1)]
            has_prev = _dil_tile(n, d)[1] > 0
            q2, k2, s = _dil_scores(q_ref, kp_ref, ko_ref, bias_ref[g], has_prev, first)
            dov = do_ref[...]
            do2 = _two_heads(dov.astype(BF16), first)
            prod = dov * o_ref[...]
            delta = jnp.concatenate([jnp.sum(jnp.where(first, prod, 0.0), axis=1, keepdims=True),
                                     jnp.sum(jnp.where(first, 0.0, prod), axis=1, keepdims=True)], axis=0)
            shift = _head_rows(dlse_ref[...], first) - delta
            p = jnp.exp(s - _head_rows(lse_ref[...], first))
            v2 = jnp.concatenate([vp_ref[...], vo_ref[...]], axis=0)
            ds = p * (_dot(do2, v2, NT) + shift)
            dbias_ref[g] += ds
            ds = ds.astype(BF16)
            dq2 = _dot(ds, k2, NN) * QK_SCALE
            dk2 = _dot(ds, q2, TN)
            dv2 = _dot(p.astype(BF16), do2, TN)
            base = 5 * g
            outs[base][...] = jnp.where(first, dq2[0:BLK], dq2[BLK:2 * BLK])
            outs[base + 1][...] = dk2[BLK:2 * BLK]
            outs[base + 2][...] = dk2[0:BLK]
            outs[base + 3][...] = dv2[BLK:2 * BLK]
            outs[base + 4][...] = dv2[0:BLK]

    operands, in_specs, out_specs, out_shape = [], [], [], []
    for g, (_, d) in enumerate(DIL_PATTERNS):
        own = _dil_spec(d, 0, 1)
        operands += [views[g]] * 5 + [_dil_view(t[g], d) for t in (o, lse, do, dlse)]
        in_specs += [_dil_spec(d, DIL_Q_BLOCK + g, LANE_BLOCKS), _dil_spec(d, DIL_K_BLOCK + g, LANE_BLOCKS, -1),
                     _dil_spec(d, DIL_K_BLOCK + g, LANE_BLOCKS), _dil_spec(d, DIL_V_BLOCK + g, LANE_BLOCKS, -1),
                     _dil_spec(d, DIL_V_BLOCK + g, LANE_BLOCKS)] + [own] * 4
        out_specs += [own] * 5
        out_shape += [jax.ShapeDtypeStruct((SEQ // d, d * GROUP_W), F32)] * 5
    tiles = pl.BlockSpec((N_GROUPS, 2 * BLK, 2 * BLK), lambda n: (0, 0, 0))
    out = pl.pallas_call(
        body, grid=(N_BLK,), in_specs=in_specs + [tiles], out_specs=out_specs + [tiles],
        out_shape=out_shape + [jax.ShapeDtypeStruct((N_GROUPS, 2 * BLK, 2 * BLK), F32)],
        name="dilated_backward", compiler_params=_params(("arbitrary",)),
    )(*operands, bias)
    return [out[5 * g:5 * g + 5] for g in range(N_GROUPS)], out[-1]


def dilated_key_grads(parts):
    def body(*refs):
        ins, outs = refs[:4 * N_GROUPS], refs[4 * N_GROUPS:]
        n = pl.program_id(0)
        for g, (_, d) in enumerate(DIL_PATTERNS):
            has_next = _dil_tile(n, d)[1] < N_BLK // d - 1
            own_k, next_k, own_v, next_v = ins[4 * g:4 * g + 4]
            outs[2 * g][...] = own_k[...] + jnp.where(has_next, next_k[...], 0.0)
            outs[2 * g + 1][...] = own_v[...] + jnp.where(has_next, next_v[...], 0.0)

    operands, in_specs, out_specs, out_shape = [], [], [], []
    for g, (_, d) in enumerate(DIL_PATTERNS):
        _, dk_own, dk_prev, dv_own, dv_prev = parts[g]
        operands += [dk_own, dk_prev, dv_own, dv_prev]
        in_specs += [_dil_spec(d, 0, 1), _dil_spec(d, 0, 1, 1)] * 2
        out_specs += [_dil_spec(d, 0, 1)] * 2
        out_shape += [jax.ShapeDtypeStruct((SEQ // d, d * GROUP_W), F32)] * 2
    out = pl.pallas_call(
        body, grid=(N_BLK,), in_specs=in_specs, out_specs=out_specs, out_shape=out_shape,
        name="dilated_key_grads", compiler_params=_params(("parallel",)),
    )(*operands)
    tok = lambda ts: jnp.concatenate([t.reshape(SEQ, GROUP_W) for t in ts], axis=1)
    return tok([parts[g][0] for g in range(N_GROUPS)]), tok(out[0::2]), tok(out[1::2])


def rel_bias_reduce(dbias0, dbias1, bucket):
    def body(d0_ref, d1_ref, b_ref, o_ref):
        dv, bv = d0_ref[...] + d1_ref[...], b_ref[...]
        lane = lax.broadcasted_iota(jnp.int32, (1, BLK), 1)
        acc = jnp.zeros((1, BLK), F32)
        for bkt in range(N_BUCKETS):
            acc = acc + jnp.where(lane == bkt, jnp.sum(jnp.where(bv == bkt, dv, 0.0)), 0.0)
        o_ref[...] = acc

    tile = pl.BlockSpec((None, BLK, 2 * BLK), lambda h: (h, 0, 0))
    return pl.pallas_call(
        body, grid=(dbias0.shape[0],), in_specs=[tile, tile, tile],
        out_specs=pl.BlockSpec((None, 1, BLK), lambda h: (h, 0, 0)),
        out_shape=jax.ShapeDtypeStruct((dbias0.shape[0], 1, BLK), F32),
        name="rel_bias_reduce", compiler_params=_params(("parallel",)),
    )(dbias0, dbias1, bucket)


def _heads(t):
    return t.reshape(SEQ, -1, HEAD_DIM).transpose(1, 0, 2)


def _unheads(t):
    return t.transpose(1, 0, 2).reshape(SEQ, -1)


def _t5_bucket(n):
    max_exact = N_BUCKETS // 2
    nf = jnp.maximum(n, 1).astype(F32)
    large = max_exact + (jnp.log(nf / max_exact) / math.log(MAX_REL_DIST / max_exact)
                         * (N_BUCKETS - max_exact)).astype(jnp.int32)
    large = jnp.minimum(large, N_BUCKETS - 1)
    return jnp.where(n < max_exact, n, large)


def band_tables(rel_bias):
    rel = jnp.arange(BLK)[:, None] + BLK - jnp.arange(2 * BLK)[None, :]
    patterns = [(d, w // d, H_PER_DIL) for w, d in DIL_PATTERNS] + [(1, SWA_WINDOW - 1, H_SWA_Q)]
    buckets = []
    for d, max_dist, heads in patterns:
        band = (rel >= 0) & (rel <= max_dist)
        tile = jnp.where(band, _t5_bucket(jnp.maximum(rel, 0) * d), -1).astype(jnp.int32)
        buckets.append(jnp.broadcast_to(tile, (heads,) + tile.shape))
    buckets = jnp.concatenate(buckets, axis=0)

    def body(table_ref, b_ref, o_ref):
        h = pl.program_id(0)
        bv = b_ref[...]
        tile = jnp.full(bv.shape, NEG, F32)
        for bkt in range(N_BUCKETS):
            tile = jnp.where(bv == bkt, table_ref[h, bkt], tile)
        o_ref[...] = tile

    spec = pl.BlockSpec((None, BLK, 2 * BLK), lambda h: (h, 0, 0))
    tiles = pl.pallas_call(
        body, grid=(buckets.shape[0],), in_specs=[pl.BlockSpec(memory_space=pltpu.SMEM), spec], out_specs=spec,
        out_shape=jax.ShapeDtypeStruct(buckets.shape, F32), name="band_tables", compiler_params=_params(("parallel",)),
    )(rel_bias.T, buckets)
    return tiles[:H_DIL], tiles[H_DIL:], buckets


def _swa_rows(t):
    t = t.reshape(N_BLK, BLK, H_SWA_KV, SWA_GROUP, HEAD_DIM).transpose(2, 0, 3, 1, 4)
    return t.reshape(H_SWA_KV, N_BLK, SWA_GROUP * BLK, HEAD_DIM)


def _swa_tokens(t):
    t = t.reshape(H_SWA_KV, N_BLK, SWA_GROUP, BLK, HEAD_DIM).transpose(1, 3, 0, 2, 4)
    return t.reshape(SEQ, H_SWA_Q * HEAD_DIM)


def _sink_rows(sinks):
    return jnp.broadcast_to(sinks.reshape(H_SWA_KV, SWA_GROUP, 1, 1), (H_SWA_KV, SWA_GROUP, BLK, 1)).reshape(
        H_SWA_KV, SWA_GROUP * BLK, 1)


def _vec(v):
    return v.reshape(1, D_MODEL)


class UnitRows:
    def __init__(self, u, mod_table, gain_table):
        self.shift, self.scale, self.gate = (Row(mod_table, 3 * u + t) for t in range(3))
        self.gain = Row(gain_table, u)


def ffn_forward(x, h, rows, then, w):
    a, b, s = ffn_up(h, w[0], w[1])
    f, xo, *h_next = ffn_down(s, w[2], x, rows.gate, then)
    return xo, (h_next or [None])[0], (x, h, a, b, s, f)


def ffn_backward(u, dxo, df, saved, rows, w, sums, below):
    x, h, a, b, s, _ = saved
    da, db = ffn_bwd_hidden(df, w[2], a, b)
    grads = ffn_grad_weights(h, s, df, da, db)
    dx, sums, *df_below = ffn_bwd_input(da, db, w[0], w[1], x, dxo, rows.gain, rows.scale, sums, u, below)
    return dx, sums, df_below, grads


def mixer_forward(x, h, rows, then, sinks, bias_dil, bias_swa, w):
    proj, qkv = in_proj(h, w[0])
    q_swa, k_swa, v_swa = _swa_rows(qkv[:, 1920:2304] * QK_SCALE), _heads(qkv[:, 2304:2432]), _heads(qkv[:, 2432:2560])
    o_sb, total_sb = sb_forward(qkv)
    bias_dil = bias_dil.reshape(N_GROUPS, 2 * BLK, 2 * BLK)
    views = dilated_views(qkv)
    o_groups, lse_groups = dilated_forward(views, bias_dil)
    o_dil = dilated_merge(o_groups, lse_groups)
    bias_swa = bias_swa.reshape(H_SWA_KV, SWA_GROUP * BLK, 2 * BLK)
    o_swa, lse_swa = banded_forward("swa_forward", q_swa, k_swa, v_swa, bias_swa, _sink_rows(sinks), SWA_HEADS_PER_STEP,
                                    _swa_prev_mask)
    o_cat = jnp.concatenate([o_sb, o_dil, _swa_tokens(o_swa)], axis=1)
    merged = merge_branches(o_cat, w[1], proj)
    mo, xo, *h_next = out_proj(merged, w[2], x, rows.gate, then)
    saved = (x, h, proj, (qkv, total_sb), (views, o_groups, lse_groups),
             (q_swa, k_swa, v_swa, o_swa, lse_swa), o_cat, merged, mo)
    return xo, (h_next or [None])[0], saved


def mixer_backward(u, dxo, dmo, saved, rows, sinks, bias_dil, bias_swa, w, sums, below):
    x, h, proj, sb, dil, swa, o_cat, merged, _ = saved
    tok = pl.BlockSpec((MM_TILE, D_MODEL), lambda j, k: (k, 0))
    g_out = grad_weight("grad_w_out", merged, pl.BlockSpec((MM_TILE, D_SHARD), lambda j, k: (k, j)), dmo, tok,
                        (D_SHARD, D_MODEL))
    du0, du1, du2, dg0, dg1, dg2 = merge_bwd(dmo, w[2], o_cat, w[1], proj)
    du = (du0, du1, du2)
    do_cat = branch_bwd_input(du, w[1])
    g_br = branch_grad_weights(o_cat, du)

    qkv, total_sb = sb
    dq_sb, dk_sb, dv_sb = sb_backward(qkv, total_sb, do_cat)

    views, o_groups, lse_groups = dil
    bias_dil = bias_dil.reshape(N_GROUPS, 2 * BLK, 2 * BLK)
    do_groups, dlse_groups = dilated_merge_bwd(o_groups, lse_groups, do_cat)
    parts, dbias_dil = dilated_backward(views, bias_dil, o_groups, lse_groups, do_groups, dlse_groups)
    dq_dil, dk_dil, dv_dil = dilated_key_grads(parts)
    dbias_dil = dbias_dil.reshape(H_DIL, BLK, 2 * BLK)

    q_swa, k_swa, v_swa, o_swa, lse_swa = swa
    bias_swa = bias_swa.reshape(H_SWA_KV, SWA_GROUP * BLK, 2 * BLK)
    dq_swa, dk_swa, dv_swa, dbias_swa, dsinks = banded_backward(
        "swa_backward", q_swa, k_swa, v_swa, bias_swa, _sink_rows(sinks), o_swa, lse_swa, _swa_rows(do_cat[:, 384:768]),
        jnp.zeros_like(lse_swa), SWA_HEADS_PER_STEP, _swa_prev_mask)
    dbias_swa = dbias_swa.reshape(H_SWA_Q, BLK, 2 * BLK)

    dproj = jnp.concatenate(
        [dq_sb, dk_sb, dv_sb, dq_dil, dk_dil, dv_dil, _swa_tokens(dq_swa), _unheads(dk_swa), _unheads(dv_swa)],
        axis=1).astype(BF16)
    dproj = jnp.concatenate([dproj, dg0, dg1, dg2], axis=1)
    g_in = grad_weight("grad_w_in", h, tok, dproj, pl.BlockSpec((MM_TILE, IN_SHARD), lambda j, k: (k, j)),
                       (D_MODEL, IN_SHARD))
    dx, sums, *df_below = mixer_bwd_input(dproj, w[0], x, dxo, rows.gain, rows.scale, sums, u, below)
    dbias = jnp.concatenate([dbias_dil, dbias_swa], axis=0)
    return dx, sums, df_below, dbias, dsinks[:, :, 0].reshape(H_SWA_Q), (g_in, g_br, g_out)


N_UNITS = 3 * DEPTH


def device_step(x, target, mod, gains, final_gain, sinks, rel_bias, get_weights, put_grads):
    bias_dil, bias_swa, bucket = band_tables(rel_bias)
    mod_table = mod.reshape(3 * N_UNITS, 1, D_MODEL)
    gain_table = gains.reshape(N_UNITS, 1, D_MODEL)
    saved, weights = [], []
    units = [UnitRows(u, mod_table, gain_table) for u in range(N_UNITS)]
    h = prenorm(x, units[0].gain, units[0].scale, units[0].shift)
    for u in range(N_UNITS):
        l, j = divmod(u, 3)
        w = get_weights(u, x)
        then = units[u + 1] if u + 1 < N_UNITS else None
        if j == 1:
            x, h, s = mixer_forward(x, h, units[u], then, sinks[l], bias_dil, bias_swa, w)
        else:
            x, h, s = ffn_forward(x, h, units[u], then, w)
        saved.append(s)
        weights.append(w)
    loss, dx, dfinal = final_loss(x, _vec(final_gain), target)

    sums = (lax.empty((8 * N_UNITS, D_MODEL), F32), lax.empty((8 * N_UNITS, D_MODEL), F32))
    dbias, dsinks = [None] * DEPTH, [None] * DEPTH
    zero = jnp.zeros((1, 1), F32)
    top = N_UNITS - 1
    df, gate_sums = resid_bwd(dx, saved[top][-1], units[top].gate, 0.5, sums[1], top)
    sums = (sums[0], gate_sums)
    for u in reversed(range(N_UNITS)):
        l, j = divmod(u, 3)
        rows = UnitRows(u, mod_table, gain_table + zero)
        below = (saved[u - 1][-1], units[u - 1].gate, 1.0 if (u - 1) % 3 == 1 else 0.5) if u > 0 else None
        if j == 1:
            dx, sums, df, dbias[l], dsinks[l], grads = mixer_backward(
                u, dx, df, saved[u], rows, sinks[l], bias_dil, bias_swa, weights[u], sums, below)
        else:
            dx, sums, df, grads = ffn_backward(u, dx, df, saved[u], rows, weights[u], sums, below)
        df = df[0] if df else None
        if u > 0:
            zero = put_grads(u, grads)
    drel = rel_bias_reduce(dbias[0], dbias[1], bucket)[:, 0, :N_BUCKETS].T
    norm_sums, gate_sums = (t.reshape(DEPTH, 3, 8, D_MODEL) for t in sums)
    dmod = jnp.stack([norm_sums[:, :, 0], norm_sums[:, :, 1], gate_sums[:, :, 0]], axis=2)
    return loss, dx, dmod, norm_sums[:, :, 2], dfinal[0], jnp.stack(dsinks), drel, grads


MESH = pl.DeviceIdType.MESH
CHIP_FLIPS = ((1, 0), (0, 1), (1, 1))
ANY = pl.BlockSpec(memory_space=pl.ANY)


def _position():
    return lax.axis_index("x"), lax.axis_index("y"), lax.axis_index("c")


def all_gather_small(name, piece):
    def body(x_ref, out_ref, send_sems, recv_sems, local_sem):
        x, y, c = _position()
        me, sibling = (x, y, c), (x, y, 1 - c)
        chips = [(x ^ fx, y ^ fy) for fx, fy in CHIP_FLIPS]

        def rows(px, py, pc):
            return out_ref.at[4 * px + 2 * py + pc]

        def copy(k, block, to, src=None):
            return pltpu.make_async_remote_copy(
                src_ref=rows(*block) if src is None else src, dst_ref=rows(*block),
                send_sem=send_sems.at[k], recv_sem=recv_sems.at[k], device_id=to, device_id_type=MESH)

        mine = pltpu.make_async_copy(x_ref, rows(*me), local_sem)
        mine.start()
        first = [copy(0, me, sibling, src=x_ref)]
        first += [copy(1 + j, me, (*chip, c), src=x_ref) for j, chip in enumerate(chips)]
        for cp in first:
            cp.start()
        passed = [copy(4 + j, (*chip, c), sibling) for j, chip in enumerate(chips)]
        for j, chip in enumerate(chips):
            copy(1 + j, (*chip, c), me).wait_recv()
            passed[j].start()
        copy(0, sibling, me).wait_recv()
        for j, chip in enumerate(chips):
            copy(4 + j, (*chip, 1 - c), me).wait_recv()
        for cp in first + passed:
            cp.wait_send()
        mine.wait()

    return pl.pallas_call(
        body, out_shape=jax.ShapeDtypeStruct((N_DEV,) + piece.shape, piece.dtype),
        in_specs=[pl.BlockSpec(memory_space=pltpu.VMEM)], out_specs=pl.BlockSpec(memory_space=pltpu.VMEM),
        scratch_shapes=[pltpu.SemaphoreType.DMA((7,)), pltpu.SemaphoreType.DMA((7,)), pltpu.SemaphoreType.DMA],
        name=name,
    )(piece)


def exchange(name, operands, out_shapes, aliases, plan):
    n_in, n_out = len(operands), len(out_shapes)

    def body(*refs):
        ins, outs = refs[:n_in], refs[n_in:n_in + n_out]
        send_sems, recv_sems, local_sems = refs[n_in + n_out:]
        x, y, c = _position()
        local, sends, recvs = plan(ins, outs, x, y, c)
        local = [pltpu.make_async_copy(s, d, local_sems.at[k]) for k, (s, d) in enumerate(local)]
        for cp in local:
            cp.start()
        remote = [pltpu.make_async_remote_copy(src_ref=s, dst_ref=d, send_sem=send_sems.at[k], recv_sem=recv_sems.at[k],
                                               device_id=dev, device_id_type=MESH)
                  for k, (s, d, dev) in enumerate(sends)]
        for cp in remote:
            cp.start()
        for k, r in enumerate(recvs):
            pltpu.make_async_remote_copy(src_ref=r, dst_ref=r, send_sem=send_sems.at[k], recv_sem=recv_sems.at[k],
                                         device_id=(x, y, c), device_id_type=MESH).wait_recv()
        for cp in remote:
            cp.wait_send()
        for cp in local:
            cp.wait()

    n_sends, n_local = plan.n_sends, max(plan.n_local, 1)
    return pl.pallas_call(
        body, out_shape=out_shapes, in_specs=[ANY] * n_in, out_specs=[ANY] * n_out,
        scratch_shapes=[pltpu.SemaphoreType.DMA((n_sends,)), pltpu.SemaphoreType.DMA((n_sends,)),
                        pltpu.SemaphoreType.DMA((n_local,))],
        input_output_aliases=aliases, name=name,
    )(*operands)


def _plan(n_local, n_sends):
    def wrap(fn):
        fn.n_local, fn.n_sends = n_local, n_sends
        return fn
    return wrap


def _half(ref, axis, c):
    rows = ref.shape[axis] // 2
    idx = [slice(None)] * len(ref.shape)
    idx[axis] = pl.ds(pl.multiple_of(c * rows, 16), rows)
    return ref.at[tuple(idx)]


HBM = pl.BlockSpec(memory_space=pltpu.HBM)
SEM = pl.BlockSpec(memory_space=pltpu.SEMAPHORE)
EFFECT = pltpu.SideEffectType.DATAFLOW_SIDE_EFFECTING


def split_start(name, bufs, extra, n_copies, describe):
    n = len(bufs)

    def body(*refs):
        send_sems, recv_sems = refs[n + len(extra)], refs[n + len(extra) + 1]
        x, y, c = _position()
        for k, (src, dst, _, peer) in enumerate(describe(refs[:n], x, y, c)):
            pltpu.make_async_remote_copy(src_ref=src, dst_ref=dst, send_sem=send_sems.at[k], recv_sem=recv_sems.at[k],
                                         device_id=peer, device_id_type=MESH).start()
        token = refs[-1]
        token[...] = jnp.zeros_like(token)

    out = pl.pallas_call(
        body, name=name,
        out_shape=(pltpu.SemaphoreType.DMA((n_copies,)), pltpu.SemaphoreType.DMA((n_copies,)),
                   *[pltpu.HBM(b.shape, b.dtype) for b in bufs], jax.ShapeDtypeStruct((8, 128), F32)),
        in_specs=[HBM] * n + [ANY] * len(extra),
        out_specs=(SEM, SEM, *[HBM] * n, pl.BlockSpec(memory_space=pltpu.VMEM)),
        input_output_aliases={k: 2 + k for k in range(n)},
        compiler_params=pltpu.CompilerParams(has_side_effects=EFFECT),
    )(*[pltpu.with_memory_space_constraint(b, pltpu.HBM) for b in bufs], *extra)
    return out[0], out[1], list(out[2:2 + n]), out[-1]


def split_wait(name, bufs, send_sems, recv_sems, after, describe):
    n = len(bufs)

    def body(*refs):
        send, recv = refs[n], refs[n + 1]
        x, y, c = _position()
        for k, (src, _, dst, peer) in enumerate(describe(refs[:n], x, y, c)):
            copy = pltpu.make_async_remote_copy(src_ref=src, dst_ref=dst, send_sem=send.at[k], recv_sem=recv.at[k],
                                                device_id=peer, device_id_type=MESH)
            copy.wait_send()
            copy.wait_recv()

    out = pl.pallas_call(
        body, name=name, out_shape=[pltpu.HBM(b.shape, b.dtype) for b in bufs],
        in_specs=[HBM] * n + [SEM, SEM] + [ANY] * len(after), out_specs=[HBM] * n,
        input_output_aliases={k: k for k in range(n)},
        compiler_params=pltpu.CompilerParams(has_side_effects=EFFECT),
    )(*bufs, send_sems, recv_sems, *after)
    return list(out)


def _row_tile(rows, cols, max_elements=256 * 1024):
    best = 16
    for t in range(16, rows + 1, 16):
        if rows % t == 0 and t * cols <= max_elements:
            best = t
    return best


def cast_into_slots(name, shards, chip):
    n = len(shards)
    rows, cols = shards[0][0].shape[-2:]
    tr = _row_tile(rows, cols)

    def body(chip_ref, *refs):
        del chip_ref
        for k in range(n):
            refs[n + k][...] = refs[k][...].astype(BF16)

    def in_spec(param, index):
        return pl.BlockSpec((None,) * len(index) + (tr, cols), lambda r, chip_ref: index + (r, 0))

    return pl.pallas_call(
        body, out_shape=[jax.ShapeDtypeStruct((N_CHIPS, rows, cols), BF16)] * n,
        grid_spec=pltpu.PrefetchScalarGridSpec(
            num_scalar_prefetch=1, grid=(rows // tr,),
            in_specs=[in_spec(p, idx) for p, idx in shards],
            out_specs=[pl.BlockSpec((None, tr, cols), lambda r, chip_ref: (chip_ref[0], r, 0))] * n),
        name=name, compiler_params=_params(("parallel",)),
    )(chip, *[p for p, _ in shards])


GATHER_STAGES = ((0,), (1,), (2,), (3, 4, 5))
REDUCE_STAGES = ((5, 4, 3), (2,), (1,), (0,))


def _gather_copies(slots, x, y, c):
    me = 2 * x + y
    out = []
    for s in slots:
        for fx, fy in CHIP_FLIPS:
            mine = _half(s.at[me], 0, c)
            out.append((mine, mine, _half(s.at[2 * (x ^ fx) + (y ^ fy)], 0, c), (x ^ fx, y ^ fy, c)))
    return out


class WeightStream:
    def __init__(self, shards, chip, after=()):
        self.pending, self.ready = {}, {}
        token = tuple(after)
        for si, units in enumerate(GATHER_STAGES):
            slots = []
            for u in units:
                same = len({p.shape[-2:] for p, _ in shards[u]}) == 1
                for t, group in enumerate([shards[u]] if same else [[s] for s in shards[u]]):
                    slots += cast_into_slots(f"cast_{u}_{t}", group, chip)
            send, recv, slots, tok = split_start(f"gather_start_{si}", slots, token, 3 * len(slots), _gather_copies)
            self.pending[si] = (send, recv, slots)
            token = (tok,)
        self.token = token

    def get(self, u, after):
        if u not in self.ready:
            si = next(k for k, units in enumerate(GATHER_STAGES) if u in units)
            send, recv, slots = self.pending.pop(si)
            slots = split_wait(f"gather_wait_{si}", slots, send, recv, (after,) + self.token, _gather_copies)
            self.token = ()

            @_plan(0, 3 * len(slots))
            def to_sibling(ins, outs, x, y, c):
                sends, recvs = [], []
                for o in outs:
                    for fx, fy in CHIP_FLIPS:
                        slab = o.at[2 * (x ^ fx) + (y ^ fy)]
                        sends.append((_half(slab, 0, c), _half(slab, 0, c), (x, y, 1 - c)))
                        recvs.append(_half(slab, 0, 1 - c))
                return [], sends, recvs

            shapes = [jax.ShapeDtypeStruct(s.shape, BF16) for s in slots]
            slots = exchange(f"gather_sibling_{si}", slots, shapes, {k: k for k in range(len(slots))}, to_sibling)
            for i, v in enumerate(GATHER_STAGES[si]):
                self.ready[v] = tuple(slots[3 * i:3 * i + 3])
        return self.ready[u]


def _reduce_copies(bufs, x, y, c):
    n = len(bufs) // 2
    out = []
    for s, land in zip(bufs[:n], bufs[n:]):
        for k, (fx, fy) in enumerate(CHIP_FLIPS):
            out.append((s.at[2 * (x ^ fx) + (y ^ fy)], land.at[k], land.at[k], (x ^ fx, y ^ fy, c)))
    return out


GRAD_SLOTS = {"gate": (2 * DEPTH, FF_SHARD, D_MODEL), "up": (2 * DEPTH, FF_SHARD, D_MODEL),
              "down": (2 * DEPTH, FF_SHARD, D_MODEL), "in": (DEPTH, D_MODEL, IN_SHARD),
              "br": (DEPTH, BR_ROWS, D_SHARD), "out": (DEPTH, D_SHARD, D_MODEL)}


def _unit_tensors(u):
    l, j = divmod(u, 3)
    if j == 1:
        return [("in", l), ("br", l), ("out", l)]
    return [(k, 2 * l + j // 2) for k in ("gate", "up", "down")]


class GradStream:
    def __init__(self, chip, core):
        self.core = core
        self.place = jnp.concatenate([chip, core])
        self.held, self.flying = {}, []
        self.full = {k: lax.empty(shape, F32) for k, shape in GRAD_SLOTS.items()}

    def put(self, u, grads, after=()):
        self.held[u] = grads
        si = len(self.flying)
        units = REDUCE_STAGES[si]
        if not all(v in self.held for v in units):
            return jnp.zeros((1, 1), F32)
        gs = [g for v in units for g in self.held[v]]

        @_plan(0, len(gs))
        def swap_halves(ins, outs, x, y, c):
            sends = [(_half(g, 1, 1 - c), o, (x, y, 1 - c)) for g, o in zip(ins, outs)]
            return [], sends, list(outs)

        half_shapes = [jax.ShapeDtypeStruct((N_CHIPS, g.shape[1] // 2, g.shape[2]), BF16) for g in gs]
        landed = exchange(f"reduce_swap_{si}", gs + list(after), half_shapes, {}, swap_halves)
        sums = [None] * len(gs)
        for run in _same_shape_runs(gs):
            for k, s in zip(run, _add_halves([gs[k] for k in run], [landed[k] for k in run], self.core)):
                sums[k] = s
        landing = [lax.empty((3,) + s.shape[1:], BF16) for s in sums]
        send, recv, bufs, token = split_start(f"reduce_start_{si}", sums + landing, (), 3 * len(sums), _reduce_copies)
        self.flying.append((send, recv, bufs, [t for v in units for t in _unit_tensors(v)]))
        return token[0:1, 0:1]

    def finish(self, after):
        for si, (send, recv, bufs, tensors) in enumerate(self.flying):
            bufs = split_wait(f"reduce_wait_{si}", bufs, send, recv, tuple(after), _reduce_copies)
            n = len(tensors)
            for run in _same_shape_runs(bufs[:n]):
                names = [tensors[k][0] for k in run]
                out = _add_chips([bufs[k] for k in run], [bufs[n + k] for k in run], self.place,
                                 [self.full[t] for t in names], [tensors[k][1] for k in run])
                self.full.update(zip(names, out))
        names = list(self.full)

        @_plan(0, len(names))
        def share_halves(ins, outs, x, y, c):
            sends = [(_half(o, 1, c), _half(o, 1, c), (x, y, 1 - c)) for o in outs]
            return [], sends, [_half(o, 1, 1 - c) for o in outs]

        shapes = [jax.ShapeDtypeStruct(self.full[k].shape, F32) for k in names]
        out = exchange("reduce_share_halves", [self.full[k] for k in names], shapes, {k: k for k in range(len(names))},
                       share_halves)
        return dict(zip(names, out))


def _same_shape_runs(arrays, longest=3):
    runs = []
    for k, a in enumerate(arrays):
        if runs and len(runs[-1]) < longest and arrays[runs[-1][0]].shape == a.shape:
            runs[-1].append(k)
        else:
            runs.append([k])
    return runs


def _add_halves(gs, landeds, core):
    n = len(gs)
    _, rh, cols = landeds[0].shape
    tr = _row_tile(rh, cols, 1024 * 1024)
    per_half = rh // tr

    def body(core_ref, *refs):
        del core_ref
        for k in range(n):
            refs[2 * n + k][...] = (refs[k][...].astype(F32) + refs[n + k][...].astype(F32)).astype(BF16)

    blk = (None, tr, cols)
    landed_spec = pl.BlockSpec(blk, lambda j, r, core_ref: (j, r, 0))
    return pl.pallas_call(
        body, out_shape=[jax.ShapeDtypeStruct(landeds[0].shape, BF16)] * n,
        grid_spec=pltpu.PrefetchScalarGridSpec(
            num_scalar_prefetch=1, grid=(N_CHIPS, per_half),
            in_specs=[pl.BlockSpec(blk, lambda j, r, core_ref: (j, core_ref[0] * per_half + r, 0))] * n
            + [landed_spec] * n,
            out_specs=[landed_spec] * n),
        name="reduce_add_halves", compiler_params=_params(("parallel", "parallel")),
    )(core, *gs, *landeds)


def _add_chips(sums, landeds, place, fulls, slots):
    n = len(sums)
    _, rh, cols = sums[0].shape
    tr = _row_tile(rh, cols, 1024 * 1024)
    per_half = rh // tr

    def body(place_ref, *refs):
        del place_ref
        for k in range(n):
            s_ref, la_ref, o_ref = refs[k], refs[n + k], refs[3 * n + k]
            o_ref[...] = ((s_ref[...].astype(F32) + la_ref[0].astype(F32)) + la_ref[1].astype(F32)) + la_ref[2].astype(F32)

    def out_spec(slot):
        return pl.BlockSpec((None, tr, cols), lambda r, place_ref: (slot, place_ref[1] * per_half + r, 0))

    return pl.pallas_call(
        body, out_shape=[jax.ShapeDtypeStruct(f.shape, F32) for f in fulls],
        grid_spec=pltpu.PrefetchScalarGridSpec(
            num_scalar_prefetch=1, grid=(per_half,),
            in_specs=[pl.BlockSpec((None, tr, cols), lambda r, place_ref: (place_ref[0], r, 0))] * n
            + [pl.BlockSpec((3, tr, cols), lambda r, place_ref: (0, r, 0))] * n + [ANY] * n,
            out_specs=[out_spec(slot) for slot in slots]),
        input_output_aliases={1 + 2 * n + k: k for k in range(n)}, name="reduce_add_chips",
        compiler_params=_params(("parallel",)),
    )(place, *sums, *landeds, *fulls)


def sum_devices(parts):
    def body(p_ref, o_ref):
        acc = p_ref[0]
        for d in range(1, N_DEV):
            acc = acc + p_ref[d]
        o_ref[...] = acc

    return pl.pallas_call(body, out_shape=jax.ShapeDtypeStruct(parts.shape[1:], F32), name="sum_devices")(parts)


ADA_SHARD = 9 * D_MODEL // N_CHIPS
ADA_TILE = 768
ADA_ROWS = 16


def ada_forward(c_rows, w_ada, b_shard):
    def body(c_ref, w_ref, b_ref, o_ref):
        cv = c_ref[...]
        o_ref[...] = _dot((cv * _sigmoid(cv)).astype(BF16), w_ref[...].astype(BF16), NN) + b_ref[...]

    return pl.pallas_call(
        body, grid=(DEPTH, ADA_SHARD // ADA_TILE),
        in_specs=[pl.BlockSpec((ADA_ROWS, D_MODEL), lambda l, n: (0, 0)),
                  pl.BlockSpec((None, D_MODEL, ADA_TILE), lambda l, n: (l, 0, n)),
                  pl.BlockSpec((None, 1, ADA_TILE), lambda l, n: (l, 0, n))],
        out_specs=pl.BlockSpec((None, ADA_ROWS, ADA_TILE), lambda l, n: (l, 0, n)),
        out_shape=jax.ShapeDtypeStruct((DEPTH, ADA_ROWS, ADA_SHARD), F32),
        name="ada_forward", compiler_params=_params(("parallel", "parallel")),
    )(c_rows, w_ada, b_shard)


def ada_backward(c_rows, dmod_rows):
    def body(c_ref, d_ref, o_ref):
        cv = c_ref[...]
        o_ref[...] = _dot((cv * _sigmoid(cv)).astype(BF16), d_ref[...].astype(BF16), TN)

    return pl.pallas_call(
        body, grid=(DEPTH, ADA_SHARD // ADA_TILE),
        in_specs=[pl.BlockSpec((ADA_ROWS, D_MODEL), lambda l, n: (0, 0)),
                  pl.BlockSpec((None, ADA_ROWS, ADA_TILE), lambda l, n: (l, 0, n))],
        out_specs=pl.BlockSpec((None, D_MODEL, ADA_TILE), lambda l, n: (l, 0, n)),
        out_shape=jax.ShapeDtypeStruct((DEPTH, D_MODEL, ADA_SHARD), F32),
        name="ada_backward", compiler_params=_params(("parallel", "parallel")),
    )(c_rows, dmod_rows)


def adamw(name, w, g, m, v):
    shape = w.shape
    cols = shape[-1]
    rows = w.size // cols
    tr = _row_tile(rows, cols) if rows % 16 == 0 else rows
    c1 = 1.0 / (1.0 - ADAM_B1 ** ADAM_STEP)
    c2 = 1.0 / (1.0 - ADAM_B2 ** ADAM_STEP)

    def body(w_ref, g_ref, m_ref, v_ref, go_ref, d_ref, mo_ref, vo_ref):
        gv = g_ref[...]
        mn = ADAM_B1 * m_ref[...] + (1.0 - ADAM_B1) * gv
        vn = ADAM_B2 * v_ref[...] + (1.0 - ADAM_B2) * (gv * gv)
        go_ref[...] = gv
        mo_ref[...] = mn
        vo_ref[...] = vn
        d_ref[...] = -ADAM_LR * ((mn * c1) / (jnp.sqrt(vn * c2) + ADAM_EPS) + ADAM_WD * w_ref[...])

    spec = pl.BlockSpec((tr, cols), lambda i: (i, 0))
    out = jax.ShapeDtypeStruct((rows, cols), F32)
    res = pl.pallas_call(
        body, grid=(rows // tr,), in_specs=[spec] * 4, out_specs=[spec] * 4, out_shape=[out] * 4,
        name=name, compiler_params=_params(("parallel",)),
    )(*[t.reshape(rows, cols) for t in (w, g, m, v)])
    return tuple(r.reshape(shape) for r in res)


def _pack(parts, rows):
    flat = jnp.concatenate([p.reshape(-1) for p in parts])
    return jnp.pad(flat, (0, rows * 128 - flat.size)).reshape(rows, 128)


def _unpack(flat, shapes):
    out, at = [], 0
    for s in shapes:
        n = math.prod(s)
        out.append(flat[at:at + n].reshape(s))
        at += n
    return out


def kernel(x, c, w_ada, b_ada, norm_gain, w_ffn_gate, w_ffn_up, w_ffn_down, w_in, w_br_sb, w_br_dil, w_br_swa, w_out, sinks, rel_bias, final_gain, loss_target, m_w_ada, m_b_ada, m_norm_gain, m_w_ffn_gate, m_w_ffn_up, m_w_ffn_down, m_w_in, m_w_br_sb, m_w_br_dil, m_w_br_swa, m_w_out, m_sinks, m_rel_bias, m_final_gain, v_w_ada, v_b_ada, v_norm_gain, v_w_ffn_gate, v_w_ffn_up, v_w_ffn_down, v_w_in, v_w_br_sb, v_w_br_dil, v_w_br_swa, v_w_out, v_sinks, v_rel_bias, v_final_gain):
    xi, yi, ci = _position()
    chip = 2 * xi + yi
    dev = 2 * chip + ci

    c_all = all_gather_small("gather_c", c.reshape(8, 128)).reshape(N_DEV, D_MODEL)
    c_rows = jnp.pad(c_all, ((0, ADA_ROWS - N_DEV), (0, 0)))
    b_shard = lax.dynamic_slice_in_dim(b_ada, chip * ADA_SHARD, ADA_SHARD, axis=1).reshape(DEPTH, 1, ADA_SHARD)
    mod_shard = ada_forward(c_rows, w_ada, b_shard)[:, :N_DEV]
    n_mod = DEPTH * N_DEV * ADA_SHARD
    gathered = all_gather_small("gather_mod", _pack([mod_shard, norm_gain], 304))[::2].reshape(N_CHIPS, -1)
    mod_all = gathered[:, :n_mod].reshape(N_CHIPS, DEPTH, N_DEV, ADA_SHARD)
    mod = lax.dynamic_index_in_dim(mod_all, dev, axis=2, keepdims=False)
    mod = mod.transpose(1, 0, 2).reshape(DEPTH, 3, 3, D_MODEL)
    gains = gathered[:, n_mod:n_mod + DEPTH * 3 * D_SHARD].reshape(N_CHIPS, DEPTH, 3, D_SHARD)
    gains = gains.transpose(1, 2, 0, 3).reshape(DEPTH, 3, D_MODEL)

    chip_i, core_i = chip.astype(jnp.int32).reshape(1), ci.astype(jnp.int32).reshape(1)
    w_br = jnp.concatenate([w_br_sb, w_br_dil, w_br_swa], axis=1)
    transposed = (3, 4)
    w_gate_t, w_up_t = jnp.swapaxes(w_ffn_gate, 2, 3), jnp.swapaxes(w_ffn_up, 2, 3)
    shards = []
    for l in range(DEPTH):
        ffn = [[(w_gate_t, (l, f)), (w_up_t, (l, f)), (w_ffn_down, (l, f))] for f in range(2)]
        shards += [ffn[0], [(w_in, (l,)), (w_br, (l,)), (w_out, (l,))], ffn[1]]
    weights_in = WeightStream(shards, chip_i, (gathered,))
    grads_out = GradStream(chip_i, core_i)

    loss, dx, dmod, dgains, dfinal, dsinks, drel, last_grads = device_step(
        x[0], loss_target[0], mod, gains, final_gain, sinks, rel_bias, weights_in.get, grads_out.put)

    small_shapes = [(DEPTH, 9 * D_MODEL), (DEPTH, 3, D_MODEL), (D_MODEL,), (DEPTH, H_SWA_Q), (N_BUCKETS, 12), (1,)]
    small_all = all_gather_small("gather_small_grads", _pack([dmod, dgains, dfinal, dsinks, drel, loss[0, 0:1]], 208))
    started = grads_out.put(0, last_grads, after=(small_all,))
    small_all = small_all + started
    g_b_ada, g_gain_full, g_final, g_sinks, g_rel, loss_sum = _unpack(sum_devices(small_all).reshape(-1), small_shapes)
    g_gain = lax.dynamic_slice_in_dim(g_gain_full, chip * D_SHARD, D_SHARD, axis=2)
    dmod_all = small_all.reshape(N_DEV, -1)[:, :DEPTH * 9 * D_MODEL].reshape(N_DEV, DEPTH, 9 * D_MODEL)
    dmod_rows = lax.dynamic_slice_in_dim(dmod_all, chip * ADA_SHARD, ADA_SHARD, axis=2).transpose(1, 0, 2)
    g_w_ada = ada_backward(c_rows, jnp.pad(dmod_rows, ((0, 0), (0, ADA_ROWS - N_DEV), (0, 0))))

    weights = [w_ada, b_ada, norm_gain, w_ffn_gate, w_ffn_up, w_ffn_down, w_in, w_br_sb, w_br_dil, w_br_swa, w_out,
               sinks, rel_bias, final_gain]
    ms = [m_w_ada, m_b_ada, m_norm_gain, m_w_ffn_gate, m_w_ffn_up, m_w_ffn_down, m_w_in, m_w_br_sb, m_w_br_dil,
          m_w_br_swa, m_w_out, m_sinks, m_rel_bias, m_final_gain]
    vs = [v_w_ada, v_b_ada, v_norm_gain, v_w_ffn_gate, v_w_ffn_up, v_w_ffn_down, v_w_in, v_w_br_sb, v_w_br_dil,
          v_w_br_swa, v_w_out, v_sinks, v_rel_bias, v_final_gain]
    grads = [g_w_ada, g_b_ada, g_gain] + [None] * 8 + [g_sinks, g_rel, g_final]

    deltas, new_ms, new_vs = [None] * 14, [None] * 14, [None] * 14
    for k in (0, 1, 2, 11, 12, 13):
        _, deltas[k], new_ms[k], new_vs[k] = adamw(f"adamw_{k}", weights[k], grads[k], ms[k], vs[k])

    g = grads_out.finish((dx, deltas[0], deltas[1]))
    g_br = g["br"]
    grads[3:11] = [g["gate"].reshape(w_gate_t.shape), g["up"].reshape(w_up_t.shape),
                   g["down"].reshape(w_ffn_down.shape), g["in"], g_br[:, 0:256], g_br[:, 256:384], g_br[:, 384:768],
                   g["out"]]
    for k in range(3, 11):
        state = [weights[k], ms[k], vs[k]]
        if k in transposed:
            state = [jnp.swapaxes(t, 2, 3) for t in state]
        out = adamw(f"adamw_{k}", state[0], grads[k], state[1], state[2])
        if k in transposed:
            out = [jnp.swapaxes(t, 2, 3) for t in out]
        grads[k], deltas[k], new_ms[k], new_vs[k] = out
    return (loss_sum[0], dx[None], *grads, *deltas, *new_ms, *new_vs)
```

```python
import functools
import math

import jax
import jax.numpy as jnp
from jax import lax
from jax.experimental import pallas as pl
from jax.experimental.pallas import tpu as pltpu

F32 = jnp.float32
BF16 = jnp.bfloat16

D_MODEL = 1024
SEQ = 2048
DEPTH = 2
HEAD_DIM = 64
BLK = 128
H_SB = 4
DIL_PATTERNS = ((128, 1), (512, 4), (2048, 16))
H_PER_DIL = 2
H_DIL = 6
H_SWA_Q = 6
H_SWA_KV = 2
SWA_WINDOW = 128
N_BUCKETS = 32
MAX_REL_DIST = 2048
D_FF = 2816
RMS_EPS = 1e-6
N_CHIPS = 4
N_DEV = 8
FF_SHARD = D_FF // N_CHIPS
D_QKV = 2560
D_IN = D_QKV + 3 * D_MODEL
IN_SHARD = D_IN // N_CHIPS
D_SHARD = D_MODEL // N_CHIPS
BR_ROWS = 768
NEG = -1e30
QK_SCALE = HEAD_DIM ** -0.5

ADAM_LR = 0.001
ADAM_B1 = 0.9
ADAM_B2 = 0.999
ADAM_EPS = 1e-08
ADAM_WD = 0.01
ADAM_STEP = 10

VMEM_LIMIT = 56 * 1024 * 1024
ROW_TILE = 256
MM_TILE = 1024

NN = (((1,), (0,)), ((), ()))
NT = (((1,), (1,)), ((), ()))
TN = (((0,), (0,)), ((), ()))


def _params(sem=None):
    return pltpu.CompilerParams(dimension_semantics=sem, vmem_limit_bytes=VMEM_LIMIT)


def _dot(a, b, dims):
    return lax.dot_general(a, b, dims, preferred_element_type=F32)


def _sigmoid(x):
    return 1.0 / (1.0 + jnp.exp(-x))


def _matmul(name, grid, nk, k_axis, dims, n_pairs, in_specs, out_specs, out_shape, acc_shape, epilogue,
            operands, sem, aliases=None, prologue=None):
    n_in = len(in_specs)
    n_out = len(out_specs)

    def partial(ins):
        tot = None
        for p in range(n_pairs):
            a = ins[2 * p][...]
            if prologue is not None:
                a = prologue(p, a, ins)
            d = _dot(a, ins[2 * p + 1][...], dims)
            tot = d if tot is None else tot + d
        return tot

    def body(*refs):
        ins, outs = refs[:n_in], refs[n_in:n_in + n_out]
        ids = tuple(pl.program_id(a) for a in range(len(grid)))
        if nk == 1:
            epilogue(partial(ins), ins, outs, ids)
            return
        acc = refs[n_in + n_out]
        k = ids[k_axis]

        @pl.when(k == 0)
        def _():
            acc[...] = partial(ins)

        @pl.when(k > 0)
        def _():
            acc[...] += partial(ins)

        @pl.when(k == nk - 1)
        def _():
            epilogue(acc[...], ins, outs, ids)

    return pl.pallas_call(
        body, grid=grid, in_specs=in_specs, out_specs=out_specs, out_shape=out_shape,
        scratch_shapes=[] if nk == 1 else [pltpu.VMEM(acc_shape, F32)],
        input_output_aliases=aliases or {}, name=name, compiler_params=_params(sem),
    )(*operands)


def _row_spec(width=D_MODEL):
    return pl.BlockSpec((ROW_TILE, width), lambda i: (i, 0))


def _vec_spec(rows=1, width=D_MODEL):
    return pl.BlockSpec((rows, width), lambda i: (0, 0))


class Row:
    def __init__(self, table, index):
        self.table, self.index = table, index

    def spec(self):
        index = self.index
        return pl.BlockSpec((None, 1, D_MODEL), lambda *ids: (index, 0, 0))


def _slot_spec(u):
    return pl.BlockSpec((8, D_MODEL), lambda *ids: (u, 0))


def prenorm(x, gain, scale, shift):
    def body(x_ref, g_ref, sc_ref, sh_ref, h_ref):
        xv = x_ref[...]
        r = lax.rsqrt(jnp.mean(xv * xv, axis=-1, keepdims=True) + RMS_EPS)
        h_ref[...] = (((xv * r) * g_ref[...]) * (1.0 + sc_ref[...]) + sh_ref[...]).astype(BF16)

    return pl.pallas_call(
        body, grid=(SEQ // ROW_TILE,), in_specs=[_row_spec(), gain.spec(), scale.spec(), shift.spec()],
        out_specs=_row_spec(), out_shape=jax.ShapeDtypeStruct((SEQ, D_MODEL), BF16),
        name="prenorm", compiler_params=_params(("parallel",)),
    )(x, gain.table, scale.table, shift.table)


def resid_bwd(dxo, f, coef, mult, sums, u):
    def body(dx_ref, f_ref, c_ref, sums_in, df_ref, dc_ref):
        del sums_in
        dx = dx_ref[...]
        df_ref[...] = (dx * (mult * c_ref[...])).astype(BF16)
        part = mult * jnp.sum(dx * f_ref[...], axis=0, keepdims=True)

        @pl.when(pl.program_id(0) == 0)
        def _():
            dc_ref[...] = jnp.zeros_like(dc_ref)

        dc_ref[0:1, :] += part

    return pl.pallas_call(
        body, grid=(SEQ // ROW_TILE,),
        in_specs=[_row_spec(), _row_spec(), coef.spec(), pl.BlockSpec(memory_space=pl.ANY)],
        out_specs=[_row_spec(), _slot_spec(u)],
        out_shape=[jax.ShapeDtypeStruct((SEQ, D_MODEL), BF16), jax.ShapeDtypeStruct(sums.shape, F32)],
        input_output_aliases={3: 1}, name="resid_bwd", compiler_params=_params(("arbitrary",)),
    )(dxo, f, coef.table, sums)


def final_loss(x, gain, target):
    def body(x_ref, g_ref, t_ref, loss_ref, dx_ref, dg_ref):
        xv = x_ref[...]
        g = g_ref[...]
        r = lax.rsqrt(jnp.mean(xv * xv, axis=-1, keepdims=True) + RMS_EPS)
        xh = xv * r
        e = xh * g - t_ref[...]
        part = 0.5 * jnp.sum(jnp.mean(e * e, axis=-1, keepdims=True), axis=0, keepdims=True)
        dy = e * (1.0 / D_MODEL)
        dyg = dy * g
        dx_ref[...] = r * (dyg - xh * jnp.mean(dyg * xh, axis=-1, keepdims=True))

        @pl.when(pl.program_id(0) == 0)
        def _():
            loss_ref[...] = jnp.zeros_like(loss_ref)
            dg_ref[...] = jnp.zeros_like(dg_ref)

        loss_ref[...] += jnp.broadcast_to(part, loss_ref.shape)
        dg_ref[0:1, :] += jnp.sum(dy * xh, axis=0, keepdims=True)

    return pl.pallas_call(
        body, grid=(SEQ // ROW_TILE,), in_specs=[_row_spec(), _vec_spec(), _row_spec()],
        out_specs=[_vec_spec(8, 128), _row_spec(), _vec_spec(8)],
        out_shape=[jax.ShapeDtypeStruct((8, 128), F32), jax.ShapeDtypeStruct((SEQ, D_MODEL), F32),
                   jax.ShapeDtypeStruct((8, D_MODEL), F32)],
        name="final_loss", compiler_params=_params(("arbitrary",)),
    )(x, gain, target)


def _prenorm_bwd_epilogue(dh, x_ref, dxo_ref, g_ref, sc_ref, dx_ref, stats_ref, first):
    xv = x_ref[...]
    g = g_ref[...]
    r = lax.rsqrt(jnp.mean(xv * xv, axis=-1, keepdims=True) + RMS_EPS)
    xh = xv * r
    dn = dh * (1.0 + sc_ref[...])
    dxh = dn * g
    dx = dxo_ref[...] + r * (dxh - xh * jnp.mean(dxh * xh, axis=-1, keepdims=True))
    dx_ref[...] = dx

    @pl.when(first)
    def _():
        stats_ref[...] = jnp.zeros_like(stats_ref)

    stats_ref[0:1, :] += jnp.sum(dh, axis=0, keepdims=True)
    stats_ref[1:2, :] += jnp.sum(dh * (xh * g), axis=0, keepdims=True)
    stats_ref[2:3, :] += jnp.sum(dn * xh, axis=0, keepdims=True)
    return dx


def _resid_bwd_epilogue(dx, f_ref, c_ref, mult, df_ref, dc_ref, first):
    df_ref[...] = (dx * (mult * c_ref[...])).astype(BF16)

    @pl.when(first)
    def _():
        dc_ref[...] = jnp.zeros_like(dc_ref)

    dc_ref[0:1, :] += mult * jnp.sum(dx * f_ref[...], axis=0, keepdims=True)


def ffn_up(h, wg_all, wu_all):
    def body(h_ref, wg_ref, wu_ref, a_ref, b_ref, s_ref):
        hv = h_ref[...]
        a = _dot(hv, wg_ref[...], NT)
        b = _dot(hv, wu_ref[...], NT)
        a_ref[...] = a.astype(BF16)
        b_ref[...] = b.astype(BF16)
        s_ref[...] = (a * _sigmoid(a) * b).astype(BF16)

    w_spec = pl.BlockSpec((None, FF_SHARD, D_MODEL), lambda j, i: (j, 0, 0))
    o_spec = pl.BlockSpec((None, MM_TILE, FF_SHARD), lambda j, i: (j, i, 0))
    hid = (N_CHIPS, SEQ, FF_SHARD)
    return pl.pallas_call(
        body, grid=(N_CHIPS, SEQ // MM_TILE),
        in_specs=[pl.BlockSpec((MM_TILE, D_MODEL), lambda j, i: (i, 0)), w_spec, w_spec],
        out_specs=[o_spec, o_spec, o_spec],
        out_shape=[jax.ShapeDtypeStruct(hid, BF16)] * 3,
        name="ffn_up", compiler_params=_params(("parallel", "parallel")),
    )(h, wg_all, wu_all)


def matmul_residual(name, a, a_spec, take, w_all, x, coef, mult, then=None):
    n_rows = 0 if then is None else 3

    def body(a_ref, w_hbm, x_ref, c_ref, *rest):
        rows, (f_ref, xo_ref), w_vmem, sem = rest[:n_rows], rest[n_rows:n_rows + 2], rest[-2], rest[-1]
        h_ref = rest[n_rows + 2] if then is not None else None

        @pl.when(pl.program_id(0) == 0)
        def _():
            copy = pltpu.make_async_copy(w_hbm, w_vmem, sem)
            copy.start()
            copy.wait()

        acc = None
        for j in range(N_CHIPS):
            part = _dot(take(a_ref, j), w_vmem[j], NN)
            acc = part if acc is None else acc + part
        f_ref[...] = acc
        xo = x_ref[...] + (mult * c_ref[...]) * acc
        xo_ref[...] = xo
        if then is not None:
            r = lax.rsqrt(jnp.mean(xo * xo, axis=-1, keepdims=True) + RMS_EPS)
            h_ref[...] = (((xo * r) * rows[0][...]) * (1.0 + rows[1][...]) + rows[2][...]).astype(BF16)

    row = pl.BlockSpec((MM_TILE, D_MODEL), lambda i: (i, 0))
    f32 = jax.ShapeDtypeStruct((SEQ, D_MODEL), F32)
    extra = [] if then is None else [then.gain, then.scale, then.shift]
    return pl.pallas_call(
        body, grid=(SEQ // MM_TILE,),
        in_specs=[a_spec, pl.BlockSpec(memory_space=pl.ANY), row, coef.spec()] + [t.spec() for t in extra],
        out_specs=[row] * (2 + bool(extra)),
        out_shape=[f32, f32] + [jax.ShapeDtypeStruct((SEQ, D_MODEL), BF16)] * bool(extra),
        scratch_shapes=[pltpu.VMEM(w_all.shape, w_all.dtype), pltpu.SemaphoreType.DMA],
        name=name, compiler_params=_params(("arbitrary",)),
    )(a, w_all, x, coef.table, *[t.table for t in extra])


def ffn_down(s, wd_all, x, gate, then):
    return matmul_residual(
        "ffn_down", s, pl.BlockSpec((N_CHIPS, MM_TILE, FF_SHARD), lambda i: (0, i, 0)), lambda ref, j: ref[j],
        wd_all, x, gate, 0.5, then)


def ffn_bwd_hidden(df, wd_all, a, b):
    def epilogue(ds, ins, outs, ids):
        av, bv = ins[2][...].astype(F32), ins[3][...].astype(F32)
        sig = _sigmoid(av)
        outs[0][...] = (ds * bv * (sig * (1.0 + av * (1.0 - sig)))).astype(BF16)
        outs[1][...] = (ds * (av * sig)).astype(BF16)

    hid_spec = pl.BlockSpec((None, MM_TILE, FF_SHARD), lambda j, i: (j, i, 0))
    hid = jax.ShapeDtypeStruct((N_CHIPS, SEQ, FF_SHARD), BF16)
    return _matmul(
        "ffn_bwd_hidden", (N_CHIPS, SEQ // MM_TILE), 1, None, NT, 1,
        [pl.BlockSpec((MM_TILE, D_MODEL), lambda j, i: (i, 0)),
         pl.BlockSpec((None, FF_SHARD, D_MODEL), lambda j, i: (j, 0, 0)), hid_spec, hid_spec],
        [hid_spec, hid_spec], [hid, hid], None, epilogue, (df, wd_all, a, b), ("parallel", "parallel"))


def grad_weight(name, lhs, lhs_spec, rhs, rhs_spec, shape):
    def epilogue(acc, ins, outs, ids):
        outs[0][...] = acc.astype(BF16)

    return _matmul(
        name, (N_CHIPS, SEQ // MM_TILE), SEQ // MM_TILE, 1, TN, 1,
        [lhs_spec, rhs_spec], [pl.BlockSpec((None,) + shape, lambda j, k: (j, 0, 0))],
        [jax.ShapeDtypeStruct((N_CHIPS,) + shape, BF16)], shape, epilogue, (lhs, rhs), ("parallel", "arbitrary"))[0]


def ffn_grad_weights(h, s, df, da, db):
    tok = pl.BlockSpec((MM_TILE, D_MODEL), lambda j, k: (k, 0))
    hid = pl.BlockSpec((None, MM_TILE, FF_SHARD), lambda j, k: (j, k, 0))
    n_k = SEQ // MM_TILE

    def body(da_ref, db_ref, s_ref, h_ref, df_ref, gg_ref, gu_ref, gd_ref, acc_g, acc_u, acc_d):
        k = pl.program_id(1)
        hv = h_ref[...]
        parts = (_dot(da_ref[...], hv, TN), _dot(db_ref[...], hv, TN), _dot(s_ref[...], df_ref[...], TN))

        @pl.when(k == 0)
        def _():
            acc_g[...], acc_u[...], acc_d[...] = parts

        @pl.when(k > 0)
        def _():
            acc_g[...] += parts[0]
            acc_u[...] += parts[1]
            acc_d[...] += parts[2]

        @pl.when(k == n_k - 1)
        def _():
            gg_ref[...] = acc_g[...].astype(BF16)
            gu_ref[...] = acc_u[...].astype(BF16)
            gd_ref[...] = acc_d[...].astype(BF16)

    out = pl.BlockSpec((None, FF_SHARD, D_MODEL), lambda j, k: (j, 0, 0))
    shape = jax.ShapeDtypeStruct((N_CHIPS, FF_SHARD, D_MODEL), BF16)
    return pl.pallas_call(
        body, grid=(N_CHIPS, n_k), in_specs=[hid, hid, hid, tok, tok], out_specs=[out] * 3, out_shape=[shape] * 3,
        scratch_shapes=[pltpu.VMEM((FF_SHARD, D_MODEL), F32)] * 3,
        name="ffn_grad_weights", compiler_params=_params(("parallel", "arbitrary")),
    )(da, db, s, h, df)


BWD_TILE = 512


def matmul_prenorm_bwd(name, dims, pairs, pair_specs, x, dxo, gain, scale, sums, u, below):
    n_pairs = len(pairs) // 2
    lhs, weights = pairs[0::2], pairs[1::2]
    n_in = n_pairs * 2 + 5 + (3 if below is not None else 0)
    n_out = 2 + (2 if below is not None else 0)

    def body(*refs):
        ins, outs = refs[:n_in], refs[n_in:n_in + n_out]
        w_vmem, w_sems = refs[n_in + n_out:n_in + n_out + n_pairs], refs[-1]
        first = pl.program_id(0) == 0

        @pl.when(first)
        def _():
            copies = [pltpu.make_async_copy(ins[n_pairs + p], w_vmem[p], w_sems.at[p]) for p in range(n_pairs)]
            for cp in copies:
                cp.start()
            for cp in copies:
                cp.wait()

        dh = None
        for j in range(N_CHIPS):
            for p in range(n_pairs):
                part = _dot(pair_specs[p][1](ins[p], j), w_vmem[p][j], dims)
                dh = part if dh is None else dh + part
        k = 2 * n_pairs
        dx = _prenorm_bwd_epilogue(dh, ins[k], ins[k + 1], ins[k + 2], ins[k + 3], outs[0], outs[1], first)
        if below is not None:
            _resid_bwd_epilogue(dx, ins[k + 5], ins[k + 6], below[2], outs[2], outs[3], first)

    row = pl.BlockSpec((BWD_TILE, D_MODEL), lambda i: (i, 0))
    any_spec = pl.BlockSpec(memory_space=pl.ANY)
    f32 = jax.ShapeDtypeStruct((SEQ, D_MODEL), F32)
    in_specs = [s for s, _ in pair_specs] + [any_spec] * n_pairs + [row, row, gain.spec(), scale.spec(), any_spec]
    operands = tuple(lhs) + tuple(weights) + (x, dxo, gain.table, scale.table, sums[0])
    out_specs, out_shape = [row, _slot_spec(u)], [f32, jax.ShapeDtypeStruct(sums[0].shape, F32)]
    aliases = {2 * n_pairs + 4: 1}
    if below is not None:
        in_specs += [row, below[1].spec(), any_spec]
        operands += (below[0], below[1].table, sums[1])
        out_specs += [row, _slot_spec(u - 1)]
        out_shape += [jax.ShapeDtypeStruct((SEQ, D_MODEL), BF16), jax.ShapeDtypeStruct(sums[1].shape, F32)]
        aliases[2 * n_pairs + 7] = 3
    out = pl.pallas_call(
        body, grid=(SEQ // BWD_TILE,), in_specs=in_specs, out_specs=out_specs, out_shape=out_shape,
        scratch_shapes=[pltpu.VMEM(w.shape, w.dtype) for w in weights] + [pltpu.SemaphoreType.DMA((n_pairs,))],
        input_output_aliases=aliases, name=name, compiler_params=_params(("arbitrary",)),
    )(*operands)
    if below is None:
        return out[0], (out[1], sums[1])
    return out[0], (out[1], out[3]), out[2]


def ffn_bwd_input(da, db, wg_all, wu_all, x, dxo, gain, scale, sums, u, below):
    hid = (pl.BlockSpec((N_CHIPS, BWD_TILE, FF_SHARD), lambda i: (0, i, 0)), lambda ref, j: ref[j])
    return matmul_prenorm_bwd("ffn_bwd_input", NN, (da, wg_all, db, wu_all), (hid, hid), x, dxo, gain, scale,
                              sums, u, below)


def in_proj(h, w_all):
    def epilogue(acc, ins, outs, ids):
        outs[0][...] = acc
        outs[1][...] = acc.astype(BF16)

    out = pl.BlockSpec((MM_TILE, IN_SHARD), lambda j, i: (i, j))
    return _matmul(
        "in_proj", (N_CHIPS, SEQ // MM_TILE), 1, None, NN, 1,
        [pl.BlockSpec((MM_TILE, D_MODEL), lambda j, i: (i, 0)),
         pl.BlockSpec((None, D_MODEL, IN_SHARD), lambda j, i: (j, 0, 0))],
        [out, out], [jax.ShapeDtypeStruct((SEQ, D_IN), F32), jax.ShapeDtypeStruct((SEQ, D_IN), BF16)],
        None, epilogue, (h, w_all), ("parallel", "parallel"))


_GATE_BLOCK0 = D_QKV // D_SHARD


def _branch_products(o, w_ref):
    ob = o.astype(BF16)
    return (_dot(ob[:, 0:256], w_ref[0:256, :], NN), _dot(ob[:, 256:384], w_ref[256:384, :], NN),
            _dot(ob[:, 384:768], w_ref[384:768, :], NN))


def merge_branches(o_cat, wbr_all, proj):
    def body(o_ref, w_ref, g0_ref, g1_ref, g2_ref, m_ref):
        u = _branch_products(o_ref[...], w_ref)
        m_ref[...] = (_sigmoid(g0_ref[...]) * u[0] + _sigmoid(g1_ref[...]) * u[1]
                      + _sigmoid(g2_ref[...]) * u[2]).astype(BF16)

    def gate_spec(b):
        return pl.BlockSpec((MM_TILE, D_SHARD), lambda i, j: (i, _GATE_BLOCK0 + 4 * b + j))

    return pl.pallas_call(
        body, grid=(SEQ // MM_TILE, N_CHIPS),
        in_specs=[pl.BlockSpec((MM_TILE, BR_ROWS), lambda i, j: (i, 0)),
                  pl.BlockSpec((None, BR_ROWS, D_SHARD), lambda i, j: (j, 0, 0)),
                  gate_spec(0), gate_spec(1), gate_spec(2)],
        out_specs=pl.BlockSpec((MM_TILE, D_SHARD), lambda i, j: (i, j)),
        out_shape=jax.ShapeDtypeStruct((SEQ, D_MODEL), BF16),
        name="merge_branches", compiler_params=_params(("parallel", "parallel")),
    )(o_cat, wbr_all, proj, proj, proj)


def out_proj(merged, wout_all, x, gate, then):
    return matmul_residual(
        "out_proj", merged, pl.BlockSpec((MM_TILE, D_MODEL), lambda i: (i, 0)),
        lambda ref, j: ref[:, D_SHARD * j:D_SHARD * (j + 1)], wout_all, x, gate, 1.0, then)


def merge_bwd(dmo, wout_all, o_cat, wbr_all, proj):
    def epilogue(dm, ins, outs, ids):
        u = _branch_products(ins[2][...], ins[3])
        for b in range(3):
            sig = _sigmoid(ins[4 + b][...])
            outs[b][...] = (dm * sig).astype(BF16)
            outs[3 + b][...] = (dm * u[b] * (sig * (1.0 - sig))).astype(BF16)

    def gate_spec(b):
        return pl.BlockSpec((MM_TILE, D_SHARD), lambda j, i: (i, _GATE_BLOCK0 + 4 * b + j))

    col = pl.BlockSpec((MM_TILE, D_SHARD), lambda j, i: (i, j))
    du = jax.ShapeDtypeStruct((SEQ, D_MODEL), BF16)
    return _matmul(
        "merge_bwd", (N_CHIPS, SEQ // MM_TILE), 1, None, NT, 1,
        [pl.BlockSpec((MM_TILE, D_MODEL), lambda j, i: (i, 0)),
         pl.BlockSpec((None, D_SHARD, D_MODEL), lambda j, i: (j, 0, 0)),
         pl.BlockSpec((MM_TILE, BR_ROWS), lambda j, i: (i, 0)),
         pl.BlockSpec((None, BR_ROWS, D_SHARD), lambda j, i: (j, 0, 0)),
         gate_spec(0), gate_spec(1), gate_spec(2)],
        [col] * 6, [du] * 6,
        None, epilogue, (dmo, wout_all, o_cat, wbr_all, proj, proj, proj), ("parallel", "parallel"))


def branch_bwd_input(du, wbr_all):
    def body(d0_ref, d1_ref, d2_ref, w_ref, o_ref, acc):
        j = pl.program_id(1)
        parts = (_dot(d0_ref[...], w_ref[0:256, :], NT), _dot(d1_ref[...], w_ref[256:384, :], NT),
                 _dot(d2_ref[...], w_ref[384:768, :], NT))

        @pl.when(j == 0)
        def _():
            acc[:, 0:256], acc[:, 256:384], acc[:, 384:768] = parts

        @pl.when(j > 0)
        def _():
            acc[:, 0:256] += parts[0]
            acc[:, 256:384] += parts[1]
            acc[:, 384:768] += parts[2]

        @pl.when(j == N_CHIPS - 1)
        def _():
            o_ref[...] = acc[...]

    col = pl.BlockSpec((MM_TILE, D_SHARD), lambda i, j: (i, j))
    return pl.pallas_call(
        body, grid=(SEQ // MM_TILE, N_CHIPS),
        in_specs=[col, col, col, pl.BlockSpec((None, BR_ROWS, D_SHARD), lambda i, j: (j, 0, 0))],
        out_specs=pl.BlockSpec((MM_TILE, BR_ROWS), lambda i, j: (i, 0)),
        out_shape=jax.ShapeDtypeStruct((SEQ, BR_ROWS), F32),
        scratch_shapes=[pltpu.VMEM((MM_TILE, BR_ROWS), F32)],
        name="branch_bwd_input", compiler_params=_params(("parallel", "arbitrary")),
    )(du[0], du[1], du[2], wbr_all)


def branch_grad_weights(o_cat, du):
    def body(o_ref, d0_ref, d1_ref, d2_ref, g_ref, acc):
        k = pl.program_id(1)
        ob = o_ref[...].astype(BF16)
        parts = (_dot(ob[:, 0:256], d0_ref[...], TN), _dot(ob[:, 256:384], d1_ref[...], TN),
                 _dot(ob[:, 384:768], d2_ref[...], TN))

        @pl.when(k == 0)
        def _():
            acc[0:256, :], acc[256:384, :], acc[384:768, :] = parts

        @pl.when(k > 0)
        def _():
            acc[0:256, :] += parts[0]
            acc[256:384, :] += parts[1]
            acc[384:768, :] += parts[2]

        @pl.when(k == SEQ // MM_TILE - 1)
        def _():
            g_ref[...] = acc[...].astype(BF16)

    col = pl.BlockSpec((MM_TILE, D_SHARD), lambda j, k: (k, j))
    return pl.pallas_call(
        body, grid=(N_CHIPS, SEQ // MM_TILE),
        in_specs=[pl.BlockSpec((MM_TILE, BR_ROWS), lambda j, k: (k, 0)), col, col, col],
        out_specs=pl.BlockSpec((None, BR_ROWS, D_SHARD), lambda j, k: (j, 0, 0)),
        out_shape=jax.ShapeDtypeStruct((N_CHIPS, BR_ROWS, D_SHARD), BF16),
        scratch_shapes=[pltpu.VMEM((BR_ROWS, D_SHARD), F32)],
        name="branch_grad_weights", compiler_params=_params(("parallel", "arbitrary")),
    )(o_cat, du[0], du[1], du[2])


def mixer_bwd_input(dproj, win_all, x, dxo, gain, scale, sums, u, below):
    columns = (pl.BlockSpec((BWD_TILE, D_IN), lambda i: (i, 0)),
               lambda ref, j: ref[:, IN_SHARD * j:IN_SHARD * (j + 1)])
    return matmul_prenorm_bwd("mixer_bwd_input", NT, (dproj, win_all), (columns,), x, dxo, gain, scale, sums, u, below)


BATCH_QK = (((2,), (2,)), ((0,), (0,)))
BATCH_PV = (((2,), (1,)), ((0,), (0,)))
BATCH_TN = (((1,), (1,)), ((0,), (0,)))


SB_WIDTH = H_SB * HEAD_DIM
SB_ROWS = H_SB * BLK


def _split_dot(v, tri):
    hi = v.astype(BF16)
    lo = (v - hi.astype(F32)).astype(BF16)
    return _dot(hi, tri, NN) + _dot(lo, tri, NN)


def _tri(cmp):
    return cmp(lax.broadcasted_iota(jnp.int32, (BLK, BLK), 0), lax.broadcasted_iota(jnp.int32, (BLK, BLK), 1)).astype(BF16)


def _head_masks():
    lane = lax.broadcasted_iota(jnp.int32, (1, SB_WIDTH), 1) // HEAD_DIM
    return [lane == h for h in range(H_SB)]


def _stack_heads(x, masks):
    return jnp.concatenate([jnp.where(m, x, jnp.zeros_like(x)) for m in masks], axis=0)


def _merge_heads(y, masks):
    out = jnp.where(masks[0], y[0:BLK], 0.0)
    for h in range(1, H_SB):
        out = jnp.where(masks[h], y[h * BLK:(h + 1) * BLK], out)
    return out


def _sb_scores(q4, k_ref, j, diagonal):
    rows = pl.ds(pl.multiple_of(j * BLK, BLK), BLK)
    z = _dot(q4, k_ref[rows, :], NT)
    log_fail = -(jnp.maximum(z, 0.0) + jnp.log(1.0 + jnp.exp(-jnp.abs(z))))
    log_hit = z + log_fail
    before = None
    if diagonal:
        tile = (SB_ROWS, BLK)
        before = lax.broadcasted_iota(jnp.int32, tile, 1) < (lax.broadcasted_iota(jnp.int32, tile, 0) & (BLK - 1))
        log_fail = jnp.where(before, log_fail, 0.0)
    return rows, before, log_fail, log_hit


def _keep(before, x):
    return x if before is None else jnp.where(before, x, 0.0)


def sb_forward(qkv):
    def body(q_ref, k_ref, v_ref, o_ref, tot_ref):
        i = pl.program_id(0)
        masks = _head_masks()
        q4 = _stack_heads(q_ref[...] * QK_SCALE, masks)
        later = _tri(lambda r, c: r > c)

        def tiles(js, carry, diagonal):
            o, run = carry
            scores = [_sb_scores(q4, k_ref, j, diagonal) for j in js]
            acc = None
            for rows, before, log_fail, log_hit in scores:
                between = _split_dot(log_fail, later) + run
                w = _keep(before, jnp.exp(log_hit + between))
                part = _dot(w.astype(BF16), v_ref[rows, :], NN)
                acc = part if acc is None else acc + part
                run = run + jnp.sum(log_fail, axis=1, keepdims=True)
            return o + _merge_heads(acc, masks), run

        carry = tiles([i], (jnp.zeros((BLK, SB_WIDTH), F32), jnp.zeros((SB_ROWS, 1), F32)), True)
        carry = lax.cond((i & 1) != 0, lambda c: tiles([i - 1], c, False), lambda c: c, carry)
        at = i - 1 - (i & 1)
        carry = lax.cond((i & 2) != 0, lambda c: tiles([at, at - 1], c, False), lambda c: c, carry)
        at = at - (i & 2)
        o, run = lax.fori_loop(0, i // 4, lambda t, c: tiles([at - 4 * t - n for n in range(4)], c, False), carry)
        o_ref[...] = o
        tot_ref[...] = run

    return pl.pallas_call(
        body, grid=(N_BLK,),
        in_specs=[pl.BlockSpec((BLK, SB_WIDTH), lambda i: (i, 0)), pl.BlockSpec((SEQ, SB_WIDTH), lambda i: (0, 1)),
                  pl.BlockSpec((SEQ, SB_WIDTH), lambda i: (0, 2))],
        out_specs=[pl.BlockSpec((BLK, SB_WIDTH), lambda i: (i, 0)), pl.BlockSpec((None, SB_ROWS, 1), lambda i: (i, 0, 0))],
        out_shape=[jax.ShapeDtypeStruct((SEQ, SB_WIDTH), F32), jax.ShapeDtypeStruct((N_BLK, SB_ROWS, 1), F32)],
        name="sb_forward", compiler_params=_params(("parallel",)),
    )(qkv, qkv, qkv)


def sb_backward(qkv, total, do_cat):
    def body(q_ref, k_ref, v_ref, tot_ref, do_ref, dq_ref, dk_ref, dv_ref):
        i = pl.program_id(0)

        @pl.when(i == 0)
        def _():
            dk_ref[...] = jnp.zeros_like(dk_ref)
            dv_ref[...] = jnp.zeros_like(dv_ref)

        masks = _head_masks()
        q4 = _stack_heads(q_ref[...] * QK_SCALE, masks)
        do4 = _stack_heads(do_ref[...].astype(BF16), masks)
        total_v = tot_ref[...]
        upto = _tri(lambda r, c: r <= c)
        earlier = _tri(lambda r, c: r < c)

        def tiles(js, carry, diagonal):
            dq, seen, g_seen = carry
            scores = [_sb_scores(q4, k_ref, j, diagonal) for j in js]
            acc = None
            for rows, before, log_fail, log_hit in scores:
                between = total_v - (seen + _split_dot(log_fail, upto))
                w = _keep(before, jnp.exp(log_hit + between))
                g = _dot(do4, v_ref[rows, :], NT) * w
                g_earlier = g_seen + _split_dot(g, earlier)
                sig = jnp.exp(log_hit)
                dz = _keep(before, g * (1.0 - sig) - g_earlier * sig).astype(BF16)
                part = _dot(dz, k_ref[rows, :], NN)
                acc = part if acc is None else acc + part
                dk_ref[rows, :] += _dot(dz, q4, TN)
                dv_ref[rows, :] += _dot(w.astype(BF16), do4, TN)
                seen = seen + jnp.sum(log_fail, axis=1, keepdims=True)
                g_seen = g_seen + jnp.sum(g, axis=1, keepdims=True)
            return dq + _merge_heads(acc, masks), seen, g_seen

        zero = jnp.zeros((SB_ROWS, 1), F32)
        carry = lax.fori_loop(0, i // 4, lambda t, c: tiles([4 * t + n for n in range(4)], c, False),
                              (jnp.zeros((BLK, SB_WIDTH), F32), zero, zero))
        at = i - (i & 3)
        carry = lax.cond((i & 2) != 0, lambda c: tiles([at, at + 1], c, False), lambda c: c, carry)
        carry = lax.cond((i & 1) != 0, lambda c: tiles([i - 1], c, False), lambda c: c, carry)
        dq, _, _ = tiles([i], carry, True)
        dq_ref[...] = dq * QK_SCALE

    blk = pl.BlockSpec((BLK, SB_WIDTH), lambda i: (i, 0))
    full = pl.BlockSpec((SEQ, SB_WIDTH), lambda i: (0, 0))
    shape = jax.ShapeDtypeStruct((SEQ, SB_WIDTH), F32)
    return pl.pallas_call(
        body, grid=(N_BLK,),
        in_specs=[blk, pl.BlockSpec((SEQ, SB_WIDTH), lambda i: (0, 1)), pl.BlockSpec((SEQ, SB_WIDTH), lambda i: (0, 2)),
                  pl.BlockSpec((None, SB_ROWS, 1), lambda i: (i, 0, 0)), blk],
        out_specs=[blk, full, full], out_shape=[shape, shape, shape],
        name="sb_backward", compiler_params=_params(("arbitrary",)),
    )(qkv, qkv, qkv, total, do_cat)


def _band_scores(q_ref, kp_ref, ko_ref, bias_ref, hb, prev_mask):
    b = pl.program_id(1)
    qs = q_ref[...]
    s_prev = _dot(qs, kp_ref[...], BATCH_QK) + bias_ref[:, :, 0:BLK]
    s_prev = jnp.concatenate(
        [jnp.where((b & prev_mask(pl.program_id(0) * hb + t)) != 0, s_prev[t:t + 1], NEG) for t in range(hb)], axis=0)
    s_own = _dot(qs, ko_ref[...], BATCH_QK) + bias_ref[:, :, BLK:2 * BLK]
    return qs, s_prev, s_own


def _band_specs(hb, rows, t_n):
    def q_spec(width):
        return pl.BlockSpec((hb, None, rows, width), lambda h, b: (h, b, 0, 0))

    own = pl.BlockSpec((hb, BLK, HEAD_DIM), lambda h, b: (h, b, 0))
    prev = pl.BlockSpec((hb, BLK, HEAD_DIM), lambda h, b: (h, jnp.maximum(b - 1, 0), 0))
    per_head = lambda r, width: pl.BlockSpec((hb, r, width), lambda h, b: (h, 0, 0))
    return q_spec, own, prev, per_head


def banded_forward(name, q, k, v, bias, sinks, hb, prev_mask):
    h_n, nb, rows, _ = q.shape

    def body(q_ref, kp_ref, ko_ref, vp_ref, vo_ref, bias_ref, sink_ref, o_ref, lse_ref):
        _, s_prev, s_own = _band_scores(q_ref, kp_ref, ko_ref, bias_ref, hb, prev_mask)
        sink = sink_ref[...]
        m = jnp.maximum(jnp.maximum(jnp.max(s_prev, axis=2, keepdims=True), jnp.max(s_own, axis=2, keepdims=True)), sink)
        p_prev = jnp.exp(s_prev - m)
        p_own = jnp.exp(s_own - m)
        denom = jnp.sum(p_prev, axis=2, keepdims=True) + jnp.sum(p_own, axis=2, keepdims=True) + jnp.exp(sink - m)
        o = _dot(p_prev.astype(BF16), vp_ref[...], BATCH_PV) + _dot(p_own.astype(BF16), vo_ref[...], BATCH_PV)
        o_ref[...] = o / denom
        lse_ref[...] = m + jnp.log(denom)

    q_spec, own, prev, per_head = _band_specs(hb, rows, k.shape[1])
    return pl.pallas_call(
        body, grid=(h_n // hb, nb),
        in_specs=[q_spec(HEAD_DIM), prev, own, prev, own, per_head(rows, 2 * BLK), per_head(rows, 1)],
        out_specs=[q_spec(HEAD_DIM), q_spec(1)],
        out_shape=[jax.ShapeDtypeStruct(q.shape, F32), jax.ShapeDtypeStruct((h_n, nb, rows, 1), F32)],
        name=name, compiler_params=_params(("parallel", "parallel")),
    )(q, k, k, v, v, bias, sinks)


def banded_backward(name, q, k, v, bias, sinks, o, lse, do, dlse, hb, prev_mask):
    h_n, nb, rows, _ = q.shape
    t_n = k.shape[1]

    def body(q_ref, kp_ref, ko_ref, vp_ref, vo_ref, bias_ref, sink_ref, o_ref, lse_ref, do_ref, dlse_ref,
             dq_ref, dk_ref, dv_ref, dbias_ref, dsink_ref):
        b = pl.program_id(1)

        @pl.when(b == 0)
        def _():
            dk_ref[...] = jnp.zeros_like(dk_ref)
            dv_ref[...] = jnp.zeros_like(dv_ref)
            dbias_ref[...] = jnp.zeros_like(dbias_ref)
            dsink_ref[...] = jnp.zeros_like(dsink_ref)

        qs, s_prev, s_own = _band_scores(q_ref, kp_ref, ko_ref, bias_ref, hb, prev_mask)
        lse_v = lse_ref[...]
        dov = do_ref[...]
        dob = dov.astype(BF16)
        shift = dlse_ref[...] - jnp.sum(dov * o_ref[...], axis=2, keepdims=True)
        p_prev = jnp.exp(s_prev - lse_v)
        p_own = jnp.exp(s_own - lse_v)
        ds_prev = p_prev * (_dot(dob, vp_ref[...], BATCH_QK) + shift)
        ds_own = p_own * (_dot(dob, vo_ref[...], BATCH_QK) + shift)
        dbias_ref[:, :, 0:BLK] += ds_prev
        dbias_ref[:, :, BLK:2 * BLK] += ds_own
        d_sink = jnp.exp(sink_ref[...] - lse_v) * shift
        for g in range(rows // BLK):
            dsink_ref[:, g:g + 1, :] += jnp.sum(d_sink[:, g * BLK:(g + 1) * BLK, :], axis=1, keepdims=True)
        ds_prev = ds_prev.astype(BF16)
        ds_own = ds_own.astype(BF16)
        dq_ref[...] = (_dot(ds_prev, kp_ref[...], BATCH_PV) + _dot(ds_own, ko_ref[...], BATCH_PV)) * QK_SCALE
        rows_prev = pl.ds(pl.multiple_of(jnp.maximum(b - 1, 0) * BLK, BLK), BLK)
        rows_own = pl.ds(pl.multiple_of(b * BLK, BLK), BLK)
        dk_ref[:, rows_prev, :] += _dot(ds_prev, qs, BATCH_TN)
        dk_ref[:, rows_own, :] += _dot(ds_own, qs, BATCH_TN)
        dv_ref[:, rows_prev, :] += _dot(p_prev.astype(BF16), dob, BATCH_TN)
        dv_ref[:, rows_own, :] += _dot(p_own.astype(BF16), dob, BATCH_TN)

    q_spec, own, prev, per_head = _band_specs(hb, rows, t_n)
    kv_full = per_head(t_n, HEAD_DIM)
    kv_shape = jax.ShapeDtypeStruct((h_n, t_n, HEAD_DIM), F32)
    return pl.pallas_call(
        body, grid=(h_n // hb, nb),
        in_specs=[q_spec(HEAD_DIM), prev, own, prev, own, per_head(rows, 2 * BLK), per_head(rows, 1),
                  q_spec(HEAD_DIM), q_spec(1), q_spec(HEAD_DIM), q_spec(1)],
        out_specs=[q_spec(HEAD_DIM), kv_full, kv_full, per_head(rows, 2 * BLK), per_head(rows // BLK, BLK)],
        out_shape=[jax.ShapeDtypeStruct(q.shape, F32), kv_shape, kv_shape,
                   jax.ShapeDtypeStruct((h_n, rows, 2 * BLK), F32), jax.ShapeDtypeStruct((h_n, rows // BLK, BLK), F32)],
        name=name, compiler_params=_params(("parallel", "arbitrary")),
    )(q, k, k, v, v, bias, sinks, o, lse, do, dlse)


def _swa_prev_mask(head):
    del head
    return 15


SWA_HEADS_PER_STEP = 2
SWA_GROUP = H_SWA_Q // H_SWA_KV
N_BLK = SEQ // BLK


GROUP_W = H_PER_DIL * HEAD_DIM
DIL_COLUMNS = (768, 1920)
LANE_BLOCKS = (DIL_COLUMNS[1] - DIL_COLUMNS[0]) // GROUP_W
DIL_Q_BLOCK, DIL_K_BLOCK, DIL_V_BLOCK = 0, 3, 6
N_GROUPS = len(DIL_PATTERNS)


def dilated_views(qkv):
    cols = qkv[:, DIL_COLUMNS[0]:DIL_COLUMNS[1]]
    return [_dil_view(cols, d) for _, d in DIL_PATTERNS]


def _dil_view(t, d):
    return t.reshape(SEQ // d, d * t.shape[1])


def _dil_tile(n, d):
    per_class = N_BLK // d
    return n // per_class, n % per_class


def _dil_spec(d, lane_block, lane_blocks, shift=0):
    def index(n):
        r, m = _dil_tile(n, d)
        m = jnp.clip(m + shift, 0, N_BLK // d - 1)
        return m, r * lane_blocks + lane_block
    return pl.BlockSpec((BLK, GROUP_W), index)


def _two_heads(x, first):
    zero = jnp.zeros_like(x)
    return jnp.concatenate([jnp.where(first, x, zero), jnp.where(first, zero, x)], axis=0)


def _per_head(col, first):
    return jnp.where(first, col[0:BLK], col[BLK:2 * BLK])


def _head_rows(tile, first):
    pick = lambda keep: jnp.max(jnp.where(keep, tile, -jnp.inf), axis=1, keepdims=True)
    return jnp.concatenate([pick(first), pick(jnp.logical_not(first))], axis=0)


def _dil_scores(q_ref, kp_ref, ko_ref, bias, has_prev, first):
    q2 = _two_heads(q_ref[...] * QK_SCALE, first)
    k2 = jnp.concatenate([kp_ref[...], ko_ref[...]], axis=0)
    s = _dot(q2, k2, NT) + bias
    key = lax.broadcasted_iota(jnp.int32, s.shape, 1)
    return q2, k2, jnp.where(jnp.logical_or(has_prev, key >= BLK), s, NEG)


def dilated_forward(views, bias):
    def body(*refs):
        ins, bias_ref, outs = refs[:5 * N_GROUPS], refs[5 * N_GROUPS], refs[5 * N_GROUPS + 1:]
        n = pl.program_id(0)
        first = lax.broadcasted_iota(jnp.int32, (1, GROUP_W), 1) < HEAD_DIM
        for g, (_, d) in enumerate(DIL_PATTERNS):
            q_ref, kp_ref, ko_ref, vp_ref, vo_ref = ins[5 * g:5 * g + 5]
            has_prev = _dil_tile(n, d)[1] > 0
            _, _, s = _dil_scores(q_ref, kp_ref, ko_ref, bias_ref[g], has_prev, first)
            m = jnp.max(s, axis=1, keepdims=True)
            p = jnp.exp(s - m)
            denom = jnp.sum(p, axis=1, keepdims=True)
            v2 = jnp.concatenate([vp_ref[...], vo_ref[...]], axis=0)
            o2 = _dot(p.astype(BF16), v2, NN) / denom
            outs[2 * g][...] = _per_head(o2, first)
            outs[2 * g + 1][...] = _per_head(m + jnp.log(denom), first)

    operands, in_specs, out_specs, out_shape = [], [], [], []
    for g, (_, d) in enumerate(DIL_PATTERNS):
        operands += [views[g]] * 5
        in_specs += [_dil_spec(d, DIL_Q_BLOCK + g, LANE_BLOCKS), _dil_spec(d, DIL_K_BLOCK + g, LANE_BLOCKS, -1),
                     _dil_spec(d, DIL_K_BLOCK + g, LANE_BLOCKS), _dil_spec(d, DIL_V_BLOCK + g, LANE_BLOCKS, -1),
                     _dil_spec(d, DIL_V_BLOCK + g, LANE_BLOCKS)]
        out_specs += [_dil_spec(d, 0, 1)] * 2
        out_shape += [jax.ShapeDtypeStruct((SEQ // d, d * GROUP_W), F32)] * 2
    out = pl.pallas_call(
        body, grid=(N_BLK,), in_specs=in_specs + [pl.BlockSpec((N_GROUPS, 2 * BLK, 2 * BLK), lambda n: (0, 0, 0))],
        out_specs=out_specs, out_shape=out_shape, name="dilated_forward", compiler_params=_params(("parallel",)),
    )(*operands, bias)
    out = [t.reshape(SEQ, GROUP_W) for t in out]
    return out[0::2], out[1::2]


def _group_softmax(lses):
    m = jnp.maximum(jnp.maximum(lses[0], lses[1]), lses[2])
    e = [jnp.exp(l - m) for l in lses]
    total = e[0] + e[1] + e[2]
    return [t / total for t in e]


def dilated_merge(o, lse):
    def body(*refs):
        alpha = _group_softmax([r[...] for r in refs[N_GROUPS:2 * N_GROUPS]])
        refs[-1][...] = alpha[0] * refs[0][...] + alpha[1] * refs[1][...] + alpha[2] * refs[2][...]

    spec = pl.BlockSpec((ROW_TILE, GROUP_W), lambda i: (i, 0))
    return pl.pallas_call(
        body, grid=(SEQ // ROW_TILE,), in_specs=[spec] * (2 * N_GROUPS), out_specs=spec,
        out_shape=jax.ShapeDtypeStruct((SEQ, GROUP_W), F32), name="dilated_merge", compiler_params=_params(("parallel",)),
    )(*o, *lse)


def dilated_merge_bwd(o, lse, do_cat):
    def body(*refs):
        o_v = [r[...] for r in refs[:N_GROUPS]]
        alpha = _group_softmax([r[...] for r in refs[N_GROUPS:2 * N_GROUPS]])
        dout = refs[2 * N_GROUPS][...]
        outs = refs[2 * N_GROUPS + 1:]
        first = lax.broadcasted_iota(jnp.int32, (1, GROUP_W), 1) < HEAD_DIM

        def head_sum(x):
            a = jnp.sum(jnp.where(first, x, 0.0), axis=1, keepdims=True)
            b = jnp.sum(jnp.where(first, 0.0, x), axis=1, keepdims=True)
            return jnp.where(first, a, b)

        dalpha = [head_sum(dout * o_g) for o_g in o_v]
        mean = alpha[0] * dalpha[0] + alpha[1] * dalpha[1] + alpha[2] * dalpha[2]
        for g in range(N_GROUPS):
            outs[g][...] = alpha[g] * dout
            outs[N_GROUPS + g][...] = alpha[g] * (dalpha[g] - mean)

    spec = pl.BlockSpec((ROW_TILE, GROUP_W), lambda i: (i, 0))
    shape = jax.ShapeDtypeStruct((SEQ, GROUP_W), F32)
    out = pl.pallas_call(
        body, grid=(SEQ // ROW_TILE,), in_specs=[spec] * (2 * N_GROUPS) + [pl.BlockSpec((ROW_TILE, GROUP_W), lambda i: (i, 2))],
        out_specs=[spec] * (2 * N_GROUPS), out_shape=[shape] * (2 * N_GROUPS),
        name="dilated_merge_bwd", compiler_params=_params(("parallel",)),
    )(*o, *lse, do_cat)
    return out[:N_GROUPS], out[N_GROUPS:]


def dilated_backward(views, bias, o, lse, do, dlse):
    n_in = 9

    def body(*refs):
        ins, bias_ref = refs[:n_in * N_GROUPS], refs[n_in * N_GROUPS]
        outs, dbias_ref = refs[n_in * N_GROUPS + 1:-1], refs[-1]
        n = pl.program_id(0)

        @pl.when(n == 0)
        def _():
            dbias_ref[...] = jnp.zeros_like(dbias_ref)

        first = lax.broadcasted_iota(jnp.int32, (1, GROUP_W), 1) < HEAD_DIM
        for g, (_, d) in enumerate(DIL_PATTERNS):
            q_ref, kp_ref, ko_ref, vp_ref, vo_ref, o_ref, lse_ref, do_ref, dlse_ref = ins[n_in * g:n_in * (g + 1)]
            has_prev = _dil_tile(n, d)[1] > 0
            q2, k2, s = _dil_scores(q_ref, kp_ref, ko_ref, bias_ref[g], has_prev, first)
            dov = do_ref[...]
            do2 = _two_heads(dov.astype(BF16), first)
            prod = dov * o_ref[...]
            delta = jnp.concatenate([jnp.sum(jnp.where(first, prod, 0.0), axis=1, keepdims=True),
                                     jnp.sum(jnp.where(first, 0.0, prod), axis=1, keepdims=True)], axis=0)
            shift = _head_rows(dlse_ref[...], first) - delta
            p = jnp.exp(s - _head_rows(lse_ref[...], first))
            v2 = jnp.concatenate([vp_ref[...], vo_ref[...]], axis=0)
            ds = p * (_dot(do2, v2, NT) + shift)
            dbias_ref[g] += ds
            ds = ds.astype(BF16)
            dq2 = _dot(ds, k2, NN) * QK_SCALE
            dk2 = _dot(ds, q2, TN)
            dv2 = _dot(p.astype(BF16), do2, TN)
            base = 5 * g
            outs[base][...] = jnp.where(first, dq2[0:BLK], dq2[BLK:2 * BLK])
            outs[base + 1][...] = dk2[BLK:2 * BLK]
            outs[base + 2][...] = dk2[0:BLK]
            outs[base + 3][...] = dv2[BLK:2 * BLK]
            outs[base + 4][...] = dv2[0:BLK]

    operands, in_specs, out_specs, out_shape = [], [], [], []
    for g, (_, d) in enumerate(DIL_PATTERNS):
        own = _dil_spec(d, 0, 1)
        operands += [views[g]] * 5 + [_dil_view(t[g], d) for t in (o, lse, do, dlse)]
        in_specs += [_dil_spec(d, DIL_Q_BLOCK + g, LANE_BLOCKS), _dil_spec(d, DIL_K_BLOCK + g, LANE_BLOCKS, -1),
                     _dil_spec(d, DIL_K_BLOCK + g, LANE_BLOCKS), _dil_spec(d, DIL_V_BLOCK + g, LANE_BLOCKS, -1),
                     _dil_spec(d, DIL_V_BLOCK + g, LANE_BLOCKS)] + [own] * 4
        out_specs += [own] * 5
        out_shape += [jax.ShapeDtypeStruct((SEQ // d, d * GROUP_W), F32)] * 5
    tiles = pl.BlockSpec((N_GROUPS, 2 * BLK, 2 * BLK), lambda n: (0, 0, 0))
    out = pl.pallas_call(
        body, grid=(N_BLK,), in_specs=in_specs + [tiles], out_specs=out_specs + [tiles],
        out_shape=out_shape + [jax.ShapeDtypeStruct((N_GROUPS, 2 * BLK, 2 * BLK), F32)],
        name="dilated_backward", compiler_params=_params(("arbitrary",)),
    )(*operands, bias)
    return [out[5 * g:5 * g + 5] for g in range(N_GROUPS)], out[-1]


def dilated_key_grads(parts):
    def body(*refs):
        ins, outs = refs[:4 * N_GROUPS], refs[4 * N_GROUPS:]
        n = pl.program_id(0)
        for g, (_, d) in enumerate(DIL_PATTERNS):
            has_next = _dil_tile(n, d)[1] < N_BLK // d - 1
            own_k, next_k, own_v, next_v = ins[4 * g:4 * g + 4]
            outs[2 * g][...] = own_k[...] + jnp.where(has_next, next_k[...], 0.0)
            outs[2 * g + 1][...] = own_v[...] + jnp.where(has_next, next_v[...], 0.0)

    operands, in_specs, out_specs, out_shape = [], [], [], []
    for g, (_, d) in enumerate(DIL_PATTERNS):
        _, dk_own, dk_prev, dv_own, dv_prev = parts[g]
        operands += [dk_own, dk_prev, dv_own, dv_prev]
        in_specs += [_dil_spec(d, 0, 1), _dil_spec(d, 0, 1, 1)] * 2
        out_specs += [_dil_spec(d, 0, 1)] * 2
        out_shape += [jax.ShapeDtypeStruct((SEQ // d, d * GROUP_W), F32)] * 2
    out = pl.pallas_call(
        body, grid=(N_BLK,), in_specs=in_specs, out_specs=out_specs, out_shape=out_shape,
        name="dilated_key_grads", compiler_params=_params(("parallel",)),
    )(*operands)
    tok = lambda ts: jnp.concatenate([t.reshape(SEQ, GROUP_W) for t in ts], axis=1)
    return tok([parts[g][0] for g in range(N_GROUPS)]), tok(out[0::2]), tok(out[1::2])


def rel_bias_reduce(dbias0, dbias1, bucket):
    def body(d0_ref, d1_ref, b_ref, o_ref):
        dv, bv = d0_ref[...] + d1_ref[...], b_ref[...]
        lane = lax.broadcasted_iota(jnp.int32, (1, BLK), 1)
        acc = jnp.zeros((1, BLK), F32)
        for bkt in range(N_BUCKETS):
            acc = acc + jnp.where(lane == bkt, jnp.sum(jnp.where(bv == bkt, dv, 0.0)), 0.0)
        o_ref[...] = acc

    tile = pl.BlockSpec((None, BLK, 2 * BLK), lambda h: (h, 0, 0))
    return pl.pallas_call(
        body, grid=(dbias0.shape[0],), in_specs=[tile, tile, tile],
        out_specs=pl.BlockSpec((None, 1, BLK), lambda h: (h, 0, 0)),
        out_shape=jax.ShapeDtypeStruct((dbias0.shape[0], 1, BLK), F32),
        name="rel_bias_reduce", compiler_params=_params(("parallel",)),
    )(dbias0, dbias1, bucket)


def _heads(t):
    return t.reshape(SEQ, -1, HEAD_DIM).transpose(1, 0, 2)


def _unheads(t):
    return t.transpose(1, 0, 2).reshape(SEQ, -1)


def _t5_bucket(n):
    max_exact = N_BUCKETS // 2
    nf = jnp.maximum(n, 1).astype(F32)
    large = max_exact + (jnp.log(nf / max_exact) / math.log(MAX_REL_DIST / max_exact)
                         * (N_BUCKETS - max_exact)).astype(jnp.int32)
    large = jnp.minimum(large, N_BUCKETS - 1)
    return jnp.where(n < max_exact, n, large)


def band_tables(rel_bias):
    rel = jnp.arange(BLK)[:, None] + BLK - jnp.arange(2 * BLK)[None, :]
    patterns = [(d, w // d, H_PER_DIL) for w, d in DIL_PATTERNS] + [(1, SWA_WINDOW - 1, H_SWA_Q)]
    buckets = []
    for d, max_dist, heads in patterns:
        band = (rel >= 0) & (rel <= max_dist)
        tile = jnp.where(band, _t5_bucket(jnp.maximum(rel, 0) * d), -1).astype(jnp.int32)
        buckets.append(jnp.broadcast_to(tile, (heads,) + tile.shape))
    buckets = jnp.concatenate(buckets, axis=0)

    def body(table_ref, b_ref, o_ref):
        h = pl.program_id(0)
        bv = b_ref[...]
        tile = jnp.full(bv.shape, NEG, F32)
        for bkt in range(N_BUCKETS):
            tile = jnp.where(bv == bkt, table_ref[h, bkt], tile)
        o_ref[...] = tile

    spec = pl.BlockSpec((None, BLK, 2 * BLK), lambda h: (h, 0, 0))
    tiles = pl.pallas_call(
        body, grid=(buckets.shape[0],), in_specs=[pl.BlockSpec(memory_space=pltpu.SMEM), spec], out_specs=spec,
        out_shape=jax.ShapeDtypeStruct(buckets.shape, F32), name="band_tables", compiler_params=_params(("parallel",)),
    )(rel_bias.T, buckets)
    return tiles[:H_DIL], tiles[H_DIL:], buckets


def _swa_rows(t):
    t = t.reshape(N_BLK, BLK, H_SWA_KV, SWA_GROUP, HEAD_DIM).transpose(2, 0, 3, 1, 4)
    return t.reshape(H_SWA_KV, N_BLK, SWA_GROUP * BLK, HEAD_DIM)


def _swa_tokens(t):
    t = t.reshape(H_SWA_KV, N_BLK, SWA_GROUP, BLK, HEAD_DIM).transpose(1, 3, 0, 2, 4)
    return t.reshape(SEQ, H_SWA_Q * HEAD_DIM)


def _sink_rows(sinks):
    return jnp.broadcast_to(sinks.reshape(H_SWA_KV, SWA_GROUP, 1, 1), (H_SWA_KV, SWA_GROUP, BLK, 1)).reshape(
        H_SWA_KV, SWA_GROUP * BLK, 1)


def _vec(v):
    return v.reshape(1, D_MODEL)


class UnitRows:
    def __init__(self, u, mod_table, gain_table):
        self.shift, self.scale, self.gate = (Row(mod_table, 3 * u + t) for t in range(3))
        self.gain = Row(gain_table, u)


def ffn_forward(x, h, rows, then, w):
    a, b, s = ffn_up(h, w[0], w[1])
    f, xo, *h_next = ffn_down(s, w[2], x, rows.gate, then)
    return xo, (h_next or [None])[0], (x, h, a, b, s, f)


def ffn_backward(u, dxo, df, saved, rows, w, sums, below):
    x, h, a, b, s, _ = saved
    da, db = ffn_bwd_hidden(df, w[2], a, b)
    grads = ffn_grad_weights(h, s, df, da, db)
    dx, sums, *df_below = ffn_bwd_input(da, db, w[0], w[1], x, dxo, rows.gain, rows.scale, sums, u, below)
    return dx, sums, df_below, grads


def mixer_forward(x, h, rows, then, sinks, bias_dil, bias_swa, w):
    proj, qkv = in_proj(h, w[0])
    q_swa, k_swa, v_swa = _swa_rows(qkv[:, 1920:2304] * QK_SCALE), _heads(qkv[:, 2304:2432]), _heads(qkv[:, 2432:2560])
    o_sb, total_sb = sb_forward(qkv)
    bias_dil = bias_dil.reshape(N_GROUPS, 2 * BLK, 2 * BLK)
    views = dilated_views(qkv)
    o_groups, lse_groups = dilated_forward(views, bias_dil)
    o_dil = dilated_merge(o_groups, lse_groups)
    bias_swa = bias_swa.reshape(H_SWA_KV, SWA_GROUP * BLK, 2 * BLK)
    o_swa, lse_swa = banded_forward("swa_forward", q_swa, k_swa, v_swa, bias_swa, _sink_rows(sinks), SWA_HEADS_PER_STEP,
                                    _swa_prev_mask)
    o_cat = jnp.concatenate([o_sb, o_dil, _swa_tokens(o_swa)], axis=1)
    merged = merge_branches(o_cat, w[1], proj)
    mo, xo, *h_next = out_proj(merged, w[2], x, rows.gate, then)
    saved = (x, h, proj, (qkv, total_sb), (views, o_groups, lse_groups),
             (q_swa, k_swa, v_swa, o_swa, lse_swa), o_cat, merged, mo)
    return xo, (h_next or [None])[0], saved


def mixer_backward(u, dxo, dmo, saved, rows, sinks, bias_dil, bias_swa, w, sums, below):
    x, h, proj, sb, dil, swa, o_cat, merged, _ = saved
    tok = pl.BlockSpec((MM_TILE, D_MODEL), lambda j, k: (k, 0))
    g_out = grad_weight("grad_w_out", merged, pl.BlockSpec((MM_TILE, D_SHARD), lambda j, k: (k, j)), dmo, tok,
                        (D_SHARD, D_MODEL))
    du0, du1, du2, dg0, dg1, dg2 = merge_bwd(dmo, w[2], o_cat, w[1], proj)
    du = (du0, du1, du2)
    do_cat = branch_bwd_input(du, w[1])
    g_br = branch_grad_weights(o_cat, du)

    qkv, total_sb = sb
    dq_sb, dk_sb, dv_sb = sb_backward(qkv, total_sb, do_cat)

    views, o_groups, lse_groups = dil
    bias_dil = bias_dil.reshape(N_GROUPS, 2 * BLK, 2 * BLK)
    do_groups, dlse_groups = dilated_merge_bwd(o_groups, lse_groups, do_cat)
    parts, dbias_dil = dilated_backward(views, bias_dil, o_groups, lse_groups, do_groups, dlse_groups)
    dq_dil, dk_dil, dv_dil = dilated_key_grads(parts)
    dbias_dil = dbias_dil.reshape(H_DIL, BLK, 2 * BLK)

    q_swa, k_swa, v_swa, o_swa, lse_swa = swa
    bias_swa = bias_swa.reshape(H_SWA_KV, SWA_GROUP * BLK, 2 * BLK)
    dq_swa, dk_swa, dv_swa, dbias_swa, dsinks = banded_backward(
        "swa_backward", q_swa, k_swa, v_swa, bias_swa, _sink_rows(sinks), o_swa, lse_swa, _swa_rows(do_cat[:, 384:768]),
        jnp.zeros_like(lse_swa), SWA_HEADS_PER_STEP, _swa_prev_mask)
    dbias_swa = dbias_swa.reshape(H_SWA_Q, BLK, 2 * BLK)

    pieces = [dq_sb, dk_sb, dv_sb, dq_dil, dk_dil, dv_dil, _swa_tokens(dq_swa), _unheads(dk_swa), _unheads(dv_swa)]
    dproj = jnp.concatenate([p.astype(BF16) for p in pieces] + [dg0, dg1, dg2], axis=1)
    g_in = grad_weight("grad_w_in", h, tok, dproj, pl.BlockSpec((MM_TILE, IN_SHARD), lambda j, k: (k, j)),
                       (D_MODEL, IN_SHARD))
    dx, sums, *df_below = mixer_bwd_input(dproj, w[0], x, dxo, rows.gain, rows.scale, sums, u, below)
    dbias = jnp.concatenate([dbias_dil, dbias_swa], axis=0)
    return dx, sums, df_below, dbias, dsinks[:, :, 0].reshape(H_SWA_Q), (g_in, g_br, g_out)


N_UNITS = 3 * DEPTH


def device_step(x, target, mod, gains, final_gain, sinks, rel_bias, get_weights, put_grads):
    bias_dil, bias_swa, bucket = band_tables(rel_bias)
    mod_table = mod.reshape(3 * N_UNITS, 1, D_MODEL)
    gain_table = gains.reshape(N_UNITS, 1, D_MODEL)
    saved, weights = [], []
    units = [UnitRows(u, mod_table, gain_table) for u in range(N_UNITS)]
    h = prenorm(x, units[0].gain, units[0].scale, units[0].shift)
    for u in range(N_UNITS):
        l, j = divmod(u, 3)
        w = get_weights(u, x)
        then = units[u + 1] if u + 1 < N_UNITS else None
        if j == 1:
            x, h, s = mixer_forward(x, h, units[u], then, sinks[l], bias_dil, bias_swa, w)
        else:
            x, h, s = ffn_forward(x, h, units[u], then, w)
        saved.append(s)
        weights.append(w)
    loss, dx, dfinal = final_loss(x, _vec(final_gain), target)

    sums = (lax.empty((8 * N_UNITS, D_MODEL), F32), lax.empty((8 * N_UNITS, D_MODEL), F32))
    dbias, dsinks = [None] * DEPTH, [None] * DEPTH
    zero = jnp.zeros((1, 1), F32)
    top = N_UNITS - 1
    df, gate_sums = resid_bwd(dx, saved[top][-1], units[top].gate, 0.5, sums[1], top)
    sums = (sums[0], gate_sums)
    for u in reversed(range(N_UNITS)):
        l, j = divmod(u, 3)
        rows = UnitRows(u, mod_table, gain_table + zero)
        below = (saved[u - 1][-1], units[u - 1].gate, 1.0 if (u - 1) % 3 == 1 else 0.5) if u > 0 else None
        if j == 1:
            dx, sums, df, dbias[l], dsinks[l], grads = mixer_backward(
                u, dx, df, saved[u], rows, sinks[l], bias_dil, bias_swa, weights[u], sums, below)
        else:
            dx, sums, df, grads = ffn_backward(u, dx, df, saved[u], rows, weights[u], sums, below)
        df = df[0] if df else None
        if u > 0:
            zero = put_grads(u, grads)
    drel = rel_bias_reduce(dbias[0], dbias[1], bucket)[:, 0, :N_BUCKETS].T
    norm_sums, gate_sums = (t.reshape(DEPTH, 3, 8, D_MODEL) for t in sums)
    dmod = jnp.stack([norm_sums[:, :, 0], norm_sums[:, :, 1], gate_sums[:, :, 0]], axis=2)
    return loss, dx, dmod, norm_sums[:, :, 2], dfinal[0], jnp.stack(dsinks), drel, grads


MESH = pl.DeviceIdType.MESH
CHIP_FLIPS = ((1, 0), (0, 1), (1, 1))
ANY = pl.BlockSpec(memory_space=pl.ANY)


def _position():
    return lax.axis_index("x"), lax.axis_index("y"), lax.axis_index("c")


def all_gather_small(name, piece):
    def body(x_ref, out_ref, send_sems, recv_sems, local_sem):
        x, y, c = _position()
        me, sibling = (x, y, c), (x, y, 1 - c)
        chips = [(x ^ fx, y ^ fy) for fx, fy in CHIP_FLIPS]

        def rows(px, py, pc):
            return out_ref.at[4 * px + 2 * py + pc]

        def copy(k, block, to, src=None):
            return pltpu.make_async_remote_copy(
                src_ref=rows(*block) if src is None else src, dst_ref=rows(*block),
                send_sem=send_sems.at[k], recv_sem=recv_sems.at[k], device_id=to, device_id_type=MESH)

        mine = pltpu.make_async_copy(x_ref, rows(*me), local_sem)
        mine.start()
        first = [copy(0, me, sibling, src=x_ref)]
        first += [copy(1 + j, me, (*chip, c), src=x_ref) for j, chip in enumerate(chips)]
        for cp in first:
            cp.start()
        passed = [copy(4 + j, (*chip, c), sibling) for j, chip in enumerate(chips)]
        for j, chip in enumerate(chips):
            copy(1 + j, (*chip, c), me).wait_recv()
            passed[j].start()
        copy(0, sibling, me).wait_recv()
        for j, chip in enumerate(chips):
            copy(4 + j, (*chip, 1 - c), me).wait_recv()
        for cp in first + passed:
            cp.wait_send()
        mine.wait()

    return pl.pallas_call(
        body, out_shape=jax.ShapeDtypeStruct((N_DEV,) + piece.shape, piece.dtype),
        in_specs=[pl.BlockSpec(memory_space=pltpu.VMEM)], out_specs=pl.BlockSpec(memory_space=pltpu.VMEM),
        scratch_shapes=[pltpu.SemaphoreType.DMA((7,)), pltpu.SemaphoreType.DMA((7,)), pltpu.SemaphoreType.DMA],
        name=name,
    )(piece)


def exchange(name, operands, out_shapes, aliases, plan):
    n_in, n_out = len(operands), len(out_shapes)

    def body(*refs):
        ins, outs = refs[:n_in], refs[n_in:n_in + n_out]
        send_sems, recv_sems, local_sems = refs[n_in + n_out:]
        x, y, c = _position()
        local, sends, recvs = plan(ins, outs, x, y, c)
        local = [pltpu.make_async_copy(s, d, local_sems.at[k]) for k, (s, d) in enumerate(local)]
        for cp in local:
            cp.start()
        remote = [pltpu.make_async_remote_copy(src_ref=s, dst_ref=d, send_sem=send_sems.at[k], recv_sem=recv_sems.at[k],
                                               device_id=dev, device_id_type=MESH)
                  for k, (s, d, dev) in enumerate(sends)]
        for cp in remote:
            cp.start()
        for k, r in enumerate(recvs):
            pltpu.make_async_remote_copy(src_ref=r, dst_ref=r, send_sem=send_sems.at[k], recv_sem=recv_sems.at[k],
                                         device_id=(x, y, c), device_id_type=MESH).wait_recv()
        for cp in remote:
            cp.wait_send()
        for cp in local:
            cp.wait()

    n_sends, n_local = plan.n_sends, max(plan.n_local, 1)
    return pl.pallas_call(
        body, out_shape=out_shapes, in_specs=[ANY] * n_in, out_specs=[ANY] * n_out,
        scratch_shapes=[pltpu.SemaphoreType.DMA((n_sends,)), pltpu.SemaphoreType.DMA((n_sends,)),
                        pltpu.SemaphoreType.DMA((n_local,))],
        input_output_aliases=aliases, name=name,
    )(*operands)


def _plan(n_local, n_sends):
    def wrap(fn):
        fn.n_local, fn.n_sends = n_local, n_sends
        return fn
    return wrap


def _half(ref, axis, c):
    rows = ref.shape[axis] // 2
    idx = [slice(None)] * len(ref.shape)
    idx[axis] = pl.ds(pl.multiple_of(c * rows, 16), rows)
    return ref.at[tuple(idx)]


HBM = pl.BlockSpec(memory_space=pltpu.HBM)
SEM = pl.BlockSpec(memory_space=pltpu.SEMAPHORE)
EFFECT = pltpu.SideEffectType.DATAFLOW_SIDE_EFFECTING


def split_start(name, bufs, extra, n_copies, describe):
    n = len(bufs)

    def body(*refs):
        send_sems, recv_sems = refs[n + len(extra)], refs[n + len(extra) + 1]
        x, y, c = _position()
        for k, (src, dst, _, peer) in enumerate(describe(refs[:n], x, y, c)):
            pltpu.make_async_remote_copy(src_ref=src, dst_ref=dst, send_sem=send_sems.at[k], recv_sem=recv_sems.at[k],
                                         device_id=peer, device_id_type=MESH).start()
        token = refs[-1]
        token[...] = jnp.zeros_like(token)

    out = pl.pallas_call(
        body, name=name,
        out_shape=(pltpu.SemaphoreType.DMA((n_copies,)), pltpu.SemaphoreType.DMA((n_copies,)),
                   *[pltpu.HBM(b.shape, b.dtype) for b in bufs], jax.ShapeDtypeStruct((8, 128), F32)),
        in_specs=[HBM] * n + [ANY] * len(extra),
        out_specs=(SEM, SEM, *[HBM] * n, pl.BlockSpec(memory_space=pltpu.VMEM)),
        input_output_aliases={k: 2 + k for k in range(n)},
        compiler_params=pltpu.CompilerParams(has_side_effects=EFFECT),
    )(*[pltpu.with_memory_space_constraint(b, pltpu.HBM) for b in bufs], *extra)
    return out[0], out[1], list(out[2:2 + n]), out[-1]


def split_wait(name, bufs, send_sems, recv_sems, after, describe):
    n = len(bufs)

    def body(*refs):
        send, recv = refs[n], refs[n + 1]
        x, y, c = _position()
        for k, (src, _, dst, peer) in enumerate(describe(refs[:n], x, y, c)):
            copy = pltpu.make_async_remote_copy(src_ref=src, dst_ref=dst, send_sem=send.at[k], recv_sem=recv.at[k],
                                                device_id=peer, device_id_type=MESH)
            copy.wait_send()
            copy.wait_recv()

    out = pl.pallas_call(
        body, name=name, out_shape=[pltpu.HBM(b.shape, b.dtype) for b in bufs],
        in_specs=[HBM] * n + [SEM, SEM] + [ANY] * len(after), out_specs=[HBM] * n,
        input_output_aliases={k: k for k in range(n)},
        compiler_params=pltpu.CompilerParams(has_side_effects=EFFECT),
    )(*bufs, send_sems, recv_sems, *after)
    return list(out)


def _row_tile(rows, cols, max_elements=256 * 1024):
    best = 16
    for t in range(16, rows + 1, 16):
        if rows % t == 0 and t * cols <= max_elements:
            best = t
    return best


def cast_into_slots(name, shards, chip):
    n = len(shards)
    rows, cols = shards[0][0].shape[-2:]
    tr = _row_tile(rows, cols)

    def body(chip_ref, *refs):
        del chip_ref
        for k in range(n):
            refs[n + k][...] = refs[k][...].astype(BF16)

    def in_spec(param, index):
        return pl.BlockSpec((None,) * len(index) + (tr, cols), lambda r, chip_ref: index + (r, 0))

    return pl.pallas_call(
        body, out_shape=[jax.ShapeDtypeStruct((N_CHIPS, rows, cols), BF16)] * n,
        grid_spec=pltpu.PrefetchScalarGridSpec(
            num_scalar_prefetch=1, grid=(rows // tr,),
            in_specs=[in_spec(p, idx) for p, idx in shards],
            out_specs=[pl.BlockSpec((None, tr, cols), lambda r, chip_ref: (chip_ref[0], r, 0))] * n),
        name=name, compiler_params=_params(("parallel",)),
    )(chip, *[p for p, _ in shards])


GATHER_STAGES = ((0,), (1,), (2,), (3, 4, 5))
REDUCE_STAGES = ((5, 4, 3), (2,), (1,), (0,))


def _gather_copies(slots, x, y, c):
    me = 2 * x + y
    out = []
    for s in slots:
        for fx, fy in CHIP_FLIPS:
            mine = _half(s.at[me], 0, c)
            out.append((mine, mine, _half(s.at[2 * (x ^ fx) + (y ^ fy)], 0, c), (x ^ fx, y ^ fy, c)))
    return out


class WeightStream:
    def __init__(self, shards, chip, after=()):
        self.pending, self.ready = {}, {}
        token = tuple(after)
        for si, units in enumerate(GATHER_STAGES):
            slots = []
            for u in units:
                same = len({p.shape[-2:] for p, _ in shards[u]}) == 1
                for t, group in enumerate([shards[u]] if same else [[s] for s in shards[u]]):
                    slots += cast_into_slots(f"cast_{u}_{t}", group, chip)
            send, recv, slots, tok = split_start(f"gather_start_{si}", slots, token, 3 * len(slots), _gather_copies)
            self.pending[si] = (send, recv, slots)
            token = (tok,)
        self.token = token

    def get(self, u, after):
        if u not in self.ready:
            si = next(k for k, units in enumerate(GATHER_STAGES) if u in units)
            send, recv, slots = self.pending.pop(si)
            slots = split_wait(f"gather_wait_{si}", slots, send, recv, (after,) + self.token, _gather_copies)
            self.token = ()

            @_plan(0, 3 * len(slots))
            def to_sibling(ins, outs, x, y, c):
                sends, recvs = [], []
                for o in outs:
                    for fx, fy in CHIP_FLIPS:
                        slab = o.at[2 * (x ^ fx) + (y ^ fy)]
                        sends.append((_half(slab, 0, c), _half(slab, 0, c), (x, y, 1 - c)))
                        recvs.append(_half(slab, 0, 1 - c))
                return [], sends, recvs

            shapes = [jax.ShapeDtypeStruct(s.shape, BF16) for s in slots]
            slots = exchange(f"gather_sibling_{si}", slots, shapes, {k: k for k in range(len(slots))}, to_sibling)
            for i, v in enumerate(GATHER_STAGES[si]):
                self.ready[v] = tuple(slots[3 * i:3 * i + 3])
        return self.ready[u]


def _reduce_copies(bufs, x, y, c):
    n = len(bufs) // 2
    out = []
    for s, land in zip(bufs[:n], bufs[n:]):
        for k, (fx, fy) in enumerate(CHIP_FLIPS):
            out.append((s.at[2 * (x ^ fx) + (y ^ fy)], land.at[k], land.at[k], (x ^ fx, y ^ fy, c)))
    return out


GRAD_SLOTS = {"gate": (2 * DEPTH, FF_SHARD, D_MODEL), "up": (2 * DEPTH, FF_SHARD, D_MODEL),
              "down": (2 * DEPTH, FF_SHARD, D_MODEL), "in": (DEPTH, D_MODEL, IN_SHARD),
              "br": (DEPTH, BR_ROWS, D_SHARD), "out": (DEPTH, D_SHARD, D_MODEL)}


def _unit_tensors(u):
    l, j = divmod(u, 3)
    if j == 1:
        return [("in", l), ("br", l), ("out", l)]
    return [(k, 2 * l + j // 2) for k in ("gate", "up", "down")]


class GradStream:
    def __init__(self, chip, core):
        self.core = core
        self.place = jnp.concatenate([chip, core])
        self.held, self.flying = {}, []
        self.full = {k: lax.empty(shape, F32) for k, shape in GRAD_SLOTS.items()}

    def put(self, u, grads, after=()):
        self.held[u] = grads
        si = len(self.flying)
        units = REDUCE_STAGES[si]
        if not all(v in self.held for v in units):
            return jnp.zeros((1, 1), F32)
        gs = [g for v in units for g in self.held[v]]

        @_plan(0, len(gs))
        def swap_halves(ins, outs, x, y, c):
            sends = [(_half(g, 1, 1 - c), o, (x, y, 1 - c)) for g, o in zip(ins, outs)]
            return [], sends, list(outs)

        half_shapes = [jax.ShapeDtypeStruct((N_CHIPS, g.shape[1] // 2, g.shape[2]), BF16) for g in gs]
        landed = exchange(f"reduce_swap_{si}", gs + list(after), half_shapes, {}, swap_halves)
        sums = [None] * len(gs)
        for run in _same_shape_runs(gs):
            for k, s in zip(run, _add_halves([gs[k] for k in run], [landed[k] for k in run], self.core)):
                sums[k] = s
        landing = [lax.empty((3,) + s.shape[1:], BF16) for s in sums]
        send, recv, bufs, token = split_start(f"reduce_start_{si}", sums + landing, (), 3 * len(sums), _reduce_copies)
        self.flying.append((send, recv, bufs, [t for v in units for t in _unit_tensors(v)]))
        return token[0:1, 0:1]

    def finish(self, after):
        for si, (send, recv, bufs, tensors) in enumerate(self.flying):
            bufs = split_wait(f"reduce_wait_{si}", bufs, send, recv, tuple(after), _reduce_copies)
            n = len(tensors)
            for run in _same_shape_runs(bufs[:n]):
                names = [tensors[k][0] for k in run]
                out = _add_chips([bufs[k] for k in run], [bufs[n + k] for k in run], self.place,
                                 [self.full[t] for t in names], [tensors[k][1] for k in run])
                self.full.update(zip(names, out))
        names = list(self.full)

        @_plan(0, len(names))
        def share_halves(ins, outs, x, y, c):
            sends = [(_half(o, 1, c), _half(o, 1, c), (x, y, 1 - c)) for o in outs]
            return [], sends, [_half(o, 1, 1 - c) for o in outs]

        shapes = [jax.ShapeDtypeStruct(self.full[k].shape, F32) for k in names]
        out = exchange("reduce_share_halves", [self.full[k] for k in names], shapes, {k: k for k in range(len(names))},
                       share_halves)
        return dict(zip(names, out))


def _same_shape_runs(arrays, longest=3):
    runs = []
    for k, a in enumerate(arrays):
        if runs and len(runs[-1]) < longest and arrays[runs[-1][0]].shape == a.shape:
            runs[-1].append(k)
        else:
            runs.append([k])
    return runs


def _add_halves(gs, landeds, core):
    n = len(gs)
    _, rh, cols = landeds[0].shape
    tr = _row_tile(rh, cols, 1024 * 1024)
    per_half = rh // tr

    def body(core_ref, *refs):
        del core_ref
        for k in range(n):
            refs[2 * n + k][...] = (refs[k][...].astype(F32) + refs[n + k][...].astype(F32)).astype(BF16)

    blk = (None, tr, cols)
    landed_spec = pl.BlockSpec(blk, lambda j, r, core_ref: (j, r, 0))
    return pl.pallas_call(
        body, out_shape=[jax.ShapeDtypeStruct(landeds[0].shape, BF16)] * n,
        grid_spec=pltpu.PrefetchScalarGridSpec(
            num_scalar_prefetch=1, grid=(N_CHIPS, per_half),
            in_specs=[pl.BlockSpec(blk, lambda j, r, core_ref: (j, core_ref[0] * per_half + r, 0))] * n
            + [landed_spec] * n,
            out_specs=[landed_spec] * n),
        name="reduce_add_halves", compiler_params=_params(("parallel", "parallel")),
    )(core, *gs, *landeds)


def _add_chips(sums, landeds, place, fulls, slots):
    n = len(sums)
    _, rh, cols = sums[0].shape
    tr = _row_tile(rh, cols, 1024 * 1024)
    per_half = rh // tr

    def body(place_ref, *refs):
        del place_ref
        for k in range(n):
            s_ref, la_ref, o_ref = refs[k], refs[n + k], refs[3 * n + k]
            o_ref[...] = ((s_ref[...].astype(F32) + la_ref[0].astype(F32)) + la_ref[1].astype(F32)) + la_ref[2].astype(F32)

    def out_spec(slot):
        return pl.BlockSpec((None, tr, cols), lambda r, place_ref: (slot, place_ref[1] * per_half + r, 0))

    return pl.pallas_call(
        body, out_shape=[jax.ShapeDtypeStruct(f.shape, F32) for f in fulls],
        grid_spec=pltpu.PrefetchScalarGridSpec(
            num_scalar_prefetch=1, grid=(per_half,),
            in_specs=[pl.BlockSpec((None, tr, cols), lambda r, place_ref: (place_ref[0], r, 0))] * n
            + [pl.BlockSpec((3, tr, cols), lambda r, place_ref: (0, r, 0))] * n + [ANY] * n,
            out_specs=[out_spec(slot) for slot in slots]),
        input_output_aliases={1 + 2 * n + k: k for k in range(n)}, name="reduce_add_chips",
        compiler_params=_params(("parallel",)),
    )(place, *sums, *landeds, *fulls)


def sum_devices(parts):
    def body(p_ref, o_ref):
        acc = p_ref[0]
        for d in range(1, N_DEV):
            acc = acc + p_ref[d]
        o_ref[...] = acc

    return pl.pallas_call(body, out_shape=jax.ShapeDtypeStruct(parts.shape[1:], F32), name="sum_devices")(parts)


ADA_SHARD = 9 * D_MODEL // N_CHIPS
ADA_TILE = 768
ADA_ROWS = 16


def ada_forward(c_rows, w_ada, b_shard):
    def body(c_ref, w_ref, b_ref, o_ref):
        cv = c_ref[...]
        o_ref[...] = _dot((cv * _sigmoid(cv)).astype(BF16), w_ref[...].astype(BF16), NN) + b_ref[...]

    return pl.pallas_call(
        body, grid=(DEPTH, ADA_SHARD // ADA_TILE),
        in_specs=[pl.BlockSpec((ADA_ROWS, D_MODEL), lambda l, n: (0, 0)),
                  pl.BlockSpec((None, D_MODEL, ADA_TILE), lambda l, n: (l, 0, n)),
                  pl.BlockSpec((None, 1, ADA_TILE), lambda l, n: (l, 0, n))],
        out_specs=pl.BlockSpec((None, ADA_ROWS, ADA_TILE), lambda l, n: (l, 0, n)),
        out_shape=jax.ShapeDtypeStruct((DEPTH, ADA_ROWS, ADA_SHARD), F32),
        name="ada_forward", compiler_params=_params(("parallel", "parallel")),
    )(c_rows, w_ada, b_shard)


def ada_backward(c_rows, dmod_rows):
    def body(c_ref, d_ref, o_ref):
        cv = c_ref[...]
        o_ref[...] = _dot((cv * _sigmoid(cv)).astype(BF16), d_ref[...].astype(BF16), TN)

    return pl.pallas_call(
        body, grid=(DEPTH, ADA_SHARD // ADA_TILE),
        in_specs=[pl.BlockSpec((ADA_ROWS, D_MODEL), lambda l, n: (0, 0)),
                  pl.BlockSpec((None, ADA_ROWS, ADA_TILE), lambda l, n: (l, 0, n))],
        out_specs=pl.BlockSpec((None, D_MODEL, ADA_TILE), lambda l, n: (l, 0, n)),
        out_shape=jax.ShapeDtypeStruct((DEPTH, D_MODEL, ADA_SHARD), F32),
        name="ada_backward", compiler_params=_params(("parallel", "parallel")),
    )(c_rows, dmod_rows)


def adamw(name, w, g, m, v):
    shape = w.shape
    cols = shape[-1]
    rows = w.size // cols
    tr = _row_tile(rows, cols, 512 * 1024) if rows % 16 == 0 else rows
    c1 = 1.0 / (1.0 - ADAM_B1 ** ADAM_STEP)
    c2 = 1.0 / (1.0 - ADAM_B2 ** ADAM_STEP)

    def body(w_ref, g_ref, m_ref, v_ref, go_ref, d_ref, mo_ref, vo_ref):
        gv = g_ref[...]
        mn = ADAM_B1 * m_ref[...] + (1.0 - ADAM_B1) * gv
        vn = ADAM_B2 * v_ref[...] + (1.0 - ADAM_B2) * (gv * gv)
        go_ref[...] = gv
        mo_ref[...] = mn
        vo_ref[...] = vn
        d_ref[...] = -ADAM_LR * ((mn * c1) / (jnp.sqrt(vn * c2) + ADAM_EPS) + ADAM_WD * w_ref[...])

    spec = pl.BlockSpec((tr, cols), lambda i: (i, 0))
    out = jax.ShapeDtypeStruct((rows, cols), F32)
    res = pl.pallas_call(
        body, grid=(rows // tr,), in_specs=[spec] * 4, out_specs=[spec] * 4, out_shape=[out] * 4,
        name=name, compiler_params=_params(("parallel",)),
    )(*[t.reshape(rows, cols) for t in (w, g, m, v)])
    return tuple(r.reshape(shape) for r in res)


def _pack(parts, rows):
    flat = jnp.concatenate([p.reshape(-1) for p in parts])
    return jnp.pad(flat, (0, rows * 128 - flat.size)).reshape(rows, 128)


def _unpack(flat, shapes):
    out, at = [], 0
    for s in shapes:
        n = math.prod(s)
        out.append(flat[at:at + n].reshape(s))
        at += n
    return out


def kernel(x, c, w_ada, b_ada, norm_gain, w_ffn_gate, w_ffn_up, w_ffn_down, w_in, w_br_sb, w_br_dil, w_br_swa, w_out, sinks, rel_bias, final_gain, loss_target, m_w_ada, m_b_ada, m_norm_gain, m_w_ffn_gate, m_w_ffn_up, m_w_ffn_down, m_w_in, m_w_br_sb, m_w_br_dil, m_w_br_swa, m_w_out, m_sinks, m_rel_bias, m_final_gain, v_w_ada, v_b_ada, v_norm_gain, v_w_ffn_gate, v_w_ffn_up, v_w_ffn_down, v_w_in, v_w_br_sb, v_w_br_dil, v_w_br_swa, v_w_out, v_sinks, v_rel_bias, v_final_gain):
    xi, yi, ci = _position()
    chip = 2 * xi + yi
    dev = 2 * chip + ci

    c_all = all_gather_small("gather_c", c.reshape(8, 128)).reshape(N_DEV, D_MODEL)
    c_rows = jnp.pad(c_all, ((0, ADA_ROWS - N_DEV), (0, 0)))
    b_shard = lax.dynamic_slice_in_dim(b_ada, chip * ADA_SHARD, ADA_SHARD, axis=1).reshape(DEPTH, 1, ADA_SHARD)
    mod_shard = ada_forward(c_rows, w_ada, b_shard)[:, :N_DEV]
    n_mod = DEPTH * N_DEV * ADA_SHARD
    gathered = all_gather_small("gather_mod", _pack([mod_shard, norm_gain], 304))[::2].reshape(N_CHIPS, -1)
    mod_all = gathered[:, :n_mod].reshape(N_CHIPS, DEPTH, N_DEV, ADA_SHARD)
    mod = lax.dynamic_index_in_dim(mod_all, dev, axis=2, keepdims=False)
    mod = mod.transpose(1, 0, 2).reshape(DEPTH, 3, 3, D_MODEL)
    gains = gathered[:, n_mod:n_mod + DEPTH * 3 * D_SHARD].reshape(N_CHIPS, DEPTH, 3, D_SHARD)
    gains = gains.transpose(1, 2, 0, 3).reshape(DEPTH, 3, D_MODEL)

    chip_i, core_i = chip.astype(jnp.int32).reshape(1), ci.astype(jnp.int32).reshape(1)
    w_br = jnp.concatenate([w_br_sb, w_br_dil, w_br_swa], axis=1)
    transposed = (3, 4)
    w_gate_t, w_up_t = jnp.swapaxes(w_ffn_gate, 2, 3), jnp.swapaxes(w_ffn_up, 2, 3)
    shards = []
    for l in range(DEPTH):
        ffn = [[(w_gate_t, (l, f)), (w_up_t, (l, f)), (w_ffn_down, (l, f))] for f in range(2)]
        shards += [ffn[0], [(w_in, (l,)), (w_br, (l,)), (w_out, (l,))], ffn[1]]
    weights_in = WeightStream(shards, chip_i, (gathered,))
    grads_out = GradStream(chip_i, core_i)

    loss, dx, dmod, dgains, dfinal, dsinks, drel, last_grads = device_step(
        x[0], loss_target[0], mod, gains, final_gain, sinks, rel_bias, weights_in.get, grads_out.put)

    small_shapes = [(DEPTH, 9 * D_MODEL), (DEPTH, 3, D_MODEL), (D_MODEL,), (DEPTH, H_SWA_Q), (N_BUCKETS, 12), (1,)]
    small_all = all_gather_small("gather_small_grads", _pack([dmod, dgains, dfinal, dsinks, drel, loss[0, 0:1]], 208))
    started = grads_out.put(0, last_grads, after=(small_all,))
    small_all = small_all + started
    g_b_ada, g_gain_full, g_final, g_sinks, g_rel, loss_sum = _unpack(sum_devices(small_all).reshape(-1), small_shapes)
    g_gain = lax.dynamic_slice_in_dim(g_gain_full, chip * D_SHARD, D_SHARD, axis=2)
    dmod_all = small_all.reshape(N_DEV, -1)[:, :DEPTH * 9 * D_MODEL].reshape(N_DEV, DEPTH, 9 * D_MODEL)
    dmod_rows = lax.dynamic_slice_in_dim(dmod_all, chip * ADA_SHARD, ADA_SHARD, axis=2).transpose(1, 0, 2)
    g_w_ada = ada_backward(c_rows, jnp.pad(dmod_rows, ((0, 0), (0, ADA_ROWS - N_DEV), (0, 0))))

    weights = [w_ada, b_ada, norm_gain, w_ffn_gate, w_ffn_up, w_ffn_down, w_in, w_br_sb, w_br_dil, w_br_swa, w_out,
               sinks, rel_bias, final_gain]
    ms = [m_w_ada, m_b_ada, m_norm_gain, m_w_ffn_gate, m_w_ffn_up, m_w_ffn_down, m_w_in, m_w_br_sb, m_w_br_dil,
          m_w_br_swa, m_w_out, m_sinks, m_rel_bias, m_final_gain]
    vs = [v_w_ada, v_b_ada, v_norm_gain, v_w_ffn_gate, v_w_ffn_up, v_w_ffn_down, v_w_in, v_w_br_sb, v_w_br_dil,
          v_w_br_swa, v_w_out, v_sinks, v_rel_bias, v_final_gain]
    grads = [g_w_ada, g_b_ada, g_gain] + [None] * 8 + [g_sinks, g_rel, g_final]

    deltas, new_ms, new_vs = [None] * 14, [None] * 14, [None] * 14
    for k in (0, 1, 2, 11, 12, 13):
        _, deltas[k], new_ms[k], new_vs[k] = adamw(f"adamw_{k}", weights[k], grads[k], ms[k], vs[k])

    g = grads_out.finish((dx, deltas[0], deltas[1]))
    g_br = g["br"]
    grads[3:11] = [g["gate"].reshape(w_gate_t.shape), g["up"].reshape(w_up_t.shape),
                   g["down"].reshape(w_ffn_down.shape), g["in"], g_br[:, 0:256], g_br[:, 256:384], g_br[:, 384:768],
                   g["out"]]
    for k in range(3, 11):
        state = [weights[k], ms[k], vs[k]]
        if k in transposed:
            state = [jnp.swapaxes(t, 2, 3) for t in state]
        out = adamw(f"adamw_{k}", state[0], grads[k], state[1], state[2])
        if k in transposed:
            out = [jnp.swapaxes(t, 2, 3) for t in out]
        grads[k], deltas[k], new_ms[k], new_vs[k] = out
    return (loss_sum[0], dx[None], *grads, *deltas, *new_ms, *new_vs)
```

```python
import functools
import math

import jax
import jax.numpy as jnp
from jax import lax
from jax.experimental import pallas as pl
from jax.experimental.pallas import tpu as pltpu

F32 = jnp.float32
BF16 = jnp.bfloat16

D_MODEL = 1024
SEQ = 2048
DEPTH = 2
HEAD_DIM = 64
BLK = 128
H_SB = 4
DIL_PATTERNS = ((128, 1), (512, 4), (2048, 16))
H_PER_DIL = 2
H_DIL = 6
H_SWA_Q = 6
H_SWA_KV = 2
SWA_WINDOW = 128
N_BUCKETS = 32
MAX_REL_DIST = 2048
D_FF = 2816
RMS_EPS = 1e-6
N_CHIPS = 4
N_DEV = 8
FF_SHARD = D_FF // N_CHIPS
D_QKV = 2560
D_IN = D_QKV + 3 * D_MODEL
IN_SHARD = D_IN // N_CHIPS
D_SHARD = D_MODEL // N_CHIPS
BR_ROWS = 768
NEG = -1e30
QK_SCALE = HEAD_DIM ** -0.5

ADAM_LR = 0.001
ADAM_B1 = 0.9
ADAM_B2 = 0.999
ADAM_EPS = 1e-08
ADAM_WD = 0.01
ADAM_STEP = 10

VMEM_LIMIT = 56 * 1024 * 1024
ROW_TILE = 256
MM_TILE = 1024

NN = (((1,), (0,)), ((), ()))
NT = (((1,), (1,)), ((), ()))
TN = (((0,), (0,)), ((), ()))


def _params(sem=None):
    return pltpu.CompilerParams(dimension_semantics=sem, vmem_limit_bytes=VMEM_LIMIT)


def _dot(a, b, dims):
    return lax.dot_general(a, b, dims, preferred_element_type=F32)


def _sigmoid(x):
    return 1.0 / (1.0 + jnp.exp(-x))


def _matmul(name, grid, nk, k_axis, dims, n_pairs, in_specs, out_specs, out_shape, acc_shape, epilogue,
            operands, sem, aliases=None, prologue=None):
    n_in = len(in_specs)
    n_out = len(out_specs)

    def partial(ins):
        tot = None
        for p in range(n_pairs):
            a = ins[2 * p][...]
            if prologue is not None:
                a = prologue(p, a, ins)
            d = _dot(a, ins[2 * p + 1][...], dims)
            tot = d if tot is None else tot + d
        return tot

    def body(*refs):
        ins, outs = refs[:n_in], refs[n_in:n_in + n_out]
        ids = tuple(pl.program_id(a) for a in range(len(grid)))
        if nk == 1:
            epilogue(partial(ins), ins, outs, ids)
            return
        acc = refs[n_in + n_out]
        k = ids[k_axis]

        @pl.when(k == 0)
        def _():
            acc[...] = partial(ins)

        @pl.when(k > 0)
        def _():
            acc[...] += partial(ins)

        @pl.when(k == nk - 1)
        def _():
            epilogue(acc[...], ins, outs, ids)

    return pl.pallas_call(
        body, grid=grid, in_specs=in_specs, out_specs=out_specs, out_shape=out_shape,
        scratch_shapes=[] if nk == 1 else [pltpu.VMEM(acc_shape, F32)],
        input_output_aliases=aliases or {}, name=name, compiler_params=_params(sem),
    )(*operands)


def _row_spec(width=D_MODEL):
    return pl.BlockSpec((ROW_TILE, width), lambda i: (i, 0))


def _vec_spec(rows=1, width=D_MODEL):
    return pl.BlockSpec((rows, width), lambda i: (0, 0))


class Row:
    def __init__(self, table, index):
        self.table, self.index = table, index

    def spec(self):
        index = self.index
        return pl.BlockSpec((None, 1, D_MODEL), lambda *ids: (index, 0, 0))


def _slot_spec(u):
    return pl.BlockSpec((8, D_MODEL), lambda *ids: (u, 0))


def prenorm(x, gain, scale, shift):
    def body(x_ref, g_ref, sc_ref, sh_ref, h_ref):
        xv = x_ref[...]
        r = lax.rsqrt(jnp.mean(xv * xv, axis=-1, keepdims=True) + RMS_EPS)
        h_ref[...] = (((xv * r) * g_ref[...]) * (1.0 + sc_ref[...]) + sh_ref[...]).astype(BF16)

    return pl.pallas_call(
        body, grid=(SEQ // ROW_TILE,), in_specs=[_row_spec(), gain.spec(), scale.spec(), shift.spec()],
        out_specs=_row_spec(), out_shape=jax.ShapeDtypeStruct((SEQ, D_MODEL), BF16),
        name="prenorm", compiler_params=_params(("parallel",)),
    )(x, gain.table, scale.table, shift.table)


def resid_bwd(dxo, f, coef, mult, sums, u):
    def body(dx_ref, f_ref, c_ref, sums_in, df_ref, dc_ref):
        del sums_in
        dx = dx_ref[...]
        df_ref[...] = (dx * (mult * c_ref[...])).astype(BF16)
        part = mult * jnp.sum(dx * f_ref[...], axis=0, keepdims=True)

        @pl.when(pl.program_id(0) == 0)
        def _():
            dc_ref[...] = jnp.zeros_like(dc_ref)

        dc_ref[0:1, :] += part

    return pl.pallas_call(
        body, grid=(SEQ // ROW_TILE,),
        in_specs=[_row_spec(), _row_spec(), coef.spec(), pl.BlockSpec(memory_space=pl.ANY)],
        out_specs=[_row_spec(), _slot_spec(u)],
        out_shape=[jax.ShapeDtypeStruct((SEQ, D_MODEL), BF16), jax.ShapeDtypeStruct(sums.shape, F32)],
        input_output_aliases={3: 1}, name="resid_bwd", compiler_params=_params(("arbitrary",)),
    )(dxo, f, coef.table, sums)


def final_loss(x, gain, target):
    def body(x_ref, g_ref, t_ref, loss_ref, dx_ref, dg_ref):
        xv = x_ref[...]
        g = g_ref[...]
        r = lax.rsqrt(jnp.mean(xv * xv, axis=-1, keepdims=True) + RMS_EPS)
        xh = xv * r
        e = xh * g - t_ref[...]
        part = 0.5 * jnp.sum(jnp.mean(e * e, axis=-1, keepdims=True), axis=0, keepdims=True)
        dy = e * (1.0 / D_MODEL)
        dyg = dy * g
        dx_ref[...] = r * (dyg - xh * jnp.mean(dyg * xh, axis=-1, keepdims=True))

        @pl.when(pl.program_id(0) == 0)
        def _():
            loss_ref[...] = jnp.zeros_like(loss_ref)
            dg_ref[...] = jnp.zeros_like(dg_ref)

        loss_ref[...] += jnp.broadcast_to(part, loss_ref.shape)
        dg_ref[0:1, :] += jnp.sum(dy * xh, axis=0, keepdims=True)

    return pl.pallas_call(
        body, grid=(SEQ // ROW_TILE,), in_specs=[_row_spec(), _vec_spec(), _row_spec()],
        out_specs=[_vec_spec(8, 128), _row_spec(), _vec_spec(8)],
        out_shape=[jax.ShapeDtypeStruct((8, 128), F32), jax.ShapeDtypeStruct((SEQ, D_MODEL), F32),
                   jax.ShapeDtypeStruct((8, D_MODEL), F32)],
        name="final_loss", compiler_params=_params(("arbitrary",)),
    )(x, gain, target)


def _prenorm_bwd_epilogue(dh, x_ref, dxo_ref, g_ref, sc_ref, dx_ref, stats_ref, first):
    xv = x_ref[...]
    g = g_ref[...]
    r = lax.rsqrt(jnp.mean(xv * xv, axis=-1, keepdims=True) + RMS_EPS)
    xh = xv * r
    dn = dh * (1.0 + sc_ref[...])
    dxh = dn * g
    dx = dxo_ref[...] + r * (dxh - xh * jnp.mean(dxh * xh, axis=-1, keepdims=True))
    dx_ref[...] = dx

    @pl.when(first)
    def _():
        stats_ref[...] = jnp.zeros_like(stats_ref)

    stats_ref[0:1, :] += jnp.sum(dh, axis=0, keepdims=True)
    stats_ref[1:2, :] += jnp.sum(dh * (xh * g), axis=0, keepdims=True)
    stats_ref[2:3, :] += jnp.sum(dn * xh, axis=0, keepdims=True)
    return dx


def _resid_bwd_epilogue(dx, f_ref, c_ref, mult, df_ref, dc_ref, first):
    df_ref[...] = (dx * (mult * c_ref[...])).astype(BF16)

    @pl.when(first)
    def _():
        dc_ref[...] = jnp.zeros_like(dc_ref)

    dc_ref[0:1, :] += mult * jnp.sum(dx * f_ref[...], axis=0, keepdims=True)


def ffn_up(h, wg_all, wu_all):
    def body(h_ref, wg_ref, wu_ref, a_ref, b_ref, s_ref):
        hv = h_ref[...]
        a = _dot(hv, wg_ref[...], NT)
        b = _dot(hv, wu_ref[...], NT)
        a_ref[...] = a.astype(BF16)
        b_ref[...] = b.astype(BF16)
        s_ref[...] = (a * _sigmoid(a) * b).astype(BF16)

    w_spec = pl.BlockSpec((None, FF_SHARD, D_MODEL), lambda j, i: (j, 0, 0))
    o_spec = pl.BlockSpec((None, MM_TILE, FF_SHARD), lambda j, i: (j, i, 0))
    hid = (N_CHIPS, SEQ, FF_SHARD)
    return pl.pallas_call(
        body, grid=(N_CHIPS, SEQ // MM_TILE),
        in_specs=[pl.BlockSpec((MM_TILE, D_MODEL), lambda j, i: (i, 0)), w_spec, w_spec],
        out_specs=[o_spec, o_spec, o_spec],
        out_shape=[jax.ShapeDtypeStruct(hid, BF16)] * 3,
        name="ffn_up", compiler_params=_params(("parallel", "parallel")),
    )(h, wg_all, wu_all)


def matmul_residual(name, a, a_spec, take, w_all, x, coef, mult, then=None):
    n_rows = 0 if then is None else 3

    def body(a_ref, w_hbm, x_ref, c_ref, *rest):
        rows, (f_ref, xo_ref), w_vmem, sem = rest[:n_rows], rest[n_rows:n_rows + 2], rest[-2], rest[-1]
        h_ref = rest[n_rows + 2] if then is not None else None

        @pl.when(pl.program_id(0) == 0)
        def _():
            copy = pltpu.make_async_copy(w_hbm, w_vmem, sem)
            copy.start()
            copy.wait()

        acc = None
        for j in range(N_CHIPS):
            part = _dot(take(a_ref, j), w_vmem[j], NN)
            acc = part if acc is None else acc + part
        f_ref[...] = acc
        xo = x_ref[...] + (mult * c_ref[...]) * acc
        xo_ref[...] = xo
        if then is not None:
            r = lax.rsqrt(jnp.mean(xo * xo, axis=-1, keepdims=True) + RMS_EPS)
            h_ref[...] = (((xo * r) * rows[0][...]) * (1.0 + rows[1][...]) + rows[2][...]).astype(BF16)

    row = pl.BlockSpec((MM_TILE, D_MODEL), lambda i: (i, 0))
    f32 = jax.ShapeDtypeStruct((SEQ, D_MODEL), F32)
    extra = [] if then is None else [then.gain, then.scale, then.shift]
    return pl.pallas_call(
        body, grid=(SEQ // MM_TILE,),
        in_specs=[a_spec, pl.BlockSpec(memory_space=pl.ANY), row, coef.spec()] + [t.spec() for t in extra],
        out_specs=[row] * (2 + bool(extra)),
        out_shape=[f32, f32] + [jax.ShapeDtypeStruct((SEQ, D_MODEL), BF16)] * bool(extra),
        scratch_shapes=[pltpu.VMEM(w_all.shape, w_all.dtype), pltpu.SemaphoreType.DMA],
        name=name, compiler_params=_params(("arbitrary",)),
    )(a, w_all, x, coef.table, *[t.table for t in extra])


def ffn_down(s, wd_all, x, gate, then):
    return matmul_residual(
        "ffn_down", s, pl.BlockSpec((N_CHIPS, MM_TILE, FF_SHARD), lambda i: (0, i, 0)), lambda ref, j: ref[j],
        wd_all, x, gate, 0.5, then)


def ffn_bwd_hidden(df, wd_all, a, b):
    def epilogue(ds, ins, outs, ids):
        av, bv = ins[2][...].astype(F32), ins[3][...].astype(F32)
        sig = _sigmoid(av)
        outs[0][...] = (ds * bv * (sig * (1.0 + av * (1.0 - sig)))).astype(BF16)
        outs[1][...] = (ds * (av * sig)).astype(BF16)

    hid_spec = pl.BlockSpec((None, MM_TILE, FF_SHARD), lambda j, i: (j, i, 0))
    hid = jax.ShapeDtypeStruct((N_CHIPS, SEQ, FF_SHARD), BF16)
    return _matmul(
        "ffn_bwd_hidden", (N_CHIPS, SEQ // MM_TILE), 1, None, NT, 1,
        [pl.BlockSpec((MM_TILE, D_MODEL), lambda j, i: (i, 0)),
         pl.BlockSpec((None, FF_SHARD, D_MODEL), lambda j, i: (j, 0, 0)), hid_spec, hid_spec],
        [hid_spec, hid_spec], [hid, hid], None, epilogue, (df, wd_all, a, b), ("parallel", "parallel"))


def grad_weight(name, lhs, lhs_spec, rhs, rhs_spec, shape):
    def epilogue(acc, ins, outs, ids):
        outs[0][...] = acc.astype(BF16)

    return _matmul(
        name, (N_CHIPS, SEQ // MM_TILE), SEQ // MM_TILE, 1, TN, 1,
        [lhs_spec, rhs_spec], [pl.BlockSpec((None,) + shape, lambda j, k: (j, 0, 0))],
        [jax.ShapeDtypeStruct((N_CHIPS,) + shape, BF16)], shape, epilogue, (lhs, rhs), ("parallel", "arbitrary"))[0]


def ffn_grad_weights(h, s, df, da, db):
    tok = pl.BlockSpec((MM_TILE, D_MODEL), lambda j, k: (k, 0))
    hid = pl.BlockSpec((None, MM_TILE, FF_SHARD), lambda j, k: (j, k, 0))
    n_k = SEQ // MM_TILE

    def body(da_ref, db_ref, s_ref, h_ref, df_ref, gg_ref, gu_ref, gd_ref, acc_g, acc_u, acc_d):
        k = pl.program_id(1)
        hv = h_ref[...]
        parts = (_dot(da_ref[...], hv, TN), _dot(db_ref[...], hv, TN), _dot(s_ref[...], df_ref[...], TN))

        @pl.when(k == 0)
        def _():
            acc_g[...], acc_u[...], acc_d[...] = parts

        @pl.when(k > 0)
        def _():
            acc_g[...] += parts[0]
            acc_u[...] += parts[1]
            acc_d[...] += parts[2]

        @pl.when(k == n_k - 1)
        def _():
            gg_ref[...] = acc_g[...].astype(BF16)
            gu_ref[...] = acc_u[...].astype(BF16)
            gd_ref[...] = acc_d[...].astype(BF16)

    out = pl.BlockSpec((None, FF_SHARD, D_MODEL), lambda j, k: (j, 0, 0))
    shape = jax.ShapeDtypeStruct((N_CHIPS, FF_SHARD, D_MODEL), BF16)
    return pl.pallas_call(
        body, grid=(N_CHIPS, n_k), in_specs=[hid, hid, hid, tok, tok], out_specs=[out] * 3, out_shape=[shape] * 3,
        scratch_shapes=[pltpu.VMEM((FF_SHARD, D_MODEL), F32)] * 3,
        name="ffn_grad_weights", compiler_params=_params(("parallel", "arbitrary")),
    )(da, db, s, h, df)


BWD_TILE = 512


def matmul_prenorm_bwd(name, dims, pairs, pair_specs, x, dxo, gain, scale, sums, u, below):
    n_pairs = len(pairs) // 2
    lhs, weights = pairs[0::2], pairs[1::2]
    n_in = n_pairs * 2 + 5 + (3 if below is not None else 0)
    n_out = 2 + (2 if below is not None else 0)

    def body(*refs):
        ins, outs = refs[:n_in], refs[n_in:n_in + n_out]
        w_vmem, w_sems = refs[n_in + n_out:n_in + n_out + n_pairs], refs[-1]
        first = pl.program_id(0) == 0

        @pl.when(first)
        def _():
            copies = [pltpu.make_async_copy(ins[n_pairs + p], w_vmem[p], w_sems.at[p]) for p in range(n_pairs)]
            for cp in copies:
                cp.start()
            for cp in copies:
                cp.wait()

        dh = None
        for j in range(N_CHIPS):
            for p in range(n_pairs):
                part = _dot(pair_specs[p][1](ins[p], j), w_vmem[p][j], dims)
                dh = part if dh is None else dh + part
        k = 2 * n_pairs
        dx = _prenorm_bwd_epilogue(dh, ins[k], ins[k + 1], ins[k + 2], ins[k + 3], outs[0], outs[1], first)
        if below is not None:
            _resid_bwd_epilogue(dx, ins[k + 5], ins[k + 6], below[2], outs[2], outs[3], first)

    row = pl.BlockSpec((BWD_TILE, D_MODEL), lambda i: (i, 0))
    any_spec = pl.BlockSpec(memory_space=pl.ANY)
    f32 = jax.ShapeDtypeStruct((SEQ, D_MODEL), F32)
    in_specs = [s for s, _ in pair_specs] + [any_spec] * n_pairs + [row, row, gain.spec(), scale.spec(), any_spec]
    operands = tuple(lhs) + tuple(weights) + (x, dxo, gain.table, scale.table, sums[0])
    out_specs, out_shape = [row, _slot_spec(u)], [f32, jax.ShapeDtypeStruct(sums[0].shape, F32)]
    aliases = {2 * n_pairs + 4: 1}
    if below is not None:
        in_specs += [row, below[1].spec(), any_spec]
        operands += (below[0], below[1].table, sums[1])
        out_specs += [row, _slot_spec(u - 1)]
        out_shape += [jax.ShapeDtypeStruct((SEQ, D_MODEL), BF16), jax.ShapeDtypeStruct(sums[1].shape, F32)]
        aliases[2 * n_pairs + 7] = 3
    out = pl.pallas_call(
        body, grid=(SEQ // BWD_TILE,), in_specs=in_specs, out_specs=out_specs, out_shape=out_shape,
        scratch_shapes=[pltpu.VMEM(w.shape, w.dtype) for w in weights] + [pltpu.SemaphoreType.DMA((n_pairs,))],
        input_output_aliases=aliases, name=name, compiler_params=_params(("arbitrary",)),
    )(*operands)
    if below is None:
        return out[0], (out[1], sums[1])
    return out[0], (out[1], out[3]), out[2]


def ffn_bwd_input(da, db, wg_all, wu_all, x, dxo, gain, scale, sums, u, below):
    hid = (pl.BlockSpec((N_CHIPS, BWD_TILE, FF_SHARD), lambda i: (0, i, 0)), lambda ref, j: ref[j])
    return matmul_prenorm_bwd("ffn_bwd_input", NN, (da, wg_all, db, wu_all), (hid, hid), x, dxo, gain, scale,
                              sums, u, below)


def in_proj(h, w_all):
    def epilogue(acc, ins, outs, ids):
        outs[0][...] = acc
        outs[1][...] = acc.astype(BF16)

    out = pl.BlockSpec((MM_TILE, IN_SHARD), lambda j, i: (i, j))
    return _matmul(
        "in_proj", (N_CHIPS, SEQ // MM_TILE), 1, None, NN, 1,
        [pl.BlockSpec((MM_TILE, D_MODEL), lambda j, i: (i, 0)),
         pl.BlockSpec((None, D_MODEL, IN_SHARD), lambda j, i: (j, 0, 0))],
        [out, out], [jax.ShapeDtypeStruct((SEQ, D_IN), F32), jax.ShapeDtypeStruct((SEQ, D_IN), BF16)],
        None, epilogue, (h, w_all), ("parallel", "parallel"))


_GATE_BLOCK0 = D_QKV // D_SHARD


BRANCH_ROWS = ((0, 256), (256, 384), (384, 768))


def _branch_products(o_refs, w_ref):
    return tuple(_dot(o[...].astype(BF16), w_ref[lo:hi, :], NN) for o, (lo, hi) in zip(o_refs, BRANCH_ROWS))


def _branch_specs(index):
    return [pl.BlockSpec((MM_TILE, hi - lo), index) for lo, hi in BRANCH_ROWS]


def merge_branches(o_parts, wbr_all, proj):
    def body(o0_ref, o1_ref, o2_ref, w_ref, g0_ref, g1_ref, g2_ref, m_ref):
        u = _branch_products((o0_ref, o1_ref, o2_ref), w_ref)
        m_ref[...] = (_sigmoid(g0_ref[...]) * u[0] + _sigmoid(g1_ref[...]) * u[1]
                      + _sigmoid(g2_ref[...]) * u[2]).astype(BF16)

    def gate_spec(b):
        return pl.BlockSpec((MM_TILE, D_SHARD), lambda i, j: (i, _GATE_BLOCK0 + 4 * b + j))

    return pl.pallas_call(
        body, grid=(SEQ // MM_TILE, N_CHIPS),
        in_specs=_branch_specs(lambda i, j: (i, 0))
        + [pl.BlockSpec((None, BR_ROWS, D_SHARD), lambda i, j: (j, 0, 0)), gate_spec(0), gate_spec(1), gate_spec(2)],
        out_specs=pl.BlockSpec((MM_TILE, D_SHARD), lambda i, j: (i, j)),
        out_shape=jax.ShapeDtypeStruct((SEQ, D_MODEL), BF16),
        name="merge_branches", compiler_params=_params(("parallel", "parallel")),
    )(*o_parts, wbr_all, proj, proj, proj)


def out_proj(merged, wout_all, x, gate, then):
    return matmul_residual(
        "out_proj", merged, pl.BlockSpec((MM_TILE, D_MODEL), lambda i: (i, 0)),
        lambda ref, j: ref[:, D_SHARD * j:D_SHARD * (j + 1)], wout_all, x, gate, 1.0, then)


def merge_bwd(dmo, wout_all, o_parts, wbr_all, proj):
    def epilogue(dm, ins, outs, ids):
        u = _branch_products(ins[2:5], ins[5])
        for b in range(3):
            sig = _sigmoid(ins[6 + b][...])
            outs[b][...] = (dm * sig).astype(BF16)
            outs[3 + b][...] = (dm * u[b] * (sig * (1.0 - sig))).astype(BF16)

    def gate_spec(b):
        return pl.BlockSpec((MM_TILE, D_SHARD), lambda j, i: (i, _GATE_BLOCK0 + 4 * b + j))

    col = pl.BlockSpec((MM_TILE, D_SHARD), lambda j, i: (i, j))
    du = jax.ShapeDtypeStruct((SEQ, D_MODEL), BF16)
    return _matmul(
        "merge_bwd", (N_CHIPS, SEQ // MM_TILE), 1, None, NT, 1,
        [pl.BlockSpec((MM_TILE, D_MODEL), lambda j, i: (i, 0)),
         pl.BlockSpec((None, D_SHARD, D_MODEL), lambda j, i: (j, 0, 0))] + _branch_specs(lambda j, i: (i, 0))
        + [pl.BlockSpec((None, BR_ROWS, D_SHARD), lambda j, i: (j, 0, 0)), gate_spec(0), gate_spec(1), gate_spec(2)],
        [col] * 6, [du] * 6,
        None, epilogue, (dmo, wout_all, *o_parts, wbr_all, proj, proj, proj), ("parallel", "parallel"))


def branch_bwd_input(du, wbr_all):
    def body(d0_ref, d1_ref, d2_ref, w_ref, o_ref, acc):
        j = pl.program_id(1)
        parts = (_dot(d0_ref[...], w_ref[0:256, :], NT), _dot(d1_ref[...], w_ref[256:384, :], NT),
                 _dot(d2_ref[...], w_ref[384:768, :], NT))

        @pl.when(j == 0)
        def _():
            acc[:, 0:256], acc[:, 256:384], acc[:, 384:768] = parts

        @pl.when(j > 0)
        def _():
            acc[:, 0:256] += parts[0]
            acc[:, 256:384] += parts[1]
            acc[:, 384:768] += parts[2]

        @pl.when(j == N_CHIPS - 1)
        def _():
            o_ref[...] = acc[...]

    col = pl.BlockSpec((MM_TILE, D_SHARD), lambda i, j: (i, j))
    return pl.pallas_call(
        body, grid=(SEQ // MM_TILE, N_CHIPS),
        in_specs=[col, col, col, pl.BlockSpec((None, BR_ROWS, D_SHARD), lambda i, j: (j, 0, 0))],
        out_specs=pl.BlockSpec((MM_TILE, BR_ROWS), lambda i, j: (i, 0)),
        out_shape=jax.ShapeDtypeStruct((SEQ, BR_ROWS), F32),
        scratch_shapes=[pltpu.VMEM((MM_TILE, BR_ROWS), F32)],
        name="branch_bwd_input", compiler_params=_params(("parallel", "arbitrary")),
    )(du[0], du[1], du[2], wbr_all)


def branch_grad_weights(o_parts, du):
    def body(o0_ref, o1_ref, o2_ref, d0_ref, d1_ref, d2_ref, g_ref, acc):
        k = pl.program_id(1)
        parts = tuple(_dot(o[...].astype(BF16), d[...], TN)
                      for o, d in zip((o0_ref, o1_ref, o2_ref), (d0_ref, d1_ref, d2_ref)))

        @pl.when(k == 0)
        def _():
            acc[0:256, :], acc[256:384, :], acc[384:768, :] = parts

        @pl.when(k > 0)
        def _():
            acc[0:256, :] += parts[0]
            acc[256:384, :] += parts[1]
            acc[384:768, :] += parts[2]

        @pl.when(k == SEQ // MM_TILE - 1)
        def _():
            g_ref[...] = acc[...].astype(BF16)

    col = pl.BlockSpec((MM_TILE, D_SHARD), lambda j, k: (k, j))
    return pl.pallas_call(
        body, grid=(N_CHIPS, SEQ // MM_TILE),
        in_specs=_branch_specs(lambda j, k: (k, 0)) + [col, col, col],
        out_specs=pl.BlockSpec((None, BR_ROWS, D_SHARD), lambda j, k: (j, 0, 0)),
        out_shape=jax.ShapeDtypeStruct((N_CHIPS, BR_ROWS, D_SHARD), BF16),
        scratch_shapes=[pltpu.VMEM((BR_ROWS, D_SHARD), F32)],
        name="branch_grad_weights", compiler_params=_params(("parallel", "arbitrary")),
    )(*o_parts, du[0], du[1], du[2])


def mixer_bwd_input(dproj, win_all, x, dxo, gain, scale, sums, u, below):
    columns = (pl.BlockSpec((BWD_TILE, D_IN), lambda i: (i, 0)),
               lambda ref, j: ref[:, IN_SHARD * j:IN_SHARD * (j + 1)])
    return matmul_prenorm_bwd("mixer_bwd_input", NT, (dproj, win_all), (columns,), x, dxo, gain, scale, sums, u, below)


BATCH_QK = (((2,), (2,)), ((0,), (0,)))
BATCH_PV = (((2,), (1,)), ((0,), (0,)))
BATCH_TN = (((1,), (1,)), ((0,), (0,)))


SB_WIDTH = H_SB * HEAD_DIM
SB_ROWS = H_SB * BLK


def _split_dot(v, tri):
    hi = v.astype(BF16)
    lo = (v - hi.astype(F32)).astype(BF16)
    return _dot(hi, tri, NN) + _dot(lo, tri, NN)


def _tri(cmp):
    return cmp(lax.broadcasted_iota(jnp.int32, (BLK, BLK), 0), lax.broadcasted_iota(jnp.int32, (BLK, BLK), 1)).astype(BF16)


def _head_masks():
    lane = lax.broadcasted_iota(jnp.int32, (1, SB_WIDTH), 1) // HEAD_DIM
    return [lane == h for h in range(H_SB)]


def _stack_heads(x, masks):
    return jnp.concatenate([jnp.where(m, x, jnp.zeros_like(x)) for m in masks], axis=0)


def _merge_heads(y, masks):
    out = jnp.where(masks[0], y[0:BLK], 0.0)
    for h in range(1, H_SB):
        out = jnp.where(masks[h], y[h * BLK:(h + 1) * BLK], out)
    return out


def _sb_scores(q4, k_ref, j, diagonal):
    rows = pl.ds(pl.multiple_of(j * BLK, BLK), BLK)
    z = _dot(q4, k_ref[rows, :], NT)
    log_fail = -(jnp.maximum(z, 0.0) + jnp.log(1.0 + jnp.exp(-jnp.abs(z))))
    log_hit = z + log_fail
    before = None
    if diagonal:
        tile = (SB_ROWS, BLK)
        before = lax.broadcasted_iota(jnp.int32, tile, 1) < (lax.broadcasted_iota(jnp.int32, tile, 0) & (BLK - 1))
        log_fail = jnp.where(before, log_fail, 0.0)
    return rows, before, log_fail, log_hit


def _keep(before, x):
    return x if before is None else jnp.where(before, x, 0.0)


def sb_forward(qkv):
    def body(q_ref, k_ref, v_ref, o_ref, tot_ref):
        i = pl.program_id(0)
        masks = _head_masks()
        q4 = _stack_heads(q_ref[...] * QK_SCALE, masks)
        later = _tri(lambda r, c: r > c)

        def tiles(js, carry, diagonal):
            o, run = carry
            scores = [_sb_scores(q4, k_ref, j, diagonal) for j in js]
            acc = None
            for rows, before, log_fail, log_hit in scores:
                between = _split_dot(log_fail, later) + run
                w = _keep(before, jnp.exp(log_hit + between))
                part = _dot(w.astype(BF16), v_ref[rows, :], NN)
                acc = part if acc is None else acc + part
                run = run + jnp.sum(log_fail, axis=1, keepdims=True)
            return o + _merge_heads(acc, masks), run

        carry = tiles([i], (jnp.zeros((BLK, SB_WIDTH), F32), jnp.zeros((SB_ROWS, 1), F32)), True)
        carry = lax.cond((i & 1) != 0, lambda c: tiles([i - 1], c, False), lambda c: c, carry)
        at = i - 1 - (i & 1)
        carry = lax.cond((i & 2) != 0, lambda c: tiles([at, at - 1], c, False), lambda c: c, carry)
        at = at - (i & 2)
        o, run = lax.fori_loop(0, i // 4, lambda t, c: tiles([at - 4 * t - n for n in range(4)], c, False), carry)
        o_ref[...] = o
        tot_ref[...] = run

    return pl.pallas_call(
        body, grid=(N_BLK,),
        in_specs=[pl.BlockSpec((BLK, SB_WIDTH), lambda i: (i, 0)), pl.BlockSpec((SEQ, SB_WIDTH), lambda i: (0, 1)),
                  pl.BlockSpec((SEQ, SB_WIDTH), lambda i: (0, 2))],
        out_specs=[pl.BlockSpec((BLK, SB_WIDTH), lambda i: (i, 0)), pl.BlockSpec((None, SB_ROWS, 1), lambda i: (i, 0, 0))],
        out_shape=[jax.ShapeDtypeStruct((SEQ, SB_WIDTH), F32), jax.ShapeDtypeStruct((N_BLK, SB_ROWS, 1), F32)],
        name="sb_forward", compiler_params=_params(("parallel",)),
    )(qkv, qkv, qkv)


def sb_backward(qkv, total, do_cat):
    def body(q_ref, k_ref, v_ref, tot_ref, do_ref, dq_ref, dk_ref, dv_ref):
        i = pl.program_id(0)

        @pl.when(i == 0)
        def _():
            dk_ref[...] = jnp.zeros_like(dk_ref)
            dv_ref[...] = jnp.zeros_like(dv_ref)

        masks = _head_masks()
        q4 = _stack_heads(q_ref[...] * QK_SCALE, masks)
        do4 = _stack_heads(do_ref[...].astype(BF16), masks)
        total_v = tot_ref[...]
        upto = _tri(lambda r, c: r <= c)
        earlier = _tri(lambda r, c: r < c)

        def tiles(js, carry, diagonal):
            dq, seen, g_seen = carry
            scores = [_sb_scores(q4, k_ref, j, diagonal) for j in js]
            acc = None
            for rows, before, log_fail, log_hit in scores:
                between = total_v - (seen + _split_dot(log_fail, upto))
                w = _keep(before, jnp.exp(log_hit + between))
                g = _dot(do4, v_ref[rows, :], NT) * w
                g_earlier = g_seen + _split_dot(g, earlier)
                sig = jnp.exp(log_hit)
                dz = _keep(before, g * (1.0 - sig) - g_earlier * sig).astype(BF16)
                part = _dot(dz, k_ref[rows, :], NN)
                acc = part if acc is None else acc + part
                dk_ref[rows, :] += _dot(dz, q4, TN)
                dv_ref[rows, :] += _dot(w.astype(BF16), do4, TN)
                seen = seen + jnp.sum(log_fail, axis=1, keepdims=True)
                g_seen = g_seen + jnp.sum(g, axis=1, keepdims=True)
            return dq + _merge_heads(acc, masks), seen, g_seen

        zero = jnp.zeros((SB_ROWS, 1), F32)
        carry = lax.fori_loop(0, i // 4, lambda t, c: tiles([4 * t + n for n in range(4)], c, False),
                              (jnp.zeros((BLK, SB_WIDTH), F32), zero, zero))
        at = i - (i & 3)
        carry = lax.cond((i & 2) != 0, lambda c: tiles([at, at + 1], c, False), lambda c: c, carry)
        carry = lax.cond((i & 1) != 0, lambda c: tiles([i - 1], c, False), lambda c: c, carry)
        dq, _, _ = tiles([i], carry, True)
        dq_ref[...] = dq * QK_SCALE

    blk = pl.BlockSpec((BLK, SB_WIDTH), lambda i: (i, 0))
    full = pl.BlockSpec((SEQ, SB_WIDTH), lambda i: (0, 0))
    shape = jax.ShapeDtypeStruct((SEQ, SB_WIDTH), F32)
    return pl.pallas_call(
        body, grid=(N_BLK,),
        in_specs=[blk, pl.BlockSpec((SEQ, SB_WIDTH), lambda i: (0, 1)), pl.BlockSpec((SEQ, SB_WIDTH), lambda i: (0, 2)),
                  pl.BlockSpec((None, SB_ROWS, 1), lambda i: (i, 0, 0)), blk],
        out_specs=[blk, full, full], out_shape=[shape, shape, shape],
        name="sb_backward", compiler_params=_params(("arbitrary",)),
    )(qkv, qkv, qkv, total, do_cat)


def _band_scores(q_ref, kp_ref, ko_ref, bias_ref, hb, prev_mask):
    b = pl.program_id(1)
    qs = q_ref[...]
    s_prev = _dot(qs, kp_ref[...], BATCH_QK) + bias_ref[:, :, 0:BLK]
    s_prev = jnp.concatenate(
        [jnp.where((b & prev_mask(pl.program_id(0) * hb + t)) != 0, s_prev[t:t + 1], NEG) for t in range(hb)], axis=0)
    s_own = _dot(qs, ko_ref[...], BATCH_QK) + bias_ref[:, :, BLK:2 * BLK]
    return qs, s_prev, s_own


def _band_specs(hb, rows, t_n):
    def q_spec(width):
        return pl.BlockSpec((hb, None, rows, width), lambda h, b: (h, b, 0, 0))

    own = pl.BlockSpec((hb, BLK, HEAD_DIM), lambda h, b: (h, b, 0))
    prev = pl.BlockSpec((hb, BLK, HEAD_DIM), lambda h, b: (h, jnp.maximum(b - 1, 0), 0))
    per_head = lambda r, width: pl.BlockSpec((hb, r, width), lambda h, b: (h, 0, 0))
    return q_spec, own, prev, per_head


def banded_forward(name, q, k, v, bias, sinks, hb, prev_mask):
    h_n, nb, rows, _ = q.shape

    def body(q_ref, kp_ref, ko_ref, vp_ref, vo_ref, bias_ref, sink_ref, o_ref, lse_ref):
        _, s_prev, s_own = _band_scores(q_ref, kp_ref, ko_ref, bias_ref, hb, prev_mask)
        sink = sink_ref[...]
        m = jnp.maximum(jnp.maximum(jnp.max(s_prev, axis=2, keepdims=True), jnp.max(s_own, axis=2, keepdims=True)), sink)
        p_prev = jnp.exp(s_prev - m)
        p_own = jnp.exp(s_own - m)
        denom = jnp.sum(p_prev, axis=2, keepdims=True) + jnp.sum(p_own, axis=2, keepdims=True) + jnp.exp(sink - m)
        o = _dot(p_prev.astype(BF16), vp_ref[...], BATCH_PV) + _dot(p_own.astype(BF16), vo_ref[...], BATCH_PV)
        o_ref[...] = o / denom
        lse_ref[...] = m + jnp.log(denom)

    q_spec, own, prev, per_head = _band_specs(hb, rows, k.shape[1])
    return pl.pallas_call(
        body, grid=(h_n // hb, nb),
        in_specs=[q_spec(HEAD_DIM), prev, own, prev, own, per_head(rows, 2 * BLK), per_head(rows, 1)],
        out_specs=[q_spec(HEAD_DIM), q_spec(1)],
        out_shape=[jax.ShapeDtypeStruct(q.shape, F32), jax.ShapeDtypeStruct((h_n, nb, rows, 1), F32)],
        name=name, compiler_params=_params(("parallel", "parallel")),
    )(q, k, k, v, v, bias, sinks)


def banded_backward(name, q, k, v, bias, sinks, o, lse, do, dlse, hb, prev_mask):
    h_n, nb, rows, _ = q.shape
    t_n = k.shape[1]

    def body(q_ref, kp_ref, ko_ref, vp_ref, vo_ref, bias_ref, sink_ref, o_ref, lse_ref, do_ref, dlse_ref,
             dq_ref, dk_ref, dv_ref, dbias_ref, dsink_ref):
        b = pl.program_id(1)

        @pl.when(b == 0)
        def _():
            dk_ref[...] = jnp.zeros_like(dk_ref)
            dv_ref[...] = jnp.zeros_like(dv_ref)
            dbias_ref[...] = jnp.zeros_like(dbias_ref)
            dsink_ref[...] = jnp.zeros_like(dsink_ref)

        qs, s_prev, s_own = _band_scores(q_ref, kp_ref, ko_ref, bias_ref, hb, prev_mask)
        lse_v = lse_ref[...]
        dov = do_ref[...]
        dob = dov.astype(BF16)
        shift = dlse_ref[...] - jnp.sum(dov * o_ref[...], axis=2, keepdims=True)
        p_prev = jnp.exp(s_prev - lse_v)
        p_own = jnp.exp(s_own - lse_v)
        ds_prev = p_prev * (_dot(dob, vp_ref[...], BATCH_QK) + shift)
        ds_own = p_own * (_dot(dob, vo_ref[...], BATCH_QK) + shift)
        dbias_ref[:, :, 0:BLK] += ds_prev
        dbias_ref[:, :, BLK:2 * BLK] += ds_own
        d_sink = jnp.exp(sink_ref[...] - lse_v) * shift
        for g in range(rows // BLK):
            dsink_ref[:, g:g + 1, :] += jnp.sum(d_sink[:, g * BLK:(g + 1) * BLK, :], axis=1, keepdims=True)
        ds_prev = ds_prev.astype(BF16)
        ds_own = ds_own.astype(BF16)
        dq_ref[...] = (_dot(ds_prev, kp_ref[...], BATCH_PV) + _dot(ds_own, ko_ref[...], BATCH_PV)) * QK_SCALE
        rows_prev = pl.ds(pl.multiple_of(jnp.maximum(b - 1, 0) * BLK, BLK), BLK)
        rows_own = pl.ds(pl.multiple_of(b * BLK, BLK), BLK)
        dk_ref[:, rows_prev, :] += _dot(ds_prev, qs, BATCH_TN)
        dk_ref[:, rows_own, :] += _dot(ds_own, qs, BATCH_TN)
        dv_ref[:, rows_prev, :] += _dot(p_prev.astype(BF16), dob, BATCH_TN)
        dv_ref[:, rows_own, :] += _dot(p_own.astype(BF16), dob, BATCH_TN)

    q_spec, own, prev, per_head = _band_specs(hb, rows, t_n)
    kv_full = per_head(t_n, HEAD_DIM)
    kv_shape = jax.ShapeDtypeStruct((h_n, t_n, HEAD_DIM), F32)
    return pl.pallas_call(
        body, grid=(h_n // hb, nb),
        in_specs=[q_spec(HEAD_DIM), prev, own, prev, own, per_head(rows, 2 * BLK), per_head(rows, 1),
                  q_spec(HEAD_DIM), q_spec(1), q_spec(HEAD_DIM), q_spec(1)],
        out_specs=[q_spec(HEAD_DIM), kv_full, kv_full, per_head(rows, 2 * BLK), per_head(rows // BLK, BLK)],
        out_shape=[jax.ShapeDtypeStruct(q.shape, F32), kv_shape, kv_shape,
                   jax.ShapeDtypeStruct((h_n, rows, 2 * BLK), F32), jax.ShapeDtypeStruct((h_n, rows // BLK, BLK), F32)],
        name=name, compiler_params=_params(("parallel", "arbitrary")),
    )(q, k, k, v, v, bias, sinks, o, lse, do, dlse)


def _swa_prev_mask(head):
    del head
    return 15


SWA_HEADS_PER_STEP = 2
SWA_GROUP = H_SWA_Q // H_SWA_KV
N_BLK = SEQ // BLK


GROUP_W = H_PER_DIL * HEAD_DIM
DIL_COLUMNS = (768, 1920)
LANE_BLOCKS = (DIL_COLUMNS[1] - DIL_COLUMNS[0]) // GROUP_W
DIL_Q_BLOCK, DIL_K_BLOCK, DIL_V_BLOCK = 0, 3, 6
N_GROUPS = len(DIL_PATTERNS)


def dilated_views(qkv):
    cols = qkv[:, DIL_COLUMNS[0]:DIL_COLUMNS[1]]
    return [_dil_view(cols, d) for _, d in DIL_PATTERNS]


def _dil_view(t, d):
    return t.reshape(SEQ // d, d * t.shape[1])


def _dil_tile(n, d):
    per_class = N_BLK // d
    return n // per_class, n % per_class


def _dil_spec(d, lane_block, lane_blocks, shift=0):
    def index(n):
        r, m = _dil_tile(n, d)
        m = jnp.clip(m + shift, 0, N_BLK // d - 1)
        return m, r * lane_blocks + lane_block
    return pl.BlockSpec((BLK, GROUP_W), index)


def _two_heads(x, first):
    zero = jnp.zeros_like(x)
    return jnp.concatenate([jnp.where(first, x, zero), jnp.where(first, zero, x)], axis=0)


def _per_head(col, first):
    return jnp.where(first, col[0:BLK], col[BLK:2 * BLK])


def _head_rows(tile, first):
    pick = lambda keep: jnp.max(jnp.where(keep, tile, -jnp.inf), axis=1, keepdims=True)
    return jnp.concatenate([pick(first), pick(jnp.logical_not(first))], axis=0)


def _dil_scores(q_ref, kp_ref, ko_ref, bias, has_prev, first):
    q2 = _two_heads(q_ref[...] * QK_SCALE, first)
    k2 = jnp.concatenate([kp_ref[...], ko_ref[...]], axis=0)
    s = _dot(q2, k2, NT) + bias
    key = lax.broadcasted_iota(jnp.int32, s.shape, 1)
    return q2, k2, jnp.where(jnp.logical_or(has_prev, key >= BLK), s, NEG)


def dilated_forward(views, bias):
    def body(*refs):
        ins, bias_ref, outs = refs[:5 * N_GROUPS], refs[5 * N_GROUPS], refs[5 * N_GROUPS + 1:]
        n = pl.program_id(0)
        first = lax.broadcasted_iota(jnp.int32, (1, GROUP_W), 1) < HEAD_DIM
        for g, (_, d) in enumerate(DIL_PATTERNS):
            q_ref, kp_ref, ko_ref, vp_ref, vo_ref = ins[5 * g:5 * g + 5]
            has_prev = _dil_tile(n, d)[1] > 0
            _, _, s = _dil_scores(q_ref, kp_ref, ko_ref, bias_ref[g], has_prev, first)
            m = jnp.max(s, axis=1, keepdims=True)
            p = jnp.exp(s - m)
            denom = jnp.sum(p, axis=1, keepdims=True)
            v2 = jnp.concatenate([vp_ref[...], vo_ref[...]], axis=0)
            o2 = _dot(p.astype(BF16), v2, NN) / denom
            outs[2 * g][...] = _per_head(o2, first)
            outs[2 * g + 1][...] = _per_head(m + jnp.log(denom), first)

    operands, in_specs, out_specs, out_shape = [], [], [], []
    for g, (_, d) in enumerate(DIL_PATTERNS):
        operands += [views[g]] * 5
        in_specs += [_dil_spec(d, DIL_Q_BLOCK + g, LANE_BLOCKS), _dil_spec(d, DIL_K_BLOCK + g, LANE_BLOCKS, -1),
                     _dil_spec(d, DIL_K_BLOCK + g, LANE_BLOCKS), _dil_spec(d, DIL_V_BLOCK + g, LANE_BLOCKS, -1),
                     _dil_spec(d, DIL_V_BLOCK + g, LANE_BLOCKS)]
        out_specs += [_dil_spec(d, 0, 1)] * 2
        out_shape += [jax.ShapeDtypeStruct((SEQ // d, d * GROUP_W), F32)] * 2
    out = pl.pallas_call(
        body, grid=(N_BLK,), in_specs=in_specs + [pl.BlockSpec((N_GROUPS, 2 * BLK, 2 * BLK), lambda n: (0, 0, 0))],
        out_specs=out_specs, out_shape=out_shape, name="dilated_forward", compiler_params=_params(("parallel",)),
    )(*operands, bias)
    out = [t.reshape(SEQ, GROUP_W) for t in out]
    return out[0::2], out[1::2]


def _group_softmax(lses):
    m = jnp.maximum(jnp.maximum(lses[0], lses[1]), lses[2])
    e = [jnp.exp(l - m) for l in lses]
    total = e[0] + e[1] + e[2]
    return [t / total for t in e]


def dilated_merge(o, lse):
    def body(*refs):
        alpha = _group_softmax([r[...] for r in refs[N_GROUPS:2 * N_GROUPS]])
        refs[-1][...] = alpha[0] * refs[0][...] + alpha[1] * refs[1][...] + alpha[2] * refs[2][...]

    spec = pl.BlockSpec((ROW_TILE, GROUP_W), lambda i: (i, 0))
    return pl.pallas_call(
        body, grid=(SEQ // ROW_TILE,), in_specs=[spec] * (2 * N_GROUPS), out_specs=spec,
        out_shape=jax.ShapeDtypeStruct((SEQ, GROUP_W), F32), name="dilated_merge", compiler_params=_params(("parallel",)),
    )(*o, *lse)


def dilated_merge_bwd(o, lse, do_cat):
    def body(*refs):
        o_v = [r[...] for r in refs[:N_GROUPS]]
        alpha = _group_softmax([r[...] for r in refs[N_GROUPS:2 * N_GROUPS]])
        dout = refs[2 * N_GROUPS][...]
        outs = refs[2 * N_GROUPS + 1:]
        first = lax.broadcasted_iota(jnp.int32, (1, GROUP_W), 1) < HEAD_DIM

        def head_sum(x):
            a = jnp.sum(jnp.where(first, x, 0.0), axis=1, keepdims=True)
            b = jnp.sum(jnp.where(first, 0.0, x), axis=1, keepdims=True)
            return jnp.where(first, a, b)

        dalpha = [head_sum(dout * o_g) for o_g in o_v]
        mean = alpha[0] * dalpha[0] + alpha[1] * dalpha[1] + alpha[2] * dalpha[2]
        for g in range(N_GROUPS):
            outs[g][...] = alpha[g] * dout
            outs[N_GROUPS + g][...] = alpha[g] * (dalpha[g] - mean)

    spec = pl.BlockSpec((ROW_TILE, GROUP_W), lambda i: (i, 0))
    shape = jax.ShapeDtypeStruct((SEQ, GROUP_W), F32)
    out = pl.pallas_call(
        body, grid=(SEQ // ROW_TILE,), in_specs=[spec] * (2 * N_GROUPS) + [pl.BlockSpec((ROW_TILE, GROUP_W), lambda i: (i, 2))],
        out_specs=[spec] * (2 * N_GROUPS), out_shape=[shape] * (2 * N_GROUPS),
        name="dilated_merge_bwd", compiler_params=_params(("parallel",)),
    )(*o, *lse, do_cat)
    return out[:N_GROUPS], out[N_GROUPS:]


def dilated_backward(views, bias, o, lse, do, dlse):
    n_in = 9

    def body(*refs):
        ins, bias_ref = refs[:n_in * N_GROUPS], refs[n_in * N_GROUPS]
        outs, dbias_ref = refs[n_in * N_GROUPS + 1:-1], refs[-1]
        n = pl.program_id(0)

        @pl.when(n == 0)
        def _():
            dbias_ref[...] = jnp.zeros_like(dbias_ref)

        first = lax.broadcasted_iota(jnp.int32, (1, GROUP_W), 1) < HEAD_DIM
        for g, (_, d) in enumerate(DIL_PATTERNS):
            q_ref, kp_ref, ko_ref, vp_ref, vo_ref, o_ref, lse_ref, do_ref, dlse_ref = ins[n_in * g:n_in * (g + 1)]
            has_prev = _dil_tile(n, d)[1] > 0
            q2, k2, s = _dil_scores(q_ref, kp_ref, ko_ref, bias_ref[g], has_prev, first)
            dov = do_ref[...]
            do2 = _two_heads(dov.astype(BF16), first)
            prod = dov * o_ref[...]
            delta = jnp.concatenate([jnp.sum(jnp.where(first, prod, 0.0), axis=1, keepdims=True),
                                     jnp.sum(jnp.where(first, 0.0, prod), axis=1, keepdims=True)], axis=0)
            shift = _head_rows(dlse_ref[...], first) - delta
            p = jnp.exp(s - _head_rows(lse_ref[...], first))
            v2 = jnp.concatenate([vp_ref[...], vo_ref[...]], axis=0)
            ds = p * (_dot(do2, v2, NT) + shift)
            dbias_ref[g] += ds
            ds = ds.astype(BF16)
            dq2 = _dot(ds, k2, NN) * QK_SCALE
            dk2 = _dot(ds, q2, TN)
            dv2 = _dot(p.astype(BF16), do2, TN)
            base = 5 * g
            outs[base][...] = jnp.where(first, dq2[0:BLK], dq2[BLK:2 * BLK])
            outs[base + 1][...] = dk2[BLK:2 * BLK]
            outs[base + 2][...] = dk2[0:BLK]
            outs[base + 3][...] = dv2[BLK:2 * BLK]
            outs[base + 4][...] = dv2[0:BLK]

    operands, in_specs, out_specs, out_shape = [], [], [], []
    for g, (_, d) in enumerate(DIL_PATTERNS):
        own = _dil_spec(d, 0, 1)
        operands += [views[g]] * 5 + [_dil_view(t[g], d) for t in (o, lse, do, dlse)]
        in_specs += [_dil_spec(d, DIL_Q_BLOCK + g, LANE_BLOCKS), _dil_spec(d, DIL_K_BLOCK + g, LANE_BLOCKS, -1),
                     _dil_spec(d, DIL_K_BLOCK + g, LANE_BLOCKS), _dil_spec(d, DIL_V_BLOCK + g, LANE_BLOCKS, -1),
                     _dil_spec(d, DIL_V_BLOCK + g, LANE_BLOCKS)] + [own] * 4
        out_specs += [own] * 5
        out_shape += [jax.ShapeDtypeStruct((SEQ // d, d * GROUP_W), F32)] * 5
    tiles = pl.BlockSpec((N_GROUPS, 2 * BLK, 2 * BLK), lambda n: (0, 0, 0))
    out = pl.pallas_call(
        body, grid=(N_BLK,), in_specs=in_specs + [tiles], out_specs=out_specs + [tiles],
        out_shape=out_shape + [jax.ShapeDtypeStruct((N_GROUPS, 2 * BLK, 2 * BLK), F32)],
        name="dilated_backward", compiler_params=_params(("arbitrary",)),
    )(*operands, bias)
    return [out[5 * g:5 * g + 5] for g in range(N_GROUPS)], out[-1]


def dilated_key_grads(parts):
    def body(*refs):
        ins, outs = refs[:4 * N_GROUPS], refs[4 * N_GROUPS:]
        n = pl.program_id(0)
        for g, (_, d) in enumerate(DIL_PATTERNS):
            has_next = _dil_tile(n, d)[1] < N_BLK // d - 1
            own_k, next_k, own_v, next_v = ins[4 * g:4 * g + 4]
            outs[2 * g][...] = own_k[...] + jnp.where(has_next, next_k[...], 0.0)
            outs[2 * g + 1][...] = own_v[...] + jnp.where(has_next, next_v[...], 0.0)

    operands, in_specs, out_specs, out_shape = [], [], [], []
    for g, (_, d) in enumerate(DIL_PATTERNS):
        _, dk_own, dk_prev, dv_own, dv_prev = parts[g]
        operands += [dk_own, dk_prev, dv_own, dv_prev]
        in_specs += [_dil_spec(d, 0, 1), _dil_spec(d, 0, 1, 1)] * 2
        out_specs += [_dil_spec(d, 0, 1)] * 2
        out_shape += [jax.ShapeDtypeStruct((SEQ // d, d * GROUP_W), F32)] * 2
    out = pl.pallas_call(
        body, grid=(N_BLK,), in_specs=in_specs, out_specs=out_specs, out_shape=out_shape,
        name="dilated_key_grads", compiler_params=_params(("parallel",)),
    )(*operands)
    tok = lambda ts: jnp.concatenate([t.reshape(SEQ, GROUP_W) for t in ts], axis=1)
    return tok([parts[g][0] for g in range(N_GROUPS)]), tok(out[0::2]), tok(out[1::2])


def rel_bias_reduce(dbias0, dbias1, bucket):
    def body(d0_ref, d1_ref, b_ref, o_ref):
        dv, bv = d0_ref[...] + d1_ref[...], b_ref[...]
        lane = lax.broadcasted_iota(jnp.int32, (1, BLK), 1)
        acc = jnp.zeros((1, BLK), F32)
        for bkt in range(N_BUCKETS):
            acc = acc + jnp.where(lane == bkt, jnp.sum(jnp.where(bv == bkt, dv, 0.0)), 0.0)
        o_ref[...] = acc

    tile = pl.BlockSpec((None, BLK, 2 * BLK), lambda h: (h, 0, 0))
    return pl.pallas_call(
        body, grid=(dbias0.shape[0],), in_specs=[tile, tile, tile],
        out_specs=pl.BlockSpec((None, 1, BLK), lambda h: (h, 0, 0)),
        out_shape=jax.ShapeDtypeStruct((dbias0.shape[0], 1, BLK), F32),
        name="rel_bias_reduce", compiler_params=_params(("parallel",)),
    )(dbias0, dbias1, bucket)


def _heads(t):
    return t.reshape(SEQ, -1, HEAD_DIM).transpose(1, 0, 2)


def _unheads(t):
    return t.transpose(1, 0, 2).reshape(SEQ, -1)


def _t5_bucket(n):
    max_exact = N_BUCKETS // 2
    nf = jnp.maximum(n, 1).astype(F32)
    large = max_exact + (jnp.log(nf / max_exact) / math.log(MAX_REL_DIST / max_exact)
                         * (N_BUCKETS - max_exact)).astype(jnp.int32)
    large = jnp.minimum(large, N_BUCKETS - 1)
    return jnp.where(n < max_exact, n, large)


def band_tables(rel_bias):
    rel = jnp.arange(BLK)[:, None] + BLK - jnp.arange(2 * BLK)[None, :]
    patterns = [(d, w // d, H_PER_DIL) for w, d in DIL_PATTERNS] + [(1, SWA_WINDOW - 1, H_SWA_Q)]
    buckets = []
    for d, max_dist, heads in patterns:
        band = (rel >= 0) & (rel <= max_dist)
        tile = jnp.where(band, _t5_bucket(jnp.maximum(rel, 0) * d), -1).astype(jnp.int32)
        buckets.append(jnp.broadcast_to(tile, (heads,) + tile.shape))
    buckets = jnp.concatenate(buckets, axis=0)

    def body(table_ref, b_ref, o_ref):
        h = pl.program_id(0)
        bv = b_ref[...]
        tile = jnp.full(bv.shape, NEG, F32)
        for bkt in range(N_BUCKETS):
            tile = jnp.where(bv == bkt, table_ref[h, bkt], tile)
        o_ref[...] = tile

    spec = pl.BlockSpec((None, BLK, 2 * BLK), lambda h: (h, 0, 0))
    tiles = pl.pallas_call(
        body, grid=(buckets.shape[0],), in_specs=[pl.BlockSpec(memory_space=pltpu.SMEM), spec], out_specs=spec,
        out_shape=jax.ShapeDtypeStruct(buckets.shape, F32), name="band_tables", compiler_params=_params(("parallel",)),
    )(rel_bias.T, buckets)
    return tiles[:H_DIL], tiles[H_DIL:], buckets


def _swa_rows(t):
    t = t.reshape(N_BLK, BLK, H_SWA_KV, SWA_GROUP, HEAD_DIM).transpose(2, 0, 3, 1, 4)
    return t.reshape(H_SWA_KV, N_BLK, SWA_GROUP * BLK, HEAD_DIM)


def _swa_tokens(t):
    t = t.reshape(H_SWA_KV, N_BLK, SWA_GROUP, BLK, HEAD_DIM).transpose(1, 3, 0, 2, 4)
    return t.reshape(SEQ, H_SWA_Q * HEAD_DIM)


def _sink_rows(sinks):
    return jnp.broadcast_to(sinks.reshape(H_SWA_KV, SWA_GROUP, 1, 1), (H_SWA_KV, SWA_GROUP, BLK, 1)).reshape(
        H_SWA_KV, SWA_GROUP * BLK, 1)


def _vec(v):
    return v.reshape(1, D_MODEL)


class UnitRows:
    def __init__(self, u, mod_table, gain_table):
        self.shift, self.scale, self.gate = (Row(mod_table, 3 * u + t) for t in range(3))
        self.gain = Row(gain_table, u)


def ffn_forward(x, h, rows, then, w):
    a, b, s = ffn_up(h, w[0], w[1])
    f, xo, *h_next = ffn_down(s, w[2], x, rows.gate, then)
    return xo, (h_next or [None])[0], (x, h, a, b, s, f)


def ffn_backward(u, dxo, df, saved, rows, w, sums, below):
    x, h, a, b, s, _ = saved
    da, db = ffn_bwd_hidden(df, w[2], a, b)
    grads = ffn_grad_weights(h, s, df, da, db)
    dx, sums, *df_below = ffn_bwd_input(da, db, w[0], w[1], x, dxo, rows.gain, rows.scale, sums, u, below)
    return dx, sums, df_below, grads


def mixer_forward(x, h, rows, then, sinks, bias_dil, bias_swa, w):
    proj, qkv = in_proj(h, w[0])
    q_swa, k_swa, v_swa = _swa_rows(qkv[:, 1920:2304] * QK_SCALE), _heads(qkv[:, 2304:2432]), _heads(qkv[:, 2432:2560])
    o_sb, total_sb = sb_forward(qkv)
    bias_dil = bias_dil.reshape(N_GROUPS, 2 * BLK, 2 * BLK)
    views = dilated_views(qkv)
    o_groups, lse_groups = dilated_forward(views, bias_dil)
    o_dil = dilated_merge(o_groups, lse_groups)
    bias_swa = bias_swa.reshape(H_SWA_KV, SWA_GROUP * BLK, 2 * BLK)
    o_swa, lse_swa = banded_forward("swa_forward", q_swa, k_swa, v_swa, bias_swa, _sink_rows(sinks), SWA_HEADS_PER_STEP,
                                    _swa_prev_mask)
    o_parts = (o_sb, o_dil, _swa_tokens(o_swa))
    merged = merge_branches(o_parts, w[1], proj)
    mo, xo, *h_next = out_proj(merged, w[2], x, rows.gate, then)
    saved = (x, h, proj, (qkv, total_sb), (views, o_groups, lse_groups),
             (q_swa, k_swa, v_swa, o_swa, lse_swa), o_parts, merged, mo)
    return xo, (h_next or [None])[0], saved


def mixer_backward(u, dxo, dmo, saved, rows, sinks, bias_dil, bias_swa, w, sums, below):
    x, h, proj, sb, dil, swa, o_parts, merged, _ = saved
    tok = pl.BlockSpec((MM_TILE, D_MODEL), lambda j, k: (k, 0))
    g_out = grad_weight("grad_w_out", merged, pl.BlockSpec((MM_TILE, D_SHARD), lambda j, k: (k, j)), dmo, tok,
                        (D_SHARD, D_MODEL))
    du0, du1, du2, dg0, dg1, dg2 = merge_bwd(dmo, w[2], o_parts, w[1], proj)
    du = (du0, du1, du2)
    do_cat = branch_bwd_input(du, w[1])
    g_br = branch_grad_weights(o_parts, du)

    qkv, total_sb = sb
    dq_sb, dk_sb, dv_sb = sb_backward(qkv, total_sb, do_cat)

    views, o_groups, lse_groups = dil
    bias_dil = bias_dil.reshape(N_GROUPS, 2 * BLK, 2 * BLK)
    do_groups, dlse_groups = dilated_merge_bwd(o_groups, lse_groups, do_cat)
    parts, dbias_dil = dilated_backward(views, bias_dil, o_groups, lse_groups, do_groups, dlse_groups)
    dq_dil, dk_dil, dv_dil = dilated_key_grads(parts)
    dbias_dil = dbias_dil.reshape(H_DIL, BLK, 2 * BLK)

    q_swa, k_swa, v_swa, o_swa, lse_swa = swa
    bias_swa = bias_swa.reshape(H_SWA_KV, SWA_GROUP * BLK, 2 * BLK)
    dq_swa, dk_swa, dv_swa, dbias_swa, dsinks = banded_backward(
        "swa_backward", q_swa, k_swa, v_swa, bias_swa, _sink_rows(sinks), o_swa, lse_swa, _swa_rows(do_cat[:, 384:768]),
        jnp.zeros_like(lse_swa), SWA_HEADS_PER_STEP, _swa_prev_mask)
    dbias_swa = dbias_swa.reshape(H_SWA_Q, BLK, 2 * BLK)

    pieces = [dq_sb, dk_sb, dv_sb, dq_dil, dk_dil, dv_dil, _swa_tokens(dq_swa), _unheads(dk_swa), _unheads(dv_swa)]
    dproj = jnp.concatenate([p.astype(BF16) for p in pieces] + [dg0, dg1, dg2], axis=1)
    g_in = grad_weight("grad_w_in", h, tok, dproj, pl.BlockSpec((MM_TILE, IN_SHARD), lambda j, k: (k, j)),
                       (D_MODEL, IN_SHARD))
    dx, sums, *df_below = mixer_bwd_input(dproj, w[0], x, dxo, rows.gain, rows.scale, sums, u, below)
    dbias = jnp.concatenate([dbias_dil, dbias_swa], axis=0)
    return dx, sums, df_below, dbias, dsinks[:, :, 0].reshape(H_SWA_Q), (g_in, g_br, g_out)


N_UNITS = 3 * DEPTH


def device_step(x, target, mod, gains, final_gain, sinks, rel_bias, get_weights, put_grads):
    bias_dil, bias_swa, bucket = band_tables(rel_bias)
    mod_table = mod.reshape(3 * N_UNITS, 1, D_MODEL)
    gain_table = gains.reshape(N_UNITS, 1, D_MODEL)
    saved, weights = [], []
    units = [UnitRows(u, mod_table, gain_table) for u in range(N_UNITS)]
    h = prenorm(x, units[0].gain, units[0].scale, units[0].shift)
    for u in range(N_UNITS):
        l, j = divmod(u, 3)
        w = get_weights(u, x)
        then = units[u + 1] if u + 1 < N_UNITS else None
        if j == 1:
            x, h, s = mixer_forward(x, h, units[u], then, sinks[l], bias_dil, bias_swa, w)
        else:
            x, h, s = ffn_forward(x, h, units[u], then, w)
        saved.append(s)
        weights.append(w)
    loss, dx, dfinal = final_loss(x, _vec(final_gain), target)

    sums = (lax.empty((8 * N_UNITS, D_MODEL), F32), lax.empty((8 * N_UNITS, D_MODEL), F32))
    dbias, dsinks = [None] * DEPTH, [None] * DEPTH
    zero = jnp.zeros((1, 1), F32)
    top = N_UNITS - 1
    df, gate_sums = resid_bwd(dx, saved[top][-1], units[top].gate, 0.5, sums[1], top)
    sums = (sums[0], gate_sums)
    for u in reversed(range(N_UNITS)):
        l, j = divmod(u, 3)
        rows = UnitRows(u, mod_table, gain_table + zero)
        below = (saved[u - 1][-1], units[u - 1].gate, 1.0 if (u - 1) % 3 == 1 else 0.5) if u > 0 else None
        if j == 1:
            dx, sums, df, dbias[l], dsinks[l], grads = mixer_backward(
                u, dx, df, saved[u], rows, sinks[l], bias_dil, bias_swa, weights[u], sums, below)
        else:
            dx, sums, df, grads = ffn_backward(u, dx, df, saved[u], rows, weights[u], sums, below)
        df = df[0] if df else None
        if u > 0:
            zero = put_grads(u, grads)
    drel = rel_bias_reduce(dbias[0], dbias[1], bucket)[:, 0, :N_BUCKETS].T
    norm_sums, gate_sums = (t.reshape(DEPTH, 3, 8, D_MODEL) for t in sums)
    dmod = jnp.stack([norm_sums[:, :, 0], norm_sums[:, :, 1], gate_sums[:, :, 0]], axis=2)
    return loss, dx, dmod, norm_sums[:, :, 2], dfinal[0], jnp.stack(dsinks), drel, grads


MESH = pl.DeviceIdType.MESH
CHIP_FLIPS = ((1, 0), (0, 1), (1, 1))
ANY = pl.BlockSpec(memory_space=pl.ANY)


def _position():
    return lax.axis_index("x"), lax.axis_index("y"), lax.axis_index("c")


def all_gather_small(name, piece):
    def body(x_ref, out_ref, send_sems, recv_sems, local_sem):
        x, y, c = _position()
        me, sibling = (x, y, c), (x, y, 1 - c)
        chips = [(x ^ fx, y ^ fy) for fx, fy in CHIP_FLIPS]

        def rows(px, py, pc):
            return out_ref.at[4 * px + 2 * py + pc]

        def copy(k, block, to, src=None):
            return pltpu.make_async_remote_copy(
                src_ref=rows(*block) if src is None else src, dst_ref=rows(*block),
                send_sem=send_sems.at[k], recv_sem=recv_sems.at[k], device_id=to, device_id_type=MESH)

        mine = pltpu.make_async_copy(x_ref, rows(*me), local_sem)
        mine.start()
        first = [copy(0, me, sibling, src=x_ref)]
        first += [copy(1 + j, me, (*chip, c), src=x_ref) for j, chip in enumerate(chips)]
        for cp in first:
            cp.start()
        passed = [copy(4 + j, (*chip, c), sibling) for j, chip in enumerate(chips)]
        for j, chip in enumerate(chips):
            copy(1 + j, (*chip, c), me).wait_recv()
            passed[j].start()
        copy(0, sibling, me).wait_recv()
        for j, chip in enumerate(chips):
            copy(4 + j, (*chip, 1 - c), me).wait_recv()
        for cp in first + passed:
            cp.wait_send()
        mine.wait()

    return pl.pallas_call(
        body, out_shape=jax.ShapeDtypeStruct((N_DEV,) + piece.shape, piece.dtype),
        in_specs=[pl.BlockSpec(memory_space=pltpu.VMEM)], out_specs=pl.BlockSpec(memory_space=pltpu.VMEM),
        scratch_shapes=[pltpu.SemaphoreType.DMA((7,)), pltpu.SemaphoreType.DMA((7,)), pltpu.SemaphoreType.DMA],
        name=name,
    )(piece)


def exchange(name, operands, out_shapes, aliases, plan):
    n_in, n_out = len(operands), len(out_shapes)

    def body(*refs):
        ins, outs = refs[:n_in], refs[n_in:n_in + n_out]
        send_sems, recv_sems, local_sems = refs[n_in + n_out:]
        x, y, c = _position()
        local, sends, recvs = plan(ins, outs, x, y, c)
        local = [pltpu.make_async_copy(s, d, local_sems.at[k]) for k, (s, d) in enumerate(local)]
        for cp in local:
            cp.start()
        remote = [pltpu.make_async_remote_copy(src_ref=s, dst_ref=d, send_sem=send_sems.at[k], recv_sem=recv_sems.at[k],
                                               device_id=dev, device_id_type=MESH)
                  for k, (s, d, dev) in enumerate(sends)]
        for cp in remote:
            cp.start()
        for k, r in enumerate(recvs):
            pltpu.make_async_remote_copy(src_ref=r, dst_ref=r, send_sem=send_sems.at[k], recv_sem=recv_sems.at[k],
                                         device_id=(x, y, c), device_id_type=MESH).wait_recv()
        for cp in remote:
            cp.wait_send()
        for cp in local:
            cp.wait()

    n_sends, n_local = plan.n_sends, max(plan.n_local, 1)
    return pl.pallas_call(
        body, out_shape=out_shapes, in_specs=[ANY] * n_in, out_specs=[ANY] * n_out,
        scratch_shapes=[pltpu.SemaphoreType.DMA((n_sends,)), pltpu.SemaphoreType.DMA((n_sends,)),
                        pltpu.SemaphoreType.DMA((n_local,))],
        input_output_aliases=aliases, name=name,
    )(*operands)


def _plan(n_local, n_sends):
    def wrap(fn):
        fn.n_local, fn.n_sends = n_local, n_sends
        return fn
    return wrap


def _half(ref, axis, c):
    rows = ref.shape[axis] // 2
    idx = [slice(None)] * len(ref.shape)
    idx[axis] = pl.ds(pl.multiple_of(c * rows, 16), rows)
    return ref.at[tuple(idx)]


HBM = pl.BlockSpec(memory_space=pltpu.HBM)
SEM = pl.BlockSpec(memory_space=pltpu.SEMAPHORE)
EFFECT = pltpu.SideEffectType.DATAFLOW_SIDE_EFFECTING


def split_start(name, bufs, extra, n_copies, describe):
    n = len(bufs)

    def body(*refs):
        send_sems, recv_sems = refs[n + len(extra)], refs[n + len(extra) + 1]
        x, y, c = _position()
        for k, (src, dst, _, peer) in enumerate(describe(refs[:n], x, y, c)):
            pltpu.make_async_remote_copy(src_ref=src, dst_ref=dst, send_sem=send_sems.at[k], recv_sem=recv_sems.at[k],
                                         device_id=peer, device_id_type=MESH).start()
        token = refs[-1]
        token[...] = jnp.zeros_like(token)

    out = pl.pallas_call(
        body, name=name,
        out_shape=(pltpu.SemaphoreType.DMA((n_copies,)), pltpu.SemaphoreType.DMA((n_copies,)),
                   *[pltpu.HBM(b.shape, b.dtype) for b in bufs], jax.ShapeDtypeStruct((8, 128), F32)),
        in_specs=[HBM] * n + [ANY] * len(extra),
        out_specs=(SEM, SEM, *[HBM] * n, pl.BlockSpec(memory_space=pltpu.VMEM)),
        input_output_aliases={k: 2 + k for k in range(n)},
        compiler_params=pltpu.CompilerParams(has_side_effects=EFFECT),
    )(*[pltpu.with_memory_space_constraint(b, pltpu.HBM) for b in bufs], *extra)
    return out[0], out[1], list(out[2:2 + n]), out[-1]


def split_wait(name, bufs, send_sems, recv_sems, after, describe):
    n = len(bufs)

    def body(*refs):
        send, recv = refs[n], refs[n + 1]
        x, y, c = _position()
        for k, (src, _, dst, peer) in enumerate(describe(refs[:n], x, y, c)):
            copy = pltpu.make_async_remote_copy(src_ref=src, dst_ref=dst, send_sem=send.at[k], recv_sem=recv.at[k],
                                                device_id=peer, device_id_type=MESH)
            copy.wait_send()
            copy.wait_recv()

    out = pl.pallas_call(
        body, name=name, out_shape=[pltpu.HBM(b.shape, b.dtype) for b in bufs],
        in_specs=[HBM] * n + [SEM, SEM] + [ANY] * len(after), out_specs=[HBM] * n,
        input_output_aliases={k: k for k in range(n)},
        compiler_params=pltpu.CompilerParams(has_side_effects=EFFECT),
    )(*bufs, send_sems, recv_sems, *after)
    return list(out)


def _row_tile(rows, cols, max_elements=256 * 1024):
    best = 16
    for t in range(16, rows + 1, 16):
        if rows % t == 0 and t * cols <= max_elements:
            best = t
    return best


def cast_into_slots(name, shards, chip):
    n = len(shards)
    rows, cols = shards[0][0].shape[-2:]
    tr = _row_tile(rows, cols)

    def body(chip_ref, *refs):
        del chip_ref
        for k in range(n):
            refs[n + k][...] = refs[k][...].astype(BF16)

    def in_spec(param, index):
        return pl.BlockSpec((None,) * len(index) + (tr, cols), lambda r, chip_ref: index + (r, 0))

    return pl.pallas_call(
        body, out_shape=[jax.ShapeDtypeStruct((N_CHIPS, rows, cols), BF16)] * n,
        grid_spec=pltpu.PrefetchScalarGridSpec(
            num_scalar_prefetch=1, grid=(rows // tr,),
            in_specs=[in_spec(p, idx) for p, idx in shards],
            out_specs=[pl.BlockSpec((None, tr, cols), lambda r, chip_ref: (chip_ref[0], r, 0))] * n),
        name=name, compiler_params=_params(("parallel",)),
    )(chip, *[p for p, _ in shards])


GATHER_STAGES = ((0,), (1,), (2,), (3, 4, 5))
REDUCE_STAGES = ((5, 4, 3), (2,), (1,), (0,))


def _gather_copies(slots, x, y, c):
    me = 2 * x + y
    out = []
    for s in slots:
        for fx, fy in CHIP_FLIPS:
            mine = _half(s.at[me], 0, c)
            out.append((mine, mine, _half(s.at[2 * (x ^ fx) + (y ^ fy)], 0, c), (x ^ fx, y ^ fy, c)))
    return out


class WeightStream:
    def __init__(self, shards, chip, after=()):
        self.pending, self.ready = {}, {}
        token = tuple(after)
        for si, units in enumerate(GATHER_STAGES):
            slots = []
            for u in units:
                same = len({p.shape[-2:] for p, _ in shards[u]}) == 1
                for t, group in enumerate([shards[u]] if same else [[s] for s in shards[u]]):
                    slots += cast_into_slots(f"cast_{u}_{t}", group, chip)
            send, recv, slots, tok = split_start(f"gather_start_{si}", slots, token, 3 * len(slots), _gather_copies)
            self.pending[si] = (send, recv, slots)
            token = (tok,)
        self.token = token

    def get(self, u, after):
        if u not in self.ready:
            si = next(k for k, units in enumerate(GATHER_STAGES) if u in units)
            send, recv, slots = self.pending.pop(si)
            slots = split_wait(f"gather_wait_{si}", slots, send, recv, (after,) + self.token, _gather_copies)
            self.token = ()

            @_plan(0, 3 * len(slots))
            def to_sibling(ins, outs, x, y, c):
                sends, recvs = [], []
                for o in outs:
                    for fx, fy in CHIP_FLIPS:
                        slab = o.at[2 * (x ^ fx) + (y ^ fy)]
                        sends.append((_half(slab, 0, c), _half(slab, 0, c), (x, y, 1 - c)))
                        recvs.append(_half(slab, 0, 1 - c))
                return [], sends, recvs

            shapes = [jax.ShapeDtypeStruct(s.shape, BF16) for s in slots]
            slots = exchange(f"gather_sibling_{si}", slots, shapes, {k: k for k in range(len(slots))}, to_sibling)
            for i, v in enumerate(GATHER_STAGES[si]):
                self.ready[v] = tuple(slots[3 * i:3 * i + 3])
        return self.ready[u]


def _reduce_copies(bufs, x, y, c):
    n = len(bufs) // 2
    out = []
    for s, land in zip(bufs[:n], bufs[n:]):
        for k, (fx, fy) in enumerate(CHIP_FLIPS):
            out.append((s.at[2 * (x ^ fx) + (y ^ fy)], land.at[k], land.at[k], (x ^ fx, y ^ fy, c)))
    return out


GRAD_SLOTS = {"gate": (2 * DEPTH, FF_SHARD, D_MODEL), "up": (2 * DEPTH, FF_SHARD, D_MODEL),
              "down": (2 * DEPTH, FF_SHARD, D_MODEL), "in": (DEPTH, D_MODEL, IN_SHARD),
              "br": (DEPTH, BR_ROWS, D_SHARD), "out": (DEPTH, D_SHARD, D_MODEL)}


def _unit_tensors(u):
    l, j = divmod(u, 3)
    if j == 1:
        return [("in", l), ("br", l), ("out", l)]
    return [(k, 2 * l + j // 2) for k in ("gate", "up", "down")]


class GradStream:
    def __init__(self, chip, core):
        self.core = core
        self.place = jnp.concatenate([chip, core])
        self.held, self.flying = {}, []
        self.full = {k: lax.empty(shape, F32) for k, shape in GRAD_SLOTS.items()}

    def put(self, u, grads, after=()):
        self.held[u] = grads
        si = len(self.flying)
        units = REDUCE_STAGES[si]
        if not all(v in self.held for v in units):
            return jnp.zeros((1, 1), F32)
        gs = [g for v in units for g in self.held[v]]

        @_plan(0, len(gs))
        def swap_halves(ins, outs, x, y, c):
            sends = [(_half(g, 1, 1 - c), o, (x, y, 1 - c)) for g, o in zip(ins, outs)]
            return [], sends, list(outs)

        half_shapes = [jax.ShapeDtypeStruct((N_CHIPS, g.shape[1] // 2, g.shape[2]), BF16) for g in gs]
        landed = exchange(f"reduce_swap_{si}", gs + list(after), half_shapes, {}, swap_halves)
        sums = [None] * len(gs)
        for run in _same_shape_runs(gs):
            for k, s in zip(run, _add_halves([gs[k] for k in run], [landed[k] for k in run], self.core)):
                sums[k] = s
        landing = [lax.empty((3,) + s.shape[1:], BF16) for s in sums]
        send, recv, bufs, token = split_start(f"reduce_start_{si}", sums + landing, (), 3 * len(sums), _reduce_copies)
        self.flying.append((send, recv, bufs, [t for v in units for t in _unit_tensors(v)]))
        return token[0:1, 0:1]

    def finish(self, after):
        for si, (send, recv, bufs, tensors) in enumerate(self.flying):
            bufs = split_wait(f"reduce_wait_{si}", bufs, send, recv, tuple(after), _reduce_copies)
            n = len(tensors)
            for run in _same_shape_runs(bufs[:n]):
                names = [tensors[k][0] for k in run]
                out = _add_chips([bufs[k] for k in run], [bufs[n + k] for k in run], self.place,
                                 [self.full[t] for t in names], [tensors[k][1] for k in run])
                self.full.update(zip(names, out))
        names = list(self.full)

        @_plan(0, len(names))
        def share_halves(ins, outs, x, y, c):
            sends = [(_half(o, 1, c), _half(o, 1, c), (x, y, 1 - c)) for o in outs]
            return [], sends, [_half(o, 1, 1 - c) for o in outs]

        shapes = [jax.ShapeDtypeStruct(self.full[k].shape, F32) for k in names]
        out = exchange("reduce_share_halves", [self.full[k] for k in names], shapes, {k: k for k in range(len(names))},
                       share_halves)
        return dict(zip(names, out))


def _same_shape_runs(arrays, longest=3):
    runs = []
    for k, a in enumerate(arrays):
        if runs and len(runs[-1]) < longest and arrays[runs[-1][0]].shape == a.shape:
            runs[-1].append(k)
        else:
            runs.append([k])
    return runs


def _add_halves(gs, landeds, core):
    n = len(gs)
    _, rh, cols = landeds[0].shape
    tr = _row_tile(rh, cols, 1024 * 1024)
    per_half = rh // tr

    def body(core_ref, *refs):
        del core_ref
        for k in range(n):
            refs[2 * n + k][...] = (refs[k][...].astype(F32) + refs[n + k][...].astype(F32)).astype(BF16)

    blk = (None, tr, cols)
    landed_spec = pl.BlockSpec(blk, lambda j, r, core_ref: (j, r, 0))
    return pl.pallas_call(
        body, out_shape=[jax.ShapeDtypeStruct(landeds[0].shape, BF16)] * n,
        grid_spec=pltpu.PrefetchScalarGridSpec(
            num_scalar_prefetch=1, grid=(N_CHIPS, per_half),
            in_specs=[pl.BlockSpec(blk, lambda j, r, core_ref: (j, core_ref[0] * per_half + r, 0))] * n
            + [landed_spec] * n,
            out_specs=[landed_spec] * n),
        name="reduce_add_halves", compiler_params=_params(("parallel", "parallel")),
    )(core, *gs, *landeds)


def _add_chips(sums, landeds, place, fulls, slots):
    n = len(sums)
    _, rh, cols = sums[0].shape
    tr = _row_tile(rh, cols, 1024 * 1024)
    per_half = rh // tr

    def body(place_ref, *refs):
        del place_ref
        for k in range(n):
            s_ref, la_ref, o_ref = refs[k], refs[n + k], refs[3 * n + k]
            o_ref[...] = ((s_ref[...].astype(F32) + la_ref[0].astype(F32)) + la_ref[1].astype(F32)) + la_ref[2].astype(F32)

    def out_spec(slot):
        return pl.BlockSpec((None, tr, cols), lambda r, place_ref: (slot, place_ref[1] * per_half + r, 0))

    return pl.pallas_call(
        body, out_shape=[jax.ShapeDtypeStruct(f.shape, F32) for f in fulls],
        grid_spec=pltpu.PrefetchScalarGridSpec(
            num_scalar_prefetch=1, grid=(per_half,),
            in_specs=[pl.BlockSpec((None, tr, cols), lambda r, place_ref: (place_ref[0], r, 0))] * n
            + [pl.BlockSpec((3, tr, cols), lambda r, place_ref: (0, r, 0))] * n + [ANY] * n,
            out_specs=[out_spec(slot) for slot in slots]),
        input_output_aliases={1 + 2 * n + k: k for k in range(n)}, name="reduce_add_chips",
        compiler_params=_params(("parallel",)),
    )(place, *sums, *landeds, *fulls)


def sum_devices(parts):
    def body(p_ref, o_ref):
        acc = p_ref[0]
        for d in range(1, N_DEV):
            acc = acc + p_ref[d]
        o_ref[...] = acc

    return pl.pallas_call(body, out_shape=jax.ShapeDtypeStruct(parts.shape[1:], F32), name="sum_devices")(parts)


ADA_SHARD = 9 * D_MODEL // N_CHIPS
ADA_TILE = 768
ADA_ROWS = 16


def ada_forward(c_rows, w_ada, b_shard):
    def body(c_ref, w_ref, b_ref, o_ref):
        cv = c_ref[...]
        o_ref[...] = _dot((cv * _sigmoid(cv)).astype(BF16), w_ref[...].astype(BF16), NN) + b_ref[...]

    return pl.pallas_call(
        body, grid=(DEPTH, ADA_SHARD // ADA_TILE),
        in_specs=[pl.BlockSpec((ADA_ROWS, D_MODEL), lambda l, n: (0, 0)),
                  pl.BlockSpec((None, D_MODEL, ADA_TILE), lambda l, n: (l, 0, n)),
                  pl.BlockSpec((None, 1, ADA_TILE), lambda l, n: (l, 0, n))],
        out_specs=pl.BlockSpec((None, ADA_ROWS, ADA_TILE), lambda l, n: (l, 0, n)),
        out_shape=jax.ShapeDtypeStruct((DEPTH, ADA_ROWS, ADA_SHARD), F32),
        name="ada_forward", compiler_params=_params(("parallel", "parallel")),
    )(c_rows, w_ada, b_shard)


def ada_backward(c_rows, dmod_rows):
    def body(c_ref, d_ref, o_ref):
        cv = c_ref[...]
        o_ref[...] = _dot((cv * _sigmoid(cv)).astype(BF16), d_ref[...].astype(BF16), TN)

    return pl.pallas_call(
        body, grid=(DEPTH, ADA_SHARD // ADA_TILE),
        in_specs=[pl.BlockSpec((ADA_ROWS, D_MODEL), lambda l, n: (0, 0)),
                  pl.BlockSpec((None, ADA_ROWS, ADA_TILE), lambda l, n: (l, 0, n))],
        out_specs=pl.BlockSpec((None, D_MODEL, ADA_TILE), lambda l, n: (l, 0, n)),
        out_shape=jax.ShapeDtypeStruct((DEPTH, D_MODEL, ADA_SHARD), F32),
        name="ada_backward", compiler_params=_params(("parallel", "parallel")),
    )(c_rows, dmod_rows)


def adamw(name, w, g, m, v):
    shape = w.shape
    cols = shape[-1]
    rows = w.size // cols
    tr = _row_tile(rows, cols, 512 * 1024) if rows % 16 == 0 else rows
    c1 = 1.0 / (1.0 - ADAM_B1 ** ADAM_STEP)
    c2 = 1.0 / (1.0 - ADAM_B2 ** ADAM_STEP)

    def body(w_ref, g_ref, m_ref, v_ref, go_ref, d_ref, mo_ref, vo_ref):
        gv = g_ref[...]
        mn = ADAM_B1 * m_ref[...] + (1.0 - ADAM_B1) * gv
        vn = ADAM_B2 * v_ref[...] + (1.0 - ADAM_B2) * (gv * gv)
        go_ref[...] = gv
        mo_ref[...] = mn
        vo_ref[...] = vn
        d_ref[...] = -ADAM_LR * ((mn * c1) / (jnp.sqrt(vn * c2) + ADAM_EPS) + ADAM_WD * w_ref[...])

    spec = pl.BlockSpec((tr, cols), lambda i: (i, 0))
    out = jax.ShapeDtypeStruct((rows, cols), F32)
    res = pl.pallas_call(
        body, grid=(rows // tr,), in_specs=[spec] * 4, out_specs=[spec] * 4, out_shape=[out] * 4,
        name=name, compiler_params=_params(("parallel",)),
    )(*[t.reshape(rows, cols) for t in (w, g, m, v)])
    return tuple(r.reshape(shape) for r in res)


def _pack(parts, rows):
    flat = jnp.concatenate([p.reshape(-1) for p in parts])
    return jnp.pad(flat, (0, rows * 128 - flat.size)).reshape(rows, 128)


def _unpack(flat, shapes):
    out, at = [], 0
    for s in shapes:
        n = math.prod(s)
        out.append(flat[at:at + n].reshape(s))
        at += n
    return out


def kernel(x, c, w_ada, b_ada, norm_gain, w_ffn_gate, w_ffn_up, w_ffn_down, w_in, w_br_sb, w_br_dil, w_br_swa, w_out, sinks, rel_bias, final_gain, loss_target, m_w_ada, m_b_ada, m_norm_gain, m_w_ffn_gate, m_w_ffn_up, m_w_ffn_down, m_w_in, m_w_br_sb, m_w_br_dil, m_w_br_swa, m_w_out, m_sinks, m_rel_bias, m_final_gain, v_w_ada, v_b_ada, v_norm_gain, v_w_ffn_gate, v_w_ffn_up, v_w_ffn_down, v_w_in, v_w_br_sb, v_w_br_dil, v_w_br_swa, v_w_out, v_sinks, v_rel_bias, v_final_gain):
    xi, yi, ci = _position()
    chip = 2 * xi + yi
    dev = 2 * chip + ci

    c_all = all_gather_small("gather_c", c.reshape(8, 128)).reshape(N_DEV, D_MODEL)
    c_rows = jnp.pad(c_all, ((0, ADA_ROWS - N_DEV), (0, 0)))
    b_shard = lax.dynamic_slice_in_dim(b_ada, chip * ADA_SHARD, ADA_SHARD, axis=1).reshape(DEPTH, 1, ADA_SHARD)
    mod_shard = ada_forward(c_rows, w_ada, b_shard)[:, :N_DEV]
    n_mod = DEPTH * N_DEV * ADA_SHARD
    gathered = all_gather_small("gather_mod", _pack([mod_shard, norm_gain], 304))[::2].reshape(N_CHIPS, -1)
    mod_all = gathered[:, :n_mod].reshape(N_CHIPS, DEPTH, N_DEV, ADA_SHARD)
    mod = lax.dynamic_index_in_dim(mod_all, dev, axis=2, keepdims=False)
    mod = mod.transpose(1, 0, 2).reshape(DEPTH, 3, 3, D_MODEL)
    gains = gathered[:, n_mod:n_mod + DEPTH * 3 * D_SHARD].reshape(N_CHIPS, DEPTH, 3, D_SHARD)
    gains = gains.transpose(1, 2, 0, 3).reshape(DEPTH, 3, D_MODEL)

    chip_i, core_i = chip.astype(jnp.int32).reshape(1), ci.astype(jnp.int32).reshape(1)
    w_br = jnp.concatenate([w_br_sb, w_br_dil, w_br_swa], axis=1)
    transposed = (3, 4)
    w_gate_t, w_up_t = jnp.swapaxes(w_ffn_gate, 2, 3), jnp.swapaxes(w_ffn_up, 2, 3)
    shards = []
    for l in range(DEPTH):
        ffn = [[(w_gate_t, (l, f)), (w_up_t, (l, f)), (w_ffn_down, (l, f))] for f in range(2)]
        shards += [ffn[0], [(w_in, (l,)), (w_br, (l,)), (w_out, (l,))], ffn[1]]
    weights_in = WeightStream(shards, chip_i, (gathered,))
    grads_out = GradStream(chip_i, core_i)

    loss, dx, dmod, dgains, dfinal, dsinks, drel, last_grads = device_step(
        x[0], loss_target[0], mod, gains, final_gain, sinks, rel_bias, weights_in.get, grads_out.put)

    small_shapes = [(DEPTH, 9 * D_MODEL), (DEPTH, 3, D_MODEL), (D_MODEL,), (DEPTH, H_SWA_Q), (N_BUCKETS, 12), (1,)]
    small_all = all_gather_small("gather_small_grads", _pack([dmod, dgains, dfinal, dsinks, drel, loss[0, 0:1]], 208))
    started = grads_out.put(0, last_grads, after=(small_all,))
    small_all = small_all + started
    g_b_ada, g_gain_full, g_final, g_sinks, g_rel, loss_sum = _unpack(sum_devices(small_all).reshape(-1), small_shapes)
    g_gain = lax.dynamic_slice_in_dim(g_gain_full, chip * D_SHARD, D_SHARD, axis=2)
    dmod_all = small_all.reshape(N_DEV, -1)[:, :DEPTH * 9 * D_MODEL].reshape(N_DEV, DEPTH, 9 * D_MODEL)
    dmod_rows = lax.dynamic_slice_in_dim(dmod_all, chip * ADA_SHARD, ADA_SHARD, axis=2).transpose(1, 0, 2)
    g_w_ada = ada_backward(c_rows, jnp.pad(dmod_rows, ((0, 0), (0, ADA_ROWS - N_DEV), (0, 0))))

    weights = [w_ada, b_ada, norm_gain, w_ffn_gate, w_ffn_up, w_ffn_down, w_in, w_br_sb, w_br_dil, w_br_swa, w_out,
               sinks, rel_bias, final_gain]
    ms = [m_w_ada, m_b_ada, m_norm_gain, m_w_ffn_gate, m_w_ffn_up, m_w_ffn_down, m_w_in, m_w_br_sb, m_w_br_dil,
          m_w_br_swa, m_w_out, m_sinks, m_rel_bias, m_final_gain]
    vs = [v_w_ada, v_b_ada, v_norm_gain, v_w_ffn_gate, v_w_ffn_up, v_w_ffn_down, v_w_in, v_w_br_sb, v_w_br_dil,
          v_w_br_swa, v_w_out, v_sinks, v_rel_bias, v_final_gain]
    grads = [g_w_ada, g_b_ada, g_gain] + [None] * 8 + [g_sinks, g_rel, g_final]

    deltas, new_ms, new_vs = [None] * 14, [None] * 14, [None] * 14
    for k in (0, 1, 2, 11, 12, 13):
        _, deltas[k], new_ms[k], new_vs[k] = adamw(f"adamw_{k}", weights[k], grads[k], ms[k], vs[k])

    g = grads_out.finish((dx, deltas[0], deltas[1]))
    g_br = g["br"]
    grads[3:11] = [g["gate"].reshape(w_gate_t.shape), g["up"].reshape(w_up_t.shape),
                   g["down"].reshape(w_ffn_down.shape), g["in"], g_br[:, 0:256], g_br[:, 256:384], g_br[:, 384:768],
                   g["out"]]
    for k in range(3, 11):
        state = [weights[k], ms[k], vs[k]]
        if k in transposed:
            state = [jnp.swapaxes(t, 2, 3) for t in state]
        out = adamw(f"adamw_{k}", state[0], grads[k], state[1], state[2])
        if k in transposed:
            out = [jnp.swapaxes(t, 2, 3) for t in out]
        grads[k], deltas[k], new_ms[k], new_vs[k] = out
    return (loss_sum[0], dx[None], *grads, *deltas, *new_ms, *new_vs)
```

```python
import functools
import math

import jax
import jax.numpy as jnp
from jax import lax
from jax.experimental import pallas as pl
from jax.experimental.pallas import tpu as pltpu

F32 = jnp.float32
BF16 = jnp.bfloat16

D_MODEL = 1024
SEQ = 2048
DEPTH = 2
HEAD_DIM = 64
BLK = 128
H_SB = 4
DIL_PATTERNS = ((128, 1), (512, 4), (2048, 16))
H_PER_DIL = 2
H_DIL = 6
H_SWA_Q = 6
H_SWA_KV = 2
SWA_WINDOW = 128
N_BUCKETS = 32
MAX_REL_DIST = 2048
D_FF = 2816
RMS_EPS = 1e-6
N_CHIPS = 4
N_DEV = 8
FF_SHARD = D_FF // N_CHIPS
D_QKV = 2560
D_IN = D_QKV + 3 * D_MODEL
IN_SHARD = D_IN // N_CHIPS
D_SHARD = D_MODEL // N_CHIPS
BR_ROWS = 768
NEG = -1e30
QK_SCALE = HEAD_DIM ** -0.5

ADAM_LR = 0.001
ADAM_B1 = 0.9
ADAM_B2 = 0.999
ADAM_EPS = 1e-08
ADAM_WD = 0.01
ADAM_STEP = 10

VMEM_LIMIT = 56 * 1024 * 1024
ROW_TILE = 256
MM_TILE = 1024

NN = (((1,), (0,)), ((), ()))
NT = (((1,), (1,)), ((), ()))
TN = (((0,), (0,)), ((), ()))


def _params(sem=None):
    return pltpu.CompilerParams(dimension_semantics=sem, vmem_limit_bytes=VMEM_LIMIT)


def _dot(a, b, dims):
    return lax.dot_general(a, b, dims, preferred_element_type=F32)


def _sigmoid(x):
    return 1.0 / (1.0 + jnp.exp(-x))


def _matmul(name, grid, nk, k_axis, dims, n_pairs, in_specs, out_specs, out_shape, acc_shape, epilogue,
            operands, sem, aliases=None, prologue=None):
    n_in = len(in_specs)
    n_out = len(out_specs)

    def partial(ins):
        tot = None
        for p in range(n_pairs):
            a = ins[2 * p][...]
            if prologue is not None:
                a = prologue(p, a, ins)
            d = _dot(a, ins[2 * p + 1][...], dims)
            tot = d if tot is None else tot + d
        return tot

    def body(*refs):
        ins, outs = refs[:n_in], refs[n_in:n_in + n_out]
        ids = tuple(pl.program_id(a) for a in range(len(grid)))
        if nk == 1:
            epilogue(partial(ins), ins, outs, ids)
            return
        acc = refs[n_in + n_out]
        k = ids[k_axis]

        @pl.when(k == 0)
        def _():
            acc[...] = partial(ins)

        @pl.when(k > 0)
        def _():
            acc[...] += partial(ins)

        @pl.when(k == nk - 1)
        def _():
            epilogue(acc[...], ins, outs, ids)

    return pl.pallas_call(
        body, grid=grid, in_specs=in_specs, out_specs=out_specs, out_shape=out_shape,
        scratch_shapes=[] if nk == 1 else [pltpu.VMEM(acc_shape, F32)],
        input_output_aliases=aliases or {}, name=name, compiler_params=_params(sem),
    )(*operands)


def _row_spec(width=D_MODEL):
    return pl.BlockSpec((ROW_TILE, width), lambda i: (i, 0))


def _vec_spec(rows=1, width=D_MODEL):
    return pl.BlockSpec((rows, width), lambda i: (0, 0))


class Row:
    def __init__(self, table, index):
        self.table, self.index = table, index

    def spec(self):
        index = self.index
        return pl.BlockSpec((None, 1, D_MODEL), lambda *ids: (index, 0, 0))


def _slot_spec(u):
    return pl.BlockSpec((8, D_MODEL), lambda *ids: (u, 0))


def prenorm(x, gain, scale, shift):
    def body(x_ref, g_ref, sc_ref, sh_ref, h_ref):
        xv = x_ref[...]
        r = lax.rsqrt(jnp.mean(xv * xv, axis=-1, keepdims=True) + RMS_EPS)
        h_ref[...] = (((xv * r) * g_ref[...]) * (1.0 + sc_ref[...]) + sh_ref[...]).astype(BF16)

    return pl.pallas_call(
        body, grid=(SEQ // ROW_TILE,), in_specs=[_row_spec(), gain.spec(), scale.spec(), shift.spec()],
        out_specs=_row_spec(), out_shape=jax.ShapeDtypeStruct((SEQ, D_MODEL), BF16),
        name="prenorm", compiler_params=_params(("parallel",)),
    )(x, gain.table, scale.table, shift.table)


def resid_bwd(dxo, f, coef, mult, sums, u):
    def body(dx_ref, f_ref, c_ref, sums_in, df_ref, dc_ref):
        del sums_in
        dx = dx_ref[...]
        df_ref[...] = (dx * (mult * c_ref[...])).astype(BF16)
        part = mult * jnp.sum(dx * f_ref[...], axis=0, keepdims=True)

        @pl.when(pl.program_id(0) == 0)
        def _():
            dc_ref[...] = jnp.zeros_like(dc_ref)

        dc_ref[0:1, :] += part

    return pl.pallas_call(
        body, grid=(SEQ // ROW_TILE,),
        in_specs=[_row_spec(), _row_spec(), coef.spec(), pl.BlockSpec(memory_space=pl.ANY)],
        out_specs=[_row_spec(), _slot_spec(u)],
        out_shape=[jax.ShapeDtypeStruct((SEQ, D_MODEL), BF16), jax.ShapeDtypeStruct(sums.shape, F32)],
        input_output_aliases={3: 1}, name="resid_bwd", compiler_params=_params(("arbitrary",)),
    )(dxo, f, coef.table, sums)


def final_loss(x, gain, target):
    def body(x_ref, g_ref, t_ref, loss_ref, dx_ref, dg_ref):
        xv = x_ref[...]
        g = g_ref[...]
        r = lax.rsqrt(jnp.mean(xv * xv, axis=-1, keepdims=True) + RMS_EPS)
        xh = xv * r
        e = xh * g - t_ref[...]
        part = 0.5 * jnp.sum(jnp.mean(e * e, axis=-1, keepdims=True), axis=0, keepdims=True)
        dy = e * (1.0 / D_MODEL)
        dyg = dy * g
        dx_ref[...] = r * (dyg - xh * jnp.mean(dyg * xh, axis=-1, keepdims=True))

        @pl.when(pl.program_id(0) == 0)
        def _():
            loss_ref[...] = jnp.zeros_like(loss_ref)
            dg_ref[...] = jnp.zeros_like(dg_ref)

        loss_ref[...] += jnp.broadcast_to(part, loss_ref.shape)
        dg_ref[0:1, :] += jnp.sum(dy * xh, axis=0, keepdims=True)

    return pl.pallas_call(
        body, grid=(SEQ // ROW_TILE,), in_specs=[_row_spec(), _vec_spec(), _row_spec()],
        out_specs=[_vec_spec(8, 128), _row_spec(), _vec_spec(8)],
        out_shape=[jax.ShapeDtypeStruct((8, 128), F32), jax.ShapeDtypeStruct((SEQ, D_MODEL), F32),
                   jax.ShapeDtypeStruct((8, D_MODEL), F32)],
        name="final_loss", compiler_params=_params(("arbitrary",)),
    )(x, gain, target)


def _prenorm_bwd_epilogue(dh, x_ref, dxo_ref, g_ref, sc_ref, dx_ref, stats_ref, first):
    xv = x_ref[...]
    g = g_ref[...]
    r = lax.rsqrt(jnp.mean(xv * xv, axis=-1, keepdims=True) + RMS_EPS)
    xh = xv * r
    dn = dh * (1.0 + sc_ref[...])
    dxh = dn * g
    dx = dxo_ref[...] + r * (dxh - xh * jnp.mean(dxh * xh, axis=-1, keepdims=True))
    dx_ref[...] = dx

    @pl.when(first)
    def _():
        stats_ref[...] = jnp.zeros_like(stats_ref)

    stats_ref[0:1, :] += jnp.sum(dh, axis=0, keepdims=True)
    stats_ref[1:2, :] += jnp.sum(dh * (xh * g), axis=0, keepdims=True)
    stats_ref[2:3, :] += jnp.sum(dn * xh, axis=0, keepdims=True)
    return dx


def _resid_bwd_epilogue(dx, f_ref, c_ref, mult, df_ref, dc_ref, first):
    df_ref[...] = (dx * (mult * c_ref[...])).astype(BF16)

    @pl.when(first)
    def _():
        dc_ref[...] = jnp.zeros_like(dc_ref)

    dc_ref[0:1, :] += mult * jnp.sum(dx * f_ref[...], axis=0, keepdims=True)


def ffn_up(h, wg_all, wu_all):
    def body(h_ref, wg_ref, wu_ref, a_ref, b_ref, s_ref):
        hv = h_ref[...]
        a = _dot(hv, wg_ref[...], NT)
        b = _dot(hv, wu_ref[...], NT)
        a_ref[...] = a.astype(BF16)
        b_ref[...] = b.astype(BF16)
        s_ref[...] = (a * _sigmoid(a) * b).astype(BF16)

    w_spec = pl.BlockSpec((None, FF_SHARD, D_MODEL), lambda j, i: (j, 0, 0))
    o_spec = pl.BlockSpec((None, MM_TILE, FF_SHARD), lambda j, i: (j, i, 0))
    hid = (N_CHIPS, SEQ, FF_SHARD)
    return pl.pallas_call(
        body, grid=(N_CHIPS, SEQ // MM_TILE),
        in_specs=[pl.BlockSpec((MM_TILE, D_MODEL), lambda j, i: (i, 0)), w_spec, w_spec],
        out_specs=[o_spec, o_spec, o_spec],
        out_shape=[jax.ShapeDtypeStruct(hid, BF16)] * 3,
        name="ffn_up", compiler_params=_params(("parallel", "parallel")),
    )(h, wg_all, wu_all)


def matmul_residual(name, a, a_spec, take, w_all, x, coef, mult, then=None):
    n_rows = 0 if then is None else 3

    def body(a_ref, w_hbm, x_ref, c_ref, *rest):
        rows, (f_ref, xo_ref), w_vmem, sem = rest[:n_rows], rest[n_rows:n_rows + 2], rest[-2], rest[-1]
        h_ref = rest[n_rows + 2] if then is not None else None

        @pl.when(pl.program_id(0) == 0)
        def _():
            copy = pltpu.make_async_copy(w_hbm, w_vmem, sem)
            copy.start()
            copy.wait()

        acc = None
        for j in range(N_CHIPS):
            part = _dot(take(a_ref, j), w_vmem[j], NN)
            acc = part if acc is None else acc + part
        f_ref[...] = acc
        xo = x_ref[...] + (mult * c_ref[...]) * acc
        xo_ref[...] = xo
        if then is not None:
            r = lax.rsqrt(jnp.mean(xo * xo, axis=-1, keepdims=True) + RMS_EPS)
            h_ref[...] = (((xo * r) * rows[0][...]) * (1.0 + rows[1][...]) + rows[2][...]).astype(BF16)

    row = pl.BlockSpec((MM_TILE, D_MODEL), lambda i: (i, 0))
    f32 = jax.ShapeDtypeStruct((SEQ, D_MODEL), F32)
    extra = [] if then is None else [then.gain, then.scale, then.shift]
    return pl.pallas_call(
        body, grid=(SEQ // MM_TILE,),
        in_specs=[a_spec, pl.BlockSpec(memory_space=pl.ANY), row, coef.spec()] + [t.spec() for t in extra],
        out_specs=[row] * (2 + bool(extra)),
        out_shape=[f32, f32] + [jax.ShapeDtypeStruct((SEQ, D_MODEL), BF16)] * bool(extra),
        scratch_shapes=[pltpu.VMEM(w_all.shape, w_all.dtype), pltpu.SemaphoreType.DMA],
        name=name, compiler_params=_params(("arbitrary",)),
    )(a, w_all, x, coef.table, *[t.table for t in extra])


def ffn_down(s, wd_all, x, gate, then):
    return matmul_residual(
        "ffn_down", s, pl.BlockSpec((N_CHIPS, MM_TILE, FF_SHARD), lambda i: (0, i, 0)), lambda ref, j: ref[j],
        wd_all, x, gate, 0.5, then)


def ffn_bwd_hidden(df, wd_all, a, b):
    def epilogue(ds, ins, outs, ids):
        av, bv = ins[2][...].astype(F32), ins[3][...].astype(F32)
        sig = _sigmoid(av)
        outs[0][...] = (ds * bv * (sig * (1.0 + av * (1.0 - sig)))).astype(BF16)
        outs[1][...] = (ds * (av * sig)).astype(BF16)

    hid_spec = pl.BlockSpec((None, MM_TILE, FF_SHARD), lambda j, i: (j, i, 0))
    hid = jax.ShapeDtypeStruct((N_CHIPS, SEQ, FF_SHARD), BF16)
    return _matmul(
        "ffn_bwd_hidden", (N_CHIPS, SEQ // MM_TILE), 1, None, NT, 1,
        [pl.BlockSpec((MM_TILE, D_MODEL), lambda j, i: (i, 0)),
         pl.BlockSpec((None, FF_SHARD, D_MODEL), lambda j, i: (j, 0, 0)), hid_spec, hid_spec],
        [hid_spec, hid_spec], [hid, hid], None, epilogue, (df, wd_all, a, b), ("parallel", "parallel"))


def grad_weight(name, lhs, lhs_spec, rhs, rhs_spec, shape):
    def epilogue(acc, ins, outs, ids):
        outs[0][...] = acc.astype(BF16)

    return _matmul(
        name, (N_CHIPS, SEQ // MM_TILE), SEQ // MM_TILE, 1, TN, 1,
        [lhs_spec, rhs_spec], [pl.BlockSpec((None,) + shape, lambda j, k: (j, 0, 0))],
        [jax.ShapeDtypeStruct((N_CHIPS,) + shape, BF16)], shape, epilogue, (lhs, rhs), ("parallel", "arbitrary"))[0]


def ffn_grad_weights(h, s, df, da, db):
    tok = pl.BlockSpec((MM_TILE, D_MODEL), lambda j, k: (k, 0))
    hid = pl.BlockSpec((None, MM_TILE, FF_SHARD), lambda j, k: (j, k, 0))
    n_k = SEQ // MM_TILE

    def body(da_ref, db_ref, s_ref, h_ref, df_ref, gg_ref, gu_ref, gd_ref, acc_g, acc_u, acc_d):
        k = pl.program_id(1)
        hv = h_ref[...]
        parts = (_dot(da_ref[...], hv, TN), _dot(db_ref[...], hv, TN), _dot(s_ref[...], df_ref[...], TN))

        @pl.when(k == 0)
        def _():
            acc_g[...], acc_u[...], acc_d[...] = parts

        @pl.when(k > 0)
        def _():
            acc_g[...] += parts[0]
            acc_u[...] += parts[1]
            acc_d[...] += parts[2]

        @pl.when(k == n_k - 1)
        def _():
            gg_ref[...] = acc_g[...].astype(BF16)
            gu_ref[...] = acc_u[...].astype(BF16)
            gd_ref[...] = acc_d[...].astype(BF16)

    out = pl.BlockSpec((None, FF_SHARD, D_MODEL), lambda j, k: (j, 0, 0))
    shape = jax.ShapeDtypeStruct((N_CHIPS, FF_SHARD, D_MODEL), BF16)
    return pl.pallas_call(
        body, grid=(N_CHIPS, n_k), in_specs=[hid, hid, hid, tok, tok], out_specs=[out] * 3, out_shape=[shape] * 3,
        scratch_shapes=[pltpu.VMEM((FF_SHARD, D_MODEL), F32)] * 3,
        name="ffn_grad_weights", compiler_params=_params(("parallel", "arbitrary")),
    )(da, db, s, h, df)


BWD_TILE = 512


def matmul_prenorm_bwd(name, dims, pairs, pair_specs, x, dxo, gain, scale, sums, u, below):
    n_pairs = len(pairs) // 2
    lhs, weights = pairs[0::2], pairs[1::2]
    n_in = n_pairs * 2 + 5 + (3 if below is not None else 0)
    n_out = 2 + (2 if below is not None else 0)

    def body(*refs):
        ins, outs = refs[:n_in], refs[n_in:n_in + n_out]
        w_vmem, w_sems = refs[n_in + n_out:n_in + n_out + n_pairs], refs[-1]
        first = pl.program_id(0) == 0

        @pl.when(first)
        def _():
            copies = [pltpu.make_async_copy(ins[n_pairs + p], w_vmem[p], w_sems.at[p]) for p in range(n_pairs)]
            for cp in copies:
                cp.start()
            for cp in copies:
                cp.wait()

        dh = None
        for j in range(N_CHIPS):
            for p in range(n_pairs):
                part = _dot(pair_specs[p][1](ins[p], j), w_vmem[p][j], dims)
                dh = part if dh is None else dh + part
        k = 2 * n_pairs
        dx = _prenorm_bwd_epilogue(dh, ins[k], ins[k + 1], ins[k + 2], ins[k + 3], outs[0], outs[1], first)
        if below is not None:
            _resid_bwd_epilogue(dx, ins[k + 5], ins[k + 6], below[2], outs[2], outs[3], first)

    row = pl.BlockSpec((BWD_TILE, D_MODEL), lambda i: (i, 0))
    any_spec = pl.BlockSpec(memory_space=pl.ANY)
    f32 = jax.ShapeDtypeStruct((SEQ, D_MODEL), F32)
    in_specs = [s for s, _ in pair_specs] + [any_spec] * n_pairs + [row, row, gain.spec(), scale.spec(), any_spec]
    operands = tuple(lhs) + tuple(weights) + (x, dxo, gain.table, scale.table, sums[0])
    out_specs, out_shape = [row, _slot_spec(u)], [f32, jax.ShapeDtypeStruct(sums[0].shape, F32)]
    aliases = {2 * n_pairs + 4: 1}
    if below is not None:
        in_specs += [row, below[1].spec(), any_spec]
        operands += (below[0], below[1].table, sums[1])
        out_specs += [row, _slot_spec(u - 1)]
        out_shape += [jax.ShapeDtypeStruct((SEQ, D_MODEL), BF16), jax.ShapeDtypeStruct(sums[1].shape, F32)]
        aliases[2 * n_pairs + 7] = 3
    out = pl.pallas_call(
        body, grid=(SEQ // BWD_TILE,), in_specs=in_specs, out_specs=out_specs, out_shape=out_shape,
        scratch_shapes=[pltpu.VMEM(w.shape, w.dtype) for w in weights] + [pltpu.SemaphoreType.DMA((n_pairs,))],
        input_output_aliases=aliases, name=name, compiler_params=_params(("arbitrary",)),
    )(*operands)
    if below is None:
        return out[0], (out[1], sums[1])
    return out[0], (out[1], out[3]), out[2]


def ffn_bwd_input(da, db, wg_all, wu_all, x, dxo, gain, scale, sums, u, below):
    hid = (pl.BlockSpec((N_CHIPS, BWD_TILE, FF_SHARD), lambda i: (0, i, 0)), lambda ref, j: ref[j])
    return matmul_prenorm_bwd("ffn_bwd_input", NN, (da, wg_all, db, wu_all), (hid, hid), x, dxo, gain, scale,
                              sums, u, below)


def in_proj(h, w_all):
    def epilogue(acc, ins, outs, ids):
        outs[0][...] = acc
        outs[1][...] = acc.astype(BF16)

    out = pl.BlockSpec((MM_TILE, IN_SHARD), lambda j, i: (i, j))
    return _matmul(
        "in_proj", (N_CHIPS, SEQ // MM_TILE), 1, None, NN, 1,
        [pl.BlockSpec((MM_TILE, D_MODEL), lambda j, i: (i, 0)),
         pl.BlockSpec((None, D_MODEL, IN_SHARD), lambda j, i: (j, 0, 0))],
        [out, out], [jax.ShapeDtypeStruct((SEQ, D_IN), F32), jax.ShapeDtypeStruct((SEQ, D_IN), BF16)],
        None, epilogue, (h, w_all), ("parallel", "parallel"))


_GATE_BLOCK0 = D_QKV // D_SHARD


BRANCH_ROWS = ((0, 256), (256, 384), (384, 768))


def _branch_products(o_refs, w_ref):
    return tuple(_dot(o[...].astype(BF16), w_ref[lo:hi, :], NN) for o, (lo, hi) in zip(o_refs, BRANCH_ROWS))


def _branch_specs(index):
    return [pl.BlockSpec((MM_TILE, hi - lo), index) for lo, hi in BRANCH_ROWS]


def merge_branches(o_parts, wbr_all, proj):
    def body(o0_ref, o1_ref, o2_ref, w_ref, g0_ref, g1_ref, g2_ref, m_ref):
        u = _branch_products((o0_ref, o1_ref, o2_ref), w_ref)
        m_ref[...] = (_sigmoid(g0_ref[...]) * u[0] + _sigmoid(g1_ref[...]) * u[1]
                      + _sigmoid(g2_ref[...]) * u[2]).astype(BF16)

    def gate_spec(b):
        return pl.BlockSpec((MM_TILE, D_SHARD), lambda i, j: (i, _GATE_BLOCK0 + 4 * b + j))

    return pl.pallas_call(
        body, grid=(SEQ // MM_TILE, N_CHIPS),
        in_specs=_branch_specs(lambda i, j: (i, 0))
        + [pl.BlockSpec((None, BR_ROWS, D_SHARD), lambda i, j: (j, 0, 0)), gate_spec(0), gate_spec(1), gate_spec(2)],
        out_specs=pl.BlockSpec((MM_TILE, D_SHARD), lambda i, j: (i, j)),
        out_shape=jax.ShapeDtypeStruct((SEQ, D_MODEL), BF16),
        name="merge_branches", compiler_params=_params(("parallel", "parallel")),
    )(*o_parts, wbr_all, proj, proj, proj)


def out_proj(merged, wout_all, x, gate, then):
    return matmul_residual(
        "out_proj", merged, pl.BlockSpec((MM_TILE, D_MODEL), lambda i: (i, 0)),
        lambda ref, j: ref[:, D_SHARD * j:D_SHARD * (j + 1)], wout_all, x, gate, 1.0, then)


def merge_bwd(dmo, wout_all, o_parts, wbr_all, proj):
    def epilogue(dm, ins, outs, ids):
        u = _branch_products(ins[2:5], ins[5])
        for b in range(3):
            sig = _sigmoid(ins[6 + b][...])
            outs[b][...] = (dm * sig).astype(BF16)
            outs[3 + b][...] = (dm * u[b] * (sig * (1.0 - sig))).astype(BF16)

    def gate_spec(b):
        return pl.BlockSpec((MM_TILE, D_SHARD), lambda j, i: (i, _GATE_BLOCK0 + 4 * b + j))

    col = pl.BlockSpec((MM_TILE, D_SHARD), lambda j, i: (i, j))
    du = jax.ShapeDtypeStruct((SEQ, D_MODEL), BF16)
    return _matmul(
        "merge_bwd", (N_CHIPS, SEQ // MM_TILE), 1, None, NT, 1,
        [pl.BlockSpec((MM_TILE, D_MODEL), lambda j, i: (i, 0)),
         pl.BlockSpec((None, D_SHARD, D_MODEL), lambda j, i: (j, 0, 0))] + _branch_specs(lambda j, i: (i, 0))
        + [pl.BlockSpec((None, BR_ROWS, D_SHARD), lambda j, i: (j, 0, 0)), gate_spec(0), gate_spec(1), gate_spec(2)],
        [col] * 6, [du] * 6,
        None, epilogue, (dmo, wout_all, *o_parts, wbr_all, proj, proj, proj), ("parallel", "parallel"))


def branch_bwd_input(du, wbr_all):
    def body(d0_ref, d1_ref, d2_ref, w_ref, o_ref, acc):
        j = pl.program_id(1)
        parts = (_dot(d0_ref[...], w_ref[0:256, :], NT), _dot(d1_ref[...], w_ref[256:384, :], NT),
                 _dot(d2_ref[...], w_ref[384:768, :], NT))

        @pl.when(j == 0)
        def _():
            acc[:, 0:256], acc[:, 256:384], acc[:, 384:768] = parts

        @pl.when(j > 0)
        def _():
            acc[:, 0:256] += parts[0]
            acc[:, 256:384] += parts[1]
            acc[:, 384:768] += parts[2]

        @pl.when(j == N_CHIPS - 1)
        def _():
            o_ref[...] = acc[...]

    col = pl.BlockSpec((MM_TILE, D_SHARD), lambda i, j: (i, j))
    return pl.pallas_call(
        body, grid=(SEQ // MM_TILE, N_CHIPS),
        in_specs=[col, col, col, pl.BlockSpec((None, BR_ROWS, D_SHARD), lambda i, j: (j, 0, 0))],
        out_specs=pl.BlockSpec((MM_TILE, BR_ROWS), lambda i, j: (i, 0)),
        out_shape=jax.ShapeDtypeStruct((SEQ, BR_ROWS), F32),
        scratch_shapes=[pltpu.VMEM((MM_TILE, BR_ROWS), F32)],
        name="branch_bwd_input", compiler_params=_params(("parallel", "arbitrary")),
    )(du[0], du[1], du[2], wbr_all)


def branch_grad_weights(o_parts, du):
    def body(o0_ref, o1_ref, o2_ref, d0_ref, d1_ref, d2_ref, g_ref, acc):
        k = pl.program_id(1)
        parts = tuple(_dot(o[...].astype(BF16), d[...], TN)
                      for o, d in zip((o0_ref, o1_ref, o2_ref), (d0_ref, d1_ref, d2_ref)))

        @pl.when(k == 0)
        def _():
            acc[0:256, :], acc[256:384, :], acc[384:768, :] = parts

        @pl.when(k > 0)
        def _():
            acc[0:256, :] += parts[0]
            acc[256:384, :] += parts[1]
            acc[384:768, :] += parts[2]

        @pl.when(k == SEQ // MM_TILE - 1)
        def _():
            g_ref[...] = acc[...].astype(BF16)

    col = pl.BlockSpec((MM_TILE, D_SHARD), lambda j, k: (k, j))
    return pl.pallas_call(
        body, grid=(N_CHIPS, SEQ // MM_TILE),
        in_specs=_branch_specs(lambda j, k: (k, 0)) + [col, col, col],
        out_specs=pl.BlockSpec((None, BR_ROWS, D_SHARD), lambda j, k: (j, 0, 0)),
        out_shape=jax.ShapeDtypeStruct((N_CHIPS, BR_ROWS, D_SHARD), BF16),
        scratch_shapes=[pltpu.VMEM((BR_ROWS, D_SHARD), F32)],
        name="branch_grad_weights", compiler_params=_params(("parallel", "arbitrary")),
    )(*o_parts, du[0], du[1], du[2])


def mixer_bwd_input(dproj, win_all, x, dxo, gain, scale, sums, u, below):
    columns = (pl.BlockSpec((BWD_TILE, D_IN), lambda i: (i, 0)),
               lambda ref, j: ref[:, IN_SHARD * j:IN_SHARD * (j + 1)])
    return matmul_prenorm_bwd("mixer_bwd_input", NT, (dproj, win_all), (columns,), x, dxo, gain, scale, sums, u, below)


BATCH_QK = (((2,), (2,)), ((0,), (0,)))
BATCH_PV = (((2,), (1,)), ((0,), (0,)))
BATCH_TN = (((1,), (1,)), ((0,), (0,)))


SB_WIDTH = H_SB * HEAD_DIM
SB_ROWS = H_SB * BLK


def _split_dot(v, tri):
    hi = v.astype(BF16)
    lo = (v - hi.astype(F32)).astype(BF16)
    return _dot(hi, tri, NN) + _dot(lo, tri, NN)


def _tri(cmp):
    return cmp(lax.broadcasted_iota(jnp.int32, (BLK, BLK), 0), lax.broadcasted_iota(jnp.int32, (BLK, BLK), 1)).astype(BF16)


def _head_masks():
    lane = lax.broadcasted_iota(jnp.int32, (1, SB_WIDTH), 1) // HEAD_DIM
    return [lane == h for h in range(H_SB)]


def _stack_heads(x, masks):
    return jnp.concatenate([jnp.where(m, x, jnp.zeros_like(x)) for m in masks], axis=0)


def _merge_heads(y, masks):
    out = jnp.where(masks[0], y[0:BLK], 0.0)
    for h in range(1, H_SB):
        out = jnp.where(masks[h], y[h * BLK:(h + 1) * BLK], out)
    return out


def _sb_scores(q4, k_ref, j, diagonal):
    rows = pl.ds(pl.multiple_of(j * BLK, BLK), BLK)
    z = _dot(q4, k_ref[rows, :], NT)
    log_fail = -(jnp.maximum(z, 0.0) + jnp.log(1.0 + jnp.exp(-jnp.abs(z))))
    log_hit = z + log_fail
    before = None
    if diagonal:
        tile = (SB_ROWS, BLK)
        before = lax.broadcasted_iota(jnp.int32, tile, 1) < (lax.broadcasted_iota(jnp.int32, tile, 0) & (BLK - 1))
        log_fail = jnp.where(before, log_fail, 0.0)
    return rows, before, log_fail, log_hit


def _keep(before, x):
    return x if before is None else jnp.where(before, x, 0.0)


def sb_forward(qkv):
    def body(q_ref, k_ref, v_ref, o_ref, tot_ref):
        i = pl.program_id(0)
        masks = _head_masks()
        q4 = _stack_heads(q_ref[...] * QK_SCALE, masks)
        later = _tri(lambda r, c: r > c)

        def tiles(js, carry, diagonal):
            o, run = carry
            scores = [_sb_scores(q4, k_ref, j, diagonal) for j in js]
            acc = None
            for rows, before, log_fail, log_hit in scores:
                between = _split_dot(log_fail, later) + run
                w = _keep(before, jnp.exp(log_hit + between))
                part = _dot(w.astype(BF16), v_ref[rows, :], NN)
                acc = part if acc is None else acc + part
                run = run + jnp.sum(log_fail, axis=1, keepdims=True)
            return o + _merge_heads(acc, masks), run

        carry = tiles([i], (jnp.zeros((BLK, SB_WIDTH), F32), jnp.zeros((SB_ROWS, 1), F32)), True)
        carry = lax.cond((i & 1) != 0, lambda c: tiles([i - 1], c, False), lambda c: c, carry)
        at = i - 1 - (i & 1)
        carry = lax.cond((i & 2) != 0, lambda c: tiles([at, at - 1], c, False), lambda c: c, carry)
        at = at - (i & 2)
        o, run = lax.fori_loop(0, i // 4, lambda t, c: tiles([at - 4 * t - n for n in range(4)], c, False), carry)
        o_ref[...] = o
        tot_ref[...] = run

    return pl.pallas_call(
        body, grid=(N_BLK,),
        in_specs=[pl.BlockSpec((BLK, SB_WIDTH), lambda i: (i, 0)), pl.BlockSpec((SEQ, SB_WIDTH), lambda i: (0, 1)),
                  pl.BlockSpec((SEQ, SB_WIDTH), lambda i: (0, 2))],
        out_specs=[pl.BlockSpec((BLK, SB_WIDTH), lambda i: (i, 0)), pl.BlockSpec((None, SB_ROWS, 1), lambda i: (i, 0, 0))],
        out_shape=[jax.ShapeDtypeStruct((SEQ, SB_WIDTH), F32), jax.ShapeDtypeStruct((N_BLK, SB_ROWS, 1), F32)],
        name="sb_forward", compiler_params=_params(("parallel",)),
    )(qkv, qkv, qkv)


def sb_backward(qkv, total, do_cat):
    def body(q_ref, k_ref, v_ref, tot_ref, do_ref, dq_ref, dk_ref, dv_ref):
        i = pl.program_id(0)

        @pl.when(i == 0)
        def _():
            dk_ref[...] = jnp.zeros_like(dk_ref)
            dv_ref[...] = jnp.zeros_like(dv_ref)

        masks = _head_masks()
        q4 = _stack_heads(q_ref[...] * QK_SCALE, masks)
        do4 = _stack_heads(do_ref[...].astype(BF16), masks)
        total_v = tot_ref[...]
        upto = _tri(lambda r, c: r <= c)
        earlier = _tri(lambda r, c: r < c)

        def tiles(js, carry, diagonal):
            dq, seen, g_seen = carry
            scores = [_sb_scores(q4, k_ref, j, diagonal) for j in js]
            acc = None
            for rows, before, log_fail, log_hit in scores:
                between = total_v - (seen + _split_dot(log_fail, upto))
                w = _keep(before, jnp.exp(log_hit + between))
                g = _dot(do4, v_ref[rows, :], NT) * w
                g_earlier = g_seen + _split_dot(g, earlier)
                sig = jnp.exp(log_hit)
                dz = _keep(before, g * (1.0 - sig) - g_earlier * sig).astype(BF16)
                part = _dot(dz, k_ref[rows, :], NN)
                acc = part if acc is None else acc + part
                dk_ref[rows, :] += _dot(dz, q4, TN)
                dv_ref[rows, :] += _dot(w.astype(BF16), do4, TN)
                seen = seen + jnp.sum(log_fail, axis=1, keepdims=True)
                g_seen = g_seen + jnp.sum(g, axis=1, keepdims=True)
            return dq + _merge_heads(acc, masks), seen, g_seen

        zero = jnp.zeros((SB_ROWS, 1), F32)
        carry = lax.fori_loop(0, i // 4, lambda t, c: tiles([4 * t + n for n in range(4)], c, False),
                              (jnp.zeros((BLK, SB_WIDTH), F32), zero, zero))
        at = i - (i & 3)
        carry = lax.cond((i & 2) != 0, lambda c: tiles([at, at + 1], c, False), lambda c: c, carry)
        carry = lax.cond((i & 1) != 0, lambda c: tiles([i - 1], c, False), lambda c: c, carry)
        dq, _, _ = tiles([i], carry, True)
        dq_ref[...] = dq * QK_SCALE

    blk = pl.BlockSpec((BLK, SB_WIDTH), lambda i: (i, 0))
    full = pl.BlockSpec((SEQ, SB_WIDTH), lambda i: (0, 0))
    shape = jax.ShapeDtypeStruct((SEQ, SB_WIDTH), F32)
    return pl.pallas_call(
        body, grid=(N_BLK,),
        in_specs=[blk, pl.BlockSpec((SEQ, SB_WIDTH), lambda i: (0, 1)), pl.BlockSpec((SEQ, SB_WIDTH), lambda i: (0, 2)),
                  pl.BlockSpec((None, SB_ROWS, 1), lambda i: (i, 0, 0)), blk],
        out_specs=[blk, full, full], out_shape=[shape, shape, shape],
        name="sb_backward", compiler_params=_params(("arbitrary",)),
    )(qkv, qkv, qkv, total, do_cat)


def _band_scores(q_ref, kp_ref, ko_ref, bias_ref, hb, prev_mask):
    b = pl.program_id(1)
    qs = q_ref[...]
    s_prev = _dot(qs, kp_ref[...], BATCH_QK) + bias_ref[:, :, 0:BLK]
    s_prev = jnp.concatenate(
        [jnp.where((b & prev_mask(pl.program_id(0) * hb + t)) != 0, s_prev[t:t + 1], NEG) for t in range(hb)], axis=0)
    s_own = _dot(qs, ko_ref[...], BATCH_QK) + bias_ref[:, :, BLK:2 * BLK]
    return qs, s_prev, s_own


def _band_specs(hb, rows, t_n):
    def q_spec(width):
        return pl.BlockSpec((hb, None, rows, width), lambda h, b: (h, b, 0, 0))

    own = pl.BlockSpec((hb, BLK, HEAD_DIM), lambda h, b: (h, b, 0))
    prev = pl.BlockSpec((hb, BLK, HEAD_DIM), lambda h, b: (h, jnp.maximum(b - 1, 0), 0))
    per_head = lambda r, width: pl.BlockSpec((hb, r, width), lambda h, b: (h, 0, 0))
    return q_spec, own, prev, per_head


def banded_forward(name, q, k, v, bias, sinks, hb, prev_mask):
    h_n, nb, rows, _ = q.shape

    def body(q_ref, kp_ref, ko_ref, vp_ref, vo_ref, bias_ref, sink_ref, o_ref, lse_ref):
        _, s_prev, s_own = _band_scores(q_ref, kp_ref, ko_ref, bias_ref, hb, prev_mask)
        sink = sink_ref[...]
        m = jnp.maximum(jnp.maximum(jnp.max(s_prev, axis=2, keepdims=True), jnp.max(s_own, axis=2, keepdims=True)), sink)
        p_prev = jnp.exp(s_prev - m)
        p_own = jnp.exp(s_own - m)
        denom = jnp.sum(p_prev, axis=2, keepdims=True) + jnp.sum(p_own, axis=2, keepdims=True) + jnp.exp(sink - m)
        o = _dot(p_prev.astype(BF16), vp_ref[...], BATCH_PV) + _dot(p_own.astype(BF16), vo_ref[...], BATCH_PV)
        o_ref[...] = o / denom
        lse_ref[...] = m + jnp.log(denom)

    q_spec, own, prev, per_head = _band_specs(hb, rows, k.shape[1])
    return pl.pallas_call(
        body, grid=(h_n // hb, nb),
        in_specs=[q_spec(HEAD_DIM), prev, own, prev, own, per_head(rows, 2 * BLK), per_head(rows, 1)],
        out_specs=[q_spec(HEAD_DIM), q_spec(1)],
        out_shape=[jax.ShapeDtypeStruct(q.shape, F32), jax.ShapeDtypeStruct((h_n, nb, rows, 1), F32)],
        name=name, compiler_params=_params(("parallel", "parallel")),
    )(q, k, k, v, v, bias, sinks)


def banded_backward(name, q, k, v, bias, sinks, o, lse, do, dlse, hb, prev_mask):
    h_n, nb, rows, _ = q.shape
    t_n = k.shape[1]

    def body(q_ref, kp_ref, ko_ref, vp_ref, vo_ref, bias_ref, sink_ref, o_ref, lse_ref, do_ref, dlse_ref,
             dq_ref, dk_ref, dv_ref, dbias_ref, dsink_ref):
        b = pl.program_id(1)

        @pl.when(b == 0)
        def _():
            dk_ref[...] = jnp.zeros_like(dk_ref)
            dv_ref[...] = jnp.zeros_like(dv_ref)
            dbias_ref[...] = jnp.zeros_like(dbias_ref)
            dsink_ref[...] = jnp.zeros_like(dsink_ref)

        qs, s_prev, s_own = _band_scores(q_ref, kp_ref, ko_ref, bias_ref, hb, prev_mask)
        lse_v = lse_ref[...]
        dov = do_ref[...]
        dob = dov.astype(BF16)
        shift = dlse_ref[...] - jnp.sum(dov * o_ref[...], axis=2, keepdims=True)
        p_prev = jnp.exp(s_prev - lse_v)
        p_own = jnp.exp(s_own - lse_v)
        ds_prev = p_prev * (_dot(dob, vp_ref[...], BATCH_QK) + shift)
        ds_own = p_own * (_dot(dob, vo_ref[...], BATCH_QK) + shift)
        dbias_ref[:, :, 0:BLK] += ds_prev
        dbias_ref[:, :, BLK:2 * BLK] += ds_own
        d_sink = jnp.exp(sink_ref[...] - lse_v) * shift
        for g in range(rows // BLK):
            dsink_ref[:, g:g + 1, :] += jnp.sum(d_sink[:, g * BLK:(g + 1) * BLK, :], axis=1, keepdims=True)
        ds_prev = ds_prev.astype(BF16)
        ds_own = ds_own.astype(BF16)
        dq_ref[...] = (_dot(ds_prev, kp_ref[...], BATCH_PV) + _dot(ds_own, ko_ref[...], BATCH_PV)) * QK_SCALE
        rows_prev = pl.ds(pl.multiple_of(jnp.maximum(b - 1, 0) * BLK, BLK), BLK)
        rows_own = pl.ds(pl.multiple_of(b * BLK, BLK), BLK)
        dk_ref[:, rows_prev, :] += _dot(ds_prev, qs, BATCH_TN)
        dk_ref[:, rows_own, :] += _dot(ds_own, qs, BATCH_TN)
        dv_ref[:, rows_prev, :] += _dot(p_prev.astype(BF16), dob, BATCH_TN)
        dv_ref[:, rows_own, :] += _dot(p_own.astype(BF16), dob, BATCH_TN)

    q_spec, own, prev, per_head = _band_specs(hb, rows, t_n)
    kv_full = per_head(t_n, HEAD_DIM)
    kv_shape = jax.ShapeDtypeStruct((h_n, t_n, HEAD_DIM), F32)
    return pl.pallas_call(
        body, grid=(h_n // hb, nb),
        in_specs=[q_spec(HEAD_DIM), prev, own, prev, own, per_head(rows, 2 * BLK), per_head(rows, 1),
                  q_spec(HEAD_DIM), q_spec(1), q_spec(HEAD_DIM), q_spec(1)],
        out_specs=[q_spec(HEAD_DIM), kv_full, kv_full, per_head(rows, 2 * BLK), per_head(rows // BLK, BLK)],
        out_shape=[jax.ShapeDtypeStruct(q.shape, F32), kv_shape, kv_shape,
                   jax.ShapeDtypeStruct((h_n, rows, 2 * BLK), F32), jax.ShapeDtypeStruct((h_n, rows // BLK, BLK), F32)],
        name=name, compiler_params=_params(("parallel", "arbitrary")),
    )(q, k, k, v, v, bias, sinks, o, lse, do, dlse)


def _swa_prev_mask(head):
    del head
    return 15


SWA_HEADS_PER_STEP = 2
SWA_GROUP = H_SWA_Q // H_SWA_KV
N_BLK = SEQ // BLK


GROUP_W = H_PER_DIL * HEAD_DIM
DIL_COLUMNS = (768, 1920)
LANE_BLOCKS = (DIL_COLUMNS[1] - DIL_COLUMNS[0]) // GROUP_W
DIL_Q_BLOCK, DIL_K_BLOCK, DIL_V_BLOCK = 0, 3, 6
N_GROUPS = len(DIL_PATTERNS)


def dilated_views(qkv):
    cols = qkv[:, DIL_COLUMNS[0]:DIL_COLUMNS[1]]
    return [_dil_view(cols, d) for _, d in DIL_PATTERNS]


def _dil_view(t, d):
    return t.reshape(SEQ // d, d * t.shape[1])


def _dil_tile(n, d):
    per_class = N_BLK // d
    return n // per_class, n % per_class


def _dil_spec(d, lane_block, lane_blocks, shift=0):
    def index(n):
        r, m = _dil_tile(n, d)
        m = jnp.clip(m + shift, 0, N_BLK // d - 1)
        return m, r * lane_blocks + lane_block
    return pl.BlockSpec((BLK, GROUP_W), index)


def _two_heads(x, first):
    zero = jnp.zeros_like(x)
    return jnp.concatenate([jnp.where(first, x, zero), jnp.where(first, zero, x)], axis=0)


def _per_head(col, first):
    return jnp.where(first, col[0:BLK], col[BLK:2 * BLK])


def _head_rows(tile, first):
    pick = lambda keep: jnp.max(jnp.where(keep, tile, -jnp.inf), axis=1, keepdims=True)
    return jnp.concatenate([pick(first), pick(jnp.logical_not(first))], axis=0)


def _dil_scores(q_ref, kp_ref, ko_ref, bias, has_prev, first):
    q2 = _two_heads(q_ref[...] * QK_SCALE, first)
    k2 = jnp.concatenate([kp_ref[...], ko_ref[...]], axis=0)
    s = _dot(q2, k2, NT) + bias
    key = lax.broadcasted_iota(jnp.int32, s.shape, 1)
    return q2, k2, jnp.where(jnp.logical_or(has_prev, key >= BLK), s, NEG)


def dilated_forward(views, bias):
    def body(*refs):
        ins, bias_ref, outs = refs[:5 * N_GROUPS], refs[5 * N_GROUPS], refs[5 * N_GROUPS + 1:]
        n = pl.program_id(0)
        first = lax.broadcasted_iota(jnp.int32, (1, GROUP_W), 1) < HEAD_DIM
        for g, (_, d) in enumerate(DIL_PATTERNS):
            q_ref, kp_ref, ko_ref, vp_ref, vo_ref = ins[5 * g:5 * g + 5]
            has_prev = _dil_tile(n, d)[1] > 0
            _, _, s = _dil_scores(q_ref, kp_ref, ko_ref, bias_ref[g], has_prev, first)
            m = jnp.max(s, axis=1, keepdims=True)
            p = jnp.exp(s - m)
            denom = jnp.sum(p, axis=1, keepdims=True)
            v2 = jnp.concatenate([vp_ref[...], vo_ref[...]], axis=0)
            o2 = _dot(p.astype(BF16), v2, NN) / denom
            outs[2 * g][...] = _per_head(o2, first)
            outs[2 * g + 1][...] = _per_head(m + jnp.log(denom), first)

    operands, in_specs, out_specs, out_shape = [], [], [], []
    for g, (_, d) in enumerate(DIL_PATTERNS):
        operands += [views[g]] * 5
        in_specs += [_dil_spec(d, DIL_Q_BLOCK + g, LANE_BLOCKS), _dil_spec(d, DIL_K_BLOCK + g, LANE_BLOCKS, -1),
                     _dil_spec(d, DIL_K_BLOCK + g, LANE_BLOCKS), _dil_spec(d, DIL_V_BLOCK + g, LANE_BLOCKS, -1),
                     _dil_spec(d, DIL_V_BLOCK + g, LANE_BLOCKS)]
        out_specs += [_dil_spec(d, 0, 1)] * 2
        out_shape += [jax.ShapeDtypeStruct((SEQ // d, d * GROUP_W), F32)] * 2
    out = pl.pallas_call(
        body, grid=(N_BLK,), in_specs=in_specs + [pl.BlockSpec((N_GROUPS, 2 * BLK, 2 * BLK), lambda n: (0, 0, 0))],
        out_specs=out_specs, out_shape=out_shape, name="dilated_forward", compiler_params=_params(("parallel",)),
    )(*operands, bias)
    out = [t.reshape(SEQ, GROUP_W) for t in out]
    return out[0::2], out[1::2]


def _group_softmax(lses):
    m = jnp.maximum(jnp.maximum(lses[0], lses[1]), lses[2])
    e = [jnp.exp(l - m) for l in lses]
    total = e[0] + e[1] + e[2]
    return [t / total for t in e]


def dilated_merge(o, lse):
    def body(*refs):
        alpha = _group_softmax([r[...] for r in refs[N_GROUPS:2 * N_GROUPS]])
        refs[-1][...] = alpha[0] * refs[0][...] + alpha[1] * refs[1][...] + alpha[2] * refs[2][...]

    spec = pl.BlockSpec((ROW_TILE, GROUP_W), lambda i: (i, 0))
    return pl.pallas_call(
        body, grid=(SEQ // ROW_TILE,), in_specs=[spec] * (2 * N_GROUPS), out_specs=spec,
        out_shape=jax.ShapeDtypeStruct((SEQ, GROUP_W), F32), name="dilated_merge", compiler_params=_params(("parallel",)),
    )(*o, *lse)


def dilated_merge_bwd(o, lse, do_cat):
    def body(*refs):
        o_v = [r[...] for r in refs[:N_GROUPS]]
        alpha = _group_softmax([r[...] for r in refs[N_GROUPS:2 * N_GROUPS]])
        dout = refs[2 * N_GROUPS][...]
        outs = refs[2 * N_GROUPS + 1:]
        first = lax.broadcasted_iota(jnp.int32, (1, GROUP_W), 1) < HEAD_DIM

        def head_sum(x):
            a = jnp.sum(jnp.where(first, x, 0.0), axis=1, keepdims=True)
            b = jnp.sum(jnp.where(first, 0.0, x), axis=1, keepdims=True)
            return jnp.where(first, a, b)

        dalpha = [head_sum(dout * o_g) for o_g in o_v]
        mean = alpha[0] * dalpha[0] + alpha[1] * dalpha[1] + alpha[2] * dalpha[2]
        for g in range(N_GROUPS):
            outs[g][...] = alpha[g] * dout
            outs[N_GROUPS + g][...] = alpha[g] * (dalpha[g] - mean)

    spec = pl.BlockSpec((ROW_TILE, GROUP_W), lambda i: (i, 0))
    shape = jax.ShapeDtypeStruct((SEQ, GROUP_W), F32)
    out = pl.pallas_call(
        body, grid=(SEQ // ROW_TILE,), in_specs=[spec] * (2 * N_GROUPS) + [pl.BlockSpec((ROW_TILE, GROUP_W), lambda i: (i, 2))],
        out_specs=[spec] * (2 * N_GROUPS), out_shape=[shape] * (2 * N_GROUPS),
        name="dilated_merge_bwd", compiler_params=_params(("parallel",)),
    )(*o, *lse, do_cat)
    return out[:N_GROUPS], out[N_GROUPS:]


def dilated_backward(views, bias, o, lse, do, dlse):
    n_in = 9

    def body(*refs):
        ins, bias_ref = refs[:n_in * N_GROUPS], refs[n_in * N_GROUPS]
        outs, dbias_ref = refs[n_in * N_GROUPS + 1:-1], refs[-1]
        n = pl.program_id(0)

        @pl.when(n == 0)
        def _():
            dbias_ref[...] = jnp.zeros_like(dbias_ref)

        first = lax.broadcasted_iota(jnp.int32, (1, GROUP_W), 1) < HEAD_DIM
        for g, (_, d) in enumerate(DIL_PATTERNS):
            q_ref, kp_ref, ko_ref, vp_ref, vo_ref, o_ref, lse_ref, do_ref, dlse_ref = ins[n_in * g:n_in * (g + 1)]
            has_prev = _dil_tile(n, d)[1] > 0
            q2, k2, s = _dil_scores(q_ref, kp_ref, ko_ref, bias_ref[g], has_prev, first)
            dov = do_ref[...]
            do2 = _two_heads(dov.astype(BF16), first)
            prod = dov * o_ref[...]
            delta = jnp.concatenate([jnp.sum(jnp.where(first, prod, 0.0), axis=1, keepdims=True),
                                     jnp.sum(jnp.where(first, 0.0, prod), axis=1, keepdims=True)], axis=0)
            shift = _head_rows(dlse_ref[...], first) - delta
            p = jnp.exp(s - _head_rows(lse_ref[...], first))
            v2 = jnp.concatenate([vp_ref[...], vo_ref[...]], axis=0)
            ds = p * (_dot(do2, v2, NT) + shift)
            dbias_ref[g] += ds
            ds = ds.astype(BF16)
            dq2 = _dot(ds, k2, NN) * QK_SCALE
            dk2 = _dot(ds, q2, TN)
            dv2 = _dot(p.astype(BF16), do2, TN)
            base = 5 * g
            outs[base][...] = jnp.where(first, dq2[0:BLK], dq2[BLK:2 * BLK])
            outs[base + 1][...] = dk2[BLK:2 * BLK]
            outs[base + 2][...] = dk2[0:BLK]
            outs[base + 3][...] = dv2[BLK:2 * BLK]
            outs[base + 4][...] = dv2[0:BLK]

    operands, in_specs, out_specs, out_shape = [], [], [], []
    for g, (_, d) in enumerate(DIL_PATTERNS):
        own = _dil_spec(d, 0, 1)
        operands += [views[g]] * 5 + [_dil_view(t[g], d) for t in (o, lse, do, dlse)]
        in_specs += [_dil_spec(d, DIL_Q_BLOCK + g, LANE_BLOCKS), _dil_spec(d, DIL_K_BLOCK + g, LANE_BLOCKS, -1),
                     _dil_spec(d, DIL_K_BLOCK + g, LANE_BLOCKS), _dil_spec(d, DIL_V_BLOCK + g, LANE_BLOCKS, -1),
                     _dil_spec(d, DIL_V_BLOCK + g, LANE_BLOCKS)] + [own] * 4
        out_specs += [own] * 5
        out_shape += [jax.ShapeDtypeStruct((SEQ // d, d * GROUP_W), F32)] * 5
    tiles = pl.BlockSpec((N_GROUPS, 2 * BLK, 2 * BLK), lambda n: (0, 0, 0))
    out = pl.pallas_call(
        body, grid=(N_BLK,), in_specs=in_specs + [tiles], out_specs=out_specs + [tiles],
        out_shape=out_shape + [jax.ShapeDtypeStruct((N_GROUPS, 2 * BLK, 2 * BLK), F32)],
        name="dilated_backward", compiler_params=_params(("arbitrary",)),
    )(*operands, bias)
    return [out[5 * g:5 * g + 5] for g in range(N_GROUPS)], out[-1]


def dilated_key_grads(parts):
    def body(*refs):
        ins, outs = refs[:4 * N_GROUPS], refs[4 * N_GROUPS:]
        n = pl.program_id(0)
        for g, (_, d) in enumerate(DIL_PATTERNS):
            has_next = _dil_tile(n, d)[1] < N_BLK // d - 1
            own_k, next_k, own_v, next_v = ins[4 * g:4 * g + 4]
            outs[2 * g][...] = own_k[...] + jnp.where(has_next, next_k[...], 0.0)
            outs[2 * g + 1][...] = own_v[...] + jnp.where(has_next, next_v[...], 0.0)

    operands, in_specs, out_specs, out_shape = [], [], [], []
    for g, (_, d) in enumerate(DIL_PATTERNS):
        _, dk_own, dk_prev, dv_own, dv_prev = parts[g]
        operands += [dk_own, dk_prev, dv_own, dv_prev]
        in_specs += [_dil_spec(d, 0, 1), _dil_spec(d, 0, 1, 1)] * 2
        out_specs += [_dil_spec(d, 0, 1)] * 2
        out_shape += [jax.ShapeDtypeStruct((SEQ // d, d * GROUP_W), F32)] * 2
    out = pl.pallas_call(
        body, grid=(N_BLK,), in_specs=in_specs, out_specs=out_specs, out_shape=out_shape,
        name="dilated_key_grads", compiler_params=_params(("parallel",)),
    )(*operands)
    tok = lambda ts: jnp.concatenate([t.reshape(SEQ, GROUP_W) for t in ts], axis=1)
    return tok([parts[g][0] for g in range(N_GROUPS)]), tok(out[0::2]), tok(out[1::2])


def rel_bias_reduce(dbias0, dbias1, bucket):
    def body(d0_ref, d1_ref, b_ref, o_ref):
        dv, bv = d0_ref[...] + d1_ref[...], b_ref[...]
        lane = lax.broadcasted_iota(jnp.int32, (1, BLK), 1)
        acc = jnp.zeros((1, BLK), F32)
        for bkt in range(N_BUCKETS):
            acc = acc + jnp.where(lane == bkt, jnp.sum(jnp.where(bv == bkt, dv, 0.0)), 0.0)
        o_ref[...] = acc

    tile = pl.BlockSpec((None, BLK, 2 * BLK), lambda h: (h, 0, 0))
    return pl.pallas_call(
        body, grid=(dbias0.shape[0],), in_specs=[tile, tile, tile],
        out_specs=pl.BlockSpec((None, 1, BLK), lambda h: (h, 0, 0)),
        out_shape=jax.ShapeDtypeStruct((dbias0.shape[0], 1, BLK), F32),
        name="rel_bias_reduce", compiler_params=_params(("parallel",)),
    )(dbias0, dbias1, bucket)


def _heads(t):
    return t.reshape(SEQ, -1, HEAD_DIM).transpose(1, 0, 2)


def _unheads(t):
    return t.transpose(1, 0, 2).reshape(SEQ, -1)


def _t5_bucket(n):
    max_exact = N_BUCKETS // 2
    nf = jnp.maximum(n, 1).astype(F32)
    large = max_exact + (jnp.log(nf / max_exact) / math.log(MAX_REL_DIST / max_exact)
                         * (N_BUCKETS - max_exact)).astype(jnp.int32)
    large = jnp.minimum(large, N_BUCKETS - 1)
    return jnp.where(n < max_exact, n, large)


def band_tables(rel_bias):
    rel = jnp.arange(BLK)[:, None] + BLK - jnp.arange(2 * BLK)[None, :]
    patterns = [(d, w // d, H_PER_DIL) for w, d in DIL_PATTERNS] + [(1, SWA_WINDOW - 1, H_SWA_Q)]
    buckets = []
    for d, max_dist, heads in patterns:
        band = (rel >= 0) & (rel <= max_dist)
        tile = jnp.where(band, _t5_bucket(jnp.maximum(rel, 0) * d), -1).astype(jnp.int32)
        buckets.append(jnp.broadcast_to(tile, (heads,) + tile.shape))
    buckets = jnp.concatenate(buckets, axis=0)

    def body(table_ref, b_ref, o_ref):
        h = pl.program_id(0)
        bv = b_ref[...]
        tile = jnp.full(bv.shape, NEG, F32)
        for bkt in range(N_BUCKETS):
            tile = jnp.where(bv == bkt, table_ref[h, bkt], tile)
        o_ref[...] = tile

    spec = pl.BlockSpec((None, BLK, 2 * BLK), lambda h: (h, 0, 0))
    tiles = pl.pallas_call(
        body, grid=(buckets.shape[0],), in_specs=[pl.BlockSpec(memory_space=pltpu.SMEM), spec], out_specs=spec,
        out_shape=jax.ShapeDtypeStruct(buckets.shape, F32), name="band_tables", compiler_params=_params(("parallel",)),
    )(rel_bias.T, buckets)
    return tiles[:H_DIL], tiles[H_DIL:], buckets


def _swa_rows(t):
    t = t.reshape(N_BLK, BLK, H_SWA_KV, SWA_GROUP, HEAD_DIM).transpose(2, 0, 3, 1, 4)
    return t.reshape(H_SWA_KV, N_BLK, SWA_GROUP * BLK, HEAD_DIM)


def _swa_tokens(t):
    t = t.reshape(H_SWA_KV, N_BLK, SWA_GROUP, BLK, HEAD_DIM).transpose(1, 3, 0, 2, 4)
    return t.reshape(SEQ, H_SWA_Q * HEAD_DIM)


def _sink_rows(sinks):
    return jnp.broadcast_to(sinks.reshape(H_SWA_KV, SWA_GROUP, 1, 1), (H_SWA_KV, SWA_GROUP, BLK, 1)).reshape(
        H_SWA_KV, SWA_GROUP * BLK, 1)


def _vec(v):
    return v.reshape(1, D_MODEL)


class UnitRows:
    def __init__(self, u, mod_table, gain_table):
        self.shift, self.scale, self.gate = (Row(mod_table, 3 * u + t) for t in range(3))
        self.gain = Row(gain_table, u)


def ffn_forward(x, h, rows, then, w):
    a, b, s = ffn_up(h, w[0], w[1])
    f, xo, *h_next = ffn_down(s, w[2], x, rows.gate, then)
    return xo, (h_next or [None])[0], (x, h, a, b, s, f)


def ffn_backward(u, dxo, df, saved, rows, w, sums, below):
    x, h, a, b, s, _ = saved
    da, db = ffn_bwd_hidden(df, w[2], a, b)
    grads = ffn_grad_weights(h, s, df, da, db)
    dx, sums, *df_below = ffn_bwd_input(da, db, w[0], w[1], x, dxo, rows.gain, rows.scale, sums, u, below)
    return dx, sums, df_below, grads


def mixer_forward(x, h, rows, then, sinks, bias_dil, bias_swa, w):
    proj, qkv = in_proj(h, w[0])
    q_swa, k_swa, v_swa = _swa_rows(qkv[:, 1920:2304] * QK_SCALE), _heads(qkv[:, 2304:2432]), _heads(qkv[:, 2432:2560])
    o_sb, total_sb = sb_forward(qkv)
    bias_dil = bias_dil.reshape(N_GROUPS, 2 * BLK, 2 * BLK)
    views = dilated_views(qkv)
    o_groups, lse_groups = dilated_forward(views, bias_dil)
    o_dil = dilated_merge(o_groups, lse_groups)
    bias_swa = bias_swa.reshape(H_SWA_KV, SWA_GROUP * BLK, 2 * BLK)
    o_swa, lse_swa = banded_forward("swa_forward", q_swa, k_swa, v_swa, bias_swa, _sink_rows(sinks), SWA_HEADS_PER_STEP,
                                    _swa_prev_mask)
    o_parts = (o_sb, o_dil, _swa_tokens(o_swa))
    merged = merge_branches(o_parts, w[1], proj)
    mo, xo, *h_next = out_proj(merged, w[2], x, rows.gate, then)
    saved = (x, h, proj, (qkv, total_sb), (views, o_groups, lse_groups),
             (q_swa, k_swa, v_swa, o_swa, lse_swa), o_parts, merged, mo)
    return xo, (h_next or [None])[0], saved


def mixer_backward(u, dxo, dmo, saved, rows, sinks, bias_dil, bias_swa, w, sums, below):
    x, h, proj, sb, dil, swa, o_parts, merged, _ = saved
    tok = pl.BlockSpec((MM_TILE, D_MODEL), lambda j, k: (k, 0))
    g_out = grad_weight("grad_w_out", merged, pl.BlockSpec((MM_TILE, D_SHARD), lambda j, k: (k, j)), dmo, tok,
                        (D_SHARD, D_MODEL))
    du0, du1, du2, dg0, dg1, dg2 = merge_bwd(dmo, w[2], o_parts, w[1], proj)
    du = (du0, du1, du2)
    do_cat = branch_bwd_input(du, w[1])
    g_br = branch_grad_weights(o_parts, du)

    qkv, total_sb = sb
    dq_sb, dk_sb, dv_sb = sb_backward(qkv, total_sb, do_cat)

    views, o_groups, lse_groups = dil
    bias_dil = bias_dil.reshape(N_GROUPS, 2 * BLK, 2 * BLK)
    do_groups, dlse_groups = dilated_merge_bwd(o_groups, lse_groups, do_cat)
    parts, dbias_dil = dilated_backward(views, bias_dil, o_groups, lse_groups, do_groups, dlse_groups)
    dq_dil, dk_dil, dv_dil = dilated_key_grads(parts)
    dbias_dil = dbias_dil.reshape(H_DIL, BLK, 2 * BLK)

    q_swa, k_swa, v_swa, o_swa, lse_swa = swa
    bias_swa = bias_swa.reshape(H_SWA_KV, SWA_GROUP * BLK, 2 * BLK)
    dq_swa, dk_swa, dv_swa, dbias_swa, dsinks = banded_backward(
        "swa_backward", q_swa, k_swa, v_swa, bias_swa, _sink_rows(sinks), o_swa, lse_swa, _swa_rows(do_cat[:, 384:768]),
        jnp.zeros_like(lse_swa), SWA_HEADS_PER_STEP, _swa_prev_mask)
    dbias_swa = dbias_swa.reshape(H_SWA_Q, BLK, 2 * BLK)

    pieces = [dq_sb, dk_sb, dv_sb, dq_dil, dk_dil, dv_dil, _swa_tokens(dq_swa), _unheads(dk_swa), _unheads(dv_swa)]
    dproj = jnp.concatenate([p.astype(BF16) for p in pieces] + [dg0, dg1, dg2], axis=1)
    g_in = grad_weight("grad_w_in", h, tok, dproj, pl.BlockSpec((MM_TILE, IN_SHARD), lambda j, k: (k, j)),
                       (D_MODEL, IN_SHARD))
    dx, sums, *df_below = mixer_bwd_input(dproj, w[0], x, dxo, rows.gain, rows.scale, sums, u, below)
    dbias = jnp.concatenate([dbias_dil, dbias_swa], axis=0)
    return dx, sums, df_below, dbias, dsinks[:, :, 0].reshape(H_SWA_Q), (g_in, g_br, g_out)


N_UNITS = 3 * DEPTH


def device_step(x, target, mod, gains, final_gain, sinks, rel_bias, get_weights, put_grads):
    bias_dil, bias_swa, bucket = band_tables(rel_bias)
    mod_table = mod.reshape(3 * N_UNITS, 1, D_MODEL)
    gain_table = gains.reshape(N_UNITS, 1, D_MODEL)
    saved, weights = [], []
    units = [UnitRows(u, mod_table, gain_table) for u in range(N_UNITS)]
    h = prenorm(x, units[0].gain, units[0].scale, units[0].shift)
    for u in range(N_UNITS):
        l, j = divmod(u, 3)
        w = get_weights(u, x)
        then = units[u + 1] if u + 1 < N_UNITS else None
        if j == 1:
            x, h, s = mixer_forward(x, h, units[u], then, sinks[l], bias_dil, bias_swa, w)
        else:
            x, h, s = ffn_forward(x, h, units[u], then, w)
        saved.append(s)
        weights.append(w)
    loss, dx, dfinal = final_loss(x, _vec(final_gain), target)

    sums = (lax.empty((8 * N_UNITS, D_MODEL), F32), lax.empty((8 * N_UNITS, D_MODEL), F32))
    dbias, dsinks = [None] * DEPTH, [None] * DEPTH
    zero = jnp.zeros((1, 1), F32)
    top = N_UNITS - 1
    df, gate_sums = resid_bwd(dx, saved[top][-1], units[top].gate, 0.5, sums[1], top)
    sums = (sums[0], gate_sums)
    for u in reversed(range(N_UNITS)):
        l, j = divmod(u, 3)
        rows = UnitRows(u, mod_table, gain_table + zero)
        below = (saved[u - 1][-1], units[u - 1].gate, 1.0 if (u - 1) % 3 == 1 else 0.5) if u > 0 else None
        if j == 1:
            dx, sums, df, dbias[l], dsinks[l], grads = mixer_backward(
                u, dx, df, saved[u], rows, sinks[l], bias_dil, bias_swa, weights[u], sums, below)
        else:
            dx, sums, df, grads = ffn_backward(u, dx, df, saved[u], rows, weights[u], sums, below)
        df = df[0] if df else None
        if u > 0:
            zero = put_grads(u, grads)
    drel = rel_bias_reduce(dbias[0], dbias[1], bucket)[:, 0, :N_BUCKETS].T
    norm_sums, gate_sums = (t.reshape(DEPTH, 3, 8, D_MODEL) for t in sums)
    dmod = jnp.stack([norm_sums[:, :, 0], norm_sums[:, :, 1], gate_sums[:, :, 0]], axis=2)
    return loss, dx, dmod, norm_sums[:, :, 2], dfinal[0], jnp.stack(dsinks), drel, grads


MESH = pl.DeviceIdType.MESH
CHIP_FLIPS = ((1, 0), (0, 1), (1, 1))
ANY = pl.BlockSpec(memory_space=pl.ANY)


def _position():
    return lax.axis_index("x"), lax.axis_index("y"), lax.axis_index("c")


def all_gather_small(name, piece):
    def body(x_ref, out_ref, send_sems, recv_sems, local_sem):
        x, y, c = _position()
        me, sibling = (x, y, c), (x, y, 1 - c)
        chips = [(x ^ fx, y ^ fy) for fx, fy in CHIP_FLIPS]

        def rows(px, py, pc):
            return out_ref.at[4 * px + 2 * py + pc]

        def copy(k, block, to, src=None):
            return pltpu.make_async_remote_copy(
                src_ref=rows(*block) if src is None else src, dst_ref=rows(*block),
                send_sem=send_sems.at[k], recv_sem=recv_sems.at[k], device_id=to, device_id_type=MESH)

        mine = pltpu.make_async_copy(x_ref, rows(*me), local_sem)
        mine.start()
        first = [copy(0, me, sibling, src=x_ref)]
        first += [copy(1 + j, me, (*chip, c), src=x_ref) for j, chip in enumerate(chips)]
        for cp in first:
            cp.start()
        passed = [copy(4 + j, (*chip, c), sibling) for j, chip in enumerate(chips)]
        for j, chip in enumerate(chips):
            copy(1 + j, (*chip, c), me).wait_recv()
            passed[j].start()
        copy(0, sibling, me).wait_recv()
        for j, chip in enumerate(chips):
            copy(4 + j, (*chip, 1 - c), me).wait_recv()
        for cp in first + passed:
            cp.wait_send()
        mine.wait()

    return pl.pallas_call(
        body, out_shape=jax.ShapeDtypeStruct((N_DEV,) + piece.shape, piece.dtype),
        in_specs=[pl.BlockSpec(memory_space=pltpu.VMEM)], out_specs=pl.BlockSpec(memory_space=pltpu.VMEM),
        scratch_shapes=[pltpu.SemaphoreType.DMA((7,)), pltpu.SemaphoreType.DMA((7,)), pltpu.SemaphoreType.DMA],
        name=name,
    )(piece)


def exchange(name, operands, out_shapes, aliases, plan):
    n_in, n_out = len(operands), len(out_shapes)

    def body(*refs):
        ins, outs = refs[:n_in], refs[n_in:n_in + n_out]
        send_sems, recv_sems, local_sems = refs[n_in + n_out:]
        x, y, c = _position()
        local, sends, recvs = plan(ins, outs, x, y, c)
        local = [pltpu.make_async_copy(s, d, local_sems.at[k]) for k, (s, d) in enumerate(local)]
        for cp in local:
            cp.start()
        remote = [pltpu.make_async_remote_copy(src_ref=s, dst_ref=d, send_sem=send_sems.at[k], recv_sem=recv_sems.at[k],
                                               device_id=dev, device_id_type=MESH)
                  for k, (s, d, dev) in enumerate(sends)]
        for cp in remote:
            cp.start()
        for k, r in enumerate(recvs):
            pltpu.make_async_remote_copy(src_ref=r, dst_ref=r, send_sem=send_sems.at[k], recv_sem=recv_sems.at[k],
                                         device_id=(x, y, c), device_id_type=MESH).wait_recv()
        for cp in remote:
            cp.wait_send()
        for cp in local:
            cp.wait()

    n_sends, n_local = plan.n_sends, max(plan.n_local, 1)
    return pl.pallas_call(
        body, out_shape=out_shapes, in_specs=[ANY] * n_in, out_specs=[ANY] * n_out,
        scratch_shapes=[pltpu.SemaphoreType.DMA((n_sends,)), pltpu.SemaphoreType.DMA((n_sends,)),
                        pltpu.SemaphoreType.DMA((n_local,))],
        input_output_aliases=aliases, name=name,
    )(*operands)


def _plan(n_local, n_sends):
    def wrap(fn):
        fn.n_local, fn.n_sends = n_local, n_sends
        return fn
    return wrap


def _half(ref, axis, c):
    rows = ref.shape[axis] // 2
    idx = [slice(None)] * len(ref.shape)
    idx[axis] = pl.ds(pl.multiple_of(c * rows, 16), rows)
    return ref.at[tuple(idx)]


HBM = pl.BlockSpec(memory_space=pltpu.HBM)
SEM = pl.BlockSpec(memory_space=pltpu.SEMAPHORE)
EFFECT = pltpu.SideEffectType.DATAFLOW_SIDE_EFFECTING


def split_start(name, bufs, extra, n_copies, describe):
    n = len(bufs)

    def body(*refs):
        send_sems, recv_sems = refs[n + len(extra)], refs[n + len(extra) + 1]
        x, y, c = _position()
        for k, (src, dst, _, peer) in enumerate(describe(refs[:n], x, y, c)):
            pltpu.make_async_remote_copy(src_ref=src, dst_ref=dst, send_sem=send_sems.at[k], recv_sem=recv_sems.at[k],
                                         device_id=peer, device_id_type=MESH).start()
        token = refs[-1]
        token[...] = jnp.zeros_like(token)

    out = pl.pallas_call(
        body, name=name,
        out_shape=(pltpu.SemaphoreType.DMA((n_copies,)), pltpu.SemaphoreType.DMA((n_copies,)),
                   *[pltpu.HBM(b.shape, b.dtype) for b in bufs], jax.ShapeDtypeStruct((8, 128), F32)),
        in_specs=[HBM] * n + [ANY] * len(extra),
        out_specs=(SEM, SEM, *[HBM] * n, pl.BlockSpec(memory_space=pltpu.VMEM)),
        input_output_aliases={k: 2 + k for k in range(n)},
        compiler_params=pltpu.CompilerParams(has_side_effects=EFFECT),
    )(*[pltpu.with_memory_space_constraint(b, pltpu.HBM) for b in bufs], *extra)
    return out[0], out[1], list(out[2:2 + n]), out[-1]


def split_wait(name, bufs, send_sems, recv_sems, after, describe, first=0):
    n = len(bufs)

    def body(*refs):
        send, recv = refs[n], refs[n + 1]
        x, y, c = _position()
        for k, (src, _, dst, peer) in enumerate(describe(refs[:n], x, y, c), start=first):
            copy = pltpu.make_async_remote_copy(src_ref=src, dst_ref=dst, send_sem=send.at[k], recv_sem=recv.at[k],
                                                device_id=peer, device_id_type=MESH)
            copy.wait_send()
            copy.wait_recv()

    out = pl.pallas_call(
        body, name=name, out_shape=[pltpu.HBM(b.shape, b.dtype) for b in bufs],
        in_specs=[HBM] * n + [SEM, SEM] + [ANY] * len(after), out_specs=[HBM] * n,
        input_output_aliases={k: k for k in range(n)},
        compiler_params=pltpu.CompilerParams(has_side_effects=EFFECT),
    )(*bufs, send_sems, recv_sems, *after)
    return list(out)


def _row_tile(rows, cols, max_elements=256 * 1024):
    best = 16
    for t in range(16, rows + 1, 16):
        if rows % t == 0 and t * cols <= max_elements:
            best = t
    return best


def cast_into_slots(name, shards, chip):
    n = len(shards)
    rows, cols = shards[0][0].shape[-2:]
    tr = _row_tile(rows, cols)

    def body(chip_ref, *refs):
        del chip_ref
        for k in range(n):
            refs[n + k][...] = refs[k][...].astype(BF16)

    def in_spec(param, index):
        return pl.BlockSpec((None,) * len(index) + (tr, cols), lambda r, chip_ref: index + (r, 0))

    return pl.pallas_call(
        body, out_shape=[jax.ShapeDtypeStruct((N_CHIPS, rows, cols), BF16)] * n,
        grid_spec=pltpu.PrefetchScalarGridSpec(
            num_scalar_prefetch=1, grid=(rows // tr,),
            in_specs=[in_spec(p, idx) for p, idx in shards],
            out_specs=[pl.BlockSpec((None, tr, cols), lambda r, chip_ref: (chip_ref[0], r, 0))] * n),
        name=name, compiler_params=_params(("parallel",)),
    )(chip, *[p for p, _ in shards])


GATHER_STAGES = ((0,), (1,), (2,), (3, 4, 5))
REDUCE_STAGES = ((5, 4, 3), (2,), (1,), (0,))


def _gather_copies(slots, x, y, c):
    me = 2 * x + y
    out = []
    for s in slots:
        for fx, fy in CHIP_FLIPS:
            mine = _half(s.at[me], 0, c)
            out.append((mine, mine, _half(s.at[2 * (x ^ fx) + (y ^ fy)], 0, c), (x ^ fx, y ^ fy, c)))
    return out


class WeightStream:
    def __init__(self, shards, chip, after=()):
        self.pending, self.ready = {}, {}
        token = tuple(after)
        for si, units in enumerate(GATHER_STAGES):
            slots = []
            for u in units:
                same = len({p.shape[-2:] for p, _ in shards[u]}) == 1
                for t, group in enumerate([shards[u]] if same else [[s] for s in shards[u]]):
                    slots += cast_into_slots(f"cast_{u}_{t}", group, chip)
            send, recv, slots, tok = split_start(f"gather_start_{si}", slots, token, 3 * len(slots), _gather_copies)
            self.pending[si] = (send, recv, slots)
            token = (tok,)
        self.token = token

    def get(self, u, after):
        if u not in self.ready:
            si = next(k for k, units in enumerate(GATHER_STAGES) if u in units)
            send, recv, slots = self.pending.pop(si)
            slots = split_wait(f"gather_wait_{si}", slots, send, recv, (after,) + self.token, _gather_copies)
            self.token = ()

            @_plan(0, 3 * len(slots))
            def to_sibling(ins, outs, x, y, c):
                sends, recvs = [], []
                for o in outs:
                    for fx, fy in CHIP_FLIPS:
                        slab = o.at[2 * (x ^ fx) + (y ^ fy)]
                        sends.append((_half(slab, 0, c), _half(slab, 0, c), (x, y, 1 - c)))
                        recvs.append(_half(slab, 0, 1 - c))
                return [], sends, recvs

            shapes = [jax.ShapeDtypeStruct(s.shape, BF16) for s in slots]
            slots = exchange(f"gather_sibling_{si}", slots, shapes, {k: k for k in range(len(slots))}, to_sibling)
            for i, v in enumerate(GATHER_STAGES[si]):
                self.ready[v] = tuple(slots[3 * i:3 * i + 3])
        return self.ready[u]


def _reduce_copies(bufs, x, y, c):
    n = len(bufs) // 2
    out = []
    for s, land in zip(bufs[:n], bufs[n:]):
        for k, (fx, fy) in enumerate(CHIP_FLIPS):
            out.append((s.at[2 * (x ^ fx) + (y ^ fy)], land.at[k], land.at[k], (x ^ fx, y ^ fy, c)))
    return out


GRAD_SLOTS = {"gate": (2 * DEPTH, FF_SHARD, D_MODEL), "up": (2 * DEPTH, FF_SHARD, D_MODEL),
              "down": (2 * DEPTH, FF_SHARD, D_MODEL), "in": (DEPTH, D_MODEL, IN_SHARD),
              "br": (DEPTH, BR_ROWS, D_SHARD), "out": (DEPTH, D_SHARD, D_MODEL)}


def _unit_tensors(u):
    l, j = divmod(u, 3)
    if j == 1:
        return [("in", l), ("br", l), ("out", l)]
    return [(k, 2 * l + j // 2) for k in ("gate", "up", "down")]


class GradStream:
    def __init__(self, chip, core):
        self.core = core
        self.place = jnp.concatenate([chip, core])
        self.held, self.flying = {}, []
        self.full = {k: lax.empty(shape, F32) for k, shape in GRAD_SLOTS.items()}

    def put(self, u, grads, after=()):
        self.held[u] = grads
        si = len(self.flying)
        units = REDUCE_STAGES[si]
        if not all(v in self.held for v in units):
            return jnp.zeros((1, 1), F32)
        gs = [g for v in units for g in self.held[v]]

        @_plan(0, len(gs))
        def swap_halves(ins, outs, x, y, c):
            sends = [(_half(g, 1, 1 - c), o, (x, y, 1 - c)) for g, o in zip(ins, outs)]
            return [], sends, list(outs)

        half_shapes = [jax.ShapeDtypeStruct((N_CHIPS, g.shape[1] // 2, g.shape[2]), BF16) for g in gs]
        landed = exchange(f"reduce_swap_{si}", gs + list(after), half_shapes, {}, swap_halves)
        sums = [None] * len(gs)
        for run in _same_shape_runs(gs):
            for k, s in zip(run, _add_halves([gs[k] for k in run], [landed[k] for k in run], self.core)):
                sums[k] = s
        landing = [lax.empty((3,) + s.shape[1:], BF16) for s in sums]
        send, recv, bufs, token = split_start(f"reduce_start_{si}", sums + landing, (), 3 * len(sums), _reduce_copies)
        self.flying.append((send, recv, bufs, [t for v in units for t in _unit_tensors(v)]))
        return token[0:1, 0:1]

    def finish(self, after):
        for si, (send, recv, bufs, tensors) in enumerate(self.flying):
            bufs = split_wait(f"reduce_wait_{si}", bufs, send, recv, tuple(after), _reduce_copies)
            n = len(tensors)
            for run in _same_shape_runs(bufs[:n]):
                names = [tensors[k][0] for k in run]
                out = _add_chips([bufs[k] for k in run], [bufs[n + k] for k in run], self.place,
                                 [self.full[t] for t in names], [tensors[k][1] for k in run])
                self.full.update(zip(names, out))
        self.names = list(self.full)
        self.share = split_start("share_start", [self.full[k] for k in self.names], (), len(self.names), _share_copies)

    def take(self, name):
        send, recv, bufs, _ = self.share
        k = self.names.index(name)
        return split_wait(f"share_wait_{name}", [bufs[k]], send, recv, (), _share_copies, first=k)[0]


def _share_copies(bufs, x, y, c):
    return [(_half(o, 1, c), _half(o, 1, c), _half(o, 1, 1 - c), (x, y, 1 - c)) for o in bufs]


def _same_shape_runs(arrays, longest=3):
    runs = []
    for k, a in enumerate(arrays):
        if runs and len(runs[-1]) < longest and arrays[runs[-1][0]].shape == a.shape:
            runs[-1].append(k)
        else:
            runs.append([k])
    return runs


def _add_halves(gs, landeds, core):
    n = len(gs)
    _, rh, cols = landeds[0].shape
    tr = _row_tile(rh, cols, 1024 * 1024)
    per_half = rh // tr

    def body(core_ref, *refs):
        del core_ref
        for k in range(n):
            refs[2 * n + k][...] = (refs[k][...].astype(F32) + refs[n + k][...].astype(F32)).astype(BF16)

    blk = (None, tr, cols)
    landed_spec = pl.BlockSpec(blk, lambda j, r, core_ref: (j, r, 0))
    return pl.pallas_call(
        body, out_shape=[jax.ShapeDtypeStruct(landeds[0].shape, BF16)] * n,
        grid_spec=pltpu.PrefetchScalarGridSpec(
            num_scalar_prefetch=1, grid=(N_CHIPS, per_half),
            in_specs=[pl.BlockSpec(blk, lambda j, r, core_ref: (j, core_ref[0] * per_half + r, 0))] * n
            + [landed_spec] * n,
            out_specs=[landed_spec] * n),
        name="reduce_add_halves", compiler_params=_params(("parallel", "parallel")),
    )(core, *gs, *landeds)


def _add_chips(sums, landeds, place, fulls, slots):
    n = len(sums)
    _, rh, cols = sums[0].shape
    tr = _row_tile(rh, cols, 1024 * 1024)
    per_half = rh // tr

    def body(place_ref, *refs):
        del place_ref
        for k in range(n):
            s_ref, la_ref, o_ref = refs[k], refs[n + k], refs[3 * n + k]
            o_ref[...] = ((s_ref[...].astype(F32) + la_ref[0].astype(F32)) + la_ref[1].astype(F32)) + la_ref[2].astype(F32)

    def out_spec(slot):
        return pl.BlockSpec((None, tr, cols), lambda r, place_ref: (slot, place_ref[1] * per_half + r, 0))

    return pl.pallas_call(
        body, out_shape=[jax.ShapeDtypeStruct(f.shape, F32) for f in fulls],
        grid_spec=pltpu.PrefetchScalarGridSpec(
            num_scalar_prefetch=1, grid=(per_half,),
            in_specs=[pl.BlockSpec((None, tr, cols), lambda r, place_ref: (place_ref[0], r, 0))] * n
            + [pl.BlockSpec((3, tr, cols), lambda r, place_ref: (0, r, 0))] * n + [ANY] * n,
            out_specs=[out_spec(slot) for slot in slots]),
        input_output_aliases={1 + 2 * n + k: k for k in range(n)}, name="reduce_add_chips",
        compiler_params=_params(("parallel",)),
    )(place, *sums, *landeds, *fulls)


def sum_devices(parts):
    def body(p_ref, o_ref):
        acc = p_ref[0]
        for d in range(1, N_DEV):
            acc = acc + p_ref[d]
        o_ref[...] = acc

    return pl.pallas_call(body, out_shape=jax.ShapeDtypeStruct(parts.shape[1:], F32), name="sum_devices")(parts)


ADA_SHARD = 9 * D_MODEL // N_CHIPS
ADA_TILE = 768
ADA_ROWS = 16


def ada_forward(c_rows, w_ada, b_shard):
    def body(c_ref, w_ref, b_ref, o_ref):
        cv = c_ref[...]
        o_ref[...] = _dot((cv * _sigmoid(cv)).astype(BF16), w_ref[...].astype(BF16), NN) + b_ref[...]

    return pl.pallas_call(
        body, grid=(DEPTH, ADA_SHARD // ADA_TILE),
        in_specs=[pl.BlockSpec((ADA_ROWS, D_MODEL), lambda l, n: (0, 0)),
                  pl.BlockSpec((None, D_MODEL, ADA_TILE), lambda l, n: (l, 0, n)),
                  pl.BlockSpec((None, 1, ADA_TILE), lambda l, n: (l, 0, n))],
        out_specs=pl.BlockSpec((None, ADA_ROWS, ADA_TILE), lambda l, n: (l, 0, n)),
        out_shape=jax.ShapeDtypeStruct((DEPTH, ADA_ROWS, ADA_SHARD), F32),
        name="ada_forward", compiler_params=_params(("parallel", "parallel")),
    )(c_rows, w_ada, b_shard)


def ada_backward(c_rows, dmod_rows):
    def body(c_ref, d_ref, o_ref):
        cv = c_ref[...]
        o_ref[...] = _dot((cv * _sigmoid(cv)).astype(BF16), d_ref[...].astype(BF16), TN)

    return pl.pallas_call(
        body, grid=(DEPTH, ADA_SHARD // ADA_TILE),
        in_specs=[pl.BlockSpec((ADA_ROWS, D_MODEL), lambda l, n: (0, 0)),
                  pl.BlockSpec((None, ADA_ROWS, ADA_TILE), lambda l, n: (l, 0, n))],
        out_specs=pl.BlockSpec((None, D_MODEL, ADA_TILE), lambda l, n: (l, 0, n)),
        out_shape=jax.ShapeDtypeStruct((DEPTH, D_MODEL, ADA_SHARD), F32),
        name="ada_backward", compiler_params=_params(("parallel", "parallel")),
    )(c_rows, dmod_rows)


def adamw(name, w, g, m, v):
    shape = w.shape
    cols = shape[-1]
    rows = w.size // cols
    tr = _row_tile(rows, cols, 512 * 1024) if rows % 16 == 0 else rows
    c1 = 1.0 / (1.0 - ADAM_B1 ** ADAM_STEP)
    c2 = 1.0 / (1.0 - ADAM_B2 ** ADAM_STEP)

    def body(w_ref, g_ref, m_ref, v_ref, go_ref, d_ref, mo_ref, vo_ref):
        gv = g_ref[...]
        mn = ADAM_B1 * m_ref[...] + (1.0 - ADAM_B1) * gv
        vn = ADAM_B2 * v_ref[...] + (1.0 - ADAM_B2) * (gv * gv)
        go_ref[...] = gv
        mo_ref[...] = mn
        vo_ref[...] = vn
        d_ref[...] = -ADAM_LR * ((mn * c1) / (jnp.sqrt(vn * c2) + ADAM_EPS) + ADAM_WD * w_ref[...])

    spec = pl.BlockSpec((tr, cols), lambda i: (i, 0))
    out = jax.ShapeDtypeStruct((rows, cols), F32)
    res = pl.pallas_call(
        body, grid=(rows // tr,), in_specs=[spec] * 4, out_specs=[spec] * 4, out_shape=[out] * 4,
        name=name, compiler_params=_params(("parallel",)),
    )(*[t.reshape(rows, cols) for t in (w, g, m, v)])
    return tuple(r.reshape(shape) for r in res)


def _pack(parts, rows):
    flat = jnp.concatenate([p.reshape(-1) for p in parts])
    return jnp.pad(flat, (0, rows * 128 - flat.size)).reshape(rows, 128)


def _unpack(flat, shapes):
    out, at = [], 0
    for s in shapes:
        n = math.prod(s)
        out.append(flat[at:at + n].reshape(s))
        at += n
    return out


def kernel(x, c, w_ada, b_ada, norm_gain, w_ffn_gate, w_ffn_up, w_ffn_down, w_in, w_br_sb, w_br_dil, w_br_swa, w_out, sinks, rel_bias, final_gain, loss_target, m_w_ada, m_b_ada, m_norm_gain, m_w_ffn_gate, m_w_ffn_up, m_w_ffn_down, m_w_in, m_w_br_sb, m_w_br_dil, m_w_br_swa, m_w_out, m_sinks, m_rel_bias, m_final_gain, v_w_ada, v_b_ada, v_norm_gain, v_w_ffn_gate, v_w_ffn_up, v_w_ffn_down, v_w_in, v_w_br_sb, v_w_br_dil, v_w_br_swa, v_w_out, v_sinks, v_rel_bias, v_final_gain):
    xi, yi, ci = _position()
    chip = 2 * xi + yi
    dev = 2 * chip + ci

    c_all = all_gather_small("gather_c", c.reshape(8, 128)).reshape(N_DEV, D_MODEL)
    c_rows = jnp.pad(c_all, ((0, ADA_ROWS - N_DEV), (0, 0)))
    b_shard = lax.dynamic_slice_in_dim(b_ada, chip * ADA_SHARD, ADA_SHARD, axis=1).reshape(DEPTH, 1, ADA_SHARD)
    mod_shard = ada_forward(c_rows, w_ada, b_shard)[:, :N_DEV]
    n_mod = DEPTH * N_DEV * ADA_SHARD
    gathered = all_gather_small("gather_mod", _pack([mod_shard, norm_gain], 304))[::2].reshape(N_CHIPS, -1)
    mod_all = gathered[:, :n_mod].reshape(N_CHIPS, DEPTH, N_DEV, ADA_SHARD)
    mod = lax.dynamic_index_in_dim(mod_all, dev, axis=2, keepdims=False)
    mod = mod.transpose(1, 0, 2).reshape(DEPTH, 3, 3, D_MODEL)
    gains = gathered[:, n_mod:n_mod + DEPTH * 3 * D_SHARD].reshape(N_CHIPS, DEPTH, 3, D_SHARD)
    gains = gains.transpose(1, 2, 0, 3).reshape(DEPTH, 3, D_MODEL)

    chip_i, core_i = chip.astype(jnp.int32).reshape(1), ci.astype(jnp.int32).reshape(1)
    w_br = jnp.concatenate([w_br_sb, w_br_dil, w_br_swa], axis=1)
    transposed = (3, 4)
    w_gate_t, w_up_t = jnp.swapaxes(w_ffn_gate, 2, 3), jnp.swapaxes(w_ffn_up, 2, 3)
    shards = []
    for l in range(DEPTH):
        ffn = [[(w_gate_t, (l, f)), (w_up_t, (l, f)), (w_ffn_down, (l, f))] for f in range(2)]
        shards += [ffn[0], [(w_in, (l,)), (w_br, (l,)), (w_out, (l,))], ffn[1]]
    weights_in = WeightStream(shards, chip_i, (gathered,))
    grads_out = GradStream(chip_i, core_i)

    loss, dx, dmod, dgains, dfinal, dsinks, drel, last_grads = device_step(
        x[0], loss_target[0], mod, gains, final_gain, sinks, rel_bias, weights_in.get, grads_out.put)

    small_shapes = [(DEPTH, 9 * D_MODEL), (DEPTH, 3, D_MODEL), (D_MODEL,), (DEPTH, H_SWA_Q), (N_BUCKETS, 12), (1,)]
    small_all = all_gather_small("gather_small_grads", _pack([dmod, dgains, dfinal, dsinks, drel, loss[0, 0:1]], 208))
    started = grads_out.put(0, last_grads, after=(small_all,))
    small_all = small_all + started
    g_b_ada, g_gain_full, g_final, g_sinks, g_rel, loss_sum = _unpack(sum_devices(small_all).reshape(-1), small_shapes)
    g_gain = lax.dynamic_slice_in_dim(g_gain_full, chip * D_SHARD, D_SHARD, axis=2)
    dmod_all = small_all.reshape(N_DEV, -1)[:, :DEPTH * 9 * D_MODEL].reshape(N_DEV, DEPTH, 9 * D_MODEL)
    dmod_rows = lax.dynamic_slice_in_dim(dmod_all, chip * ADA_SHARD, ADA_SHARD, axis=2).transpose(1, 0, 2)
    g_w_ada = ada_backward(c_rows, jnp.pad(dmod_rows, ((0, 0), (0, ADA_ROWS - N_DEV), (0, 0))))

    weights = [w_ada, b_ada, norm_gain, w_ffn_gate, w_ffn_up, w_ffn_down, w_in, w_br_sb, w_br_dil, w_br_swa, w_out,
               sinks, rel_bias, final_gain]
    ms = [m_w_ada, m_b_ada, m_norm_gain, m_w_ffn_gate, m_w_ffn_up, m_w_ffn_down, m_w_in, m_w_br_sb, m_w_br_dil,
          m_w_br_swa, m_w_out, m_sinks, m_rel_bias, m_final_gain]
    vs = [v_w_ada, v_b_ada, v_norm_gain, v_w_ffn_gate, v_w_ffn_up, v_w_ffn_down, v_w_in, v_w_br_sb, v_w_br_dil,
          v_w_br_swa, v_w_out, v_sinks, v_rel_bias, v_final_gain]
    grads = [g_w_ada, g_b_ada, g_gain] + [None] * 8 + [g_sinks, g_rel, g_final]

    deltas, new_ms, new_vs = [None] * 14, [None] * 14, [None] * 14
    for k in (0, 1, 2, 11, 12, 13):
        _, deltas[k], new_ms[k], new_vs[k] = adamw(f"adamw_{k}", weights[k], grads[k], ms[k], vs[k])

    grads_out.finish((dx, deltas[0], deltas[1]))
    br_rows = {7: (0, 256), 8: (256, 384), 9: (384, 768)}
    source = {3: "gate", 4: "up", 5: "down", 6: "in", 7: "br", 8: "br", 9: "br", 10: "out"}
    taken = {}
    for k in range(3, 11):
        name = source[k]
        if name not in taken:
            taken[name] = grads_out.take(name)
        g = taken[name]
        grads[k] = g[:, br_rows[k][0]:br_rows[k][1]] if k in br_rows else g.reshape(
            w_gate_t.shape if k in transposed else weights[k].shape)
        state = [weights[k], ms[k], vs[k]]
        if k in transposed:
            state = [jnp.swapaxes(t, 2, 3) for t in state]
        out = adamw(f"adamw_{k}", state[0], grads[k], state[1], state[2])
        if k in transposed:
            out = [jnp.swapaxes(t, 2, 3) for t in out]
        grads[k], deltas[k], new_ms[k], new_vs[k] = out
    return (loss_sum[0], dx[None], *grads, *deltas, *new_ms, *new_vs)
```
